```python
import math
import jax, jax.numpy as jnp
from jax import lax
import numpy as np

D_MODEL = 1024
BATCH = 8
SEQ = 4096
DEPTH = 2

CHUNK = 64
EPS = 1e-6
RET_HEADS = 4
RET_QK_DIM = 256
RET_V_DIM = 512
RET_Q_COLS = RET_HEADS * RET_QK_DIM
RET_V_COLS = RET_HEADS * RET_V_DIM
RET_IN_COLS = 2 * RET_Q_COLS + 2 * RET_V_COLS
ROPE_BASE = 10000.0
ATT_HEADS = 16
ATT_HEAD_DIM = D_MODEL // ATT_HEADS
PAST_CHUNKS = 8
BAND = (PAST_CHUNKS + 1) * CHUNK
REL_CLIP = 256
REL_TABLE = 2 * REL_CLIP + 1
FFN_HIDDEN = int(math.ceil(math.ceil(8 * D_MODEL / 3) / 256) * 256)

kernel_name = "yoco_retention_chunked_relbias_attention"


def rmsnorm(x, g):
    x32 = x.astype(jnp.float32)
    y = x32 * lax.rsqrt(jnp.mean(x32 * x32, axis=-1, keepdims=True) + EPS)
    return (y * g.astype(jnp.float32)).astype(x.dtype)


def swiglu_ffn(h, w_gu, w_down):
    gate, up = jnp.split(h @ w_gu, 2, axis=-1)
    return (jax.nn.silu(gate) * up) @ w_down


def rope(x, positions):
    half = x.shape[-1] // 2
    inv_freq = ROPE_BASE ** (-jnp.arange(half, dtype=jnp.float32) / half)
    ang = positions.astype(jnp.float32)[:, None] * inv_freq[None, :]
    cos = jnp.cos(ang)[None, :, None, :]
    sin = jnp.sin(ang)[None, :, None, :]
    x1, x2 = x[..., :half], x[..., half:]
    return jnp.concatenate([x1 * cos - x2 * sin, x1 * sin + x2 * cos], axis=-1).astype(x.dtype)


def retention_mixer(h, w_in, gn_g, w_o):
    b, s, _ = h.shape
    n = s // CHUNK
    proj = h @ w_in
    q, k, v, g = jnp.split(proj, [RET_Q_COLS, 2 * RET_Q_COLS, 2 * RET_Q_COLS + RET_V_COLS], axis=-1)
    pos = jnp.arange(s)
    q = rope(q.reshape(b, s, RET_HEADS, RET_QK_DIM), pos)
    k = rope(k.reshape(b, s, RET_HEADS, RET_QK_DIM), pos) * (RET_QK_DIM ** -0.5)
    v = v.reshape(b, s, RET_HEADS, RET_V_DIM)
    to_chunks = lambda t: t.reshape(b, n, CHUNK, RET_HEADS, t.shape[-1]).transpose(1, 0, 3, 2, 4)
    qc, kc, vc = to_chunks(q), to_chunks(k), to_chunks(v)

    log_gamma = jnp.log(1.0 - 2.0 ** (-5.0 - jnp.arange(RET_HEADS, dtype=jnp.float32)))
    lg = log_gamma[:, None]
    t = jnp.arange(CHUNK, dtype=jnp.float32)
    intra = jnp.exp(lg[:, :, None] * jnp.abs(t[:, None] - t[None, :]))
    q_dec = jnp.exp(lg * (t + 1.0))[:, :, None]
    k_dec = jnp.exp(lg * (CHUNK - 1.0 - t))[:, :, None]
    s_dec = jnp.exp(lg * CHUNK)[:, :, None]

    def step(state, inp):
        qi, ki, vi = inp
        scores = jnp.einsum('bhtd,bhsd->bhts', qi, ki) * intra
        o = jnp.einsum('bhts,bhsv->bhtv', scores, vi) + jnp.einsum('bhtd,bhdv->bhtv', qi * q_dec, state)
        state = state * s_dec + jnp.einsum('bhsd,bhsv->bhdv', ki * k_dec, vi)
        return state, o

    state0 = jnp.zeros((b, RET_HEADS, RET_QK_DIM, RET_V_DIM), dtype=jnp.result_type(qc.dtype, intra.dtype))
    _, o = lax.scan(step, state0, (qc, kc, vc))
    o = o.transpose(1, 0, 3, 2, 4).reshape(b, s, RET_HEADS, RET_V_DIM)
    o = rmsnorm(o, gn_g.reshape(RET_HEADS, RET_V_DIM)).reshape(b, s, RET_V_COLS).astype(h.dtype)
    return (jax.nn.silu(g) * o) @ w_o


def shared_kv(h, kv_norm_g, w_kv, k_norm_g):
    b, s, _ = h.shape
    u = rmsnorm(h, kv_norm_g)
    k, v = jnp.split(u @ w_kv, 2, axis=-1)
    k = rmsnorm(k.reshape(b, s, ATT_HEADS, ATT_HEAD_DIM), k_norm_g)
    v = v.reshape(b, s, ATT_HEADS, ATT_HEAD_DIM)
    pad = ((0, 0), (PAST_CHUNKS * CHUNK, 0), (0, 0), (0, 0))
    return jnp.pad(k, pad), jnp.pad(v, pad)


def chunk_band_attention(h, w_q, q_norm_g, rel_bias, w_o, k_pad, v_pad):
    b, s, _ = h.shape
    n = s // CHUNK
    q = rmsnorm((h @ w_q).reshape(b, s, ATT_HEADS, ATT_HEAD_DIM), q_norm_g) * (ATT_HEAD_DIM ** -0.5)
    qc = q.reshape(b, n, CHUNK, ATT_HEADS, ATT_HEAD_DIM).swapaxes(0, 1)
    t = jnp.arange(CHUNK)
    j = jnp.arange(BAND)
    dist = PAST_CHUNKS * CHUNK + t[:, None] - j[None, :]
    bias = rel_bias.astype(jnp.float32)[:, jnp.clip(dist, -REL_CLIP, REL_CLIP) + REL_CLIP]

    def one_chunk(args):
        i, qi = args
        kb = lax.dynamic_slice_in_dim(k_pad, i * CHUNK, BAND, axis=1)
        vb = lax.dynamic_slice_in_dim(v_pad, i * CHUNK, BAND, axis=1)
        sc = jnp.einsum('bthd,bshd->bhts', qi, kb).astype(jnp.float32) + bias
        valid = j >= (PAST_CHUNKS - i) * CHUNK
        sc = jnp.where(valid[None, None, None, :], sc, -jnp.inf)
        p = jax.nn.softmax(sc, axis=-1).astype(vb.dtype)
        return jnp.einsum('bhts,bshd->bthd', p, vb)

    o = lax.map(one_chunk, (jnp.arange(n), qc))
    o = o.swapaxes(0, 1).reshape(b, s, ATT_HEADS * ATT_HEAD_DIM)
    return o @ w_o


def _fwd_setup_inputs(seed: int = 0) -> dict:
    key = jax.random.key(seed)
    ks = iter(jax.random.split(key, 32))
    n_a = DEPTH // 2
    n_b = DEPTH - n_a
    f32 = jnp.float32

    def w(shape, fan_in):
        return jax.random.normal(next(ks), shape, f32) * (fan_in ** -0.5)

    def gain(shape):
        return 1.0 + 0.1 * jax.random.normal(next(ks), shape, f32)

    return {
        "x": jax.random.normal(next(ks), (BATCH, SEQ, D_MODEL), f32),
        "a_norm_g": gain((n_a, D_MODEL)),
        "a_w_in": w((n_a, D_MODEL, RET_IN_COLS), D_MODEL),
        "a_gn_g": gain((n_a, RET_V_COLS)),
        "a_w_o": w((n_a, RET_V_COLS, D_MODEL), RET_V_COLS),
        "a_ffn_norm_g": gain((n_a, D_MODEL)),
        "a_w_gu": w((n_a, D_MODEL, 2 * FFN_HIDDEN), D_MODEL),
        "a_w_down": w((n_a, FFN_HIDDEN, D_MODEL), FFN_HIDDEN),
        "kv_norm_g": gain((D_MODEL,)),
        "w_kv": w((D_MODEL, 2 * D_MODEL), D_MODEL),
        "k_norm_g": gain((ATT_HEAD_DIM,)),
        "b_norm_g": gain((n_b, D_MODEL)),
        "b_w_q": w((n_b, D_MODEL, D_MODEL), D_MODEL),
        "b_q_norm_g": gain((n_b, ATT_HEAD_DIM)),
        "b_rel_bias": 0.5 * jax.random.normal(next(ks), (n_b, ATT_HEADS, REL_TABLE), f32),
        "b_w_o": w((n_b, D_MODEL, D_MODEL), D_MODEL),
        "b_ffn_norm_g": gain((n_b, D_MODEL)),
        "b_w_gu": w((n_b, D_MODEL, 2 * FFN_HIDDEN), D_MODEL),
        "b_w_down": w((n_b, FFN_HIDDEN, D_MODEL), FFN_HIDDEN),
    }


def _fwd_reference(x, a_norm_g, a_w_in, a_gn_g, a_w_o, a_ffn_norm_g, a_w_gu, a_w_down,
              kv_norm_g, w_kv, k_norm_g,
              b_norm_g, b_w_q, b_q_norm_g, b_rel_bias, b_w_o, b_ffn_norm_g, b_w_gu, b_w_down):
    n_a = DEPTH // 2
    k_pad = v_pad = None
    for layer in range(DEPTH):
        if layer < n_a:
            i = layer
            x = x + retention_mixer(rmsnorm(x, a_norm_g[i]), a_w_in[i], a_gn_g[i], a_w_o[i])
            x = x + swiglu_ffn(rmsnorm(x, a_ffn_norm_g[i]), a_w_gu[i], a_w_down[i])
        else:
            if layer == n_a:
                k_pad, v_pad = shared_kv(x, kv_norm_g, w_kv, k_norm_g)
            i = layer - n_a
            x = x + chunk_band_attention(rmsnorm(x, b_norm_g[i]), b_w_q[i], b_q_norm_g[i],
                                         b_rel_bias[i], b_w_o[i], k_pad, v_pad)
            x = x + swiglu_ffn(rmsnorm(x, b_ffn_norm_g[i]), b_w_gu[i], b_w_down[i])
    return x


import jax as _jax
import jax.numpy as _jnp

TWIN_FORMAT = 'train_step'
FWD_PARAMS = ['x', 'a_norm_g', 'a_w_in', 'a_gn_g', 'a_w_o', 'a_ffn_norm_g', 'a_w_gu', 'a_w_down', 'kv_norm_g', 'w_kv', 'k_norm_g', 'b_norm_g', 'b_w_q', 'b_q_norm_g', 'b_rel_bias', 'b_w_o', 'b_ffn_norm_g', 'b_w_gu', 'b_w_down']
TWIN_WEIGHTS = ['a_norm_g', 'a_w_in', 'a_gn_g', 'a_w_o', 'a_ffn_norm_g', 'a_w_gu', 'a_w_down', 'kv_norm_g', 'w_kv', 'k_norm_g', 'b_norm_g', 'b_w_q', 'b_q_norm_g', 'b_rel_bias', 'b_w_o', 'b_ffn_norm_g', 'b_w_gu', 'b_w_down']
TWIN_DIFF_INPUT = 'x'
TWIN_INPUTS = ['x', 'a_norm_g', 'a_w_in', 'a_gn_g', 'a_w_o', 'a_ffn_norm_g', 'a_w_gu', 'a_w_down', 'kv_norm_g', 'w_kv', 'k_norm_g', 'b_norm_g', 'b_w_q', 'b_q_norm_g', 'b_rel_bias', 'b_w_o', 'b_ffn_norm_g', 'b_w_gu', 'b_w_down', 'loss_target', 'm_a_norm_g', 'm_a_w_in', 'm_a_gn_g', 'm_a_w_o', 'm_a_ffn_norm_g', 'm_a_w_gu', 'm_a_w_down', 'm_kv_norm_g', 'm_w_kv', 'm_k_norm_g', 'm_b_norm_g', 'm_b_w_q', 'm_b_q_norm_g', 'm_b_rel_bias', 'm_b_w_o', 'm_b_ffn_norm_g', 'm_b_w_gu', 'm_b_w_down', 'v_a_norm_g', 'v_a_w_in', 'v_a_gn_g', 'v_a_w_o', 'v_a_ffn_norm_g', 'v_a_w_gu', 'v_a_w_down', 'v_kv_norm_g', 'v_w_kv', 'v_k_norm_g', 'v_b_norm_g', 'v_b_w_q', 'v_b_q_norm_g', 'v_b_rel_bias', 'v_b_w_o', 'v_b_ffn_norm_g', 'v_b_w_gu', 'v_b_w_down']
TWIN_OUTPUTS = ['loss', 'grad_x', 'grad_a_norm_g', 'grad_a_w_in', 'grad_a_gn_g', 'grad_a_w_o', 'grad_a_ffn_norm_g', 'grad_a_w_gu', 'grad_a_w_down', 'grad_kv_norm_g', 'grad_w_kv', 'grad_k_norm_g', 'grad_b_norm_g', 'grad_b_w_q', 'grad_b_q_norm_g', 'grad_b_rel_bias', 'grad_b_w_o', 'grad_b_ffn_norm_g', 'grad_b_w_gu', 'grad_b_w_down', 'delta_a_norm_g', 'delta_a_w_in', 'delta_a_gn_g', 'delta_a_w_o', 'delta_a_ffn_norm_g', 'delta_a_w_gu', 'delta_a_w_down', 'delta_kv_norm_g', 'delta_w_kv', 'delta_k_norm_g', 'delta_b_norm_g', 'delta_b_w_q', 'delta_b_q_norm_g', 'delta_b_rel_bias', 'delta_b_w_o', 'delta_b_ffn_norm_g', 'delta_b_w_gu', 'delta_b_w_down', 'new_m_a_norm_g', 'new_m_a_w_in', 'new_m_a_gn_g', 'new_m_a_w_o', 'new_m_a_ffn_norm_g', 'new_m_a_w_gu', 'new_m_a_w_down', 'new_m_kv_norm_g', 'new_m_w_kv', 'new_m_k_norm_g', 'new_m_b_norm_g', 'new_m_b_w_q', 'new_m_b_q_norm_g', 'new_m_b_rel_bias', 'new_m_b_w_o', 'new_m_b_ffn_norm_g', 'new_m_b_w_gu', 'new_m_b_w_down', 'new_v_a_norm_g', 'new_v_a_w_in', 'new_v_a_gn_g', 'new_v_a_w_o', 'new_v_a_ffn_norm_g', 'new_v_a_w_gu', 'new_v_a_w_down', 'new_v_kv_norm_g', 'new_v_w_kv', 'new_v_k_norm_g', 'new_v_b_norm_g', 'new_v_b_w_q', 'new_v_b_q_norm_g', 'new_v_b_rel_bias', 'new_v_b_w_o', 'new_v_b_ffn_norm_g', 'new_v_b_w_gu', 'new_v_b_w_down']
TWIN_LEAF_KINDS = {'loss': 'loss', 'grad_x': 'grad_x', 'grad_a_norm_g': 'grad_w', 'grad_a_w_in': 'grad_w', 'grad_a_gn_g': 'grad_w', 'grad_a_w_o': 'grad_w', 'grad_a_ffn_norm_g': 'grad_w', 'grad_a_w_gu': 'grad_w', 'grad_a_w_down': 'grad_w', 'grad_kv_norm_g': 'grad_w', 'grad_w_kv': 'grad_w', 'grad_k_norm_g': 'grad_w', 'grad_b_norm_g': 'grad_w', 'grad_b_w_q': 'grad_w', 'grad_b_q_norm_g': 'grad_w', 'grad_b_rel_bias': 'grad_w', 'grad_b_w_o': 'grad_w', 'grad_b_ffn_norm_g': 'grad_w', 'grad_b_w_gu': 'grad_w', 'grad_b_w_down': 'grad_w', 'delta_a_norm_g': 'delta_w', 'delta_a_w_in': 'delta_w', 'delta_a_gn_g': 'delta_w', 'delta_a_w_o': 'delta_w', 'delta_a_ffn_norm_g': 'delta_w', 'delta_a_w_gu': 'delta_w', 'delta_a_w_down': 'delta_w', 'delta_kv_norm_g': 'delta_w', 'delta_w_kv': 'delta_w', 'delta_k_norm_g': 'delta_w', 'delta_b_norm_g': 'delta_w', 'delta_b_w_q': 'delta_w', 'delta_b_q_norm_g': 'delta_w', 'delta_b_rel_bias': 'delta_w', 'delta_b_w_o': 'delta_w', 'delta_b_ffn_norm_g': 'delta_w', 'delta_b_w_gu': 'delta_w', 'delta_b_w_down': 'delta_w', 'new_m_a_norm_g': 'new_m', 'new_m_a_w_in': 'new_m', 'new_m_a_gn_g': 'new_m', 'new_m_a_w_o': 'new_m', 'new_m_a_ffn_norm_g': 'new_m', 'new_m_a_w_gu': 'new_m', 'new_m_a_w_down': 'new_m', 'new_m_kv_norm_g': 'new_m', 'new_m_w_kv': 'new_m', 'new_m_k_norm_g': 'new_m', 'new_m_b_norm_g': 'new_m', 'new_m_b_w_q': 'new_m', 'new_m_b_q_norm_g': 'new_m', 'new_m_b_rel_bias': 'new_m', 'new_m_b_w_o': 'new_m', 'new_m_b_ffn_norm_g': 'new_m', 'new_m_b_w_gu': 'new_m', 'new_m_b_w_down': 'new_m', 'new_v_a_norm_g': 'new_v', 'new_v_a_w_in': 'new_v', 'new_v_a_gn_g': 'new_v', 'new_v_a_w_o': 'new_v', 'new_v_a_ffn_norm_g': 'new_v', 'new_v_a_w_gu': 'new_v', 'new_v_a_w_down': 'new_v', 'new_v_kv_norm_g': 'new_v', 'new_v_w_kv': 'new_v', 'new_v_k_norm_g': 'new_v', 'new_v_b_norm_g': 'new_v', 'new_v_b_w_q': 'new_v', 'new_v_b_q_norm_g': 'new_v', 'new_v_b_rel_bias': 'new_v', 'new_v_b_w_o': 'new_v', 'new_v_b_ffn_norm_g': 'new_v', 'new_v_b_w_gu': 'new_v', 'new_v_b_w_down': 'new_v'}


def _forward(args):
    return _fwd_reference(*[args[k] for k in FWD_PARAMS])


def _output_shape():
    out = _jax.eval_shape(lambda: _forward(_fwd_setup_inputs(0)))
    return out.shape, out.dtype

N_MICROBATCH = 1
ADAM_LR = 0.001
ADAM_B1 = 0.9
ADAM_B2 = 0.999
ADAM_EPS = 1e-08
ADAM_WD = 0.01
ADAM_STEP = 10
PER_EXAMPLE_BATCH_AXIS = {'x': 0, 'loss_target': 0}
SHARED_INPUTS = []
_WEIGHT_DTYPES = {'a_norm_g': _jnp.float32, 'a_w_in': _jnp.float32, 'a_gn_g': _jnp.float32, 'a_w_o': _jnp.float32, 'a_ffn_norm_g': _jnp.float32, 'a_w_gu': _jnp.float32, 'a_w_down': _jnp.float32, 'kv_norm_g': _jnp.float32, 'w_kv': _jnp.float32, 'k_norm_g': _jnp.float32, 'b_norm_g': _jnp.float32, 'b_w_q': _jnp.float32, 'b_q_norm_g': _jnp.float32, 'b_rel_bias': _jnp.float32, 'b_w_o': _jnp.float32, 'b_ffn_norm_g': _jnp.float32, 'b_w_gu': _jnp.float32, 'b_w_down': _jnp.float32}
MOMENT_SCALE = {'a_norm_g': 1.303100e+01, 'a_w_in': 2.799737e-01, 'a_gn_g': 5.626430e+00, 'a_w_o': 3.426709e-01, 'a_ffn_norm_g': 2.529616e+01, 'a_w_gu': 1.904932e-01, 'a_w_down': 3.076754e-01, 'kv_norm_g': 1.874179e-01, 'w_kv': 7.285607e-02, 'k_norm_g': 2.024712e+00, 'b_norm_g': 6.353584e-02, 'b_w_q': 6.536223e-02, 'b_q_norm_g': 1.988061e+00, 'b_rel_bias': 2.678599e-02, 'b_w_o': 7.686224e-02, 'b_ffn_norm_g': 2.559152e+01, 'b_w_gu': 1.767577e-01, 'b_w_down': 2.846743e-01}


def _to_microbatches(a, axis):
    t = _jnp.moveaxis(a, axis, 0)
    t = t.reshape((N_MICROBATCH, t.shape[0] // N_MICROBATCH) + t.shape[1:])
    return _jnp.moveaxis(t, 1, axis + 1)


def setup_inputs(seed: int = 0) -> dict:
    inp = _fwd_setup_inputs(seed)
    key = _jax.random.fold_in(_jax.random.key(seed), 7919)
    shape, _ = _output_shape()
    out = dict(inp)
    out["loss_target"] = _jax.random.normal(_jax.random.fold_in(key, 0), shape, _jnp.float32)
    for i, name in enumerate(TWIN_WEIGHTS):
        w = inp[name].astype(_jnp.float32)
        if MOMENT_SCALE is None:
            s = _jnp.sqrt(_jnp.mean(_jnp.square(w)) + 1e-30)
        else:
            s = MOMENT_SCALE[name]
        km, kv = _jax.random.split(_jax.random.fold_in(key, i + 1))
        out[name] = w
        out["m_" + name] = s * _jax.random.normal(km, w.shape, _jnp.float32)
        out["v_" + name] = (s * s) * _jax.random.uniform(kv, w.shape, _jnp.float32, 0.5, 1.5)
    if N_MICROBATCH > 1:
        for name, axis in PER_EXAMPLE_BATCH_AXIS.items():
            out[name] = _to_microbatches(out[name], axis)
    return {'x': out['x'], 'a_norm_g': out['a_norm_g'], 'a_w_in': out['a_w_in'], 'a_gn_g': out['a_gn_g'], 'a_w_o': out['a_w_o'], 'a_ffn_norm_g': out['a_ffn_norm_g'], 'a_w_gu': out['a_w_gu'], 'a_w_down': out['a_w_down'], 'kv_norm_g': out['kv_norm_g'], 'w_kv': out['w_kv'], 'k_norm_g': out['k_norm_g'], 'b_norm_g': out['b_norm_g'], 'b_w_q': out['b_w_q'], 'b_q_norm_g': out['b_q_norm_g'], 'b_rel_bias': out['b_rel_bias'], 'b_w_o': out['b_w_o'], 'b_ffn_norm_g': out['b_ffn_norm_g'], 'b_w_gu': out['b_w_gu'], 'b_w_down': out['b_w_down'], 'loss_target': out['loss_target'], 'm_a_norm_g': out['m_a_norm_g'], 'm_a_w_in': out['m_a_w_in'], 'm_a_gn_g': out['m_a_gn_g'], 'm_a_w_o': out['m_a_w_o'], 'm_a_ffn_norm_g': out['m_a_ffn_norm_g'], 'm_a_w_gu': out['m_a_w_gu'], 'm_a_w_down': out['m_a_w_down'], 'm_kv_norm_g': out['m_kv_norm_g'], 'm_w_kv': out['m_w_kv'], 'm_k_norm_g': out['m_k_norm_g'], 'm_b_norm_g': out['m_b_norm_g'], 'm_b_w_q': out['m_b_w_q'], 'm_b_q_norm_g': out['m_b_q_norm_g'], 'm_b_rel_bias': out['m_b_rel_bias'], 'm_b_w_o': out['m_b_w_o'], 'm_b_ffn_norm_g': out['m_b_ffn_norm_g'], 'm_b_w_gu': out['m_b_w_gu'], 'm_b_w_down': out['m_b_w_down'], 'v_a_norm_g': out['v_a_norm_g'], 'v_a_w_in': out['v_a_w_in'], 'v_a_gn_g': out['v_a_gn_g'], 'v_a_w_o': out['v_a_w_o'], 'v_a_ffn_norm_g': out['v_a_ffn_norm_g'], 'v_a_w_gu': out['v_a_w_gu'], 'v_a_w_down': out['v_a_w_down'], 'v_kv_norm_g': out['v_kv_norm_g'], 'v_w_kv': out['v_w_kv'], 'v_k_norm_g': out['v_k_norm_g'], 'v_b_norm_g': out['v_b_norm_g'], 'v_b_w_q': out['v_b_w_q'], 'v_b_q_norm_g': out['v_b_q_norm_g'], 'v_b_rel_bias': out['v_b_rel_bias'], 'v_b_w_o': out['v_b_w_o'], 'v_b_ffn_norm_g': out['v_b_ffn_norm_g'], 'v_b_w_gu': out['v_b_w_gu'], 'v_b_w_down': out['v_b_w_down']}


def _loss(weights, diff, rest, loss_target):
    with _jax.named_scope("forward"):
        args = {**rest, TWIN_DIFF_INPUT: diff, **{k: w.astype(_WEIGHT_DTYPES[k]) for k, w in weights.items()}}
        y = _forward(args)
    with _jax.named_scope("loss_head"):
        err = _jnp.square(y.astype(_jnp.float32) - loss_target)
        return 0.5 * _jnp.sum(_jnp.mean(err, axis=-1)) if err.ndim else 0.5 * err


def _adamw(w, g, m, v):
    m = ADAM_B1 * m + (1.0 - ADAM_B1) * g
    v = ADAM_B2 * v + (1.0 - ADAM_B2) * _jnp.square(g)
    m_hat = m / (1.0 - ADAM_B1 ** ADAM_STEP)
    v_hat = v / (1.0 - ADAM_B2 ** ADAM_STEP)
    delta = -ADAM_LR * (m_hat / (_jnp.sqrt(v_hat) + ADAM_EPS) + ADAM_WD * w)
    return delta, m, v


def reference(x, a_norm_g, a_w_in, a_gn_g, a_w_o, a_ffn_norm_g, a_w_gu, a_w_down, kv_norm_g, w_kv, k_norm_g, b_norm_g, b_w_q, b_q_norm_g, b_rel_bias, b_w_o, b_ffn_norm_g, b_w_gu, b_w_down, loss_target, m_a_norm_g, m_a_w_in, m_a_gn_g, m_a_w_o, m_a_ffn_norm_g, m_a_w_gu, m_a_w_down, m_kv_norm_g, m_w_kv, m_k_norm_g, m_b_norm_g, m_b_w_q, m_b_q_norm_g, m_b_rel_bias, m_b_w_o, m_b_ffn_norm_g, m_b_w_gu, m_b_w_down, v_a_norm_g, v_a_w_in, v_a_gn_g, v_a_w_o, v_a_ffn_norm_g, v_a_w_gu, v_a_w_down, v_kv_norm_g, v_w_kv, v_k_norm_g, v_b_norm_g, v_b_w_q, v_b_q_norm_g, v_b_rel_bias, v_b_w_o, v_b_ffn_norm_g, v_b_w_gu, v_b_w_down):
    given = dict(x=x, a_norm_g=a_norm_g, a_w_in=a_w_in, a_gn_g=a_gn_g, a_w_o=a_w_o, a_ffn_norm_g=a_ffn_norm_g, a_w_gu=a_w_gu, a_w_down=a_w_down, kv_norm_g=kv_norm_g, w_kv=w_kv, k_norm_g=k_norm_g, b_norm_g=b_norm_g, b_w_q=b_w_q, b_q_norm_g=b_q_norm_g, b_rel_bias=b_rel_bias, b_w_o=b_w_o, b_ffn_norm_g=b_ffn_norm_g, b_w_gu=b_w_gu, b_w_down=b_w_down, loss_target=loss_target, m_a_norm_g=m_a_norm_g, m_a_w_in=m_a_w_in, m_a_gn_g=m_a_gn_g, m_a_w_o=m_a_w_o, m_a_ffn_norm_g=m_a_ffn_norm_g, m_a_w_gu=m_a_w_gu, m_a_w_down=m_a_w_down, m_kv_norm_g=m_kv_norm_g, m_w_kv=m_w_kv, m_k_norm_g=m_k_norm_g, m_b_norm_g=m_b_norm_g, m_b_w_q=m_b_w_q, m_b_q_norm_g=m_b_q_norm_g, m_b_rel_bias=m_b_rel_bias, m_b_w_o=m_b_w_o, m_b_ffn_norm_g=m_b_ffn_norm_g, m_b_w_gu=m_b_w_gu, m_b_w_down=m_b_w_down, v_a_norm_g=v_a_norm_g, v_a_w_in=v_a_w_in, v_a_gn_g=v_a_gn_g, v_a_w_o=v_a_w_o, v_a_ffn_norm_g=v_a_ffn_norm_g, v_a_w_gu=v_a_w_gu, v_a_w_down=v_a_w_down, v_kv_norm_g=v_kv_norm_g, v_w_kv=v_w_kv, v_k_norm_g=v_k_norm_g, v_b_norm_g=v_b_norm_g, v_b_w_q=v_b_w_q, v_b_q_norm_g=v_b_q_norm_g, v_b_rel_bias=v_b_rel_bias, v_b_w_o=v_b_w_o, v_b_ffn_norm_g=v_b_ffn_norm_g, v_b_w_gu=v_b_w_gu, v_b_w_down=v_b_w_down)
    weights = {n: given[n] for n in TWIN_WEIGHTS}
    shared = {n: given[n] for n in SHARED_INPUTS}
    per_example = {n: given[n] for n in ['x']}
    grad_fn = _jax.value_and_grad(_loss, argnums=(0, 1))

    def one_microbatch(ex, loss_target):
        ex = dict(ex)
        diff = ex.pop(TWIN_DIFF_INPUT)
        return grad_fn(weights, diff, {**shared, **ex}, loss_target)

    if N_MICROBATCH == 1:
        loss, (grad_w, grad_x) = one_microbatch(per_example, given["loss_target"])
    else:
        def body(carry, xs):
            loss_sum, grad_sum = carry
            l_k, (gw_k, gx_k) = one_microbatch(xs[0], xs[1])
            with _jax.named_scope("update"):
                return (loss_sum + l_k, _jax.tree.map(_jnp.add, grad_sum, gw_k)), gx_k

        init = (_jnp.zeros((), _jnp.float32), _jax.tree.map(_jnp.zeros_like, weights))
        (loss, grad_w), grad_x = _jax.lax.scan(body, init, (per_example, given["loss_target"]))
    with _jax.named_scope("update"):
        delta_w, new_m, new_v = {}, {}, {}
        for n in TWIN_WEIGHTS:
            delta_w[n], new_m[n], new_v[n] = _adamw(weights[n], grad_w[n], given["m_" + n], given["v_" + n])
    return (loss, grad_x, *[grad_w[n] for n in TWIN_WEIGHTS], *[delta_w[n] for n in TWIN_WEIGHTS],
            *[new_m[n] for n in TWIN_WEIGHTS], *[new_v[n] for n in TWIN_WEIGHTS])
```

```python
import functools

import numpy as np
import jax
import jax.numpy as jnp
from jax import lax
from jax.experimental import pallas as pl
from jax.experimental.pallas import tpu as pltpu

F32 = jnp.float32
BF16 = jnp.bfloat16

N_DEV = 8
D_MODEL = 1024
CHUNK = 64
EPS = 1e-6
RET_HEADS, RET_DK, RET_DV = 4, 256, 512
RET_Q_COLS = RET_HEADS * RET_DK
RET_V_COLS = RET_HEADS * RET_DV
ATT_HEADS, ATT_DH = 16, 64
PAST_CHUNKS = 8
REL_CLIP = 256
REL_TABLE = 2 * REL_CLIP + 1
FFN_HIDDEN = 2816
ROPE_BASE = 10000.0
Q_BLOCK = 256
K_PAD = PAST_CHUNKS * CHUNK
K_WINDOW = Q_BLOCK + K_PAD
REL_BLK = 128
REL_DELTAS = Q_BLOCK // REL_BLK + K_WINDOW // REL_BLK - 1
REL_PAD = 640
NEG = -1e30
VMEM_LIMIT_V7X = 56 * 1024 * 1024
ADAM_LR, ADAM_B1, ADAM_B2, ADAM_EPS, ADAM_WD, ADAM_STEP = 1e-3, 0.9, 0.999, 1e-8, 0.01, 10
MESH = pl.DeviceIdType.MESH
ANY = pl.BlockSpec(memory_space=pl.ANY)


def _params(*semantics):
    return pltpu.CompilerParams(dimension_semantics=semantics, vmem_limit_bytes=VMEM_LIMIT_V7X)


def _pick(dim, cap, align):
    best = None
    for t in range(align, min(dim, cap) + 1, align):
        if dim % t == 0:
            best = t
    assert best is not None, (dim, cap, align)
    return best


def _dot(a, b):
    return lax.dot_general(a, b, (((1,), (0,)), ((), ())), preferred_element_type=F32)


def _dot_nt(a, b):
    return lax.dot_general(a, b, (((1,), (1,)), ((), ())), preferred_element_type=F32)


def _dot_tn(a, b):
    return lax.dot_general(a, b, (((0,), (0,)), ((), ())), preferred_element_type=F32)


def _split2(x):
    hi = x.astype(BF16)
    lo = (x - hi.astype(F32)).astype(BF16)
    return hi, lo


def _split3(x):
    hi = x.astype(BF16)
    r = x - hi.astype(F32)
    mid = r.astype(BF16)
    lo = (r - mid.astype(F32)).astype(BF16)
    return hi, mid, lo


def _sigmoid(x):
    return 1.0 / (1.0 + jnp.exp(-x))


def _mm(a, b, mode, name, out_dtype=F32, res=None):
    if mode == "nn":
        (m, k), (k2, n) = a.shape, b.shape
    elif mode == "nt":
        (m, k), (n, k2) = a.shape, b.shape
    else:
        (k, m), (k2, n) = a.shape, b.shape
    assert k == k2, (a.shape, b.shape, mode)
    tm = _pick(m, 512, 128 if mode == "tn" else 16)
    tn = _pick(n, 1408, 128)
    tk = _pick(k, 1024 if mode == "tn" else 2048, 128)
    nk = k // tk
    dot = {"nn": _dot, "nt": _dot_nt, "tn": _dot_tn}[mode]

    def body(a_ref, b_ref, *rest):
        if res is None:
            o_ref, acc_ref = rest
        else:
            r_ref, o_ref, acc_ref = rest
        kk = pl.program_id(2)
        part = dot(a_ref[...].astype(BF16), b_ref[...].astype(BF16))

        def finish(total):
            if res is not None:
                total = r_ref[...] + total
            o_ref[...] = total.astype(out_dtype)

        if nk == 1:
            finish(part)
        else:
            @pl.when(kk == 0)
            def _():
                acc_ref[...] = part

            @pl.when(jnp.logical_and(kk > 0, kk < nk - 1))
            def _():
                acc_ref[...] += part

            @pl.when(kk == nk - 1)
            def _():
                finish(acc_ref[...] + part)

    if mode == "nn":
        a_spec = pl.BlockSpec((tm, tk), lambda i, j, kk: (i, kk))
        b_spec = pl.BlockSpec((tk, tn), lambda i, j, kk: (kk, j))
    elif mode == "nt":
        a_spec = pl.BlockSpec((tm, tk), lambda i, j, kk: (i, kk))
        b_spec = pl.BlockSpec((tn, tk), lambda i, j, kk: (j, kk))
    else:
        a_spec = pl.BlockSpec((tk, tm), lambda i, j, kk: (kk, i))
        b_spec = pl.BlockSpec((tk, tn), lambda i, j, kk: (kk, j))
    o_spec = pl.BlockSpec((tm, tn), lambda i, j, kk: (i, j))
    in_specs = [a_spec, b_spec] + ([o_spec] if res is not None else [])
    args = (a, b) + ((res,) if res is not None else ())
    return pl.pallas_call(
        body, name=name, grid=(m // tm, n // tn, nk),
        in_specs=in_specs, out_specs=o_spec,
        out_shape=jax.ShapeDtypeStruct((m, n), out_dtype),
        scratch_shapes=[pltpu.VMEM((tm, tn), F32)],
        compiler_params=_params("parallel", "parallel", "arbitrary"),
    )(*args)


def _rms_fwd(x, g, name):
    t, d = x.shape
    tm = _pick(t, 512, 16)

    def body(x_ref, g_ref, o_ref):
        xv = x_ref[...]
        rstd = lax.rsqrt(jnp.mean(xv * xv, axis=-1, keepdims=True) + EPS)
        o_ref[...] = (xv * rstd * g_ref[...]).astype(BF16)

    return pl.pallas_call(
        body, name=name, grid=(t // tm,),
        in_specs=[pl.BlockSpec((tm, d), lambda i: (i, 0)), pl.BlockSpec((1, d), lambda i: (0, 0))],
        out_specs=pl.BlockSpec((tm, d), lambda i: (i, 0)),
        out_shape=jax.ShapeDtypeStruct((t, d), BF16),
        compiler_params=_params("parallel"),
    )(x, g)


def _rms_bwd(x, g, dh, dres, name):
    t, d = x.shape
    tm = _pick(t, 512, 16)

    def body(x_ref, g_ref, dh_ref, dres_ref, dx_ref, dg_ref):
        i = pl.program_id(0)
        xv = x_ref[...]
        rstd = lax.rsqrt(jnp.mean(xv * xv, axis=-1, keepdims=True) + EPS)
        xh = xv * rstd
        dhv = dh_ref[...]
        dyg = dhv * g_ref[...]
        c = jnp.mean(dyg * xh, axis=-1, keepdims=True)
        dx_ref[...] = dres_ref[...] + rstd * (dyg - xh * c)
        part = jnp.sum(dhv * xh, axis=0, keepdims=True)

        @pl.when(i == 0)
        def _():
            dg_ref[...] = part

        @pl.when(i > 0)
        def _():
            dg_ref[...] += part

    row = pl.BlockSpec((tm, d), lambda i: (i, 0))
    vec = pl.BlockSpec((1, d), lambda i: (0, 0))
    return pl.pallas_call(
        body, name=name, grid=(t // tm,),
        in_specs=[row, vec, row, row], out_specs=[row, vec],
        out_shape=[jax.ShapeDtypeStruct((t, d), F32), jax.ShapeDtypeStruct((1, d), F32)],
        compiler_params=_params("arbitrary"),
    )(x, g, dh, dres)


def _seg_mean(v, bd):
    hi, lo = _split2(v)
    return (_dot(hi, bd) + _dot(lo, bd)) * (1.0 / ATT_DH)


def _hn_fwd(x, g_tiled, bd, scale, name):
    t, d = x.shape
    tm = _pick(t, 512, 16)

    def body(x_ref, g_ref, bd_ref, o_ref):
        xv = x_ref[...]
        rstd = lax.rsqrt(_seg_mean(xv * xv, bd_ref[...]) + EPS)
        o_ref[...] = (xv * rstd * g_ref[...] * scale).astype(BF16)

    return pl.pallas_call(
        body, name=name, grid=(t // tm,),
        in_specs=[pl.BlockSpec((tm, d), lambda i: (i, 0)), pl.BlockSpec((1, d), lambda i: (0, 0)),
                  pl.BlockSpec((d, d), lambda i: (0, 0))],
        out_specs=pl.BlockSpec((tm, d), lambda i: (i, 0)),
        out_shape=jax.ShapeDtypeStruct((t, d), BF16),
        compiler_params=_params("parallel"),
    )(x, g_tiled, bd)


def _hn_bwd(x, g_tiled, bd, dy, scale, name):
    t, d = x.shape
    tm = _pick(t, 512, 16)

    def body(x_ref, g_ref, bd_ref, dy_ref, dx_ref, dg_ref):
        i = pl.program_id(0)
        xv = x_ref[...]
        bdv = bd_ref[...]
        rstd = lax.rsqrt(_seg_mean(xv * xv, bdv) + EPS)
        xh = xv * rstd
        dyn = dy_ref[...] * scale
        dyg = dyn * g_ref[...]
        c = _seg_mean(dyg * xh, bdv)
        dx_ref[...] = (rstd * (dyg - xh * c)).astype(BF16)
        part = jnp.sum(dyn * xh, axis=0, keepdims=True)

        @pl.when(i == 0)
        def _():
            dg_ref[...] = part

        @pl.when(i > 0)
        def _():
            dg_ref[...] += part

    row = pl.BlockSpec((tm, d), lambda i: (i, 0))
    vec = pl.BlockSpec((1, d), lambda i: (0, 0))
    return pl.pallas_call(
        body, name=name, grid=(t // tm,),
        in_specs=[row, vec, pl.BlockSpec((d, d), lambda i: (0, 0)), row],
        out_specs=[row, vec],
        out_shape=[jax.ShapeDtypeStruct((t, d), BF16), jax.ShapeDtypeStruct((1, d), F32)],
        compiler_params=_params("arbitrary"),
    )(x, g_tiled, bd, dy)


def _swiglu_fwd(gu, name):
    t, two_f = gu.shape
    f = two_f // 2
    tm = _pick(t, 256, 16)

    def body(g_ref, u_ref, o_ref):
        gv = g_ref[...]
        o_ref[...] = (gv * _sigmoid(gv) * u_ref[...]).astype(BF16)

    return pl.pallas_call(
        body, name=name, grid=(t // tm,),
        in_specs=[pl.BlockSpec((tm, f), lambda i: (i, 0)), pl.BlockSpec((tm, f), lambda i: (i, 1))],
        out_specs=pl.BlockSpec((tm, f), lambda i: (i, 0)),
        out_shape=jax.ShapeDtypeStruct((t, f), BF16),
        compiler_params=_params("parallel"),
    )(gu, gu)


def _swiglu_bwd(gu, dact, name):
    t, two_f = gu.shape
    f = two_f // 2
    tm = _pick(t, 256, 16)

    def body(g_ref, u_ref, da_ref, o_ref):
        gv = g_ref[...]
        sg = _sigmoid(gv)
        dav = da_ref[...]
        o_ref[:, :f] = (dav * u_ref[...] * (sg * (1.0 + gv * (1.0 - sg)))).astype(BF16)
        o_ref[:, f:] = (dav * (gv * sg)).astype(BF16)

    return pl.pallas_call(
        body, name=name, grid=(t // tm,),
        in_specs=[pl.BlockSpec((tm, f), lambda i: (i, 0)), pl.BlockSpec((tm, f), lambda i: (i, 1)),
                  pl.BlockSpec((tm, f), lambda i: (i, 0))],
        out_specs=pl.BlockSpec((tm, two_f), lambda i: (i, 0)),
        out_shape=jax.ShapeDtypeStruct((t, two_f), BF16),
        compiler_params=_params("parallel"),
    )(gu, gu, dact)


def _loss_head(y, target, name):
    t, d = y.shape
    tm = _pick(t, 512, 16)

    def body(y_ref, t_ref, dy_ref, l_ref):
        i = pl.program_id(0)
        diff = y_ref[...] - t_ref[...]
        dy_ref[...] = diff * (1.0 / d)
        part = jnp.sum(jnp.sum(diff * diff, axis=-1, keepdims=True), axis=0, keepdims=True) * (0.5 / d)

        @pl.when(i == 0)
        def _():
            l_ref[...] = part

        @pl.when(i > 0)
        def _():
            l_ref[...] += part

    row = pl.BlockSpec((tm, d), lambda i: (i, 0))
    return pl.pallas_call(
        body, name=name, grid=(t // tm,),
        in_specs=[row, row], out_specs=[row, pl.BlockSpec((1, 1), lambda i: (0, 0))],
        out_shape=[jax.ShapeDtypeStruct((t, d), F32), jax.ShapeDtypeStruct((1, 1), F32)],
        compiler_params=_params("arbitrary"),
    )(y, target)


def _ret_consts(t):
    h = np.arange(RET_HEADS, dtype=np.float32)
    lg = np.log(np.float32(1.0) - np.float32(2.0) ** (np.float32(-5.0) - h)).astype(np.float32)
    tt = np.arange(CHUNK, dtype=np.float32)
    intra = np.exp(lg[:, None, None] * np.abs(tt[:, None] - tt[None, :])).astype(np.float32)
    q_dec = np.exp(lg[:, None] * (tt + 1.0)).astype(np.float32)
    k_dec = np.exp(lg[:, None] * (CHUNK - 1.0 - tt)).astype(np.float32)
    s_dec = [float(v) for v in np.exp(lg * np.float32(CHUNK)).astype(np.float32)]
    qd = np.broadcast_to(q_dec[:, :, None], (RET_HEADS, CHUNK, RET_DK)).copy()
    kd = np.broadcast_to(k_dec[:, :, None], (RET_HEADS, CHUNK, RET_DK)).copy()
    half = RET_DK // 2
    inv_freq = ROPE_BASE ** (-jnp.arange(half, dtype=F32) / half)
    ang = jnp.arange(t).astype(F32)[:, None] * inv_freq[None, :]
    return jnp.asarray(intra), jnp.asarray(qd), jnp.asarray(kd), s_dec, jnp.cos(ang), jnp.sin(ang)


def _rope(x, cos, sin):
    half = RET_DK // 2
    x1, x2 = x[:, :half], x[:, half:]
    return jnp.concatenate([x1 * cos - x2 * sin, x1 * sin + x2 * cos], axis=-1)


def _unrope(d, cos, sin):
    half = RET_DK // 2
    d1, d2 = d[:, :half], d[:, half:]
    return jnp.concatenate([d1 * cos + d2 * sin, d2 * cos - d1 * sin], axis=-1)


def _ret_slices(h):
    q = slice(h * RET_DK, (h + 1) * RET_DK)
    k = slice(RET_Q_COLS + h * RET_DK, RET_Q_COLS + (h + 1) * RET_DK)
    v = slice(2 * RET_Q_COLS + h * RET_DV, 2 * RET_Q_COLS + (h + 1) * RET_DV)
    g = slice(2 * RET_Q_COLS + RET_V_COLS + h * RET_DV, 2 * RET_Q_COLS + RET_V_COLS + (h + 1) * RET_DV)
    o = slice(h * RET_DV, (h + 1) * RET_DV)
    return q, k, v, g, o


def _ret_fwd(proj, gn, consts, name):
    t, cols = proj.shape
    n = t // CHUNK
    intra, qd, kd, s_dec, cos, sin = consts
    k_scale = RET_DK ** -0.5

    def body(p_ref, cos_ref, sin_ref, intra_ref, qd_ref, kd_ref, gn_ref, y_ref, o_ref, st_ref, state):
        i = pl.program_id(0)

        @pl.when(i == 0)
        def _():
            state[...] = jnp.zeros_like(state)

        cosv, sinv = cos_ref[...], sin_ref[...]
        for h in range(RET_HEADS):
            qs, ks, vs, gs, os_ = _ret_slices(h)
            qr = _rope(p_ref[:, qs], cosv, sinv)
            kr = _rope(p_ref[:, ks], cosv, sinv) * k_scale
            vb = p_ref[:, vs].astype(BF16)
            gv = p_ref[:, gs]
            scores = _dot_nt(qr.astype(BF16), kr.astype(BF16)) * intra_ref[h]
            s_old = state[h]
            s_old_b = s_old.astype(BF16)
            st_ref[0, h] = s_old_b
            o = _dot(scores.astype(BF16), vb) + _dot((qr * qd_ref[h]).astype(BF16), s_old_b)
            state[h] = s_old * s_dec[h] + _dot_tn((kr * kd_ref[h]).astype(BF16), vb)
            rstd = lax.rsqrt(jnp.mean(o * o, axis=-1, keepdims=True) + EPS)
            on = o * rstd * gn_ref[:, os_]
            o_ref[:, os_] = o
            y_ref[:, os_] = (gv * _sigmoid(gv) * on).astype(BF16)

    full3 = lambda a: pl.BlockSpec(a.shape, lambda i: (0, 0, 0))
    return pl.pallas_call(
        body, name=name, grid=(n,),
        in_specs=[pl.BlockSpec((CHUNK, cols), lambda i: (i, 0)),
                  pl.BlockSpec((CHUNK, RET_DK // 2), lambda i: (i, 0)),
                  pl.BlockSpec((CHUNK, RET_DK // 2), lambda i: (i, 0)),
                  full3(intra), full3(qd), full3(kd),
                  pl.BlockSpec((1, RET_V_COLS), lambda i: (0, 0))],
        out_specs=[pl.BlockSpec((CHUNK, RET_V_COLS), lambda i: (i, 0)),
                   pl.BlockSpec((CHUNK, RET_V_COLS), lambda i: (i, 0)),
                   pl.BlockSpec((1, RET_HEADS, RET_DK, RET_DV), lambda i: (i, 0, 0, 0))],
        out_shape=[jax.ShapeDtypeStruct((t, RET_V_COLS), BF16),
                   jax.ShapeDtypeStruct((t, RET_V_COLS), F32),
                   jax.ShapeDtypeStruct((n, RET_HEADS, RET_DK, RET_DV), BF16)],
        scratch_shapes=[pltpu.VMEM((RET_HEADS, RET_DK, RET_DV), F32)],
        compiler_params=_params("arbitrary"),
    )(proj, cos, sin, intra, qd, kd, gn)


def _ret_bwd(proj, gn, o_saved, states, dy, consts, name):
    t, cols = proj.shape
    n = t // CHUNK
    intra, qd, kd, s_dec, cos, sin = consts
    k_scale = RET_DK ** -0.5

    def body(p_ref, cos_ref, sin_ref, intra_ref, qd_ref, kd_ref, gn_ref, o_ref, st_ref, dy_ref,
             dp_ref, dgn_ref, dstate):
        i = pl.program_id(0)

        @pl.when(i == 0)
        def _():
            dstate[...] = jnp.zeros_like(dstate)

        cosv, sinv = cos_ref[...], sin_ref[...]
        dgn_parts = []
        for h in range(RET_HEADS):
            qs, ks, vs, gs, os_ = _ret_slices(h)
            qr = _rope(p_ref[:, qs], cosv, sinv)
            kr = _rope(p_ref[:, ks], cosv, sinv) * k_scale
            qb, kb = qr.astype(BF16), kr.astype(BF16)
            vb = p_ref[:, vs].astype(BF16)
            gv = p_ref[:, gs]
            ov = o_ref[:, os_]
            dyv = dy_ref[:, os_]
            gnv = gn_ref[:, os_]
            sg = _sigmoid(gv)
            rstd = lax.rsqrt(jnp.mean(ov * ov, axis=-1, keepdims=True) + EPS)
            oh = ov * rstd
            d_on = dyv * (gv * sg)
            dg = dyv * (oh * gnv) * (sg * (1.0 + gv * (1.0 - sg)))
            dgn_parts.append(jnp.sum(d_on * oh, axis=0, keepdims=True))
            d_oh = d_on * gnv
            do = rstd * (d_oh - oh * jnp.mean(d_oh * oh, axis=-1, keepdims=True))
            dob = do.astype(BF16)
            mask = intra_ref[h]
            a_b = (_dot_nt(qb, kb) * mask).astype(BF16)
            da_b = (_dot_nt(dob, vb) * mask).astype(BF16)
            ds_new = dstate[h]
            ds_new_b = ds_new.astype(BF16)
            s_old_b = st_ref[0, h]
            qdv, kdv = qd_ref[h], kd_ref[h]
            dv = _dot_tn(a_b, dob) + _dot((kr * kdv).astype(BF16), ds_new_b)
            dqr = _dot(da_b, kb) + _dot_nt(dob, s_old_b) * qdv
            dkr = _dot_tn(da_b, qb) + _dot_nt(vb, ds_new_b) * kdv
            dstate[h] = ds_new * s_dec[h] + _dot_tn((qr * qdv).astype(BF16), dob)
            dp_ref[:, qs] = _unrope(dqr, cosv, sinv).astype(BF16)
            dp_ref[:, ks] = _unrope(dkr * k_scale, cosv, sinv).astype(BF16)
            dp_ref[:, vs] = dv.astype(BF16)
            dp_ref[:, gs] = dg.astype(BF16)
        part = jnp.concatenate(dgn_parts, axis=-1)

        @pl.when(i == 0)
        def _():
            dgn_ref[...] = part

        @pl.when(i > 0)
        def _():
            dgn_ref[...] += part

    rev = lambda i: (n - 1 - i, 0)
    full3 = lambda a: pl.BlockSpec(a.shape, lambda i: (0, 0, 0))
    return pl.pallas_call(
        body, name=name, grid=(n,),
        in_specs=[pl.BlockSpec((CHUNK, cols), rev),
                  pl.BlockSpec((CHUNK, RET_DK // 2), rev),
                  pl.BlockSpec((CHUNK, RET_DK // 2), rev),
                  full3(intra), full3(qd), full3(kd),
                  pl.BlockSpec((1, RET_V_COLS), lambda i: (0, 0)),
                  pl.BlockSpec((CHUNK, RET_V_COLS), rev),
                  pl.BlockSpec((1, RET_HEADS, RET_DK, RET_DV), lambda i: (n - 1 - i, 0, 0, 0)),
                  pl.BlockSpec((CHUNK, RET_V_COLS), rev)],
        out_specs=[pl.BlockSpec((CHUNK, cols), rev),
                   pl.BlockSpec((1, RET_V_COLS), lambda i: (0, 0))],
        out_shape=[jax.ShapeDtypeStruct((t, cols), BF16),
                   jax.ShapeDtypeStruct((1, RET_V_COLS), F32)],
        scratch_shapes=[pltpu.VMEM((RET_HEADS, RET_DK, RET_DV), F32)],
        compiler_params=_params("arbitrary"),
    )(proj, cos, sin, intra, qd, kd, gn, o_saved, states, dy)


def _att_scores(q_ref, kp_ref, vp_ref, bias_ref):
    blk = pl.program_id(1)
    start = pl.multiple_of(blk * Q_BLOCK, Q_BLOCK)
    qv = q_ref[0]
    kw = kp_ref[0, pl.ds(start, K_WINDOW), :]
    vw = vp_ref[0, pl.ds(start, K_WINDOW), :]
    s = _dot_nt(qv, kw) + bias_ref[0]
    kpos = blk * Q_BLOCK - K_PAD + lax.broadcasted_iota(jnp.int32, (1, K_WINDOW), 1)
    s = jnp.where(kpos >= 0, s, NEG)
    e = jnp.exp(s - jnp.max(s, axis=-1, keepdims=True))
    p = e * (1.0 / jnp.sum(e, axis=-1, keepdims=True))
    return start, qv, kw, vw, p


def _att_fwd(q, kp, vp, bias, name):
    heads, t, dh = q.shape
    tp = kp.shape[1]

    def body(q_ref, kp_ref, vp_ref, bias_ref, o_ref):
        _, _, _, vw, p = _att_scores(q_ref, kp_ref, vp_ref, bias_ref)
        o_ref[0] = _dot(p.astype(BF16), vw).astype(BF16)

    qspec = pl.BlockSpec((1, Q_BLOCK, dh), lambda h, i: (h, i, 0))
    kspec = pl.BlockSpec((1, tp, dh), lambda h, i: (h, 0, 0))
    bspec = pl.BlockSpec((1, Q_BLOCK, K_WINDOW), lambda h, i: (h, 0, 0))
    return pl.pallas_call(
        body, name=name, grid=(heads, t // Q_BLOCK),
        in_specs=[qspec, kspec, kspec, bspec], out_specs=qspec,
        out_shape=jax.ShapeDtypeStruct((heads, t, dh), BF16),
        compiler_params=_params("parallel", "arbitrary"),
    )(q, kp, vp, bias)


def _att_bwd(q, kp, vp, bias, do, name):
    heads, t, dh = q.shape
    tp = kp.shape[1]

    def body(q_ref, kp_ref, vp_ref, bias_ref, do_ref, dq_ref, dkp_ref, dvp_ref, db_ref):
        blk = pl.program_id(1)

        @pl.when(blk == 0)
        def _():
            dkp_ref[...] = jnp.zeros_like(dkp_ref)
            dvp_ref[...] = jnp.zeros_like(dvp_ref)
            db_ref[...] = jnp.zeros_like(db_ref)

        start, qv, kw, vw, p = _att_scores(q_ref, kp_ref, vp_ref, bias_ref)
        dov = do_ref[0]
        dp = _dot_nt(dov, vw)
        ds = p * (dp - jnp.sum(dp * p, axis=-1, keepdims=True))
        db_ref[0] += ds
        dsb = ds.astype(BF16)
        dq_ref[0] = _dot(dsb, kw)
        dkp_ref[0, pl.ds(start, K_WINDOW), :] += _dot_tn(dsb, qv)
        dvp_ref[0, pl.ds(start, K_WINDOW), :] += _dot_tn(p.astype(BF16), dov)

    qspec = pl.BlockSpec((1, Q_BLOCK, dh), lambda h, i: (h, i, 0))
    kspec = pl.BlockSpec((1, tp, dh), lambda h, i: (h, 0, 0))
    bspec = pl.BlockSpec((1, Q_BLOCK, K_WINDOW), lambda h, i: (h, 0, 0))
    return pl.pallas_call(
        body, name=name, grid=(heads, t // Q_BLOCK),
        in_specs=[qspec, kspec, kspec, bspec, qspec],
        out_specs=[qspec, kspec, kspec, bspec],
        out_shape=[jax.ShapeDtypeStruct((heads, t, dh), F32),
                   jax.ShapeDtypeStruct((heads, tp, dh), F32),
                   jax.ShapeDtypeStruct((heads, tp, dh), F32),
                   jax.ShapeDtypeStruct((heads, Q_BLOCK, K_WINDOW), F32)],
        compiler_params=_params("parallel", "arbitrary"),
    )(q, kp, vp, bias, do)


def _rel_bin_matrix():
    rows = REL_DELTAS * 2 * REL_BLK
    rho = lax.broadcasted_iota(jnp.int32, (rows, REL_PAD), 0)
    col = lax.broadcasted_iota(jnp.int32, (rows, REL_PAD), 1)
    assert 2 * REL_BLK == 256
    delta = rho >> 8
    c = rho & 255
    dist = K_PAD + REL_BLK * (delta - (K_WINDOW // REL_BLK - 1)) + (c - (REL_BLK - 1))
    idx = jnp.clip(dist, -REL_CLIP, REL_CLIP) + REL_CLIP
    return col == idx


def _rel_shift_matrix(r):
    c = lax.broadcasted_iota(jnp.int32, (2 * REL_BLK, REL_BLK), 0)
    s = lax.broadcasted_iota(jnp.int32, (2 * REL_BLK, REL_BLK), 1)
    return c == r - s + (REL_BLK - 1)


def _rel_expand(rel_pad, name):
    heads = rel_pad.shape[0]
    rows = REL_DELTAS * 2 * REL_BLK

    def body_bin(r_ref, o_ref):
        onehot = jnp.where(_rel_bin_matrix(), 1.0, 0.0).astype(BF16)
        hi, mid, lo = _split3(r_ref[...])
        o_ref[...] = _dot_nt(hi, onehot) + _dot_nt(mid, onehot) + _dot_nt(lo, onehot)

    by_delta = pl.pallas_call(
        body_bin, name=name + "_bin",
        out_shape=jax.ShapeDtypeStruct((heads, rows), F32),
        compiler_params=pltpu.CompilerParams(vmem_limit_bytes=VMEM_LIMIT_V7X),
    )(rel_pad)
    by_delta = by_delta.reshape(heads * REL_DELTAS, 2 * REL_BLK)

    def body_shift(t_ref, o_ref):
        r = pl.program_id(0)
        onehot = jnp.where(_rel_shift_matrix(r), 1.0, 0.0).astype(BF16)
        hi, mid, lo = _split3(t_ref[...])
        o_ref[0] = _dot(hi, onehot) + _dot(mid, onehot) + _dot(lo, onehot)

    return pl.pallas_call(
        body_shift, name=name + "_shift", grid=(REL_BLK,),
        in_specs=[pl.BlockSpec(by_delta.shape, lambda r: (0, 0))],
        out_specs=pl.BlockSpec((1, heads * REL_DELTAS, REL_BLK), lambda r: (r, 0, 0)),
        out_shape=jax.ShapeDtypeStruct((REL_BLK, heads * REL_DELTAS, REL_BLK), F32),
        compiler_params=_params("parallel"),
    )(by_delta)


def _bias_table(rel_bias, name):
    heads = rel_bias.shape[0]
    rel_pad = jnp.pad(rel_bias, ((0, 0), (0, REL_PAD - REL_TABLE)))
    tiles = _rel_expand(rel_pad, name)
    tiles = tiles.reshape(REL_BLK, heads, REL_DELTAS, REL_BLK).transpose(1, 2, 0, 3)
    na, nb = Q_BLOCK // REL_BLK, K_WINDOW // REL_BLK
    rows = [jnp.concatenate([tiles[:, a - b + nb - 1] for b in range(nb)], axis=-1) for a in range(na)]
    table = jnp.concatenate(rows, axis=-2)
    qc = np.arange(Q_BLOCK)[:, None] // CHUNK
    kc = np.arange(K_WINDOW)[None, :] // CHUNK
    band = (kc >= qc) & (kc <= qc + PAST_CHUNKS)
    return jnp.where(jnp.asarray(band)[None], table, NEG)


def _rel_reduce(db, name):
    heads = db.shape[0]
    na, nb = Q_BLOCK // REL_BLK, K_WINDOW // REL_BLK

    def body_fold(db_ref, g_ref):
        for delta in range(REL_DELTAS):
            acc = None
            for a in range(na):
                b = a - (delta - (nb - 1))
                if 0 <= b < nb:
                    tile = db_ref[0, a * REL_BLK:(a + 1) * REL_BLK, b * REL_BLK:(b + 1) * REL_BLK]
                    acc = tile if acc is None else acc + tile
            g_ref[0, delta] = acc

    folded = pl.pallas_call(
        body_fold, name=name + "_fold", grid=(heads,),
        in_specs=[pl.BlockSpec((1, Q_BLOCK, K_WINDOW), lambda h: (h, 0, 0))],
        out_specs=pl.BlockSpec((1, REL_DELTAS, REL_BLK, REL_BLK), lambda h: (h, 0, 0, 0)),
        out_shape=jax.ShapeDtypeStruct((heads, REL_DELTAS, REL_BLK, REL_BLK), F32),
        compiler_params=_params("parallel"),
    )(db)
    by_row = folded.transpose(2, 0, 1, 3).reshape(REL_BLK, heads * REL_DELTAS, REL_BLK)

    def body_diag(g_ref, d_ref):
        r = pl.program_id(0)
        onehot = jnp.where(_rel_shift_matrix(r), 1.0, 0.0).astype(BF16)
        hi, mid, lo = _split3(g_ref[0])
        part = _dot_nt(hi, onehot) + _dot_nt(mid, onehot) + _dot_nt(lo, onehot)

        @pl.when(r == 0)
        def _():
            d_ref[...] = part

        @pl.when(r > 0)
        def _():
            d_ref[...] += part

    diag = pl.pallas_call(
        body_diag, name=name + "_diag", grid=(REL_BLK,),
        in_specs=[pl.BlockSpec((1, heads * REL_DELTAS, REL_BLK), lambda r: (r, 0, 0))],
        out_specs=pl.BlockSpec((heads * REL_DELTAS, 2 * REL_BLK), lambda r: (0, 0)),
        out_shape=jax.ShapeDtypeStruct((heads * REL_DELTAS, 2 * REL_BLK), F32),
        compiler_params=_params("arbitrary"),
    )(by_row)
    diag = diag.reshape(heads, REL_DELTAS * 2 * REL_BLK)

    def body_bin(d_ref, o_ref):
        onehot = jnp.where(_rel_bin_matrix(), 1.0, 0.0).astype(BF16)
        hi, mid, lo = _split3(d_ref[...])
        o_ref[...] = _dot(hi, onehot) + _dot(mid, onehot) + _dot(lo, onehot)

    out = pl.pallas_call(
        body_bin, name=name + "_bin",
        out_shape=jax.ShapeDtypeStruct((heads, REL_PAD), F32),
        compiler_params=pltpu.CompilerParams(vmem_limit_bytes=VMEM_LIMIT_V7X),
    )(diag)
    return out[:, :REL_TABLE]


def _sum_leading(x, name):
    n, r, c = x.shape
    tr = _pick(r, 256, 8)

    def body(x_ref, o_ref):
        acc = x_ref[0]
        for k in range(1, n):
            acc = acc + x_ref[k]
        o_ref[...] = acc

    return pl.pallas_call(
        body, name=name, grid=(r // tr,),
        in_specs=[pl.BlockSpec((n, tr, c), lambda i: (0, i, 0))],
        out_specs=pl.BlockSpec((tr, c), lambda i: (i, 0)),
        out_shape=jax.ShapeDtypeStruct((r, c), F32),
        compiler_params=_params("parallel"),
    )(x)


def _pair_add(g, recv, parity, name):
    _, r, c = g.shape
    tr = _pick(r, 256, 8)

    def body(par_ref, g_ref, r_ref, o_ref):
        o_ref[...] = g_ref[...] + r_ref[...]

    return pl.pallas_call(
        body, name=name,
        grid_spec=pltpu.PrefetchScalarGridSpec(
            num_scalar_prefetch=1, grid=(4, r // tr),
            in_specs=[pl.BlockSpec((1, tr, c), lambda k, i, par: (2 * k + par[0], i, 0)),
                      pl.BlockSpec((1, tr, c), lambda k, i, par: (k, i, 0))],
            out_specs=pl.BlockSpec((1, tr, c), lambda k, i, par: (k, i, 0))),
        out_shape=jax.ShapeDtypeStruct((4, r, c), F32),
        compiler_params=_params("parallel", "parallel"),
    )(parity, g, recv)


def _adamw(w, g, m, v, name):
    r, c = w.shape
    tr = _pick(r, 256, 8)
    c1 = 1.0 - ADAM_B1 ** ADAM_STEP
    c2 = 1.0 - ADAM_B2 ** ADAM_STEP

    def body(w_ref, g_ref, m_ref, v_ref, d_ref, nm_ref, nv_ref):
        gv = g_ref[...]
        nm = ADAM_B1 * m_ref[...] + (1.0 - ADAM_B1) * gv
        nv = ADAM_B2 * v_ref[...] + (1.0 - ADAM_B2) * (gv * gv)
        d_ref[...] = -ADAM_LR * ((nm / c1) / (jnp.sqrt(nv / c2) + ADAM_EPS) + ADAM_WD * w_ref[...])
        nm_ref[...] = nm
        nv_ref[...] = nv

    spec = pl.BlockSpec((tr, c), lambda i: (i, 0))
    shp = jax.ShapeDtypeStruct((r, c), F32)
    return pl.pallas_call(
        body, name=name, grid=(r // tr,),
        in_specs=[spec] * 4, out_specs=[spec] * 3, out_shape=[shp] * 3,
        compiler_params=_params("parallel"),
    )(w, g, m, v)


def _my_place():
    return lax.axis_index("x"), lax.axis_index("y"), lax.axis_index("c")


def _all_gather(x, name):
    def body(x_ref, out_ref, send_sems, recv_sems, local_sem):
        x, y, c = _my_place()
        me, sibling = (x, y, c), (x, y, 1 - c)
        chips = [(1 - x, y), (x, 1 - y), (1 - x, 1 - y)]

        def slot(px, py, pc):
            return out_ref.at[4 * px + 2 * py + pc]

        def copy(k, block, to, src=None):
            return pltpu.make_async_remote_copy(
                src_ref=slot(*block) if src is None else src, dst_ref=slot(*block),
                send_sem=send_sems.at[k], recv_sem=recv_sems.at[k],
                device_id=to, device_id_type=MESH)

        mine = pltpu.make_async_copy(x_ref, slot(*me), local_sem)
        mine.start()
        first = [copy(0, me, sibling, src=x_ref)]
        first += [copy(1 + j, me, (*chip, c), src=x_ref) for j, chip in enumerate(chips)]
        for cp in first:
            cp.start()
        passed = [copy(4 + j, (*chip, c), sibling) for j, chip in enumerate(chips)]
        for j, chip in enumerate(chips):
            copy(1 + j, (*chip, c), me).wait_recv()
            passed[j].start()
        copy(0, sibling, me).wait_recv()
        for j, chip in enumerate(chips):
            copy(4 + j, (*chip, 1 - c), me).wait_recv()
        for cp in first + passed:
            cp.wait_send()
        mine.wait()

    return pl.pallas_call(
        body, name=name,
        out_shape=jax.ShapeDtypeStruct((N_DEV,) + x.shape, x.dtype),
        in_specs=[ANY], out_specs=ANY,
        scratch_shapes=[pltpu.SemaphoreType.DMA((7,)), pltpu.SemaphoreType.DMA((7,)),
                        pltpu.SemaphoreType.DMA],
    )(x)


def _swap_with_sibling(g, name):
    _, r, c_ = g.shape

    def body(g_ref, out_ref, send_sems, recv_sems):
        x, y, c = _my_place()
        copies = [pltpu.make_async_remote_copy(
            src_ref=g_ref.at[2 * k + 1 - c], dst_ref=out_ref.at[k],
            send_sem=send_sems.at[k], recv_sem=recv_sems.at[k],
            device_id=(x, y, 1 - c), device_id_type=MESH) for k in range(4)]
        for cp in copies:
            cp.start()
        for cp in copies:
            cp.wait()

    return pl.pallas_call(
        body, name=name,
        out_shape=jax.ShapeDtypeStruct((4, r, c_), g.dtype),
        in_specs=[ANY], out_specs=ANY,
        scratch_shapes=[pltpu.SemaphoreType.DMA((4,)), pltpu.SemaphoreType.DMA((4,))],
    )(g)


def _scatter_to_chips(p, name):
    def body(p_ref, out_ref, send_sems, recv_sems, local_sem):
        x, y, c = _my_place()
        my_chip = 2 * x + y
        chips = [(1 - x, y), (x, 1 - y), (1 - x, 1 - y)]
        mine = pltpu.make_async_copy(p_ref.at[my_chip], out_ref.at[my_chip], local_sem)
        mine.start()
        copies = [pltpu.make_async_remote_copy(
            src_ref=p_ref.at[2 * cx + cy], dst_ref=out_ref.at[my_chip],
            send_sem=send_sems.at[j], recv_sem=recv_sems.at[j],
            device_id=(cx, cy, c), device_id_type=MESH) for j, (cx, cy) in enumerate(chips)]
        for cp in copies:
            cp.start()
        for cp in copies:
            cp.wait()
        mine.wait()

    return pl.pallas_call(
        body, name=name,
        out_shape=jax.ShapeDtypeStruct(p.shape, p.dtype),
        in_specs=[ANY], out_specs=ANY,
        scratch_shapes=[pltpu.SemaphoreType.DMA((3,)), pltpu.SemaphoreType.DMA((3,)),
                        pltpu.SemaphoreType.DMA],
    )(p)


BIG = (("a_w_in", (D_MODEL, 2 * RET_Q_COLS + 2 * RET_V_COLS), 1),
       ("a_w_o", (RET_V_COLS, D_MODEL), 0),
       ("a_w_gu", (D_MODEL, 2 * FFN_HIDDEN), 1),
       ("a_w_down", (FFN_HIDDEN, D_MODEL), 0),
       ("w_kv", (D_MODEL, 2 * D_MODEL), 1),
       ("b_w_q", (D_MODEL, D_MODEL), 0),
       ("b_w_o", (D_MODEL, D_MODEL), 0),
       ("b_w_gu", (D_MODEL, 2 * FFN_HIDDEN), 1),
       ("b_w_down", (FFN_HIDDEN, D_MODEL), 0))
PACK_COLS = 1024


def _shard_shape(shape, axis):
    return tuple(s // N_DEV if a == axis else s for a, s in enumerate(shape))


def _pack_rows(shape, axis):
    sr, sc = _shard_shape(shape, axis)
    return sr * sc // PACK_COLS


def _pack_shards(shards):
    return jnp.concatenate([shards[n].reshape(-1, PACK_COLS) for n, _, _ in BIG], axis=0)


def _unpack_gathered(packed):
    out, row = {}, 0
    for n, shape, axis in BIG:
        rows = _pack_rows(shape, axis)
        sr, sc = _shard_shape(shape, axis)
        w = packed[:, row:row + rows].reshape(N_DEV, sr, sc)
        out[n] = w.reshape(shape) if axis == 0 else jnp.moveaxis(w, 0, 1).reshape(shape)
        row += rows
    return out


def _pack_full_by_shard(full):
    parts = []
    for n, shape, axis in BIG:
        sr, sc = _shard_shape(shape, axis)
        w = full[n]
        w = w.reshape(N_DEV, sr, sc) if axis == 0 else jnp.moveaxis(w.reshape(sr, N_DEV, sc), 1, 0)
        parts.append(w.reshape(N_DEV, -1, PACK_COLS))
    return jnp.concatenate(parts, axis=1)


def _unpack_shards(packed):
    out, row = {}, 0
    for n, shape, axis in BIG:
        rows = _pack_rows(shape, axis)
        out[n] = packed[row:row + rows].reshape(_shard_shape(shape, axis))
        row += rows
    return out


SMALL = (("a_norm_g", D_MODEL, True), ("a_gn_g", RET_V_COLS, True), ("a_ffn_norm_g", D_MODEL, True),
         ("kv_norm_g", D_MODEL, False), ("b_norm_g", D_MODEL, False), ("b_ffn_norm_g", D_MODEL, False),
         ("k_norm_g", ATT_DH, False), ("b_q_norm_g", ATT_DH, False),
         ("b_rel_bias", ATT_HEADS * REL_TABLE, False))
SMALL_ROWS = 16


def _pack_small(vals, local):
    flat = jnp.concatenate([vals[n].reshape(-1) for n, _, _ in SMALL])
    total = SMALL_ROWS * PACK_COLS
    return jnp.pad(flat, (0, total - flat.shape[0])).reshape(SMALL_ROWS, PACK_COLS)


def _unpack_small(packed, local):
    flat, out, pos = packed.reshape(-1), {}, 0
    for n, length, sharded in SMALL:
        ln = length // N_DEV if (local and sharded) else length
        out[n] = flat[pos:pos + ln]
        pos += ln
    return out


def _heads_major(a):
    t = a.shape[0]
    return a.reshape(t, ATT_HEADS, ATT_DH).transpose(1, 0, 2)


def _tokens_major(a):
    heads, t, dh = a.shape
    return a.transpose(1, 0, 2).reshape(t, heads * dh)


def _ffn_fwd(x_in, norm_g, w_gu, w_down, tag):
    h = _rms_fwd(x_in, norm_g, tag + "_norm")
    gu = _mm(h, w_gu, "nn", tag + "_gu")
    act = _swiglu_fwd(gu, tag + "_act")
    x_out = _mm(act, w_down, "nn", tag + "_down", res=x_in)
    return x_out, (h, gu, act)


def _ffn_bwd(dx_out, x_in, norm_g, w_gu, w_down, saved, tag):
    h, gu, act = saved
    dact = _mm(dx_out, w_down, "nt", tag + "_dact")
    g_down = _mm(act, dx_out, "tn", tag + "_gdown")
    dgu = _swiglu_bwd(gu, dact, tag + "_dgu")
    dh = _mm(dgu, w_gu, "nt", tag + "_dh")
    g_gu = _mm(h, dgu, "tn", tag + "_ggu")
    dx_in, g_norm = _rms_bwd(x_in, norm_g, dh, dx_out, tag + "_dnorm")
    return dx_in, g_gu, g_down, g_norm


def _local_step(x, target, w, s):
    t = x.shape[0]
    consts = _ret_consts(t)
    bd = jnp.asarray(np.kron(np.eye(ATT_HEADS, dtype=np.float32),
                             np.ones((ATT_DH, ATT_DH), np.float32))).astype(BF16)
    kg_t = jnp.tile(s["k_norm_g"], (1, ATT_HEADS))
    qg_t = jnp.tile(s["b_q_norm_g"], (1, ATT_HEADS))
    q_scale = ATT_DH ** -0.5

    h1 = _rms_fwd(x, s["a_norm_g"], "a_norm")
    proj = _mm(h1, w["a_w_in"], "nn", "a_proj")
    y, o_ret, states = _ret_fwd(proj, s["a_gn_g"], consts, "a_ret")
    x1 = _mm(y, w["a_w_o"], "nn", "a_out", res=x)
    x2, ffn_a = _ffn_fwd(x1, s["a_ffn_norm_g"], w["a_w_gu"], w["a_w_down"], "a_ffn")

    u = _rms_fwd(x2, s["kv_norm_g"], "kv_norm")
    kv = _mm(u, w["w_kv"], "nn", "kv_proj")
    k_raw, v_raw = kv[:, :D_MODEL], kv[:, D_MODEL:]
    kn = _hn_fwd(k_raw, kg_t, bd, 1.0, "k_hnorm")
    pad = ((0, 0), (K_PAD, 0), (0, 0))
    kp = jnp.pad(_heads_major(kn), pad)
    vp = jnp.pad(_heads_major(v_raw.astype(BF16)), pad)

    h3 = _rms_fwd(x2, s["b_norm_g"], "b_norm")
    q_raw = _mm(h3, w["b_w_q"], "nn", "b_q")
    qn = _heads_major(_hn_fwd(q_raw, qg_t, bd, q_scale, "q_hnorm"))
    bias = _bias_table(s["b_rel_bias"].reshape(ATT_HEADS, REL_TABLE), "rel")
    o_att = _tokens_major(_att_fwd(qn, kp, vp, bias, "b_att"))
    x3 = _mm(o_att, w["b_w_o"], "nn", "b_out", res=x2)
    x4, ffn_b = _ffn_fwd(x3, s["b_ffn_norm_g"], w["b_w_gu"], w["b_w_down"], "b_ffn")

    dy, loss = _loss_head(x4, target, "loss")

    g = {}
    dx3, g["b_w_gu"], g["b_w_down"], g["b_ffn_norm_g"] = _ffn_bwd(
        dy, x3, s["b_ffn_norm_g"], w["b_w_gu"], w["b_w_down"], ffn_b, "b_ffn")
    do_att = _mm(dx3, w["b_w_o"], "nt", "b_dout", out_dtype=BF16)
    g["b_w_o"] = _mm(o_att, dx3, "tn", "b_gout")
    dq_h, dkp, dvp, db = _att_bwd(qn, kp, vp, bias, _heads_major(do_att), "b_datt")
    g["b_rel_bias"] = _rel_reduce(db, "drel").reshape(1, -1)
    dq_raw, gq = _hn_bwd(q_raw, qg_t, bd, _tokens_major(dq_h), q_scale, "q_dhnorm")
    g["b_q_norm_g"] = gq.reshape(ATT_HEADS, ATT_DH).sum(axis=0, keepdims=True)
    dh3 = _mm(dq_raw, w["b_w_q"], "nt", "b_dq")
    g["b_w_q"] = _mm(h3, dq_raw, "tn", "b_gq")
    dx2, g["b_norm_g"] = _rms_bwd(x2, s["b_norm_g"], dh3, dx3, "b_dnorm")

    dk_raw, gk = _hn_bwd(k_raw, kg_t, bd, _tokens_major(dkp[:, K_PAD:]), 1.0, "k_dhnorm")
    g["k_norm_g"] = gk.reshape(ATT_HEADS, ATT_DH).sum(axis=0, keepdims=True)
    dkv = jnp.concatenate([dk_raw, _tokens_major(dvp[:, K_PAD:]).astype(BF16)], axis=-1)
    du = _mm(dkv, w["w_kv"], "nt", "kv_du")
    g["w_kv"] = _mm(u, dkv, "tn", "kv_g")
    dx2, g["kv_norm_g"] = _rms_bwd(x2, s["kv_norm_g"], du, dx2, "kv_dnorm")

    dx1, g["a_w_gu"], g["a_w_down"], g["a_ffn_norm_g"] = _ffn_bwd(
        dx2, x1, s["a_ffn_norm_g"], w["a_w_gu"], w["a_w_down"], ffn_a, "a_ffn")
    dy_ret = _mm(dx1, w["a_w_o"], "nt", "a_dout")
    g["a_w_o"] = _mm(y, dx1, "tn", "a_gout")
    dproj, g["a_gn_g"] = _ret_bwd(proj, s["a_gn_g"], o_ret, states, dy_ret, consts, "a_dret")
    dh1 = _mm(dproj, w["a_w_in"], "nt", "a_dproj")
    g["a_w_in"] = _mm(h1, dproj, "tn", "a_gin")
    grad_x, g["a_norm_g"] = _rms_bwd(x, s["a_norm_g"], dh1, dx1, "a_dnorm")
    return loss, grad_x, g


ARG_NAMES = ("x", "a_norm_g", "a_w_in", "a_gn_g", "a_w_o", "a_ffn_norm_g", "a_w_gu", "a_w_down",
             "kv_norm_g", "w_kv", "k_norm_g", "b_norm_g", "b_w_q", "b_q_norm_g", "b_rel_bias", "b_w_o",
             "b_ffn_norm_g", "b_w_gu", "b_w_down")
WEIGHT_NAMES = ARG_NAMES[1:]


def _big_shard(a):
    return a[0] if a.ndim == 3 else a


def kernel(x, a_norm_g, a_w_in, a_gn_g, a_w_o, a_ffn_norm_g, a_w_gu, a_w_down, kv_norm_g, w_kv, k_norm_g, b_norm_g, b_w_q, b_q_norm_g, b_rel_bias, b_w_o, b_ffn_norm_g, b_w_gu, b_w_down, loss_target, m_a_norm_g, m_a_w_in, m_a_gn_g, m_a_w_o, m_a_ffn_norm_g, m_a_w_gu, m_a_w_down, m_kv_norm_g, m_w_kv, m_k_norm_g, m_b_norm_g, m_b_w_q, m_b_q_norm_g, m_b_rel_bias, m_b_w_o, m_b_ffn_norm_g, m_b_w_gu, m_b_w_down, v_a_norm_g, v_a_w_in, v_a_gn_g, v_a_w_o, v_a_ffn_norm_g, v_a_w_gu, v_a_w_down, v_kv_norm_g, v_w_kv, v_k_norm_g, v_b_norm_g, v_b_w_q, v_b_q_norm_g, v_b_rel_bias, v_b_w_o, v_b_ffn_norm_g, v_b_w_gu, v_b_w_down):
    args = (x, a_norm_g, a_w_in, a_gn_g, a_w_o, a_ffn_norm_g, a_w_gu, a_w_down, kv_norm_g, w_kv, k_norm_g,
            b_norm_g, b_w_q, b_q_norm_g, b_rel_bias, b_w_o, b_ffn_norm_g, b_w_gu, b_w_down)
    p = dict(zip(ARG_NAMES, args))
    m_all = dict(zip(WEIGHT_NAMES, (m_a_norm_g, m_a_w_in, m_a_gn_g, m_a_w_o, m_a_ffn_norm_g, m_a_w_gu,
                                    m_a_w_down, m_kv_norm_g, m_w_kv, m_k_norm_g, m_b_norm_g, m_b_w_q,
                                    m_b_q_norm_g, m_b_rel_bias, m_b_w_o, m_b_ffn_norm_g, m_b_w_gu, m_b_w_down)))
    v_all = dict(zip(WEIGHT_NAMES, (v_a_norm_g, v_a_w_in, v_a_gn_g, v_a_w_o, v_a_ffn_norm_g, v_a_w_gu,
                                    v_a_w_down, v_kv_norm_g, v_w_kv, v_k_norm_g, v_b_norm_g, v_b_w_q,
                                    v_b_q_norm_g, v_b_rel_bias, v_b_w_o, v_b_ffn_norm_g, v_b_w_gu, v_b_w_down)))
    xi, yi, ci = _my_place()
    me = 4 * xi + 2 * yi + ci

    big_local = {n: _big_shard(p[n]) for n, _, _ in BIG}
    w_gathered = _all_gather(_pack_shards({n: a.astype(BF16) for n, a in big_local.items()}), "gather_w")
    w_full = _unpack_gathered(w_gathered)
    small_local = _pack_small({n: p[n] for n, _, _ in SMALL}, local=True)
    small_gathered = _all_gather(small_local, "gather_small")
    s_full = {}
    pos = 0
    flat_g = small_gathered.reshape(N_DEV, -1)
    for n, length, sharded in SMALL:
        ln = length // N_DEV if sharded else length
        piece = flat_g[:, pos:pos + ln]
        s_full[n] = piece.reshape(1, -1) if sharded else p[n].reshape(1, -1)
        pos += ln

    loss, grad_x, g = _local_step(x[0], loss_target[0], w_full, s_full)
    loss = lax.psum(loss[0, 0], ("x", "y", "c"))

    g_packed = _pack_full_by_shard(g)
    from_sibling = _swap_with_sibling(g_packed, "rs_sibling")
    chip_sums = _pair_add(g_packed, from_sibling, jnp.reshape(ci, (1,)).astype(jnp.int32), "rs_pair_add")
    from_chips = _scatter_to_chips(chip_sums, "rs_chips")
    g_big = _unpack_shards(_sum_leading(from_chips, "rs_sum"))

    g_small_all = _all_gather(_pack_small({n: g[n] for n, _, _ in SMALL}, local=False), "gather_gsmall")
    g_small = _unpack_small(_sum_leading(g_small_all, "gsmall_sum"), local=False)
    for n, length, sharded in SMALL:
        if sharded:
            g_small[n] = lax.dynamic_slice(g_small[n], (me * (length // N_DEV),), (length // N_DEV,))

    grads, deltas, new_m, new_v = {}, {}, {}, {}
    for n, _, _ in BIG:
        grads[n] = g_big[n].reshape(p[n].shape)
        d, nm, nv = _adamw(big_local[n], g_big[n], _big_shard(m_all[n]), _big_shard(v_all[n]), "adamw_" + n)
        deltas[n], new_m[n], new_v[n] = d.reshape(p[n].shape), nm.reshape(p[n].shape), nv.reshape(p[n].shape)
    pk = lambda src: _pack_small({n: src[n] for n, _, _ in SMALL}, local=True)
    d_s, nm_s, nv_s = _adamw(small_local, pk(g_small), pk(m_all), pk(v_all), "adamw_small")
    d_s, nm_s, nv_s = (_unpack_small(a, local=True) for a in (d_s, nm_s, nv_s))
    for n, _, _ in SMALL:
        grads[n] = g_small[n].reshape(p[n].shape)
        deltas[n], new_m[n], new_v[n] = (a[n].reshape(p[n].shape) for a in (d_s, nm_s, nv_s))

    return (loss, grad_x[None], *[grads[n] for n in WEIGHT_NAMES], *[deltas[n] for n in WEIGHT_NAMES],
            *[new_m[n] for n in WEIGHT_NAMES], *[new_v[n] for n in WEIGHT_NAMES])
```

```python
import numpy as np
import jax
import jax.numpy as jnp
from jax import lax
from jax.experimental import pallas as pl
from jax.experimental.pallas import tpu as pltpu

F32 = jnp.float32
BF16 = jnp.bfloat16

N_DEV = 8
D_MODEL = 1024
CHUNK = 64
EPS = 1e-6
RET_HEADS, RET_DK, RET_DV = 4, 256, 512
RET_Q_COLS = RET_HEADS * RET_DK
RET_V_COLS = RET_HEADS * RET_DV
ATT_HEADS, ATT_DH = 16, 64
PAST_CHUNKS = 8
REL_CLIP = 256
REL_TABLE = 2 * REL_CLIP + 1
FFN_HIDDEN = 2816
ROPE_BASE = 10000.0
LANES = 128
Q_BLOCK = 256
K_PAD = PAST_CHUNKS * CHUNK
K_WINDOW = Q_BLOCK + K_PAD
REL_BLK = 128
REL_DELTAS = Q_BLOCK // REL_BLK + K_WINDOW // REL_BLK - 1
REL_PAD = 640
NEG = -1e30
VMEM_LIMIT_V7X = 56 * 1024 * 1024
ADAM_LR, ADAM_B1, ADAM_B2, ADAM_EPS, ADAM_WD, ADAM_STEP = 1e-3, 0.9, 0.999, 1e-8, 0.01, 10
MESH = pl.DeviceIdType.MESH
ANY = pl.BlockSpec(memory_space=pl.ANY)


def _params(*semantics):
    return pltpu.CompilerParams(dimension_semantics=semantics, vmem_limit_bytes=VMEM_LIMIT_V7X)


def _pick(dim, cap, align):
    best = None
    for t in range(align, min(dim, cap) + 1, align):
        if dim % t == 0:
            best = t
    assert best is not None, (dim, cap, align)
    return best


def _dot(a, b):
    return lax.dot_general(a, b, (((1,), (0,)), ((), ())), preferred_element_type=F32)


def _dot_nt(a, b):
    return lax.dot_general(a, b, (((1,), (1,)), ((), ())), preferred_element_type=F32)


def _dot_tn(a, b):
    return lax.dot_general(a, b, (((0,), (0,)), ((), ())), preferred_element_type=F32)


def _split2(x):
    hi = x.astype(BF16)
    lo = (x - hi.astype(F32)).astype(BF16)
    return hi, lo


def _split3(x):
    hi = x.astype(BF16)
    r = x - hi.astype(F32)
    mid = r.astype(BF16)
    lo = (r - mid.astype(F32)).astype(BF16)
    return hi, mid, lo


def _sigmoid(x):
    return 1.0 / (1.0 + jnp.exp(-x))


def _accumulate(ref, part, step):
    @pl.when(step == 0)
    def _():
        ref[...] = part

    @pl.when(step > 0)
    def _():
        ref[...] += part


def _mm(a, b, mode, name, out_dtype=F32, res=None, out_block=None):
    a3, b3 = a.ndim == 3, b.ndim == 3
    tm = tn = tk = None
    if mode in ("nn", "nt"):
        if a3:
            nba, m, tk = a.shape
            k = nba * tk
        else:
            m, k = a.shape
    else:
        if a3:
            nba, k, tm = a.shape
            m = nba * tm
        else:
            k, m = a.shape
    if mode in ("nn", "tn"):
        if b3:
            nbb, kb, tn = b.shape
            n = nbb * tn
        else:
            kb, n = b.shape
        assert kb == k, (a.shape, b.shape, mode)
    else:
        if b3:
            nbb, n, tkb = b.shape
            assert nbb * tkb == k and tk in (None, tkb), (a.shape, b.shape, mode)
            tk = tkb
        else:
            n, kb = b.shape
            assert kb == k, (a.shape, b.shape, mode)
    if out_block is not None:
        assert tn in (None, out_block)
        tn = out_block
    tm = tm or _pick(m, 512, 128 if mode == "tn" else 16)
    tn = tn or _pick(n, 1408, 128)
    tk = tk or _pick(k, 1024 if mode == "tn" else 2048, 128)
    nk = k // tk
    dot = {"nn": _dot, "nt": _dot_nt, "tn": _dot_tn}[mode]

    def load(ref):
        return (ref[0] if len(ref.shape) == 3 else ref[...]).astype(BF16)

    def body(a_ref, b_ref, *rest):
        if res is None:
            o_ref, acc_ref = rest
        else:
            r_ref, o_ref, acc_ref = rest
        kk = pl.program_id(2)
        part = dot(load(a_ref), load(b_ref))

        def finish(total):
            if res is not None:
                total = r_ref[...] + total
            if out_block is None:
                o_ref[...] = total.astype(out_dtype)
            else:
                o_ref[0] = total.astype(out_dtype)

        if nk == 1:
            finish(part)
        else:
            @pl.when(kk == 0)
            def _():
                acc_ref[...] = part

            @pl.when(jnp.logical_and(kk > 0, kk < nk - 1))
            def _():
                acc_ref[...] += part

            @pl.when(kk == nk - 1)
            def _():
                finish(acc_ref[...] + part)

    if mode in ("nn", "nt"):
        a_spec = (pl.BlockSpec((1, tm, tk), lambda i, j, kk: (kk, i, 0)) if a3
                  else pl.BlockSpec((tm, tk), lambda i, j, kk: (i, kk)))
    else:
        a_spec = (pl.BlockSpec((1, tk, tm), lambda i, j, kk: (i, kk, 0)) if a3
                  else pl.BlockSpec((tk, tm), lambda i, j, kk: (kk, i)))
    if mode in ("nn", "tn"):
        b_spec = (pl.BlockSpec((1, tk, tn), lambda i, j, kk: (j, kk, 0)) if b3
                  else pl.BlockSpec((tk, tn), lambda i, j, kk: (kk, j)))
    else:
        b_spec = (pl.BlockSpec((1, tn, tk), lambda i, j, kk: (kk, j, 0)) if b3
                  else pl.BlockSpec((tn, tk), lambda i, j, kk: (j, kk)))
    if out_block is None:
        o_spec = pl.BlockSpec((tm, tn), lambda i, j, kk: (i, j))
        out_shape = jax.ShapeDtypeStruct((m, n), out_dtype)
    else:
        assert res is None
        o_spec = pl.BlockSpec((1, tm, tn), lambda i, j, kk: (j, i, 0))
        out_shape = jax.ShapeDtypeStruct((n // tn, m, tn), out_dtype)
    in_specs = [a_spec, b_spec] + ([o_spec] if res is not None else [])
    args = (a, b) + ((res,) if res is not None else ())
    return pl.pallas_call(
        body, name=name, grid=(m // tm, n // tn, nk),
        in_specs=in_specs, out_specs=o_spec, out_shape=out_shape,
        scratch_shapes=[pltpu.VMEM((tm, tn), F32)],
        compiler_params=_params("parallel", "parallel", "arbitrary"),
    )(*args)


def _rms_fwd(x, g, name):
    t, d = x.shape
    tm = _pick(t, 512, 16)

    def body(x_ref, g_ref, o_ref):
        xv = x_ref[...]
        rstd = lax.rsqrt(jnp.mean(xv * xv, axis=-1, keepdims=True) + EPS)
        o_ref[...] = (xv * rstd * g_ref[...]).astype(BF16)

    return pl.pallas_call(
        body, name=name, grid=(t // tm,),
        in_specs=[pl.BlockSpec((tm, d), lambda i: (i, 0)), pl.BlockSpec((1, d), lambda i: (0, 0))],
        out_specs=pl.BlockSpec((tm, d), lambda i: (i, 0)),
        out_shape=jax.ShapeDtypeStruct((t, d), BF16),
        compiler_params=_params("parallel"),
    )(x, g)


def _rms_bwd(x, g, dh, dres, name):
    t, d = x.shape
    tm = _pick(t, 512, 16)

    def body(x_ref, g_ref, dh_ref, dres_ref, dx_ref, dg_ref):
        xv = x_ref[...]
        rstd = lax.rsqrt(jnp.mean(xv * xv, axis=-1, keepdims=True) + EPS)
        xh = xv * rstd
        dhv = dh_ref[...]
        dyg = dhv * g_ref[...]
        c = jnp.mean(dyg * xh, axis=-1, keepdims=True)
        dx_ref[...] = dres_ref[...] + rstd * (dyg - xh * c)
        _accumulate(dg_ref, jnp.sum(dhv * xh, axis=0, keepdims=True), pl.program_id(0))

    row = pl.BlockSpec((tm, d), lambda i: (i, 0))
    vec = pl.BlockSpec((1, d), lambda i: (0, 0))
    return pl.pallas_call(
        body, name=name, grid=(t // tm,),
        in_specs=[row, vec, row, row], out_specs=[row, vec],
        out_shape=[jax.ShapeDtypeStruct((t, d), F32), jax.ShapeDtypeStruct((1, d), F32)],
        compiler_params=_params("arbitrary"),
    )(x, g, dh, dres)


def _seg_mean(v, bd):
    hi, lo = _split2(v)
    return (_dot(hi, bd) + _dot(lo, bd)) * (1.0 / ATT_DH)


def _hn_bwd_math(xv, gv, bdv, dyv, scale):
    rstd = lax.rsqrt(_seg_mean(xv * xv, bdv) + EPS)
    xh = xv * rstd
    dyn = dyv * scale
    dyg = dyn * gv
    dx = rstd * (dyg - xh * _seg_mean(dyg * xh, bdv))
    return dx, jnp.sum(dyn * xh, axis=0, keepdims=True)


def _q_hnorm(x, g_tiled, bd, scale, name):
    t, d = x.shape
    tm = _pick(t, 512, 16)

    def body(x_ref, g_ref, bd_ref, o_ref):
        xv = x_ref[...]
        rstd = lax.rsqrt(_seg_mean(xv * xv, bd_ref[...]) + EPS)
        o_ref[...] = (xv * rstd * g_ref[...] * scale).astype(BF16)

    return pl.pallas_call(
        body, name=name, grid=(t // tm,),
        in_specs=[pl.BlockSpec((tm, d), lambda i: (i, 0)), pl.BlockSpec((1, d), lambda i: (0, 0)),
                  pl.BlockSpec((d, d), lambda i: (0, 0))],
        out_specs=pl.BlockSpec((tm, d), lambda i: (i, 0)),
        out_shape=jax.ShapeDtypeStruct((t, d), BF16),
        compiler_params=_params("parallel"),
    )(x, g_tiled, bd)


def _q_dhnorm(x, g_tiled, bd, dy, scale, name):
    t, d = x.shape
    tm = _pick(t, 512, 16)

    def body(x_ref, g_ref, bd_ref, dy_ref, dx_ref, dg_ref):
        dx, part = _hn_bwd_math(x_ref[...], g_ref[...], bd_ref[...], dy_ref[...], scale)
        dx_ref[...] = dx.astype(BF16)
        _accumulate(dg_ref, part, pl.program_id(0))

    row = pl.BlockSpec((tm, d), lambda i: (i, 0))
    vec = pl.BlockSpec((1, d), lambda i: (0, 0))
    return pl.pallas_call(
        body, name=name, grid=(t // tm,),
        in_specs=[row, vec, pl.BlockSpec((d, d), lambda i: (0, 0)), row],
        out_specs=[row, vec],
        out_shape=[jax.ShapeDtypeStruct((t, d), BF16), jax.ShapeDtypeStruct((1, d), F32)],
        compiler_params=_params("arbitrary"),
    )(x, g_tiled, bd, dy)


def _kv_prep(kv, g_tiled, bd, name):
    t = kv.shape[0]
    d = D_MODEL
    tm = K_PAD
    assert t % tm == 0

    def body(k_ref, v_ref, g_ref, bd_ref, kp_ref, vp_ref):
        i = pl.program_id(0)

        @pl.when(i == 0)
        def _():
            kp_ref[...] = jnp.zeros_like(kp_ref)
            vp_ref[...] = jnp.zeros_like(vp_ref)

        @pl.when(i > 0)
        def _():
            xv = k_ref[...]
            rstd = lax.rsqrt(_seg_mean(xv * xv, bd_ref[...]) + EPS)
            kp_ref[...] = (xv * rstd * g_ref[...]).astype(BF16)
            vp_ref[...] = v_ref[...].astype(BF16)

    shp = jax.ShapeDtypeStruct((t + K_PAD, d), BF16)
    out = pl.BlockSpec((tm, d), lambda i: (i, 0))
    return pl.pallas_call(
        body, name=name, grid=(t // tm + 1,),
        in_specs=[pl.BlockSpec((tm, d), lambda i: (jnp.maximum(i - 1, 0), 0)),
                  pl.BlockSpec((tm, d), lambda i: (jnp.maximum(i - 1, 0), 1)),
                  pl.BlockSpec((1, d), lambda i: (0, 0)), pl.BlockSpec((d, d), lambda i: (0, 0))],
        out_specs=[out, out], out_shape=[shp, shp],
        compiler_params=_params("arbitrary"),
    )(kv, kv, g_tiled, bd)


def _kv_dprep(kv, g_tiled, bd, dkp, dvp, name):
    t = kv.shape[0]
    d = D_MODEL
    tm = K_PAD

    def body(k_ref, g_ref, bd_ref, dk_ref, dv_ref, o_ref, dg_ref):
        dx, part = _hn_bwd_math(k_ref[...], g_ref[...], bd_ref[...], dk_ref[...], 1.0)
        o_ref[:, :d] = dx.astype(BF16)
        o_ref[:, d:] = dv_ref[...].astype(BF16)
        _accumulate(dg_ref, part, pl.program_id(0))

    vec = pl.BlockSpec((1, d), lambda i: (0, 0))
    padded = pl.BlockSpec((tm, d), lambda i: (i + 1, 0))
    return pl.pallas_call(
        body, name=name, grid=(t // tm,),
        in_specs=[pl.BlockSpec((tm, d), lambda i: (i, 0)), vec, pl.BlockSpec((d, d), lambda i: (0, 0)),
                  padded, padded],
        out_specs=[pl.BlockSpec((tm, 2 * d), lambda i: (i, 0)), vec],
        out_shape=[jax.ShapeDtypeStruct((t, 2 * d), BF16), jax.ShapeDtypeStruct((1, d), F32)],
        compiler_params=_params("arbitrary"),
    )(kv, g_tiled, bd, dkp, dvp)


def _swiglu_fwd(gu, name):
    nb, t, fb = gu.shape
    half = nb // 2
    tm = _pick(t, 512, 16)

    def body(gu_ref, o_ref):
        gv = gu_ref[0, 0]
        o_ref[0] = (gv * _sigmoid(gv) * gu_ref[1, 0]).astype(BF16)

    return pl.pallas_call(
        body, name=name, grid=(half, t // tm),
        in_specs=[pl.BlockSpec((2, 1, tm, fb), lambda j, i: (0, j, i, 0))],
        out_specs=pl.BlockSpec((1, tm, fb), lambda j, i: (j, i, 0)),
        out_shape=jax.ShapeDtypeStruct((half, t, fb), BF16),
        compiler_params=_params("parallel", "parallel"),
    )(gu.reshape(2, half, t, fb))


def _swiglu_bwd(gu, dact, name):
    nb, t, fb = gu.shape
    half = nb // 2
    tm = _pick(t, 512, 16)

    def body(gu_ref, da_ref, o_ref):
        gv = gu_ref[0, 0]
        sg = _sigmoid(gv)
        dav = da_ref[0]
        o_ref[0, 0] = (dav * gu_ref[1, 0] * (sg * (1.0 + gv * (1.0 - sg)))).astype(BF16)
        o_ref[1, 0] = (dav * (gv * sg)).astype(BF16)

    pair = pl.BlockSpec((2, 1, tm, fb), lambda j, i: (0, j, i, 0))
    return pl.pallas_call(
        body, name=name, grid=(half, t // tm),
        in_specs=[pair, pl.BlockSpec((1, tm, fb), lambda j, i: (j, i, 0))],
        out_specs=pair,
        out_shape=jax.ShapeDtypeStruct((2, half, t, fb), BF16),
        compiler_params=_params("parallel", "parallel"),
    )(gu.reshape(2, half, t, fb), dact).reshape(nb, t, fb)


def _loss_head(y, target, name):
    t, d = y.shape
    tm = _pick(t, 512, 16)

    def body(y_ref, t_ref, dy_ref, l_ref):
        diff = y_ref[...] - t_ref[...]
        dy_ref[...] = diff * (1.0 / d)
        part = jnp.sum(jnp.sum(diff * diff, axis=-1, keepdims=True), axis=0, keepdims=True) * (0.5 / d)
        _accumulate(l_ref, part, pl.program_id(0))

    row = pl.BlockSpec((tm, d), lambda i: (i, 0))
    return pl.pallas_call(
        body, name=name, grid=(t // tm,),
        in_specs=[row, row], out_specs=[row, pl.BlockSpec((1, 1), lambda i: (0, 0))],
        out_shape=[jax.ShapeDtypeStruct((t, d), F32), jax.ShapeDtypeStruct((1, 1), F32)],
        compiler_params=_params("arbitrary"),
    )(y, target)


def _ret_consts(t):
    h = np.arange(RET_HEADS, dtype=np.float32)
    lg = np.log(np.float32(1.0) - np.float32(2.0) ** (np.float32(-5.0) - h)).astype(np.float32)
    tt = np.arange(CHUNK, dtype=np.float32)
    intra = np.exp(lg[:, None, None] * np.abs(tt[:, None] - tt[None, :])).astype(np.float32)
    q_dec = np.exp(lg[:, None] * (tt + 1.0)).astype(np.float32)
    k_dec = np.exp(lg[:, None] * (CHUNK - 1.0 - tt)).astype(np.float32)
    s_dec = [float(v) for v in np.exp(lg * np.float32(CHUNK)).astype(np.float32)]
    qd = np.broadcast_to(q_dec[:, :, None], (RET_HEADS, CHUNK, RET_DK)).copy()
    kd = np.broadcast_to(k_dec[:, :, None], (RET_HEADS, CHUNK, RET_DK)).copy()
    half = RET_DK // 2
    inv_freq = ROPE_BASE ** (-jnp.arange(half, dtype=F32) / half)
    ang = jnp.arange(t).astype(F32)[:, None] * inv_freq[None, :]
    return jnp.asarray(intra), jnp.asarray(qd), jnp.asarray(kd), s_dec, jnp.cos(ang), jnp.sin(ang)


def _rope(x, cos, sin):
    half = RET_DK // 2
    x1, x2 = x[:, :half], x[:, half:]
    return jnp.concatenate([x1 * cos - x2 * sin, x1 * sin + x2 * cos], axis=-1)


def _unrope(d, cos, sin):
    half = RET_DK // 2
    d1, d2 = d[:, :half], d[:, half:]
    return jnp.concatenate([d1 * cos + d2 * sin, d2 * cos - d1 * sin], axis=-1)


def _ret_slices(h):
    q = slice(h * RET_DK, (h + 1) * RET_DK)
    k = slice(RET_Q_COLS + h * RET_DK, RET_Q_COLS + (h + 1) * RET_DK)
    v = slice(2 * RET_Q_COLS + h * RET_DV, 2 * RET_Q_COLS + (h + 1) * RET_DV)
    g = slice(2 * RET_Q_COLS + RET_V_COLS + h * RET_DV, 2 * RET_Q_COLS + RET_V_COLS + (h + 1) * RET_DV)
    o = slice(h * RET_DV, (h + 1) * RET_DV)
    return q, k, v, g, o


def _ret_fwd(proj, gn, consts, name):
    t, cols = proj.shape
    n = t // CHUNK
    intra, qd, kd, s_dec, cos, sin = consts
    k_scale = RET_DK ** -0.5

    def body(p_ref, cos_ref, sin_ref, intra_ref, qd_ref, kd_ref, gn_ref, y_ref, o_ref, st_ref, state):
        i = pl.program_id(0)

        @pl.when(i == 0)
        def _():
            state[...] = jnp.zeros_like(state)

        cosv, sinv = cos_ref[...], sin_ref[...]
        for h in range(RET_HEADS):
            qs, ks, vs, gs, os_ = _ret_slices(h)
            qr = _rope(p_ref[:, qs], cosv, sinv)
            kr = _rope(p_ref[:, ks], cosv, sinv) * k_scale
            vb = p_ref[:, vs].astype(BF16)
            gv = p_ref[:, gs]
            scores = _dot_nt(qr.astype(BF16), kr.astype(BF16)) * intra_ref[h]
            s_old = state[h]
            s_old_b = s_old.astype(BF16)
            st_ref[0, h] = s_old_b
            o = _dot(scores.astype(BF16), vb) + _dot((qr * qd_ref[h]).astype(BF16), s_old_b)
            state[h] = s_old * s_dec[h] + _dot_tn((kr * kd_ref[h]).astype(BF16), vb)
            rstd = lax.rsqrt(jnp.mean(o * o, axis=-1, keepdims=True) + EPS)
            on = o * rstd * gn_ref[:, os_]
            o_ref[:, os_] = o
            y_ref[:, os_] = (gv * _sigmoid(gv) * on).astype(BF16)

    full3 = lambda a: pl.BlockSpec(a.shape, lambda i: (0, 0, 0))
    return pl.pallas_call(
        body, name=name, grid=(n,),
        in_specs=[pl.BlockSpec((CHUNK, cols), lambda i: (i, 0)),
                  pl.BlockSpec((CHUNK, RET_DK // 2), lambda i: (i, 0)),
                  pl.BlockSpec((CHUNK, RET_DK // 2), lambda i: (i, 0)),
                  full3(intra), full3(qd), full3(kd),
                  pl.BlockSpec((1, RET_V_COLS), lambda i: (0, 0))],
        out_specs=[pl.BlockSpec((CHUNK, RET_V_COLS), lambda i: (i, 0)),
                   pl.BlockSpec((CHUNK, RET_V_COLS), lambda i: (i, 0)),
                   pl.BlockSpec((1, RET_HEADS, RET_DK, RET_DV), lambda i: (i, 0, 0, 0))],
        out_shape=[jax.ShapeDtypeStruct((t, RET_V_COLS), BF16),
                   jax.ShapeDtypeStruct((t, RET_V_COLS), F32),
                   jax.ShapeDtypeStruct((n, RET_HEADS, RET_DK, RET_DV), BF16)],
        scratch_shapes=[pltpu.VMEM((RET_HEADS, RET_DK, RET_DV), F32)],
        compiler_params=_params("arbitrary"),
    )(proj, cos, sin, intra, qd, kd, gn)


def _ret_bwd(proj, gn, o_saved, states, dy, consts, name):
    t, cols = proj.shape
    n = t // CHUNK
    intra, qd, kd, s_dec, cos, sin = consts
    k_scale = RET_DK ** -0.5

    def body(p_ref, cos_ref, sin_ref, intra_ref, qd_ref, kd_ref, gn_ref, o_ref, st_ref, dy_ref,
             dp_ref, dgn_ref, dstate):
        i = pl.program_id(0)

        @pl.when(i == 0)
        def _():
            dstate[...] = jnp.zeros_like(dstate)

        cosv, sinv = cos_ref[...], sin_ref[...]
        dgn_parts = []
        for h in range(RET_HEADS):
            qs, ks, vs, gs, os_ = _ret_slices(h)
            qr = _rope(p_ref[:, qs], cosv, sinv)
            kr = _rope(p_ref[:, ks], cosv, sinv) * k_scale
            qb, kb = qr.astype(BF16), kr.astype(BF16)
            vb = p_ref[:, vs].astype(BF16)
            gv = p_ref[:, gs]
            ov = o_ref[:, os_]
            dyv = dy_ref[:, os_]
            gnv = gn_ref[:, os_]
            sg = _sigmoid(gv)
            rstd = lax.rsqrt(jnp.mean(ov * ov, axis=-1, keepdims=True) + EPS)
            oh = ov * rstd
            d_on = dyv * (gv * sg)
            dg = dyv * (oh * gnv) * (sg * (1.0 + gv * (1.0 - sg)))
            dgn_parts.append(jnp.sum(d_on * oh, axis=0, keepdims=True))
            d_oh = d_on * gnv
            do = rstd * (d_oh - oh * jnp.mean(d_oh * oh, axis=-1, keepdims=True))
            dob = do.astype(BF16)
            mask = intra_ref[h]
            a_b = (_dot_nt(qb, kb) * mask).astype(BF16)
            da_b = (_dot_nt(dob, vb) * mask).astype(BF16)
            ds_new = dstate[h]
            ds_new_b = ds_new.astype(BF16)
            s_old_b = st_ref[0, h]
            qdv, kdv = qd_ref[h], kd_ref[h]
            dv = _dot_tn(a_b, dob) + _dot((kr * kdv).astype(BF16), ds_new_b)
            dqr = _dot(da_b, kb) + _dot_nt(dob, s_old_b) * qdv
            dkr = _dot_tn(da_b, qb) + _dot_nt(vb, ds_new_b) * kdv
            dstate[h] = ds_new * s_dec[h] + _dot_tn((qr * qdv).astype(BF16), dob)
            dp_ref[:, qs] = _unrope(dqr, cosv, sinv).astype(BF16)
            dp_ref[:, ks] = _unrope(dkr * k_scale, cosv, sinv).astype(BF16)
            dp_ref[:, vs] = dv.astype(BF16)
            dp_ref[:, gs] = dg.astype(BF16)
        _accumulate(dgn_ref, jnp.concatenate(dgn_parts, axis=-1), i)

    rev = lambda i: (n - 1 - i, 0)
    full3 = lambda a: pl.BlockSpec(a.shape, lambda i: (0, 0, 0))
    return pl.pallas_call(
        body, name=name, grid=(n,),
        in_specs=[pl.BlockSpec((CHUNK, cols), rev),
                  pl.BlockSpec((CHUNK, RET_DK // 2), rev),
                  pl.BlockSpec((CHUNK, RET_DK // 2), rev),
                  full3(intra), full3(qd), full3(kd),
                  pl.BlockSpec((1, RET_V_COLS), lambda i: (0, 0)),
                  pl.BlockSpec((CHUNK, RET_V_COLS), rev),
                  pl.BlockSpec((1, RET_HEADS, RET_DK, RET_DV), lambda i: (n - 1 - i, 0, 0, 0)),
                  pl.BlockSpec((CHUNK, RET_V_COLS), rev)],
        out_specs=[pl.BlockSpec((CHUNK, cols), rev),
                   pl.BlockSpec((1, RET_V_COLS), lambda i: (0, 0))],
        out_shape=[jax.ShapeDtypeStruct((t, cols), BF16),
                   jax.ShapeDtypeStruct((1, RET_V_COLS), F32)],
        scratch_shapes=[pltpu.VMEM((RET_HEADS, RET_DK, RET_DV), F32)],
        compiler_params=_params("arbitrary"),
    )(proj, cos, sin, intra, qd, kd, gn, o_saved, states, dy)


def _att_common(q_ref, kp_ref, vp_ref):
    blk = pl.program_id(1)
    start = pl.multiple_of(blk * Q_BLOCK, Q_BLOCK)
    kw = kp_ref[pl.ds(start, K_WINDOW), :]
    vw = vp_ref[pl.ds(start, K_WINDOW), :]
    kvalid = blk * Q_BLOCK - K_PAD + lax.broadcasted_iota(jnp.int32, (1, K_WINDOW), 1) >= 0
    lane = lax.broadcasted_iota(jnp.int32, (1, LANES), 1)
    return start, q_ref[...], kw, vw, kvalid, (lane < ATT_DH, lane >= ATT_DH)


def _att_probs(qm, kw, bias, kvalid):
    s = jnp.where(kvalid, _dot_nt(qm, kw) + bias, NEG)
    e = jnp.exp(s - jnp.max(s, axis=-1, keepdims=True))
    return e * (1.0 / jnp.sum(e, axis=-1, keepdims=True))


def _att_specs(t, tp):
    qspec = pl.BlockSpec((Q_BLOCK, LANES), lambda h, i: (i, h))
    kspec = pl.BlockSpec((tp, LANES), lambda h, i: (0, h))
    bspec = pl.BlockSpec((2, Q_BLOCK, K_WINDOW), lambda h, i: (h, 0, 0))
    return qspec, kspec, bspec


def _att_fwd(q, kp, vp, bias, name):
    t, d = q.shape
    tp = kp.shape[0]

    def body(q_ref, kp_ref, vp_ref, bias_ref, o_ref):
        _, q2, kw, vw, kvalid, sel = _att_common(q_ref, kp_ref, vp_ref)
        outs = []
        for hh in range(2):
            p = _att_probs(jnp.where(sel[hh], q2, 0), kw, bias_ref[hh], kvalid)
            outs.append(_dot(p.astype(BF16), vw))
        o_ref[...] = jnp.where(sel[0], outs[0], outs[1]).astype(BF16)

    qspec, kspec, bspec = _att_specs(t, tp)
    return pl.pallas_call(
        body, name=name, grid=(d // LANES, t // Q_BLOCK),
        in_specs=[qspec, kspec, kspec, bspec], out_specs=qspec,
        out_shape=jax.ShapeDtypeStruct((t, d), BF16),
        compiler_params=_params("parallel", "arbitrary"),
    )(q, kp, vp, bias)


def _att_bwd(q, kp, vp, bias, do, name):
    t, d = q.shape
    tp = kp.shape[0]

    def body(q_ref, kp_ref, vp_ref, bias_ref, do_ref, dq_ref, dkp_ref, dvp_ref, db_ref):
        @pl.when(pl.program_id(1) == 0)
        def _():
            dkp_ref[...] = jnp.zeros_like(dkp_ref)
            dvp_ref[...] = jnp.zeros_like(dvp_ref)
            db_ref[...] = jnp.zeros_like(db_ref)

        start, q2, kw, vw, kvalid, sel = _att_common(q_ref, kp_ref, vp_ref)
        do2 = do_ref[...]
        dqs, dk, dv = [], None, None
        for hh in range(2):
            qm = jnp.where(sel[hh], q2, 0)
            dom = jnp.where(sel[hh], do2, 0)
            p = _att_probs(qm, kw, bias_ref[hh], kvalid)
            dp = _dot_nt(dom, vw)
            ds = p * (dp - jnp.sum(dp * p, axis=-1, keepdims=True))
            db_ref[hh] += ds
            dsb = ds.astype(BF16)
            dqs.append(_dot(dsb, kw))
            dk_h = _dot_tn(dsb, qm)
            dv_h = _dot_tn(p.astype(BF16), dom)
            dk = dk_h if dk is None else dk + dk_h
            dv = dv_h if dv is None else dv + dv_h
        dq_ref[...] = jnp.where(sel[0], dqs[0], dqs[1])
        dkp_ref[pl.ds(start, K_WINDOW), :] += dk
        dvp_ref[pl.ds(start, K_WINDOW), :] += dv

    qspec, kspec, bspec = _att_specs(t, tp)
    return pl.pallas_call(
        body, name=name, grid=(d // LANES, t // Q_BLOCK),
        in_specs=[qspec, kspec, kspec, bspec, qspec],
        out_specs=[qspec, kspec, kspec, bspec],
        out_shape=[jax.ShapeDtypeStruct((t, d), F32),
                   jax.ShapeDtypeStruct((tp, d), F32),
                   jax.ShapeDtypeStruct((tp, d), F32),
                   jax.ShapeDtypeStruct((ATT_HEADS, Q_BLOCK, K_WINDOW), F32)],
        compiler_params=_params("parallel", "arbitrary"),
    )(q, kp, vp, bias, do)


def _rel_bin_matrix():
    rows = REL_DELTAS * 2 * REL_BLK
    rho = lax.broadcasted_iota(jnp.int32, (rows, REL_PAD), 0)
    col = lax.broadcasted_iota(jnp.int32, (rows, REL_PAD), 1)
    assert 2 * REL_BLK == 256
    delta = rho >> 8
    c = rho & 255
    dist = K_PAD + REL_BLK * (delta - (K_WINDOW // REL_BLK - 1)) + (c - (REL_BLK - 1))
    idx = jnp.clip(dist, -REL_CLIP, REL_CLIP) + REL_CLIP
    return col == idx


def _rel_shift_matrix(r):
    c = lax.broadcasted_iota(jnp.int32, (2 * REL_BLK, REL_BLK), 0)
    s = lax.broadcasted_iota(jnp.int32, (2 * REL_BLK, REL_BLK), 1)
    return c == r - s + (REL_BLK - 1)


def _rel_expand(rel_pad, name):
    heads = rel_pad.shape[0]
    rows = REL_DELTAS * 2 * REL_BLK

    def body_bin(r_ref, o_ref):
        onehot = jnp.where(_rel_bin_matrix(), 1.0, 0.0).astype(BF16)
        hi, mid, lo = _split3(r_ref[...])
        o_ref[...] = _dot_nt(hi, onehot) + _dot_nt(mid, onehot) + _dot_nt(lo, onehot)

    by_delta = pl.pallas_call(
        body_bin, name=name + "_bin",
        out_shape=jax.ShapeDtypeStruct((heads, rows), F32),
        compiler_params=pltpu.CompilerParams(vmem_limit_bytes=VMEM_LIMIT_V7X),
    )(rel_pad)
    by_delta = by_delta.reshape(heads * REL_DELTAS, 2 * REL_BLK)

    def body_shift(t_ref, o_ref):
        r = pl.program_id(0)
        onehot = jnp.where(_rel_shift_matrix(r), 1.0, 0.0).astype(BF16)
        hi, mid, lo = _split3(t_ref[...])
        o_ref[0] = _dot(hi, onehot) + _dot(mid, onehot) + _dot(lo, onehot)

    return pl.pallas_call(
        body_shift, name=name + "_shift", grid=(REL_BLK,),
        in_specs=[pl.BlockSpec(by_delta.shape, lambda r: (0, 0))],
        out_specs=pl.BlockSpec((1, heads * REL_DELTAS, REL_BLK), lambda r: (r, 0, 0)),
        out_shape=jax.ShapeDtypeStruct((REL_BLK, heads * REL_DELTAS, REL_BLK), F32),
        compiler_params=_params("parallel"),
    )(by_delta)


def _bias_table(rel_bias, name):
    heads = rel_bias.shape[0]
    rel_pad = jnp.pad(rel_bias, ((0, 0), (0, REL_PAD - REL_TABLE)))
    tiles = _rel_expand(rel_pad, name)
    tiles = tiles.reshape(REL_BLK, heads, REL_DELTAS, REL_BLK).transpose(1, 2, 0, 3)
    na, nb = Q_BLOCK // REL_BLK, K_WINDOW // REL_BLK
    rows = [jnp.concatenate([tiles[:, a - b + nb - 1] for b in range(nb)], axis=-1) for a in range(na)]
    table = jnp.concatenate(rows, axis=-2)
    qc = np.arange(Q_BLOCK)[:, None] // CHUNK
    kc = np.arange(K_WINDOW)[None, :] // CHUNK
    band = (kc >= qc) & (kc <= qc + PAST_CHUNKS)
    return jnp.where(jnp.asarray(band)[None], table, NEG)


def _rel_reduce(db, name):
    heads = db.shape[0]
    na, nb = Q_BLOCK // REL_BLK, K_WINDOW // REL_BLK

    def body_fold(db_ref, g_ref):
        for delta in range(REL_DELTAS):
            acc = None
            for a in range(na):
                b = a - (delta - (nb - 1))
                if 0 <= b < nb:
                    tile = db_ref[0, a * REL_BLK:(a + 1) * REL_BLK, b * REL_BLK:(b + 1) * REL_BLK]
                    acc = tile if acc is None else acc + tile
            g_ref[0, delta] = acc

    folded = pl.pallas_call(
        body_fold, name=name + "_fold", grid=(heads,),
        in_specs=[pl.BlockSpec((1, Q_BLOCK, K_WINDOW), lambda h: (h, 0, 0))],
        out_specs=pl.BlockSpec((1, REL_DELTAS, REL_BLK, REL_BLK), lambda h: (h, 0, 0, 0)),
        out_shape=jax.ShapeDtypeStruct((heads, REL_DELTAS, REL_BLK, REL_BLK), F32),
        compiler_params=_params("parallel"),
    )(db)
    by_row = folded.transpose(2, 0, 1, 3).reshape(REL_BLK, heads * REL_DELTAS, REL_BLK)

    def body_diag(g_ref, d_ref):
        r = pl.program_id(0)
        onehot = jnp.where(_rel_shift_matrix(r), 1.0, 0.0).astype(BF16)
        hi, mid, lo = _split3(g_ref[0])
        _accumulate(d_ref, _dot_nt(hi, onehot) + _dot_nt(mid, onehot) + _dot_nt(lo, onehot), r)

    diag = pl.pallas_call(
        body_diag, name=name + "_diag", grid=(REL_BLK,),
        in_specs=[pl.BlockSpec((1, heads * REL_DELTAS, REL_BLK), lambda r: (r, 0, 0))],
        out_specs=pl.BlockSpec((heads * REL_DELTAS, 2 * REL_BLK), lambda r: (0, 0)),
        out_shape=jax.ShapeDtypeStruct((heads * REL_DELTAS, 2 * REL_BLK), F32),
        compiler_params=_params("arbitrary"),
    )(by_row)
    diag = diag.reshape(heads, REL_DELTAS * 2 * REL_BLK)

    def body_bin(d_ref, o_ref):
        onehot = jnp.where(_rel_bin_matrix(), 1.0, 0.0).astype(BF16)
        hi, mid, lo = _split3(d_ref[...])
        o_ref[...] = _dot(hi, onehot) + _dot(mid, onehot) + _dot(lo, onehot)

    out = pl.pallas_call(
        body_bin, name=name + "_bin",
        out_shape=jax.ShapeDtypeStruct((heads, REL_PAD), F32),
        compiler_params=pltpu.CompilerParams(vmem_limit_bytes=VMEM_LIMIT_V7X),
    )(diag)
    return out[:, :REL_TABLE]


def _sum_leading(x, name):
    n, r, c = x.shape
    tr = _pick(r, 256, 8)

    def body(x_ref, o_ref):
        acc = x_ref[0].astype(F32)
        for k in range(1, n):
            acc = acc + x_ref[k].astype(F32)
        o_ref[...] = acc

    return pl.pallas_call(
        body, name=name, grid=(r // tr,),
        in_specs=[pl.BlockSpec((n, tr, c), lambda i: (0, i, 0))],
        out_specs=pl.BlockSpec((tr, c), lambda i: (i, 0)),
        out_shape=jax.ShapeDtypeStruct((r, c), F32),
        compiler_params=_params("parallel"),
    )(x)


def _pair_add(g, recv, parity, name):
    _, r, c = g.shape
    tr = _pick(r, 256, 16)

    def body(par_ref, g_ref, r_ref, o_ref):
        o_ref[...] = (g_ref[...].astype(F32) + r_ref[...].astype(F32)).astype(BF16)

    return pl.pallas_call(
        body, name=name,
        grid_spec=pltpu.PrefetchScalarGridSpec(
            num_scalar_prefetch=1, grid=(4, r // tr),
            in_specs=[pl.BlockSpec((1, tr, c), lambda k, i, par: (2 * k + par[0], i, 0)),
                      pl.BlockSpec((1, tr, c), lambda k, i, par: (k, i, 0))],
            out_specs=pl.BlockSpec((1, tr, c), lambda k, i, par: (k, i, 0))),
        out_shape=jax.ShapeDtypeStruct((4, r, c), BF16),
        compiler_params=_params("parallel", "parallel"),
    )(parity, g, recv)


def _adamw(w, g_parts, m, v, name):
    r, c = w.shape
    n = g_parts.shape[0]
    tr = _pick(r, 256, 16 if g_parts.dtype == BF16 else 8)
    c1 = 1.0 - ADAM_B1 ** ADAM_STEP
    c2 = 1.0 - ADAM_B2 ** ADAM_STEP

    def body(w_ref, g_ref, m_ref, v_ref, go_ref, d_ref, nm_ref, nv_ref):
        gv = g_ref[0].astype(F32)
        for k in range(1, n):
            gv = gv + g_ref[k].astype(F32)
        nm = ADAM_B1 * m_ref[...] + (1.0 - ADAM_B1) * gv
        nv = ADAM_B2 * v_ref[...] + (1.0 - ADAM_B2) * (gv * gv)
        go_ref[...] = gv
        d_ref[...] = -ADAM_LR * ((nm / c1) / (jnp.sqrt(nv / c2) + ADAM_EPS) + ADAM_WD * w_ref[...])
        nm_ref[...] = nm
        nv_ref[...] = nv

    spec = pl.BlockSpec((tr, c), lambda i: (i, 0))
    shp = jax.ShapeDtypeStruct((r, c), F32)
    return pl.pallas_call(
        body, name=name, grid=(r // tr,),
        in_specs=[spec, pl.BlockSpec((n, tr, c), lambda i: (0, i, 0)), spec, spec],
        out_specs=[spec] * 4, out_shape=[shp] * 4,
        compiler_params=_params("parallel"),
    )(w, g_parts, m, v)


def _my_place():
    return lax.axis_index("x"), lax.axis_index("y"), lax.axis_index("c")


def _all_gather(xs, name):
    n = len(xs)

    def body(*refs):
        x_refs, out_refs = refs[:n], refs[n:2 * n]
        send_sems, recv_sems, local_sems = refs[2 * n:]
        x, y, c = _my_place()
        me, sibling = (x, y, c), (x, y, 1 - c)
        chips = [(1 - x, y), (x, 1 - y), (1 - x, 1 - y)]

        def slot(a, px, py, pc):
            return out_refs[a].at[4 * px + 2 * py + pc]

        def copy(k, a, block, to, own=False):
            return pltpu.make_async_remote_copy(
                src_ref=x_refs[a] if own else slot(a, *block), dst_ref=slot(a, *block),
                send_sem=send_sems.at[k, a], recv_sem=recv_sems.at[k, a],
                device_id=to, device_id_type=MESH)

        mine = [pltpu.make_async_copy(x_refs[a], slot(a, *me), local_sems.at[a]) for a in range(n)]
        first = []
        for a in range(n):
            mine[a].start()
            first.append(copy(0, a, me, sibling, own=True))
            first += [copy(1 + j, a, me, (*chip, c), own=True) for j, chip in enumerate(chips)]
        for cp in first:
            cp.start()
        passed = []
        for j, chip in enumerate(chips):
            for a in range(n):
                copy(1 + j, a, (*chip, c), me).wait_recv()
                passed.append(copy(4 + j, a, (*chip, c), sibling))
                passed[-1].start()
        for a in range(n):
            copy(0, a, sibling, me).wait_recv()
            for j, chip in enumerate(chips):
                copy(4 + j, a, (*chip, 1 - c), me).wait_recv()
        for cp in first + passed:
            cp.wait_send()
        for cp in mine:
            cp.wait()

    return pl.pallas_call(
        body, name=name,
        out_shape=[jax.ShapeDtypeStruct((N_DEV,) + x.shape, x.dtype) for x in xs],
        in_specs=[ANY] * n, out_specs=[ANY] * n,
        scratch_shapes=[pltpu.SemaphoreType.DMA((7, n)), pltpu.SemaphoreType.DMA((7, n)),
                        pltpu.SemaphoreType.DMA((n,))],
    )(*xs)


def _swap_with_sibling(gs, name):
    n = len(gs)

    def body(*refs):
        g_refs, out_refs = refs[:n], refs[n:2 * n]
        send_sems, recv_sems = refs[2 * n:]
        x, y, c = _my_place()
        copies = [pltpu.make_async_remote_copy(
            src_ref=g_refs[a].at[2 * k + 1 - c], dst_ref=out_refs[a].at[k],
            send_sem=send_sems.at[k, a], recv_sem=recv_sems.at[k, a],
            device_id=(x, y, 1 - c), device_id_type=MESH) for a in range(n) for k in range(4)]
        for cp in copies:
            cp.start()
        for cp in copies:
            cp.wait()

    return pl.pallas_call(
        body, name=name,
        out_shape=[jax.ShapeDtypeStruct((4,) + g.shape[1:], g.dtype) for g in gs],
        in_specs=[ANY] * n, out_specs=[ANY] * n,
        scratch_shapes=[pltpu.SemaphoreType.DMA((4, n)), pltpu.SemaphoreType.DMA((4, n))],
    )(*gs)


def _scatter_to_chips(ps, name):
    n = len(ps)

    def body(*refs):
        p_refs, out_refs = refs[:n], refs[n:2 * n]
        send_sems, recv_sems, local_sems = refs[2 * n:]
        x, y, c = _my_place()
        my_chip = 2 * x + y
        chips = [(1 - x, y), (x, 1 - y), (1 - x, 1 - y)]
        mine = [pltpu.make_async_copy(p_refs[a].at[my_chip], out_refs[a].at[my_chip], local_sems.at[a])
                for a in range(n)]
        for cp in mine:
            cp.start()
        copies = [pltpu.make_async_remote_copy(
            src_ref=p_refs[a].at[2 * cx + cy], dst_ref=out_refs[a].at[my_chip],
            send_sem=send_sems.at[j, a], recv_sem=recv_sems.at[j, a],
            device_id=(cx, cy, c), device_id_type=MESH)
            for a in range(n) for j, (cx, cy) in enumerate(chips)]
        for cp in copies:
            cp.start()
        for cp in copies:
            cp.wait()
        for cp in mine:
            cp.wait()

    return pl.pallas_call(
        body, name=name,
        out_shape=[jax.ShapeDtypeStruct(p.shape, p.dtype) for p in ps],
        in_specs=[ANY] * n, out_specs=[ANY] * n,
        scratch_shapes=[pltpu.SemaphoreType.DMA((3, n)), pltpu.SemaphoreType.DMA((3, n)),
                        pltpu.SemaphoreType.DMA((n,))],
    )(*ps)


BIG = (("a_w_in", 1), ("a_w_o", 0), ("a_w_gu", 1), ("a_w_down", 0), ("w_kv", 1),
       ("b_w_q", 0), ("b_w_o", 0), ("b_w_gu", 1), ("b_w_down", 0))

SMALL = (("a_norm_g", D_MODEL, True), ("a_gn_g", RET_V_COLS, True), ("a_ffn_norm_g", D_MODEL, True),
         ("kv_norm_g", D_MODEL, False), ("b_norm_g", D_MODEL, False), ("b_ffn_norm_g", D_MODEL, False),
         ("k_norm_g", ATT_DH, False), ("b_q_norm_g", ATT_DH, False),
         ("b_rel_bias", ATT_HEADS * REL_TABLE, False))
SMALL_ROWS, SMALL_COLS = 16, 1024


def _pack_small(vals):
    flat = jnp.concatenate([vals[n].reshape(-1) for n, _, _ in SMALL])
    return jnp.pad(flat, (0, SMALL_ROWS * SMALL_COLS - flat.shape[0])).reshape(SMALL_ROWS, SMALL_COLS)


def _unpack_small(packed, local):
    flat, out, pos = packed.reshape(-1), {}, 0
    for n, length, sharded in SMALL:
        ln = length // N_DEV if (local and sharded) else length
        out[n] = flat[pos:pos + ln]
        pos += ln
    return out


def _ffn_fwd(x_in, norm_g, w_gu, w_down, tag):
    h = _rms_fwd(x_in, norm_g, tag + "_norm")
    gu = _mm(h, w_gu, "nn", tag + "_gu", out_block=w_gu.shape[2])
    act = _swiglu_fwd(gu, tag + "_act")
    x_out = _mm(act, w_down, "nn", tag + "_down", res=x_in)
    return x_out, (h, gu, act)


def _ffn_bwd(dx_out, x_in, norm_g, w_gu, w_down, saved, tag):
    h, gu, act = saved
    fb = w_gu.shape[2]
    dact = _mm(dx_out, w_down, "nt", tag + "_dact", out_block=fb)
    g_down = _mm(act, dx_out, "tn", tag + "_gdown", out_dtype=BF16)
    dgu = _swiglu_bwd(gu, dact, tag + "_dgu")
    dh = _mm(dgu, w_gu, "nt", tag + "_dh")
    g_gu = _mm(h, dgu, "tn", tag + "_ggu", out_dtype=BF16, out_block=fb)
    dx_in, g_norm = _rms_bwd(x_in, norm_g, dh, dx_out, tag + "_dnorm")
    return dx_in, g_gu, g_down, g_norm


def _local_step(x, target, w, s):
    t = x.shape[0]
    consts = _ret_consts(t)
    bd = jnp.asarray(np.kron(np.eye(ATT_HEADS, dtype=np.float32),
                             np.ones((ATT_DH, ATT_DH), np.float32))).astype(BF16)
    kg_t = jnp.tile(s["k_norm_g"], (1, ATT_HEADS))
    qg_t = jnp.tile(s["b_q_norm_g"], (1, ATT_HEADS))
    q_scale = ATT_DH ** -0.5
    in_blk, kv_blk = w["a_w_in"].shape[2], w["w_kv"].shape[2]

    h1 = _rms_fwd(x, s["a_norm_g"], "a_norm")
    proj = _mm(h1, w["a_w_in"], "nn", "a_proj")
    y, o_ret, states = _ret_fwd(proj, s["a_gn_g"], consts, "a_ret")
    x1 = _mm(y, w["a_w_o"], "nn", "a_out", res=x)
    x2, ffn_a = _ffn_fwd(x1, s["a_ffn_norm_g"], w["a_w_gu"], w["a_w_down"], "a_ffn")

    u = _rms_fwd(x2, s["kv_norm_g"], "kv_norm")
    kv = _mm(u, w["w_kv"], "nn", "kv_proj")
    kp, vp = _kv_prep(kv, kg_t, bd, "kv_prep")

    h3 = _rms_fwd(x2, s["b_norm_g"], "b_norm")
    q_raw = _mm(h3, w["b_w_q"], "nn", "b_q")
    qn = _q_hnorm(q_raw, qg_t, bd, q_scale, "q_hnorm")
    bias = _bias_table(s["b_rel_bias"].reshape(ATT_HEADS, REL_TABLE), "rel")
    o_att = _att_fwd(qn, kp, vp, bias, "b_att")
    x3 = _mm(o_att, w["b_w_o"], "nn", "b_out", res=x2)
    x4, ffn_b = _ffn_fwd(x3, s["b_ffn_norm_g"], w["b_w_gu"], w["b_w_down"], "b_ffn")

    dy, loss = _loss_head(x4, target, "loss")

    g = {}
    dx3, g["b_w_gu"], g["b_w_down"], g["b_ffn_norm_g"] = _ffn_bwd(
        dy, x3, s["b_ffn_norm_g"], w["b_w_gu"], w["b_w_down"], ffn_b, "b_ffn")
    do_att = _mm(dx3, w["b_w_o"], "nt", "b_dout", out_dtype=BF16)
    g["b_w_o"] = _mm(o_att, dx3, "tn", "b_gout", out_dtype=BF16)
    dq, dkp, dvp, db = _att_bwd(qn, kp, vp, bias, do_att, "b_datt")
    g["b_rel_bias"] = _rel_reduce(db, "drel").reshape(1, -1)
    dq_raw, gq = _q_dhnorm(q_raw, qg_t, bd, dq, q_scale, "q_dhnorm")
    g["b_q_norm_g"] = gq.reshape(ATT_HEADS, ATT_DH).sum(axis=0, keepdims=True)
    dh3 = _mm(dq_raw, w["b_w_q"], "nt", "b_dq")
    g["b_w_q"] = _mm(h3, dq_raw, "tn", "b_gq", out_dtype=BF16)
    dx2, g["b_norm_g"] = _rms_bwd(x2, s["b_norm_g"], dh3, dx3, "b_dnorm")

    dkv, gk = _kv_dprep(kv, kg_t, bd, dkp, dvp, "kv_dprep")
    g["k_norm_g"] = gk.reshape(ATT_HEADS, ATT_DH).sum(axis=0, keepdims=True)
    du = _mm(dkv, w["w_kv"], "nt", "kv_du")
    g["w_kv"] = _mm(u, dkv, "tn", "kv_g", out_dtype=BF16, out_block=kv_blk)
    dx2, g["kv_norm_g"] = _rms_bwd(x2, s["kv_norm_g"], du, dx2, "kv_dnorm")

    dx1, g["a_w_gu"], g["a_w_down"], g["a_ffn_norm_g"] = _ffn_bwd(
        dx2, x1, s["a_ffn_norm_g"], w["a_w_gu"], w["a_w_down"], ffn_a, "a_ffn")
    dy_ret = _mm(dx1, w["a_w_o"], "nt", "a_dout")
    g["a_w_o"] = _mm(y, dx1, "tn", "a_gout", out_dtype=BF16)
    dproj, g["a_gn_g"] = _ret_bwd(proj, s["a_gn_g"], o_ret, states, dy_ret, consts, "a_dret")
    dh1 = _mm(dproj, w["a_w_in"], "nt", "a_dproj")
    g["a_w_in"] = _mm(h1, dproj, "tn", "a_gin", out_dtype=BF16, out_block=in_blk)
    grad_x, g["a_norm_g"] = _rms_bwd(x, s["a_norm_g"], dh1, dx1, "a_dnorm")
    return loss, grad_x, g


ARG_NAMES = ("x", "a_norm_g", "a_w_in", "a_gn_g", "a_w_o", "a_ffn_norm_g", "a_w_gu", "a_w_down",
             "kv_norm_g", "w_kv", "k_norm_g", "b_norm_g", "b_w_q", "b_q_norm_g", "b_rel_bias", "b_w_o",
             "b_ffn_norm_g", "b_w_gu", "b_w_down")
WEIGHT_NAMES = ARG_NAMES[1:]


def _big_shard(a):
    return a[0] if a.ndim == 3 else a


def kernel(x, a_norm_g, a_w_in, a_gn_g, a_w_o, a_ffn_norm_g, a_w_gu, a_w_down, kv_norm_g, w_kv, k_norm_g, b_norm_g, b_w_q, b_q_norm_g, b_rel_bias, b_w_o, b_ffn_norm_g, b_w_gu, b_w_down, loss_target, m_a_norm_g, m_a_w_in, m_a_gn_g, m_a_w_o, m_a_ffn_norm_g, m_a_w_gu, m_a_w_down, m_kv_norm_g, m_w_kv, m_k_norm_g, m_b_norm_g, m_b_w_q, m_b_q_norm_g, m_b_rel_bias, m_b_w_o, m_b_ffn_norm_g, m_b_w_gu, m_b_w_down, v_a_norm_g, v_a_w_in, v_a_gn_g, v_a_w_o, v_a_ffn_norm_g, v_a_w_gu, v_a_w_down, v_kv_norm_g, v_w_kv, v_k_norm_g, v_b_norm_g, v_b_w_q, v_b_q_norm_g, v_b_rel_bias, v_b_w_o, v_b_ffn_norm_g, v_b_w_gu, v_b_w_down):
    args = (x, a_norm_g, a_w_in, a_gn_g, a_w_o, a_ffn_norm_g, a_w_gu, a_w_down, kv_norm_g, w_kv, k_norm_g,
            b_norm_g, b_w_q, b_q_norm_g, b_rel_bias, b_w_o, b_ffn_norm_g, b_w_gu, b_w_down)
    p = dict(zip(ARG_NAMES, args))
    m_all = dict(zip(WEIGHT_NAMES, (m_a_norm_g, m_a_w_in, m_a_gn_g, m_a_w_o, m_a_ffn_norm_g, m_a_w_gu,
                                    m_a_w_down, m_kv_norm_g, m_w_kv, m_k_norm_g, m_b_norm_g, m_b_w_q,
                                    m_b_q_norm_g, m_b_rel_bias, m_b_w_o, m_b_ffn_norm_g, m_b_w_gu, m_b_w_down)))
    v_all = dict(zip(WEIGHT_NAMES, (v_a_norm_g, v_a_w_in, v_a_gn_g, v_a_w_o, v_a_ffn_norm_g, v_a_w_gu,
                                    v_a_w_down, v_kv_norm_g, v_w_kv, v_k_norm_g, v_b_norm_g, v_b_w_q,
                                    v_b_q_norm_g, v_b_rel_bias, v_b_w_o, v_b_ffn_norm_g, v_b_w_gu, v_b_w_down)))
    xi, yi, ci = _my_place()
    me = 4 * xi + 2 * yi + ci
    big_names = [n for n, _ in BIG]
    axis_of = dict(BIG)

    big_local = {n: _big_shard(p[n]) for n in big_names}
    small_local = _pack_small({n: p[n] for n, _, _ in SMALL})
    gathered = _all_gather([big_local[n].astype(BF16) for n in big_names] + [small_local], "gather_w")
    w_full = {}
    for n, arr in zip(big_names, gathered):
        w_full[n] = arr.reshape(-1, arr.shape[2]) if axis_of[n] == 0 else arr
    flat_g = gathered[-1].reshape(N_DEV, -1)
    s_full, pos = {}, 0
    for n, length, sharded in SMALL:
        ln = length // N_DEV if sharded else length
        s_full[n] = flat_g[:, pos:pos + ln].reshape(1, -1) if sharded else p[n].reshape(1, -1)
        pos += ln

    loss, grad_x, g = _local_step(x[0], loss_target[0], w_full, s_full)
    loss = lax.psum(loss[0, 0], ("x", "y", "c"))

    g_blocks = [g[n].reshape(N_DEV, -1, g[n].shape[-1]) if axis_of[n] == 0 else g[n] for n in big_names]
    from_sibling = _swap_with_sibling(g_blocks, "rs_sibling")
    parity = jnp.reshape(ci, (1,)).astype(jnp.int32)
    chip_sums = [_pair_add(gb, fs, parity, "rs_pair_" + n) for n, gb, fs in zip(big_names, g_blocks, from_sibling)]
    from_chips = dict(zip(big_names, _scatter_to_chips(chip_sums, "rs_chips")))

    g_small_all = _all_gather([_pack_small({n: g[n] for n, _, _ in SMALL})], "gather_gsmall")[0]
    g_small = _unpack_small(_sum_leading(g_small_all, "gsmall_sum"), local=False)
    for n, length, sharded in SMALL:
        if sharded:
            g_small[n] = lax.dynamic_slice(g_small[n], (me * (length // N_DEV),), (length // N_DEV,))

    grads, deltas, new_m, new_v = {}, {}, {}, {}
    for n in big_names:
        outs = _adamw(big_local[n], from_chips[n], _big_shard(m_all[n]), _big_shard(v_all[n]), "adamw_" + n)
        grads[n], deltas[n], new_m[n], new_v[n] = (a.reshape(p[n].shape) for a in outs)
    pk = lambda src: _pack_small({n: src[n] for n, _, _ in SMALL})
    outs = _adamw(small_local, pk(g_small)[None], pk(m_all), pk(v_all), "adamw_small")
    g_s, d_s, nm_s, nv_s = (_unpack_small(a, local=True) for a in outs)
    for n, _, _ in SMALL:
        grads[n], deltas[n], new_m[n], new_v[n] = (a[n].reshape(p[n].shape) for a in (g_s, d_s, nm_s, nv_s))

    return (loss, grad_x[None], *[grads[n] for n in WEIGHT_NAMES], *[deltas[n] for n in WEIGHT_NAMES],
            *[new_m[n] for n in WEIGHT_NAMES], *[new_v[n] for n in WEIGHT_NAMES])
```

```python
import numpy as np
import jax
import jax.numpy as jnp
from jax import lax
from jax.experimental import pallas as pl
from jax.experimental.pallas import tpu as pltpu

F32 = jnp.float32
BF16 = jnp.bfloat16

N_DEV = 8
D_MODEL = 1024
CHUNK = 64
EPS = 1e-6
RET_HEADS, RET_DK, RET_DV = 4, 256, 512
RET_Q_COLS = RET_HEADS * RET_DK
RET_V_COLS = RET_HEADS * RET_DV
ATT_HEADS, ATT_DH = 16, 64
PAST_CHUNKS = 8
REL_CLIP = 256
REL_TABLE = 2 * REL_CLIP + 1
FFN_HIDDEN = 2816
ROPE_BASE = 10000.0
LANES = 128
Q_BLOCK = 256
K_PAD = PAST_CHUNKS * CHUNK
K_WINDOW = Q_BLOCK + K_PAD
REL_BLK = 128
REL_DELTAS = Q_BLOCK // REL_BLK + K_WINDOW // REL_BLK - 1
REL_PAD = 640
NEG = -1e30
VMEM_LIMIT_V7X = 56 * 1024 * 1024
ADAM_LR, ADAM_B1, ADAM_B2, ADAM_EPS, ADAM_WD, ADAM_STEP = 1e-3, 0.9, 0.999, 1e-8, 0.01, 10
MESH = pl.DeviceIdType.MESH
ANY = pl.BlockSpec(memory_space=pl.ANY)


def _params(*semantics):
    return pltpu.CompilerParams(dimension_semantics=semantics, vmem_limit_bytes=VMEM_LIMIT_V7X)


def _pick(dim, cap, align):
    best = None
    for t in range(align, min(dim, cap) + 1, align):
        if dim % t == 0:
            best = t
    assert best is not None, (dim, cap, align)
    return best


def _dot(a, b):
    return lax.dot_general(a, b, (((1,), (0,)), ((), ())), preferred_element_type=F32)


def _dot_nt(a, b):
    return lax.dot_general(a, b, (((1,), (1,)), ((), ())), preferred_element_type=F32)


def _dot_tn(a, b):
    return lax.dot_general(a, b, (((0,), (0,)), ((), ())), preferred_element_type=F32)


def _split2(x):
    hi = x.astype(BF16)
    lo = (x - hi.astype(F32)).astype(BF16)
    return hi, lo


def _split3(x):
    hi = x.astype(BF16)
    r = x - hi.astype(F32)
    mid = r.astype(BF16)
    lo = (r - mid.astype(F32)).astype(BF16)
    return hi, mid, lo


def _sigmoid(x):
    return 1.0 / (1.0 + jnp.exp(-x))


def _accumulate(ref, part, step):
    @pl.when(step == 0)
    def _():
        ref[...] = part

    @pl.when(step > 0)
    def _():
        ref[...] += part


MM_CAP_MN = 1024
MM_CAP_N = 1536
MM_CAP_K = 1536


def _mm(a, b, mode, name, out_dtype=F32, res=None, out_block=None, epilogue=None, extra=None):
    a3, b3 = a.ndim == 3, b.ndim == 3
    um = un = uk = None
    if mode in ("nn", "nt"):
        if a3:
            m, uk = a.shape[1:]
            k = a.shape[0] * uk
        else:
            m, k = a.shape
    else:
        if a3:
            k, um = a.shape[1:]
            m = a.shape[0] * um
        else:
            k, m = a.shape
    if mode in ("nn", "tn"):
        if b3:
            kb, un = b.shape[1:]
            n = b.shape[0] * un
        else:
            kb, n = b.shape
        assert kb == k, (a.shape, b.shape, mode)
    else:
        if b3:
            n, ukb = b.shape[1:]
            assert b.shape[0] * ukb == k and uk in (None, ukb), (a.shape, b.shape, mode)
            uk = ukb
        else:
            n, kb = b.shape
            assert kb == k, (a.shape, b.shape, mode)
    if out_block is not None:
        assert un in (None, out_block)
        un = out_block

    def tile(dim, unit, cap, align):
        if unit is None:
            return _pick(dim, cap, align), 1
        c = max(1, cap // unit)
        while (dim // unit) % c:
            c -= 1
        return unit, c

    um, cm = tile(m, um, MM_CAP_MN if mode != "tn" else 1408, 128 if mode == "tn" else 16)
    un, cn = tile(n, un, MM_CAP_N, 128)
    uk, ck = tile(k, uk, MM_CAP_K if mode != "tn" else 1024, 128)
    if epilogue == "swiglu":
        assert mode == "nn" and b3 and res is None and out_block is None
        cn = 2
    if epilogue == "swiglu_bwd":
        assert mode == "nt" and out_block is not None and extra is not None and res is None
        cn = 1
    tm, tn, tk = cm * um, cn * un, ck * uk
    nk = k // tk
    dot = {"nn": _dot, "nt": _dot_nt, "tn": _dot_tn}[mode]
    half = n // un // 2
    blocked_out = out_block is not None or epilogue is not None

    def sl(idx, unit, count):
        return slice(None) if count == 1 else slice(idx * unit, (idx + 1) * unit)

    def body(*refs):
        a_ref, b_ref = refs[0], refs[1]
        pos = 2
        r_ref = e_ref = None
        if res is not None:
            r_ref, pos = refs[pos], pos + 1
        if extra is not None:
            e_ref, pos = refs[pos], pos + 1
        outs, acc_ref = refs[pos:-1], refs[-1]
        kk = pl.program_id(2)

        def a_blk(mi, ki):
            if mode in ("nn", "nt"):
                return a_ref[ki] if a3 else a_ref[:, sl(ki, uk, ck)]
            return a_ref[mi] if a3 else a_ref[:, sl(mi, um, cm)]

        def b_blk(ki, ni):
            if epilogue == "swiglu":
                return b_ref[ni, 0]
            if mode in ("nn", "tn"):
                return b_ref[ni] if b3 else b_ref[sl(ki, uk, ck), sl(ni, un, cn)]
            return b_ref[ki][sl(ni, un, cn), :] if b3 else b_ref[sl(ni, un, cn), sl(ki, uk, ck)]

        parts = {}
        for mi in range(cm):
            for ni in range(cn):
                part = None
                for ki in range(ck):
                    d = dot(a_blk(mi, ki).astype(BF16), b_blk(ki, ni).astype(BF16))
                    part = d if part is None else part + d
                parts[mi, ni] = part

        def finish(total):
            if epilogue == "swiglu":
                gate, up = total[0, 0], total[0, 1]
                outs[0][0, 0] = gate.astype(BF16)
                outs[0][1, 0] = up.astype(BF16)
                outs[1][0] = (gate * _sigmoid(gate) * up).astype(BF16)
                return
            if epilogue == "swiglu_bwd":
                dact = total[0, 0]
                gate, up = e_ref[0, 0].astype(F32), e_ref[1, 0].astype(F32)
                sg = _sigmoid(gate)
                outs[0][0, 0] = (dact * up * (sg * (1.0 + gate * (1.0 - sg)))).astype(BF16)
                outs[0][1, 0] = (dact * (gate * sg)).astype(BF16)
                return
            for (mi, ni), val in total.items():
                rows, cols = sl(mi, um, cm), sl(ni, un, cn)
                if res is not None:
                    val = r_ref[rows, cols] + val
                if blocked_out:
                    outs[0][ni, rows] = val.astype(out_dtype)
                else:
                    outs[0][rows, cols] = val.astype(out_dtype)

        if nk == 1:
            finish(parts)
        else:
            @pl.when(kk == 0)
            def _():
                for (mi, ni), val in parts.items():
                    acc_ref[mi * cn + ni] = val

            @pl.when(jnp.logical_and(kk > 0, kk < nk - 1))
            def _():
                for (mi, ni), val in parts.items():
                    acc_ref[mi * cn + ni] += val

            @pl.when(kk == nk - 1)
            def _():
                finish({key: acc_ref[key[0] * cn + key[1]] + val for key, val in parts.items()})

    if mode in ("nn", "nt"):
        a_spec = (pl.BlockSpec((ck, tm, uk), lambda i, j, kk: (kk, i, 0)) if a3
                  else pl.BlockSpec((tm, tk), lambda i, j, kk: (i, kk)))
    else:
        a_spec = (pl.BlockSpec((cm, tk, um), lambda i, j, kk: (i, kk, 0)) if a3
                  else pl.BlockSpec((tk, tm), lambda i, j, kk: (kk, i)))
    pair_spec = pl.BlockSpec((2, 1, tm, un), lambda i, j, kk: (0, j, i, 0))
    if epilogue == "swiglu":
        b = b.reshape(2, half, k, un)
        b_spec = pl.BlockSpec((2, 1, tk, un), lambda i, j, kk: (0, j, kk, 0))
    elif mode in ("nn", "tn"):
        b_spec = (pl.BlockSpec((cn, tk, un), lambda i, j, kk: (j, kk, 0)) if b3
                  else pl.BlockSpec((tk, tn), lambda i, j, kk: (kk, j)))
    else:
        b_spec = (pl.BlockSpec((ck, tn, uk), lambda i, j, kk: (kk, j, 0)) if b3
                  else pl.BlockSpec((tn, tk), lambda i, j, kk: (j, kk)))
    if epilogue == "swiglu":
        out_specs = [pair_spec, pl.BlockSpec((1, tm, un), lambda i, j, kk: (j, i, 0))]
        out_shape = [jax.ShapeDtypeStruct((2, half, m, un), BF16), jax.ShapeDtypeStruct((half, m, un), BF16)]
    elif epilogue == "swiglu_bwd":
        out_specs = [pair_spec]
        out_shape = [jax.ShapeDtypeStruct(extra.shape, BF16)]
    elif blocked_out:
        out_specs = [pl.BlockSpec((cn, tm, un), lambda i, j, kk: (j, i, 0))]
        out_shape = [jax.ShapeDtypeStruct((n // un, m, un), out_dtype)]
    else:
        out_specs = [pl.BlockSpec((tm, tn), lambda i, j, kk: (i, j))]
        out_shape = [jax.ShapeDtypeStruct((m, n), out_dtype)]
    in_specs, args = [a_spec, b_spec], [a, b]
    if res is not None:
        in_specs.append(pl.BlockSpec((tm, tn), lambda i, j, kk: (i, j)))
        args.append(res)
    if extra is not None:
        in_specs.append(pair_spec)
        args.append(extra)
    out = pl.pallas_call(
        body, name=name, grid=(m // tm, n // tn, nk),
        in_specs=in_specs, out_specs=out_specs, out_shape=out_shape,
        scratch_shapes=[pltpu.VMEM((cm * cn, um, un), F32)],
        compiler_params=_params("parallel", "parallel", "arbitrary"),
    )(*args)
    return out if epilogue == "swiglu" else out[0]


def _rms_fwd(x, g, name):
    t, d = x.shape
    tm = _pick(t, 512, 16)

    def body(x_ref, g_ref, o_ref):
        xv = x_ref[...]
        rstd = lax.rsqrt(jnp.mean(xv * xv, axis=-1, keepdims=True) + EPS)
        o_ref[...] = (xv * rstd * g_ref[...]).astype(BF16)

    return pl.pallas_call(
        body, name=name, grid=(t // tm,),
        in_specs=[pl.BlockSpec((tm, d), lambda i: (i, 0)), pl.BlockSpec((1, d), lambda i: (0, 0))],
        out_specs=pl.BlockSpec((tm, d), lambda i: (i, 0)),
        out_shape=jax.ShapeDtypeStruct((t, d), BF16),
        compiler_params=_params("parallel"),
    )(x, g)


def _rms_bwd(x, g, dh, dres, name):
    t, d = x.shape
    tm = _pick(t, 512, 16)

    def body(x_ref, g_ref, dh_ref, dres_ref, dx_ref, dg_ref):
        xv = x_ref[...]
        rstd = lax.rsqrt(jnp.mean(xv * xv, axis=-1, keepdims=True) + EPS)
        xh = xv * rstd
        dhv = dh_ref[...]
        dyg = dhv * g_ref[...]
        c = jnp.mean(dyg * xh, axis=-1, keepdims=True)
        dx_ref[...] = dres_ref[...] + rstd * (dyg - xh * c)
        _accumulate(dg_ref, jnp.sum(dhv * xh, axis=0, keepdims=True), pl.program_id(0))

    row = pl.BlockSpec((tm, d), lambda i: (i, 0))
    vec = pl.BlockSpec((1, d), lambda i: (0, 0))
    return pl.pallas_call(
        body, name=name, grid=(t // tm,),
        in_specs=[row, vec, row, row], out_specs=[row, vec],
        out_shape=[jax.ShapeDtypeStruct((t, d), F32), jax.ShapeDtypeStruct((1, d), F32)],
        compiler_params=_params("arbitrary"),
    )(x, g, dh, dres)


def _seg_mean(v, bd):
    hi, lo = _split2(v)
    return (_dot(hi, bd) + _dot(lo, bd)) * (1.0 / ATT_DH)


def _hn_bwd_math(xv, gv, bdv, dyv, scale):
    rstd = lax.rsqrt(_seg_mean(xv * xv, bdv) + EPS)
    xh = xv * rstd
    dyn = dyv * scale
    dyg = dyn * gv
    dx = rstd * (dyg - xh * _seg_mean(dyg * xh, bdv))
    return dx, jnp.sum(dyn * xh, axis=0, keepdims=True)


def _q_hnorm(x, g_tiled, bd, scale, name):
    t, d = x.shape
    tm = _pick(t, 512, 16)

    def body(x_ref, g_ref, bd_ref, o_ref):
        xv = x_ref[...]
        rstd = lax.rsqrt(_seg_mean(xv * xv, bd_ref[...]) + EPS)
        o_ref[...] = (xv * rstd * g_ref[...] * scale).astype(BF16)

    return pl.pallas_call(
        body, name=name, grid=(t // tm,),
        in_specs=[pl.BlockSpec((tm, d), lambda i: (i, 0)), pl.BlockSpec((1, d), lambda i: (0, 0)),
                  pl.BlockSpec((d, d), lambda i: (0, 0))],
        out_specs=pl.BlockSpec((tm, d), lambda i: (i, 0)),
        out_shape=jax.ShapeDtypeStruct((t, d), BF16),
        compiler_params=_params("parallel"),
    )(x, g_tiled, bd)


def _q_dhnorm(x, g_tiled, bd, dy, scale, name):
    t, d = x.shape
    tm = _pick(t, 512, 16)

    def body(x_ref, g_ref, bd_ref, dy_ref, dx_ref, dg_ref):
        dx, part = _hn_bwd_math(x_ref[...], g_ref[...], bd_ref[...], dy_ref[...], scale)
        dx_ref[...] = dx.astype(BF16)
        _accumulate(dg_ref, part, pl.program_id(0))

    row = pl.BlockSpec((tm, d), lambda i: (i, 0))
    vec = pl.BlockSpec((1, d), lambda i: (0, 0))
    return pl.pallas_call(
        body, name=name, grid=(t // tm,),
        in_specs=[row, vec, pl.BlockSpec((d, d), lambda i: (0, 0)), row],
        out_specs=[row, vec],
        out_shape=[jax.ShapeDtypeStruct((t, d), BF16), jax.ShapeDtypeStruct((1, d), F32)],
        compiler_params=_params("arbitrary"),
    )(x, g_tiled, bd, dy)


def _kv_prep(kv, g_tiled, bd, name):
    t = kv.shape[0]
    d = D_MODEL
    tm = K_PAD
    assert t % tm == 0

    def body(k_ref, v_ref, g_ref, bd_ref, kp_ref, vp_ref):
        i = pl.program_id(0)

        @pl.when(i == 0)
        def _():
            kp_ref[...] = jnp.zeros_like(kp_ref)
            vp_ref[...] = jnp.zeros_like(vp_ref)

        @pl.when(i > 0)
        def _():
            xv = k_ref[...]
            rstd = lax.rsqrt(_seg_mean(xv * xv, bd_ref[...]) + EPS)
            kp_ref[...] = (xv * rstd * g_ref[...]).astype(BF16)
            vp_ref[...] = v_ref[...].astype(BF16)

    shp = jax.ShapeDtypeStruct((t + K_PAD, d), BF16)
    out = pl.BlockSpec((tm, d), lambda i: (i, 0))
    return pl.pallas_call(
        body, name=name, grid=(t // tm + 1,),
        in_specs=[pl.BlockSpec((tm, d), lambda i: (jnp.maximum(i - 1, 0), 0)),
                  pl.BlockSpec((tm, d), lambda i: (jnp.maximum(i - 1, 0), 1)),
                  pl.BlockSpec((1, d), lambda i: (0, 0)), pl.BlockSpec((d, d), lambda i: (0, 0))],
        out_specs=[out, out], out_shape=[shp, shp],
        compiler_params=_params("arbitrary"),
    )(kv, kv, g_tiled, bd)


def _kv_dprep(kv, g_tiled, bd, dkp, dvp, name):
    t = kv.shape[0]
    d = D_MODEL
    tm = K_PAD

    def body(k_ref, g_ref, bd_ref, dk_ref, dv_ref, o_ref, dg_ref):
        dx, part = _hn_bwd_math(k_ref[...], g_ref[...], bd_ref[...], dk_ref[...], 1.0)
        o_ref[:, :d] = dx.astype(BF16)
        o_ref[:, d:] = dv_ref[...].astype(BF16)
        _accumulate(dg_ref, part, pl.program_id(0))

    vec = pl.BlockSpec((1, d), lambda i: (0, 0))
    padded = pl.BlockSpec((tm, d), lambda i: (i + 1, 0))
    return pl.pallas_call(
        body, name=name, grid=(t // tm,),
        in_specs=[pl.BlockSpec((tm, d), lambda i: (i, 0)), vec, pl.BlockSpec((d, d), lambda i: (0, 0)),
                  padded, padded],
        out_specs=[pl.BlockSpec((tm, 2 * d), lambda i: (i, 0)), vec],
        out_shape=[jax.ShapeDtypeStruct((t, 2 * d), BF16), jax.ShapeDtypeStruct((1, d), F32)],
        compiler_params=_params("arbitrary"),
    )(kv, g_tiled, bd, dkp, dvp)


def _loss_head(y, target, name):
    t, d = y.shape
    tm = _pick(t, 512, 16)

    def body(y_ref, t_ref, dy_ref, l_ref):
        diff = y_ref[...] - t_ref[...]
        dy_ref[...] = diff * (1.0 / d)
        part = jnp.sum(jnp.sum(diff * diff, axis=-1, keepdims=True), axis=0, keepdims=True) * (0.5 / d)
        _accumulate(l_ref, part, pl.program_id(0))

    row = pl.BlockSpec((tm, d), lambda i: (i, 0))
    return pl.pallas_call(
        body, name=name, grid=(t // tm,),
        in_specs=[row, row], out_specs=[row, pl.BlockSpec((1, 1), lambda i: (0, 0))],
        out_shape=[jax.ShapeDtypeStruct((t, d), F32), jax.ShapeDtypeStruct((1, 1), F32)],
        compiler_params=_params("arbitrary"),
    )(y, target)


def _ret_consts(t):
    h = np.arange(RET_HEADS, dtype=np.float32)
    lg = np.log(np.float32(1.0) - np.float32(2.0) ** (np.float32(-5.0) - h)).astype(np.float32)
    tt = np.arange(CHUNK, dtype=np.float32)
    intra = np.exp(lg[:, None, None] * np.abs(tt[:, None] - tt[None, :])).astype(np.float32)
    q_dec = np.exp(lg[:, None] * (tt + 1.0)).astype(np.float32)
    k_dec = np.exp(lg[:, None] * (CHUNK - 1.0 - tt)).astype(np.float32)
    s_dec = [float(v) for v in np.exp(lg * np.float32(CHUNK)).astype(np.float32)]
    qd = np.broadcast_to(q_dec[:, :, None], (RET_HEADS, CHUNK, RET_DK)).copy()
    kd = np.broadcast_to(k_dec[:, :, None], (RET_HEADS, CHUNK, RET_DK)).copy()
    half = RET_DK // 2
    inv_freq = ROPE_BASE ** (-jnp.arange(half, dtype=F32) / half)
    ang = jnp.arange(t).astype(F32)[:, None] * inv_freq[None, :]
    return jnp.asarray(intra), jnp.asarray(qd), jnp.asarray(kd), s_dec, jnp.cos(ang), jnp.sin(ang)


def _rope(x, cos, sin):
    half = RET_DK // 2
    x1, x2 = x[:, :half], x[:, half:]
    return jnp.concatenate([x1 * cos - x2 * sin, x1 * sin + x2 * cos], axis=-1)


def _unrope(d, cos, sin):
    half = RET_DK // 2
    d1, d2 = d[:, :half], d[:, half:]
    return jnp.concatenate([d1 * cos + d2 * sin, d2 * cos - d1 * sin], axis=-1)


def _ret_slices(h):
    q = slice(h * RET_DK, (h + 1) * RET_DK)
    k = slice(RET_Q_COLS + h * RET_DK, RET_Q_COLS + (h + 1) * RET_DK)
    v = slice(2 * RET_Q_COLS + h * RET_DV, 2 * RET_Q_COLS + (h + 1) * RET_DV)
    g = slice(2 * RET_Q_COLS + RET_V_COLS + h * RET_DV, 2 * RET_Q_COLS + RET_V_COLS + (h + 1) * RET_DV)
    o = slice(h * RET_DV, (h + 1) * RET_DV)
    return q, k, v, g, o


def _ret_fwd(proj, gn, consts, name):
    t, cols = proj.shape
    n = t // CHUNK
    intra, qd, kd, s_dec, cos, sin = consts
    k_scale = RET_DK ** -0.5

    def body(p_ref, cos_ref, sin_ref, intra_ref, qd_ref, kd_ref, gn_ref, y_ref, o_ref, st_ref, state):
        i = pl.program_id(0)

        @pl.when(i == 0)
        def _():
            state[...] = jnp.zeros_like(state)

        cosv, sinv = cos_ref[...], sin_ref[...]
        for h in range(RET_HEADS):
            qs, ks, vs, gs, os_ = _ret_slices(h)
            qr = _rope(p_ref[:, qs], cosv, sinv)
            kr = _rope(p_ref[:, ks], cosv, sinv) * k_scale
            vb = p_ref[:, vs].astype(BF16)
            gv = p_ref[:, gs]
            scores = _dot_nt(qr.astype(BF16), kr.astype(BF16)) * intra_ref[h]
            s_old = state[h]
            s_old_b = s_old.astype(BF16)
            st_ref[0, h] = s_old_b
            o = _dot(scores.astype(BF16), vb) + _dot((qr * qd_ref[h]).astype(BF16), s_old_b)
            state[h] = s_old * s_dec[h] + _dot_tn((kr * kd_ref[h]).astype(BF16), vb)
            rstd = lax.rsqrt(jnp.mean(o * o, axis=-1, keepdims=True) + EPS)
            on = o * rstd * gn_ref[:, os_]
            o_ref[:, os_] = o
            y_ref[:, os_] = (gv * _sigmoid(gv) * on).astype(BF16)

    full3 = lambda a: pl.BlockSpec(a.shape, lambda i: (0, 0, 0))
    return pl.pallas_call(
        body, name=name, grid=(n,),
        in_specs=[pl.BlockSpec((CHUNK, cols), lambda i: (i, 0)),
                  pl.BlockSpec((CHUNK, RET_DK // 2), lambda i: (i, 0)),
                  pl.BlockSpec((CHUNK, RET_DK // 2), lambda i: (i, 0)),
                  full3(intra), full3(qd), full3(kd),
                  pl.BlockSpec((1, RET_V_COLS), lambda i: (0, 0))],
        out_specs=[pl.BlockSpec((CHUNK, RET_V_COLS), lambda i: (i, 0)),
                   pl.BlockSpec((CHUNK, RET_V_COLS), lambda i: (i, 0)),
                   pl.BlockSpec((1, RET_HEADS, RET_DK, RET_DV), lambda i: (i, 0, 0, 0))],
        out_shape=[jax.ShapeDtypeStruct((t, RET_V_COLS), BF16),
                   jax.ShapeDtypeStruct((t, RET_V_COLS), F32),
                   jax.ShapeDtypeStruct((n, RET_HEADS, RET_DK, RET_DV), BF16)],
        scratch_shapes=[pltpu.VMEM((RET_HEADS, RET_DK, RET_DV), F32)],
        compiler_params=_params("arbitrary"),
    )(proj, cos, sin, intra, qd, kd, gn)


def _ret_bwd(proj, gn, o_saved, states, dy, consts, name):
    t, cols = proj.shape
    n = t // CHUNK
    intra, qd, kd, s_dec, cos, sin = consts
    k_scale = RET_DK ** -0.5

    def body(p_ref, cos_ref, sin_ref, intra_ref, qd_ref, kd_ref, gn_ref, o_ref, st_ref, dy_ref,
             dp_ref, dgn_ref, dstate):
        i = pl.program_id(0)

        @pl.when(i == 0)
        def _():
            dstate[...] = jnp.zeros_like(dstate)

        cosv, sinv = cos_ref[...], sin_ref[...]
        dgn_parts = []
        for h in range(RET_HEADS):
            qs, ks, vs, gs, os_ = _ret_slices(h)
            qr = _rope(p_ref[:, qs], cosv, sinv)
            kr = _rope(p_ref[:, ks], cosv, sinv) * k_scale
            qb, kb = qr.astype(BF16), kr.astype(BF16)
            vb = p_ref[:, vs].astype(BF16)
            gv = p_ref[:, gs]
            ov = o_ref[:, os_]
            dyv = dy_ref[:, os_]
            gnv = gn_ref[:, os_]
            sg = _sigmoid(gv)
            rstd = lax.rsqrt(jnp.mean(ov * ov, axis=-1, keepdims=True) + EPS)
            oh = ov * rstd
            d_on = dyv * (gv * sg)
            dg = dyv * (oh * gnv) * (sg * (1.0 + gv * (1.0 - sg)))
            dgn_parts.append(jnp.sum(d_on * oh, axis=0, keepdims=True))
            d_oh = d_on * gnv
            do = rstd * (d_oh - oh * jnp.mean(d_oh * oh, axis=-1, keepdims=True))
            dob = do.astype(BF16)
            mask = intra_ref[h]
            a_b = (_dot_nt(qb, kb) * mask).astype(BF16)
            da_b = (_dot_nt(dob, vb) * mask).astype(BF16)
            ds_new = dstate[h]
            ds_new_b = ds_new.astype(BF16)
            s_old_b = st_ref[0, h]
            qdv, kdv = qd_ref[h], kd_ref[h]
            dv = _dot_tn(a_b, dob) + _dot((kr * kdv).astype(BF16), ds_new_b)
            dqr = _dot(da_b, kb) + _dot_nt(dob, s_old_b) * qdv
            dkr = _dot_tn(da_b, qb) + _dot_nt(vb, ds_new_b) * kdv
            dstate[h] = ds_new * s_dec[h] + _dot_tn((qr * qdv).astype(BF16), dob)
            dp_ref[:, qs] = _unrope(dqr, cosv, sinv).astype(BF16)
            dp_ref[:, ks] = _unrope(dkr * k_scale, cosv, sinv).astype(BF16)
            dp_ref[:, vs] = dv.astype(BF16)
            dp_ref[:, gs] = dg.astype(BF16)
        _accumulate(dgn_ref, jnp.concatenate(dgn_parts, axis=-1), i)

    rev = lambda i: (n - 1 - i, 0)
    full3 = lambda a: pl.BlockSpec(a.shape, lambda i: (0, 0, 0))
    return pl.pallas_call(
        body, name=name, grid=(n,),
        in_specs=[pl.BlockSpec((CHUNK, cols), rev),
                  pl.BlockSpec((CHUNK, RET_DK // 2), rev),
                  pl.BlockSpec((CHUNK, RET_DK // 2), rev),
                  full3(intra), full3(qd), full3(kd),
                  pl.BlockSpec((1, RET_V_COLS), lambda i: (0, 0)),
                  pl.BlockSpec((CHUNK, RET_V_COLS), rev),
                  pl.BlockSpec((1, RET_HEADS, RET_DK, RET_DV), lambda i: (n - 1 - i, 0, 0, 0)),
                  pl.BlockSpec((CHUNK, RET_V_COLS), rev)],
        out_specs=[pl.BlockSpec((CHUNK, cols), rev),
                   pl.BlockSpec((1, RET_V_COLS), lambda i: (0, 0))],
        out_shape=[jax.ShapeDtypeStruct((t, cols), BF16),
                   jax.ShapeDtypeStruct((1, RET_V_COLS), F32)],
        scratch_shapes=[pltpu.VMEM((RET_HEADS, RET_DK, RET_DV), F32)],
        compiler_params=_params("arbitrary"),
    )(proj, cos, sin, intra, qd, kd, gn, o_saved, states, dy)


def _att_common(q_ref, kp_ref, vp_ref):
    blk = pl.program_id(1)
    start = pl.multiple_of(blk * Q_BLOCK, Q_BLOCK)
    kw = kp_ref[pl.ds(start, K_WINDOW), :]
    vw = vp_ref[pl.ds(start, K_WINDOW), :]
    kvalid = blk * Q_BLOCK - K_PAD + lax.broadcasted_iota(jnp.int32, (1, K_WINDOW), 1) >= 0
    lane = lax.broadcasted_iota(jnp.int32, (1, LANES), 1)
    return start, q_ref[...], kw, vw, kvalid, (lane < ATT_DH, lane >= ATT_DH)


def _att_probs(qm, kw, bias, kvalid):
    s = jnp.where(kvalid, _dot_nt(qm, kw) + bias, NEG)
    e = jnp.exp(s - jnp.max(s, axis=-1, keepdims=True))
    return e * (1.0 / jnp.sum(e, axis=-1, keepdims=True))


def _att_specs(t, tp):
    qspec = pl.BlockSpec((Q_BLOCK, LANES), lambda h, i: (i, h))
    kspec = pl.BlockSpec((tp, LANES), lambda h, i: (0, h))
    bspec = pl.BlockSpec((2, Q_BLOCK, K_WINDOW), lambda h, i: (h, 0, 0))
    return qspec, kspec, bspec


def _att_fwd(q, kp, vp, bias, name):
    t, d = q.shape
    tp = kp.shape[0]

    def body(q_ref, kp_ref, vp_ref, bias_ref, o_ref):
        _, q2, kw, vw, kvalid, sel = _att_common(q_ref, kp_ref, vp_ref)
        outs = []
        for hh in range(2):
            p = _att_probs(jnp.where(sel[hh], q2, 0), kw, bias_ref[hh], kvalid)
            outs.append(_dot(p.astype(BF16), vw))
        o_ref[...] = jnp.where(sel[0], outs[0], outs[1]).astype(BF16)

    qspec, kspec, bspec = _att_specs(t, tp)
    return pl.pallas_call(
        body, name=name, grid=(d // LANES, t // Q_BLOCK),
        in_specs=[qspec, kspec, kspec, bspec], out_specs=qspec,
        out_shape=jax.ShapeDtypeStruct((t, d), BF16),
        compiler_params=_params("parallel", "arbitrary"),
    )(q, kp, vp, bias)


def _att_bwd(q, kp, vp, bias, do, name):
    t, d = q.shape
    tp = kp.shape[0]

    def body(q_ref, kp_ref, vp_ref, bias_ref, do_ref, dq_ref, dkp_ref, dvp_ref, db_ref):
        @pl.when(pl.program_id(1) == 0)
        def _():
            dkp_ref[...] = jnp.zeros_like(dkp_ref)
            dvp_ref[...] = jnp.zeros_like(dvp_ref)
            db_ref[...] = jnp.zeros_like(db_ref)

        start, q2, kw, vw, kvalid, sel = _att_common(q_ref, kp_ref, vp_ref)
        do2 = do_ref[...]
        dqs, dk, dv = [], None, None
        for hh in range(2):
            qm = jnp.where(sel[hh], q2, 0)
            dom = jnp.where(sel[hh], do2, 0)
            p = _att_probs(qm, kw, bias_ref[hh], kvalid)
            dp = _dot_nt(dom, vw)
            ds = p * (dp - jnp.sum(dp * p, axis=-1, keepdims=True))
            db_ref[hh] += ds
            dsb = ds.astype(BF16)
            dqs.append(_dot(dsb, kw))
            dk_h = _dot_tn(dsb, qm)
            dv_h = _dot_tn(p.astype(BF16), dom)
            dk = dk_h if dk is None else dk + dk_h
            dv = dv_h if dv is None else dv + dv_h
        dq_ref[...] = jnp.where(sel[0], dqs[0], dqs[1])
        dkp_ref[pl.ds(start, K_WINDOW), :] += dk
        dvp_ref[pl.ds(start, K_WINDOW), :] += dv

    qspec, kspec, bspec = _att_specs(t, tp)
    return pl.pallas_call(
        body, name=name, grid=(d // LANES, t // Q_BLOCK),
        in_specs=[qspec, kspec, kspec, bspec, qspec],
        out_specs=[qspec, kspec, kspec, bspec],
        out_shape=[jax.ShapeDtypeStruct((t, d), F32),
                   jax.ShapeDtypeStruct((tp, d), F32),
                   jax.ShapeDtypeStruct((tp, d), F32),
                   jax.ShapeDtypeStruct((ATT_HEADS, Q_BLOCK, K_WINDOW), F32)],
        compiler_params=_params("parallel", "arbitrary"),
    )(q, kp, vp, bias, do)


def _rel_bin_matrix():
    rows = REL_DELTAS * 2 * REL_BLK
    rho = lax.broadcasted_iota(jnp.int32, (rows, REL_PAD), 0)
    col = lax.broadcasted_iota(jnp.int32, (rows, REL_PAD), 1)
    assert 2 * REL_BLK == 256
    delta = rho >> 8
    c = rho & 255
    dist = K_PAD + REL_BLK * (delta - (K_WINDOW // REL_BLK - 1)) + (c - (REL_BLK - 1))
    idx = jnp.clip(dist, -REL_CLIP, REL_CLIP) + REL_CLIP
    return col == idx


def _rel_shift_matrix(r):
    c = lax.broadcasted_iota(jnp.int32, (2 * REL_BLK, REL_BLK), 0)
    s = lax.broadcasted_iota(jnp.int32, (2 * REL_BLK, REL_BLK), 1)
    return c == r - s + (REL_BLK - 1)


def _rel_expand(rel_pad, name):
    heads = rel_pad.shape[0]
    rows = REL_DELTAS * 2 * REL_BLK

    def body_bin(r_ref, o_ref):
        onehot = jnp.where(_rel_bin_matrix(), 1.0, 0.0).astype(BF16)
        hi, mid, lo = _split3(r_ref[...])
        o_ref[...] = _dot_nt(hi, onehot) + _dot_nt(mid, onehot) + _dot_nt(lo, onehot)

    by_delta = pl.pallas_call(
        body_bin, name=name + "_bin",
        out_shape=jax.ShapeDtypeStruct((heads, rows), F32),
        compiler_params=pltpu.CompilerParams(vmem_limit_bytes=VMEM_LIMIT_V7X),
    )(rel_pad)
    by_delta = by_delta.reshape(heads * REL_DELTAS, 2 * REL_BLK)

    def body_shift(t_ref, o_ref):
        r = pl.program_id(0)
        onehot = jnp.where(_rel_shift_matrix(r), 1.0, 0.0).astype(BF16)
        hi, mid, lo = _split3(t_ref[...])
        o_ref[0] = _dot(hi, onehot) + _dot(mid, onehot) + _dot(lo, onehot)

    return pl.pallas_call(
        body_shift, name=name + "_shift", grid=(REL_BLK,),
        in_specs=[pl.BlockSpec(by_delta.shape, lambda r: (0, 0))],
        out_specs=pl.BlockSpec((1, heads * REL_DELTAS, REL_BLK), lambda r: (r, 0, 0)),
        out_shape=jax.ShapeDtypeStruct((REL_BLK, heads * REL_DELTAS, REL_BLK), F32),
        compiler_params=_params("parallel"),
    )(by_delta)


def _bias_table(rel_bias, name):
    heads = rel_bias.shape[0]
    rel_pad = jnp.pad(rel_bias, ((0, 0), (0, REL_PAD - REL_TABLE)))
    tiles = _rel_expand(rel_pad, name)
    tiles = tiles.reshape(REL_BLK, heads, REL_DELTAS, REL_BLK).transpose(1, 2, 0, 3)
    na, nb = Q_BLOCK // REL_BLK, K_WINDOW // REL_BLK
    rows = [jnp.concatenate([tiles[:, a - b + nb - 1] for b in range(nb)], axis=-1) for a in range(na)]
    table = jnp.concatenate(rows, axis=-2)
    qc = np.arange(Q_BLOCK)[:, None] // CHUNK
    kc = np.arange(K_WINDOW)[None, :] // CHUNK
    band = (kc >= qc) & (kc <= qc + PAST_CHUNKS)
    return jnp.where(jnp.asarray(band)[None], table, NEG)


def _rel_reduce(db, name):
    heads = db.shape[0]
    na, nb = Q_BLOCK // REL_BLK, K_WINDOW // REL_BLK

    def body_fold(db_ref, g_ref):
        for delta in range(REL_DELTAS):
            acc = None
            for a in range(na):
                b = a - (delta - (nb - 1))
                if 0 <= b < nb:
                    tile = db_ref[0, a * REL_BLK:(a + 1) * REL_BLK, b * REL_BLK:(b + 1) * REL_BLK]
                    acc = tile if acc is None else acc + tile
            g_ref[0, delta] = acc

    folded = pl.pallas_call(
        body_fold, name=name + "_fold", grid=(heads,),
        in_specs=[pl.BlockSpec((1, Q_BLOCK, K_WINDOW), lambda h: (h, 0, 0))],
        out_specs=pl.BlockSpec((1, REL_DELTAS, REL_BLK, REL_BLK), lambda h: (h, 0, 0, 0)),
        out_shape=jax.ShapeDtypeStruct((heads, REL_DELTAS, REL_BLK, REL_BLK), F32),
        compiler_params=_params("parallel"),
    )(db)
    by_row = folded.transpose(2, 0, 1, 3).reshape(REL_BLK, heads * REL_DELTAS, REL_BLK)

    def body_diag(g_ref, d_ref):
        r = pl.program_id(0)
        onehot = jnp.where(_rel_shift_matrix(r), 1.0, 0.0).astype(BF16)
        hi, mid, lo = _split3(g_ref[0])
        _accumulate(d_ref, _dot_nt(hi, onehot) + _dot_nt(mid, onehot) + _dot_nt(lo, onehot), r)

    diag = pl.pallas_call(
        body_diag, name=name + "_diag", grid=(REL_BLK,),
        in_specs=[pl.BlockSpec((1, heads * REL_DELTAS, REL_BLK), lambda r: (r, 0, 0))],
        out_specs=pl.BlockSpec((heads * REL_DELTAS, 2 * REL_BLK), lambda r: (0, 0)),
        out_shape=jax.ShapeDtypeStruct((heads * REL_DELTAS, 2 * REL_BLK), F32),
        compiler_params=_params("arbitrary"),
    )(by_row)
    diag = diag.reshape(heads, REL_DELTAS * 2 * REL_BLK)

    def body_bin(d_ref, o_ref):
        onehot = jnp.where(_rel_bin_matrix(), 1.0, 0.0).astype(BF16)
        hi, mid, lo = _split3(d_ref[...])
        o_ref[...] = _dot(hi, onehot) + _dot(mid, onehot) + _dot(lo, onehot)

    out = pl.pallas_call(
        body_bin, name=name + "_bin",
        out_shape=jax.ShapeDtypeStruct((heads, REL_PAD), F32),
        compiler_params=pltpu.CompilerParams(vmem_limit_bytes=VMEM_LIMIT_V7X),
    )(diag)
    return out[:, :REL_TABLE]


def _sum_leading(x, name):
    n, r, c = x.shape
    tr = _pick(r, 256, 8)

    def body(x_ref, o_ref):
        acc = x_ref[0].astype(F32)
        for k in range(1, n):
            acc = acc + x_ref[k].astype(F32)
        o_ref[...] = acc

    return pl.pallas_call(
        body, name=name, grid=(r // tr,),
        in_specs=[pl.BlockSpec((n, tr, c), lambda i: (0, i, 0))],
        out_specs=pl.BlockSpec((tr, c), lambda i: (i, 0)),
        out_shape=jax.ShapeDtypeStruct((r, c), F32),
        compiler_params=_params("parallel"),
    )(x)


def _pair_add(g, recv, parity, name):
    _, r, c = g.shape
    tr = _pick(r, 256, 16)

    def body(par_ref, g_ref, r_ref, o_ref):
        o_ref[...] = (g_ref[...].astype(F32) + r_ref[...].astype(F32)).astype(BF16)

    return pl.pallas_call(
        body, name=name,
        grid_spec=pltpu.PrefetchScalarGridSpec(
            num_scalar_prefetch=1, grid=(4, r // tr),
            in_specs=[pl.BlockSpec((1, tr, c), lambda k, i, par: (2 * k + par[0], i, 0)),
                      pl.BlockSpec((1, tr, c), lambda k, i, par: (k, i, 0))],
            out_specs=pl.BlockSpec((1, tr, c), lambda k, i, par: (k, i, 0))),
        out_shape=jax.ShapeDtypeStruct((4, r, c), BF16),
        compiler_params=_params("parallel", "parallel"),
    )(parity, g, recv)


def _adamw(w, g_parts, m, v, name):
    r, c = w.shape
    n = g_parts.shape[0]
    tr = _pick(r, 256, 16 if g_parts.dtype == BF16 else 8)
    c1 = 1.0 - ADAM_B1 ** ADAM_STEP
    c2 = 1.0 - ADAM_B2 ** ADAM_STEP

    def body(w_ref, g_ref, m_ref, v_ref, go_ref, d_ref, nm_ref, nv_ref):
        gv = g_ref[0].astype(F32)
        for k in range(1, n):
            gv = gv + g_ref[k].astype(F32)
        nm = ADAM_B1 * m_ref[...] + (1.0 - ADAM_B1) * gv
        nv = ADAM_B2 * v_ref[...] + (1.0 - ADAM_B2) * (gv * gv)
        go_ref[...] = gv
        d_ref[...] = -ADAM_LR * ((nm / c1) / (jnp.sqrt(nv / c2) + ADAM_EPS) + ADAM_WD * w_ref[...])
        nm_ref[...] = nm
        nv_ref[...] = nv

    spec = pl.BlockSpec((tr, c), lambda i: (i, 0))
    shp = jax.ShapeDtypeStruct((r, c), F32)
    return pl.pallas_call(
        body, name=name, grid=(r // tr,),
        in_specs=[spec, pl.BlockSpec((n, tr, c), lambda i: (0, i, 0)), spec, spec],
        out_specs=[spec] * 4, out_shape=[shp] * 4,
        compiler_params=_params("parallel"),
    )(w, g_parts, m, v)


def _my_place():
    return lax.axis_index("x"), lax.axis_index("y"), lax.axis_index("c")


def _all_gather(xs, name):
    n = len(xs)

    def body(*refs):
        x_refs, out_refs = refs[:n], refs[n:2 * n]
        send_sems, recv_sems, local_sems = refs[2 * n:]
        x, y, c = _my_place()
        me, sibling = (x, y, c), (x, y, 1 - c)
        chips = [(1 - x, y), (x, 1 - y), (1 - x, 1 - y)]

        def slot(a, px, py, pc):
            return out_refs[a].at[4 * px + 2 * py + pc]

        def copy(k, a, block, to, own=False):
            return pltpu.make_async_remote_copy(
                src_ref=x_refs[a] if own else slot(a, *block), dst_ref=slot(a, *block),
                send_sem=send_sems.at[k, a], recv_sem=recv_sems.at[k, a],
                device_id=to, device_id_type=MESH)

        mine = [pltpu.make_async_copy(x_refs[a], slot(a, *me), local_sems.at[a]) for a in range(n)]
        first = []
        for a in range(n):
            mine[a].start()
            first.append(copy(0, a, me, sibling, own=True))
            first += [copy(1 + j, a, me, (*chip, c), own=True) for j, chip in enumerate(chips)]
        for cp in first:
            cp.start()
        passed = []
        for j, chip in enumerate(chips):
            for a in range(n):
                copy(1 + j, a, (*chip, c), me).wait_recv()
                passed.append(copy(4 + j, a, (*chip, c), sibling))
                passed[-1].start()
        for a in range(n):
            copy(0, a, sibling, me).wait_recv()
            for j, chip in enumerate(chips):
                copy(4 + j, a, (*chip, 1 - c), me).wait_recv()
        for cp in first + passed:
            cp.wait_send()
        for cp in mine:
            cp.wait()

    return pl.pallas_call(
        body, name=name,
        out_shape=[jax.ShapeDtypeStruct((N_DEV,) + x.shape, x.dtype) for x in xs],
        in_specs=[ANY] * n, out_specs=[ANY] * n,
        scratch_shapes=[pltpu.SemaphoreType.DMA((7, n)), pltpu.SemaphoreType.DMA((7, n)),
                        pltpu.SemaphoreType.DMA((n,))],
    )(*xs)


def _swap_with_sibling(gs, name):
    n = len(gs)

    def body(*refs):
        g_refs, out_refs = refs[:n], refs[n:2 * n]
        send_sems, recv_sems = refs[2 * n:]
        x, y, c = _my_place()
        copies = [pltpu.make_async_remote_copy(
            src_ref=g_refs[a].at[2 * k + 1 - c], dst_ref=out_refs[a].at[k],
            send_sem=send_sems.at[k, a], recv_sem=recv_sems.at[k, a],
            device_id=(x, y, 1 - c), device_id_type=MESH) for a in range(n) for k in range(4)]
        for cp in copies:
            cp.start()
        for cp in copies:
            cp.wait()

    return pl.pallas_call(
        body, name=name,
        out_shape=[jax.ShapeDtypeStruct((4,) + g.shape[1:], g.dtype) for g in gs],
        in_specs=[ANY] * n, out_specs=[ANY] * n,
        scratch_shapes=[pltpu.SemaphoreType.DMA((4, n)), pltpu.SemaphoreType.DMA((4, n))],
    )(*gs)


def _scatter_to_chips(ps, name):
    n = len(ps)

    def body(*refs):
        p_refs, out_refs = refs[:n], refs[n:2 * n]
        send_sems, recv_sems, local_sems = refs[2 * n:]
        x, y, c = _my_place()
        my_chip = 2 * x + y
        chips = [(1 - x, y), (x, 1 - y), (1 - x, 1 - y)]
        mine = [pltpu.make_async_copy(p_refs[a].at[my_chip], out_refs[a].at[my_chip], local_sems.at[a])
                for a in range(n)]
        for cp in mine:
            cp.start()
        copies = [pltpu.make_async_remote_copy(
            src_ref=p_refs[a].at[2 * cx + cy], dst_ref=out_refs[a].at[my_chip],
            send_sem=send_sems.at[j, a], recv_sem=recv_sems.at[j, a],
            device_id=(cx, cy, c), device_id_type=MESH)
            for a in range(n) for j, (cx, cy) in enumerate(chips)]
        for cp in copies:
            cp.start()
        for cp in copies:
            cp.wait()
        for cp in mine:
            cp.wait()

    return pl.pallas_call(
        body, name=name,
        out_shape=[jax.ShapeDtypeStruct(p.shape, p.dtype) for p in ps],
        in_specs=[ANY] * n, out_specs=[ANY] * n,
        scratch_shapes=[pltpu.SemaphoreType.DMA((3, n)), pltpu.SemaphoreType.DMA((3, n)),
                        pltpu.SemaphoreType.DMA((n,))],
    )(*ps)


BIG = (("a_w_in", 1), ("a_w_o", 0), ("a_w_gu", 1), ("a_w_down", 0), ("w_kv", 1),
       ("b_w_q", 0), ("b_w_o", 0), ("b_w_gu", 1), ("b_w_down", 0))

SMALL = (("a_norm_g", D_MODEL, True), ("a_gn_g", RET_V_COLS, True), ("a_ffn_norm_g", D_MODEL, True),
         ("kv_norm_g", D_MODEL, False), ("b_norm_g", D_MODEL, False), ("b_ffn_norm_g", D_MODEL, False),
         ("k_norm_g", ATT_DH, False), ("b_q_norm_g", ATT_DH, False),
         ("b_rel_bias", ATT_HEADS * REL_TABLE, False))
SMALL_ROWS, SMALL_COLS = 16, 1024


def _pack_small(vals):
    flat = jnp.concatenate([vals[n].reshape(-1) for n, _, _ in SMALL])
    return jnp.pad(flat, (0, SMALL_ROWS * SMALL_COLS - flat.shape[0])).reshape(SMALL_ROWS, SMALL_COLS)


def _unpack_small(packed, local):
    flat, out, pos = packed.reshape(-1), {}, 0
    for n, length, sharded in SMALL:
        ln = length // N_DEV if (local and sharded) else length
        out[n] = flat[pos:pos + ln]
        pos += ln
    return out


def _ffn_fwd(x_in, norm_g, w_gu, w_down, tag):
    h = _rms_fwd(x_in, norm_g, tag + "_norm")
    gu, act = _mm(h, w_gu, "nn", tag + "_gu", epilogue="swiglu")
    x_out = _mm(act, w_down, "nn", tag + "_down", res=x_in)
    return x_out, (h, gu, act)


def _ffn_bwd(dx_out, x_in, norm_g, w_gu, w_down, saved, tag):
    h, gu, act = saved
    fb = w_gu.shape[2]
    dgu = _mm(dx_out, w_down, "nt", tag + "_dgu", out_block=fb, epilogue="swiglu_bwd", extra=gu)
    dgu = dgu.reshape(w_gu.shape[0], -1, fb)
    g_down = _mm(act, dx_out, "tn", tag + "_gdown", out_dtype=BF16)
    dh = _mm(dgu, w_gu, "nt", tag + "_dh")
    g_gu = _mm(h, dgu, "tn", tag + "_ggu", out_dtype=BF16, out_block=fb)
    dx_in, g_norm = _rms_bwd(x_in, norm_g, dh, dx_out, tag + "_dnorm")
    return dx_in, g_gu, g_down, g_norm


def _local_step(x, target, w, s):
    t = x.shape[0]
    consts = _ret_consts(t)
    bd = jnp.asarray(np.kron(np.eye(ATT_HEADS, dtype=np.float32),
                             np.ones((ATT_DH, ATT_DH), np.float32))).astype(BF16)
    kg_t = jnp.tile(s["k_norm_g"], (1, ATT_HEADS))
    qg_t = jnp.tile(s["b_q_norm_g"], (1, ATT_HEADS))
    q_scale = ATT_DH ** -0.5
    in_blk, kv_blk = w["a_w_in"].shape[2], w["w_kv"].shape[2]

    h1 = _rms_fwd(x, s["a_norm_g"], "a_norm")
    proj = _mm(h1, w["a_w_in"], "nn", "a_proj")
    y, o_ret, states = _ret_fwd(proj, s["a_gn_g"], consts, "a_ret")
    x1 = _mm(y, w["a_w_o"], "nn", "a_out", res=x)
    x2, ffn_a = _ffn_fwd(x1, s["a_ffn_norm_g"], w["a_w_gu"], w["a_w_down"], "a_ffn")

    u = _rms_fwd(x2, s["kv_norm_g"], "kv_norm")
    kv = _mm(u, w["w_kv"], "nn", "kv_proj")
    kp, vp = _kv_prep(kv, kg_t, bd, "kv_prep")

    h3 = _rms_fwd(x2, s["b_norm_g"], "b_norm")
    q_raw = _mm(h3, w["b_w_q"], "nn", "b_q")
    qn = _q_hnorm(q_raw, qg_t, bd, q_scale, "q_hnorm")
    bias = _bias_table(s["b_rel_bias"].reshape(ATT_HEADS, REL_TABLE), "rel")
    o_att = _att_fwd(qn, kp, vp, bias, "b_att")
    x3 = _mm(o_att, w["b_w_o"], "nn", "b_out", res=x2)
    x4, ffn_b = _ffn_fwd(x3, s["b_ffn_norm_g"], w["b_w_gu"], w["b_w_down"], "b_ffn")

    dy, loss = _loss_head(x4, target, "loss")

    g = {}
    dx3, g["b_w_gu"], g["b_w_down"], g["b_ffn_norm_g"] = _ffn_bwd(
        dy, x3, s["b_ffn_norm_g"], w["b_w_gu"], w["b_w_down"], ffn_b, "b_ffn")
    do_att = _mm(dx3, w["b_w_o"], "nt", "b_dout", out_dtype=BF16)
    g["b_w_o"] = _mm(o_att, dx3, "tn", "b_gout", out_dtype=BF16)
    dq, dkp, dvp, db = _att_bwd(qn, kp, vp, bias, do_att, "b_datt")
    g["b_rel_bias"] = _rel_reduce(db, "drel").reshape(1, -1)
    dq_raw, gq = _q_dhnorm(q_raw, qg_t, bd, dq, q_scale, "q_dhnorm")
    g["b_q_norm_g"] = gq.reshape(ATT_HEADS, ATT_DH).sum(axis=0, keepdims=True)
    dh3 = _mm(dq_raw, w["b_w_q"], "nt", "b_dq")
    g["b_w_q"] = _mm(h3, dq_raw, "tn", "b_gq", out_dtype=BF16)
    dx2, g["b_norm_g"] = _rms_bwd(x2, s["b_norm_g"], dh3, dx3, "b_dnorm")

    dkv, gk = _kv_dprep(kv, kg_t, bd, dkp, dvp, "kv_dprep")
    g["k_norm_g"] = gk.reshape(ATT_HEADS, ATT_DH).sum(axis=0, keepdims=True)
    du = _mm(dkv, w["w_kv"], "nt", "kv_du")
    g["w_kv"] = _mm(u, dkv, "tn", "kv_g", out_dtype=BF16, out_block=kv_blk)
    dx2, g["kv_norm_g"] = _rms_bwd(x2, s["kv_norm_g"], du, dx2, "kv_dnorm")

    dx1, g["a_w_gu"], g["a_w_down"], g["a_ffn_norm_g"] = _ffn_bwd(
        dx2, x1, s["a_ffn_norm_g"], w["a_w_gu"], w["a_w_down"], ffn_a, "a_ffn")
    dy_ret = _mm(dx1, w["a_w_o"], "nt", "a_dout")
    g["a_w_o"] = _mm(y, dx1, "tn", "a_gout", out_dtype=BF16)
    dproj, g["a_gn_g"] = _ret_bwd(proj, s["a_gn_g"], o_ret, states, dy_ret, consts, "a_dret")
    dh1 = _mm(dproj, w["a_w_in"], "nt", "a_dproj")
    g["a_w_in"] = _mm(h1, dproj, "tn", "a_gin", out_dtype=BF16, out_block=in_blk)
    grad_x, g["a_norm_g"] = _rms_bwd(x, s["a_norm_g"], dh1, dx1, "a_dnorm")
    return loss, grad_x, g


ARG_NAMES = ("x", "a_norm_g", "a_w_in", "a_gn_g", "a_w_o", "a_ffn_norm_g", "a_w_gu", "a_w_down",
             "kv_norm_g", "w_kv", "k_norm_g", "b_norm_g", "b_w_q", "b_q_norm_g", "b_rel_bias", "b_w_o",
             "b_ffn_norm_g", "b_w_gu", "b_w_down")
WEIGHT_NAMES = ARG_NAMES[1:]


def _big_shard(a):
    return a[0] if a.ndim == 3 else a


def kernel(x, a_norm_g, a_w_in, a_gn_g, a_w_o, a_ffn_norm_g, a_w_gu, a_w_down, kv_norm_g, w_kv, k_norm_g, b_norm_g, b_w_q, b_q_norm_g, b_rel_bias, b_w_o, b_ffn_norm_g, b_w_gu, b_w_down, loss_target, m_a_norm_g, m_a_w_in, m_a_gn_g, m_a_w_o, m_a_ffn_norm_g, m_a_w_gu, m_a_w_down, m_kv_norm_g, m_w_kv, m_k_norm_g, m_b_norm_g, m_b_w_q, m_b_q_norm_g, m_b_rel_bias, m_b_w_o, m_b_ffn_norm_g, m_b_w_gu, m_b_w_down, v_a_norm_g, v_a_w_in, v_a_gn_g, v_a_w_o, v_a_ffn_norm_g, v_a_w_gu, v_a_w_down, v_kv_norm_g, v_w_kv, v_k_norm_g, v_b_norm_g, v_b_w_q, v_b_q_norm_g, v_b_rel_bias, v_b_w_o, v_b_ffn_norm_g, v_b_w_gu, v_b_w_down):
    args = (x, a_norm_g, a_w_in, a_gn_g, a_w_o, a_ffn_norm_g, a_w_gu, a_w_down, kv_norm_g, w_kv, k_norm_g,
            b_norm_g, b_w_q, b_q_norm_g, b_rel_bias, b_w_o, b_ffn_norm_g, b_w_gu, b_w_down)
    p = dict(zip(ARG_NAMES, args))
    m_all = dict(zip(WEIGHT_NAMES, (m_a_norm_g, m_a_w_in, m_a_gn_g, m_a_w_o, m_a_ffn_norm_g, m_a_w_gu,
                                    m_a_w_down, m_kv_norm_g, m_w_kv, m_k_norm_g, m_b_norm_g, m_b_w_q,
                                    m_b_q_norm_g, m_b_rel_bias, m_b_w_o, m_b_ffn_norm_g, m_b_w_gu, m_b_w_down)))
    v_all = dict(zip(WEIGHT_NAMES, (v_a_norm_g, v_a_w_in, v_a_gn_g, v_a_w_o, v_a_ffn_norm_g, v_a_w_gu,
                                    v_a_w_down, v_kv_norm_g, v_w_kv, v_k_norm_g, v_b_norm_g, v_b_w_q,
                                    v_b_q_norm_g, v_b_rel_bias, v_b_w_o, v_b_ffn_norm_g, v_b_w_gu, v_b_w_down)))
    xi, yi, ci = _my_place()
    me = 4 * xi + 2 * yi + ci
    big_names = [n for n, _ in BIG]
    axis_of = dict(BIG)

    big_local = {n: _big_shard(p[n]) for n in big_names}
    small_local = _pack_small({n: p[n] for n, _, _ in SMALL})
    gathered = _all_gather([big_local[n].astype(BF16) for n in big_names] + [small_local], "gather_w")
    w_full = {}
    for n, arr in zip(big_names, gathered):
        w_full[n] = arr.reshape(-1, arr.shape[2]) if axis_of[n] == 0 else arr
    flat_g = gathered[-1].reshape(N_DEV, -1)
    s_full, pos = {}, 0
    for n, length, sharded in SMALL:
        ln = length // N_DEV if sharded else length
        s_full[n] = flat_g[:, pos:pos + ln].reshape(1, -1) if sharded else p[n].reshape(1, -1)
        pos += ln

    loss, grad_x, g = _local_step(x[0], loss_target[0], w_full, s_full)
    loss = lax.psum(loss[0, 0], ("x", "y", "c"))

    g_blocks = [g[n].reshape(N_DEV, -1, g[n].shape[-1]) if axis_of[n] == 0 else g[n] for n in big_names]
    from_sibling = _swap_with_sibling(g_blocks, "rs_sibling")
    parity = jnp.reshape(ci, (1,)).astype(jnp.int32)
    chip_sums = [_pair_add(gb, fs, parity, "rs_pair_" + n) for n, gb, fs in zip(big_names, g_blocks, from_sibling)]
    from_chips = dict(zip(big_names, _scatter_to_chips(chip_sums, "rs_chips")))

    g_small_all = _all_gather([_pack_small({n: g[n] for n, _, _ in SMALL})], "gather_gsmall")[0]
    g_small = _unpack_small(_sum_leading(g_small_all, "gsmall_sum"), local=False)
    for n, length, sharded in SMALL:
        if sharded:
            g_small[n] = lax.dynamic_slice(g_small[n], (me * (length // N_DEV),), (length // N_DEV,))

    grads, deltas, new_m, new_v = {}, {}, {}, {}
    for n in big_names:
        outs = _adamw(big_local[n], from_chips[n], _big_shard(m_all[n]), _big_shard(v_all[n]), "adamw_" + n)
        grads[n], deltas[n], new_m[n], new_v[n] = (a.reshape(p[n].shape) for a in outs)
    pk = lambda src: _pack_small({n: src[n] for n, _, _ in SMALL})
    outs = _adamw(small_local, pk(g_small)[None], pk(m_all), pk(v_all), "adamw_small")
    g_s, d_s, nm_s, nv_s = (_unpack_small(a, local=True) for a in outs)
    for n, _, _ in SMALL:
        grads[n], deltas[n], new_m[n], new_v[n] = (a[n].reshape(p[n].shape) for a in (g_s, d_s, nm_s, nv_s))

    return (loss, grad_x[None], *[grads[n] for n in WEIGHT_NAMES], *[deltas[n] for n in WEIGHT_NAMES],
            *[new_m[n] for n in WEIGHT_NAMES], *[new_v[n] for n in WEIGHT_NAMES])
```

```python
import numpy as np
import jax
import jax.numpy as jnp
from jax import lax
from jax.experimental import pallas as pl
from jax.experimental.pallas import tpu as pltpu

F32 = jnp.float32
BF16 = jnp.bfloat16

N_DEV = 8
D_MODEL = 1024
CHUNK = 64
EPS = 1e-6
RET_HEADS, RET_DK, RET_DV = 4, 256, 512
RET_Q_COLS = RET_HEADS * RET_DK
RET_V_COLS = RET_HEADS * RET_DV
ATT_HEADS, ATT_DH = 16, 64
PAST_CHUNKS = 8
REL_CLIP = 256
REL_TABLE = 2 * REL_CLIP + 1
FFN_HIDDEN = 2816
ROPE_BASE = 10000.0
LANES = 128
Q_BLOCK = 256
K_PAD = PAST_CHUNKS * CHUNK
K_WINDOW = Q_BLOCK + K_PAD
REL_BLK = 128
REL_DELTAS = Q_BLOCK // REL_BLK + K_WINDOW // REL_BLK - 1
REL_PAD = 640
NEG = -1e30
VMEM_LIMIT_V7X = 56 * 1024 * 1024
ADAM_LR, ADAM_B1, ADAM_B2, ADAM_EPS, ADAM_WD, ADAM_STEP = 1e-3, 0.9, 0.999, 1e-8, 0.01, 10
MESH = pl.DeviceIdType.MESH
ANY = pl.BlockSpec(memory_space=pl.ANY)


def _params(*semantics):
    return pltpu.CompilerParams(dimension_semantics=semantics, vmem_limit_bytes=VMEM_LIMIT_V7X)


def _pick(dim, cap, align):
    best = None
    for t in range(align, min(dim, cap) + 1, align):
        if dim % t == 0:
            best = t
    assert best is not None, (dim, cap, align)
    return best


def _dot(a, b):
    return lax.dot_general(a, b, (((1,), (0,)), ((), ())), preferred_element_type=F32)


def _dot_nt(a, b):
    return lax.dot_general(a, b, (((1,), (1,)), ((), ())), preferred_element_type=F32)


def _dot_tn(a, b):
    return lax.dot_general(a, b, (((0,), (0,)), ((), ())), preferred_element_type=F32)


def _split2(x):
    hi = x.astype(BF16)
    lo = (x - hi.astype(F32)).astype(BF16)
    return hi, lo


def _split3(x):
    hi = x.astype(BF16)
    r = x - hi.astype(F32)
    mid = r.astype(BF16)
    lo = (r - mid.astype(F32)).astype(BF16)
    return hi, mid, lo


def _sigmoid(x):
    return 1.0 / (1.0 + jnp.exp(-x))


def _accumulate(ref, part, step):
    @pl.when(step == 0)
    def _():
        ref[...] = part

    @pl.when(step > 0)
    def _():
        ref[...] += part


def _my_place():
    return lax.axis_index("x"), lax.axis_index("y"), lax.axis_index("c")


def _flip(v, bit):
    return 1 - v if bit else v


class _GatherRider:
    def __init__(self, xs):
        self.inputs = list(xs)
        n = len(xs)
        self.out_shape = [jax.ShapeDtypeStruct((N_DEV,) + x.shape, x.dtype) for x in xs]
        self.scratch = [pltpu.SemaphoreType.DMA((7, n)), pltpu.SemaphoreType.DMA((7, n)),
                        pltpu.SemaphoreType.DMA((n,))]
        self.results = None

    def _copies(self, x_refs, out_refs, sems):
        send_sems, recv_sems, local_sems = sems
        n = len(x_refs)
        x, y, c = _my_place()
        me, sibling = (x, y, c), (x, y, 1 - c)
        chips = [(1 - x, y), (x, 1 - y), (1 - x, 1 - y)]

        def slot(a, px, py, pc):
            return out_refs[a].at[4 * px + 2 * py + pc]

        def copy(k, a, block, to, own=False):
            return pltpu.make_async_remote_copy(
                src_ref=x_refs[a] if own else slot(a, *block), dst_ref=slot(a, *block),
                send_sem=send_sems.at[k, a], recv_sem=recv_sems.at[k, a],
                device_id=to, device_id_type=MESH)

        mine = [pltpu.make_async_copy(x_refs[a], slot(a, *me), local_sems.at[a]) for a in range(n)]
        first = []
        for a in range(n):
            first.append(copy(0, a, me, sibling, own=True))
            first += [copy(1 + j, a, me, (*chip, c), own=True) for j, chip in enumerate(chips)]
        return n, c, me, sibling, chips, copy, mine, first

    def start(self, x_refs, out_refs, sems):
        _, _, _, _, _, _, mine, first = self._copies(x_refs, out_refs, sems)
        for cp in mine + first:
            cp.start()

    def finish(self, x_refs, out_refs, sems):
        n, c, me, sibling, chips, copy, mine, first = self._copies(x_refs, out_refs, sems)
        passed = []
        for j, chip in enumerate(chips):
            for a in range(n):
                copy(1 + j, a, (*chip, c), me).wait_recv()
                passed.append(copy(4 + j, a, (*chip, c), sibling))
                passed[-1].start()
        for a in range(n):
            copy(0, a, sibling, me).wait_recv()
            for j, chip in enumerate(chips):
                copy(4 + j, a, (*chip, 1 - c), me).wait_recv()
        for cp in first + passed:
            cp.wait_send()
        for cp in mine:
            cp.wait()


class _ScatterRider:
    def __init__(self, gs):
        self.inputs = list(gs)
        n = len(gs)
        self.out_shape = [jax.ShapeDtypeStruct(g.shape, g.dtype) for g in gs]
        self.scratch = [pltpu.SemaphoreType.DMA((7, n)), pltpu.SemaphoreType.DMA((7, n)),
                        pltpu.SemaphoreType.DMA((n,))]
        self.results = None

    def _copies(self, g_refs, out_refs, sems):
        send_sems, recv_sems, local_sems = sems
        x, y, c = _my_place()
        me = 4 * x + 2 * y + c
        mine, copies = [], []
        for a in range(len(g_refs)):
            mine.append(pltpu.make_async_copy(g_refs[a].at[me], out_refs[a].at[me], local_sems.at[a]))
            for k in range(1, N_DEV):
                px, py, pc = _flip(x, k & 4), _flip(y, k & 2), _flip(c, k & 1)
                copies.append(pltpu.make_async_remote_copy(
                    src_ref=g_refs[a].at[4 * px + 2 * py + pc], dst_ref=out_refs[a].at[me],
                    send_sem=send_sems.at[k - 1, a], recv_sem=recv_sems.at[k - 1, a],
                    device_id=(px, py, pc), device_id_type=MESH))
        return mine, copies

    def start(self, g_refs, out_refs, sems):
        mine, copies = self._copies(g_refs, out_refs, sems)
        for cp in mine + copies:
            cp.start()

    def finish(self, g_refs, out_refs, sems):
        mine, copies = self._copies(g_refs, out_refs, sems)
        for cp in copies + mine:
            cp.wait()


class _SiblingSwapRider:
    def __init__(self, gs):
        self.inputs = list(gs)
        n = len(gs)
        self.out_shape = [jax.ShapeDtypeStruct((4,) + g.shape[1:], g.dtype) for g in gs]
        self.scratch = [pltpu.SemaphoreType.DMA((4, n)), pltpu.SemaphoreType.DMA((4, n))]
        self.results = None

    def _copies(self, g_refs, out_refs, sems):
        send_sems, recv_sems = sems
        x, y, c = _my_place()
        return [pltpu.make_async_remote_copy(
            src_ref=g_refs[a].at[2 * k + 1 - c], dst_ref=out_refs[a].at[k],
            send_sem=send_sems.at[k, a], recv_sem=recv_sems.at[k, a],
            device_id=(x, y, 1 - c), device_id_type=MESH)
            for a in range(len(g_refs)) for k in range(4)]

    def start(self, g_refs, out_refs, sems):
        for cp in self._copies(g_refs, out_refs, sems):
            cp.start()

    def finish(self, g_refs, out_refs, sems):
        for cp in self._copies(g_refs, out_refs, sems):
            cp.wait()


class _ChipScatterRider:
    def __init__(self, ps):
        self.inputs = list(ps)
        n = len(ps)
        self.out_shape = [jax.ShapeDtypeStruct(p.shape, p.dtype) for p in ps]
        self.scratch = [pltpu.SemaphoreType.DMA((3, n)), pltpu.SemaphoreType.DMA((3, n)),
                        pltpu.SemaphoreType.DMA((n,))]
        self.results = None

    def _copies(self, p_refs, out_refs, sems):
        send_sems, recv_sems, local_sems = sems
        x, y, c = _my_place()
        my_chip = 2 * x + y
        chips = [(1 - x, y), (x, 1 - y), (1 - x, 1 - y)]
        n = len(p_refs)
        mine = [pltpu.make_async_copy(p_refs[a].at[my_chip], out_refs[a].at[my_chip], local_sems.at[a])
                for a in range(n)]
        copies = [pltpu.make_async_remote_copy(
            src_ref=p_refs[a].at[2 * cx + cy], dst_ref=out_refs[a].at[my_chip],
            send_sem=send_sems.at[j, a], recv_sem=recv_sems.at[j, a],
            device_id=(cx, cy, c), device_id_type=MESH)
            for a in range(n) for j, (cx, cy) in enumerate(chips)]
        return mine, copies

    def start(self, p_refs, out_refs, sems):
        mine, copies = self._copies(p_refs, out_refs, sems)
        for cp in mine + copies:
            cp.start()

    def finish(self, p_refs, out_refs, sems):
        mine, copies = self._copies(p_refs, out_refs, sems)
        for cp in copies + mine:
            cp.wait()


def _call(body, name, grid, in_specs, out_specs, out_shape, scratch, semantics, args, rider=None):
    in_specs, out_specs, out_shape, scratch = list(in_specs), list(out_specs), list(out_shape), list(scratch)
    if rider is None:
        return list(pl.pallas_call(
            body, name=name, grid=grid, in_specs=in_specs, out_specs=out_specs, out_shape=out_shape,
            scratch_shapes=scratch, compiler_params=_params(*semantics))(*args))
    n_in, n_out, n_scr = len(in_specs), len(out_specs), len(scratch)
    r_in, r_out = len(rider.inputs), len(rider.out_shape)

    def wrapped(*refs):
        cuts = np.cumsum([0, n_in, r_in, n_out, r_out, n_scr])
        hi, ri, ho, ro, hs = (refs[cuts[i]:cuts[i + 1]] for i in range(5))
        rs = refs[cuts[5]:]
        ids = [pl.program_id(d) for d in range(len(grid))]
        first, last = ids[0] == 0, ids[0] == grid[0] - 1
        for d in range(1, len(grid)):
            first = jnp.logical_and(first, ids[d] == 0)
            last = jnp.logical_and(last, ids[d] == grid[d] - 1)

        @pl.when(first)
        def _():
            rider.start(ri, ro, rs)

        body(*hi, *ho, *hs)

        @pl.when(last)
        def _():
            rider.finish(ri, ro, rs)

    outs = pl.pallas_call(
        wrapped, name=name, grid=grid,
        in_specs=in_specs + [ANY] * r_in, out_specs=out_specs + [ANY] * r_out,
        out_shape=out_shape + rider.out_shape, scratch_shapes=scratch + rider.scratch,
        compiler_params=_params(*(["arbitrary"] * len(grid))),
    )(*args, *rider.inputs)
    rider.results = list(outs[n_out:])
    return list(outs[:n_out])


def _exchange(rider, name):
    r_in, r_out = len(rider.inputs), len(rider.out_shape)

    def body(*refs):
        ri, ro, rs = refs[:r_in], refs[r_in:r_in + r_out], refs[r_in + r_out:]
        rider.start(ri, ro, rs)
        rider.finish(ri, ro, rs)

    return list(pl.pallas_call(
        body, name=name, in_specs=[ANY] * r_in, out_specs=[ANY] * r_out,
        out_shape=rider.out_shape, scratch_shapes=rider.scratch)(*rider.inputs))


MM_CAP_MN = 1024
MM_CAP_N = 1536
MM_CAP_K = 1536


def _mm(a, b, mode, name, out_dtype=F32, res=None, out_block=None, epilogue=None, extra=None, rider=None):
    a3, b3 = a.ndim == 3, b.ndim == 3
    um = un = uk = None
    if mode in ("nn", "nt"):
        if a3:
            m, uk = a.shape[1:]
            k = a.shape[0] * uk
        else:
            m, k = a.shape
    else:
        if a3:
            k, um = a.shape[1:]
            m = a.shape[0] * um
        else:
            k, m = a.shape
    if mode in ("nn", "tn"):
        if b3:
            kb, un = b.shape[1:]
            n = b.shape[0] * un
        else:
            kb, n = b.shape
        assert kb == k, (a.shape, b.shape, mode)
    else:
        if b3:
            n, ukb = b.shape[1:]
            assert b.shape[0] * ukb == k and uk in (None, ukb), (a.shape, b.shape, mode)
            uk = ukb
        else:
            n, kb = b.shape
            assert kb == k, (a.shape, b.shape, mode)
    if out_block is not None:
        assert un in (None, out_block)
        un = out_block

    def tile(dim, unit, cap, align):
        if unit is None:
            return _pick(dim, cap, align), 1
        c = max(1, cap // unit)
        while (dim // unit) % c:
            c -= 1
        return unit, c

    um, cm = tile(m, um, MM_CAP_MN if mode != "tn" else 1408, 128 if mode == "tn" else 16)
    un, cn = tile(n, un, MM_CAP_N, 128)
    uk, ck = tile(k, uk, MM_CAP_K if mode != "tn" else 1024, 128)
    if epilogue == "swiglu":
        assert mode == "nn" and b3 and res is None and out_block is None
        cn = 2
    if epilogue == "swiglu_bwd":
        assert mode == "nt" and out_block is not None and extra is not None and res is None
        cn = 1
    tm, tn, tk = cm * um, cn * un, ck * uk
    nk = k // tk
    dot = {"nn": _dot, "nt": _dot_nt, "tn": _dot_tn}[mode]
    half = n // un // 2
    blocked_out = out_block is not None or epilogue is not None

    def sl(idx, unit, count):
        return slice(None) if count == 1 else slice(idx * unit, (idx + 1) * unit)

    def body(*refs):
        a_ref, b_ref = refs[0], refs[1]
        pos = 2
        r_ref = e_ref = None
        if res is not None:
            r_ref, pos = refs[pos], pos + 1
        if extra is not None:
            e_ref, pos = refs[pos], pos + 1
        outs, acc_ref = refs[pos:-1], refs[-1]
        kk = pl.program_id(2)

        def a_blk(mi, ki):
            if mode in ("nn", "nt"):
                return a_ref[ki] if a3 else a_ref[:, sl(ki, uk, ck)]
            return a_ref[mi] if a3 else a_ref[:, sl(mi, um, cm)]

        def b_blk(ki, ni):
            if epilogue == "swiglu":
                return b_ref[ni, 0]
            if mode in ("nn", "tn"):
                return b_ref[ni] if b3 else b_ref[sl(ki, uk, ck), sl(ni, un, cn)]
            return b_ref[ki][sl(ni, un, cn), :] if b3 else b_ref[sl(ni, un, cn), sl(ki, uk, ck)]

        parts = {}
        for mi in range(cm):
            for ni in range(cn):
                part = None
                for ki in range(ck):
                    d = dot(a_blk(mi, ki).astype(BF16), b_blk(ki, ni).astype(BF16))
                    part = d if part is None else part + d
                parts[mi, ni] = part

        def finish(total):
            if epilogue == "swiglu":
                gate, up = total[0, 0], total[0, 1]
                outs[0][0, 0] = gate.astype(BF16)
                outs[0][1, 0] = up.astype(BF16)
                outs[1][0] = (gate * _sigmoid(gate) * up).astype(BF16)
                return
            if epilogue == "swiglu_bwd":
                dact = total[0, 0]
                gate, up = e_ref[0, 0].astype(F32), e_ref[1, 0].astype(F32)
                sg = _sigmoid(gate)
                outs[0][0, 0] = (dact * up * (sg * (1.0 + gate * (1.0 - sg)))).astype(BF16)
                outs[0][1, 0] = (dact * (gate * sg)).astype(BF16)
                return
            for (mi, ni), val in total.items():
                rows, cols = sl(mi, um, cm), sl(ni, un, cn)
                if res is not None:
                    val = r_ref[rows, cols] + val
                if blocked_out:
                    outs[0][ni, rows] = val.astype(out_dtype)
                else:
                    outs[0][rows, cols] = val.astype(out_dtype)

        if nk == 1:
            finish(parts)
        else:
            @pl.when(kk == 0)
            def _():
                for (mi, ni), val in parts.items():
                    acc_ref[mi * cn + ni] = val

            @pl.when(jnp.logical_and(kk > 0, kk < nk - 1))
            def _():
                for (mi, ni), val in parts.items():
                    acc_ref[mi * cn + ni] += val

            @pl.when(kk == nk - 1)
            def _():
                finish({key: acc_ref[key[0] * cn + key[1]] + val for key, val in parts.items()})

    if mode in ("nn", "nt"):
        a_spec = (pl.BlockSpec((ck, tm, uk), lambda i, j, kk: (kk, i, 0)) if a3
                  else pl.BlockSpec((tm, tk), lambda i, j, kk: (i, kk)))
    else:
        a_spec = (pl.BlockSpec((cm, tk, um), lambda i, j, kk: (i, kk, 0)) if a3
                  else pl.BlockSpec((tk, tm), lambda i, j, kk: (kk, i)))
    pair_spec = pl.BlockSpec((2, 1, tm, un), lambda i, j, kk: (0, j, i, 0))
    if epilogue == "swiglu":
        b = b.reshape(2, half, k, un)
        b_spec = pl.BlockSpec((2, 1, tk, un), lambda i, j, kk: (0, j, kk, 0))
    elif mode in ("nn", "tn"):
        b_spec = (pl.BlockSpec((cn, tk, un), lambda i, j, kk: (j, kk, 0)) if b3
                  else pl.BlockSpec((tk, tn), lambda i, j, kk: (kk, j)))
    else:
        b_spec = (pl.BlockSpec((ck, tn, uk), lambda i, j, kk: (kk, j, 0)) if b3
                  else pl.BlockSpec((tn, tk), lambda i, j, kk: (j, kk)))
    if epilogue == "swiglu":
        out_specs = [pair_spec, pl.BlockSpec((1, tm, un), lambda i, j, kk: (j, i, 0))]
        out_shape = [jax.ShapeDtypeStruct((2, half, m, un), BF16), jax.ShapeDtypeStruct((half, m, un), BF16)]
    elif epilogue == "swiglu_bwd":
        out_specs = [pair_spec]
        out_shape = [jax.ShapeDtypeStruct(extra.shape, BF16)]
    elif blocked_out:
        out_specs = [pl.BlockSpec((cn, tm, un), lambda i, j, kk: (j, i, 0))]
        out_shape = [jax.ShapeDtypeStruct((n // un, m, un), out_dtype)]
    else:
        out_specs = [pl.BlockSpec((tm, tn), lambda i, j, kk: (i, j))]
        out_shape = [jax.ShapeDtypeStruct((m, n), out_dtype)]
    in_specs, args = [a_spec, b_spec], [a, b]
    if res is not None:
        in_specs.append(pl.BlockSpec((tm, tn), lambda i, j, kk: (i, j)))
        args.append(res)
    if extra is not None:
        in_specs.append(pair_spec)
        args.append(extra)
    out = _call(body, name, (m // tm, n // tn, nk), in_specs, out_specs, out_shape,
                [pltpu.VMEM((cm * cn, um, un), F32)], ("parallel", "parallel", "arbitrary"), args, rider)
    return out if epilogue == "swiglu" else out[0]


def _rms_fwd(x, g, name):
    t, d = x.shape
    tm = _pick(t, 512, 16)

    def body(x_ref, g_ref, o_ref):
        xv = x_ref[...]
        rstd = lax.rsqrt(jnp.mean(xv * xv, axis=-1, keepdims=True) + EPS)
        o_ref[...] = (xv * rstd * g_ref[...]).astype(BF16)

    return pl.pallas_call(
        body, name=name, grid=(t // tm,),
        in_specs=[pl.BlockSpec((tm, d), lambda i: (i, 0)), pl.BlockSpec((1, d), lambda i: (0, 0))],
        out_specs=pl.BlockSpec((tm, d), lambda i: (i, 0)),
        out_shape=jax.ShapeDtypeStruct((t, d), BF16),
        compiler_params=_params("parallel"),
    )(x, g)


def _rms_bwd(x, g, dh, dres, name):
    t, d = x.shape
    tm = _pick(t, 512, 16)

    def body(x_ref, g_ref, dh_ref, dres_ref, dx_ref, dg_ref):
        xv = x_ref[...]
        rstd = lax.rsqrt(jnp.mean(xv * xv, axis=-1, keepdims=True) + EPS)
        xh = xv * rstd
        dhv = dh_ref[...]
        dyg = dhv * g_ref[...]
        c = jnp.mean(dyg * xh, axis=-1, keepdims=True)
        dx_ref[...] = dres_ref[...] + rstd * (dyg - xh * c)
        _accumulate(dg_ref, jnp.sum(dhv * xh, axis=0, keepdims=True), pl.program_id(0))

    row = pl.BlockSpec((tm, d), lambda i: (i, 0))
    vec = pl.BlockSpec((1, d), lambda i: (0, 0))
    return pl.pallas_call(
        body, name=name, grid=(t // tm,),
        in_specs=[row, vec, row, row], out_specs=[row, vec],
        out_shape=[jax.ShapeDtypeStruct((t, d), F32), jax.ShapeDtypeStruct((1, d), F32)],
        compiler_params=_params("arbitrary"),
    )(x, g, dh, dres)


def _seg_mean(v, bd):
    hi, lo = _split2(v)
    return (_dot(hi, bd) + _dot(lo, bd)) * (1.0 / ATT_DH)


def _hn_bwd_math(xv, gv, bdv, dyv, scale):
    rstd = lax.rsqrt(_seg_mean(xv * xv, bdv) + EPS)
    xh = xv * rstd
    dyn = dyv * scale
    dyg = dyn * gv
    dx = rstd * (dyg - xh * _seg_mean(dyg * xh, bdv))
    return dx, jnp.sum(dyn * xh, axis=0, keepdims=True)


def _q_hnorm(x, g_tiled, bd, scale, name):
    t, d = x.shape
    tm = _pick(t, 512, 16)

    def body(x_ref, g_ref, bd_ref, o_ref):
        xv = x_ref[...]
        rstd = lax.rsqrt(_seg_mean(xv * xv, bd_ref[...]) + EPS)
        o_ref[...] = (xv * rstd * g_ref[...] * scale).astype(BF16)

    return pl.pallas_call(
        body, name=name, grid=(t // tm,),
        in_specs=[pl.BlockSpec((tm, d), lambda i: (i, 0)), pl.BlockSpec((1, d), lambda i: (0, 0)),
                  pl.BlockSpec((d, d), lambda i: (0, 0))],
        out_specs=pl.BlockSpec((tm, d), lambda i: (i, 0)),
        out_shape=jax.ShapeDtypeStruct((t, d), BF16),
        compiler_params=_params("parallel"),
    )(x, g_tiled, bd)


def _q_dhnorm(x, g_tiled, bd, dy, scale, name):
    t, d = x.shape
    tm = _pick(t, 512, 16)

    def body(x_ref, g_ref, bd_ref, dy_ref, dx_ref, dg_ref):
        dx, part = _hn_bwd_math(x_ref[...], g_ref[...], bd_ref[...], dy_ref[...], scale)
        dx_ref[...] = dx.astype(BF16)
        _accumulate(dg_ref, part, pl.program_id(0))

    row = pl.BlockSpec((tm, d), lambda i: (i, 0))
    vec = pl.BlockSpec((1, d), lambda i: (0, 0))
    return pl.pallas_call(
        body, name=name, grid=(t // tm,),
        in_specs=[row, vec, pl.BlockSpec((d, d), lambda i: (0, 0)), row],
        out_specs=[row, vec],
        out_shape=[jax.ShapeDtypeStruct((t, d), BF16), jax.ShapeDtypeStruct((1, d), F32)],
        compiler_params=_params("arbitrary"),
    )(x, g_tiled, bd, dy)


def _kv_prep(kv, g_tiled, bd, name):
    t = kv.shape[0]
    d = D_MODEL
    tm = K_PAD
    assert t % tm == 0

    def body(k_ref, v_ref, g_ref, bd_ref, kp_ref, vp_ref):
        i = pl.program_id(0)

        @pl.when(i == 0)
        def _():
            kp_ref[...] = jnp.zeros_like(kp_ref)
            vp_ref[...] = jnp.zeros_like(vp_ref)

        @pl.when(i > 0)
        def _():
            xv = k_ref[...]
            rstd = lax.rsqrt(_seg_mean(xv * xv, bd_ref[...]) + EPS)
            kp_ref[...] = (xv * rstd * g_ref[...]).astype(BF16)
            vp_ref[...] = v_ref[...].astype(BF16)

    shp = jax.ShapeDtypeStruct((t + K_PAD, d), BF16)
    out = pl.BlockSpec((tm, d), lambda i: (i, 0))
    return pl.pallas_call(
        body, name=name, grid=(t // tm + 1,),
        in_specs=[pl.BlockSpec((tm, d), lambda i: (jnp.maximum(i - 1, 0), 0)),
                  pl.BlockSpec((tm, d), lambda i: (jnp.maximum(i - 1, 0), 1)),
                  pl.BlockSpec((1, d), lambda i: (0, 0)), pl.BlockSpec((d, d), lambda i: (0, 0))],
        out_specs=[out, out], out_shape=[shp, shp],
        compiler_params=_params("arbitrary"),
    )(kv, kv, g_tiled, bd)


def _kv_dprep(kv, g_tiled, bd, dkp, dvp, name):
    t = kv.shape[0]
    d = D_MODEL
    tm = K_PAD

    def body(k_ref, g_ref, bd_ref, dk_ref, dv_ref, o_ref, dg_ref):
        dx, part = _hn_bwd_math(k_ref[...], g_ref[...], bd_ref[...], dk_ref[...], 1.0)
        o_ref[:, :d] = dx.astype(BF16)
        o_ref[:, d:] = dv_ref[...].astype(BF16)
        _accumulate(dg_ref, part, pl.program_id(0))

    vec = pl.BlockSpec((1, d), lambda i: (0, 0))
    padded = pl.BlockSpec((tm, d), lambda i: (i + 1, 0))
    return pl.pallas_call(
        body, name=name, grid=(t // tm,),
        in_specs=[pl.BlockSpec((tm, d), lambda i: (i, 0)), vec, pl.BlockSpec((d, d), lambda i: (0, 0)),
                  padded, padded],
        out_specs=[pl.BlockSpec((tm, 2 * d), lambda i: (i, 0)), vec],
        out_shape=[jax.ShapeDtypeStruct((t, 2 * d), BF16), jax.ShapeDtypeStruct((1, d), F32)],
        compiler_params=_params("arbitrary"),
    )(kv, g_tiled, bd, dkp, dvp)


def _loss_head(y, target, name):
    t, d = y.shape
    tm = _pick(t, 512, 16)

    def body(y_ref, t_ref, dy_ref, l_ref):
        diff = y_ref[...] - t_ref[...]
        dy_ref[...] = diff * (1.0 / d)
        part = jnp.sum(jnp.sum(diff * diff, axis=-1, keepdims=True), axis=0, keepdims=True) * (0.5 / d)
        _accumulate(l_ref, part, pl.program_id(0))

    row = pl.BlockSpec((tm, d), lambda i: (i, 0))
    return pl.pallas_call(
        body, name=name, grid=(t // tm,),
        in_specs=[row, row], out_specs=[row, pl.BlockSpec((1, 1), lambda i: (0, 0))],
        out_shape=[jax.ShapeDtypeStruct((t, d), F32), jax.ShapeDtypeStruct((1, 1), F32)],
        compiler_params=_params("arbitrary"),
    )(y, target)


def _ret_consts(t):
    h = np.arange(RET_HEADS, dtype=np.float32)
    lg = np.log(np.float32(1.0) - np.float32(2.0) ** (np.float32(-5.0) - h)).astype(np.float32)
    tt = np.arange(CHUNK, dtype=np.float32)
    intra = np.exp(lg[:, None, None] * np.abs(tt[:, None] - tt[None, :])).astype(np.float32)
    q_dec = np.exp(lg[:, None] * (tt + 1.0)).astype(np.float32)
    k_dec = np.exp(lg[:, None] * (CHUNK - 1.0 - tt)).astype(np.float32)
    s_dec = [float(v) for v in np.exp(lg * np.float32(CHUNK)).astype(np.float32)]
    qd = np.broadcast_to(q_dec[:, :, None], (RET_HEADS, CHUNK, RET_DK)).copy()
    kd = np.broadcast_to(k_dec[:, :, None], (RET_HEADS, CHUNK, RET_DK)).copy()
    half = RET_DK // 2
    inv_freq = ROPE_BASE ** (-jnp.arange(half, dtype=F32) / half)
    ang = jnp.arange(t).astype(F32)[:, None] * inv_freq[None, :]
    return jnp.asarray(intra), jnp.asarray(qd), jnp.asarray(kd), s_dec, jnp.cos(ang), jnp.sin(ang)


def _rope(x, cos, sin):
    half = RET_DK // 2
    x1, x2 = x[:, :half], x[:, half:]
    return jnp.concatenate([x1 * cos - x2 * sin, x1 * sin + x2 * cos], axis=-1)


def _unrope(d, cos, sin):
    half = RET_DK // 2
    d1, d2 = d[:, :half], d[:, half:]
    return jnp.concatenate([d1 * cos + d2 * sin, d2 * cos - d1 * sin], axis=-1)


def _ret_slices(h):
    q = slice(h * RET_DK, (h + 1) * RET_DK)
    k = slice(RET_Q_COLS + h * RET_DK, RET_Q_COLS + (h + 1) * RET_DK)
    v = slice(2 * RET_Q_COLS + h * RET_DV, 2 * RET_Q_COLS + (h + 1) * RET_DV)
    g = slice(2 * RET_Q_COLS + RET_V_COLS + h * RET_DV, 2 * RET_Q_COLS + RET_V_COLS + (h + 1) * RET_DV)
    o = slice(h * RET_DV, (h + 1) * RET_DV)
    return q, k, v, g, o


def _ret_fwd(proj, gn, consts, name, rider=None):
    t, cols = proj.shape
    n = t // CHUNK
    intra, qd, kd, s_dec, cos, sin = consts
    k_scale = RET_DK ** -0.5

    def body(p_ref, cos_ref, sin_ref, intra_ref, qd_ref, kd_ref, gn_ref, y_ref, o_ref, st_ref, state):
        i = pl.program_id(0)

        @pl.when(i == 0)
        def _():
            state[...] = jnp.zeros_like(state)

        cosv, sinv = cos_ref[...], sin_ref[...]
        for h in range(RET_HEADS):
            qs, ks, vs, gs, os_ = _ret_slices(h)
            qr = _rope(p_ref[:, qs], cosv, sinv)
            kr = _rope(p_ref[:, ks], cosv, sinv) * k_scale
            vb = p_ref[:, vs].astype(BF16)
            gv = p_ref[:, gs]
            scores = _dot_nt(qr.astype(BF16), kr.astype(BF16)) * intra_ref[h]
            s_old = state[h]
            s_old_b = s_old.astype(BF16)
            st_ref[0, h] = s_old_b
            o = _dot(scores.astype(BF16), vb) + _dot((qr * qd_ref[h]).astype(BF16), s_old_b)
            state[h] = s_old * s_dec[h] + _dot_tn((kr * kd_ref[h]).astype(BF16), vb)
            rstd = lax.rsqrt(jnp.mean(o * o, axis=-1, keepdims=True) + EPS)
            on = o * rstd * gn_ref[:, os_]
            o_ref[:, os_] = o
            y_ref[:, os_] = (gv * _sigmoid(gv) * on).astype(BF16)

    full3 = lambda a: pl.BlockSpec(a.shape, lambda i: (0, 0, 0))
    return _call(
        body, name, (n,),
        [pl.BlockSpec((CHUNK, cols), lambda i: (i, 0)),
         pl.BlockSpec((CHUNK, RET_DK // 2), lambda i: (i, 0)),
         pl.BlockSpec((CHUNK, RET_DK // 2), lambda i: (i, 0)),
         full3(intra), full3(qd), full3(kd),
         pl.BlockSpec((1, RET_V_COLS), lambda i: (0, 0))],
        [pl.BlockSpec((CHUNK, RET_V_COLS), lambda i: (i, 0)),
         pl.BlockSpec((CHUNK, RET_V_COLS), lambda i: (i, 0)),
         pl.BlockSpec((1, RET_HEADS, RET_DK, RET_DV), lambda i: (i, 0, 0, 0))],
        [jax.ShapeDtypeStruct((t, RET_V_COLS), BF16),
         jax.ShapeDtypeStruct((t, RET_V_COLS), F32),
         jax.ShapeDtypeStruct((n, RET_HEADS, RET_DK, RET_DV), BF16)],
        [pltpu.VMEM((RET_HEADS, RET_DK, RET_DV), F32)], ("arbitrary",),
        (proj, cos, sin, intra, qd, kd, gn), rider)


def _ret_bwd(proj, gn, o_saved, states, dy, consts, name, rider=None):
    t, cols = proj.shape
    n = t // CHUNK
    intra, qd, kd, s_dec, cos, sin = consts
    k_scale = RET_DK ** -0.5

    def body(p_ref, cos_ref, sin_ref, intra_ref, qd_ref, kd_ref, gn_ref, o_ref, st_ref, dy_ref,
             dp_ref, dgn_ref, dstate):
        i = pl.program_id(0)

        @pl.when(i == 0)
        def _():
            dstate[...] = jnp.zeros_like(dstate)

        cosv, sinv = cos_ref[...], sin_ref[...]
        dgn_parts = []
        for h in range(RET_HEADS):
            qs, ks, vs, gs, os_ = _ret_slices(h)
            qr = _rope(p_ref[:, qs], cosv, sinv)
            kr = _rope(p_ref[:, ks], cosv, sinv) * k_scale
            qb, kb = qr.astype(BF16), kr.astype(BF16)
            vb = p_ref[:, vs].astype(BF16)
            gv = p_ref[:, gs]
            ov = o_ref[:, os_]
            dyv = dy_ref[:, os_]
            gnv = gn_ref[:, os_]
            sg = _sigmoid(gv)
            rstd = lax.rsqrt(jnp.mean(ov * ov, axis=-1, keepdims=True) + EPS)
            oh = ov * rstd
            d_on = dyv * (gv * sg)
            dg = dyv * (oh * gnv) * (sg * (1.0 + gv * (1.0 - sg)))
            dgn_parts.append(jnp.sum(d_on * oh, axis=0, keepdims=True))
            d_oh = d_on * gnv
            do = rstd * (d_oh - oh * jnp.mean(d_oh * oh, axis=-1, keepdims=True))
            dob = do.astype(BF16)
            mask = intra_ref[h]
            a_b = (_dot_nt(qb, kb) * mask).astype(BF16)
            da_b = (_dot_nt(dob, vb) * mask).astype(BF16)
            ds_new = dstate[h]
            ds_new_b = ds_new.astype(BF16)
            s_old_b = st_ref[0, h]
            qdv, kdv = qd_ref[h], kd_ref[h]
            dv = _dot_tn(a_b, dob) + _dot((kr * kdv).astype(BF16), ds_new_b)
            dqr = _dot(da_b, kb) + _dot_nt(dob, s_old_b) * qdv
            dkr = _dot_tn(da_b, qb) + _dot_nt(vb, ds_new_b) * kdv
            dstate[h] = ds_new * s_dec[h] + _dot_tn((qr * qdv).astype(BF16), dob)
            dp_ref[:, qs] = _unrope(dqr, cosv, sinv).astype(BF16)
            dp_ref[:, ks] = _unrope(dkr * k_scale, cosv, sinv).astype(BF16)
            dp_ref[:, vs] = dv.astype(BF16)
            dp_ref[:, gs] = dg.astype(BF16)
        _accumulate(dgn_ref, jnp.concatenate(dgn_parts, axis=-1), i)

    rev = lambda i: (n - 1 - i, 0)
    full3 = lambda a: pl.BlockSpec(a.shape, lambda i: (0, 0, 0))
    return _call(
        body, name, (n,),
        [pl.BlockSpec((CHUNK, cols), rev),
         pl.BlockSpec((CHUNK, RET_DK // 2), rev),
         pl.BlockSpec((CHUNK, RET_DK // 2), rev),
         full3(intra), full3(qd), full3(kd),
         pl.BlockSpec((1, RET_V_COLS), lambda i: (0, 0)),
         pl.BlockSpec((CHUNK, RET_V_COLS), rev),
         pl.BlockSpec((1, RET_HEADS, RET_DK, RET_DV), lambda i: (n - 1 - i, 0, 0, 0)),
         pl.BlockSpec((CHUNK, RET_V_COLS), rev)],
        [pl.BlockSpec((CHUNK, cols), rev),
         pl.BlockSpec((1, RET_V_COLS), lambda i: (0, 0))],
        [jax.ShapeDtypeStruct((t, cols), BF16),
         jax.ShapeDtypeStruct((1, RET_V_COLS), F32)],
        [pltpu.VMEM((RET_HEADS, RET_DK, RET_DV), F32)], ("arbitrary",),
        (proj, cos, sin, intra, qd, kd, gn, o_saved, states, dy), rider)


def _att_common(q_ref, kp_ref, vp_ref):
    blk = pl.program_id(1)
    start = pl.multiple_of(blk * Q_BLOCK, Q_BLOCK)
    kw = kp_ref[pl.ds(start, K_WINDOW), :]
    vw = vp_ref[pl.ds(start, K_WINDOW), :]
    kvalid = blk * Q_BLOCK - K_PAD + lax.broadcasted_iota(jnp.int32, (1, K_WINDOW), 1) >= 0
    lane = lax.broadcasted_iota(jnp.int32, (1, LANES), 1)
    return start, q_ref[...], kw, vw, kvalid, (lane < ATT_DH, lane >= ATT_DH)


def _att_probs(qm, kw, bias, kvalid):
    s = jnp.where(kvalid, _dot_nt(qm, kw) + bias, NEG)
    e = jnp.exp(s - jnp.max(s, axis=-1, keepdims=True))
    return e * (1.0 / jnp.sum(e, axis=-1, keepdims=True))


def _att_specs(t, tp):
    qspec = pl.BlockSpec((Q_BLOCK, LANES), lambda h, i: (i, h))
    kspec = pl.BlockSpec((tp, LANES), lambda h, i: (0, h))
    bspec = pl.BlockSpec((2, Q_BLOCK, K_WINDOW), lambda h, i: (h, 0, 0))
    return qspec, kspec, bspec


def _att_fwd(q, kp, vp, bias, name, rider=None):
    t, d = q.shape
    tp = kp.shape[0]

    def body(q_ref, kp_ref, vp_ref, bias_ref, o_ref):
        _, q2, kw, vw, kvalid, sel = _att_common(q_ref, kp_ref, vp_ref)
        outs = []
        for hh in range(2):
            p = _att_probs(jnp.where(sel[hh], q2, 0), kw, bias_ref[hh], kvalid)
            outs.append(_dot(p.astype(BF16), vw))
        o_ref[...] = jnp.where(sel[0], outs[0], outs[1]).astype(BF16)

    qspec, kspec, bspec = _att_specs(t, tp)
    return _call(body, name, (d // LANES, t // Q_BLOCK), [qspec, kspec, kspec, bspec], [qspec],
                 [jax.ShapeDtypeStruct((t, d), BF16)], [], ("parallel", "arbitrary"),
                 (q, kp, vp, bias), rider)[0]


def _att_bwd(q, kp, vp, bias, do, name, rider=None):
    t, d = q.shape
    tp = kp.shape[0]

    def body(q_ref, kp_ref, vp_ref, bias_ref, do_ref, dq_ref, dkp_ref, dvp_ref, db_ref):
        @pl.when(pl.program_id(1) == 0)
        def _():
            dkp_ref[...] = jnp.zeros_like(dkp_ref)
            dvp_ref[...] = jnp.zeros_like(dvp_ref)
            db_ref[...] = jnp.zeros_like(db_ref)

        start, q2, kw, vw, kvalid, sel = _att_common(q_ref, kp_ref, vp_ref)
        do2 = do_ref[...]
        dqs, dk, dv = [], None, None
        for hh in range(2):
            qm = jnp.where(sel[hh], q2, 0)
            dom = jnp.where(sel[hh], do2, 0)
            p = _att_probs(qm, kw, bias_ref[hh], kvalid)
            dp = _dot_nt(dom, vw)
            ds = p * (dp - jnp.sum(dp * p, axis=-1, keepdims=True))
            db_ref[hh] += ds
            dsb = ds.astype(BF16)
            dqs.append(_dot(dsb, kw))
            dk_h = _dot_tn(dsb, qm)
            dv_h = _dot_tn(p.astype(BF16), dom)
            dk = dk_h if dk is None else dk + dk_h
            dv = dv_h if dv is None else dv + dv_h
        dq_ref[...] = jnp.where(sel[0], dqs[0], dqs[1])
        dkp_ref[pl.ds(start, K_WINDOW), :] += dk
        dvp_ref[pl.ds(start, K_WINDOW), :] += dv

    qspec, kspec, bspec = _att_specs(t, tp)
    return _call(body, name, (d // LANES, t // Q_BLOCK), [qspec, kspec, kspec, bspec, qspec],
                 [qspec, kspec, kspec, bspec],
                 [jax.ShapeDtypeStruct((t, d), F32),
                  jax.ShapeDtypeStruct((tp, d), F32),
                  jax.ShapeDtypeStruct((tp, d), F32),
                  jax.ShapeDtypeStruct((ATT_HEADS, Q_BLOCK, K_WINDOW), F32)],
                 [], ("parallel", "arbitrary"), (q, kp, vp, bias, do), rider)


def _rel_bin_matrix():
    rows = REL_DELTAS * 2 * REL_BLK
    rho = lax.broadcasted_iota(jnp.int32, (rows, REL_PAD), 0)
    col = lax.broadcasted_iota(jnp.int32, (rows, REL_PAD), 1)
    assert 2 * REL_BLK == 256
    delta = rho >> 8
    c = rho & 255
    dist = K_PAD + REL_BLK * (delta - (K_WINDOW // REL_BLK - 1)) + (c - (REL_BLK - 1))
    idx = jnp.clip(dist, -REL_CLIP, REL_CLIP) + REL_CLIP
    return col == idx


def _rel_shift_matrix(r):
    c = lax.broadcasted_iota(jnp.int32, (2 * REL_BLK, REL_BLK), 0)
    s = lax.broadcasted_iota(jnp.int32, (2 * REL_BLK, REL_BLK), 1)
    return c == r - s + (REL_BLK - 1)


def _rel_expand(rel_pad, name):
    heads = rel_pad.shape[0]
    rows = REL_DELTAS * 2 * REL_BLK

    def body_bin(r_ref, o_ref):
        onehot = jnp.where(_rel_bin_matrix(), 1.0, 0.0).astype(BF16)
        hi, mid, lo = _split3(r_ref[...])
        o_ref[...] = _dot_nt(hi, onehot) + _dot_nt(mid, onehot) + _dot_nt(lo, onehot)

    by_delta = pl.pallas_call(
        body_bin, name=name + "_bin",
        out_shape=jax.ShapeDtypeStruct((heads, rows), F32),
        compiler_params=pltpu.CompilerParams(vmem_limit_bytes=VMEM_LIMIT_V7X),
    )(rel_pad)
    by_delta = by_delta.reshape(heads * REL_DELTAS, 2 * REL_BLK)

    def body_shift(t_ref, o_ref):
        r = pl.program_id(0)
        onehot = jnp.where(_rel_shift_matrix(r), 1.0, 0.0).astype(BF16)
        hi, mid, lo = _split3(t_ref[...])
        o_ref[0] = _dot(hi, onehot) + _dot(mid, onehot) + _dot(lo, onehot)

    return pl.pallas_call(
        body_shift, name=name + "_shift", grid=(REL_BLK,),
        in_specs=[pl.BlockSpec(by_delta.shape, lambda r: (0, 0))],
        out_specs=pl.BlockSpec((1, heads * REL_DELTAS, REL_BLK), lambda r: (r, 0, 0)),
        out_shape=jax.ShapeDtypeStruct((REL_BLK, heads * REL_DELTAS, REL_BLK), F32),
        compiler_params=_params("parallel"),
    )(by_delta)


def _bias_table(rel_bias, name):
    heads = rel_bias.shape[0]
    rel_pad = jnp.pad(rel_bias, ((0, 0), (0, REL_PAD - REL_TABLE)))
    tiles = _rel_expand(rel_pad, name)
    tiles = tiles.reshape(REL_BLK, heads, REL_DELTAS, REL_BLK).transpose(1, 2, 0, 3)
    na, nb = Q_BLOCK // REL_BLK, K_WINDOW // REL_BLK
    rows = [jnp.concatenate([tiles[:, a - b + nb - 1] for b in range(nb)], axis=-1) for a in range(na)]
    table = jnp.concatenate(rows, axis=-2)
    qc = np.arange(Q_BLOCK)[:, None] // CHUNK
    kc = np.arange(K_WINDOW)[None, :] // CHUNK
    band = (kc >= qc) & (kc <= qc + PAST_CHUNKS)
    return jnp.where(jnp.asarray(band)[None], table, NEG)


def _rel_reduce(db, name):
    heads = db.shape[0]
    na, nb = Q_BLOCK // REL_BLK, K_WINDOW // REL_BLK

    def body_fold(db_ref, g_ref):
        for delta in range(REL_DELTAS):
            acc = None
            for a in range(na):
                b = a - (delta - (nb - 1))
                if 0 <= b < nb:
                    tile = db_ref[0, a * REL_BLK:(a + 1) * REL_BLK, b * REL_BLK:(b + 1) * REL_BLK]
                    acc = tile if acc is None else acc + tile
            g_ref[0, delta] = acc

    folded = pl.pallas_call(
        body_fold, name=name + "_fold", grid=(heads,),
        in_specs=[pl.BlockSpec((1, Q_BLOCK, K_WINDOW), lambda h: (h, 0, 0))],
        out_specs=pl.BlockSpec((1, REL_DELTAS, REL_BLK, REL_BLK), lambda h: (h, 0, 0, 0)),
        out_shape=jax.ShapeDtypeStruct((heads, REL_DELTAS, REL_BLK, REL_BLK), F32),
        compiler_params=_params("parallel"),
    )(db)
    by_row = folded.transpose(2, 0, 1, 3).reshape(REL_BLK, heads * REL_DELTAS, REL_BLK)

    def body_diag(g_ref, d_ref):
        r = pl.program_id(0)
        onehot = jnp.where(_rel_shift_matrix(r), 1.0, 0.0).astype(BF16)
        hi, mid, lo = _split3(g_ref[0])
        _accumulate(d_ref, _dot_nt(hi, onehot) + _dot_nt(mid, onehot) + _dot_nt(lo, onehot), r)

    diag = pl.pallas_call(
        body_diag, name=name + "_diag", grid=(REL_BLK,),
        in_specs=[pl.BlockSpec((1, heads * REL_DELTAS, REL_BLK), lambda r: (r, 0, 0))],
        out_specs=pl.BlockSpec((heads * REL_DELTAS, 2 * REL_BLK), lambda r: (0, 0)),
        out_shape=jax.ShapeDtypeStruct((heads * REL_DELTAS, 2 * REL_BLK), F32),
        compiler_params=_params("arbitrary"),
    )(by_row)
    diag = diag.reshape(heads, REL_DELTAS * 2 * REL_BLK)

    def body_bin(d_ref, o_ref):
        onehot = jnp.where(_rel_bin_matrix(), 1.0, 0.0).astype(BF16)
        hi, mid, lo = _split3(d_ref[...])
        o_ref[...] = _dot(hi, onehot) + _dot(mid, onehot) + _dot(lo, onehot)

    out = pl.pallas_call(
        body_bin, name=name + "_bin",
        out_shape=jax.ShapeDtypeStruct((heads, REL_PAD), F32),
        compiler_params=pltpu.CompilerParams(vmem_limit_bytes=VMEM_LIMIT_V7X),
    )(diag)
    return out[:, :REL_TABLE]


def _sum_leading(x, name):
    n, r, c = x.shape
    tr = _pick(r, 256, 8)

    def body(x_ref, o_ref):
        acc = x_ref[0].astype(F32)
        for k in range(1, n):
            acc = acc + x_ref[k].astype(F32)
        o_ref[...] = acc

    return pl.pallas_call(
        body, name=name, grid=(r // tr,),
        in_specs=[pl.BlockSpec((n, tr, c), lambda i: (0, i, 0))],
        out_specs=pl.BlockSpec((tr, c), lambda i: (i, 0)),
        out_shape=jax.ShapeDtypeStruct((r, c), F32),
        compiler_params=_params("parallel"),
    )(x)


def _pair_add(g, recv, parity, name):
    _, r, c = g.shape
    tr = _pick(r, 256, 16)

    def body(par_ref, g_ref, r_ref, o_ref):
        o_ref[...] = (g_ref[...].astype(F32) + r_ref[...].astype(F32)).astype(BF16)

    return pl.pallas_call(
        body, name=name,
        grid_spec=pltpu.PrefetchScalarGridSpec(
            num_scalar_prefetch=1, grid=(4, r // tr),
            in_specs=[pl.BlockSpec((1, tr, c), lambda k, i, par: (2 * k + par[0], i, 0)),
                      pl.BlockSpec((1, tr, c), lambda k, i, par: (k, i, 0))],
            out_specs=pl.BlockSpec((1, tr, c), lambda k, i, par: (k, i, 0))),
        out_shape=jax.ShapeDtypeStruct((4, r, c), BF16),
        compiler_params=_params("parallel", "parallel"),
    )(parity, g, recv)


def _adamw(w, g_parts, m, v, name):
    r, c = w.shape
    n = g_parts.shape[0]
    tr = _pick(r, 256, 16 if g_parts.dtype == BF16 else 8)
    c1 = 1.0 - ADAM_B1 ** ADAM_STEP
    c2 = 1.0 - ADAM_B2 ** ADAM_STEP

    def body(w_ref, g_ref, m_ref, v_ref, go_ref, d_ref, nm_ref, nv_ref):
        gv = g_ref[0].astype(F32)
        for k in range(1, n):
            gv = gv + g_ref[k].astype(F32)
        nm = ADAM_B1 * m_ref[...] + (1.0 - ADAM_B1) * gv
        nv = ADAM_B2 * v_ref[...] + (1.0 - ADAM_B2) * (gv * gv)
        go_ref[...] = gv
        d_ref[...] = -ADAM_LR * ((nm / c1) / (jnp.sqrt(nv / c2) + ADAM_EPS) + ADAM_WD * w_ref[...])
        nm_ref[...] = nm
        nv_ref[...] = nv

    spec = pl.BlockSpec((tr, c), lambda i: (i, 0))
    shp = jax.ShapeDtypeStruct((r, c), F32)
    return pl.pallas_call(
        body, name=name, grid=(r // tr,),
        in_specs=[spec, pl.BlockSpec((n, tr, c), lambda i: (0, i, 0)), spec, spec],
        out_specs=[spec] * 4, out_shape=[shp] * 4,
        compiler_params=_params("parallel"),
    )(w, g_parts, m, v)


BIG = (("a_w_in", 1), ("a_w_o", 0), ("a_w_gu", 1), ("a_w_down", 0), ("w_kv", 1),
       ("b_w_q", 0), ("b_w_o", 0), ("b_w_gu", 1), ("b_w_down", 0))

SMALL = (("a_norm_g", D_MODEL, True), ("a_gn_g", RET_V_COLS, True), ("a_ffn_norm_g", D_MODEL, True),
         ("kv_norm_g", D_MODEL, False), ("b_norm_g", D_MODEL, False), ("b_ffn_norm_g", D_MODEL, False),
         ("k_norm_g", ATT_DH, False), ("b_q_norm_g", ATT_DH, False),
         ("b_rel_bias", ATT_HEADS * REL_TABLE, False))
SMALL_ROWS, SMALL_COLS = 16, 1024


def _pack_small(vals):
    flat = jnp.concatenate([vals[n].reshape(-1) for n, _, _ in SMALL])
    return jnp.pad(flat, (0, SMALL_ROWS * SMALL_COLS - flat.shape[0])).reshape(SMALL_ROWS, SMALL_COLS)


def _unpack_small(packed, local):
    flat, out, pos = packed.reshape(-1), {}, 0
    for n, length, sharded in SMALL:
        ln = length // N_DEV if (local and sharded) else length
        out[n] = flat[pos:pos + ln]
        pos += ln
    return out


def _gather_rider(shards, names):
    return _GatherRider([shards[n] for n in names])


def _gathered(rider, names, axis_of):
    return {n: (r.reshape(-1, r.shape[2]) if axis_of[n] == 0 else r) for n, r in zip(names, rider.results)}


def _blocks(g):
    return g if g.ndim == 3 else g.reshape(N_DEV, -1, g.shape[-1])


def _local_step(x, target, shards, w_in, s, parity):
    t = x.shape[0]
    axis_of = dict(BIG)
    consts = _ret_consts(t)
    bd = jnp.asarray(np.kron(np.eye(ATT_HEADS, dtype=np.float32),
                             np.ones((ATT_DH, ATT_DH), np.float32))).astype(BF16)
    kg_t = jnp.tile(s["k_norm_g"], (1, ATT_HEADS))
    qg_t = jnp.tile(s["b_q_norm_g"], (1, ATT_HEADS))
    q_scale = ATT_DH ** -0.5
    w = {"a_w_in": w_in}
    g, recv = {}, {}

    def gather_on(names):
        return _gather_rider(shards, names), names

    def landed(ride):
        w.update(_gathered(ride[0], ride[1], axis_of))

    def scatter_on(names):
        return _ScatterRider([_blocks(g[n]) for n in names]), names

    def reduced(ride):
        recv.update(zip(ride[1], ride[0].results))

    h1 = _rms_fwd(x, s["a_norm_g"], "a_norm")
    ride = gather_on(["a_w_o", "a_w_down"])
    proj = _mm(h1, w["a_w_in"], "nn", "a_proj", rider=ride[0])
    landed(ride)
    ride = gather_on(["a_w_gu", "w_kv"])
    y, o_ret, states = _ret_fwd(proj, s["a_gn_g"], consts, "a_ret", rider=ride[0])
    landed(ride)
    x1 = _mm(y, w["a_w_o"], "nn", "a_out", res=x)
    h2 = _rms_fwd(x1, s["a_ffn_norm_g"], "a_ffn_norm")
    ride = gather_on(["b_w_q", "b_w_o", "b_w_down"])
    gu_a, act_a = _mm(h2, w["a_w_gu"], "nn", "a_ffn_gu", epilogue="swiglu", rider=ride[0])
    landed(ride)
    x2 = _mm(act_a, w["a_w_down"], "nn", "a_ffn_down", res=x1)

    u = _rms_fwd(x2, s["kv_norm_g"], "kv_norm")
    kv = _mm(u, w["w_kv"], "nn", "kv_proj")
    kp, vp = _kv_prep(kv, kg_t, bd, "kv_prep")

    h3 = _rms_fwd(x2, s["b_norm_g"], "b_norm")
    q_raw = _mm(h3, w["b_w_q"], "nn", "b_q")
    qn = _q_hnorm(q_raw, qg_t, bd, q_scale, "q_hnorm")
    bias = _bias_table(s["b_rel_bias"].reshape(ATT_HEADS, REL_TABLE), "rel")
    ride = gather_on(["b_w_gu"])
    o_att = _att_fwd(qn, kp, vp, bias, "b_att", rider=ride[0])
    landed(ride)
    x3 = _mm(o_att, w["b_w_o"], "nn", "b_out", res=x2)
    h4 = _rms_fwd(x3, s["b_ffn_norm_g"], "b_ffn_norm")
    gu_b, act_b = _mm(h4, w["b_w_gu"], "nn", "b_ffn_gu", epilogue="swiglu")
    x4 = _mm(act_b, w["b_w_down"], "nn", "b_ffn_down", res=x3)

    dy, loss = _loss_head(x4, target, "loss")
    in_blk, kv_blk, ffn_blk = w["a_w_in"].shape[2], w["w_kv"].shape[2], w["b_w_gu"].shape[2]

    dgu = _mm(dy, w["b_w_down"], "nt", "b_ffn_dgu", out_block=ffn_blk, epilogue="swiglu_bwd", extra=gu_b)
    dgu = dgu.reshape(N_DEV, t, ffn_blk)
    g["b_w_down"] = _mm(act_b, dy, "tn", "b_ffn_gdown", out_dtype=BF16)
    ride = scatter_on(["b_w_down"])
    dh4 = _mm(dgu, w["b_w_gu"], "nt", "b_ffn_dh", rider=ride[0])
    reduced(ride)
    g["b_w_gu"] = _mm(h4, dgu, "tn", "b_ffn_ggu", out_dtype=BF16, out_block=ffn_blk)
    dx3, g["b_ffn_norm_g"] = _rms_bwd(x3, s["b_ffn_norm_g"], dh4, dy, "b_ffn_dnorm")

    do_att = _mm(dx3, w["b_w_o"], "nt", "b_dout", out_dtype=BF16)
    g["b_w_o"] = _mm(o_att, dx3, "tn", "b_gout", out_dtype=BF16)
    ride = scatter_on(["b_w_gu", "b_w_o"])
    dq, dkp, dvp, db = _att_bwd(qn, kp, vp, bias, do_att, "b_datt", rider=ride[0])
    reduced(ride)
    g["b_rel_bias"] = _rel_reduce(db, "drel").reshape(1, -1)
    dq_raw, gq = _q_dhnorm(q_raw, qg_t, bd, dq, q_scale, "q_dhnorm")
    g["b_q_norm_g"] = gq.reshape(ATT_HEADS, ATT_DH).sum(axis=0, keepdims=True)
    dh3 = _mm(dq_raw, w["b_w_q"], "nt", "b_dq")
    g["b_w_q"] = _mm(h3, dq_raw, "tn", "b_gq", out_dtype=BF16)
    dx2, g["b_norm_g"] = _rms_bwd(x2, s["b_norm_g"], dh3, dx3, "b_dnorm")

    dkv, gk = _kv_dprep(kv, kg_t, bd, dkp, dvp, "kv_dprep")
    g["k_norm_g"] = gk.reshape(ATT_HEADS, ATT_DH).sum(axis=0, keepdims=True)
    du = _mm(dkv, w["w_kv"], "nt", "kv_du")
    g["w_kv"] = _mm(u, dkv, "tn", "kv_g", out_dtype=BF16, out_block=kv_blk)
    dx2, g["kv_norm_g"] = _rms_bwd(x2, s["kv_norm_g"], du, dx2, "kv_dnorm")

    ride = scatter_on(["b_w_q", "w_kv"])
    dgu = _mm(dx2, w["a_w_down"], "nt", "a_ffn_dgu", out_block=ffn_blk, epilogue="swiglu_bwd", extra=gu_a,
              rider=ride[0])
    reduced(ride)
    dgu = dgu.reshape(N_DEV, t, ffn_blk)
    g["a_w_down"] = _mm(act_a, dx2, "tn", "a_ffn_gdown", out_dtype=BF16)
    ride = scatter_on(["a_w_down"])
    dh2 = _mm(dgu, w["a_w_gu"], "nt", "a_ffn_dh", rider=ride[0])
    reduced(ride)
    g["a_w_gu"] = _mm(h2, dgu, "tn", "a_ffn_ggu", out_dtype=BF16, out_block=ffn_blk)
    dx1, g["a_ffn_norm_g"] = _rms_bwd(x1, s["a_ffn_norm_g"], dh2, dx2, "a_ffn_dnorm")

    dy_ret = _mm(dx1, w["a_w_o"], "nt", "a_dout")
    g["a_w_o"] = _mm(y, dx1, "tn", "a_gout", out_dtype=BF16)
    ride = scatter_on(["a_w_gu"])
    dproj, g["a_gn_g"] = _ret_bwd(proj, s["a_gn_g"], o_ret, states, dy_ret, consts, "a_dret", rider=ride[0])
    reduced(ride)
    ride = scatter_on(["a_w_o"])
    g["a_w_in"] = _mm(h1, dproj, "tn", "a_gin", out_dtype=BF16, out_block=in_blk, rider=ride[0])
    reduced(ride)
    from_sibling = _exchange(_SiblingSwapRider([g["a_w_in"]]), "rs_sibling")[0]
    chip_sums = _pair_add(g["a_w_in"], from_sibling, parity, "rs_pair_add")
    last = _ChipScatterRider([chip_sums])
    dh1 = _mm(dproj, w["a_w_in"], "nt", "a_dproj", rider=last)
    recv["a_w_in"] = last.results[0]
    grad_x, g["a_norm_g"] = _rms_bwd(x, s["a_norm_g"], dh1, dx1, "a_dnorm")
    return loss, grad_x, recv, g


ARG_NAMES = ("x", "a_norm_g", "a_w_in", "a_gn_g", "a_w_o", "a_ffn_norm_g", "a_w_gu", "a_w_down",
             "kv_norm_g", "w_kv", "k_norm_g", "b_norm_g", "b_w_q", "b_q_norm_g", "b_rel_bias", "b_w_o",
             "b_ffn_norm_g", "b_w_gu", "b_w_down")
WEIGHT_NAMES = ARG_NAMES[1:]


def _big_shard(a):
    return a[0] if a.ndim == 3 else a


def kernel(x, a_norm_g, a_w_in, a_gn_g, a_w_o, a_ffn_norm_g, a_w_gu, a_w_down, kv_norm_g, w_kv, k_norm_g, b_norm_g, b_w_q, b_q_norm_g, b_rel_bias, b_w_o, b_ffn_norm_g, b_w_gu, b_w_down, loss_target, m_a_norm_g, m_a_w_in, m_a_gn_g, m_a_w_o, m_a_ffn_norm_g, m_a_w_gu, m_a_w_down, m_kv_norm_g, m_w_kv, m_k_norm_g, m_b_norm_g, m_b_w_q, m_b_q_norm_g, m_b_rel_bias, m_b_w_o, m_b_ffn_norm_g, m_b_w_gu, m_b_w_down, v_a_norm_g, v_a_w_in, v_a_gn_g, v_a_w_o, v_a_ffn_norm_g, v_a_w_gu, v_a_w_down, v_kv_norm_g, v_w_kv, v_k_norm_g, v_b_norm_g, v_b_w_q, v_b_q_norm_g, v_b_rel_bias, v_b_w_o, v_b_ffn_norm_g, v_b_w_gu, v_b_w_down):
    args = (x, a_norm_g, a_w_in, a_gn_g, a_w_o, a_ffn_norm_g, a_w_gu, a_w_down, kv_norm_g, w_kv, k_norm_g,
            b_norm_g, b_w_q, b_q_norm_g, b_rel_bias, b_w_o, b_ffn_norm_g, b_w_gu, b_w_down)
    p = dict(zip(ARG_NAMES, args))
    m_all = dict(zip(WEIGHT_NAMES, (m_a_norm_g, m_a_w_in, m_a_gn_g, m_a_w_o, m_a_ffn_norm_g, m_a_w_gu,
                                    m_a_w_down, m_kv_norm_g, m_w_kv, m_k_norm_g, m_b_norm_g, m_b_w_q,
                                    m_b_q_norm_g, m_b_rel_bias, m_b_w_o, m_b_ffn_norm_g, m_b_w_gu, m_b_w_down)))
    v_all = dict(zip(WEIGHT_NAMES, (v_a_norm_g, v_a_w_in, v_a_gn_g, v_a_w_o, v_a_ffn_norm_g, v_a_w_gu,
                                    v_a_w_down, v_kv_norm_g, v_w_kv, v_k_norm_g, v_b_norm_g, v_b_w_q,
                                    v_b_q_norm_g, v_b_rel_bias, v_b_w_o, v_b_ffn_norm_g, v_b_w_gu, v_b_w_down)))
    xi, yi, ci = _my_place()
    me = 4 * xi + 2 * yi + ci
    big_names = [n for n, _ in BIG]
    axis_of = dict(BIG)

    big_local = {n: _big_shard(p[n]) for n in big_names}
    shards = {n: a.astype(BF16) for n, a in big_local.items()}
    small_local = _pack_small({n: p[n] for n, _, _ in SMALL})
    w_in, small_all = _exchange(_GatherRider([shards["a_w_in"], small_local]), "gather_in")
    flat_g = small_all.reshape(N_DEV, -1)
    s_full, pos = {}, 0
    for n, length, sharded in SMALL:
        ln = length // N_DEV if sharded else length
        s_full[n] = flat_g[:, pos:pos + ln].reshape(1, -1) if sharded else p[n].reshape(1, -1)
        pos += ln

    parity = jnp.reshape(ci, (1,)).astype(jnp.int32)
    loss, grad_x, recv, g = _local_step(x[0], loss_target[0], shards, w_in, s_full, parity)
    loss = lax.psum(loss[0, 0], ("x", "y", "c"))

    g_small_all = _exchange(_GatherRider([_pack_small({n: g[n] for n, _, _ in SMALL})]), "gather_gsmall")[0]
    g_small = _unpack_small(_sum_leading(g_small_all, "gsmall_sum"), local=False)
    for n, length, sharded in SMALL:
        if sharded:
            g_small[n] = lax.dynamic_slice(g_small[n], (me * (length // N_DEV),), (length // N_DEV,))

    grads, deltas, new_m, new_v = {}, {}, {}, {}
    for n in big_names:
        outs = _adamw(big_local[n], recv[n], _big_shard(m_all[n]), _big_shard(v_all[n]), "adamw_" + n)
        grads[n], deltas[n], new_m[n], new_v[n] = (a.reshape(p[n].shape) for a in outs)
    pk = lambda src: _pack_small({n: src[n] for n, _, _ in SMALL})
    outs = _adamw(small_local, pk(g_small)[None], pk(m_all), pk(v_all), "adamw_small")
    g_s, d_s, nm_s, nv_s = (_unpack_small(a, local=True) for a in outs)
    for n, _, _ in SMALL:
        grads[n], deltas[n], new_m[n], new_v[n] = (a[n].reshape(p[n].shape) for a in (g_s, d_s, nm_s, nv_s))

    return (loss, grad_x[None], *[grads[n] for n in WEIGHT_NAMES], *[deltas[n] for n in WEIGHT_NAMES],
            *[new_m[n] for n in WEIGHT_NAMES], *[new_v[n] for n in WEIGHT_NAMES])
```

```python
import numpy as np
import jax
import jax.numpy as jnp
from jax import lax
from jax.experimental import pallas as pl
from jax.experimental.pallas import tpu as pltpu

F32 = jnp.float32
BF16 = jnp.bfloat16

N_DEV = 8
D_MODEL = 1024
CHUNK = 64
EPS = 1e-6
RET_HEADS, RET_DK, RET_DV = 4, 256, 512
RET_Q_COLS = RET_HEADS * RET_DK
RET_V_COLS = RET_HEADS * RET_DV
ATT_HEADS, ATT_DH = 16, 64
PAST_CHUNKS = 8
REL_CLIP = 256
REL_TABLE = 2 * REL_CLIP + 1
FFN_HIDDEN = 2816
ROPE_BASE = 10000.0
LANES = 128
Q_BLOCK = 256
K_PAD = PAST_CHUNKS * CHUNK
K_WINDOW = Q_BLOCK + K_PAD
REL_BLK = 128
REL_DELTAS = Q_BLOCK // REL_BLK + K_WINDOW // REL_BLK - 1
REL_PAD = 640
NEG = -1e30
VMEM_LIMIT_V7X = 56 * 1024 * 1024
ADAM_LR, ADAM_B1, ADAM_B2, ADAM_EPS, ADAM_WD, ADAM_STEP = 1e-3, 0.9, 0.999, 1e-8, 0.01, 10
MESH = pl.DeviceIdType.MESH
ANY = pl.BlockSpec(memory_space=pl.ANY)


def _params(*semantics):
    return pltpu.CompilerParams(dimension_semantics=semantics, vmem_limit_bytes=VMEM_LIMIT_V7X)


def _pick(dim, cap, align):
    best = None
    for t in range(align, min(dim, cap) + 1, align):
        if dim % t == 0:
            best = t
    assert best is not None, (dim, cap, align)
    return best


def _dot(a, b):
    return lax.dot_general(a, b, (((1,), (0,)), ((), ())), preferred_element_type=F32)


def _dot_nt(a, b):
    return lax.dot_general(a, b, (((1,), (1,)), ((), ())), preferred_element_type=F32)


def _dot_tn(a, b):
    return lax.dot_general(a, b, (((0,), (0,)), ((), ())), preferred_element_type=F32)


def _split2(x):
    hi = x.astype(BF16)
    lo = (x - hi.astype(F32)).astype(BF16)
    return hi, lo


def _split3(x):
    hi = x.astype(BF16)
    r = x - hi.astype(F32)
    mid = r.astype(BF16)
    lo = (r - mid.astype(F32)).astype(BF16)
    return hi, mid, lo


def _sigmoid(x):
    return 1.0 / (1.0 + jnp.exp(-x))


def _accumulate(ref, part, step):
    @pl.when(step == 0)
    def _():
        ref[...] = part

    @pl.when(step > 0)
    def _():
        ref[...] += part


def _my_place():
    return lax.axis_index("x"), lax.axis_index("y"), lax.axis_index("c")


def _flip(v, bit):
    return 1 - v if bit else v


class _GatherRider:
    def __init__(self, xs):
        self.inputs = list(xs)
        n = len(xs)
        self.out_shape = [jax.ShapeDtypeStruct((N_DEV,) + x.shape, x.dtype) for x in xs]
        self.scratch = [pltpu.SemaphoreType.DMA((7, n)), pltpu.SemaphoreType.DMA((7, n)),
                        pltpu.SemaphoreType.DMA((n,))]
        self.results = None

    def _copies(self, x_refs, out_refs, sems):
        send_sems, recv_sems, local_sems = sems
        n = len(x_refs)
        x, y, c = _my_place()
        me, sibling = (x, y, c), (x, y, 1 - c)
        chips = [(1 - x, y), (x, 1 - y), (1 - x, 1 - y)]

        def slot(a, px, py, pc):
            return out_refs[a].at[4 * px + 2 * py + pc]

        def copy(k, a, block, to, own=False):
            return pltpu.make_async_remote_copy(
                src_ref=x_refs[a] if own else slot(a, *block), dst_ref=slot(a, *block),
                send_sem=send_sems.at[k, a], recv_sem=recv_sems.at[k, a],
                device_id=to, device_id_type=MESH)

        mine = [pltpu.make_async_copy(x_refs[a], slot(a, *me), local_sems.at[a]) for a in range(n)]
        first = []
        for a in range(n):
            first.append(copy(0, a, me, sibling, own=True))
            first += [copy(1 + j, a, me, (*chip, c), own=True) for j, chip in enumerate(chips)]
        return n, c, me, sibling, chips, copy, mine, first

    def start(self, x_refs, out_refs, sems):
        _, _, _, _, _, _, mine, first = self._copies(x_refs, out_refs, sems)
        for cp in mine + first:
            cp.start()

    def finish(self, x_refs, out_refs, sems):
        n, c, me, sibling, chips, copy, mine, first = self._copies(x_refs, out_refs, sems)
        passed = []
        for j, chip in enumerate(chips):
            for a in range(n):
                copy(1 + j, a, (*chip, c), me).wait_recv()
                passed.append(copy(4 + j, a, (*chip, c), sibling))
                passed[-1].start()
        for a in range(n):
            copy(0, a, sibling, me).wait_recv()
            for j, chip in enumerate(chips):
                copy(4 + j, a, (*chip, 1 - c), me).wait_recv()
        for cp in first + passed:
            cp.wait_send()
        for cp in mine:
            cp.wait()


class _ScatterRider:
    def __init__(self, gs):
        self.inputs = list(gs)
        n = len(gs)
        self.out_shape = [jax.ShapeDtypeStruct(g.shape, g.dtype) for g in gs]
        self.scratch = [pltpu.SemaphoreType.DMA((7, n)), pltpu.SemaphoreType.DMA((7, n)),
                        pltpu.SemaphoreType.DMA((n,))]
        self.results = None

    def _copies(self, g_refs, out_refs, sems):
        send_sems, recv_sems, local_sems = sems
        x, y, c = _my_place()
        me = 4 * x + 2 * y + c
        mine, copies = [], []
        for a in range(len(g_refs)):
            mine.append(pltpu.make_async_copy(g_refs[a].at[me], out_refs[a].at[me], local_sems.at[a]))
            for k in range(1, N_DEV):
                px, py, pc = _flip(x, k & 4), _flip(y, k & 2), _flip(c, k & 1)
                copies.append(pltpu.make_async_remote_copy(
                    src_ref=g_refs[a].at[4 * px + 2 * py + pc], dst_ref=out_refs[a].at[me],
                    send_sem=send_sems.at[k - 1, a], recv_sem=recv_sems.at[k - 1, a],
                    device_id=(px, py, pc), device_id_type=MESH))
        return mine, copies

    def start(self, g_refs, out_refs, sems):
        mine, copies = self._copies(g_refs, out_refs, sems)
        for cp in mine + copies:
            cp.start()

    def finish(self, g_refs, out_refs, sems):
        mine, copies = self._copies(g_refs, out_refs, sems)
        for cp in copies + mine:
            cp.wait()


class _SiblingSwapRider:
    def __init__(self, gs):
        self.inputs = list(gs)
        n = len(gs)
        self.out_shape = [jax.ShapeDtypeStruct((4,) + g.shape[1:], g.dtype) for g in gs]
        self.scratch = [pltpu.SemaphoreType.DMA((4, n)), pltpu.SemaphoreType.DMA((4, n))]
        self.results = None

    def _copies(self, g_refs, out_refs, sems):
        send_sems, recv_sems = sems
        x, y, c = _my_place()
        return [pltpu.make_async_remote_copy(
            src_ref=g_refs[a].at[2 * k + 1 - c], dst_ref=out_refs[a].at[k],
            send_sem=send_sems.at[k, a], recv_sem=recv_sems.at[k, a],
            device_id=(x, y, 1 - c), device_id_type=MESH)
            for a in range(len(g_refs)) for k in range(4)]

    def start(self, g_refs, out_refs, sems):
        for cp in self._copies(g_refs, out_refs, sems):
            cp.start()

    def finish(self, g_refs, out_refs, sems):
        for cp in self._copies(g_refs, out_refs, sems):
            cp.wait()


class _ChipScatterRider:
    def __init__(self, ps):
        self.inputs = list(ps)
        n = len(ps)
        self.out_shape = [jax.ShapeDtypeStruct(p.shape, p.dtype) for p in ps]
        self.scratch = [pltpu.SemaphoreType.DMA((3, n)), pltpu.SemaphoreType.DMA((3, n)),
                        pltpu.SemaphoreType.DMA((n,))]
        self.results = None

    def _copies(self, p_refs, out_refs, sems):
        send_sems, recv_sems, local_sems = sems
        x, y, c = _my_place()
        my_chip = 2 * x + y
        chips = [(1 - x, y), (x, 1 - y), (1 - x, 1 - y)]
        n = len(p_refs)
        mine = [pltpu.make_async_copy(p_refs[a].at[my_chip], out_refs[a].at[my_chip], local_sems.at[a])
                for a in range(n)]
        copies = [pltpu.make_async_remote_copy(
            src_ref=p_refs[a].at[2 * cx + cy], dst_ref=out_refs[a].at[my_chip],
            send_sem=send_sems.at[j, a], recv_sem=recv_sems.at[j, a],
            device_id=(cx, cy, c), device_id_type=MESH)
            for a in range(n) for j, (cx, cy) in enumerate(chips)]
        return mine, copies

    def start(self, p_refs, out_refs, sems):
        mine, copies = self._copies(p_refs, out_refs, sems)
        for cp in mine + copies:
            cp.start()

    def finish(self, p_refs, out_refs, sems):
        mine, copies = self._copies(p_refs, out_refs, sems)
        for cp in copies + mine:
            cp.wait()


def _call(body, name, grid, in_specs, out_specs, out_shape, scratch, semantics, args, rider=None):
    in_specs, out_specs, out_shape, scratch = list(in_specs), list(out_specs), list(out_shape), list(scratch)
    if rider is None:
        return list(pl.pallas_call(
            body, name=name, grid=grid, in_specs=in_specs, out_specs=out_specs, out_shape=out_shape,
            scratch_shapes=scratch, compiler_params=_params(*semantics))(*args))
    n_in, n_out, n_scr = len(in_specs), len(out_specs), len(scratch)
    r_in, r_out = len(rider.inputs), len(rider.out_shape)

    def wrapped(*refs):
        cuts = np.cumsum([0, n_in, r_in, n_out, r_out, n_scr])
        hi, ri, ho, ro, hs = (refs[cuts[i]:cuts[i + 1]] for i in range(5))
        rs = refs[cuts[5]:]
        ids = [pl.program_id(d) for d in range(len(grid))]
        first, last = ids[0] == 0, ids[0] == grid[0] - 1
        for d in range(1, len(grid)):
            first = jnp.logical_and(first, ids[d] == 0)
            last = jnp.logical_and(last, ids[d] == grid[d] - 1)

        @pl.when(first)
        def _():
            rider.start(ri, ro, rs)

        body(*hi, *ho, *hs)

        @pl.when(last)
        def _():
            rider.finish(ri, ro, rs)

    outs = pl.pallas_call(
        wrapped, name=name, grid=grid,
        in_specs=in_specs + [ANY] * r_in, out_specs=out_specs + [ANY] * r_out,
        out_shape=out_shape + rider.out_shape, scratch_shapes=scratch + rider.scratch,
        compiler_params=_params(*(["arbitrary"] * len(grid))),
    )(*args, *rider.inputs)
    rider.results = list(outs[n_out:])
    return list(outs[:n_out])


def _exchange(rider, name):
    r_in, r_out = len(rider.inputs), len(rider.out_shape)

    def body(*refs):
        ri, ro, rs = refs[:r_in], refs[r_in:r_in + r_out], refs[r_in + r_out:]
        rider.start(ri, ro, rs)
        rider.finish(ri, ro, rs)

    return list(pl.pallas_call(
        body, name=name, in_specs=[ANY] * r_in, out_specs=[ANY] * r_out,
        out_shape=rider.out_shape, scratch_shapes=rider.scratch)(*rider.inputs))


MM_CAP_MN = 1024
MM_CAP_N = 1536
MM_CAP_K = 3072
MM_CAP_K_TOKENS = 2048


def _mm(a, b, mode, name, out_dtype=F32, res=None, out_block=None, epilogue=None, extra=None, rider=None):
    a3, b3 = a.ndim == 3, b.ndim == 3
    um = un = uk = None
    if mode in ("nn", "nt"):
        if a3:
            m, uk = a.shape[1:]
            k = a.shape[0] * uk
        else:
            m, k = a.shape
    else:
        if a3:
            k, um = a.shape[1:]
            m = a.shape[0] * um
        else:
            k, m = a.shape
    if mode in ("nn", "tn"):
        if b3:
            kb, un = b.shape[1:]
            n = b.shape[0] * un
        else:
            kb, n = b.shape
        assert kb == k, (a.shape, b.shape, mode)
    else:
        if b3:
            n, ukb = b.shape[1:]
            assert b.shape[0] * ukb == k and uk in (None, ukb), (a.shape, b.shape, mode)
            uk = ukb
        else:
            n, kb = b.shape
            assert kb == k, (a.shape, b.shape, mode)
    if out_block is not None:
        assert un in (None, out_block)
        un = out_block

    def tile(dim, unit, cap, align):
        if unit is None:
            return _pick(dim, cap, align), 1
        c = max(1, cap // unit)
        while (dim // unit) % c:
            c -= 1
        return unit, c

    um, cm = tile(m, um, MM_CAP_MN if mode != "tn" else 1408, 128 if mode == "tn" else 16)
    un, cn = tile(n, un, MM_CAP_N, 128)
    uk, ck = tile(k, uk, MM_CAP_K if mode != "tn" else MM_CAP_K_TOKENS, 128)
    if epilogue == "swiglu":
        assert mode == "nn" and b3 and res is None and out_block is None
        cn = 2
    if epilogue == "swiglu_bwd":
        assert mode == "nt" and out_block is not None and extra is not None and res is None
        cn = 1
    tm, tn, tk = cm * um, cn * un, ck * uk
    nk = k // tk
    dot = {"nn": _dot, "nt": _dot_nt, "tn": _dot_tn}[mode]
    half = n // un // 2
    blocked_out = out_block is not None or epilogue is not None

    def sl(idx, unit, count):
        return slice(None) if count == 1 else slice(idx * unit, (idx + 1) * unit)

    def body(*refs):
        a_ref, b_ref = refs[0], refs[1]
        pos = 2
        r_ref = e_ref = None
        if res is not None:
            r_ref, pos = refs[pos], pos + 1
        if extra is not None:
            e_ref, pos = refs[pos], pos + 1
        outs, acc_ref = refs[pos:-1], refs[-1]
        kk = pl.program_id(2)

        def a_blk(mi, ki):
            if mode in ("nn", "nt"):
                return a_ref[ki] if a3 else a_ref[:, sl(ki, uk, ck)]
            return a_ref[mi] if a3 else a_ref[:, sl(mi, um, cm)]

        def b_blk(ki, ni):
            if epilogue == "swiglu":
                return b_ref[ni, 0]
            if mode in ("nn", "tn"):
                return b_ref[ni] if b3 else b_ref[sl(ki, uk, ck), sl(ni, un, cn)]
            return b_ref[ki][sl(ni, un, cn), :] if b3 else b_ref[sl(ni, un, cn), sl(ki, uk, ck)]

        parts = {}
        for mi in range(cm):
            for ni in range(cn):
                part = None
                for ki in range(ck):
                    d = dot(a_blk(mi, ki).astype(BF16), b_blk(ki, ni).astype(BF16))
                    part = d if part is None else part + d
                parts[mi, ni] = part

        def finish(total):
            if epilogue == "swiglu":
                gate, up = total[0, 0], total[0, 1]
                outs[0][0, 0] = gate.astype(BF16)
                outs[0][1, 0] = up.astype(BF16)
                outs[1][0] = (gate * _sigmoid(gate) * up).astype(BF16)
                return
            if epilogue == "swiglu_bwd":
                dact = total[0, 0]
                gate, up = e_ref[0, 0].astype(F32), e_ref[1, 0].astype(F32)
                sg = _sigmoid(gate)
                outs[0][0, 0] = (dact * up * (sg * (1.0 + gate * (1.0 - sg)))).astype(BF16)
                outs[0][1, 0] = (dact * (gate * sg)).astype(BF16)
                return
            for (mi, ni), val in total.items():
                rows, cols = sl(mi, um, cm), sl(ni, un, cn)
                if res is not None:
                    val = r_ref[rows, cols] + val
                if blocked_out:
                    outs[0][ni, rows] = val.astype(out_dtype)
                else:
                    outs[0][rows, cols] = val.astype(out_dtype)

        if nk == 1:
            finish(parts)
        else:
            @pl.when(kk == 0)
            def _():
                for (mi, ni), val in parts.items():
                    acc_ref[mi * cn + ni] = val

            @pl.when(jnp.logical_and(kk > 0, kk < nk - 1))
            def _():
                for (mi, ni), val in parts.items():
                    acc_ref[mi * cn + ni] += val

            @pl.when(kk == nk - 1)
            def _():
                finish({key: acc_ref[key[0] * cn + key[1]] + val for key, val in parts.items()})

    if mode in ("nn", "nt"):
        a_spec = (pl.BlockSpec((ck, tm, uk), lambda i, j, kk: (kk, i, 0)) if a3
                  else pl.BlockSpec((tm, tk), lambda i, j, kk: (i, kk)))
    else:
        a_spec = (pl.BlockSpec((cm, tk, um), lambda i, j, kk: (i, kk, 0)) if a3
                  else pl.BlockSpec((tk, tm), lambda i, j, kk: (kk, i)))
    pair_spec = pl.BlockSpec((2, 1, tm, un), lambda i, j, kk: (0, j, i, 0))
    if epilogue == "swiglu":
        b = b.reshape(2, half, k, un)
        b_spec = pl.BlockSpec((2, 1, tk, un), lambda i, j, kk: (0, j, kk, 0))
    elif mode in ("nn", "tn"):
        b_spec = (pl.BlockSpec((cn, tk, un), lambda i, j, kk: (j, kk, 0)) if b3
                  else pl.BlockSpec((tk, tn), lambda i, j, kk: (kk, j)))
    else:
        b_spec = (pl.BlockSpec((ck, tn, uk), lambda i, j, kk: (kk, j, 0)) if b3
                  else pl.BlockSpec((tn, tk), lambda i, j, kk: (j, kk)))
    if epilogue == "swiglu":
        out_specs = [pair_spec, pl.BlockSpec((1, tm, un), lambda i, j, kk: (j, i, 0))]
        out_shape = [jax.ShapeDtypeStruct((2, half, m, un), BF16), jax.ShapeDtypeStruct((half, m, un), BF16)]
    elif epilogue == "swiglu_bwd":
        out_specs = [pair_spec]
        out_shape = [jax.ShapeDtypeStruct(extra.shape, BF16)]
    elif blocked_out:
        out_specs = [pl.BlockSpec((cn, tm, un), lambda i, j, kk: (j, i, 0))]
        out_shape = [jax.ShapeDtypeStruct((n // un, m, un), out_dtype)]
    else:
        out_specs = [pl.BlockSpec((tm, tn), lambda i, j, kk: (i, j))]
        out_shape = [jax.ShapeDtypeStruct((m, n), out_dtype)]
    in_specs, args = [a_spec, b_spec], [a, b]
    if res is not None:
        in_specs.append(pl.BlockSpec((tm, tn), lambda i, j, kk: (i, j)))
        args.append(res)
    if extra is not None:
        in_specs.append(pair_spec)
        args.append(extra)
    out = _call(body, name, (m // tm, n // tn, nk), in_specs, out_specs, out_shape,
                [pltpu.VMEM((cm * cn, um, un), F32)], ("parallel", "parallel", "arbitrary"), args, rider)
    return out if epilogue == "swiglu" else out[0]


def _rms_fwd(x, g, name):
    t, d = x.shape
    tm = _pick(t, 512, 16)

    def body(x_ref, g_ref, o_ref):
        xv = x_ref[...]
        rstd = lax.rsqrt(jnp.mean(xv * xv, axis=-1, keepdims=True) + EPS)
        o_ref[...] = (xv * rstd * g_ref[...]).astype(BF16)

    return pl.pallas_call(
        body, name=name, grid=(t // tm,),
        in_specs=[pl.BlockSpec((tm, d), lambda i: (i, 0)), pl.BlockSpec((1, d), lambda i: (0, 0))],
        out_specs=pl.BlockSpec((tm, d), lambda i: (i, 0)),
        out_shape=jax.ShapeDtypeStruct((t, d), BF16),
        compiler_params=_params("parallel"),
    )(x, g)


def _rms_bwd(x, g, dh, dres, name):
    t, d = x.shape
    tm = _pick(t, 512, 16)

    def body(x_ref, g_ref, dh_ref, dres_ref, dx_ref, dg_ref):
        xv = x_ref[...]
        rstd = lax.rsqrt(jnp.mean(xv * xv, axis=-1, keepdims=True) + EPS)
        xh = xv * rstd
        dhv = dh_ref[...]
        dyg = dhv * g_ref[...]
        c = jnp.mean(dyg * xh, axis=-1, keepdims=True)
        dx_ref[...] = dres_ref[...] + rstd * (dyg - xh * c)
        _accumulate(dg_ref, jnp.sum(dhv * xh, axis=0, keepdims=True), pl.program_id(0))

    row = pl.BlockSpec((tm, d), lambda i: (i, 0))
    vec = pl.BlockSpec((1, d), lambda i: (0, 0))
    return pl.pallas_call(
        body, name=name, grid=(t // tm,),
        in_specs=[row, vec, row, row], out_specs=[row, vec],
        out_shape=[jax.ShapeDtypeStruct((t, d), F32), jax.ShapeDtypeStruct((1, d), F32)],
        compiler_params=_params("arbitrary"),
    )(x, g, dh, dres)


def _seg_mean(v, bd):
    hi, lo = _split2(v)
    return (_dot(hi, bd) + _dot(lo, bd)) * (1.0 / ATT_DH)


def _hn_bwd_math(xv, gv, bdv, dyv, scale):
    rstd = lax.rsqrt(_seg_mean(xv * xv, bdv) + EPS)
    xh = xv * rstd
    dyn = dyv * scale
    dyg = dyn * gv
    dx = rstd * (dyg - xh * _seg_mean(dyg * xh, bdv))
    return dx, jnp.sum(dyn * xh, axis=0, keepdims=True)


def _q_hnorm(x, g_tiled, bd, scale, name):
    t, d = x.shape
    tm = _pick(t, 512, 16)

    def body(x_ref, g_ref, bd_ref, o_ref):
        xv = x_ref[...]
        rstd = lax.rsqrt(_seg_mean(xv * xv, bd_ref[...]) + EPS)
        o_ref[...] = (xv * rstd * g_ref[...] * scale).astype(BF16)

    return pl.pallas_call(
        body, name=name, grid=(t // tm,),
        in_specs=[pl.BlockSpec((tm, d), lambda i: (i, 0)), pl.BlockSpec((1, d), lambda i: (0, 0)),
                  pl.BlockSpec((d, d), lambda i: (0, 0))],
        out_specs=pl.BlockSpec((tm, d), lambda i: (i, 0)),
        out_shape=jax.ShapeDtypeStruct((t, d), BF16),
        compiler_params=_params("parallel"),
    )(x, g_tiled, bd)


def _q_dhnorm(x, g_tiled, bd, dy, scale, name):
    t, d = x.shape
    tm = _pick(t, 512, 16)

    def body(x_ref, g_ref, bd_ref, dy_ref, dx_ref, dg_ref):
        dx, part = _hn_bwd_math(x_ref[...], g_ref[...], bd_ref[...], dy_ref[...], scale)
        dx_ref[...] = dx.astype(BF16)
        _accumulate(dg_ref, part, pl.program_id(0))

    row = pl.BlockSpec((tm, d), lambda i: (i, 0))
    vec = pl.BlockSpec((1, d), lambda i: (0, 0))
    return pl.pallas_call(
        body, name=name, grid=(t // tm,),
        in_specs=[row, vec, pl.BlockSpec((d, d), lambda i: (0, 0)), row],
        out_specs=[row, vec],
        out_shape=[jax.ShapeDtypeStruct((t, d), BF16), jax.ShapeDtypeStruct((1, d), F32)],
        compiler_params=_params("arbitrary"),
    )(x, g_tiled, bd, dy)


def _kv_prep(kv, g_tiled, bd, name):
    t = kv.shape[0]
    d = D_MODEL
    tm = K_PAD
    assert t % tm == 0

    def body(k_ref, v_ref, g_ref, bd_ref, kp_ref, vp_ref):
        i = pl.program_id(0)

        @pl.when(i == 0)
        def _():
            kp_ref[...] = jnp.zeros_like(kp_ref)
            vp_ref[...] = jnp.zeros_like(vp_ref)

        @pl.when(i > 0)
        def _():
            xv = k_ref[...]
            rstd = lax.rsqrt(_seg_mean(xv * xv, bd_ref[...]) + EPS)
            kp_ref[...] = (xv * rstd * g_ref[...]).astype(BF16)
            vp_ref[...] = v_ref[...].astype(BF16)

    shp = jax.ShapeDtypeStruct((t + K_PAD, d), BF16)
    out = pl.BlockSpec((tm, d), lambda i: (i, 0))
    return pl.pallas_call(
        body, name=name, grid=(t // tm + 1,),
        in_specs=[pl.BlockSpec((tm, d), lambda i: (jnp.maximum(i - 1, 0), 0)),
                  pl.BlockSpec((tm, d), lambda i: (jnp.maximum(i - 1, 0), 1)),
                  pl.BlockSpec((1, d), lambda i: (0, 0)), pl.BlockSpec((d, d), lambda i: (0, 0))],
        out_specs=[out, out], out_shape=[shp, shp],
        compiler_params=_params("arbitrary"),
    )(kv, kv, g_tiled, bd)


def _kv_dprep(kv, g_tiled, bd, dkp, dvp, name):
    t = kv.shape[0]
    d = D_MODEL
    tm = K_PAD

    def body(k_ref, g_ref, bd_ref, dk_ref, dv_ref, o_ref, dg_ref):
        dx, part = _hn_bwd_math(k_ref[...], g_ref[...], bd_ref[...], dk_ref[...], 1.0)
        o_ref[:, :d] = dx.astype(BF16)
        o_ref[:, d:] = dv_ref[...].astype(BF16)
        _accumulate(dg_ref, part, pl.program_id(0))

    vec = pl.BlockSpec((1, d), lambda i: (0, 0))
    padded = pl.BlockSpec((tm, d), lambda i: (i + 1, 0))
    return pl.pallas_call(
        body, name=name, grid=(t // tm,),
        in_specs=[pl.BlockSpec((tm, d), lambda i: (i, 0)), vec, pl.BlockSpec((d, d), lambda i: (0, 0)),
                  padded, padded],
        out_specs=[pl.BlockSpec((tm, 2 * d), lambda i: (i, 0)), vec],
        out_shape=[jax.ShapeDtypeStruct((t, 2 * d), BF16), jax.ShapeDtypeStruct((1, d), F32)],
        compiler_params=_params("arbitrary"),
    )(kv, g_tiled, bd, dkp, dvp)


def _loss_head(y, target, name):
    t, d = y.shape
    tm = _pick(t, 512, 16)

    def body(y_ref, t_ref, dy_ref, l_ref):
        diff = y_ref[...] - t_ref[...]
        dy_ref[...] = diff * (1.0 / d)
        part = jnp.sum(jnp.sum(diff * diff, axis=-1, keepdims=True), axis=0, keepdims=True) * (0.5 / d)
        _accumulate(l_ref, part, pl.program_id(0))

    row = pl.BlockSpec((tm, d), lambda i: (i, 0))
    return pl.pallas_call(
        body, name=name, grid=(t // tm,),
        in_specs=[row, row], out_specs=[row, pl.BlockSpec((1, 1), lambda i: (0, 0))],
        out_shape=[jax.ShapeDtypeStruct((t, d), F32), jax.ShapeDtypeStruct((1, 1), F32)],
        compiler_params=_params("arbitrary"),
    )(y, target)


def _ret_consts(t):
    h = np.arange(RET_HEADS, dtype=np.float32)
    lg = np.log(np.float32(1.0) - np.float32(2.0) ** (np.float32(-5.0) - h)).astype(np.float32)
    tt = np.arange(CHUNK, dtype=np.float32)
    intra = np.exp(lg[:, None, None] * np.abs(tt[:, None] - tt[None, :])).astype(np.float32)
    q_dec = np.exp(lg[:, None] * (tt + 1.0)).astype(np.float32)
    k_dec = np.exp(lg[:, None] * (CHUNK - 1.0 - tt)).astype(np.float32)
    s_dec = [float(v) for v in np.exp(lg * np.float32(CHUNK)).astype(np.float32)]
    qd = np.broadcast_to(q_dec[:, :, None], (RET_HEADS, CHUNK, RET_DK)).copy()
    kd = np.broadcast_to(k_dec[:, :, None], (RET_HEADS, CHUNK, RET_DK)).copy()
    half = RET_DK // 2
    inv_freq = ROPE_BASE ** (-jnp.arange(half, dtype=F32) / half)
    ang = jnp.arange(t).astype(F32)[:, None] * inv_freq[None, :]
    return jnp.asarray(intra), jnp.asarray(qd), jnp.asarray(kd), s_dec, jnp.cos(ang), jnp.sin(ang)


def _rope(x, cos, sin):
    half = RET_DK // 2
    x1, x2 = x[:, :half], x[:, half:]
    return jnp.concatenate([x1 * cos - x2 * sin, x1 * sin + x2 * cos], axis=-1)


def _unrope(d, cos, sin):
    half = RET_DK // 2
    d1, d2 = d[:, :half], d[:, half:]
    return jnp.concatenate([d1 * cos + d2 * sin, d2 * cos - d1 * sin], axis=-1)


def _ret_slices(h):
    q = slice(h * RET_DK, (h + 1) * RET_DK)
    k = slice(RET_Q_COLS + h * RET_DK, RET_Q_COLS + (h + 1) * RET_DK)
    v = slice(2 * RET_Q_COLS + h * RET_DV, 2 * RET_Q_COLS + (h + 1) * RET_DV)
    g = slice(2 * RET_Q_COLS + RET_V_COLS + h * RET_DV, 2 * RET_Q_COLS + RET_V_COLS + (h + 1) * RET_DV)
    o = slice(h * RET_DV, (h + 1) * RET_DV)
    return q, k, v, g, o


def _ret_fwd(proj, gn, consts, name, rider=None):
    t, cols = proj.shape
    n = t // CHUNK
    intra, qd, kd, s_dec, cos, sin = consts
    k_scale = RET_DK ** -0.5

    def body(p_ref, cos_ref, sin_ref, intra_ref, qd_ref, kd_ref, gn_ref, y_ref, o_ref, st_ref, state):
        i = pl.program_id(0)

        @pl.when(i == 0)
        def _():
            state[...] = jnp.zeros_like(state)

        cosv, sinv = cos_ref[...], sin_ref[...]
        for h in range(RET_HEADS):
            qs, ks, vs, gs, os_ = _ret_slices(h)
            qr = _rope(p_ref[:, qs], cosv, sinv)
            kr = _rope(p_ref[:, ks], cosv, sinv) * k_scale
            vb = p_ref[:, vs].astype(BF16)
            gv = p_ref[:, gs]
            scores = _dot_nt(qr.astype(BF16), kr.astype(BF16)) * intra_ref[h]
            s_old = state[h]
            s_old_b = s_old.astype(BF16)
            st_ref[0, h] = s_old_b
            o = _dot(scores.astype(BF16), vb) + _dot((qr * qd_ref[h]).astype(BF16), s_old_b)
            state[h] = s_old * s_dec[h] + _dot_tn((kr * kd_ref[h]).astype(BF16), vb)
            rstd = lax.rsqrt(jnp.mean(o * o, axis=-1, keepdims=True) + EPS)
            on = o * rstd * gn_ref[:, os_]
            o_ref[:, os_] = o
            y_ref[:, os_] = (gv * _sigmoid(gv) * on).astype(BF16)

    full3 = lambda a: pl.BlockSpec(a.shape, lambda i: (0, 0, 0))
    return _call(
        body, name, (n,),
        [pl.BlockSpec((CHUNK, cols), lambda i: (i, 0)),
         pl.BlockSpec((CHUNK, RET_DK // 2), lambda i: (i, 0)),
         pl.BlockSpec((CHUNK, RET_DK // 2), lambda i: (i, 0)),
         full3(intra), full3(qd), full3(kd),
         pl.BlockSpec((1, RET_V_COLS), lambda i: (0, 0))],
        [pl.BlockSpec((CHUNK, RET_V_COLS), lambda i: (i, 0)),
         pl.BlockSpec((CHUNK, RET_V_COLS), lambda i: (i, 0)),
         pl.BlockSpec((1, RET_HEADS, RET_DK, RET_DV), lambda i: (i, 0, 0, 0))],
        [jax.ShapeDtypeStruct((t, RET_V_COLS), BF16),
         jax.ShapeDtypeStruct((t, RET_V_COLS), F32),
         jax.ShapeDtypeStruct((n, RET_HEADS, RET_DK, RET_DV), BF16)],
        [pltpu.VMEM((RET_HEADS, RET_DK, RET_DV), F32)], ("arbitrary",),
        (proj, cos, sin, intra, qd, kd, gn), rider)


def _ret_bwd(proj, gn, o_saved, states, dy, consts, name, rider=None):
    t, cols = proj.shape
    n = t // CHUNK
    intra, qd, kd, s_dec, cos, sin = consts
    k_scale = RET_DK ** -0.5

    def body(p_ref, cos_ref, sin_ref, intra_ref, qd_ref, kd_ref, gn_ref, o_ref, st_ref, dy_ref,
             dp_ref, dgn_ref, dstate):
        i = pl.program_id(0)

        @pl.when(i == 0)
        def _():
            dstate[...] = jnp.zeros_like(dstate)

        cosv, sinv = cos_ref[...], sin_ref[...]
        dgn_parts = []
        for h in range(RET_HEADS):
            qs, ks, vs, gs, os_ = _ret_slices(h)
            qr = _rope(p_ref[:, qs], cosv, sinv)
            kr = _rope(p_ref[:, ks], cosv, sinv) * k_scale
            qb, kb = qr.astype(BF16), kr.astype(BF16)
            vb = p_ref[:, vs].astype(BF16)
            gv = p_ref[:, gs]
            ov = o_ref[:, os_]
            dyv = dy_ref[:, os_]
            gnv = gn_ref[:, os_]
            sg = _sigmoid(gv)
            rstd = lax.rsqrt(jnp.mean(ov * ov, axis=-1, keepdims=True) + EPS)
            oh = ov * rstd
            d_on = dyv * (gv * sg)
            dg = dyv * (oh * gnv) * (sg * (1.0 + gv * (1.0 - sg)))
            dgn_parts.append(jnp.sum(d_on * oh, axis=0, keepdims=True))
            d_oh = d_on * gnv
            do = rstd * (d_oh - oh * jnp.mean(d_oh * oh, axis=-1, keepdims=True))
            dob = do.astype(BF16)
            mask = intra_ref[h]
            a_b = (_dot_nt(qb, kb) * mask).astype(BF16)
            da_b = (_dot_nt(dob, vb) * mask).astype(BF16)
            ds_new = dstate[h]
            ds_new_b = ds_new.astype(BF16)
            s_old_b = st_ref[0, h]
            qdv, kdv = qd_ref[h], kd_ref[h]
            dv = _dot_tn(a_b, dob) + _dot((kr * kdv).astype(BF16), ds_new_b)
            dqr = _dot(da_b, kb) + _dot_nt(dob, s_old_b) * qdv
            dkr = _dot_tn(da_b, qb) + _dot_nt(vb, ds_new_b) * kdv
            dstate[h] = ds_new * s_dec[h] + _dot_tn((qr * qdv).astype(BF16), dob)
            dp_ref[:, qs] = _unrope(dqr, cosv, sinv).astype(BF16)
            dp_ref[:, ks] = _unrope(dkr * k_scale, cosv, sinv).astype(BF16)
            dp_ref[:, vs] = dv.astype(BF16)
            dp_ref[:, gs] = dg.astype(BF16)
        _accumulate(dgn_ref, jnp.concatenate(dgn_parts, axis=-1), i)

    rev = lambda i: (n - 1 - i, 0)
    full3 = lambda a: pl.BlockSpec(a.shape, lambda i: (0, 0, 0))
    return _call(
        body, name, (n,),
        [pl.BlockSpec((CHUNK, cols), rev),
         pl.BlockSpec((CHUNK, RET_DK // 2), rev),
         pl.BlockSpec((CHUNK, RET_DK // 2), rev),
         full3(intra), full3(qd), full3(kd),
         pl.BlockSpec((1, RET_V_COLS), lambda i: (0, 0)),
         pl.BlockSpec((CHUNK, RET_V_COLS), rev),
         pl.BlockSpec((1, RET_HEADS, RET_DK, RET_DV), lambda i: (n - 1 - i, 0, 0, 0)),
         pl.BlockSpec((CHUNK, RET_V_COLS), rev)],
        [pl.BlockSpec((CHUNK, cols), rev),
         pl.BlockSpec((1, RET_V_COLS), lambda i: (0, 0))],
        [jax.ShapeDtypeStruct((t, cols), BF16),
         jax.ShapeDtypeStruct((1, RET_V_COLS), F32)],
        [pltpu.VMEM((RET_HEADS, RET_DK, RET_DV), F32)], ("arbitrary",),
        (proj, cos, sin, intra, qd, kd, gn, o_saved, states, dy), rider)


def _att_common(q_ref, kp_ref, vp_ref):
    blk = pl.program_id(1)
    start = pl.multiple_of(blk * Q_BLOCK, Q_BLOCK)
    kw = kp_ref[pl.ds(start, K_WINDOW), :]
    vw = vp_ref[pl.ds(start, K_WINDOW), :]
    kvalid = blk * Q_BLOCK - K_PAD + lax.broadcasted_iota(jnp.int32, (1, K_WINDOW), 1) >= 0
    lane = lax.broadcasted_iota(jnp.int32, (1, LANES), 1)
    return start, q_ref[...], kw, vw, kvalid, (lane < ATT_DH, lane >= ATT_DH)


def _att_probs(qk, bias, kvalid):
    s = jnp.where(kvalid, qk + bias, NEG)
    e = jnp.exp(s - jnp.max(s, axis=-1, keepdims=True))
    return e * (1.0 / jnp.sum(e, axis=-1, keepdims=True))


def _att_specs(t, tp):
    qspec = pl.BlockSpec((Q_BLOCK, LANES), lambda h, i: (i, h))
    kspec = pl.BlockSpec((tp, LANES), lambda h, i: (0, h))
    bspec = pl.BlockSpec((2, Q_BLOCK, K_WINDOW), lambda h, i: (h, 0, 0))
    return qspec, kspec, bspec


def _att_fwd(q, kp, vp, bias, name, rider=None):
    t, d = q.shape
    tp = kp.shape[0]

    def body(q_ref, kp_ref, vp_ref, bias_ref, o_ref):
        _, q2, kw, vw, kvalid, sel = _att_common(q_ref, kp_ref, vp_ref)
        qk = [_dot_nt(jnp.where(sel[hh], q2, 0), kw) for hh in range(2)]
        outs = []
        for hh in range(2):
            p = _att_probs(qk[hh], bias_ref[hh], kvalid)
            outs.append(_dot(p.astype(BF16), vw))
        o_ref[...] = jnp.where(sel[0], outs[0], outs[1]).astype(BF16)

    qspec, kspec, bspec = _att_specs(t, tp)
    return _call(body, name, (d // LANES, t // Q_BLOCK), [qspec, kspec, kspec, bspec], [qspec],
                 [jax.ShapeDtypeStruct((t, d), BF16)], [], ("parallel", "arbitrary"),
                 (q, kp, vp, bias), rider)[0]


def _att_bwd(q, kp, vp, bias, do, name, rider=None):
    t, d = q.shape
    tp = kp.shape[0]

    def body(q_ref, kp_ref, vp_ref, bias_ref, do_ref, dq_ref, dkp_ref, dvp_ref, db_ref):
        @pl.when(pl.program_id(1) == 0)
        def _():
            dkp_ref[...] = jnp.zeros_like(dkp_ref)
            dvp_ref[...] = jnp.zeros_like(dvp_ref)
            db_ref[...] = jnp.zeros_like(db_ref)

        start, q2, kw, vw, kvalid, sel = _att_common(q_ref, kp_ref, vp_ref)
        do2 = do_ref[...]
        qm = [jnp.where(sel[hh], q2, 0) for hh in range(2)]
        dom = [jnp.where(sel[hh], do2, 0) for hh in range(2)]
        qk = [_dot_nt(qm[hh], kw) for hh in range(2)]
        dps = [_dot_nt(dom[hh], vw) for hh in range(2)]
        dqs, dk, dv = [], None, None
        for hh in range(2):
            p = _att_probs(qk[hh], bias_ref[hh], kvalid)
            dp = dps[hh]
            ds = p * (dp - jnp.sum(dp * p, axis=-1, keepdims=True))
            db_ref[hh] += ds
            dsb = ds.astype(BF16)
            dqs.append(_dot(dsb, kw))
            dk_h = _dot_tn(dsb, qm[hh])
            dv_h = _dot_tn(p.astype(BF16), dom[hh])
            dk = dk_h if dk is None else dk + dk_h
            dv = dv_h if dv is None else dv + dv_h
        dq_ref[...] = jnp.where(sel[0], dqs[0], dqs[1])
        dkp_ref[pl.ds(start, K_WINDOW), :] += dk
        dvp_ref[pl.ds(start, K_WINDOW), :] += dv

    qspec, kspec, bspec = _att_specs(t, tp)
    return _call(body, name, (d // LANES, t // Q_BLOCK), [qspec, kspec, kspec, bspec, qspec],
                 [qspec, kspec, kspec, bspec],
                 [jax.ShapeDtypeStruct((t, d), F32),
                  jax.ShapeDtypeStruct((tp, d), F32),
                  jax.ShapeDtypeStruct((tp, d), F32),
                  jax.ShapeDtypeStruct((ATT_HEADS, Q_BLOCK, K_WINDOW), F32)],
                 [], ("parallel", "arbitrary"), (q, kp, vp, bias, do), rider)


def _rel_bin_matrix():
    rows = REL_DELTAS * 2 * REL_BLK
    rho = lax.broadcasted_iota(jnp.int32, (rows, REL_PAD), 0)
    col = lax.broadcasted_iota(jnp.int32, (rows, REL_PAD), 1)
    assert 2 * REL_BLK == 256
    delta = rho >> 8
    c = 255 - (rho & 255)
    dist = K_PAD + REL_BLK * (delta - (K_WINDOW // REL_BLK - 1)) + (c - (REL_BLK - 1))
    idx = jnp.clip(dist, -REL_CLIP, REL_CLIP) + REL_CLIP
    return col == idx


def _rel_expand(rel_pad, name):
    heads = rel_pad.shape[0]
    rows = REL_DELTAS * 2 * REL_BLK

    def body_bin(r_ref, o_ref):
        onehot = jnp.where(_rel_bin_matrix(), 1.0, 0.0).astype(BF16)
        hi, mid, lo = _split3(r_ref[...])
        o_ref[...] = _dot_nt(hi, onehot) + _dot_nt(mid, onehot) + _dot_nt(lo, onehot)

    by_delta = pl.pallas_call(
        body_bin, name=name + "_bin",
        out_shape=jax.ShapeDtypeStruct((heads, rows), F32),
        compiler_params=pltpu.CompilerParams(vmem_limit_bytes=VMEM_LIMIT_V7X),
    )(rel_pad)
    by_delta = by_delta.reshape(heads * REL_DELTAS, 2 * REL_BLK)

    def body_shift(t_ref, o_ref):
        tv = t_ref[...]
        for r in range(REL_BLK):
            o_ref[r] = pltpu.roll(tv, (r + REL_BLK) % (2 * REL_BLK), 1)[:, :REL_BLK]

    return pl.pallas_call(
        body_shift, name=name + "_shift",
        out_shape=jax.ShapeDtypeStruct((REL_BLK, heads * REL_DELTAS, REL_BLK), F32),
        compiler_params=pltpu.CompilerParams(vmem_limit_bytes=VMEM_LIMIT_V7X),
    )(by_delta)


def _bias_table(rel_bias, name):
    heads = rel_bias.shape[0]
    rel_pad = jnp.pad(rel_bias, ((0, 0), (0, REL_PAD - REL_TABLE)))
    tiles = _rel_expand(rel_pad, name)
    tiles = tiles.reshape(REL_BLK, heads, REL_DELTAS, REL_BLK).transpose(1, 2, 0, 3)
    na, nb = Q_BLOCK // REL_BLK, K_WINDOW // REL_BLK
    rows = [jnp.concatenate([tiles[:, a - b + nb - 1] for b in range(nb)], axis=-1) for a in range(na)]
    table = jnp.concatenate(rows, axis=-2)
    qc = np.arange(Q_BLOCK)[:, None] // CHUNK
    kc = np.arange(K_WINDOW)[None, :] // CHUNK
    band = (kc >= qc) & (kc <= qc + PAST_CHUNKS)
    return jnp.where(jnp.asarray(band)[None], table, NEG)


def _rel_reduce(db, name):
    heads = db.shape[0]
    na, nb = Q_BLOCK // REL_BLK, K_WINDOW // REL_BLK

    def body_fold(db_ref, g_ref):
        for delta in range(REL_DELTAS):
            acc = None
            for a in range(na):
                b = a - (delta - (nb - 1))
                if 0 <= b < nb:
                    tile = db_ref[0, a * REL_BLK:(a + 1) * REL_BLK, b * REL_BLK:(b + 1) * REL_BLK]
                    acc = tile if acc is None else acc + tile
            g_ref[0, delta] = acc

    folded = pl.pallas_call(
        body_fold, name=name + "_fold", grid=(heads,),
        in_specs=[pl.BlockSpec((1, Q_BLOCK, K_WINDOW), lambda h: (h, 0, 0))],
        out_specs=pl.BlockSpec((1, REL_DELTAS, REL_BLK, REL_BLK), lambda h: (h, 0, 0, 0)),
        out_shape=jax.ShapeDtypeStruct((heads, REL_DELTAS, REL_BLK, REL_BLK), F32),
        compiler_params=_params("parallel"),
    )(db)
    by_row = folded.transpose(2, 0, 1, 3).reshape(REL_BLK, heads * REL_DELTAS, REL_BLK)

    def body_diag(g_ref, d_ref):
        zeros = jnp.zeros((heads * REL_DELTAS, REL_BLK), F32)
        acc = None
        for r in range(REL_BLK):
            part = pltpu.roll(jnp.concatenate([g_ref[r], zeros], axis=1), REL_BLK - r, 1)
            acc = part if acc is None else acc + part
        d_ref[...] = acc

    diag = pl.pallas_call(
        body_diag, name=name + "_diag",
        out_shape=jax.ShapeDtypeStruct((heads * REL_DELTAS, 2 * REL_BLK), F32),
        compiler_params=pltpu.CompilerParams(vmem_limit_bytes=VMEM_LIMIT_V7X),
    )(by_row)
    diag = diag.reshape(heads, REL_DELTAS * 2 * REL_BLK)

    def body_bin(d_ref, o_ref):
        onehot = jnp.where(_rel_bin_matrix(), 1.0, 0.0).astype(BF16)
        hi, mid, lo = _split3(d_ref[...])
        o_ref[...] = _dot(hi, onehot) + _dot(mid, onehot) + _dot(lo, onehot)

    out = pl.pallas_call(
        body_bin, name=name + "_bin",
        out_shape=jax.ShapeDtypeStruct((heads, REL_PAD), F32),
        compiler_params=pltpu.CompilerParams(vmem_limit_bytes=VMEM_LIMIT_V7X),
    )(diag)
    return out[:, :REL_TABLE]


def _sum_leading(x, name):
    n, r, c = x.shape
    tr = _pick(r, 256, 8)

    def body(x_ref, o_ref):
        acc = x_ref[0].astype(F32)
        for k in range(1, n):
            acc = acc + x_ref[k].astype(F32)
        o_ref[...] = acc

    return pl.pallas_call(
        body, name=name, grid=(r // tr,),
        in_specs=[pl.BlockSpec((n, tr, c), lambda i: (0, i, 0))],
        out_specs=pl.BlockSpec((tr, c), lambda i: (i, 0)),
        out_shape=jax.ShapeDtypeStruct((r, c), F32),
        compiler_params=_params("parallel"),
    )(x)


def _pair_add(g, recv, parity, name):
    _, r, c = g.shape
    tr = _pick(r, 256, 16)

    def body(par_ref, g_ref, r_ref, o_ref):
        o_ref[...] = (g_ref[...].astype(F32) + r_ref[...].astype(F32)).astype(BF16)

    return pl.pallas_call(
        body, name=name,
        grid_spec=pltpu.PrefetchScalarGridSpec(
            num_scalar_prefetch=1, grid=(4, r // tr),
            in_specs=[pl.BlockSpec((1, tr, c), lambda k, i, par: (2 * k + par[0], i, 0)),
                      pl.BlockSpec((1, tr, c), lambda k, i, par: (k, i, 0))],
            out_specs=pl.BlockSpec((1, tr, c), lambda k, i, par: (k, i, 0))),
        out_shape=jax.ShapeDtypeStruct((4, r, c), BF16),
        compiler_params=_params("parallel", "parallel"),
    )(parity, g, recv)


def _adamw(w, g_parts, m, v, name):
    r, c = w.shape
    n = g_parts.shape[0]
    tr = _pick(r, 256, 16 if g_parts.dtype == BF16 else 8)
    c1 = 1.0 - ADAM_B1 ** ADAM_STEP
    c2 = 1.0 - ADAM_B2 ** ADAM_STEP

    def body(w_ref, g_ref, m_ref, v_ref, go_ref, d_ref, nm_ref, nv_ref):
        gv = g_ref[0].astype(F32)
        for k in range(1, n):
            gv = gv + g_ref[k].astype(F32)
        nm = ADAM_B1 * m_ref[...] + (1.0 - ADAM_B1) * gv
        nv = ADAM_B2 * v_ref[...] + (1.0 - ADAM_B2) * (gv * gv)
        go_ref[...] = gv
        d_ref[...] = -ADAM_LR * ((nm / c1) / (jnp.sqrt(nv / c2) + ADAM_EPS) + ADAM_WD * w_ref[...])
        nm_ref[...] = nm
        nv_ref[...] = nv

    spec = pl.BlockSpec((tr, c), lambda i: (i, 0))
    shp = jax.ShapeDtypeStruct((r, c), F32)
    return pl.pallas_call(
        body, name=name, grid=(r // tr,),
        in_specs=[spec, pl.BlockSpec((n, tr, c), lambda i: (0, i, 0)), spec, spec],
        out_specs=[spec] * 4, out_shape=[shp] * 4,
        compiler_params=_params("parallel"),
    )(w, g_parts, m, v)


BIG = (("a_w_in", 1), ("a_w_o", 0), ("a_w_gu", 1), ("a_w_down", 0), ("w_kv", 1),
       ("b_w_q", 0), ("b_w_o", 0), ("b_w_gu", 1), ("b_w_down", 0))

SMALL = (("a_norm_g", D_MODEL, True), ("a_gn_g", RET_V_COLS, True), ("a_ffn_norm_g", D_MODEL, True),
         ("kv_norm_g", D_MODEL, False), ("b_norm_g", D_MODEL, False), ("b_ffn_norm_g", D_MODEL, False),
         ("k_norm_g", ATT_DH, False), ("b_q_norm_g", ATT_DH, False),
         ("b_rel_bias", ATT_HEADS * REL_TABLE, False))
SMALL_ROWS, SMALL_COLS = 16, 1024


def _pack_small(vals):
    flat = jnp.concatenate([vals[n].reshape(-1) for n, _, _ in SMALL])
    return jnp.pad(flat, (0, SMALL_ROWS * SMALL_COLS - flat.shape[0])).reshape(SMALL_ROWS, SMALL_COLS)


def _unpack_small(packed, local):
    flat, out, pos = packed.reshape(-1), {}, 0
    for n, length, sharded in SMALL:
        ln = length // N_DEV if (local and sharded) else length
        out[n] = flat[pos:pos + ln]
        pos += ln
    return out


def _gather_rider(shards, names):
    return _GatherRider([shards[n] for n in names])


def _gathered(rider, names, axis_of):
    return {n: (r.reshape(-1, r.shape[2]) if axis_of[n] == 0 else r) for n, r in zip(names, rider.results)}


def _blocks(g):
    return g if g.ndim == 3 else g.reshape(N_DEV, -1, g.shape[-1])


def _local_step(x, target, shards, w_in, s, parity):
    t = x.shape[0]
    axis_of = dict(BIG)
    consts = _ret_consts(t)
    bd = jnp.asarray(np.kron(np.eye(ATT_HEADS, dtype=np.float32),
                             np.ones((ATT_DH, ATT_DH), np.float32))).astype(BF16)
    kg_t = jnp.tile(s["k_norm_g"], (1, ATT_HEADS))
    qg_t = jnp.tile(s["b_q_norm_g"], (1, ATT_HEADS))
    q_scale = ATT_DH ** -0.5
    w = {"a_w_in": w_in}
    g, recv = {}, {}

    def gather_on(names):
        return _gather_rider(shards, names), names

    def landed(ride):
        w.update(_gathered(ride[0], ride[1], axis_of))

    def scatter_on(names):
        return _ScatterRider([_blocks(g[n]) for n in names]), names

    def reduced(ride):
        recv.update(zip(ride[1], ride[0].results))

    h1 = _rms_fwd(x, s["a_norm_g"], "a_norm")
    ride = gather_on(["a_w_o", "a_w_down"])
    proj = _mm(h1, w["a_w_in"], "nn", "a_proj", rider=ride[0])
    landed(ride)
    ride = gather_on(["a_w_gu", "w_kv"])
    y, o_ret, states = _ret_fwd(proj, s["a_gn_g"], consts, "a_ret", rider=ride[0])
    landed(ride)
    x1 = _mm(y, w["a_w_o"], "nn", "a_out", res=x)
    h2 = _rms_fwd(x1, s["a_ffn_norm_g"], "a_ffn_norm")
    ride = gather_on(["b_w_q", "b_w_o", "b_w_down"])
    gu_a, act_a = _mm(h2, w["a_w_gu"], "nn", "a_ffn_gu", epilogue="swiglu", rider=ride[0])
    landed(ride)
    x2 = _mm(act_a, w["a_w_down"], "nn", "a_ffn_down", res=x1)

    u = _rms_fwd(x2, s["kv_norm_g"], "kv_norm")
    kv = _mm(u, w["w_kv"], "nn", "kv_proj")
    kp, vp = _kv_prep(kv, kg_t, bd, "kv_prep")

    h3 = _rms_fwd(x2, s["b_norm_g"], "b_norm")
    q_raw = _mm(h3, w["b_w_q"], "nn", "b_q")
    qn = _q_hnorm(q_raw, qg_t, bd, q_scale, "q_hnorm")
    bias = _bias_table(s["b_rel_bias"].reshape(ATT_HEADS, REL_TABLE), "rel")
    ride = gather_on(["b_w_gu"])
    o_att = _att_fwd(qn, kp, vp, bias, "b_att", rider=ride[0])
    landed(ride)
    x3 = _mm(o_att, w["b_w_o"], "nn", "b_out", res=x2)
    h4 = _rms_fwd(x3, s["b_ffn_norm_g"], "b_ffn_norm")
    gu_b, act_b = _mm(h4, w["b_w_gu"], "nn", "b_ffn_gu", epilogue="swiglu")
    x4 = _mm(act_b, w["b_w_down"], "nn", "b_ffn_down", res=x3)

    dy, loss = _loss_head(x4, target, "loss")
    in_blk, kv_blk, ffn_blk = w["a_w_in"].shape[2], w["w_kv"].shape[2], w["b_w_gu"].shape[2]

    dgu = _mm(dy, w["b_w_down"], "nt", "b_ffn_dgu", out_block=ffn_blk, epilogue="swiglu_bwd", extra=gu_b)
    dgu = dgu.reshape(N_DEV, t, ffn_blk)
    g["b_w_down"] = _mm(act_b, dy, "tn", "b_ffn_gdown", out_dtype=BF16)
    ride = scatter_on(["b_w_down"])
    dh4 = _mm(dgu, w["b_w_gu"], "nt", "b_ffn_dh", rider=ride[0])
    reduced(ride)
    g["b_w_gu"] = _mm(h4, dgu, "tn", "b_ffn_ggu", out_dtype=BF16, out_block=ffn_blk)
    dx3, g["b_ffn_norm_g"] = _rms_bwd(x3, s["b_ffn_norm_g"], dh4, dy, "b_ffn_dnorm")

    do_att = _mm(dx3, w["b_w_o"], "nt", "b_dout", out_dtype=BF16)
    g["b_w_o"] = _mm(o_att, dx3, "tn", "b_gout", out_dtype=BF16)
    ride = scatter_on(["b_w_gu", "b_w_o"])
    dq, dkp, dvp, db = _att_bwd(qn, kp, vp, bias, do_att, "b_datt", rider=ride[0])
    reduced(ride)
    g["b_rel_bias"] = _rel_reduce(db, "drel").reshape(1, -1)
    dq_raw, gq = _q_dhnorm(q_raw, qg_t, bd, dq, q_scale, "q_dhnorm")
    g["b_q_norm_g"] = gq.reshape(ATT_HEADS, ATT_DH).sum(axis=0, keepdims=True)
    dh3 = _mm(dq_raw, w["b_w_q"], "nt", "b_dq")
    g["b_w_q"] = _mm(h3, dq_raw, "tn", "b_gq", out_dtype=BF16)
    dx2, g["b_norm_g"] = _rms_bwd(x2, s["b_norm_g"], dh3, dx3, "b_dnorm")

    dkv, gk = _kv_dprep(kv, kg_t, bd, dkp, dvp, "kv_dprep")
    g["k_norm_g"] = gk.reshape(ATT_HEADS, ATT_DH).sum(axis=0, keepdims=True)
    du = _mm(dkv, w["w_kv"], "nt", "kv_du")
    g["w_kv"] = _mm(u, dkv, "tn", "kv_g", out_dtype=BF16, out_block=kv_blk)
    dx2, g["kv_norm_g"] = _rms_bwd(x2, s["kv_norm_g"], du, dx2, "kv_dnorm")

    ride = scatter_on(["b_w_q", "w_kv"])
    dgu = _mm(dx2, w["a_w_down"], "nt", "a_ffn_dgu", out_block=ffn_blk, epilogue="swiglu_bwd", extra=gu_a,
              rider=ride[0])
    reduced(ride)
    dgu = dgu.reshape(N_DEV, t, ffn_blk)
    g["a_w_down"] = _mm(act_a, dx2, "tn", "a_ffn_gdown", out_dtype=BF16)
    ride = scatter_on(["a_w_down"])
    dh2 = _mm(dgu, w["a_w_gu"], "nt", "a_ffn_dh", rider=ride[0])
    reduced(ride)
    g["a_w_gu"] = _mm(h2, dgu, "tn", "a_ffn_ggu", out_dtype=BF16, out_block=ffn_blk)
    dx1, g["a_ffn_norm_g"] = _rms_bwd(x1, s["a_ffn_norm_g"], dh2, dx2, "a_ffn_dnorm")

    dy_ret = _mm(dx1, w["a_w_o"], "nt", "a_dout")
    g["a_w_o"] = _mm(y, dx1, "tn", "a_gout", out_dtype=BF16)
    ride = scatter_on(["a_w_gu"])
    dproj, g["a_gn_g"] = _ret_bwd(proj, s["a_gn_g"], o_ret, states, dy_ret, consts, "a_dret", rider=ride[0])
    reduced(ride)
    ride = scatter_on(["a_w_o"])
    g["a_w_in"] = _mm(h1, dproj, "tn", "a_gin", out_dtype=BF16, out_block=in_blk, rider=ride[0])
    reduced(ride)
    from_sibling = _exchange(_SiblingSwapRider([g["a_w_in"]]), "rs_sibling")[0]
    chip_sums = _pair_add(g["a_w_in"], from_sibling, parity, "rs_pair_add")
    last = _ChipScatterRider([chip_sums])
    dh1 = _mm(dproj, w["a_w_in"], "nt", "a_dproj", rider=last)
    recv["a_w_in"] = last.results[0]
    grad_x, g["a_norm_g"] = _rms_bwd(x, s["a_norm_g"], dh1, dx1, "a_dnorm")
    return loss, grad_x, recv, g


ARG_NAMES = ("x", "a_norm_g", "a_w_in", "a_gn_g", "a_w_o", "a_ffn_norm_g", "a_w_gu", "a_w_down",
             "kv_norm_g", "w_kv", "k_norm_g", "b_norm_g", "b_w_q", "b_q_norm_g", "b_rel_bias", "b_w_o",
             "b_ffn_norm_g", "b_w_gu", "b_w_down")
WEIGHT_NAMES = ARG_NAMES[1:]


def _big_shard(a):
    return a[0] if a.ndim == 3 else a


def kernel(x, a_norm_g, a_w_in, a_gn_g, a_w_o, a_ffn_norm_g, a_w_gu, a_w_down, kv_norm_g, w_kv, k_norm_g, b_norm_g, b_w_q, b_q_norm_g, b_rel_bias, b_w_o, b_ffn_norm_g, b_w_gu, b_w_down, loss_target, m_a_norm_g, m_a_w_in, m_a_gn_g, m_a_w_o, m_a_ffn_norm_g, m_a_w_gu, m_a_w_down, m_kv_norm_g, m_w_kv, m_k_norm_g, m_b_norm_g, m_b_w_q, m_b_q_norm_g, m_b_rel_bias, m_b_w_o, m_b_ffn_norm_g, m_b_w_gu, m_b_w_down, v_a_norm_g, v_a_w_in, v_a_gn_g, v_a_w_o, v_a_ffn_norm_g, v_a_w_gu, v_a_w_down, v_kv_norm_g, v_w_kv, v_k_norm_g, v_b_norm_g, v_b_w_q, v_b_q_norm_g, v_b_rel_bias, v_b_w_o, v_b_ffn_norm_g, v_b_w_gu, v_b_w_down):
    args = (x, a_norm_g, a_w_in, a_gn_g, a_w_o, a_ffn_norm_g, a_w_gu, a_w_down, kv_norm_g, w_kv, k_norm_g,
            b_norm_g, b_w_q, b_q_norm_g, b_rel_bias, b_w_o, b_ffn_norm_g, b_w_gu, b_w_down)
    p = dict(zip(ARG_NAMES, args))
    m_all = dict(zip(WEIGHT_NAMES, (m_a_norm_g, m_a_w_in, m_a_gn_g, m_a_w_o, m_a_ffn_norm_g, m_a_w_gu,
                                    m_a_w_down, m_kv_norm_g, m_w_kv, m_k_norm_g, m_b_norm_g, m_b_w_q,
                                    m_b_q_norm_g, m_b_rel_bias, m_b_w_o, m_b_ffn_norm_g, m_b_w_gu, m_b_w_down)))
    v_all = dict(zip(WEIGHT_NAMES, (v_a_norm_g, v_a_w_in, v_a_gn_g, v_a_w_o, v_a_ffn_norm_g, v_a_w_gu,
                                    v_a_w_down, v_kv_norm_g, v_w_kv, v_k_norm_g, v_b_norm_g, v_b_w_q,
                                    v_b_q_norm_g, v_b_rel_bias, v_b_w_o, v_b_ffn_norm_g, v_b_w_gu, v_b_w_down)))
    xi, yi, ci = _my_place()
    me = 4 * xi + 2 * yi + ci
    big_names = [n for n, _ in BIG]
    axis_of = dict(BIG)

    big_local = {n: _big_shard(p[n]) for n in big_names}
    shards = {n: a.astype(BF16) for n, a in big_local.items()}
    small_local = _pack_small({n: p[n] for n, _, _ in SMALL})
    w_in, small_all = _exchange(_GatherRider([shards["a_w_in"], small_local]), "gather_in")
    flat_g = small_all.reshape(N_DEV, -1)
    s_full, pos = {}, 0
    for n, length, sharded in SMALL:
        ln = length // N_DEV if sharded else length
        s_full[n] = flat_g[:, pos:pos + ln].reshape(1, -1) if sharded else p[n].reshape(1, -1)
        pos += ln

    parity = jnp.reshape(ci, (1,)).astype(jnp.int32)
    loss, grad_x, recv, g = _local_step(x[0], loss_target[0], shards, w_in, s_full, parity)
    loss = lax.psum(loss[0, 0], ("x", "y", "c"))

    g_small_all = _exchange(_GatherRider([_pack_small({n: g[n] for n, _, _ in SMALL})]), "gather_gsmall")[0]
    g_small = _unpack_small(_sum_leading(g_small_all, "gsmall_sum"), local=False)
    for n, length, sharded in SMALL:
        if sharded:
            g_small[n] = lax.dynamic_slice(g_small[n], (me * (length // N_DEV),), (length // N_DEV,))

    grads, deltas, new_m, new_v = {}, {}, {}, {}
    for n in big_names:
        outs = _adamw(big_local[n], recv[n], _big_shard(m_all[n]), _big_shard(v_all[n]), "adamw_" + n)
        grads[n], deltas[n], new_m[n], new_v[n] = (a.reshape(p[n].shape) for a in outs)
    pk = lambda src: _pack_small({n: src[n] for n, _, _ in SMALL})
    outs = _adamw(small_local, pk(g_small)[None], pk(m_all), pk(v_all), "adamw_small")
    g_s, d_s, nm_s, nv_s = (_unpack_small(a, local=True) for a in outs)
    for n, _, _ in SMALL:
        grads[n], deltas[n], new_m[n], new_v[n] = (a[n].reshape(p[n].shape) for a in (g_s, d_s, nm_s, nv_s))

    return (loss, grad_x[None], *[grads[n] for n in WEIGHT_NAMES], *[deltas[n] for n in WEIGHT_NAMES],
            *[new_m[n] for n in WEIGHT_NAMES], *[new_v[n] for n in WEIGHT_NAMES])
```

```python
import numpy as np
import jax
import jax.numpy as jnp
from jax import lax
from jax.experimental import pallas as pl
from jax.experimental.pallas import tpu as pltpu

F32 = jnp.float32
BF16 = jnp.bfloat16

N_DEV = 8
D_MODEL = 1024
CHUNK = 64
EPS = 1e-6
RET_HEADS, RET_DK, RET_DV = 4, 256, 512
RET_Q_COLS = RET_HEADS * RET_DK
RET_V_COLS = RET_HEADS * RET_DV
ATT_HEADS, ATT_DH = 16, 64
PAST_CHUNKS = 8
REL_CLIP = 256
REL_TABLE = 2 * REL_CLIP + 1
FFN_HIDDEN = 2816
ROPE_BASE = 10000.0
LANES = 128
Q_BLOCK = 256
ATT_ROWS = 32
K_PAD = PAST_CHUNKS * CHUNK
K_WINDOW = Q_BLOCK + K_PAD
REL_BLK = 128
REL_DELTAS = Q_BLOCK // REL_BLK + K_WINDOW // REL_BLK - 1
REL_PAD = 640
NEG = -1e30
VMEM_LIMIT_V7X = 56 * 1024 * 1024
ADAM_LR, ADAM_B1, ADAM_B2, ADAM_EPS, ADAM_WD, ADAM_STEP = 1e-3, 0.9, 0.999, 1e-8, 0.01, 10
MESH = pl.DeviceIdType.MESH
ANY = pl.BlockSpec(memory_space=pl.ANY)


def _params(*semantics):
    return pltpu.CompilerParams(dimension_semantics=semantics, vmem_limit_bytes=VMEM_LIMIT_V7X)


def _pick(dim, cap, align):
    best = None
    for t in range(align, min(dim, cap) + 1, align):
        if dim % t == 0:
            best = t
    assert best is not None, (dim, cap, align)
    return best


def _dot(a, b):
    return lax.dot_general(a, b, (((1,), (0,)), ((), ())), preferred_element_type=F32)


def _dot_nt(a, b):
    return lax.dot_general(a, b, (((1,), (1,)), ((), ())), preferred_element_type=F32)


def _dot_tn(a, b):
    return lax.dot_general(a, b, (((0,), (0,)), ((), ())), preferred_element_type=F32)


def _split2(x):
    hi = x.astype(BF16)
    lo = (x - hi.astype(F32)).astype(BF16)
    return hi, lo


def _split3(x):
    hi = x.astype(BF16)
    r = x - hi.astype(F32)
    mid = r.astype(BF16)
    lo = (r - mid.astype(F32)).astype(BF16)
    return hi, mid, lo


def _sigmoid(x):
    return 1.0 / (1.0 + jnp.exp(-x))


def _accumulate(ref, part, step):
    @pl.when(step == 0)
    def _():
        ref[...] = part

    @pl.when(step > 0)
    def _():
        ref[...] += part


def _my_place():
    return lax.axis_index("x"), lax.axis_index("y"), lax.axis_index("c")


def _flip(v, bit):
    return 1 - v if bit else v


class _GatherRider:
    def __init__(self, xs):
        self.inputs = list(xs)
        n = len(xs)
        self.out_shape = [jax.ShapeDtypeStruct((N_DEV,) + x.shape, x.dtype) for x in xs]
        self.scratch = [pltpu.SemaphoreType.DMA((7, n)), pltpu.SemaphoreType.DMA((7, n)),
                        pltpu.SemaphoreType.DMA((n,))]
        self.results = None

    def _copies(self, x_refs, out_refs, sems):
        send_sems, recv_sems, local_sems = sems
        n = len(x_refs)
        x, y, c = _my_place()
        me, sibling = (x, y, c), (x, y, 1 - c)
        chips = [(1 - x, y), (x, 1 - y), (1 - x, 1 - y)]

        def slot(a, px, py, pc):
            return out_refs[a].at[4 * px + 2 * py + pc]

        def copy(k, a, block, to, own=False):
            return pltpu.make_async_remote_copy(
                src_ref=x_refs[a] if own else slot(a, *block), dst_ref=slot(a, *block),
                send_sem=send_sems.at[k, a], recv_sem=recv_sems.at[k, a],
                device_id=to, device_id_type=MESH)

        mine = [pltpu.make_async_copy(x_refs[a], slot(a, *me), local_sems.at[a]) for a in range(n)]
        first = []
        for a in range(n):
            first.append(copy(0, a, me, sibling, own=True))
            first += [copy(1 + j, a, me, (*chip, c), own=True) for j, chip in enumerate(chips)]
        return n, c, me, sibling, chips, copy, mine, first

    def start(self, x_refs, out_refs, sems):
        _, _, _, _, _, _, mine, first = self._copies(x_refs, out_refs, sems)
        for cp in mine + first:
            cp.start()

    def finish(self, x_refs, out_refs, sems):
        n, c, me, sibling, chips, copy, mine, first = self._copies(x_refs, out_refs, sems)
        passed = []
        for j, chip in enumerate(chips):
            for a in range(n):
                copy(1 + j, a, (*chip, c), me).wait_recv()
                passed.append(copy(4 + j, a, (*chip, c), sibling))
                passed[-1].start()
        for a in range(n):
            copy(0, a, sibling, me).wait_recv()
            for j, chip in enumerate(chips):
                copy(4 + j, a, (*chip, 1 - c), me).wait_recv()
        for cp in first + passed:
            cp.wait_send()
        for cp in mine:
            cp.wait()


class _ScatterRider:
    def __init__(self, gs):
        self.inputs = list(gs)
        n = len(gs)
        self.out_shape = [jax.ShapeDtypeStruct(g.shape, g.dtype) for g in gs]
        self.scratch = [pltpu.SemaphoreType.DMA((7, n)), pltpu.SemaphoreType.DMA((7, n)),
                        pltpu.SemaphoreType.DMA((n,))]
        self.results = None

    def _copies(self, g_refs, out_refs, sems):
        send_sems, recv_sems, local_sems = sems
        x, y, c = _my_place()
        me = 4 * x + 2 * y + c
        mine, copies = [], []
        for a in range(len(g_refs)):
            mine.append(pltpu.make_async_copy(g_refs[a].at[me], out_refs[a].at[me], local_sems.at[a]))
            for k in range(1, N_DEV):
                px, py, pc = _flip(x, k & 4), _flip(y, k & 2), _flip(c, k & 1)
                copies.append(pltpu.make_async_remote_copy(
                    src_ref=g_refs[a].at[4 * px + 2 * py + pc], dst_ref=out_refs[a].at[me],
                    send_sem=send_sems.at[k - 1, a], recv_sem=recv_sems.at[k - 1, a],
                    device_id=(px, py, pc), device_id_type=MESH))
        return mine, copies

    def start(self, g_refs, out_refs, sems):
        mine, copies = self._copies(g_refs, out_refs, sems)
        for cp in mine + copies:
            cp.start()

    def finish(self, g_refs, out_refs, sems):
        mine, copies = self._copies(g_refs, out_refs, sems)
        for cp in copies + mine:
            cp.wait()


class _SiblingSwapRider:
    def __init__(self, gs):
        self.inputs = list(gs)
        n = len(gs)
        self.out_shape = [jax.ShapeDtypeStruct((4,) + g.shape[1:], g.dtype) for g in gs]
        self.scratch = [pltpu.SemaphoreType.DMA((4, n)), pltpu.SemaphoreType.DMA((4, n))]
        self.results = None

    def _copies(self, g_refs, out_refs, sems):
        send_sems, recv_sems = sems
        x, y, c = _my_place()
        return [pltpu.make_async_remote_copy(
            src_ref=g_refs[a].at[2 * k + 1 - c], dst_ref=out_refs[a].at[k],
            send_sem=send_sems.at[k, a], recv_sem=recv_sems.at[k, a],
            device_id=(x, y, 1 - c), device_id_type=MESH)
            for a in range(len(g_refs)) for k in range(4)]

    def start(self, g_refs, out_refs, sems):
        for cp in self._copies(g_refs, out_refs, sems):
            cp.start()

    def finish(self, g_refs, out_refs, sems):
        for cp in self._copies(g_refs, out_refs, sems):
            cp.wait()


class _ChipScatterRider:
    def __init__(self, ps):
        self.inputs = list(ps)
        n = len(ps)
        self.out_shape = [jax.ShapeDtypeStruct(p.shape, p.dtype) for p in ps]
        self.scratch = [pltpu.SemaphoreType.DMA((3, n)), pltpu.SemaphoreType.DMA((3, n)),
                        pltpu.SemaphoreType.DMA((n,))]
        self.results = None

    def _copies(self, p_refs, out_refs, sems):
        send_sems, recv_sems, local_sems = sems
        x, y, c = _my_place()
        my_chip = 2 * x + y
        chips = [(1 - x, y), (x, 1 - y), (1 - x, 1 - y)]
        n = len(p_refs)
        mine = [pltpu.make_async_copy(p_refs[a].at[my_chip], out_refs[a].at[my_chip], local_sems.at[a])
                for a in range(n)]
        copies = [pltpu.make_async_remote_copy(
            src_ref=p_refs[a].at[2 * cx + cy], dst_ref=out_refs[a].at[my_chip],
            send_sem=send_sems.at[j, a], recv_sem=recv_sems.at[j, a],
            device_id=(cx, cy, c), device_id_type=MESH)
            for a in range(n) for j, (cx, cy) in enumerate(chips)]
        return mine, copies

    def start(self, p_refs, out_refs, sems):
        mine, copies = self._copies(p_refs, out_refs, sems)
        for cp in mine + copies:
            cp.start()

    def finish(self, p_refs, out_refs, sems):
        mine, copies = self._copies(p_refs, out_refs, sems)
        for cp in copies + mine:
            cp.wait()


def _call(body, name, grid, in_specs, out_specs, out_shape, scratch, semantics, args, rider=None):
    in_specs, out_specs, out_shape, scratch = list(in_specs), list(out_specs), list(out_shape), list(scratch)
    if rider is None:
        return list(pl.pallas_call(
            body, name=name, grid=grid, in_specs=in_specs, out_specs=out_specs, out_shape=out_shape,
            scratch_shapes=scratch, compiler_params=_params(*semantics))(*args))
    n_in, n_out, n_scr = len(in_specs), len(out_specs), len(scratch)
    r_in, r_out = len(rider.inputs), len(rider.out_shape)

    def wrapped(*refs):
        cuts = np.cumsum([0, n_in, r_in, n_out, r_out, n_scr])
        hi, ri, ho, ro, hs = (refs[cuts[i]:cuts[i + 1]] for i in range(5))
        rs = refs[cuts[5]:]
        ids = [pl.program_id(d) for d in range(len(grid))]
        first, last = ids[0] == 0, ids[0] == grid[0] - 1
        for d in range(1, len(grid)):
            first = jnp.logical_and(first, ids[d] == 0)
            last = jnp.logical_and(last, ids[d] == grid[d] - 1)

        @pl.when(first)
        def _():
            rider.start(ri, ro, rs)

        body(*hi, *ho, *hs)

        @pl.when(last)
        def _():
            rider.finish(ri, ro, rs)

    outs = pl.pallas_call(
        wrapped, name=name, grid=grid,
        in_specs=in_specs + [ANY] * r_in, out_specs=out_specs + [ANY] * r_out,
        out_shape=out_shape + rider.out_shape, scratch_shapes=scratch + rider.scratch,
        compiler_params=_params(*(["arbitrary"] * len(grid))),
    )(*args, *rider.inputs)
    rider.results = list(outs[n_out:])
    return list(outs[:n_out])


def _exchange(rider, name):
    r_in, r_out = len(rider.inputs), len(rider.out_shape)

    def body(*refs):
        ri, ro, rs = refs[:r_in], refs[r_in:r_in + r_out], refs[r_in + r_out:]
        rider.start(ri, ro, rs)
        rider.finish(ri, ro, rs)

    return list(pl.pallas_call(
        body, name=name, in_specs=[ANY] * r_in, out_specs=[ANY] * r_out,
        out_shape=rider.out_shape, scratch_shapes=rider.scratch)(*rider.inputs))


MM_CAP_MN = 1024
MM_CAP_N = 1536
MM_CAP_K = 3072
MM_CAP_K_TOKENS = 2048


def _mm(a, b, mode, name, out_dtype=F32, res=None, out_block=None, epilogue=None, extra=None, rider=None):
    a3, b3 = a.ndim == 3, b.ndim == 3
    um = un = uk = None
    if mode in ("nn", "nt"):
        if a3:
            m, uk = a.shape[1:]
            k = a.shape[0] * uk
        else:
            m, k = a.shape
    else:
        if a3:
            k, um = a.shape[1:]
            m = a.shape[0] * um
        else:
            k, m = a.shape
    if mode in ("nn", "tn"):
        if b3:
            kb, un = b.shape[1:]
            n = b.shape[0] * un
        else:
            kb, n = b.shape
        assert kb == k, (a.shape, b.shape, mode)
    else:
        if b3:
            n, ukb = b.shape[1:]
            assert b.shape[0] * ukb == k and uk in (None, ukb), (a.shape, b.shape, mode)
            uk = ukb
        else:
            n, kb = b.shape
            assert kb == k, (a.shape, b.shape, mode)
    if out_block is not None:
        assert un in (None, out_block)
        un = out_block

    def tile(dim, unit, cap, align):
        if unit is None:
            return _pick(dim, cap, align), 1
        c = max(1, cap // unit)
        while (dim // unit) % c:
            c -= 1
        return unit, c

    um, cm = tile(m, um, MM_CAP_MN if mode != "tn" else 1408, 128 if mode == "tn" else 16)
    un, cn = tile(n, un, MM_CAP_N, 128)
    uk, ck = tile(k, uk, MM_CAP_K if mode != "tn" else MM_CAP_K_TOKENS, 128)
    if epilogue == "swiglu":
        assert mode == "nn" and b3 and res is None and out_block is None
        cn = 2
    if epilogue == "swiglu_bwd":
        assert mode == "nt" and out_block is not None and extra is not None and res is None
        cn = 1
    tm, tn, tk = cm * um, cn * un, ck * uk
    nk = k // tk
    dot = {"nn": _dot, "nt": _dot_nt, "tn": _dot_tn}[mode]
    half = n // un // 2
    blocked_out = out_block is not None or epilogue is not None

    def sl(idx, unit, count):
        return slice(None) if count == 1 else slice(idx * unit, (idx + 1) * unit)

    def body(*refs):
        a_ref, b_ref = refs[0], refs[1]
        pos = 2
        r_ref = e_ref = None
        if res is not None:
            r_ref, pos = refs[pos], pos + 1
        if extra is not None:
            e_ref, pos = refs[pos], pos + 1
        outs, acc_ref = refs[pos:-1], refs[-1]
        kk = pl.program_id(2)

        def a_blk(mi, ki):
            if mode in ("nn", "nt"):
                return a_ref[ki] if a3 else a_ref[:, sl(ki, uk, ck)]
            return a_ref[mi] if a3 else a_ref[:, sl(mi, um, cm)]

        def b_blk(ki, ni):
            if epilogue == "swiglu":
                return b_ref[ni, 0]
            if mode in ("nn", "tn"):
                return b_ref[ni] if b3 else b_ref[sl(ki, uk, ck), sl(ni, un, cn)]
            return b_ref[ki][sl(ni, un, cn), :] if b3 else b_ref[sl(ni, un, cn), sl(ki, uk, ck)]

        parts = {}
        for mi in range(cm):
            for ni in range(cn):
                part = None
                for ki in range(ck):
                    d = dot(a_blk(mi, ki).astype(BF16), b_blk(ki, ni).astype(BF16))
                    part = d if part is None else part + d
                parts[mi, ni] = part

        def finish(total):
            if epilogue == "swiglu":
                gate, up = total[0, 0], total[0, 1]
                outs[0][0, 0] = gate.astype(BF16)
                outs[0][1, 0] = up.astype(BF16)
                outs[1][0] = (gate * _sigmoid(gate) * up).astype(BF16)
                return
            if epilogue == "swiglu_bwd":
                dact = total[0, 0]
                gate, up = e_ref[0, 0].astype(F32), e_ref[1, 0].astype(F32)
                sg = _sigmoid(gate)
                outs[0][0, 0] = (dact * up * (sg * (1.0 + gate * (1.0 - sg)))).astype(BF16)
                outs[0][1, 0] = (dact * (gate * sg)).astype(BF16)
                return
            for (mi, ni), val in total.items():
                rows, cols = sl(mi, um, cm), sl(ni, un, cn)
                if res is not None:
                    val = r_ref[rows, cols] + val
                if blocked_out:
                    outs[0][ni, rows] = val.astype(out_dtype)
                else:
                    outs[0][rows, cols] = val.astype(out_dtype)

        if nk == 1:
            finish(parts)
        else:
            @pl.when(kk == 0)
            def _():
                for (mi, ni), val in parts.items():
                    acc_ref[mi * cn + ni] = val

            @pl.when(jnp.logical_and(kk > 0, kk < nk - 1))
            def _():
                for (mi, ni), val in parts.items():
                    acc_ref[mi * cn + ni] += val

            @pl.when(kk == nk - 1)
            def _():
                finish({key: acc_ref[key[0] * cn + key[1]] + val for key, val in parts.items()})

    if mode in ("nn", "nt"):
        a_spec = (pl.BlockSpec((ck, tm, uk), lambda i, j, kk: (kk, i, 0)) if a3
                  else pl.BlockSpec((tm, tk), lambda i, j, kk: (i, kk)))
    else:
        a_spec = (pl.BlockSpec((cm, tk, um), lambda i, j, kk: (i, kk, 0)) if a3
                  else pl.BlockSpec((tk, tm), lambda i, j, kk: (kk, i)))
    pair_spec = pl.BlockSpec((2, 1, tm, un), lambda i, j, kk: (0, j, i, 0))
    if epilogue == "swiglu":
        b = b.reshape(2, half, k, un)
        b_spec = pl.BlockSpec((2, 1, tk, un), lambda i, j, kk: (0, j, kk, 0))
    elif mode in ("nn", "tn"):
        b_spec = (pl.BlockSpec((cn, tk, un), lambda i, j, kk: (j, kk, 0)) if b3
                  else pl.BlockSpec((tk, tn), lambda i, j, kk: (kk, j)))
    else:
        b_spec = (pl.BlockSpec((ck, tn, uk), lambda i, j, kk: (kk, j, 0)) if b3
                  else pl.BlockSpec((tn, tk), lambda i, j, kk: (j, kk)))
    if epilogue == "swiglu":
        out_specs = [pair_spec, pl.BlockSpec((1, tm, un), lambda i, j, kk: (j, i, 0))]
        out_shape = [jax.ShapeDtypeStruct((2, half, m, un), BF16), jax.ShapeDtypeStruct((half, m, un), BF16)]
    elif epilogue == "swiglu_bwd":
        out_specs = [pair_spec]
        out_shape = [jax.ShapeDtypeStruct(extra.shape, BF16)]
    elif blocked_out:
        out_specs = [pl.BlockSpec((cn, tm, un), lambda i, j, kk: (j, i, 0))]
        out_shape = [jax.ShapeDtypeStruct((n // un, m, un), out_dtype)]
    else:
        out_specs = [pl.BlockSpec((tm, tn), lambda i, j, kk: (i, j))]
        out_shape = [jax.ShapeDtypeStruct((m, n), out_dtype)]
    in_specs, args = [a_spec, b_spec], [a, b]
    if res is not None:
        in_specs.append(pl.BlockSpec((tm, tn), lambda i, j, kk: (i, j)))
        args.append(res)
    if extra is not None:
        in_specs.append(pair_spec)
        args.append(extra)
    out = _call(body, name, (m // tm, n // tn, nk), in_specs, out_specs, out_shape,
                [pltpu.VMEM((cm * cn, um, un), F32)], ("parallel", "parallel", "arbitrary"), args, rider)
    return out if epilogue == "swiglu" else out[0]


def _rms_fwd(x, g, name):
    t, d = x.shape
    tm = _pick(t, 512, 16)

    def body(x_ref, g_ref, o_ref):
        xv = x_ref[...]
        rstd = lax.rsqrt(jnp.mean(xv * xv, axis=-1, keepdims=True) + EPS)
        o_ref[...] = (xv * rstd * g_ref[...]).astype(BF16)

    return pl.pallas_call(
        body, name=name, grid=(t // tm,),
        in_specs=[pl.BlockSpec((tm, d), lambda i: (i, 0)), pl.BlockSpec((1, d), lambda i: (0, 0))],
        out_specs=pl.BlockSpec((tm, d), lambda i: (i, 0)),
        out_shape=jax.ShapeDtypeStruct((t, d), BF16),
        compiler_params=_params("parallel"),
    )(x, g)


def _rms_bwd(x, g, dh, dres, name):
    t, d = x.shape
    tm = _pick(t, 512, 16)

    def body(x_ref, g_ref, dh_ref, dres_ref, dx_ref, dg_ref):
        xv = x_ref[...]
        rstd = lax.rsqrt(jnp.mean(xv * xv, axis=-1, keepdims=True) + EPS)
        xh = xv * rstd
        dhv = dh_ref[...]
        dyg = dhv * g_ref[...]
        c = jnp.mean(dyg * xh, axis=-1, keepdims=True)
        dx_ref[...] = dres_ref[...] + rstd * (dyg - xh * c)
        _accumulate(dg_ref, jnp.sum(dhv * xh, axis=0, keepdims=True), pl.program_id(0))

    row = pl.BlockSpec((tm, d), lambda i: (i, 0))
    vec = pl.BlockSpec((1, d), lambda i: (0, 0))
    return pl.pallas_call(
        body, name=name, grid=(t // tm,),
        in_specs=[row, vec, row, row], out_specs=[row, vec],
        out_shape=[jax.ShapeDtypeStruct((t, d), F32), jax.ShapeDtypeStruct((1, d), F32)],
        compiler_params=_params("arbitrary"),
    )(x, g, dh, dres)


def _seg_mean(v, bd):
    hi, lo = _split2(v)
    return (_dot(hi, bd) + _dot(lo, bd)) * (1.0 / ATT_DH)


def _hn_bwd_math(xv, gv, bdv, dyv, scale):
    rstd = lax.rsqrt(_seg_mean(xv * xv, bdv) + EPS)
    xh = xv * rstd
    dyn = dyv * scale
    dyg = dyn * gv
    dx = rstd * (dyg - xh * _seg_mean(dyg * xh, bdv))
    return dx, jnp.sum(dyn * xh, axis=0, keepdims=True)


def _q_hnorm(x, g_tiled, bd, scale, name):
    t, d = x.shape
    tm = _pick(t, 512, 16)

    def body(x_ref, g_ref, bd_ref, o_ref):
        xv = x_ref[...]
        rstd = lax.rsqrt(_seg_mean(xv * xv, bd_ref[...]) + EPS)
        o_ref[...] = (xv * rstd * g_ref[...] * scale).astype(BF16)

    return pl.pallas_call(
        body, name=name, grid=(t // tm,),
        in_specs=[pl.BlockSpec((tm, d), lambda i: (i, 0)), pl.BlockSpec((1, d), lambda i: (0, 0)),
                  pl.BlockSpec((d, d), lambda i: (0, 0))],
        out_specs=pl.BlockSpec((tm, d), lambda i: (i, 0)),
        out_shape=jax.ShapeDtypeStruct((t, d), BF16),
        compiler_params=_params("parallel"),
    )(x, g_tiled, bd)


def _q_dhnorm(x, g_tiled, bd, dy, scale, name):
    t, d = x.shape
    tm = _pick(t, 512, 16)

    def body(x_ref, g_ref, bd_ref, dy_ref, dx_ref, dg_ref):
        dx, part = _hn_bwd_math(x_ref[...], g_ref[...], bd_ref[...], dy_ref[...], scale)
        dx_ref[...] = dx.astype(BF16)
        _accumulate(dg_ref, part, pl.program_id(0))

    row = pl.BlockSpec((tm, d), lambda i: (i, 0))
    vec = pl.BlockSpec((1, d), lambda i: (0, 0))
    return pl.pallas_call(
        body, name=name, grid=(t // tm,),
        in_specs=[row, vec, pl.BlockSpec((d, d), lambda i: (0, 0)), row],
        out_specs=[row, vec],
        out_shape=[jax.ShapeDtypeStruct((t, d), BF16), jax.ShapeDtypeStruct((1, d), F32)],
        compiler_params=_params("arbitrary"),
    )(x, g_tiled, bd, dy)


def _kv_prep(kv, g_tiled, bd, name):
    t = kv.shape[0]
    d = D_MODEL
    tm = K_PAD
    assert t % tm == 0

    def body(k_ref, v_ref, g_ref, bd_ref, kp_ref, vp_ref):
        i = pl.program_id(0)

        @pl.when(i == 0)
        def _():
            kp_ref[...] = jnp.zeros_like(kp_ref)
            vp_ref[...] = jnp.zeros_like(vp_ref)

        @pl.when(i > 0)
        def _():
            xv = k_ref[...]
            rstd = lax.rsqrt(_seg_mean(xv * xv, bd_ref[...]) + EPS)
            kp_ref[...] = (xv * rstd * g_ref[...]).astype(BF16)
            vp_ref[...] = v_ref[...].astype(BF16)

    shp = jax.ShapeDtypeStruct((t + K_PAD, d), BF16)
    out = pl.BlockSpec((tm, d), lambda i: (i, 0))
    return pl.pallas_call(
        body, name=name, grid=(t // tm + 1,),
        in_specs=[pl.BlockSpec((tm, d), lambda i: (jnp.maximum(i - 1, 0), 0)),
                  pl.BlockSpec((tm, d), lambda i: (jnp.maximum(i - 1, 0), 1)),
                  pl.BlockSpec((1, d), lambda i: (0, 0)), pl.BlockSpec((d, d), lambda i: (0, 0))],
        out_specs=[out, out], out_shape=[shp, shp],
        compiler_params=_params("arbitrary"),
    )(kv, kv, g_tiled, bd)


def _kv_dprep(kv, g_tiled, bd, dkp, dvp, name):
    t = kv.shape[0]
    d = D_MODEL
    tm = K_PAD

    def body(k_ref, g_ref, bd_ref, dk_ref, dv_ref, o_ref, dg_ref):
        dx, part = _hn_bwd_math(k_ref[...], g_ref[...], bd_ref[...], dk_ref[...], 1.0)
        o_ref[:, :d] = dx.astype(BF16)
        o_ref[:, d:] = dv_ref[...].astype(BF16)
        _accumulate(dg_ref, part, pl.program_id(0))

    vec = pl.BlockSpec((1, d), lambda i: (0, 0))
    padded = pl.BlockSpec((tm, d), lambda i: (i + 1, 0))
    return pl.pallas_call(
        body, name=name, grid=(t // tm,),
        in_specs=[pl.BlockSpec((tm, d), lambda i: (i, 0)), vec, pl.BlockSpec((d, d), lambda i: (0, 0)),
                  padded, padded],
        out_specs=[pl.BlockSpec((tm, 2 * d), lambda i: (i, 0)), vec],
        out_shape=[jax.ShapeDtypeStruct((t, 2 * d), BF16), jax.ShapeDtypeStruct((1, d), F32)],
        compiler_params=_params("arbitrary"),
    )(kv, g_tiled, bd, dkp, dvp)


def _loss_head(y, target, name):
    t, d = y.shape
    tm = _pick(t, 512, 16)

    def body(y_ref, t_ref, dy_ref, l_ref):
        diff = y_ref[...] - t_ref[...]
        dy_ref[...] = diff * (1.0 / d)
        part = jnp.sum(jnp.sum(diff * diff, axis=-1, keepdims=True), axis=0, keepdims=True) * (0.5 / d)
        _accumulate(l_ref, part, pl.program_id(0))

    row = pl.BlockSpec((tm, d), lambda i: (i, 0))
    return pl.pallas_call(
        body, name=name, grid=(t // tm,),
        in_specs=[row, row], out_specs=[row, pl.BlockSpec((1, 1), lambda i: (0, 0))],
        out_shape=[jax.ShapeDtypeStruct((t, d), F32), jax.ShapeDtypeStruct((1, 1), F32)],
        compiler_params=_params("arbitrary"),
    )(y, target)


def _ret_consts(t):
    h = np.arange(RET_HEADS, dtype=np.float32)
    lg = np.log(np.float32(1.0) - np.float32(2.0) ** (np.float32(-5.0) - h)).astype(np.float32)
    tt = np.arange(CHUNK, dtype=np.float32)
    intra = np.exp(lg[:, None, None] * np.abs(tt[:, None] - tt[None, :])).astype(np.float32)
    q_dec = np.exp(lg[:, None] * (tt + 1.0)).astype(np.float32)
    k_dec = np.exp(lg[:, None] * (CHUNK - 1.0 - tt)).astype(np.float32)
    s_dec = [float(v) for v in np.exp(lg * np.float32(CHUNK)).astype(np.float32)]
    qd = np.broadcast_to(q_dec[:, :, None], (RET_HEADS, CHUNK, RET_DK)).copy()
    kd = np.broadcast_to(k_dec[:, :, None], (RET_HEADS, CHUNK, RET_DK)).copy()
    half = RET_DK // 2
    inv_freq = ROPE_BASE ** (-jnp.arange(half, dtype=F32) / half)
    ang = jnp.arange(t).astype(F32)[:, None] * inv_freq[None, :]
    return jnp.asarray(intra), jnp.asarray(qd), jnp.asarray(kd), s_dec, jnp.cos(ang), jnp.sin(ang)


def _rope(x, cos, sin):
    half = RET_DK // 2
    x1, x2 = x[:, :half], x[:, half:]
    return jnp.concatenate([x1 * cos - x2 * sin, x1 * sin + x2 * cos], axis=-1)


def _unrope(d, cos, sin):
    half = RET_DK // 2
    d1, d2 = d[:, :half], d[:, half:]
    return jnp.concatenate([d1 * cos + d2 * sin, d2 * cos - d1 * sin], axis=-1)


def _ret_slices(h):
    q = slice(h * RET_DK, (h + 1) * RET_DK)
    k = slice(RET_Q_COLS + h * RET_DK, RET_Q_COLS + (h + 1) * RET_DK)
    v = slice(2 * RET_Q_COLS + h * RET_DV, 2 * RET_Q_COLS + (h + 1) * RET_DV)
    g = slice(2 * RET_Q_COLS + RET_V_COLS + h * RET_DV, 2 * RET_Q_COLS + RET_V_COLS + (h + 1) * RET_DV)
    o = slice(h * RET_DV, (h + 1) * RET_DV)
    return q, k, v, g, o


def _ret_fwd(proj, gn, consts, name, rider=None):
    t, cols = proj.shape
    n = t // CHUNK
    intra, qd, kd, s_dec, cos, sin = consts
    k_scale = RET_DK ** -0.5

    def body(p_ref, cos_ref, sin_ref, intra_ref, qd_ref, kd_ref, gn_ref, y_ref, o_ref, st_ref, state):
        i = pl.program_id(0)

        @pl.when(i == 0)
        def _():
            state[...] = jnp.zeros_like(state)

        cosv, sinv = cos_ref[...], sin_ref[...]
        for h in range(RET_HEADS):
            qs, ks, vs, gs, os_ = _ret_slices(h)
            qr = _rope(p_ref[:, qs], cosv, sinv)
            kr = _rope(p_ref[:, ks], cosv, sinv) * k_scale
            vb = p_ref[:, vs].astype(BF16)
            gv = p_ref[:, gs]
            scores = _dot_nt(qr.astype(BF16), kr.astype(BF16)) * intra_ref[h]
            s_old = state[h]
            s_old_b = s_old.astype(BF16)
            st_ref[0, h] = s_old_b
            o = _dot(scores.astype(BF16), vb) + _dot((qr * qd_ref[h]).astype(BF16), s_old_b)
            state[h] = s_old * s_dec[h] + _dot_tn((kr * kd_ref[h]).astype(BF16), vb)
            rstd = lax.rsqrt(jnp.mean(o * o, axis=-1, keepdims=True) + EPS)
            on = o * rstd * gn_ref[:, os_]
            o_ref[:, os_] = o
            y_ref[:, os_] = (gv * _sigmoid(gv) * on).astype(BF16)

    full3 = lambda a: pl.BlockSpec(a.shape, lambda i: (0, 0, 0))
    return _call(
        body, name, (n,),
        [pl.BlockSpec((CHUNK, cols), lambda i: (i, 0)),
         pl.BlockSpec((CHUNK, RET_DK // 2), lambda i: (i, 0)),
         pl.BlockSpec((CHUNK, RET_DK // 2), lambda i: (i, 0)),
         full3(intra), full3(qd), full3(kd),
         pl.BlockSpec((1, RET_V_COLS), lambda i: (0, 0))],
        [pl.BlockSpec((CHUNK, RET_V_COLS), lambda i: (i, 0)),
         pl.BlockSpec((CHUNK, RET_V_COLS), lambda i: (i, 0)),
         pl.BlockSpec((1, RET_HEADS, RET_DK, RET_DV), lambda i: (i, 0, 0, 0))],
        [jax.ShapeDtypeStruct((t, RET_V_COLS), BF16),
         jax.ShapeDtypeStruct((t, RET_V_COLS), F32),
         jax.ShapeDtypeStruct((n, RET_HEADS, RET_DK, RET_DV), BF16)],
        [pltpu.VMEM((RET_HEADS, RET_DK, RET_DV), F32)], ("arbitrary",),
        (proj, cos, sin, intra, qd, kd, gn), rider)


def _ret_bwd(proj, gn, o_saved, states, dy, consts, name, rider=None):
    t, cols = proj.shape
    n = t // CHUNK
    intra, qd, kd, s_dec, cos, sin = consts
    k_scale = RET_DK ** -0.5

    def body(p_ref, cos_ref, sin_ref, intra_ref, qd_ref, kd_ref, gn_ref, o_ref, st_ref, dy_ref,
             dp_ref, dgn_ref, dstate):
        i = pl.program_id(0)

        @pl.when(i == 0)
        def _():
            dstate[...] = jnp.zeros_like(dstate)

        cosv, sinv = cos_ref[...], sin_ref[...]
        dgn_parts = []
        for h in range(RET_HEADS):
            qs, ks, vs, gs, os_ = _ret_slices(h)
            qr = _rope(p_ref[:, qs], cosv, sinv)
            kr = _rope(p_ref[:, ks], cosv, sinv) * k_scale
            qb, kb = qr.astype(BF16), kr.astype(BF16)
            vb = p_ref[:, vs].astype(BF16)
            gv = p_ref[:, gs]
            ov = o_ref[:, os_]
            dyv = dy_ref[:, os_]
            gnv = gn_ref[:, os_]
            sg = _sigmoid(gv)
            rstd = lax.rsqrt(jnp.mean(ov * ov, axis=-1, keepdims=True) + EPS)
            oh = ov * rstd
            d_on = dyv * (gv * sg)
            dg = dyv * (oh * gnv) * (sg * (1.0 + gv * (1.0 - sg)))
            dgn_parts.append(jnp.sum(d_on * oh, axis=0, keepdims=True))
            d_oh = d_on * gnv
            do = rstd * (d_oh - oh * jnp.mean(d_oh * oh, axis=-1, keepdims=True))
            dob = do.astype(BF16)
            mask = intra_ref[h]
            a_b = (_dot_nt(qb, kb) * mask).astype(BF16)
            da_b = (_dot_nt(dob, vb) * mask).astype(BF16)
            ds_new = dstate[h]
            ds_new_b = ds_new.astype(BF16)
            s_old_b = st_ref[0, h]
            qdv, kdv = qd_ref[h], kd_ref[h]
            dv = _dot_tn(a_b, dob) + _dot((kr * kdv).astype(BF16), ds_new_b)
            dqr = _dot(da_b, kb) + _dot_nt(dob, s_old_b) * qdv
            dkr = _dot_tn(da_b, qb) + _dot_nt(vb, ds_new_b) * kdv
            dstate[h] = ds_new * s_dec[h] + _dot_tn((qr * qdv).astype(BF16), dob)
            dp_ref[:, qs] = _unrope(dqr, cosv, sinv).astype(BF16)
            dp_ref[:, ks] = _unrope(dkr * k_scale, cosv, sinv).astype(BF16)
            dp_ref[:, vs] = dv.astype(BF16)
            dp_ref[:, gs] = dg.astype(BF16)
        _accumulate(dgn_ref, jnp.concatenate(dgn_parts, axis=-1), i)

    rev = lambda i: (n - 1 - i, 0)
    full3 = lambda a: pl.BlockSpec(a.shape, lambda i: (0, 0, 0))
    return _call(
        body, name, (n,),
        [pl.BlockSpec((CHUNK, cols), rev),
         pl.BlockSpec((CHUNK, RET_DK // 2), rev),
         pl.BlockSpec((CHUNK, RET_DK // 2), rev),
         full3(intra), full3(qd), full3(kd),
         pl.BlockSpec((1, RET_V_COLS), lambda i: (0, 0)),
         pl.BlockSpec((CHUNK, RET_V_COLS), rev),
         pl.BlockSpec((1, RET_HEADS, RET_DK, RET_DV), lambda i: (n - 1 - i, 0, 0, 0)),
         pl.BlockSpec((CHUNK, RET_V_COLS), rev)],
        [pl.BlockSpec((CHUNK, cols), rev),
         pl.BlockSpec((1, RET_V_COLS), lambda i: (0, 0))],
        [jax.ShapeDtypeStruct((t, cols), BF16),
         jax.ShapeDtypeStruct((1, RET_V_COLS), F32)],
        [pltpu.VMEM((RET_HEADS, RET_DK, RET_DV), F32)], ("arbitrary",),
        (proj, cos, sin, intra, qd, kd, gn, o_saved, states, dy), rider)


def _att_common(q_ref, kp_ref, vp_ref):
    blk = pl.program_id(1)
    start = pl.multiple_of(blk * Q_BLOCK, Q_BLOCK)
    kw = kp_ref[pl.ds(start, K_WINDOW), :]
    vw = vp_ref[pl.ds(start, K_WINDOW), :]
    kvalid = blk * Q_BLOCK - K_PAD + lax.broadcasted_iota(jnp.int32, (1, K_WINDOW), 1) >= 0
    lane = lax.broadcasted_iota(jnp.int32, (1, LANES), 1)
    return start, q_ref[...], kw, vw, kvalid, (lane < ATT_DH, lane >= ATT_DH)


def _row_groups():
    return [slice(r * ATT_ROWS, (r + 1) * ATT_ROWS) for r in range(Q_BLOCK // ATT_ROWS)]


def _lane_copies(x):
    return jnp.tile(x, (1, K_WINDOW // LANES))


def _att_specs(t, tp):
    qspec = pl.BlockSpec((Q_BLOCK, LANES), lambda h, i: (i, h))
    kspec = pl.BlockSpec((tp, LANES), lambda h, i: (0, h))
    bspec = pl.BlockSpec((2, Q_BLOCK, K_WINDOW), lambda h, i: (h, 0, 0))
    return qspec, kspec, bspec


def _att_fwd(q, kp, vp, bias, name, rider=None):
    t, d = q.shape
    tp = kp.shape[0]

    def body(q_ref, kp_ref, vp_ref, bias_ref, o_ref, lse_ref, s_scr, p_scr, lse_scr):
        _, q2, kw, vw, kvalid, sel = _att_common(q_ref, kp_ref, vp_ref)
        for hh in range(2):
            s_scr[hh] = _dot_nt(jnp.where(sel[hh], q2, 0), kw)
        for hh in range(2):
            for rows in _row_groups():
                s = jnp.where(kvalid, s_scr[hh, rows, :] + bias_ref[hh, rows, :], NEG)
                m = jnp.max(s, axis=-1, keepdims=True)
                e = jnp.exp(s - m)
                l = jnp.sum(e, axis=-1, keepdims=True)
                p_scr[hh, rows, :] = (e * (1.0 / l)).astype(BF16)
                lse_scr[hh, rows, :] = jnp.broadcast_to(m + jnp.log(l), (ATT_ROWS, LANES))
        outs = [_dot(p_scr[hh], vw) for hh in range(2)]
        o_ref[...] = jnp.where(sel[0], outs[0], outs[1]).astype(BF16)
        lse_ref[...] = jnp.where(sel[0], lse_scr[0], lse_scr[1])

    qspec, kspec, bspec = _att_specs(t, tp)
    return _call(body, name, (d // LANES, t // Q_BLOCK), [qspec, kspec, kspec, bspec], [qspec, qspec],
                 [jax.ShapeDtypeStruct((t, d), BF16), jax.ShapeDtypeStruct((t, d), F32)],
                 [pltpu.VMEM((2, Q_BLOCK, K_WINDOW), F32), pltpu.VMEM((2, Q_BLOCK, K_WINDOW), BF16),
                  pltpu.VMEM((2, Q_BLOCK, LANES), F32)],
                 ("parallel", "arbitrary"), (q, kp, vp, bias), rider)


def _att_bwd(q, kp, vp, bias, do, o, lse, name, rider=None):
    t, d = q.shape
    tp = kp.shape[0]

    def body(q_ref, kp_ref, vp_ref, bias_ref, do_ref, o_ref, lse_ref, dq_ref, dkp_ref, dvp_ref, db_ref,
             s_scr, dp_scr, p_scr, ds_scr, row_scr):
        @pl.when(pl.program_id(1) == 0)
        def _():
            dkp_ref[...] = jnp.zeros_like(dkp_ref)
            dvp_ref[...] = jnp.zeros_like(dvp_ref)
            db_ref[...] = jnp.zeros_like(db_ref)

        start, q2, kw, vw, kvalid, sel = _att_common(q_ref, kp_ref, vp_ref)
        do2 = do_ref[...]
        qm = [jnp.where(sel[hh], q2, 0) for hh in range(2)]
        dom = [jnp.where(sel[hh], do2, 0) for hh in range(2)]
        do_o = do2.astype(F32) * o_ref[...].astype(F32)
        lse2 = lse_ref[...]
        for hh in range(2):
            s_scr[hh] = _dot_nt(qm[hh], kw)
            dp_scr[hh] = _dot_nt(dom[hh], vw)
            lse_h = jnp.max(jnp.where(sel[hh], lse2, NEG), axis=-1, keepdims=True)
            delta = jnp.sum(jnp.where(sel[hh], do_o, 0.0), axis=-1, keepdims=True)
            row_scr[hh, 0] = jnp.broadcast_to(lse_h, (Q_BLOCK, LANES))
            row_scr[hh, 1] = jnp.broadcast_to(delta, (Q_BLOCK, LANES))
        for hh in range(2):
            for rows in _row_groups():
                s = jnp.where(kvalid, s_scr[hh, rows, :] + bias_ref[hh, rows, :], NEG)
                p = jnp.exp(s - _lane_copies(row_scr[hh, 0, rows, :]))
                ds = p * (dp_scr[hh, rows, :] - _lane_copies(row_scr[hh, 1, rows, :]))
                db_ref[hh, rows, :] += ds
                p_scr[hh, rows, :] = p.astype(BF16)
                ds_scr[hh, rows, :] = ds.astype(BF16)
        dqs = [_dot(ds_scr[hh], kw) for hh in range(2)]
        dq_ref[...] = jnp.where(sel[0], dqs[0], dqs[1])
        dkp_ref[:, pl.ds(start, K_WINDOW)] += _dot_tn(qm[0], ds_scr[0]) + _dot_tn(qm[1], ds_scr[1])
        dvp_ref[:, pl.ds(start, K_WINDOW)] += _dot_tn(dom[0], p_scr[0]) + _dot_tn(dom[1], p_scr[1])

    qspec, kspec, bspec = _att_specs(t, tp)
    tspec = pl.BlockSpec((LANES, tp), lambda h, i: (h, 0))
    stage = lambda dt: pltpu.VMEM((2, Q_BLOCK, K_WINDOW), dt)
    return _call(body, name, (d // LANES, t // Q_BLOCK),
                 [qspec, kspec, kspec, bspec, qspec, qspec, qspec],
                 [qspec, tspec, tspec, bspec],
                 [jax.ShapeDtypeStruct((t, d), F32),
                  jax.ShapeDtypeStruct((d, tp), F32),
                  jax.ShapeDtypeStruct((d, tp), F32),
                  jax.ShapeDtypeStruct((ATT_HEADS, Q_BLOCK, K_WINDOW), F32)],
                 [stage(F32), stage(F32), stage(BF16), stage(BF16),
                  pltpu.VMEM((2, 2, Q_BLOCK, LANES), F32)],
                 ("parallel", "arbitrary"), (q, kp, vp, bias, do, o, lse), rider)


def _rel_bin_matrix():
    rows = REL_DELTAS * 2 * REL_BLK
    rho = lax.broadcasted_iota(jnp.int32, (rows, REL_PAD), 0)
    col = lax.broadcasted_iota(jnp.int32, (rows, REL_PAD), 1)
    assert 2 * REL_BLK == 256
    delta = rho >> 8
    c = 255 - (rho & 255)
    dist = K_PAD + REL_BLK * (delta - (K_WINDOW // REL_BLK - 1)) + (c - (REL_BLK - 1))
    idx = jnp.clip(dist, -REL_CLIP, REL_CLIP) + REL_CLIP
    return col == idx


def _rel_expand(rel_pad, name):
    heads = rel_pad.shape[0]
    rows = REL_DELTAS * 2 * REL_BLK

    def body_bin(r_ref, o_ref):
        onehot = jnp.where(_rel_bin_matrix(), 1.0, 0.0).astype(BF16)
        hi, mid, lo = _split3(r_ref[...])
        o_ref[...] = _dot_nt(hi, onehot) + _dot_nt(mid, onehot) + _dot_nt(lo, onehot)

    by_delta = pl.pallas_call(
        body_bin, name=name + "_bin",
        out_shape=jax.ShapeDtypeStruct((heads, rows), F32),
        compiler_params=pltpu.CompilerParams(vmem_limit_bytes=VMEM_LIMIT_V7X),
    )(rel_pad)
    by_delta = by_delta.reshape(heads * REL_DELTAS, 2 * REL_BLK)

    def body_shift(t_ref, o_ref):
        tv = t_ref[...]
        for r in range(REL_BLK):
            o_ref[r] = pltpu.roll(tv, (r + REL_BLK) % (2 * REL_BLK), 1)[:, :REL_BLK]

    return pl.pallas_call(
        body_shift, name=name + "_shift",
        out_shape=jax.ShapeDtypeStruct((REL_BLK, heads * REL_DELTAS, REL_BLK), F32),
        compiler_params=pltpu.CompilerParams(vmem_limit_bytes=VMEM_LIMIT_V7X),
    )(by_delta)


def _bias_table(rel_bias, name):
    heads = rel_bias.shape[0]
    rel_pad = jnp.pad(rel_bias, ((0, 0), (0, REL_PAD - REL_TABLE)))
    tiles = _rel_expand(rel_pad, name)
    tiles = tiles.reshape(REL_BLK, heads, REL_DELTAS, REL_BLK).transpose(1, 2, 0, 3)
    na, nb = Q_BLOCK // REL_BLK, K_WINDOW // REL_BLK
    rows = [jnp.concatenate([tiles[:, a - b + nb - 1] for b in range(nb)], axis=-1) for a in range(na)]
    table = jnp.concatenate(rows, axis=-2)
    qc = np.arange(Q_BLOCK)[:, None] // CHUNK
    kc = np.arange(K_WINDOW)[None, :] // CHUNK
    band = (kc >= qc) & (kc <= qc + PAST_CHUNKS)
    return jnp.where(jnp.asarray(band)[None], table, NEG)


def _rel_reduce(db, name):
    heads = db.shape[0]
    na, nb = Q_BLOCK // REL_BLK, K_WINDOW // REL_BLK

    def body_fold(db_ref, g_ref):
        for delta in range(REL_DELTAS):
            acc = None
            for a in range(na):
                b = a - (delta - (nb - 1))
                if 0 <= b < nb:
                    tile = db_ref[0, a * REL_BLK:(a + 1) * REL_BLK, b * REL_BLK:(b + 1) * REL_BLK]
                    acc = tile if acc is None else acc + tile
            g_ref[0, delta] = acc

    folded = pl.pallas_call(
        body_fold, name=name + "_fold", grid=(heads,),
        in_specs=[pl.BlockSpec((1, Q_BLOCK, K_WINDOW), lambda h: (h, 0, 0))],
        out_specs=pl.BlockSpec((1, REL_DELTAS, REL_BLK, REL_BLK), lambda h: (h, 0, 0, 0)),
        out_shape=jax.ShapeDtypeStruct((heads, REL_DELTAS, REL_BLK, REL_BLK), F32),
        compiler_params=_params("parallel"),
    )(db)
    by_row = folded.transpose(2, 0, 1, 3).reshape(REL_BLK, heads * REL_DELTAS, REL_BLK)

    def body_diag(g_ref, d_ref):
        zeros = jnp.zeros((heads * REL_DELTAS, REL_BLK), F32)
        acc = None
        for r in range(REL_BLK):
            part = pltpu.roll(jnp.concatenate([g_ref[r], zeros], axis=1), REL_BLK - r, 1)
            acc = part if acc is None else acc + part
        d_ref[...] = acc

    diag = pl.pallas_call(
        body_diag, name=name + "_diag",
        out_shape=jax.ShapeDtypeStruct((heads * REL_DELTAS, 2 * REL_BLK), F32),
        compiler_params=pltpu.CompilerParams(vmem_limit_bytes=VMEM_LIMIT_V7X),
    )(by_row)
    diag = diag.reshape(heads, REL_DELTAS * 2 * REL_BLK)

    def body_bin(d_ref, o_ref):
        onehot = jnp.where(_rel_bin_matrix(), 1.0, 0.0).astype(BF16)
        hi, mid, lo = _split3(d_ref[...])
        o_ref[...] = _dot(hi, onehot) + _dot(mid, onehot) + _dot(lo, onehot)

    out = pl.pallas_call(
        body_bin, name=name + "_bin",
        out_shape=jax.ShapeDtypeStruct((heads, REL_PAD), F32),
        compiler_params=pltpu.CompilerParams(vmem_limit_bytes=VMEM_LIMIT_V7X),
    )(diag)
    return out[:, :REL_TABLE]


def _sum_leading(x, name):
    n, r, c = x.shape
    tr = _pick(r, 256, 8)

    def body(x_ref, o_ref):
        acc = x_ref[0].astype(F32)
        for k in range(1, n):
            acc = acc + x_ref[k].astype(F32)
        o_ref[...] = acc

    return pl.pallas_call(
        body, name=name, grid=(r // tr,),
        in_specs=[pl.BlockSpec((n, tr, c), lambda i: (0, i, 0))],
        out_specs=pl.BlockSpec((tr, c), lambda i: (i, 0)),
        out_shape=jax.ShapeDtypeStruct((r, c), F32),
        compiler_params=_params("parallel"),
    )(x)


def _pair_add(g, recv, parity, name):
    _, r, c = g.shape
    tr = _pick(r, 256, 16)

    def body(par_ref, g_ref, r_ref, o_ref):
        o_ref[...] = (g_ref[...].astype(F32) + r_ref[...].astype(F32)).astype(BF16)

    return pl.pallas_call(
        body, name=name,
        grid_spec=pltpu.PrefetchScalarGridSpec(
            num_scalar_prefetch=1, grid=(4, r // tr),
            in_specs=[pl.BlockSpec((1, tr, c), lambda k, i, par: (2 * k + par[0], i, 0)),
                      pl.BlockSpec((1, tr, c), lambda k, i, par: (k, i, 0))],
            out_specs=pl.BlockSpec((1, tr, c), lambda k, i, par: (k, i, 0))),
        out_shape=jax.ShapeDtypeStruct((4, r, c), BF16),
        compiler_params=_params("parallel", "parallel"),
    )(parity, g, recv)


def _adamw(w, g_parts, m, v, name):
    r, c = w.shape
    n = g_parts.shape[0]
    tr = _pick(r, 256, 16 if g_parts.dtype == BF16 else 8)
    c1 = 1.0 - ADAM_B1 ** ADAM_STEP
    c2 = 1.0 - ADAM_B2 ** ADAM_STEP

    def body(w_ref, g_ref, m_ref, v_ref, go_ref, d_ref, nm_ref, nv_ref):
        gv = g_ref[0].astype(F32)
        for k in range(1, n):
            gv = gv + g_ref[k].astype(F32)
        nm = ADAM_B1 * m_ref[...] + (1.0 - ADAM_B1) * gv
        nv = ADAM_B2 * v_ref[...] + (1.0 - ADAM_B2) * (gv * gv)
        go_ref[...] = gv
        d_ref[...] = -ADAM_LR * ((nm / c1) / (jnp.sqrt(nv / c2) + ADAM_EPS) + ADAM_WD * w_ref[...])
        nm_ref[...] = nm
        nv_ref[...] = nv

    spec = pl.BlockSpec((tr, c), lambda i: (i, 0))
    shp = jax.ShapeDtypeStruct((r, c), F32)
    return pl.pallas_call(
        body, name=name, grid=(r // tr,),
        in_specs=[spec, pl.BlockSpec((n, tr, c), lambda i: (0, i, 0)), spec, spec],
        out_specs=[spec] * 4, out_shape=[shp] * 4,
        compiler_params=_params("parallel"),
    )(w, g_parts, m, v)


BIG = (("a_w_in", 1), ("a_w_o", 0), ("a_w_gu", 1), ("a_w_down", 0), ("w_kv", 1),
       ("b_w_q", 0), ("b_w_o", 0), ("b_w_gu", 1), ("b_w_down", 0))

SMALL = (("a_norm_g", D_MODEL, True), ("a_gn_g", RET_V_COLS, True), ("a_ffn_norm_g", D_MODEL, True),
         ("kv_norm_g", D_MODEL, False), ("b_norm_g", D_MODEL, False), ("b_ffn_norm_g", D_MODEL, False),
         ("k_norm_g", ATT_DH, False), ("b_q_norm_g", ATT_DH, False),
         ("b_rel_bias", ATT_HEADS * REL_TABLE, False))
SMALL_ROWS, SMALL_COLS = 16, 1024


def _pack_small(vals):
    flat = jnp.concatenate([vals[n].reshape(-1) for n, _, _ in SMALL])
    return jnp.pad(flat, (0, SMALL_ROWS * SMALL_COLS - flat.shape[0])).reshape(SMALL_ROWS, SMALL_COLS)


def _unpack_small(packed, local):
    flat, out, pos = packed.reshape(-1), {}, 0
    for n, length, sharded in SMALL:
        ln = length // N_DEV if (local and sharded) else length
        out[n] = flat[pos:pos + ln]
        pos += ln
    return out


def _gather_rider(shards, names):
    return _GatherRider([shards[n] for n in names])


def _gathered(rider, names, axis_of):
    return {n: (r.reshape(-1, r.shape[2]) if axis_of[n] == 0 else r) for n, r in zip(names, rider.results)}


def _blocks(g):
    return g if g.ndim == 3 else g.reshape(N_DEV, -1, g.shape[-1])


def _local_step(x, target, shards, w_in, s, parity):
    t = x.shape[0]
    axis_of = dict(BIG)
    consts = _ret_consts(t)
    bd = jnp.asarray(np.kron(np.eye(ATT_HEADS, dtype=np.float32),
                             np.ones((ATT_DH, ATT_DH), np.float32))).astype(BF16)
    kg_t = jnp.tile(s["k_norm_g"], (1, ATT_HEADS))
    qg_t = jnp.tile(s["b_q_norm_g"], (1, ATT_HEADS))
    q_scale = ATT_DH ** -0.5
    w = {"a_w_in": w_in}
    g, recv = {}, {}

    def gather_on(names):
        return _gather_rider(shards, names), names

    def landed(ride):
        w.update(_gathered(ride[0], ride[1], axis_of))

    def scatter_on(names):
        return _ScatterRider([_blocks(g[n]) for n in names]), names

    def reduced(ride):
        recv.update(zip(ride[1], ride[0].results))

    h1 = _rms_fwd(x, s["a_norm_g"], "a_norm")
    ride = gather_on(["a_w_o", "a_w_down"])
    proj = _mm(h1, w["a_w_in"], "nn", "a_proj", rider=ride[0])
    landed(ride)
    ride = gather_on(["a_w_gu", "w_kv"])
    y, o_ret, states = _ret_fwd(proj, s["a_gn_g"], consts, "a_ret", rider=ride[0])
    landed(ride)
    x1 = _mm(y, w["a_w_o"], "nn", "a_out", res=x)
    h2 = _rms_fwd(x1, s["a_ffn_norm_g"], "a_ffn_norm")
    ride = gather_on(["b_w_q", "b_w_o", "b_w_down"])
    gu_a, act_a = _mm(h2, w["a_w_gu"], "nn", "a_ffn_gu", epilogue="swiglu", rider=ride[0])
    landed(ride)
    x2 = _mm(act_a, w["a_w_down"], "nn", "a_ffn_down", res=x1)

    u = _rms_fwd(x2, s["kv_norm_g"], "kv_norm")
    kv = _mm(u, w["w_kv"], "nn", "kv_proj")
    kp, vp = _kv_prep(kv, kg_t, bd, "kv_prep")

    h3 = _rms_fwd(x2, s["b_norm_g"], "b_norm")
    q_raw = _mm(h3, w["b_w_q"], "nn", "b_q")
    qn = _q_hnorm(q_raw, qg_t, bd, q_scale, "q_hnorm")
    bias = _bias_table(s["b_rel_bias"].reshape(ATT_HEADS, REL_TABLE), "rel")
    ride = gather_on(["b_w_gu"])
    o_att, lse = _att_fwd(qn, kp, vp, bias, "b_att", rider=ride[0])
    landed(ride)
    x3 = _mm(o_att, w["b_w_o"], "nn", "b_out", res=x2)
    h4 = _rms_fwd(x3, s["b_ffn_norm_g"], "b_ffn_norm")
    gu_b, act_b = _mm(h4, w["b_w_gu"], "nn", "b_ffn_gu", epilogue="swiglu")
    x4 = _mm(act_b, w["b_w_down"], "nn", "b_ffn_down", res=x3)

    dy, loss = _loss_head(x4, target, "loss")
    in_blk, kv_blk, ffn_blk = w["a_w_in"].shape[2], w["w_kv"].shape[2], w["b_w_gu"].shape[2]

    dgu = _mm(dy, w["b_w_down"], "nt", "b_ffn_dgu", out_block=ffn_blk, epilogue="swiglu_bwd", extra=gu_b)
    dgu = dgu.reshape(N_DEV, t, ffn_blk)
    g["b_w_down"] = _mm(act_b, dy, "tn", "b_ffn_gdown", out_dtype=BF16)
    ride = scatter_on(["b_w_down"])
    dh4 = _mm(dgu, w["b_w_gu"], "nt", "b_ffn_dh", rider=ride[0])
    reduced(ride)
    g["b_w_gu"] = _mm(h4, dgu, "tn", "b_ffn_ggu", out_dtype=BF16, out_block=ffn_blk)
    dx3, g["b_ffn_norm_g"] = _rms_bwd(x3, s["b_ffn_norm_g"], dh4, dy, "b_ffn_dnorm")

    do_att = _mm(dx3, w["b_w_o"], "nt", "b_dout", out_dtype=BF16)
    g["b_w_o"] = _mm(o_att, dx3, "tn", "b_gout", out_dtype=BF16)
    ride = scatter_on(["b_w_gu", "b_w_o"])
    dq, dkp, dvp, db = _att_bwd(qn, kp, vp, bias, do_att, o_att, lse, "b_datt", rider=ride[0])
    reduced(ride)
    g["b_rel_bias"] = _rel_reduce(db, "drel").reshape(1, -1)
    dq_raw, gq = _q_dhnorm(q_raw, qg_t, bd, dq, q_scale, "q_dhnorm")
    g["b_q_norm_g"] = gq.reshape(ATT_HEADS, ATT_DH).sum(axis=0, keepdims=True)
    dh3 = _mm(dq_raw, w["b_w_q"], "nt", "b_dq")
    g["b_w_q"] = _mm(h3, dq_raw, "tn", "b_gq", out_dtype=BF16)
    dx2, g["b_norm_g"] = _rms_bwd(x2, s["b_norm_g"], dh3, dx3, "b_dnorm")

    dkv, gk = _kv_dprep(kv, kg_t, bd, dkp.T, dvp.T, "kv_dprep")
    g["k_norm_g"] = gk.reshape(ATT_HEADS, ATT_DH).sum(axis=0, keepdims=True)
    du = _mm(dkv, w["w_kv"], "nt", "kv_du")
    g["w_kv"] = _mm(u, dkv, "tn", "kv_g", out_dtype=BF16, out_block=kv_blk)
    dx2, g["kv_norm_g"] = _rms_bwd(x2, s["kv_norm_g"], du, dx2, "kv_dnorm")

    ride = scatter_on(["b_w_q", "w_kv"])
    dgu = _mm(dx2, w["a_w_down"], "nt", "a_ffn_dgu", out_block=ffn_blk, epilogue="swiglu_bwd", extra=gu_a,
              rider=ride[0])
    reduced(ride)
    dgu = dgu.reshape(N_DEV, t, ffn_blk)
    g["a_w_down"] = _mm(act_a, dx2, "tn", "a_ffn_gdown", out_dtype=BF16)
    ride = scatter_on(["a_w_down"])
    dh2 = _mm(dgu, w["a_w_gu"], "nt", "a_ffn_dh", rider=ride[0])
    reduced(ride)
    g["a_w_gu"] = _mm(h2, dgu, "tn", "a_ffn_ggu", out_dtype=BF16, out_block=ffn_blk)
    dx1, g["a_ffn_norm_g"] = _rms_bwd(x1, s["a_ffn_norm_g"], dh2, dx2, "a_ffn_dnorm")

    dy_ret = _mm(dx1, w["a_w_o"], "nt", "a_dout")
    g["a_w_o"] = _mm(y, dx1, "tn", "a_gout", out_dtype=BF16)
    ride = scatter_on(["a_w_gu"])
    dproj, g["a_gn_g"] = _ret_bwd(proj, s["a_gn_g"], o_ret, states, dy_ret, consts, "a_dret", rider=ride[0])
    reduced(ride)
    ride = scatter_on(["a_w_o"])
    g["a_w_in"] = _mm(h1, dproj, "tn", "a_gin", out_dtype=BF16, out_block=in_blk, rider=ride[0])
    reduced(ride)
    from_sibling = _exchange(_SiblingSwapRider([g["a_w_in"]]), "rs_sibling")[0]
    chip_sums = _pair_add(g["a_w_in"], from_sibling, parity, "rs_pair_add")
    last = _ChipScatterRider([chip_sums])
    dh1 = _mm(dproj, w["a_w_in"], "nt", "a_dproj", rider=last)
    recv["a_w_in"] = last.results[0]
    grad_x, g["a_norm_g"] = _rms_bwd(x, s["a_norm_g"], dh1, dx1, "a_dnorm")
    return loss, grad_x, recv, g


ARG_NAMES = ("x", "a_norm_g", "a_w_in", "a_gn_g", "a_w_o", "a_ffn_norm_g", "a_w_gu", "a_w_down",
             "kv_norm_g", "w_kv", "k_norm_g", "b_norm_g", "b_w_q", "b_q_norm_g", "b_rel_bias", "b_w_o",
             "b_ffn_norm_g", "b_w_gu", "b_w_down")
WEIGHT_NAMES = ARG_NAMES[1:]


def _big_shard(a):
    return a[0] if a.ndim == 3 else a


def kernel(x, a_norm_g, a_w_in, a_gn_g, a_w_o, a_ffn_norm_g, a_w_gu, a_w_down, kv_norm_g, w_kv, k_norm_g, b_norm_g, b_w_q, b_q_norm_g, b_rel_bias, b_w_o, b_ffn_norm_g, b_w_gu, b_w_down, loss_target, m_a_norm_g, m_a_w_in, m_a_gn_g, m_a_w_o, m_a_ffn_norm_g, m_a_w_gu, m_a_w_down, m_kv_norm_g, m_w_kv, m_k_norm_g, m_b_norm_g, m_b_w_q, m_b_q_norm_g, m_b_rel_bias, m_b_w_o, m_b_ffn_norm_g, m_b_w_gu, m_b_w_down, v_a_norm_g, v_a_w_in, v_a_gn_g, v_a_w_o, v_a_ffn_norm_g, v_a_w_gu, v_a_w_down, v_kv_norm_g, v_w_kv, v_k_norm_g, v_b_norm_g, v_b_w_q, v_b_q_norm_g, v_b_rel_bias, v_b_w_o, v_b_ffn_norm_g, v_b_w_gu, v_b_w_down):
    args = (x, a_norm_g, a_w_in, a_gn_g, a_w_o, a_ffn_norm_g, a_w_gu, a_w_down, kv_norm_g, w_kv, k_norm_g,
            b_norm_g, b_w_q, b_q_norm_g, b_rel_bias, b_w_o, b_ffn_norm_g, b_w_gu, b_w_down)
    p = dict(zip(ARG_NAMES, args))
    m_all = dict(zip(WEIGHT_NAMES, (m_a_norm_g, m_a_w_in, m_a_gn_g, m_a_w_o, m_a_ffn_norm_g, m_a_w_gu,
                                    m_a_w_down, m_kv_norm_g, m_w_kv, m_k_norm_g, m_b_norm_g, m_b_w_q,
                                    m_b_q_norm_g, m_b_rel_bias, m_b_w_o, m_b_ffn_norm_g, m_b_w_gu, m_b_w_down)))
    v_all = dict(zip(WEIGHT_NAMES, (v_a_norm_g, v_a_w_in, v_a_gn_g, v_a_w_o, v_a_ffn_norm_g, v_a_w_gu,
                                    v_a_w_down, v_kv_norm_g, v_w_kv, v_k_norm_g, v_b_norm_g, v_b_w_q,
                                    v_b_q_norm_g, v_b_rel_bias, v_b_w_o, v_b_ffn_norm_g, v_b_w_gu, v_b_w_down)))
    xi, yi, ci = _my_place()
    me = 4 * xi + 2 * yi + ci
    big_names = [n for n, _ in BIG]
    axis_of = dict(BIG)

    big_local = {n: _big_shard(p[n]) for n in big_names}
    shards = {n: a.astype(BF16) for n, a in big_local.items()}
    small_local = _pack_small({n: p[n] for n, _, _ in SMALL})
    w_in, small_all = _exchange(_GatherRider([shards["a_w_in"], small_local]), "gather_in")
    flat_g = small_all.reshape(N_DEV, -1)
    s_full, pos = {}, 0
    for n, length, sharded in SMALL:
        ln = length // N_DEV if sharded else length
        s_full[n] = flat_g[:, pos:pos + ln].reshape(1, -1) if sharded else p[n].reshape(1, -1)
        pos += ln

    parity = jnp.reshape(ci, (1,)).astype(jnp.int32)
    loss, grad_x, recv, g = _local_step(x[0], loss_target[0], shards, w_in, s_full, parity)
    loss = lax.psum(loss[0, 0], ("x", "y", "c"))

    g_small_all = _exchange(_GatherRider([_pack_small({n: g[n] for n, _, _ in SMALL})]), "gather_gsmall")[0]
    g_small = _unpack_small(_sum_leading(g_small_all, "gsmall_sum"), local=False)
    for n, length, sharded in SMALL:
        if sharded:
            g_small[n] = lax.dynamic_slice(g_small[n], (me * (length // N_DEV),), (length // N_DEV,))

    grads, deltas, new_m, new_v = {}, {}, {}, {}
    for n in big_names:
        outs = _adamw(big_local[n], recv[n], _big_shard(m_all[n]), _big_shard(v_all[n]), "adamw_" + n)
        grads[n], deltas[n], new_m[n], new_v[n] = (a.reshape(p[n].shape) for a in outs)
    pk = lambda src: _pack_small({n: src[n] for n, _, _ in SMALL})
    outs = _adamw(small_local, pk(g_small)[None], pk(m_all), pk(v_all), "adamw_small")
    g_s, d_s, nm_s, nv_s = (_unpack_small(a, local=True) for a in outs)
    for n, _, _ in SMALL:
        grads[n], deltas[n], new_m[n], new_v[n] = (a[n].reshape(p[n].shape) for a in (g_s, d_s, nm_s, nv_s))

    return (loss, grad_x[None], *[grads[n] for n in WEIGHT_NAMES], *[deltas[n] for n in WEIGHT_NAMES],
            *[new_m[n] for n in WEIGHT_NAMES], *[new_v[n] for n in WEIGHT_NAMES])
```

```python
import numpy as np
import jax
import jax.numpy as jnp
from jax import lax
from jax.experimental import pallas as pl
from jax.experimental.pallas import tpu as pltpu

F32 = jnp.float32
BF16 = jnp.bfloat16

N_DEV = 8
D_MODEL = 1024
CHUNK = 64
EPS = 1e-6
RET_HEADS, RET_DK, RET_DV = 4, 256, 512
RET_Q_COLS = RET_HEADS * RET_DK
RET_V_COLS = RET_HEADS * RET_DV
ATT_HEADS, ATT_DH = 16, 64
PAST_CHUNKS = 8
REL_CLIP = 256
REL_TABLE = 2 * REL_CLIP + 1
FFN_HIDDEN = 2816
ROPE_BASE = 10000.0
LANES = 128
Q_BLOCK = 256
ATT_ROWS = 32
K_PAD = PAST_CHUNKS * CHUNK
K_WINDOW = Q_BLOCK + K_PAD
REL_BLK = 128
REL_DELTAS = Q_BLOCK // REL_BLK + K_WINDOW // REL_BLK - 1
REL_PAD = 640
NEG = -1e30
VMEM_LIMIT_V7X = 56 * 1024 * 1024
ADAM_LR, ADAM_B1, ADAM_B2, ADAM_EPS, ADAM_WD, ADAM_STEP = 1e-3, 0.9, 0.999, 1e-8, 0.01, 10
MESH = pl.DeviceIdType.MESH
ANY = pl.BlockSpec(memory_space=pl.ANY)


def _params(*semantics):
    return pltpu.CompilerParams(dimension_semantics=semantics, vmem_limit_bytes=VMEM_LIMIT_V7X)


def _pick(dim, cap, align):
    best = None
    for t in range(align, min(dim, cap) + 1, align):
        if dim % t == 0:
            best = t
    assert best is not None, (dim, cap, align)
    return best


def _dot(a, b):
    return lax.dot_general(a, b, (((1,), (0,)), ((), ())), preferred_element_type=F32)


def _dot_nt(a, b):
    return lax.dot_general(a, b, (((1,), (1,)), ((), ())), preferred_element_type=F32)


def _dot_tn(a, b):
    return lax.dot_general(a, b, (((0,), (0,)), ((), ())), preferred_element_type=F32)


def _split2(x):
    hi = x.astype(BF16)
    lo = (x - hi.astype(F32)).astype(BF16)
    return hi, lo


def _split3(x):
    hi = x.astype(BF16)
    r = x - hi.astype(F32)
    mid = r.astype(BF16)
    lo = (r - mid.astype(F32)).astype(BF16)
    return hi, mid, lo


def _sigmoid(x):
    return 1.0 / (1.0 + jnp.exp(-x))


def _accumulate(ref, part, step):
    @pl.when(step == 0)
    def _():
        ref[...] = part

    @pl.when(step > 0)
    def _():
        ref[...] += part


def _my_place():
    return lax.axis_index("x"), lax.axis_index("y"), lax.axis_index("c")


def _flip(v, bit):
    return 1 - v if bit else v


class _GatherRider:
    def __init__(self, xs):
        self.inputs = list(xs)
        n = len(xs)
        self.out_shape = [jax.ShapeDtypeStruct((N_DEV,) + x.shape, x.dtype) for x in xs]
        self.scratch = [pltpu.SemaphoreType.DMA((7, n)), pltpu.SemaphoreType.DMA((7, n)),
                        pltpu.SemaphoreType.DMA((n,))]
        self.results = None

    def _copies(self, x_refs, out_refs, sems):
        send_sems, recv_sems, local_sems = sems
        n = len(x_refs)
        x, y, c = _my_place()
        me, sibling = (x, y, c), (x, y, 1 - c)
        chips = [(1 - x, y), (x, 1 - y), (1 - x, 1 - y)]

        def slot(a, px, py, pc):
            return out_refs[a].at[4 * px + 2 * py + pc]

        def copy(k, a, block, to, own=False):
            return pltpu.make_async_remote_copy(
                src_ref=x_refs[a] if own else slot(a, *block), dst_ref=slot(a, *block),
                send_sem=send_sems.at[k, a], recv_sem=recv_sems.at[k, a],
                device_id=to, device_id_type=MESH)

        mine = [pltpu.make_async_copy(x_refs[a], slot(a, *me), local_sems.at[a]) for a in range(n)]
        first = []
        for a in range(n):
            first.append(copy(0, a, me, sibling, own=True))
            first += [copy(1 + j, a, me, (*chip, c), own=True) for j, chip in enumerate(chips)]
        return n, c, me, sibling, chips, copy, mine, first

    def start(self, x_refs, out_refs, sems):
        _, _, _, _, _, _, mine, first = self._copies(x_refs, out_refs, sems)
        for cp in mine + first:
            cp.start()

    def finish(self, x_refs, out_refs, sems):
        n, c, me, sibling, chips, copy, mine, first = self._copies(x_refs, out_refs, sems)
        passed = []
        for j, chip in enumerate(chips):
            for a in range(n):
                copy(1 + j, a, (*chip, c), me).wait_recv()
                passed.append(copy(4 + j, a, (*chip, c), sibling))
                passed[-1].start()
        for a in range(n):
            copy(0, a, sibling, me).wait_recv()
            for j, chip in enumerate(chips):
                copy(4 + j, a, (*chip, 1 - c), me).wait_recv()
        for cp in first + passed:
            cp.wait_send()
        for cp in mine:
            cp.wait()


class _ScatterRider:
    def __init__(self, gs):
        self.inputs = list(gs)
        n = len(gs)
        self.out_shape = [jax.ShapeDtypeStruct(g.shape, g.dtype) for g in gs]
        self.scratch = [pltpu.SemaphoreType.DMA((7, n)), pltpu.SemaphoreType.DMA((7, n)),
                        pltpu.SemaphoreType.DMA((n,))]
        self.results = None

    def _copies(self, g_refs, out_refs, sems):
        send_sems, recv_sems, local_sems = sems
        x, y, c = _my_place()
        me = 4 * x + 2 * y + c
        mine, copies = [], []
        for a in range(len(g_refs)):
            mine.append(pltpu.make_async_copy(g_refs[a].at[me], out_refs[a].at[me], local_sems.at[a]))
            for k in range(1, N_DEV):
                px, py, pc = _flip(x, k & 4), _flip(y, k & 2), _flip(c, k & 1)
                copies.append(pltpu.make_async_remote_copy(
                    src_ref=g_refs[a].at[4 * px + 2 * py + pc], dst_ref=out_refs[a].at[me],
                    send_sem=send_sems.at[k - 1, a], recv_sem=recv_sems.at[k - 1, a],
                    device_id=(px, py, pc), device_id_type=MESH))
        return mine, copies

    def start(self, g_refs, out_refs, sems):
        mine, copies = self._copies(g_refs, out_refs, sems)
        for cp in mine + copies:
            cp.start()

    def finish(self, g_refs, out_refs, sems):
        mine, copies = self._copies(g_refs, out_refs, sems)
        for cp in copies + mine:
            cp.wait()


class _SiblingSwapRider:
    def __init__(self, gs):
        self.inputs = list(gs)
        n = len(gs)
        self.out_shape = [jax.ShapeDtypeStruct((4,) + g.shape[1:], g.dtype) for g in gs]
        self.scratch = [pltpu.SemaphoreType.DMA((4, n)), pltpu.SemaphoreType.DMA((4, n))]
        self.results = None

    def _copies(self, g_refs, out_refs, sems):
        send_sems, recv_sems = sems
        x, y, c = _my_place()
        return [pltpu.make_async_remote_copy(
            src_ref=g_refs[a].at[2 * k + 1 - c], dst_ref=out_refs[a].at[k],
            send_sem=send_sems.at[k, a], recv_sem=recv_sems.at[k, a],
            device_id=(x, y, 1 - c), device_id_type=MESH)
            for a in range(len(g_refs)) for k in range(4)]

    def start(self, g_refs, out_refs, sems):
        for cp in self._copies(g_refs, out_refs, sems):
            cp.start()

    def finish(self, g_refs, out_refs, sems):
        for cp in self._copies(g_refs, out_refs, sems):
            cp.wait()


class _ChipScatterRider:
    def __init__(self, ps):
        self.inputs = list(ps)
        n = len(ps)
        self.out_shape = [jax.ShapeDtypeStruct(p.shape, p.dtype) for p in ps]
        self.scratch = [pltpu.SemaphoreType.DMA((3, n)), pltpu.SemaphoreType.DMA((3, n)),
                        pltpu.SemaphoreType.DMA((n,))]
        self.results = None

    def _copies(self, p_refs, out_refs, sems):
        send_sems, recv_sems, local_sems = sems
        x, y, c = _my_place()
        my_chip = 2 * x + y
        chips = [(1 - x, y), (x, 1 - y), (1 - x, 1 - y)]
        n = len(p_refs)
        mine = [pltpu.make_async_copy(p_refs[a].at[my_chip], out_refs[a].at[my_chip], local_sems.at[a])
                for a in range(n)]
        copies = [pltpu.make_async_remote_copy(
            src_ref=p_refs[a].at[2 * cx + cy], dst_ref=out_refs[a].at[my_chip],
            send_sem=send_sems.at[j, a], recv_sem=recv_sems.at[j, a],
            device_id=(cx, cy, c), device_id_type=MESH)
            for a in range(n) for j, (cx, cy) in enumerate(chips)]
        return mine, copies

    def start(self, p_refs, out_refs, sems):
        mine, copies = self._copies(p_refs, out_refs, sems)
        for cp in mine + copies:
            cp.start()

    def finish(self, p_refs, out_refs, sems):
        mine, copies = self._copies(p_refs, out_refs, sems)
        for cp in copies + mine:
            cp.wait()


def _call(body, name, grid, in_specs, out_specs, out_shape, scratch, semantics, args, rider=None):
    in_specs, out_specs, out_shape, scratch = list(in_specs), list(out_specs), list(out_shape), list(scratch)
    if rider is None:
        return list(pl.pallas_call(
            body, name=name, grid=grid, in_specs=in_specs, out_specs=out_specs, out_shape=out_shape,
            scratch_shapes=scratch, compiler_params=_params(*semantics))(*args))
    n_in, n_out, n_scr = len(in_specs), len(out_specs), len(scratch)
    r_in, r_out = len(rider.inputs), len(rider.out_shape)

    def wrapped(*refs):
        cuts = np.cumsum([0, n_in, r_in, n_out, r_out, n_scr])
        hi, ri, ho, ro, hs = (refs[cuts[i]:cuts[i + 1]] for i in range(5))
        rs = refs[cuts[5]:]
        ids = [pl.program_id(d) for d in range(len(grid))]
        first, last = ids[0] == 0, ids[0] == grid[0] - 1
        for d in range(1, len(grid)):
            first = jnp.logical_and(first, ids[d] == 0)
            last = jnp.logical_and(last, ids[d] == grid[d] - 1)

        @pl.when(first)
        def _():
            rider.start(ri, ro, rs)

        body(*hi, *ho, *hs)

        @pl.when(last)
        def _():
            rider.finish(ri, ro, rs)

    outs = pl.pallas_call(
        wrapped, name=name, grid=grid,
        in_specs=in_specs + [ANY] * r_in, out_specs=out_specs + [ANY] * r_out,
        out_shape=out_shape + rider.out_shape, scratch_shapes=scratch + rider.scratch,
        compiler_params=_params(*(["arbitrary"] * len(grid))),
    )(*args, *rider.inputs)
    rider.results = list(outs[n_out:])
    return list(outs[:n_out])


def _exchange(rider, name):
    r_in, r_out = len(rider.inputs), len(rider.out_shape)

    def body(*refs):
        ri, ro, rs = refs[:r_in], refs[r_in:r_in + r_out], refs[r_in + r_out:]
        rider.start(ri, ro, rs)
        rider.finish(ri, ro, rs)

    return list(pl.pallas_call(
        body, name=name, in_specs=[ANY] * r_in, out_specs=[ANY] * r_out,
        out_shape=rider.out_shape, scratch_shapes=rider.scratch)(*rider.inputs))


MM_CAP_MN = 1024
MM_CAP_N = 1536
MM_CAP_K = 3072
MM_CAP_K_TOKENS = 2048
MM_CAP_K_RMS = 1536
NORM_ROWS = 256


def _mm(a, b, mode, name, out_dtype=F32, res=None, out_block=None, epilogue=None, extra=None, norm_g=None,
        rider=None):
    a3, b3 = a.ndim == 3, b.ndim == 3
    um = un = uk = None
    if mode in ("nn", "nt"):
        if a3:
            m, uk = a.shape[1:]
            k = a.shape[0] * uk
        else:
            m, k = a.shape
    else:
        if a3:
            k, um = a.shape[1:]
            m = a.shape[0] * um
        else:
            k, m = a.shape
    if mode in ("nn", "tn"):
        if b3:
            kb, un = b.shape[1:]
            n = b.shape[0] * un
        else:
            kb, n = b.shape
        assert kb == k, (a.shape, b.shape, mode)
    else:
        if b3:
            n, ukb = b.shape[1:]
            assert b.shape[0] * ukb == k and uk in (None, ukb), (a.shape, b.shape, mode)
            uk = ukb
        else:
            n, kb = b.shape
            assert kb == k, (a.shape, b.shape, mode)
    if out_block is not None:
        assert un in (None, out_block)
        un = out_block

    def tile(dim, unit, cap, align):
        if unit is None:
            return _pick(dim, cap, align), 1
        c = max(1, cap // unit)
        while (dim // unit) % c:
            c -= 1
        return unit, c

    um, cm = tile(m, um, MM_CAP_MN if mode != "tn" else 1408, 128 if mode == "tn" else 16)
    un, cn = tile(n, un, MM_CAP_N, 128)
    cap_k = MM_CAP_K_TOKENS if mode == "tn" else (MM_CAP_K_RMS if epilogue == "rms_bwd" else MM_CAP_K)
    uk, ck = tile(k, uk, cap_k, 128)
    if epilogue == "rms_bwd":
        assert mode == "nt" and n == D_MODEL and cm == cn == 1 and res is None and out_block is None
    if norm_g is not None:
        assert not a3 and (k if mode == "nn" else m) == D_MODEL and (ck if mode == "nn" else cm) == 1
    if epilogue == "swiglu":
        assert mode == "nn" and b3 and res is None and out_block is None
        cn = 2
    if epilogue == "swiglu_bwd":
        assert mode == "nt" and out_block is not None and extra is not None and res is None
        cn = 1
    tm, tn, tk = cm * um, cn * un, ck * uk
    nk = k // tk
    dot = {"nn": _dot, "nt": _dot_nt, "tn": _dot_tn}[mode]
    half = n // un // 2
    blocked_out = out_block is not None or epilogue in ("swiglu", "swiglu_bwd")
    extras = [] if extra is None else (list(extra) if isinstance(extra, (tuple, list)) else [extra])

    def sl(idx, unit, count):
        return slice(None) if count == 1 else slice(idx * unit, (idx + 1) * unit)

    def body(*refs):
        a_ref, b_ref = refs[0], refs[1]
        pos = 2
        r_ref = ng_ref = None
        if res is not None:
            r_ref, pos = refs[pos], pos + 1
        e_refs, pos = refs[pos:pos + len(extras)], pos + len(extras)
        if norm_g is not None:
            ng_ref, pos = refs[pos], pos + 1
        outs, acc_ref = refs[pos:-1], refs[-1]
        kk = pl.program_id(2)

        def normed():
            groups = []
            for r in range(0, a_ref.shape[0], NORM_ROWS):
                xv = a_ref[r:r + NORM_ROWS, :]
                rstd = lax.rsqrt(jnp.mean(xv * xv, axis=-1, keepdims=True) + EPS)
                groups.append((xv * rstd * ng_ref[...]).astype(BF16))
            return jnp.concatenate(groups, axis=0)

        def a_blk(mi, ki):
            if norm_g is not None:
                return normed()
            if mode in ("nn", "nt"):
                return a_ref[ki] if a3 else a_ref[:, sl(ki, uk, ck)]
            return a_ref[mi] if a3 else a_ref[:, sl(mi, um, cm)]

        def b_blk(ki, ni):
            if epilogue == "swiglu":
                return b_ref[ni, 0]
            if mode in ("nn", "tn"):
                return b_ref[ni] if b3 else b_ref[sl(ki, uk, ck), sl(ni, un, cn)]
            return b_ref[ki][sl(ni, un, cn), :] if b3 else b_ref[sl(ni, un, cn), sl(ki, uk, ck)]

        parts = {}
        for mi in range(cm):
            for ni in range(cn):
                part = None
                for ki in range(ck):
                    d = dot(a_blk(mi, ki).astype(BF16), b_blk(ki, ni).astype(BF16))
                    part = d if part is None else part + d
                parts[mi, ni] = part

        def finish(total):
            if epilogue == "swiglu":
                gate, up = total[0, 0], total[0, 1]
                outs[0][0, 0] = gate.astype(BF16)
                outs[0][1, 0] = up.astype(BF16)
                outs[1][0] = (gate * _sigmoid(gate) * up).astype(BF16)
                return
            if epilogue == "swiglu_bwd":
                dact = total[0, 0]
                gate, up = e_refs[0][0, 0].astype(F32), e_refs[0][1, 0].astype(F32)
                sg = _sigmoid(gate)
                outs[0][0, 0] = (dact * up * (sg * (1.0 + gate * (1.0 - sg)))).astype(BF16)
                outs[0][1, 0] = (dact * (gate * sg)).astype(BF16)
                return
            if epilogue == "rms_bwd":
                x_ref, g_ref, dres_ref = e_refs
                dh, dg = total[0, 0], None
                for r in range(0, tm, NORM_ROWS):
                    rows = slice(r, r + NORM_ROWS)
                    xv, dhv = x_ref[rows, :], dh[rows, :]
                    rstd = lax.rsqrt(jnp.mean(xv * xv, axis=-1, keepdims=True) + EPS)
                    xh = xv * rstd
                    dyg = dhv * g_ref[...]
                    c = jnp.mean(dyg * xh, axis=-1, keepdims=True)
                    outs[0][rows, :] = dres_ref[rows, :] + rstd * (dyg - xh * c)
                    part = jnp.sum(dhv * xh, axis=0, keepdims=True)
                    dg = part if dg is None else dg + part
                _accumulate(outs[1], dg, pl.program_id(0))
                return
            for (mi, ni), val in total.items():
                rows, cols = sl(mi, um, cm), sl(ni, un, cn)
                if res is not None:
                    val = r_ref[rows, cols] + val
                if blocked_out:
                    outs[0][ni, rows] = val.astype(out_dtype)
                else:
                    outs[0][rows, cols] = val.astype(out_dtype)

        if nk == 1:
            finish(parts)
        else:
            @pl.when(kk == 0)
            def _():
                for (mi, ni), val in parts.items():
                    acc_ref[mi * cn + ni] = val

            @pl.when(jnp.logical_and(kk > 0, kk < nk - 1))
            def _():
                for (mi, ni), val in parts.items():
                    acc_ref[mi * cn + ni] += val

            @pl.when(kk == nk - 1)
            def _():
                finish({key: acc_ref[key[0] * cn + key[1]] + val for key, val in parts.items()})

    if mode in ("nn", "nt"):
        a_spec = (pl.BlockSpec((ck, tm, uk), lambda i, j, kk: (kk, i, 0)) if a3
                  else pl.BlockSpec((tm, tk), lambda i, j, kk: (i, kk)))
    else:
        a_spec = (pl.BlockSpec((cm, tk, um), lambda i, j, kk: (i, kk, 0)) if a3
                  else pl.BlockSpec((tk, tm), lambda i, j, kk: (kk, i)))
    pair_spec = pl.BlockSpec((2, 1, tm, un), lambda i, j, kk: (0, j, i, 0))
    row_spec = pl.BlockSpec((tm, tn), lambda i, j, kk: (i, 0))
    vec_spec = pl.BlockSpec((1, tn), lambda i, j, kk: (0, 0))
    if epilogue == "swiglu":
        b = b.reshape(2, half, k, un)
        b_spec = pl.BlockSpec((2, 1, tk, un), lambda i, j, kk: (0, j, kk, 0))
    elif mode in ("nn", "tn"):
        b_spec = (pl.BlockSpec((cn, tk, un), lambda i, j, kk: (j, kk, 0)) if b3
                  else pl.BlockSpec((tk, tn), lambda i, j, kk: (kk, j)))
    else:
        b_spec = (pl.BlockSpec((ck, tn, uk), lambda i, j, kk: (kk, j, 0)) if b3
                  else pl.BlockSpec((tn, tk), lambda i, j, kk: (j, kk)))
    if epilogue == "swiglu":
        out_specs = [pair_spec, pl.BlockSpec((1, tm, un), lambda i, j, kk: (j, i, 0))]
        out_shape = [jax.ShapeDtypeStruct((2, half, m, un), BF16), jax.ShapeDtypeStruct((half, m, un), BF16)]
    elif epilogue == "swiglu_bwd":
        out_specs = [pair_spec]
        out_shape = [jax.ShapeDtypeStruct(extra.shape, BF16)]
    elif epilogue == "rms_bwd":
        out_specs = [row_spec, vec_spec]
        out_shape = [jax.ShapeDtypeStruct((m, n), F32), jax.ShapeDtypeStruct((1, n), F32)]
    elif blocked_out:
        out_specs = [pl.BlockSpec((cn, tm, un), lambda i, j, kk: (j, i, 0))]
        out_shape = [jax.ShapeDtypeStruct((n // un, m, un), out_dtype)]
    else:
        out_specs = [pl.BlockSpec((tm, tn), lambda i, j, kk: (i, j))]
        out_shape = [jax.ShapeDtypeStruct((m, n), out_dtype)]
    in_specs, args = [a_spec, b_spec], [a, b]
    if res is not None:
        in_specs.append(pl.BlockSpec((tm, tn), lambda i, j, kk: (i, j)))
        args.append(res)
    if epilogue == "swiglu_bwd":
        in_specs.append(pair_spec)
    elif epilogue == "rms_bwd":
        in_specs += [row_spec, vec_spec, row_spec]
    args += extras
    if norm_g is not None:
        in_specs.append(pl.BlockSpec((1, D_MODEL), lambda i, j, kk: (0, 0)))
        args.append(norm_g)
    semantics = ("arbitrary",) * 3 if epilogue == "rms_bwd" else ("parallel", "parallel", "arbitrary")
    out = _call(body, name, (m // tm, n // tn, nk), in_specs, out_specs, out_shape,
                [pltpu.VMEM((cm * cn, um, un), F32)], semantics, args, rider)
    return out if epilogue in ("swiglu", "rms_bwd") else out[0]


def _seg_mean(v, bd):
    hi, lo = _split2(v)
    return (_dot(hi, bd) + _dot(lo, bd)) * (1.0 / ATT_DH)


def _hn_bwd_math(xv, gv, bdv, dyv, scale):
    rstd = lax.rsqrt(_seg_mean(xv * xv, bdv) + EPS)
    xh = xv * rstd
    dyn = dyv * scale
    dyg = dyn * gv
    dx = rstd * (dyg - xh * _seg_mean(dyg * xh, bdv))
    return dx, jnp.sum(dyn * xh, axis=0, keepdims=True)


def _q_hnorm(x, g_tiled, bd, scale, name):
    t, d = x.shape
    tm = _pick(t, 512, 16)

    def body(x_ref, g_ref, bd_ref, o_ref):
        xv = x_ref[...]
        rstd = lax.rsqrt(_seg_mean(xv * xv, bd_ref[...]) + EPS)
        o_ref[...] = (xv * rstd * g_ref[...] * scale).astype(BF16)

    return pl.pallas_call(
        body, name=name, grid=(t // tm,),
        in_specs=[pl.BlockSpec((tm, d), lambda i: (i, 0)), pl.BlockSpec((1, d), lambda i: (0, 0)),
                  pl.BlockSpec((d, d), lambda i: (0, 0))],
        out_specs=pl.BlockSpec((tm, d), lambda i: (i, 0)),
        out_shape=jax.ShapeDtypeStruct((t, d), BF16),
        compiler_params=_params("parallel"),
    )(x, g_tiled, bd)


def _q_dhnorm(x, g_tiled, bd, dy, scale, name):
    t, d = x.shape
    tm = _pick(t, 512, 16)

    def body(x_ref, g_ref, bd_ref, dy_ref, dx_ref, dg_ref):
        dx, part = _hn_bwd_math(x_ref[...], g_ref[...], bd_ref[...], dy_ref[...], scale)
        dx_ref[...] = dx.astype(BF16)
        _accumulate(dg_ref, part, pl.program_id(0))

    row = pl.BlockSpec((tm, d), lambda i: (i, 0))
    vec = pl.BlockSpec((1, d), lambda i: (0, 0))
    return pl.pallas_call(
        body, name=name, grid=(t // tm,),
        in_specs=[row, vec, pl.BlockSpec((d, d), lambda i: (0, 0)), row],
        out_specs=[row, vec],
        out_shape=[jax.ShapeDtypeStruct((t, d), BF16), jax.ShapeDtypeStruct((1, d), F32)],
        compiler_params=_params("arbitrary"),
    )(x, g_tiled, bd, dy)


def _kv_prep(kv, g_tiled, bd, name):
    t = kv.shape[0]
    d = D_MODEL
    tm = K_PAD
    assert t % tm == 0

    def body(k_ref, v_ref, g_ref, bd_ref, kp_ref, vp_ref):
        i = pl.program_id(0)

        @pl.when(i == 0)
        def _():
            kp_ref[...] = jnp.zeros_like(kp_ref)
            vp_ref[...] = jnp.zeros_like(vp_ref)

        @pl.when(i > 0)
        def _():
            xv = k_ref[...]
            rstd = lax.rsqrt(_seg_mean(xv * xv, bd_ref[...]) + EPS)
            kp_ref[...] = (xv * rstd * g_ref[...]).astype(BF16)
            vp_ref[...] = v_ref[...].astype(BF16)

    shp = jax.ShapeDtypeStruct((t + K_PAD, d), BF16)
    out = pl.BlockSpec((tm, d), lambda i: (i, 0))
    return pl.pallas_call(
        body, name=name, grid=(t // tm + 1,),
        in_specs=[pl.BlockSpec((tm, d), lambda i: (jnp.maximum(i - 1, 0), 0)),
                  pl.BlockSpec((tm, d), lambda i: (jnp.maximum(i - 1, 0), 1)),
                  pl.BlockSpec((1, d), lambda i: (0, 0)), pl.BlockSpec((d, d), lambda i: (0, 0))],
        out_specs=[out, out], out_shape=[shp, shp],
        compiler_params=_params("arbitrary"),
    )(kv, kv, g_tiled, bd)


def _kv_dprep(kv, g_tiled, bd, dkp, dvp, name):
    t = kv.shape[0]
    d = D_MODEL
    tm = K_PAD

    def body(k_ref, g_ref, bd_ref, dk_ref, dv_ref, o_ref, dg_ref):
        dx, part = _hn_bwd_math(k_ref[...], g_ref[...], bd_ref[...], dk_ref[...], 1.0)
        o_ref[:, :d] = dx.astype(BF16)
        o_ref[:, d:] = dv_ref[...].astype(BF16)
        _accumulate(dg_ref, part, pl.program_id(0))

    vec = pl.BlockSpec((1, d), lambda i: (0, 0))
    padded = pl.BlockSpec((tm, d), lambda i: (i + 1, 0))
    return pl.pallas_call(
        body, name=name, grid=(t // tm,),
        in_specs=[pl.BlockSpec((tm, d), lambda i: (i, 0)), vec, pl.BlockSpec((d, d), lambda i: (0, 0)),
                  padded, padded],
        out_specs=[pl.BlockSpec((tm, 2 * d), lambda i: (i, 0)), vec],
        out_shape=[jax.ShapeDtypeStruct((t, 2 * d), BF16), jax.ShapeDtypeStruct((1, d), F32)],
        compiler_params=_params("arbitrary"),
    )(kv, g_tiled, bd, dkp, dvp)


def _loss_head(y, target, name):
    t, d = y.shape
    tm = _pick(t, 512, 16)

    def body(y_ref, t_ref, dy_ref, l_ref):
        diff = y_ref[...] - t_ref[...]
        dy_ref[...] = diff * (1.0 / d)
        part = jnp.sum(jnp.sum(diff * diff, axis=-1, keepdims=True), axis=0, keepdims=True) * (0.5 / d)
        _accumulate(l_ref, part, pl.program_id(0))

    row = pl.BlockSpec((tm, d), lambda i: (i, 0))
    return pl.pallas_call(
        body, name=name, grid=(t // tm,),
        in_specs=[row, row], out_specs=[row, pl.BlockSpec((1, 1), lambda i: (0, 0))],
        out_shape=[jax.ShapeDtypeStruct((t, d), F32), jax.ShapeDtypeStruct((1, 1), F32)],
        compiler_params=_params("arbitrary"),
    )(y, target)


def _ret_consts(t):
    h = np.arange(RET_HEADS, dtype=np.float32)
    lg = np.log(np.float32(1.0) - np.float32(2.0) ** (np.float32(-5.0) - h)).astype(np.float32)
    tt = np.arange(CHUNK, dtype=np.float32)
    intra = np.exp(lg[:, None, None] * np.abs(tt[:, None] - tt[None, :])).astype(np.float32)
    q_dec = np.exp(lg[:, None] * (tt + 1.0)).astype(np.float32)
    k_dec = np.exp(lg[:, None] * (CHUNK - 1.0 - tt)).astype(np.float32)
    s_dec = [float(v) for v in np.exp(lg * np.float32(CHUNK)).astype(np.float32)]
    qd = np.broadcast_to(q_dec[:, :, None], (RET_HEADS, CHUNK, RET_DK)).copy()
    kd = np.broadcast_to(k_dec[:, :, None], (RET_HEADS, CHUNK, RET_DK)).copy()
    half = RET_DK // 2
    inv_freq = ROPE_BASE ** (-jnp.arange(half, dtype=F32) / half)
    ang = jnp.arange(t).astype(F32)[:, None] * inv_freq[None, :]
    return jnp.asarray(intra), jnp.asarray(qd), jnp.asarray(kd), s_dec, jnp.cos(ang), jnp.sin(ang)


def _rope(x, cos, sin):
    half = RET_DK // 2
    x1, x2 = x[:, :half], x[:, half:]
    return jnp.concatenate([x1 * cos - x2 * sin, x1 * sin + x2 * cos], axis=-1)


def _unrope(d, cos, sin):
    half = RET_DK // 2
    d1, d2 = d[:, :half], d[:, half:]
    return jnp.concatenate([d1 * cos + d2 * sin, d2 * cos - d1 * sin], axis=-1)


def _ret_slices(h):
    q = slice(h * RET_DK, (h + 1) * RET_DK)
    k = slice(RET_Q_COLS + h * RET_DK, RET_Q_COLS + (h + 1) * RET_DK)
    v = slice(2 * RET_Q_COLS + h * RET_DV, 2 * RET_Q_COLS + (h + 1) * RET_DV)
    g = slice(2 * RET_Q_COLS + RET_V_COLS + h * RET_DV, 2 * RET_Q_COLS + RET_V_COLS + (h + 1) * RET_DV)
    o = slice(h * RET_DV, (h + 1) * RET_DV)
    return q, k, v, g, o


def _ret_fwd(proj, gn, consts, name, rider=None):
    t, cols = proj.shape
    n = t // CHUNK
    intra, qd, kd, s_dec, cos, sin = consts
    k_scale = RET_DK ** -0.5

    def body(p_ref, cos_ref, sin_ref, intra_ref, qd_ref, kd_ref, gn_ref, y_ref, o_ref, st_ref, state):
        i = pl.program_id(0)

        @pl.when(i == 0)
        def _():
            state[...] = jnp.zeros_like(state)

        cosv, sinv = cos_ref[...], sin_ref[...]
        for h in range(RET_HEADS):
            qs, ks, vs, gs, os_ = _ret_slices(h)
            qr = _rope(p_ref[:, qs], cosv, sinv)
            kr = _rope(p_ref[:, ks], cosv, sinv) * k_scale
            vb = p_ref[:, vs].astype(BF16)
            gv = p_ref[:, gs]
            scores = _dot_nt(qr.astype(BF16), kr.astype(BF16)) * intra_ref[h]
            s_old = state[h]
            s_old_b = s_old.astype(BF16)
            st_ref[0, h] = s_old_b
            o = _dot(scores.astype(BF16), vb) + _dot((qr * qd_ref[h]).astype(BF16), s_old_b)
            state[h] = s_old * s_dec[h] + _dot_tn((kr * kd_ref[h]).astype(BF16), vb)
            rstd = lax.rsqrt(jnp.mean(o * o, axis=-1, keepdims=True) + EPS)
            on = o * rstd * gn_ref[:, os_]
            o_ref[:, os_] = o
            y_ref[:, os_] = (gv * _sigmoid(gv) * on).astype(BF16)

    full3 = lambda a: pl.BlockSpec(a.shape, lambda i: (0, 0, 0))
    return _call(
        body, name, (n,),
        [pl.BlockSpec((CHUNK, cols), lambda i: (i, 0)),
         pl.BlockSpec((CHUNK, RET_DK // 2), lambda i: (i, 0)),
         pl.BlockSpec((CHUNK, RET_DK // 2), lambda i: (i, 0)),
         full3(intra), full3(qd), full3(kd),
         pl.BlockSpec((1, RET_V_COLS), lambda i: (0, 0))],
        [pl.BlockSpec((CHUNK, RET_V_COLS), lambda i: (i, 0)),
         pl.BlockSpec((CHUNK, RET_V_COLS), lambda i: (i, 0)),
         pl.BlockSpec((1, RET_HEADS, RET_DK, RET_DV), lambda i: (i, 0, 0, 0))],
        [jax.ShapeDtypeStruct((t, RET_V_COLS), BF16),
         jax.ShapeDtypeStruct((t, RET_V_COLS), F32),
         jax.ShapeDtypeStruct((n, RET_HEADS, RET_DK, RET_DV), BF16)],
        [pltpu.VMEM((RET_HEADS, RET_DK, RET_DV), F32)], ("arbitrary",),
        (proj, cos, sin, intra, qd, kd, gn), rider)


def _ret_bwd(proj, gn, o_saved, states, dy, consts, name, rider=None):
    t, cols = proj.shape
    n = t // CHUNK
    intra, qd, kd, s_dec, cos, sin = consts
    k_scale = RET_DK ** -0.5

    def body(p_ref, cos_ref, sin_ref, intra_ref, qd_ref, kd_ref, gn_ref, o_ref, st_ref, dy_ref,
             dp_ref, dgn_ref, dstate):
        i = pl.program_id(0)

        @pl.when(i == 0)
        def _():
            dstate[...] = jnp.zeros_like(dstate)

        cosv, sinv = cos_ref[...], sin_ref[...]
        dgn_parts = []
        for h in range(RET_HEADS):
            qs, ks, vs, gs, os_ = _ret_slices(h)
            qr = _rope(p_ref[:, qs], cosv, sinv)
            kr = _rope(p_ref[:, ks], cosv, sinv) * k_scale
            qb, kb = qr.astype(BF16), kr.astype(BF16)
            vb = p_ref[:, vs].astype(BF16)
            gv = p_ref[:, gs]
            ov = o_ref[:, os_]
            dyv = dy_ref[:, os_]
            gnv = gn_ref[:, os_]
            sg = _sigmoid(gv)
            rstd = lax.rsqrt(jnp.mean(ov * ov, axis=-1, keepdims=True) + EPS)
            oh = ov * rstd
            d_on = dyv * (gv * sg)
            dg = dyv * (oh * gnv) * (sg * (1.0 + gv * (1.0 - sg)))
            dgn_parts.append(jnp.sum(d_on * oh, axis=0, keepdims=True))
            d_oh = d_on * gnv
            do = rstd * (d_oh - oh * jnp.mean(d_oh * oh, axis=-1, keepdims=True))
            dob = do.astype(BF16)
            mask = intra_ref[h]
            a_b = (_dot_nt(qb, kb) * mask).astype(BF16)
            da_b = (_dot_nt(dob, vb) * mask).astype(BF16)
            ds_new = dstate[h]
            ds_new_b = ds_new.astype(BF16)
            s_old_b = st_ref[0, h]
            qdv, kdv = qd_ref[h], kd_ref[h]
            dv = _dot_tn(a_b, dob) + _dot((kr * kdv).astype(BF16), ds_new_b)
            dqr = _dot(da_b, kb) + _dot_nt(dob, s_old_b) * qdv
            dkr = _dot_tn(da_b, qb) + _dot_nt(vb, ds_new_b) * kdv
            dstate[h] = ds_new * s_dec[h] + _dot_tn((qr * qdv).astype(BF16), dob)
            dp_ref[:, qs] = _unrope(dqr, cosv, sinv).astype(BF16)
            dp_ref[:, ks] = _unrope(dkr * k_scale, cosv, sinv).astype(BF16)
            dp_ref[:, vs] = dv.astype(BF16)
            dp_ref[:, gs] = dg.astype(BF16)
        _accumulate(dgn_ref, jnp.concatenate(dgn_parts, axis=-1), i)

    rev = lambda i: (n - 1 - i, 0)
    full3 = lambda a: pl.BlockSpec(a.shape, lambda i: (0, 0, 0))
    return _call(
        body, name, (n,),
        [pl.BlockSpec((CHUNK, cols), rev),
         pl.BlockSpec((CHUNK, RET_DK // 2), rev),
         pl.BlockSpec((CHUNK, RET_DK // 2), rev),
         full3(intra), full3(qd), full3(kd),
         pl.BlockSpec((1, RET_V_COLS), lambda i: (0, 0)),
         pl.BlockSpec((CHUNK, RET_V_COLS), rev),
         pl.BlockSpec((1, RET_HEADS, RET_DK, RET_DV), lambda i: (n - 1 - i, 0, 0, 0)),
         pl.BlockSpec((CHUNK, RET_V_COLS), rev)],
        [pl.BlockSpec((CHUNK, cols), rev),
         pl.BlockSpec((1, RET_V_COLS), lambda i: (0, 0))],
        [jax.ShapeDtypeStruct((t, cols), BF16),
         jax.ShapeDtypeStruct((1, RET_V_COLS), F32)],
        [pltpu.VMEM((RET_HEADS, RET_DK, RET_DV), F32)], ("arbitrary",),
        (proj, cos, sin, intra, qd, kd, gn, o_saved, states, dy), rider)


def _att_common(q_ref, kp_ref, vp_ref):
    blk = pl.program_id(1)
    start = pl.multiple_of(blk * Q_BLOCK, Q_BLOCK)
    kw = kp_ref[pl.ds(start, K_WINDOW), :]
    vw = vp_ref[pl.ds(start, K_WINDOW), :]
    kvalid = blk * Q_BLOCK - K_PAD + lax.broadcasted_iota(jnp.int32, (1, K_WINDOW), 1) >= 0
    lane = lax.broadcasted_iota(jnp.int32, (1, LANES), 1)
    return start, q_ref[...], kw, vw, kvalid, (lane < ATT_DH, lane >= ATT_DH)


def _row_groups():
    return [slice(r * ATT_ROWS, (r + 1) * ATT_ROWS) for r in range(Q_BLOCK // ATT_ROWS)]


def _lane_copies(x):
    return jnp.tile(x, (1, K_WINDOW // LANES))


def _att_specs(t, tp):
    qspec = pl.BlockSpec((Q_BLOCK, LANES), lambda h, i: (i, h))
    kspec = pl.BlockSpec((tp, LANES), lambda h, i: (0, h))
    bspec = pl.BlockSpec((2, Q_BLOCK, K_WINDOW), lambda h, i: (h, 0, 0))
    return qspec, kspec, bspec


def _att_fwd(q, kp, vp, bias, name, rider=None):
    t, d = q.shape
    tp = kp.shape[0]

    def body(q_ref, kp_ref, vp_ref, bias_ref, o_ref, lse_ref, s_scr, p_scr, lse_scr):
        _, q2, kw, vw, kvalid, sel = _att_common(q_ref, kp_ref, vp_ref)
        for hh in range(2):
            s_scr[hh] = _dot_nt(jnp.where(sel[hh], q2, 0), kw)
        for hh in range(2):
            for rows in _row_groups():
                s = jnp.where(kvalid, s_scr[hh, rows, :] + bias_ref[hh, rows, :], NEG)
                m = jnp.max(s, axis=-1, keepdims=True)
                e = jnp.exp(s - m)
                l = jnp.sum(e, axis=-1, keepdims=True)
                p_scr[hh, rows, :] = (e * (1.0 / l)).astype(BF16)
                lse_scr[hh, rows, :] = jnp.broadcast_to(m + jnp.log(l), (ATT_ROWS, LANES))
        outs = [_dot(p_scr[hh], vw) for hh in range(2)]
        o_ref[...] = jnp.where(sel[0], outs[0], outs[1]).astype(BF16)
        lse_ref[...] = jnp.where(sel[0], lse_scr[0], lse_scr[1])

    qspec, kspec, bspec = _att_specs(t, tp)
    return _call(body, name, (d // LANES, t // Q_BLOCK), [qspec, kspec, kspec, bspec], [qspec, qspec],
                 [jax.ShapeDtypeStruct((t, d), BF16), jax.ShapeDtypeStruct((t, d), F32)],
                 [pltpu.VMEM((2, Q_BLOCK, K_WINDOW), F32), pltpu.VMEM((2, Q_BLOCK, K_WINDOW), BF16),
                  pltpu.VMEM((2, Q_BLOCK, LANES), F32)],
                 ("parallel", "arbitrary"), (q, kp, vp, bias), rider)


def _att_bwd(q, kp, vp, bias, do, o, lse, name, rider=None):
    t, d = q.shape
    tp = kp.shape[0]

    def body(q_ref, kp_ref, vp_ref, bias_ref, do_ref, o_ref, lse_ref, dq_ref, dkp_ref, dvp_ref, db_ref,
             s_scr, dp_scr, p_scr, ds_scr, row_scr):
        @pl.when(pl.program_id(1) == 0)
        def _():
            dkp_ref[...] = jnp.zeros_like(dkp_ref)
            dvp_ref[...] = jnp.zeros_like(dvp_ref)
            db_ref[...] = jnp.zeros_like(db_ref)

        start, q2, kw, vw, kvalid, sel = _att_common(q_ref, kp_ref, vp_ref)
        do2 = do_ref[...]
        qm = [jnp.where(sel[hh], q2, 0) for hh in range(2)]
        dom = [jnp.where(sel[hh], do2, 0) for hh in range(2)]
        do_o = do2.astype(F32) * o_ref[...].astype(F32)
        lse2 = lse_ref[...]
        for hh in range(2):
            s_scr[hh] = _dot_nt(qm[hh], kw)
            dp_scr[hh] = _dot_nt(dom[hh], vw)
            lse_h = jnp.max(jnp.where(sel[hh], lse2, NEG), axis=-1, keepdims=True)
            delta = jnp.sum(jnp.where(sel[hh], do_o, 0.0), axis=-1, keepdims=True)
            row_scr[hh, 0] = jnp.broadcast_to(lse_h, (Q_BLOCK, LANES))
            row_scr[hh, 1] = jnp.broadcast_to(delta, (Q_BLOCK, LANES))
        for hh in range(2):
            for rows in _row_groups():
                s = jnp.where(kvalid, s_scr[hh, rows, :] + bias_ref[hh, rows, :], NEG)
                p = jnp.exp(s - _lane_copies(row_scr[hh, 0, rows, :]))
                ds = p * (dp_scr[hh, rows, :] - _lane_copies(row_scr[hh, 1, rows, :]))
                db_ref[hh, rows, :] += ds
                p_scr[hh, rows, :] = p.astype(BF16)
                ds_scr[hh, rows, :] = ds.astype(BF16)
        dqs = [_dot(ds_scr[hh], kw) for hh in range(2)]
        dq_ref[...] = jnp.where(sel[0], dqs[0], dqs[1])
        dkp_ref[:, pl.ds(start, K_WINDOW)] += _dot_tn(qm[0], ds_scr[0]) + _dot_tn(qm[1], ds_scr[1])
        dvp_ref[:, pl.ds(start, K_WINDOW)] += _dot_tn(dom[0], p_scr[0]) + _dot_tn(dom[1], p_scr[1])

    qspec, kspec, bspec = _att_specs(t, tp)
    tspec = pl.BlockSpec((LANES, tp), lambda h, i: (h, 0))
    stage = lambda dt: pltpu.VMEM((2, Q_BLOCK, K_WINDOW), dt)
    return _call(body, name, (d // LANES, t // Q_BLOCK),
                 [qspec, kspec, kspec, bspec, qspec, qspec, qspec],
                 [qspec, tspec, tspec, bspec],
                 [jax.ShapeDtypeStruct((t, d), F32),
                  jax.ShapeDtypeStruct((d, tp), F32),
                  jax.ShapeDtypeStruct((d, tp), F32),
                  jax.ShapeDtypeStruct((ATT_HEADS, Q_BLOCK, K_WINDOW), F32)],
                 [stage(F32), stage(F32), stage(BF16), stage(BF16),
                  pltpu.VMEM((2, 2, Q_BLOCK, LANES), F32)],
                 ("parallel", "arbitrary"), (q, kp, vp, bias, do, o, lse), rider)


def _rel_bin_matrix():
    rows = REL_DELTAS * 2 * REL_BLK
    rho = lax.broadcasted_iota(jnp.int32, (rows, REL_PAD), 0)
    col = lax.broadcasted_iota(jnp.int32, (rows, REL_PAD), 1)
    assert 2 * REL_BLK == 256
    delta = rho >> 8
    c = 255 - (rho & 255)
    dist = K_PAD + REL_BLK * (delta - (K_WINDOW // REL_BLK - 1)) + (c - (REL_BLK - 1))
    idx = jnp.clip(dist, -REL_CLIP, REL_CLIP) + REL_CLIP
    return col == idx


def _rel_expand(rel_pad, name):
    heads = rel_pad.shape[0]
    rows = REL_DELTAS * 2 * REL_BLK

    def body_bin(r_ref, o_ref):
        onehot = jnp.where(_rel_bin_matrix(), 1.0, 0.0).astype(BF16)
        hi, mid, lo = _split3(r_ref[...])
        o_ref[...] = _dot_nt(hi, onehot) + _dot_nt(mid, onehot) + _dot_nt(lo, onehot)

    by_delta = pl.pallas_call(
        body_bin, name=name + "_bin",
        out_shape=jax.ShapeDtypeStruct((heads, rows), F32),
        compiler_params=pltpu.CompilerParams(vmem_limit_bytes=VMEM_LIMIT_V7X),
    )(rel_pad)
    by_delta = by_delta.reshape(heads * REL_DELTAS, 2 * REL_BLK)

    def body_shift(t_ref, o_ref):
        tv = t_ref[...]
        for r in range(REL_BLK):
            o_ref[r] = pltpu.roll(tv, (r + REL_BLK) % (2 * REL_BLK), 1)[:, :REL_BLK]

    return pl.pallas_call(
        body_shift, name=name + "_shift",
        out_shape=jax.ShapeDtypeStruct((REL_BLK, heads * REL_DELTAS, REL_BLK), F32),
        compiler_params=pltpu.CompilerParams(vmem_limit_bytes=VMEM_LIMIT_V7X),
    )(by_delta)


def _bias_table(rel_bias, name):
    heads = rel_bias.shape[0]
    rel_pad = jnp.pad(rel_bias, ((0, 0), (0, REL_PAD - REL_TABLE)))
    tiles = _rel_expand(rel_pad, name)
    tiles = tiles.reshape(REL_BLK, heads, REL_DELTAS, REL_BLK).transpose(1, 2, 0, 3)
    na, nb = Q_BLOCK // REL_BLK, K_WINDOW // REL_BLK
    rows = [jnp.concatenate([tiles[:, a - b + nb - 1] for b in range(nb)], axis=-1) for a in range(na)]
    table = jnp.concatenate(rows, axis=-2)
    qc = np.arange(Q_BLOCK)[:, None] // CHUNK
    kc = np.arange(K_WINDOW)[None, :] // CHUNK
    band = (kc >= qc) & (kc <= qc + PAST_CHUNKS)
    return jnp.where(jnp.asarray(band)[None], table, NEG)


def _rel_reduce(db, name):
    heads = db.shape[0]
    na, nb = Q_BLOCK // REL_BLK, K_WINDOW // REL_BLK

    def body_fold(db_ref, g_ref):
        for delta in range(REL_DELTAS):
            acc = None
            for a in range(na):
                b = a - (delta - (nb - 1))
                if 0 <= b < nb:
                    tile = db_ref[0, a * REL_BLK:(a + 1) * REL_BLK, b * REL_BLK:(b + 1) * REL_BLK]
                    acc = tile if acc is None else acc + tile
            g_ref[0, delta] = acc

    folded = pl.pallas_call(
        body_fold, name=name + "_fold", grid=(heads,),
        in_specs=[pl.BlockSpec((1, Q_BLOCK, K_WINDOW), lambda h: (h, 0, 0))],
        out_specs=pl.BlockSpec((1, REL_DELTAS, REL_BLK, REL_BLK), lambda h: (h, 0, 0, 0)),
        out_shape=jax.ShapeDtypeStruct((heads, REL_DELTAS, REL_BLK, REL_BLK), F32),
        compiler_params=_params("parallel"),
    )(db)
    by_row = folded.transpose(2, 0, 1, 3).reshape(REL_BLK, heads * REL_DELTAS, REL_BLK)

    def body_diag(g_ref, d_ref):
        zeros = jnp.zeros((heads * REL_DELTAS, REL_BLK), F32)
        acc = None
        for r in range(REL_BLK):
            part = pltpu.roll(jnp.concatenate([g_ref[r], zeros], axis=1), REL_BLK - r, 1)
            acc = part if acc is None else acc + part
        d_ref[...] = acc

    diag = pl.pallas_call(
        body_diag, name=name + "_diag",
        out_shape=jax.ShapeDtypeStruct((heads * REL_DELTAS, 2 * REL_BLK), F32),
        compiler_params=pltpu.CompilerParams(vmem_limit_bytes=VMEM_LIMIT_V7X),
    )(by_row)
    diag = diag.reshape(heads, REL_DELTAS * 2 * REL_BLK)

    def body_bin(d_ref, o_ref):
        onehot = jnp.where(_rel_bin_matrix(), 1.0, 0.0).astype(BF16)
        hi, mid, lo = _split3(d_ref[...])
        o_ref[...] = _dot(hi, onehot) + _dot(mid, onehot) + _dot(lo, onehot)

    out = pl.pallas_call(
        body_bin, name=name + "_bin",
        out_shape=jax.ShapeDtypeStruct((heads, REL_PAD), F32),
        compiler_params=pltpu.CompilerParams(vmem_limit_bytes=VMEM_LIMIT_V7X),
    )(diag)
    return out[:, :REL_TABLE]


def _sum_leading(x, name):
    n, r, c = x.shape
    tr = _pick(r, 256, 8)

    def body(x_ref, o_ref):
        acc = x_ref[0].astype(F32)
        for k in range(1, n):
            acc = acc + x_ref[k].astype(F32)
        o_ref[...] = acc

    return pl.pallas_call(
        body, name=name, grid=(r // tr,),
        in_specs=[pl.BlockSpec((n, tr, c), lambda i: (0, i, 0))],
        out_specs=pl.BlockSpec((tr, c), lambda i: (i, 0)),
        out_shape=jax.ShapeDtypeStruct((r, c), F32),
        compiler_params=_params("parallel"),
    )(x)


def _pair_add(g, recv, parity, name):
    _, r, c = g.shape
    tr = _pick(r, 256, 16)

    def body(par_ref, g_ref, r_ref, o_ref):
        o_ref[...] = (g_ref[...].astype(F32) + r_ref[...].astype(F32)).astype(BF16)

    return pl.pallas_call(
        body, name=name,
        grid_spec=pltpu.PrefetchScalarGridSpec(
            num_scalar_prefetch=1, grid=(4, r // tr),
            in_specs=[pl.BlockSpec((1, tr, c), lambda k, i, par: (2 * k + par[0], i, 0)),
                      pl.BlockSpec((1, tr, c), lambda k, i, par: (k, i, 0))],
            out_specs=pl.BlockSpec((1, tr, c), lambda k, i, par: (k, i, 0))),
        out_shape=jax.ShapeDtypeStruct((4, r, c), BF16),
        compiler_params=_params("parallel", "parallel"),
    )(parity, g, recv)


def _adamw(w, g_parts, m, v, name):
    r, c = w.shape
    n = g_parts.shape[0]
    tr = _pick(r, 256, 16 if g_parts.dtype == BF16 else 8)
    c1 = 1.0 - ADAM_B1 ** ADAM_STEP
    c2 = 1.0 - ADAM_B2 ** ADAM_STEP

    def body(w_ref, g_ref, m_ref, v_ref, go_ref, d_ref, nm_ref, nv_ref):
        gv = g_ref[0].astype(F32)
        for k in range(1, n):
            gv = gv + g_ref[k].astype(F32)
        nm = ADAM_B1 * m_ref[...] + (1.0 - ADAM_B1) * gv
        nv = ADAM_B2 * v_ref[...] + (1.0 - ADAM_B2) * (gv * gv)
        go_ref[...] = gv
        d_ref[...] = -ADAM_LR * ((nm / c1) / (jnp.sqrt(nv / c2) + ADAM_EPS) + ADAM_WD * w_ref[...])
        nm_ref[...] = nm
        nv_ref[...] = nv

    spec = pl.BlockSpec((tr, c), lambda i: (i, 0))
    shp = jax.ShapeDtypeStruct((r, c), F32)
    return pl.pallas_call(
        body, name=name, grid=(r // tr,),
        in_specs=[spec, pl.BlockSpec((n, tr, c), lambda i: (0, i, 0)), spec, spec],
        out_specs=[spec] * 4, out_shape=[shp] * 4,
        compiler_params=_params("parallel"),
    )(w, g_parts, m, v)


BIG = (("a_w_in", 1), ("a_w_o", 0), ("a_w_gu", 1), ("a_w_down", 0), ("w_kv", 1),
       ("b_w_q", 0), ("b_w_o", 0), ("b_w_gu", 1), ("b_w_down", 0))

SMALL = (("a_norm_g", D_MODEL, True), ("a_gn_g", RET_V_COLS, True), ("a_ffn_norm_g", D_MODEL, True),
         ("kv_norm_g", D_MODEL, False), ("b_norm_g", D_MODEL, False), ("b_ffn_norm_g", D_MODEL, False),
         ("k_norm_g", ATT_DH, False), ("b_q_norm_g", ATT_DH, False),
         ("b_rel_bias", ATT_HEADS * REL_TABLE, False))
SMALL_ROWS, SMALL_COLS = 16, 1024


def _pack_small(vals):
    flat = jnp.concatenate([vals[n].reshape(-1) for n, _, _ in SMALL])
    return jnp.pad(flat, (0, SMALL_ROWS * SMALL_COLS - flat.shape[0])).reshape(SMALL_ROWS, SMALL_COLS)


def _unpack_small(packed, local):
    flat, out, pos = packed.reshape(-1), {}, 0
    for n, length, sharded in SMALL:
        ln = length // N_DEV if (local and sharded) else length
        out[n] = flat[pos:pos + ln]
        pos += ln
    return out


def _gather_rider(shards, names):
    return _GatherRider([shards[n] for n in names])


def _gathered(rider, names, axis_of):
    return {n: (r.reshape(-1, r.shape[2]) if axis_of[n] == 0 else r) for n, r in zip(names, rider.results)}


def _blocks(g):
    return g if g.ndim == 3 else g.reshape(N_DEV, -1, g.shape[-1])


def _local_step(x, target, shards, w_in, s, parity):
    t = x.shape[0]
    axis_of = dict(BIG)
    consts = _ret_consts(t)
    bd = jnp.asarray(np.kron(np.eye(ATT_HEADS, dtype=np.float32),
                             np.ones((ATT_DH, ATT_DH), np.float32))).astype(BF16)
    kg_t = jnp.tile(s["k_norm_g"], (1, ATT_HEADS))
    qg_t = jnp.tile(s["b_q_norm_g"], (1, ATT_HEADS))
    q_scale = ATT_DH ** -0.5
    w = {"a_w_in": w_in}
    g, recv = {}, {}

    def gather_on(names):
        return _gather_rider(shards, names), names

    def landed(ride):
        w.update(_gathered(ride[0], ride[1], axis_of))

    def scatter_on(names):
        return _ScatterRider([_blocks(g[n]) for n in names]), names

    def reduced(ride):
        recv.update(zip(ride[1], ride[0].results))

    ride = gather_on(["a_w_o", "a_w_down"])
    proj = _mm(x, w["a_w_in"], "nn", "a_proj", norm_g=s["a_norm_g"], rider=ride[0])
    landed(ride)
    ride = gather_on(["a_w_gu", "w_kv"])
    y, o_ret, states = _ret_fwd(proj, s["a_gn_g"], consts, "a_ret", rider=ride[0])
    landed(ride)
    x1 = _mm(y, w["a_w_o"], "nn", "a_out", res=x)
    ride = gather_on(["b_w_q", "b_w_o", "b_w_down"])
    gu_a, act_a = _mm(x1, w["a_w_gu"], "nn", "a_ffn_gu", epilogue="swiglu", norm_g=s["a_ffn_norm_g"],
                      rider=ride[0])
    landed(ride)
    x2 = _mm(act_a, w["a_w_down"], "nn", "a_ffn_down", res=x1)

    kv = _mm(x2, w["w_kv"], "nn", "kv_proj", norm_g=s["kv_norm_g"])
    kp, vp = _kv_prep(kv, kg_t, bd, "kv_prep")

    q_raw = _mm(x2, w["b_w_q"], "nn", "b_q", norm_g=s["b_norm_g"])
    qn = _q_hnorm(q_raw, qg_t, bd, q_scale, "q_hnorm")
    bias = _bias_table(s["b_rel_bias"].reshape(ATT_HEADS, REL_TABLE), "rel")
    ride = gather_on(["b_w_gu"])
    o_att, lse = _att_fwd(qn, kp, vp, bias, "b_att", rider=ride[0])
    landed(ride)
    x3 = _mm(o_att, w["b_w_o"], "nn", "b_out", res=x2)
    gu_b, act_b = _mm(x3, w["b_w_gu"], "nn", "b_ffn_gu", epilogue="swiglu", norm_g=s["b_ffn_norm_g"])
    x4 = _mm(act_b, w["b_w_down"], "nn", "b_ffn_down", res=x3)

    dy, loss = _loss_head(x4, target, "loss")
    in_blk, kv_blk, ffn_blk = w["a_w_in"].shape[2], w["w_kv"].shape[2], w["b_w_gu"].shape[2]

    dgu = _mm(dy, w["b_w_down"], "nt", "b_ffn_dgu", out_block=ffn_blk, epilogue="swiglu_bwd", extra=gu_b)
    dgu = dgu.reshape(N_DEV, t, ffn_blk)
    g["b_w_down"] = _mm(act_b, dy, "tn", "b_ffn_gdown", out_dtype=BF16)
    ride = scatter_on(["b_w_down"])
    dx3, g["b_ffn_norm_g"] = _mm(dgu, w["b_w_gu"], "nt", "b_ffn_dh", epilogue="rms_bwd",
                                 extra=(x3, s["b_ffn_norm_g"], dy), rider=ride[0])
    reduced(ride)
    g["b_w_gu"] = _mm(x3, dgu, "tn", "b_ffn_ggu", out_dtype=BF16, out_block=ffn_blk, norm_g=s["b_ffn_norm_g"])

    do_att = _mm(dx3, w["b_w_o"], "nt", "b_dout", out_dtype=BF16)
    g["b_w_o"] = _mm(o_att, dx3, "tn", "b_gout", out_dtype=BF16)
    ride = scatter_on(["b_w_gu", "b_w_o"])
    dq, dkp, dvp, db = _att_bwd(qn, kp, vp, bias, do_att, o_att, lse, "b_datt", rider=ride[0])
    reduced(ride)
    g["b_rel_bias"] = _rel_reduce(db, "drel").reshape(1, -1)
    dq_raw, gq = _q_dhnorm(q_raw, qg_t, bd, dq, q_scale, "q_dhnorm")
    g["b_q_norm_g"] = gq.reshape(ATT_HEADS, ATT_DH).sum(axis=0, keepdims=True)
    g["b_w_q"] = _mm(x2, dq_raw, "tn", "b_gq", out_dtype=BF16, norm_g=s["b_norm_g"])
    dx2, g["b_norm_g"] = _mm(dq_raw, w["b_w_q"], "nt", "b_dq", epilogue="rms_bwd",
                             extra=(x2, s["b_norm_g"], dx3))

    dkv, gk = _kv_dprep(kv, kg_t, bd, dkp.T, dvp.T, "kv_dprep")
    g["k_norm_g"] = gk.reshape(ATT_HEADS, ATT_DH).sum(axis=0, keepdims=True)
    g["w_kv"] = _mm(x2, dkv, "tn", "kv_g", out_dtype=BF16, out_block=kv_blk, norm_g=s["kv_norm_g"])
    dx2, g["kv_norm_g"] = _mm(dkv, w["w_kv"], "nt", "kv_du", epilogue="rms_bwd",
                              extra=(x2, s["kv_norm_g"], dx2))

    ride = scatter_on(["b_w_q", "w_kv"])
    dgu = _mm(dx2, w["a_w_down"], "nt", "a_ffn_dgu", out_block=ffn_blk, epilogue="swiglu_bwd", extra=gu_a,
              rider=ride[0])
    reduced(ride)
    dgu = dgu.reshape(N_DEV, t, ffn_blk)
    g["a_w_down"] = _mm(act_a, dx2, "tn", "a_ffn_gdown", out_dtype=BF16)
    ride = scatter_on(["a_w_down"])
    dx1, g["a_ffn_norm_g"] = _mm(dgu, w["a_w_gu"], "nt", "a_ffn_dh", epilogue="rms_bwd",
                                 extra=(x1, s["a_ffn_norm_g"], dx2), rider=ride[0])
    reduced(ride)
    g["a_w_gu"] = _mm(x1, dgu, "tn", "a_ffn_ggu", out_dtype=BF16, out_block=ffn_blk, norm_g=s["a_ffn_norm_g"])

    dy_ret = _mm(dx1, w["a_w_o"], "nt", "a_dout")
    g["a_w_o"] = _mm(y, dx1, "tn", "a_gout", out_dtype=BF16)
    ride = scatter_on(["a_w_gu"])
    dproj, g["a_gn_g"] = _ret_bwd(proj, s["a_gn_g"], o_ret, states, dy_ret, consts, "a_dret", rider=ride[0])
    reduced(ride)
    ride = scatter_on(["a_w_o"])
    g["a_w_in"] = _mm(x, dproj, "tn", "a_gin", out_dtype=BF16, out_block=in_blk, norm_g=s["a_norm_g"],
                      rider=ride[0])
    reduced(ride)
    from_sibling = _exchange(_SiblingSwapRider([g["a_w_in"]]), "rs_sibling")[0]
    chip_sums = _pair_add(g["a_w_in"], from_sibling, parity, "rs_pair_add")
    last = _ChipScatterRider([chip_sums])
    grad_x, g["a_norm_g"] = _mm(dproj, w["a_w_in"], "nt", "a_dproj", epilogue="rms_bwd",
                                extra=(x, s["a_norm_g"], dx1), rider=last)
    recv["a_w_in"] = last.results[0]
    return loss, grad_x, recv, g


ARG_NAMES = ("x", "a_norm_g", "a_w_in", "a_gn_g", "a_w_o", "a_ffn_norm_g", "a_w_gu", "a_w_down",
             "kv_norm_g", "w_kv", "k_norm_g", "b_norm_g", "b_w_q", "b_q_norm_g", "b_rel_bias", "b_w_o",
             "b_ffn_norm_g", "b_w_gu", "b_w_down")
WEIGHT_NAMES = ARG_NAMES[1:]


def _big_shard(a):
    return a[0] if a.ndim == 3 else a


def kernel(x, a_norm_g, a_w_in, a_gn_g, a_w_o, a_ffn_norm_g, a_w_gu, a_w_down, kv_norm_g, w_kv, k_norm_g, b_norm_g, b_w_q, b_q_norm_g, b_rel_bias, b_w_o, b_ffn_norm_g, b_w_gu, b_w_down, loss_target, m_a_norm_g, m_a_w_in, m_a_gn_g, m_a_w_o, m_a_ffn_norm_g, m_a_w_gu, m_a_w_down, m_kv_norm_g, m_w_kv, m_k_norm_g, m_b_norm_g, m_b_w_q, m_b_q_norm_g, m_b_rel_bias, m_b_w_o, m_b_ffn_norm_g, m_b_w_gu, m_b_w_down, v_a_norm_g, v_a_w_in, v_a_gn_g, v_a_w_o, v_a_ffn_norm_g, v_a_w_gu, v_a_w_down, v_kv_norm_g, v_w_kv, v_k_norm_g, v_b_norm_g, v_b_w_q, v_b_q_norm_g, v_b_rel_bias, v_b_w_o, v_b_ffn_norm_g, v_b_w_gu, v_b_w_down):
    args = (x, a_norm_g, a_w_in, a_gn_g, a_w_o, a_ffn_norm_g, a_w_gu, a_w_down, kv_norm_g, w_kv, k_norm_g,
            b_norm_g, b_w_q, b_q_norm_g, b_rel_bias, b_w_o, b_ffn_norm_g, b_w_gu, b_w_down)
    p = dict(zip(ARG_NAMES, args))
    m_all = dict(zip(WEIGHT_NAMES, (m_a_norm_g, m_a_w_in, m_a_gn_g, m_a_w_o, m_a_ffn_norm_g, m_a_w_gu,
                                    m_a_w_down, m_kv_norm_g, m_w_kv, m_k_norm_g, m_b_norm_g, m_b_w_q,
                                    m_b_q_norm_g, m_b_rel_bias, m_b_w_o, m_b_ffn_norm_g, m_b_w_gu, m_b_w_down)))
    v_all = dict(zip(WEIGHT_NAMES, (v_a_norm_g, v_a_w_in, v_a_gn_g, v_a_w_o, v_a_ffn_norm_g, v_a_w_gu,
                                    v_a_w_down, v_kv_norm_g, v_w_kv, v_k_norm_g, v_b_norm_g, v_b_w_q,
                                    v_b_q_norm_g, v_b_rel_bias, v_b_w_o, v_b_ffn_norm_g, v_b_w_gu, v_b_w_down)))
    xi, yi, ci = _my_place()
    me = 4 * xi + 2 * yi + ci
    big_names = [n for n, _ in BIG]
    axis_of = dict(BIG)

    big_local = {n: _big_shard(p[n]) for n in big_names}
    shards = {n: a.astype(BF16) for n, a in big_local.items()}
    small_local = _pack_small({n: p[n] for n, _, _ in SMALL})
    w_in, small_all = _exchange(_GatherRider([shards["a_w_in"], small_local]), "gather_in")
    flat_g = small_all.reshape(N_DEV, -1)
    s_full, pos = {}, 0
    for n, length, sharded in SMALL:
        ln = length // N_DEV if sharded else length
        s_full[n] = flat_g[:, pos:pos + ln].reshape(1, -1) if sharded else p[n].reshape(1, -1)
        pos += ln

    parity = jnp.reshape(ci, (1,)).astype(jnp.int32)
    loss, grad_x, recv, g = _local_step(x[0], loss_target[0], shards, w_in, s_full, parity)
    loss = lax.psum(loss[0, 0], ("x", "y", "c"))

    g_small_all = _exchange(_GatherRider([_pack_small({n: g[n] for n, _, _ in SMALL})]), "gather_gsmall")[0]
    g_small = _unpack_small(_sum_leading(g_small_all, "gsmall_sum"), local=False)
    for n, length, sharded in SMALL:
        if sharded:
            g_small[n] = lax.dynamic_slice(g_small[n], (me * (length // N_DEV),), (length // N_DEV,))

    grads, deltas, new_m, new_v = {}, {}, {}, {}
    for n in big_names:
        outs = _adamw(big_local[n], recv[n], _big_shard(m_all[n]), _big_shard(v_all[n]), "adamw_" + n)
        grads[n], deltas[n], new_m[n], new_v[n] = (a.reshape(p[n].shape) for a in outs)
    pk = lambda src: _pack_small({n: src[n] for n, _, _ in SMALL})
    outs = _adamw(small_local, pk(g_small)[None], pk(m_all), pk(v_all), "adamw_small")
    g_s, d_s, nm_s, nv_s = (_unpack_small(a, local=True) for a in outs)
    for n, _, _ in SMALL:
        grads[n], deltas[n], new_m[n], new_v[n] = (a[n].reshape(p[n].shape) for a in (g_s, d_s, nm_s, nv_s))

    return (loss, grad_x[None], *[grads[n] for n in WEIGHT_NAMES], *[deltas[n] for n in WEIGHT_NAMES],
            *[new_m[n] for n in WEIGHT_NAMES], *[new_v[n] for n in WEIGHT_NAMES])
```

```python
import numpy as np
import jax
import jax.numpy as jnp
from jax import lax
from jax.experimental import pallas as pl
from jax.experimental.pallas import tpu as pltpu

F32 = jnp.float32
BF16 = jnp.bfloat16

N_DEV = 8
D_MODEL = 1024
CHUNK = 64
EPS = 1e-6
RET_HEADS, RET_DK, RET_DV = 4, 256, 512
RET_STEP = 2
RET_Q_COLS = RET_HEADS * RET_DK
RET_V_COLS = RET_HEADS * RET_DV
ATT_HEADS, ATT_DH = 16, 64
PAST_CHUNKS = 8
REL_CLIP = 256
REL_TABLE = 2 * REL_CLIP + 1
FFN_HIDDEN = 2816
ROPE_BASE = 10000.0
LANES = 128
Q_BLOCK = 256
ATT_ROWS = 32
K_PAD = PAST_CHUNKS * CHUNK
K_WINDOW = Q_BLOCK + K_PAD
REL_BLK = 128
REL_DELTAS = Q_BLOCK // REL_BLK + K_WINDOW // REL_BLK - 1
REL_PAD = 640
NEG = -1e30
VMEM_LIMIT_V7X = 56 * 1024 * 1024
ADAM_LR, ADAM_B1, ADAM_B2, ADAM_EPS, ADAM_WD, ADAM_STEP = 1e-3, 0.9, 0.999, 1e-8, 0.01, 10
MESH = pl.DeviceIdType.MESH
ANY = pl.BlockSpec(memory_space=pl.ANY)


def _params(*semantics):
    return pltpu.CompilerParams(dimension_semantics=semantics, vmem_limit_bytes=VMEM_LIMIT_V7X)


def _pick(dim, cap, align):
    best = None
    for t in range(align, min(dim, cap) + 1, align):
        if dim % t == 0:
            best = t
    assert best is not None, (dim, cap, align)
    return best


def _dot(a, b):
    return lax.dot_general(a, b, (((1,), (0,)), ((), ())), preferred_element_type=F32)


def _dot_nt(a, b):
    return lax.dot_general(a, b, (((1,), (1,)), ((), ())), preferred_element_type=F32)


def _dot_tn(a, b):
    return lax.dot_general(a, b, (((0,), (0,)), ((), ())), preferred_element_type=F32)


def _split2(x):
    hi = x.astype(BF16)
    lo = (x - hi.astype(F32)).astype(BF16)
    return hi, lo


def _split3(x):
    hi = x.astype(BF16)
    r = x - hi.astype(F32)
    mid = r.astype(BF16)
    lo = (r - mid.astype(F32)).astype(BF16)
    return hi, mid, lo


def _sigmoid(x):
    return 1.0 / (1.0 + jnp.exp(-x))


def _accumulate(ref, part, step):
    @pl.when(step == 0)
    def _():
        ref[...] = part

    @pl.when(step > 0)
    def _():
        ref[...] += part


def _my_place():
    return lax.axis_index("x"), lax.axis_index("y"), lax.axis_index("c")


def _flip(v, bit):
    return 1 - v if bit else v


class _GatherRider:
    def __init__(self, xs):
        self.inputs = list(xs)
        n = len(xs)
        self.out_shape = [jax.ShapeDtypeStruct((N_DEV,) + x.shape, x.dtype) for x in xs]
        self.scratch = [pltpu.SemaphoreType.DMA((7, n)), pltpu.SemaphoreType.DMA((7, n)),
                        pltpu.SemaphoreType.DMA((n,))]
        self.results = None

    def _copies(self, x_refs, out_refs, sems):
        send_sems, recv_sems, local_sems = sems
        n = len(x_refs)
        x, y, c = _my_place()
        me, sibling = (x, y, c), (x, y, 1 - c)
        chips = [(1 - x, y), (x, 1 - y), (1 - x, 1 - y)]

        def slot(a, px, py, pc):
            return out_refs[a].at[4 * px + 2 * py + pc]

        def copy(k, a, block, to, own=False):
            return pltpu.make_async_remote_copy(
                src_ref=x_refs[a] if own else slot(a, *block), dst_ref=slot(a, *block),
                send_sem=send_sems.at[k, a], recv_sem=recv_sems.at[k, a],
                device_id=to, device_id_type=MESH)

        mine = [pltpu.make_async_copy(x_refs[a], slot(a, *me), local_sems.at[a]) for a in range(n)]
        first = []
        for a in range(n):
            first.append(copy(0, a, me, sibling, own=True))
            first += [copy(1 + j, a, me, (*chip, c), own=True) for j, chip in enumerate(chips)]
        return n, c, me, sibling, chips, copy, mine, first

    def start(self, x_refs, out_refs, sems):
        _, _, _, _, _, _, mine, first = self._copies(x_refs, out_refs, sems)
        for cp in mine + first:
            cp.start()

    def finish(self, x_refs, out_refs, sems):
        n, c, me, sibling, chips, copy, mine, first = self._copies(x_refs, out_refs, sems)
        passed = []
        for j, chip in enumerate(chips):
            for a in range(n):
                copy(1 + j, a, (*chip, c), me).wait_recv()
                passed.append(copy(4 + j, a, (*chip, c), sibling))
                passed[-1].start()
        for a in range(n):
            copy(0, a, sibling, me).wait_recv()
            for j, chip in enumerate(chips):
                copy(4 + j, a, (*chip, 1 - c), me).wait_recv()
        for cp in first + passed:
            cp.wait_send()
        for cp in mine:
            cp.wait()


class _ScatterRider:
    def __init__(self, gs):
        self.inputs = list(gs)
        n = len(gs)
        self.out_shape = [jax.ShapeDtypeStruct(g.shape, g.dtype) for g in gs]
        self.scratch = [pltpu.SemaphoreType.DMA((7, n)), pltpu.SemaphoreType.DMA((7, n)),
                        pltpu.SemaphoreType.DMA((n,))]
        self.results = None

    def _copies(self, g_refs, out_refs, sems):
        send_sems, recv_sems, local_sems = sems
        x, y, c = _my_place()
        me = 4 * x + 2 * y + c
        mine, copies = [], []
        for a in range(len(g_refs)):
            mine.append(pltpu.make_async_copy(g_refs[a].at[me], out_refs[a].at[me], local_sems.at[a]))
            for k in range(1, N_DEV):
                px, py, pc = _flip(x, k & 4), _flip(y, k & 2), _flip(c, k & 1)
                copies.append(pltpu.make_async_remote_copy(
                    src_ref=g_refs[a].at[4 * px + 2 * py + pc], dst_ref=out_refs[a].at[me],
                    send_sem=send_sems.at[k - 1, a], recv_sem=recv_sems.at[k - 1, a],
                    device_id=(px, py, pc), device_id_type=MESH))
        return mine, copies

    def start(self, g_refs, out_refs, sems):
        mine, copies = self._copies(g_refs, out_refs, sems)
        for cp in mine + copies:
            cp.start()

    def finish(self, g_refs, out_refs, sems):
        mine, copies = self._copies(g_refs, out_refs, sems)
        for cp in copies + mine:
            cp.wait()


class _SiblingSwapRider:
    def __init__(self, gs):
        self.inputs = list(gs)
        n = len(gs)
        self.out_shape = [jax.ShapeDtypeStruct((4,) + g.shape[1:], g.dtype) for g in gs]
        self.scratch = [pltpu.SemaphoreType.DMA((4, n)), pltpu.SemaphoreType.DMA((4, n))]
        self.results = None

    def _copies(self, g_refs, out_refs, sems):
        send_sems, recv_sems = sems
        x, y, c = _my_place()
        return [pltpu.make_async_remote_copy(
            src_ref=g_refs[a].at[2 * k + 1 - c], dst_ref=out_refs[a].at[k],
            send_sem=send_sems.at[k, a], recv_sem=recv_sems.at[k, a],
            device_id=(x, y, 1 - c), device_id_type=MESH)
            for a in range(len(g_refs)) for k in range(4)]

    def start(self, g_refs, out_refs, sems):
        for cp in self._copies(g_refs, out_refs, sems):
            cp.start()

    def finish(self, g_refs, out_refs, sems):
        for cp in self._copies(g_refs, out_refs, sems):
            cp.wait()


class _ChipScatterRider:
    def __init__(self, ps):
        self.inputs = list(ps)
        n = len(ps)
        self.out_shape = [jax.ShapeDtypeStruct(p.shape, p.dtype) for p in ps]
        self.scratch = [pltpu.SemaphoreType.DMA((3, n)), pltpu.SemaphoreType.DMA((3, n)),
                        pltpu.SemaphoreType.DMA((n,))]
        self.results = None

    def _copies(self, p_refs, out_refs, sems):
        send_sems, recv_sems, local_sems = sems
        x, y, c = _my_place()
        my_chip = 2 * x + y
        chips = [(1 - x, y), (x, 1 - y), (1 - x, 1 - y)]
        n = len(p_refs)
        mine = [pltpu.make_async_copy(p_refs[a].at[my_chip], out_refs[a].at[my_chip], local_sems.at[a])
                for a in range(n)]
        copies = [pltpu.make_async_remote_copy(
            src_ref=p_refs[a].at[2 * cx + cy], dst_ref=out_refs[a].at[my_chip],
            send_sem=send_sems.at[j, a], recv_sem=recv_sems.at[j, a],
            device_id=(cx, cy, c), device_id_type=MESH)
            for a in range(n) for j, (cx, cy) in enumerate(chips)]
        return mine, copies

    def start(self, p_refs, out_refs, sems):
        mine, copies = self._copies(p_refs, out_refs, sems)
        for cp in mine + copies:
            cp.start()

    def finish(self, p_refs, out_refs, sems):
        mine, copies = self._copies(p_refs, out_refs, sems)
        for cp in copies + mine:
            cp.wait()


def _call(body, name, grid, in_specs, out_specs, out_shape, scratch, semantics, args, rider=None):
    in_specs, out_specs, out_shape, scratch = list(in_specs), list(out_specs), list(out_shape), list(scratch)
    if rider is None:
        return list(pl.pallas_call(
            body, name=name, grid=grid, in_specs=in_specs, out_specs=out_specs, out_shape=out_shape,
            scratch_shapes=scratch, compiler_params=_params(*semantics))(*args))
    n_in, n_out, n_scr = len(in_specs), len(out_specs), len(scratch)
    r_in, r_out = len(rider.inputs), len(rider.out_shape)

    def wrapped(*refs):
        cuts = np.cumsum([0, n_in, r_in, n_out, r_out, n_scr])
        hi, ri, ho, ro, hs = (refs[cuts[i]:cuts[i + 1]] for i in range(5))
        rs = refs[cuts[5]:]
        ids = [pl.program_id(d) for d in range(len(grid))]
        first, last = ids[0] == 0, ids[0] == grid[0] - 1
        for d in range(1, len(grid)):
            first = jnp.logical_and(first, ids[d] == 0)
            last = jnp.logical_and(last, ids[d] == grid[d] - 1)

        @pl.when(first)
        def _():
            rider.start(ri, ro, rs)

        body(*hi, *ho, *hs)

        @pl.when(last)
        def _():
            rider.finish(ri, ro, rs)

    outs = pl.pallas_call(
        wrapped, name=name, grid=grid,
        in_specs=in_specs + [ANY] * r_in, out_specs=out_specs + [ANY] * r_out,
        out_shape=out_shape + rider.out_shape, scratch_shapes=scratch + rider.scratch,
        compiler_params=_params(*(["arbitrary"] * len(grid))),
    )(*args, *rider.inputs)
    rider.results = list(outs[n_out:])
    return list(outs[:n_out])


def _exchange(rider, name):
    r_in, r_out = len(rider.inputs), len(rider.out_shape)

    def body(*refs):
        ri, ro, rs = refs[:r_in], refs[r_in:r_in + r_out], refs[r_in + r_out:]
        rider.start(ri, ro, rs)
        rider.finish(ri, ro, rs)

    return list(pl.pallas_call(
        body, name=name, in_specs=[ANY] * r_in, out_specs=[ANY] * r_out,
        out_shape=rider.out_shape, scratch_shapes=rider.scratch)(*rider.inputs))


MM_CAP_MN = 1024
MM_CAP_N = 1536
MM_CAP_K = 3072
MM_CAP_K_TOKENS = 2048
MM_CAP_K_RMS = 1536
NORM_ROWS = 256


def _mm(a, b, mode, name, out_dtype=F32, res=None, out_block=None, epilogue=None, extra=None, norm_g=None,
        rider=None):
    a3, b3 = a.ndim == 3, b.ndim == 3
    um = un = uk = None
    if mode in ("nn", "nt"):
        if a3:
            m, uk = a.shape[1:]
            k = a.shape[0] * uk
        else:
            m, k = a.shape
    else:
        if a3:
            k, um = a.shape[1:]
            m = a.shape[0] * um
        else:
            k, m = a.shape
    if mode in ("nn", "tn"):
        if b3:
            kb, un = b.shape[1:]
            n = b.shape[0] * un
        else:
            kb, n = b.shape
        assert kb == k, (a.shape, b.shape, mode)
    else:
        if b3:
            n, ukb = b.shape[1:]
            assert b.shape[0] * ukb == k and uk in (None, ukb), (a.shape, b.shape, mode)
            uk = ukb
        else:
            n, kb = b.shape
            assert kb == k, (a.shape, b.shape, mode)
    if out_block is not None:
        assert un in (None, out_block)
        un = out_block

    def tile(dim, unit, cap, align):
        if unit is None:
            return _pick(dim, cap, align), 1
        c = max(1, cap // unit)
        while (dim // unit) % c:
            c -= 1
        return unit, c

    um, cm = tile(m, um, MM_CAP_MN if mode != "tn" else 1408, 128 if mode == "tn" else 16)
    un, cn = tile(n, un, MM_CAP_N, 128)
    cap_k = MM_CAP_K_TOKENS if mode == "tn" else (MM_CAP_K_RMS if epilogue == "rms_bwd" else MM_CAP_K)
    uk, ck = tile(k, uk, cap_k, 128)
    if epilogue == "rms_bwd":
        assert mode == "nt" and n == D_MODEL and cm == cn == 1 and res is None and out_block is None
    if norm_g is not None:
        assert not a3 and (k if mode == "nn" else m) == D_MODEL and (ck if mode == "nn" else cm) == 1
    if epilogue == "swiglu":
        assert mode == "nn" and b3 and res is None and out_block is None
        cn = 2
    if epilogue == "swiglu_bwd":
        assert mode == "nt" and out_block is not None and extra is not None and res is None
        cn = 1
    tm, tn, tk = cm * um, cn * un, ck * uk
    nk = k // tk
    dot = {"nn": _dot, "nt": _dot_nt, "tn": _dot_tn}[mode]
    half = n // un // 2
    blocked_out = out_block is not None or epilogue in ("swiglu", "swiglu_bwd")
    extras = [] if extra is None else (list(extra) if isinstance(extra, (tuple, list)) else [extra])

    def sl(idx, unit, count):
        return slice(None) if count == 1 else slice(idx * unit, (idx + 1) * unit)

    def body(*refs):
        a_ref, b_ref = refs[0], refs[1]
        pos = 2
        r_ref = ng_ref = None
        if res is not None:
            r_ref, pos = refs[pos], pos + 1
        e_refs, pos = refs[pos:pos + len(extras)], pos + len(extras)
        if norm_g is not None:
            ng_ref, pos = refs[pos], pos + 1
        outs, acc_ref = refs[pos:-1], refs[-1]
        kk = pl.program_id(2)

        def normed():
            groups = []
            for r in range(0, a_ref.shape[0], NORM_ROWS):
                xv = a_ref[r:r + NORM_ROWS, :]
                rstd = lax.rsqrt(jnp.mean(xv * xv, axis=-1, keepdims=True) + EPS)
                groups.append((xv * rstd * ng_ref[...]).astype(BF16))
            return jnp.concatenate(groups, axis=0)

        def a_blk(mi, ki):
            if norm_g is not None:
                return normed()
            if mode in ("nn", "nt"):
                return a_ref[ki] if a3 else a_ref[:, sl(ki, uk, ck)]
            return a_ref[mi] if a3 else a_ref[:, sl(mi, um, cm)]

        def b_blk(ki, ni):
            if epilogue == "swiglu":
                return b_ref[ni, 0]
            if mode in ("nn", "tn"):
                return b_ref[ni] if b3 else b_ref[sl(ki, uk, ck), sl(ni, un, cn)]
            return b_ref[ki][sl(ni, un, cn), :] if b3 else b_ref[sl(ni, un, cn), sl(ki, uk, ck)]

        parts = {}
        for mi in range(cm):
            for ni in range(cn):
                part = None
                for ki in range(ck):
                    d = dot(a_blk(mi, ki).astype(BF16), b_blk(ki, ni).astype(BF16))
                    part = d if part is None else part + d
                parts[mi, ni] = part

        def finish(total):
            if epilogue == "swiglu":
                gate, up = total[0, 0], total[0, 1]
                outs[0][0, 0] = gate.astype(BF16)
                outs[0][1, 0] = up.astype(BF16)
                outs[1][0] = (gate * _sigmoid(gate) * up).astype(BF16)
                return
            if epilogue == "swiglu_bwd":
                dact = total[0, 0]
                gate, up = e_refs[0][0, 0].astype(F32), e_refs[0][1, 0].astype(F32)
                sg = _sigmoid(gate)
                outs[0][0, 0] = (dact * up * (sg * (1.0 + gate * (1.0 - sg)))).astype(BF16)
                outs[0][1, 0] = (dact * (gate * sg)).astype(BF16)
                return
            if epilogue == "rms_bwd":
                x_ref, g_ref, dres_ref = e_refs
                dh, dg = total[0, 0], None
                for r in range(0, tm, NORM_ROWS):
                    rows = slice(r, r + NORM_ROWS)
                    xv, dhv = x_ref[rows, :], dh[rows, :]
                    rstd = lax.rsqrt(jnp.mean(xv * xv, axis=-1, keepdims=True) + EPS)
                    xh = xv * rstd
                    dyg = dhv * g_ref[...]
                    c = jnp.mean(dyg * xh, axis=-1, keepdims=True)
                    outs[0][rows, :] = dres_ref[rows, :] + rstd * (dyg - xh * c)
                    part = jnp.sum(dhv * xh, axis=0, keepdims=True)
                    dg = part if dg is None else dg + part
                _accumulate(outs[1], dg, pl.program_id(0))
                return
            for (mi, ni), val in total.items():
                rows, cols = sl(mi, um, cm), sl(ni, un, cn)
                if res is not None:
                    val = r_ref[rows, cols] + val
                if blocked_out:
                    outs[0][ni, rows] = val.astype(out_dtype)
                else:
                    outs[0][rows, cols] = val.astype(out_dtype)

        if nk == 1:
            finish(parts)
        else:
            @pl.when(kk == 0)
            def _():
                for (mi, ni), val in parts.items():
                    acc_ref[mi * cn + ni] = val

            @pl.when(jnp.logical_and(kk > 0, kk < nk - 1))
            def _():
                for (mi, ni), val in parts.items():
                    acc_ref[mi * cn + ni] += val

            @pl.when(kk == nk - 1)
            def _():
                finish({key: acc_ref[key[0] * cn + key[1]] + val for key, val in parts.items()})

    if mode in ("nn", "nt"):
        a_spec = (pl.BlockSpec((ck, tm, uk), lambda i, j, kk: (kk, i, 0)) if a3
                  else pl.BlockSpec((tm, tk), lambda i, j, kk: (i, kk)))
    else:
        a_spec = (pl.BlockSpec((cm, tk, um), lambda i, j, kk: (i, kk, 0)) if a3
                  else pl.BlockSpec((tk, tm), lambda i, j, kk: (kk, i)))
    pair_spec = pl.BlockSpec((2, 1, tm, un), lambda i, j, kk: (0, j, i, 0))
    row_spec = pl.BlockSpec((tm, tn), lambda i, j, kk: (i, 0))
    vec_spec = pl.BlockSpec((1, tn), lambda i, j, kk: (0, 0))
    if epilogue == "swiglu":
        b = b.reshape(2, half, k, un)
        b_spec = pl.BlockSpec((2, 1, tk, un), lambda i, j, kk: (0, j, kk, 0))
    elif mode in ("nn", "tn"):
        b_spec = (pl.BlockSpec((cn, tk, un), lambda i, j, kk: (j, kk, 0)) if b3
                  else pl.BlockSpec((tk, tn), lambda i, j, kk: (kk, j)))
    else:
        b_spec = (pl.BlockSpec((ck, tn, uk), lambda i, j, kk: (kk, j, 0)) if b3
                  else pl.BlockSpec((tn, tk), lambda i, j, kk: (j, kk)))
    if epilogue == "swiglu":
        out_specs = [pair_spec, pl.BlockSpec((1, tm, un), lambda i, j, kk: (j, i, 0))]
        out_shape = [jax.ShapeDtypeStruct((2, half, m, un), BF16), jax.ShapeDtypeStruct((half, m, un), BF16)]
    elif epilogue == "swiglu_bwd":
        out_specs = [pair_spec]
        out_shape = [jax.ShapeDtypeStruct(extra.shape, BF16)]
    elif epilogue == "rms_bwd":
        out_specs = [row_spec, vec_spec]
        out_shape = [jax.ShapeDtypeStruct((m, n), F32), jax.ShapeDtypeStruct((1, n), F32)]
    elif blocked_out:
        out_specs = [pl.BlockSpec((cn, tm, un), lambda i, j, kk: (j, i, 0))]
        out_shape = [jax.ShapeDtypeStruct((n // un, m, un), out_dtype)]
    else:
        out_specs = [pl.BlockSpec((tm, tn), lambda i, j, kk: (i, j))]
        out_shape = [jax.ShapeDtypeStruct((m, n), out_dtype)]
    in_specs, args = [a_spec, b_spec], [a, b]
    if res is not None:
        in_specs.append(pl.BlockSpec((tm, tn), lambda i, j, kk: (i, j)))
        args.append(res)
    if epilogue == "swiglu_bwd":
        in_specs.append(pair_spec)
    elif epilogue == "rms_bwd":
        in_specs += [row_spec, vec_spec, row_spec]
    args += extras
    if norm_g is not None:
        in_specs.append(pl.BlockSpec((1, D_MODEL), lambda i, j, kk: (0, 0)))
        args.append(norm_g)
    semantics = ("arbitrary",) * 3 if epilogue == "rms_bwd" else ("parallel", "parallel", "arbitrary")
    out = _call(body, name, (m // tm, n // tn, nk), in_specs, out_specs, out_shape,
                [pltpu.VMEM((cm * cn, um, un), F32)], semantics, args, rider)
    return out if epilogue in ("swiglu", "rms_bwd") else out[0]


def _head_sums(v, ind):
    hi, lo = _split2(v)
    return _dot(hi, ind) + _dot(lo, ind)


def _head_spread(per_head, ind):
    hi, lo = _split2(per_head)
    return _dot_nt(hi, ind) + _dot_nt(lo, ind)


def _head_rstd(xv, ind):
    return _head_spread(lax.rsqrt(_head_sums(xv * xv, ind) * (1.0 / ATT_DH) + EPS), ind)


def _hn_bwd_math(xv, gv, ind, dyv, scale):
    rstd = _head_rstd(xv, ind)
    xh = xv * rstd
    dyn = dyv * scale
    dyg = dyn * gv
    dx = rstd * (dyg - xh * _head_spread(_head_sums(dyg * xh, ind) * (1.0 / ATT_DH), ind))
    return dx, jnp.sum(dyn * xh, axis=0, keepdims=True)


def _q_hnorm(x, g_tiled, bd, scale, name):
    t, d = x.shape
    tm = _pick(t, 512, 16)

    def body(x_ref, g_ref, bd_ref, o_ref):
        xv = x_ref[...]
        o_ref[...] = (xv * _head_rstd(xv, bd_ref[...]) * g_ref[...] * scale).astype(BF16)

    return pl.pallas_call(
        body, name=name, grid=(t // tm,),
        in_specs=[pl.BlockSpec((tm, d), lambda i: (i, 0)), pl.BlockSpec((1, d), lambda i: (0, 0)),
                  pl.BlockSpec((d, LANES), lambda i: (0, 0))],
        out_specs=pl.BlockSpec((tm, d), lambda i: (i, 0)),
        out_shape=jax.ShapeDtypeStruct((t, d), BF16),
        compiler_params=_params("parallel"),
    )(x, g_tiled, bd)


def _q_dhnorm(x, g_tiled, bd, dy, scale, name):
    t, d = x.shape
    tm = _pick(t, 512, 16)

    def body(x_ref, g_ref, bd_ref, dy_ref, dx_ref, dg_ref):
        dx, part = _hn_bwd_math(x_ref[...], g_ref[...], bd_ref[...], dy_ref[...], scale)
        dx_ref[...] = dx.astype(BF16)
        _accumulate(dg_ref, part, pl.program_id(0))

    row = pl.BlockSpec((tm, d), lambda i: (i, 0))
    vec = pl.BlockSpec((1, d), lambda i: (0, 0))
    return pl.pallas_call(
        body, name=name, grid=(t // tm,),
        in_specs=[row, vec, pl.BlockSpec((d, LANES), lambda i: (0, 0)), row],
        out_specs=[row, vec],
        out_shape=[jax.ShapeDtypeStruct((t, d), BF16), jax.ShapeDtypeStruct((1, d), F32)],
        compiler_params=_params("arbitrary"),
    )(x, g_tiled, bd, dy)


def _kv_prep(kv, g_tiled, bd, name):
    t = kv.shape[0]
    d = D_MODEL
    tm = K_PAD
    assert t % tm == 0

    def body(k_ref, v_ref, g_ref, bd_ref, kp_ref, vp_ref):
        i = pl.program_id(0)

        @pl.when(i == 0)
        def _():
            kp_ref[...] = jnp.zeros_like(kp_ref)
            vp_ref[...] = jnp.zeros_like(vp_ref)

        @pl.when(i > 0)
        def _():
            xv = k_ref[...]
            kp_ref[...] = (xv * _head_rstd(xv, bd_ref[...]) * g_ref[...]).astype(BF16)
            vp_ref[...] = v_ref[...].astype(BF16)

    shp = jax.ShapeDtypeStruct((t + K_PAD, d), BF16)
    out = pl.BlockSpec((tm, d), lambda i: (i, 0))
    return pl.pallas_call(
        body, name=name, grid=(t // tm + 1,),
        in_specs=[pl.BlockSpec((tm, d), lambda i: (jnp.maximum(i - 1, 0), 0)),
                  pl.BlockSpec((tm, d), lambda i: (jnp.maximum(i - 1, 0), 1)),
                  pl.BlockSpec((1, d), lambda i: (0, 0)), pl.BlockSpec((d, LANES), lambda i: (0, 0))],
        out_specs=[out, out], out_shape=[shp, shp],
        compiler_params=_params("arbitrary"),
    )(kv, kv, g_tiled, bd)


def _kv_dprep(kv, g_tiled, bd, dkp_t, dvp_t, name):
    t = kv.shape[0]
    d = D_MODEL
    tm = K_PAD

    def body(k_ref, g_ref, bd_ref, dk_ref, dv_ref, o_ref, dg_ref):
        dx, part = _hn_bwd_math(k_ref[...], g_ref[...], bd_ref[...], dk_ref[...].T, 1.0)
        o_ref[:, :d] = dx.astype(BF16)
        o_ref[:, d:] = dv_ref[...].T.astype(BF16)
        _accumulate(dg_ref, part, pl.program_id(0))

    vec = pl.BlockSpec((1, d), lambda i: (0, 0))
    padded = pl.BlockSpec((d, tm), lambda i: (0, i + 1))
    return pl.pallas_call(
        body, name=name, grid=(t // tm,),
        in_specs=[pl.BlockSpec((tm, d), lambda i: (i, 0)), vec, pl.BlockSpec((d, LANES), lambda i: (0, 0)),
                  padded, padded],
        out_specs=[pl.BlockSpec((tm, 2 * d), lambda i: (i, 0)), vec],
        out_shape=[jax.ShapeDtypeStruct((t, 2 * d), BF16), jax.ShapeDtypeStruct((1, d), F32)],
        compiler_params=_params("arbitrary"),
    )(kv, g_tiled, bd, dkp_t, dvp_t)


def _loss_head(y, target, name):
    t, d = y.shape
    tm = _pick(t, 512, 16)

    def body(y_ref, t_ref, dy_ref, l_ref):
        diff = y_ref[...] - t_ref[...]
        dy_ref[...] = diff * (1.0 / d)
        part = jnp.sum(jnp.sum(diff * diff, axis=-1, keepdims=True), axis=0, keepdims=True) * (0.5 / d)
        _accumulate(l_ref, part, pl.program_id(0))

    row = pl.BlockSpec((tm, d), lambda i: (i, 0))
    return pl.pallas_call(
        body, name=name, grid=(t // tm,),
        in_specs=[row, row], out_specs=[row, pl.BlockSpec((1, 1), lambda i: (0, 0))],
        out_shape=[jax.ShapeDtypeStruct((t, d), F32), jax.ShapeDtypeStruct((1, 1), F32)],
        compiler_params=_params("arbitrary"),
    )(y, target)


def _ret_consts(t):
    h = np.arange(RET_HEADS, dtype=np.float32)
    lg = np.log(np.float32(1.0) - np.float32(2.0) ** (np.float32(-5.0) - h)).astype(np.float32)
    tt = np.arange(CHUNK, dtype=np.float32)
    intra = np.exp(lg[:, None, None] * np.abs(tt[:, None] - tt[None, :])).astype(np.float32)
    q_dec = np.exp(lg[:, None] * (tt + 1.0)).astype(np.float32)
    k_dec = np.exp(lg[:, None] * (CHUNK - 1.0 - tt)).astype(np.float32)
    s_dec = [float(v) for v in np.exp(lg * np.float32(CHUNK)).astype(np.float32)]
    qd = np.broadcast_to(q_dec[:, :, None], (RET_HEADS, CHUNK, RET_DK)).copy()
    kd = np.broadcast_to(k_dec[:, :, None], (RET_HEADS, CHUNK, RET_DK)).copy()
    half = RET_DK // 2
    inv_freq = ROPE_BASE ** (-jnp.arange(half, dtype=F32) / half)
    ang = jnp.arange(t).astype(F32)[:, None] * inv_freq[None, :]
    return jnp.asarray(intra), jnp.asarray(qd), jnp.asarray(kd), s_dec, jnp.cos(ang), jnp.sin(ang)


def _rope(x, cos, sin):
    half = RET_DK // 2
    x1, x2 = x[:, :half], x[:, half:]
    return jnp.concatenate([x1 * cos - x2 * sin, x1 * sin + x2 * cos], axis=-1)


def _unrope(d, cos, sin):
    half = RET_DK // 2
    d1, d2 = d[:, :half], d[:, half:]
    return jnp.concatenate([d1 * cos + d2 * sin, d2 * cos - d1 * sin], axis=-1)


def _ret_slices(h):
    q = slice(h * RET_DK, (h + 1) * RET_DK)
    k = slice(RET_Q_COLS + h * RET_DK, RET_Q_COLS + (h + 1) * RET_DK)
    v = slice(2 * RET_Q_COLS + h * RET_DV, 2 * RET_Q_COLS + (h + 1) * RET_DV)
    g = slice(2 * RET_Q_COLS + RET_V_COLS + h * RET_DV, 2 * RET_Q_COLS + RET_V_COLS + (h + 1) * RET_DV)
    o = slice(h * RET_DV, (h + 1) * RET_DV)
    return q, k, v, g, o


def _ret_fwd(proj, gn, consts, name, rider=None):
    t, cols = proj.shape
    n = t // CHUNK
    intra, qd, kd, s_dec, cos, sin = consts
    k_scale = RET_DK ** -0.5

    def body(p_ref, cos_ref, sin_ref, intra_ref, qd_ref, kd_ref, gn_ref, y_ref, o_ref, st_ref, state):
        i = pl.program_id(0)

        @pl.when(i == 0)
        def _():
            state[...] = jnp.zeros_like(state)

        for c in range(RET_STEP):
            rows = slice(c * CHUNK, (c + 1) * CHUNK)
            cosv, sinv = cos_ref[rows, :], sin_ref[rows, :]
            for h in range(RET_HEADS):
                qs, ks, vs, gs, os_ = _ret_slices(h)
                qr = _rope(p_ref[rows, qs], cosv, sinv)
                kr = _rope(p_ref[rows, ks], cosv, sinv) * k_scale
                vb = p_ref[rows, vs].astype(BF16)
                gv = p_ref[rows, gs]
                scores = _dot_nt(qr.astype(BF16), kr.astype(BF16)) * intra_ref[h]
                s_old = state[h]
                s_old_b = s_old.astype(BF16)
                st_ref[c, h] = s_old_b
                o = _dot(scores.astype(BF16), vb) + _dot((qr * qd_ref[h]).astype(BF16), s_old_b)
                state[h] = s_old * s_dec[h] + _dot_tn((kr * kd_ref[h]).astype(BF16), vb)
                rstd = lax.rsqrt(jnp.mean(o * o, axis=-1, keepdims=True) + EPS)
                on = o * rstd * gn_ref[:, os_]
                o_ref[rows, os_] = o
                y_ref[rows, os_] = (gv * _sigmoid(gv) * on).astype(BF16)

    full3 = lambda a: pl.BlockSpec(a.shape, lambda i: (0, 0, 0))
    step = RET_STEP * CHUNK
    return _call(
        body, name, (n // RET_STEP,),
        [pl.BlockSpec((step, cols), lambda i: (i, 0)),
         pl.BlockSpec((step, RET_DK // 2), lambda i: (i, 0)),
         pl.BlockSpec((step, RET_DK // 2), lambda i: (i, 0)),
         full3(intra), full3(qd), full3(kd),
         pl.BlockSpec((1, RET_V_COLS), lambda i: (0, 0))],
        [pl.BlockSpec((step, RET_V_COLS), lambda i: (i, 0)),
         pl.BlockSpec((step, RET_V_COLS), lambda i: (i, 0)),
         pl.BlockSpec((RET_STEP, RET_HEADS, RET_DK, RET_DV), lambda i: (i, 0, 0, 0))],
        [jax.ShapeDtypeStruct((t, RET_V_COLS), BF16),
         jax.ShapeDtypeStruct((t, RET_V_COLS), F32),
         jax.ShapeDtypeStruct((n, RET_HEADS, RET_DK, RET_DV), BF16)],
        [pltpu.VMEM((RET_HEADS, RET_DK, RET_DV), F32)], ("arbitrary",),
        (proj, cos, sin, intra, qd, kd, gn), rider)


def _ret_bwd(proj, gn, o_saved, states, dy, consts, name, rider=None):
    t, cols = proj.shape
    n = t // CHUNK
    intra, qd, kd, s_dec, cos, sin = consts
    k_scale = RET_DK ** -0.5

    def body(p_ref, cos_ref, sin_ref, intra_ref, qd_ref, kd_ref, gn_ref, o_ref, st_ref, dy_ref,
             dp_ref, dgn_ref, dstate):
        i = pl.program_id(0)

        @pl.when(i == 0)
        def _():
            dstate[...] = jnp.zeros_like(dstate)

        dgn = None
        for c in reversed(range(RET_STEP)):
            rows = slice(c * CHUNK, (c + 1) * CHUNK)
            cosv, sinv = cos_ref[rows, :], sin_ref[rows, :]
            dgn_parts = []
            for h in range(RET_HEADS):
                qs, ks, vs, gs, os_ = _ret_slices(h)
                qr = _rope(p_ref[rows, qs], cosv, sinv)
                kr = _rope(p_ref[rows, ks], cosv, sinv) * k_scale
                qb, kb = qr.astype(BF16), kr.astype(BF16)
                vb = p_ref[rows, vs].astype(BF16)
                gv = p_ref[rows, gs]
                ov = o_ref[rows, os_]
                dyv = dy_ref[rows, os_]
                gnv = gn_ref[:, os_]
                sg = _sigmoid(gv)
                rstd = lax.rsqrt(jnp.mean(ov * ov, axis=-1, keepdims=True) + EPS)
                oh = ov * rstd
                d_on = dyv * (gv * sg)
                dg = dyv * (oh * gnv) * (sg * (1.0 + gv * (1.0 - sg)))
                dgn_parts.append(jnp.sum(d_on * oh, axis=0, keepdims=True))
                d_oh = d_on * gnv
                do = rstd * (d_oh - oh * jnp.mean(d_oh * oh, axis=-1, keepdims=True))
                dob = do.astype(BF16)
                mask = intra_ref[h]
                a_b = (_dot_nt(qb, kb) * mask).astype(BF16)
                da_b = (_dot_nt(dob, vb) * mask).astype(BF16)
                ds_new = dstate[h]
                ds_new_b = ds_new.astype(BF16)
                s_old_b = st_ref[c, h]
                qdv, kdv = qd_ref[h], kd_ref[h]
                dv = _dot_tn(a_b, dob) + _dot((kr * kdv).astype(BF16), ds_new_b)
                dqr = _dot(da_b, kb) + _dot_nt(dob, s_old_b) * qdv
                dkr = _dot_tn(da_b, qb) + _dot_nt(vb, ds_new_b) * kdv
                dstate[h] = ds_new * s_dec[h] + _dot_tn((qr * qdv).astype(BF16), dob)
                dp_ref[rows, qs] = _unrope(dqr, cosv, sinv).astype(BF16)
                dp_ref[rows, ks] = _unrope(dkr * k_scale, cosv, sinv).astype(BF16)
                dp_ref[rows, vs] = dv.astype(BF16)
                dp_ref[rows, gs] = dg.astype(BF16)
            part = jnp.concatenate(dgn_parts, axis=-1)
            dgn = part if dgn is None else dgn + part
        _accumulate(dgn_ref, dgn, i)

    steps = n // RET_STEP
    step = RET_STEP * CHUNK
    rev = lambda i: (steps - 1 - i, 0)
    full3 = lambda a: pl.BlockSpec(a.shape, lambda i: (0, 0, 0))
    return _call(
        body, name, (steps,),
        [pl.BlockSpec((step, cols), rev),
         pl.BlockSpec((step, RET_DK // 2), rev),
         pl.BlockSpec((step, RET_DK // 2), rev),
         full3(intra), full3(qd), full3(kd),
         pl.BlockSpec((1, RET_V_COLS), lambda i: (0, 0)),
         pl.BlockSpec((step, RET_V_COLS), rev),
         pl.BlockSpec((RET_STEP, RET_HEADS, RET_DK, RET_DV), lambda i: (steps - 1 - i, 0, 0, 0)),
         pl.BlockSpec((step, RET_V_COLS), rev)],
        [pl.BlockSpec((step, cols), rev),
         pl.BlockSpec((1, RET_V_COLS), lambda i: (0, 0))],
        [jax.ShapeDtypeStruct((t, cols), BF16),
         jax.ShapeDtypeStruct((1, RET_V_COLS), F32)],
        [pltpu.VMEM((RET_HEADS, RET_DK, RET_DV), F32)], ("arbitrary",),
        (proj, cos, sin, intra, qd, kd, gn, o_saved, states, dy), rider)


def _att_common(q_ref, kp_ref, vp_ref):
    blk = pl.program_id(1)
    start = pl.multiple_of(blk * Q_BLOCK, Q_BLOCK)
    kw = kp_ref[pl.ds(start, K_WINDOW), :]
    vw = vp_ref[pl.ds(start, K_WINDOW), :]
    kvalid = blk * Q_BLOCK - K_PAD + lax.broadcasted_iota(jnp.int32, (1, K_WINDOW), 1) >= 0
    lane = lax.broadcasted_iota(jnp.int32, (1, LANES), 1)
    return start, q_ref[...], kw, vw, kvalid, (lane < ATT_DH, lane >= ATT_DH)


def _row_groups():
    return [slice(r * ATT_ROWS, (r + 1) * ATT_ROWS) for r in range(Q_BLOCK // ATT_ROWS)]


def _lane_copies(x):
    return jnp.tile(x, (1, K_WINDOW // LANES))


def _att_specs(t, tp):
    qspec = pl.BlockSpec((Q_BLOCK, LANES), lambda h, i: (i, h))
    kspec = pl.BlockSpec((tp, LANES), lambda h, i: (0, h))
    bspec = pl.BlockSpec((2, Q_BLOCK, K_WINDOW), lambda h, i: (h, 0, 0))
    return qspec, kspec, bspec


def _att_fwd(q, kp, vp, bias, name, rider=None):
    t, d = q.shape
    tp = kp.shape[0]

    def body(q_ref, kp_ref, vp_ref, bias_ref, o_ref, lse_ref, s_scr, p_scr, lse_scr):
        _, q2, kw, vw, kvalid, sel = _att_common(q_ref, kp_ref, vp_ref)
        for hh in range(2):
            s_scr[hh] = _dot_nt(jnp.where(sel[hh], q2, 0), kw)
        for hh in range(2):
            for rows in _row_groups():
                s = jnp.where(kvalid, s_scr[hh, rows, :] + bias_ref[hh, rows, :], NEG)
                m = jnp.max(s, axis=-1, keepdims=True)
                e = jnp.exp(s - m)
                l = jnp.sum(e, axis=-1, keepdims=True)
                p_scr[hh, rows, :] = (e * (1.0 / l)).astype(BF16)
                lse_scr[hh, rows, :] = jnp.broadcast_to(m + jnp.log(l), (ATT_ROWS, LANES))
        outs = [_dot(p_scr[hh], vw) for hh in range(2)]
        o_ref[...] = jnp.where(sel[0], outs[0], outs[1]).astype(BF16)
        lse_ref[...] = jnp.where(sel[0], lse_scr[0], lse_scr[1])

    qspec, kspec, bspec = _att_specs(t, tp)
    return _call(body, name, (d // LANES, t // Q_BLOCK), [qspec, kspec, kspec, bspec], [qspec, qspec],
                 [jax.ShapeDtypeStruct((t, d), BF16), jax.ShapeDtypeStruct((t, d), F32)],
                 [pltpu.VMEM((2, Q_BLOCK, K_WINDOW), F32), pltpu.VMEM((2, Q_BLOCK, K_WINDOW), BF16),
                  pltpu.VMEM((2, Q_BLOCK, LANES), F32)],
                 ("parallel", "arbitrary"), (q, kp, vp, bias), rider)


def _att_bwd(q, kp, vp, bias, do, o, lse, name, rider=None):
    t, d = q.shape
    tp = kp.shape[0]

    def body(q_ref, kp_ref, vp_ref, bias_ref, do_ref, o_ref, lse_ref, dq_ref, dkp_ref, dvp_ref, db_ref,
             s_scr, dp_scr, p_scr, ds_scr, row_scr):
        @pl.when(pl.program_id(1) == 0)
        def _():
            dkp_ref[...] = jnp.zeros_like(dkp_ref)
            dvp_ref[...] = jnp.zeros_like(dvp_ref)
            db_ref[...] = jnp.zeros_like(db_ref)

        start, q2, kw, vw, kvalid, sel = _att_common(q_ref, kp_ref, vp_ref)
        do2 = do_ref[...]
        qm = [jnp.where(sel[hh], q2, 0) for hh in range(2)]
        dom = [jnp.where(sel[hh], do2, 0) for hh in range(2)]
        do_o = do2.astype(F32) * o_ref[...].astype(F32)
        lse2 = lse_ref[...]
        for hh in range(2):
            s_scr[hh] = _dot_nt(qm[hh], kw)
            dp_scr[hh] = _dot_nt(dom[hh], vw)
            lse_h = jnp.max(jnp.where(sel[hh], lse2, NEG), axis=-1, keepdims=True)
            delta = jnp.sum(jnp.where(sel[hh], do_o, 0.0), axis=-1, keepdims=True)
            row_scr[hh, 0] = jnp.broadcast_to(lse_h, (Q_BLOCK, LANES))
            row_scr[hh, 1] = jnp.broadcast_to(delta, (Q_BLOCK, LANES))
        for hh in range(2):
            for rows in _row_groups():
                s = jnp.where(kvalid, s_scr[hh, rows, :] + bias_ref[hh, rows, :], NEG)
                p = jnp.exp(s - _lane_copies(row_scr[hh, 0, rows, :]))
                ds = p * (dp_scr[hh, rows, :] - _lane_copies(row_scr[hh, 1, rows, :]))
                db_ref[hh, rows, :] += ds
                p_scr[hh, rows, :] = p.astype(BF16)
                ds_scr[hh, rows, :] = ds.astype(BF16)
        dqs = [_dot(ds_scr[hh], kw) for hh in range(2)]
        dq_ref[...] = jnp.where(sel[0], dqs[0], dqs[1])
        dkp_ref[:, pl.ds(start, K_WINDOW)] += _dot_tn(qm[0], ds_scr[0]) + _dot_tn(qm[1], ds_scr[1])
        dvp_ref[:, pl.ds(start, K_WINDOW)] += _dot_tn(dom[0], p_scr[0]) + _dot_tn(dom[1], p_scr[1])

    qspec, kspec, bspec = _att_specs(t, tp)
    tspec = pl.BlockSpec((LANES, tp), lambda h, i: (h, 0))
    stage = lambda dt: pltpu.VMEM((2, Q_BLOCK, K_WINDOW), dt)
    return _call(body, name, (d // LANES, t // Q_BLOCK),
                 [qspec, kspec, kspec, bspec, qspec, qspec, qspec],
                 [qspec, tspec, tspec, bspec],
                 [jax.ShapeDtypeStruct((t, d), F32),
                  jax.ShapeDtypeStruct((d, tp), F32),
                  jax.ShapeDtypeStruct((d, tp), F32),
                  jax.ShapeDtypeStruct((ATT_HEADS, Q_BLOCK, K_WINDOW), F32)],
                 [stage(F32), stage(F32), stage(BF16), stage(BF16),
                  pltpu.VMEM((2, 2, Q_BLOCK, LANES), F32)],
                 ("parallel", "arbitrary"), (q, kp, vp, bias, do, o, lse), rider)


def _rel_bin_matrix():
    rows = REL_DELTAS * 2 * REL_BLK
    rho = lax.broadcasted_iota(jnp.int32, (rows, REL_PAD), 0)
    col = lax.broadcasted_iota(jnp.int32, (rows, REL_PAD), 1)
    assert 2 * REL_BLK == 256
    delta = rho >> 8
    c = 255 - (rho & 255)
    dist = K_PAD + REL_BLK * (delta - (K_WINDOW // REL_BLK - 1)) + (c - (REL_BLK - 1))
    idx = jnp.clip(dist, -REL_CLIP, REL_CLIP) + REL_CLIP
    return col == idx


def _rel_expand(rel_pad, name):
    heads = rel_pad.shape[0]
    rows = REL_DELTAS * 2 * REL_BLK

    def body_bin(r_ref, o_ref):
        onehot = jnp.where(_rel_bin_matrix(), 1.0, 0.0).astype(BF16)
        hi, mid, lo = _split3(r_ref[...])
        o_ref[...] = _dot_nt(hi, onehot) + _dot_nt(mid, onehot) + _dot_nt(lo, onehot)

    by_delta = pl.pallas_call(
        body_bin, name=name + "_bin",
        out_shape=jax.ShapeDtypeStruct((heads, rows), F32),
        compiler_params=pltpu.CompilerParams(vmem_limit_bytes=VMEM_LIMIT_V7X),
    )(rel_pad)
    by_delta = by_delta.reshape(heads * REL_DELTAS, 2 * REL_BLK)

    def body_shift(t_ref, o_ref):
        tv = t_ref[...]
        for r in range(REL_BLK):
            o_ref[r] = pltpu.roll(tv, (r + REL_BLK) % (2 * REL_BLK), 1)[:, :REL_BLK]

    return pl.pallas_call(
        body_shift, name=name + "_shift",
        out_shape=jax.ShapeDtypeStruct((REL_BLK, heads * REL_DELTAS, REL_BLK), F32),
        compiler_params=pltpu.CompilerParams(vmem_limit_bytes=VMEM_LIMIT_V7X),
    )(by_delta)


def _bias_table(rel_bias, name):
    heads = rel_bias.shape[0]
    rel_pad = jnp.pad(rel_bias, ((0, 0), (0, REL_PAD - REL_TABLE)))
    tiles = _rel_expand(rel_pad, name)
    tiles = tiles.reshape(REL_BLK, heads, REL_DELTAS, REL_BLK).transpose(1, 2, 0, 3)
    na, nb = Q_BLOCK // REL_BLK, K_WINDOW // REL_BLK
    rows = [jnp.concatenate([tiles[:, a - b + nb - 1] for b in range(nb)], axis=-1) for a in range(na)]
    table = jnp.concatenate(rows, axis=-2)
    qc = np.arange(Q_BLOCK)[:, None] // CHUNK
    kc = np.arange(K_WINDOW)[None, :] // CHUNK
    band = (kc >= qc) & (kc <= qc + PAST_CHUNKS)
    return jnp.where(jnp.asarray(band)[None], table, NEG)


def _rel_reduce(db, name):
    heads = db.shape[0]
    na, nb = Q_BLOCK // REL_BLK, K_WINDOW // REL_BLK

    def body_fold(db_ref, g_ref):
        for delta in range(REL_DELTAS):
            acc = None
            for a in range(na):
                b = a - (delta - (nb - 1))
                if 0 <= b < nb:
                    tile = db_ref[0, a * REL_BLK:(a + 1) * REL_BLK, b * REL_BLK:(b + 1) * REL_BLK]
                    acc = tile if acc is None else acc + tile
            g_ref[0, delta] = acc

    folded = pl.pallas_call(
        body_fold, name=name + "_fold", grid=(heads,),
        in_specs=[pl.BlockSpec((1, Q_BLOCK, K_WINDOW), lambda h: (h, 0, 0))],
        out_specs=pl.BlockSpec((1, REL_DELTAS, REL_BLK, REL_BLK), lambda h: (h, 0, 0, 0)),
        out_shape=jax.ShapeDtypeStruct((heads, REL_DELTAS, REL_BLK, REL_BLK), F32),
        compiler_params=_params("parallel"),
    )(db)
    by_row = folded.transpose(2, 0, 1, 3).reshape(REL_BLK, heads * REL_DELTAS, REL_BLK)

    def body_diag(g_ref, d_ref):
        zeros = jnp.zeros((heads * REL_DELTAS, REL_BLK), F32)
        acc = None
        for r in range(REL_BLK):
            part = pltpu.roll(jnp.concatenate([g_ref[r], zeros], axis=1), REL_BLK - r, 1)
            acc = part if acc is None else acc + part
        d_ref[...] = acc

    diag = pl.pallas_call(
        body_diag, name=name + "_diag",
        out_shape=jax.ShapeDtypeStruct((heads * REL_DELTAS, 2 * REL_BLK), F32),
        compiler_params=pltpu.CompilerParams(vmem_limit_bytes=VMEM_LIMIT_V7X),
    )(by_row)
    diag = diag.reshape(heads, REL_DELTAS * 2 * REL_BLK)

    def body_bin(d_ref, o_ref):
        onehot = jnp.where(_rel_bin_matrix(), 1.0, 0.0).astype(BF16)
        hi, mid, lo = _split3(d_ref[...])
        o_ref[...] = _dot(hi, onehot) + _dot(mid, onehot) + _dot(lo, onehot)

    out = pl.pallas_call(
        body_bin, name=name + "_bin",
        out_shape=jax.ShapeDtypeStruct((heads, REL_PAD), F32),
        compiler_params=pltpu.CompilerParams(vmem_limit_bytes=VMEM_LIMIT_V7X),
    )(diag)
    return out[:, :REL_TABLE]


def _sum_leading(x, name):
    n, r, c = x.shape
    tr = _pick(r, 256, 8)

    def body(x_ref, o_ref):
        acc = x_ref[0].astype(F32)
        for k in range(1, n):
            acc = acc + x_ref[k].astype(F32)
        o_ref[...] = acc

    return pl.pallas_call(
        body, name=name, grid=(r // tr,),
        in_specs=[pl.BlockSpec((n, tr, c), lambda i: (0, i, 0))],
        out_specs=pl.BlockSpec((tr, c), lambda i: (i, 0)),
        out_shape=jax.ShapeDtypeStruct((r, c), F32),
        compiler_params=_params("parallel"),
    )(x)


def _pair_add(g, recv, parity, name):
    _, r, c = g.shape
    tr = _pick(r, 256, 16)

    def body(par_ref, g_ref, r_ref, o_ref):
        o_ref[...] = (g_ref[...].astype(F32) + r_ref[...].astype(F32)).astype(BF16)

    return pl.pallas_call(
        body, name=name,
        grid_spec=pltpu.PrefetchScalarGridSpec(
            num_scalar_prefetch=1, grid=(4, r // tr),
            in_specs=[pl.BlockSpec((1, tr, c), lambda k, i, par: (2 * k + par[0], i, 0)),
                      pl.BlockSpec((1, tr, c), lambda k, i, par: (k, i, 0))],
            out_specs=pl.BlockSpec((1, tr, c), lambda k, i, par: (k, i, 0))),
        out_shape=jax.ShapeDtypeStruct((4, r, c), BF16),
        compiler_params=_params("parallel", "parallel"),
    )(parity, g, recv)


def _adamw(w, g_parts, m, v, name):
    r, c = w.shape
    n = g_parts.shape[0]
    tr = _pick(r, 256, 16 if g_parts.dtype == BF16 else 8)
    c1 = 1.0 - ADAM_B1 ** ADAM_STEP
    c2 = 1.0 - ADAM_B2 ** ADAM_STEP

    def body(w_ref, g_ref, m_ref, v_ref, go_ref, d_ref, nm_ref, nv_ref):
        gv = g_ref[0].astype(F32)
        for k in range(1, n):
            gv = gv + g_ref[k].astype(F32)
        nm = ADAM_B1 * m_ref[...] + (1.0 - ADAM_B1) * gv
        nv = ADAM_B2 * v_ref[...] + (1.0 - ADAM_B2) * (gv * gv)
        go_ref[...] = gv
        d_ref[...] = -ADAM_LR * ((nm / c1) / (jnp.sqrt(nv / c2) + ADAM_EPS) + ADAM_WD * w_ref[...])
        nm_ref[...] = nm
        nv_ref[...] = nv

    spec = pl.BlockSpec((tr, c), lambda i: (i, 0))
    shp = jax.ShapeDtypeStruct((r, c), F32)
    return pl.pallas_call(
        body, name=name, grid=(r // tr,),
        in_specs=[spec, pl.BlockSpec((n, tr, c), lambda i: (0, i, 0)), spec, spec],
        out_specs=[spec] * 4, out_shape=[shp] * 4,
        compiler_params=_params("parallel"),
    )(w, g_parts, m, v)


BIG = (("a_w_in", 1), ("a_w_o", 0), ("a_w_gu", 1), ("a_w_down", 0), ("w_kv", 1),
       ("b_w_q", 0), ("b_w_o", 0), ("b_w_gu", 1), ("b_w_down", 0))

SMALL = (("a_norm_g", D_MODEL, True), ("a_gn_g", RET_V_COLS, True), ("a_ffn_norm_g", D_MODEL, True),
         ("kv_norm_g", D_MODEL, False), ("b_norm_g", D_MODEL, False), ("b_ffn_norm_g", D_MODEL, False),
         ("k_norm_g", ATT_DH, False), ("b_q_norm_g", ATT_DH, False),
         ("b_rel_bias", ATT_HEADS * REL_TABLE, False))
SMALL_ROWS, SMALL_COLS = 16, 1024


def _pack_small(vals):
    flat = jnp.concatenate([vals[n].reshape(-1) for n, _, _ in SMALL])
    return jnp.pad(flat, (0, SMALL_ROWS * SMALL_COLS - flat.shape[0])).reshape(SMALL_ROWS, SMALL_COLS)


def _unpack_small(packed, local):
    flat, out, pos = packed.reshape(-1), {}, 0
    for n, length, sharded in SMALL:
        ln = length // N_DEV if (local and sharded) else length
        out[n] = flat[pos:pos + ln]
        pos += ln
    return out


def _gather_rider(shards, names):
    return _GatherRider([shards[n] for n in names])


def _gathered(rider, names, axis_of):
    return {n: (r.reshape(-1, r.shape[2]) if axis_of[n] == 0 else r) for n, r in zip(names, rider.results)}


def _blocks(g):
    return g if g.ndim == 3 else g.reshape(N_DEV, -1, g.shape[-1])


def _local_step(x, target, shards, w_in, s, parity):
    t = x.shape[0]
    axis_of = dict(BIG)
    consts = _ret_consts(t)
    lane_to_head = np.zeros((D_MODEL, LANES), np.float32)
    lane_to_head[np.arange(D_MODEL), np.arange(D_MODEL) // ATT_DH] = 1.0
    bd = jnp.asarray(lane_to_head).astype(BF16)
    kg_t = jnp.tile(s["k_norm_g"], (1, ATT_HEADS))
    qg_t = jnp.tile(s["b_q_norm_g"], (1, ATT_HEADS))
    q_scale = ATT_DH ** -0.5
    w = {"a_w_in": w_in}
    g, recv = {}, {}

    def gather_on(names):
        return _gather_rider(shards, names), names

    def landed(ride):
        w.update(_gathered(ride[0], ride[1], axis_of))

    def scatter_on(names):
        return _ScatterRider([_blocks(g[n]) for n in names]), names

    def reduced(ride):
        recv.update(zip(ride[1], ride[0].results))

    ride = gather_on(["a_w_o", "a_w_down"])
    proj = _mm(x, w["a_w_in"], "nn", "a_proj", norm_g=s["a_norm_g"], rider=ride[0])
    landed(ride)
    ride = gather_on(["a_w_gu", "w_kv"])
    y, o_ret, states = _ret_fwd(proj, s["a_gn_g"], consts, "a_ret", rider=ride[0])
    landed(ride)
    x1 = _mm(y, w["a_w_o"], "nn", "a_out", res=x)
    ride = gather_on(["b_w_q", "b_w_o", "b_w_down"])
    gu_a, act_a = _mm(x1, w["a_w_gu"], "nn", "a_ffn_gu", epilogue="swiglu", norm_g=s["a_ffn_norm_g"],
                      rider=ride[0])
    landed(ride)
    x2 = _mm(act_a, w["a_w_down"], "nn", "a_ffn_down", res=x1)

    kv = _mm(x2, w["w_kv"], "nn", "kv_proj", norm_g=s["kv_norm_g"])
    kp, vp = _kv_prep(kv, kg_t, bd, "kv_prep")

    q_raw = _mm(x2, w["b_w_q"], "nn", "b_q", norm_g=s["b_norm_g"])
    qn = _q_hnorm(q_raw, qg_t, bd, q_scale, "q_hnorm")
    bias = _bias_table(s["b_rel_bias"].reshape(ATT_HEADS, REL_TABLE), "rel")
    ride = gather_on(["b_w_gu"])
    o_att, lse = _att_fwd(qn, kp, vp, bias, "b_att", rider=ride[0])
    landed(ride)
    x3 = _mm(o_att, w["b_w_o"], "nn", "b_out", res=x2)
    gu_b, act_b = _mm(x3, w["b_w_gu"], "nn", "b_ffn_gu", epilogue="swiglu", norm_g=s["b_ffn_norm_g"])
    x4 = _mm(act_b, w["b_w_down"], "nn", "b_ffn_down", res=x3)

    dy, loss = _loss_head(x4, target, "loss")
    in_blk, kv_blk, ffn_blk = w["a_w_in"].shape[2], w["w_kv"].shape[2], w["b_w_gu"].shape[2]

    dgu = _mm(dy, w["b_w_down"], "nt", "b_ffn_dgu", out_block=ffn_blk, epilogue="swiglu_bwd", extra=gu_b)
    dgu = dgu.reshape(N_DEV, t, ffn_blk)
    g["b_w_down"] = _mm(act_b, dy, "tn", "b_ffn_gdown", out_dtype=BF16)
    ride = scatter_on(["b_w_down"])
    dx3, g["b_ffn_norm_g"] = _mm(dgu, w["b_w_gu"], "nt", "b_ffn_dh", epilogue="rms_bwd",
                                 extra=(x3, s["b_ffn_norm_g"], dy), rider=ride[0])
    reduced(ride)
    g["b_w_gu"] = _mm(x3, dgu, "tn", "b_ffn_ggu", out_dtype=BF16, out_block=ffn_blk, norm_g=s["b_ffn_norm_g"])

    do_att = _mm(dx3, w["b_w_o"], "nt", "b_dout", out_dtype=BF16)
    g["b_w_o"] = _mm(o_att, dx3, "tn", "b_gout", out_dtype=BF16)
    ride = scatter_on(["b_w_gu", "b_w_o"])
    dq, dkp, dvp, db = _att_bwd(qn, kp, vp, bias, do_att, o_att, lse, "b_datt", rider=ride[0])
    reduced(ride)
    g["b_rel_bias"] = _rel_reduce(db, "drel").reshape(1, -1)
    dq_raw, gq = _q_dhnorm(q_raw, qg_t, bd, dq, q_scale, "q_dhnorm")
    g["b_q_norm_g"] = gq.reshape(ATT_HEADS, ATT_DH).sum(axis=0, keepdims=True)
    g["b_w_q"] = _mm(x2, dq_raw, "tn", "b_gq", out_dtype=BF16, norm_g=s["b_norm_g"])
    dx2, g["b_norm_g"] = _mm(dq_raw, w["b_w_q"], "nt", "b_dq", epilogue="rms_bwd",
                             extra=(x2, s["b_norm_g"], dx3))

    dkv, gk = _kv_dprep(kv, kg_t, bd, dkp, dvp, "kv_dprep")
    g["k_norm_g"] = gk.reshape(ATT_HEADS, ATT_DH).sum(axis=0, keepdims=True)
    g["w_kv"] = _mm(x2, dkv, "tn", "kv_g", out_dtype=BF16, out_block=kv_blk, norm_g=s["kv_norm_g"])
    dx2, g["kv_norm_g"] = _mm(dkv, w["w_kv"], "nt", "kv_du", epilogue="rms_bwd",
                              extra=(x2, s["kv_norm_g"], dx2))

    ride = scatter_on(["b_w_q", "w_kv"])
    dgu = _mm(dx2, w["a_w_down"], "nt", "a_ffn_dgu", out_block=ffn_blk, epilogue="swiglu_bwd", extra=gu_a,
              rider=ride[0])
    reduced(ride)
    dgu = dgu.reshape(N_DEV, t, ffn_blk)
    g["a_w_down"] = _mm(act_a, dx2, "tn", "a_ffn_gdown", out_dtype=BF16)
    ride = scatter_on(["a_w_down"])
    dx1, g["a_ffn_norm_g"] = _mm(dgu, w["a_w_gu"], "nt", "a_ffn_dh", epilogue="rms_bwd",
                                 extra=(x1, s["a_ffn_norm_g"], dx2), rider=ride[0])
    reduced(ride)
    g["a_w_gu"] = _mm(x1, dgu, "tn", "a_ffn_ggu", out_dtype=BF16, out_block=ffn_blk, norm_g=s["a_ffn_norm_g"])

    dy_ret = _mm(dx1, w["a_w_o"], "nt", "a_dout")
    g["a_w_o"] = _mm(y, dx1, "tn", "a_gout", out_dtype=BF16)
    ride = scatter_on(["a_w_gu"])
    dproj, g["a_gn_g"] = _ret_bwd(proj, s["a_gn_g"], o_ret, states, dy_ret, consts, "a_dret", rider=ride[0])
    reduced(ride)
    ride = scatter_on(["a_w_o"])
    g["a_w_in"] = _mm(x, dproj, "tn", "a_gin", out_dtype=BF16, out_block=in_blk, norm_g=s["a_norm_g"],
                      rider=ride[0])
    reduced(ride)
    from_sibling = _exchange(_SiblingSwapRider([g["a_w_in"]]), "rs_sibling")[0]
    chip_sums = _pair_add(g["a_w_in"], from_sibling, parity, "rs_pair_add")
    last = _ChipScatterRider([chip_sums])
    grad_x, g["a_norm_g"] = _mm(dproj, w["a_w_in"], "nt", "a_dproj", epilogue="rms_bwd",
                                extra=(x, s["a_norm_g"], dx1), rider=last)
    recv["a_w_in"] = last.results[0]
    return loss, grad_x, recv, g


ARG_NAMES = ("x", "a_norm_g", "a_w_in", "a_gn_g", "a_w_o", "a_ffn_norm_g", "a_w_gu", "a_w_down",
             "kv_norm_g", "w_kv", "k_norm_g", "b_norm_g", "b_w_q", "b_q_norm_g", "b_rel_bias", "b_w_o",
             "b_ffn_norm_g", "b_w_gu", "b_w_down")
WEIGHT_NAMES = ARG_NAMES[1:]


def _big_shard(a):
    return a[0] if a.ndim == 3 else a


def kernel(x, a_norm_g, a_w_in, a_gn_g, a_w_o, a_ffn_norm_g, a_w_gu, a_w_down, kv_norm_g, w_kv, k_norm_g, b_norm_g, b_w_q, b_q_norm_g, b_rel_bias, b_w_o, b_ffn_norm_g, b_w_gu, b_w_down, loss_target, m_a_norm_g, m_a_w_in, m_a_gn_g, m_a_w_o, m_a_ffn_norm_g, m_a_w_gu, m_a_w_down, m_kv_norm_g, m_w_kv, m_k_norm_g, m_b_norm_g, m_b_w_q, m_b_q_norm_g, m_b_rel_bias, m_b_w_o, m_b_ffn_norm_g, m_b_w_gu, m_b_w_down, v_a_norm_g, v_a_w_in, v_a_gn_g, v_a_w_o, v_a_ffn_norm_g, v_a_w_gu, v_a_w_down, v_kv_norm_g, v_w_kv, v_k_norm_g, v_b_norm_g, v_b_w_q, v_b_q_norm_g, v_b_rel_bias, v_b_w_o, v_b_ffn_norm_g, v_b_w_gu, v_b_w_down):
    args = (x, a_norm_g, a_w_in, a_gn_g, a_w_o, a_ffn_norm_g, a_w_gu, a_w_down, kv_norm_g, w_kv, k_norm_g,
            b_norm_g, b_w_q, b_q_norm_g, b_rel_bias, b_w_o, b_ffn_norm_g, b_w_gu, b_w_down)
    p = dict(zip(ARG_NAMES, args))
    m_all = dict(zip(WEIGHT_NAMES, (m_a_norm_g, m_a_w_in, m_a_gn_g, m_a_w_o, m_a_ffn_norm_g, m_a_w_gu,
                                    m_a_w_down, m_kv_norm_g, m_w_kv, m_k_norm_g, m_b_norm_g, m_b_w_q,
                                    m_b_q_norm_g, m_b_rel_bias, m_b_w_o, m_b_ffn_norm_g, m_b_w_gu, m_b_w_down)))
    v_all = dict(zip(WEIGHT_NAMES, (v_a_norm_g, v_a_w_in, v_a_gn_g, v_a_w_o, v_a_ffn_norm_g, v_a_w_gu,
                                    v_a_w_down, v_kv_norm_g, v_w_kv, v_k_norm_g, v_b_norm_g, v_b_w_q,
                                    v_b_q_norm_g, v_b_rel_bias, v_b_w_o, v_b_ffn_norm_g, v_b_w_gu, v_b_w_down)))
    xi, yi, ci = _my_place()
    me = 4 * xi + 2 * yi + ci
    big_names = [n for n, _ in BIG]
    axis_of = dict(BIG)

    big_local = {n: _big_shard(p[n]) for n in big_names}
    shards = {n: a.astype(BF16) for n, a in big_local.items()}
    small_local = _pack_small({n: p[n] for n, _, _ in SMALL})
    w_in, small_all = _exchange(_GatherRider([shards["a_w_in"], small_local]), "gather_in")
    flat_g = small_all.reshape(N_DEV, -1)
    s_full, pos = {}, 0
    for n, length, sharded in SMALL:
        ln = length // N_DEV if sharded else length
        s_full[n] = flat_g[:, pos:pos + ln].reshape(1, -1) if sharded else p[n].reshape(1, -1)
        pos += ln

    parity = jnp.reshape(ci, (1,)).astype(jnp.int32)
    loss, grad_x, recv, g = _local_step(x[0], loss_target[0], shards, w_in, s_full, parity)
    loss = lax.psum(loss[0, 0], ("x", "y", "c"))

    g_small_all = _exchange(_GatherRider([_pack_small({n: g[n] for n, _, _ in SMALL})]), "gather_gsmall")[0]
    g_small = _unpack_small(_sum_leading(g_small_all, "gsmall_sum"), local=False)
    for n, length, sharded in SMALL:
        if sharded:
            g_small[n] = lax.dynamic_slice(g_small[n], (me * (length // N_DEV),), (length // N_DEV,))

    grads, deltas, new_m, new_v = {}, {}, {}, {}
    for n in big_names:
        outs = _adamw(big_local[n], recv[n], _big_shard(m_all[n]), _big_shard(v_all[n]), "adamw_" + n)
        grads[n], deltas[n], new_m[n], new_v[n] = (a.reshape(p[n].shape) for a in outs)
    pk = lambda src: _pack_small({n: src[n] for n, _, _ in SMALL})
    outs = _adamw(small_local, pk(g_small)[None], pk(m_all), pk(v_all), "adamw_small")
    g_s, d_s, nm_s, nv_s = (_unpack_small(a, local=True) for a in outs)
    for n, _, _ in SMALL:
        grads[n], deltas[n], new_m[n], new_v[n] = (a[n].reshape(p[n].shape) for a in (g_s, d_s, nm_s, nv_s))

    return (loss, grad_x[None], *[grads[n] for n in WEIGHT_NAMES], *[deltas[n] for n in WEIGHT_NAMES],
            *[new_m[n] for n in WEIGHT_NAMES], *[new_v[n] for n in WEIGHT_NAMES])
```

```python
import numpy as np
import jax
import jax.numpy as jnp
from jax import lax
from jax.experimental import pallas as pl
from jax.experimental.pallas import tpu as pltpu

F32 = jnp.float32
BF16 = jnp.bfloat16

N_DEV = 8
D_MODEL = 1024
CHUNK = 64
EPS = 1e-6
RET_HEADS, RET_DK, RET_DV = 4, 256, 512
RET_STEP = 2
RET_Q_COLS = RET_HEADS * RET_DK
RET_V_COLS = RET_HEADS * RET_DV
ATT_HEADS, ATT_DH = 16, 64
PAST_CHUNKS = 8
REL_CLIP = 256
REL_TABLE = 2 * REL_CLIP + 1
FFN_HIDDEN = 2816
ROPE_BASE = 10000.0
LANES = 128
Q_BLOCK = 256
ATT_ROWS = 32
K_PAD = PAST_CHUNKS * CHUNK
K_WINDOW = Q_BLOCK + K_PAD
REL_BLK = 128
REL_DELTAS = Q_BLOCK // REL_BLK + K_WINDOW // REL_BLK - 1
REL_PAD = 640
NEG = -1e30
VMEM_LIMIT_V7X = 56 * 1024 * 1024
ADAM_LR, ADAM_B1, ADAM_B2, ADAM_EPS, ADAM_WD, ADAM_STEP = 1e-3, 0.9, 0.999, 1e-8, 0.01, 10
MESH = pl.DeviceIdType.MESH
ANY = pl.BlockSpec(memory_space=pl.ANY)


def _params(*semantics):
    return pltpu.CompilerParams(dimension_semantics=semantics, vmem_limit_bytes=VMEM_LIMIT_V7X)


def _pick(dim, cap, align):
    best = None
    for t in range(align, min(dim, cap) + 1, align):
        if dim % t == 0:
            best = t
    assert best is not None, (dim, cap, align)
    return best


def _dot(a, b):
    return lax.dot_general(a, b, (((1,), (0,)), ((), ())), preferred_element_type=F32)


def _dot_nt(a, b):
    return lax.dot_general(a, b, (((1,), (1,)), ((), ())), preferred_element_type=F32)


def _dot_tn(a, b):
    return lax.dot_general(a, b, (((0,), (0,)), ((), ())), preferred_element_type=F32)


def _split2(x):
    hi = x.astype(BF16)
    lo = (x - hi.astype(F32)).astype(BF16)
    return hi, lo


def _split3(x):
    hi = x.astype(BF16)
    r = x - hi.astype(F32)
    mid = r.astype(BF16)
    lo = (r - mid.astype(F32)).astype(BF16)
    return hi, mid, lo


def _sigmoid(x):
    return 1.0 / (1.0 + jnp.exp(-x))


def _accumulate(ref, part, step):
    @pl.when(step == 0)
    def _():
        ref[...] = part

    @pl.when(step > 0)
    def _():
        ref[...] += part


RELAY_AT_NUM, RELAY_AT_DEN = 3, 4


def _my_place():
    return lax.axis_index("x"), lax.axis_index("y"), lax.axis_index("c")


def _flip(v, bit):
    return 1 - v if bit else v


class _NoRelay:
    def relay(self, in_refs, out_refs, sems):
        pass


class _GatherRider:
    def __init__(self, xs):
        self.inputs = list(xs)
        n = len(xs)
        self.out_shape = [jax.ShapeDtypeStruct((N_DEV,) + x.shape, x.dtype) for x in xs]
        self.scratch = [pltpu.SemaphoreType.DMA((7, n)), pltpu.SemaphoreType.DMA((7, n)),
                        pltpu.SemaphoreType.DMA((n,))]
        self.results = None

    def _copies(self, x_refs, out_refs, sems):
        send_sems, recv_sems, local_sems = sems
        n = len(x_refs)
        x, y, c = _my_place()
        me, sibling = (x, y, c), (x, y, 1 - c)
        chips = [(1 - x, y), (x, 1 - y), (1 - x, 1 - y)]

        def slot(a, px, py, pc):
            return out_refs[a].at[4 * px + 2 * py + pc]

        def copy(k, a, block, to, own=False):
            return pltpu.make_async_remote_copy(
                src_ref=x_refs[a] if own else slot(a, *block), dst_ref=slot(a, *block),
                send_sem=send_sems.at[k, a], recv_sem=recv_sems.at[k, a],
                device_id=to, device_id_type=MESH)

        mine = [pltpu.make_async_copy(x_refs[a], slot(a, *me), local_sems.at[a]) for a in range(n)]
        first = []
        for a in range(n):
            first.append(copy(0, a, me, sibling, own=True))
            first += [copy(1 + j, a, me, (*chip, c), own=True) for j, chip in enumerate(chips)]
        return n, c, me, sibling, chips, copy, mine, first

    def start(self, x_refs, out_refs, sems):
        _, _, _, _, _, _, mine, first = self._copies(x_refs, out_refs, sems)
        for cp in mine + first:
            cp.start()

    def relay(self, x_refs, out_refs, sems):
        n, c, me, sibling, chips, copy, _, _ = self._copies(x_refs, out_refs, sems)
        for j, chip in enumerate(chips):
            for a in range(n):
                copy(1 + j, a, (*chip, c), me).wait_recv()
                copy(4 + j, a, (*chip, c), sibling).start()

    def finish(self, x_refs, out_refs, sems):
        n, c, me, sibling, chips, copy, mine, first = self._copies(x_refs, out_refs, sems)
        passed = [copy(4 + j, a, (*chip, c), sibling) for j, chip in enumerate(chips) for a in range(n)]
        for a in range(n):
            copy(0, a, sibling, me).wait_recv()
            for j, chip in enumerate(chips):
                copy(4 + j, a, (*chip, 1 - c), me).wait_recv()
        for cp in first + passed:
            cp.wait_send()
        for cp in mine:
            cp.wait()


class _ScatterRider(_NoRelay):
    def __init__(self, gs):
        self.inputs = list(gs)
        n = len(gs)
        self.out_shape = [jax.ShapeDtypeStruct(g.shape, g.dtype) for g in gs]
        self.scratch = [pltpu.SemaphoreType.DMA((7, n)), pltpu.SemaphoreType.DMA((7, n)),
                        pltpu.SemaphoreType.DMA((n,))]
        self.results = None

    def _copies(self, g_refs, out_refs, sems):
        send_sems, recv_sems, local_sems = sems
        x, y, c = _my_place()
        me = 4 * x + 2 * y + c
        mine, copies = [], []
        for a in range(len(g_refs)):
            mine.append(pltpu.make_async_copy(g_refs[a].at[me], out_refs[a].at[me], local_sems.at[a]))
            for k in range(1, N_DEV):
                px, py, pc = _flip(x, k & 4), _flip(y, k & 2), _flip(c, k & 1)
                copies.append(pltpu.make_async_remote_copy(
                    src_ref=g_refs[a].at[4 * px + 2 * py + pc], dst_ref=out_refs[a].at[me],
                    send_sem=send_sems.at[k - 1, a], recv_sem=recv_sems.at[k - 1, a],
                    device_id=(px, py, pc), device_id_type=MESH))
        return mine, copies

    def start(self, g_refs, out_refs, sems):
        mine, copies = self._copies(g_refs, out_refs, sems)
        for cp in mine + copies:
            cp.start()

    def finish(self, g_refs, out_refs, sems):
        mine, copies = self._copies(g_refs, out_refs, sems)
        for cp in copies + mine:
            cp.wait()


class _SiblingSwapRider(_NoRelay):
    def __init__(self, gs):
        self.inputs = list(gs)
        n = len(gs)
        self.out_shape = [jax.ShapeDtypeStruct((4,) + g.shape[1:], g.dtype) for g in gs]
        self.scratch = [pltpu.SemaphoreType.DMA((4, n)), pltpu.SemaphoreType.DMA((4, n))]
        self.results = None

    def _copies(self, g_refs, out_refs, sems):
        send_sems, recv_sems = sems
        x, y, c = _my_place()
        return [pltpu.make_async_remote_copy(
            src_ref=g_refs[a].at[2 * k + 1 - c], dst_ref=out_refs[a].at[k],
            send_sem=send_sems.at[k, a], recv_sem=recv_sems.at[k, a],
            device_id=(x, y, 1 - c), device_id_type=MESH)
            for a in range(len(g_refs)) for k in range(4)]

    def start(self, g_refs, out_refs, sems):
        for cp in self._copies(g_refs, out_refs, sems):
            cp.start()

    def finish(self, g_refs, out_refs, sems):
        for cp in self._copies(g_refs, out_refs, sems):
            cp.wait()


class _ChipScatterRider(_NoRelay):
    def __init__(self, ps):
        self.inputs = list(ps)
        n = len(ps)
        self.out_shape = [jax.ShapeDtypeStruct(p.shape, p.dtype) for p in ps]
        self.scratch = [pltpu.SemaphoreType.DMA((3, n)), pltpu.SemaphoreType.DMA((3, n)),
                        pltpu.SemaphoreType.DMA((n,))]
        self.results = None

    def _copies(self, p_refs, out_refs, sems):
        send_sems, recv_sems, local_sems = sems
        x, y, c = _my_place()
        my_chip = 2 * x + y
        chips = [(1 - x, y), (x, 1 - y), (1 - x, 1 - y)]
        n = len(p_refs)
        mine = [pltpu.make_async_copy(p_refs[a].at[my_chip], out_refs[a].at[my_chip], local_sems.at[a])
                for a in range(n)]
        copies = [pltpu.make_async_remote_copy(
            src_ref=p_refs[a].at[2 * cx + cy], dst_ref=out_refs[a].at[my_chip],
            send_sem=send_sems.at[j, a], recv_sem=recv_sems.at[j, a],
            device_id=(cx, cy, c), device_id_type=MESH)
            for a in range(n) for j, (cx, cy) in enumerate(chips)]
        return mine, copies

    def start(self, p_refs, out_refs, sems):
        mine, copies = self._copies(p_refs, out_refs, sems)
        for cp in mine + copies:
            cp.start()

    def finish(self, p_refs, out_refs, sems):
        mine, copies = self._copies(p_refs, out_refs, sems)
        for cp in copies + mine:
            cp.wait()


def _call(body, name, grid, in_specs, out_specs, out_shape, scratch, semantics, args, rider=None):
    in_specs, out_specs, out_shape, scratch = list(in_specs), list(out_specs), list(out_shape), list(scratch)
    if rider is None:
        return list(pl.pallas_call(
            body, name=name, grid=grid, in_specs=in_specs, out_specs=out_specs, out_shape=out_shape,
            scratch_shapes=scratch, compiler_params=_params(*semantics))(*args))
    n_in, n_out, n_scr = len(in_specs), len(out_specs), len(scratch)
    r_in, r_out = len(rider.inputs), len(rider.out_shape)

    def wrapped(*refs):
        cuts = np.cumsum([0, n_in, r_in, n_out, r_out, n_scr])
        hi, ri, ho, ro, hs = (refs[cuts[i]:cuts[i + 1]] for i in range(5))
        rs = refs[cuts[5]:]
        step, steps = pl.program_id(0), grid[0]
        for d in range(1, len(grid)):
            step, steps = step * grid[d] + pl.program_id(d), steps * grid[d]

        @pl.when(step == 0)
        def _():
            rider.start(ri, ro, rs)

        body(*hi, *ho, *hs)

        @pl.when(step == (steps * RELAY_AT_NUM) // RELAY_AT_DEN)
        def _():
            rider.relay(ri, ro, rs)

        @pl.when(step == steps - 1)
        def _():
            rider.finish(ri, ro, rs)

    outs = pl.pallas_call(
        wrapped, name=name, grid=grid,
        in_specs=in_specs + [ANY] * r_in, out_specs=out_specs + [ANY] * r_out,
        out_shape=out_shape + rider.out_shape, scratch_shapes=scratch + rider.scratch,
        compiler_params=_params(*(["arbitrary"] * len(grid))),
    )(*args, *rider.inputs)
    rider.results = list(outs[n_out:])
    return list(outs[:n_out])


def _exchange(rider, name):
    r_in, r_out = len(rider.inputs), len(rider.out_shape)

    def body(*refs):
        ri, ro, rs = refs[:r_in], refs[r_in:r_in + r_out], refs[r_in + r_out:]
        rider.start(ri, ro, rs)
        rider.relay(ri, ro, rs)
        rider.finish(ri, ro, rs)

    return list(pl.pallas_call(
        body, name=name, in_specs=[ANY] * r_in, out_specs=[ANY] * r_out,
        out_shape=rider.out_shape, scratch_shapes=rider.scratch)(*rider.inputs))


MM_CAP_MN = 1024
MM_CAP_N = 1536
MM_CAP_K = 3072
MM_CAP_K_TOKENS = 2048
MM_CAP_K_RMS = 1536
NORM_ROWS = 256


def _mm(a, b, mode, name, out_dtype=F32, res=None, out_block=None, epilogue=None, extra=None, norm_g=None,
        norm_b=False, rider=None):
    a3, b3 = a.ndim == 3, b.ndim == 3
    um = un = uk = None
    if mode in ("nn", "nt"):
        if a3:
            m, uk = a.shape[1:]
            k = a.shape[0] * uk
        else:
            m, k = a.shape
    else:
        if a3:
            k, um = a.shape[1:]
            m = a.shape[0] * um
        else:
            k, m = a.shape
    if mode in ("nn", "tn"):
        if b3:
            kb, un = b.shape[1:]
            n = b.shape[0] * un
        else:
            kb, n = b.shape
        assert kb == k, (a.shape, b.shape, mode)
    else:
        if b3:
            n, ukb = b.shape[1:]
            assert b.shape[0] * ukb == k and uk in (None, ukb), (a.shape, b.shape, mode)
            uk = ukb
        else:
            n, kb = b.shape
            assert kb == k, (a.shape, b.shape, mode)
    if out_block is not None:
        assert un in (None, out_block)
        un = out_block

    def tile(dim, unit, cap, align):
        if unit is None:
            return _pick(dim, cap, align), 1
        c = max(1, cap // unit)
        while (dim // unit) % c:
            c -= 1
        return unit, c

    um, cm = tile(m, um, MM_CAP_MN if mode != "tn" else 1408, 128 if mode == "tn" else 16)
    un, cn = tile(n, un, MM_CAP_N, 128)
    cap_k = MM_CAP_K_TOKENS if mode == "tn" else (MM_CAP_K_RMS if epilogue == "rms_bwd" else MM_CAP_K)
    uk, ck = tile(k, uk, cap_k, 128)
    if epilogue == "rms_bwd":
        assert mode != "tn" and n == D_MODEL and cm == cn == 1 and res is None and out_block is None
    if norm_g is not None and norm_b:
        assert mode == "tn" and not b3 and n == D_MODEL and cn == 1
    elif norm_g is not None:
        assert not a3 and (m if mode == "tn" else k) == D_MODEL and (cm if mode == "tn" else ck) == 1
    if epilogue == "swiglu":
        assert res is None and ((mode == "nn" and b3 and out_block is None) or
                                (mode == "nt" and not b3 and out_block is not None))
        cn = 2
    if epilogue == "swiglu_bwd":
        assert mode == "nt" and out_block is not None and extra is not None and res is None
        cn = 1
    tm, tn, tk = cm * um, cn * un, ck * uk
    nk = k // tk
    dot = {"nn": _dot, "nt": _dot_nt, "tn": _dot_tn}[mode]
    half = n // un // 2
    blocked_out = out_block is not None or epilogue in ("swiglu", "swiglu_bwd")
    extras = [] if extra is None else (list(extra) if isinstance(extra, (tuple, list)) else [extra])

    def sl(idx, unit, count):
        return slice(None) if count == 1 else slice(idx * unit, (idx + 1) * unit)

    def body(*refs):
        a_ref, b_ref = refs[0], refs[1]
        pos = 2
        r_ref = ng_ref = None
        if res is not None:
            r_ref, pos = refs[pos], pos + 1
        e_refs, pos = refs[pos:pos + len(extras)], pos + len(extras)
        if norm_g is not None:
            ng_ref, pos = refs[pos], pos + 1
        outs, acc_ref = refs[pos:-1], refs[-1]
        kk = pl.program_id(2)

        def normed(x_ref):
            groups = []
            for r in range(0, x_ref.shape[0], NORM_ROWS):
                xv = x_ref[r:r + NORM_ROWS, :]
                rstd = lax.rsqrt(jnp.mean(xv * xv, axis=-1, keepdims=True) + EPS)
                groups.append((xv * rstd * ng_ref[...]).astype(BF16))
            return jnp.concatenate(groups, axis=0)

        def a_blk(mi, ki):
            if norm_g is not None and not norm_b:
                return normed(a_ref)
            if mode in ("nn", "nt"):
                return a_ref[ki] if a3 else a_ref[:, sl(ki, uk, ck)]
            return a_ref[mi] if a3 else a_ref[:, sl(mi, um, cm)]

        def b_blk(ki, ni):
            if norm_b:
                return normed(b_ref)
            if epilogue == "swiglu":
                return b_ref[ni, 0]
            if mode in ("nn", "tn"):
                return b_ref[ni] if b3 else b_ref[sl(ki, uk, ck), sl(ni, un, cn)]
            return b_ref[ki][sl(ni, un, cn), :] if b3 else b_ref[sl(ni, un, cn), sl(ki, uk, ck)]

        parts = {}
        for mi in range(cm):
            for ni in range(cn):
                part = None
                for ki in range(ck):
                    d = dot(a_blk(mi, ki).astype(BF16), b_blk(ki, ni).astype(BF16))
                    part = d if part is None else part + d
                parts[mi, ni] = part

        def finish(total):
            if epilogue == "swiglu":
                gate, up = total[0, 0], total[0, 1]
                outs[0][0, 0] = gate.astype(BF16)
                outs[0][1, 0] = up.astype(BF16)
                outs[1][0] = (gate * _sigmoid(gate) * up).astype(BF16)
                return
            if epilogue == "swiglu_bwd":
                dact = total[0, 0]
                gate, up = e_refs[0][0, 0].astype(F32), e_refs[0][1, 0].astype(F32)
                sg = _sigmoid(gate)
                outs[0][0, 0] = (dact * up * (sg * (1.0 + gate * (1.0 - sg)))).astype(BF16)
                outs[0][1, 0] = (dact * (gate * sg)).astype(BF16)
                return
            if epilogue == "rms_bwd":
                x_ref, g_ref, dres_ref = e_refs
                dh, dg = total[0, 0], None
                for r in range(0, tm, NORM_ROWS):
                    rows = slice(r, r + NORM_ROWS)
                    xv, dhv = x_ref[rows, :], dh[rows, :]
                    rstd = lax.rsqrt(jnp.mean(xv * xv, axis=-1, keepdims=True) + EPS)
                    xh = xv * rstd
                    dyg = dhv * g_ref[...]
                    c = jnp.mean(dyg * xh, axis=-1, keepdims=True)
                    outs[0][rows, :] = dres_ref[rows, :] + rstd * (dyg - xh * c)
                    part = jnp.sum(dhv * xh, axis=0, keepdims=True)
                    dg = part if dg is None else dg + part
                _accumulate(outs[1], dg, pl.program_id(0))
                return
            for (mi, ni), val in total.items():
                rows, cols = sl(mi, um, cm), sl(ni, un, cn)
                if res is not None:
                    val = r_ref[rows, cols] + val
                if blocked_out:
                    outs[0][ni, rows] = val.astype(out_dtype)
                else:
                    outs[0][rows, cols] = val.astype(out_dtype)

        if nk == 1:
            finish(parts)
        else:
            @pl.when(kk == 0)
            def _():
                for (mi, ni), val in parts.items():
                    acc_ref[mi * cn + ni] = val

            @pl.when(jnp.logical_and(kk > 0, kk < nk - 1))
            def _():
                for (mi, ni), val in parts.items():
                    acc_ref[mi * cn + ni] += val

            @pl.when(kk == nk - 1)
            def _():
                finish({key: acc_ref[key[0] * cn + key[1]] + val for key, val in parts.items()})

    if mode in ("nn", "nt"):
        a_spec = (pl.BlockSpec((ck, tm, uk), lambda i, j, kk: (kk, i, 0)) if a3
                  else pl.BlockSpec((tm, tk), lambda i, j, kk: (i, kk)))
    else:
        a_spec = (pl.BlockSpec((cm, tk, um), lambda i, j, kk: (i, kk, 0)) if a3
                  else pl.BlockSpec((tk, tm), lambda i, j, kk: (kk, i)))
    pair_spec = pl.BlockSpec((2, 1, tm, un), lambda i, j, kk: (0, j, i, 0))
    row_spec = pl.BlockSpec((tm, tn), lambda i, j, kk: (i, 0))
    vec_spec = pl.BlockSpec((1, tn), lambda i, j, kk: (0, 0))
    if epilogue == "swiglu" and mode == "nn":
        b = b.reshape(2, half, k, un)
        b_spec = pl.BlockSpec((2, 1, tk, un), lambda i, j, kk: (0, j, kk, 0))
    elif epilogue == "swiglu":
        b = b.reshape(2, half, un, k)
        b_spec = pl.BlockSpec((2, 1, un, tk), lambda i, j, kk: (0, j, 0, kk))
    elif mode in ("nn", "tn"):
        b_spec = (pl.BlockSpec((cn, tk, un), lambda i, j, kk: (j, kk, 0)) if b3
                  else pl.BlockSpec((tk, tn), lambda i, j, kk: (kk, j)))
    else:
        b_spec = (pl.BlockSpec((ck, tn, uk), lambda i, j, kk: (kk, j, 0)) if b3
                  else pl.BlockSpec((tn, tk), lambda i, j, kk: (j, kk)))
    if epilogue == "swiglu":
        out_specs = [pair_spec, pl.BlockSpec((1, tm, un), lambda i, j, kk: (j, i, 0))]
        out_shape = [jax.ShapeDtypeStruct((2, half, m, un), BF16), jax.ShapeDtypeStruct((half, m, un), BF16)]
    elif epilogue == "swiglu_bwd":
        out_specs = [pair_spec]
        out_shape = [jax.ShapeDtypeStruct(extra.shape, BF16)]
    elif epilogue == "rms_bwd":
        out_specs = [row_spec, vec_spec]
        out_shape = [jax.ShapeDtypeStruct((m, n), F32), jax.ShapeDtypeStruct((1, n), F32)]
    elif blocked_out:
        out_specs = [pl.BlockSpec((cn, tm, un), lambda i, j, kk: (j, i, 0))]
        out_shape = [jax.ShapeDtypeStruct((n // un, m, un), out_dtype)]
    else:
        out_specs = [pl.BlockSpec((tm, tn), lambda i, j, kk: (i, j))]
        out_shape = [jax.ShapeDtypeStruct((m, n), out_dtype)]
    in_specs, args = [a_spec, b_spec], [a, b]
    if res is not None:
        in_specs.append(pl.BlockSpec((tm, tn), lambda i, j, kk: (i, j)))
        args.append(res)
    if epilogue == "swiglu_bwd":
        in_specs.append(pair_spec)
    elif epilogue == "rms_bwd":
        in_specs += [row_spec, vec_spec, row_spec]
    args += extras
    if norm_g is not None:
        in_specs.append(pl.BlockSpec((1, D_MODEL), lambda i, j, kk: (0, 0)))
        args.append(norm_g)
    semantics = ("arbitrary",) * 3 if epilogue == "rms_bwd" else ("parallel", "parallel", "arbitrary")
    out = _call(body, name, (m // tm, n // tn, nk), in_specs, out_specs, out_shape,
                [pltpu.VMEM((cm * cn, um, un), F32)], semantics, args, rider)
    return out if epilogue in ("swiglu", "rms_bwd") else out[0]


def _head_sums(v, ind):
    hi, lo = _split2(v)
    return _dot(hi, ind) + _dot(lo, ind)


def _head_spread(per_head, ind):
    hi, lo = _split2(per_head)
    return _dot_nt(hi, ind) + _dot_nt(lo, ind)


def _head_rstd(xv, ind):
    return _head_spread(lax.rsqrt(_head_sums(xv * xv, ind) * (1.0 / ATT_DH) + EPS), ind)


def _hn_bwd_math(xv, gv, ind, dyv, scale):
    rstd = _head_rstd(xv, ind)
    xh = xv * rstd
    dyn = dyv * scale
    dyg = dyn * gv
    dx = rstd * (dyg - xh * _head_spread(_head_sums(dyg * xh, ind) * (1.0 / ATT_DH), ind))
    return dx, jnp.sum(dyn * xh, axis=0, keepdims=True)


def _q_hnorm(x, g_tiled, bd, scale, name):
    t, d = x.shape
    tm = _pick(t, 512, 16)

    def body(x_ref, g_ref, bd_ref, o_ref):
        xv = x_ref[...]
        o_ref[...] = (xv * _head_rstd(xv, bd_ref[...]) * g_ref[...] * scale).astype(BF16)

    return pl.pallas_call(
        body, name=name, grid=(t // tm,),
        in_specs=[pl.BlockSpec((tm, d), lambda i: (i, 0)), pl.BlockSpec((1, d), lambda i: (0, 0)),
                  pl.BlockSpec((d, LANES), lambda i: (0, 0))],
        out_specs=pl.BlockSpec((tm, d), lambda i: (i, 0)),
        out_shape=jax.ShapeDtypeStruct((t, d), BF16),
        compiler_params=_params("parallel"),
    )(x, g_tiled, bd)


def _q_dhnorm(x, g_tiled, bd, dy, scale, name):
    t, d = x.shape
    tm = _pick(t, 512, 16)

    def body(x_ref, g_ref, bd_ref, dy_ref, dx_ref, dg_ref):
        dx, part = _hn_bwd_math(x_ref[...], g_ref[...], bd_ref[...], dy_ref[...], scale)
        dx_ref[...] = dx.astype(BF16)
        _accumulate(dg_ref, part, pl.program_id(0))

    row = pl.BlockSpec((tm, d), lambda i: (i, 0))
    vec = pl.BlockSpec((1, d), lambda i: (0, 0))
    return pl.pallas_call(
        body, name=name, grid=(t // tm,),
        in_specs=[row, vec, pl.BlockSpec((d, LANES), lambda i: (0, 0)), row],
        out_specs=[row, vec],
        out_shape=[jax.ShapeDtypeStruct((t, d), BF16), jax.ShapeDtypeStruct((1, d), F32)],
        compiler_params=_params("arbitrary"),
    )(x, g_tiled, bd, dy)


def _kv_prep(kv, g_tiled, bd, name):
    t = kv.shape[0]
    d = D_MODEL
    tm = K_PAD
    assert t % tm == 0

    def body(k_ref, v_ref, g_ref, bd_ref, kp_ref, vp_ref):
        i = pl.program_id(0)

        @pl.when(i == 0)
        def _():
            kp_ref[...] = jnp.zeros_like(kp_ref)
            vp_ref[...] = jnp.zeros_like(vp_ref)

        @pl.when(i > 0)
        def _():
            xv = k_ref[...]
            kp_ref[...] = (xv * _head_rstd(xv, bd_ref[...]) * g_ref[...]).astype(BF16)
            vp_ref[...] = v_ref[...].astype(BF16)

    shp = jax.ShapeDtypeStruct((t + K_PAD, d), BF16)
    out = pl.BlockSpec((tm, d), lambda i: (i, 0))
    return pl.pallas_call(
        body, name=name, grid=(t // tm + 1,),
        in_specs=[pl.BlockSpec((tm, d), lambda i: (jnp.maximum(i - 1, 0), 0)),
                  pl.BlockSpec((tm, d), lambda i: (jnp.maximum(i - 1, 0), 1)),
                  pl.BlockSpec((1, d), lambda i: (0, 0)), pl.BlockSpec((d, LANES), lambda i: (0, 0))],
        out_specs=[out, out], out_shape=[shp, shp],
        compiler_params=_params("arbitrary"),
    )(kv, kv, g_tiled, bd)


def _kv_dprep(kv, g_tiled, bd, dkp_t, dvp_t, name):
    t = kv.shape[0]
    d = D_MODEL
    tm = K_PAD

    def body(k_ref, g_ref, bd_ref, dk_ref, dv_ref, o_ref, dg_ref):
        dx, part = _hn_bwd_math(k_ref[...], g_ref[...], bd_ref[...], dk_ref[...].T, 1.0)
        o_ref[:, :d] = dx.astype(BF16)
        o_ref[:, d:] = dv_ref[...].T.astype(BF16)
        _accumulate(dg_ref, part, pl.program_id(0))

    vec = pl.BlockSpec((1, d), lambda i: (0, 0))
    padded = pl.BlockSpec((d, tm), lambda i: (0, i + 1))
    return pl.pallas_call(
        body, name=name, grid=(t // tm,),
        in_specs=[pl.BlockSpec((tm, d), lambda i: (i, 0)), vec, pl.BlockSpec((d, LANES), lambda i: (0, 0)),
                  padded, padded],
        out_specs=[pl.BlockSpec((tm, 2 * d), lambda i: (i, 0)), vec],
        out_shape=[jax.ShapeDtypeStruct((t, 2 * d), BF16), jax.ShapeDtypeStruct((1, d), F32)],
        compiler_params=_params("arbitrary"),
    )(kv, g_tiled, bd, dkp_t, dvp_t)


def _loss_head(y, target, name):
    t, d = y.shape
    tm = _pick(t, 512, 16)

    def body(y_ref, t_ref, dy_ref, l_ref):
        diff = y_ref[...] - t_ref[...]
        dy_ref[...] = diff * (1.0 / d)
        part = jnp.sum(jnp.sum(diff * diff, axis=-1, keepdims=True), axis=0, keepdims=True) * (0.5 / d)
        _accumulate(l_ref, part, pl.program_id(0))

    row = pl.BlockSpec((tm, d), lambda i: (i, 0))
    return pl.pallas_call(
        body, name=name, grid=(t // tm,),
        in_specs=[row, row], out_specs=[row, pl.BlockSpec((1, 1), lambda i: (0, 0))],
        out_shape=[jax.ShapeDtypeStruct((t, d), F32), jax.ShapeDtypeStruct((1, 1), F32)],
        compiler_params=_params("arbitrary"),
    )(y, target)


def _ret_consts(t):
    h = np.arange(RET_HEADS, dtype=np.float32)
    lg = np.log(np.float32(1.0) - np.float32(2.0) ** (np.float32(-5.0) - h)).astype(np.float32)
    tt = np.arange(CHUNK, dtype=np.float32)
    intra = np.exp(lg[:, None, None] * np.abs(tt[:, None] - tt[None, :])).astype(np.float32)
    q_dec = np.exp(lg[:, None] * (tt + 1.0)).astype(np.float32)
    k_dec = np.exp(lg[:, None] * (CHUNK - 1.0 - tt)).astype(np.float32)
    s_dec = [float(v) for v in np.exp(lg * np.float32(CHUNK)).astype(np.float32)]
    qd = np.broadcast_to(q_dec[:, :, None], (RET_HEADS, CHUNK, RET_DK)).copy()
    kd = np.broadcast_to(k_dec[:, :, None], (RET_HEADS, CHUNK, RET_DK)).copy()
    half = RET_DK // 2
    inv_freq = ROPE_BASE ** (-jnp.arange(half, dtype=F32) / half)
    ang = jnp.arange(t).astype(F32)[:, None] * inv_freq[None, :]
    return jnp.asarray(intra), jnp.asarray(qd), jnp.asarray(kd), s_dec, jnp.cos(ang), jnp.sin(ang)


def _rope(x, cos, sin):
    half = RET_DK // 2
    x1, x2 = x[:, :half], x[:, half:]
    return jnp.concatenate([x1 * cos - x2 * sin, x1 * sin + x2 * cos], axis=-1)


def _unrope(d, cos, sin):
    half = RET_DK // 2
    d1, d2 = d[:, :half], d[:, half:]
    return jnp.concatenate([d1 * cos + d2 * sin, d2 * cos - d1 * sin], axis=-1)


def _ret_slices(h):
    q = slice(h * RET_DK, (h + 1) * RET_DK)
    k = slice(RET_Q_COLS + h * RET_DK, RET_Q_COLS + (h + 1) * RET_DK)
    v = slice(2 * RET_Q_COLS + h * RET_DV, 2 * RET_Q_COLS + (h + 1) * RET_DV)
    g = slice(2 * RET_Q_COLS + RET_V_COLS + h * RET_DV, 2 * RET_Q_COLS + RET_V_COLS + (h + 1) * RET_DV)
    o = slice(h * RET_DV, (h + 1) * RET_DV)
    return q, k, v, g, o


def _ret_fwd(proj, gn, consts, name, rider=None):
    t, cols = proj.shape
    n = t // CHUNK
    intra, qd, kd, s_dec, cos, sin = consts
    k_scale = RET_DK ** -0.5

    def body(p_ref, cos_ref, sin_ref, intra_ref, qd_ref, kd_ref, gn_ref, y_ref, o_ref, st_ref, state):
        i = pl.program_id(0)

        @pl.when(i == 0)
        def _():
            state[...] = jnp.zeros_like(state)

        for c in range(RET_STEP):
            rows = slice(c * CHUNK, (c + 1) * CHUNK)
            cosv, sinv = cos_ref[rows, :], sin_ref[rows, :]
            for h in range(RET_HEADS):
                qs, ks, vs, gs, os_ = _ret_slices(h)
                qr = _rope(p_ref[rows, qs], cosv, sinv)
                kr = _rope(p_ref[rows, ks], cosv, sinv) * k_scale
                vb = p_ref[rows, vs].astype(BF16)
                gv = p_ref[rows, gs]
                scores = _dot_nt(qr.astype(BF16), kr.astype(BF16)) * intra_ref[h]
                s_old = state[h]
                s_old_b = s_old.astype(BF16)
                st_ref[c, h] = s_old_b
                o = _dot(scores.astype(BF16), vb) + _dot((qr * qd_ref[h]).astype(BF16), s_old_b)
                state[h] = s_old * s_dec[h] + _dot_tn((kr * kd_ref[h]).astype(BF16), vb)
                rstd = lax.rsqrt(jnp.mean(o * o, axis=-1, keepdims=True) + EPS)
                on = o * rstd * gn_ref[:, os_]
                o_ref[rows, os_] = o
                y_ref[rows, os_] = (gv * _sigmoid(gv) * on).astype(BF16)

    full3 = lambda a: pl.BlockSpec(a.shape, lambda i: (0, 0, 0))
    step = RET_STEP * CHUNK
    return _call(
        body, name, (n // RET_STEP,),
        [pl.BlockSpec((step, cols), lambda i: (i, 0)),
         pl.BlockSpec((step, RET_DK // 2), lambda i: (i, 0)),
         pl.BlockSpec((step, RET_DK // 2), lambda i: (i, 0)),
         full3(intra), full3(qd), full3(kd),
         pl.BlockSpec((1, RET_V_COLS), lambda i: (0, 0))],
        [pl.BlockSpec((step, RET_V_COLS), lambda i: (i, 0)),
         pl.BlockSpec((step, RET_V_COLS), lambda i: (i, 0)),
         pl.BlockSpec((RET_STEP, RET_HEADS, RET_DK, RET_DV), lambda i: (i, 0, 0, 0))],
        [jax.ShapeDtypeStruct((t, RET_V_COLS), BF16),
         jax.ShapeDtypeStruct((t, RET_V_COLS), F32),
         jax.ShapeDtypeStruct((n, RET_HEADS, RET_DK, RET_DV), BF16)],
        [pltpu.VMEM((RET_HEADS, RET_DK, RET_DV), F32)], ("arbitrary",),
        (proj, cos, sin, intra, qd, kd, gn), rider)


def _ret_bwd(proj, gn, o_saved, states, dy, consts, name, rider=None):
    t, cols = proj.shape
    n = t // CHUNK
    intra, qd, kd, s_dec, cos, sin = consts
    k_scale = RET_DK ** -0.5

    def body(p_ref, cos_ref, sin_ref, intra_ref, qd_ref, kd_ref, gn_ref, o_ref, st_ref, dy_ref,
             dp_ref, dgn_ref, dstate):
        i = pl.program_id(0)

        @pl.when(i == 0)
        def _():
            dstate[...] = jnp.zeros_like(dstate)

        dgn = None
        for c in reversed(range(RET_STEP)):
            rows = slice(c * CHUNK, (c + 1) * CHUNK)
            cosv, sinv = cos_ref[rows, :], sin_ref[rows, :]
            dgn_parts = []
            for h in range(RET_HEADS):
                qs, ks, vs, gs, os_ = _ret_slices(h)
                qr = _rope(p_ref[rows, qs], cosv, sinv)
                kr = _rope(p_ref[rows, ks], cosv, sinv) * k_scale
                qb, kb = qr.astype(BF16), kr.astype(BF16)
                vb = p_ref[rows, vs].astype(BF16)
                gv = p_ref[rows, gs]
                ov = o_ref[rows, os_]
                dyv = dy_ref[rows, os_]
                gnv = gn_ref[:, os_]
                sg = _sigmoid(gv)
                rstd = lax.rsqrt(jnp.mean(ov * ov, axis=-1, keepdims=True) + EPS)
                oh = ov * rstd
                d_on = dyv * (gv * sg)
                dg = dyv * (oh * gnv) * (sg * (1.0 + gv * (1.0 - sg)))
                dgn_parts.append(jnp.sum(d_on * oh, axis=0, keepdims=True))
                d_oh = d_on * gnv
                do = rstd * (d_oh - oh * jnp.mean(d_oh * oh, axis=-1, keepdims=True))
                dob = do.astype(BF16)
                mask = intra_ref[h]
                a_b = (_dot_nt(qb, kb) * mask).astype(BF16)
                da_b = (_dot_nt(dob, vb) * mask).astype(BF16)
                ds_new = dstate[h]
                ds_new_b = ds_new.astype(BF16)
                s_old_b = st_ref[c, h]
                qdv, kdv = qd_ref[h], kd_ref[h]
                dv = _dot_tn(a_b, dob) + _dot((kr * kdv).astype(BF16), ds_new_b)
                dqr = _dot(da_b, kb) + _dot_nt(dob, s_old_b) * qdv
                dkr = _dot_tn(da_b, qb) + _dot_nt(vb, ds_new_b) * kdv
                dstate[h] = ds_new * s_dec[h] + _dot_tn((qr * qdv).astype(BF16), dob)
                dp_ref[rows, qs] = _unrope(dqr, cosv, sinv).astype(BF16)
                dp_ref[rows, ks] = _unrope(dkr * k_scale, cosv, sinv).astype(BF16)
                dp_ref[rows, vs] = dv.astype(BF16)
                dp_ref[rows, gs] = dg.astype(BF16)
            part = jnp.concatenate(dgn_parts, axis=-1)
            dgn = part if dgn is None else dgn + part
        _accumulate(dgn_ref, dgn, i)

    steps = n // RET_STEP
    step = RET_STEP * CHUNK
    rev = lambda i: (steps - 1 - i, 0)
    full3 = lambda a: pl.BlockSpec(a.shape, lambda i: (0, 0, 0))
    return _call(
        body, name, (steps,),
        [pl.BlockSpec((step, cols), rev),
         pl.BlockSpec((step, RET_DK // 2), rev),
         pl.BlockSpec((step, RET_DK // 2), rev),
         full3(intra), full3(qd), full3(kd),
         pl.BlockSpec((1, RET_V_COLS), lambda i: (0, 0)),
         pl.BlockSpec((step, RET_V_COLS), rev),
         pl.BlockSpec((RET_STEP, RET_HEADS, RET_DK, RET_DV), lambda i: (steps - 1 - i, 0, 0, 0)),
         pl.BlockSpec((step, RET_V_COLS), rev)],
        [pl.BlockSpec((step, cols), rev),
         pl.BlockSpec((1, RET_V_COLS), lambda i: (0, 0))],
        [jax.ShapeDtypeStruct((t, cols), BF16),
         jax.ShapeDtypeStruct((1, RET_V_COLS), F32)],
        [pltpu.VMEM((RET_HEADS, RET_DK, RET_DV), F32)], ("arbitrary",),
        (proj, cos, sin, intra, qd, kd, gn, o_saved, states, dy), rider)


def _att_common(q_ref, kp_ref, vp_ref):
    blk = pl.program_id(1)
    start = pl.multiple_of(blk * Q_BLOCK, Q_BLOCK)
    kw = kp_ref[pl.ds(start, K_WINDOW), :]
    vw = vp_ref[pl.ds(start, K_WINDOW), :]
    kvalid = blk * Q_BLOCK - K_PAD + lax.broadcasted_iota(jnp.int32, (1, K_WINDOW), 1) >= 0
    lane = lax.broadcasted_iota(jnp.int32, (1, LANES), 1)
    return start, q_ref[...], kw, vw, kvalid, (lane < ATT_DH, lane >= ATT_DH)


def _row_groups():
    return [slice(r * ATT_ROWS, (r + 1) * ATT_ROWS) for r in range(Q_BLOCK // ATT_ROWS)]


def _lane_copies(x):
    return jnp.tile(x, (1, K_WINDOW // LANES))


def _att_specs(t, tp):
    qspec = pl.BlockSpec((Q_BLOCK, LANES), lambda h, i: (i, h))
    kspec = pl.BlockSpec((tp, LANES), lambda h, i: (0, h))
    bspec = pl.BlockSpec((2, Q_BLOCK, K_WINDOW), lambda h, i: (h, 0, 0))
    return qspec, kspec, bspec


def _att_fwd(q, kp, vp, bias, name, rider=None):
    t, d = q.shape
    tp = kp.shape[0]

    def body(q_ref, kp_ref, vp_ref, bias_ref, o_ref, lse_ref, s_scr, p_scr, lse_scr):
        _, q2, kw, vw, kvalid, sel = _att_common(q_ref, kp_ref, vp_ref)
        for hh in range(2):
            s_scr[hh] = _dot_nt(jnp.where(sel[hh], q2, 0), kw)
        for hh in range(2):
            for rows in _row_groups():
                s = jnp.where(kvalid, s_scr[hh, rows, :] + bias_ref[hh, rows, :], NEG)
                m = jnp.max(s, axis=-1, keepdims=True)
                e = jnp.exp(s - m)
                l = jnp.sum(e, axis=-1, keepdims=True)
                p_scr[hh, rows, :] = (e * (1.0 / l)).astype(BF16)
                lse_scr[hh, rows, :] = jnp.broadcast_to(m + jnp.log(l), (ATT_ROWS, LANES))
        outs = [_dot(p_scr[hh], vw) for hh in range(2)]
        o_ref[...] = jnp.where(sel[0], outs[0], outs[1]).astype(BF16)
        lse_ref[...] = jnp.where(sel[0], lse_scr[0], lse_scr[1])

    qspec, kspec, bspec = _att_specs(t, tp)
    return _call(body, name, (d // LANES, t // Q_BLOCK), [qspec, kspec, kspec, bspec], [qspec, qspec],
                 [jax.ShapeDtypeStruct((t, d), BF16), jax.ShapeDtypeStruct((t, d), F32)],
                 [pltpu.VMEM((2, Q_BLOCK, K_WINDOW), F32), pltpu.VMEM((2, Q_BLOCK, K_WINDOW), BF16),
                  pltpu.VMEM((2, Q_BLOCK, LANES), F32)],
                 ("parallel", "arbitrary"), (q, kp, vp, bias), rider)


def _att_bwd(q, kp, vp, bias, do, o, lse, name, rider=None):
    t, d = q.shape
    tp = kp.shape[0]

    def body(q_ref, kp_ref, vp_ref, bias_ref, do_ref, o_ref, lse_ref, dq_ref, dkp_ref, dvp_ref, db_ref,
             s_scr, dp_scr, p_scr, ds_scr, row_scr):
        @pl.when(pl.program_id(1) == 0)
        def _():
            dkp_ref[...] = jnp.zeros_like(dkp_ref)
            dvp_ref[...] = jnp.zeros_like(dvp_ref)
            db_ref[...] = jnp.zeros_like(db_ref)

        start, q2, kw, vw, kvalid, sel = _att_common(q_ref, kp_ref, vp_ref)
        do2 = do_ref[...]
        qm = [jnp.where(sel[hh], q2, 0) for hh in range(2)]
        dom = [jnp.where(sel[hh], do2, 0) for hh in range(2)]
        do_o = do2.astype(F32) * o_ref[...].astype(F32)
        lse2 = lse_ref[...]
        for hh in range(2):
            s_scr[hh] = _dot_nt(qm[hh], kw)
            dp_scr[hh] = _dot_nt(dom[hh], vw)
            lse_h = jnp.max(jnp.where(sel[hh], lse2, NEG), axis=-1, keepdims=True)
            delta = jnp.sum(jnp.where(sel[hh], do_o, 0.0), axis=-1, keepdims=True)
            row_scr[hh, 0] = jnp.broadcast_to(lse_h, (Q_BLOCK, LANES))
            row_scr[hh, 1] = jnp.broadcast_to(delta, (Q_BLOCK, LANES))
        for hh in range(2):
            for rows in _row_groups():
                s = jnp.where(kvalid, s_scr[hh, rows, :] + bias_ref[hh, rows, :], NEG)
                p = jnp.exp(s - _lane_copies(row_scr[hh, 0, rows, :]))
                ds = p * (dp_scr[hh, rows, :] - _lane_copies(row_scr[hh, 1, rows, :]))
                db_ref[hh, rows, :] += ds
                p_scr[hh, rows, :] = p.astype(BF16)
                ds_scr[hh, rows, :] = ds.astype(BF16)
        dqs = [_dot(ds_scr[hh], kw) for hh in range(2)]
        dq_ref[...] = jnp.where(sel[0], dqs[0], dqs[1])
        dkp_ref[:, pl.ds(start, K_WINDOW)] += _dot_tn(qm[0], ds_scr[0]) + _dot_tn(qm[1], ds_scr[1])
        dvp_ref[:, pl.ds(start, K_WINDOW)] += _dot_tn(dom[0], p_scr[0]) + _dot_tn(dom[1], p_scr[1])

    qspec, kspec, bspec = _att_specs(t, tp)
    tspec = pl.BlockSpec((LANES, tp), lambda h, i: (h, 0))
    stage = lambda dt: pltpu.VMEM((2, Q_BLOCK, K_WINDOW), dt)
    return _call(body, name, (d // LANES, t // Q_BLOCK),
                 [qspec, kspec, kspec, bspec, qspec, qspec, qspec],
                 [qspec, tspec, tspec, bspec],
                 [jax.ShapeDtypeStruct((t, d), F32),
                  jax.ShapeDtypeStruct((d, tp), F32),
                  jax.ShapeDtypeStruct((d, tp), F32),
                  jax.ShapeDtypeStruct((ATT_HEADS, Q_BLOCK, K_WINDOW), F32)],
                 [stage(F32), stage(F32), stage(BF16), stage(BF16),
                  pltpu.VMEM((2, 2, Q_BLOCK, LANES), F32)],
                 ("parallel", "arbitrary"), (q, kp, vp, bias, do, o, lse), rider)


def _rel_bin_matrix():
    rows = REL_DELTAS * 2 * REL_BLK
    rho = lax.broadcasted_iota(jnp.int32, (rows, REL_PAD), 0)
    col = lax.broadcasted_iota(jnp.int32, (rows, REL_PAD), 1)
    assert 2 * REL_BLK == 256
    delta = rho >> 8
    c = 255 - (rho & 255)
    dist = K_PAD + REL_BLK * (delta - (K_WINDOW // REL_BLK - 1)) + (c - (REL_BLK - 1))
    idx = jnp.clip(dist, -REL_CLIP, REL_CLIP) + REL_CLIP
    return col == idx


def _rel_expand(rel_pad, name):
    heads = rel_pad.shape[0]
    rows = REL_DELTAS * 2 * REL_BLK

    def body_bin(r_ref, o_ref):
        onehot = jnp.where(_rel_bin_matrix(), 1.0, 0.0).astype(BF16)
        hi, mid, lo = _split3(r_ref[...])
        o_ref[...] = _dot_nt(hi, onehot) + _dot_nt(mid, onehot) + _dot_nt(lo, onehot)

    by_delta = pl.pallas_call(
        body_bin, name=name + "_bin",
        out_shape=jax.ShapeDtypeStruct((heads, rows), F32),
        compiler_params=pltpu.CompilerParams(vmem_limit_bytes=VMEM_LIMIT_V7X),
    )(rel_pad)
    by_delta = by_delta.reshape(heads * REL_DELTAS, 2 * REL_BLK)

    def body_shift(t_ref, o_ref):
        tv = t_ref[...]
        for r in range(REL_BLK):
            o_ref[r] = pltpu.roll(tv, (r + REL_BLK) % (2 * REL_BLK), 1)[:, :REL_BLK]

    return pl.pallas_call(
        body_shift, name=name + "_shift",
        out_shape=jax.ShapeDtypeStruct((REL_BLK, heads * REL_DELTAS, REL_BLK), F32),
        compiler_params=pltpu.CompilerParams(vmem_limit_bytes=VMEM_LIMIT_V7X),
    )(by_delta)


def _bias_table(rel_bias, name):
    heads = rel_bias.shape[0]
    rel_pad = jnp.pad(rel_bias, ((0, 0), (0, REL_PAD - REL_TABLE)))
    tiles = _rel_expand(rel_pad, name)
    tiles = tiles.reshape(REL_BLK, heads, REL_DELTAS, REL_BLK).transpose(1, 2, 0, 3)
    na, nb = Q_BLOCK // REL_BLK, K_WINDOW // REL_BLK
    rows = [jnp.concatenate([tiles[:, a - b + nb - 1] for b in range(nb)], axis=-1) for a in range(na)]
    table = jnp.concatenate(rows, axis=-2)
    qc = np.arange(Q_BLOCK)[:, None] // CHUNK
    kc = np.arange(K_WINDOW)[None, :] // CHUNK
    band = (kc >= qc) & (kc <= qc + PAST_CHUNKS)
    return jnp.where(jnp.asarray(band)[None], table, NEG)


def _rel_reduce(db, name):
    heads = db.shape[0]
    na, nb = Q_BLOCK // REL_BLK, K_WINDOW // REL_BLK

    def body_fold(db_ref, g_ref):
        for delta in range(REL_DELTAS):
            acc = None
            for a in range(na):
                b = a - (delta - (nb - 1))
                if 0 <= b < nb:
                    tile = db_ref[0, a * REL_BLK:(a + 1) * REL_BLK, b * REL_BLK:(b + 1) * REL_BLK]
                    acc = tile if acc is None else acc + tile
            g_ref[0, delta] = acc

    folded = pl.pallas_call(
        body_fold, name=name + "_fold", grid=(heads,),
        in_specs=[pl.BlockSpec((1, Q_BLOCK, K_WINDOW), lambda h: (h, 0, 0))],
        out_specs=pl.BlockSpec((1, REL_DELTAS, REL_BLK, REL_BLK), lambda h: (h, 0, 0, 0)),
        out_shape=jax.ShapeDtypeStruct((heads, REL_DELTAS, REL_BLK, REL_BLK), F32),
        compiler_params=_params("parallel"),
    )(db)
    by_row = folded.transpose(2, 0, 1, 3).reshape(REL_BLK, heads * REL_DELTAS, REL_BLK)

    def body_diag(g_ref, d_ref):
        zeros = jnp.zeros((heads * REL_DELTAS, REL_BLK), F32)
        acc = None
        for r in range(REL_BLK):
            part = pltpu.roll(jnp.concatenate([g_ref[r], zeros], axis=1), REL_BLK - r, 1)
            acc = part if acc is None else acc + part
        d_ref[...] = acc

    diag = pl.pallas_call(
        body_diag, name=name + "_diag",
        out_shape=jax.ShapeDtypeStruct((heads * REL_DELTAS, 2 * REL_BLK), F32),
        compiler_params=pltpu.CompilerParams(vmem_limit_bytes=VMEM_LIMIT_V7X),
    )(by_row)
    diag = diag.reshape(heads, REL_DELTAS * 2 * REL_BLK)

    def body_bin(d_ref, o_ref):
        onehot = jnp.where(_rel_bin_matrix(), 1.0, 0.0).astype(BF16)
        hi, mid, lo = _split3(d_ref[...])
        o_ref[...] = _dot(hi, onehot) + _dot(mid, onehot) + _dot(lo, onehot)

    out = pl.pallas_call(
        body_bin, name=name + "_bin",
        out_shape=jax.ShapeDtypeStruct((heads, REL_PAD), F32),
        compiler_params=pltpu.CompilerParams(vmem_limit_bytes=VMEM_LIMIT_V7X),
    )(diag)
    return out[:, :REL_TABLE]


def _sum_leading(x, name):
    n, r, c = x.shape
    tr = _pick(r, 256, 8)

    def body(x_ref, o_ref):
        acc = x_ref[0].astype(F32)
        for k in range(1, n):
            acc = acc + x_ref[k].astype(F32)
        o_ref[...] = acc

    return pl.pallas_call(
        body, name=name, grid=(r // tr,),
        in_specs=[pl.BlockSpec((n, tr, c), lambda i: (0, i, 0))],
        out_specs=pl.BlockSpec((tr, c), lambda i: (i, 0)),
        out_shape=jax.ShapeDtypeStruct((r, c), F32),
        compiler_params=_params("parallel"),
    )(x)


def _pair_add(g, recv, parity, name):
    _, r, c = g.shape
    tr = _pick(r, 256, 16)

    def body(par_ref, g_ref, r_ref, o_ref):
        o_ref[...] = (g_ref[...].astype(F32) + r_ref[...].astype(F32)).astype(BF16)

    return pl.pallas_call(
        body, name=name,
        grid_spec=pltpu.PrefetchScalarGridSpec(
            num_scalar_prefetch=1, grid=(4, r // tr),
            in_specs=[pl.BlockSpec((1, tr, c), lambda k, i, par: (2 * k + par[0], i, 0)),
                      pl.BlockSpec((1, tr, c), lambda k, i, par: (k, i, 0))],
            out_specs=pl.BlockSpec((1, tr, c), lambda k, i, par: (k, i, 0))),
        out_shape=jax.ShapeDtypeStruct((4, r, c), BF16),
        compiler_params=_params("parallel", "parallel"),
    )(parity, g, recv)


def _adamw(w, g_parts, m, v, name):
    r, c = w.shape
    n = g_parts.shape[0]
    tr = _pick(r, 256, 16 if g_parts.dtype == BF16 else 8)
    c1 = 1.0 - ADAM_B1 ** ADAM_STEP
    c2 = 1.0 - ADAM_B2 ** ADAM_STEP

    def body(w_ref, g_ref, m_ref, v_ref, go_ref, d_ref, nm_ref, nv_ref):
        gv = g_ref[0].astype(F32)
        for k in range(1, n):
            gv = gv + g_ref[k].astype(F32)
        nm = ADAM_B1 * m_ref[...] + (1.0 - ADAM_B1) * gv
        nv = ADAM_B2 * v_ref[...] + (1.0 - ADAM_B2) * (gv * gv)
        go_ref[...] = gv
        d_ref[...] = -ADAM_LR * ((nm / c1) / (jnp.sqrt(nv / c2) + ADAM_EPS) + ADAM_WD * w_ref[...])
        nm_ref[...] = nm
        nv_ref[...] = nv

    spec = pl.BlockSpec((tr, c), lambda i: (i, 0))
    shp = jax.ShapeDtypeStruct((r, c), F32)
    return pl.pallas_call(
        body, name=name, grid=(r // tr,),
        in_specs=[spec, pl.BlockSpec((n, tr, c), lambda i: (0, i, 0)), spec, spec],
        out_specs=[spec] * 4, out_shape=[shp] * 4,
        compiler_params=_params("parallel"),
    )(w, g_parts, m, v)


BIG = (("a_w_in", 1), ("a_w_o", 0), ("a_w_gu", 0), ("a_w_down", 0), ("w_kv", 1),
       ("b_w_q", 0), ("b_w_o", 0), ("b_w_gu", 0), ("b_w_down", 0))
TRANSPOSED = ("a_w_gu", "b_w_gu")
FFN_BLK = 2 * FFN_HIDDEN // N_DEV

SMALL = (("a_norm_g", D_MODEL, True), ("a_gn_g", RET_V_COLS, True), ("a_ffn_norm_g", D_MODEL, True),
         ("kv_norm_g", D_MODEL, False), ("b_norm_g", D_MODEL, False), ("b_ffn_norm_g", D_MODEL, False),
         ("k_norm_g", ATT_DH, False), ("b_q_norm_g", ATT_DH, False),
         ("b_rel_bias", ATT_HEADS * REL_TABLE, False))
SMALL_ROWS, SMALL_COLS = 16, 1024


def _pack_small(vals):
    flat = jnp.concatenate([vals[n].reshape(-1) for n, _, _ in SMALL])
    return jnp.pad(flat, (0, SMALL_ROWS * SMALL_COLS - flat.shape[0])).reshape(SMALL_ROWS, SMALL_COLS)


def _unpack_small(packed, local):
    flat, out, pos = packed.reshape(-1), {}, 0
    for n, length, sharded in SMALL:
        ln = length // N_DEV if (local and sharded) else length
        out[n] = flat[pos:pos + ln]
        pos += ln
    return out


def _gather_rider(shards, names):
    return _GatherRider([shards[n] for n in names])


def _gathered(rider, names, axis_of):
    return {n: (r.reshape(-1, r.shape[2]) if axis_of[n] == 0 else r) for n, r in zip(names, rider.results)}


def _blocks(g):
    return g if g.ndim == 3 else g.reshape(N_DEV, -1, g.shape[-1])


def _local_step(x, target, shards, w_in, s, parity):
    t = x.shape[0]
    axis_of = dict(BIG)
    consts = _ret_consts(t)
    lane_to_head = np.zeros((D_MODEL, LANES), np.float32)
    lane_to_head[np.arange(D_MODEL), np.arange(D_MODEL) // ATT_DH] = 1.0
    bd = jnp.asarray(lane_to_head).astype(BF16)
    kg_t = jnp.tile(s["k_norm_g"], (1, ATT_HEADS))
    qg_t = jnp.tile(s["b_q_norm_g"], (1, ATT_HEADS))
    q_scale = ATT_DH ** -0.5
    w = {"a_w_in": w_in}
    g, recv = {}, {}

    def gather_on(names):
        return _gather_rider(shards, names), names

    def landed(ride):
        w.update(_gathered(ride[0], ride[1], axis_of))

    def scatter_on(names):
        return _ScatterRider([_blocks(g[n]) for n in names]), names

    def reduced(ride):
        recv.update(zip(ride[1], ride[0].results))

    ride = gather_on(["a_w_o", "a_w_down"])
    proj = _mm(x, w["a_w_in"], "nn", "a_proj", norm_g=s["a_norm_g"], rider=ride[0])
    landed(ride)
    ride = gather_on(["a_w_gu", "w_kv"])
    y, o_ret, states = _ret_fwd(proj, s["a_gn_g"], consts, "a_ret", rider=ride[0])
    landed(ride)
    x1 = _mm(y, w["a_w_o"], "nn", "a_out", res=x)
    ride = gather_on(["b_w_q", "b_w_o"])
    gu_a, act_a = _mm(x1, w["a_w_gu"], "nt", "a_ffn_gu", epilogue="swiglu", out_block=FFN_BLK,
                      norm_g=s["a_ffn_norm_g"], rider=ride[0])
    landed(ride)
    x2 = _mm(act_a, w["a_w_down"], "nn", "a_ffn_down", res=x1)

    kv = _mm(x2, w["w_kv"], "nn", "kv_proj", norm_g=s["kv_norm_g"])
    kp, vp = _kv_prep(kv, kg_t, bd, "kv_prep")

    q_raw = _mm(x2, w["b_w_q"], "nn", "b_q", norm_g=s["b_norm_g"])
    qn = _q_hnorm(q_raw, qg_t, bd, q_scale, "q_hnorm")
    bias = _bias_table(s["b_rel_bias"].reshape(ATT_HEADS, REL_TABLE), "rel")
    ride = gather_on(["b_w_gu", "b_w_down"])
    o_att, lse = _att_fwd(qn, kp, vp, bias, "b_att", rider=ride[0])
    landed(ride)
    x3 = _mm(o_att, w["b_w_o"], "nn", "b_out", res=x2)
    gu_b, act_b = _mm(x3, w["b_w_gu"], "nt", "b_ffn_gu", epilogue="swiglu", out_block=FFN_BLK,
                      norm_g=s["b_ffn_norm_g"])
    x4 = _mm(act_b, w["b_w_down"], "nn", "b_ffn_down", res=x3)

    dy, loss = _loss_head(x4, target, "loss")
    in_blk, kv_blk, ffn_blk = w["a_w_in"].shape[2], w["w_kv"].shape[2], FFN_BLK

    dgu = _mm(dy, w["b_w_down"], "nt", "b_ffn_dgu", out_block=ffn_blk, epilogue="swiglu_bwd", extra=gu_b)
    dgu = dgu.reshape(N_DEV, t, ffn_blk)
    g["b_w_down"] = _mm(act_b, dy, "tn", "b_ffn_gdown", out_dtype=BF16)
    ride = scatter_on(["b_w_down"])
    dx3, g["b_ffn_norm_g"] = _mm(dgu, w["b_w_gu"], "nn", "b_ffn_dh", epilogue="rms_bwd",
                                 extra=(x3, s["b_ffn_norm_g"], dy), rider=ride[0])
    reduced(ride)
    g["b_w_gu"] = _mm(dgu, x3, "tn", "b_ffn_ggu", out_dtype=BF16, norm_g=s["b_ffn_norm_g"], norm_b=True)

    do_att = _mm(dx3, w["b_w_o"], "nt", "b_dout", out_dtype=BF16)
    g["b_w_o"] = _mm(o_att, dx3, "tn", "b_gout", out_dtype=BF16)
    ride = scatter_on(["b_w_gu", "b_w_o"])
    dq, dkp, dvp, db = _att_bwd(qn, kp, vp, bias, do_att, o_att, lse, "b_datt", rider=ride[0])
    reduced(ride)
    g["b_rel_bias"] = _rel_reduce(db, "drel").reshape(1, -1)
    dq_raw, gq = _q_dhnorm(q_raw, qg_t, bd, dq, q_scale, "q_dhnorm")
    g["b_q_norm_g"] = gq.reshape(ATT_HEADS, ATT_DH).sum(axis=0, keepdims=True)
    g["b_w_q"] = _mm(x2, dq_raw, "tn", "b_gq", out_dtype=BF16, norm_g=s["b_norm_g"])
    dx2, g["b_norm_g"] = _mm(dq_raw, w["b_w_q"], "nt", "b_dq", epilogue="rms_bwd",
                             extra=(x2, s["b_norm_g"], dx3))

    dkv, gk = _kv_dprep(kv, kg_t, bd, dkp, dvp, "kv_dprep")
    g["k_norm_g"] = gk.reshape(ATT_HEADS, ATT_DH).sum(axis=0, keepdims=True)
    g["w_kv"] = _mm(x2, dkv, "tn", "kv_g", out_dtype=BF16, out_block=kv_blk, norm_g=s["kv_norm_g"])
    dx2, g["kv_norm_g"] = _mm(dkv, w["w_kv"], "nt", "kv_du", epilogue="rms_bwd",
                              extra=(x2, s["kv_norm_g"], dx2))

    ride = scatter_on(["b_w_q"])
    dgu = _mm(dx2, w["a_w_down"], "nt", "a_ffn_dgu", out_block=ffn_blk, epilogue="swiglu_bwd", extra=gu_a,
              rider=ride[0])
    reduced(ride)
    dgu = dgu.reshape(N_DEV, t, ffn_blk)
    g["a_w_down"] = _mm(act_a, dx2, "tn", "a_ffn_gdown", out_dtype=BF16)
    ride = scatter_on(["a_w_down"])
    dx1, g["a_ffn_norm_g"] = _mm(dgu, w["a_w_gu"], "nn", "a_ffn_dh", epilogue="rms_bwd",
                                 extra=(x1, s["a_ffn_norm_g"], dx2), rider=ride[0])
    reduced(ride)
    ride = scatter_on(["w_kv"])
    g["a_w_gu"] = _mm(dgu, x1, "tn", "a_ffn_ggu", out_dtype=BF16, norm_g=s["a_ffn_norm_g"], norm_b=True,
                      rider=ride[0])
    reduced(ride)

    dy_ret = _mm(dx1, w["a_w_o"], "nt", "a_dout")
    g["a_w_o"] = _mm(y, dx1, "tn", "a_gout", out_dtype=BF16)
    ride = scatter_on(["a_w_gu"])
    dproj, g["a_gn_g"] = _ret_bwd(proj, s["a_gn_g"], o_ret, states, dy_ret, consts, "a_dret", rider=ride[0])
    reduced(ride)
    ride = scatter_on(["a_w_o"])
    g["a_w_in"] = _mm(x, dproj, "tn", "a_gin", out_dtype=BF16, out_block=in_blk, norm_g=s["a_norm_g"],
                      rider=ride[0])
    reduced(ride)
    from_sibling = _exchange(_SiblingSwapRider([g["a_w_in"]]), "rs_sibling")[0]
    chip_sums = _pair_add(g["a_w_in"], from_sibling, parity, "rs_pair_add")
    last = _ChipScatterRider([chip_sums])
    grad_x, g["a_norm_g"] = _mm(dproj, w["a_w_in"], "nt", "a_dproj", epilogue="rms_bwd",
                                extra=(x, s["a_norm_g"], dx1), rider=last)
    recv["a_w_in"] = last.results[0]
    return loss, grad_x, recv, g


ARG_NAMES = ("x", "a_norm_g", "a_w_in", "a_gn_g", "a_w_o", "a_ffn_norm_g", "a_w_gu", "a_w_down",
             "kv_norm_g", "w_kv", "k_norm_g", "b_norm_g", "b_w_q", "b_q_norm_g", "b_rel_bias", "b_w_o",
             "b_ffn_norm_g", "b_w_gu", "b_w_down")
WEIGHT_NAMES = ARG_NAMES[1:]


def _big_shard(a, name):
    a = a[0] if a.ndim == 3 else a
    return a.T if name in TRANSPOSED else a


def _as_given(a, name, shape):
    return (a.T if name in TRANSPOSED else a).reshape(shape)


def kernel(x, a_norm_g, a_w_in, a_gn_g, a_w_o, a_ffn_norm_g, a_w_gu, a_w_down, kv_norm_g, w_kv, k_norm_g, b_norm_g, b_w_q, b_q_norm_g, b_rel_bias, b_w_o, b_ffn_norm_g, b_w_gu, b_w_down, loss_target, m_a_norm_g, m_a_w_in, m_a_gn_g, m_a_w_o, m_a_ffn_norm_g, m_a_w_gu, m_a_w_down, m_kv_norm_g, m_w_kv, m_k_norm_g, m_b_norm_g, m_b_w_q, m_b_q_norm_g, m_b_rel_bias, m_b_w_o, m_b_ffn_norm_g, m_b_w_gu, m_b_w_down, v_a_norm_g, v_a_w_in, v_a_gn_g, v_a_w_o, v_a_ffn_norm_g, v_a_w_gu, v_a_w_down, v_kv_norm_g, v_w_kv, v_k_norm_g, v_b_norm_g, v_b_w_q, v_b_q_norm_g, v_b_rel_bias, v_b_w_o, v_b_ffn_norm_g, v_b_w_gu, v_b_w_down):
    args = (x, a_norm_g, a_w_in, a_gn_g, a_w_o, a_ffn_norm_g, a_w_gu, a_w_down, kv_norm_g, w_kv, k_norm_g,
            b_norm_g, b_w_q, b_q_norm_g, b_rel_bias, b_w_o, b_ffn_norm_g, b_w_gu, b_w_down)
    p = dict(zip(ARG_NAMES, args))
    m_all = dict(zip(WEIGHT_NAMES, (m_a_norm_g, m_a_w_in, m_a_gn_g, m_a_w_o, m_a_ffn_norm_g, m_a_w_gu,
                                    m_a_w_down, m_kv_norm_g, m_w_kv, m_k_norm_g, m_b_norm_g, m_b_w_q,
                                    m_b_q_norm_g, m_b_rel_bias, m_b_w_o, m_b_ffn_norm_g, m_b_w_gu, m_b_w_down)))
    v_all = dict(zip(WEIGHT_NAMES, (v_a_norm_g, v_a_w_in, v_a_gn_g, v_a_w_o, v_a_ffn_norm_g, v_a_w_gu,
                                    v_a_w_down, v_kv_norm_g, v_w_kv, v_k_norm_g, v_b_norm_g, v_b_w_q,
                                    v_b_q_norm_g, v_b_rel_bias, v_b_w_o, v_b_ffn_norm_g, v_b_w_gu, v_b_w_down)))
    xi, yi, ci = _my_place()
    me = 4 * xi + 2 * yi + ci
    big_names = [n for n, _ in BIG]
    axis_of = dict(BIG)

    big_local = {n: _big_shard(p[n], n) for n in big_names}
    shards = {n: a.astype(BF16) for n, a in big_local.items()}
    small_local = _pack_small({n: p[n] for n, _, _ in SMALL})
    w_in, small_all = _exchange(_GatherRider([shards["a_w_in"], small_local]), "gather_in")
    flat_g = small_all.reshape(N_DEV, -1)
    s_full, pos = {}, 0
    for n, length, sharded in SMALL:
        ln = length // N_DEV if sharded else length
        s_full[n] = flat_g[:, pos:pos + ln].reshape(1, -1) if sharded else p[n].reshape(1, -1)
        pos += ln

    parity = jnp.reshape(ci, (1,)).astype(jnp.int32)
    loss, grad_x, recv, g = _local_step(x[0], loss_target[0], shards, w_in, s_full, parity)
    loss = lax.psum(loss[0, 0], ("x", "y", "c"))

    g_small_all = _exchange(_GatherRider([_pack_small({n: g[n] for n, _, _ in SMALL})]), "gather_gsmall")[0]
    g_small = _unpack_small(_sum_leading(g_small_all, "gsmall_sum"), local=False)
    for n, length, sharded in SMALL:
        if sharded:
            g_small[n] = lax.dynamic_slice(g_small[n], (me * (length // N_DEV),), (length // N_DEV,))

    grads, deltas, new_m, new_v = {}, {}, {}, {}
    for n in big_names:
        outs = _adamw(big_local[n], recv[n], _big_shard(m_all[n], n), _big_shard(v_all[n], n), "adamw_" + n)
        grads[n], deltas[n], new_m[n], new_v[n] = (_as_given(a, n, p[n].shape) for a in outs)
    pk = lambda src: _pack_small({n: src[n] for n, _, _ in SMALL})
    outs = _adamw(small_local, pk(g_small)[None], pk(m_all), pk(v_all), "adamw_small")
    g_s, d_s, nm_s, nv_s = (_unpack_small(a, local=True) for a in outs)
    for n, _, _ in SMALL:
        grads[n], deltas[n], new_m[n], new_v[n] = (a[n].reshape(p[n].shape) for a in (g_s, d_s, nm_s, nv_s))

    return (loss, grad_x[None], *[grads[n] for n in WEIGHT_NAMES], *[deltas[n] for n in WEIGHT_NAMES],
            *[new_m[n] for n in WEIGHT_NAMES], *[new_v[n] for n in WEIGHT_NAMES])
```

```python
import numpy as np
import jax
import jax.numpy as jnp
from jax import lax
from jax.experimental import pallas as pl
from jax.experimental.pallas import tpu as pltpu

F32 = jnp.float32
BF16 = jnp.bfloat16

N_DEV = 8
D_MODEL = 1024
CHUNK = 64
EPS = 1e-6
RET_HEADS, RET_DK, RET_DV = 4, 256, 512
RET_STEP = 2
RET_Q_COLS = RET_HEADS * RET_DK
RET_V_COLS = RET_HEADS * RET_DV
ATT_HEADS, ATT_DH = 16, 64
PAST_CHUNKS = 8
REL_CLIP = 256
REL_TABLE = 2 * REL_CLIP + 1
FFN_HIDDEN = 2816
ROPE_BASE = 10000.0
LANES = 128
Q_BLOCK = 256
ATT_ROWS = 32
K_PAD = PAST_CHUNKS * CHUNK
K_WINDOW = Q_BLOCK + K_PAD
REL_BLK = 128
REL_DELTAS = Q_BLOCK // REL_BLK + K_WINDOW // REL_BLK - 1
REL_PAD = 640
NEG = -1e30
VMEM_LIMIT_V7X = 56 * 1024 * 1024
ADAM_LR, ADAM_B1, ADAM_B2, ADAM_EPS, ADAM_WD, ADAM_STEP = 1e-3, 0.9, 0.999, 1e-8, 0.01, 10
MESH = pl.DeviceIdType.MESH
ANY = pl.BlockSpec(memory_space=pl.ANY)


def _params(*semantics):
    return pltpu.CompilerParams(dimension_semantics=semantics, vmem_limit_bytes=VMEM_LIMIT_V7X)


def _pick(dim, cap, align):
    best = None
    for t in range(align, min(dim, cap) + 1, align):
        if dim % t == 0:
            best = t
    assert best is not None, (dim, cap, align)
    return best


def _dot(a, b):
    return lax.dot_general(a, b, (((1,), (0,)), ((), ())), preferred_element_type=F32)


def _dot_nt(a, b):
    return lax.dot_general(a, b, (((1,), (1,)), ((), ())), preferred_element_type=F32)


def _dot_tn(a, b):
    return lax.dot_general(a, b, (((0,), (0,)), ((), ())), preferred_element_type=F32)


def _split2(x):
    hi = x.astype(BF16)
    lo = (x - hi.astype(F32)).astype(BF16)
    return hi, lo


def _split3(x):
    hi = x.astype(BF16)
    r = x - hi.astype(F32)
    mid = r.astype(BF16)
    lo = (r - mid.astype(F32)).astype(BF16)
    return hi, mid, lo


def _sigmoid(x):
    return 1.0 / (1.0 + jnp.exp(-x))


def _accumulate(ref, part, step):
    @pl.when(step == 0)
    def _():
        ref[...] = part

    @pl.when(step > 0)
    def _():
        ref[...] += part


RELAY_AT_NUM, RELAY_AT_DEN = 3, 4


def _my_place():
    return lax.axis_index("x"), lax.axis_index("y"), lax.axis_index("c")


def _flip(v, bit):
    return 1 - v if bit else v


class _NoRelay:
    def relay(self, in_refs, out_refs, sems):
        pass


class _GatherRider:
    def __init__(self, xs):
        self.inputs = list(xs)
        n = len(xs)
        self.out_shape = [jax.ShapeDtypeStruct((N_DEV,) + x.shape, x.dtype) for x in xs]
        self.scratch = [pltpu.SemaphoreType.DMA((7, n)), pltpu.SemaphoreType.DMA((7, n)),
                        pltpu.SemaphoreType.DMA((n,))]
        self.results = None

    def _copies(self, x_refs, out_refs, sems):
        send_sems, recv_sems, local_sems = sems
        n = len(x_refs)
        x, y, c = _my_place()
        me, sibling = (x, y, c), (x, y, 1 - c)
        chips = [(1 - x, y), (x, 1 - y), (1 - x, 1 - y)]

        def slot(a, px, py, pc):
            return out_refs[a].at[4 * px + 2 * py + pc]

        def copy(k, a, block, to, own=False):
            return pltpu.make_async_remote_copy(
                src_ref=x_refs[a] if own else slot(a, *block), dst_ref=slot(a, *block),
                send_sem=send_sems.at[k, a], recv_sem=recv_sems.at[k, a],
                device_id=to, device_id_type=MESH)

        mine = [pltpu.make_async_copy(x_refs[a], slot(a, *me), local_sems.at[a]) for a in range(n)]
        first = []
        for a in range(n):
            first.append(copy(0, a, me, sibling, own=True))
            first += [copy(1 + j, a, me, (*chip, c), own=True) for j, chip in enumerate(chips)]
        return n, c, me, sibling, chips, copy, mine, first

    def start(self, x_refs, out_refs, sems):
        _, _, _, _, _, _, mine, first = self._copies(x_refs, out_refs, sems)
        for cp in mine + first:
            cp.start()

    def relay(self, x_refs, out_refs, sems):
        n, c, me, sibling, chips, copy, _, _ = self._copies(x_refs, out_refs, sems)
        for j, chip in enumerate(chips):
            for a in range(n):
                copy(1 + j, a, (*chip, c), me).wait_recv()
                copy(4 + j, a, (*chip, c), sibling).start()

    def finish(self, x_refs, out_refs, sems):
        n, c, me, sibling, chips, copy, mine, first = self._copies(x_refs, out_refs, sems)
        passed = [copy(4 + j, a, (*chip, c), sibling) for j, chip in enumerate(chips) for a in range(n)]
        for a in range(n):
            copy(0, a, sibling, me).wait_recv()
            for j, chip in enumerate(chips):
                copy(4 + j, a, (*chip, 1 - c), me).wait_recv()
        for cp in first + passed:
            cp.wait_send()
        for cp in mine:
            cp.wait()


class _ScatterRider(_NoRelay):
    def __init__(self, gs):
        self.inputs = list(gs)
        n = len(gs)
        self.out_shape = [jax.ShapeDtypeStruct(g.shape, g.dtype) for g in gs]
        self.scratch = [pltpu.SemaphoreType.DMA((7, n)), pltpu.SemaphoreType.DMA((7, n)),
                        pltpu.SemaphoreType.DMA((n,))]
        self.results = None

    def _copies(self, g_refs, out_refs, sems):
        send_sems, recv_sems, local_sems = sems
        x, y, c = _my_place()
        me = 4 * x + 2 * y + c
        mine, copies = [], []
        for a in range(len(g_refs)):
            mine.append(pltpu.make_async_copy(g_refs[a].at[me], out_refs[a].at[me], local_sems.at[a]))
            for k in range(1, N_DEV):
                px, py, pc = _flip(x, k & 4), _flip(y, k & 2), _flip(c, k & 1)
                copies.append(pltpu.make_async_remote_copy(
                    src_ref=g_refs[a].at[4 * px + 2 * py + pc], dst_ref=out_refs[a].at[me],
                    send_sem=send_sems.at[k - 1, a], recv_sem=recv_sems.at[k - 1, a],
                    device_id=(px, py, pc), device_id_type=MESH))
        return mine, copies

    def start(self, g_refs, out_refs, sems):
        mine, copies = self._copies(g_refs, out_refs, sems)
        for cp in mine + copies:
            cp.start()

    def finish(self, g_refs, out_refs, sems):
        mine, copies = self._copies(g_refs, out_refs, sems)
        for cp in copies + mine:
            cp.wait()


class _SiblingSwapRider(_NoRelay):
    def __init__(self, gs):
        self.inputs = list(gs)
        n = len(gs)
        self.out_shape = [jax.ShapeDtypeStruct((4,) + g.shape[1:], g.dtype) for g in gs]
        self.scratch = [pltpu.SemaphoreType.DMA((4, n)), pltpu.SemaphoreType.DMA((4, n))]
        self.results = None

    def _copies(self, g_refs, out_refs, sems):
        send_sems, recv_sems = sems
        x, y, c = _my_place()
        return [pltpu.make_async_remote_copy(
            src_ref=g_refs[a].at[2 * k + 1 - c], dst_ref=out_refs[a].at[k],
            send_sem=send_sems.at[k, a], recv_sem=recv_sems.at[k, a],
            device_id=(x, y, 1 - c), device_id_type=MESH)
            for a in range(len(g_refs)) for k in range(4)]

    def start(self, g_refs, out_refs, sems):
        for cp in self._copies(g_refs, out_refs, sems):
            cp.start()

    def finish(self, g_refs, out_refs, sems):
        for cp in self._copies(g_refs, out_refs, sems):
            cp.wait()


class _ChipScatterRider(_NoRelay):
    def __init__(self, ps):
        self.inputs = list(ps)
        n = len(ps)
        self.out_shape = [jax.ShapeDtypeStruct(p.shape, p.dtype) for p in ps]
        self.scratch = [pltpu.SemaphoreType.DMA((3, n)), pltpu.SemaphoreType.DMA((3, n)),
                        pltpu.SemaphoreType.DMA((n,))]
        self.results = None

    def _copies(self, p_refs, out_refs, sems):
        send_sems, recv_sems, local_sems = sems
        x, y, c = _my_place()
        my_chip = 2 * x + y
        chips = [(1 - x, y), (x, 1 - y), (1 - x, 1 - y)]
        n = len(p_refs)
        mine = [pltpu.make_async_copy(p_refs[a].at[my_chip], out_refs[a].at[my_chip], local_sems.at[a])
                for a in range(n)]
        copies = [pltpu.make_async_remote_copy(
            src_ref=p_refs[a].at[2 * cx + cy], dst_ref=out_refs[a].at[my_chip],
            send_sem=send_sems.at[j, a], recv_sem=recv_sems.at[j, a],
            device_id=(cx, cy, c), device_id_type=MESH)
            for a in range(n) for j, (cx, cy) in enumerate(chips)]
        return mine, copies

    def start(self, p_refs, out_refs, sems):
        mine, copies = self._copies(p_refs, out_refs, sems)
        for cp in mine + copies:
            cp.start()

    def finish(self, p_refs, out_refs, sems):
        mine, copies = self._copies(p_refs, out_refs, sems)
        for cp in copies + mine:
            cp.wait()


def _call(body, name, grid, in_specs, out_specs, out_shape, scratch, semantics, args, rider=None):
    in_specs, out_specs, out_shape, scratch = list(in_specs), list(out_specs), list(out_shape), list(scratch)
    if rider is None:
        return list(pl.pallas_call(
            body, name=name, grid=grid, in_specs=in_specs, out_specs=out_specs, out_shape=out_shape,
            scratch_shapes=scratch, compiler_params=_params(*semantics))(*args))
    n_in, n_out, n_scr = len(in_specs), len(out_specs), len(scratch)
    r_in, r_out = len(rider.inputs), len(rider.out_shape)

    def wrapped(*refs):
        cuts = np.cumsum([0, n_in, r_in, n_out, r_out, n_scr])
        hi, ri, ho, ro, hs = (refs[cuts[i]:cuts[i + 1]] for i in range(5))
        rs = refs[cuts[5]:]
        step, steps = pl.program_id(0), grid[0]
        for d in range(1, len(grid)):
            step, steps = step * grid[d] + pl.program_id(d), steps * grid[d]

        @pl.when(step == 0)
        def _():
            rider.start(ri, ro, rs)

        body(*hi, *ho, *hs)

        @pl.when(step == (steps * RELAY_AT_NUM) // RELAY_AT_DEN)
        def _():
            rider.relay(ri, ro, rs)

        @pl.when(step == steps - 1)
        def _():
            rider.finish(ri, ro, rs)

    outs = pl.pallas_call(
        wrapped, name=name, grid=grid,
        in_specs=in_specs + [ANY] * r_in, out_specs=out_specs + [ANY] * r_out,
        out_shape=out_shape + rider.out_shape, scratch_shapes=scratch + rider.scratch,
        compiler_params=_params(*(["arbitrary"] * len(grid))),
    )(*args, *rider.inputs)
    rider.results = list(outs[n_out:])
    return list(outs[:n_out])


def _exchange(rider, name):
    r_in, r_out = len(rider.inputs), len(rider.out_shape)

    def body(*refs):
        ri, ro, rs = refs[:r_in], refs[r_in:r_in + r_out], refs[r_in + r_out:]
        rider.start(ri, ro, rs)
        rider.relay(ri, ro, rs)
        rider.finish(ri, ro, rs)

    return list(pl.pallas_call(
        body, name=name, in_specs=[ANY] * r_in, out_specs=[ANY] * r_out,
        out_shape=rider.out_shape, scratch_shapes=rider.scratch)(*rider.inputs))


MM_CAP_MN = 1024
MM_CAP_N = 1536
MM_CAP_K = 3072
MM_CAP_K_TOKENS = 2048
MM_CAP_K_RMS = 1536
NORM_ROWS = 256


def _mm(a, b, mode, name, out_dtype=F32, res=None, out_block=None, epilogue=None, extra=None, norm_g=None,
        norm_b=False, rider=None):
    a3, b3 = a.ndim == 3, b.ndim == 3
    um = un = uk = None
    if mode in ("nn", "nt"):
        if a3:
            m, uk = a.shape[1:]
            k = a.shape[0] * uk
        else:
            m, k = a.shape
    else:
        if a3:
            k, um = a.shape[1:]
            m = a.shape[0] * um
        else:
            k, m = a.shape
    if mode in ("nn", "tn"):
        if b3:
            kb, un = b.shape[1:]
            n = b.shape[0] * un
        else:
            kb, n = b.shape
        assert kb == k, (a.shape, b.shape, mode)
    else:
        if b3:
            n, ukb = b.shape[1:]
            assert b.shape[0] * ukb == k and uk in (None, ukb), (a.shape, b.shape, mode)
            uk = ukb
        else:
            n, kb = b.shape
            assert kb == k, (a.shape, b.shape, mode)
    if out_block is not None:
        assert un in (None, out_block)
        un = out_block

    def tile(dim, unit, cap, align):
        if unit is None:
            return _pick(dim, cap, align), 1
        c = max(1, cap // unit)
        while (dim // unit) % c:
            c -= 1
        return unit, c

    um, cm = tile(m, um, MM_CAP_MN if mode != "tn" else 1408, 128 if mode == "tn" else 16)
    un, cn = tile(n, un, MM_CAP_N, 128)
    cap_k = MM_CAP_K_TOKENS if mode == "tn" else (MM_CAP_K_RMS if epilogue == "rms_bwd" else MM_CAP_K)
    uk, ck = tile(k, uk, cap_k, 128)
    if epilogue == "rms_bwd":
        assert mode != "tn" and n == D_MODEL and cm == cn == 1 and res is None and out_block is None
    if epilogue == "loss":
        assert n == D_MODEL and cm == cn == 1 and res is not None and out_block is None
    if norm_g is not None and norm_b:
        assert mode == "tn" and not b3 and n == D_MODEL and cn == 1
    elif norm_g is not None:
        assert not a3 and (m if mode == "tn" else k) == D_MODEL and (cm if mode == "tn" else ck) == 1
    if epilogue == "swiglu":
        assert res is None and ((mode == "nn" and b3 and out_block is None) or
                                (mode == "nt" and not b3 and out_block is not None))
        cn = 2
    if epilogue == "swiglu_bwd":
        assert mode == "nt" and out_block is not None and extra is not None and res is None
        cn = 1
    tm, tn, tk = cm * um, cn * un, ck * uk
    nk = k // tk
    dot = {"nn": _dot, "nt": _dot_nt, "tn": _dot_tn}[mode]
    half = n // un // 2
    blocked_out = out_block is not None or epilogue in ("swiglu", "swiglu_bwd")
    extras = [] if extra is None else (list(extra) if isinstance(extra, (tuple, list)) else [extra])

    def sl(idx, unit, count):
        return slice(None) if count == 1 else slice(idx * unit, (idx + 1) * unit)

    def body(*refs):
        a_ref, b_ref = refs[0], refs[1]
        pos = 2
        r_ref = ng_ref = None
        if res is not None:
            r_ref, pos = refs[pos], pos + 1
        e_refs, pos = refs[pos:pos + len(extras)], pos + len(extras)
        if norm_g is not None:
            ng_ref, pos = refs[pos], pos + 1
        outs, acc_ref = refs[pos:-1], refs[-1]
        kk = pl.program_id(2)

        def normed(x_ref):
            groups = []
            for r in range(0, x_ref.shape[0], NORM_ROWS):
                xv = x_ref[r:r + NORM_ROWS, :]
                rstd = lax.rsqrt(jnp.mean(xv * xv, axis=-1, keepdims=True) + EPS)
                groups.append((xv * rstd * ng_ref[...]).astype(BF16))
            return jnp.concatenate(groups, axis=0)

        def a_blk(mi, ki):
            if norm_g is not None and not norm_b:
                return normed(a_ref)
            if mode in ("nn", "nt"):
                return a_ref[ki] if a3 else a_ref[:, sl(ki, uk, ck)]
            return a_ref[mi] if a3 else a_ref[:, sl(mi, um, cm)]

        def b_blk(ki, ni):
            if norm_b:
                return normed(b_ref)
            if epilogue == "swiglu":
                return b_ref[ni, 0]
            if mode in ("nn", "tn"):
                return b_ref[ni] if b3 else b_ref[sl(ki, uk, ck), sl(ni, un, cn)]
            return b_ref[ki][sl(ni, un, cn), :] if b3 else b_ref[sl(ni, un, cn), sl(ki, uk, ck)]

        parts = {}
        for mi in range(cm):
            for ni in range(cn):
                part = None
                for ki in range(ck):
                    d = dot(a_blk(mi, ki).astype(BF16), b_blk(ki, ni).astype(BF16))
                    part = d if part is None else part + d
                parts[mi, ni] = part

        def finish(total):
            if epilogue == "swiglu":
                gate, up = total[0, 0], total[0, 1]
                outs[0][0, 0] = gate.astype(BF16)
                outs[0][1, 0] = up.astype(BF16)
                outs[1][0] = (gate * _sigmoid(gate) * up).astype(BF16)
                return
            if epilogue == "swiglu_bwd":
                dact = total[0, 0]
                gate, up = e_refs[0][0, 0].astype(F32), e_refs[0][1, 0].astype(F32)
                sg = _sigmoid(gate)
                outs[0][0, 0] = (dact * up * (sg * (1.0 + gate * (1.0 - sg)))).astype(BF16)
                outs[0][1, 0] = (dact * (gate * sg)).astype(BF16)
                return
            if epilogue == "rms_bwd":
                x_ref, g_ref, dres_ref = e_refs
                dh, dg = total[0, 0], None
                for r in range(0, tm, NORM_ROWS):
                    rows = slice(r, r + NORM_ROWS)
                    xv, dhv = x_ref[rows, :], dh[rows, :]
                    rstd = lax.rsqrt(jnp.mean(xv * xv, axis=-1, keepdims=True) + EPS)
                    xh = xv * rstd
                    dyg = dhv * g_ref[...]
                    c = jnp.mean(dyg * xh, axis=-1, keepdims=True)
                    outs[0][rows, :] = dres_ref[rows, :] + rstd * (dyg - xh * c)
                    part = jnp.sum(dhv * xh, axis=0, keepdims=True)
                    dg = part if dg is None else dg + part
                _accumulate(outs[1], dg, pl.program_id(0))
                return
            if epilogue == "loss":
                diff = r_ref[...] + total[0, 0] - e_refs[0][...]
                outs[0][...] = diff * (1.0 / n)
                sq = jnp.sum(jnp.sum(diff * diff, axis=-1, keepdims=True), axis=0, keepdims=True)
                _accumulate(outs[1], sq * (0.5 / n), pl.program_id(0))
                return
            for (mi, ni), val in total.items():
                rows, cols = sl(mi, um, cm), sl(ni, un, cn)
                if res is not None:
                    val = r_ref[rows, cols] + val
                if blocked_out:
                    outs[0][ni, rows] = val.astype(out_dtype)
                else:
                    outs[0][rows, cols] = val.astype(out_dtype)

        if nk == 1:
            finish(parts)
        else:
            @pl.when(kk == 0)
            def _():
                for (mi, ni), val in parts.items():
                    acc_ref[mi * cn + ni] = val

            @pl.when(jnp.logical_and(kk > 0, kk < nk - 1))
            def _():
                for (mi, ni), val in parts.items():
                    acc_ref[mi * cn + ni] += val

            @pl.when(kk == nk - 1)
            def _():
                finish({key: acc_ref[key[0] * cn + key[1]] + val for key, val in parts.items()})

    if mode in ("nn", "nt"):
        a_spec = (pl.BlockSpec((ck, tm, uk), lambda i, j, kk: (kk, i, 0)) if a3
                  else pl.BlockSpec((tm, tk), lambda i, j, kk: (i, kk)))
    else:
        a_spec = (pl.BlockSpec((cm, tk, um), lambda i, j, kk: (i, kk, 0)) if a3
                  else pl.BlockSpec((tk, tm), lambda i, j, kk: (kk, i)))
    pair_spec = pl.BlockSpec((2, 1, tm, un), lambda i, j, kk: (0, j, i, 0))
    row_spec = pl.BlockSpec((tm, tn), lambda i, j, kk: (i, 0))
    vec_spec = pl.BlockSpec((1, tn), lambda i, j, kk: (0, 0))
    if epilogue == "swiglu" and mode == "nn":
        b = b.reshape(2, half, k, un)
        b_spec = pl.BlockSpec((2, 1, tk, un), lambda i, j, kk: (0, j, kk, 0))
    elif epilogue == "swiglu":
        b = b.reshape(2, half, un, k)
        b_spec = pl.BlockSpec((2, 1, un, tk), lambda i, j, kk: (0, j, 0, kk))
    elif mode in ("nn", "tn"):
        b_spec = (pl.BlockSpec((cn, tk, un), lambda i, j, kk: (j, kk, 0)) if b3
                  else pl.BlockSpec((tk, tn), lambda i, j, kk: (kk, j)))
    else:
        b_spec = (pl.BlockSpec((ck, tn, uk), lambda i, j, kk: (kk, j, 0)) if b3
                  else pl.BlockSpec((tn, tk), lambda i, j, kk: (j, kk)))
    if epilogue == "swiglu":
        out_specs = [pair_spec, pl.BlockSpec((1, tm, un), lambda i, j, kk: (j, i, 0))]
        out_shape = [jax.ShapeDtypeStruct((2, half, m, un), BF16), jax.ShapeDtypeStruct((half, m, un), BF16)]
    elif epilogue == "swiglu_bwd":
        out_specs = [pair_spec]
        out_shape = [jax.ShapeDtypeStruct(extra.shape, BF16)]
    elif epilogue == "rms_bwd":
        out_specs = [row_spec, vec_spec]
        out_shape = [jax.ShapeDtypeStruct((m, n), F32), jax.ShapeDtypeStruct((1, n), F32)]
    elif epilogue == "loss":
        out_specs = [row_spec, pl.BlockSpec((1, 1), lambda i, j, kk: (0, 0))]
        out_shape = [jax.ShapeDtypeStruct((m, n), F32), jax.ShapeDtypeStruct((1, 1), F32)]
    elif blocked_out:
        out_specs = [pl.BlockSpec((cn, tm, un), lambda i, j, kk: (j, i, 0))]
        out_shape = [jax.ShapeDtypeStruct((n // un, m, un), out_dtype)]
    else:
        out_specs = [pl.BlockSpec((tm, tn), lambda i, j, kk: (i, j))]
        out_shape = [jax.ShapeDtypeStruct((m, n), out_dtype)]
    in_specs, args = [a_spec, b_spec], [a, b]
    if res is not None:
        in_specs.append(pl.BlockSpec((tm, tn), lambda i, j, kk: (i, j)))
        args.append(res)
    if epilogue == "swiglu_bwd":
        in_specs.append(pair_spec)
    elif epilogue == "rms_bwd":
        in_specs += [row_spec, vec_spec, row_spec]
    elif epilogue == "loss":
        in_specs.append(row_spec)
    args += extras
    if norm_g is not None:
        in_specs.append(pl.BlockSpec((1, D_MODEL), lambda i, j, kk: (0, 0)))
        args.append(norm_g)
    semantics = ("arbitrary",) * 3 if epilogue in ("rms_bwd", "loss") else ("parallel", "parallel", "arbitrary")
    out = _call(body, name, (m // tm, n // tn, nk), in_specs, out_specs, out_shape,
                [pltpu.VMEM((cm * cn, um, un), F32)], semantics, args, rider)
    return out if epilogue in ("swiglu", "rms_bwd", "loss") else out[0]


def _head_sums(v, ind):
    hi, lo = _split2(v)
    return _dot(hi, ind) + _dot(lo, ind)


def _head_spread(per_head, ind):
    hi, lo = _split2(per_head)
    return _dot_nt(hi, ind) + _dot_nt(lo, ind)


def _head_rstd(xv, ind):
    return _head_spread(lax.rsqrt(_head_sums(xv * xv, ind) * (1.0 / ATT_DH) + EPS), ind)


def _hn_bwd_math(xv, gv, ind, dyv, scale):
    rstd = _head_rstd(xv, ind)
    xh = xv * rstd
    dyn = dyv * scale
    dyg = dyn * gv
    dx = rstd * (dyg - xh * _head_spread(_head_sums(dyg * xh, ind) * (1.0 / ATT_DH), ind))
    return dx, jnp.sum(dyn * xh, axis=0, keepdims=True)


def _q_hnorm(x, g_tiled, bd, scale, name):
    t, d = x.shape
    tm = _pick(t, 512, 16)

    def body(x_ref, g_ref, bd_ref, o_ref):
        xv = x_ref[...]
        o_ref[...] = (xv * _head_rstd(xv, bd_ref[...]) * g_ref[...] * scale).astype(BF16)

    return pl.pallas_call(
        body, name=name, grid=(t // tm,),
        in_specs=[pl.BlockSpec((tm, d), lambda i: (i, 0)), pl.BlockSpec((1, d), lambda i: (0, 0)),
                  pl.BlockSpec((d, LANES), lambda i: (0, 0))],
        out_specs=pl.BlockSpec((tm, d), lambda i: (i, 0)),
        out_shape=jax.ShapeDtypeStruct((t, d), BF16),
        compiler_params=_params("parallel"),
    )(x, g_tiled, bd)


def _q_dhnorm(x, g_tiled, bd, dy, scale, name):
    t, d = x.shape
    tm = _pick(t, 512, 16)

    def body(x_ref, g_ref, bd_ref, dy_ref, dx_ref, dg_ref):
        dx, part = _hn_bwd_math(x_ref[...], g_ref[...], bd_ref[...], dy_ref[...], scale)
        dx_ref[...] = dx.astype(BF16)
        _accumulate(dg_ref, part, pl.program_id(0))

    row = pl.BlockSpec((tm, d), lambda i: (i, 0))
    vec = pl.BlockSpec((1, d), lambda i: (0, 0))
    return pl.pallas_call(
        body, name=name, grid=(t // tm,),
        in_specs=[row, vec, pl.BlockSpec((d, LANES), lambda i: (0, 0)), row],
        out_specs=[row, vec],
        out_shape=[jax.ShapeDtypeStruct((t, d), BF16), jax.ShapeDtypeStruct((1, d), F32)],
        compiler_params=_params("arbitrary"),
    )(x, g_tiled, bd, dy)


def _kv_prep(kv, g_tiled, bd, name):
    t = kv.shape[0]
    d = D_MODEL
    tm = K_PAD
    assert t % tm == 0

    def body(k_ref, v_ref, g_ref, bd_ref, kp_ref, vp_ref):
        i = pl.program_id(0)

        @pl.when(i == 0)
        def _():
            kp_ref[...] = jnp.zeros_like(kp_ref)
            vp_ref[...] = jnp.zeros_like(vp_ref)

        @pl.when(i > 0)
        def _():
            xv = k_ref[...]
            kp_ref[...] = (xv * _head_rstd(xv, bd_ref[...]) * g_ref[...]).astype(BF16)
            vp_ref[...] = v_ref[...].astype(BF16)

    shp = jax.ShapeDtypeStruct((t + K_PAD, d), BF16)
    out = pl.BlockSpec((tm, d), lambda i: (i, 0))
    return pl.pallas_call(
        body, name=name, grid=(t // tm + 1,),
        in_specs=[pl.BlockSpec((tm, d), lambda i: (jnp.maximum(i - 1, 0), 0)),
                  pl.BlockSpec((tm, d), lambda i: (jnp.maximum(i - 1, 0), 1)),
                  pl.BlockSpec((1, d), lambda i: (0, 0)), pl.BlockSpec((d, LANES), lambda i: (0, 0))],
        out_specs=[out, out], out_shape=[shp, shp],
        compiler_params=_params("arbitrary"),
    )(kv, kv, g_tiled, bd)


def _kv_dprep(kv, g_tiled, bd, dkp_t, dvp_t, name):
    t = kv.shape[0]
    d = D_MODEL
    tm = K_PAD

    def body(k_ref, g_ref, bd_ref, dk_ref, dv_ref, o_ref, dg_ref):
        dx, part = _hn_bwd_math(k_ref[...], g_ref[...], bd_ref[...], dk_ref[...].T, 1.0)
        o_ref[:, :d] = dx.astype(BF16)
        o_ref[:, d:] = dv_ref[...].T.astype(BF16)
        _accumulate(dg_ref, part, pl.program_id(0))

    vec = pl.BlockSpec((1, d), lambda i: (0, 0))
    padded = pl.BlockSpec((d, tm), lambda i: (0, i + 1))
    return pl.pallas_call(
        body, name=name, grid=(t // tm,),
        in_specs=[pl.BlockSpec((tm, d), lambda i: (i, 0)), vec, pl.BlockSpec((d, LANES), lambda i: (0, 0)),
                  padded, padded],
        out_specs=[pl.BlockSpec((tm, 2 * d), lambda i: (i, 0)), vec],
        out_shape=[jax.ShapeDtypeStruct((t, 2 * d), BF16), jax.ShapeDtypeStruct((1, d), F32)],
        compiler_params=_params("arbitrary"),
    )(kv, g_tiled, bd, dkp_t, dvp_t)


def _ret_consts(t):
    h = np.arange(RET_HEADS, dtype=np.float32)
    lg = np.log(np.float32(1.0) - np.float32(2.0) ** (np.float32(-5.0) - h)).astype(np.float32)
    tt = np.arange(CHUNK, dtype=np.float32)
    intra = np.exp(lg[:, None, None] * np.abs(tt[:, None] - tt[None, :])).astype(np.float32)
    q_dec = np.exp(lg[:, None] * (tt + 1.0)).astype(np.float32)
    k_dec = np.exp(lg[:, None] * (CHUNK - 1.0 - tt)).astype(np.float32)
    s_dec = [float(v) for v in np.exp(lg * np.float32(CHUNK)).astype(np.float32)]
    qd = np.broadcast_to(q_dec[:, :, None], (RET_HEADS, CHUNK, RET_DK)).copy()
    kd = np.broadcast_to(k_dec[:, :, None], (RET_HEADS, CHUNK, RET_DK)).copy()
    half = RET_DK // 2
    inv_freq = ROPE_BASE ** (-jnp.arange(half, dtype=F32) / half)
    ang = jnp.arange(t).astype(F32)[:, None] * inv_freq[None, :]
    return jnp.asarray(intra), jnp.asarray(qd), jnp.asarray(kd), s_dec, jnp.cos(ang), jnp.sin(ang)


def _rope(x, cos, sin):
    half = RET_DK // 2
    x1, x2 = x[:, :half], x[:, half:]
    return jnp.concatenate([x1 * cos - x2 * sin, x1 * sin + x2 * cos], axis=-1)


def _unrope(d, cos, sin):
    half = RET_DK // 2
    d1, d2 = d[:, :half], d[:, half:]
    return jnp.concatenate([d1 * cos + d2 * sin, d2 * cos - d1 * sin], axis=-1)


def _ret_slices(h):
    q = slice(h * RET_DK, (h + 1) * RET_DK)
    k = slice(RET_Q_COLS + h * RET_DK, RET_Q_COLS + (h + 1) * RET_DK)
    v = slice(2 * RET_Q_COLS + h * RET_DV, 2 * RET_Q_COLS + (h + 1) * RET_DV)
    g = slice(2 * RET_Q_COLS + RET_V_COLS + h * RET_DV, 2 * RET_Q_COLS + RET_V_COLS + (h + 1) * RET_DV)
    o = slice(h * RET_DV, (h + 1) * RET_DV)
    return q, k, v, g, o


def _ret_fwd(proj, gn, consts, name, rider=None):
    t, cols = proj.shape
    n = t // CHUNK
    intra, qd, kd, s_dec, cos, sin = consts
    k_scale = RET_DK ** -0.5

    def body(p_ref, cos_ref, sin_ref, intra_ref, qd_ref, kd_ref, gn_ref, y_ref, o_ref, st_ref, state):
        i = pl.program_id(0)

        @pl.when(i == 0)
        def _():
            state[...] = jnp.zeros_like(state)

        for c in range(RET_STEP):
            rows = slice(c * CHUNK, (c + 1) * CHUNK)
            cosv, sinv = cos_ref[rows, :], sin_ref[rows, :]
            for h in range(RET_HEADS):
                qs, ks, vs, gs, os_ = _ret_slices(h)
                qr = _rope(p_ref[rows, qs], cosv, sinv)
                kr = _rope(p_ref[rows, ks], cosv, sinv) * k_scale
                vb = p_ref[rows, vs].astype(BF16)
                gv = p_ref[rows, gs]
                scores = _dot_nt(qr.astype(BF16), kr.astype(BF16)) * intra_ref[h]
                s_old = state[h]
                s_old_b = s_old.astype(BF16)
                st_ref[c, h] = s_old_b
                o = _dot(scores.astype(BF16), vb) + _dot((qr * qd_ref[h]).astype(BF16), s_old_b)
                state[h] = s_old * s_dec[h] + _dot_tn((kr * kd_ref[h]).astype(BF16), vb)
                rstd = lax.rsqrt(jnp.mean(o * o, axis=-1, keepdims=True) + EPS)
                on = o * rstd * gn_ref[:, os_]
                o_ref[rows, os_] = o
                y_ref[rows, os_] = (gv * _sigmoid(gv) * on).astype(BF16)

    full3 = lambda a: pl.BlockSpec(a.shape, lambda i: (0, 0, 0))
    step = RET_STEP * CHUNK
    return _call(
        body, name, (n // RET_STEP,),
        [pl.BlockSpec((step, cols), lambda i: (i, 0)),
         pl.BlockSpec((step, RET_DK // 2), lambda i: (i, 0)),
         pl.BlockSpec((step, RET_DK // 2), lambda i: (i, 0)),
         full3(intra), full3(qd), full3(kd),
         pl.BlockSpec((1, RET_V_COLS), lambda i: (0, 0))],
        [pl.BlockSpec((step, RET_V_COLS), lambda i: (i, 0)),
         pl.BlockSpec((step, RET_V_COLS), lambda i: (i, 0)),
         pl.BlockSpec((RET_STEP, RET_HEADS, RET_DK, RET_DV), lambda i: (i, 0, 0, 0))],
        [jax.ShapeDtypeStruct((t, RET_V_COLS), BF16),
         jax.ShapeDtypeStruct((t, RET_V_COLS), F32),
         jax.ShapeDtypeStruct((n, RET_HEADS, RET_DK, RET_DV), BF16)],
        [pltpu.VMEM((RET_HEADS, RET_DK, RET_DV), F32)], ("arbitrary",),
        (proj, cos, sin, intra, qd, kd, gn), rider)


def _ret_bwd(proj, gn, o_saved, states, dy, consts, name, rider=None):
    t, cols = proj.shape
    n = t // CHUNK
    intra, qd, kd, s_dec, cos, sin = consts
    k_scale = RET_DK ** -0.5

    def body(p_ref, cos_ref, sin_ref, intra_ref, qd_ref, kd_ref, gn_ref, o_ref, st_ref, dy_ref,
             dp_ref, dgn_ref, dstate):
        i = pl.program_id(0)

        @pl.when(i == 0)
        def _():
            dstate[...] = jnp.zeros_like(dstate)

        dgn = None
        for c in reversed(range(RET_STEP)):
            rows = slice(c * CHUNK, (c + 1) * CHUNK)
            cosv, sinv = cos_ref[rows, :], sin_ref[rows, :]
            dgn_parts = []
            for h in range(RET_HEADS):
                qs, ks, vs, gs, os_ = _ret_slices(h)
                qr = _rope(p_ref[rows, qs], cosv, sinv)
                kr = _rope(p_ref[rows, ks], cosv, sinv) * k_scale
                qb, kb = qr.astype(BF16), kr.astype(BF16)
                vb = p_ref[rows, vs].astype(BF16)
                gv = p_ref[rows, gs]
                ov = o_ref[rows, os_]
                dyv = dy_ref[rows, os_]
                gnv = gn_ref[:, os_]
                sg = _sigmoid(gv)
                rstd = lax.rsqrt(jnp.mean(ov * ov, axis=-1, keepdims=True) + EPS)
                oh = ov * rstd
                d_on = dyv * (gv * sg)
                dg = dyv * (oh * gnv) * (sg * (1.0 + gv * (1.0 - sg)))
                dgn_parts.append(jnp.sum(d_on * oh, axis=0, keepdims=True))
                d_oh = d_on * gnv
                do = rstd * (d_oh - oh * jnp.mean(d_oh * oh, axis=-1, keepdims=True))
                dob = do.astype(BF16)
                mask = intra_ref[h]
                a_b = (_dot_nt(qb, kb) * mask).astype(BF16)
                da_b = (_dot_nt(dob, vb) * mask).astype(BF16)
                ds_new = dstate[h]
                ds_new_b = ds_new.astype(BF16)
                s_old_b = st_ref[c, h]
                qdv, kdv = qd_ref[h], kd_ref[h]
                dv = _dot_tn(a_b, dob) + _dot((kr * kdv).astype(BF16), ds_new_b)
                dqr = _dot(da_b, kb) + _dot_nt(dob, s_old_b) * qdv
                dkr = _dot_tn(da_b, qb) + _dot_nt(vb, ds_new_b) * kdv
                dstate[h] = ds_new * s_dec[h] + _dot_tn((qr * qdv).astype(BF16), dob)
                dp_ref[rows, qs] = _unrope(dqr, cosv, sinv).astype(BF16)
                dp_ref[rows, ks] = _unrope(dkr * k_scale, cosv, sinv).astype(BF16)
                dp_ref[rows, vs] = dv.astype(BF16)
                dp_ref[rows, gs] = dg.astype(BF16)
            part = jnp.concatenate(dgn_parts, axis=-1)
            dgn = part if dgn is None else dgn + part
        _accumulate(dgn_ref, dgn, i)

    steps = n // RET_STEP
    step = RET_STEP * CHUNK
    rev = lambda i: (steps - 1 - i, 0)
    full3 = lambda a: pl.BlockSpec(a.shape, lambda i: (0, 0, 0))
    return _call(
        body, name, (steps,),
        [pl.BlockSpec((step, cols), rev),
         pl.BlockSpec((step, RET_DK // 2), rev),
         pl.BlockSpec((step, RET_DK // 2), rev),
         full3(intra), full3(qd), full3(kd),
         pl.BlockSpec((1, RET_V_COLS), lambda i: (0, 0)),
         pl.BlockSpec((step, RET_V_COLS), rev),
         pl.BlockSpec((RET_STEP, RET_HEADS, RET_DK, RET_DV), lambda i: (steps - 1 - i, 0, 0, 0)),
         pl.BlockSpec((step, RET_V_COLS), rev)],
        [pl.BlockSpec((step, cols), rev),
         pl.BlockSpec((1, RET_V_COLS), lambda i: (0, 0))],
        [jax.ShapeDtypeStruct((t, cols), BF16),
         jax.ShapeDtypeStruct((1, RET_V_COLS), F32)],
        [pltpu.VMEM((RET_HEADS, RET_DK, RET_DV), F32)], ("arbitrary",),
        (proj, cos, sin, intra, qd, kd, gn, o_saved, states, dy), rider)


def _att_common(q_ref, kp_ref, vp_ref):
    blk = pl.program_id(1)
    start = pl.multiple_of(blk * Q_BLOCK, Q_BLOCK)
    kw = kp_ref[pl.ds(start, K_WINDOW), :]
    vw = vp_ref[pl.ds(start, K_WINDOW), :]
    kvalid = blk * Q_BLOCK - K_PAD + lax.broadcasted_iota(jnp.int32, (1, K_WINDOW), 1) >= 0
    lane = lax.broadcasted_iota(jnp.int32, (1, LANES), 1)
    return start, q_ref[...], kw, vw, kvalid, (lane < ATT_DH, lane >= ATT_DH)


def _row_groups():
    return [slice(r * ATT_ROWS, (r + 1) * ATT_ROWS) for r in range(Q_BLOCK // ATT_ROWS)]


def _lane_copies(x):
    return jnp.tile(x, (1, K_WINDOW // LANES))


def _att_specs(t, tp):
    qspec = pl.BlockSpec((Q_BLOCK, LANES), lambda h, i: (i, h))
    kspec = pl.BlockSpec((tp, LANES), lambda h, i: (0, h))
    bspec = pl.BlockSpec((2, Q_BLOCK, K_WINDOW), lambda h, i: (h, 0, 0))
    return qspec, kspec, bspec


def _att_fwd(q, kp, vp, bias, name, rider=None):
    t, d = q.shape
    tp = kp.shape[0]

    def body(q_ref, kp_ref, vp_ref, bias_ref, o_ref, lse_ref, s_scr, p_scr, lse_scr):
        _, q2, kw, vw, kvalid, sel = _att_common(q_ref, kp_ref, vp_ref)
        for hh in range(2):
            s_scr[hh] = _dot_nt(jnp.where(sel[hh], q2, 0), kw)
        for hh in range(2):
            for rows in _row_groups():
                s = jnp.where(kvalid, s_scr[hh, rows, :] + bias_ref[hh, rows, :], NEG)
                m = jnp.max(s, axis=-1, keepdims=True)
                e = jnp.exp(s - m)
                l = jnp.sum(e, axis=-1, keepdims=True)
                p_scr[hh, rows, :] = (e * (1.0 / l)).astype(BF16)
                lse_scr[hh, rows, :] = jnp.broadcast_to(m + jnp.log(l), (ATT_ROWS, LANES))
        outs = [_dot(p_scr[hh], vw) for hh in range(2)]
        o_ref[...] = jnp.where(sel[0], outs[0], outs[1]).astype(BF16)
        lse_ref[...] = jnp.where(sel[0], lse_scr[0], lse_scr[1])

    qspec, kspec, bspec = _att_specs(t, tp)
    return _call(body, name, (d // LANES, t // Q_BLOCK), [qspec, kspec, kspec, bspec], [qspec, qspec],
                 [jax.ShapeDtypeStruct((t, d), BF16), jax.ShapeDtypeStruct((t, d), F32)],
                 [pltpu.VMEM((2, Q_BLOCK, K_WINDOW), F32), pltpu.VMEM((2, Q_BLOCK, K_WINDOW), BF16),
                  pltpu.VMEM((2, Q_BLOCK, LANES), F32)],
                 ("parallel", "arbitrary"), (q, kp, vp, bias), rider)


def _att_bwd(q, kp, vp, bias, do, o, lse, name, rider=None):
    t, d = q.shape
    tp = kp.shape[0]

    def body(q_ref, kp_ref, vp_ref, bias_ref, do_ref, o_ref, lse_ref, dq_ref, dkp_ref, dvp_ref, db_ref,
             s_scr, dp_scr, p_scr, ds_scr, row_scr):
        @pl.when(pl.program_id(1) == 0)
        def _():
            dkp_ref[...] = jnp.zeros_like(dkp_ref)
            dvp_ref[...] = jnp.zeros_like(dvp_ref)
            db_ref[...] = jnp.zeros_like(db_ref)

        start, q2, kw, vw, kvalid, sel = _att_common(q_ref, kp_ref, vp_ref)
        do2 = do_ref[...]
        qm = [jnp.where(sel[hh], q2, 0) for hh in range(2)]
        dom = [jnp.where(sel[hh], do2, 0) for hh in range(2)]
        do_o = do2.astype(F32) * o_ref[...].astype(F32)
        lse2 = lse_ref[...]
        for hh in range(2):
            s_scr[hh] = _dot_nt(qm[hh], kw)
            dp_scr[hh] = _dot_nt(dom[hh], vw)
            lse_h = jnp.max(jnp.where(sel[hh], lse2, NEG), axis=-1, keepdims=True)
            delta = jnp.sum(jnp.where(sel[hh], do_o, 0.0), axis=-1, keepdims=True)
            row_scr[hh, 0] = jnp.broadcast_to(lse_h, (Q_BLOCK, LANES))
            row_scr[hh, 1] = jnp.broadcast_to(delta, (Q_BLOCK, LANES))
        for hh in range(2):
            for rows in _row_groups():
                s = jnp.where(kvalid, s_scr[hh, rows, :] + bias_ref[hh, rows, :], NEG)
                p = jnp.exp(s - _lane_copies(row_scr[hh, 0, rows, :]))
                ds = p * (dp_scr[hh, rows, :] - _lane_copies(row_scr[hh, 1, rows, :]))
                db_ref[hh, rows, :] += ds
                p_scr[hh, rows, :] = p.astype(BF16)
                ds_scr[hh, rows, :] = ds.astype(BF16)
        dqs = [_dot(ds_scr[hh], kw) for hh in range(2)]
        dq_ref[...] = jnp.where(sel[0], dqs[0], dqs[1])
        dkp_ref[:, pl.ds(start, K_WINDOW)] += _dot_tn(qm[0], ds_scr[0]) + _dot_tn(qm[1], ds_scr[1])
        dvp_ref[:, pl.ds(start, K_WINDOW)] += _dot_tn(dom[0], p_scr[0]) + _dot_tn(dom[1], p_scr[1])

    qspec, kspec, bspec = _att_specs(t, tp)
    tspec = pl.BlockSpec((LANES, tp), lambda h, i: (h, 0))
    stage = lambda dt: pltpu.VMEM((2, Q_BLOCK, K_WINDOW), dt)
    return _call(body, name, (d // LANES, t // Q_BLOCK),
                 [qspec, kspec, kspec, bspec, qspec, qspec, qspec],
                 [qspec, tspec, tspec, bspec],
                 [jax.ShapeDtypeStruct((t, d), F32),
                  jax.ShapeDtypeStruct((d, tp), F32),
                  jax.ShapeDtypeStruct((d, tp), F32),
                  jax.ShapeDtypeStruct((ATT_HEADS, Q_BLOCK, K_WINDOW), F32)],
                 [stage(F32), stage(F32), stage(BF16), stage(BF16),
                  pltpu.VMEM((2, 2, Q_BLOCK, LANES), F32)],
                 ("parallel", "arbitrary"), (q, kp, vp, bias, do, o, lse), rider)


def _rel_bin_matrix():
    rows = REL_DELTAS * 2 * REL_BLK
    rho = lax.broadcasted_iota(jnp.int32, (rows, REL_PAD), 0)
    col = lax.broadcasted_iota(jnp.int32, (rows, REL_PAD), 1)
    assert 2 * REL_BLK == 256
    delta = rho >> 8
    c = 255 - (rho & 255)
    dist = K_PAD + REL_BLK * (delta - (K_WINDOW // REL_BLK - 1)) + (c - (REL_BLK - 1))
    idx = jnp.clip(dist, -REL_CLIP, REL_CLIP) + REL_CLIP
    return col == idx


def _rel_expand(rel_pad, name):
    heads = rel_pad.shape[0]
    rows = REL_DELTAS * 2 * REL_BLK

    def body_bin(r_ref, o_ref):
        onehot = jnp.where(_rel_bin_matrix(), 1.0, 0.0).astype(BF16)
        hi, mid, lo = _split3(r_ref[...])
        o_ref[...] = _dot_nt(hi, onehot) + _dot_nt(mid, onehot) + _dot_nt(lo, onehot)

    by_delta = pl.pallas_call(
        body_bin, name=name + "_bin",
        out_shape=jax.ShapeDtypeStruct((heads, rows), F32),
        compiler_params=pltpu.CompilerParams(vmem_limit_bytes=VMEM_LIMIT_V7X),
    )(rel_pad)
    by_delta = by_delta.reshape(heads * REL_DELTAS, 2 * REL_BLK)

    def body_shift(t_ref, o_ref):
        tv = t_ref[...]
        for r in range(REL_BLK):
            o_ref[r] = pltpu.roll(tv, (r + REL_BLK) % (2 * REL_BLK), 1)[:, :REL_BLK]

    return pl.pallas_call(
        body_shift, name=name + "_shift",
        out_shape=jax.ShapeDtypeStruct((REL_BLK, heads * REL_DELTAS, REL_BLK), F32),
        compiler_params=pltpu.CompilerParams(vmem_limit_bytes=VMEM_LIMIT_V7X),
    )(by_delta)


def _bias_table(rel_bias, name):
    heads = rel_bias.shape[0]
    rel_pad = jnp.pad(rel_bias, ((0, 0), (0, REL_PAD - REL_TABLE)))
    tiles = _rel_expand(rel_pad, name)
    tiles = tiles.reshape(REL_BLK, heads, REL_DELTAS, REL_BLK).transpose(1, 2, 0, 3)
    na, nb = Q_BLOCK // REL_BLK, K_WINDOW // REL_BLK
    rows = [jnp.concatenate([tiles[:, a - b + nb - 1] for b in range(nb)], axis=-1) for a in range(na)]
    table = jnp.concatenate(rows, axis=-2)
    qc = np.arange(Q_BLOCK)[:, None] // CHUNK
    kc = np.arange(K_WINDOW)[None, :] // CHUNK
    band = (kc >= qc) & (kc <= qc + PAST_CHUNKS)
    return jnp.where(jnp.asarray(band)[None], table, NEG)


def _rel_reduce(db, name):
    heads = db.shape[0]
    na, nb = Q_BLOCK // REL_BLK, K_WINDOW // REL_BLK

    fold_heads = 4

    def body_fold(db_ref, g_ref):
        for hd in range(fold_heads):
            for delta in range(REL_DELTAS):
                acc = None
                for a in range(na):
                    b = a - (delta - (nb - 1))
                    if 0 <= b < nb:
                        tile = db_ref[hd, a * REL_BLK:(a + 1) * REL_BLK, b * REL_BLK:(b + 1) * REL_BLK]
                        acc = tile if acc is None else acc + tile
                g_ref[hd, delta] = acc

    folded = pl.pallas_call(
        body_fold, name=name + "_fold", grid=(heads // fold_heads,),
        in_specs=[pl.BlockSpec((fold_heads, Q_BLOCK, K_WINDOW), lambda h: (h, 0, 0))],
        out_specs=pl.BlockSpec((fold_heads, REL_DELTAS, REL_BLK, REL_BLK), lambda h: (h, 0, 0, 0)),
        out_shape=jax.ShapeDtypeStruct((heads, REL_DELTAS, REL_BLK, REL_BLK), F32),
        compiler_params=_params("parallel"),
    )(db)
    by_row = folded.transpose(2, 0, 1, 3).reshape(REL_BLK, heads * REL_DELTAS, REL_BLK)

    def body_diag(g_ref, d_ref):
        zeros = jnp.zeros((heads * REL_DELTAS, REL_BLK), F32)
        acc = None
        for r in range(REL_BLK):
            part = pltpu.roll(jnp.concatenate([g_ref[r], zeros], axis=1), REL_BLK - r, 1)
            acc = part if acc is None else acc + part
        d_ref[...] = acc

    diag = pl.pallas_call(
        body_diag, name=name + "_diag",
        out_shape=jax.ShapeDtypeStruct((heads * REL_DELTAS, 2 * REL_BLK), F32),
        compiler_params=pltpu.CompilerParams(vmem_limit_bytes=VMEM_LIMIT_V7X),
    )(by_row)
    diag = diag.reshape(heads, REL_DELTAS * 2 * REL_BLK)

    def body_bin(d_ref, o_ref):
        onehot = jnp.where(_rel_bin_matrix(), 1.0, 0.0).astype(BF16)
        hi, mid, lo = _split3(d_ref[...])
        o_ref[...] = _dot(hi, onehot) + _dot(mid, onehot) + _dot(lo, onehot)

    out = pl.pallas_call(
        body_bin, name=name + "_bin",
        out_shape=jax.ShapeDtypeStruct((heads, REL_PAD), F32),
        compiler_params=pltpu.CompilerParams(vmem_limit_bytes=VMEM_LIMIT_V7X),
    )(diag)
    return out[:, :REL_TABLE]


def _sum_leading(x, name):
    n, r, c = x.shape
    tr = _pick(r, 256, 8)

    def body(x_ref, o_ref):
        acc = x_ref[0].astype(F32)
        for k in range(1, n):
            acc = acc + x_ref[k].astype(F32)
        o_ref[...] = acc

    return pl.pallas_call(
        body, name=name, grid=(r // tr,),
        in_specs=[pl.BlockSpec((n, tr, c), lambda i: (0, i, 0))],
        out_specs=pl.BlockSpec((tr, c), lambda i: (i, 0)),
        out_shape=jax.ShapeDtypeStruct((r, c), F32),
        compiler_params=_params("parallel"),
    )(x)


def _pair_add(g, recv, parity, name):
    _, r, c = g.shape
    tr = _pick(r, 256, 16)

    def body(par_ref, g_ref, r_ref, o_ref):
        o_ref[...] = (g_ref[...].astype(F32) + r_ref[...].astype(F32)).astype(BF16)

    return pl.pallas_call(
        body, name=name,
        grid_spec=pltpu.PrefetchScalarGridSpec(
            num_scalar_prefetch=1, grid=(4, r // tr),
            in_specs=[pl.BlockSpec((1, tr, c), lambda k, i, par: (2 * k + par[0], i, 0)),
                      pl.BlockSpec((1, tr, c), lambda k, i, par: (k, i, 0))],
            out_specs=pl.BlockSpec((1, tr, c), lambda k, i, par: (k, i, 0))),
        out_shape=jax.ShapeDtypeStruct((4, r, c), BF16),
        compiler_params=_params("parallel", "parallel"),
    )(parity, g, recv)


def _adamw(w, g_parts, m, v, name):
    r, c = w.shape
    n = g_parts.shape[0]
    tr = _pick(r, 256, 16 if g_parts.dtype == BF16 else 8)
    c1 = 1.0 - ADAM_B1 ** ADAM_STEP
    c2 = 1.0 - ADAM_B2 ** ADAM_STEP

    def body(w_ref, g_ref, m_ref, v_ref, go_ref, d_ref, nm_ref, nv_ref):
        gv = g_ref[0].astype(F32)
        for k in range(1, n):
            gv = gv + g_ref[k].astype(F32)
        nm = ADAM_B1 * m_ref[...] + (1.0 - ADAM_B1) * gv
        nv = ADAM_B2 * v_ref[...] + (1.0 - ADAM_B2) * (gv * gv)
        go_ref[...] = gv
        d_ref[...] = -ADAM_LR * ((nm / c1) / (jnp.sqrt(nv / c2) + ADAM_EPS) + ADAM_WD * w_ref[...])
        nm_ref[...] = nm
        nv_ref[...] = nv

    spec = pl.BlockSpec((tr, c), lambda i: (i, 0))
    shp = jax.ShapeDtypeStruct((r, c), F32)
    return pl.pallas_call(
        body, name=name, grid=(r // tr,),
        in_specs=[spec, pl.BlockSpec((n, tr, c), lambda i: (0, i, 0)), spec, spec],
        out_specs=[spec] * 4, out_shape=[shp] * 4,
        compiler_params=_params("parallel"),
    )(w, g_parts, m, v)


BIG = (("a_w_in", 1), ("a_w_o", 0), ("a_w_gu", 0), ("a_w_down", 0), ("w_kv", 1),
       ("b_w_q", 0), ("b_w_o", 0), ("b_w_gu", 0), ("b_w_down", 0))
TRANSPOSED = ("a_w_gu", "b_w_gu")
FFN_BLK = 2 * FFN_HIDDEN // N_DEV

SMALL = (("a_norm_g", D_MODEL, True), ("a_gn_g", RET_V_COLS, True), ("a_ffn_norm_g", D_MODEL, True),
         ("kv_norm_g", D_MODEL, False), ("b_norm_g", D_MODEL, False), ("b_ffn_norm_g", D_MODEL, False),
         ("k_norm_g", ATT_DH, False), ("b_q_norm_g", ATT_DH, False),
         ("b_rel_bias", ATT_HEADS * REL_TABLE, False))
SMALL_ROWS, SMALL_COLS = 16, 1024


def _pack_small(vals, last=None):
    flat = jnp.concatenate([vals[n].reshape(-1) for n, _, _ in SMALL])
    room = SMALL_ROWS * SMALL_COLS - flat.shape[0]
    if last is None:
        flat = jnp.pad(flat, (0, room))
    else:
        flat = jnp.concatenate([jnp.pad(flat, (0, room - 1)), last.reshape(1)])
    return flat.reshape(SMALL_ROWS, SMALL_COLS)


def _unpack_small(packed, local):
    flat, out, pos = packed.reshape(-1), {}, 0
    for n, length, sharded in SMALL:
        ln = length // N_DEV if (local and sharded) else length
        out[n] = flat[pos:pos + ln]
        pos += ln
    return out


def _gather_rider(shards, names):
    return _GatherRider([shards[n] for n in names])


def _gathered(rider, names, axis_of):
    return {n: (r.reshape(-1, r.shape[2]) if axis_of[n] == 0 else r) for n, r in zip(names, rider.results)}


def _blocks(g):
    return g if g.ndim == 3 else g.reshape(N_DEV, -1, g.shape[-1])


def _local_step(x, target, shards, w_in, s, parity):
    t = x.shape[0]
    axis_of = dict(BIG)
    consts = _ret_consts(t)
    lane_to_head = np.zeros((D_MODEL, LANES), np.float32)
    lane_to_head[np.arange(D_MODEL), np.arange(D_MODEL) // ATT_DH] = 1.0
    bd = jnp.asarray(lane_to_head).astype(BF16)
    kg_t = jnp.tile(s["k_norm_g"], (1, ATT_HEADS))
    qg_t = jnp.tile(s["b_q_norm_g"], (1, ATT_HEADS))
    q_scale = ATT_DH ** -0.5
    w = {"a_w_in": w_in}
    g, recv = {}, {}

    def gather_on(names):
        return _gather_rider(shards, names), names

    def landed(ride):
        w.update(_gathered(ride[0], ride[1], axis_of))

    def scatter_on(names):
        return _ScatterRider([_blocks(g[n]) for n in names]), names

    def reduced(ride):
        recv.update(zip(ride[1], ride[0].results))

    ride = gather_on(["a_w_o", "a_w_down"])
    proj = _mm(x, w["a_w_in"], "nn", "a_proj", norm_g=s["a_norm_g"], rider=ride[0])
    landed(ride)
    ride = gather_on(["a_w_gu", "w_kv"])
    y, o_ret, states = _ret_fwd(proj, s["a_gn_g"], consts, "a_ret", rider=ride[0])
    landed(ride)
    x1 = _mm(y, w["a_w_o"], "nn", "a_out", res=x)
    ride = gather_on(["b_w_q", "b_w_o"])
    gu_a, act_a = _mm(x1, w["a_w_gu"], "nt", "a_ffn_gu", epilogue="swiglu", out_block=FFN_BLK,
                      norm_g=s["a_ffn_norm_g"], rider=ride[0])
    landed(ride)
    x2 = _mm(act_a, w["a_w_down"], "nn", "a_ffn_down", res=x1)

    kv = _mm(x2, w["w_kv"], "nn", "kv_proj", norm_g=s["kv_norm_g"])
    kp, vp = _kv_prep(kv, kg_t, bd, "kv_prep")

    q_raw = _mm(x2, w["b_w_q"], "nn", "b_q", norm_g=s["b_norm_g"])
    qn = _q_hnorm(q_raw, qg_t, bd, q_scale, "q_hnorm")
    bias = _bias_table(s["b_rel_bias"].reshape(ATT_HEADS, REL_TABLE), "rel")
    ride = gather_on(["b_w_gu", "b_w_down"])
    o_att, lse = _att_fwd(qn, kp, vp, bias, "b_att", rider=ride[0])
    landed(ride)
    x3 = _mm(o_att, w["b_w_o"], "nn", "b_out", res=x2)
    gu_b, act_b = _mm(x3, w["b_w_gu"], "nt", "b_ffn_gu", epilogue="swiglu", out_block=FFN_BLK,
                      norm_g=s["b_ffn_norm_g"])
    dy, loss = _mm(act_b, w["b_w_down"], "nn", "b_ffn_down", res=x3, epilogue="loss", extra=(target,))
    in_blk, kv_blk, ffn_blk = w["a_w_in"].shape[2], w["w_kv"].shape[2], FFN_BLK

    dgu = _mm(dy, w["b_w_down"], "nt", "b_ffn_dgu", out_block=ffn_blk, epilogue="swiglu_bwd", extra=gu_b)
    dgu = dgu.reshape(N_DEV, t, ffn_blk)
    g["b_w_down"] = _mm(act_b, dy, "tn", "b_ffn_gdown", out_dtype=BF16)
    ride = scatter_on(["b_w_down"])
    dx3, g["b_ffn_norm_g"] = _mm(dgu, w["b_w_gu"], "nn", "b_ffn_dh", epilogue="rms_bwd",
                                 extra=(x3, s["b_ffn_norm_g"], dy), rider=ride[0])
    reduced(ride)
    g["b_w_gu"] = _mm(dgu, x3, "tn", "b_ffn_ggu", out_dtype=BF16, norm_g=s["b_ffn_norm_g"], norm_b=True)

    do_att = _mm(dx3, w["b_w_o"], "nt", "b_dout", out_dtype=BF16)
    g["b_w_o"] = _mm(o_att, dx3, "tn", "b_gout", out_dtype=BF16)
    ride = scatter_on(["b_w_gu", "b_w_o"])
    dq, dkp, dvp, db = _att_bwd(qn, kp, vp, bias, do_att, o_att, lse, "b_datt", rider=ride[0])
    reduced(ride)
    g["b_rel_bias"] = _rel_reduce(db, "drel").reshape(1, -1)
    dq_raw, gq = _q_dhnorm(q_raw, qg_t, bd, dq, q_scale, "q_dhnorm")
    g["b_q_norm_g"] = gq.reshape(ATT_HEADS, ATT_DH).sum(axis=0, keepdims=True)
    g["b_w_q"] = _mm(x2, dq_raw, "tn", "b_gq", out_dtype=BF16, norm_g=s["b_norm_g"])
    dx2, g["b_norm_g"] = _mm(dq_raw, w["b_w_q"], "nt", "b_dq", epilogue="rms_bwd",
                             extra=(x2, s["b_norm_g"], dx3))

    dkv, gk = _kv_dprep(kv, kg_t, bd, dkp, dvp, "kv_dprep")
    g["k_norm_g"] = gk.reshape(ATT_HEADS, ATT_DH).sum(axis=0, keepdims=True)
    g["w_kv"] = _mm(x2, dkv, "tn", "kv_g", out_dtype=BF16, out_block=kv_blk, norm_g=s["kv_norm_g"])
    dx2, g["kv_norm_g"] = _mm(dkv, w["w_kv"], "nt", "kv_du", epilogue="rms_bwd",
                              extra=(x2, s["kv_norm_g"], dx2))

    ride = scatter_on(["b_w_q"])
    dgu = _mm(dx2, w["a_w_down"], "nt", "a_ffn_dgu", out_block=ffn_blk, epilogue="swiglu_bwd", extra=gu_a,
              rider=ride[0])
    reduced(ride)
    dgu = dgu.reshape(N_DEV, t, ffn_blk)
    g["a_w_down"] = _mm(act_a, dx2, "tn", "a_ffn_gdown", out_dtype=BF16)
    ride = scatter_on(["a_w_down"])
    dx1, g["a_ffn_norm_g"] = _mm(dgu, w["a_w_gu"], "nn", "a_ffn_dh", epilogue="rms_bwd",
                                 extra=(x1, s["a_ffn_norm_g"], dx2), rider=ride[0])
    reduced(ride)
    ride = scatter_on(["w_kv"])
    g["a_w_gu"] = _mm(dgu, x1, "tn", "a_ffn_ggu", out_dtype=BF16, norm_g=s["a_ffn_norm_g"], norm_b=True,
                      rider=ride[0])
    reduced(ride)

    dy_ret = _mm(dx1, w["a_w_o"], "nt", "a_dout")
    g["a_w_o"] = _mm(y, dx1, "tn", "a_gout", out_dtype=BF16)
    ride = scatter_on(["a_w_gu"])
    dproj, g["a_gn_g"] = _ret_bwd(proj, s["a_gn_g"], o_ret, states, dy_ret, consts, "a_dret", rider=ride[0])
    reduced(ride)
    ride = scatter_on(["a_w_o"])
    g["a_w_in"] = _mm(x, dproj, "tn", "a_gin", out_dtype=BF16, out_block=in_blk, norm_g=s["a_norm_g"],
                      rider=ride[0])
    reduced(ride)
    from_sibling = _exchange(_SiblingSwapRider([g["a_w_in"]]), "rs_sibling")[0]
    chip_sums = _pair_add(g["a_w_in"], from_sibling, parity, "rs_pair_add")
    last = _ChipScatterRider([chip_sums])
    grad_x, g["a_norm_g"] = _mm(dproj, w["a_w_in"], "nt", "a_dproj", epilogue="rms_bwd",
                                extra=(x, s["a_norm_g"], dx1), rider=last)
    recv["a_w_in"] = last.results[0]
    return loss, grad_x, recv, g


ARG_NAMES = ("x", "a_norm_g", "a_w_in", "a_gn_g", "a_w_o", "a_ffn_norm_g", "a_w_gu", "a_w_down",
             "kv_norm_g", "w_kv", "k_norm_g", "b_norm_g", "b_w_q", "b_q_norm_g", "b_rel_bias", "b_w_o",
             "b_ffn_norm_g", "b_w_gu", "b_w_down")
WEIGHT_NAMES = ARG_NAMES[1:]


def _big_shard(a, name):
    a = a[0] if a.ndim == 3 else a
    return a.T if name in TRANSPOSED else a


def _as_given(a, name, shape):
    return (a.T if name in TRANSPOSED else a).reshape(shape)


def kernel(x, a_norm_g, a_w_in, a_gn_g, a_w_o, a_ffn_norm_g, a_w_gu, a_w_down, kv_norm_g, w_kv, k_norm_g, b_norm_g, b_w_q, b_q_norm_g, b_rel_bias, b_w_o, b_ffn_norm_g, b_w_gu, b_w_down, loss_target, m_a_norm_g, m_a_w_in, m_a_gn_g, m_a_w_o, m_a_ffn_norm_g, m_a_w_gu, m_a_w_down, m_kv_norm_g, m_w_kv, m_k_norm_g, m_b_norm_g, m_b_w_q, m_b_q_norm_g, m_b_rel_bias, m_b_w_o, m_b_ffn_norm_g, m_b_w_gu, m_b_w_down, v_a_norm_g, v_a_w_in, v_a_gn_g, v_a_w_o, v_a_ffn_norm_g, v_a_w_gu, v_a_w_down, v_kv_norm_g, v_w_kv, v_k_norm_g, v_b_norm_g, v_b_w_q, v_b_q_norm_g, v_b_rel_bias, v_b_w_o, v_b_ffn_norm_g, v_b_w_gu, v_b_w_down):
    args = (x, a_norm_g, a_w_in, a_gn_g, a_w_o, a_ffn_norm_g, a_w_gu, a_w_down, kv_norm_g, w_kv, k_norm_g,
            b_norm_g, b_w_q, b_q_norm_g, b_rel_bias, b_w_o, b_ffn_norm_g, b_w_gu, b_w_down)
    p = dict(zip(ARG_NAMES, args))
    m_all = dict(zip(WEIGHT_NAMES, (m_a_norm_g, m_a_w_in, m_a_gn_g, m_a_w_o, m_a_ffn_norm_g, m_a_w_gu,
                                    m_a_w_down, m_kv_norm_g, m_w_kv, m_k_norm_g, m_b_norm_g, m_b_w_q,
                                    m_b_q_norm_g, m_b_rel_bias, m_b_w_o, m_b_ffn_norm_g, m_b_w_gu, m_b_w_down)))
    v_all = dict(zip(WEIGHT_NAMES, (v_a_norm_g, v_a_w_in, v_a_gn_g, v_a_w_o, v_a_ffn_norm_g, v_a_w_gu,
                                    v_a_w_down, v_kv_norm_g, v_w_kv, v_k_norm_g, v_b_norm_g, v_b_w_q,
                                    v_b_q_norm_g, v_b_rel_bias, v_b_w_o, v_b_ffn_norm_g, v_b_w_gu, v_b_w_down)))
    xi, yi, ci = _my_place()
    me = 4 * xi + 2 * yi + ci
    big_names = [n for n, _ in BIG]
    axis_of = dict(BIG)

    big_local = {n: _big_shard(p[n], n) for n in big_names}
    shards = {n: a.astype(BF16) for n, a in big_local.items()}
    small_local = _pack_small({n: p[n] for n, _, _ in SMALL})
    w_in, small_all = _exchange(_GatherRider([shards["a_w_in"], small_local]), "gather_in")
    flat_g = small_all.reshape(N_DEV, -1)
    s_full, pos = {}, 0
    for n, length, sharded in SMALL:
        ln = length // N_DEV if sharded else length
        s_full[n] = flat_g[:, pos:pos + ln].reshape(1, -1) if sharded else p[n].reshape(1, -1)
        pos += ln

    parity = jnp.reshape(ci, (1,)).astype(jnp.int32)
    loss, grad_x, recv, g = _local_step(x[0], loss_target[0], shards, w_in, s_full, parity)

    partial = _pack_small({n: g[n] for n, _, _ in SMALL}, last=loss)
    summed = _sum_leading(_exchange(_GatherRider([partial]), "gather_gsmall")[0], "gsmall_sum")
    loss = summed[SMALL_ROWS - 1, SMALL_COLS - 1]
    g_small = _unpack_small(summed, local=False)
    for n, length, sharded in SMALL:
        if sharded:
            g_small[n] = lax.dynamic_slice(g_small[n], (me * (length // N_DEV),), (length // N_DEV,))

    grads, deltas, new_m, new_v = {}, {}, {}, {}
    for n in big_names:
        outs = _adamw(big_local[n], recv[n], _big_shard(m_all[n], n), _big_shard(v_all[n], n), "adamw_" + n)
        grads[n], deltas[n], new_m[n], new_v[n] = (_as_given(a, n, p[n].shape) for a in outs)
    pk = lambda src: _pack_small({n: src[n] for n, _, _ in SMALL})
    outs = _adamw(small_local, pk(g_small)[None], pk(m_all), pk(v_all), "adamw_small")
    g_s, d_s, nm_s, nv_s = (_unpack_small(a, local=True) for a in outs)
    for n, _, _ in SMALL:
        grads[n], deltas[n], new_m[n], new_v[n] = (a[n].reshape(p[n].shape) for a in (g_s, d_s, nm_s, nv_s))

    return (loss, grad_x[None], *[grads[n] for n in WEIGHT_NAMES], *[deltas[n] for n in WEIGHT_NAMES],
            *[new_m[n] for n in WEIGHT_NAMES], *[new_v[n] for n in WEIGHT_NAMES])
```

```python
import numpy as np
import jax
import jax.numpy as jnp
from jax import lax
from jax.experimental import pallas as pl
from jax.experimental.pallas import tpu as pltpu

F32 = jnp.float32
BF16 = jnp.bfloat16

N_DEV = 8
D_MODEL = 1024
CHUNK = 64
EPS = 1e-6
RET_HEADS, RET_DK, RET_DV = 4, 256, 512
RET_STEP = 2
RET_Q_COLS = RET_HEADS * RET_DK
RET_V_COLS = RET_HEADS * RET_DV
ATT_HEADS, ATT_DH = 16, 64
PAST_CHUNKS = 8
REL_CLIP = 256
REL_TABLE = 2 * REL_CLIP + 1
FFN_HIDDEN = 2816
ROPE_BASE = 10000.0
LANES = 128
Q_BLOCK = 256
ATT_SUBS = 4
ATT_ROWS = 32
K_PAD = PAST_CHUNKS * CHUNK
K_WINDOW = Q_BLOCK + K_PAD
REL_BLK = 128
REL_DELTAS = Q_BLOCK // REL_BLK + K_WINDOW // REL_BLK - 1
REL_PAD = 640
NEG = -1e30
VMEM_LIMIT_V7X = 56 * 1024 * 1024
ADAM_LR, ADAM_B1, ADAM_B2, ADAM_EPS, ADAM_WD, ADAM_STEP = 1e-3, 0.9, 0.999, 1e-8, 0.01, 10
MESH = pl.DeviceIdType.MESH
ANY = pl.BlockSpec(memory_space=pl.ANY)


def _params(*semantics):
    return pltpu.CompilerParams(dimension_semantics=semantics, vmem_limit_bytes=VMEM_LIMIT_V7X)


def _pick(dim, cap, align):
    best = None
    for t in range(align, min(dim, cap) + 1, align):
        if dim % t == 0:
            best = t
    assert best is not None, (dim, cap, align)
    return best


def _dot(a, b):
    return lax.dot_general(a, b, (((1,), (0,)), ((), ())), preferred_element_type=F32)


def _dot_nt(a, b):
    return lax.dot_general(a, b, (((1,), (1,)), ((), ())), preferred_element_type=F32)


def _dot_tn(a, b):
    return lax.dot_general(a, b, (((0,), (0,)), ((), ())), preferred_element_type=F32)


def _split2(x):
    hi = x.astype(BF16)
    lo = (x - hi.astype(F32)).astype(BF16)
    return hi, lo


def _split3(x):
    hi = x.astype(BF16)
    r = x - hi.astype(F32)
    mid = r.astype(BF16)
    lo = (r - mid.astype(F32)).astype(BF16)
    return hi, mid, lo


def _sigmoid(x):
    return 1.0 / (1.0 + jnp.exp(-x))


def _accumulate(ref, part, step):
    @pl.when(step == 0)
    def _():
        ref[...] = part

    @pl.when(step > 0)
    def _():
        ref[...] += part


RELAY_AT_NUM, RELAY_AT_DEN = 3, 4


def _my_place():
    return lax.axis_index("x"), lax.axis_index("y"), lax.axis_index("c")


def _flip(v, bit):
    return 1 - v if bit else v


class _NoRelay:
    def relay(self, in_refs, out_refs, sems):
        pass


class _GatherRider:
    def __init__(self, xs):
        self.inputs = list(xs)
        n = len(xs)
        self.out_shape = [jax.ShapeDtypeStruct((N_DEV,) + x.shape, x.dtype) for x in xs]
        self.scratch = [pltpu.SemaphoreType.DMA((7, n)), pltpu.SemaphoreType.DMA((7, n)),
                        pltpu.SemaphoreType.DMA((n,))]
        self.results = None

    def _copies(self, x_refs, out_refs, sems):
        send_sems, recv_sems, local_sems = sems
        n = len(x_refs)
        x, y, c = _my_place()
        me, sibling = (x, y, c), (x, y, 1 - c)
        chips = [(1 - x, y), (x, 1 - y), (1 - x, 1 - y)]

        def slot(a, px, py, pc):
            return out_refs[a].at[4 * px + 2 * py + pc]

        def copy(k, a, block, to, own=False):
            return pltpu.make_async_remote_copy(
                src_ref=x_refs[a] if own else slot(a, *block), dst_ref=slot(a, *block),
                send_sem=send_sems.at[k, a], recv_sem=recv_sems.at[k, a],
                device_id=to, device_id_type=MESH)

        mine = [pltpu.make_async_copy(x_refs[a], slot(a, *me), local_sems.at[a]) for a in range(n)]
        first = []
        for a in range(n):
            first.append(copy(0, a, me, sibling, own=True))
            first += [copy(1 + j, a, me, (*chip, c), own=True) for j, chip in enumerate(chips)]
        return n, c, me, sibling, chips, copy, mine, first

    def start(self, x_refs, out_refs, sems):
        _, _, _, _, _, _, mine, first = self._copies(x_refs, out_refs, sems)
        for cp in mine + first:
            cp.start()

    def relay(self, x_refs, out_refs, sems):
        n, c, me, sibling, chips, copy, _, _ = self._copies(x_refs, out_refs, sems)
        for j, chip in enumerate(chips):
            for a in range(n):
                copy(1 + j, a, (*chip, c), me).wait_recv()
                copy(4 + j, a, (*chip, c), sibling).start()

    def finish(self, x_refs, out_refs, sems):
        n, c, me, sibling, chips, copy, mine, first = self._copies(x_refs, out_refs, sems)
        passed = [copy(4 + j, a, (*chip, c), sibling) for j, chip in enumerate(chips) for a in range(n)]
        for a in range(n):
            copy(0, a, sibling, me).wait_recv()
            for j, chip in enumerate(chips):
                copy(4 + j, a, (*chip, 1 - c), me).wait_recv()
        for cp in first + passed:
            cp.wait_send()
        for cp in mine:
            cp.wait()


class _ScatterRider(_NoRelay):
    def __init__(self, gs):
        self.inputs = list(gs)
        n = len(gs)
        self.out_shape = [jax.ShapeDtypeStruct(g.shape, g.dtype) for g in gs]
        self.scratch = [pltpu.SemaphoreType.DMA((7, n)), pltpu.SemaphoreType.DMA((7, n)),
                        pltpu.SemaphoreType.DMA((n,))]
        self.results = None

    def _copies(self, g_refs, out_refs, sems):
        send_sems, recv_sems, local_sems = sems
        x, y, c = _my_place()
        me = 4 * x + 2 * y + c
        mine, copies = [], []
        for a in range(len(g_refs)):
            mine.append(pltpu.make_async_copy(g_refs[a].at[me], out_refs[a].at[me], local_sems.at[a]))
            for k in range(1, N_DEV):
                px, py, pc = _flip(x, k & 4), _flip(y, k & 2), _flip(c, k & 1)
                copies.append(pltpu.make_async_remote_copy(
                    src_ref=g_refs[a].at[4 * px + 2 * py + pc], dst_ref=out_refs[a].at[me],
                    send_sem=send_sems.at[k - 1, a], recv_sem=recv_sems.at[k - 1, a],
                    device_id=(px, py, pc), device_id_type=MESH))
        return mine, copies

    def start(self, g_refs, out_refs, sems):
        mine, copies = self._copies(g_refs, out_refs, sems)
        for cp in mine + copies:
            cp.start()

    def finish(self, g_refs, out_refs, sems):
        mine, copies = self._copies(g_refs, out_refs, sems)
        for cp in copies + mine:
            cp.wait()


class _SiblingSwapRider(_NoRelay):
    def __init__(self, gs):
        self.inputs = list(gs)
        n = len(gs)
        self.out_shape = [jax.ShapeDtypeStruct((4,) + g.shape[1:], g.dtype) for g in gs]
        self.scratch = [pltpu.SemaphoreType.DMA((4, n)), pltpu.SemaphoreType.DMA((4, n))]
        self.results = None

    def _copies(self, g_refs, out_refs, sems):
        send_sems, recv_sems = sems
        x, y, c = _my_place()
        return [pltpu.make_async_remote_copy(
            src_ref=g_refs[a].at[2 * k + 1 - c], dst_ref=out_refs[a].at[k],
            send_sem=send_sems.at[k, a], recv_sem=recv_sems.at[k, a],
            device_id=(x, y, 1 - c), device_id_type=MESH)
            for a in range(len(g_refs)) for k in range(4)]

    def start(self, g_refs, out_refs, sems):
        for cp in self._copies(g_refs, out_refs, sems):
            cp.start()

    def finish(self, g_refs, out_refs, sems):
        for cp in self._copies(g_refs, out_refs, sems):
            cp.wait()


class _ChipScatterRider(_NoRelay):
    def __init__(self, ps):
        self.inputs = list(ps)
        n = len(ps)
        self.out_shape = [jax.ShapeDtypeStruct(p.shape, p.dtype) for p in ps]
        self.scratch = [pltpu.SemaphoreType.DMA((3, n)), pltpu.SemaphoreType.DMA((3, n)),
                        pltpu.SemaphoreType.DMA((n,))]
        self.results = None

    def _copies(self, p_refs, out_refs, sems):
        send_sems, recv_sems, local_sems = sems
        x, y, c = _my_place()
        my_chip = 2 * x + y
        chips = [(1 - x, y), (x, 1 - y), (1 - x, 1 - y)]
        n = len(p_refs)
        mine = [pltpu.make_async_copy(p_refs[a].at[my_chip], out_refs[a].at[my_chip], local_sems.at[a])
                for a in range(n)]
        copies = [pltpu.make_async_remote_copy(
            src_ref=p_refs[a].at[2 * cx + cy], dst_ref=out_refs[a].at[my_chip],
            send_sem=send_sems.at[j, a], recv_sem=recv_sems.at[j, a],
            device_id=(cx, cy, c), device_id_type=MESH)
            for a in range(n) for j, (cx, cy) in enumerate(chips)]
        return mine, copies

    def start(self, p_refs, out_refs, sems):
        mine, copies = self._copies(p_refs, out_refs, sems)
        for cp in mine + copies:
            cp.start()

    def finish(self, p_refs, out_refs, sems):
        mine, copies = self._copies(p_refs, out_refs, sems)
        for cp in copies + mine:
            cp.wait()


def _call(body, name, grid, in_specs, out_specs, out_shape, scratch, semantics, args, rider=None):
    in_specs, out_specs, out_shape, scratch = list(in_specs), list(out_specs), list(out_shape), list(scratch)
    if rider is None:
        return list(pl.pallas_call(
            body, name=name, grid=grid, in_specs=in_specs, out_specs=out_specs, out_shape=out_shape,
            scratch_shapes=scratch, compiler_params=_params(*semantics))(*args))
    n_in, n_out, n_scr = len(in_specs), len(out_specs), len(scratch)
    r_in, r_out = len(rider.inputs), len(rider.out_shape)

    def wrapped(*refs):
        cuts = np.cumsum([0, n_in, r_in, n_out, r_out, n_scr])
        hi, ri, ho, ro, hs = (refs[cuts[i]:cuts[i + 1]] for i in range(5))
        rs = refs[cuts[5]:]
        step, steps = pl.program_id(0), grid[0]
        for d in range(1, len(grid)):
            step, steps = step * grid[d] + pl.program_id(d), steps * grid[d]

        @pl.when(step == 0)
        def _():
            rider.start(ri, ro, rs)

        body(*hi, *ho, *hs)

        @pl.when(step == (steps * RELAY_AT_NUM) // RELAY_AT_DEN)
        def _():
            rider.relay(ri, ro, rs)

        @pl.when(step == steps - 1)
        def _():
            rider.finish(ri, ro, rs)

    outs = pl.pallas_call(
        wrapped, name=name, grid=grid,
        in_specs=in_specs + [ANY] * r_in, out_specs=out_specs + [ANY] * r_out,
        out_shape=out_shape + rider.out_shape, scratch_shapes=scratch + rider.scratch,
        compiler_params=_params(*(["arbitrary"] * len(grid))),
    )(*args, *rider.inputs)
    rider.results = list(outs[n_out:])
    return list(outs[:n_out])


def _exchange(rider, name):
    r_in, r_out = len(rider.inputs), len(rider.out_shape)

    def body(*refs):
        ri, ro, rs = refs[:r_in], refs[r_in:r_in + r_out], refs[r_in + r_out:]
        rider.start(ri, ro, rs)
        rider.relay(ri, ro, rs)
        rider.finish(ri, ro, rs)

    return list(pl.pallas_call(
        body, name=name, in_specs=[ANY] * r_in, out_specs=[ANY] * r_out,
        out_shape=rider.out_shape, scratch_shapes=rider.scratch)(*rider.inputs))


MM_CAP_MN = 1024
MM_CAP_N = 1536
MM_CAP_K = 3072
MM_CAP_K_TOKENS = 2048
MM_CAP_K_RMS = 1536
NORM_ROWS = 256


def _mm(a, b, mode, name, out_dtype=F32, res=None, out_block=None, epilogue=None, extra=None, norm_g=None,
        norm_b=False, rider=None):
    a3, b3 = a.ndim == 3, b.ndim == 3
    um = un = uk = None
    if mode in ("nn", "nt"):
        if a3:
            m, uk = a.shape[1:]
            k = a.shape[0] * uk
        else:
            m, k = a.shape
    else:
        if a3:
            k, um = a.shape[1:]
            m = a.shape[0] * um
        else:
            k, m = a.shape
    if mode in ("nn", "tn"):
        if b3:
            kb, un = b.shape[1:]
            n = b.shape[0] * un
        else:
            kb, n = b.shape
        assert kb == k, (a.shape, b.shape, mode)
    else:
        if b3:
            n, ukb = b.shape[1:]
            assert b.shape[0] * ukb == k and uk in (None, ukb), (a.shape, b.shape, mode)
            uk = ukb
        else:
            n, kb = b.shape
            assert kb == k, (a.shape, b.shape, mode)
    if out_block is not None:
        assert un in (None, out_block)
        un = out_block

    def tile(dim, unit, cap, align):
        if unit is None:
            return _pick(dim, cap, align), 1
        c = max(1, cap // unit)
        while (dim // unit) % c:
            c -= 1
        return unit, c

    um, cm = tile(m, um, MM_CAP_MN if mode != "tn" else 1408, 128 if mode == "tn" else 16)
    un, cn = tile(n, un, MM_CAP_N, 128)
    cap_k = MM_CAP_K_TOKENS if mode == "tn" else (MM_CAP_K_RMS if epilogue == "rms_bwd" else MM_CAP_K)
    uk, ck = tile(k, uk, cap_k, 128)
    if epilogue == "rms_bwd":
        assert mode != "tn" and n == D_MODEL and cm == cn == 1 and res is None and out_block is None
    if epilogue == "loss":
        assert n == D_MODEL and cm == cn == 1 and res is not None and out_block is None
    if norm_g is not None and norm_b:
        assert mode == "tn" and not b3 and n == D_MODEL and cn == 1
    elif norm_g is not None:
        assert not a3 and (m if mode == "tn" else k) == D_MODEL and (cm if mode == "tn" else ck) == 1
    if epilogue == "swiglu":
        assert res is None and ((mode == "nn" and b3 and out_block is None) or
                                (mode == "nt" and not b3 and out_block is not None))
        cn = 2
    if epilogue == "swiglu_bwd":
        assert mode == "nt" and out_block is not None and extra is not None and res is None
        cn = 1
    tm, tn, tk = cm * um, cn * un, ck * uk
    nk = k // tk
    dot = {"nn": _dot, "nt": _dot_nt, "tn": _dot_tn}[mode]
    half = n // un // 2
    blocked_out = out_block is not None or epilogue in ("swiglu", "swiglu_bwd")
    extras = [] if extra is None else (list(extra) if isinstance(extra, (tuple, list)) else [extra])

    def sl(idx, unit, count):
        return slice(None) if count == 1 else slice(idx * unit, (idx + 1) * unit)

    def body(*refs):
        a_ref, b_ref = refs[0], refs[1]
        pos = 2
        r_ref = ng_ref = None
        if res is not None:
            r_ref, pos = refs[pos], pos + 1
        e_refs, pos = refs[pos:pos + len(extras)], pos + len(extras)
        if norm_g is not None:
            ng_ref, pos = refs[pos], pos + 1
        outs, acc_ref = refs[pos:-1], refs[-1]
        kk = pl.program_id(2)

        def normed(x_ref):
            groups = []
            for r in range(0, x_ref.shape[0], NORM_ROWS):
                xv = x_ref[r:r + NORM_ROWS, :]
                rstd = lax.rsqrt(jnp.mean(xv * xv, axis=-1, keepdims=True) + EPS)
                groups.append((xv * rstd * ng_ref[...]).astype(BF16))
            return jnp.concatenate(groups, axis=0)

        def a_blk(mi, ki):
            if norm_g is not None and not norm_b:
                return normed(a_ref)
            if mode in ("nn", "nt"):
                return a_ref[ki] if a3 else a_ref[:, sl(ki, uk, ck)]
            return a_ref[mi] if a3 else a_ref[:, sl(mi, um, cm)]

        def b_blk(ki, ni):
            if norm_b:
                return normed(b_ref)
            if epilogue == "swiglu":
                return b_ref[ni, 0]
            if mode in ("nn", "tn"):
                return b_ref[ni] if b3 else b_ref[sl(ki, uk, ck), sl(ni, un, cn)]
            return b_ref[ki][sl(ni, un, cn), :] if b3 else b_ref[sl(ni, un, cn), sl(ki, uk, ck)]

        parts = {}
        for mi in range(cm):
            for ni in range(cn):
                part = None
                for ki in range(ck):
                    d = dot(a_blk(mi, ki).astype(BF16), b_blk(ki, ni).astype(BF16))
                    part = d if part is None else part + d
                parts[mi, ni] = part

        def finish(total):
            if epilogue == "swiglu":
                gate, up = total[0, 0], total[0, 1]
                outs[0][0, 0] = gate.astype(BF16)
                outs[0][1, 0] = up.astype(BF16)
                outs[1][0] = (gate * _sigmoid(gate) * up).astype(BF16)
                return
            if epilogue == "swiglu_bwd":
                dact = total[0, 0]
                gate, up = e_refs[0][0, 0].astype(F32), e_refs[0][1, 0].astype(F32)
                sg = _sigmoid(gate)
                outs[0][0, 0] = (dact * up * (sg * (1.0 + gate * (1.0 - sg)))).astype(BF16)
                outs[0][1, 0] = (dact * (gate * sg)).astype(BF16)
                return
            if epilogue == "rms_bwd":
                x_ref, g_ref, dres_ref = e_refs
                dh, dg = total[0, 0], None
                for r in range(0, tm, NORM_ROWS):
                    rows = slice(r, r + NORM_ROWS)
                    xv, dhv = x_ref[rows, :], dh[rows, :]
                    rstd = lax.rsqrt(jnp.mean(xv * xv, axis=-1, keepdims=True) + EPS)
                    xh = xv * rstd
                    dyg = dhv * g_ref[...]
                    c = jnp.mean(dyg * xh, axis=-1, keepdims=True)
                    outs[0][rows, :] = dres_ref[rows, :] + rstd * (dyg - xh * c)
                    part = jnp.sum(dhv * xh, axis=0, keepdims=True)
                    dg = part if dg is None else dg + part
                _accumulate(outs[1], dg, pl.program_id(0))
                return
            if epilogue == "loss":
                diff = r_ref[...] + total[0, 0] - e_refs[0][...]
                outs[0][...] = diff * (1.0 / n)
                sq = jnp.sum(jnp.sum(diff * diff, axis=-1, keepdims=True), axis=0, keepdims=True)
                _accumulate(outs[1], sq * (0.5 / n), pl.program_id(0))
                return
            for (mi, ni), val in total.items():
                rows, cols = sl(mi, um, cm), sl(ni, un, cn)
                if res is not None:
                    val = r_ref[rows, cols] + val
                if blocked_out:
                    outs[0][ni, rows] = val.astype(out_dtype)
                else:
                    outs[0][rows, cols] = val.astype(out_dtype)

        if nk == 1:
            finish(parts)
        else:
            @pl.when(kk == 0)
            def _():
                for (mi, ni), val in parts.items():
                    acc_ref[mi * cn + ni] = val

            @pl.when(jnp.logical_and(kk > 0, kk < nk - 1))
            def _():
                for (mi, ni), val in parts.items():
                    acc_ref[mi * cn + ni] += val

            @pl.when(kk == nk - 1)
            def _():
                finish({key: acc_ref[key[0] * cn + key[1]] + val for key, val in parts.items()})

    if mode in ("nn", "nt"):
        a_spec = (pl.BlockSpec((ck, tm, uk), lambda i, j, kk: (kk, i, 0)) if a3
                  else pl.BlockSpec((tm, tk), lambda i, j, kk: (i, kk)))
    else:
        a_spec = (pl.BlockSpec((cm, tk, um), lambda i, j, kk: (i, kk, 0)) if a3
                  else pl.BlockSpec((tk, tm), lambda i, j, kk: (kk, i)))
    pair_spec = pl.BlockSpec((2, 1, tm, un), lambda i, j, kk: (0, j, i, 0))
    row_spec = pl.BlockSpec((tm, tn), lambda i, j, kk: (i, 0))
    vec_spec = pl.BlockSpec((1, tn), lambda i, j, kk: (0, 0))
    if epilogue == "swiglu" and mode == "nn":
        b = b.reshape(2, half, k, un)
        b_spec = pl.BlockSpec((2, 1, tk, un), lambda i, j, kk: (0, j, kk, 0))
    elif epilogue == "swiglu":
        b = b.reshape(2, half, un, k)
        b_spec = pl.BlockSpec((2, 1, un, tk), lambda i, j, kk: (0, j, 0, kk))
    elif mode in ("nn", "tn"):
        b_spec = (pl.BlockSpec((cn, tk, un), lambda i, j, kk: (j, kk, 0)) if b3
                  else pl.BlockSpec((tk, tn), lambda i, j, kk: (kk, j)))
    else:
        b_spec = (pl.BlockSpec((ck, tn, uk), lambda i, j, kk: (kk, j, 0)) if b3
                  else pl.BlockSpec((tn, tk), lambda i, j, kk: (j, kk)))
    if epilogue == "swiglu":
        out_specs = [pair_spec, pl.BlockSpec((1, tm, un), lambda i, j, kk: (j, i, 0))]
        out_shape = [jax.ShapeDtypeStruct((2, half, m, un), BF16), jax.ShapeDtypeStruct((half, m, un), BF16)]
    elif epilogue == "swiglu_bwd":
        out_specs = [pair_spec]
        out_shape = [jax.ShapeDtypeStruct(extra.shape, BF16)]
    elif epilogue == "rms_bwd":
        out_specs = [row_spec, vec_spec]
        out_shape = [jax.ShapeDtypeStruct((m, n), F32), jax.ShapeDtypeStruct((1, n), F32)]
    elif epilogue == "loss":
        out_specs = [row_spec, pl.BlockSpec((1, 1), lambda i, j, kk: (0, 0))]
        out_shape = [jax.ShapeDtypeStruct((m, n), F32), jax.ShapeDtypeStruct((1, 1), F32)]
    elif blocked_out:
        out_specs = [pl.BlockSpec((cn, tm, un), lambda i, j, kk: (j, i, 0))]
        out_shape = [jax.ShapeDtypeStruct((n // un, m, un), out_dtype)]
    else:
        out_specs = [pl.BlockSpec((tm, tn), lambda i, j, kk: (i, j))]
        out_shape = [jax.ShapeDtypeStruct((m, n), out_dtype)]
    in_specs, args = [a_spec, b_spec], [a, b]
    if res is not None:
        in_specs.append(pl.BlockSpec((tm, tn), lambda i, j, kk: (i, j)))
        args.append(res)
    if epilogue == "swiglu_bwd":
        in_specs.append(pair_spec)
    elif epilogue == "rms_bwd":
        in_specs += [row_spec, vec_spec, row_spec]
    elif epilogue == "loss":
        in_specs.append(row_spec)
    args += extras
    if norm_g is not None:
        in_specs.append(pl.BlockSpec((1, D_MODEL), lambda i, j, kk: (0, 0)))
        args.append(norm_g)
    semantics = ("arbitrary",) * 3 if epilogue in ("rms_bwd", "loss") else ("parallel", "parallel", "arbitrary")
    out = _call(body, name, (m // tm, n // tn, nk), in_specs, out_specs, out_shape,
                [pltpu.VMEM((cm * cn, um, un), F32)], semantics, args, rider)
    return out if epilogue in ("swiglu", "rms_bwd", "loss") else out[0]


def _head_sums(v, ind):
    hi, lo = _split2(v)
    return _dot(hi, ind) + _dot(lo, ind)


def _head_spread(per_head, ind):
    hi, lo = _split2(per_head)
    return _dot_nt(hi, ind) + _dot_nt(lo, ind)


def _head_rstd(xv, ind):
    return _head_spread(lax.rsqrt(_head_sums(xv * xv, ind) * (1.0 / ATT_DH) + EPS), ind)


def _hn_bwd_math(xv, gv, ind, dyv, scale):
    rstd = _head_rstd(xv, ind)
    xh = xv * rstd
    dyn = dyv * scale
    dyg = dyn * gv
    dx = rstd * (dyg - xh * _head_spread(_head_sums(dyg * xh, ind) * (1.0 / ATT_DH), ind))
    return dx, jnp.sum(dyn * xh, axis=0, keepdims=True)


def _q_hnorm(x, g_tiled, bd, scale, name):
    t, d = x.shape
    tm = _pick(t, 512, 16)

    def body(x_ref, g_ref, bd_ref, o_ref):
        xv = x_ref[...]
        o_ref[...] = (xv * _head_rstd(xv, bd_ref[...]) * g_ref[...] * scale).astype(BF16)

    return pl.pallas_call(
        body, name=name, grid=(t // tm,),
        in_specs=[pl.BlockSpec((tm, d), lambda i: (i, 0)), pl.BlockSpec((1, d), lambda i: (0, 0)),
                  pl.BlockSpec((d, LANES), lambda i: (0, 0))],
        out_specs=pl.BlockSpec((tm, d), lambda i: (i, 0)),
        out_shape=jax.ShapeDtypeStruct((t, d), BF16),
        compiler_params=_params("parallel"),
    )(x, g_tiled, bd)


def _q_dhnorm(x, g_tiled, bd, dy, scale, name):
    t, d = x.shape
    tm = _pick(t, 512, 16)

    def body(x_ref, g_ref, bd_ref, dy_ref, dx_ref, dg_ref):
        dx, part = _hn_bwd_math(x_ref[...], g_ref[...], bd_ref[...], dy_ref[...], scale)
        dx_ref[...] = dx.astype(BF16)
        _accumulate(dg_ref, part, pl.program_id(0))

    row = pl.BlockSpec((tm, d), lambda i: (i, 0))
    vec = pl.BlockSpec((1, d), lambda i: (0, 0))
    return pl.pallas_call(
        body, name=name, grid=(t // tm,),
        in_specs=[row, vec, pl.BlockSpec((d, LANES), lambda i: (0, 0)), row],
        out_specs=[row, vec],
        out_shape=[jax.ShapeDtypeStruct((t, d), BF16), jax.ShapeDtypeStruct((1, d), F32)],
        compiler_params=_params("arbitrary"),
    )(x, g_tiled, bd, dy)


def _kv_prep(kv, g_tiled, bd, name):
    t = kv.shape[0]
    d = D_MODEL
    tm = K_PAD
    assert t % tm == 0

    def body(k_ref, v_ref, g_ref, bd_ref, kp_ref, vp_ref):
        i = pl.program_id(0)

        @pl.when(i == 0)
        def _():
            kp_ref[...] = jnp.zeros_like(kp_ref)
            vp_ref[...] = jnp.zeros_like(vp_ref)

        @pl.when(i > 0)
        def _():
            xv = k_ref[...]
            kp_ref[...] = (xv * _head_rstd(xv, bd_ref[...]) * g_ref[...]).astype(BF16)
            vp_ref[...] = v_ref[...].astype(BF16)

    shp = jax.ShapeDtypeStruct((t + K_PAD, d), BF16)
    out = pl.BlockSpec((tm, d), lambda i: (i, 0))
    return pl.pallas_call(
        body, name=name, grid=(t // tm + 1,),
        in_specs=[pl.BlockSpec((tm, d), lambda i: (jnp.maximum(i - 1, 0), 0)),
                  pl.BlockSpec((tm, d), lambda i: (jnp.maximum(i - 1, 0), 1)),
                  pl.BlockSpec((1, d), lambda i: (0, 0)), pl.BlockSpec((d, LANES), lambda i: (0, 0))],
        out_specs=[out, out], out_shape=[shp, shp],
        compiler_params=_params("arbitrary"),
    )(kv, kv, g_tiled, bd)


def _kv_dprep(kv, g_tiled, bd, dkp_t, dvp_t, name):
    t = kv.shape[0]
    d = D_MODEL
    tm = K_PAD

    def body(k_ref, g_ref, bd_ref, dk_ref, dv_ref, o_ref, dg_ref):
        dx, part = _hn_bwd_math(k_ref[...], g_ref[...], bd_ref[...], dk_ref[...].T, 1.0)
        o_ref[:, :d] = dx.astype(BF16)
        o_ref[:, d:] = dv_ref[...].T.astype(BF16)
        _accumulate(dg_ref, part, pl.program_id(0))

    vec = pl.BlockSpec((1, d), lambda i: (0, 0))
    padded = pl.BlockSpec((d, tm), lambda i: (0, i + 1))
    return pl.pallas_call(
        body, name=name, grid=(t // tm,),
        in_specs=[pl.BlockSpec((tm, d), lambda i: (i, 0)), vec, pl.BlockSpec((d, LANES), lambda i: (0, 0)),
                  padded, padded],
        out_specs=[pl.BlockSpec((tm, 2 * d), lambda i: (i, 0)), vec],
        out_shape=[jax.ShapeDtypeStruct((t, 2 * d), BF16), jax.ShapeDtypeStruct((1, d), F32)],
        compiler_params=_params("arbitrary"),
    )(kv, g_tiled, bd, dkp_t, dvp_t)


def _ret_consts(t):
    h = np.arange(RET_HEADS, dtype=np.float32)
    lg = np.log(np.float32(1.0) - np.float32(2.0) ** (np.float32(-5.0) - h)).astype(np.float32)
    tt = np.arange(CHUNK, dtype=np.float32)
    intra = np.exp(lg[:, None, None] * np.abs(tt[:, None] - tt[None, :])).astype(np.float32)
    q_dec = np.exp(lg[:, None] * (tt + 1.0)).astype(np.float32)
    k_dec = np.exp(lg[:, None] * (CHUNK - 1.0 - tt)).astype(np.float32)
    s_dec = [float(v) for v in np.exp(lg * np.float32(CHUNK)).astype(np.float32)]
    qd = np.broadcast_to(q_dec[:, :, None], (RET_HEADS, CHUNK, RET_DK)).copy()
    kd = np.broadcast_to(k_dec[:, :, None], (RET_HEADS, CHUNK, RET_DK)).copy()
    half = RET_DK // 2
    inv_freq = ROPE_BASE ** (-jnp.arange(half, dtype=F32) / half)
    ang = jnp.arange(t).astype(F32)[:, None] * inv_freq[None, :]
    return jnp.asarray(intra), jnp.asarray(qd), jnp.asarray(kd), s_dec, jnp.cos(ang), jnp.sin(ang)


def _rope(x, cos, sin):
    half = RET_DK // 2
    x1, x2 = x[:, :half], x[:, half:]
    return jnp.concatenate([x1 * cos - x2 * sin, x1 * sin + x2 * cos], axis=-1)


def _unrope(d, cos, sin):
    half = RET_DK // 2
    d1, d2 = d[:, :half], d[:, half:]
    return jnp.concatenate([d1 * cos + d2 * sin, d2 * cos - d1 * sin], axis=-1)


def _ret_slices(h):
    q = slice(h * RET_DK, (h + 1) * RET_DK)
    k = slice(RET_Q_COLS + h * RET_DK, RET_Q_COLS + (h + 1) * RET_DK)
    v = slice(2 * RET_Q_COLS + h * RET_DV, 2 * RET_Q_COLS + (h + 1) * RET_DV)
    g = slice(2 * RET_Q_COLS + RET_V_COLS + h * RET_DV, 2 * RET_Q_COLS + RET_V_COLS + (h + 1) * RET_DV)
    o = slice(h * RET_DV, (h + 1) * RET_DV)
    return q, k, v, g, o


def _ret_fwd(proj, gn, consts, name, rider=None):
    t, cols = proj.shape
    n = t // CHUNK
    intra, qd, kd, s_dec, cos, sin = consts
    k_scale = RET_DK ** -0.5

    def body(p_ref, cos_ref, sin_ref, intra_ref, qd_ref, kd_ref, gn_ref, y_ref, o_ref, st_ref, state):
        i = pl.program_id(0)

        @pl.when(i == 0)
        def _():
            state[...] = jnp.zeros_like(state)

        for c in range(RET_STEP):
            rows = slice(c * CHUNK, (c + 1) * CHUNK)
            cosv, sinv = cos_ref[rows, :], sin_ref[rows, :]
            for h in range(RET_HEADS):
                qs, ks, vs, gs, os_ = _ret_slices(h)
                qr = _rope(p_ref[rows, qs], cosv, sinv)
                kr = _rope(p_ref[rows, ks], cosv, sinv) * k_scale
                vb = p_ref[rows, vs].astype(BF16)
                gv = p_ref[rows, gs]
                scores = _dot_nt(qr.astype(BF16), kr.astype(BF16)) * intra_ref[h]
                s_old = state[h]
                s_old_b = s_old.astype(BF16)
                st_ref[c, h] = s_old_b
                o = _dot(scores.astype(BF16), vb) + _dot((qr * qd_ref[h]).astype(BF16), s_old_b)
                state[h] = s_old * s_dec[h] + _dot_tn((kr * kd_ref[h]).astype(BF16), vb)
                rstd = lax.rsqrt(jnp.mean(o * o, axis=-1, keepdims=True) + EPS)
                on = o * rstd * gn_ref[:, os_]
                o_ref[rows, os_] = o
                y_ref[rows, os_] = (gv * _sigmoid(gv) * on).astype(BF16)

    full3 = lambda a: pl.BlockSpec(a.shape, lambda i: (0, 0, 0))
    step = RET_STEP * CHUNK
    return _call(
        body, name, (n // RET_STEP,),
        [pl.BlockSpec((step, cols), lambda i: (i, 0)),
         pl.BlockSpec((step, RET_DK // 2), lambda i: (i, 0)),
         pl.BlockSpec((step, RET_DK // 2), lambda i: (i, 0)),
         full3(intra), full3(qd), full3(kd),
         pl.BlockSpec((1, RET_V_COLS), lambda i: (0, 0))],
        [pl.BlockSpec((step, RET_V_COLS), lambda i: (i, 0)),
         pl.BlockSpec((step, RET_V_COLS), lambda i: (i, 0)),
         pl.BlockSpec((RET_STEP, RET_HEADS, RET_DK, RET_DV), lambda i: (i, 0, 0, 0))],
        [jax.ShapeDtypeStruct((t, RET_V_COLS), BF16),
         jax.ShapeDtypeStruct((t, RET_V_COLS), F32),
         jax.ShapeDtypeStruct((n, RET_HEADS, RET_DK, RET_DV), BF16)],
        [pltpu.VMEM((RET_HEADS, RET_DK, RET_DV), F32)], ("arbitrary",),
        (proj, cos, sin, intra, qd, kd, gn), rider)


def _ret_bwd(proj, gn, o_saved, states, dy, consts, name, rider=None):
    t, cols = proj.shape
    n = t // CHUNK
    intra, qd, kd, s_dec, cos, sin = consts
    k_scale = RET_DK ** -0.5

    def body(p_ref, cos_ref, sin_ref, intra_ref, qd_ref, kd_ref, gn_ref, o_ref, st_ref, dy_ref,
             dp_ref, dgn_ref, dstate):
        i = pl.program_id(0)

        @pl.when(i == 0)
        def _():
            dstate[...] = jnp.zeros_like(dstate)

        dgn = None
        for c in reversed(range(RET_STEP)):
            rows = slice(c * CHUNK, (c + 1) * CHUNK)
            cosv, sinv = cos_ref[rows, :], sin_ref[rows, :]
            dgn_parts = []
            for h in range(RET_HEADS):
                qs, ks, vs, gs, os_ = _ret_slices(h)
                qr = _rope(p_ref[rows, qs], cosv, sinv)
                kr = _rope(p_ref[rows, ks], cosv, sinv) * k_scale
                qb, kb = qr.astype(BF16), kr.astype(BF16)
                vb = p_ref[rows, vs].astype(BF16)
                gv = p_ref[rows, gs]
                ov = o_ref[rows, os_]
                dyv = dy_ref[rows, os_]
                gnv = gn_ref[:, os_]
                sg = _sigmoid(gv)
                rstd = lax.rsqrt(jnp.mean(ov * ov, axis=-1, keepdims=True) + EPS)
                oh = ov * rstd
                d_on = dyv * (gv * sg)
                dg = dyv * (oh * gnv) * (sg * (1.0 + gv * (1.0 - sg)))
                dgn_parts.append(jnp.sum(d_on * oh, axis=0, keepdims=True))
                d_oh = d_on * gnv
                do = rstd * (d_oh - oh * jnp.mean(d_oh * oh, axis=-1, keepdims=True))
                dob = do.astype(BF16)
                mask = intra_ref[h]
                a_b = (_dot_nt(qb, kb) * mask).astype(BF16)
                da_b = (_dot_nt(dob, vb) * mask).astype(BF16)
                ds_new = dstate[h]
                ds_new_b = ds_new.astype(BF16)
                s_old_b = st_ref[c, h]
                qdv, kdv = qd_ref[h], kd_ref[h]
                dv = _dot_tn(a_b, dob) + _dot((kr * kdv).astype(BF16), ds_new_b)
                dqr = _dot(da_b, kb) + _dot_nt(dob, s_old_b) * qdv
                dkr = _dot_tn(da_b, qb) + _dot_nt(vb, ds_new_b) * kdv
                dstate[h] = ds_new * s_dec[h] + _dot_tn((qr * qdv).astype(BF16), dob)
                dp_ref[rows, qs] = _unrope(dqr, cosv, sinv).astype(BF16)
                dp_ref[rows, ks] = _unrope(dkr * k_scale, cosv, sinv).astype(BF16)
                dp_ref[rows, vs] = dv.astype(BF16)
                dp_ref[rows, gs] = dg.astype(BF16)
            part = jnp.concatenate(dgn_parts, axis=-1)
            dgn = part if dgn is None else dgn + part
        _accumulate(dgn_ref, dgn, i)

    steps = n // RET_STEP
    step = RET_STEP * CHUNK
    rev = lambda i: (steps - 1 - i, 0)
    full3 = lambda a: pl.BlockSpec(a.shape, lambda i: (0, 0, 0))
    return _call(
        body, name, (steps,),
        [pl.BlockSpec((step, cols), rev),
         pl.BlockSpec((step, RET_DK // 2), rev),
         pl.BlockSpec((step, RET_DK // 2), rev),
         full3(intra), full3(qd), full3(kd),
         pl.BlockSpec((1, RET_V_COLS), lambda i: (0, 0)),
         pl.BlockSpec((step, RET_V_COLS), rev),
         pl.BlockSpec((RET_STEP, RET_HEADS, RET_DK, RET_DV), lambda i: (steps - 1 - i, 0, 0, 0)),
         pl.BlockSpec((step, RET_V_COLS), rev)],
        [pl.BlockSpec((step, cols), rev),
         pl.BlockSpec((1, RET_V_COLS), lambda i: (0, 0))],
        [jax.ShapeDtypeStruct((t, cols), BF16),
         jax.ShapeDtypeStruct((1, RET_V_COLS), F32)],
        [pltpu.VMEM((RET_HEADS, RET_DK, RET_DV), F32)], ("arbitrary",),
        (proj, cos, sin, intra, qd, kd, gn, o_saved, states, dy), rider)


def _att_common(q_ref, kp_ref, vp_ref, sub):
    blk = pl.program_id(1) * ATT_SUBS + sub
    start = pl.multiple_of(blk * Q_BLOCK, Q_BLOCK)
    kw = kp_ref[pl.ds(start, K_WINDOW), :]
    vw = vp_ref[pl.ds(start, K_WINDOW), :]
    kvalid = blk * Q_BLOCK - K_PAD + lax.broadcasted_iota(jnp.int32, (1, K_WINDOW), 1) >= 0
    lane = lax.broadcasted_iota(jnp.int32, (1, LANES), 1)
    qrows = slice(sub * Q_BLOCK, (sub + 1) * Q_BLOCK)
    return start, qrows, q_ref[qrows, :], kw, vw, kvalid, (lane < ATT_DH, lane >= ATT_DH)


def _row_groups():
    return [slice(r * ATT_ROWS, (r + 1) * ATT_ROWS) for r in range(Q_BLOCK // ATT_ROWS)]


def _lane_copies(x):
    return jnp.tile(x, (1, K_WINDOW // LANES))


def _att_specs(t, tp):
    qspec = pl.BlockSpec((ATT_SUBS * Q_BLOCK, LANES), lambda h, i: (i, h))
    kspec = pl.BlockSpec((tp, LANES), lambda h, i: (0, h))
    bspec = pl.BlockSpec((2, Q_BLOCK, K_WINDOW), lambda h, i: (h, 0, 0))
    return qspec, kspec, bspec


def _att_fwd(q, kp, vp, bias, name, rider=None):
    t, d = q.shape
    tp = kp.shape[0]

    def body(q_ref, kp_ref, vp_ref, bias_ref, o_ref, lse_ref, s_scr, p_scr, lse_scr):
        for sub in range(ATT_SUBS):
            _, qrows, q2, kw, vw, kvalid, sel = _att_common(q_ref, kp_ref, vp_ref, sub)
            for hh in range(2):
                s_scr[sub, hh] = _dot_nt(jnp.where(sel[hh], q2, 0), kw)
            for hh in range(2):
                for rows in _row_groups():
                    s = jnp.where(kvalid, s_scr[sub, hh, rows, :] + bias_ref[hh, rows, :], NEG)
                    m = jnp.max(s, axis=-1, keepdims=True)
                    e = jnp.exp(s - m)
                    l = jnp.sum(e, axis=-1, keepdims=True)
                    p_scr[sub, hh, rows, :] = (e * (1.0 / l)).astype(BF16)
                    lse_scr[sub, hh, rows, :] = jnp.broadcast_to(m + jnp.log(l), (ATT_ROWS, LANES))
            outs = [_dot(p_scr[sub, hh], vw) for hh in range(2)]
            o_ref[qrows, :] = jnp.where(sel[0], outs[0], outs[1]).astype(BF16)
            lse_ref[qrows, :] = jnp.where(sel[0], lse_scr[sub, 0], lse_scr[sub, 1])

    qspec, kspec, bspec = _att_specs(t, tp)
    return _call(body, name, (d // LANES, t // (ATT_SUBS * Q_BLOCK)), [qspec, kspec, kspec, bspec], [qspec, qspec],
                 [jax.ShapeDtypeStruct((t, d), BF16), jax.ShapeDtypeStruct((t, d), F32)],
                 [pltpu.VMEM((ATT_SUBS, 2, Q_BLOCK, K_WINDOW), F32),
                  pltpu.VMEM((ATT_SUBS, 2, Q_BLOCK, K_WINDOW), BF16),
                  pltpu.VMEM((ATT_SUBS, 2, Q_BLOCK, LANES), F32)],
                 ("parallel", "arbitrary"), (q, kp, vp, bias), rider)


def _att_bwd(q, kp, vp, bias, do, o, lse, name, rider=None):
    t, d = q.shape
    tp = kp.shape[0]

    def body(q_ref, kp_ref, vp_ref, bias_ref, do_ref, o_ref, lse_ref, dq_ref, dkp_ref, dvp_ref, db_ref,
             s_scr, dp_scr, p_scr, ds_scr, row_scr):
        @pl.when(pl.program_id(1) == 0)
        def _():
            dkp_ref[...] = jnp.zeros_like(dkp_ref)
            dvp_ref[...] = jnp.zeros_like(dvp_ref)
            db_ref[...] = jnp.zeros_like(db_ref)

        for sub in range(ATT_SUBS):
            start, qrows, q2, kw, vw, kvalid, sel = _att_common(q_ref, kp_ref, vp_ref, sub)
            do2 = do_ref[qrows, :]
            qm = [jnp.where(sel[hh], q2, 0) for hh in range(2)]
            dom = [jnp.where(sel[hh], do2, 0) for hh in range(2)]
            do_o = do2.astype(F32) * o_ref[qrows, :].astype(F32)
            lse2 = lse_ref[qrows, :]
            for hh in range(2):
                s_scr[sub, hh] = _dot_nt(qm[hh], kw)
                dp_scr[sub, hh] = _dot_nt(dom[hh], vw)
                lse_h = jnp.max(jnp.where(sel[hh], lse2, NEG), axis=-1, keepdims=True)
                delta = jnp.sum(jnp.where(sel[hh], do_o, 0.0), axis=-1, keepdims=True)
                row_scr[sub, hh, 0] = jnp.broadcast_to(lse_h, (Q_BLOCK, LANES))
                row_scr[sub, hh, 1] = jnp.broadcast_to(delta, (Q_BLOCK, LANES))
            for hh in range(2):
                for rows in _row_groups():
                    s = jnp.where(kvalid, s_scr[sub, hh, rows, :] + bias_ref[hh, rows, :], NEG)
                    p = jnp.exp(s - _lane_copies(row_scr[sub, hh, 0, rows, :]))
                    ds = p * (dp_scr[sub, hh, rows, :] - _lane_copies(row_scr[sub, hh, 1, rows, :]))
                    db_ref[hh, rows, :] += ds
                    p_scr[sub, hh, rows, :] = p.astype(BF16)
                    ds_scr[sub, hh, rows, :] = ds.astype(BF16)
            dqs = [_dot(ds_scr[sub, hh], kw) for hh in range(2)]
            dq_ref[qrows, :] = jnp.where(sel[0], dqs[0], dqs[1])
            dkp_ref[:, pl.ds(start, K_WINDOW)] += (_dot_tn(qm[0], ds_scr[sub, 0]) +
                                                   _dot_tn(qm[1], ds_scr[sub, 1]))
            dvp_ref[:, pl.ds(start, K_WINDOW)] += (_dot_tn(dom[0], p_scr[sub, 0]) +
                                                   _dot_tn(dom[1], p_scr[sub, 1]))

    qspec, kspec, bspec = _att_specs(t, tp)
    tspec = pl.BlockSpec((LANES, tp), lambda h, i: (h, 0))
    stage = lambda dt: pltpu.VMEM((ATT_SUBS, 2, Q_BLOCK, K_WINDOW), dt)
    return _call(body, name, (d // LANES, t // (ATT_SUBS * Q_BLOCK)),
                 [qspec, kspec, kspec, bspec, qspec, qspec, qspec],
                 [qspec, tspec, tspec, bspec],
                 [jax.ShapeDtypeStruct((t, d), F32),
                  jax.ShapeDtypeStruct((d, tp), F32),
                  jax.ShapeDtypeStruct((d, tp), F32),
                  jax.ShapeDtypeStruct((ATT_HEADS, Q_BLOCK, K_WINDOW), F32)],
                 [stage(F32), stage(F32), stage(BF16), stage(BF16),
                  pltpu.VMEM((ATT_SUBS, 2, 2, Q_BLOCK, LANES), F32)],
                 ("parallel", "arbitrary"), (q, kp, vp, bias, do, o, lse), rider)


def _rel_bin_matrix():
    rows = REL_DELTAS * 2 * REL_BLK
    rho = lax.broadcasted_iota(jnp.int32, (rows, REL_PAD), 0)
    col = lax.broadcasted_iota(jnp.int32, (rows, REL_PAD), 1)
    assert 2 * REL_BLK == 256
    delta = rho >> 8
    c = 255 - (rho & 255)
    dist = K_PAD + REL_BLK * (delta - (K_WINDOW // REL_BLK - 1)) + (c - (REL_BLK - 1))
    idx = jnp.clip(dist, -REL_CLIP, REL_CLIP) + REL_CLIP
    return col == idx


def _rel_expand(rel_pad, name):
    heads = rel_pad.shape[0]
    rows = REL_DELTAS * 2 * REL_BLK

    def body_bin(r_ref, o_ref):
        onehot = jnp.where(_rel_bin_matrix(), 1.0, 0.0).astype(BF16)
        hi, mid, lo = _split3(r_ref[...])
        o_ref[...] = _dot_nt(hi, onehot) + _dot_nt(mid, onehot) + _dot_nt(lo, onehot)

    by_delta = pl.pallas_call(
        body_bin, name=name + "_bin",
        out_shape=jax.ShapeDtypeStruct((heads, rows), F32),
        compiler_params=pltpu.CompilerParams(vmem_limit_bytes=VMEM_LIMIT_V7X),
    )(rel_pad)
    by_delta = by_delta.reshape(heads * REL_DELTAS, 2 * REL_BLK)

    def body_shift(t_ref, o_ref):
        tv = t_ref[...]
        for r in range(REL_BLK):
            o_ref[r] = pltpu.roll(tv, (r + REL_BLK) % (2 * REL_BLK), 1)[:, :REL_BLK]

    return pl.pallas_call(
        body_shift, name=name + "_shift",
        out_shape=jax.ShapeDtypeStruct((REL_BLK, heads * REL_DELTAS, REL_BLK), F32),
        compiler_params=pltpu.CompilerParams(vmem_limit_bytes=VMEM_LIMIT_V7X),
    )(by_delta)


def _bias_table(rel_bias, name):
    heads = rel_bias.shape[0]
    rel_pad = jnp.pad(rel_bias, ((0, 0), (0, REL_PAD - REL_TABLE)))
    tiles = _rel_expand(rel_pad, name)
    tiles = tiles.reshape(REL_BLK, heads, REL_DELTAS, REL_BLK).transpose(1, 2, 0, 3)
    na, nb = Q_BLOCK // REL_BLK, K_WINDOW // REL_BLK
    rows = [jnp.concatenate([tiles[:, a - b + nb - 1] for b in range(nb)], axis=-1) for a in range(na)]
    table = jnp.concatenate(rows, axis=-2)
    qc = np.arange(Q_BLOCK)[:, None] // CHUNK
    kc = np.arange(K_WINDOW)[None, :] // CHUNK
    band = (kc >= qc) & (kc <= qc + PAST_CHUNKS)
    return jnp.where(jnp.asarray(band)[None], table, NEG)


def _rel_reduce(db, name):
    heads = db.shape[0]
    na, nb = Q_BLOCK // REL_BLK, K_WINDOW // REL_BLK

    fold_heads = 4

    def body_fold(db_ref, g_ref):
        for hd in range(fold_heads):
            for delta in range(REL_DELTAS):
                acc = None
                for a in range(na):
                    b = a - (delta - (nb - 1))
                    if 0 <= b < nb:
                        tile = db_ref[hd, a * REL_BLK:(a + 1) * REL_BLK, b * REL_BLK:(b + 1) * REL_BLK]
                        acc = tile if acc is None else acc + tile
                g_ref[hd, delta] = acc

    folded = pl.pallas_call(
        body_fold, name=name + "_fold", grid=(heads // fold_heads,),
        in_specs=[pl.BlockSpec((fold_heads, Q_BLOCK, K_WINDOW), lambda h: (h, 0, 0))],
        out_specs=pl.BlockSpec((fold_heads, REL_DELTAS, REL_BLK, REL_BLK), lambda h: (h, 0, 0, 0)),
        out_shape=jax.ShapeDtypeStruct((heads, REL_DELTAS, REL_BLK, REL_BLK), F32),
        compiler_params=_params("parallel"),
    )(db)
    by_row = folded.transpose(2, 0, 1, 3).reshape(REL_BLK, heads * REL_DELTAS, REL_BLK)

    def body_diag(g_ref, d_ref):
        zeros = jnp.zeros((heads * REL_DELTAS, REL_BLK), F32)
        acc = None
        for r in range(REL_BLK):
            part = pltpu.roll(jnp.concatenate([g_ref[r], zeros], axis=1), REL_BLK - r, 1)
            acc = part if acc is None else acc + part
        d_ref[...] = acc

    diag = pl.pallas_call(
        body_diag, name=name + "_diag",
        out_shape=jax.ShapeDtypeStruct((heads * REL_DELTAS, 2 * REL_BLK), F32),
        compiler_params=pltpu.CompilerParams(vmem_limit_bytes=VMEM_LIMIT_V7X),
    )(by_row)
    diag = diag.reshape(heads, REL_DELTAS * 2 * REL_BLK)

    def body_bin(d_ref, o_ref):
        onehot = jnp.where(_rel_bin_matrix(), 1.0, 0.0).astype(BF16)
        hi, mid, lo = _split3(d_ref[...])
        o_ref[...] = _dot(hi, onehot) + _dot(mid, onehot) + _dot(lo, onehot)

    out = pl.pallas_call(
        body_bin, name=name + "_bin",
        out_shape=jax.ShapeDtypeStruct((heads, REL_PAD), F32),
        compiler_params=pltpu.CompilerParams(vmem_limit_bytes=VMEM_LIMIT_V7X),
    )(diag)
    return out[:, :REL_TABLE]


def _sum_leading(x, name):
    n, r, c = x.shape
    tr = _pick(r, 256, 8)

    def body(x_ref, o_ref):
        acc = x_ref[0].astype(F32)
        for k in range(1, n):
            acc = acc + x_ref[k].astype(F32)
        o_ref[...] = acc

    return pl.pallas_call(
        body, name=name, grid=(r // tr,),
        in_specs=[pl.BlockSpec((n, tr, c), lambda i: (0, i, 0))],
        out_specs=pl.BlockSpec((tr, c), lambda i: (i, 0)),
        out_shape=jax.ShapeDtypeStruct((r, c), F32),
        compiler_params=_params("parallel"),
    )(x)


def _pair_add(g, recv, parity, name):
    _, r, c = g.shape
    tr = _pick(r, 256, 16)

    def body(par_ref, g_ref, r_ref, o_ref):
        o_ref[...] = (g_ref[...].astype(F32) + r_ref[...].astype(F32)).astype(BF16)

    return pl.pallas_call(
        body, name=name,
        grid_spec=pltpu.PrefetchScalarGridSpec(
            num_scalar_prefetch=1, grid=(4, r // tr),
            in_specs=[pl.BlockSpec((1, tr, c), lambda k, i, par: (2 * k + par[0], i, 0)),
                      pl.BlockSpec((1, tr, c), lambda k, i, par: (k, i, 0))],
            out_specs=pl.BlockSpec((1, tr, c), lambda k, i, par: (k, i, 0))),
        out_shape=jax.ShapeDtypeStruct((4, r, c), BF16),
        compiler_params=_params("parallel", "parallel"),
    )(parity, g, recv)


def _adamw(w, g_parts, m, v, name):
    r, c = w.shape
    n = g_parts.shape[0]
    tr = _pick(r, 256, 16 if g_parts.dtype == BF16 else 8)
    c1 = 1.0 - ADAM_B1 ** ADAM_STEP
    c2 = 1.0 - ADAM_B2 ** ADAM_STEP

    def body(w_ref, g_ref, m_ref, v_ref, go_ref, d_ref, nm_ref, nv_ref):
        gv = g_ref[0].astype(F32)
        for k in range(1, n):
            gv = gv + g_ref[k].astype(F32)
        nm = ADAM_B1 * m_ref[...] + (1.0 - ADAM_B1) * gv
        nv = ADAM_B2 * v_ref[...] + (1.0 - ADAM_B2) * (gv * gv)
        go_ref[...] = gv
        d_ref[...] = -ADAM_LR * ((nm / c1) / (jnp.sqrt(nv / c2) + ADAM_EPS) + ADAM_WD * w_ref[...])
        nm_ref[...] = nm
        nv_ref[...] = nv

    spec = pl.BlockSpec((tr, c), lambda i: (i, 0))
    shp = jax.ShapeDtypeStruct((r, c), F32)
    return pl.pallas_call(
        body, name=name, grid=(r // tr,),
        in_specs=[spec, pl.BlockSpec((n, tr, c), lambda i: (0, i, 0)), spec, spec],
        out_specs=[spec] * 4, out_shape=[shp] * 4,
        compiler_params=_params("parallel"),
    )(w, g_parts, m, v)


BIG = (("a_w_in", 1), ("a_w_o", 0), ("a_w_gu", 0), ("a_w_down", 0), ("w_kv", 1),
       ("b_w_q", 0), ("b_w_o", 0), ("b_w_gu", 0), ("b_w_down", 0))
TRANSPOSED = ("a_w_gu", "b_w_gu")
FFN_BLK = 2 * FFN_HIDDEN // N_DEV

SMALL = (("a_norm_g", D_MODEL, True), ("a_gn_g", RET_V_COLS, True), ("a_ffn_norm_g", D_MODEL, True),
         ("kv_norm_g", D_MODEL, False), ("b_norm_g", D_MODEL, False), ("b_ffn_norm_g", D_MODEL, False),
         ("k_norm_g", ATT_DH, False), ("b_q_norm_g", ATT_DH, False),
         ("b_rel_bias", ATT_HEADS * REL_TABLE, False))
SMALL_ROWS, SMALL_COLS = 16, 1024


def _pack_small(vals, last=None):
    flat = jnp.concatenate([vals[n].reshape(-1) for n, _, _ in SMALL])
    room = SMALL_ROWS * SMALL_COLS - flat.shape[0]
    if last is None:
        flat = jnp.pad(flat, (0, room))
    else:
        flat = jnp.concatenate([jnp.pad(flat, (0, room - 1)), last.reshape(1)])
    return flat.reshape(SMALL_ROWS, SMALL_COLS)


def _unpack_small(packed, local):
    flat, out, pos = packed.reshape(-1), {}, 0
    for n, length, sharded in SMALL:
        ln = length // N_DEV if (local and sharded) else length
        out[n] = flat[pos:pos + ln]
        pos += ln
    return out


def _gather_rider(shards, names):
    return _GatherRider([shards[n] for n in names])


def _gathered(rider, names, axis_of):
    return {n: (r.reshape(-1, r.shape[2]) if axis_of[n] == 0 else r) for n, r in zip(names, rider.results)}


def _blocks(g):
    return g if g.ndim == 3 else g.reshape(N_DEV, -1, g.shape[-1])


def _local_step(x, target, shards, w_in, s, parity):
    t = x.shape[0]
    axis_of = dict(BIG)
    consts = _ret_consts(t)
    lane_to_head = np.zeros((D_MODEL, LANES), np.float32)
    lane_to_head[np.arange(D_MODEL), np.arange(D_MODEL) // ATT_DH] = 1.0
    bd = jnp.asarray(lane_to_head).astype(BF16)
    kg_t = jnp.tile(s["k_norm_g"], (1, ATT_HEADS))
    qg_t = jnp.tile(s["b_q_norm_g"], (1, ATT_HEADS))
    q_scale = ATT_DH ** -0.5
    w = {"a_w_in": w_in}
    g, recv = {}, {}

    def gather_on(names):
        return _gather_rider(shards, names), names

    def landed(ride):
        w.update(_gathered(ride[0], ride[1], axis_of))

    def scatter_on(names):
        return _ScatterRider([_blocks(g[n]) for n in names]), names

    def reduced(ride):
        recv.update(zip(ride[1], ride[0].results))

    ride = gather_on(["a_w_o", "a_w_down"])
    proj = _mm(x, w["a_w_in"], "nn", "a_proj", norm_g=s["a_norm_g"], rider=ride[0])
    landed(ride)
    ride = gather_on(["a_w_gu", "w_kv"])
    y, o_ret, states = _ret_fwd(proj, s["a_gn_g"], consts, "a_ret", rider=ride[0])
    landed(ride)
    x1 = _mm(y, w["a_w_o"], "nn", "a_out", res=x)
    ride = gather_on(["b_w_q", "b_w_o"])
    gu_a, act_a = _mm(x1, w["a_w_gu"], "nt", "a_ffn_gu", epilogue="swiglu", out_block=FFN_BLK,
                      norm_g=s["a_ffn_norm_g"], rider=ride[0])
    landed(ride)
    x2 = _mm(act_a, w["a_w_down"], "nn", "a_ffn_down", res=x1)

    kv = _mm(x2, w["w_kv"], "nn", "kv_proj", norm_g=s["kv_norm_g"])
    kp, vp = _kv_prep(kv, kg_t, bd, "kv_prep")

    q_raw = _mm(x2, w["b_w_q"], "nn", "b_q", norm_g=s["b_norm_g"])
    qn = _q_hnorm(q_raw, qg_t, bd, q_scale, "q_hnorm")
    bias = _bias_table(s["b_rel_bias"].reshape(ATT_HEADS, REL_TABLE), "rel")
    ride = gather_on(["b_w_gu", "b_w_down"])
    o_att, lse = _att_fwd(qn, kp, vp, bias, "b_att", rider=ride[0])
    landed(ride)
    x3 = _mm(o_att, w["b_w_o"], "nn", "b_out", res=x2)
    gu_b, act_b = _mm(x3, w["b_w_gu"], "nt", "b_ffn_gu", epilogue="swiglu", out_block=FFN_BLK,
                      norm_g=s["b_ffn_norm_g"])
    dy, loss = _mm(act_b, w["b_w_down"], "nn", "b_ffn_down", res=x3, epilogue="loss", extra=(target,))
    in_blk, kv_blk, ffn_blk = w["a_w_in"].shape[2], w["w_kv"].shape[2], FFN_BLK

    dgu = _mm(dy, w["b_w_down"], "nt", "b_ffn_dgu", out_block=ffn_blk, epilogue="swiglu_bwd", extra=gu_b)
    dgu = dgu.reshape(N_DEV, t, ffn_blk)
    g["b_w_down"] = _mm(act_b, dy, "tn", "b_ffn_gdown", out_dtype=BF16)
    ride = scatter_on(["b_w_down"])
    dx3, g["b_ffn_norm_g"] = _mm(dgu, w["b_w_gu"], "nn", "b_ffn_dh", epilogue="rms_bwd",
                                 extra=(x3, s["b_ffn_norm_g"], dy), rider=ride[0])
    reduced(ride)
    g["b_w_gu"] = _mm(dgu, x3, "tn", "b_ffn_ggu", out_dtype=BF16, norm_g=s["b_ffn_norm_g"], norm_b=True)

    do_att = _mm(dx3, w["b_w_o"], "nt", "b_dout", out_dtype=BF16)
    g["b_w_o"] = _mm(o_att, dx3, "tn", "b_gout", out_dtype=BF16)
    ride = scatter_on(["b_w_gu", "b_w_o"])
    dq, dkp, dvp, db = _att_bwd(qn, kp, vp, bias, do_att, o_att, lse, "b_datt", rider=ride[0])
    reduced(ride)
    g["b_rel_bias"] = _rel_reduce(db, "drel").reshape(1, -1)
    dq_raw, gq = _q_dhnorm(q_raw, qg_t, bd, dq, q_scale, "q_dhnorm")
    g["b_q_norm_g"] = gq.reshape(ATT_HEADS, ATT_DH).sum(axis=0, keepdims=True)
    g["b_w_q"] = _mm(x2, dq_raw, "tn", "b_gq", out_dtype=BF16, norm_g=s["b_norm_g"])
    dx2, g["b_norm_g"] = _mm(dq_raw, w["b_w_q"], "nt", "b_dq", epilogue="rms_bwd",
                             extra=(x2, s["b_norm_g"], dx3))

    dkv, gk = _kv_dprep(kv, kg_t, bd, dkp, dvp, "kv_dprep")
    g["k_norm_g"] = gk.reshape(ATT_HEADS, ATT_DH).sum(axis=0, keepdims=True)
    g["w_kv"] = _mm(x2, dkv, "tn", "kv_g", out_dtype=BF16, out_block=kv_blk, norm_g=s["kv_norm_g"])
    dx2, g["kv_norm_g"] = _mm(dkv, w["w_kv"], "nt", "kv_du", epilogue="rms_bwd",
                              extra=(x2, s["kv_norm_g"], dx2))

    ride = scatter_on(["b_w_q"])
    dgu = _mm(dx2, w["a_w_down"], "nt", "a_ffn_dgu", out_block=ffn_blk, epilogue="swiglu_bwd", extra=gu_a,
              rider=ride[0])
    reduced(ride)
    dgu = dgu.reshape(N_DEV, t, ffn_blk)
    g["a_w_down"] = _mm(act_a, dx2, "tn", "a_ffn_gdown", out_dtype=BF16)
    ride = scatter_on(["a_w_down"])
    dx1, g["a_ffn_norm_g"] = _mm(dgu, w["a_w_gu"], "nn", "a_ffn_dh", epilogue="rms_bwd",
                                 extra=(x1, s["a_ffn_norm_g"], dx2), rider=ride[0])
    reduced(ride)
    ride = scatter_on(["w_kv"])
    g["a_w_gu"] = _mm(dgu, x1, "tn", "a_ffn_ggu", out_dtype=BF16, norm_g=s["a_ffn_norm_g"], norm_b=True,
                      rider=ride[0])
    reduced(ride)

    dy_ret = _mm(dx1, w["a_w_o"], "nt", "a_dout")
    g["a_w_o"] = _mm(y, dx1, "tn", "a_gout", out_dtype=BF16)
    ride = scatter_on(["a_w_gu"])
    dproj, g["a_gn_g"] = _ret_bwd(proj, s["a_gn_g"], o_ret, states, dy_ret, consts, "a_dret", rider=ride[0])
    reduced(ride)
    ride = scatter_on(["a_w_o"])
    g["a_w_in"] = _mm(x, dproj, "tn", "a_gin", out_dtype=BF16, out_block=in_blk, norm_g=s["a_norm_g"],
                      rider=ride[0])
    reduced(ride)
    from_sibling = _exchange(_SiblingSwapRider([g["a_w_in"]]), "rs_sibling")[0]
    chip_sums = _pair_add(g["a_w_in"], from_sibling, parity, "rs_pair_add")
    last = _ChipScatterRider([chip_sums])
    grad_x, g["a_norm_g"] = _mm(dproj, w["a_w_in"], "nt", "a_dproj", epilogue="rms_bwd",
                                extra=(x, s["a_norm_g"], dx1), rider=last)
    recv["a_w_in"] = last.results[0]
    return loss, grad_x, recv, g


ARG_NAMES = ("x", "a_norm_g", "a_w_in", "a_gn_g", "a_w_o", "a_ffn_norm_g", "a_w_gu", "a_w_down",
             "kv_norm_g", "w_kv", "k_norm_g", "b_norm_g", "b_w_q", "b_q_norm_g", "b_rel_bias", "b_w_o",
             "b_ffn_norm_g", "b_w_gu", "b_w_down")
WEIGHT_NAMES = ARG_NAMES[1:]


def _big_shard(a, name):
    a = a[0] if a.ndim == 3 else a
    return a.T if name in TRANSPOSED else a


def _as_given(a, name, shape):
    return (a.T if name in TRANSPOSED else a).reshape(shape)


def kernel(x, a_norm_g, a_w_in, a_gn_g, a_w_o, a_ffn_norm_g, a_w_gu, a_w_down, kv_norm_g, w_kv, k_norm_g, b_norm_g, b_w_q, b_q_norm_g, b_rel_bias, b_w_o, b_ffn_norm_g, b_w_gu, b_w_down, loss_target, m_a_norm_g, m_a_w_in, m_a_gn_g, m_a_w_o, m_a_ffn_norm_g, m_a_w_gu, m_a_w_down, m_kv_norm_g, m_w_kv, m_k_norm_g, m_b_norm_g, m_b_w_q, m_b_q_norm_g, m_b_rel_bias, m_b_w_o, m_b_ffn_norm_g, m_b_w_gu, m_b_w_down, v_a_norm_g, v_a_w_in, v_a_gn_g, v_a_w_o, v_a_ffn_norm_g, v_a_w_gu, v_a_w_down, v_kv_norm_g, v_w_kv, v_k_norm_g, v_b_norm_g, v_b_w_q, v_b_q_norm_g, v_b_rel_bias, v_b_w_o, v_b_ffn_norm_g, v_b_w_gu, v_b_w_down):
    args = (x, a_norm_g, a_w_in, a_gn_g, a_w_o, a_ffn_norm_g, a_w_gu, a_w_down, kv_norm_g, w_kv, k_norm_g,
            b_norm_g, b_w_q, b_q_norm_g, b_rel_bias, b_w_o, b_ffn_norm_g, b_w_gu, b_w_down)
    p = dict(zip(ARG_NAMES, args))
    m_all = dict(zip(WEIGHT_NAMES, (m_a_norm_g, m_a_w_in, m_a_gn_g, m_a_w_o, m_a_ffn_norm_g, m_a_w_gu,
                                    m_a_w_down, m_kv_norm_g, m_w_kv, m_k_norm_g, m_b_norm_g, m_b_w_q,
                                    m_b_q_norm_g, m_b_rel_bias, m_b_w_o, m_b_ffn_norm_g, m_b_w_gu, m_b_w_down)))
    v_all = dict(zip(WEIGHT_NAMES, (v_a_norm_g, v_a_w_in, v_a_gn_g, v_a_w_o, v_a_ffn_norm_g, v_a_w_gu,
                                    v_a_w_down, v_kv_norm_g, v_w_kv, v_k_norm_g, v_b_norm_g, v_b_w_q,
                                    v_b_q_norm_g, v_b_rel_bias, v_b_w_o, v_b_ffn_norm_g, v_b_w_gu, v_b_w_down)))
    xi, yi, ci = _my_place()
    me = 4 * xi + 2 * yi + ci
    big_names = [n for n, _ in BIG]
    axis_of = dict(BIG)

    big_local = {n: _big_shard(p[n], n) for n in big_names}
    shards = {n: a.astype(BF16) for n, a in big_local.items()}
    small_local = _pack_small({n: p[n] for n, _, _ in SMALL})
    w_in, small_all = _exchange(_GatherRider([shards["a_w_in"], small_local]), "gather_in")
    flat_g = small_all.reshape(N_DEV, -1)
    s_full, pos = {}, 0
    for n, length, sharded in SMALL:
        ln = length // N_DEV if sharded else length
        s_full[n] = flat_g[:, pos:pos + ln].reshape(1, -1) if sharded else p[n].reshape(1, -1)
        pos += ln

    parity = jnp.reshape(ci, (1,)).astype(jnp.int32)
    loss, grad_x, recv, g = _local_step(x[0], loss_target[0], shards, w_in, s_full, parity)

    partial = _pack_small({n: g[n] for n, _, _ in SMALL}, last=loss)
    summed = _sum_leading(_exchange(_GatherRider([partial]), "gather_gsmall")[0], "gsmall_sum")
    loss = summed[SMALL_ROWS - 1, SMALL_COLS - 1]
    g_small = _unpack_small(summed, local=False)
    for n, length, sharded in SMALL:
        if sharded:
            g_small[n] = lax.dynamic_slice(g_small[n], (me * (length // N_DEV),), (length // N_DEV,))

    grads, deltas, new_m, new_v = {}, {}, {}, {}
    for n in big_names:
        outs = _adamw(big_local[n], recv[n], _big_shard(m_all[n], n), _big_shard(v_all[n], n), "adamw_" + n)
        grads[n], deltas[n], new_m[n], new_v[n] = (_as_given(a, n, p[n].shape) for a in outs)
    pk = lambda src: _pack_small({n: src[n] for n, _, _ in SMALL})
    outs = _adamw(small_local, pk(g_small)[None], pk(m_all), pk(v_all), "adamw_small")
    g_s, d_s, nm_s, nv_s = (_unpack_small(a, local=True) for a in outs)
    for n, _, _ in SMALL:
        grads[n], deltas[n], new_m[n], new_v[n] = (a[n].reshape(p[n].shape) for a in (g_s, d_s, nm_s, nv_s))

    return (loss, grad_x[None], *[grads[n] for n in WEIGHT_NAMES], *[deltas[n] for n in WEIGHT_NAMES],
            *[new_m[n] for n in WEIGHT_NAMES], *[new_v[n] for n in WEIGHT_NAMES])
```

```python
import numpy as np
import jax
import jax.numpy as jnp
from jax import lax
from jax.experimental import pallas as pl
from jax.experimental.pallas import tpu as pltpu

F32 = jnp.float32
BF16 = jnp.bfloat16

N_DEV = 8
D_MODEL = 1024
CHUNK = 64
EPS = 1e-6
RET_HEADS, RET_DK, RET_DV = 4, 256, 512
RET_STEP = 4
RET_Q_COLS = RET_HEADS * RET_DK
RET_V_COLS = RET_HEADS * RET_DV
ATT_HEADS, ATT_DH = 16, 64
PAST_CHUNKS = 8
REL_CLIP = 256
REL_TABLE = 2 * REL_CLIP + 1
FFN_HIDDEN = 2816
ROPE_BASE = 10000.0
LANES = 128
Q_BLOCK = 256
ATT_SUBS = 4
ATT_ROWS = 32
K_PAD = PAST_CHUNKS * CHUNK
K_WINDOW = Q_BLOCK + K_PAD
REL_BLK = 128
REL_DELTAS = Q_BLOCK // REL_BLK + K_WINDOW // REL_BLK - 1
REL_PAD = 640
NEG = -1e30
VMEM_LIMIT_V7X = 56 * 1024 * 1024
ADAM_LR, ADAM_B1, ADAM_B2, ADAM_EPS, ADAM_WD, ADAM_STEP = 1e-3, 0.9, 0.999, 1e-8, 0.01, 10
MESH = pl.DeviceIdType.MESH
ANY = pl.BlockSpec(memory_space=pl.ANY)


def _params(*semantics):
    return pltpu.CompilerParams(dimension_semantics=semantics, vmem_limit_bytes=VMEM_LIMIT_V7X)


def _pick(dim, cap, align):
    best = None
    for t in range(align, min(dim, cap) + 1, align):
        if dim % t == 0:
            best = t
    assert best is not None, (dim, cap, align)
    return best


def _dot(a, b):
    return lax.dot_general(a, b, (((1,), (0,)), ((), ())), preferred_element_type=F32)


def _dot_nt(a, b):
    return lax.dot_general(a, b, (((1,), (1,)), ((), ())), preferred_element_type=F32)


def _dot_tn(a, b):
    return lax.dot_general(a, b, (((0,), (0,)), ((), ())), preferred_element_type=F32)


def _split2(x):
    hi = x.astype(BF16)
    lo = (x - hi.astype(F32)).astype(BF16)
    return hi, lo


def _split3(x):
    hi = x.astype(BF16)
    r = x - hi.astype(F32)
    mid = r.astype(BF16)
    lo = (r - mid.astype(F32)).astype(BF16)
    return hi, mid, lo


def _sigmoid(x):
    return 1.0 / (1.0 + jnp.exp(-x))


def _accumulate(ref, part, step):
    @pl.when(step == 0)
    def _():
        ref[...] = part

    @pl.when(step > 0)
    def _():
        ref[...] += part


RELAY_AT_NUM, RELAY_AT_DEN = 3, 4


def _my_place():
    return lax.axis_index("x"), lax.axis_index("y"), lax.axis_index("c")


def _flip(v, bit):
    return 1 - v if bit else v


class _NoRelay:
    def relay(self, in_refs, out_refs, sems):
        pass


class _GatherRider:
    def __init__(self, xs):
        self.inputs = list(xs)
        n = len(xs)
        self.out_shape = [jax.ShapeDtypeStruct((N_DEV,) + x.shape, x.dtype) for x in xs]
        self.scratch = [pltpu.SemaphoreType.DMA((7, n)), pltpu.SemaphoreType.DMA((7, n)),
                        pltpu.SemaphoreType.DMA((n,))]
        self.results = None

    def _copies(self, x_refs, out_refs, sems):
        send_sems, recv_sems, local_sems = sems
        n = len(x_refs)
        x, y, c = _my_place()
        me, sibling = (x, y, c), (x, y, 1 - c)
        chips = [(1 - x, y), (x, 1 - y), (1 - x, 1 - y)]

        def slot(a, px, py, pc):
            return out_refs[a].at[4 * px + 2 * py + pc]

        def copy(k, a, block, to, own=False):
            return pltpu.make_async_remote_copy(
                src_ref=x_refs[a] if own else slot(a, *block), dst_ref=slot(a, *block),
                send_sem=send_sems.at[k, a], recv_sem=recv_sems.at[k, a],
                device_id=to, device_id_type=MESH)

        mine = [pltpu.make_async_copy(x_refs[a], slot(a, *me), local_sems.at[a]) for a in range(n)]
        first = []
        for a in range(n):
            first.append(copy(0, a, me, sibling, own=True))
            first += [copy(1 + j, a, me, (*chip, c), own=True) for j, chip in enumerate(chips)]
        return n, c, me, sibling, chips, copy, mine, first

    def start(self, x_refs, out_refs, sems):
        _, _, _, _, _, _, mine, first = self._copies(x_refs, out_refs, sems)
        for cp in mine + first:
            cp.start()

    def relay(self, x_refs, out_refs, sems):
        n, c, me, sibling, chips, copy, _, _ = self._copies(x_refs, out_refs, sems)
        for j, chip in enumerate(chips):
            for a in range(n):
                copy(1 + j, a, (*chip, c), me).wait_recv()
                copy(4 + j, a, (*chip, c), sibling).start()

    def finish(self, x_refs, out_refs, sems):
        n, c, me, sibling, chips, copy, mine, first = self._copies(x_refs, out_refs, sems)
        passed = [copy(4 + j, a, (*chip, c), sibling) for j, chip in enumerate(chips) for a in range(n)]
        for a in range(n):
            copy(0, a, sibling, me).wait_recv()
            for j, chip in enumerate(chips):
                copy(4 + j, a, (*chip, 1 - c), me).wait_recv()
        for cp in first + passed:
            cp.wait_send()
        for cp in mine:
            cp.wait()


class _ScatterRider(_NoRelay):
    def __init__(self, gs):
        self.inputs = list(gs)
        n = len(gs)
        self.out_shape = [jax.ShapeDtypeStruct(g.shape, g.dtype) for g in gs]
        self.scratch = [pltpu.SemaphoreType.DMA((7, n)), pltpu.SemaphoreType.DMA((7, n)),
                        pltpu.SemaphoreType.DMA((n,))]
        self.results = None

    def _copies(self, g_refs, out_refs, sems):
        send_sems, recv_sems, local_sems = sems
        x, y, c = _my_place()
        me = 4 * x + 2 * y + c
        mine, copies = [], []
        for a in range(len(g_refs)):
            mine.append(pltpu.make_async_copy(g_refs[a].at[me], out_refs[a].at[me], local_sems.at[a]))
            for k in range(1, N_DEV):
                px, py, pc = _flip(x, k & 4), _flip(y, k & 2), _flip(c, k & 1)
                copies.append(pltpu.make_async_remote_copy(
                    src_ref=g_refs[a].at[4 * px + 2 * py + pc], dst_ref=out_refs[a].at[me],
                    send_sem=send_sems.at[k - 1, a], recv_sem=recv_sems.at[k - 1, a],
                    device_id=(px, py, pc), device_id_type=MESH))
        return mine, copies

    def start(self, g_refs, out_refs, sems):
        mine, copies = self._copies(g_refs, out_refs, sems)
        for cp in mine + copies:
            cp.start()

    def finish(self, g_refs, out_refs, sems):
        mine, copies = self._copies(g_refs, out_refs, sems)
        for cp in copies + mine:
            cp.wait()


class _SiblingSwapRider(_NoRelay):
    def __init__(self, gs):
        self.inputs = list(gs)
        n = len(gs)
        self.out_shape = [jax.ShapeDtypeStruct((4,) + g.shape[1:], g.dtype) for g in gs]
        self.scratch = [pltpu.SemaphoreType.DMA((4, n)), pltpu.SemaphoreType.DMA((4, n))]
        self.results = None

    def _copies(self, g_refs, out_refs, sems):
        send_sems, recv_sems = sems
        x, y, c = _my_place()
        return [pltpu.make_async_remote_copy(
            src_ref=g_refs[a].at[2 * k + 1 - c], dst_ref=out_refs[a].at[k],
            send_sem=send_sems.at[k, a], recv_sem=recv_sems.at[k, a],
            device_id=(x, y, 1 - c), device_id_type=MESH)
            for a in range(len(g_refs)) for k in range(4)]

    def start(self, g_refs, out_refs, sems):
        for cp in self._copies(g_refs, out_refs, sems):
            cp.start()

    def finish(self, g_refs, out_refs, sems):
        for cp in self._copies(g_refs, out_refs, sems):
            cp.wait()


class _ChipScatterRider(_NoRelay):
    def __init__(self, ps):
        self.inputs = list(ps)
        n = len(ps)
        self.out_shape = [jax.ShapeDtypeStruct(p.shape, p.dtype) for p in ps]
        self.scratch = [pltpu.SemaphoreType.DMA((3, n)), pltpu.SemaphoreType.DMA((3, n)),
                        pltpu.SemaphoreType.DMA((n,))]
        self.results = None

    def _copies(self, p_refs, out_refs, sems):
        send_sems, recv_sems, local_sems = sems
        x, y, c = _my_place()
        my_chip = 2 * x + y
        chips = [(1 - x, y), (x, 1 - y), (1 - x, 1 - y)]
        n = len(p_refs)
        mine = [pltpu.make_async_copy(p_refs[a].at[my_chip], out_refs[a].at[my_chip], local_sems.at[a])
                for a in range(n)]
        copies = [pltpu.make_async_remote_copy(
            src_ref=p_refs[a].at[2 * cx + cy], dst_ref=out_refs[a].at[my_chip],
            send_sem=send_sems.at[j, a], recv_sem=recv_sems.at[j, a],
            device_id=(cx, cy, c), device_id_type=MESH)
            for a in range(n) for j, (cx, cy) in enumerate(chips)]
        return mine, copies

    def start(self, p_refs, out_refs, sems):
        mine, copies = self._copies(p_refs, out_refs, sems)
        for cp in mine + copies:
            cp.start()

    def finish(self, p_refs, out_refs, sems):
        mine, copies = self._copies(p_refs, out_refs, sems)
        for cp in copies + mine:
            cp.wait()


def _call(body, name, grid, in_specs, out_specs, out_shape, scratch, semantics, args, rider=None):
    in_specs, out_specs, out_shape, scratch = list(in_specs), list(out_specs), list(out_shape), list(scratch)
    if rider is None:
        return list(pl.pallas_call(
            body, name=name, grid=grid, in_specs=in_specs, out_specs=out_specs, out_shape=out_shape,
            scratch_shapes=scratch, compiler_params=_params(*semantics))(*args))
    n_in, n_out, n_scr = len(in_specs), len(out_specs), len(scratch)
    r_in, r_out = len(rider.inputs), len(rider.out_shape)

    def wrapped(*refs):
        cuts = np.cumsum([0, n_in, r_in, n_out, r_out, n_scr])
        hi, ri, ho, ro, hs = (refs[cuts[i]:cuts[i + 1]] for i in range(5))
        rs = refs[cuts[5]:]
        step, steps = pl.program_id(0), grid[0]
        for d in range(1, len(grid)):
            step, steps = step * grid[d] + pl.program_id(d), steps * grid[d]

        @pl.when(step == 0)
        def _():
            rider.start(ri, ro, rs)

        body(*hi, *ho, *hs)

        @pl.when(step == (steps * RELAY_AT_NUM) // RELAY_AT_DEN)
        def _():
            rider.relay(ri, ro, rs)

        @pl.when(step == steps - 1)
        def _():
            rider.finish(ri, ro, rs)

    outs = pl.pallas_call(
        wrapped, name=name, grid=grid,
        in_specs=in_specs + [ANY] * r_in, out_specs=out_specs + [ANY] * r_out,
        out_shape=out_shape + rider.out_shape, scratch_shapes=scratch + rider.scratch,
        compiler_params=_params(*(["arbitrary"] * len(grid))),
    )(*args, *rider.inputs)
    rider.results = list(outs[n_out:])
    return list(outs[:n_out])


def _exchange(rider, name):
    r_in, r_out = len(rider.inputs), len(rider.out_shape)

    def body(*refs):
        ri, ro, rs = refs[:r_in], refs[r_in:r_in + r_out], refs[r_in + r_out:]
        rider.start(ri, ro, rs)
        rider.relay(ri, ro, rs)
        rider.finish(ri, ro, rs)

    return list(pl.pallas_call(
        body, name=name, in_specs=[ANY] * r_in, out_specs=[ANY] * r_out,
        out_shape=rider.out_shape, scratch_shapes=rider.scratch)(*rider.inputs))


MM_CAP_MN = 1024
MM_CAP_N = 1536
MM_CAP_K = 3072
MM_CAP_K_TOKENS = 2048
MM_CAP_K_RMS = 1536
NORM_ROWS = 256


def _mm(a, b, mode, name, out_dtype=F32, res=None, out_block=None, epilogue=None, extra=None, norm_g=None,
        norm_b=False, rider=None):
    a3, b3 = a.ndim == 3, b.ndim == 3
    um = un = uk = None
    if mode in ("nn", "nt"):
        if a3:
            m, uk = a.shape[1:]
            k = a.shape[0] * uk
        else:
            m, k = a.shape
    else:
        if a3:
            k, um = a.shape[1:]
            m = a.shape[0] * um
        else:
            k, m = a.shape
    if mode in ("nn", "tn"):
        if b3:
            kb, un = b.shape[1:]
            n = b.shape[0] * un
        else:
            kb, n = b.shape
        assert kb == k, (a.shape, b.shape, mode)
    else:
        if b3:
            n, ukb = b.shape[1:]
            assert b.shape[0] * ukb == k and uk in (None, ukb), (a.shape, b.shape, mode)
            uk = ukb
        else:
            n, kb = b.shape
            assert kb == k, (a.shape, b.shape, mode)
    if out_block is not None:
        assert un in (None, out_block)
        un = out_block

    def tile(dim, unit, cap, align):
        if unit is None:
            return _pick(dim, cap, align), 1
        c = max(1, cap // unit)
        while (dim // unit) % c:
            c -= 1
        return unit, c

    um, cm = tile(m, um, MM_CAP_MN if mode != "tn" else 1408, 128 if mode == "tn" else 16)
    un, cn = tile(n, un, MM_CAP_N, 128)
    cap_k = MM_CAP_K_TOKENS if mode == "tn" else (MM_CAP_K_RMS if epilogue == "rms_bwd" else MM_CAP_K)
    uk, ck = tile(k, uk, cap_k, 128)
    if epilogue == "rms_bwd":
        assert mode != "tn" and n == D_MODEL and cm == cn == 1 and res is None and out_block is None
    if epilogue == "loss":
        assert n == D_MODEL and cm == cn == 1 and res is not None and out_block is None
    if norm_g is not None and norm_b:
        assert mode == "tn" and not b3 and n == D_MODEL and cn == 1
    elif norm_g is not None:
        assert not a3 and (m if mode == "tn" else k) == D_MODEL and (cm if mode == "tn" else ck) == 1
    if epilogue == "swiglu":
        assert res is None and ((mode == "nn" and b3 and out_block is None) or
                                (mode == "nt" and not b3 and out_block is not None))
        cn = 2
    if epilogue == "swiglu_bwd":
        assert mode == "nt" and out_block is not None and extra is not None and res is None
        cn = 1
    tm, tn, tk = cm * um, cn * un, ck * uk
    nk = k // tk
    dot = {"nn": _dot, "nt": _dot_nt, "tn": _dot_tn}[mode]
    half = n // un // 2
    blocked_out = out_block is not None or epilogue in ("swiglu", "swiglu_bwd")
    extras = [] if extra is None else (list(extra) if isinstance(extra, (tuple, list)) else [extra])

    def sl(idx, unit, count):
        return slice(None) if count == 1 else slice(idx * unit, (idx + 1) * unit)

    def body(*refs):
        a_ref, b_ref = refs[0], refs[1]
        pos = 2
        r_ref = ng_ref = None
        if res is not None:
            r_ref, pos = refs[pos], pos + 1
        e_refs, pos = refs[pos:pos + len(extras)], pos + len(extras)
        if norm_g is not None:
            ng_ref, pos = refs[pos], pos + 1
        outs, acc_ref = refs[pos:-1], refs[-1]
        kk = pl.program_id(2)

        def normed(x_ref):
            groups = []
            for r in range(0, x_ref.shape[0], NORM_ROWS):
                xv = x_ref[r:r + NORM_ROWS, :]
                rstd = lax.rsqrt(jnp.mean(xv * xv, axis=-1, keepdims=True) + EPS)
                groups.append((xv * rstd * ng_ref[...]).astype(BF16))
            return jnp.concatenate(groups, axis=0)

        def a_blk(mi, ki):
            if norm_g is not None and not norm_b:
                return normed(a_ref)
            if mode in ("nn", "nt"):
                return a_ref[ki] if a3 else a_ref[:, sl(ki, uk, ck)]
            return a_ref[mi] if a3 else a_ref[:, sl(mi, um, cm)]

        def b_blk(ki, ni):
            if norm_b:
                return normed(b_ref)
            if epilogue == "swiglu":
                return b_ref[ni, 0]
            if mode in ("nn", "tn"):
                return b_ref[ni] if b3 else b_ref[sl(ki, uk, ck), sl(ni, un, cn)]
            return b_ref[ki][sl(ni, un, cn), :] if b3 else b_ref[sl(ni, un, cn), sl(ki, uk, ck)]

        parts = {}
        for mi in range(cm):
            for ni in range(cn):
                part = None
                for ki in range(ck):
                    d = dot(a_blk(mi, ki).astype(BF16), b_blk(ki, ni).astype(BF16))
                    part = d if part is None else part + d
                parts[mi, ni] = part

        def finish(total):
            if epilogue == "swiglu":
                gate, up = total[0, 0], total[0, 1]
                outs[0][0, 0] = gate.astype(BF16)
                outs[0][1, 0] = up.astype(BF16)
                outs[1][0] = (gate * _sigmoid(gate) * up).astype(BF16)
                return
            if epilogue == "swiglu_bwd":
                dact = total[0, 0]
                gate, up = e_refs[0][0, 0].astype(F32), e_refs[0][1, 0].astype(F32)
                sg = _sigmoid(gate)
                outs[0][0, 0] = (dact * up * (sg * (1.0 + gate * (1.0 - sg)))).astype(BF16)
                outs[0][1, 0] = (dact * (gate * sg)).astype(BF16)
                return
            if epilogue == "rms_bwd":
                x_ref, g_ref, dres_ref = e_refs
                dh, dg = total[0, 0], None
                for r in range(0, tm, NORM_ROWS):
                    rows = slice(r, r + NORM_ROWS)
                    xv, dhv = x_ref[rows, :], dh[rows, :]
                    rstd = lax.rsqrt(jnp.mean(xv * xv, axis=-1, keepdims=True) + EPS)
                    xh = xv * rstd
                    dyg = dhv * g_ref[...]
                    c = jnp.mean(dyg * xh, axis=-1, keepdims=True)
                    outs[0][rows, :] = dres_ref[rows, :] + rstd * (dyg - xh * c)
                    part = jnp.sum(dhv * xh, axis=0, keepdims=True)
                    dg = part if dg is None else dg + part
                _accumulate(outs[1], dg, pl.program_id(0))
                return
            if epilogue == "loss":
                diff = r_ref[...] + total[0, 0] - e_refs[0][...]
                outs[0][...] = diff * (1.0 / n)
                sq = jnp.sum(jnp.sum(diff * diff, axis=-1, keepdims=True), axis=0, keepdims=True)
                _accumulate(outs[1], sq * (0.5 / n), pl.program_id(0))
                return
            for (mi, ni), val in total.items():
                rows, cols = sl(mi, um, cm), sl(ni, un, cn)
                if res is not None:
                    val = r_ref[rows, cols] + val
                if blocked_out:
                    outs[0][ni, rows] = val.astype(out_dtype)
                else:
                    outs[0][rows, cols] = val.astype(out_dtype)

        if nk == 1:
            finish(parts)
        else:
            @pl.when(kk == 0)
            def _():
                for (mi, ni), val in parts.items():
                    acc_ref[mi * cn + ni] = val

            @pl.when(jnp.logical_and(kk > 0, kk < nk - 1))
            def _():
                for (mi, ni), val in parts.items():
                    acc_ref[mi * cn + ni] += val

            @pl.when(kk == nk - 1)
            def _():
                finish({key: acc_ref[key[0] * cn + key[1]] + val for key, val in parts.items()})

    if mode in ("nn", "nt"):
        a_spec = (pl.BlockSpec((ck, tm, uk), lambda i, j, kk: (kk, i, 0)) if a3
                  else pl.BlockSpec((tm, tk), lambda i, j, kk: (i, kk)))
    else:
        a_spec = (pl.BlockSpec((cm, tk, um), lambda i, j, kk: (i, kk, 0)) if a3
                  else pl.BlockSpec((tk, tm), lambda i, j, kk: (kk, i)))
    pair_spec = pl.BlockSpec((2, 1, tm, un), lambda i, j, kk: (0, j, i, 0))
    row_spec = pl.BlockSpec((tm, tn), lambda i, j, kk: (i, 0))
    vec_spec = pl.BlockSpec((1, tn), lambda i, j, kk: (0, 0))
    if epilogue == "swiglu" and mode == "nn":
        b = b.reshape(2, half, k, un)
        b_spec = pl.BlockSpec((2, 1, tk, un), lambda i, j, kk: (0, j, kk, 0))
    elif epilogue == "swiglu":
        b = b.reshape(2, half, un, k)
        b_spec = pl.BlockSpec((2, 1, un, tk), lambda i, j, kk: (0, j, 0, kk))
    elif mode in ("nn", "tn"):
        b_spec = (pl.BlockSpec((cn, tk, un), lambda i, j, kk: (j, kk, 0)) if b3
                  else pl.BlockSpec((tk, tn), lambda i, j, kk: (kk, j)))
    else:
        b_spec = (pl.BlockSpec((ck, tn, uk), lambda i, j, kk: (kk, j, 0)) if b3
                  else pl.BlockSpec((tn, tk), lambda i, j, kk: (j, kk)))
    if epilogue == "swiglu":
        out_specs = [pair_spec, pl.BlockSpec((1, tm, un), lambda i, j, kk: (j, i, 0))]
        out_shape = [jax.ShapeDtypeStruct((2, half, m, un), BF16), jax.ShapeDtypeStruct((half, m, un), BF16)]
    elif epilogue == "swiglu_bwd":
        out_specs = [pair_spec]
        out_shape = [jax.ShapeDtypeStruct(extra.shape, BF16)]
    elif epilogue == "rms_bwd":
        out_specs = [row_spec, vec_spec]
        out_shape = [jax.ShapeDtypeStruct((m, n), F32), jax.ShapeDtypeStruct((1, n), F32)]
    elif epilogue == "loss":
        out_specs = [row_spec, pl.BlockSpec((1, 1), lambda i, j, kk: (0, 0))]
        out_shape = [jax.ShapeDtypeStruct((m, n), F32), jax.ShapeDtypeStruct((1, 1), F32)]
    elif blocked_out:
        out_specs = [pl.BlockSpec((cn, tm, un), lambda i, j, kk: (j, i, 0))]
        out_shape = [jax.ShapeDtypeStruct((n // un, m, un), out_dtype)]
    else:
        out_specs = [pl.BlockSpec((tm, tn), lambda i, j, kk: (i, j))]
        out_shape = [jax.ShapeDtypeStruct((m, n), out_dtype)]
    in_specs, args = [a_spec, b_spec], [a, b]
    if res is not None:
        in_specs.append(pl.BlockSpec((tm, tn), lambda i, j, kk: (i, j)))
        args.append(res)
    if epilogue == "swiglu_bwd":
        in_specs.append(pair_spec)
    elif epilogue == "rms_bwd":
        in_specs += [row_spec, vec_spec, row_spec]
    elif epilogue == "loss":
        in_specs.append(row_spec)
    args += extras
    if norm_g is not None:
        in_specs.append(pl.BlockSpec((1, D_MODEL), lambda i, j, kk: (0, 0)))
        args.append(norm_g)
    semantics = ("arbitrary",) * 3 if epilogue in ("rms_bwd", "loss") else ("parallel", "parallel", "arbitrary")
    out = _call(body, name, (m // tm, n // tn, nk), in_specs, out_specs, out_shape,
                [pltpu.VMEM((cm * cn, um, un), F32)], semantics, args, rider)
    return out if epilogue in ("swiglu", "rms_bwd", "loss") else out[0]


def _head_sums(v, ind):
    hi, lo = _split2(v)
    return _dot(hi, ind) + _dot(lo, ind)


def _head_spread(per_head, ind):
    hi, lo = _split2(per_head)
    return _dot_nt(hi, ind) + _dot_nt(lo, ind)


def _head_rstd(xv, ind):
    return _head_spread(lax.rsqrt(_head_sums(xv * xv, ind) * (1.0 / ATT_DH) + EPS), ind)


def _hn_bwd_math(xv, gv, ind, dyv, scale):
    rstd = _head_rstd(xv, ind)
    xh = xv * rstd
    dyn = dyv * scale
    dyg = dyn * gv
    dx = rstd * (dyg - xh * _head_spread(_head_sums(dyg * xh, ind) * (1.0 / ATT_DH), ind))
    return dx, jnp.sum(dyn * xh, axis=0, keepdims=True)


def _q_hnorm(x, g_tiled, bd, scale, name):
    t, d = x.shape
    tm = _pick(t, 512, 16)

    def body(x_ref, g_ref, bd_ref, o_ref):
        xv = x_ref[...]
        o_ref[...] = (xv * _head_rstd(xv, bd_ref[...]) * g_ref[...] * scale).astype(BF16)

    return pl.pallas_call(
        body, name=name, grid=(t // tm,),
        in_specs=[pl.BlockSpec((tm, d), lambda i: (i, 0)), pl.BlockSpec((1, d), lambda i: (0, 0)),
                  pl.BlockSpec((d, LANES), lambda i: (0, 0))],
        out_specs=pl.BlockSpec((tm, d), lambda i: (i, 0)),
        out_shape=jax.ShapeDtypeStruct((t, d), BF16),
        compiler_params=_params("parallel"),
    )(x, g_tiled, bd)


def _q_dhnorm(x, g_tiled, bd, dy, scale, name):
    t, d = x.shape
    tm = _pick(t, 512, 16)

    def body(x_ref, g_ref, bd_ref, dy_ref, dx_ref, dg_ref):
        dx, part = _hn_bwd_math(x_ref[...], g_ref[...], bd_ref[...], dy_ref[...], scale)
        dx_ref[...] = dx.astype(BF16)
        _accumulate(dg_ref, part, pl.program_id(0))

    row = pl.BlockSpec((tm, d), lambda i: (i, 0))
    vec = pl.BlockSpec((1, d), lambda i: (0, 0))
    return pl.pallas_call(
        body, name=name, grid=(t // tm,),
        in_specs=[row, vec, pl.BlockSpec((d, LANES), lambda i: (0, 0)), row],
        out_specs=[row, vec],
        out_shape=[jax.ShapeDtypeStruct((t, d), BF16), jax.ShapeDtypeStruct((1, d), F32)],
        compiler_params=_params("arbitrary"),
    )(x, g_tiled, bd, dy)


def _kv_prep(kv, g_tiled, bd, name):
    t = kv.shape[0]
    d = D_MODEL
    tm = K_PAD
    assert t % tm == 0

    def body(k_ref, v_ref, g_ref, bd_ref, kp_ref, vp_ref):
        i = pl.program_id(0)

        @pl.when(i == 0)
        def _():
            kp_ref[...] = jnp.zeros_like(kp_ref)
            vp_ref[...] = jnp.zeros_like(vp_ref)

        @pl.when(i > 0)
        def _():
            xv = k_ref[...]
            kp_ref[...] = (xv * _head_rstd(xv, bd_ref[...]) * g_ref[...]).astype(BF16)
            vp_ref[...] = v_ref[...].astype(BF16)

    shp = jax.ShapeDtypeStruct((t + K_PAD, d), BF16)
    out = pl.BlockSpec((tm, d), lambda i: (i, 0))
    return pl.pallas_call(
        body, name=name, grid=(t // tm + 1,),
        in_specs=[pl.BlockSpec((tm, d), lambda i: (jnp.maximum(i - 1, 0), 0)),
                  pl.BlockSpec((tm, d), lambda i: (jnp.maximum(i - 1, 0), 1)),
                  pl.BlockSpec((1, d), lambda i: (0, 0)), pl.BlockSpec((d, LANES), lambda i: (0, 0))],
        out_specs=[out, out], out_shape=[shp, shp],
        compiler_params=_params("arbitrary"),
    )(kv, kv, g_tiled, bd)


def _kv_dprep(kv, g_tiled, bd, dkp_t, dvp_t, name):
    t = kv.shape[0]
    d = D_MODEL
    tm = K_PAD

    def body(k_ref, g_ref, bd_ref, dk_ref, dv_ref, o_ref, dg_ref):
        dx, part = _hn_bwd_math(k_ref[...], g_ref[...], bd_ref[...], dk_ref[...].T, 1.0)
        o_ref[:, :d] = dx.astype(BF16)
        o_ref[:, d:] = dv_ref[...].T.astype(BF16)
        _accumulate(dg_ref, part, pl.program_id(0))

    vec = pl.BlockSpec((1, d), lambda i: (0, 0))
    padded = pl.BlockSpec((d, tm), lambda i: (0, i + 1))
    return pl.pallas_call(
        body, name=name, grid=(t // tm,),
        in_specs=[pl.BlockSpec((tm, d), lambda i: (i, 0)), vec, pl.BlockSpec((d, LANES), lambda i: (0, 0)),
                  padded, padded],
        out_specs=[pl.BlockSpec((tm, 2 * d), lambda i: (i, 0)), vec],
        out_shape=[jax.ShapeDtypeStruct((t, 2 * d), BF16), jax.ShapeDtypeStruct((1, d), F32)],
        compiler_params=_params("arbitrary"),
    )(kv, g_tiled, bd, dkp_t, dvp_t)


def _ret_consts(t):
    h = np.arange(RET_HEADS, dtype=np.float32)
    lg = np.log(np.float32(1.0) - np.float32(2.0) ** (np.float32(-5.0) - h)).astype(np.float32)
    tt = np.arange(CHUNK, dtype=np.float32)
    intra = np.exp(lg[:, None, None] * np.abs(tt[:, None] - tt[None, :])).astype(np.float32)
    q_dec = np.exp(lg[:, None] * (tt + 1.0)).astype(np.float32)
    k_dec = np.exp(lg[:, None] * (CHUNK - 1.0 - tt)).astype(np.float32)
    s_dec = [float(v) for v in np.exp(lg * np.float32(CHUNK)).astype(np.float32)]
    qd = np.broadcast_to(q_dec[:, :, None], (RET_HEADS, CHUNK, RET_DK)).copy()
    kd = np.broadcast_to(k_dec[:, :, None], (RET_HEADS, CHUNK, RET_DK)).copy()
    half = RET_DK // 2
    inv_freq = ROPE_BASE ** (-jnp.arange(half, dtype=F32) / half)
    ang = jnp.arange(t).astype(F32)[:, None] * inv_freq[None, :]
    return jnp.asarray(intra), jnp.asarray(qd), jnp.asarray(kd), s_dec, jnp.cos(ang), jnp.sin(ang)


def _rope(x, cos, sin):
    half = RET_DK // 2
    x1, x2 = x[:, :half], x[:, half:]
    return jnp.concatenate([x1 * cos - x2 * sin, x1 * sin + x2 * cos], axis=-1)


def _unrope(d, cos, sin):
    half = RET_DK // 2
    d1, d2 = d[:, :half], d[:, half:]
    return jnp.concatenate([d1 * cos + d2 * sin, d2 * cos - d1 * sin], axis=-1)


def _ret_slices(h):
    q = slice(h * RET_DK, (h + 1) * RET_DK)
    k = slice(RET_Q_COLS + h * RET_DK, RET_Q_COLS + (h + 1) * RET_DK)
    v = slice(2 * RET_Q_COLS + h * RET_DV, 2 * RET_Q_COLS + (h + 1) * RET_DV)
    g = slice(2 * RET_Q_COLS + RET_V_COLS + h * RET_DV, 2 * RET_Q_COLS + RET_V_COLS + (h + 1) * RET_DV)
    o = slice(h * RET_DV, (h + 1) * RET_DV)
    return q, k, v, g, o


def _ret_fwd(proj, gn, consts, name, rider=None):
    t, cols = proj.shape
    n = t // CHUNK
    intra, qd, kd, s_dec, cos, sin = consts
    k_scale = RET_DK ** -0.5

    def body(p_ref, cos_ref, sin_ref, intra_ref, qd_ref, kd_ref, gn_ref, y_ref, o_ref, st_ref, state):
        i = pl.program_id(0)

        @pl.when(i == 0)
        def _():
            state[...] = jnp.zeros_like(state)

        for c in range(RET_STEP):
            rows = slice(c * CHUNK, (c + 1) * CHUNK)
            cosv, sinv = cos_ref[rows, :], sin_ref[rows, :]
            for h in range(RET_HEADS):
                qs, ks, vs, gs, os_ = _ret_slices(h)
                qr = _rope(p_ref[rows, qs], cosv, sinv)
                kr = _rope(p_ref[rows, ks], cosv, sinv) * k_scale
                vb = p_ref[rows, vs].astype(BF16)
                gv = p_ref[rows, gs]
                scores = _dot_nt(qr.astype(BF16), kr.astype(BF16)) * intra_ref[h]
                s_old = state[h]
                s_old_b = s_old.astype(BF16)
                st_ref[c, h] = s_old_b
                o = _dot(scores.astype(BF16), vb) + _dot((qr * qd_ref[h]).astype(BF16), s_old_b)
                state[h] = s_old * s_dec[h] + _dot_tn((kr * kd_ref[h]).astype(BF16), vb)
                rstd = lax.rsqrt(jnp.mean(o * o, axis=-1, keepdims=True) + EPS)
                on = o * rstd * gn_ref[:, os_]
                o_ref[rows, os_] = o
                y_ref[rows, os_] = (gv * _sigmoid(gv) * on).astype(BF16)

    full3 = lambda a: pl.BlockSpec(a.shape, lambda i: (0, 0, 0))
    step = RET_STEP * CHUNK
    return _call(
        body, name, (n // RET_STEP,),
        [pl.BlockSpec((step, cols), lambda i: (i, 0)),
         pl.BlockSpec((step, RET_DK // 2), lambda i: (i, 0)),
         pl.BlockSpec((step, RET_DK // 2), lambda i: (i, 0)),
         full3(intra), full3(qd), full3(kd),
         pl.BlockSpec((1, RET_V_COLS), lambda i: (0, 0))],
        [pl.BlockSpec((step, RET_V_COLS), lambda i: (i, 0)),
         pl.BlockSpec((step, RET_V_COLS), lambda i: (i, 0)),
         pl.BlockSpec((RET_STEP, RET_HEADS, RET_DK, RET_DV), lambda i: (i, 0, 0, 0))],
        [jax.ShapeDtypeStruct((t, RET_V_COLS), BF16),
         jax.ShapeDtypeStruct((t, RET_V_COLS), F32),
         jax.ShapeDtypeStruct((n, RET_HEADS, RET_DK, RET_DV), BF16)],
        [pltpu.VMEM((RET_HEADS, RET_DK, RET_DV), F32)], ("arbitrary",),
        (proj, cos, sin, intra, qd, kd, gn), rider)


def _ret_bwd(proj, gn, o_saved, states, dy, consts, name, rider=None):
    t, cols = proj.shape
    n = t // CHUNK
    intra, qd, kd, s_dec, cos, sin = consts
    k_scale = RET_DK ** -0.5

    def body(p_ref, cos_ref, sin_ref, intra_ref, qd_ref, kd_ref, gn_ref, o_ref, st_ref, dy_ref,
             dp_ref, dgn_ref, dstate):
        i = pl.program_id(0)

        @pl.when(i == 0)
        def _():
            dstate[...] = jnp.zeros_like(dstate)

        dgn = None
        for c in reversed(range(RET_STEP)):
            rows = slice(c * CHUNK, (c + 1) * CHUNK)
            cosv, sinv = cos_ref[rows, :], sin_ref[rows, :]
            dgn_parts = []
            for h in range(RET_HEADS):
                qs, ks, vs, gs, os_ = _ret_slices(h)
                qr = _rope(p_ref[rows, qs], cosv, sinv)
                kr = _rope(p_ref[rows, ks], cosv, sinv) * k_scale
                qb, kb = qr.astype(BF16), kr.astype(BF16)
                vb = p_ref[rows, vs].astype(BF16)
                gv = p_ref[rows, gs]
                ov = o_ref[rows, os_]
                dyv = dy_ref[rows, os_]
                gnv = gn_ref[:, os_]
                sg = _sigmoid(gv)
                rstd = lax.rsqrt(jnp.mean(ov * ov, axis=-1, keepdims=True) + EPS)
                oh = ov * rstd
                d_on = dyv * (gv * sg)
                dg = dyv * (oh * gnv) * (sg * (1.0 + gv * (1.0 - sg)))
                dgn_parts.append(jnp.sum(d_on * oh, axis=0, keepdims=True))
                d_oh = d_on * gnv
                do = rstd * (d_oh - oh * jnp.mean(d_oh * oh, axis=-1, keepdims=True))
                dob = do.astype(BF16)
                mask = intra_ref[h]
                a_b = (_dot_nt(qb, kb) * mask).astype(BF16)
                da_b = (_dot_nt(dob, vb) * mask).astype(BF16)
                ds_new = dstate[h]
                ds_new_b = ds_new.astype(BF16)
                s_old_b = st_ref[c, h]
                qdv, kdv = qd_ref[h], kd_ref[h]
                dv = _dot_tn(a_b, dob) + _dot((kr * kdv).astype(BF16), ds_new_b)
                dqr = _dot(da_b, kb) + _dot_nt(dob, s_old_b) * qdv
                dkr = _dot_tn(da_b, qb) + _dot_nt(vb, ds_new_b) * kdv
                dstate[h] = ds_new * s_dec[h] + _dot_tn((qr * qdv).astype(BF16), dob)
                dp_ref[rows, qs] = _unrope(dqr, cosv, sinv).astype(BF16)
                dp_ref[rows, ks] = _unrope(dkr * k_scale, cosv, sinv).astype(BF16)
                dp_ref[rows, vs] = dv.astype(BF16)
                dp_ref[rows, gs] = dg.astype(BF16)
            part = jnp.concatenate(dgn_parts, axis=-1)
            dgn = part if dgn is None else dgn + part
        _accumulate(dgn_ref, dgn, i)

    steps = n // RET_STEP
    step = RET_STEP * CHUNK
    rev = lambda i: (steps - 1 - i, 0)
    full3 = lambda a: pl.BlockSpec(a.shape, lambda i: (0, 0, 0))
    return _call(
        body, name, (steps,),
        [pl.BlockSpec((step, cols), rev),
         pl.BlockSpec((step, RET_DK // 2), rev),
         pl.BlockSpec((step, RET_DK // 2), rev),
         full3(intra), full3(qd), full3(kd),
         pl.BlockSpec((1, RET_V_COLS), lambda i: (0, 0)),
         pl.BlockSpec((step, RET_V_COLS), rev),
         pl.BlockSpec((RET_STEP, RET_HEADS, RET_DK, RET_DV), lambda i: (steps - 1 - i, 0, 0, 0)),
         pl.BlockSpec((step, RET_V_COLS), rev)],
        [pl.BlockSpec((step, cols), rev),
         pl.BlockSpec((1, RET_V_COLS), lambda i: (0, 0))],
        [jax.ShapeDtypeStruct((t, cols), BF16),
         jax.ShapeDtypeStruct((1, RET_V_COLS), F32)],
        [pltpu.VMEM((RET_HEADS, RET_DK, RET_DV), F32)], ("arbitrary",),
        (proj, cos, sin, intra, qd, kd, gn, o_saved, states, dy), rider)


def _att_common(q_ref, kp_ref, vp_ref, sub):
    blk = pl.program_id(1) * ATT_SUBS + sub
    start = pl.multiple_of(blk * Q_BLOCK, Q_BLOCK)
    kw = kp_ref[pl.ds(start, K_WINDOW), :]
    vw = vp_ref[pl.ds(start, K_WINDOW), :]
    kvalid = blk * Q_BLOCK - K_PAD + lax.broadcasted_iota(jnp.int32, (1, K_WINDOW), 1) >= 0
    lane = lax.broadcasted_iota(jnp.int32, (1, LANES), 1)
    qrows = slice(sub * Q_BLOCK, (sub + 1) * Q_BLOCK)
    return start, qrows, q_ref[qrows, :], kw, vw, kvalid, (lane < ATT_DH, lane >= ATT_DH)


def _row_groups():
    return [slice(r * ATT_ROWS, (r + 1) * ATT_ROWS) for r in range(Q_BLOCK // ATT_ROWS)]


def _lane_copies(x):
    return jnp.tile(x, (1, K_WINDOW // LANES))


def _att_specs(t, tp):
    qspec = pl.BlockSpec((ATT_SUBS * Q_BLOCK, LANES), lambda h, i: (i, h))
    kspec = pl.BlockSpec((tp, LANES), lambda h, i: (0, h))
    bspec = pl.BlockSpec((2, Q_BLOCK, K_WINDOW), lambda h, i: (h, 0, 0))
    return qspec, kspec, bspec


def _att_fwd(q, kp, vp, bias, name, rider=None):
    t, d = q.shape
    tp = kp.shape[0]

    def body(q_ref, kp_ref, vp_ref, bias_ref, o_ref, lse_ref, s_scr, p_scr, lse_scr):
        for sub in range(ATT_SUBS):
            _, qrows, q2, kw, vw, kvalid, sel = _att_common(q_ref, kp_ref, vp_ref, sub)
            for hh in range(2):
                s_scr[sub, hh] = _dot_nt(jnp.where(sel[hh], q2, 0), kw)
            for hh in range(2):
                for rows in _row_groups():
                    s = jnp.where(kvalid, s_scr[sub, hh, rows, :] + bias_ref[hh, rows, :], NEG)
                    m = jnp.max(s, axis=-1, keepdims=True)
                    e = jnp.exp(s - m)
                    l = jnp.sum(e, axis=-1, keepdims=True)
                    p_scr[sub, hh, rows, :] = (e * (1.0 / l)).astype(BF16)
                    lse_scr[sub, hh, rows, :] = jnp.broadcast_to(m + jnp.log(l), (ATT_ROWS, LANES))
            outs = [_dot(p_scr[sub, hh], vw) for hh in range(2)]
            o_ref[qrows, :] = jnp.where(sel[0], outs[0], outs[1]).astype(BF16)
            lse_ref[qrows, :] = jnp.where(sel[0], lse_scr[sub, 0], lse_scr[sub, 1])

    qspec, kspec, bspec = _att_specs(t, tp)
    return _call(body, name, (d // LANES, t // (ATT_SUBS * Q_BLOCK)), [qspec, kspec, kspec, bspec], [qspec, qspec],
                 [jax.ShapeDtypeStruct((t, d), BF16), jax.ShapeDtypeStruct((t, d), F32)],
                 [pltpu.VMEM((ATT_SUBS, 2, Q_BLOCK, K_WINDOW), F32),
                  pltpu.VMEM((ATT_SUBS, 2, Q_BLOCK, K_WINDOW), BF16),
                  pltpu.VMEM((ATT_SUBS, 2, Q_BLOCK, LANES), F32)],
                 ("parallel", "arbitrary"), (q, kp, vp, bias), rider)


def _att_bwd(q, kp, vp, bias, do, o, lse, name, rider=None):
    t, d = q.shape
    tp = kp.shape[0]

    def body(q_ref, kp_ref, vp_ref, bias_ref, do_ref, o_ref, lse_ref, dq_ref, dkp_ref, dvp_ref, db_ref,
             s_scr, dp_scr, p_scr, ds_scr, row_scr):
        @pl.when(pl.program_id(1) == 0)
        def _():
            dkp_ref[...] = jnp.zeros_like(dkp_ref)
            dvp_ref[...] = jnp.zeros_like(dvp_ref)
            db_ref[...] = jnp.zeros_like(db_ref)

        for sub in range(ATT_SUBS):
            start, qrows, q2, kw, vw, kvalid, sel = _att_common(q_ref, kp_ref, vp_ref, sub)
            do2 = do_ref[qrows, :]
            qm = [jnp.where(sel[hh], q2, 0) for hh in range(2)]
            dom = [jnp.where(sel[hh], do2, 0) for hh in range(2)]
            do_o = do2.astype(F32) * o_ref[qrows, :].astype(F32)
            lse2 = lse_ref[qrows, :]
            for hh in range(2):
                s_scr[sub, hh] = _dot_nt(qm[hh], kw)
                dp_scr[sub, hh] = _dot_nt(dom[hh], vw)
                lse_h = jnp.max(jnp.where(sel[hh], lse2, NEG), axis=-1, keepdims=True)
                delta = jnp.sum(jnp.where(sel[hh], do_o, 0.0), axis=-1, keepdims=True)
                row_scr[sub, hh, 0] = jnp.broadcast_to(lse_h, (Q_BLOCK, LANES))
                row_scr[sub, hh, 1] = jnp.broadcast_to(delta, (Q_BLOCK, LANES))
            for hh in range(2):
                for rows in _row_groups():
                    s = jnp.where(kvalid, s_scr[sub, hh, rows, :] + bias_ref[hh, rows, :], NEG)
                    p = jnp.exp(s - _lane_copies(row_scr[sub, hh, 0, rows, :]))
                    ds = p * (dp_scr[sub, hh, rows, :] - _lane_copies(row_scr[sub, hh, 1, rows, :]))
                    db_ref[hh, rows, :] += ds
                    p_scr[sub, hh, rows, :] = p.astype(BF16)
                    ds_scr[sub, hh, rows, :] = ds.astype(BF16)
            dqs = [_dot(ds_scr[sub, hh], kw) for hh in range(2)]
            dq_ref[qrows, :] = jnp.where(sel[0], dqs[0], dqs[1])
            dkp_ref[:, pl.ds(start, K_WINDOW)] += (_dot_tn(qm[0], ds_scr[sub, 0]) +
                                                   _dot_tn(qm[1], ds_scr[sub, 1]))
            dvp_ref[:, pl.ds(start, K_WINDOW)] += (_dot_tn(dom[0], p_scr[sub, 0]) +
                                                   _dot_tn(dom[1], p_scr[sub, 1]))

    qspec, kspec, bspec = _att_specs(t, tp)
    tspec = pl.BlockSpec((LANES, tp), lambda h, i: (h, 0))
    stage = lambda dt: pltpu.VMEM((ATT_SUBS, 2, Q_BLOCK, K_WINDOW), dt)
    return _call(body, name, (d // LANES, t // (ATT_SUBS * Q_BLOCK)),
                 [qspec, kspec, kspec, bspec, qspec, qspec, qspec],
                 [qspec, tspec, tspec, bspec],
                 [jax.ShapeDtypeStruct((t, d), F32),
                  jax.ShapeDtypeStruct((d, tp), F32),
                  jax.ShapeDtypeStruct((d, tp), F32),
                  jax.ShapeDtypeStruct((ATT_HEADS, Q_BLOCK, K_WINDOW), F32)],
                 [stage(F32), stage(F32), stage(BF16), stage(BF16),
                  pltpu.VMEM((ATT_SUBS, 2, 2, Q_BLOCK, LANES), F32)],
                 ("parallel", "arbitrary"), (q, kp, vp, bias, do, o, lse), rider)


def _rel_bin_matrix():
    rows = REL_DELTAS * 2 * REL_BLK
    rho = lax.broadcasted_iota(jnp.int32, (rows, REL_PAD), 0)
    col = lax.broadcasted_iota(jnp.int32, (rows, REL_PAD), 1)
    assert 2 * REL_BLK == 256
    delta = rho >> 8
    c = 255 - (rho & 255)
    dist = K_PAD + REL_BLK * (delta - (K_WINDOW // REL_BLK - 1)) + (c - (REL_BLK - 1))
    idx = jnp.clip(dist, -REL_CLIP, REL_CLIP) + REL_CLIP
    return col == idx


def _rel_expand(rel_pad, name):
    heads = rel_pad.shape[0]
    rows = REL_DELTAS * 2 * REL_BLK

    def body_bin(r_ref, o_ref):
        onehot = jnp.where(_rel_bin_matrix(), 1.0, 0.0).astype(BF16)
        hi, mid, lo = _split3(r_ref[...])
        o_ref[...] = _dot_nt(hi, onehot) + _dot_nt(mid, onehot) + _dot_nt(lo, onehot)

    by_delta = pl.pallas_call(
        body_bin, name=name + "_bin",
        out_shape=jax.ShapeDtypeStruct((heads, rows), F32),
        compiler_params=pltpu.CompilerParams(vmem_limit_bytes=VMEM_LIMIT_V7X),
    )(rel_pad)
    by_delta = by_delta.reshape(heads * REL_DELTAS, 2 * REL_BLK)

    def body_shift(t_ref, o_ref):
        tv = t_ref[...]
        for r in range(REL_BLK):
            o_ref[r] = pltpu.roll(tv, (r + REL_BLK) % (2 * REL_BLK), 1)[:, :REL_BLK]

    return pl.pallas_call(
        body_shift, name=name + "_shift",
        out_shape=jax.ShapeDtypeStruct((REL_BLK, heads * REL_DELTAS, REL_BLK), F32),
        compiler_params=pltpu.CompilerParams(vmem_limit_bytes=VMEM_LIMIT_V7X),
    )(by_delta)


def _bias_table(rel_bias, name):
    heads = rel_bias.shape[0]
    rel_pad = jnp.pad(rel_bias, ((0, 0), (0, REL_PAD - REL_TABLE)))
    tiles = _rel_expand(rel_pad, name)
    tiles = tiles.reshape(REL_BLK, heads, REL_DELTAS, REL_BLK).transpose(1, 2, 0, 3)
    na, nb = Q_BLOCK // REL_BLK, K_WINDOW // REL_BLK
    rows = [jnp.concatenate([tiles[:, a - b + nb - 1] for b in range(nb)], axis=-1) for a in range(na)]
    table = jnp.concatenate(rows, axis=-2)
    qc = np.arange(Q_BLOCK)[:, None] // CHUNK
    kc = np.arange(K_WINDOW)[None, :] // CHUNK
    band = (kc >= qc) & (kc <= qc + PAST_CHUNKS)
    return jnp.where(jnp.asarray(band)[None], table, NEG)


def _rel_reduce(db, name):
    heads = db.shape[0]
    na, nb = Q_BLOCK // REL_BLK, K_WINDOW // REL_BLK

    fold_heads = 4

    def body_fold(db_ref, g_ref):
        for hd in range(fold_heads):
            for delta in range(REL_DELTAS):
                acc = None
                for a in range(na):
                    b = a - (delta - (nb - 1))
                    if 0 <= b < nb:
                        tile = db_ref[hd, a * REL_BLK:(a + 1) * REL_BLK, b * REL_BLK:(b + 1) * REL_BLK]
                        acc = tile if acc is None else acc + tile
                g_ref[hd, delta] = acc

    folded = pl.pallas_call(
        body_fold, name=name + "_fold", grid=(heads // fold_heads,),
        in_specs=[pl.BlockSpec((fold_heads, Q_BLOCK, K_WINDOW), lambda h: (h, 0, 0))],
        out_specs=pl.BlockSpec((fold_heads, REL_DELTAS, REL_BLK, REL_BLK), lambda h: (h, 0, 0, 0)),
        out_shape=jax.ShapeDtypeStruct((heads, REL_DELTAS, REL_BLK, REL_BLK), F32),
        compiler_params=_params("parallel"),
    )(db)
    by_row = folded.transpose(2, 0, 1, 3).reshape(REL_BLK, heads * REL_DELTAS, REL_BLK)

    def body_diag(g_ref, d_ref):
        zeros = jnp.zeros((heads * REL_DELTAS, REL_BLK), F32)
        acc = None
        for r in range(REL_BLK):
            part = pltpu.roll(jnp.concatenate([g_ref[r], zeros], axis=1), REL_BLK - r, 1)
            acc = part if acc is None else acc + part
        d_ref[...] = acc

    diag = pl.pallas_call(
        body_diag, name=name + "_diag",
        out_shape=jax.ShapeDtypeStruct((heads * REL_DELTAS, 2 * REL_BLK), F32),
        compiler_params=pltpu.CompilerParams(vmem_limit_bytes=VMEM_LIMIT_V7X),
    )(by_row)
    diag = diag.reshape(heads, REL_DELTAS * 2 * REL_BLK)

    def body_bin(d_ref, o_ref):
        onehot = jnp.where(_rel_bin_matrix(), 1.0, 0.0).astype(BF16)
        hi, mid, lo = _split3(d_ref[...])
        o_ref[...] = _dot(hi, onehot) + _dot(mid, onehot) + _dot(lo, onehot)

    out = pl.pallas_call(
        body_bin, name=name + "_bin",
        out_shape=jax.ShapeDtypeStruct((heads, REL_PAD), F32),
        compiler_params=pltpu.CompilerParams(vmem_limit_bytes=VMEM_LIMIT_V7X),
    )(diag)
    return out[:, :REL_TABLE]


def _sum_leading(x, name):
    n, r, c = x.shape
    tr = _pick(r, 256, 8)

    def body(x_ref, o_ref):
        acc = x_ref[0].astype(F32)
        for k in range(1, n):
            acc = acc + x_ref[k].astype(F32)
        o_ref[...] = acc

    return pl.pallas_call(
        body, name=name, grid=(r // tr,),
        in_specs=[pl.BlockSpec((n, tr, c), lambda i: (0, i, 0))],
        out_specs=pl.BlockSpec((tr, c), lambda i: (i, 0)),
        out_shape=jax.ShapeDtypeStruct((r, c), F32),
        compiler_params=_params("parallel"),
    )(x)


def _pair_add(g, recv, parity, name):
    _, r, c = g.shape
    tr = _pick(r, 256, 16)

    def body(par_ref, g_ref, r_ref, o_ref):
        o_ref[...] = (g_ref[...].astype(F32) + r_ref[...].astype(F32)).astype(BF16)

    return pl.pallas_call(
        body, name=name,
        grid_spec=pltpu.PrefetchScalarGridSpec(
            num_scalar_prefetch=1, grid=(4, r // tr),
            in_specs=[pl.BlockSpec((1, tr, c), lambda k, i, par: (2 * k + par[0], i, 0)),
                      pl.BlockSpec((1, tr, c), lambda k, i, par: (k, i, 0))],
            out_specs=pl.BlockSpec((1, tr, c), lambda k, i, par: (k, i, 0))),
        out_shape=jax.ShapeDtypeStruct((4, r, c), BF16),
        compiler_params=_params("parallel", "parallel"),
    )(parity, g, recv)


def _adamw(w, g_parts, m, v, name):
    r, c = w.shape
    n = g_parts.shape[0]
    tr = _pick(r, 256, 16 if g_parts.dtype == BF16 else 8)
    c1 = 1.0 - ADAM_B1 ** ADAM_STEP
    c2 = 1.0 - ADAM_B2 ** ADAM_STEP

    def body(w_ref, g_ref, m_ref, v_ref, go_ref, d_ref, nm_ref, nv_ref):
        gv = g_ref[0].astype(F32)
        for k in range(1, n):
            gv = gv + g_ref[k].astype(F32)
        nm = ADAM_B1 * m_ref[...] + (1.0 - ADAM_B1) * gv
        nv = ADAM_B2 * v_ref[...] + (1.0 - ADAM_B2) * (gv * gv)
        go_ref[...] = gv
        d_ref[...] = -ADAM_LR * ((nm / c1) / (jnp.sqrt(nv / c2) + ADAM_EPS) + ADAM_WD * w_ref[...])
        nm_ref[...] = nm
        nv_ref[...] = nv

    spec = pl.BlockSpec((tr, c), lambda i: (i, 0))
    shp = jax.ShapeDtypeStruct((r, c), F32)
    return pl.pallas_call(
        body, name=name, grid=(r // tr,),
        in_specs=[spec, pl.BlockSpec((n, tr, c), lambda i: (0, i, 0)), spec, spec],
        out_specs=[spec] * 4, out_shape=[shp] * 4,
        compiler_params=_params("parallel"),
    )(w, g_parts, m, v)


BIG = (("a_w_in", 1), ("a_w_o", 0), ("a_w_gu", 0), ("a_w_down", 0), ("w_kv", 1),
       ("b_w_q", 0), ("b_w_o", 0), ("b_w_gu", 0), ("b_w_down", 0))
TRANSPOSED = ("a_w_gu", "b_w_gu")
FFN_BLK = 2 * FFN_HIDDEN // N_DEV

SMALL = (("a_norm_g", D_MODEL, True), ("a_gn_g", RET_V_COLS, True), ("a_ffn_norm_g", D_MODEL, True),
         ("kv_norm_g", D_MODEL, False), ("b_norm_g", D_MODEL, False), ("b_ffn_norm_g", D_MODEL, False),
         ("k_norm_g", ATT_DH, False), ("b_q_norm_g", ATT_DH, False),
         ("b_rel_bias", ATT_HEADS * REL_TABLE, False))
SMALL_ROWS, SMALL_COLS = 16, 1024


def _pack_small(vals, last=None):
    flat = jnp.concatenate([vals[n].reshape(-1) for n, _, _ in SMALL])
    room = SMALL_ROWS * SMALL_COLS - flat.shape[0]
    if last is None:
        flat = jnp.pad(flat, (0, room))
    else:
        flat = jnp.concatenate([jnp.pad(flat, (0, room - 1)), last.reshape(1)])
    return flat.reshape(SMALL_ROWS, SMALL_COLS)


def _unpack_small(packed, local):
    flat, out, pos = packed.reshape(-1), {}, 0
    for n, length, sharded in SMALL:
        ln = length // N_DEV if (local and sharded) else length
        out[n] = flat[pos:pos + ln]
        pos += ln
    return out


def _gather_rider(shards, names):
    return _GatherRider([shards[n] for n in names])


def _gathered(rider, names, axis_of):
    return {n: (r.reshape(-1, r.shape[2]) if axis_of[n] == 0 else r) for n, r in zip(names, rider.results)}


def _blocks(g):
    return g if g.ndim == 3 else g.reshape(N_DEV, -1, g.shape[-1])


def _local_step(x, target, shards, w_in, s, parity):
    t = x.shape[0]
    axis_of = dict(BIG)
    consts = _ret_consts(t)
    lane_to_head = np.zeros((D_MODEL, LANES), np.float32)
    lane_to_head[np.arange(D_MODEL), np.arange(D_MODEL) // ATT_DH] = 1.0
    bd = jnp.asarray(lane_to_head).astype(BF16)
    kg_t = jnp.tile(s["k_norm_g"], (1, ATT_HEADS))
    qg_t = jnp.tile(s["b_q_norm_g"], (1, ATT_HEADS))
    q_scale = ATT_DH ** -0.5
    w = {"a_w_in": w_in}
    g, recv = {}, {}

    def gather_on(names):
        return _gather_rider(shards, names), names

    def landed(ride):
        w.update(_gathered(ride[0], ride[1], axis_of))

    def scatter_on(names):
        return _ScatterRider([_blocks(g[n]) for n in names]), names

    def reduced(ride):
        recv.update(zip(ride[1], ride[0].results))

    ride = gather_on(["a_w_o", "a_w_down"])
    proj = _mm(x, w["a_w_in"], "nn", "a_proj", norm_g=s["a_norm_g"], rider=ride[0])
    landed(ride)
    ride = gather_on(["a_w_gu", "w_kv"])
    y, o_ret, states = _ret_fwd(proj, s["a_gn_g"], consts, "a_ret", rider=ride[0])
    landed(ride)
    x1 = _mm(y, w["a_w_o"], "nn", "a_out", res=x)
    ride = gather_on(["b_w_q", "b_w_o"])
    gu_a, act_a = _mm(x1, w["a_w_gu"], "nt", "a_ffn_gu", epilogue="swiglu", out_block=FFN_BLK,
                      norm_g=s["a_ffn_norm_g"], rider=ride[0])
    landed(ride)
    x2 = _mm(act_a, w["a_w_down"], "nn", "a_ffn_down", res=x1)

    kv = _mm(x2, w["w_kv"], "nn", "kv_proj", norm_g=s["kv_norm_g"])
    kp, vp = _kv_prep(kv, kg_t, bd, "kv_prep")

    q_raw = _mm(x2, w["b_w_q"], "nn", "b_q", norm_g=s["b_norm_g"])
    qn = _q_hnorm(q_raw, qg_t, bd, q_scale, "q_hnorm")
    bias = _bias_table(s["b_rel_bias"].reshape(ATT_HEADS, REL_TABLE), "rel")
    ride = gather_on(["b_w_gu", "b_w_down"])
    o_att, lse = _att_fwd(qn, kp, vp, bias, "b_att", rider=ride[0])
    landed(ride)
    x3 = _mm(o_att, w["b_w_o"], "nn", "b_out", res=x2)
    gu_b, act_b = _mm(x3, w["b_w_gu"], "nt", "b_ffn_gu", epilogue="swiglu", out_block=FFN_BLK,
                      norm_g=s["b_ffn_norm_g"])
    dy, loss = _mm(act_b, w["b_w_down"], "nn", "b_ffn_down", res=x3, epilogue="loss", extra=(target,))
    in_blk, kv_blk, ffn_blk = w["a_w_in"].shape[2], w["w_kv"].shape[2], FFN_BLK

    dgu = _mm(dy, w["b_w_down"], "nt", "b_ffn_dgu", out_block=ffn_blk, epilogue="swiglu_bwd", extra=gu_b)
    dgu = dgu.reshape(N_DEV, t, ffn_blk)
    g["b_w_down"] = _mm(act_b, dy, "tn", "b_ffn_gdown", out_dtype=BF16)
    ride = scatter_on(["b_w_down"])
    dx3, g["b_ffn_norm_g"] = _mm(dgu, w["b_w_gu"], "nn", "b_ffn_dh", epilogue="rms_bwd",
                                 extra=(x3, s["b_ffn_norm_g"], dy), rider=ride[0])
    reduced(ride)
    g["b_w_gu"] = _mm(dgu, x3, "tn", "b_ffn_ggu", out_dtype=BF16, norm_g=s["b_ffn_norm_g"], norm_b=True)

    do_att = _mm(dx3, w["b_w_o"], "nt", "b_dout", out_dtype=BF16)
    g["b_w_o"] = _mm(o_att, dx3, "tn", "b_gout", out_dtype=BF16)
    ride = scatter_on(["b_w_gu", "b_w_o"])
    dq, dkp, dvp, db = _att_bwd(qn, kp, vp, bias, do_att, o_att, lse, "b_datt", rider=ride[0])
    reduced(ride)
    g["b_rel_bias"] = _rel_reduce(db, "drel").reshape(1, -1)
    dq_raw, gq = _q_dhnorm(q_raw, qg_t, bd, dq, q_scale, "q_dhnorm")
    g["b_q_norm_g"] = gq.reshape(ATT_HEADS, ATT_DH).sum(axis=0, keepdims=True)
    g["b_w_q"] = _mm(x2, dq_raw, "tn", "b_gq", out_dtype=BF16, norm_g=s["b_norm_g"])
    dx2, g["b_norm_g"] = _mm(dq_raw, w["b_w_q"], "nt", "b_dq", epilogue="rms_bwd",
                             extra=(x2, s["b_norm_g"], dx3))

    dkv, gk = _kv_dprep(kv, kg_t, bd, dkp, dvp, "kv_dprep")
    g["k_norm_g"] = gk.reshape(ATT_HEADS, ATT_DH).sum(axis=0, keepdims=True)
    g["w_kv"] = _mm(x2, dkv, "tn", "kv_g", out_dtype=BF16, out_block=kv_blk, norm_g=s["kv_norm_g"])
    dx2, g["kv_norm_g"] = _mm(dkv, w["w_kv"], "nt", "kv_du", epilogue="rms_bwd",
                              extra=(x2, s["kv_norm_g"], dx2))

    ride = scatter_on(["b_w_q"])
    dgu = _mm(dx2, w["a_w_down"], "nt", "a_ffn_dgu", out_block=ffn_blk, epilogue="swiglu_bwd", extra=gu_a,
              rider=ride[0])
    reduced(ride)
    dgu = dgu.reshape(N_DEV, t, ffn_blk)
    g["a_w_down"] = _mm(act_a, dx2, "tn", "a_ffn_gdown", out_dtype=BF16)
    ride = scatter_on(["a_w_down"])
    dx1, g["a_ffn_norm_g"] = _mm(dgu, w["a_w_gu"], "nn", "a_ffn_dh", epilogue="rms_bwd",
                                 extra=(x1, s["a_ffn_norm_g"], dx2), rider=ride[0])
    reduced(ride)
    ride = scatter_on(["w_kv"])
    g["a_w_gu"] = _mm(dgu, x1, "tn", "a_ffn_ggu", out_dtype=BF16, norm_g=s["a_ffn_norm_g"], norm_b=True,
                      rider=ride[0])
    reduced(ride)

    dy_ret = _mm(dx1, w["a_w_o"], "nt", "a_dout")
    g["a_w_o"] = _mm(y, dx1, "tn", "a_gout", out_dtype=BF16)
    ride = scatter_on(["a_w_gu"])
    dproj, g["a_gn_g"] = _ret_bwd(proj, s["a_gn_g"], o_ret, states, dy_ret, consts, "a_dret", rider=ride[0])
    reduced(ride)
    ride = scatter_on(["a_w_o"])
    g["a_w_in"] = _mm(x, dproj, "tn", "a_gin", out_dtype=BF16, out_block=in_blk, norm_g=s["a_norm_g"],
                      rider=ride[0])
    reduced(ride)
    from_sibling = _exchange(_SiblingSwapRider([g["a_w_in"]]), "rs_sibling")[0]
    chip_sums = _pair_add(g["a_w_in"], from_sibling, parity, "rs_pair_add")
    last = _ChipScatterRider([chip_sums])
    grad_x, g["a_norm_g"] = _mm(dproj, w["a_w_in"], "nt", "a_dproj", epilogue="rms_bwd",
                                extra=(x, s["a_norm_g"], dx1), rider=last)
    recv["a_w_in"] = last.results[0]
    return loss, grad_x, recv, g


ARG_NAMES = ("x", "a_norm_g", "a_w_in", "a_gn_g", "a_w_o", "a_ffn_norm_g", "a_w_gu", "a_w_down",
             "kv_norm_g", "w_kv", "k_norm_g", "b_norm_g", "b_w_q", "b_q_norm_g", "b_rel_bias", "b_w_o",
             "b_ffn_norm_g", "b_w_gu", "b_w_down")
WEIGHT_NAMES = ARG_NAMES[1:]


def _big_shard(a, name):
    a = a[0] if a.ndim == 3 else a
    return a.T if name in TRANSPOSED else a


def _as_given(a, name, shape):
    return (a.T if name in TRANSPOSED else a).reshape(shape)


def kernel(x, a_norm_g, a_w_in, a_gn_g, a_w_o, a_ffn_norm_g, a_w_gu, a_w_down, kv_norm_g, w_kv, k_norm_g, b_norm_g, b_w_q, b_q_norm_g, b_rel_bias, b_w_o, b_ffn_norm_g, b_w_gu, b_w_down, loss_target, m_a_norm_g, m_a_w_in, m_a_gn_g, m_a_w_o, m_a_ffn_norm_g, m_a_w_gu, m_a_w_down, m_kv_norm_g, m_w_kv, m_k_norm_g, m_b_norm_g, m_b_w_q, m_b_q_norm_g, m_b_rel_bias, m_b_w_o, m_b_ffn_norm_g, m_b_w_gu, m_b_w_down, v_a_norm_g, v_a_w_in, v_a_gn_g, v_a_w_o, v_a_ffn_norm_g, v_a_w_gu, v_a_w_down, v_kv_norm_g, v_w_kv, v_k_norm_g, v_b_norm_g, v_b_w_q, v_b_q_norm_g, v_b_rel_bias, v_b_w_o, v_b_ffn_norm_g, v_b_w_gu, v_b_w_down):
    args = (x, a_norm_g, a_w_in, a_gn_g, a_w_o, a_ffn_norm_g, a_w_gu, a_w_down, kv_norm_g, w_kv, k_norm_g,
            b_norm_g, b_w_q, b_q_norm_g, b_rel_bias, b_w_o, b_ffn_norm_g, b_w_gu, b_w_down)
    p = dict(zip(ARG_NAMES, args))
    m_all = dict(zip(WEIGHT_NAMES, (m_a_norm_g, m_a_w_in, m_a_gn_g, m_a_w_o, m_a_ffn_norm_g, m_a_w_gu,
                                    m_a_w_down, m_kv_norm_g, m_w_kv, m_k_norm_g, m_b_norm_g, m_b_w_q,
                                    m_b_q_norm_g, m_b_rel_bias, m_b_w_o, m_b_ffn_norm_g, m_b_w_gu, m_b_w_down)))
    v_all = dict(zip(WEIGHT_NAMES, (v_a_norm_g, v_a_w_in, v_a_gn_g, v_a_w_o, v_a_ffn_norm_g, v_a_w_gu,
                                    v_a_w_down, v_kv_norm_g, v_w_kv, v_k_norm_g, v_b_norm_g, v_b_w_q,
                                    v_b_q_norm_g, v_b_rel_bias, v_b_w_o, v_b_ffn_norm_g, v_b_w_gu, v_b_w_down)))
    xi, yi, ci = _my_place()
    me = 4 * xi + 2 * yi + ci
    big_names = [n for n, _ in BIG]
    axis_of = dict(BIG)

    big_local = {n: _big_shard(p[n], n) for n in big_names}
    shards = {n: a.astype(BF16) for n, a in big_local.items()}
    small_local = _pack_small({n: p[n] for n, _, _ in SMALL})
    w_in, small_all = _exchange(_GatherRider([shards["a_w_in"], small_local]), "gather_in")
    flat_g = small_all.reshape(N_DEV, -1)
    s_full, pos = {}, 0
    for n, length, sharded in SMALL:
        ln = length // N_DEV if sharded else length
        s_full[n] = flat_g[:, pos:pos + ln].reshape(1, -1) if sharded else p[n].reshape(1, -1)
        pos += ln

    parity = jnp.reshape(ci, (1,)).astype(jnp.int32)
    loss, grad_x, recv, g = _local_step(x[0], loss_target[0], shards, w_in, s_full, parity)

    partial = _pack_small({n: g[n] for n, _, _ in SMALL}, last=loss)
    summed = _sum_leading(_exchange(_GatherRider([partial]), "gather_gsmall")[0], "gsmall_sum")
    loss = summed[SMALL_ROWS - 1, SMALL_COLS - 1]
    g_small = _unpack_small(summed, local=False)
    for n, length, sharded in SMALL:
        if sharded:
            g_small[n] = lax.dynamic_slice(g_small[n], (me * (length // N_DEV),), (length // N_DEV,))

    grads, deltas, new_m, new_v = {}, {}, {}, {}
    for n in big_names:
        outs = _adamw(big_local[n], recv[n], _big_shard(m_all[n], n), _big_shard(v_all[n], n), "adamw_" + n)
        grads[n], deltas[n], new_m[n], new_v[n] = (_as_given(a, n, p[n].shape) for a in outs)
    pk = lambda src: _pack_small({n: src[n] for n, _, _ in SMALL})
    outs = _adamw(small_local, pk(g_small)[None], pk(m_all), pk(v_all), "adamw_small")
    g_s, d_s, nm_s, nv_s = (_unpack_small(a, local=True) for a in outs)
    for n, _, _ in SMALL:
        grads[n], deltas[n], new_m[n], new_v[n] = (a[n].reshape(p[n].shape) for a in (g_s, d_s, nm_s, nv_s))

    return (loss, grad_x[None], *[grads[n] for n in WEIGHT_NAMES], *[deltas[n] for n in WEIGHT_NAMES],
            *[new_m[n] for n in WEIGHT_NAMES], *[new_v[n] for n in WEIGHT_NAMES])
```

```python
import numpy as np
import jax
import jax.numpy as jnp
from jax import lax
from jax.experimental import pallas as pl
from jax.experimental.pallas import tpu as pltpu

F32 = jnp.float32
BF16 = jnp.bfloat16

N_DEV = 8
D_MODEL = 1024
CHUNK = 64
EPS = 1e-6
RET_HEADS, RET_DK, RET_DV = 4, 256, 512
RET_STEP = 4
RET_Q_COLS = RET_HEADS * RET_DK
RET_V_COLS = RET_HEADS * RET_DV
ATT_HEADS, ATT_DH = 16, 64
PAST_CHUNKS = 8
REL_CLIP = 256
REL_TABLE = 2 * REL_CLIP + 1
FFN_HIDDEN = 2816
ROPE_BASE = 10000.0
LANES = 128
Q_BLOCK = 256
ATT_SUBS = 4
ATT_ROWS = 32
K_PAD = PAST_CHUNKS * CHUNK
K_WINDOW = Q_BLOCK + K_PAD
REL_BLK = 128
REL_DELTAS = Q_BLOCK // REL_BLK + K_WINDOW // REL_BLK - 1
REL_PAD = 640
NEG = -1e30
VMEM_LIMIT_V7X = 56 * 1024 * 1024
ADAM_LR, ADAM_B1, ADAM_B2, ADAM_EPS, ADAM_WD, ADAM_STEP = 1e-3, 0.9, 0.999, 1e-8, 0.01, 10
MESH = pl.DeviceIdType.MESH
ANY = pl.BlockSpec(memory_space=pl.ANY)


def _params(*semantics):
    return pltpu.CompilerParams(dimension_semantics=semantics, vmem_limit_bytes=VMEM_LIMIT_V7X)


def _pick(dim, cap, align):
    best = None
    for t in range(align, min(dim, cap) + 1, align):
        if dim % t == 0:
            best = t
    assert best is not None, (dim, cap, align)
    return best


def _dot(a, b):
    return lax.dot_general(a, b, (((1,), (0,)), ((), ())), preferred_element_type=F32)


def _dot_nt(a, b):
    return lax.dot_general(a, b, (((1,), (1,)), ((), ())), preferred_element_type=F32)


def _dot_tn(a, b):
    return lax.dot_general(a, b, (((0,), (0,)), ((), ())), preferred_element_type=F32)


def _split2(x):
    hi = x.astype(BF16)
    lo = (x - hi.astype(F32)).astype(BF16)
    return hi, lo


def _split3(x):
    hi = x.astype(BF16)
    r = x - hi.astype(F32)
    mid = r.astype(BF16)
    lo = (r - mid.astype(F32)).astype(BF16)
    return hi, mid, lo


def _sigmoid(x):
    return 1.0 / (1.0 + jnp.exp(-x))


def _accumulate(ref, part, step):
    @pl.when(step == 0)
    def _():
        ref[...] = part

    @pl.when(step > 0)
    def _():
        ref[...] += part


RELAY_AT_NUM, RELAY_AT_DEN = 3, 4


def _my_place():
    return lax.axis_index("x"), lax.axis_index("y"), lax.axis_index("c")


def _flip(v, bit):
    return 1 - v if bit else v


class _NoRelay:
    def relay(self, in_refs, out_refs, sems):
        pass


class _GatherRider:
    def __init__(self, xs):
        self.inputs = list(xs)
        n = len(xs)
        self.out_shape = [jax.ShapeDtypeStruct((N_DEV,) + x.shape, x.dtype) for x in xs]
        self.scratch = [pltpu.SemaphoreType.DMA((7, n)), pltpu.SemaphoreType.DMA((7, n)),
                        pltpu.SemaphoreType.DMA((n,))]
        self.results = None

    def _copies(self, x_refs, out_refs, sems):
        send_sems, recv_sems, local_sems = sems
        n = len(x_refs)
        x, y, c = _my_place()
        me, sibling = (x, y, c), (x, y, 1 - c)
        chips = [(1 - x, y), (x, 1 - y), (1 - x, 1 - y)]

        def slot(a, px, py, pc):
            return out_refs[a].at[4 * px + 2 * py + pc]

        def copy(k, a, block, to, own=False):
            return pltpu.make_async_remote_copy(
                src_ref=x_refs[a] if own else slot(a, *block), dst_ref=slot(a, *block),
                send_sem=send_sems.at[k, a], recv_sem=recv_sems.at[k, a],
                device_id=to, device_id_type=MESH)

        mine = [pltpu.make_async_copy(x_refs[a], slot(a, *me), local_sems.at[a]) for a in range(n)]
        first = []
        for a in range(n):
            first.append(copy(0, a, me, sibling, own=True))
            first += [copy(1 + j, a, me, (*chip, c), own=True) for j, chip in enumerate(chips)]
        return n, c, me, sibling, chips, copy, mine, first

    def start(self, x_refs, out_refs, sems):
        _, _, _, _, _, _, mine, first = self._copies(x_refs, out_refs, sems)
        for cp in mine + first:
            cp.start()

    def relay(self, x_refs, out_refs, sems):
        n, c, me, sibling, chips, copy, _, _ = self._copies(x_refs, out_refs, sems)
        for j, chip in enumerate(chips):
            for a in range(n):
                copy(1 + j, a, (*chip, c), me).wait_recv()
                copy(4 + j, a, (*chip, c), sibling).start()

    def finish(self, x_refs, out_refs, sems):
        n, c, me, sibling, chips, copy, mine, first = self._copies(x_refs, out_refs, sems)
        passed = [copy(4 + j, a, (*chip, c), sibling) for j, chip in enumerate(chips) for a in range(n)]
        for a in range(n):
            copy(0, a, sibling, me).wait_recv()
            for j, chip in enumerate(chips):
                copy(4 + j, a, (*chip, 1 - c), me).wait_recv()
        for cp in first + passed:
            cp.wait_send()
        for cp in mine:
            cp.wait()


class _ScatterRider(_NoRelay):
    def __init__(self, gs):
        self.inputs = list(gs)
        n = len(gs)
        self.out_shape = [jax.ShapeDtypeStruct(g.shape, g.dtype) for g in gs]
        self.scratch = [pltpu.SemaphoreType.DMA((7, n)), pltpu.SemaphoreType.DMA((7, n)),
                        pltpu.SemaphoreType.DMA((n,))]
        self.results = None

    def _copies(self, g_refs, out_refs, sems):
        send_sems, recv_sems, local_sems = sems
        x, y, c = _my_place()
        me = 4 * x + 2 * y + c
        mine, copies = [], []
        for a in range(len(g_refs)):
            mine.append(pltpu.make_async_copy(g_refs[a].at[me], out_refs[a].at[me], local_sems.at[a]))
            for k in range(1, N_DEV):
                px, py, pc = _flip(x, k & 4), _flip(y, k & 2), _flip(c, k & 1)
                copies.append(pltpu.make_async_remote_copy(
                    src_ref=g_refs[a].at[4 * px + 2 * py + pc], dst_ref=out_refs[a].at[me],
                    send_sem=send_sems.at[k - 1, a], recv_sem=recv_sems.at[k - 1, a],
                    device_id=(px, py, pc), device_id_type=MESH))
        return mine, copies

    def start(self, g_refs, out_refs, sems):
        mine, copies = self._copies(g_refs, out_refs, sems)
        for cp in mine + copies:
            cp.start()

    def finish(self, g_refs, out_refs, sems):
        mine, copies = self._copies(g_refs, out_refs, sems)
        for cp in copies + mine:
            cp.wait()


class _SiblingSwapRider(_NoRelay):
    def __init__(self, gs):
        self.inputs = list(gs)
        n = len(gs)
        self.out_shape = [jax.ShapeDtypeStruct((4,) + g.shape[1:], g.dtype) for g in gs]
        self.scratch = [pltpu.SemaphoreType.DMA((4, n)), pltpu.SemaphoreType.DMA((4, n))]
        self.results = None

    def _copies(self, g_refs, out_refs, sems):
        send_sems, recv_sems = sems
        x, y, c = _my_place()
        return [pltpu.make_async_remote_copy(
            src_ref=g_refs[a].at[2 * k + 1 - c], dst_ref=out_refs[a].at[k],
            send_sem=send_sems.at[k, a], recv_sem=recv_sems.at[k, a],
            device_id=(x, y, 1 - c), device_id_type=MESH)
            for a in range(len(g_refs)) for k in range(4)]

    def start(self, g_refs, out_refs, sems):
        for cp in self._copies(g_refs, out_refs, sems):
            cp.start()

    def finish(self, g_refs, out_refs, sems):
        for cp in self._copies(g_refs, out_refs, sems):
            cp.wait()


class _ChipScatterRider(_NoRelay):
    def __init__(self, ps):
        self.inputs = list(ps)
        n = len(ps)
        self.out_shape = [jax.ShapeDtypeStruct(p.shape, p.dtype) for p in ps]
        self.scratch = [pltpu.SemaphoreType.DMA((3, n)), pltpu.SemaphoreType.DMA((3, n)),
                        pltpu.SemaphoreType.DMA((n,))]
        self.results = None

    def _copies(self, p_refs, out_refs, sems):
        send_sems, recv_sems, local_sems = sems
        x, y, c = _my_place()
        my_chip = 2 * x + y
        chips = [(1 - x, y), (x, 1 - y), (1 - x, 1 - y)]
        n = len(p_refs)
        mine = [pltpu.make_async_copy(p_refs[a].at[my_chip], out_refs[a].at[my_chip], local_sems.at[a])
                for a in range(n)]
        copies = [pltpu.make_async_remote_copy(
            src_ref=p_refs[a].at[2 * cx + cy], dst_ref=out_refs[a].at[my_chip],
            send_sem=send_sems.at[j, a], recv_sem=recv_sems.at[j, a],
            device_id=(cx, cy, c), device_id_type=MESH)
            for a in range(n) for j, (cx, cy) in enumerate(chips)]
        return mine, copies

    def start(self, p_refs, out_refs, sems):
        mine, copies = self._copies(p_refs, out_refs, sems)
        for cp in mine + copies:
            cp.start()

    def finish(self, p_refs, out_refs, sems):
        mine, copies = self._copies(p_refs, out_refs, sems)
        for cp in copies + mine:
            cp.wait()


def _call(body, name, grid, in_specs, out_specs, out_shape, scratch, semantics, args, rider=None):
    in_specs, out_specs, out_shape, scratch = list(in_specs), list(out_specs), list(out_shape), list(scratch)
    if rider is None:
        return list(pl.pallas_call(
            body, name=name, grid=grid, in_specs=in_specs, out_specs=out_specs, out_shape=out_shape,
            scratch_shapes=scratch, compiler_params=_params(*semantics))(*args))
    n_in, n_out, n_scr = len(in_specs), len(out_specs), len(scratch)
    r_in, r_out = len(rider.inputs), len(rider.out_shape)

    def wrapped(*refs):
        cuts = np.cumsum([0, n_in, r_in, n_out, r_out, n_scr])
        hi, ri, ho, ro, hs = (refs[cuts[i]:cuts[i + 1]] for i in range(5))
        rs = refs[cuts[5]:]
        step, steps = pl.program_id(0), grid[0]
        for d in range(1, len(grid)):
            step, steps = step * grid[d] + pl.program_id(d), steps * grid[d]

        @pl.when(step == 0)
        def _():
            rider.start(ri, ro, rs)

        body(*hi, *ho, *hs)

        @pl.when(step == (steps * RELAY_AT_NUM) // RELAY_AT_DEN)
        def _():
            rider.relay(ri, ro, rs)

        @pl.when(step == steps - 1)
        def _():
            rider.finish(ri, ro, rs)

    outs = pl.pallas_call(
        wrapped, name=name, grid=grid,
        in_specs=in_specs + [ANY] * r_in, out_specs=out_specs + [ANY] * r_out,
        out_shape=out_shape + rider.out_shape, scratch_shapes=scratch + rider.scratch,
        compiler_params=_params(*(["arbitrary"] * len(grid))),
    )(*args, *rider.inputs)
    rider.results = list(outs[n_out:])
    return list(outs[:n_out])


def _exchange(rider, name):
    r_in, r_out = len(rider.inputs), len(rider.out_shape)

    def body(*refs):
        ri, ro, rs = refs[:r_in], refs[r_in:r_in + r_out], refs[r_in + r_out:]
        rider.start(ri, ro, rs)
        rider.relay(ri, ro, rs)
        rider.finish(ri, ro, rs)

    return list(pl.pallas_call(
        body, name=name, in_specs=[ANY] * r_in, out_specs=[ANY] * r_out,
        out_shape=rider.out_shape, scratch_shapes=rider.scratch)(*rider.inputs))


MM_CAP_MN = 1024
MM_CAP_N = 1536
MM_CAP_K = 3072
MM_CAP_K_TOKENS = 2048
MM_CAP_K_RMS = 8192
MM_CAP_M_RMS = 512
NORM_ROWS = 256


def _mm(a, b, mode, name, out_dtype=F32, res=None, out_block=None, epilogue=None, extra=None, norm_g=None,
        norm_b=False, rider=None):
    a3, b3 = a.ndim == 3, b.ndim == 3
    um = un = uk = None
    if mode in ("nn", "nt"):
        if a3:
            m, uk = a.shape[1:]
            k = a.shape[0] * uk
        else:
            m, k = a.shape
    else:
        if a3:
            k, um = a.shape[1:]
            m = a.shape[0] * um
        else:
            k, m = a.shape
    if mode in ("nn", "tn"):
        if b3:
            kb, un = b.shape[1:]
            n = b.shape[0] * un
        else:
            kb, n = b.shape
        assert kb == k, (a.shape, b.shape, mode)
    else:
        if b3:
            n, ukb = b.shape[1:]
            assert b.shape[0] * ukb == k and uk in (None, ukb), (a.shape, b.shape, mode)
            uk = ukb
        else:
            n, kb = b.shape
            assert kb == k, (a.shape, b.shape, mode)
    if out_block is not None:
        assert un in (None, out_block)
        un = out_block

    def tile(dim, unit, cap, align):
        if unit is None:
            return _pick(dim, cap, align), 1
        c = max(1, cap // unit)
        while (dim // unit) % c:
            c -= 1
        return unit, c

    cap_m = 1408 if mode == "tn" else (MM_CAP_M_RMS if epilogue == "rms_bwd" else MM_CAP_MN)
    um, cm = tile(m, um, cap_m, 128 if mode == "tn" else 16)
    un, cn = tile(n, un, MM_CAP_N, 128)
    cap_k = MM_CAP_K_TOKENS if mode == "tn" else (MM_CAP_K_RMS if epilogue == "rms_bwd" else MM_CAP_K)
    uk, ck = tile(k, uk, cap_k, 128)
    if epilogue == "rms_bwd":
        assert mode != "tn" and n == D_MODEL and cm == cn == 1 and res is None and out_block is None
    if epilogue == "loss":
        assert n == D_MODEL and cm == cn == 1 and res is not None and out_block is None
    if norm_g is not None and norm_b:
        assert mode == "tn" and not b3 and n == D_MODEL and cn == 1
    elif norm_g is not None:
        assert not a3 and (m if mode == "tn" else k) == D_MODEL and (cm if mode == "tn" else ck) == 1
    if epilogue == "swiglu":
        assert res is None and ((mode == "nn" and b3 and out_block is None) or
                                (mode == "nt" and not b3 and out_block is not None))
        cn = 2
    if epilogue == "swiglu_bwd":
        assert mode == "nt" and out_block is not None and extra is not None and res is None
        cn = 1
    tm, tn, tk = cm * um, cn * un, ck * uk
    nk = k // tk
    dot = {"nn": _dot, "nt": _dot_nt, "tn": _dot_tn}[mode]
    half = n // un // 2
    blocked_out = out_block is not None or epilogue in ("swiglu", "swiglu_bwd")
    extras = [] if extra is None else (list(extra) if isinstance(extra, (tuple, list)) else [extra])

    def sl(idx, unit, count):
        return slice(None) if count == 1 else slice(idx * unit, (idx + 1) * unit)

    def body(*refs):
        a_ref, b_ref = refs[0], refs[1]
        pos = 2
        r_ref = ng_ref = None
        if res is not None:
            r_ref, pos = refs[pos], pos + 1
        e_refs, pos = refs[pos:pos + len(extras)], pos + len(extras)
        if norm_g is not None:
            ng_ref, pos = refs[pos], pos + 1
        outs, acc_ref = refs[pos:-1], refs[-1]
        kk = pl.program_id(2)

        def normed(x_ref):
            groups = []
            for r in range(0, x_ref.shape[0], NORM_ROWS):
                xv = x_ref[r:r + NORM_ROWS, :]
                rstd = lax.rsqrt(jnp.mean(xv * xv, axis=-1, keepdims=True) + EPS)
                groups.append((xv * rstd * ng_ref[...]).astype(BF16))
            return jnp.concatenate(groups, axis=0)

        def a_blk(mi, ki):
            if norm_g is not None and not norm_b:
                return normed(a_ref)
            if mode in ("nn", "nt"):
                return a_ref[ki] if a3 else a_ref[:, sl(ki, uk, ck)]
            return a_ref[mi] if a3 else a_ref[:, sl(mi, um, cm)]

        def b_blk(ki, ni):
            if norm_b:
                return normed(b_ref)
            if epilogue == "swiglu":
                return b_ref[ni, 0]
            if mode in ("nn", "tn"):
                return b_ref[ni] if b3 else b_ref[sl(ki, uk, ck), sl(ni, un, cn)]
            return b_ref[ki][sl(ni, un, cn), :] if b3 else b_ref[sl(ni, un, cn), sl(ki, uk, ck)]

        parts = {}
        for mi in range(cm):
            for ni in range(cn):
                part = None
                for ki in range(ck):
                    d = dot(a_blk(mi, ki).astype(BF16), b_blk(ki, ni).astype(BF16))
                    part = d if part is None else part + d
                parts[mi, ni] = part

        def finish(total):
            if epilogue == "swiglu":
                gate, up = total[0, 0], total[0, 1]
                outs[0][0, 0] = gate.astype(BF16)
                outs[0][1, 0] = up.astype(BF16)
                outs[1][0] = (gate * _sigmoid(gate) * up).astype(BF16)
                return
            if epilogue == "swiglu_bwd":
                dact = total[0, 0]
                gate, up = e_refs[0][0, 0].astype(F32), e_refs[0][1, 0].astype(F32)
                sg = _sigmoid(gate)
                outs[0][0, 0] = (dact * up * (sg * (1.0 + gate * (1.0 - sg)))).astype(BF16)
                outs[0][1, 0] = (dact * (gate * sg)).astype(BF16)
                return
            if epilogue == "rms_bwd":
                x_ref, g_ref, dres_ref = e_refs
                dh, dg = total[0, 0], None
                for r in range(0, tm, NORM_ROWS):
                    rows = slice(r, r + NORM_ROWS)
                    xv, dhv = x_ref[rows, :], dh[rows, :]
                    rstd = lax.rsqrt(jnp.mean(xv * xv, axis=-1, keepdims=True) + EPS)
                    xh = xv * rstd
                    dyg = dhv * g_ref[...]
                    c = jnp.mean(dyg * xh, axis=-1, keepdims=True)
                    outs[0][rows, :] = dres_ref[rows, :] + rstd * (dyg - xh * c)
                    part = jnp.sum(dhv * xh, axis=0, keepdims=True)
                    dg = part if dg is None else dg + part
                _accumulate(outs[1], dg, pl.program_id(0))
                return
            if epilogue == "loss":
                diff = r_ref[...] + total[0, 0] - e_refs[0][...]
                outs[0][...] = diff * (1.0 / n)
                sq = jnp.sum(jnp.sum(diff * diff, axis=-1, keepdims=True), axis=0, keepdims=True)
                _accumulate(outs[1], sq * (0.5 / n), pl.program_id(0))
                return
            for (mi, ni), val in total.items():
                rows, cols = sl(mi, um, cm), sl(ni, un, cn)
                if res is not None:
                    val = r_ref[rows, cols] + val
                if blocked_out:
                    outs[0][ni, rows] = val.astype(out_dtype)
                else:
                    outs[0][rows, cols] = val.astype(out_dtype)

        if nk == 1:
            finish(parts)
        else:
            @pl.when(kk == 0)
            def _():
                for (mi, ni), val in parts.items():
                    acc_ref[mi * cn + ni] = val

            @pl.when(jnp.logical_and(kk > 0, kk < nk - 1))
            def _():
                for (mi, ni), val in parts.items():
                    acc_ref[mi * cn + ni] += val

            @pl.when(kk == nk - 1)
            def _():
                finish({key: acc_ref[key[0] * cn + key[1]] + val for key, val in parts.items()})

    if mode in ("nn", "nt"):
        a_spec = (pl.BlockSpec((ck, tm, uk), lambda i, j, kk: (kk, i, 0)) if a3
                  else pl.BlockSpec((tm, tk), lambda i, j, kk: (i, kk)))
    else:
        a_spec = (pl.BlockSpec((cm, tk, um), lambda i, j, kk: (i, kk, 0)) if a3
                  else pl.BlockSpec((tk, tm), lambda i, j, kk: (kk, i)))
    pair_spec = pl.BlockSpec((2, 1, tm, un), lambda i, j, kk: (0, j, i, 0))
    row_spec = pl.BlockSpec((tm, tn), lambda i, j, kk: (i, 0))
    vec_spec = pl.BlockSpec((1, tn), lambda i, j, kk: (0, 0))
    if epilogue == "swiglu" and mode == "nn":
        b = b.reshape(2, half, k, un)
        b_spec = pl.BlockSpec((2, 1, tk, un), lambda i, j, kk: (0, j, kk, 0))
    elif epilogue == "swiglu":
        b = b.reshape(2, half, un, k)
        b_spec = pl.BlockSpec((2, 1, un, tk), lambda i, j, kk: (0, j, 0, kk))
    elif mode in ("nn", "tn"):
        b_spec = (pl.BlockSpec((cn, tk, un), lambda i, j, kk: (j, kk, 0)) if b3
                  else pl.BlockSpec((tk, tn), lambda i, j, kk: (kk, j)))
    else:
        b_spec = (pl.BlockSpec((ck, tn, uk), lambda i, j, kk: (kk, j, 0)) if b3
                  else pl.BlockSpec((tn, tk), lambda i, j, kk: (j, kk)))
    if epilogue == "swiglu":
        out_specs = [pair_spec, pl.BlockSpec((1, tm, un), lambda i, j, kk: (j, i, 0))]
        out_shape = [jax.ShapeDtypeStruct((2, half, m, un), BF16), jax.ShapeDtypeStruct((half, m, un), BF16)]
    elif epilogue == "swiglu_bwd":
        out_specs = [pair_spec]
        out_shape = [jax.ShapeDtypeStruct(extra.shape, BF16)]
    elif epilogue == "rms_bwd":
        out_specs = [row_spec, vec_spec]
        out_shape = [jax.ShapeDtypeStruct((m, n), F32), jax.ShapeDtypeStruct((1, n), F32)]
    elif epilogue == "loss":
        out_specs = [row_spec, pl.BlockSpec((1, 1), lambda i, j, kk: (0, 0))]
        out_shape = [jax.ShapeDtypeStruct((m, n), F32), jax.ShapeDtypeStruct((1, 1), F32)]
    elif blocked_out:
        out_specs = [pl.BlockSpec((cn, tm, un), lambda i, j, kk: (j, i, 0))]
        out_shape = [jax.ShapeDtypeStruct((n // un, m, un), out_dtype)]
    else:
        out_specs = [pl.BlockSpec((tm, tn), lambda i, j, kk: (i, j))]
        out_shape = [jax.ShapeDtypeStruct((m, n), out_dtype)]
    in_specs, args = [a_spec, b_spec], [a, b]
    if res is not None:
        in_specs.append(pl.BlockSpec((tm, tn), lambda i, j, kk: (i, j)))
        args.append(res)
    if epilogue == "swiglu_bwd":
        in_specs.append(pair_spec)
    elif epilogue == "rms_bwd":
        in_specs += [row_spec, vec_spec, row_spec]
    elif epilogue == "loss":
        in_specs.append(row_spec)
    args += extras
    if norm_g is not None:
        in_specs.append(pl.BlockSpec((1, D_MODEL), lambda i, j, kk: (0, 0)))
        args.append(norm_g)
    semantics = ("arbitrary",) * 3 if epilogue in ("rms_bwd", "loss") else ("parallel", "parallel", "arbitrary")
    out = _call(body, name, (m // tm, n // tn, nk), in_specs, out_specs, out_shape,
                [pltpu.VMEM((cm * cn, um, un), F32)], semantics, args, rider)
    return out if epilogue in ("swiglu", "rms_bwd", "loss") else out[0]


def _head_sums(v, ind):
    hi, lo = _split2(v)
    return _dot(hi, ind) + _dot(lo, ind)


def _head_spread(per_head, ind):
    hi, lo = _split2(per_head)
    return _dot_nt(hi, ind) + _dot_nt(lo, ind)


def _head_rstd(xv, ind):
    return _head_spread(lax.rsqrt(_head_sums(xv * xv, ind) * (1.0 / ATT_DH) + EPS), ind)


def _hn_bwd_math(xv, gv, ind, dyv, scale):
    rstd = _head_rstd(xv, ind)
    xh = xv * rstd
    dyn = dyv * scale
    dyg = dyn * gv
    dx = rstd * (dyg - xh * _head_spread(_head_sums(dyg * xh, ind) * (1.0 / ATT_DH), ind))
    return dx, jnp.sum(dyn * xh, axis=0, keepdims=True)


def _q_hnorm(x, g_tiled, bd, scale, name):
    t, d = x.shape
    tm = _pick(t, 512, 16)

    def body(x_ref, g_ref, bd_ref, o_ref):
        xv = x_ref[...]
        o_ref[...] = (xv * _head_rstd(xv, bd_ref[...]) * g_ref[...] * scale).astype(BF16)

    return pl.pallas_call(
        body, name=name, grid=(t // tm,),
        in_specs=[pl.BlockSpec((tm, d), lambda i: (i, 0)), pl.BlockSpec((1, d), lambda i: (0, 0)),
                  pl.BlockSpec((d, LANES), lambda i: (0, 0))],
        out_specs=pl.BlockSpec((tm, d), lambda i: (i, 0)),
        out_shape=jax.ShapeDtypeStruct((t, d), BF16),
        compiler_params=_params("parallel"),
    )(x, g_tiled, bd)


def _q_dhnorm(x, g_tiled, bd, dy, scale, name):
    t, d = x.shape
    tm = _pick(t, 512, 16)

    def body(x_ref, g_ref, bd_ref, dy_ref, dx_ref, dg_ref):
        dx, part = _hn_bwd_math(x_ref[...], g_ref[...], bd_ref[...], dy_ref[...], scale)
        dx_ref[...] = dx.astype(BF16)
        _accumulate(dg_ref, part, pl.program_id(0))

    row = pl.BlockSpec((tm, d), lambda i: (i, 0))
    vec = pl.BlockSpec((1, d), lambda i: (0, 0))
    return pl.pallas_call(
        body, name=name, grid=(t // tm,),
        in_specs=[row, vec, pl.BlockSpec((d, LANES), lambda i: (0, 0)), row],
        out_specs=[row, vec],
        out_shape=[jax.ShapeDtypeStruct((t, d), BF16), jax.ShapeDtypeStruct((1, d), F32)],
        compiler_params=_params("arbitrary"),
    )(x, g_tiled, bd, dy)


def _kv_prep(kv, g_tiled, bd, name):
    t = kv.shape[0]
    d = D_MODEL
    tm = K_PAD
    assert t % tm == 0

    def body(k_ref, v_ref, g_ref, bd_ref, kp_ref, vp_ref):
        i = pl.program_id(0)

        @pl.when(i == 0)
        def _():
            kp_ref[...] = jnp.zeros_like(kp_ref)
            vp_ref[...] = jnp.zeros_like(vp_ref)

        @pl.when(i > 0)
        def _():
            xv = k_ref[...]
            kp_ref[...] = (xv * _head_rstd(xv, bd_ref[...]) * g_ref[...]).astype(BF16)
            vp_ref[...] = v_ref[...].astype(BF16)

    shp = jax.ShapeDtypeStruct((t + K_PAD, d), BF16)
    out = pl.BlockSpec((tm, d), lambda i: (i, 0))
    return pl.pallas_call(
        body, name=name, grid=(t // tm + 1,),
        in_specs=[pl.BlockSpec((tm, d), lambda i: (jnp.maximum(i - 1, 0), 0)),
                  pl.BlockSpec((tm, d), lambda i: (jnp.maximum(i - 1, 0), 1)),
                  pl.BlockSpec((1, d), lambda i: (0, 0)), pl.BlockSpec((d, LANES), lambda i: (0, 0))],
        out_specs=[out, out], out_shape=[shp, shp],
        compiler_params=_params("arbitrary"),
    )(kv, kv, g_tiled, bd)


def _kv_dprep(kv, g_tiled, bd, dkp_t, dvp_t, name):
    t = kv.shape[0]
    d = D_MODEL
    tm = K_PAD

    def body(k_ref, g_ref, bd_ref, dk_ref, dv_ref, o_ref, dg_ref):
        dx, part = _hn_bwd_math(k_ref[...], g_ref[...], bd_ref[...], dk_ref[...].T, 1.0)
        o_ref[:, :d] = dx.astype(BF16)
        o_ref[:, d:] = dv_ref[...].T.astype(BF16)
        _accumulate(dg_ref, part, pl.program_id(0))

    vec = pl.BlockSpec((1, d), lambda i: (0, 0))
    padded = pl.BlockSpec((d, tm), lambda i: (0, i + 1))
    return pl.pallas_call(
        body, name=name, grid=(t // tm,),
        in_specs=[pl.BlockSpec((tm, d), lambda i: (i, 0)), vec, pl.BlockSpec((d, LANES), lambda i: (0, 0)),
                  padded, padded],
        out_specs=[pl.BlockSpec((tm, 2 * d), lambda i: (i, 0)), vec],
        out_shape=[jax.ShapeDtypeStruct((t, 2 * d), BF16), jax.ShapeDtypeStruct((1, d), F32)],
        compiler_params=_params("arbitrary"),
    )(kv, g_tiled, bd, dkp_t, dvp_t)


def _ret_consts(t):
    h = np.arange(RET_HEADS, dtype=np.float32)
    lg = np.log(np.float32(1.0) - np.float32(2.0) ** (np.float32(-5.0) - h)).astype(np.float32)
    tt = np.arange(CHUNK, dtype=np.float32)
    intra = np.exp(lg[:, None, None] * np.abs(tt[:, None] - tt[None, :])).astype(np.float32)
    q_dec = np.exp(lg[:, None] * (tt + 1.0)).astype(np.float32)
    k_dec = np.exp(lg[:, None] * (CHUNK - 1.0 - tt)).astype(np.float32)
    s_dec = [float(v) for v in np.exp(lg * np.float32(CHUNK)).astype(np.float32)]
    qd = np.broadcast_to(q_dec[:, :, None], (RET_HEADS, CHUNK, RET_DK)).copy()
    kd = np.broadcast_to(k_dec[:, :, None], (RET_HEADS, CHUNK, RET_DK)).copy()
    half = RET_DK // 2
    inv_freq = ROPE_BASE ** (-jnp.arange(half, dtype=F32) / half)
    ang = jnp.arange(t).astype(F32)[:, None] * inv_freq[None, :]
    return jnp.asarray(intra), jnp.asarray(qd), jnp.asarray(kd), s_dec, jnp.cos(ang), jnp.sin(ang)


def _rope(x, cos, sin):
    half = RET_DK // 2
    x1, x2 = x[:, :half], x[:, half:]
    return jnp.concatenate([x1 * cos - x2 * sin, x1 * sin + x2 * cos], axis=-1)


def _unrope(d, cos, sin):
    half = RET_DK // 2
    d1, d2 = d[:, :half], d[:, half:]
    return jnp.concatenate([d1 * cos + d2 * sin, d2 * cos - d1 * sin], axis=-1)


def _ret_slices(h):
    q = slice(h * RET_DK, (h + 1) * RET_DK)
    k = slice(RET_Q_COLS + h * RET_DK, RET_Q_COLS + (h + 1) * RET_DK)
    v = slice(2 * RET_Q_COLS + h * RET_DV, 2 * RET_Q_COLS + (h + 1) * RET_DV)
    g = slice(2 * RET_Q_COLS + RET_V_COLS + h * RET_DV, 2 * RET_Q_COLS + RET_V_COLS + (h + 1) * RET_DV)
    o = slice(h * RET_DV, (h + 1) * RET_DV)
    return q, k, v, g, o


def _ret_fwd(proj, gn, consts, name, rider=None):
    t, cols = proj.shape
    n = t // CHUNK
    intra, qd, kd, s_dec, cos, sin = consts
    k_scale = RET_DK ** -0.5

    def body(p_ref, cos_ref, sin_ref, intra_ref, qd_ref, kd_ref, gn_ref, y_ref, o_ref, st_ref, state):
        i = pl.program_id(0)

        @pl.when(i == 0)
        def _():
            state[...] = jnp.zeros_like(state)

        for c in range(RET_STEP):
            rows = slice(c * CHUNK, (c + 1) * CHUNK)
            cosv, sinv = cos_ref[rows, :], sin_ref[rows, :]
            for h in range(RET_HEADS):
                qs, ks, vs, gs, os_ = _ret_slices(h)
                qr = _rope(p_ref[rows, qs], cosv, sinv)
                kr = _rope(p_ref[rows, ks], cosv, sinv) * k_scale
                vb = p_ref[rows, vs].astype(BF16)
                gv = p_ref[rows, gs]
                scores = _dot_nt(qr.astype(BF16), kr.astype(BF16)) * intra_ref[h]
                s_old = state[h]
                s_old_b = s_old.astype(BF16)
                st_ref[c, h] = s_old_b
                o = _dot(scores.astype(BF16), vb) + _dot((qr * qd_ref[h]).astype(BF16), s_old_b)
                state[h] = s_old * s_dec[h] + _dot_tn((kr * kd_ref[h]).astype(BF16), vb)
                rstd = lax.rsqrt(jnp.mean(o * o, axis=-1, keepdims=True) + EPS)
                on = o * rstd * gn_ref[:, os_]
                o_ref[rows, os_] = o
                y_ref[rows, os_] = (gv * _sigmoid(gv) * on).astype(BF16)

    full3 = lambda a: pl.BlockSpec(a.shape, lambda i: (0, 0, 0))
    step = RET_STEP * CHUNK
    return _call(
        body, name, (n // RET_STEP,),
        [pl.BlockSpec((step, cols), lambda i: (i, 0)),
         pl.BlockSpec((step, RET_DK // 2), lambda i: (i, 0)),
         pl.BlockSpec((step, RET_DK // 2), lambda i: (i, 0)),
         full3(intra), full3(qd), full3(kd),
         pl.BlockSpec((1, RET_V_COLS), lambda i: (0, 0))],
        [pl.BlockSpec((step, RET_V_COLS), lambda i: (i, 0)),
         pl.BlockSpec((step, RET_V_COLS), lambda i: (i, 0)),
         pl.BlockSpec((RET_STEP, RET_HEADS, RET_DK, RET_DV), lambda i: (i, 0, 0, 0))],
        [jax.ShapeDtypeStruct((t, RET_V_COLS), BF16),
         jax.ShapeDtypeStruct((t, RET_V_COLS), F32),
         jax.ShapeDtypeStruct((n, RET_HEADS, RET_DK, RET_DV), BF16)],
        [pltpu.VMEM((RET_HEADS, RET_DK, RET_DV), F32)], ("arbitrary",),
        (proj, cos, sin, intra, qd, kd, gn), rider)


def _ret_bwd(proj, gn, o_saved, states, dy, consts, name, rider=None):
    t, cols = proj.shape
    n = t // CHUNK
    intra, qd, kd, s_dec, cos, sin = consts
    k_scale = RET_DK ** -0.5

    def body(p_ref, cos_ref, sin_ref, intra_ref, qd_ref, kd_ref, gn_ref, o_ref, st_ref, dy_ref,
             dp_ref, dgn_ref, dstate):
        i = pl.program_id(0)

        @pl.when(i == 0)
        def _():
            dstate[...] = jnp.zeros_like(dstate)

        dgn = None
        for c in reversed(range(RET_STEP)):
            rows = slice(c * CHUNK, (c + 1) * CHUNK)
            cosv, sinv = cos_ref[rows, :], sin_ref[rows, :]
            dgn_parts = []
            for h in range(RET_HEADS):
                qs, ks, vs, gs, os_ = _ret_slices(h)
                qr = _rope(p_ref[rows, qs], cosv, sinv)
                kr = _rope(p_ref[rows, ks], cosv, sinv) * k_scale
                qb, kb = qr.astype(BF16), kr.astype(BF16)
                vb = p_ref[rows, vs].astype(BF16)
                gv = p_ref[rows, gs]
                ov = o_ref[rows, os_]
                dyv = dy_ref[rows, os_]
                gnv = gn_ref[:, os_]
                sg = _sigmoid(gv)
                rstd = lax.rsqrt(jnp.mean(ov * ov, axis=-1, keepdims=True) + EPS)
                oh = ov * rstd
                d_on = dyv * (gv * sg)
                dg = dyv * (oh * gnv) * (sg * (1.0 + gv * (1.0 - sg)))
                dgn_parts.append(jnp.sum(d_on * oh, axis=0, keepdims=True))
                d_oh = d_on * gnv
                do = rstd * (d_oh - oh * jnp.mean(d_oh * oh, axis=-1, keepdims=True))
                dob = do.astype(BF16)
                mask = intra_ref[h]
                a_b = (_dot_nt(qb, kb) * mask).astype(BF16)
                da_b = (_dot_nt(dob, vb) * mask).astype(BF16)
                ds_new = dstate[h]
                ds_new_b = ds_new.astype(BF16)
                s_old_b = st_ref[c, h]
                qdv, kdv = qd_ref[h], kd_ref[h]
                dv = _dot_tn(a_b, dob) + _dot((kr * kdv).astype(BF16), ds_new_b)
                dqr = _dot(da_b, kb) + _dot_nt(dob, s_old_b) * qdv
                dkr = _dot_tn(da_b, qb) + _dot_nt(vb, ds_new_b) * kdv
                dstate[h] = ds_new * s_dec[h] + _dot_tn((qr * qdv).astype(BF16), dob)
                dp_ref[rows, qs] = _unrope(dqr, cosv, sinv).astype(BF16)
                dp_ref[rows, ks] = _unrope(dkr * k_scale, cosv, sinv).astype(BF16)
                dp_ref[rows, vs] = dv.astype(BF16)
                dp_ref[rows, gs] = dg.astype(BF16)
            part = jnp.concatenate(dgn_parts, axis=-1)
            dgn = part if dgn is None else dgn + part
        _accumulate(dgn_ref, dgn, i)

    steps = n // RET_STEP
    step = RET_STEP * CHUNK
    rev = lambda i: (steps - 1 - i, 0)
    full3 = lambda a: pl.BlockSpec(a.shape, lambda i: (0, 0, 0))
    return _call(
        body, name, (steps,),
        [pl.BlockSpec((step, cols), rev),
         pl.BlockSpec((step, RET_DK // 2), rev),
         pl.BlockSpec((step, RET_DK // 2), rev),
         full3(intra), full3(qd), full3(kd),
         pl.BlockSpec((1, RET_V_COLS), lambda i: (0, 0)),
         pl.BlockSpec((step, RET_V_COLS), rev),
         pl.BlockSpec((RET_STEP, RET_HEADS, RET_DK, RET_DV), lambda i: (steps - 1 - i, 0, 0, 0)),
         pl.BlockSpec((step, RET_V_COLS), rev)],
        [pl.BlockSpec((step, cols), rev),
         pl.BlockSpec((1, RET_V_COLS), lambda i: (0, 0))],
        [jax.ShapeDtypeStruct((t, cols), BF16),
         jax.ShapeDtypeStruct((1, RET_V_COLS), F32)],
        [pltpu.VMEM((RET_HEADS, RET_DK, RET_DV), F32)], ("arbitrary",),
        (proj, cos, sin, intra, qd, kd, gn, o_saved, states, dy), rider)


def _att_common(q_ref, kp_ref, vp_ref, sub):
    blk = pl.program_id(1) * ATT_SUBS + sub
    start = pl.multiple_of(blk * Q_BLOCK, Q_BLOCK)
    kw = kp_ref[pl.ds(start, K_WINDOW), :]
    vw = vp_ref[pl.ds(start, K_WINDOW), :]
    kvalid = blk * Q_BLOCK - K_PAD + lax.broadcasted_iota(jnp.int32, (1, K_WINDOW), 1) >= 0
    lane = lax.broadcasted_iota(jnp.int32, (1, LANES), 1)
    qrows = slice(sub * Q_BLOCK, (sub + 1) * Q_BLOCK)
    return start, qrows, q_ref[qrows, :], kw, vw, kvalid, (lane < ATT_DH, lane >= ATT_DH)


def _row_groups():
    return [slice(r * ATT_ROWS, (r + 1) * ATT_ROWS) for r in range(Q_BLOCK // ATT_ROWS)]


def _lane_copies(x):
    return jnp.tile(x, (1, K_WINDOW // LANES))


def _att_specs(t, tp):
    qspec = pl.BlockSpec((ATT_SUBS * Q_BLOCK, LANES), lambda h, i: (i, h))
    kspec = pl.BlockSpec((tp, LANES), lambda h, i: (0, h))
    bspec = pl.BlockSpec((2, Q_BLOCK, K_WINDOW), lambda h, i: (h, 0, 0))
    return qspec, kspec, bspec


def _att_fwd(q, kp, vp, bias, name, rider=None):
    t, d = q.shape
    tp = kp.shape[0]

    def body(q_ref, kp_ref, vp_ref, bias_ref, o_ref, lse_ref, s_scr, p_scr, lse_scr):
        for sub in range(ATT_SUBS):
            _, qrows, q2, kw, vw, kvalid, sel = _att_common(q_ref, kp_ref, vp_ref, sub)
            for hh in range(2):
                s_scr[sub, hh] = _dot_nt(jnp.where(sel[hh], q2, 0), kw)
            for hh in range(2):
                for rows in _row_groups():
                    s = jnp.where(kvalid, s_scr[sub, hh, rows, :] + bias_ref[hh, rows, :], NEG)
                    m = jnp.max(s, axis=-1, keepdims=True)
                    e = jnp.exp(s - m)
                    l = jnp.sum(e, axis=-1, keepdims=True)
                    p_scr[sub, hh, rows, :] = (e * (1.0 / l)).astype(BF16)
                    lse_scr[sub, hh, rows, :] = jnp.broadcast_to(m + jnp.log(l), (ATT_ROWS, LANES))
            outs = [_dot(p_scr[sub, hh], vw) for hh in range(2)]
            o_ref[qrows, :] = jnp.where(sel[0], outs[0], outs[1]).astype(BF16)
            lse_ref[qrows, :] = jnp.where(sel[0], lse_scr[sub, 0], lse_scr[sub, 1])

    qspec, kspec, bspec = _att_specs(t, tp)
    return _call(body, name, (d // LANES, t // (ATT_SUBS * Q_BLOCK)), [qspec, kspec, kspec, bspec], [qspec, qspec],
                 [jax.ShapeDtypeStruct((t, d), BF16), jax.ShapeDtypeStruct((t, d), F32)],
                 [pltpu.VMEM((ATT_SUBS, 2, Q_BLOCK, K_WINDOW), F32),
                  pltpu.VMEM((ATT_SUBS, 2, Q_BLOCK, K_WINDOW), BF16),
                  pltpu.VMEM((ATT_SUBS, 2, Q_BLOCK, LANES), F32)],
                 ("parallel", "arbitrary"), (q, kp, vp, bias), rider)


def _att_bwd(q, kp, vp, bias, do, o, lse, name, rider=None):
    t, d = q.shape
    tp = kp.shape[0]

    def body(q_ref, kp_ref, vp_ref, bias_ref, do_ref, o_ref, lse_ref, dq_ref, dkp_ref, dvp_ref, db_ref,
             s_scr, dp_scr, p_scr, ds_scr, row_scr):
        @pl.when(pl.program_id(1) == 0)
        def _():
            dkp_ref[...] = jnp.zeros_like(dkp_ref)
            dvp_ref[...] = jnp.zeros_like(dvp_ref)
            db_ref[...] = jnp.zeros_like(db_ref)

        for sub in range(ATT_SUBS):
            start, qrows, q2, kw, vw, kvalid, sel = _att_common(q_ref, kp_ref, vp_ref, sub)
            do2 = do_ref[qrows, :]
            qm = [jnp.where(sel[hh], q2, 0) for hh in range(2)]
            dom = [jnp.where(sel[hh], do2, 0) for hh in range(2)]
            do_o = do2.astype(F32) * o_ref[qrows, :].astype(F32)
            lse2 = lse_ref[qrows, :]
            for hh in range(2):
                s_scr[sub, hh] = _dot_nt(qm[hh], kw)
                dp_scr[sub, hh] = _dot_nt(dom[hh], vw)
                lse_h = jnp.max(jnp.where(sel[hh], lse2, NEG), axis=-1, keepdims=True)
                delta = jnp.sum(jnp.where(sel[hh], do_o, 0.0), axis=-1, keepdims=True)
                row_scr[sub, hh, 0] = jnp.broadcast_to(lse_h, (Q_BLOCK, LANES))
                row_scr[sub, hh, 1] = jnp.broadcast_to(delta, (Q_BLOCK, LANES))
            for hh in range(2):
                for rows in _row_groups():
                    s = jnp.where(kvalid, s_scr[sub, hh, rows, :] + bias_ref[hh, rows, :], NEG)
                    p = jnp.exp(s - _lane_copies(row_scr[sub, hh, 0, rows, :]))
                    ds = p * (dp_scr[sub, hh, rows, :] - _lane_copies(row_scr[sub, hh, 1, rows, :]))
                    db_ref[hh, rows, :] += ds
                    p_scr[sub, hh, rows, :] = p.astype(BF16)
                    ds_scr[sub, hh, rows, :] = ds.astype(BF16)
            dqs = [_dot(ds_scr[sub, hh], kw) for hh in range(2)]
            dq_ref[qrows, :] = jnp.where(sel[0], dqs[0], dqs[1])
            dkp_ref[:, pl.ds(start, K_WINDOW)] += (_dot_tn(qm[0], ds_scr[sub, 0]) +
                                                   _dot_tn(qm[1], ds_scr[sub, 1]))
            dvp_ref[:, pl.ds(start, K_WINDOW)] += (_dot_tn(dom[0], p_scr[sub, 0]) +
                                                   _dot_tn(dom[1], p_scr[sub, 1]))

    qspec, kspec, bspec = _att_specs(t, tp)
    tspec = pl.BlockSpec((LANES, tp), lambda h, i: (h, 0))
    stage = lambda dt: pltpu.VMEM((ATT_SUBS, 2, Q_BLOCK, K_WINDOW), dt)
    return _call(body, name, (d // LANES, t // (ATT_SUBS * Q_BLOCK)),
                 [qspec, kspec, kspec, bspec, qspec, qspec, qspec],
                 [qspec, tspec, tspec, bspec],
                 [jax.ShapeDtypeStruct((t, d), F32),
                  jax.ShapeDtypeStruct((d, tp), F32),
                  jax.ShapeDtypeStruct((d, tp), F32),
                  jax.ShapeDtypeStruct((ATT_HEADS, Q_BLOCK, K_WINDOW), F32)],
                 [stage(F32), stage(F32), stage(BF16), stage(BF16),
                  pltpu.VMEM((ATT_SUBS, 2, 2, Q_BLOCK, LANES), F32)],
                 ("parallel", "arbitrary"), (q, kp, vp, bias, do, o, lse), rider)


def _rel_bin_matrix():
    rows = REL_DELTAS * 2 * REL_BLK
    rho = lax.broadcasted_iota(jnp.int32, (rows, REL_PAD), 0)
    col = lax.broadcasted_iota(jnp.int32, (rows, REL_PAD), 1)
    assert 2 * REL_BLK == 256
    delta = rho >> 8
    c = 255 - (rho & 255)
    dist = K_PAD + REL_BLK * (delta - (K_WINDOW // REL_BLK - 1)) + (c - (REL_BLK - 1))
    idx = jnp.clip(dist, -REL_CLIP, REL_CLIP) + REL_CLIP
    return col == idx


def _rel_expand(rel_pad, name):
    heads = rel_pad.shape[0]
    rows = REL_DELTAS * 2 * REL_BLK

    def body_bin(r_ref, o_ref):
        onehot = jnp.where(_rel_bin_matrix(), 1.0, 0.0).astype(BF16)
        hi, mid, lo = _split3(r_ref[...])
        o_ref[...] = _dot_nt(hi, onehot) + _dot_nt(mid, onehot) + _dot_nt(lo, onehot)

    by_delta = pl.pallas_call(
        body_bin, name=name + "_bin",
        out_shape=jax.ShapeDtypeStruct((heads, rows), F32),
        compiler_params=pltpu.CompilerParams(vmem_limit_bytes=VMEM_LIMIT_V7X),
    )(rel_pad)
    by_delta = by_delta.reshape(heads * REL_DELTAS, 2 * REL_BLK)

    def body_shift(t_ref, o_ref):
        tv = t_ref[...]
        for r in range(REL_BLK):
            o_ref[r] = pltpu.roll(tv, (r + REL_BLK) % (2 * REL_BLK), 1)[:, :REL_BLK]

    return pl.pallas_call(
        body_shift, name=name + "_shift",
        out_shape=jax.ShapeDtypeStruct((REL_BLK, heads * REL_DELTAS, REL_BLK), F32),
        compiler_params=pltpu.CompilerParams(vmem_limit_bytes=VMEM_LIMIT_V7X),
    )(by_delta)


def _bias_table(rel_bias, name):
    heads = rel_bias.shape[0]
    rel_pad = jnp.pad(rel_bias, ((0, 0), (0, REL_PAD - REL_TABLE)))
    tiles = _rel_expand(rel_pad, name)
    tiles = tiles.reshape(REL_BLK, heads, REL_DELTAS, REL_BLK).transpose(1, 2, 0, 3)
    na, nb = Q_BLOCK // REL_BLK, K_WINDOW // REL_BLK
    rows = [jnp.concatenate([tiles[:, a - b + nb - 1] for b in range(nb)], axis=-1) for a in range(na)]
    table = jnp.concatenate(rows, axis=-2)
    qc = np.arange(Q_BLOCK)[:, None] // CHUNK
    kc = np.arange(K_WINDOW)[None, :] // CHUNK
    band = (kc >= qc) & (kc <= qc + PAST_CHUNKS)
    return jnp.where(jnp.asarray(band)[None], table, NEG)


def _rel_reduce(db, name):
    heads = db.shape[0]
    na, nb = Q_BLOCK // REL_BLK, K_WINDOW // REL_BLK

    fold_heads = 4

    def body_fold(db_ref, g_ref):
        for hd in range(fold_heads):
            for delta in range(REL_DELTAS):
                acc = None
                for a in range(na):
                    b = a - (delta - (nb - 1))
                    if 0 <= b < nb:
                        tile = db_ref[hd, a * REL_BLK:(a + 1) * REL_BLK, b * REL_BLK:(b + 1) * REL_BLK]
                        acc = tile if acc is None else acc + tile
                g_ref[hd, delta] = acc

    folded = pl.pallas_call(
        body_fold, name=name + "_fold", grid=(heads // fold_heads,),
        in_specs=[pl.BlockSpec((fold_heads, Q_BLOCK, K_WINDOW), lambda h: (h, 0, 0))],
        out_specs=pl.BlockSpec((fold_heads, REL_DELTAS, REL_BLK, REL_BLK), lambda h: (h, 0, 0, 0)),
        out_shape=jax.ShapeDtypeStruct((heads, REL_DELTAS, REL_BLK, REL_BLK), F32),
        compiler_params=_params("parallel"),
    )(db)
    by_row = folded.transpose(2, 0, 1, 3).reshape(REL_BLK, heads * REL_DELTAS, REL_BLK)

    def body_diag(g_ref, d_ref):
        zeros = jnp.zeros((heads * REL_DELTAS, REL_BLK), F32)
        acc = None
        for r in range(REL_BLK):
            part = pltpu.roll(jnp.concatenate([g_ref[r], zeros], axis=1), REL_BLK - r, 1)
            acc = part if acc is None else acc + part
        d_ref[...] = acc

    diag = pl.pallas_call(
        body_diag, name=name + "_diag",
        out_shape=jax.ShapeDtypeStruct((heads * REL_DELTAS, 2 * REL_BLK), F32),
        compiler_params=pltpu.CompilerParams(vmem_limit_bytes=VMEM_LIMIT_V7X),
    )(by_row)
    diag = diag.reshape(heads, REL_DELTAS * 2 * REL_BLK)

    def body_bin(d_ref, o_ref):
        onehot = jnp.where(_rel_bin_matrix(), 1.0, 0.0).astype(BF16)
        hi, mid, lo = _split3(d_ref[...])
        o_ref[...] = _dot(hi, onehot) + _dot(mid, onehot) + _dot(lo, onehot)

    out = pl.pallas_call(
        body_bin, name=name + "_bin",
        out_shape=jax.ShapeDtypeStruct((heads, REL_PAD), F32),
        compiler_params=pltpu.CompilerParams(vmem_limit_bytes=VMEM_LIMIT_V7X),
    )(diag)
    return out[:, :REL_TABLE]


def _sum_leading(x, name):
    n, r, c = x.shape
    tr = _pick(r, 256, 8)

    def body(x_ref, o_ref):
        acc = x_ref[0].astype(F32)
        for k in range(1, n):
            acc = acc + x_ref[k].astype(F32)
        o_ref[...] = acc

    return pl.pallas_call(
        body, name=name, grid=(r // tr,),
        in_specs=[pl.BlockSpec((n, tr, c), lambda i: (0, i, 0))],
        out_specs=pl.BlockSpec((tr, c), lambda i: (i, 0)),
        out_shape=jax.ShapeDtypeStruct((r, c), F32),
        compiler_params=_params("parallel"),
    )(x)


def _pair_add(g, recv, parity, name):
    _, r, c = g.shape
    tr = _pick(r, 256, 16)

    def body(par_ref, g_ref, r_ref, o_ref):
        o_ref[...] = (g_ref[...].astype(F32) + r_ref[...].astype(F32)).astype(BF16)

    return pl.pallas_call(
        body, name=name,
        grid_spec=pltpu.PrefetchScalarGridSpec(
            num_scalar_prefetch=1, grid=(4, r // tr),
            in_specs=[pl.BlockSpec((1, tr, c), lambda k, i, par: (2 * k + par[0], i, 0)),
                      pl.BlockSpec((1, tr, c), lambda k, i, par: (k, i, 0))],
            out_specs=pl.BlockSpec((1, tr, c), lambda k, i, par: (k, i, 0))),
        out_shape=jax.ShapeDtypeStruct((4, r, c), BF16),
        compiler_params=_params("parallel", "parallel"),
    )(parity, g, recv)


def _adamw(w, g_parts, m, v, name):
    r, c = w.shape
    n = g_parts.shape[0]
    tr = _pick(r, 256, 16 if g_parts.dtype == BF16 else 8)
    c1 = 1.0 - ADAM_B1 ** ADAM_STEP
    c2 = 1.0 - ADAM_B2 ** ADAM_STEP

    def body(w_ref, g_ref, m_ref, v_ref, go_ref, d_ref, nm_ref, nv_ref):
        gv = g_ref[0].astype(F32)
        for k in range(1, n):
            gv = gv + g_ref[k].astype(F32)
        nm = ADAM_B1 * m_ref[...] + (1.0 - ADAM_B1) * gv
        nv = ADAM_B2 * v_ref[...] + (1.0 - ADAM_B2) * (gv * gv)
        go_ref[...] = gv
        d_ref[...] = -ADAM_LR * ((nm / c1) / (jnp.sqrt(nv / c2) + ADAM_EPS) + ADAM_WD * w_ref[...])
        nm_ref[...] = nm
        nv_ref[...] = nv

    spec = pl.BlockSpec((tr, c), lambda i: (i, 0))
    shp = jax.ShapeDtypeStruct((r, c), F32)
    return pl.pallas_call(
        body, name=name, grid=(r // tr,),
        in_specs=[spec, pl.BlockSpec((n, tr, c), lambda i: (0, i, 0)), spec, spec],
        out_specs=[spec] * 4, out_shape=[shp] * 4,
        compiler_params=_params("parallel"),
    )(w, g_parts, m, v)


BIG = (("a_w_in", 1), ("a_w_o", 0), ("a_w_gu", 0), ("a_w_down", 0), ("w_kv", 1),
       ("b_w_q", 0), ("b_w_o", 0), ("b_w_gu", 0), ("b_w_down", 0))
TRANSPOSED = ("a_w_gu", "b_w_gu")
FFN_BLK = 2 * FFN_HIDDEN // N_DEV

SMALL = (("a_norm_g", D_MODEL, True), ("a_gn_g", RET_V_COLS, True), ("a_ffn_norm_g", D_MODEL, True),
         ("kv_norm_g", D_MODEL, False), ("b_norm_g", D_MODEL, False), ("b_ffn_norm_g", D_MODEL, False),
         ("k_norm_g", ATT_DH, False), ("b_q_norm_g", ATT_DH, False),
         ("b_rel_bias", ATT_HEADS * REL_TABLE, False))
SMALL_ROWS, SMALL_COLS = 16, 1024


def _pack_small(vals, last=None):
    flat = jnp.concatenate([vals[n].reshape(-1) for n, _, _ in SMALL])
    room = SMALL_ROWS * SMALL_COLS - flat.shape[0]
    if last is None:
        flat = jnp.pad(flat, (0, room))
    else:
        flat = jnp.concatenate([jnp.pad(flat, (0, room - 1)), last.reshape(1)])
    return flat.reshape(SMALL_ROWS, SMALL_COLS)


def _unpack_small(packed, local):
    flat, out, pos = packed.reshape(-1), {}, 0
    for n, length, sharded in SMALL:
        ln = length // N_DEV if (local and sharded) else length
        out[n] = flat[pos:pos + ln]
        pos += ln
    return out


def _gather_rider(shards, names):
    return _GatherRider([shards[n] for n in names])


def _gathered(rider, names, axis_of):
    return {n: (r.reshape(-1, r.shape[2]) if axis_of[n] == 0 else r) for n, r in zip(names, rider.results)}


def _blocks(g):
    return g if g.ndim == 3 else g.reshape(N_DEV, -1, g.shape[-1])


def _local_step(x, target, shards, w_in, s, parity):
    t = x.shape[0]
    axis_of = dict(BIG)
    consts = _ret_consts(t)
    lane_to_head = np.zeros((D_MODEL, LANES), np.float32)
    lane_to_head[np.arange(D_MODEL), np.arange(D_MODEL) // ATT_DH] = 1.0
    bd = jnp.asarray(lane_to_head).astype(BF16)
    kg_t = jnp.tile(s["k_norm_g"], (1, ATT_HEADS))
    qg_t = jnp.tile(s["b_q_norm_g"], (1, ATT_HEADS))
    q_scale = ATT_DH ** -0.5
    w = {"a_w_in": w_in}
    g, recv = {}, {}

    def gather_on(names):
        return _gather_rider(shards, names), names

    def landed(ride):
        w.update(_gathered(ride[0], ride[1], axis_of))

    def scatter_on(names):
        return _ScatterRider([_blocks(g[n]) for n in names]), names

    def reduced(ride):
        recv.update(zip(ride[1], ride[0].results))

    ride = gather_on(["a_w_o", "a_w_down"])
    proj = _mm(x, w["a_w_in"], "nn", "a_proj", norm_g=s["a_norm_g"], rider=ride[0])
    landed(ride)
    ride = gather_on(["a_w_gu", "w_kv"])
    y, o_ret, states = _ret_fwd(proj, s["a_gn_g"], consts, "a_ret", rider=ride[0])
    landed(ride)
    x1 = _mm(y, w["a_w_o"], "nn", "a_out", res=x)
    ride = gather_on(["b_w_q", "b_w_o"])
    gu_a, act_a = _mm(x1, w["a_w_gu"], "nt", "a_ffn_gu", epilogue="swiglu", out_block=FFN_BLK,
                      norm_g=s["a_ffn_norm_g"], rider=ride[0])
    landed(ride)
    x2 = _mm(act_a, w["a_w_down"], "nn", "a_ffn_down", res=x1)

    kv = _mm(x2, w["w_kv"], "nn", "kv_proj", norm_g=s["kv_norm_g"])
    kp, vp = _kv_prep(kv, kg_t, bd, "kv_prep")

    q_raw = _mm(x2, w["b_w_q"], "nn", "b_q", norm_g=s["b_norm_g"])
    qn = _q_hnorm(q_raw, qg_t, bd, q_scale, "q_hnorm")
    bias = _bias_table(s["b_rel_bias"].reshape(ATT_HEADS, REL_TABLE), "rel")
    ride = gather_on(["b_w_gu", "b_w_down"])
    o_att, lse = _att_fwd(qn, kp, vp, bias, "b_att", rider=ride[0])
    landed(ride)
    x3 = _mm(o_att, w["b_w_o"], "nn", "b_out", res=x2)
    gu_b, act_b = _mm(x3, w["b_w_gu"], "nt", "b_ffn_gu", epilogue="swiglu", out_block=FFN_BLK,
                      norm_g=s["b_ffn_norm_g"])
    dy, loss = _mm(act_b, w["b_w_down"], "nn", "b_ffn_down", res=x3, epilogue="loss", extra=(target,))
    in_blk, kv_blk, ffn_blk = w["a_w_in"].shape[2], w["w_kv"].shape[2], FFN_BLK

    dgu = _mm(dy, w["b_w_down"], "nt", "b_ffn_dgu", out_block=ffn_blk, epilogue="swiglu_bwd", extra=gu_b)
    dgu = dgu.reshape(N_DEV, t, ffn_blk)
    g["b_w_down"] = _mm(act_b, dy, "tn", "b_ffn_gdown", out_dtype=BF16)
    ride = scatter_on(["b_w_down"])
    dx3, g["b_ffn_norm_g"] = _mm(dgu, w["b_w_gu"], "nn", "b_ffn_dh", epilogue="rms_bwd",
                                 extra=(x3, s["b_ffn_norm_g"], dy), rider=ride[0])
    reduced(ride)
    g["b_w_gu"] = _mm(dgu, x3, "tn", "b_ffn_ggu", out_dtype=BF16, norm_g=s["b_ffn_norm_g"], norm_b=True)

    do_att = _mm(dx3, w["b_w_o"], "nt", "b_dout", out_dtype=BF16)
    g["b_w_o"] = _mm(o_att, dx3, "tn", "b_gout", out_dtype=BF16)
    ride = scatter_on(["b_w_gu", "b_w_o"])
    dq, dkp, dvp, db = _att_bwd(qn, kp, vp, bias, do_att, o_att, lse, "b_datt", rider=ride[0])
    reduced(ride)
    g["b_rel_bias"] = _rel_reduce(db, "drel").reshape(1, -1)
    dq_raw, gq = _q_dhnorm(q_raw, qg_t, bd, dq, q_scale, "q_dhnorm")
    g["b_q_norm_g"] = gq.reshape(ATT_HEADS, ATT_DH).sum(axis=0, keepdims=True)
    g["b_w_q"] = _mm(x2, dq_raw, "tn", "b_gq", out_dtype=BF16, norm_g=s["b_norm_g"])
    dx2, g["b_norm_g"] = _mm(dq_raw, w["b_w_q"], "nt", "b_dq", epilogue="rms_bwd",
                             extra=(x2, s["b_norm_g"], dx3))

    dkv, gk = _kv_dprep(kv, kg_t, bd, dkp, dvp, "kv_dprep")
    g["k_norm_g"] = gk.reshape(ATT_HEADS, ATT_DH).sum(axis=0, keepdims=True)
    g["w_kv"] = _mm(x2, dkv, "tn", "kv_g", out_dtype=BF16, out_block=kv_blk, norm_g=s["kv_norm_g"])
    dx2, g["kv_norm_g"] = _mm(dkv, w["w_kv"], "nt", "kv_du", epilogue="rms_bwd",
                              extra=(x2, s["kv_norm_g"], dx2))

    ride = scatter_on(["b_w_q"])
    dgu = _mm(dx2, w["a_w_down"], "nt", "a_ffn_dgu", out_block=ffn_blk, epilogue="swiglu_bwd", extra=gu_a,
              rider=ride[0])
    reduced(ride)
    dgu = dgu.reshape(N_DEV, t, ffn_blk)
    g["a_w_down"] = _mm(act_a, dx2, "tn", "a_ffn_gdown", out_dtype=BF16)
    ride = scatter_on(["a_w_down"])
    dx1, g["a_ffn_norm_g"] = _mm(dgu, w["a_w_gu"], "nn", "a_ffn_dh", epilogue="rms_bwd",
                                 extra=(x1, s["a_ffn_norm_g"], dx2), rider=ride[0])
    reduced(ride)
    ride = scatter_on(["w_kv"])
    g["a_w_gu"] = _mm(dgu, x1, "tn", "a_ffn_ggu", out_dtype=BF16, norm_g=s["a_ffn_norm_g"], norm_b=True,
                      rider=ride[0])
    reduced(ride)

    dy_ret = _mm(dx1, w["a_w_o"], "nt", "a_dout")
    g["a_w_o"] = _mm(y, dx1, "tn", "a_gout", out_dtype=BF16)
    ride = scatter_on(["a_w_gu"])
    dproj, g["a_gn_g"] = _ret_bwd(proj, s["a_gn_g"], o_ret, states, dy_ret, consts, "a_dret", rider=ride[0])
    reduced(ride)
    ride = scatter_on(["a_w_o"])
    g["a_w_in"] = _mm(x, dproj, "tn", "a_gin", out_dtype=BF16, out_block=in_blk, norm_g=s["a_norm_g"],
                      rider=ride[0])
    reduced(ride)
    from_sibling = _exchange(_SiblingSwapRider([g["a_w_in"]]), "rs_sibling")[0]
    chip_sums = _pair_add(g["a_w_in"], from_sibling, parity, "rs_pair_add")
    last = _ChipScatterRider([chip_sums])
    grad_x, g["a_norm_g"] = _mm(dproj, w["a_w_in"], "nt", "a_dproj", epilogue="rms_bwd",
                                extra=(x, s["a_norm_g"], dx1), rider=last)
    recv["a_w_in"] = last.results[0]
    return loss, grad_x, recv, g


ARG_NAMES = ("x", "a_norm_g", "a_w_in", "a_gn_g", "a_w_o", "a_ffn_norm_g", "a_w_gu", "a_w_down",
             "kv_norm_g", "w_kv", "k_norm_g", "b_norm_g", "b_w_q", "b_q_norm_g", "b_rel_bias", "b_w_o",
             "b_ffn_norm_g", "b_w_gu", "b_w_down")
WEIGHT_NAMES = ARG_NAMES[1:]


def _big_shard(a, name):
    a = a[0] if a.ndim == 3 else a
    return a.T if name in TRANSPOSED else a


def _as_given(a, name, shape):
    return (a.T if name in TRANSPOSED else a).reshape(shape)


def kernel(x, a_norm_g, a_w_in, a_gn_g, a_w_o, a_ffn_norm_g, a_w_gu, a_w_down, kv_norm_g, w_kv, k_norm_g, b_norm_g, b_w_q, b_q_norm_g, b_rel_bias, b_w_o, b_ffn_norm_g, b_w_gu, b_w_down, loss_target, m_a_norm_g, m_a_w_in, m_a_gn_g, m_a_w_o, m_a_ffn_norm_g, m_a_w_gu, m_a_w_down, m_kv_norm_g, m_w_kv, m_k_norm_g, m_b_norm_g, m_b_w_q, m_b_q_norm_g, m_b_rel_bias, m_b_w_o, m_b_ffn_norm_g, m_b_w_gu, m_b_w_down, v_a_norm_g, v_a_w_in, v_a_gn_g, v_a_w_o, v_a_ffn_norm_g, v_a_w_gu, v_a_w_down, v_kv_norm_g, v_w_kv, v_k_norm_g, v_b_norm_g, v_b_w_q, v_b_q_norm_g, v_b_rel_bias, v_b_w_o, v_b_ffn_norm_g, v_b_w_gu, v_b_w_down):
    args = (x, a_norm_g, a_w_in, a_gn_g, a_w_o, a_ffn_norm_g, a_w_gu, a_w_down, kv_norm_g, w_kv, k_norm_g,
            b_norm_g, b_w_q, b_q_norm_g, b_rel_bias, b_w_o, b_ffn_norm_g, b_w_gu, b_w_down)
    p = dict(zip(ARG_NAMES, args))
    m_all = dict(zip(WEIGHT_NAMES, (m_a_norm_g, m_a_w_in, m_a_gn_g, m_a_w_o, m_a_ffn_norm_g, m_a_w_gu,
                                    m_a_w_down, m_kv_norm_g, m_w_kv, m_k_norm_g, m_b_norm_g, m_b_w_q,
                                    m_b_q_norm_g, m_b_rel_bias, m_b_w_o, m_b_ffn_norm_g, m_b_w_gu, m_b_w_down)))
    v_all = dict(zip(WEIGHT_NAMES, (v_a_norm_g, v_a_w_in, v_a_gn_g, v_a_w_o, v_a_ffn_norm_g, v_a_w_gu,
                                    v_a_w_down, v_kv_norm_g, v_w_kv, v_k_norm_g, v_b_norm_g, v_b_w_q,
                                    v_b_q_norm_g, v_b_rel_bias, v_b_w_o, v_b_ffn_norm_g, v_b_w_gu, v_b_w_down)))
    xi, yi, ci = _my_place()
    me = 4 * xi + 2 * yi + ci
    big_names = [n for n, _ in BIG]
    axis_of = dict(BIG)

    big_local = {n: _big_shard(p[n], n) for n in big_names}
    shards = {n: a.astype(BF16) for n, a in big_local.items()}
    small_local = _pack_small({n: p[n] for n, _, _ in SMALL})
    w_in, small_all = _exchange(_GatherRider([shards["a_w_in"], small_local]), "gather_in")
    flat_g = small_all.reshape(N_DEV, -1)
    s_full, pos = {}, 0
    for n, length, sharded in SMALL:
        ln = length // N_DEV if sharded else length
        s_full[n] = flat_g[:, pos:pos + ln].reshape(1, -1) if sharded else p[n].reshape(1, -1)
        pos += ln

    parity = jnp.reshape(ci, (1,)).astype(jnp.int32)
    loss, grad_x, recv, g = _local_step(x[0], loss_target[0], shards, w_in, s_full, parity)

    partial = _pack_small({n: g[n] for n, _, _ in SMALL}, last=loss)
    summed = _sum_leading(_exchange(_GatherRider([partial]), "gather_gsmall")[0], "gsmall_sum")
    loss = summed[SMALL_ROWS - 1, SMALL_COLS - 1]
    g_small = _unpack_small(summed, local=False)
    for n, length, sharded in SMALL:
        if sharded:
            g_small[n] = lax.dynamic_slice(g_small[n], (me * (length // N_DEV),), (length // N_DEV,))

    grads, deltas, new_m, new_v = {}, {}, {}, {}
    for n in big_names:
        outs = _adamw(big_local[n], recv[n], _big_shard(m_all[n], n), _big_shard(v_all[n], n), "adamw_" + n)
        grads[n], deltas[n], new_m[n], new_v[n] = (_as_given(a, n, p[n].shape) for a in outs)
    pk = lambda src: _pack_small({n: src[n] for n, _, _ in SMALL})
    outs = _adamw(small_local, pk(g_small)[None], pk(m_all), pk(v_all), "adamw_small")
    g_s, d_s, nm_s, nv_s = (_unpack_small(a, local=True) for a in outs)
    for n, _, _ in SMALL:
        grads[n], deltas[n], new_m[n], new_v[n] = (a[n].reshape(p[n].shape) for a in (g_s, d_s, nm_s, nv_s))

    return (loss, grad_x[None], *[grads[n] for n in WEIGHT_NAMES], *[deltas[n] for n in WEIGHT_NAMES],
            *[new_m[n] for n in WEIGHT_NAMES], *[new_v[n] for n in WEIGHT_NAMES])
```

```python
import numpy as np
import jax
import jax.numpy as jnp
from jax import lax
from jax.experimental import pallas as pl
from jax.experimental.pallas import tpu as pltpu

F32 = jnp.float32
BF16 = jnp.bfloat16

N_DEV = 8
D_MODEL = 1024
CHUNK = 64
EPS = 1e-6
RET_HEADS, RET_DK, RET_DV = 4, 256, 512
RET_STEP = 4
RET_Q_COLS = RET_HEADS * RET_DK
RET_V_COLS = RET_HEADS * RET_DV
ATT_HEADS, ATT_DH = 16, 64
PAST_CHUNKS = 8
REL_CLIP = 256
REL_TABLE = 2 * REL_CLIP + 1
FFN_HIDDEN = 2816
ROPE_BASE = 10000.0
LANES = 128
Q_BLOCK = 256
ATT_SUBS = 4
ATT_ROWS = 32
K_PAD = PAST_CHUNKS * CHUNK
K_WINDOW = Q_BLOCK + K_PAD
REL_BLK = 128
REL_DELTAS = Q_BLOCK // REL_BLK + K_WINDOW // REL_BLK - 1
REL_PAD = 640
NEG = -1e30
VMEM_LIMIT_V7X = 56 * 1024 * 1024
ADAM_LR, ADAM_B1, ADAM_B2, ADAM_EPS, ADAM_WD, ADAM_STEP = 1e-3, 0.9, 0.999, 1e-8, 0.01, 10
MESH = pl.DeviceIdType.MESH
ANY = pl.BlockSpec(memory_space=pl.ANY)


def _params(*semantics):
    return pltpu.CompilerParams(dimension_semantics=semantics, vmem_limit_bytes=VMEM_LIMIT_V7X)


def _pick(dim, cap, align):
    best = None
    for t in range(align, min(dim, cap) + 1, align):
        if dim % t == 0:
            best = t
    assert best is not None, (dim, cap, align)
    return best


def _dot(a, b):
    return lax.dot_general(a, b, (((1,), (0,)), ((), ())), preferred_element_type=F32)


def _dot_nt(a, b):
    return lax.dot_general(a, b, (((1,), (1,)), ((), ())), preferred_element_type=F32)


def _dot_tn(a, b):
    return lax.dot_general(a, b, (((0,), (0,)), ((), ())), preferred_element_type=F32)


def _split2(x):
    hi = x.astype(BF16)
    lo = (x - hi.astype(F32)).astype(BF16)
    return hi, lo


def _split3(x):
    hi = x.astype(BF16)
    r = x - hi.astype(F32)
    mid = r.astype(BF16)
    lo = (r - mid.astype(F32)).astype(BF16)
    return hi, mid, lo


def _sigmoid(x):
    return 1.0 / (1.0 + jnp.exp(-x))


def _accumulate(ref, part, step):
    @pl.when(step == 0)
    def _():
        ref[...] = part

    @pl.when(step > 0)
    def _():
        ref[...] += part


RELAY_AT_NUM, RELAY_AT_DEN = 3, 4


def _my_place():
    return lax.axis_index("x"), lax.axis_index("y"), lax.axis_index("c")


def _flip(v, bit):
    return 1 - v if bit else v


class _NoRelay:
    def relay(self, in_refs, out_refs, sems):
        pass


class _GatherRider:
    def __init__(self, xs):
        self.inputs = list(xs)
        n = len(xs)
        self.out_shape = [jax.ShapeDtypeStruct((N_DEV,) + x.shape, x.dtype) for x in xs]
        self.scratch = [pltpu.SemaphoreType.DMA((7, n)), pltpu.SemaphoreType.DMA((7, n)),
                        pltpu.SemaphoreType.DMA((n,))]
        self.results = None

    def _copies(self, x_refs, out_refs, sems):
        send_sems, recv_sems, local_sems = sems
        n = len(x_refs)
        x, y, c = _my_place()
        me, sibling = (x, y, c), (x, y, 1 - c)
        chips = [(1 - x, y), (x, 1 - y), (1 - x, 1 - y)]

        def slot(a, px, py, pc):
            return out_refs[a].at[4 * px + 2 * py + pc]

        def copy(k, a, block, to, own=False):
            return pltpu.make_async_remote_copy(
                src_ref=x_refs[a] if own else slot(a, *block), dst_ref=slot(a, *block),
                send_sem=send_sems.at[k, a], recv_sem=recv_sems.at[k, a],
                device_id=to, device_id_type=MESH)

        mine = [pltpu.make_async_copy(x_refs[a], slot(a, *me), local_sems.at[a]) for a in range(n)]
        first = []
        for a in range(n):
            first.append(copy(0, a, me, sibling, own=True))
            first += [copy(1 + j, a, me, (*chip, c), own=True) for j, chip in enumerate(chips)]
        return n, c, me, sibling, chips, copy, mine, first

    def start(self, x_refs, out_refs, sems):
        _, _, _, _, _, _, mine, first = self._copies(x_refs, out_refs, sems)
        for cp in mine + first:
            cp.start()

    def relay(self, x_refs, out_refs, sems):
        n, c, me, sibling, chips, copy, _, _ = self._copies(x_refs, out_refs, sems)
        for j, chip in enumerate(chips):
            for a in range(n):
                copy(1 + j, a, (*chip, c), me).wait_recv()
                copy(4 + j, a, (*chip, c), sibling).start()

    def finish(self, x_refs, out_refs, sems):
        n, c, me, sibling, chips, copy, mine, first = self._copies(x_refs, out_refs, sems)
        passed = [copy(4 + j, a, (*chip, c), sibling) for j, chip in enumerate(chips) for a in range(n)]
        for a in range(n):
            copy(0, a, sibling, me).wait_recv()
            for j, chip in enumerate(chips):
                copy(4 + j, a, (*chip, 1 - c), me).wait_recv()
        for cp in first + passed:
            cp.wait_send()
        for cp in mine:
            cp.wait()


class _ScatterRider(_NoRelay):
    def __init__(self, gs):
        self.inputs = list(gs)
        n = len(gs)
        self.out_shape = [jax.ShapeDtypeStruct(g.shape, g.dtype) for g in gs]
        self.scratch = [pltpu.SemaphoreType.DMA((7, n)), pltpu.SemaphoreType.DMA((7, n)),
                        pltpu.SemaphoreType.DMA((n,))]
        self.results = None

    def _copies(self, g_refs, out_refs, sems):
        send_sems, recv_sems, local_sems = sems
        x, y, c = _my_place()
        me = 4 * x + 2 * y + c
        mine, copies = [], []
        for a in range(len(g_refs)):
            mine.append(pltpu.make_async_copy(g_refs[a].at[me], out_refs[a].at[me], local_sems.at[a]))
            for k in range(1, N_DEV):
                px, py, pc = _flip(x, k & 4), _flip(y, k & 2), _flip(c, k & 1)
                copies.append(pltpu.make_async_remote_copy(
                    src_ref=g_refs[a].at[4 * px + 2 * py + pc], dst_ref=out_refs[a].at[me],
                    send_sem=send_sems.at[k - 1, a], recv_sem=recv_sems.at[k - 1, a],
                    device_id=(px, py, pc), device_id_type=MESH))
        return mine, copies

    def start(self, g_refs, out_refs, sems):
        mine, copies = self._copies(g_refs, out_refs, sems)
        for cp in mine + copies:
            cp.start()

    def finish(self, g_refs, out_refs, sems):
        mine, copies = self._copies(g_refs, out_refs, sems)
        for cp in copies + mine:
            cp.wait()


class _SiblingSwapRider(_NoRelay):
    def __init__(self, gs):
        self.inputs = list(gs)
        n = len(gs)
        self.out_shape = [jax.ShapeDtypeStruct((4,) + g.shape[1:], g.dtype) for g in gs]
        self.scratch = [pltpu.SemaphoreType.DMA((4, n)), pltpu.SemaphoreType.DMA((4, n))]
        self.results = None

    def _copies(self, g_refs, out_refs, sems):
        send_sems, recv_sems = sems
        x, y, c = _my_place()
        return [pltpu.make_async_remote_copy(
            src_ref=g_refs[a].at[2 * k + 1 - c], dst_ref=out_refs[a].at[k],
            send_sem=send_sems.at[k, a], recv_sem=recv_sems.at[k, a],
            device_id=(x, y, 1 - c), device_id_type=MESH)
            for a in range(len(g_refs)) for k in range(4)]

    def start(self, g_refs, out_refs, sems):
        for cp in self._copies(g_refs, out_refs, sems):
            cp.start()

    def finish(self, g_refs, out_refs, sems):
        for cp in self._copies(g_refs, out_refs, sems):
            cp.wait()


class _ChipScatterRider(_NoRelay):
    def __init__(self, ps):
        self.inputs = list(ps)
        n = len(ps)
        self.out_shape = [jax.ShapeDtypeStruct(p.shape, p.dtype) for p in ps]
        self.scratch = [pltpu.SemaphoreType.DMA((3, n)), pltpu.SemaphoreType.DMA((3, n)),
                        pltpu.SemaphoreType.DMA((n,))]
        self.results = None

    def _copies(self, p_refs, out_refs, sems):
        send_sems, recv_sems, local_sems = sems
        x, y, c = _my_place()
        my_chip = 2 * x + y
        chips = [(1 - x, y), (x, 1 - y), (1 - x, 1 - y)]
        n = len(p_refs)
        mine = [pltpu.make_async_copy(p_refs[a].at[my_chip], out_refs[a].at[my_chip], local_sems.at[a])
                for a in range(n)]
        copies = [pltpu.make_async_remote_copy(
            src_ref=p_refs[a].at[2 * cx + cy], dst_ref=out_refs[a].at[my_chip],
            send_sem=send_sems.at[j, a], recv_sem=recv_sems.at[j, a],
            device_id=(cx, cy, c), device_id_type=MESH)
            for a in range(n) for j, (cx, cy) in enumerate(chips)]
        return mine, copies

    def start(self, p_refs, out_refs, sems):
        mine, copies = self._copies(p_refs, out_refs, sems)
        for cp in mine + copies:
            cp.start()

    def finish(self, p_refs, out_refs, sems):
        mine, copies = self._copies(p_refs, out_refs, sems)
        for cp in copies + mine:
            cp.wait()


def _call(body, name, grid, in_specs, out_specs, out_shape, scratch, semantics, args, rider=None):
    in_specs, out_specs, out_shape, scratch = list(in_specs), list(out_specs), list(out_shape), list(scratch)
    if rider is None:
        return list(pl.pallas_call(
            body, name=name, grid=grid, in_specs=in_specs, out_specs=out_specs, out_shape=out_shape,
            scratch_shapes=scratch, compiler_params=_params(*semantics))(*args))
    n_in, n_out, n_scr = len(in_specs), len(out_specs), len(scratch)
    r_in, r_out = len(rider.inputs), len(rider.out_shape)

    def wrapped(*refs):
        cuts = np.cumsum([0, n_in, r_in, n_out, r_out, n_scr])
        hi, ri, ho, ro, hs = (refs[cuts[i]:cuts[i + 1]] for i in range(5))
        rs = refs[cuts[5]:]
        step, steps = pl.program_id(0), grid[0]
        for d in range(1, len(grid)):
            step, steps = step * grid[d] + pl.program_id(d), steps * grid[d]

        @pl.when(step == 0)
        def _():
            rider.start(ri, ro, rs)

        body(*hi, *ho, *hs)

        @pl.when(step == (steps * RELAY_AT_NUM) // RELAY_AT_DEN)
        def _():
            rider.relay(ri, ro, rs)

        @pl.when(step == steps - 1)
        def _():
            rider.finish(ri, ro, rs)

    outs = pl.pallas_call(
        wrapped, name=name, grid=grid,
        in_specs=in_specs + [ANY] * r_in, out_specs=out_specs + [ANY] * r_out,
        out_shape=out_shape + rider.out_shape, scratch_shapes=scratch + rider.scratch,
        compiler_params=_params(*(["arbitrary"] * len(grid))),
    )(*args, *rider.inputs)
    rider.results = list(outs[n_out:])
    return list(outs[:n_out])


def _exchange(rider, name):
    r_in, r_out = len(rider.inputs), len(rider.out_shape)

    def body(*refs):
        ri, ro, rs = refs[:r_in], refs[r_in:r_in + r_out], refs[r_in + r_out:]
        rider.start(ri, ro, rs)
        rider.relay(ri, ro, rs)
        rider.finish(ri, ro, rs)

    return list(pl.pallas_call(
        body, name=name, in_specs=[ANY] * r_in, out_specs=[ANY] * r_out,
        out_shape=rider.out_shape, scratch_shapes=rider.scratch)(*rider.inputs))


MM_CAP_MN = 1024
MM_CAP_N = 1536
MM_CAP_K = 3072
MM_CAP_K_TOKENS = 2048
MM_CAP_K_RMS = 8192
MM_CAP_M_RMS = 512
NORM_ROWS = 256


def _mm(a, b, mode, name, out_dtype=F32, res=None, out_block=None, epilogue=None, extra=None, norm_g=None,
        norm_b=False, rider=None):
    a3, b3 = a.ndim == 3, b.ndim == 3
    um = un = uk = None
    if mode in ("nn", "nt"):
        if a3:
            m, uk = a.shape[1:]
            k = a.shape[0] * uk
        else:
            m, k = a.shape
    else:
        if a3:
            k, um = a.shape[1:]
            m = a.shape[0] * um
        else:
            k, m = a.shape
    if mode in ("nn", "tn"):
        if b3:
            kb, un = b.shape[1:]
            n = b.shape[0] * un
        else:
            kb, n = b.shape
        assert kb == k, (a.shape, b.shape, mode)
    else:
        if b3:
            n, ukb = b.shape[1:]
            assert b.shape[0] * ukb == k and uk in (None, ukb), (a.shape, b.shape, mode)
            uk = ukb
        else:
            n, kb = b.shape
            assert kb == k, (a.shape, b.shape, mode)
    if out_block is not None:
        assert un in (None, out_block)
        un = out_block

    def tile(dim, unit, cap, align):
        if unit is None:
            return _pick(dim, cap, align), 1
        c = max(1, cap // unit)
        while (dim // unit) % c:
            c -= 1
        return unit, c

    cap_m = 1408 if mode == "tn" else (MM_CAP_M_RMS if epilogue == "rms_bwd" else MM_CAP_MN)
    um, cm = tile(m, um, cap_m, 128 if mode == "tn" else 16)
    un, cn = tile(n, un, MM_CAP_N, 128)
    cap_k = MM_CAP_K_TOKENS if mode == "tn" else (MM_CAP_K_RMS if epilogue == "rms_bwd" else MM_CAP_K)
    uk, ck = tile(k, uk, cap_k, 128)
    if epilogue == "rms_bwd":
        assert mode != "tn" and n == D_MODEL and cm == cn == 1 and res is None and out_block is None
    if epilogue == "loss":
        assert n == D_MODEL and cm == cn == 1 and res is not None and out_block is None
    if norm_g is not None and norm_b:
        assert mode == "tn" and not b3 and n == D_MODEL and cn == 1
    elif norm_g is not None:
        assert not a3 and (m if mode == "tn" else k) == D_MODEL and (cm if mode == "tn" else ck) == 1
    if epilogue == "swiglu":
        assert res is None and ((mode == "nn" and b3 and out_block is None) or
                                (mode == "nt" and not b3 and out_block is not None))
        cn = 2
    if epilogue == "swiglu_bwd":
        assert mode == "nt" and out_block is not None and extra is not None and res is None
        cn = 1
    tm, tn, tk = cm * um, cn * un, ck * uk
    nk = k // tk
    dot = {"nn": _dot, "nt": _dot_nt, "tn": _dot_tn}[mode]
    half = n // un // 2
    blocked_out = out_block is not None or epilogue in ("swiglu", "swiglu_bwd")
    extras = [] if extra is None else (list(extra) if isinstance(extra, (tuple, list)) else [extra])

    def sl(idx, unit, count):
        return slice(None) if count == 1 else slice(idx * unit, (idx + 1) * unit)

    def body(*refs):
        a_ref, b_ref = refs[0], refs[1]
        pos = 2
        r_ref = ng_ref = None
        if res is not None:
            r_ref, pos = refs[pos], pos + 1
        e_refs, pos = refs[pos:pos + len(extras)], pos + len(extras)
        if norm_g is not None:
            ng_ref, pos = refs[pos], pos + 1
        outs, acc_ref = refs[pos:-1], refs[-1]
        kk = pl.program_id(2)

        def normed(x_ref):
            groups = []
            for r in range(0, x_ref.shape[0], NORM_ROWS):
                xv = x_ref[r:r + NORM_ROWS, :]
                rstd = lax.rsqrt(jnp.mean(xv * xv, axis=-1, keepdims=True) + EPS)
                groups.append((xv * rstd * ng_ref[...]).astype(BF16))
            return jnp.concatenate(groups, axis=0)

        def a_blk(mi, ki):
            if norm_g is not None and not norm_b:
                return normed(a_ref)
            if mode in ("nn", "nt"):
                return a_ref[ki] if a3 else a_ref[:, sl(ki, uk, ck)]
            return a_ref[mi] if a3 else a_ref[:, sl(mi, um, cm)]

        def b_blk(ki, ni):
            if norm_b:
                return normed(b_ref)
            if epilogue == "swiglu":
                return b_ref[ni, 0]
            if mode in ("nn", "tn"):
                return b_ref[ni] if b3 else b_ref[sl(ki, uk, ck), sl(ni, un, cn)]
            return b_ref[ki][sl(ni, un, cn), :] if b3 else b_ref[sl(ni, un, cn), sl(ki, uk, ck)]

        parts = {}
        for mi in range(cm):
            for ni in range(cn):
                part = None
                for ki in range(ck):
                    d = dot(a_blk(mi, ki).astype(BF16), b_blk(ki, ni).astype(BF16))
                    part = d if part is None else part + d
                parts[mi, ni] = part

        def finish(total):
            if epilogue == "swiglu":
                gate, up = total[0, 0], total[0, 1]
                outs[0][0, 0] = gate.astype(BF16)
                outs[0][1, 0] = up.astype(BF16)
                outs[1][0] = (gate * _sigmoid(gate) * up).astype(BF16)
                return
            if epilogue == "swiglu_bwd":
                dact = total[0, 0]
                gate, up = e_refs[0][0, 0].astype(F32), e_refs[0][1, 0].astype(F32)
                sg = _sigmoid(gate)
                outs[0][0, 0] = (dact * up * (sg * (1.0 + gate * (1.0 - sg)))).astype(BF16)
                outs[0][1, 0] = (dact * (gate * sg)).astype(BF16)
                return
            if epilogue == "rms_bwd":
                x_ref, g_ref, dres_ref = e_refs
                dh, dg = total[0, 0], None
                for r in range(0, tm, NORM_ROWS):
                    rows = slice(r, r + NORM_ROWS)
                    xv, dhv = x_ref[rows, :], dh[rows, :]
                    rstd = lax.rsqrt(jnp.mean(xv * xv, axis=-1, keepdims=True) + EPS)
                    xh = xv * rstd
                    dyg = dhv * g_ref[...]
                    c = jnp.mean(dyg * xh, axis=-1, keepdims=True)
                    outs[0][rows, :] = dres_ref[rows, :] + rstd * (dyg - xh * c)
                    part = jnp.sum(dhv * xh, axis=0, keepdims=True)
                    dg = part if dg is None else dg + part
                _accumulate(outs[1], dg, pl.program_id(0))
                return
            if epilogue == "loss":
                diff = r_ref[...] + total[0, 0] - e_refs[0][...]
                outs[0][...] = diff * (1.0 / n)
                sq = jnp.sum(jnp.sum(diff * diff, axis=-1, keepdims=True), axis=0, keepdims=True)
                _accumulate(outs[1], sq * (0.5 / n), pl.program_id(0))
                return
            for (mi, ni), val in total.items():
                rows, cols = sl(mi, um, cm), sl(ni, un, cn)
                if res is not None:
                    val = r_ref[rows, cols] + val
                if blocked_out:
                    outs[0][ni, rows] = val.astype(out_dtype)
                else:
                    outs[0][rows, cols] = val.astype(out_dtype)

        if nk == 1:
            finish(parts)
        else:
            @pl.when(kk == 0)
            def _():
                for (mi, ni), val in parts.items():
                    acc_ref[mi * cn + ni] = val

            @pl.when(jnp.logical_and(kk > 0, kk < nk - 1))
            def _():
                for (mi, ni), val in parts.items():
                    acc_ref[mi * cn + ni] += val

            @pl.when(kk == nk - 1)
            def _():
                finish({key: acc_ref[key[0] * cn + key[1]] + val for key, val in parts.items()})

    if mode in ("nn", "nt"):
        a_spec = (pl.BlockSpec((ck, tm, uk), lambda i, j, kk: (kk, i, 0)) if a3
                  else pl.BlockSpec((tm, tk), lambda i, j, kk: (i, kk)))
    else:
        a_spec = (pl.BlockSpec((cm, tk, um), lambda i, j, kk: (i, kk, 0)) if a3
                  else pl.BlockSpec((tk, tm), lambda i, j, kk: (kk, i)))
    pair_spec = pl.BlockSpec((2, 1, tm, un), lambda i, j, kk: (0, j, i, 0))
    row_spec = pl.BlockSpec((tm, tn), lambda i, j, kk: (i, 0))
    vec_spec = pl.BlockSpec((1, tn), lambda i, j, kk: (0, 0))
    if epilogue == "swiglu" and mode == "nn":
        b = b.reshape(2, half, k, un)
        b_spec = pl.BlockSpec((2, 1, tk, un), lambda i, j, kk: (0, j, kk, 0))
    elif epilogue == "swiglu":
        b = b.reshape(2, half, un, k)
        b_spec = pl.BlockSpec((2, 1, un, tk), lambda i, j, kk: (0, j, 0, kk))
    elif mode in ("nn", "tn"):
        b_spec = (pl.BlockSpec((cn, tk, un), lambda i, j, kk: (j, kk, 0)) if b3
                  else pl.BlockSpec((tk, tn), lambda i, j, kk: (kk, j)))
    else:
        b_spec = (pl.BlockSpec((ck, tn, uk), lambda i, j, kk: (kk, j, 0)) if b3
                  else pl.BlockSpec((tn, tk), lambda i, j, kk: (j, kk)))
    if epilogue == "swiglu":
        out_specs = [pair_spec, pl.BlockSpec((1, tm, un), lambda i, j, kk: (j, i, 0))]
        out_shape = [jax.ShapeDtypeStruct((2, half, m, un), BF16), jax.ShapeDtypeStruct((half, m, un), BF16)]
    elif epilogue == "swiglu_bwd":
        out_specs = [pair_spec]
        out_shape = [jax.ShapeDtypeStruct(extra.shape, BF16)]
    elif epilogue == "rms_bwd":
        out_specs = [row_spec, vec_spec]
        out_shape = [jax.ShapeDtypeStruct((m, n), F32), jax.ShapeDtypeStruct((1, n), F32)]
    elif epilogue == "loss":
        out_specs = [row_spec, pl.BlockSpec((1, 1), lambda i, j, kk: (0, 0))]
        out_shape = [jax.ShapeDtypeStruct((m, n), F32), jax.ShapeDtypeStruct((1, 1), F32)]
    elif blocked_out:
        out_specs = [pl.BlockSpec((cn, tm, un), lambda i, j, kk: (j, i, 0))]
        out_shape = [jax.ShapeDtypeStruct((n // un, m, un), out_dtype)]
    else:
        out_specs = [pl.BlockSpec((tm, tn), lambda i, j, kk: (i, j))]
        out_shape = [jax.ShapeDtypeStruct((m, n), out_dtype)]
    in_specs, args = [a_spec, b_spec], [a, b]
    if res is not None:
        in_specs.append(pl.BlockSpec((tm, tn), lambda i, j, kk: (i, j)))
        args.append(res)
    if epilogue == "swiglu_bwd":
        in_specs.append(pair_spec)
    elif epilogue == "rms_bwd":
        in_specs += [row_spec, vec_spec, row_spec]
    elif epilogue == "loss":
        in_specs.append(row_spec)
    args += extras
    if norm_g is not None:
        in_specs.append(pl.BlockSpec((1, D_MODEL), lambda i, j, kk: (0, 0)))
        args.append(norm_g)
    semantics = ("arbitrary",) * 3 if epilogue in ("rms_bwd", "loss") else ("parallel", "parallel", "arbitrary")
    out = _call(body, name, (m // tm, n // tn, nk), in_specs, out_specs, out_shape,
                [pltpu.VMEM((cm * cn, um, un), F32)], semantics, args, rider)
    return out if epilogue in ("swiglu", "rms_bwd", "loss") else out[0]


def _head_sums(v, ind):
    hi, lo = _split2(v)
    return _dot(hi, ind) + _dot(lo, ind)


def _head_spread(per_head, ind):
    hi, lo = _split2(per_head)
    return _dot_nt(hi, ind) + _dot_nt(lo, ind)


def _head_rstd(xv, ind):
    return _head_spread(lax.rsqrt(_head_sums(xv * xv, ind) * (1.0 / ATT_DH) + EPS), ind)


def _hn_bwd_math(xv, gv, ind, dyv, scale):
    rstd = _head_rstd(xv, ind)
    xh = xv * rstd
    dyn = dyv * scale
    dyg = dyn * gv
    dx = rstd * (dyg - xh * _head_spread(_head_sums(dyg * xh, ind) * (1.0 / ATT_DH), ind))
    return dx, jnp.sum(dyn * xh, axis=0, keepdims=True)


def _q_hnorm(x, g_tiled, bd, scale, name):
    t, d = x.shape
    tm = _pick(t, 512, 16)

    def body(x_ref, g_ref, bd_ref, o_ref):
        xv = x_ref[...]
        o_ref[...] = (xv * _head_rstd(xv, bd_ref[...]) * g_ref[...] * scale).astype(BF16)

    return pl.pallas_call(
        body, name=name, grid=(t // tm,),
        in_specs=[pl.BlockSpec((tm, d), lambda i: (i, 0)), pl.BlockSpec((1, d), lambda i: (0, 0)),
                  pl.BlockSpec((d, LANES), lambda i: (0, 0))],
        out_specs=pl.BlockSpec((tm, d), lambda i: (i, 0)),
        out_shape=jax.ShapeDtypeStruct((t, d), BF16),
        compiler_params=_params("parallel"),
    )(x, g_tiled, bd)


def _q_dhnorm(x, g_tiled, bd, dy, scale, name):
    t, d = x.shape
    tm = _pick(t, 512, 16)

    def body(x_ref, g_ref, bd_ref, dy_ref, dx_ref, dg_ref):
        dx, part = _hn_bwd_math(x_ref[...], g_ref[...], bd_ref[...], dy_ref[...], scale)
        dx_ref[...] = dx.astype(BF16)
        _accumulate(dg_ref, part, pl.program_id(0))

    row = pl.BlockSpec((tm, d), lambda i: (i, 0))
    vec = pl.BlockSpec((1, d), lambda i: (0, 0))
    return pl.pallas_call(
        body, name=name, grid=(t // tm,),
        in_specs=[row, vec, pl.BlockSpec((d, LANES), lambda i: (0, 0)), row],
        out_specs=[row, vec],
        out_shape=[jax.ShapeDtypeStruct((t, d), BF16), jax.ShapeDtypeStruct((1, d), F32)],
        compiler_params=_params("arbitrary"),
    )(x, g_tiled, bd, dy)


def _kv_prep(kv, g_tiled, bd, name):
    t = kv.shape[0]
    d = D_MODEL
    tm = K_PAD
    assert t % tm == 0

    def body(k_ref, v_ref, g_ref, bd_ref, kp_ref, vp_ref):
        i = pl.program_id(0)

        @pl.when(i == 0)
        def _():
            kp_ref[...] = jnp.zeros_like(kp_ref)
            vp_ref[...] = jnp.zeros_like(vp_ref)

        @pl.when(i > 0)
        def _():
            xv = k_ref[...]
            kp_ref[...] = (xv * _head_rstd(xv, bd_ref[...]) * g_ref[...]).astype(BF16)
            vp_ref[...] = v_ref[...].astype(BF16)

    shp = jax.ShapeDtypeStruct((t + K_PAD, d), BF16)
    out = pl.BlockSpec((tm, d), lambda i: (i, 0))
    return pl.pallas_call(
        body, name=name, grid=(t // tm + 1,),
        in_specs=[pl.BlockSpec((tm, d), lambda i: (jnp.maximum(i - 1, 0), 0)),
                  pl.BlockSpec((tm, d), lambda i: (jnp.maximum(i - 1, 0), 1)),
                  pl.BlockSpec((1, d), lambda i: (0, 0)), pl.BlockSpec((d, LANES), lambda i: (0, 0))],
        out_specs=[out, out], out_shape=[shp, shp],
        compiler_params=_params("arbitrary"),
    )(kv, kv, g_tiled, bd)


def _kv_dprep(kv, g_tiled, bd, dkp_t, dvp_t, name):
    t = kv.shape[0]
    d = D_MODEL
    tm = K_PAD

    def body(k_ref, g_ref, bd_ref, dk_ref, dv_ref, o_ref, dg_ref):
        dx, part = _hn_bwd_math(k_ref[...], g_ref[...], bd_ref[...], dk_ref[...].T, 1.0)
        o_ref[:, :d] = dx.astype(BF16)
        o_ref[:, d:] = dv_ref[...].T.astype(BF16)
        _accumulate(dg_ref, part, pl.program_id(0))

    vec = pl.BlockSpec((1, d), lambda i: (0, 0))
    padded = pl.BlockSpec((d, tm), lambda i: (0, i + 1))
    return pl.pallas_call(
        body, name=name, grid=(t // tm,),
        in_specs=[pl.BlockSpec((tm, d), lambda i: (i, 0)), vec, pl.BlockSpec((d, LANES), lambda i: (0, 0)),
                  padded, padded],
        out_specs=[pl.BlockSpec((tm, 2 * d), lambda i: (i, 0)), vec],
        out_shape=[jax.ShapeDtypeStruct((t, 2 * d), BF16), jax.ShapeDtypeStruct((1, d), F32)],
        compiler_params=_params("arbitrary"),
    )(kv, g_tiled, bd, dkp_t, dvp_t)


def _ret_consts(t):
    h = np.arange(RET_HEADS, dtype=np.float32)
    lg = np.log(np.float32(1.0) - np.float32(2.0) ** (np.float32(-5.0) - h)).astype(np.float32)
    tt = np.arange(CHUNK, dtype=np.float32)
    intra = np.exp(lg[:, None, None] * np.abs(tt[:, None] - tt[None, :])).astype(np.float32)
    q_dec = np.exp(lg[:, None] * (tt + 1.0)).astype(np.float32)
    k_dec = np.exp(lg[:, None] * (CHUNK - 1.0 - tt)).astype(np.float32)
    s_dec = [float(v) for v in np.exp(lg * np.float32(CHUNK)).astype(np.float32)]
    qd = np.broadcast_to(q_dec[:, :, None], (RET_HEADS, CHUNK, RET_DK)).copy()
    kd = np.broadcast_to(k_dec[:, :, None], (RET_HEADS, CHUNK, RET_DK)).copy()
    half = RET_DK // 2
    inv_freq = ROPE_BASE ** (-jnp.arange(half, dtype=F32) / half)
    ang = jnp.arange(t).astype(F32)[:, None] * inv_freq[None, :]
    return jnp.asarray(intra), jnp.asarray(qd), jnp.asarray(kd), s_dec, jnp.cos(ang), jnp.sin(ang)


def _rope(x, cos, sin):
    half = RET_DK // 2
    x1, x2 = x[:, :half], x[:, half:]
    return jnp.concatenate([x1 * cos - x2 * sin, x1 * sin + x2 * cos], axis=-1)


def _unrope(d, cos, sin):
    half = RET_DK // 2
    d1, d2 = d[:, :half], d[:, half:]
    return jnp.concatenate([d1 * cos + d2 * sin, d2 * cos - d1 * sin], axis=-1)


def _ret_slices(h):
    q = slice(h * RET_DK, (h + 1) * RET_DK)
    k = slice(RET_Q_COLS + h * RET_DK, RET_Q_COLS + (h + 1) * RET_DK)
    v = slice(2 * RET_Q_COLS + h * RET_DV, 2 * RET_Q_COLS + (h + 1) * RET_DV)
    g = slice(2 * RET_Q_COLS + RET_V_COLS + h * RET_DV, 2 * RET_Q_COLS + RET_V_COLS + (h + 1) * RET_DV)
    o = slice(h * RET_DV, (h + 1) * RET_DV)
    return q, k, v, g, o


def _ret_fwd(proj, gn, consts, name, rider=None):
    t, cols = proj.shape
    n = t // CHUNK
    intra, qd, kd, s_dec, cos, sin = consts
    k_scale = RET_DK ** -0.5

    def body(p_ref, cos_ref, sin_ref, intra_ref, qd_ref, kd_ref, gn_ref, y_ref, o_ref, st_ref, state):
        i = pl.program_id(0)

        @pl.when(i == 0)
        def _():
            state[...] = jnp.zeros_like(state)

        for c in range(RET_STEP):
            rows = slice(c * CHUNK, (c + 1) * CHUNK)
            cosv, sinv = cos_ref[rows, :], sin_ref[rows, :]
            for h in range(RET_HEADS):
                qs, ks, vs, gs, os_ = _ret_slices(h)
                qr = _rope(p_ref[rows, qs], cosv, sinv)
                kr = _rope(p_ref[rows, ks], cosv, sinv) * k_scale
                vb = p_ref[rows, vs].astype(BF16)
                gv = p_ref[rows, gs]
                scores = _dot_nt(qr.astype(BF16), kr.astype(BF16)) * intra_ref[h]
                s_old = state[h]
                s_old_b = s_old.astype(BF16)
                st_ref[c, h] = s_old_b
                o = _dot(scores.astype(BF16), vb) + _dot((qr * qd_ref[h]).astype(BF16), s_old_b)
                state[h] = s_old * s_dec[h] + _dot_tn((kr * kd_ref[h]).astype(BF16), vb)
                rstd = lax.rsqrt(jnp.mean(o * o, axis=-1, keepdims=True) + EPS)
                on = o * rstd * gn_ref[:, os_]
                o_ref[rows, os_] = o
                y_ref[rows, os_] = (gv * _sigmoid(gv) * on).astype(BF16)

    full3 = lambda a: pl.BlockSpec(a.shape, lambda i: (0, 0, 0))
    step = RET_STEP * CHUNK
    return _call(
        body, name, (n // RET_STEP,),
        [pl.BlockSpec((step, cols), lambda i: (i, 0)),
         pl.BlockSpec((step, RET_DK // 2), lambda i: (i, 0)),
         pl.BlockSpec((step, RET_DK // 2), lambda i: (i, 0)),
         full3(intra), full3(qd), full3(kd),
         pl.BlockSpec((1, RET_V_COLS), lambda i: (0, 0))],
        [pl.BlockSpec((step, RET_V_COLS), lambda i: (i, 0)),
         pl.BlockSpec((step, RET_V_COLS), lambda i: (i, 0)),
         pl.BlockSpec((RET_STEP, RET_HEADS, RET_DK, RET_DV), lambda i: (i, 0, 0, 0))],
        [jax.ShapeDtypeStruct((t, RET_V_COLS), BF16),
         jax.ShapeDtypeStruct((t, RET_V_COLS), F32),
         jax.ShapeDtypeStruct((n, RET_HEADS, RET_DK, RET_DV), BF16)],
        [pltpu.VMEM((RET_HEADS, RET_DK, RET_DV), F32)], ("arbitrary",),
        (proj, cos, sin, intra, qd, kd, gn), rider)


def _ret_bwd(proj, gn, o_saved, states, dy, consts, name, rider=None):
    t, cols = proj.shape
    n = t // CHUNK
    intra, qd, kd, s_dec, cos, sin = consts
    k_scale = RET_DK ** -0.5

    def body(p_ref, cos_ref, sin_ref, intra_ref, qd_ref, kd_ref, gn_ref, o_ref, st_ref, dy_ref,
             dp_ref, dgn_ref, dstate):
        i = pl.program_id(0)

        @pl.when(i == 0)
        def _():
            dstate[...] = jnp.zeros_like(dstate)

        dgn = None
        for c in reversed(range(RET_STEP)):
            rows = slice(c * CHUNK, (c + 1) * CHUNK)
            cosv, sinv = cos_ref[rows, :], sin_ref[rows, :]
            dgn_parts = []
            for h in range(RET_HEADS):
                qs, ks, vs, gs, os_ = _ret_slices(h)
                qr = _rope(p_ref[rows, qs], cosv, sinv)
                kr = _rope(p_ref[rows, ks], cosv, sinv) * k_scale
                qb, kb = qr.astype(BF16), kr.astype(BF16)
                vb = p_ref[rows, vs].astype(BF16)
                gv = p_ref[rows, gs]
                ov = o_ref[rows, os_]
                dyv = dy_ref[rows, os_]
                gnv = gn_ref[:, os_]
                sg = _sigmoid(gv)
                rstd = lax.rsqrt(jnp.mean(ov * ov, axis=-1, keepdims=True) + EPS)
                oh = ov * rstd
                d_on = dyv * (gv * sg)
                dg = dyv * (oh * gnv) * (sg * (1.0 + gv * (1.0 - sg)))
                dgn_parts.append(jnp.sum(d_on * oh, axis=0, keepdims=True))
                d_oh = d_on * gnv
                do = rstd * (d_oh - oh * jnp.mean(d_oh * oh, axis=-1, keepdims=True))
                dob = do.astype(BF16)
                mask = intra_ref[h]
                a_b = (_dot_nt(qb, kb) * mask).astype(BF16)
                da_b = (_dot_nt(dob, vb) * mask).astype(BF16)
                ds_new = dstate[h]
                ds_new_b = ds_new.astype(BF16)
                s_old_b = st_ref[c, h]
                qdv, kdv = qd_ref[h], kd_ref[h]
                dv = _dot_tn(a_b, dob) + _dot((kr * kdv).astype(BF16), ds_new_b)
                dqr = _dot(da_b, kb) + _dot_nt(dob, s_old_b) * qdv
                dkr = _dot_tn(da_b, qb) + _dot_nt(vb, ds_new_b) * kdv
                dstate[h] = ds_new * s_dec[h] + _dot_tn((qr * qdv).astype(BF16), dob)
                dp_ref[rows, qs] = _unrope(dqr, cosv, sinv).astype(BF16)
                dp_ref[rows, ks] = _unrope(dkr * k_scale, cosv, sinv).astype(BF16)
                dp_ref[rows, vs] = dv.astype(BF16)
                dp_ref[rows, gs] = dg.astype(BF16)
            part = jnp.concatenate(dgn_parts, axis=-1)
            dgn = part if dgn is None else dgn + part
        _accumulate(dgn_ref, dgn, i)

    steps = n // RET_STEP
    step = RET_STEP * CHUNK
    rev = lambda i: (steps - 1 - i, 0)
    full3 = lambda a: pl.BlockSpec(a.shape, lambda i: (0, 0, 0))
    return _call(
        body, name, (steps,),
        [pl.BlockSpec((step, cols), rev),
         pl.BlockSpec((step, RET_DK // 2), rev),
         pl.BlockSpec((step, RET_DK // 2), rev),
         full3(intra), full3(qd), full3(kd),
         pl.BlockSpec((1, RET_V_COLS), lambda i: (0, 0)),
         pl.BlockSpec((step, RET_V_COLS), rev),
         pl.BlockSpec((RET_STEP, RET_HEADS, RET_DK, RET_DV), lambda i: (steps - 1 - i, 0, 0, 0)),
         pl.BlockSpec((step, RET_V_COLS), rev)],
        [pl.BlockSpec((step, cols), rev),
         pl.BlockSpec((1, RET_V_COLS), lambda i: (0, 0))],
        [jax.ShapeDtypeStruct((t, cols), BF16),
         jax.ShapeDtypeStruct((1, RET_V_COLS), F32)],
        [pltpu.VMEM((RET_HEADS, RET_DK, RET_DV), F32)], ("arbitrary",),
        (proj, cos, sin, intra, qd, kd, gn, o_saved, states, dy), rider)


def _att_common(q_ref, kp_ref, vp_ref, sub):
    blk = pl.program_id(1) * ATT_SUBS + sub
    start = pl.multiple_of(blk * Q_BLOCK, Q_BLOCK)
    kw = kp_ref[pl.ds(start, K_WINDOW), :]
    vw = vp_ref[pl.ds(start, K_WINDOW), :]
    kvalid = blk * Q_BLOCK - K_PAD + lax.broadcasted_iota(jnp.int32, (1, K_WINDOW), 1) >= 0
    lane = lax.broadcasted_iota(jnp.int32, (1, LANES), 1)
    qrows = slice(sub * Q_BLOCK, (sub + 1) * Q_BLOCK)
    return start, qrows, q_ref[qrows, :], kw, vw, kvalid, (lane < ATT_DH, lane >= ATT_DH)


def _row_groups():
    return [slice(r * ATT_ROWS, (r + 1) * ATT_ROWS) for r in range(Q_BLOCK // ATT_ROWS)]


def _lane_copies(x):
    return jnp.tile(x, (1, K_WINDOW // LANES))


def _att_specs(t, tp):
    qspec = pl.BlockSpec((ATT_SUBS * Q_BLOCK, LANES), lambda h, i: (i, h))
    kspec = pl.BlockSpec((tp, LANES), lambda h, i: (0, h))
    bspec = pl.BlockSpec((2, Q_BLOCK, K_WINDOW), lambda h, i: (h, 0, 0))
    return qspec, kspec, bspec


def _att_fwd(q, kp, vp, bias, name, rider=None):
    t, d = q.shape
    tp = kp.shape[0]

    def body(q_ref, kp_ref, vp_ref, bias_ref, o_ref, lse_ref, s_scr, p_scr, lse_scr):
        for sub in range(ATT_SUBS):
            _, qrows, q2, kw, vw, kvalid, sel = _att_common(q_ref, kp_ref, vp_ref, sub)
            for hh in range(2):
                s_scr[sub, hh] = _dot_nt(jnp.where(sel[hh], q2, 0), kw)
            for hh in range(2):
                for rows in _row_groups():
                    s = jnp.where(kvalid, s_scr[sub, hh, rows, :] + bias_ref[hh, rows, :], NEG)
                    m = jnp.max(s, axis=-1, keepdims=True)
                    e = jnp.exp(s - m)
                    l = jnp.sum(e, axis=-1, keepdims=True)
                    p_scr[sub, hh, rows, :] = (e * (1.0 / l)).astype(BF16)
                    lse_scr[sub, hh, rows, :] = jnp.broadcast_to(m + jnp.log(l), (ATT_ROWS, LANES))
            outs = [_dot(p_scr[sub, hh], vw) for hh in range(2)]
            o_ref[qrows, :] = jnp.where(sel[0], outs[0], outs[1]).astype(BF16)
            lse_ref[qrows, :] = jnp.where(sel[0], lse_scr[sub, 0], lse_scr[sub, 1])

    qspec, kspec, bspec = _att_specs(t, tp)
    return _call(body, name, (d // LANES, t // (ATT_SUBS * Q_BLOCK)), [qspec, kspec, kspec, bspec], [qspec, qspec],
                 [jax.ShapeDtypeStruct((t, d), BF16), jax.ShapeDtypeStruct((t, d), F32)],
                 [pltpu.VMEM((ATT_SUBS, 2, Q_BLOCK, K_WINDOW), F32),
                  pltpu.VMEM((ATT_SUBS, 2, Q_BLOCK, K_WINDOW), BF16),
                  pltpu.VMEM((ATT_SUBS, 2, Q_BLOCK, LANES), F32)],
                 ("parallel", "arbitrary"), (q, kp, vp, bias), rider)


def _att_bwd(q, kp, vp, bias, do, o, lse, name, rider=None):
    t, d = q.shape
    tp = kp.shape[0]

    def body(q_ref, kp_ref, vp_ref, bias_ref, do_ref, o_ref, lse_ref, dq_ref, dkp_ref, dvp_ref, db_ref,
             s_scr, dp_scr, p_scr, ds_scr, row_scr):
        @pl.when(pl.program_id(1) == 0)
        def _():
            dkp_ref[...] = jnp.zeros_like(dkp_ref)
            dvp_ref[...] = jnp.zeros_like(dvp_ref)
            db_ref[...] = jnp.zeros_like(db_ref)

        for sub in range(ATT_SUBS):
            start, qrows, q2, kw, vw, kvalid, sel = _att_common(q_ref, kp_ref, vp_ref, sub)
            do2 = do_ref[qrows, :]
            qm = [jnp.where(sel[hh], q2, 0) for hh in range(2)]
            dom = [jnp.where(sel[hh], do2, 0) for hh in range(2)]
            do_o = do2.astype(F32) * o_ref[qrows, :].astype(F32)
            lse2 = lse_ref[qrows, :]
            for hh in range(2):
                s_scr[sub, hh] = _dot_nt(qm[hh], kw)
                dp_scr[sub, hh] = _dot_nt(dom[hh], vw)
                lse_h = jnp.max(jnp.where(sel[hh], lse2, NEG), axis=-1, keepdims=True)
                delta = jnp.sum(jnp.where(sel[hh], do_o, 0.0), axis=-1, keepdims=True)
                row_scr[sub, hh, 0] = jnp.broadcast_to(lse_h, (Q_BLOCK, LANES))
                row_scr[sub, hh, 1] = jnp.broadcast_to(delta, (Q_BLOCK, LANES))
            for hh in range(2):
                for rows in _row_groups():
                    s = jnp.where(kvalid, s_scr[sub, hh, rows, :] + bias_ref[hh, rows, :], NEG)
                    p = jnp.exp(s - _lane_copies(row_scr[sub, hh, 0, rows, :]))
                    ds = p * (dp_scr[sub, hh, rows, :] - _lane_copies(row_scr[sub, hh, 1, rows, :]))
                    db_ref[hh, rows, :] += ds
                    p_scr[sub, hh, rows, :] = p.astype(BF16)
                    ds_scr[sub, hh, rows, :] = ds.astype(BF16)
            dqs = [_dot(ds_scr[sub, hh], kw) for hh in range(2)]
            dq_ref[qrows, :] = jnp.where(sel[0], dqs[0], dqs[1])
            dkp_ref[:, pl.ds(start, K_WINDOW)] += (_dot_tn(qm[0], ds_scr[sub, 0]) +
                                                   _dot_tn(qm[1], ds_scr[sub, 1]))
            dvp_ref[:, pl.ds(start, K_WINDOW)] += (_dot_tn(dom[0], p_scr[sub, 0]) +
                                                   _dot_tn(dom[1], p_scr[sub, 1]))

    qspec, kspec, bspec = _att_specs(t, tp)
    tspec = pl.BlockSpec((LANES, tp), lambda h, i: (h, 0))
    stage = lambda dt: pltpu.VMEM((ATT_SUBS, 2, Q_BLOCK, K_WINDOW), dt)
    return _call(body, name, (d // LANES, t // (ATT_SUBS * Q_BLOCK)),
                 [qspec, kspec, kspec, bspec, qspec, qspec, qspec],
                 [qspec, tspec, tspec, bspec],
                 [jax.ShapeDtypeStruct((t, d), F32),
                  jax.ShapeDtypeStruct((d, tp), F32),
                  jax.ShapeDtypeStruct((d, tp), F32),
                  jax.ShapeDtypeStruct((ATT_HEADS, Q_BLOCK, K_WINDOW), F32)],
                 [stage(F32), stage(F32), stage(BF16), stage(BF16),
                  pltpu.VMEM((ATT_SUBS, 2, 2, Q_BLOCK, LANES), F32)],
                 ("parallel", "arbitrary"), (q, kp, vp, bias, do, o, lse), rider)


def _rel_bin_matrix():
    rows = REL_DELTAS * 2 * REL_BLK
    rho = lax.broadcasted_iota(jnp.int32, (rows, REL_PAD), 0)
    col = lax.broadcasted_iota(jnp.int32, (rows, REL_PAD), 1)
    assert 2 * REL_BLK == 256
    delta = rho >> 8
    c = 255 - (rho & 255)
    dist = K_PAD + REL_BLK * (delta - (K_WINDOW // REL_BLK - 1)) + (c - (REL_BLK - 1))
    idx = jnp.clip(dist, -REL_CLIP, REL_CLIP) + REL_CLIP
    return col == idx


def _rel_expand(rel_pad, name):
    heads = rel_pad.shape[0]
    rows = REL_DELTAS * 2 * REL_BLK

    def body_bin(r_ref, o_ref):
        onehot = jnp.where(_rel_bin_matrix(), 1.0, 0.0).astype(BF16)
        hi, mid, lo = _split3(r_ref[...])
        o_ref[...] = _dot_nt(hi, onehot) + _dot_nt(mid, onehot) + _dot_nt(lo, onehot)

    by_delta = pl.pallas_call(
        body_bin, name=name + "_bin",
        out_shape=jax.ShapeDtypeStruct((heads, rows), F32),
        compiler_params=pltpu.CompilerParams(vmem_limit_bytes=VMEM_LIMIT_V7X),
    )(rel_pad)
    by_delta = by_delta.reshape(heads * REL_DELTAS, 2 * REL_BLK)

    def body_shift(t_ref, o_ref):
        tv = t_ref[...]
        for r in range(REL_BLK):
            o_ref[r] = pltpu.roll(tv, (r + REL_BLK) % (2 * REL_BLK), 1)[:, :REL_BLK]

    return pl.pallas_call(
        body_shift, name=name + "_shift",
        out_shape=jax.ShapeDtypeStruct((REL_BLK, heads * REL_DELTAS, REL_BLK), F32),
        compiler_params=pltpu.CompilerParams(vmem_limit_bytes=VMEM_LIMIT_V7X),
    )(by_delta)


def _bias_table(rel_bias, name):
    heads = rel_bias.shape[0]
    rel_pad = jnp.pad(rel_bias, ((0, 0), (0, REL_PAD - REL_TABLE)))
    tiles = _rel_expand(rel_pad, name)
    tiles = tiles.reshape(REL_BLK, heads, REL_DELTAS, REL_BLK).transpose(1, 2, 0, 3)
    na, nb = Q_BLOCK // REL_BLK, K_WINDOW // REL_BLK
    rows = [jnp.concatenate([tiles[:, a - b + nb - 1] for b in range(nb)], axis=-1) for a in range(na)]
    table = jnp.concatenate(rows, axis=-2)
    qc = np.arange(Q_BLOCK)[:, None] // CHUNK
    kc = np.arange(K_WINDOW)[None, :] // CHUNK
    band = (kc >= qc) & (kc <= qc + PAST_CHUNKS)
    return jnp.where(jnp.asarray(band)[None], table, NEG)


def _rel_reduce(db, name):
    heads = db.shape[0]
    na, nb = Q_BLOCK // REL_BLK, K_WINDOW // REL_BLK

    fold_heads = 4

    def body_fold(db_ref, g_ref):
        for hd in range(fold_heads):
            for delta in range(REL_DELTAS):
                acc = None
                for a in range(na):
                    b = a - (delta - (nb - 1))
                    if 0 <= b < nb:
                        tile = db_ref[hd, a * REL_BLK:(a + 1) * REL_BLK, b * REL_BLK:(b + 1) * REL_BLK]
                        acc = tile if acc is None else acc + tile
                g_ref[hd, delta] = acc

    folded = pl.pallas_call(
        body_fold, name=name + "_fold", grid=(heads // fold_heads,),
        in_specs=[pl.BlockSpec((fold_heads, Q_BLOCK, K_WINDOW), lambda h: (h, 0, 0))],
        out_specs=pl.BlockSpec((fold_heads, REL_DELTAS, REL_BLK, REL_BLK), lambda h: (h, 0, 0, 0)),
        out_shape=jax.ShapeDtypeStruct((heads, REL_DELTAS, REL_BLK, REL_BLK), F32),
        compiler_params=_params("parallel"),
    )(db)
    by_row = folded.transpose(2, 0, 1, 3).reshape(REL_BLK, heads * REL_DELTAS, REL_BLK)

    def body_diag(g_ref, d_ref):
        zeros = jnp.zeros((heads * REL_DELTAS, REL_BLK), F32)
        acc = None
        for r in range(REL_BLK):
            part = pltpu.roll(jnp.concatenate([g_ref[r], zeros], axis=1), REL_BLK - r, 1)
            acc = part if acc is None else acc + part
        d_ref[...] = acc

    diag = pl.pallas_call(
        body_diag, name=name + "_diag",
        out_shape=jax.ShapeDtypeStruct((heads * REL_DELTAS, 2 * REL_BLK), F32),
        compiler_params=pltpu.CompilerParams(vmem_limit_bytes=VMEM_LIMIT_V7X),
    )(by_row)
    diag = diag.reshape(heads, REL_DELTAS * 2 * REL_BLK)

    def body_bin(d_ref, o_ref):
        onehot = jnp.where(_rel_bin_matrix(), 1.0, 0.0).astype(BF16)
        hi, mid, lo = _split3(d_ref[...])
        o_ref[...] = _dot(hi, onehot) + _dot(mid, onehot) + _dot(lo, onehot)

    out = pl.pallas_call(
        body_bin, name=name + "_bin",
        out_shape=jax.ShapeDtypeStruct((heads, REL_PAD), F32),
        compiler_params=pltpu.CompilerParams(vmem_limit_bytes=VMEM_LIMIT_V7X),
    )(diag)
    return out[:, :REL_TABLE]


def _sum_leading(x, name):
    n, r, c = x.shape
    tr = _pick(r, 256, 8)

    def body(x_ref, o_ref):
        acc = x_ref[0].astype(F32)
        for k in range(1, n):
            acc = acc + x_ref[k].astype(F32)
        o_ref[...] = acc

    return pl.pallas_call(
        body, name=name, grid=(r // tr,),
        in_specs=[pl.BlockSpec((n, tr, c), lambda i: (0, i, 0))],
        out_specs=pl.BlockSpec((tr, c), lambda i: (i, 0)),
        out_shape=jax.ShapeDtypeStruct((r, c), F32),
        compiler_params=_params("parallel"),
    )(x)


def _pair_add(g, recv, parity, name):
    _, r, c = g.shape
    tr = _pick(r, 256, 16)

    def body(par_ref, g_ref, r_ref, o_ref):
        o_ref[...] = (g_ref[...].astype(F32) + r_ref[...].astype(F32)).astype(BF16)

    return pl.pallas_call(
        body, name=name,
        grid_spec=pltpu.PrefetchScalarGridSpec(
            num_scalar_prefetch=1, grid=(4, r // tr),
            in_specs=[pl.BlockSpec((1, tr, c), lambda k, i, par: (2 * k + par[0], i, 0)),
                      pl.BlockSpec((1, tr, c), lambda k, i, par: (k, i, 0))],
            out_specs=pl.BlockSpec((1, tr, c), lambda k, i, par: (k, i, 0))),
        out_shape=jax.ShapeDtypeStruct((4, r, c), BF16),
        compiler_params=_params("parallel", "parallel"),
    )(parity, g, recv)


def _adamw(w, g_parts, m, v, name):
    r, c = w.shape
    n = g_parts.shape[0]
    tr = _pick(r, 256, 16 if g_parts.dtype == BF16 else 8)
    c1 = 1.0 - ADAM_B1 ** ADAM_STEP
    c2 = 1.0 - ADAM_B2 ** ADAM_STEP

    def body(w_ref, g_ref, m_ref, v_ref, go_ref, d_ref, nm_ref, nv_ref):
        gv = g_ref[0].astype(F32)
        for k in range(1, n):
            gv = gv + g_ref[k].astype(F32)
        nm = ADAM_B1 * m_ref[...] + (1.0 - ADAM_B1) * gv
        nv = ADAM_B2 * v_ref[...] + (1.0 - ADAM_B2) * (gv * gv)
        go_ref[...] = gv
        d_ref[...] = -ADAM_LR * ((nm / c1) / (jnp.sqrt(nv / c2) + ADAM_EPS) + ADAM_WD * w_ref[...])
        nm_ref[...] = nm
        nv_ref[...] = nv

    spec = pl.BlockSpec((tr, c), lambda i: (i, 0))
    shp = jax.ShapeDtypeStruct((r, c), F32)
    return pl.pallas_call(
        body, name=name, grid=(r // tr,),
        in_specs=[spec, pl.BlockSpec((n, tr, c), lambda i: (0, i, 0)), spec, spec],
        out_specs=[spec] * 4, out_shape=[shp] * 4,
        compiler_params=_params("parallel"),
    )(w, g_parts, m, v)


BIG = (("a_w_in", 1), ("a_w_o", 0), ("a_w_gu", 0), ("a_w_down", 0), ("w_kv", 1),
       ("b_w_q", 0), ("b_w_o", 0), ("b_w_gu", 0), ("b_w_down", 0))
TRANSPOSED = ("a_w_gu", "b_w_gu")
FFN_BLK = 2 * FFN_HIDDEN // N_DEV

SMALL = (("a_norm_g", D_MODEL, True), ("a_gn_g", RET_V_COLS, True), ("a_ffn_norm_g", D_MODEL, True),
         ("kv_norm_g", D_MODEL, False), ("b_norm_g", D_MODEL, False), ("b_ffn_norm_g", D_MODEL, False),
         ("k_norm_g", ATT_DH, False), ("b_q_norm_g", ATT_DH, False),
         ("b_rel_bias", ATT_HEADS * REL_TABLE, False))
SMALL_ROWS, SMALL_COLS = 16, 1024


def _pack_small(vals, last=None):
    flat = jnp.concatenate([vals[n].reshape(-1) for n, _, _ in SMALL])
    room = SMALL_ROWS * SMALL_COLS - flat.shape[0]
    if last is None:
        flat = jnp.pad(flat, (0, room))
    else:
        flat = jnp.concatenate([jnp.pad(flat, (0, room - 1)), last.reshape(1)])
    return flat.reshape(SMALL_ROWS, SMALL_COLS)


def _unpack_small(packed, local):
    flat, out, pos = packed.reshape(-1), {}, 0
    for n, length, sharded in SMALL:
        ln = length // N_DEV if (local and sharded) else length
        out[n] = flat[pos:pos + ln]
        pos += ln
    return out


def _gather_rider(shards, names):
    return _GatherRider([shards[n] for n in names])


def _gathered(rider, names, axis_of):
    return {n: (r.reshape(-1, r.shape[2]) if axis_of[n] == 0 else r) for n, r in zip(names, rider.results)}


def _blocks(g):
    return g if g.ndim == 3 else g.reshape(N_DEV, -1, g.shape[-1])


def _local_step(x, target, shards, w_in, s, parity):
    t = x.shape[0]
    axis_of = dict(BIG)
    consts = _ret_consts(t)
    lane_to_head = np.zeros((D_MODEL, LANES), np.float32)
    lane_to_head[np.arange(D_MODEL), np.arange(D_MODEL) // ATT_DH] = 1.0
    bd = jnp.asarray(lane_to_head).astype(BF16)
    kg_t = jnp.tile(s["k_norm_g"], (1, ATT_HEADS))
    qg_t = jnp.tile(s["b_q_norm_g"], (1, ATT_HEADS))
    q_scale = ATT_DH ** -0.5
    w = {"a_w_in": w_in}
    g, recv = {}, {}

    def gather_on(names):
        return _gather_rider(shards, names), names

    def landed(ride):
        w.update(_gathered(ride[0], ride[1], axis_of))

    halves = {}

    def part(key):
        name, _, half = key.partition(":")
        blocks = _blocks(g[name])
        if not half:
            return blocks
        r = blocks.shape[1] // 2
        return blocks[:, :r] if half == "0" else blocks[:, r:]

    def scatter_on(keys):
        return _ScatterRider([part(k) for k in keys]), keys

    def reduced(ride):
        for key, res in zip(ride[1], ride[0].results):
            name, _, half = key.partition(":")
            if not half:
                recv[name] = res
                continue
            halves.setdefault(name, {})[half] = res
            if len(halves[name]) == 2:
                recv[name] = jnp.concatenate([halves[name]["0"], halves[name]["1"]], axis=1)

    ride = gather_on(["a_w_o", "a_w_down"])
    proj = _mm(x, w["a_w_in"], "nn", "a_proj", norm_g=s["a_norm_g"], rider=ride[0])
    landed(ride)
    ride = gather_on(["a_w_gu", "w_kv"])
    y, o_ret, states = _ret_fwd(proj, s["a_gn_g"], consts, "a_ret", rider=ride[0])
    landed(ride)
    x1 = _mm(y, w["a_w_o"], "nn", "a_out", res=x)
    ride = gather_on(["b_w_q", "b_w_o"])
    gu_a, act_a = _mm(x1, w["a_w_gu"], "nt", "a_ffn_gu", epilogue="swiglu", out_block=FFN_BLK,
                      norm_g=s["a_ffn_norm_g"], rider=ride[0])
    landed(ride)
    x2 = _mm(act_a, w["a_w_down"], "nn", "a_ffn_down", res=x1)

    kv = _mm(x2, w["w_kv"], "nn", "kv_proj", norm_g=s["kv_norm_g"])
    kp, vp = _kv_prep(kv, kg_t, bd, "kv_prep")

    q_raw = _mm(x2, w["b_w_q"], "nn", "b_q", norm_g=s["b_norm_g"])
    qn = _q_hnorm(q_raw, qg_t, bd, q_scale, "q_hnorm")
    bias = _bias_table(s["b_rel_bias"].reshape(ATT_HEADS, REL_TABLE), "rel")
    ride = gather_on(["b_w_gu", "b_w_down"])
    o_att, lse = _att_fwd(qn, kp, vp, bias, "b_att", rider=ride[0])
    landed(ride)
    x3 = _mm(o_att, w["b_w_o"], "nn", "b_out", res=x2)
    gu_b, act_b = _mm(x3, w["b_w_gu"], "nt", "b_ffn_gu", epilogue="swiglu", out_block=FFN_BLK,
                      norm_g=s["b_ffn_norm_g"])
    dy, loss = _mm(act_b, w["b_w_down"], "nn", "b_ffn_down", res=x3, epilogue="loss", extra=(target,))
    in_blk, kv_blk, ffn_blk = w["a_w_in"].shape[2], w["w_kv"].shape[2], FFN_BLK

    dgu = _mm(dy, w["b_w_down"], "nt", "b_ffn_dgu", out_block=ffn_blk, epilogue="swiglu_bwd", extra=gu_b)
    dgu = dgu.reshape(N_DEV, t, ffn_blk)
    g["b_w_down"] = _mm(act_b, dy, "tn", "b_ffn_gdown", out_dtype=BF16)
    ride = scatter_on(["b_w_down:0"])
    dx3, g["b_ffn_norm_g"] = _mm(dgu, w["b_w_gu"], "nn", "b_ffn_dh", epilogue="rms_bwd",
                                 extra=(x3, s["b_ffn_norm_g"], dy), rider=ride[0])
    reduced(ride)
    ride = scatter_on(["b_w_down:1"])
    g["b_w_gu"] = _mm(dgu, x3, "tn", "b_ffn_ggu", out_dtype=BF16, norm_g=s["b_ffn_norm_g"], norm_b=True,
                      rider=ride[0])
    reduced(ride)

    do_att = _mm(dx3, w["b_w_o"], "nt", "b_dout", out_dtype=BF16)
    g["b_w_o"] = _mm(o_att, dx3, "tn", "b_gout", out_dtype=BF16)
    ride = scatter_on(["b_w_gu", "b_w_o"])
    dq, dkp, dvp, db = _att_bwd(qn, kp, vp, bias, do_att, o_att, lse, "b_datt", rider=ride[0])
    reduced(ride)
    g["b_rel_bias"] = _rel_reduce(db, "drel").reshape(1, -1)
    dq_raw, gq = _q_dhnorm(q_raw, qg_t, bd, dq, q_scale, "q_dhnorm")
    g["b_q_norm_g"] = gq.reshape(ATT_HEADS, ATT_DH).sum(axis=0, keepdims=True)
    g["b_w_q"] = _mm(x2, dq_raw, "tn", "b_gq", out_dtype=BF16, norm_g=s["b_norm_g"])
    dx2, g["b_norm_g"] = _mm(dq_raw, w["b_w_q"], "nt", "b_dq", epilogue="rms_bwd",
                             extra=(x2, s["b_norm_g"], dx3))

    dkv, gk = _kv_dprep(kv, kg_t, bd, dkp, dvp, "kv_dprep")
    g["k_norm_g"] = gk.reshape(ATT_HEADS, ATT_DH).sum(axis=0, keepdims=True)
    g["w_kv"] = _mm(x2, dkv, "tn", "kv_g", out_dtype=BF16, out_block=kv_blk, norm_g=s["kv_norm_g"])
    dx2, g["kv_norm_g"] = _mm(dkv, w["w_kv"], "nt", "kv_du", epilogue="rms_bwd",
                              extra=(x2, s["kv_norm_g"], dx2))

    ride = scatter_on(["b_w_q", "w_kv:0"])
    dgu = _mm(dx2, w["a_w_down"], "nt", "a_ffn_dgu", out_block=ffn_blk, epilogue="swiglu_bwd", extra=gu_a,
              rider=ride[0])
    reduced(ride)
    dgu = dgu.reshape(N_DEV, t, ffn_blk)
    g["a_w_down"] = _mm(act_a, dx2, "tn", "a_ffn_gdown", out_dtype=BF16)
    ride = scatter_on(["a_w_down:0", "w_kv:1"])
    dx1, g["a_ffn_norm_g"] = _mm(dgu, w["a_w_gu"], "nn", "a_ffn_dh", epilogue="rms_bwd",
                                 extra=(x1, s["a_ffn_norm_g"], dx2), rider=ride[0])
    reduced(ride)
    ride = scatter_on(["a_w_down:1"])
    g["a_w_gu"] = _mm(dgu, x1, "tn", "a_ffn_ggu", out_dtype=BF16, norm_g=s["a_ffn_norm_g"], norm_b=True,
                      rider=ride[0])
    reduced(ride)

    dy_ret = _mm(dx1, w["a_w_o"], "nt", "a_dout")
    g["a_w_o"] = _mm(y, dx1, "tn", "a_gout", out_dtype=BF16)
    ride = scatter_on(["a_w_gu"])
    dproj, g["a_gn_g"] = _ret_bwd(proj, s["a_gn_g"], o_ret, states, dy_ret, consts, "a_dret", rider=ride[0])
    reduced(ride)
    ride = scatter_on(["a_w_o"])
    g["a_w_in"] = _mm(x, dproj, "tn", "a_gin", out_dtype=BF16, out_block=in_blk, norm_g=s["a_norm_g"],
                      rider=ride[0])
    reduced(ride)
    from_sibling = _exchange(_SiblingSwapRider([g["a_w_in"]]), "rs_sibling")[0]
    chip_sums = _pair_add(g["a_w_in"], from_sibling, parity, "rs_pair_add")
    last = _ChipScatterRider([chip_sums])
    grad_x, g["a_norm_g"] = _mm(dproj, w["a_w_in"], "nt", "a_dproj", epilogue="rms_bwd",
                                extra=(x, s["a_norm_g"], dx1), rider=last)
    recv["a_w_in"] = last.results[0]
    return loss, grad_x, recv, g


ARG_NAMES = ("x", "a_norm_g", "a_w_in", "a_gn_g", "a_w_o", "a_ffn_norm_g", "a_w_gu", "a_w_down",
             "kv_norm_g", "w_kv", "k_norm_g", "b_norm_g", "b_w_q", "b_q_norm_g", "b_rel_bias", "b_w_o",
             "b_ffn_norm_g", "b_w_gu", "b_w_down")
WEIGHT_NAMES = ARG_NAMES[1:]


def _big_shard(a, name):
    a = a[0] if a.ndim == 3 else a
    return a.T if name in TRANSPOSED else a


def _as_given(a, name, shape):
    return (a.T if name in TRANSPOSED else a).reshape(shape)


def kernel(x, a_norm_g, a_w_in, a_gn_g, a_w_o, a_ffn_norm_g, a_w_gu, a_w_down, kv_norm_g, w_kv, k_norm_g, b_norm_g, b_w_q, b_q_norm_g, b_rel_bias, b_w_o, b_ffn_norm_g, b_w_gu, b_w_down, loss_target, m_a_norm_g, m_a_w_in, m_a_gn_g, m_a_w_o, m_a_ffn_norm_g, m_a_w_gu, m_a_w_down, m_kv_norm_g, m_w_kv, m_k_norm_g, m_b_norm_g, m_b_w_q, m_b_q_norm_g, m_b_rel_bias, m_b_w_o, m_b_ffn_norm_g, m_b_w_gu, m_b_w_down, v_a_norm_g, v_a_w_in, v_a_gn_g, v_a_w_o, v_a_ffn_norm_g, v_a_w_gu, v_a_w_down, v_kv_norm_g, v_w_kv, v_k_norm_g, v_b_norm_g, v_b_w_q, v_b_q_norm_g, v_b_rel_bias, v_b_w_o, v_b_ffn_norm_g, v_b_w_gu, v_b_w_down):
    args = (x, a_norm_g, a_w_in, a_gn_g, a_w_o, a_ffn_norm_g, a_w_gu, a_w_down, kv_norm_g, w_kv, k_norm_g,
            b_norm_g, b_w_q, b_q_norm_g, b_rel_bias, b_w_o, b_ffn_norm_g, b_w_gu, b_w_down)
    p = dict(zip(ARG_NAMES, args))
    m_all = dict(zip(WEIGHT_NAMES, (m_a_norm_g, m_a_w_in, m_a_gn_g, m_a_w_o, m_a_ffn_norm_g, m_a_w_gu,
                                    m_a_w_down, m_kv_norm_g, m_w_kv, m_k_norm_g, m_b_norm_g, m_b_w_q,
                                    m_b_q_norm_g, m_b_rel_bias, m_b_w_o, m_b_ffn_norm_g, m_b_w_gu, m_b_w_down)))
    v_all = dict(zip(WEIGHT_NAMES, (v_a_norm_g, v_a_w_in, v_a_gn_g, v_a_w_o, v_a_ffn_norm_g, v_a_w_gu,
                                    v_a_w_down, v_kv_norm_g, v_w_kv, v_k_norm_g, v_b_norm_g, v_b_w_q,
                                    v_b_q_norm_g, v_b_rel_bias, v_b_w_o, v_b_ffn_norm_g, v_b_w_gu, v_b_w_down)))
    xi, yi, ci = _my_place()
    me = 4 * xi + 2 * yi + ci
    big_names = [n for n, _ in BIG]
    axis_of = dict(BIG)

    big_local = {n: _big_shard(p[n], n) for n in big_names}
    shards = {n: a.astype(BF16) for n, a in big_local.items()}
    small_local = _pack_small({n: p[n] for n, _, _ in SMALL})
    w_in, small_all = _exchange(_GatherRider([shards["a_w_in"], small_local]), "gather_in")
    flat_g = small_all.reshape(N_DEV, -1)
    s_full, pos = {}, 0
    for n, length, sharded in SMALL:
        ln = length // N_DEV if sharded else length
        s_full[n] = flat_g[:, pos:pos + ln].reshape(1, -1) if sharded else p[n].reshape(1, -1)
        pos += ln

    parity = jnp.reshape(ci, (1,)).astype(jnp.int32)
    loss, grad_x, recv, g = _local_step(x[0], loss_target[0], shards, w_in, s_full, parity)

    partial = _pack_small({n: g[n] for n, _, _ in SMALL}, last=loss)
    summed = _sum_leading(_exchange(_GatherRider([partial]), "gather_gsmall")[0], "gsmall_sum")
    loss = summed[SMALL_ROWS - 1, SMALL_COLS - 1]
    g_small = _unpack_small(summed, local=False)
    for n, length, sharded in SMALL:
        if sharded:
            g_small[n] = lax.dynamic_slice(g_small[n], (me * (length // N_DEV),), (length // N_DEV,))

    grads, deltas, new_m, new_v = {}, {}, {}, {}
    for n in big_names:
        outs = _adamw(big_local[n], recv[n], _big_shard(m_all[n], n), _big_shard(v_all[n], n), "adamw_" + n)
        grads[n], deltas[n], new_m[n], new_v[n] = (_as_given(a, n, p[n].shape) for a in outs)
    pk = lambda src: _pack_small({n: src[n] for n, _, _ in SMALL})
    outs = _adamw(small_local, pk(g_small)[None], pk(m_all), pk(v_all), "adamw_small")
    g_s, d_s, nm_s, nv_s = (_unpack_small(a, local=True) for a in outs)
    for n, _, _ in SMALL:
        grads[n], deltas[n], new_m[n], new_v[n] = (a[n].reshape(p[n].shape) for a in (g_s, d_s, nm_s, nv_s))

    return (loss, grad_x[None], *[grads[n] for n in WEIGHT_NAMES], *[deltas[n] for n in WEIGHT_NAMES],
            *[new_m[n] for n in WEIGHT_NAMES], *[new_v[n] for n in WEIGHT_NAMES])
```

```python
import numpy as np
import jax
import jax.numpy as jnp
from jax import lax
from jax.experimental import pallas as pl
from jax.experimental.pallas import tpu as pltpu

F32 = jnp.float32
BF16 = jnp.bfloat16

N_DEV = 8
D_MODEL = 1024
CHUNK = 64
EPS = 1e-6
RET_HEADS, RET_DK, RET_DV = 4, 256, 512
RET_STEP = 4
RET_Q_COLS = RET_HEADS * RET_DK
RET_V_COLS = RET_HEADS * RET_DV
ATT_HEADS, ATT_DH = 16, 64
PAST_CHUNKS = 8
REL_CLIP = 256
REL_TABLE = 2 * REL_CLIP + 1
FFN_HIDDEN = 2816
ROPE_BASE = 10000.0
LANES = 128
Q_BLOCK = 256
ATT_SUBS = 4
ATT_ROWS = 32
K_PAD = PAST_CHUNKS * CHUNK
K_WINDOW = Q_BLOCK + K_PAD
REL_BLK = 128
REL_DELTAS = Q_BLOCK // REL_BLK + K_WINDOW // REL_BLK - 1
REL_PAD = 640
NEG = -1e30
VMEM_LIMIT_V7X = 56 * 1024 * 1024
ADAM_LR, ADAM_B1, ADAM_B2, ADAM_EPS, ADAM_WD, ADAM_STEP = 1e-3, 0.9, 0.999, 1e-8, 0.01, 10
MESH = pl.DeviceIdType.MESH
ANY = pl.BlockSpec(memory_space=pl.ANY)


def _params(*semantics):
    return pltpu.CompilerParams(dimension_semantics=semantics, vmem_limit_bytes=VMEM_LIMIT_V7X)


def _pick(dim, cap, align):
    best = None
    for t in range(align, min(dim, cap) + 1, align):
        if dim % t == 0:
            best = t
    assert best is not None, (dim, cap, align)
    return best


def _dot(a, b):
    return lax.dot_general(a, b, (((1,), (0,)), ((), ())), preferred_element_type=F32)


def _dot_nt(a, b):
    return lax.dot_general(a, b, (((1,), (1,)), ((), ())), preferred_element_type=F32)


def _dot_tn(a, b):
    return lax.dot_general(a, b, (((0,), (0,)), ((), ())), preferred_element_type=F32)


def _split2(x):
    hi = x.astype(BF16)
    lo = (x - hi.astype(F32)).astype(BF16)
    return hi, lo


def _split3(x):
    hi = x.astype(BF16)
    r = x - hi.astype(F32)
    mid = r.astype(BF16)
    lo = (r - mid.astype(F32)).astype(BF16)
    return hi, mid, lo


def _sigmoid(x):
    return 1.0 / (1.0 + jnp.exp(-x))


def _accumulate(ref, part, step):
    @pl.when(step == 0)
    def _():
        ref[...] = part

    @pl.when(step > 0)
    def _():
        ref[...] += part


RELAY_AT_NUM, RELAY_AT_DEN = 3, 4


def _my_place():
    return lax.axis_index("x"), lax.axis_index("y"), lax.axis_index("c")


def _flip(v, bit):
    return 1 - v if bit else v


class _NoRelay:
    def relay(self, in_refs, out_refs, sems):
        pass


class _GatherRider:
    def __init__(self, xs):
        self.inputs = list(xs)
        n = len(xs)
        self.out_shape = [jax.ShapeDtypeStruct((N_DEV,) + x.shape, x.dtype) for x in xs]
        self.scratch = [pltpu.SemaphoreType.DMA((7, n)), pltpu.SemaphoreType.DMA((7, n)),
                        pltpu.SemaphoreType.DMA((n,))]
        self.results = None

    def _copies(self, x_refs, out_refs, sems):
        send_sems, recv_sems, local_sems = sems
        n = len(x_refs)
        x, y, c = _my_place()
        me, sibling = (x, y, c), (x, y, 1 - c)
        chips = [(1 - x, y), (x, 1 - y), (1 - x, 1 - y)]

        def slot(a, px, py, pc):
            return out_refs[a].at[4 * px + 2 * py + pc]

        def copy(k, a, block, to, own=False):
            return pltpu.make_async_remote_copy(
                src_ref=x_refs[a] if own else slot(a, *block), dst_ref=slot(a, *block),
                send_sem=send_sems.at[k, a], recv_sem=recv_sems.at[k, a],
                device_id=to, device_id_type=MESH)

        mine = [pltpu.make_async_copy(x_refs[a], slot(a, *me), local_sems.at[a]) for a in range(n)]
        first = []
        for a in range(n):
            first.append(copy(0, a, me, sibling, own=True))
            first += [copy(1 + j, a, me, (*chip, c), own=True) for j, chip in enumerate(chips)]
        return n, c, me, sibling, chips, copy, mine, first

    def start(self, x_refs, out_refs, sems):
        _, _, _, _, _, _, mine, first = self._copies(x_refs, out_refs, sems)
        for cp in mine + first:
            cp.start()

    def relay(self, x_refs, out_refs, sems):
        n, c, me, sibling, chips, copy, _, _ = self._copies(x_refs, out_refs, sems)
        for j, chip in enumerate(chips):
            for a in range(n):
                copy(1 + j, a, (*chip, c), me).wait_recv()
                copy(4 + j, a, (*chip, c), sibling).start()

    def finish(self, x_refs, out_refs, sems):
        n, c, me, sibling, chips, copy, mine, first = self._copies(x_refs, out_refs, sems)
        passed = [copy(4 + j, a, (*chip, c), sibling) for j, chip in enumerate(chips) for a in range(n)]
        for a in range(n):
            copy(0, a, sibling, me).wait_recv()
            for j, chip in enumerate(chips):
                copy(4 + j, a, (*chip, 1 - c), me).wait_recv()
        for cp in first + passed:
            cp.wait_send()
        for cp in mine:
            cp.wait()


class _ScatterRider(_NoRelay):
    def __init__(self, gs):
        self.inputs = list(gs)
        n = len(gs)
        self.out_shape = [jax.ShapeDtypeStruct(g.shape, g.dtype) for g in gs]
        self.scratch = [pltpu.SemaphoreType.DMA((7, n)), pltpu.SemaphoreType.DMA((7, n)),
                        pltpu.SemaphoreType.DMA((n,))]
        self.results = None

    def _copies(self, g_refs, out_refs, sems):
        send_sems, recv_sems, local_sems = sems
        x, y, c = _my_place()
        me = 4 * x + 2 * y + c
        mine, copies = [], []
        for a in range(len(g_refs)):
            mine.append(pltpu.make_async_copy(g_refs[a].at[me], out_refs[a].at[me], local_sems.at[a]))
            for k in range(1, N_DEV):
                px, py, pc = _flip(x, k & 4), _flip(y, k & 2), _flip(c, k & 1)
                copies.append(pltpu.make_async_remote_copy(
                    src_ref=g_refs[a].at[4 * px + 2 * py + pc], dst_ref=out_refs[a].at[me],
                    send_sem=send_sems.at[k - 1, a], recv_sem=recv_sems.at[k - 1, a],
                    device_id=(px, py, pc), device_id_type=MESH))
        return mine, copies

    def start(self, g_refs, out_refs, sems):
        mine, copies = self._copies(g_refs, out_refs, sems)
        for cp in mine + copies:
            cp.start()

    def finish(self, g_refs, out_refs, sems):
        mine, copies = self._copies(g_refs, out_refs, sems)
        for cp in copies + mine:
            cp.wait()


class _SiblingSwapRider(_NoRelay):
    def __init__(self, gs):
        self.inputs = list(gs)
        n = len(gs)
        self.out_shape = [jax.ShapeDtypeStruct((4,) + g.shape[1:], g.dtype) for g in gs]
        self.scratch = [pltpu.SemaphoreType.DMA((4, n)), pltpu.SemaphoreType.DMA((4, n))]
        self.results = None

    def _copies(self, g_refs, out_refs, sems):
        send_sems, recv_sems = sems
        x, y, c = _my_place()
        return [pltpu.make_async_remote_copy(
            src_ref=g_refs[a].at[2 * k + 1 - c], dst_ref=out_refs[a].at[k],
            send_sem=send_sems.at[k, a], recv_sem=recv_sems.at[k, a],
            device_id=(x, y, 1 - c), device_id_type=MESH)
            for a in range(len(g_refs)) for k in range(4)]

    def start(self, g_refs, out_refs, sems):
        for cp in self._copies(g_refs, out_refs, sems):
            cp.start()

    def finish(self, g_refs, out_refs, sems):
        for cp in self._copies(g_refs, out_refs, sems):
            cp.wait()


class _ChipScatterRider(_NoRelay):
    def __init__(self, ps):
        self.inputs = list(ps)
        n = len(ps)
        self.out_shape = [jax.ShapeDtypeStruct(p.shape, p.dtype) for p in ps]
        self.scratch = [pltpu.SemaphoreType.DMA((3, n)), pltpu.SemaphoreType.DMA((3, n)),
                        pltpu.SemaphoreType.DMA((n,))]
        self.results = None

    def _copies(self, p_refs, out_refs, sems):
        send_sems, recv_sems, local_sems = sems
        x, y, c = _my_place()
        my_chip = 2 * x + y
        chips = [(1 - x, y), (x, 1 - y), (1 - x, 1 - y)]
        n = len(p_refs)
        mine = [pltpu.make_async_copy(p_refs[a].at[my_chip], out_refs[a].at[my_chip], local_sems.at[a])
                for a in range(n)]
        copies = [pltpu.make_async_remote_copy(
            src_ref=p_refs[a].at[2 * cx + cy], dst_ref=out_refs[a].at[my_chip],
            send_sem=send_sems.at[j, a], recv_sem=recv_sems.at[j, a],
            device_id=(cx, cy, c), device_id_type=MESH)
            for a in range(n) for j, (cx, cy) in enumerate(chips)]
        return mine, copies

    def start(self, p_refs, out_refs, sems):
        mine, copies = self._copies(p_refs, out_refs, sems)
        for cp in mine + copies:
            cp.start()

    def finish(self, p_refs, out_refs, sems):
        mine, copies = self._copies(p_refs, out_refs, sems)
        for cp in copies + mine:
            cp.wait()


def _call(body, name, grid, in_specs, out_specs, out_shape, scratch, semantics, args, rider=None):
    in_specs, out_specs, out_shape, scratch = list(in_specs), list(out_specs), list(out_shape), list(scratch)
    if rider is None:
        return list(pl.pallas_call(
            body, name=name, grid=grid, in_specs=in_specs, out_specs=out_specs, out_shape=out_shape,
            scratch_shapes=scratch, compiler_params=_params(*semantics))(*args))
    n_in, n_out, n_scr = len(in_specs), len(out_specs), len(scratch)
    r_in, r_out = len(rider.inputs), len(rider.out_shape)

    def wrapped(*refs):
        cuts = np.cumsum([0, n_in, r_in, n_out, r_out, n_scr])
        hi, ri, ho, ro, hs = (refs[cuts[i]:cuts[i + 1]] for i in range(5))
        rs = refs[cuts[5]:]
        step, steps = pl.program_id(0), grid[0]
        for d in range(1, len(grid)):
            step, steps = step * grid[d] + pl.program_id(d), steps * grid[d]

        @pl.when(step == 0)
        def _():
            rider.start(ri, ro, rs)

        body(*hi, *ho, *hs)

        @pl.when(step == (steps * RELAY_AT_NUM) // RELAY_AT_DEN)
        def _():
            rider.relay(ri, ro, rs)

        @pl.when(step == steps - 1)
        def _():
            rider.finish(ri, ro, rs)

    outs = pl.pallas_call(
        wrapped, name=name, grid=grid,
        in_specs=in_specs + [ANY] * r_in, out_specs=out_specs + [ANY] * r_out,
        out_shape=out_shape + rider.out_shape, scratch_shapes=scratch + rider.scratch,
        compiler_params=_params(*(["arbitrary"] * len(grid))),
    )(*args, *rider.inputs)
    rider.results = list(outs[n_out:])
    return list(outs[:n_out])


def _exchange(rider, name):
    r_in, r_out = len(rider.inputs), len(rider.out_shape)

    def body(*refs):
        ri, ro, rs = refs[:r_in], refs[r_in:r_in + r_out], refs[r_in + r_out:]
        rider.start(ri, ro, rs)
        rider.relay(ri, ro, rs)
        rider.finish(ri, ro, rs)

    return list(pl.pallas_call(
        body, name=name, in_specs=[ANY] * r_in, out_specs=[ANY] * r_out,
        out_shape=rider.out_shape, scratch_shapes=rider.scratch)(*rider.inputs))


MM_CAP_MN = 1024
MM_CAP_M_GRAD = 1408
MM_CAP_N = 1536
MM_CAP_K = 3072
MM_CAP_K_TOKENS = 2048
MM_CAP_K_RMS = 8192
MM_CAP_M_RMS = 512
NORM_ROWS = 256


def _mm(a, b, mode, name, out_dtype=F32, res=None, out_block=None, epilogue=None, extra=None, norm_g=None,
        norm_b=False, rider=None):
    a3, b3 = a.ndim == 3, b.ndim == 3
    um = un = uk = None
    if mode in ("nn", "nt"):
        if a3:
            m, uk = a.shape[1:]
            k = a.shape[0] * uk
        else:
            m, k = a.shape
    else:
        if a3:
            k, um = a.shape[1:]
            m = a.shape[0] * um
        else:
            k, m = a.shape
    if mode in ("nn", "tn"):
        if b3:
            kb, un = b.shape[1:]
            n = b.shape[0] * un
        else:
            kb, n = b.shape
        assert kb == k, (a.shape, b.shape, mode)
    else:
        if b3:
            n, ukb = b.shape[1:]
            assert b.shape[0] * ukb == k and uk in (None, ukb), (a.shape, b.shape, mode)
            uk = ukb
        else:
            n, kb = b.shape
            assert kb == k, (a.shape, b.shape, mode)
    if out_block is not None:
        assert un in (None, out_block)
        un = out_block

    def tile(dim, unit, cap, align):
        if unit is None:
            return _pick(dim, cap, align), 1
        c = max(1, cap // unit)
        while (dim // unit) % c:
            c -= 1
        return unit, c

    cap_m = MM_CAP_M_GRAD if mode == "tn" else (MM_CAP_M_RMS if epilogue == "rms_bwd" else MM_CAP_MN)
    um, cm = tile(m, um, cap_m, 128 if mode == "tn" else 16)
    un, cn = tile(n, un, MM_CAP_N, 128)
    cap_k = MM_CAP_K_TOKENS if mode == "tn" else (MM_CAP_K_RMS if epilogue == "rms_bwd" else MM_CAP_K)
    uk, ck = tile(k, uk, cap_k, 128)
    if epilogue == "rms_bwd":
        assert mode != "tn" and n == D_MODEL and cm == cn == 1 and res is None and out_block is None
    if epilogue == "loss":
        assert n == D_MODEL and cm == cn == 1 and res is not None and out_block is None
    if norm_g is not None and norm_b:
        assert mode == "tn" and not b3 and n == D_MODEL and cn == 1
    elif norm_g is not None:
        assert not a3 and (m if mode == "tn" else k) == D_MODEL and (cm if mode == "tn" else ck) == 1
    if epilogue == "swiglu":
        assert res is None and ((mode == "nn" and b3 and out_block is None) or
                                (mode == "nt" and not b3 and out_block is not None))
        cn = 2
    if epilogue == "swiglu_bwd":
        assert mode == "nt" and out_block is not None and extra is not None and res is None
        cn = 1
    tm, tn, tk = cm * um, cn * un, ck * uk
    nk = k // tk
    dot = {"nn": _dot, "nt": _dot_nt, "tn": _dot_tn}[mode]
    half = n // un // 2
    blocked_out = out_block is not None or epilogue in ("swiglu", "swiglu_bwd")
    extras = [] if extra is None else (list(extra) if isinstance(extra, (tuple, list)) else [extra])

    def sl(idx, unit, count):
        return slice(None) if count == 1 else slice(idx * unit, (idx + 1) * unit)

    def body(*refs):
        a_ref, b_ref = refs[0], refs[1]
        pos = 2
        r_ref = ng_ref = None
        if res is not None:
            r_ref, pos = refs[pos], pos + 1
        e_refs, pos = refs[pos:pos + len(extras)], pos + len(extras)
        if norm_g is not None:
            ng_ref, pos = refs[pos], pos + 1
        outs, acc_ref = refs[pos:-1], refs[-1]
        kk = pl.program_id(2)

        def normed(x_ref):
            groups = []
            for r in range(0, x_ref.shape[0], NORM_ROWS):
                xv = x_ref[r:r + NORM_ROWS, :]
                rstd = lax.rsqrt(jnp.mean(xv * xv, axis=-1, keepdims=True) + EPS)
                groups.append((xv * rstd * ng_ref[...]).astype(BF16))
            return jnp.concatenate(groups, axis=0)

        def a_blk(mi, ki):
            if norm_g is not None and not norm_b:
                return normed(a_ref)
            if mode in ("nn", "nt"):
                return a_ref[ki] if a3 else a_ref[:, sl(ki, uk, ck)]
            return a_ref[mi] if a3 else a_ref[:, sl(mi, um, cm)]

        def b_blk(ki, ni):
            if norm_b:
                return normed(b_ref)
            if epilogue == "swiglu":
                return b_ref[ni, 0]
            if mode in ("nn", "tn"):
                return b_ref[ni] if b3 else b_ref[sl(ki, uk, ck), sl(ni, un, cn)]
            return b_ref[ki][sl(ni, un, cn), :] if b3 else b_ref[sl(ni, un, cn), sl(ki, uk, ck)]

        parts = {}
        for mi in range(cm):
            for ni in range(cn):
                part = None
                for ki in range(ck):
                    d = dot(a_blk(mi, ki).astype(BF16), b_blk(ki, ni).astype(BF16))
                    part = d if part is None else part + d
                parts[mi, ni] = part

        def finish(total):
            if epilogue == "swiglu":
                gate, up = total[0, 0], total[0, 1]
                outs[0][0, 0] = gate.astype(BF16)
                outs[0][1, 0] = up.astype(BF16)
                outs[1][0] = (gate * _sigmoid(gate) * up).astype(BF16)
                return
            if epilogue == "swiglu_bwd":
                dact = total[0, 0]
                gate, up = e_refs[0][0, 0].astype(F32), e_refs[0][1, 0].astype(F32)
                sg = _sigmoid(gate)
                outs[0][0, 0] = (dact * up * (sg * (1.0 + gate * (1.0 - sg)))).astype(BF16)
                outs[0][1, 0] = (dact * (gate * sg)).astype(BF16)
                return
            if epilogue == "rms_bwd":
                x_ref, g_ref, dres_ref = e_refs
                dh, dg = total[0, 0], None
                for r in range(0, tm, NORM_ROWS):
                    rows = slice(r, r + NORM_ROWS)
                    xv, dhv = x_ref[rows, :], dh[rows, :]
                    rstd = lax.rsqrt(jnp.mean(xv * xv, axis=-1, keepdims=True) + EPS)
                    xh = xv * rstd
                    dyg = dhv * g_ref[...]
                    c = jnp.mean(dyg * xh, axis=-1, keepdims=True)
                    outs[0][rows, :] = dres_ref[rows, :] + rstd * (dyg - xh * c)
                    part = jnp.sum(dhv * xh, axis=0, keepdims=True)
                    dg = part if dg is None else dg + part
                _accumulate(outs[1], dg, pl.program_id(0))
                return
            if epilogue == "loss":
                diff = r_ref[...] + total[0, 0] - e_refs[0][...]
                outs[0][...] = diff * (1.0 / n)
                sq = jnp.sum(jnp.sum(diff * diff, axis=-1, keepdims=True), axis=0, keepdims=True)
                _accumulate(outs[1], sq * (0.5 / n), pl.program_id(0))
                return
            for (mi, ni), val in total.items():
                rows, cols = sl(mi, um, cm), sl(ni, un, cn)
                if res is not None:
                    val = r_ref[rows, cols] + val
                if blocked_out:
                    outs[0][ni, rows] = val.astype(out_dtype)
                else:
                    outs[0][rows, cols] = val.astype(out_dtype)

        if nk == 1:
            finish(parts)
        else:
            @pl.when(kk == 0)
            def _():
                for (mi, ni), val in parts.items():
                    acc_ref[mi * cn + ni] = val

            @pl.when(jnp.logical_and(kk > 0, kk < nk - 1))
            def _():
                for (mi, ni), val in parts.items():
                    acc_ref[mi * cn + ni] += val

            @pl.when(kk == nk - 1)
            def _():
                finish({key: acc_ref[key[0] * cn + key[1]] + val for key, val in parts.items()})

    if mode in ("nn", "nt"):
        a_spec = (pl.BlockSpec((ck, tm, uk), lambda i, j, kk: (kk, i, 0)) if a3
                  else pl.BlockSpec((tm, tk), lambda i, j, kk: (i, kk)))
    else:
        a_spec = (pl.BlockSpec((cm, tk, um), lambda i, j, kk: (i, kk, 0)) if a3
                  else pl.BlockSpec((tk, tm), lambda i, j, kk: (kk, i)))
    pair_spec = pl.BlockSpec((2, 1, tm, un), lambda i, j, kk: (0, j, i, 0))
    row_spec = pl.BlockSpec((tm, tn), lambda i, j, kk: (i, 0))
    vec_spec = pl.BlockSpec((1, tn), lambda i, j, kk: (0, 0))
    if epilogue == "swiglu" and mode == "nn":
        b = b.reshape(2, half, k, un)
        b_spec = pl.BlockSpec((2, 1, tk, un), lambda i, j, kk: (0, j, kk, 0))
    elif epilogue == "swiglu":
        b = b.reshape(2, half, un, k)
        b_spec = pl.BlockSpec((2, 1, un, tk), lambda i, j, kk: (0, j, 0, kk))
    elif mode in ("nn", "tn"):
        b_spec = (pl.BlockSpec((cn, tk, un), lambda i, j, kk: (j, kk, 0)) if b3
                  else pl.BlockSpec((tk, tn), lambda i, j, kk: (kk, j)))
    else:
        b_spec = (pl.BlockSpec((ck, tn, uk), lambda i, j, kk: (kk, j, 0)) if b3
                  else pl.BlockSpec((tn, tk), lambda i, j, kk: (j, kk)))
    if epilogue == "swiglu":
        out_specs = [pair_spec, pl.BlockSpec((1, tm, un), lambda i, j, kk: (j, i, 0))]
        out_shape = [jax.ShapeDtypeStruct((2, half, m, un), BF16), jax.ShapeDtypeStruct((half, m, un), BF16)]
    elif epilogue == "swiglu_bwd":
        out_specs = [pair_spec]
        out_shape = [jax.ShapeDtypeStruct(extra.shape, BF16)]
    elif epilogue == "rms_bwd":
        out_specs = [row_spec, vec_spec]
        out_shape = [jax.ShapeDtypeStruct((m, n), F32), jax.ShapeDtypeStruct((1, n), F32)]
    elif epilogue == "loss":
        out_specs = [row_spec, pl.BlockSpec((1, 1), lambda i, j, kk: (0, 0))]
        out_shape = [jax.ShapeDtypeStruct((m, n), F32), jax.ShapeDtypeStruct((1, 1), F32)]
    elif blocked_out:
        out_specs = [pl.BlockSpec((cn, tm, un), lambda i, j, kk: (j, i, 0))]
        out_shape = [jax.ShapeDtypeStruct((n // un, m, un), out_dtype)]
    else:
        out_specs = [pl.BlockSpec((tm, tn), lambda i, j, kk: (i, j))]
        out_shape = [jax.ShapeDtypeStruct((m, n), out_dtype)]
    in_specs, args = [a_spec, b_spec], [a, b]
    if res is not None:
        in_specs.append(pl.BlockSpec((tm, tn), lambda i, j, kk: (i, j)))
        args.append(res)
    if epilogue == "swiglu_bwd":
        in_specs.append(pair_spec)
    elif epilogue == "rms_bwd":
        in_specs += [row_spec, vec_spec, row_spec]
    elif epilogue == "loss":
        in_specs.append(row_spec)
    args += extras
    if norm_g is not None:
        in_specs.append(pl.BlockSpec((1, D_MODEL), lambda i, j, kk: (0, 0)))
        args.append(norm_g)
    semantics = ("arbitrary",) * 3 if epilogue in ("rms_bwd", "loss") else ("parallel", "parallel", "arbitrary")
    out = _call(body, name, (m // tm, n // tn, nk), in_specs, out_specs, out_shape,
                [pltpu.VMEM((cm * cn, um, un), F32)], semantics, args, rider)
    return out if epilogue in ("swiglu", "rms_bwd", "loss") else out[0]


def _head_sums(v, ind):
    hi, lo = _split2(v)
    return _dot(hi, ind) + _dot(lo, ind)


def _head_spread(per_head, ind):
    hi, lo = _split2(per_head)
    return _dot_nt(hi, ind) + _dot_nt(lo, ind)


def _head_rstd(xv, ind):
    return _head_spread(lax.rsqrt(_head_sums(xv * xv, ind) * (1.0 / ATT_DH) + EPS), ind)


def _hn_bwd_math(xv, gv, ind, dyv, scale):
    rstd = _head_rstd(xv, ind)
    xh = xv * rstd
    dyn = dyv * scale
    dyg = dyn * gv
    dx = rstd * (dyg - xh * _head_spread(_head_sums(dyg * xh, ind) * (1.0 / ATT_DH), ind))
    return dx, jnp.sum(dyn * xh, axis=0, keepdims=True)


def _q_hnorm(x, g_tiled, bd, scale, name):
    t, d = x.shape
    tm = _pick(t, 512, 16)

    def body(x_ref, g_ref, bd_ref, o_ref):
        xv = x_ref[...]
        o_ref[...] = (xv * _head_rstd(xv, bd_ref[...]) * g_ref[...] * scale).astype(BF16)

    return pl.pallas_call(
        body, name=name, grid=(t // tm,),
        in_specs=[pl.BlockSpec((tm, d), lambda i: (i, 0)), pl.BlockSpec((1, d), lambda i: (0, 0)),
                  pl.BlockSpec((d, LANES), lambda i: (0, 0))],
        out_specs=pl.BlockSpec((tm, d), lambda i: (i, 0)),
        out_shape=jax.ShapeDtypeStruct((t, d), BF16),
        compiler_params=_params("parallel"),
    )(x, g_tiled, bd)


def _q_dhnorm(x, g_tiled, bd, dy, scale, name):
    t, d = x.shape
    tm = _pick(t, 512, 16)

    def body(x_ref, g_ref, bd_ref, dy_ref, dx_ref, dg_ref):
        dx, part = _hn_bwd_math(x_ref[...], g_ref[...], bd_ref[...], dy_ref[...], scale)
        dx_ref[...] = dx.astype(BF16)
        _accumulate(dg_ref, part, pl.program_id(0))

    row = pl.BlockSpec((tm, d), lambda i: (i, 0))
    vec = pl.BlockSpec((1, d), lambda i: (0, 0))
    return pl.pallas_call(
        body, name=name, grid=(t // tm,),
        in_specs=[row, vec, pl.BlockSpec((d, LANES), lambda i: (0, 0)), row],
        out_specs=[row, vec],
        out_shape=[jax.ShapeDtypeStruct((t, d), BF16), jax.ShapeDtypeStruct((1, d), F32)],
        compiler_params=_params("arbitrary"),
    )(x, g_tiled, bd, dy)


def _kv_prep(kv, g_tiled, bd, name):
    t = kv.shape[0]
    d = D_MODEL
    tm = K_PAD
    assert t % tm == 0

    def body(k_ref, v_ref, g_ref, bd_ref, kp_ref, vp_ref):
        i = pl.program_id(0)

        @pl.when(i == 0)
        def _():
            kp_ref[...] = jnp.zeros_like(kp_ref)
            vp_ref[...] = jnp.zeros_like(vp_ref)

        @pl.when(i > 0)
        def _():
            xv = k_ref[...]
            kp_ref[...] = (xv * _head_rstd(xv, bd_ref[...]) * g_ref[...]).astype(BF16)
            vp_ref[...] = v_ref[...].astype(BF16)

    shp = jax.ShapeDtypeStruct((t + K_PAD, d), BF16)
    out = pl.BlockSpec((tm, d), lambda i: (i, 0))
    return pl.pallas_call(
        body, name=name, grid=(t // tm + 1,),
        in_specs=[pl.BlockSpec((tm, d), lambda i: (jnp.maximum(i - 1, 0), 0)),
                  pl.BlockSpec((tm, d), lambda i: (jnp.maximum(i - 1, 0), 1)),
                  pl.BlockSpec((1, d), lambda i: (0, 0)), pl.BlockSpec((d, LANES), lambda i: (0, 0))],
        out_specs=[out, out], out_shape=[shp, shp],
        compiler_params=_params("arbitrary"),
    )(kv, kv, g_tiled, bd)


def _kv_dprep(kv, g_tiled, bd, dkp_t, dvp_t, name):
    t = kv.shape[0]
    d = D_MODEL
    tm = K_PAD

    def body(k_ref, g_ref, bd_ref, dk_ref, dv_ref, o_ref, dg_ref):
        dx, part = _hn_bwd_math(k_ref[...], g_ref[...], bd_ref[...], dk_ref[...].T, 1.0)
        o_ref[:, :d] = dx.astype(BF16)
        o_ref[:, d:] = dv_ref[...].T.astype(BF16)
        _accumulate(dg_ref, part, pl.program_id(0))

    vec = pl.BlockSpec((1, d), lambda i: (0, 0))
    padded = pl.BlockSpec((d, tm), lambda i: (0, i + 1))
    return pl.pallas_call(
        body, name=name, grid=(t // tm,),
        in_specs=[pl.BlockSpec((tm, d), lambda i: (i, 0)), vec, pl.BlockSpec((d, LANES), lambda i: (0, 0)),
                  padded, padded],
        out_specs=[pl.BlockSpec((tm, 2 * d), lambda i: (i, 0)), vec],
        out_shape=[jax.ShapeDtypeStruct((t, 2 * d), BF16), jax.ShapeDtypeStruct((1, d), F32)],
        compiler_params=_params("arbitrary"),
    )(kv, g_tiled, bd, dkp_t, dvp_t)


def _ret_consts(t):
    h = np.arange(RET_HEADS, dtype=np.float32)
    lg = np.log(np.float32(1.0) - np.float32(2.0) ** (np.float32(-5.0) - h)).astype(np.float32)
    tt = np.arange(CHUNK, dtype=np.float32)
    intra = np.exp(lg[:, None, None] * np.abs(tt[:, None] - tt[None, :])).astype(np.float32)
    q_dec = np.exp(lg[:, None] * (tt + 1.0)).astype(np.float32)
    k_dec = np.exp(lg[:, None] * (CHUNK - 1.0 - tt)).astype(np.float32)
    s_dec = [float(v) for v in np.exp(lg * np.float32(CHUNK)).astype(np.float32)]
    qd = np.broadcast_to(q_dec[:, :, None], (RET_HEADS, CHUNK, RET_DK)).copy()
    kd = np.broadcast_to(k_dec[:, :, None], (RET_HEADS, CHUNK, RET_DK)).copy()
    half = RET_DK // 2
    inv_freq = ROPE_BASE ** (-jnp.arange(half, dtype=F32) / half)
    ang = jnp.arange(t).astype(F32)[:, None] * inv_freq[None, :]
    return jnp.asarray(intra), jnp.asarray(qd), jnp.asarray(kd), s_dec, jnp.cos(ang), jnp.sin(ang)


def _rope(x, cos, sin):
    half = RET_DK // 2
    x1, x2 = x[:, :half], x[:, half:]
    return jnp.concatenate([x1 * cos - x2 * sin, x1 * sin + x2 * cos], axis=-1)


def _unrope(d, cos, sin):
    half = RET_DK // 2
    d1, d2 = d[:, :half], d[:, half:]
    return jnp.concatenate([d1 * cos + d2 * sin, d2 * cos - d1 * sin], axis=-1)


def _ret_slices(h):
    q = slice(h * RET_DK, (h + 1) * RET_DK)
    k = slice(RET_Q_COLS + h * RET_DK, RET_Q_COLS + (h + 1) * RET_DK)
    v = slice(2 * RET_Q_COLS + h * RET_DV, 2 * RET_Q_COLS + (h + 1) * RET_DV)
    g = slice(2 * RET_Q_COLS + RET_V_COLS + h * RET_DV, 2 * RET_Q_COLS + RET_V_COLS + (h + 1) * RET_DV)
    o = slice(h * RET_DV, (h + 1) * RET_DV)
    return q, k, v, g, o


def _ret_fwd(proj, gn, consts, name, rider=None):
    t, cols = proj.shape
    n = t // CHUNK
    intra, qd, kd, s_dec, cos, sin = consts
    k_scale = RET_DK ** -0.5

    def body(p_ref, cos_ref, sin_ref, intra_ref, qd_ref, kd_ref, gn_ref, y_ref, o_ref, st_ref, state):
        i = pl.program_id(0)

        @pl.when(i == 0)
        def _():
            state[...] = jnp.zeros_like(state)

        for c in range(RET_STEP):
            rows = slice(c * CHUNK, (c + 1) * CHUNK)
            cosv, sinv = cos_ref[rows, :], sin_ref[rows, :]
            for h in range(RET_HEADS):
                qs, ks, vs, gs, os_ = _ret_slices(h)
                qr = _rope(p_ref[rows, qs], cosv, sinv)
                kr = _rope(p_ref[rows, ks], cosv, sinv) * k_scale
                vb = p_ref[rows, vs].astype(BF16)
                gv = p_ref[rows, gs]
                scores = _dot_nt(qr.astype(BF16), kr.astype(BF16)) * intra_ref[h]
                s_old = state[h]
                s_old_b = s_old.astype(BF16)
                st_ref[c, h] = s_old_b
                o = _dot(scores.astype(BF16), vb) + _dot((qr * qd_ref[h]).astype(BF16), s_old_b)
                state[h] = s_old * s_dec[h] + _dot_tn((kr * kd_ref[h]).astype(BF16), vb)
                rstd = lax.rsqrt(jnp.mean(o * o, axis=-1, keepdims=True) + EPS)
                on = o * rstd * gn_ref[:, os_]
                o_ref[rows, os_] = o
                y_ref[rows, os_] = (gv * _sigmoid(gv) * on).astype(BF16)

    full3 = lambda a: pl.BlockSpec(a.shape, lambda i: (0, 0, 0))
    step = RET_STEP * CHUNK
    return _call(
        body, name, (n // RET_STEP,),
        [pl.BlockSpec((step, cols), lambda i: (i, 0)),
         pl.BlockSpec((step, RET_DK // 2), lambda i: (i, 0)),
         pl.BlockSpec((step, RET_DK // 2), lambda i: (i, 0)),
         full3(intra), full3(qd), full3(kd),
         pl.BlockSpec((1, RET_V_COLS), lambda i: (0, 0))],
        [pl.BlockSpec((step, RET_V_COLS), lambda i: (i, 0)),
         pl.BlockSpec((step, RET_V_COLS), lambda i: (i, 0)),
         pl.BlockSpec((RET_STEP, RET_HEADS, RET_DK, RET_DV), lambda i: (i, 0, 0, 0))],
        [jax.ShapeDtypeStruct((t, RET_V_COLS), BF16),
         jax.ShapeDtypeStruct((t, RET_V_COLS), F32),
         jax.ShapeDtypeStruct((n, RET_HEADS, RET_DK, RET_DV), BF16)],
        [pltpu.VMEM((RET_HEADS, RET_DK, RET_DV), F32)], ("arbitrary",),
        (proj, cos, sin, intra, qd, kd, gn), rider)


def _ret_bwd(proj, gn, o_saved, states, dy, consts, name, rider=None):
    t, cols = proj.shape
    n = t // CHUNK
    intra, qd, kd, s_dec, cos, sin = consts
    k_scale = RET_DK ** -0.5

    def body(p_ref, cos_ref, sin_ref, intra_ref, qd_ref, kd_ref, gn_ref, o_ref, st_ref, dy_ref,
             dp_ref, dgn_ref, dstate):
        i = pl.program_id(0)

        @pl.when(i == 0)
        def _():
            dstate[...] = jnp.zeros_like(dstate)

        dgn = None
        for c in reversed(range(RET_STEP)):
            rows = slice(c * CHUNK, (c + 1) * CHUNK)
            cosv, sinv = cos_ref[rows, :], sin_ref[rows, :]
            dgn_parts = []
            for h in range(RET_HEADS):
                qs, ks, vs, gs, os_ = _ret_slices(h)
                qr = _rope(p_ref[rows, qs], cosv, sinv)
                kr = _rope(p_ref[rows, ks], cosv, sinv) * k_scale
                qb, kb = qr.astype(BF16), kr.astype(BF16)
                vb = p_ref[rows, vs].astype(BF16)
                gv = p_ref[rows, gs]
                ov = o_ref[rows, os_]
                dyv = dy_ref[rows, os_]
                gnv = gn_ref[:, os_]
                sg = _sigmoid(gv)
                rstd = lax.rsqrt(jnp.mean(ov * ov, axis=-1, keepdims=True) + EPS)
                oh = ov * rstd
                d_on = dyv * (gv * sg)
                dg = dyv * (oh * gnv) * (sg * (1.0 + gv * (1.0 - sg)))
                dgn_parts.append(jnp.sum(d_on * oh, axis=0, keepdims=True))
                d_oh = d_on * gnv
                do = rstd * (d_oh - oh * jnp.mean(d_oh * oh, axis=-1, keepdims=True))
                dob = do.astype(BF16)
                mask = intra_ref[h]
                a_b = (_dot_nt(qb, kb) * mask).astype(BF16)
                da_b = (_dot_nt(dob, vb) * mask).astype(BF16)
                ds_new = dstate[h]
                ds_new_b = ds_new.astype(BF16)
                s_old_b = st_ref[c, h]
                qdv, kdv = qd_ref[h], kd_ref[h]
                dv = _dot_tn(a_b, dob) + _dot((kr * kdv).astype(BF16), ds_new_b)
                dqr = _dot(da_b, kb) + _dot_nt(dob, s_old_b) * qdv
                dkr = _dot_tn(da_b, qb) + _dot_nt(vb, ds_new_b) * kdv
                dstate[h] = ds_new * s_dec[h] + _dot_tn((qr * qdv).astype(BF16), dob)
                dp_ref[rows, qs] = _unrope(dqr, cosv, sinv).astype(BF16)
                dp_ref[rows, ks] = _unrope(dkr * k_scale, cosv, sinv).astype(BF16)
                dp_ref[rows, vs] = dv.astype(BF16)
                dp_ref[rows, gs] = dg.astype(BF16)
            part = jnp.concatenate(dgn_parts, axis=-1)
            dgn = part if dgn is None else dgn + part
        _accumulate(dgn_ref, dgn, i)

    steps = n // RET_STEP
    step = RET_STEP * CHUNK
    rev = lambda i: (steps - 1 - i, 0)
    full3 = lambda a: pl.BlockSpec(a.shape, lambda i: (0, 0, 0))
    return _call(
        body, name, (steps,),
        [pl.BlockSpec((step, cols), rev),
         pl.BlockSpec((step, RET_DK // 2), rev),
         pl.BlockSpec((step, RET_DK // 2), rev),
         full3(intra), full3(qd), full3(kd),
         pl.BlockSpec((1, RET_V_COLS), lambda i: (0, 0)),
         pl.BlockSpec((step, RET_V_COLS), rev),
         pl.BlockSpec((RET_STEP, RET_HEADS, RET_DK, RET_DV), lambda i: (steps - 1 - i, 0, 0, 0)),
         pl.BlockSpec((step, RET_V_COLS), rev)],
        [pl.BlockSpec((step, cols), rev),
         pl.BlockSpec((1, RET_V_COLS), lambda i: (0, 0))],
        [jax.ShapeDtypeStruct((t, cols), BF16),
         jax.ShapeDtypeStruct((1, RET_V_COLS), F32)],
        [pltpu.VMEM((RET_HEADS, RET_DK, RET_DV), F32)], ("arbitrary",),
        (proj, cos, sin, intra, qd, kd, gn, o_saved, states, dy), rider)


def _att_common(q_ref, kp_ref, vp_ref, sub):
    blk = pl.program_id(1) * ATT_SUBS + sub
    start = pl.multiple_of(blk * Q_BLOCK, Q_BLOCK)
    kw = kp_ref[pl.ds(start, K_WINDOW), :]
    vw = vp_ref[pl.ds(start, K_WINDOW), :]
    kvalid = blk * Q_BLOCK - K_PAD + lax.broadcasted_iota(jnp.int32, (1, K_WINDOW), 1) >= 0
    lane = lax.broadcasted_iota(jnp.int32, (1, LANES), 1)
    qrows = slice(sub * Q_BLOCK, (sub + 1) * Q_BLOCK)
    return start, qrows, q_ref[qrows, :], kw, vw, kvalid, (lane < ATT_DH, lane >= ATT_DH)


def _row_groups():
    return [slice(r * ATT_ROWS, (r + 1) * ATT_ROWS) for r in range(Q_BLOCK // ATT_ROWS)]


def _lane_copies(x):
    return jnp.tile(x, (1, K_WINDOW // LANES))


def _att_specs(t, tp):
    qspec = pl.BlockSpec((ATT_SUBS * Q_BLOCK, LANES), lambda h, i: (i, h))
    kspec = pl.BlockSpec((tp, LANES), lambda h, i: (0, h))
    bspec = pl.BlockSpec((2, Q_BLOCK, K_WINDOW), lambda h, i: (h, 0, 0))
    return qspec, kspec, bspec


def _att_fwd(q, kp, vp, bias, name, rider=None):
    t, d = q.shape
    tp = kp.shape[0]

    def body(q_ref, kp_ref, vp_ref, bias_ref, o_ref, lse_ref, s_scr, p_scr, lse_scr):
        for sub in range(ATT_SUBS):
            _, qrows, q2, kw, vw, kvalid, sel = _att_common(q_ref, kp_ref, vp_ref, sub)
            for hh in range(2):
                s_scr[sub, hh] = _dot_nt(jnp.where(sel[hh], q2, 0), kw)
            for hh in range(2):
                for rows in _row_groups():
                    s = jnp.where(kvalid, s_scr[sub, hh, rows, :] + bias_ref[hh, rows, :], NEG)
                    m = jnp.max(s, axis=-1, keepdims=True)
                    e = jnp.exp(s - m)
                    l = jnp.sum(e, axis=-1, keepdims=True)
                    p_scr[sub, hh, rows, :] = (e * (1.0 / l)).astype(BF16)
                    lse_scr[sub, hh, rows, :] = jnp.broadcast_to(m + jnp.log(l), (ATT_ROWS, LANES))
            outs = [_dot(p_scr[sub, hh], vw) for hh in range(2)]
            o_ref[qrows, :] = jnp.where(sel[0], outs[0], outs[1]).astype(BF16)
            lse_ref[qrows, :] = jnp.where(sel[0], lse_scr[sub, 0], lse_scr[sub, 1])

    qspec, kspec, bspec = _att_specs(t, tp)
    return _call(body, name, (d // LANES, t // (ATT_SUBS * Q_BLOCK)), [qspec, kspec, kspec, bspec], [qspec, qspec],
                 [jax.ShapeDtypeStruct((t, d), BF16), jax.ShapeDtypeStruct((t, d), F32)],
                 [pltpu.VMEM((ATT_SUBS, 2, Q_BLOCK, K_WINDOW), F32),
                  pltpu.VMEM((ATT_SUBS, 2, Q_BLOCK, K_WINDOW), BF16),
                  pltpu.VMEM((ATT_SUBS, 2, Q_BLOCK, LANES), F32)],
                 ("parallel", "arbitrary"), (q, kp, vp, bias), rider)


def _att_bwd(q, kp, vp, bias, do, o, lse, name, rider=None):
    t, d = q.shape
    tp = kp.shape[0]

    def body(q_ref, kp_ref, vp_ref, bias_ref, do_ref, o_ref, lse_ref, dq_ref, dkp_ref, dvp_ref, db_ref,
             s_scr, dp_scr, p_scr, ds_scr, row_scr):
        @pl.when(pl.program_id(1) == 0)
        def _():
            dkp_ref[...] = jnp.zeros_like(dkp_ref)
            dvp_ref[...] = jnp.zeros_like(dvp_ref)
            db_ref[...] = jnp.zeros_like(db_ref)

        for sub in range(ATT_SUBS):
            start, qrows, q2, kw, vw, kvalid, sel = _att_common(q_ref, kp_ref, vp_ref, sub)
            do2 = do_ref[qrows, :]
            qm = [jnp.where(sel[hh], q2, 0) for hh in range(2)]
            dom = [jnp.where(sel[hh], do2, 0) for hh in range(2)]
            do_o = do2.astype(F32) * o_ref[qrows, :].astype(F32)
            lse2 = lse_ref[qrows, :]
            for hh in range(2):
                s_scr[sub, hh] = _dot_nt(qm[hh], kw)
                dp_scr[sub, hh] = _dot_nt(dom[hh], vw)
                lse_h = jnp.max(jnp.where(sel[hh], lse2, NEG), axis=-1, keepdims=True)
                delta = jnp.sum(jnp.where(sel[hh], do_o, 0.0), axis=-1, keepdims=True)
                row_scr[sub, hh, 0] = jnp.broadcast_to(lse_h, (Q_BLOCK, LANES))
                row_scr[sub, hh, 1] = jnp.broadcast_to(delta, (Q_BLOCK, LANES))
            for hh in range(2):
                for rows in _row_groups():
                    s = jnp.where(kvalid, s_scr[sub, hh, rows, :] + bias_ref[hh, rows, :], NEG)
                    p = jnp.exp(s - _lane_copies(row_scr[sub, hh, 0, rows, :]))
                    ds = p * (dp_scr[sub, hh, rows, :] - _lane_copies(row_scr[sub, hh, 1, rows, :]))
                    db_ref[hh, rows, :] += ds
                    p_scr[sub, hh, rows, :] = p.astype(BF16)
                    ds_scr[sub, hh, rows, :] = ds.astype(BF16)
            dqs = [_dot(ds_scr[sub, hh], kw) for hh in range(2)]
            dq_ref[qrows, :] = jnp.where(sel[0], dqs[0], dqs[1])
            dkp_ref[:, pl.ds(start, K_WINDOW)] += (_dot_tn(qm[0], ds_scr[sub, 0]) +
                                                   _dot_tn(qm[1], ds_scr[sub, 1]))
            dvp_ref[:, pl.ds(start, K_WINDOW)] += (_dot_tn(dom[0], p_scr[sub, 0]) +
                                                   _dot_tn(dom[1], p_scr[sub, 1]))

    qspec, kspec, bspec = _att_specs(t, tp)
    tspec = pl.BlockSpec((LANES, tp), lambda h, i: (h, 0))
    stage = lambda dt: pltpu.VMEM((ATT_SUBS, 2, Q_BLOCK, K_WINDOW), dt)
    return _call(body, name, (d // LANES, t // (ATT_SUBS * Q_BLOCK)),
                 [qspec, kspec, kspec, bspec, qspec, qspec, qspec],
                 [qspec, tspec, tspec, bspec],
                 [jax.ShapeDtypeStruct((t, d), F32),
                  jax.ShapeDtypeStruct((d, tp), F32),
                  jax.ShapeDtypeStruct((d, tp), F32),
                  jax.ShapeDtypeStruct((ATT_HEADS, Q_BLOCK, K_WINDOW), F32)],
                 [stage(F32), stage(F32), stage(BF16), stage(BF16),
                  pltpu.VMEM((ATT_SUBS, 2, 2, Q_BLOCK, LANES), F32)],
                 ("parallel", "arbitrary"), (q, kp, vp, bias, do, o, lse), rider)


def _rel_bin_matrix():
    rows = REL_DELTAS * 2 * REL_BLK
    rho = lax.broadcasted_iota(jnp.int32, (rows, REL_PAD), 0)
    col = lax.broadcasted_iota(jnp.int32, (rows, REL_PAD), 1)
    assert 2 * REL_BLK == 256
    delta = rho >> 8
    c = 255 - (rho & 255)
    dist = K_PAD + REL_BLK * (delta - (K_WINDOW // REL_BLK - 1)) + (c - (REL_BLK - 1))
    idx = jnp.clip(dist, -REL_CLIP, REL_CLIP) + REL_CLIP
    return col == idx


def _rel_expand(rel_pad, name):
    heads = rel_pad.shape[0]
    rows = REL_DELTAS * 2 * REL_BLK

    def body_bin(r_ref, o_ref):
        onehot = jnp.where(_rel_bin_matrix(), 1.0, 0.0).astype(BF16)
        hi, mid, lo = _split3(r_ref[...])
        o_ref[...] = _dot_nt(hi, onehot) + _dot_nt(mid, onehot) + _dot_nt(lo, onehot)

    by_delta = pl.pallas_call(
        body_bin, name=name + "_bin",
        out_shape=jax.ShapeDtypeStruct((heads, rows), F32),
        compiler_params=pltpu.CompilerParams(vmem_limit_bytes=VMEM_LIMIT_V7X),
    )(rel_pad)
    by_delta = by_delta.reshape(heads * REL_DELTAS, 2 * REL_BLK)

    def body_shift(t_ref, o_ref):
        tv = t_ref[...]
        for r in range(REL_BLK):
            o_ref[r] = pltpu.roll(tv, (r + REL_BLK) % (2 * REL_BLK), 1)[:, :REL_BLK]

    return pl.pallas_call(
        body_shift, name=name + "_shift",
        out_shape=jax.ShapeDtypeStruct((REL_BLK, heads * REL_DELTAS, REL_BLK), F32),
        compiler_params=pltpu.CompilerParams(vmem_limit_bytes=VMEM_LIMIT_V7X),
    )(by_delta)


def _bias_table(rel_bias, name):
    heads = rel_bias.shape[0]
    rel_pad = jnp.pad(rel_bias, ((0, 0), (0, REL_PAD - REL_TABLE)))
    tiles = _rel_expand(rel_pad, name)
    tiles = tiles.reshape(REL_BLK, heads, REL_DELTAS, REL_BLK).transpose(1, 2, 0, 3)
    na, nb = Q_BLOCK // REL_BLK, K_WINDOW // REL_BLK
    rows = [jnp.concatenate([tiles[:, a - b + nb - 1] for b in range(nb)], axis=-1) for a in range(na)]
    table = jnp.concatenate(rows, axis=-2)
    qc = np.arange(Q_BLOCK)[:, None] // CHUNK
    kc = np.arange(K_WINDOW)[None, :] // CHUNK
    band = (kc >= qc) & (kc <= qc + PAST_CHUNKS)
    return jnp.where(jnp.asarray(band)[None], table, NEG)


def _rel_reduce(db, name):
    heads = db.shape[0]
    na, nb = Q_BLOCK // REL_BLK, K_WINDOW // REL_BLK

    fold_heads = 4

    def body_fold(db_ref, g_ref):
        for hd in range(fold_heads):
            for delta in range(REL_DELTAS):
                acc = None
                for a in range(na):
                    b = a - (delta - (nb - 1))
                    if 0 <= b < nb:
                        tile = db_ref[hd, a * REL_BLK:(a + 1) * REL_BLK, b * REL_BLK:(b + 1) * REL_BLK]
                        acc = tile if acc is None else acc + tile
                g_ref[hd, delta] = acc

    folded = pl.pallas_call(
        body_fold, name=name + "_fold", grid=(heads // fold_heads,),
        in_specs=[pl.BlockSpec((fold_heads, Q_BLOCK, K_WINDOW), lambda h: (h, 0, 0))],
        out_specs=pl.BlockSpec((fold_heads, REL_DELTAS, REL_BLK, REL_BLK), lambda h: (h, 0, 0, 0)),
        out_shape=jax.ShapeDtypeStruct((heads, REL_DELTAS, REL_BLK, REL_BLK), F32),
        compiler_params=_params("parallel"),
    )(db)
    by_row = folded.transpose(2, 0, 1, 3).reshape(REL_BLK, heads * REL_DELTAS, REL_BLK)

    def body_diag(g_ref, d_ref):
        zeros = jnp.zeros((heads * REL_DELTAS, REL_BLK), F32)
        acc = None
        for r in range(REL_BLK):
            part = pltpu.roll(jnp.concatenate([g_ref[r], zeros], axis=1), REL_BLK - r, 1)
            acc = part if acc is None else acc + part
        d_ref[...] = acc

    diag = pl.pallas_call(
        body_diag, name=name + "_diag",
        out_shape=jax.ShapeDtypeStruct((heads * REL_DELTAS, 2 * REL_BLK), F32),
        compiler_params=pltpu.CompilerParams(vmem_limit_bytes=VMEM_LIMIT_V7X),
    )(by_row)
    diag = diag.reshape(heads, REL_DELTAS * 2 * REL_BLK)

    def body_bin(d_ref, o_ref):
        onehot = jnp.where(_rel_bin_matrix(), 1.0, 0.0).astype(BF16)
        hi, mid, lo = _split3(d_ref[...])
        o_ref[...] = _dot(hi, onehot) + _dot(mid, onehot) + _dot(lo, onehot)

    out = pl.pallas_call(
        body_bin, name=name + "_bin",
        out_shape=jax.ShapeDtypeStruct((heads, REL_PAD), F32),
        compiler_params=pltpu.CompilerParams(vmem_limit_bytes=VMEM_LIMIT_V7X),
    )(diag)
    return out[:, :REL_TABLE]


def _sum_leading(x, name):
    n, r, c = x.shape
    tr = _pick(r, 256, 8)

    def body(x_ref, o_ref):
        acc = x_ref[0].astype(F32)
        for k in range(1, n):
            acc = acc + x_ref[k].astype(F32)
        o_ref[...] = acc

    return pl.pallas_call(
        body, name=name, grid=(r // tr,),
        in_specs=[pl.BlockSpec((n, tr, c), lambda i: (0, i, 0))],
        out_specs=pl.BlockSpec((tr, c), lambda i: (i, 0)),
        out_shape=jax.ShapeDtypeStruct((r, c), F32),
        compiler_params=_params("parallel"),
    )(x)


def _pair_add(g, recv, parity, name):
    _, r, c = g.shape
    tr = _pick(r, 256, 16)

    def body(par_ref, g_ref, r_ref, o_ref):
        o_ref[...] = (g_ref[...].astype(F32) + r_ref[...].astype(F32)).astype(BF16)

    return pl.pallas_call(
        body, name=name,
        grid_spec=pltpu.PrefetchScalarGridSpec(
            num_scalar_prefetch=1, grid=(4, r // tr),
            in_specs=[pl.BlockSpec((1, tr, c), lambda k, i, par: (2 * k + par[0], i, 0)),
                      pl.BlockSpec((1, tr, c), lambda k, i, par: (k, i, 0))],
            out_specs=pl.BlockSpec((1, tr, c), lambda k, i, par: (k, i, 0))),
        out_shape=jax.ShapeDtypeStruct((4, r, c), BF16),
        compiler_params=_params("parallel", "parallel"),
    )(parity, g, recv)


def _adamw(w, g_parts, m, v, name):
    r, c = w.shape
    n = g_parts.shape[0]
    tr = _pick(r, 256, 16 if g_parts.dtype == BF16 else 8)
    c1 = 1.0 - ADAM_B1 ** ADAM_STEP
    c2 = 1.0 - ADAM_B2 ** ADAM_STEP

    def body(w_ref, g_ref, m_ref, v_ref, go_ref, d_ref, nm_ref, nv_ref):
        gv = g_ref[0].astype(F32)
        for k in range(1, n):
            gv = gv + g_ref[k].astype(F32)
        nm = ADAM_B1 * m_ref[...] + (1.0 - ADAM_B1) * gv
        nv = ADAM_B2 * v_ref[...] + (1.0 - ADAM_B2) * (gv * gv)
        go_ref[...] = gv
        d_ref[...] = -ADAM_LR * ((nm / c1) / (jnp.sqrt(nv / c2) + ADAM_EPS) + ADAM_WD * w_ref[...])
        nm_ref[...] = nm
        nv_ref[...] = nv

    spec = pl.BlockSpec((tr, c), lambda i: (i, 0))
    shp = jax.ShapeDtypeStruct((r, c), F32)
    return pl.pallas_call(
        body, name=name, grid=(r // tr,),
        in_specs=[spec, pl.BlockSpec((n, tr, c), lambda i: (0, i, 0)), spec, spec],
        out_specs=[spec] * 4, out_shape=[shp] * 4,
        compiler_params=_params("parallel"),
    )(w, g_parts, m, v)


BIG = (("a_w_in", 1), ("a_w_o", 0), ("a_w_gu", 0), ("a_w_down", 0), ("w_kv", 1),
       ("b_w_q", 0), ("b_w_o", 0), ("b_w_gu", 0), ("b_w_down", 0))
TRANSPOSED = ("a_w_gu", "b_w_gu")
FFN_BLK = 2 * FFN_HIDDEN // N_DEV

SMALL = (("a_norm_g", D_MODEL, True), ("a_gn_g", RET_V_COLS, True), ("a_ffn_norm_g", D_MODEL, True),
         ("kv_norm_g", D_MODEL, False), ("b_norm_g", D_MODEL, False), ("b_ffn_norm_g", D_MODEL, False),
         ("k_norm_g", ATT_DH, False), ("b_q_norm_g", ATT_DH, False),
         ("b_rel_bias", ATT_HEADS * REL_TABLE, False))
SMALL_ROWS, SMALL_COLS = 16, 1024


def _pack_small(vals, last=None):
    flat = jnp.concatenate([vals[n].reshape(-1) for n, _, _ in SMALL])
    room = SMALL_ROWS * SMALL_COLS - flat.shape[0]
    if last is None:
        flat = jnp.pad(flat, (0, room))
    else:
        flat = jnp.concatenate([jnp.pad(flat, (0, room - 1)), last.reshape(1)])
    return flat.reshape(SMALL_ROWS, SMALL_COLS)


def _unpack_small(packed, local):
    flat, out, pos = packed.reshape(-1), {}, 0
    for n, length, sharded in SMALL:
        ln = length // N_DEV if (local and sharded) else length
        out[n] = flat[pos:pos + ln]
        pos += ln
    return out


def _gather_rider(shards, names):
    return _GatherRider([shards[n] for n in names])


def _gathered(rider, names, axis_of):
    return {n: (r.reshape(-1, r.shape[2]) if axis_of[n] == 0 else r) for n, r in zip(names, rider.results)}


def _blocks(g):
    return g if g.ndim == 3 else g.reshape(N_DEV, -1, g.shape[-1])


def _local_step(x, target, shards, w_in, s, parity):
    t = x.shape[0]
    axis_of = dict(BIG)
    consts = _ret_consts(t)
    lane_to_head = np.zeros((D_MODEL, LANES), np.float32)
    lane_to_head[np.arange(D_MODEL), np.arange(D_MODEL) // ATT_DH] = 1.0
    bd = jnp.asarray(lane_to_head).astype(BF16)
    kg_t = jnp.tile(s["k_norm_g"], (1, ATT_HEADS))
    qg_t = jnp.tile(s["b_q_norm_g"], (1, ATT_HEADS))
    q_scale = ATT_DH ** -0.5
    w = {"a_w_in": w_in}
    g, recv = {}, {}

    def gather_on(names):
        return _gather_rider(shards, names), names

    def landed(ride):
        w.update(_gathered(ride[0], ride[1], axis_of))

    def scatter_on(names):
        return _ScatterRider([_blocks(g[n]) for n in names]), names

    def reduced(ride):
        recv.update(zip(ride[1], ride[0].results))

    ride = gather_on(["a_w_o", "a_w_down"])
    proj = _mm(x, w["a_w_in"], "nn", "a_proj", norm_g=s["a_norm_g"], rider=ride[0])
    landed(ride)
    ride = gather_on(["a_w_gu"])
    y, o_ret, states = _ret_fwd(proj, s["a_gn_g"], consts, "a_ret", rider=ride[0])
    landed(ride)
    ride = gather_on(["w_kv"])
    x1 = _mm(y, w["a_w_o"], "nn", "a_out", res=x, rider=ride[0])
    landed(ride)
    ride = gather_on(["b_w_q", "b_w_o", "b_w_down"])
    gu_a, act_a = _mm(x1, w["a_w_gu"], "nt", "a_ffn_gu", epilogue="swiglu", out_block=FFN_BLK,
                      norm_g=s["a_ffn_norm_g"], rider=ride[0])
    landed(ride)
    x2 = _mm(act_a, w["a_w_down"], "nn", "a_ffn_down", res=x1)

    kv = _mm(x2, w["w_kv"], "nn", "kv_proj", norm_g=s["kv_norm_g"])
    kp, vp = _kv_prep(kv, kg_t, bd, "kv_prep")

    q_raw = _mm(x2, w["b_w_q"], "nn", "b_q", norm_g=s["b_norm_g"])
    qn = _q_hnorm(q_raw, qg_t, bd, q_scale, "q_hnorm")
    bias = _bias_table(s["b_rel_bias"].reshape(ATT_HEADS, REL_TABLE), "rel")
    ride = gather_on(["b_w_gu"])
    o_att, lse = _att_fwd(qn, kp, vp, bias, "b_att", rider=ride[0])
    landed(ride)
    x3 = _mm(o_att, w["b_w_o"], "nn", "b_out", res=x2)
    gu_b, act_b = _mm(x3, w["b_w_gu"], "nt", "b_ffn_gu", epilogue="swiglu", out_block=FFN_BLK,
                      norm_g=s["b_ffn_norm_g"])
    dy, loss = _mm(act_b, w["b_w_down"], "nn", "b_ffn_down", res=x3, epilogue="loss", extra=(target,))
    in_blk, kv_blk, ffn_blk = w["a_w_in"].shape[2], w["w_kv"].shape[2], FFN_BLK

    dgu = _mm(dy, w["b_w_down"], "nt", "b_ffn_dgu", out_block=ffn_blk, epilogue="swiglu_bwd", extra=gu_b)
    dgu = dgu.reshape(N_DEV, t, ffn_blk)
    g["b_w_down"] = _mm(act_b, dy, "tn", "b_ffn_gdown", out_dtype=BF16)
    ride = scatter_on(["b_w_down"])
    dx3, g["b_ffn_norm_g"] = _mm(dgu, w["b_w_gu"], "nn", "b_ffn_dh", epilogue="rms_bwd",
                                 extra=(x3, s["b_ffn_norm_g"], dy), rider=ride[0])
    reduced(ride)
    g["b_w_gu"] = _mm(dgu, x3, "tn", "b_ffn_ggu", out_dtype=BF16, norm_g=s["b_ffn_norm_g"], norm_b=True)

    do_att = _mm(dx3, w["b_w_o"], "nt", "b_dout", out_dtype=BF16)
    g["b_w_o"] = _mm(o_att, dx3, "tn", "b_gout", out_dtype=BF16)
    ride = scatter_on(["b_w_gu", "b_w_o"])
    dq, dkp, dvp, db = _att_bwd(qn, kp, vp, bias, do_att, o_att, lse, "b_datt", rider=ride[0])
    reduced(ride)
    g["b_rel_bias"] = _rel_reduce(db, "drel").reshape(1, -1)
    dq_raw, gq = _q_dhnorm(q_raw, qg_t, bd, dq, q_scale, "q_dhnorm")
    g["b_q_norm_g"] = gq.reshape(ATT_HEADS, ATT_DH).sum(axis=0, keepdims=True)
    g["b_w_q"] = _mm(x2, dq_raw, "tn", "b_gq", out_dtype=BF16, norm_g=s["b_norm_g"])
    dx2, g["b_norm_g"] = _mm(dq_raw, w["b_w_q"], "nt", "b_dq", epilogue="rms_bwd",
                             extra=(x2, s["b_norm_g"], dx3))

    dkv, gk = _kv_dprep(kv, kg_t, bd, dkp, dvp, "kv_dprep")
    g["k_norm_g"] = gk.reshape(ATT_HEADS, ATT_DH).sum(axis=0, keepdims=True)
    g["w_kv"] = _mm(x2, dkv, "tn", "kv_g", out_dtype=BF16, out_block=kv_blk, norm_g=s["kv_norm_g"])
    dx2, g["kv_norm_g"] = _mm(dkv, w["w_kv"], "nt", "kv_du", epilogue="rms_bwd",
                              extra=(x2, s["kv_norm_g"], dx2))

    ride = scatter_on(["b_w_q"])
    dgu = _mm(dx2, w["a_w_down"], "nt", "a_ffn_dgu", out_block=ffn_blk, epilogue="swiglu_bwd", extra=gu_a,
              rider=ride[0])
    reduced(ride)
    dgu = dgu.reshape(N_DEV, t, ffn_blk)
    g["a_w_down"] = _mm(act_a, dx2, "tn", "a_ffn_gdown", out_dtype=BF16)
    ride = scatter_on(["a_w_down"])
    dx1, g["a_ffn_norm_g"] = _mm(dgu, w["a_w_gu"], "nn", "a_ffn_dh", epilogue="rms_bwd",
                                 extra=(x1, s["a_ffn_norm_g"], dx2), rider=ride[0])
    reduced(ride)
    ride = scatter_on(["w_kv"])
    g["a_w_gu"] = _mm(dgu, x1, "tn", "a_ffn_ggu", out_dtype=BF16, norm_g=s["a_ffn_norm_g"], norm_b=True,
                      rider=ride[0])
    reduced(ride)

    swap = _SiblingSwapRider([_blocks(g["a_w_gu"])])
    dy_ret = _mm(dx1, w["a_w_o"], "nt", "a_dout", rider=swap)
    g["a_w_o"] = _mm(y, dx1, "tn", "a_gout", out_dtype=BF16)
    chips = _ChipScatterRider([_pair_add(_blocks(g["a_w_gu"]), swap.results[0], parity, "rs_pair_add_gu")])
    dproj, g["a_gn_g"] = _ret_bwd(proj, s["a_gn_g"], o_ret, states, dy_ret, consts, "a_dret", rider=chips)
    recv["a_w_gu"] = chips.results[0]
    ride = scatter_on(["a_w_o"])
    g["a_w_in"] = _mm(x, dproj, "tn", "a_gin", out_dtype=BF16, out_block=in_blk, norm_g=s["a_norm_g"],
                      rider=ride[0])
    reduced(ride)
    from_sibling = _exchange(_SiblingSwapRider([g["a_w_in"]]), "rs_sibling")[0]
    chip_sums = _pair_add(g["a_w_in"], from_sibling, parity, "rs_pair_add")
    last = _ChipScatterRider([chip_sums])
    grad_x, g["a_norm_g"] = _mm(dproj, w["a_w_in"], "nt", "a_dproj", epilogue="rms_bwd",
                                extra=(x, s["a_norm_g"], dx1), rider=last)
    recv["a_w_in"] = last.results[0]
    return loss, grad_x, recv, g


ARG_NAMES = ("x", "a_norm_g", "a_w_in", "a_gn_g", "a_w_o", "a_ffn_norm_g", "a_w_gu", "a_w_down",
             "kv_norm_g", "w_kv", "k_norm_g", "b_norm_g", "b_w_q", "b_q_norm_g", "b_rel_bias", "b_w_o",
             "b_ffn_norm_g", "b_w_gu", "b_w_down")
WEIGHT_NAMES = ARG_NAMES[1:]


def _big_shard(a, name):
    a = a[0] if a.ndim == 3 else a
    return a.T if name in TRANSPOSED else a


def _as_given(a, name, shape):
    return (a.T if name in TRANSPOSED else a).reshape(shape)


def kernel(x, a_norm_g, a_w_in, a_gn_g, a_w_o, a_ffn_norm_g, a_w_gu, a_w_down, kv_norm_g, w_kv, k_norm_g, b_norm_g, b_w_q, b_q_norm_g, b_rel_bias, b_w_o, b_ffn_norm_g, b_w_gu, b_w_down, loss_target, m_a_norm_g, m_a_w_in, m_a_gn_g, m_a_w_o, m_a_ffn_norm_g, m_a_w_gu, m_a_w_down, m_kv_norm_g, m_w_kv, m_k_norm_g, m_b_norm_g, m_b_w_q, m_b_q_norm_g, m_b_rel_bias, m_b_w_o, m_b_ffn_norm_g, m_b_w_gu, m_b_w_down, v_a_norm_g, v_a_w_in, v_a_gn_g, v_a_w_o, v_a_ffn_norm_g, v_a_w_gu, v_a_w_down, v_kv_norm_g, v_w_kv, v_k_norm_g, v_b_norm_g, v_b_w_q, v_b_q_norm_g, v_b_rel_bias, v_b_w_o, v_b_ffn_norm_g, v_b_w_gu, v_b_w_down):
    args = (x, a_norm_g, a_w_in, a_gn_g, a_w_o, a_ffn_norm_g, a_w_gu, a_w_down, kv_norm_g, w_kv, k_norm_g,
            b_norm_g, b_w_q, b_q_norm_g, b_rel_bias, b_w_o, b_ffn_norm_g, b_w_gu, b_w_down)
    p = dict(zip(ARG_NAMES, args))
    m_all = dict(zip(WEIGHT_NAMES, (m_a_norm_g, m_a_w_in, m_a_gn_g, m_a_w_o, m_a_ffn_norm_g, m_a_w_gu,
                                    m_a_w_down, m_kv_norm_g, m_w_kv, m_k_norm_g, m_b_norm_g, m_b_w_q,
                                    m_b_q_norm_g, m_b_rel_bias, m_b_w_o, m_b_ffn_norm_g, m_b_w_gu, m_b_w_down)))
    v_all = dict(zip(WEIGHT_NAMES, (v_a_norm_g, v_a_w_in, v_a_gn_g, v_a_w_o, v_a_ffn_norm_g, v_a_w_gu,
                                    v_a_w_down, v_kv_norm_g, v_w_kv, v_k_norm_g, v_b_norm_g, v_b_w_q,
                                    v_b_q_norm_g, v_b_rel_bias, v_b_w_o, v_b_ffn_norm_g, v_b_w_gu, v_b_w_down)))
    xi, yi, ci = _my_place()
    me = 4 * xi + 2 * yi + ci
    big_names = [n for n, _ in BIG]

    big_local = {n: _big_shard(p[n], n) for n in big_names}
    shards = {n: a.astype(BF16) for n, a in big_local.items()}
    small_local = _pack_small({n: p[n] for n, _, _ in SMALL})
    w_in, small_all = _exchange(_GatherRider([shards["a_w_in"], small_local]), "gather_in")
    flat_g = small_all.reshape(N_DEV, -1)
    s_full, pos = {}, 0
    for n, length, sharded in SMALL:
        ln = length // N_DEV if sharded else length
        s_full[n] = flat_g[:, pos:pos + ln].reshape(1, -1) if sharded else p[n].reshape(1, -1)
        pos += ln

    parity = jnp.reshape(ci, (1,)).astype(jnp.int32)
    loss, grad_x, recv, g = _local_step(x[0], loss_target[0], shards, w_in, s_full, parity)

    partial = _pack_small({n: g[n] for n, _, _ in SMALL}, last=loss)
    summed = _sum_leading(_exchange(_GatherRider([partial]), "gather_gsmall")[0], "gsmall_sum")
    loss = summed[SMALL_ROWS - 1, SMALL_COLS - 1]
    g_small = _unpack_small(summed, local=False)
    for n, length, sharded in SMALL:
        if sharded:
            g_small[n] = lax.dynamic_slice(g_small[n], (me * (length // N_DEV),), (length // N_DEV,))

    grads, deltas, new_m, new_v = {}, {}, {}, {}
    for n in big_names:
        outs = _adamw(big_local[n], recv[n], _big_shard(m_all[n], n), _big_shard(v_all[n], n), "adamw_" + n)
        grads[n], deltas[n], new_m[n], new_v[n] = (_as_given(a, n, p[n].shape) for a in outs)
    pk = lambda src: _pack_small({n: src[n] for n, _, _ in SMALL})
    outs = _adamw(small_local, pk(g_small)[None], pk(m_all), pk(v_all), "adamw_small")
    g_s, d_s, nm_s, nv_s = (_unpack_small(a, local=True) for a in outs)
    for n, _, _ in SMALL:
        grads[n], deltas[n], new_m[n], new_v[n] = (a[n].reshape(p[n].shape) for a in (g_s, d_s, nm_s, nv_s))

    return (loss, grad_x[None], *[grads[n] for n in WEIGHT_NAMES], *[deltas[n] for n in WEIGHT_NAMES],
            *[new_m[n] for n in WEIGHT_NAMES], *[new_v[n] for n in WEIGHT_NAMES])
```

```python
import numpy as np
import jax
import jax.numpy as jnp
from jax import lax
from jax.experimental import pallas as pl
from jax.experimental.pallas import tpu as pltpu

F32 = jnp.float32
BF16 = jnp.bfloat16

N_DEV = 8
D_MODEL = 1024
CHUNK = 64
EPS = 1e-6
RET_HEADS, RET_DK, RET_DV = 4, 256, 512
RET_STEP = 4
RET_Q_COLS = RET_HEADS * RET_DK
RET_V_COLS = RET_HEADS * RET_DV
ATT_HEADS, ATT_DH = 16, 64
PAST_CHUNKS = 8
REL_CLIP = 256
REL_TABLE = 2 * REL_CLIP + 1
FFN_HIDDEN = 2816
ROPE_BASE = 10000.0
LANES = 128
Q_BLOCK = 256
ATT_SUBS = 4
ATT_ROWS = 32
K_PAD = PAST_CHUNKS * CHUNK
K_WINDOW = Q_BLOCK + K_PAD
REL_BLK = 128
REL_DELTAS = Q_BLOCK // REL_BLK + K_WINDOW // REL_BLK - 1
REL_PAD = 640
NEG = -1e30
VMEM_LIMIT_V7X = 56 * 1024 * 1024
ADAM_LR, ADAM_B1, ADAM_B2, ADAM_EPS, ADAM_WD, ADAM_STEP = 1e-3, 0.9, 0.999, 1e-8, 0.01, 10
MESH = pl.DeviceIdType.MESH
ANY = pl.BlockSpec(memory_space=pl.ANY)


def _params(*semantics):
    return pltpu.CompilerParams(dimension_semantics=semantics, vmem_limit_bytes=VMEM_LIMIT_V7X)


def _pick(dim, cap, align):
    best = None
    for t in range(align, min(dim, cap) + 1, align):
        if dim % t == 0:
            best = t
    assert best is not None, (dim, cap, align)
    return best


def _dot(a, b):
    return lax.dot_general(a, b, (((1,), (0,)), ((), ())), preferred_element_type=F32)


def _dot_nt(a, b):
    return lax.dot_general(a, b, (((1,), (1,)), ((), ())), preferred_element_type=F32)


def _dot_tn(a, b):
    return lax.dot_general(a, b, (((0,), (0,)), ((), ())), preferred_element_type=F32)


def _split2(x):
    hi = x.astype(BF16)
    lo = (x - hi.astype(F32)).astype(BF16)
    return hi, lo


def _split3(x):
    hi = x.astype(BF16)
    r = x - hi.astype(F32)
    mid = r.astype(BF16)
    lo = (r - mid.astype(F32)).astype(BF16)
    return hi, mid, lo


def _sigmoid(x):
    return 1.0 / (1.0 + jnp.exp(-x))


def _accumulate(ref, part, step):
    @pl.when(step == 0)
    def _():
        ref[...] = part

    @pl.when(step > 0)
    def _():
        ref[...] += part


RELAY_AT_NUM, RELAY_AT_DEN = 3, 4


def _my_place():
    return lax.axis_index("x"), lax.axis_index("y"), lax.axis_index("c")


def _flip(v, bit):
    return 1 - v if bit else v


class _NoRelay:
    def relay(self, in_refs, out_refs, sems):
        pass


class _GatherRider:
    def __init__(self, xs):
        self.inputs = list(xs)
        n = len(xs)
        self.out_shape = [jax.ShapeDtypeStruct((N_DEV,) + x.shape, x.dtype) for x in xs]
        self.scratch = [pltpu.SemaphoreType.DMA((7, n)), pltpu.SemaphoreType.DMA((7, n)),
                        pltpu.SemaphoreType.DMA((n,))]
        self.results = None

    def _copies(self, x_refs, out_refs, sems):
        send_sems, recv_sems, local_sems = sems
        n = len(x_refs)
        x, y, c = _my_place()
        me, sibling = (x, y, c), (x, y, 1 - c)
        chips = [(1 - x, y), (x, 1 - y), (1 - x, 1 - y)]

        def slot(a, px, py, pc):
            return out_refs[a].at[4 * px + 2 * py + pc]

        def copy(k, a, block, to, own=False):
            return pltpu.make_async_remote_copy(
                src_ref=x_refs[a] if own else slot(a, *block), dst_ref=slot(a, *block),
                send_sem=send_sems.at[k, a], recv_sem=recv_sems.at[k, a],
                device_id=to, device_id_type=MESH)

        mine = [pltpu.make_async_copy(x_refs[a], slot(a, *me), local_sems.at[a]) for a in range(n)]
        first = []
        for a in range(n):
            first.append(copy(0, a, me, sibling, own=True))
            first += [copy(1 + j, a, me, (*chip, c), own=True) for j, chip in enumerate(chips)]
        return n, c, me, sibling, chips, copy, mine, first

    def start(self, x_refs, out_refs, sems):
        _, _, _, _, _, _, mine, first = self._copies(x_refs, out_refs, sems)
        for cp in mine + first:
            cp.start()

    def relay(self, x_refs, out_refs, sems):
        n, c, me, sibling, chips, copy, _, _ = self._copies(x_refs, out_refs, sems)
        for j, chip in enumerate(chips):
            for a in range(n):
                copy(1 + j, a, (*chip, c), me).wait_recv()
                copy(4 + j, a, (*chip, c), sibling).start()

    def finish(self, x_refs, out_refs, sems):
        n, c, me, sibling, chips, copy, mine, first = self._copies(x_refs, out_refs, sems)
        passed = [copy(4 + j, a, (*chip, c), sibling) for j, chip in enumerate(chips) for a in range(n)]
        for a in range(n):
            copy(0, a, sibling, me).wait_recv()
            for j, chip in enumerate(chips):
                copy(4 + j, a, (*chip, 1 - c), me).wait_recv()
        for cp in first + passed:
            cp.wait_send()
        for cp in mine:
            cp.wait()


class _ScatterRider(_NoRelay):
    def __init__(self, gs):
        self.inputs = list(gs)
        n = len(gs)
        self.out_shape = [jax.ShapeDtypeStruct(g.shape, g.dtype) for g in gs]
        self.scratch = [pltpu.SemaphoreType.DMA((7, n)), pltpu.SemaphoreType.DMA((7, n)),
                        pltpu.SemaphoreType.DMA((n,))]
        self.results = None

    def _copies(self, g_refs, out_refs, sems):
        send_sems, recv_sems, local_sems = sems
        x, y, c = _my_place()
        me = 4 * x + 2 * y + c
        mine, copies = [], []
        for a in range(len(g_refs)):
            mine.append(pltpu.make_async_copy(g_refs[a].at[me], out_refs[a].at[me], local_sems.at[a]))
            for k in range(1, N_DEV):
                px, py, pc = _flip(x, k & 4), _flip(y, k & 2), _flip(c, k & 1)
                copies.append(pltpu.make_async_remote_copy(
                    src_ref=g_refs[a].at[4 * px + 2 * py + pc], dst_ref=out_refs[a].at[me],
                    send_sem=send_sems.at[k - 1, a], recv_sem=recv_sems.at[k - 1, a],
                    device_id=(px, py, pc), device_id_type=MESH))
        return mine, copies

    def start(self, g_refs, out_refs, sems):
        mine, copies = self._copies(g_refs, out_refs, sems)
        for cp in mine + copies:
            cp.start()

    def finish(self, g_refs, out_refs, sems):
        mine, copies = self._copies(g_refs, out_refs, sems)
        for cp in copies + mine:
            cp.wait()


class _SiblingSwapRider(_NoRelay):
    def __init__(self, gs):
        self.inputs = list(gs)
        n = len(gs)
        self.out_shape = [jax.ShapeDtypeStruct((4,) + g.shape[1:], g.dtype) for g in gs]
        self.scratch = [pltpu.SemaphoreType.DMA((4, n)), pltpu.SemaphoreType.DMA((4, n))]
        self.results = None

    def _copies(self, g_refs, out_refs, sems):
        send_sems, recv_sems = sems
        x, y, c = _my_place()
        return [pltpu.make_async_remote_copy(
            src_ref=g_refs[a].at[2 * k + 1 - c], dst_ref=out_refs[a].at[k],
            send_sem=send_sems.at[k, a], recv_sem=recv_sems.at[k, a],
            device_id=(x, y, 1 - c), device_id_type=MESH)
            for a in range(len(g_refs)) for k in range(4)]

    def start(self, g_refs, out_refs, sems):
        for cp in self._copies(g_refs, out_refs, sems):
            cp.start()

    def finish(self, g_refs, out_refs, sems):
        for cp in self._copies(g_refs, out_refs, sems):
            cp.wait()


class _ChipScatterRider(_NoRelay):
    def __init__(self, ps):
        self.inputs = list(ps)
        n = len(ps)
        self.out_shape = [jax.ShapeDtypeStruct(p.shape, p.dtype) for p in ps]
        self.scratch = [pltpu.SemaphoreType.DMA((3, n)), pltpu.SemaphoreType.DMA((3, n)),
                        pltpu.SemaphoreType.DMA((n,))]
        self.results = None

    def _copies(self, p_refs, out_refs, sems):
        send_sems, recv_sems, local_sems = sems
        x, y, c = _my_place()
        my_chip = 2 * x + y
        chips = [(1 - x, y), (x, 1 - y), (1 - x, 1 - y)]
        n = len(p_refs)
        mine = [pltpu.make_async_copy(p_refs[a].at[my_chip], out_refs[a].at[my_chip], local_sems.at[a])
                for a in range(n)]
        copies = [pltpu.make_async_remote_copy(
            src_ref=p_refs[a].at[2 * cx + cy], dst_ref=out_refs[a].at[my_chip],
            send_sem=send_sems.at[j, a], recv_sem=recv_sems.at[j, a],
            device_id=(cx, cy, c), device_id_type=MESH)
            for a in range(n) for j, (cx, cy) in enumerate(chips)]
        return mine, copies

    def start(self, p_refs, out_refs, sems):
        mine, copies = self._copies(p_refs, out_refs, sems)
        for cp in mine + copies:
            cp.start()

    def finish(self, p_refs, out_refs, sems):
        mine, copies = self._copies(p_refs, out_refs, sems)
        for cp in copies + mine:
            cp.wait()


def _call(body, name, grid, in_specs, out_specs, out_shape, scratch, semantics, args, rider=None):
    in_specs, out_specs, out_shape, scratch = list(in_specs), list(out_specs), list(out_shape), list(scratch)
    if rider is None:
        return list(pl.pallas_call(
            body, name=name, grid=grid, in_specs=in_specs, out_specs=out_specs, out_shape=out_shape,
            scratch_shapes=scratch, compiler_params=_params(*semantics))(*args))
    n_in, n_out, n_scr = len(in_specs), len(out_specs), len(scratch)
    r_in, r_out = len(rider.inputs), len(rider.out_shape)

    def wrapped(*refs):
        cuts = np.cumsum([0, n_in, r_in, n_out, r_out, n_scr])
        hi, ri, ho, ro, hs = (refs[cuts[i]:cuts[i + 1]] for i in range(5))
        rs = refs[cuts[5]:]
        step, steps = pl.program_id(0), grid[0]
        for d in range(1, len(grid)):
            step, steps = step * grid[d] + pl.program_id(d), steps * grid[d]

        @pl.when(step == 0)
        def _():
            rider.start(ri, ro, rs)

        body(*hi, *ho, *hs)

        @pl.when(step == (steps * RELAY_AT_NUM) // RELAY_AT_DEN)
        def _():
            rider.relay(ri, ro, rs)

        @pl.when(step == steps - 1)
        def _():
            rider.finish(ri, ro, rs)

    outs = pl.pallas_call(
        wrapped, name=name, grid=grid,
        in_specs=in_specs + [ANY] * r_in, out_specs=out_specs + [ANY] * r_out,
        out_shape=out_shape + rider.out_shape, scratch_shapes=scratch + rider.scratch,
        compiler_params=_params(*(["arbitrary"] * len(grid))),
    )(*args, *rider.inputs)
    rider.results = list(outs[n_out:])
    return list(outs[:n_out])


def _gather_order():
    x, y, c = _my_place()
    chips = [(1 - x, y), (x, 1 - y), (1 - x, 1 - y)]
    ids = [4 * x + 2 * y + c, 4 * x + 2 * y + 1 - c]
    ids += [4 * cx + 2 * cy + c for cx, cy in chips] + [4 * cx + 2 * cy + 1 - c for cx, cy in chips]
    return jnp.stack(ids).astype(jnp.int32)


def _proj_gather(x, norm_g, w_shard, name):
    t, d = x.shape
    cols = w_shard.shape[1]
    tm = _pick(t, MM_CAP_MN, 16)
    ni = t // tm
    rider = _GatherRider([w_shard])

    def body(ord_ref, x_ref, g_ref, wsh_ref, proj_ref, wg_ref, h_all, bbuf, bsem, send_sems, recv_sems, local_sems):
        j, i = pl.program_id(0), pl.program_id(1)
        sems = (send_sems, recv_sems, local_sems)
        n, c, me, sibling, chips, copy, mine, first = rider._copies([wsh_ref], [wg_ref], sems)
        rows = pl.ds(pl.multiple_of(i * tm, tm), tm)

        def load(step, src):
            return pltpu.make_async_copy(src, bbuf.at[step % 2], bsem.at[step % 2])

        @pl.when(jnp.logical_and(j == 0, i == 0))
        def _():
            for cp in mine + first:
                cp.start()
            load(0, wsh_ref).start()

        for step in range(N_DEV):
            @pl.when(jnp.logical_and(j == step, i == 0))
            def _(step=step):
                load(step, wsh_ref).wait()
                nxt = step + 1
                if nxt == 1:
                    copy(0, 0, sibling, me).wait_recv()
                elif nxt <= 4:
                    k = nxt - 2
                    copy(1 + k, 0, (*chips[k], c), me).wait_recv()
                    copy(4 + k, 0, (*chips[k], c), sibling).start()
                elif nxt < N_DEV:
                    k = nxt - 5
                    copy(4 + k, 0, (*chips[k], 1 - c), me).wait_recv()
                if nxt < N_DEV:
                    load(nxt, wg_ref.at[ord_ref[nxt]]).start()

        @pl.when(j == 0)
        def _():
            groups = []
            for r in range(0, tm, NORM_ROWS):
                xv = x_ref[r:r + NORM_ROWS, :]
                rstd = lax.rsqrt(jnp.mean(xv * xv, axis=-1, keepdims=True) + EPS)
                groups.append((xv * rstd * g_ref[...]).astype(BF16))
            h_all[rows, :] = jnp.concatenate(groups, axis=0)

        proj_ref[...] = _dot(h_all[rows, :], bbuf[j % 2])

        @pl.when(jnp.logical_and(j == N_DEV - 1, i == ni - 1))
        def _():
            passed = [copy(4 + k, 0, (*chips[k], c), sibling) for k in range(3)]
            for cp in first + passed:
                cp.wait_send()
            for cp in mine:
                cp.wait()

    proj, gathered = pl.pallas_call(
        body, name=name,
        grid_spec=pltpu.PrefetchScalarGridSpec(
            num_scalar_prefetch=1, grid=(N_DEV, ni),
            in_specs=[pl.BlockSpec((tm, d), lambda j, i, o: (jnp.where(j == 0, i, ni - 1), 0)),
                      pl.BlockSpec((1, d), lambda j, i, o: (0, 0)), ANY],
            out_specs=[pl.BlockSpec((tm, cols), lambda j, i, o: (i, o[j])), ANY],
            scratch_shapes=[pltpu.VMEM((t, d), BF16), pltpu.VMEM((2, d, cols), BF16),
                            pltpu.SemaphoreType.DMA((2,))] + rider.scratch),
        out_shape=[jax.ShapeDtypeStruct((t, N_DEV * cols), F32)] + rider.out_shape,
        compiler_params=_params("arbitrary", "arbitrary"),
    )(_gather_order(), x, norm_g, w_shard)
    return proj, gathered


def _exchange(rider, name):
    r_in, r_out = len(rider.inputs), len(rider.out_shape)

    def body(*refs):
        ri, ro, rs = refs[:r_in], refs[r_in:r_in + r_out], refs[r_in + r_out:]
        rider.start(ri, ro, rs)
        rider.relay(ri, ro, rs)
        rider.finish(ri, ro, rs)

    return list(pl.pallas_call(
        body, name=name, in_specs=[ANY] * r_in, out_specs=[ANY] * r_out,
        out_shape=rider.out_shape, scratch_shapes=rider.scratch)(*rider.inputs))


MM_CAP_MN = 1024
MM_CAP_M_GRAD = 1408
MM_CAP_N = 1536
MM_CAP_K = 3072
MM_CAP_K_TOKENS = 2048
MM_CAP_K_RMS = 8192
MM_CAP_M_RMS = 512
NORM_ROWS = 256


def _mm(a, b, mode, name, out_dtype=F32, res=None, out_block=None, epilogue=None, extra=None, norm_g=None,
        norm_b=False, rider=None):
    a3, b3 = a.ndim == 3, b.ndim == 3
    um = un = uk = None
    if mode in ("nn", "nt"):
        if a3:
            m, uk = a.shape[1:]
            k = a.shape[0] * uk
        else:
            m, k = a.shape
    else:
        if a3:
            k, um = a.shape[1:]
            m = a.shape[0] * um
        else:
            k, m = a.shape
    if mode in ("nn", "tn"):
        if b3:
            kb, un = b.shape[1:]
            n = b.shape[0] * un
        else:
            kb, n = b.shape
        assert kb == k, (a.shape, b.shape, mode)
    else:
        if b3:
            n, ukb = b.shape[1:]
            assert b.shape[0] * ukb == k and uk in (None, ukb), (a.shape, b.shape, mode)
            uk = ukb
        else:
            n, kb = b.shape
            assert kb == k, (a.shape, b.shape, mode)
    if out_block is not None:
        assert un in (None, out_block)
        un = out_block

    def tile(dim, unit, cap, align):
        if unit is None:
            return _pick(dim, cap, align), 1
        c = max(1, cap // unit)
        while (dim // unit) % c:
            c -= 1
        return unit, c

    cap_m = MM_CAP_M_GRAD if mode == "tn" else (MM_CAP_M_RMS if epilogue == "rms_bwd" else MM_CAP_MN)
    um, cm = tile(m, um, cap_m, 128 if mode == "tn" else 16)
    un, cn = tile(n, un, MM_CAP_N, 128)
    cap_k = MM_CAP_K_TOKENS if mode == "tn" else (MM_CAP_K_RMS if epilogue == "rms_bwd" else MM_CAP_K)
    uk, ck = tile(k, uk, cap_k, 128)
    if epilogue == "rms_bwd":
        assert mode != "tn" and n == D_MODEL and cm == cn == 1 and res is None and out_block is None
    if epilogue == "loss":
        assert n == D_MODEL and cm == cn == 1 and res is not None and out_block is None
    if norm_g is not None and norm_b:
        assert mode == "tn" and not b3 and n == D_MODEL and cn == 1
    elif norm_g is not None:
        assert not a3 and (m if mode == "tn" else k) == D_MODEL and (cm if mode == "tn" else ck) == 1
    if epilogue == "swiglu":
        assert res is None and ((mode == "nn" and b3 and out_block is None) or
                                (mode == "nt" and not b3 and out_block is not None))
        cn = 2
    if epilogue == "swiglu_bwd":
        assert mode == "nt" and out_block is not None and extra is not None and res is None
        cn = 1
    tm, tn, tk = cm * um, cn * un, ck * uk
    nk = k // tk
    dot = {"nn": _dot, "nt": _dot_nt, "tn": _dot_tn}[mode]
    half = n // un // 2
    blocked_out = out_block is not None or epilogue in ("swiglu", "swiglu_bwd")
    extras = [] if extra is None else (list(extra) if isinstance(extra, (tuple, list)) else [extra])

    def sl(idx, unit, count):
        return slice(None) if count == 1 else slice(idx * unit, (idx + 1) * unit)

    def body(*refs):
        a_ref, b_ref = refs[0], refs[1]
        pos = 2
        r_ref = ng_ref = None
        if res is not None:
            r_ref, pos = refs[pos], pos + 1
        e_refs, pos = refs[pos:pos + len(extras)], pos + len(extras)
        if norm_g is not None:
            ng_ref, pos = refs[pos], pos + 1
        outs, acc_ref = refs[pos:-1], refs[-1]
        kk = pl.program_id(2)

        def normed(x_ref):
            groups = []
            for r in range(0, x_ref.shape[0], NORM_ROWS):
                xv = x_ref[r:r + NORM_ROWS, :]
                rstd = lax.rsqrt(jnp.mean(xv * xv, axis=-1, keepdims=True) + EPS)
                groups.append((xv * rstd * ng_ref[...]).astype(BF16))
            return jnp.concatenate(groups, axis=0)

        def a_blk(mi, ki):
            if norm_g is not None and not norm_b:
                return normed(a_ref)
            if mode in ("nn", "nt"):
                return a_ref[ki] if a3 else a_ref[:, sl(ki, uk, ck)]
            return a_ref[mi] if a3 else a_ref[:, sl(mi, um, cm)]

        def b_blk(ki, ni):
            if norm_b:
                return normed(b_ref)
            if epilogue == "swiglu":
                return b_ref[ni, 0]
            if mode in ("nn", "tn"):
                return b_ref[ni] if b3 else b_ref[sl(ki, uk, ck), sl(ni, un, cn)]
            return b_ref[ki][sl(ni, un, cn), :] if b3 else b_ref[sl(ni, un, cn), sl(ki, uk, ck)]

        parts = {}
        for mi in range(cm):
            for ni in range(cn):
                part = None
                for ki in range(ck):
                    d = dot(a_blk(mi, ki).astype(BF16), b_blk(ki, ni).astype(BF16))
                    part = d if part is None else part + d
                parts[mi, ni] = part

        def finish(total):
            if epilogue == "swiglu":
                gate, up = total[0, 0], total[0, 1]
                outs[0][0, 0] = gate.astype(BF16)
                outs[0][1, 0] = up.astype(BF16)
                outs[1][0] = (gate * _sigmoid(gate) * up).astype(BF16)
                return
            if epilogue == "swiglu_bwd":
                dact = total[0, 0]
                gate, up = e_refs[0][0, 0].astype(F32), e_refs[0][1, 0].astype(F32)
                sg = _sigmoid(gate)
                outs[0][0, 0] = (dact * up * (sg * (1.0 + gate * (1.0 - sg)))).astype(BF16)
                outs[0][1, 0] = (dact * (gate * sg)).astype(BF16)
                return
            if epilogue == "rms_bwd":
                x_ref, g_ref, dres_ref = e_refs
                dh, dg = total[0, 0], None
                for r in range(0, tm, NORM_ROWS):
                    rows = slice(r, r + NORM_ROWS)
                    xv, dhv = x_ref[rows, :], dh[rows, :]
                    rstd = lax.rsqrt(jnp.mean(xv * xv, axis=-1, keepdims=True) + EPS)
                    xh = xv * rstd
                    dyg = dhv * g_ref[...]
                    c = jnp.mean(dyg * xh, axis=-1, keepdims=True)
                    outs[0][rows, :] = dres_ref[rows, :] + rstd * (dyg - xh * c)
                    part = jnp.sum(dhv * xh, axis=0, keepdims=True)
                    dg = part if dg is None else dg + part
                _accumulate(outs[1], dg, pl.program_id(0))
                return
            if epilogue == "loss":
                diff = r_ref[...] + total[0, 0] - e_refs[0][...]
                outs[0][...] = diff * (1.0 / n)
                sq = jnp.sum(jnp.sum(diff * diff, axis=-1, keepdims=True), axis=0, keepdims=True)
                _accumulate(outs[1], sq * (0.5 / n), pl.program_id(0))
                return
            for (mi, ni), val in total.items():
                rows, cols = sl(mi, um, cm), sl(ni, un, cn)
                if res is not None:
                    val = r_ref[rows, cols] + val
                if blocked_out:
                    outs[0][ni, rows] = val.astype(out_dtype)
                else:
                    outs[0][rows, cols] = val.astype(out_dtype)

        if nk == 1:
            finish(parts)
        else:
            @pl.when(kk == 0)
            def _():
                for (mi, ni), val in parts.items():
                    acc_ref[mi * cn + ni] = val

            @pl.when(jnp.logical_and(kk > 0, kk < nk - 1))
            def _():
                for (mi, ni), val in parts.items():
                    acc_ref[mi * cn + ni] += val

            @pl.when(kk == nk - 1)
            def _():
                finish({key: acc_ref[key[0] * cn + key[1]] + val for key, val in parts.items()})

    if mode in ("nn", "nt"):
        a_spec = (pl.BlockSpec((ck, tm, uk), lambda i, j, kk: (kk, i, 0)) if a3
                  else pl.BlockSpec((tm, tk), lambda i, j, kk: (i, kk)))
    else:
        a_spec = (pl.BlockSpec((cm, tk, um), lambda i, j, kk: (i, kk, 0)) if a3
                  else pl.BlockSpec((tk, tm), lambda i, j, kk: (kk, i)))
    pair_spec = pl.BlockSpec((2, 1, tm, un), lambda i, j, kk: (0, j, i, 0))
    row_spec = pl.BlockSpec((tm, tn), lambda i, j, kk: (i, 0))
    vec_spec = pl.BlockSpec((1, tn), lambda i, j, kk: (0, 0))
    if epilogue == "swiglu" and mode == "nn":
        b = b.reshape(2, half, k, un)
        b_spec = pl.BlockSpec((2, 1, tk, un), lambda i, j, kk: (0, j, kk, 0))
    elif epilogue == "swiglu":
        b = b.reshape(2, half, un, k)
        b_spec = pl.BlockSpec((2, 1, un, tk), lambda i, j, kk: (0, j, 0, kk))
    elif mode in ("nn", "tn"):
        b_spec = (pl.BlockSpec((cn, tk, un), lambda i, j, kk: (j, kk, 0)) if b3
                  else pl.BlockSpec((tk, tn), lambda i, j, kk: (kk, j)))
    else:
        b_spec = (pl.BlockSpec((ck, tn, uk), lambda i, j, kk: (kk, j, 0)) if b3
                  else pl.BlockSpec((tn, tk), lambda i, j, kk: (j, kk)))
    if epilogue == "swiglu":
        out_specs = [pair_spec, pl.BlockSpec((1, tm, un), lambda i, j, kk: (j, i, 0))]
        out_shape = [jax.ShapeDtypeStruct((2, half, m, un), BF16), jax.ShapeDtypeStruct((half, m, un), BF16)]
    elif epilogue == "swiglu_bwd":
        out_specs = [pair_spec]
        out_shape = [jax.ShapeDtypeStruct(extra.shape, BF16)]
    elif epilogue == "rms_bwd":
        out_specs = [row_spec, vec_spec]
        out_shape = [jax.ShapeDtypeStruct((m, n), F32), jax.ShapeDtypeStruct((1, n), F32)]
    elif epilogue == "loss":
        out_specs = [row_spec, pl.BlockSpec((1, 1), lambda i, j, kk: (0, 0))]
        out_shape = [jax.ShapeDtypeStruct((m, n), F32), jax.ShapeDtypeStruct((1, 1), F32)]
    elif blocked_out:
        out_specs = [pl.BlockSpec((cn, tm, un), lambda i, j, kk: (j, i, 0))]
        out_shape = [jax.ShapeDtypeStruct((n // un, m, un), out_dtype)]
    else:
        out_specs = [pl.BlockSpec((tm, tn), lambda i, j, kk: (i, j))]
        out_shape = [jax.ShapeDtypeStruct((m, n), out_dtype)]
    in_specs, args = [a_spec, b_spec], [a, b]
    if res is not None:
        in_specs.append(pl.BlockSpec((tm, tn), lambda i, j, kk: (i, j)))
        args.append(res)
    if epilogue == "swiglu_bwd":
        in_specs.append(pair_spec)
    elif epilogue == "rms_bwd":
        in_specs += [row_spec, vec_spec, row_spec]
    elif epilogue == "loss":
        in_specs.append(row_spec)
    args += extras
    if norm_g is not None:
        in_specs.append(pl.BlockSpec((1, D_MODEL), lambda i, j, kk: (0, 0)))
        args.append(norm_g)
    semantics = ("arbitrary",) * 3 if epilogue in ("rms_bwd", "loss") else ("parallel", "parallel", "arbitrary")
    out = _call(body, name, (m // tm, n // tn, nk), in_specs, out_specs, out_shape,
                [pltpu.VMEM((cm * cn, um, un), F32)], semantics, args, rider)
    return out if epilogue in ("swiglu", "rms_bwd", "loss") else out[0]


def _head_sums(v, ind):
    hi, lo = _split2(v)
    return _dot(hi, ind) + _dot(lo, ind)


def _head_spread(per_head, ind):
    hi, lo = _split2(per_head)
    return _dot_nt(hi, ind) + _dot_nt(lo, ind)


def _head_rstd(xv, ind):
    return _head_spread(lax.rsqrt(_head_sums(xv * xv, ind) * (1.0 / ATT_DH) + EPS), ind)


def _hn_bwd_math(xv, gv, ind, dyv, scale):
    rstd = _head_rstd(xv, ind)
    xh = xv * rstd
    dyn = dyv * scale
    dyg = dyn * gv
    dx = rstd * (dyg - xh * _head_spread(_head_sums(dyg * xh, ind) * (1.0 / ATT_DH), ind))
    return dx, jnp.sum(dyn * xh, axis=0, keepdims=True)


def _q_hnorm(x, g_tiled, bd, scale, name):
    t, d = x.shape
    tm = _pick(t, 512, 16)

    def body(x_ref, g_ref, bd_ref, o_ref):
        xv = x_ref[...]
        o_ref[...] = (xv * _head_rstd(xv, bd_ref[...]) * g_ref[...] * scale).astype(BF16)

    return pl.pallas_call(
        body, name=name, grid=(t // tm,),
        in_specs=[pl.BlockSpec((tm, d), lambda i: (i, 0)), pl.BlockSpec((1, d), lambda i: (0, 0)),
                  pl.BlockSpec((d, LANES), lambda i: (0, 0))],
        out_specs=pl.BlockSpec((tm, d), lambda i: (i, 0)),
        out_shape=jax.ShapeDtypeStruct((t, d), BF16),
        compiler_params=_params("parallel"),
    )(x, g_tiled, bd)


def _q_dhnorm(x, g_tiled, bd, dy, scale, name):
    t, d = x.shape
    tm = _pick(t, 512, 16)

    def body(x_ref, g_ref, bd_ref, dy_ref, dx_ref, dg_ref):
        dx, part = _hn_bwd_math(x_ref[...], g_ref[...], bd_ref[...], dy_ref[...], scale)
        dx_ref[...] = dx.astype(BF16)
        _accumulate(dg_ref, part, pl.program_id(0))

    row = pl.BlockSpec((tm, d), lambda i: (i, 0))
    vec = pl.BlockSpec((1, d), lambda i: (0, 0))
    return pl.pallas_call(
        body, name=name, grid=(t // tm,),
        in_specs=[row, vec, pl.BlockSpec((d, LANES), lambda i: (0, 0)), row],
        out_specs=[row, vec],
        out_shape=[jax.ShapeDtypeStruct((t, d), BF16), jax.ShapeDtypeStruct((1, d), F32)],
        compiler_params=_params("arbitrary"),
    )(x, g_tiled, bd, dy)


def _kv_prep(kv, g_tiled, bd, name):
    t = kv.shape[0]
    d = D_MODEL
    tm = K_PAD
    assert t % tm == 0

    def body(k_ref, v_ref, g_ref, bd_ref, kp_ref, vp_ref):
        i = pl.program_id(0)

        @pl.when(i == 0)
        def _():
            kp_ref[...] = jnp.zeros_like(kp_ref)
            vp_ref[...] = jnp.zeros_like(vp_ref)

        @pl.when(i > 0)
        def _():
            xv = k_ref[...]
            kp_ref[...] = (xv * _head_rstd(xv, bd_ref[...]) * g_ref[...]).astype(BF16)
            vp_ref[...] = v_ref[...].astype(BF16)

    shp = jax.ShapeDtypeStruct((t + K_PAD, d), BF16)
    out = pl.BlockSpec((tm, d), lambda i: (i, 0))
    return pl.pallas_call(
        body, name=name, grid=(t // tm + 1,),
        in_specs=[pl.BlockSpec((tm, d), lambda i: (jnp.maximum(i - 1, 0), 0)),
                  pl.BlockSpec((tm, d), lambda i: (jnp.maximum(i - 1, 0), 1)),
                  pl.BlockSpec((1, d), lambda i: (0, 0)), pl.BlockSpec((d, LANES), lambda i: (0, 0))],
        out_specs=[out, out], out_shape=[shp, shp],
        compiler_params=_params("arbitrary"),
    )(kv, kv, g_tiled, bd)


def _kv_dprep(kv, g_tiled, bd, dkp_t, dvp_t, name):
    t = kv.shape[0]
    d = D_MODEL
    tm = K_PAD

    def body(k_ref, g_ref, bd_ref, dk_ref, dv_ref, o_ref, dg_ref):
        dx, part = _hn_bwd_math(k_ref[...], g_ref[...], bd_ref[...], dk_ref[...].T, 1.0)
        o_ref[:, :d] = dx.astype(BF16)
        o_ref[:, d:] = dv_ref[...].T.astype(BF16)
        _accumulate(dg_ref, part, pl.program_id(0))

    vec = pl.BlockSpec((1, d), lambda i: (0, 0))
    padded = pl.BlockSpec((d, tm), lambda i: (0, i + 1))
    return pl.pallas_call(
        body, name=name, grid=(t // tm,),
        in_specs=[pl.BlockSpec((tm, d), lambda i: (i, 0)), vec, pl.BlockSpec((d, LANES), lambda i: (0, 0)),
                  padded, padded],
        out_specs=[pl.BlockSpec((tm, 2 * d), lambda i: (i, 0)), vec],
        out_shape=[jax.ShapeDtypeStruct((t, 2 * d), BF16), jax.ShapeDtypeStruct((1, d), F32)],
        compiler_params=_params("arbitrary"),
    )(kv, g_tiled, bd, dkp_t, dvp_t)


def _ret_consts(t):
    h = np.arange(RET_HEADS, dtype=np.float32)
    lg = np.log(np.float32(1.0) - np.float32(2.0) ** (np.float32(-5.0) - h)).astype(np.float32)
    tt = np.arange(CHUNK, dtype=np.float32)
    intra = np.exp(lg[:, None, None] * np.abs(tt[:, None] - tt[None, :])).astype(np.float32)
    q_dec = np.exp(lg[:, None] * (tt + 1.0)).astype(np.float32)
    k_dec = np.exp(lg[:, None] * (CHUNK - 1.0 - tt)).astype(np.float32)
    s_dec = [float(v) for v in np.exp(lg * np.float32(CHUNK)).astype(np.float32)]
    qd = np.broadcast_to(q_dec[:, :, None], (RET_HEADS, CHUNK, RET_DK)).copy()
    kd = np.broadcast_to(k_dec[:, :, None], (RET_HEADS, CHUNK, RET_DK)).copy()
    half = RET_DK // 2
    inv_freq = ROPE_BASE ** (-jnp.arange(half, dtype=F32) / half)
    ang = jnp.arange(t).astype(F32)[:, None] * inv_freq[None, :]
    return jnp.asarray(intra), jnp.asarray(qd), jnp.asarray(kd), s_dec, jnp.cos(ang), jnp.sin(ang)


def _rope(x, cos, sin):
    half = RET_DK // 2
    x1, x2 = x[:, :half], x[:, half:]
    return jnp.concatenate([x1 * cos - x2 * sin, x1 * sin + x2 * cos], axis=-1)


def _unrope(d, cos, sin):
    half = RET_DK // 2
    d1, d2 = d[:, :half], d[:, half:]
    return jnp.concatenate([d1 * cos + d2 * sin, d2 * cos - d1 * sin], axis=-1)


def _ret_slices(h):
    q = slice(h * RET_DK, (h + 1) * RET_DK)
    k = slice(RET_Q_COLS + h * RET_DK, RET_Q_COLS + (h + 1) * RET_DK)
    v = slice(2 * RET_Q_COLS + h * RET_DV, 2 * RET_Q_COLS + (h + 1) * RET_DV)
    g = slice(2 * RET_Q_COLS + RET_V_COLS + h * RET_DV, 2 * RET_Q_COLS + RET_V_COLS + (h + 1) * RET_DV)
    o = slice(h * RET_DV, (h + 1) * RET_DV)
    return q, k, v, g, o


def _ret_fwd(proj, gn, consts, name, rider=None):
    t, cols = proj.shape
    n = t // CHUNK
    intra, qd, kd, s_dec, cos, sin = consts
    k_scale = RET_DK ** -0.5

    def body(p_ref, cos_ref, sin_ref, intra_ref, qd_ref, kd_ref, gn_ref, y_ref, o_ref, st_ref, state):
        i = pl.program_id(0)

        @pl.when(i == 0)
        def _():
            state[...] = jnp.zeros_like(state)

        for c in range(RET_STEP):
            rows = slice(c * CHUNK, (c + 1) * CHUNK)
            cosv, sinv = cos_ref[rows, :], sin_ref[rows, :]
            for h in range(RET_HEADS):
                qs, ks, vs, gs, os_ = _ret_slices(h)
                qr = _rope(p_ref[rows, qs], cosv, sinv)
                kr = _rope(p_ref[rows, ks], cosv, sinv) * k_scale
                vb = p_ref[rows, vs].astype(BF16)
                gv = p_ref[rows, gs]
                scores = _dot_nt(qr.astype(BF16), kr.astype(BF16)) * intra_ref[h]
                s_old = state[h]
                s_old_b = s_old.astype(BF16)
                st_ref[c, h] = s_old_b
                o = _dot(scores.astype(BF16), vb) + _dot((qr * qd_ref[h]).astype(BF16), s_old_b)
                state[h] = s_old * s_dec[h] + _dot_tn((kr * kd_ref[h]).astype(BF16), vb)
                rstd = lax.rsqrt(jnp.mean(o * o, axis=-1, keepdims=True) + EPS)
                on = o * rstd * gn_ref[:, os_]
                o_ref[rows, os_] = o
                y_ref[rows, os_] = (gv * _sigmoid(gv) * on).astype(BF16)

    full3 = lambda a: pl.BlockSpec(a.shape, lambda i: (0, 0, 0))
    step = RET_STEP * CHUNK
    return _call(
        body, name, (n // RET_STEP,),
        [pl.BlockSpec((step, cols), lambda i: (i, 0)),
         pl.BlockSpec((step, RET_DK // 2), lambda i: (i, 0)),
         pl.BlockSpec((step, RET_DK // 2), lambda i: (i, 0)),
         full3(intra), full3(qd), full3(kd),
         pl.BlockSpec((1, RET_V_COLS), lambda i: (0, 0))],
        [pl.BlockSpec((step, RET_V_COLS), lambda i: (i, 0)),
         pl.BlockSpec((step, RET_V_COLS), lambda i: (i, 0)),
         pl.BlockSpec((RET_STEP, RET_HEADS, RET_DK, RET_DV), lambda i: (i, 0, 0, 0))],
        [jax.ShapeDtypeStruct((t, RET_V_COLS), BF16),
         jax.ShapeDtypeStruct((t, RET_V_COLS), F32),
         jax.ShapeDtypeStruct((n, RET_HEADS, RET_DK, RET_DV), BF16)],
        [pltpu.VMEM((RET_HEADS, RET_DK, RET_DV), F32)], ("arbitrary",),
        (proj, cos, sin, intra, qd, kd, gn), rider)


def _ret_bwd(proj, gn, o_saved, states, dy, consts, name, rider=None):
    t, cols = proj.shape
    n = t // CHUNK
    intra, qd, kd, s_dec, cos, sin = consts
    k_scale = RET_DK ** -0.5

    def body(p_ref, cos_ref, sin_ref, intra_ref, qd_ref, kd_ref, gn_ref, o_ref, st_ref, dy_ref,
             dp_ref, dgn_ref, dstate):
        i = pl.program_id(0)

        @pl.when(i == 0)
        def _():
            dstate[...] = jnp.zeros_like(dstate)

        dgn = None
        for c in reversed(range(RET_STEP)):
            rows = slice(c * CHUNK, (c + 1) * CHUNK)
            cosv, sinv = cos_ref[rows, :], sin_ref[rows, :]
            dgn_parts = []
            for h in range(RET_HEADS):
                qs, ks, vs, gs, os_ = _ret_slices(h)
                qr = _rope(p_ref[rows, qs], cosv, sinv)
                kr = _rope(p_ref[rows, ks], cosv, sinv) * k_scale
                qb, kb = qr.astype(BF16), kr.astype(BF16)
                vb = p_ref[rows, vs].astype(BF16)
                gv = p_ref[rows, gs]
                ov = o_ref[rows, os_]
                dyv = dy_ref[rows, os_]
                gnv = gn_ref[:, os_]
                sg = _sigmoid(gv)
                rstd = lax.rsqrt(jnp.mean(ov * ov, axis=-1, keepdims=True) + EPS)
                oh = ov * rstd
                d_on = dyv * (gv * sg)
                dg = dyv * (oh * gnv) * (sg * (1.0 + gv * (1.0 - sg)))
                dgn_parts.append(jnp.sum(d_on * oh, axis=0, keepdims=True))
                d_oh = d_on * gnv
                do = rstd * (d_oh - oh * jnp.mean(d_oh * oh, axis=-1, keepdims=True))
                dob = do.astype(BF16)
                mask = intra_ref[h]
                a_b = (_dot_nt(qb, kb) * mask).astype(BF16)
                da_b = (_dot_nt(dob, vb) * mask).astype(BF16)
                ds_new = dstate[h]
                ds_new_b = ds_new.astype(BF16)
                s_old_b = st_ref[c, h]
                qdv, kdv = qd_ref[h], kd_ref[h]
                dv = _dot_tn(a_b, dob) + _dot((kr * kdv).astype(BF16), ds_new_b)
                dqr = _dot(da_b, kb) + _dot_nt(dob, s_old_b) * qdv
                dkr = _dot_tn(da_b, qb) + _dot_nt(vb, ds_new_b) * kdv
                dstate[h] = ds_new * s_dec[h] + _dot_tn((qr * qdv).astype(BF16), dob)
                dp_ref[rows, qs] = _unrope(dqr, cosv, sinv).astype(BF16)
                dp_ref[rows, ks] = _unrope(dkr * k_scale, cosv, sinv).astype(BF16)
                dp_ref[rows, vs] = dv.astype(BF16)
                dp_ref[rows, gs] = dg.astype(BF16)
            part = jnp.concatenate(dgn_parts, axis=-1)
            dgn = part if dgn is None else dgn + part
        _accumulate(dgn_ref, dgn, i)

    steps = n // RET_STEP
    step = RET_STEP * CHUNK
    rev = lambda i: (steps - 1 - i, 0)
    full3 = lambda a: pl.BlockSpec(a.shape, lambda i: (0, 0, 0))
    return _call(
        body, name, (steps,),
        [pl.BlockSpec((step, cols), rev),
         pl.BlockSpec((step, RET_DK // 2), rev),
         pl.BlockSpec((step, RET_DK // 2), rev),
         full3(intra), full3(qd), full3(kd),
         pl.BlockSpec((1, RET_V_COLS), lambda i: (0, 0)),
         pl.BlockSpec((step, RET_V_COLS), rev),
         pl.BlockSpec((RET_STEP, RET_HEADS, RET_DK, RET_DV), lambda i: (steps - 1 - i, 0, 0, 0)),
         pl.BlockSpec((step, RET_V_COLS), rev)],
        [pl.BlockSpec((step, cols), rev),
         pl.BlockSpec((1, RET_V_COLS), lambda i: (0, 0))],
        [jax.ShapeDtypeStruct((t, cols), BF16),
         jax.ShapeDtypeStruct((1, RET_V_COLS), F32)],
        [pltpu.VMEM((RET_HEADS, RET_DK, RET_DV), F32)], ("arbitrary",),
        (proj, cos, sin, intra, qd, kd, gn, o_saved, states, dy), rider)


def _att_common(q_ref, kp_ref, vp_ref, sub):
    blk = pl.program_id(1) * ATT_SUBS + sub
    start = pl.multiple_of(blk * Q_BLOCK, Q_BLOCK)
    kw = kp_ref[pl.ds(start, K_WINDOW), :]
    vw = vp_ref[pl.ds(start, K_WINDOW), :]
    kvalid = blk * Q_BLOCK - K_PAD + lax.broadcasted_iota(jnp.int32, (1, K_WINDOW), 1) >= 0
    lane = lax.broadcasted_iota(jnp.int32, (1, LANES), 1)
    qrows = slice(sub * Q_BLOCK, (sub + 1) * Q_BLOCK)
    return start, qrows, q_ref[qrows, :], kw, vw, kvalid, (lane < ATT_DH, lane >= ATT_DH)


def _row_groups():
    return [slice(r * ATT_ROWS, (r + 1) * ATT_ROWS) for r in range(Q_BLOCK // ATT_ROWS)]


def _lane_copies(x):
    return jnp.tile(x, (1, K_WINDOW // LANES))


def _att_specs(t, tp):
    qspec = pl.BlockSpec((ATT_SUBS * Q_BLOCK, LANES), lambda h, i: (i, h))
    kspec = pl.BlockSpec((tp, LANES), lambda h, i: (0, h))
    bspec = pl.BlockSpec((2, Q_BLOCK, K_WINDOW), lambda h, i: (h, 0, 0))
    return qspec, kspec, bspec


def _att_fwd(q, kp, vp, bias, name, rider=None):
    t, d = q.shape
    tp = kp.shape[0]

    def body(q_ref, kp_ref, vp_ref, bias_ref, o_ref, lse_ref, s_scr, p_scr, lse_scr):
        for sub in range(ATT_SUBS):
            _, qrows, q2, kw, vw, kvalid, sel = _att_common(q_ref, kp_ref, vp_ref, sub)
            for hh in range(2):
                s_scr[sub, hh] = _dot_nt(jnp.where(sel[hh], q2, 0), kw)
            for hh in range(2):
                for rows in _row_groups():
                    s = jnp.where(kvalid, s_scr[sub, hh, rows, :] + bias_ref[hh, rows, :], NEG)
                    m = jnp.max(s, axis=-1, keepdims=True)
                    e = jnp.exp(s - m)
                    l = jnp.sum(e, axis=-1, keepdims=True)
                    p_scr[sub, hh, rows, :] = (e * (1.0 / l)).astype(BF16)
                    lse_scr[sub, hh, rows, :] = jnp.broadcast_to(m + jnp.log(l), (ATT_ROWS, LANES))
            outs = [_dot(p_scr[sub, hh], vw) for hh in range(2)]
            o_ref[qrows, :] = jnp.where(sel[0], outs[0], outs[1]).astype(BF16)
            lse_ref[qrows, :] = jnp.where(sel[0], lse_scr[sub, 0], lse_scr[sub, 1])

    qspec, kspec, bspec = _att_specs(t, tp)
    return _call(body, name, (d // LANES, t // (ATT_SUBS * Q_BLOCK)), [qspec, kspec, kspec, bspec], [qspec, qspec],
                 [jax.ShapeDtypeStruct((t, d), BF16), jax.ShapeDtypeStruct((t, d), F32)],
                 [pltpu.VMEM((ATT_SUBS, 2, Q_BLOCK, K_WINDOW), F32),
                  pltpu.VMEM((ATT_SUBS, 2, Q_BLOCK, K_WINDOW), BF16),
                  pltpu.VMEM((ATT_SUBS, 2, Q_BLOCK, LANES), F32)],
                 ("parallel", "arbitrary"), (q, kp, vp, bias), rider)


def _att_bwd(q, kp, vp, bias, do, o, lse, name, rider=None):
    t, d = q.shape
    tp = kp.shape[0]

    def body(q_ref, kp_ref, vp_ref, bias_ref, do_ref, o_ref, lse_ref, dq_ref, dkp_ref, dvp_ref, db_ref,
             s_scr, dp_scr, p_scr, ds_scr, row_scr):
        @pl.when(pl.program_id(1) == 0)
        def _():
            dkp_ref[...] = jnp.zeros_like(dkp_ref)
            dvp_ref[...] = jnp.zeros_like(dvp_ref)
            db_ref[...] = jnp.zeros_like(db_ref)

        for sub in range(ATT_SUBS):
            start, qrows, q2, kw, vw, kvalid, sel = _att_common(q_ref, kp_ref, vp_ref, sub)
            do2 = do_ref[qrows, :]
            qm = [jnp.where(sel[hh], q2, 0) for hh in range(2)]
            dom = [jnp.where(sel[hh], do2, 0) for hh in range(2)]
            do_o = do2.astype(F32) * o_ref[qrows, :].astype(F32)
            lse2 = lse_ref[qrows, :]
            for hh in range(2):
                s_scr[sub, hh] = _dot_nt(qm[hh], kw)
                dp_scr[sub, hh] = _dot_nt(dom[hh], vw)
                lse_h = jnp.max(jnp.where(sel[hh], lse2, NEG), axis=-1, keepdims=True)
                delta = jnp.sum(jnp.where(sel[hh], do_o, 0.0), axis=-1, keepdims=True)
                row_scr[sub, hh, 0] = jnp.broadcast_to(lse_h, (Q_BLOCK, LANES))
                row_scr[sub, hh, 1] = jnp.broadcast_to(delta, (Q_BLOCK, LANES))
            for hh in range(2):
                for rows in _row_groups():
                    s = jnp.where(kvalid, s_scr[sub, hh, rows, :] + bias_ref[hh, rows, :], NEG)
                    p = jnp.exp(s - _lane_copies(row_scr[sub, hh, 0, rows, :]))
                    ds = p * (dp_scr[sub, hh, rows, :] - _lane_copies(row_scr[sub, hh, 1, rows, :]))
                    db_ref[hh, rows, :] += ds
                    p_scr[sub, hh, rows, :] = p.astype(BF16)
                    ds_scr[sub, hh, rows, :] = ds.astype(BF16)
            dqs = [_dot(ds_scr[sub, hh], kw) for hh in range(2)]
            dq_ref[qrows, :] = jnp.where(sel[0], dqs[0], dqs[1])
            dkp_ref[:, pl.ds(start, K_WINDOW)] += (_dot_tn(qm[0], ds_scr[sub, 0]) +
                                                   _dot_tn(qm[1], ds_scr[sub, 1]))
            dvp_ref[:, pl.ds(start, K_WINDOW)] += (_dot_tn(dom[0], p_scr[sub, 0]) +
                                                   _dot_tn(dom[1], p_scr[sub, 1]))

    qspec, kspec, bspec = _att_specs(t, tp)
    tspec = pl.BlockSpec((LANES, tp), lambda h, i: (h, 0))
    stage = lambda dt: pltpu.VMEM((ATT_SUBS, 2, Q_BLOCK, K_WINDOW), dt)
    return _call(body, name, (d // LANES, t // (ATT_SUBS * Q_BLOCK)),
                 [qspec, kspec, kspec, bspec, qspec, qspec, qspec],
                 [qspec, tspec, tspec, bspec],
                 [jax.ShapeDtypeStruct((t, d), F32),
                  jax.ShapeDtypeStruct((d, tp), F32),
                  jax.ShapeDtypeStruct((d, tp), F32),
                  jax.ShapeDtypeStruct((ATT_HEADS, Q_BLOCK, K_WINDOW), F32)],
                 [stage(F32), stage(F32), stage(BF16), stage(BF16),
                  pltpu.VMEM((ATT_SUBS, 2, 2, Q_BLOCK, LANES), F32)],
                 ("parallel", "arbitrary"), (q, kp, vp, bias, do, o, lse), rider)


def _rel_bin_matrix():
    rows = REL_DELTAS * 2 * REL_BLK
    rho = lax.broadcasted_iota(jnp.int32, (rows, REL_PAD), 0)
    col = lax.broadcasted_iota(jnp.int32, (rows, REL_PAD), 1)
    assert 2 * REL_BLK == 256
    delta = rho >> 8
    c = 255 - (rho & 255)
    dist = K_PAD + REL_BLK * (delta - (K_WINDOW // REL_BLK - 1)) + (c - (REL_BLK - 1))
    idx = jnp.clip(dist, -REL_CLIP, REL_CLIP) + REL_CLIP
    return col == idx


def _rel_expand(rel_pad, name):
    heads = rel_pad.shape[0]
    rows = REL_DELTAS * 2 * REL_BLK

    def body_bin(r_ref, o_ref):
        onehot = jnp.where(_rel_bin_matrix(), 1.0, 0.0).astype(BF16)
        hi, mid, lo = _split3(r_ref[...])
        o_ref[...] = _dot_nt(hi, onehot) + _dot_nt(mid, onehot) + _dot_nt(lo, onehot)

    by_delta = pl.pallas_call(
        body_bin, name=name + "_bin",
        out_shape=jax.ShapeDtypeStruct((heads, rows), F32),
        compiler_params=pltpu.CompilerParams(vmem_limit_bytes=VMEM_LIMIT_V7X),
    )(rel_pad)
    by_delta = by_delta.reshape(heads * REL_DELTAS, 2 * REL_BLK)

    def body_shift(t_ref, o_ref):
        tv = t_ref[...]
        for r in range(REL_BLK):
            o_ref[r] = pltpu.roll(tv, (r + REL_BLK) % (2 * REL_BLK), 1)[:, :REL_BLK]

    return pl.pallas_call(
        body_shift, name=name + "_shift",
        out_shape=jax.ShapeDtypeStruct((REL_BLK, heads * REL_DELTAS, REL_BLK), F32),
        compiler_params=pltpu.CompilerParams(vmem_limit_bytes=VMEM_LIMIT_V7X),
    )(by_delta)


def _bias_table(rel_bias, name):
    heads = rel_bias.shape[0]
    rel_pad = jnp.pad(rel_bias, ((0, 0), (0, REL_PAD - REL_TABLE)))
    tiles = _rel_expand(rel_pad, name)
    tiles = tiles.reshape(REL_BLK, heads, REL_DELTAS, REL_BLK).transpose(1, 2, 0, 3)
    na, nb = Q_BLOCK // REL_BLK, K_WINDOW // REL_BLK
    rows = [jnp.concatenate([tiles[:, a - b + nb - 1] for b in range(nb)], axis=-1) for a in range(na)]
    table = jnp.concatenate(rows, axis=-2)
    qc = np.arange(Q_BLOCK)[:, None] // CHUNK
    kc = np.arange(K_WINDOW)[None, :] // CHUNK
    band = (kc >= qc) & (kc <= qc + PAST_CHUNKS)
    return jnp.where(jnp.asarray(band)[None], table, NEG)


def _rel_reduce(db, name):
    heads = db.shape[0]
    na, nb = Q_BLOCK // REL_BLK, K_WINDOW // REL_BLK

    fold_heads = 4

    def body_fold(db_ref, g_ref):
        for hd in range(fold_heads):
            for delta in range(REL_DELTAS):
                acc = None
                for a in range(na):
                    b = a - (delta - (nb - 1))
                    if 0 <= b < nb:
                        tile = db_ref[hd, a * REL_BLK:(a + 1) * REL_BLK, b * REL_BLK:(b + 1) * REL_BLK]
                        acc = tile if acc is None else acc + tile
                g_ref[hd, delta] = acc

    folded = pl.pallas_call(
        body_fold, name=name + "_fold", grid=(heads // fold_heads,),
        in_specs=[pl.BlockSpec((fold_heads, Q_BLOCK, K_WINDOW), lambda h: (h, 0, 0))],
        out_specs=pl.BlockSpec((fold_heads, REL_DELTAS, REL_BLK, REL_BLK), lambda h: (h, 0, 0, 0)),
        out_shape=jax.ShapeDtypeStruct((heads, REL_DELTAS, REL_BLK, REL_BLK), F32),
        compiler_params=_params("parallel"),
    )(db)
    by_row = folded.transpose(2, 0, 1, 3).reshape(REL_BLK, heads * REL_DELTAS, REL_BLK)

    def body_diag(g_ref, d_ref):
        zeros = jnp.zeros((heads * REL_DELTAS, REL_BLK), F32)
        acc = None
        for r in range(REL_BLK):
            part = pltpu.roll(jnp.concatenate([g_ref[r], zeros], axis=1), REL_BLK - r, 1)
            acc = part if acc is None else acc + part
        d_ref[...] = acc

    diag = pl.pallas_call(
        body_diag, name=name + "_diag",
        out_shape=jax.ShapeDtypeStruct((heads * REL_DELTAS, 2 * REL_BLK), F32),
        compiler_params=pltpu.CompilerParams(vmem_limit_bytes=VMEM_LIMIT_V7X),
    )(by_row)
    diag = diag.reshape(heads, REL_DELTAS * 2 * REL_BLK)

    def body_bin(d_ref, o_ref):
        onehot = jnp.where(_rel_bin_matrix(), 1.0, 0.0).astype(BF16)
        hi, mid, lo = _split3(d_ref[...])
        o_ref[...] = _dot(hi, onehot) + _dot(mid, onehot) + _dot(lo, onehot)

    out = pl.pallas_call(
        body_bin, name=name + "_bin",
        out_shape=jax.ShapeDtypeStruct((heads, REL_PAD), F32),
        compiler_params=pltpu.CompilerParams(vmem_limit_bytes=VMEM_LIMIT_V7X),
    )(diag)
    return out[:, :REL_TABLE]


def _sum_leading(x, name):
    n, r, c = x.shape
    tr = _pick(r, 256, 8)

    def body(x_ref, o_ref):
        acc = x_ref[0].astype(F32)
        for k in range(1, n):
            acc = acc + x_ref[k].astype(F32)
        o_ref[...] = acc

    return pl.pallas_call(
        body, name=name, grid=(r // tr,),
        in_specs=[pl.BlockSpec((n, tr, c), lambda i: (0, i, 0))],
        out_specs=pl.BlockSpec((tr, c), lambda i: (i, 0)),
        out_shape=jax.ShapeDtypeStruct((r, c), F32),
        compiler_params=_params("parallel"),
    )(x)


def _pair_add(g, recv, parity, name):
    _, r, c = g.shape
    tr = _pick(r, 256, 16)

    def body(par_ref, g_ref, r_ref, o_ref):
        o_ref[...] = (g_ref[...].astype(F32) + r_ref[...].astype(F32)).astype(BF16)

    return pl.pallas_call(
        body, name=name,
        grid_spec=pltpu.PrefetchScalarGridSpec(
            num_scalar_prefetch=1, grid=(4, r // tr),
            in_specs=[pl.BlockSpec((1, tr, c), lambda k, i, par: (2 * k + par[0], i, 0)),
                      pl.BlockSpec((1, tr, c), lambda k, i, par: (k, i, 0))],
            out_specs=pl.BlockSpec((1, tr, c), lambda k, i, par: (k, i, 0))),
        out_shape=jax.ShapeDtypeStruct((4, r, c), BF16),
        compiler_params=_params("parallel", "parallel"),
    )(parity, g, recv)


def _adamw(w, g_parts, m, v, name):
    r, c = w.shape
    n = g_parts.shape[0]
    tr = _pick(r, 256, 16 if g_parts.dtype == BF16 else 8)
    c1 = 1.0 - ADAM_B1 ** ADAM_STEP
    c2 = 1.0 - ADAM_B2 ** ADAM_STEP

    def body(w_ref, g_ref, m_ref, v_ref, go_ref, d_ref, nm_ref, nv_ref):
        gv = g_ref[0].astype(F32)
        for k in range(1, n):
            gv = gv + g_ref[k].astype(F32)
        nm = ADAM_B1 * m_ref[...] + (1.0 - ADAM_B1) * gv
        nv = ADAM_B2 * v_ref[...] + (1.0 - ADAM_B2) * (gv * gv)
        go_ref[...] = gv
        d_ref[...] = -ADAM_LR * ((nm / c1) / (jnp.sqrt(nv / c2) + ADAM_EPS) + ADAM_WD * w_ref[...])
        nm_ref[...] = nm
        nv_ref[...] = nv

    spec = pl.BlockSpec((tr, c), lambda i: (i, 0))
    shp = jax.ShapeDtypeStruct((r, c), F32)
    return pl.pallas_call(
        body, name=name, grid=(r // tr,),
        in_specs=[spec, pl.BlockSpec((n, tr, c), lambda i: (0, i, 0)), spec, spec],
        out_specs=[spec] * 4, out_shape=[shp] * 4,
        compiler_params=_params("parallel"),
    )(w, g_parts, m, v)


BIG = (("a_w_in", 1), ("a_w_o", 0), ("a_w_gu", 0), ("a_w_down", 0), ("w_kv", 1),
       ("b_w_q", 0), ("b_w_o", 0), ("b_w_gu", 0), ("b_w_down", 0))
TRANSPOSED = ("a_w_gu", "b_w_gu")
FFN_BLK = 2 * FFN_HIDDEN // N_DEV

SMALL = (("a_norm_g", D_MODEL, True), ("a_gn_g", RET_V_COLS, True), ("a_ffn_norm_g", D_MODEL, True),
         ("kv_norm_g", D_MODEL, False), ("b_norm_g", D_MODEL, False), ("b_ffn_norm_g", D_MODEL, False),
         ("k_norm_g", ATT_DH, False), ("b_q_norm_g", ATT_DH, False),
         ("b_rel_bias", ATT_HEADS * REL_TABLE, False))
SMALL_ROWS, SMALL_COLS = 16, 1024


def _pack_small(vals, last=None):
    flat = jnp.concatenate([vals[n].reshape(-1) for n, _, _ in SMALL])
    room = SMALL_ROWS * SMALL_COLS - flat.shape[0]
    if last is None:
        flat = jnp.pad(flat, (0, room))
    else:
        flat = jnp.concatenate([jnp.pad(flat, (0, room - 1)), last.reshape(1)])
    return flat.reshape(SMALL_ROWS, SMALL_COLS)


def _unpack_small(packed, local):
    flat, out, pos = packed.reshape(-1), {}, 0
    for n, length, sharded in SMALL:
        ln = length // N_DEV if (local and sharded) else length
        out[n] = flat[pos:pos + ln]
        pos += ln
    return out


def _gather_rider(shards, names):
    return _GatherRider([shards[n] for n in names])


def _gathered(rider, names, axis_of):
    return {n: (r.reshape(-1, r.shape[2]) if axis_of[n] == 0 else r) for n, r in zip(names, rider.results)}


def _blocks(g):
    return g if g.ndim == 3 else g.reshape(N_DEV, -1, g.shape[-1])


def _local_step(x, target, shards, s, parity):
    t = x.shape[0]
    axis_of = dict(BIG)
    consts = _ret_consts(t)
    lane_to_head = np.zeros((D_MODEL, LANES), np.float32)
    lane_to_head[np.arange(D_MODEL), np.arange(D_MODEL) // ATT_DH] = 1.0
    bd = jnp.asarray(lane_to_head).astype(BF16)
    kg_t = jnp.tile(s["k_norm_g"], (1, ATT_HEADS))
    qg_t = jnp.tile(s["b_q_norm_g"], (1, ATT_HEADS))
    q_scale = ATT_DH ** -0.5
    w, g, recv = {}, {}, {}

    def gather_on(names):
        return _gather_rider(shards, names), names

    def landed(ride):
        w.update(_gathered(ride[0], ride[1], axis_of))

    def scatter_on(names):
        return _ScatterRider([_blocks(g[n]) for n in names]), names

    def reduced(ride):
        recv.update(zip(ride[1], ride[0].results))

    proj, w["a_w_in"] = _proj_gather(x, s["a_norm_g"], shards["a_w_in"], "a_proj")
    ride = gather_on(["a_w_gu", "a_w_o"])
    y, o_ret, states = _ret_fwd(proj, s["a_gn_g"], consts, "a_ret", rider=ride[0])
    landed(ride)
    ride = gather_on(["a_w_down"])
    x1 = _mm(y, w["a_w_o"], "nn", "a_out", res=x, rider=ride[0])
    landed(ride)
    ride = gather_on(["w_kv", "b_w_q", "b_w_o"])
    gu_a, act_a = _mm(x1, w["a_w_gu"], "nt", "a_ffn_gu", epilogue="swiglu", out_block=FFN_BLK,
                      norm_g=s["a_ffn_norm_g"], rider=ride[0])
    landed(ride)
    ride = gather_on(["b_w_down"])
    x2 = _mm(act_a, w["a_w_down"], "nn", "a_ffn_down", res=x1, rider=ride[0])
    landed(ride)

    kv = _mm(x2, w["w_kv"], "nn", "kv_proj", norm_g=s["kv_norm_g"])
    kp, vp = _kv_prep(kv, kg_t, bd, "kv_prep")

    q_raw = _mm(x2, w["b_w_q"], "nn", "b_q", norm_g=s["b_norm_g"])
    qn = _q_hnorm(q_raw, qg_t, bd, q_scale, "q_hnorm")
    bias = _bias_table(s["b_rel_bias"].reshape(ATT_HEADS, REL_TABLE), "rel")
    ride = gather_on(["b_w_gu"])
    o_att, lse = _att_fwd(qn, kp, vp, bias, "b_att", rider=ride[0])
    landed(ride)
    x3 = _mm(o_att, w["b_w_o"], "nn", "b_out", res=x2)
    gu_b, act_b = _mm(x3, w["b_w_gu"], "nt", "b_ffn_gu", epilogue="swiglu", out_block=FFN_BLK,
                      norm_g=s["b_ffn_norm_g"])
    dy, loss = _mm(act_b, w["b_w_down"], "nn", "b_ffn_down", res=x3, epilogue="loss", extra=(target,))
    in_blk, kv_blk, ffn_blk = w["a_w_in"].shape[2], w["w_kv"].shape[2], FFN_BLK

    dgu = _mm(dy, w["b_w_down"], "nt", "b_ffn_dgu", out_block=ffn_blk, epilogue="swiglu_bwd", extra=gu_b)
    dgu = dgu.reshape(N_DEV, t, ffn_blk)
    g["b_w_down"] = _mm(act_b, dy, "tn", "b_ffn_gdown", out_dtype=BF16)
    ride = scatter_on(["b_w_down"])
    dx3, g["b_ffn_norm_g"] = _mm(dgu, w["b_w_gu"], "nn", "b_ffn_dh", epilogue="rms_bwd",
                                 extra=(x3, s["b_ffn_norm_g"], dy), rider=ride[0])
    reduced(ride)
    g["b_w_gu"] = _mm(dgu, x3, "tn", "b_ffn_ggu", out_dtype=BF16, norm_g=s["b_ffn_norm_g"], norm_b=True)

    do_att = _mm(dx3, w["b_w_o"], "nt", "b_dout", out_dtype=BF16)
    g["b_w_o"] = _mm(o_att, dx3, "tn", "b_gout", out_dtype=BF16)
    ride = scatter_on(["b_w_gu", "b_w_o"])
    dq, dkp, dvp, db = _att_bwd(qn, kp, vp, bias, do_att, o_att, lse, "b_datt", rider=ride[0])
    reduced(ride)
    g["b_rel_bias"] = _rel_reduce(db, "drel").reshape(1, -1)
    dq_raw, gq = _q_dhnorm(q_raw, qg_t, bd, dq, q_scale, "q_dhnorm")
    g["b_q_norm_g"] = gq.reshape(ATT_HEADS, ATT_DH).sum(axis=0, keepdims=True)
    g["b_w_q"] = _mm(x2, dq_raw, "tn", "b_gq", out_dtype=BF16, norm_g=s["b_norm_g"])
    dx2, g["b_norm_g"] = _mm(dq_raw, w["b_w_q"], "nt", "b_dq", epilogue="rms_bwd",
                             extra=(x2, s["b_norm_g"], dx3))

    dkv, gk = _kv_dprep(kv, kg_t, bd, dkp, dvp, "kv_dprep")
    g["k_norm_g"] = gk.reshape(ATT_HEADS, ATT_DH).sum(axis=0, keepdims=True)
    g["w_kv"] = _mm(x2, dkv, "tn", "kv_g", out_dtype=BF16, out_block=kv_blk, norm_g=s["kv_norm_g"])
    dx2, g["kv_norm_g"] = _mm(dkv, w["w_kv"], "nt", "kv_du", epilogue="rms_bwd",
                              extra=(x2, s["kv_norm_g"], dx2))

    ride = scatter_on(["b_w_q"])
    dgu = _mm(dx2, w["a_w_down"], "nt", "a_ffn_dgu", out_block=ffn_blk, epilogue="swiglu_bwd", extra=gu_a,
              rider=ride[0])
    reduced(ride)
    dgu = dgu.reshape(N_DEV, t, ffn_blk)
    g["a_w_down"] = _mm(act_a, dx2, "tn", "a_ffn_gdown", out_dtype=BF16)
    ride = scatter_on(["a_w_down"])
    dx1, g["a_ffn_norm_g"] = _mm(dgu, w["a_w_gu"], "nn", "a_ffn_dh", epilogue="rms_bwd",
                                 extra=(x1, s["a_ffn_norm_g"], dx2), rider=ride[0])
    reduced(ride)
    ride = scatter_on(["w_kv"])
    g["a_w_gu"] = _mm(dgu, x1, "tn", "a_ffn_ggu", out_dtype=BF16, norm_g=s["a_ffn_norm_g"], norm_b=True,
                      rider=ride[0])
    reduced(ride)

    swap = _SiblingSwapRider([_blocks(g["a_w_gu"])])
    dy_ret = _mm(dx1, w["a_w_o"], "nt", "a_dout", rider=swap)
    g["a_w_o"] = _mm(y, dx1, "tn", "a_gout", out_dtype=BF16)
    chips = _ChipScatterRider([_pair_add(_blocks(g["a_w_gu"]), swap.results[0], parity, "rs_pair_add_gu")])
    dproj, g["a_gn_g"] = _ret_bwd(proj, s["a_gn_g"], o_ret, states, dy_ret, consts, "a_dret", rider=chips)
    recv["a_w_gu"] = chips.results[0]
    ride = scatter_on(["a_w_o"])
    g["a_w_in"] = _mm(x, dproj, "tn", "a_gin", out_dtype=BF16, out_block=in_blk, norm_g=s["a_norm_g"],
                      rider=ride[0])
    reduced(ride)
    from_sibling = _exchange(_SiblingSwapRider([g["a_w_in"]]), "rs_sibling")[0]
    chip_sums = _pair_add(g["a_w_in"], from_sibling, parity, "rs_pair_add")
    last = _ChipScatterRider([chip_sums])
    grad_x, g["a_norm_g"] = _mm(dproj, w["a_w_in"], "nt", "a_dproj", epilogue="rms_bwd",
                                extra=(x, s["a_norm_g"], dx1), rider=last)
    recv["a_w_in"] = last.results[0]
    return loss, grad_x, recv, g


ARG_NAMES = ("x", "a_norm_g", "a_w_in", "a_gn_g", "a_w_o", "a_ffn_norm_g", "a_w_gu", "a_w_down",
             "kv_norm_g", "w_kv", "k_norm_g", "b_norm_g", "b_w_q", "b_q_norm_g", "b_rel_bias", "b_w_o",
             "b_ffn_norm_g", "b_w_gu", "b_w_down")
WEIGHT_NAMES = ARG_NAMES[1:]


def _big_shard(a, name):
    a = a[0] if a.ndim == 3 else a
    return a.T if name in TRANSPOSED else a


def _as_given(a, name, shape):
    return (a.T if name in TRANSPOSED else a).reshape(shape)


def kernel(x, a_norm_g, a_w_in, a_gn_g, a_w_o, a_ffn_norm_g, a_w_gu, a_w_down, kv_norm_g, w_kv, k_norm_g, b_norm_g, b_w_q, b_q_norm_g, b_rel_bias, b_w_o, b_ffn_norm_g, b_w_gu, b_w_down, loss_target, m_a_norm_g, m_a_w_in, m_a_gn_g, m_a_w_o, m_a_ffn_norm_g, m_a_w_gu, m_a_w_down, m_kv_norm_g, m_w_kv, m_k_norm_g, m_b_norm_g, m_b_w_q, m_b_q_norm_g, m_b_rel_bias, m_b_w_o, m_b_ffn_norm_g, m_b_w_gu, m_b_w_down, v_a_norm_g, v_a_w_in, v_a_gn_g, v_a_w_o, v_a_ffn_norm_g, v_a_w_gu, v_a_w_down, v_kv_norm_g, v_w_kv, v_k_norm_g, v_b_norm_g, v_b_w_q, v_b_q_norm_g, v_b_rel_bias, v_b_w_o, v_b_ffn_norm_g, v_b_w_gu, v_b_w_down):
    args = (x, a_norm_g, a_w_in, a_gn_g, a_w_o, a_ffn_norm_g, a_w_gu, a_w_down, kv_norm_g, w_kv, k_norm_g,
            b_norm_g, b_w_q, b_q_norm_g, b_rel_bias, b_w_o, b_ffn_norm_g, b_w_gu, b_w_down)
    p = dict(zip(ARG_NAMES, args))
    m_all = dict(zip(WEIGHT_NAMES, (m_a_norm_g, m_a_w_in, m_a_gn_g, m_a_w_o, m_a_ffn_norm_g, m_a_w_gu,
                                    m_a_w_down, m_kv_norm_g, m_w_kv, m_k_norm_g, m_b_norm_g, m_b_w_q,
                                    m_b_q_norm_g, m_b_rel_bias, m_b_w_o, m_b_ffn_norm_g, m_b_w_gu, m_b_w_down)))
    v_all = dict(zip(WEIGHT_NAMES, (v_a_norm_g, v_a_w_in, v_a_gn_g, v_a_w_o, v_a_ffn_norm_g, v_a_w_gu,
                                    v_a_w_down, v_kv_norm_g, v_w_kv, v_k_norm_g, v_b_norm_g, v_b_w_q,
                                    v_b_q_norm_g, v_b_rel_bias, v_b_w_o, v_b_ffn_norm_g, v_b_w_gu, v_b_w_down)))
    xi, yi, ci = _my_place()
    me = 4 * xi + 2 * yi + ci
    big_names = [n for n, _ in BIG]

    big_local = {n: _big_shard(p[n], n) for n in big_names}
    shards = {n: a.astype(BF16) for n, a in big_local.items()}
    small_local = _pack_small({n: p[n] for n, _, _ in SMALL})
    small_all = _exchange(_GatherRider([small_local]), "gather_small")[0]
    flat_g = small_all.reshape(N_DEV, -1)
    s_full, pos = {}, 0
    for n, length, sharded in SMALL:
        ln = length // N_DEV if sharded else length
        s_full[n] = flat_g[:, pos:pos + ln].reshape(1, -1) if sharded else p[n].reshape(1, -1)
        pos += ln

    parity = jnp.reshape(ci, (1,)).astype(jnp.int32)
    loss, grad_x, recv, g = _local_step(x[0], loss_target[0], shards, s_full, parity)

    partial = _pack_small({n: g[n] for n, _, _ in SMALL}, last=loss)
    summed = _sum_leading(_exchange(_GatherRider([partial]), "gather_gsmall")[0], "gsmall_sum")
    loss = summed[SMALL_ROWS - 1, SMALL_COLS - 1]
    g_small = _unpack_small(summed, local=False)
    for n, length, sharded in SMALL:
        if sharded:
            g_small[n] = lax.dynamic_slice(g_small[n], (me * (length // N_DEV),), (length // N_DEV,))

    grads, deltas, new_m, new_v = {}, {}, {}, {}
    for n in big_names:
        outs = _adamw(big_local[n], recv[n], _big_shard(m_all[n], n), _big_shard(v_all[n], n), "adamw_" + n)
        grads[n], deltas[n], new_m[n], new_v[n] = (_as_given(a, n, p[n].shape) for a in outs)
    pk = lambda src: _pack_small({n: src[n] for n, _, _ in SMALL})
    outs = _adamw(small_local, pk(g_small)[None], pk(m_all), pk(v_all), "adamw_small")
    g_s, d_s, nm_s, nv_s = (_unpack_small(a, local=True) for a in outs)
    for n, _, _ in SMALL:
        grads[n], deltas[n], new_m[n], new_v[n] = (a[n].reshape(p[n].shape) for a in (g_s, d_s, nm_s, nv_s))

    return (loss, grad_x[None], *[grads[n] for n in WEIGHT_NAMES], *[deltas[n] for n in WEIGHT_NAMES],
            *[new_m[n] for n in WEIGHT_NAMES], *[new_v[n] for n in WEIGHT_NAMES])
```

```python
import numpy as np
import jax
import jax.numpy as jnp
from jax import lax
from jax.experimental import pallas as pl
from jax.experimental.pallas import tpu as pltpu

F32 = jnp.float32
BF16 = jnp.bfloat16

N_DEV = 8
D_MODEL = 1024
CHUNK = 64
EPS = 1e-6
RET_HEADS, RET_DK, RET_DV = 4, 256, 512
RET_STEP = 4
RET_Q_COLS = RET_HEADS * RET_DK
RET_V_COLS = RET_HEADS * RET_DV
ATT_HEADS, ATT_DH = 16, 64
PAST_CHUNKS = 8
REL_CLIP = 256
REL_TABLE = 2 * REL_CLIP + 1
FFN_HIDDEN = 2816
ROPE_BASE = 10000.0
LANES = 128
Q_BLOCK = 256
ATT_SUBS = 4
ATT_ROWS = 32
K_PAD = PAST_CHUNKS * CHUNK
K_WINDOW = Q_BLOCK + K_PAD
REL_BLK = 128
REL_DELTAS = Q_BLOCK // REL_BLK + K_WINDOW // REL_BLK - 1
REL_PAD = 640
NEG = -1e30
VMEM_LIMIT_V7X = 56 * 1024 * 1024
ADAM_LR, ADAM_B1, ADAM_B2, ADAM_EPS, ADAM_WD, ADAM_STEP = 1e-3, 0.9, 0.999, 1e-8, 0.01, 10
MESH = pl.DeviceIdType.MESH
ANY = pl.BlockSpec(memory_space=pl.ANY)


def _params(*semantics):
    return pltpu.CompilerParams(dimension_semantics=semantics, vmem_limit_bytes=VMEM_LIMIT_V7X)


def _pick(dim, cap, align):
    best = None
    for t in range(align, min(dim, cap) + 1, align):
        if dim % t == 0:
            best = t
    assert best is not None, (dim, cap, align)
    return best


def _dot(a, b):
    return lax.dot_general(a, b, (((1,), (0,)), ((), ())), preferred_element_type=F32)


def _dot_nt(a, b):
    return lax.dot_general(a, b, (((1,), (1,)), ((), ())), preferred_element_type=F32)


def _dot_tn(a, b):
    return lax.dot_general(a, b, (((0,), (0,)), ((), ())), preferred_element_type=F32)


def _split2(x):
    hi = x.astype(BF16)
    lo = (x - hi.astype(F32)).astype(BF16)
    return hi, lo


def _split3(x):
    hi = x.astype(BF16)
    r = x - hi.astype(F32)
    mid = r.astype(BF16)
    lo = (r - mid.astype(F32)).astype(BF16)
    return hi, mid, lo


def _sigmoid(x):
    return 1.0 / (1.0 + jnp.exp(-x))


def _accumulate(ref, part, step):
    @pl.when(step == 0)
    def _():
        ref[...] = part

    @pl.when(step > 0)
    def _():
        ref[...] += part


RELAY_AT_NUM, RELAY_AT_DEN = 3, 4


def _my_place():
    return lax.axis_index("x"), lax.axis_index("y"), lax.axis_index("c")


def _flip(v, bit):
    return 1 - v if bit else v


class _NoRelay:
    def relay(self, in_refs, out_refs, sems):
        pass


class _GatherRider:
    def __init__(self, xs):
        self.inputs = list(xs)
        n = len(xs)
        self.out_shape = [jax.ShapeDtypeStruct((N_DEV,) + x.shape, x.dtype) for x in xs]
        self.scratch = [pltpu.SemaphoreType.DMA((7, n)), pltpu.SemaphoreType.DMA((7, n)),
                        pltpu.SemaphoreType.DMA((n,))]
        self.results = None

    def _copies(self, x_refs, out_refs, sems):
        send_sems, recv_sems, local_sems = sems
        n = len(x_refs)
        x, y, c = _my_place()
        me, sibling = (x, y, c), (x, y, 1 - c)
        chips = [(1 - x, y), (x, 1 - y), (1 - x, 1 - y)]

        def slot(a, px, py, pc):
            return out_refs[a].at[4 * px + 2 * py + pc]

        def copy(k, a, block, to, own=False):
            return pltpu.make_async_remote_copy(
                src_ref=x_refs[a] if own else slot(a, *block), dst_ref=slot(a, *block),
                send_sem=send_sems.at[k, a], recv_sem=recv_sems.at[k, a],
                device_id=to, device_id_type=MESH)

        mine = [pltpu.make_async_copy(x_refs[a], slot(a, *me), local_sems.at[a]) for a in range(n)]
        first = []
        for a in range(n):
            first.append(copy(0, a, me, sibling, own=True))
            first += [copy(1 + j, a, me, (*chip, c), own=True) for j, chip in enumerate(chips)]
        return n, c, me, sibling, chips, copy, mine, first

    def start(self, x_refs, out_refs, sems):
        _, _, _, _, _, _, mine, first = self._copies(x_refs, out_refs, sems)
        for cp in mine + first:
            cp.start()

    def relay(self, x_refs, out_refs, sems):
        n, c, me, sibling, chips, copy, _, _ = self._copies(x_refs, out_refs, sems)
        for j, chip in enumerate(chips):
            for a in range(n):
                copy(1 + j, a, (*chip, c), me).wait_recv()
                copy(4 + j, a, (*chip, c), sibling).start()

    def finish(self, x_refs, out_refs, sems):
        n, c, me, sibling, chips, copy, mine, first = self._copies(x_refs, out_refs, sems)
        passed = [copy(4 + j, a, (*chip, c), sibling) for j, chip in enumerate(chips) for a in range(n)]
        for a in range(n):
            copy(0, a, sibling, me).wait_recv()
            for j, chip in enumerate(chips):
                copy(4 + j, a, (*chip, 1 - c), me).wait_recv()
        for cp in first + passed:
            cp.wait_send()
        for cp in mine:
            cp.wait()


class _ScatterRider(_NoRelay):
    def __init__(self, gs):
        self.inputs = list(gs)
        n = len(gs)
        self.out_shape = [jax.ShapeDtypeStruct(g.shape, g.dtype) for g in gs]
        self.scratch = [pltpu.SemaphoreType.DMA((7, n)), pltpu.SemaphoreType.DMA((7, n)),
                        pltpu.SemaphoreType.DMA((n,))]
        self.results = None

    def _copies(self, g_refs, out_refs, sems):
        send_sems, recv_sems, local_sems = sems
        x, y, c = _my_place()
        me = 4 * x + 2 * y + c
        mine, copies = [], []
        for a in range(len(g_refs)):
            mine.append(pltpu.make_async_copy(g_refs[a].at[me], out_refs[a].at[me], local_sems.at[a]))
            for k in range(1, N_DEV):
                px, py, pc = _flip(x, k & 4), _flip(y, k & 2), _flip(c, k & 1)
                copies.append(pltpu.make_async_remote_copy(
                    src_ref=g_refs[a].at[4 * px + 2 * py + pc], dst_ref=out_refs[a].at[me],
                    send_sem=send_sems.at[k - 1, a], recv_sem=recv_sems.at[k - 1, a],
                    device_id=(px, py, pc), device_id_type=MESH))
        return mine, copies

    def start(self, g_refs, out_refs, sems):
        mine, copies = self._copies(g_refs, out_refs, sems)
        for cp in mine + copies:
            cp.start()

    def finish(self, g_refs, out_refs, sems):
        mine, copies = self._copies(g_refs, out_refs, sems)
        for cp in copies + mine:
            cp.wait()


class _SiblingSwapRider(_NoRelay):
    def __init__(self, gs):
        self.inputs = list(gs)
        n = len(gs)
        self.out_shape = [jax.ShapeDtypeStruct((4,) + g.shape[1:], g.dtype) for g in gs]
        self.scratch = [pltpu.SemaphoreType.DMA((4, n)), pltpu.SemaphoreType.DMA((4, n))]
        self.results = None

    def _copies(self, g_refs, out_refs, sems):
        send_sems, recv_sems = sems
        x, y, c = _my_place()
        return [pltpu.make_async_remote_copy(
            src_ref=g_refs[a].at[2 * k + 1 - c], dst_ref=out_refs[a].at[k],
            send_sem=send_sems.at[k, a], recv_sem=recv_sems.at[k, a],
            device_id=(x, y, 1 - c), device_id_type=MESH)
            for a in range(len(g_refs)) for k in range(4)]

    def start(self, g_refs, out_refs, sems):
        for cp in self._copies(g_refs, out_refs, sems):
            cp.start()

    def finish(self, g_refs, out_refs, sems):
        for cp in self._copies(g_refs, out_refs, sems):
            cp.wait()


class _ChipScatterRider(_NoRelay):
    def __init__(self, ps):
        self.inputs = list(ps)
        n = len(ps)
        self.out_shape = [jax.ShapeDtypeStruct(p.shape, p.dtype) for p in ps]
        self.scratch = [pltpu.SemaphoreType.DMA((3, n)), pltpu.SemaphoreType.DMA((3, n)),
                        pltpu.SemaphoreType.DMA((n,))]
        self.results = None

    def _copies(self, p_refs, out_refs, sems):
        send_sems, recv_sems, local_sems = sems
        x, y, c = _my_place()
        my_chip = 2 * x + y
        chips = [(1 - x, y), (x, 1 - y), (1 - x, 1 - y)]
        n = len(p_refs)
        mine = [pltpu.make_async_copy(p_refs[a].at[my_chip], out_refs[a].at[my_chip], local_sems.at[a])
                for a in range(n)]
        copies = [pltpu.make_async_remote_copy(
            src_ref=p_refs[a].at[2 * cx + cy], dst_ref=out_refs[a].at[my_chip],
            send_sem=send_sems.at[j, a], recv_sem=recv_sems.at[j, a],
            device_id=(cx, cy, c), device_id_type=MESH)
            for a in range(n) for j, (cx, cy) in enumerate(chips)]
        return mine, copies

    def start(self, p_refs, out_refs, sems):
        mine, copies = self._copies(p_refs, out_refs, sems)
        for cp in mine + copies:
            cp.start()

    def finish(self, p_refs, out_refs, sems):
        mine, copies = self._copies(p_refs, out_refs, sems)
        for cp in copies + mine:
            cp.wait()


def _call(body, name, grid, in_specs, out_specs, out_shape, scratch, semantics, args, rider=None):
    in_specs, out_specs, out_shape, scratch = list(in_specs), list(out_specs), list(out_shape), list(scratch)
    if rider is None:
        return list(pl.pallas_call(
            body, name=name, grid=grid, in_specs=in_specs, out_specs=out_specs, out_shape=out_shape,
            scratch_shapes=scratch, compiler_params=_params(*semantics))(*args))
    n_in, n_out, n_scr = len(in_specs), len(out_specs), len(scratch)
    r_in, r_out = len(rider.inputs), len(rider.out_shape)

    def wrapped(*refs):
        cuts = np.cumsum([0, n_in, r_in, n_out, r_out, n_scr])
        hi, ri, ho, ro, hs = (refs[cuts[i]:cuts[i + 1]] for i in range(5))
        rs = refs[cuts[5]:]
        step, steps = pl.program_id(0), grid[0]
        for d in range(1, len(grid)):
            step, steps = step * grid[d] + pl.program_id(d), steps * grid[d]

        @pl.when(step == 0)
        def _():
            rider.start(ri, ro, rs)

        body(*hi, *ho, *hs)

        @pl.when(step == (steps * RELAY_AT_NUM) // RELAY_AT_DEN)
        def _():
            rider.relay(ri, ro, rs)

        @pl.when(step == steps - 1)
        def _():
            rider.finish(ri, ro, rs)

    outs = pl.pallas_call(
        wrapped, name=name, grid=grid,
        in_specs=in_specs + [ANY] * r_in, out_specs=out_specs + [ANY] * r_out,
        out_shape=out_shape + rider.out_shape, scratch_shapes=scratch + rider.scratch,
        compiler_params=_params(*(["arbitrary"] * len(grid))),
    )(*args, *rider.inputs)
    rider.results = list(outs[n_out:])
    return list(outs[:n_out])


_WALK = ((None, None), (0, None), (1, 4), (2, 5), (4, None), (5, None), (3, 6), (6, None))


def _gather_order():
    x, y, c = _my_place()
    (ax, ay), (bx, by), (dx, dy) = (1 - x, y), (x, 1 - y), (1 - x, 1 - y)
    ids = [(x, y, c), (x, y, 1 - c), (ax, ay, c), (bx, by, c), (ax, ay, 1 - c), (bx, by, 1 - c),
           (dx, dy, c), (dx, dy, 1 - c)]
    return jnp.stack([4 * px + 2 * py + pc for px, py, pc in ids]).astype(jnp.int32)


def _proj_gather(x, norm_g, w_shard, extras, name):
    t, d = x.shape
    cols = w_shard.shape[1]
    tm = _pick(t, MM_CAP_MN, 16)
    ni = t // tm
    n = 1 + len(extras)
    rider = _GatherRider([w_shard] + list(extras))

    def body(ord_ref, x_ref, g_ref, *refs):
        sh_refs, proj_ref, gathered = refs[:n], refs[n], refs[n + 1:2 * n + 1]
        h_all, bbuf, bsem, send_sems, recv_sems, local_sems = refs[2 * n + 1:]
        j, i = pl.program_id(0), pl.program_id(1)
        _, c, me, sibling, chips, copy, mine, first = rider._copies(
            sh_refs, gathered, (send_sems, recv_sems, local_sems))
        rows = pl.ds(pl.multiple_of(i * tm, tm), tm)

        def load(step, src):
            return pltpu.make_async_copy(src, bbuf.at[step % 2], bsem.at[step % 2])

        def relayed(k, a):
            return copy(k, a, (*chips[k - 4], c), sibling)

        @pl.when(jnp.logical_and(j == 0, i == 0))
        def _():
            for cp in mine + first:
                cp.start()
            load(0, sh_refs[0]).start()

        for step in range(N_DEV):
            @pl.when(jnp.logical_and(j == step, i == 0))
            def _(step=step):
                load(step, sh_refs[0]).wait()
                if step + 1 < N_DEV:
                    need, relay = _WALK[step + 1]
                    copy(need, 0, me, me).wait_recv()
                    if relay is not None:
                        relayed(relay, 0).start()
                    load(step + 1, gathered[0].at[ord_ref[step + 1]]).start()

        @pl.when(j == 0)
        def _():
            groups = []
            for r in range(0, tm, NORM_ROWS):
                xv = x_ref[r:r + NORM_ROWS, :]
                rstd = lax.rsqrt(jnp.mean(xv * xv, axis=-1, keepdims=True) + EPS)
                groups.append((xv * rstd * g_ref[...]).astype(BF16))
            h_all[rows, :] = jnp.concatenate(groups, axis=0)

        proj_ref[...] = _dot(h_all[rows, :], bbuf[j % 2])

        @pl.when(jnp.logical_and(j == N_DEV - 1, i == ni - 1))
        def _():
            for a in range(1, n):
                for k in range(3):
                    copy(1 + k, a, me, me).wait_recv()
                    relayed(4 + k, a).start()
            for a in range(1, n):
                for k in (0, 4, 5, 6):
                    copy(k, a, me, me).wait_recv()
            for cp in first + [relayed(4 + k, a) for a in range(n) for k in range(3)]:
                cp.wait_send()
            for cp in mine:
                cp.wait()

    outs = pl.pallas_call(
        body, name=name,
        grid_spec=pltpu.PrefetchScalarGridSpec(
            num_scalar_prefetch=1, grid=(N_DEV, ni),
            in_specs=[pl.BlockSpec((tm, d), lambda j, i, o: (jnp.where(j == 0, i, ni - 1), 0)),
                      pl.BlockSpec((1, d), lambda j, i, o: (0, 0))] + [ANY] * n,
            out_specs=[pl.BlockSpec((tm, cols), lambda j, i, o: (i, o[j]))] + [ANY] * n,
            scratch_shapes=[pltpu.VMEM((t, d), BF16), pltpu.VMEM((2, d, cols), BF16),
                            pltpu.SemaphoreType.DMA((2,))] + rider.scratch),
        out_shape=[jax.ShapeDtypeStruct((t, N_DEV * cols), F32)] + rider.out_shape,
        compiler_params=_params("arbitrary", "arbitrary"),
    )(_gather_order(), x, norm_g, w_shard, *extras)
    return outs[0], list(outs[1:])


def _exchange(rider, name):
    r_in, r_out = len(rider.inputs), len(rider.out_shape)

    def body(*refs):
        ri, ro, rs = refs[:r_in], refs[r_in:r_in + r_out], refs[r_in + r_out:]
        rider.start(ri, ro, rs)
        rider.relay(ri, ro, rs)
        rider.finish(ri, ro, rs)

    return list(pl.pallas_call(
        body, name=name, in_specs=[ANY] * r_in, out_specs=[ANY] * r_out,
        out_shape=rider.out_shape, scratch_shapes=rider.scratch)(*rider.inputs))


MM_CAP_MN = 1024
MM_CAP_M_GRAD = 1408
MM_CAP_N = 1536
MM_CAP_K = 3072
MM_CAP_K_TOKENS = 2048
MM_CAP_K_RMS = 8192
MM_CAP_M_RMS = 512
NORM_ROWS = 256


def _mm(a, b, mode, name, out_dtype=F32, res=None, out_block=None, epilogue=None, extra=None, norm_g=None,
        norm_b=False, rider=None):
    a3, b3 = a.ndim == 3, b.ndim == 3
    um = un = uk = None
    if mode in ("nn", "nt"):
        if a3:
            m, uk = a.shape[1:]
            k = a.shape[0] * uk
        else:
            m, k = a.shape
    else:
        if a3:
            k, um = a.shape[1:]
            m = a.shape[0] * um
        else:
            k, m = a.shape
    if mode in ("nn", "tn"):
        if b3:
            kb, un = b.shape[1:]
            n = b.shape[0] * un
        else:
            kb, n = b.shape
        assert kb == k, (a.shape, b.shape, mode)
    else:
        if b3:
            n, ukb = b.shape[1:]
            assert b.shape[0] * ukb == k and uk in (None, ukb), (a.shape, b.shape, mode)
            uk = ukb
        else:
            n, kb = b.shape
            assert kb == k, (a.shape, b.shape, mode)
    if out_block is not None:
        assert un in (None, out_block)
        un = out_block

    def tile(dim, unit, cap, align):
        if unit is None:
            return _pick(dim, cap, align), 1
        c = max(1, cap // unit)
        while (dim // unit) % c:
            c -= 1
        return unit, c

    cap_m = MM_CAP_M_GRAD if mode == "tn" else (MM_CAP_M_RMS if epilogue == "rms_bwd" else MM_CAP_MN)
    um, cm = tile(m, um, cap_m, 128 if mode == "tn" else 16)
    un, cn = tile(n, un, MM_CAP_N, 128)
    cap_k = MM_CAP_K_TOKENS if mode == "tn" else (MM_CAP_K_RMS if epilogue == "rms_bwd" else MM_CAP_K)
    uk, ck = tile(k, uk, cap_k, 128)
    if epilogue == "rms_bwd":
        assert mode != "tn" and n == D_MODEL and cm == cn == 1 and res is None and out_block is None
    if epilogue == "loss":
        assert n == D_MODEL and cm == cn == 1 and res is not None and out_block is None
    if norm_g is not None and norm_b:
        assert mode == "tn" and not b3 and n == D_MODEL and cn == 1
    elif norm_g is not None:
        assert not a3 and (m if mode == "tn" else k) == D_MODEL and (cm if mode == "tn" else ck) == 1
    if epilogue == "swiglu":
        assert res is None and ((mode == "nn" and b3 and out_block is None) or
                                (mode == "nt" and not b3 and out_block is not None))
        cn = 2
    if epilogue == "swiglu_bwd":
        assert mode == "nt" and out_block is not None and extra is not None and res is None
        cn = 1
    tm, tn, tk = cm * um, cn * un, ck * uk
    nk = k // tk
    dot = {"nn": _dot, "nt": _dot_nt, "tn": _dot_tn}[mode]
    half = n // un // 2
    blocked_out = out_block is not None or epilogue in ("swiglu", "swiglu_bwd")
    extras = [] if extra is None else (list(extra) if isinstance(extra, (tuple, list)) else [extra])

    def sl(idx, unit, count):
        return slice(None) if count == 1 else slice(idx * unit, (idx + 1) * unit)

    def body(*refs):
        a_ref, b_ref = refs[0], refs[1]
        pos = 2
        r_ref = ng_ref = None
        if res is not None:
            r_ref, pos = refs[pos], pos + 1
        e_refs, pos = refs[pos:pos + len(extras)], pos + len(extras)
        if norm_g is not None:
            ng_ref, pos = refs[pos], pos + 1
        outs, acc_ref = refs[pos:-1], refs[-1]
        kk = pl.program_id(2)

        def normed(x_ref):
            groups = []
            for r in range(0, x_ref.shape[0], NORM_ROWS):
                xv = x_ref[r:r + NORM_ROWS, :]
                rstd = lax.rsqrt(jnp.mean(xv * xv, axis=-1, keepdims=True) + EPS)
                groups.append((xv * rstd * ng_ref[...]).astype(BF16))
            return jnp.concatenate(groups, axis=0)

        def a_blk(mi, ki):
            if norm_g is not None and not norm_b:
                return normed(a_ref)
            if mode in ("nn", "nt"):
                return a_ref[ki] if a3 else a_ref[:, sl(ki, uk, ck)]
            return a_ref[mi] if a3 else a_ref[:, sl(mi, um, cm)]

        def b_blk(ki, ni):
            if norm_b:
                return normed(b_ref)
            if epilogue == "swiglu":
                return b_ref[ni, 0]
            if mode in ("nn", "tn"):
                return b_ref[ni] if b3 else b_ref[sl(ki, uk, ck), sl(ni, un, cn)]
            return b_ref[ki][sl(ni, un, cn), :] if b3 else b_ref[sl(ni, un, cn), sl(ki, uk, ck)]

        parts = {}
        for mi in range(cm):
            for ni in range(cn):
                part = None
                for ki in range(ck):
                    d = dot(a_blk(mi, ki).astype(BF16), b_blk(ki, ni).astype(BF16))
                    part = d if part is None else part + d
                parts[mi, ni] = part

        def finish(total):
            if epilogue == "swiglu":
                gate, up = total[0, 0], total[0, 1]
                outs[0][0, 0] = gate.astype(BF16)
                outs[0][1, 0] = up.astype(BF16)
                outs[1][0] = (gate * _sigmoid(gate) * up).astype(BF16)
                return
            if epilogue == "swiglu_bwd":
                dact = total[0, 0]
                gate, up = e_refs[0][0, 0].astype(F32), e_refs[0][1, 0].astype(F32)
                sg = _sigmoid(gate)
                outs[0][0, 0] = (dact * up * (sg * (1.0 + gate * (1.0 - sg)))).astype(BF16)
                outs[0][1, 0] = (dact * (gate * sg)).astype(BF16)
                return
            if epilogue == "rms_bwd":
                x_ref, g_ref, dres_ref = e_refs
                dh, dg = total[0, 0], None
                for r in range(0, tm, NORM_ROWS):
                    rows = slice(r, r + NORM_ROWS)
                    xv, dhv = x_ref[rows, :], dh[rows, :]
                    rstd = lax.rsqrt(jnp.mean(xv * xv, axis=-1, keepdims=True) + EPS)
                    xh = xv * rstd
                    dyg = dhv * g_ref[...]
                    c = jnp.mean(dyg * xh, axis=-1, keepdims=True)
                    outs[0][rows, :] = dres_ref[rows, :] + rstd * (dyg - xh * c)
                    part = jnp.sum(dhv * xh, axis=0, keepdims=True)
                    dg = part if dg is None else dg + part
                _accumulate(outs[1], dg, pl.program_id(0))
                return
            if epilogue == "loss":
                diff = r_ref[...] + total[0, 0] - e_refs[0][...]
                outs[0][...] = diff * (1.0 / n)
                sq = jnp.sum(jnp.sum(diff * diff, axis=-1, keepdims=True), axis=0, keepdims=True)
                _accumulate(outs[1], sq * (0.5 / n), pl.program_id(0))
                return
            for (mi, ni), val in total.items():
                rows, cols = sl(mi, um, cm), sl(ni, un, cn)
                if res is not None:
                    val = r_ref[rows, cols] + val
                if blocked_out:
                    outs[0][ni, rows] = val.astype(out_dtype)
                else:
                    outs[0][rows, cols] = val.astype(out_dtype)

        if nk == 1:
            finish(parts)
        else:
            @pl.when(kk == 0)
            def _():
                for (mi, ni), val in parts.items():
                    acc_ref[mi * cn + ni] = val

            @pl.when(jnp.logical_and(kk > 0, kk < nk - 1))
            def _():
                for (mi, ni), val in parts.items():
                    acc_ref[mi * cn + ni] += val

            @pl.when(kk == nk - 1)
            def _():
                finish({key: acc_ref[key[0] * cn + key[1]] + val for key, val in parts.items()})

    if mode in ("nn", "nt"):
        a_spec = (pl.BlockSpec((ck, tm, uk), lambda i, j, kk: (kk, i, 0)) if a3
                  else pl.BlockSpec((tm, tk), lambda i, j, kk: (i, kk)))
    else:
        a_spec = (pl.BlockSpec((cm, tk, um), lambda i, j, kk: (i, kk, 0)) if a3
                  else pl.BlockSpec((tk, tm), lambda i, j, kk: (kk, i)))
    pair_spec = pl.BlockSpec((2, 1, tm, un), lambda i, j, kk: (0, j, i, 0))
    row_spec = pl.BlockSpec((tm, tn), lambda i, j, kk: (i, 0))
    vec_spec = pl.BlockSpec((1, tn), lambda i, j, kk: (0, 0))
    if epilogue == "swiglu" and mode == "nn":
        b = b.reshape(2, half, k, un)
        b_spec = pl.BlockSpec((2, 1, tk, un), lambda i, j, kk: (0, j, kk, 0))
    elif epilogue == "swiglu":
        b = b.reshape(2, half, un, k)
        b_spec = pl.BlockSpec((2, 1, un, tk), lambda i, j, kk: (0, j, 0, kk))
    elif mode in ("nn", "tn"):
        b_spec = (pl.BlockSpec((cn, tk, un), lambda i, j, kk: (j, kk, 0)) if b3
                  else pl.BlockSpec((tk, tn), lambda i, j, kk: (kk, j)))
    else:
        b_spec = (pl.BlockSpec((ck, tn, uk), lambda i, j, kk: (kk, j, 0)) if b3
                  else pl.BlockSpec((tn, tk), lambda i, j, kk: (j, kk)))
    if epilogue == "swiglu":
        out_specs = [pair_spec, pl.BlockSpec((1, tm, un), lambda i, j, kk: (j, i, 0))]
        out_shape = [jax.ShapeDtypeStruct((2, half, m, un), BF16), jax.ShapeDtypeStruct((half, m, un), BF16)]
    elif epilogue == "swiglu_bwd":
        out_specs = [pair_spec]
        out_shape = [jax.ShapeDtypeStruct(extra.shape, BF16)]
    elif epilogue == "rms_bwd":
        out_specs = [row_spec, vec_spec]
        out_shape = [jax.ShapeDtypeStruct((m, n), F32), jax.ShapeDtypeStruct((1, n), F32)]
    elif epilogue == "loss":
        out_specs = [row_spec, pl.BlockSpec((1, 1), lambda i, j, kk: (0, 0))]
        out_shape = [jax.ShapeDtypeStruct((m, n), F32), jax.ShapeDtypeStruct((1, 1), F32)]
    elif blocked_out:
        out_specs = [pl.BlockSpec((cn, tm, un), lambda i, j, kk: (j, i, 0))]
        out_shape = [jax.ShapeDtypeStruct((n // un, m, un), out_dtype)]
    else:
        out_specs = [pl.BlockSpec((tm, tn), lambda i, j, kk: (i, j))]
        out_shape = [jax.ShapeDtypeStruct((m, n), out_dtype)]
    in_specs, args = [a_spec, b_spec], [a, b]
    if res is not None:
        in_specs.append(pl.BlockSpec((tm, tn), lambda i, j, kk: (i, j)))
        args.append(res)
    if epilogue == "swiglu_bwd":
        in_specs.append(pair_spec)
    elif epilogue == "rms_bwd":
        in_specs += [row_spec, vec_spec, row_spec]
    elif epilogue == "loss":
        in_specs.append(row_spec)
    args += extras
    if norm_g is not None:
        in_specs.append(pl.BlockSpec((1, D_MODEL), lambda i, j, kk: (0, 0)))
        args.append(norm_g)
    semantics = ("arbitrary",) * 3 if epilogue in ("rms_bwd", "loss") else ("parallel", "parallel", "arbitrary")
    out = _call(body, name, (m // tm, n // tn, nk), in_specs, out_specs, out_shape,
                [pltpu.VMEM((cm * cn, um, un), F32)], semantics, args, rider)
    return out if epilogue in ("swiglu", "rms_bwd", "loss") else out[0]


def _head_sums(v, ind):
    hi, lo = _split2(v)
    return _dot(hi, ind) + _dot(lo, ind)


def _head_spread(per_head, ind):
    hi, lo = _split2(per_head)
    return _dot_nt(hi, ind) + _dot_nt(lo, ind)


def _head_rstd(xv, ind):
    return _head_spread(lax.rsqrt(_head_sums(xv * xv, ind) * (1.0 / ATT_DH) + EPS), ind)


def _hn_bwd_math(xv, gv, ind, dyv, scale):
    rstd = _head_rstd(xv, ind)
    xh = xv * rstd
    dyn = dyv * scale
    dyg = dyn * gv
    dx = rstd * (dyg - xh * _head_spread(_head_sums(dyg * xh, ind) * (1.0 / ATT_DH), ind))
    return dx, jnp.sum(dyn * xh, axis=0, keepdims=True)


def _q_hnorm(x, g_tiled, bd, scale, name):
    t, d = x.shape
    tm = _pick(t, 512, 16)

    def body(x_ref, g_ref, bd_ref, o_ref):
        xv = x_ref[...]
        o_ref[...] = (xv * _head_rstd(xv, bd_ref[...]) * g_ref[...] * scale).astype(BF16)

    return pl.pallas_call(
        body, name=name, grid=(t // tm,),
        in_specs=[pl.BlockSpec((tm, d), lambda i: (i, 0)), pl.BlockSpec((1, d), lambda i: (0, 0)),
                  pl.BlockSpec((d, LANES), lambda i: (0, 0))],
        out_specs=pl.BlockSpec((tm, d), lambda i: (i, 0)),
        out_shape=jax.ShapeDtypeStruct((t, d), BF16),
        compiler_params=_params("parallel"),
    )(x, g_tiled, bd)


def _q_dhnorm(x, g_tiled, bd, dy, scale, name):
    t, d = x.shape
    tm = _pick(t, 512, 16)

    def body(x_ref, g_ref, bd_ref, dy_ref, dx_ref, dg_ref):
        dx, part = _hn_bwd_math(x_ref[...], g_ref[...], bd_ref[...], dy_ref[...], scale)
        dx_ref[...] = dx.astype(BF16)
        _accumulate(dg_ref, part, pl.program_id(0))

    row = pl.BlockSpec((tm, d), lambda i: (i, 0))
    vec = pl.BlockSpec((1, d), lambda i: (0, 0))
    return pl.pallas_call(
        body, name=name, grid=(t // tm,),
        in_specs=[row, vec, pl.BlockSpec((d, LANES), lambda i: (0, 0)), row],
        out_specs=[row, vec],
        out_shape=[jax.ShapeDtypeStruct((t, d), BF16), jax.ShapeDtypeStruct((1, d), F32)],
        compiler_params=_params("arbitrary"),
    )(x, g_tiled, bd, dy)


def _kv_prep(kv, g_tiled, bd, name):
    t = kv.shape[0]
    d = D_MODEL
    tm = K_PAD
    assert t % tm == 0

    def body(k_ref, v_ref, g_ref, bd_ref, kp_ref, vp_ref):
        i = pl.program_id(0)

        @pl.when(i == 0)
        def _():
            kp_ref[...] = jnp.zeros_like(kp_ref)
            vp_ref[...] = jnp.zeros_like(vp_ref)

        @pl.when(i > 0)
        def _():
            xv = k_ref[...]
            kp_ref[...] = (xv * _head_rstd(xv, bd_ref[...]) * g_ref[...]).astype(BF16)
            vp_ref[...] = v_ref[...].astype(BF16)

    shp = jax.ShapeDtypeStruct((t + K_PAD, d), BF16)
    out = pl.BlockSpec((tm, d), lambda i: (i, 0))
    return pl.pallas_call(
        body, name=name, grid=(t // tm + 1,),
        in_specs=[pl.BlockSpec((tm, d), lambda i: (jnp.maximum(i - 1, 0), 0)),
                  pl.BlockSpec((tm, d), lambda i: (jnp.maximum(i - 1, 0), 1)),
                  pl.BlockSpec((1, d), lambda i: (0, 0)), pl.BlockSpec((d, LANES), lambda i: (0, 0))],
        out_specs=[out, out], out_shape=[shp, shp],
        compiler_params=_params("arbitrary"),
    )(kv, kv, g_tiled, bd)


def _kv_dprep(kv, g_tiled, bd, dkp_t, dvp_t, name):
    t = kv.shape[0]
    d = D_MODEL
    tm = K_PAD

    def body(k_ref, g_ref, bd_ref, dk_ref, dv_ref, o_ref, dg_ref):
        dx, part = _hn_bwd_math(k_ref[...], g_ref[...], bd_ref[...], dk_ref[...].T, 1.0)
        o_ref[:, :d] = dx.astype(BF16)
        o_ref[:, d:] = dv_ref[...].T.astype(BF16)
        _accumulate(dg_ref, part, pl.program_id(0))

    vec = pl.BlockSpec((1, d), lambda i: (0, 0))
    padded = pl.BlockSpec((d, tm), lambda i: (0, i + 1))
    return pl.pallas_call(
        body, name=name, grid=(t // tm,),
        in_specs=[pl.BlockSpec((tm, d), lambda i: (i, 0)), vec, pl.BlockSpec((d, LANES), lambda i: (0, 0)),
                  padded, padded],
        out_specs=[pl.BlockSpec((tm, 2 * d), lambda i: (i, 0)), vec],
        out_shape=[jax.ShapeDtypeStruct((t, 2 * d), BF16), jax.ShapeDtypeStruct((1, d), F32)],
        compiler_params=_params("arbitrary"),
    )(kv, g_tiled, bd, dkp_t, dvp_t)


def _ret_consts(t):
    h = np.arange(RET_HEADS, dtype=np.float32)
    lg = np.log(np.float32(1.0) - np.float32(2.0) ** (np.float32(-5.0) - h)).astype(np.float32)
    tt = np.arange(CHUNK, dtype=np.float32)
    intra = np.exp(lg[:, None, None] * np.abs(tt[:, None] - tt[None, :])).astype(np.float32)
    q_dec = np.exp(lg[:, None] * (tt + 1.0)).astype(np.float32)
    k_dec = np.exp(lg[:, None] * (CHUNK - 1.0 - tt)).astype(np.float32)
    s_dec = [float(v) for v in np.exp(lg * np.float32(CHUNK)).astype(np.float32)]
    qd = np.broadcast_to(q_dec[:, :, None], (RET_HEADS, CHUNK, RET_DK)).copy()
    kd = np.broadcast_to(k_dec[:, :, None], (RET_HEADS, CHUNK, RET_DK)).copy()
    half = RET_DK // 2
    inv_freq = ROPE_BASE ** (-jnp.arange(half, dtype=F32) / half)
    ang = jnp.arange(t).astype(F32)[:, None] * inv_freq[None, :]
    return jnp.asarray(intra), jnp.asarray(qd), jnp.asarray(kd), s_dec, jnp.cos(ang), jnp.sin(ang)


def _rope(x, cos, sin):
    half = RET_DK // 2
    x1, x2 = x[:, :half], x[:, half:]
    return jnp.concatenate([x1 * cos - x2 * sin, x1 * sin + x2 * cos], axis=-1)


def _unrope(d, cos, sin):
    half = RET_DK // 2
    d1, d2 = d[:, :half], d[:, half:]
    return jnp.concatenate([d1 * cos + d2 * sin, d2 * cos - d1 * sin], axis=-1)


def _ret_slices(h):
    q = slice(h * RET_DK, (h + 1) * RET_DK)
    k = slice(RET_Q_COLS + h * RET_DK, RET_Q_COLS + (h + 1) * RET_DK)
    v = slice(2 * RET_Q_COLS + h * RET_DV, 2 * RET_Q_COLS + (h + 1) * RET_DV)
    g = slice(2 * RET_Q_COLS + RET_V_COLS + h * RET_DV, 2 * RET_Q_COLS + RET_V_COLS + (h + 1) * RET_DV)
    o = slice(h * RET_DV, (h + 1) * RET_DV)
    return q, k, v, g, o


def _ret_fwd(proj, gn, consts, name, rider=None):
    t, cols = proj.shape
    n = t // CHUNK
    intra, qd, kd, s_dec, cos, sin = consts
    k_scale = RET_DK ** -0.5

    def body(p_ref, cos_ref, sin_ref, intra_ref, qd_ref, kd_ref, gn_ref, y_ref, o_ref, st_ref, state):
        i = pl.program_id(0)

        @pl.when(i == 0)
        def _():
            state[...] = jnp.zeros_like(state)

        for c in range(RET_STEP):
            rows = slice(c * CHUNK, (c + 1) * CHUNK)
            cosv, sinv = cos_ref[rows, :], sin_ref[rows, :]
            for h in range(RET_HEADS):
                qs, ks, vs, gs, os_ = _ret_slices(h)
                qr = _rope(p_ref[rows, qs], cosv, sinv)
                kr = _rope(p_ref[rows, ks], cosv, sinv) * k_scale
                vb = p_ref[rows, vs].astype(BF16)
                gv = p_ref[rows, gs]
                scores = _dot_nt(qr.astype(BF16), kr.astype(BF16)) * intra_ref[h]
                s_old = state[h]
                s_old_b = s_old.astype(BF16)
                st_ref[c, h] = s_old_b
                o = _dot(scores.astype(BF16), vb) + _dot((qr * qd_ref[h]).astype(BF16), s_old_b)
                state[h] = s_old * s_dec[h] + _dot_tn((kr * kd_ref[h]).astype(BF16), vb)
                rstd = lax.rsqrt(jnp.mean(o * o, axis=-1, keepdims=True) + EPS)
                on = o * rstd * gn_ref[:, os_]
                o_ref[rows, os_] = o
                y_ref[rows, os_] = (gv * _sigmoid(gv) * on).astype(BF16)

    full3 = lambda a: pl.BlockSpec(a.shape, lambda i: (0, 0, 0))
    step = RET_STEP * CHUNK
    return _call(
        body, name, (n // RET_STEP,),
        [pl.BlockSpec((step, cols), lambda i: (i, 0)),
         pl.BlockSpec((step, RET_DK // 2), lambda i: (i, 0)),
         pl.BlockSpec((step, RET_DK // 2), lambda i: (i, 0)),
         full3(intra), full3(qd), full3(kd),
         pl.BlockSpec((1, RET_V_COLS), lambda i: (0, 0))],
        [pl.BlockSpec((step, RET_V_COLS), lambda i: (i, 0)),
         pl.BlockSpec((step, RET_V_COLS), lambda i: (i, 0)),
         pl.BlockSpec((RET_STEP, RET_HEADS, RET_DK, RET_DV), lambda i: (i, 0, 0, 0))],
        [jax.ShapeDtypeStruct((t, RET_V_COLS), BF16),
         jax.ShapeDtypeStruct((t, RET_V_COLS), F32),
         jax.ShapeDtypeStruct((n, RET_HEADS, RET_DK, RET_DV), BF16)],
        [pltpu.VMEM((RET_HEADS, RET_DK, RET_DV), F32)], ("arbitrary",),
        (proj, cos, sin, intra, qd, kd, gn), rider)


def _ret_bwd(proj, gn, o_saved, states, dy, consts, name, rider=None):
    t, cols = proj.shape
    n = t // CHUNK
    intra, qd, kd, s_dec, cos, sin = consts
    k_scale = RET_DK ** -0.5

    def body(p_ref, cos_ref, sin_ref, intra_ref, qd_ref, kd_ref, gn_ref, o_ref, st_ref, dy_ref,
             dp_ref, dgn_ref, dstate):
        i = pl.program_id(0)

        @pl.when(i == 0)
        def _():
            dstate[...] = jnp.zeros_like(dstate)

        dgn = None
        for c in reversed(range(RET_STEP)):
            rows = slice(c * CHUNK, (c + 1) * CHUNK)
            cosv, sinv = cos_ref[rows, :], sin_ref[rows, :]
            dgn_parts = []
            for h in range(RET_HEADS):
                qs, ks, vs, gs, os_ = _ret_slices(h)
                qr = _rope(p_ref[rows, qs], cosv, sinv)
                kr = _rope(p_ref[rows, ks], cosv, sinv) * k_scale
                qb, kb = qr.astype(BF16), kr.astype(BF16)
                vb = p_ref[rows, vs].astype(BF16)
                gv = p_ref[rows, gs]
                ov = o_ref[rows, os_]
                dyv = dy_ref[rows, os_]
                gnv = gn_ref[:, os_]
                sg = _sigmoid(gv)
                rstd = lax.rsqrt(jnp.mean(ov * ov, axis=-1, keepdims=True) + EPS)
                oh = ov * rstd
                d_on = dyv * (gv * sg)
                dg = dyv * (oh * gnv) * (sg * (1.0 + gv * (1.0 - sg)))
                dgn_parts.append(jnp.sum(d_on * oh, axis=0, keepdims=True))
                d_oh = d_on * gnv
                do = rstd * (d_oh - oh * jnp.mean(d_oh * oh, axis=-1, keepdims=True))
                dob = do.astype(BF16)
                mask = intra_ref[h]
                a_b = (_dot_nt(qb, kb) * mask).astype(BF16)
                da_b = (_dot_nt(dob, vb) * mask).astype(BF16)
                ds_new = dstate[h]
                ds_new_b = ds_new.astype(BF16)
                s_old_b = st_ref[c, h]
                qdv, kdv = qd_ref[h], kd_ref[h]
                dv = _dot_tn(a_b, dob) + _dot((kr * kdv).astype(BF16), ds_new_b)
                dqr = _dot(da_b, kb) + _dot_nt(dob, s_old_b) * qdv
                dkr = _dot_tn(da_b, qb) + _dot_nt(vb, ds_new_b) * kdv
                dstate[h] = ds_new * s_dec[h] + _dot_tn((qr * qdv).astype(BF16), dob)
                dp_ref[rows, qs] = _unrope(dqr, cosv, sinv).astype(BF16)
                dp_ref[rows, ks] = _unrope(dkr * k_scale, cosv, sinv).astype(BF16)
                dp_ref[rows, vs] = dv.astype(BF16)
                dp_ref[rows, gs] = dg.astype(BF16)
            part = jnp.concatenate(dgn_parts, axis=-1)
            dgn = part if dgn is None else dgn + part
        _accumulate(dgn_ref, dgn, i)

    steps = n // RET_STEP
    step = RET_STEP * CHUNK
    rev = lambda i: (steps - 1 - i, 0)
    full3 = lambda a: pl.BlockSpec(a.shape, lambda i: (0, 0, 0))
    return _call(
        body, name, (steps,),
        [pl.BlockSpec((step, cols), rev),
         pl.BlockSpec((step, RET_DK // 2), rev),
         pl.BlockSpec((step, RET_DK // 2), rev),
         full3(intra), full3(qd), full3(kd),
         pl.BlockSpec((1, RET_V_COLS), lambda i: (0, 0)),
         pl.BlockSpec((step, RET_V_COLS), rev),
         pl.BlockSpec((RET_STEP, RET_HEADS, RET_DK, RET_DV), lambda i: (steps - 1 - i, 0, 0, 0)),
         pl.BlockSpec((step, RET_V_COLS), rev)],
        [pl.BlockSpec((step, cols), rev),
         pl.BlockSpec((1, RET_V_COLS), lambda i: (0, 0))],
        [jax.ShapeDtypeStruct((t, cols), BF16),
         jax.ShapeDtypeStruct((1, RET_V_COLS), F32)],
        [pltpu.VMEM((RET_HEADS, RET_DK, RET_DV), F32)], ("arbitrary",),
        (proj, cos, sin, intra, qd, kd, gn, o_saved, states, dy), rider)


def _att_common(q_ref, kp_ref, vp_ref, sub):
    blk = pl.program_id(1) * ATT_SUBS + sub
    start = pl.multiple_of(blk * Q_BLOCK, Q_BLOCK)
    kw = kp_ref[pl.ds(start, K_WINDOW), :]
    vw = vp_ref[pl.ds(start, K_WINDOW), :]
    kvalid = blk * Q_BLOCK - K_PAD + lax.broadcasted_iota(jnp.int32, (1, K_WINDOW), 1) >= 0
    lane = lax.broadcasted_iota(jnp.int32, (1, LANES), 1)
    qrows = slice(sub * Q_BLOCK, (sub + 1) * Q_BLOCK)
    return start, qrows, q_ref[qrows, :], kw, vw, kvalid, (lane < ATT_DH, lane >= ATT_DH)


def _row_groups():
    return [slice(r * ATT_ROWS, (r + 1) * ATT_ROWS) for r in range(Q_BLOCK // ATT_ROWS)]


def _lane_copies(x):
    return jnp.tile(x, (1, K_WINDOW // LANES))


def _att_specs(t, tp):
    qspec = pl.BlockSpec((ATT_SUBS * Q_BLOCK, LANES), lambda h, i: (i, h))
    kspec = pl.BlockSpec((tp, LANES), lambda h, i: (0, h))
    bspec = pl.BlockSpec((2, Q_BLOCK, K_WINDOW), lambda h, i: (h, 0, 0))
    return qspec, kspec, bspec


def _att_fwd(q, kp, vp, bias, name, rider=None):
    t, d = q.shape
    tp = kp.shape[0]

    def body(q_ref, kp_ref, vp_ref, bias_ref, o_ref, lse_ref, s_scr, p_scr, lse_scr):
        for sub in range(ATT_SUBS):
            _, qrows, q2, kw, vw, kvalid, sel = _att_common(q_ref, kp_ref, vp_ref, sub)
            for hh in range(2):
                s_scr[sub, hh] = _dot_nt(jnp.where(sel[hh], q2, 0), kw)
            for hh in range(2):
                for rows in _row_groups():
                    s = jnp.where(kvalid, s_scr[sub, hh, rows, :] + bias_ref[hh, rows, :], NEG)
                    m = jnp.max(s, axis=-1, keepdims=True)
                    e = jnp.exp(s - m)
                    l = jnp.sum(e, axis=-1, keepdims=True)
                    p_scr[sub, hh, rows, :] = (e * (1.0 / l)).astype(BF16)
                    lse_scr[sub, hh, rows, :] = jnp.broadcast_to(m + jnp.log(l), (ATT_ROWS, LANES))
            outs = [_dot(p_scr[sub, hh], vw) for hh in range(2)]
            o_ref[qrows, :] = jnp.where(sel[0], outs[0], outs[1]).astype(BF16)
            lse_ref[qrows, :] = jnp.where(sel[0], lse_scr[sub, 0], lse_scr[sub, 1])

    qspec, kspec, bspec = _att_specs(t, tp)
    return _call(body, name, (d // LANES, t // (ATT_SUBS * Q_BLOCK)), [qspec, kspec, kspec, bspec], [qspec, qspec],
                 [jax.ShapeDtypeStruct((t, d), BF16), jax.ShapeDtypeStruct((t, d), F32)],
                 [pltpu.VMEM((ATT_SUBS, 2, Q_BLOCK, K_WINDOW), F32),
                  pltpu.VMEM((ATT_SUBS, 2, Q_BLOCK, K_WINDOW), BF16),
                  pltpu.VMEM((ATT_SUBS, 2, Q_BLOCK, LANES), F32)],
                 ("parallel", "arbitrary"), (q, kp, vp, bias), rider)


def _att_bwd(q, kp, vp, bias, do, o, lse, name, rider=None):
    t, d = q.shape
    tp = kp.shape[0]

    def body(q_ref, kp_ref, vp_ref, bias_ref, do_ref, o_ref, lse_ref, dq_ref, dkp_ref, dvp_ref, db_ref,
             s_scr, dp_scr, p_scr, ds_scr, row_scr):
        @pl.when(pl.program_id(1) == 0)
        def _():
            dkp_ref[...] = jnp.zeros_like(dkp_ref)
            dvp_ref[...] = jnp.zeros_like(dvp_ref)
            db_ref[...] = jnp.zeros_like(db_ref)

        for sub in range(ATT_SUBS):
            start, qrows, q2, kw, vw, kvalid, sel = _att_common(q_ref, kp_ref, vp_ref, sub)
            do2 = do_ref[qrows, :]
            qm = [jnp.where(sel[hh], q2, 0) for hh in range(2)]
            dom = [jnp.where(sel[hh], do2, 0) for hh in range(2)]
            do_o = do2.astype(F32) * o_ref[qrows, :].astype(F32)
            lse2 = lse_ref[qrows, :]
            for hh in range(2):
                s_scr[sub, hh] = _dot_nt(qm[hh], kw)
                dp_scr[sub, hh] = _dot_nt(dom[hh], vw)
                lse_h = jnp.max(jnp.where(sel[hh], lse2, NEG), axis=-1, keepdims=True)
                delta = jnp.sum(jnp.where(sel[hh], do_o, 0.0), axis=-1, keepdims=True)
                row_scr[sub, hh, 0] = jnp.broadcast_to(lse_h, (Q_BLOCK, LANES))
                row_scr[sub, hh, 1] = jnp.broadcast_to(delta, (Q_BLOCK, LANES))
            for hh in range(2):
                for rows in _row_groups():
                    s = jnp.where(kvalid, s_scr[sub, hh, rows, :] + bias_ref[hh, rows, :], NEG)
                    p = jnp.exp(s - _lane_copies(row_scr[sub, hh, 0, rows, :]))
                    ds = p * (dp_scr[sub, hh, rows, :] - _lane_copies(row_scr[sub, hh, 1, rows, :]))
                    db_ref[hh, rows, :] += ds
                    p_scr[sub, hh, rows, :] = p.astype(BF16)
                    ds_scr[sub, hh, rows, :] = ds.astype(BF16)
            dqs = [_dot(ds_scr[sub, hh], kw) for hh in range(2)]
            dq_ref[qrows, :] = jnp.where(sel[0], dqs[0], dqs[1])
            dkp_ref[:, pl.ds(start, K_WINDOW)] += (_dot_tn(qm[0], ds_scr[sub, 0]) +
                                                   _dot_tn(qm[1], ds_scr[sub, 1]))
            dvp_ref[:, pl.ds(start, K_WINDOW)] += (_dot_tn(dom[0], p_scr[sub, 0]) +
                                                   _dot_tn(dom[1], p_scr[sub, 1]))

    qspec, kspec, bspec = _att_specs(t, tp)
    tspec = pl.BlockSpec((LANES, tp), lambda h, i: (h, 0))
    stage = lambda dt: pltpu.VMEM((ATT_SUBS, 2, Q_BLOCK, K_WINDOW), dt)
    return _call(body, name, (d // LANES, t // (ATT_SUBS * Q_BLOCK)),
                 [qspec, kspec, kspec, bspec, qspec, qspec, qspec],
                 [qspec, tspec, tspec, bspec],
                 [jax.ShapeDtypeStruct((t, d), F32),
                  jax.ShapeDtypeStruct((d, tp), F32),
                  jax.ShapeDtypeStruct((d, tp), F32),
                  jax.ShapeDtypeStruct((ATT_HEADS, Q_BLOCK, K_WINDOW), F32)],
                 [stage(F32), stage(F32), stage(BF16), stage(BF16),
                  pltpu.VMEM((ATT_SUBS, 2, 2, Q_BLOCK, LANES), F32)],
                 ("parallel", "arbitrary"), (q, kp, vp, bias, do, o, lse), rider)


def _rel_bin_matrix():
    rows = REL_DELTAS * 2 * REL_BLK
    rho = lax.broadcasted_iota(jnp.int32, (rows, REL_PAD), 0)
    col = lax.broadcasted_iota(jnp.int32, (rows, REL_PAD), 1)
    assert 2 * REL_BLK == 256
    delta = rho >> 8
    c = 255 - (rho & 255)
    dist = K_PAD + REL_BLK * (delta - (K_WINDOW // REL_BLK - 1)) + (c - (REL_BLK - 1))
    idx = jnp.clip(dist, -REL_CLIP, REL_CLIP) + REL_CLIP
    return col == idx


def _rel_expand(rel_pad, name):
    heads = rel_pad.shape[0]
    rows = REL_DELTAS * 2 * REL_BLK

    def body_bin(r_ref, o_ref):
        onehot = jnp.where(_rel_bin_matrix(), 1.0, 0.0).astype(BF16)
        hi, mid, lo = _split3(r_ref[...])
        o_ref[...] = _dot_nt(hi, onehot) + _dot_nt(mid, onehot) + _dot_nt(lo, onehot)

    by_delta = pl.pallas_call(
        body_bin, name=name + "_bin",
        out_shape=jax.ShapeDtypeStruct((heads, rows), F32),
        compiler_params=pltpu.CompilerParams(vmem_limit_bytes=VMEM_LIMIT_V7X),
    )(rel_pad)
    by_delta = by_delta.reshape(heads * REL_DELTAS, 2 * REL_BLK)

    def body_shift(t_ref, o_ref):
        tv = t_ref[...]
        for r in range(REL_BLK):
            o_ref[r] = pltpu.roll(tv, (r + REL_BLK) % (2 * REL_BLK), 1)[:, :REL_BLK]

    return pl.pallas_call(
        body_shift, name=name + "_shift",
        out_shape=jax.ShapeDtypeStruct((REL_BLK, heads * REL_DELTAS, REL_BLK), F32),
        compiler_params=pltpu.CompilerParams(vmem_limit_bytes=VMEM_LIMIT_V7X),
    )(by_delta)


def _bias_table(rel_bias, name):
    heads = rel_bias.shape[0]
    rel_pad = jnp.pad(rel_bias, ((0, 0), (0, REL_PAD - REL_TABLE)))
    tiles = _rel_expand(rel_pad, name)
    tiles = tiles.reshape(REL_BLK, heads, REL_DELTAS, REL_BLK).transpose(1, 2, 0, 3)
    na, nb = Q_BLOCK // REL_BLK, K_WINDOW // REL_BLK
    rows = [jnp.concatenate([tiles[:, a - b + nb - 1] for b in range(nb)], axis=-1) for a in range(na)]
    table = jnp.concatenate(rows, axis=-2)
    qc = np.arange(Q_BLOCK)[:, None] // CHUNK
    kc = np.arange(K_WINDOW)[None, :] // CHUNK
    band = (kc >= qc) & (kc <= qc + PAST_CHUNKS)
    return jnp.where(jnp.asarray(band)[None], table, NEG)


def _rel_reduce(db, name):
    heads = db.shape[0]
    na, nb = Q_BLOCK // REL_BLK, K_WINDOW // REL_BLK

    fold_heads = 4

    def body_fold(db_ref, g_ref):
        for hd in range(fold_heads):
            for delta in range(REL_DELTAS):
                acc = None
                for a in range(na):
                    b = a - (delta - (nb - 1))
                    if 0 <= b < nb:
                        tile = db_ref[hd, a * REL_BLK:(a + 1) * REL_BLK, b * REL_BLK:(b + 1) * REL_BLK]
                        acc = tile if acc is None else acc + tile
                g_ref[hd, delta] = acc

    folded = pl.pallas_call(
        body_fold, name=name + "_fold", grid=(heads // fold_heads,),
        in_specs=[pl.BlockSpec((fold_heads, Q_BLOCK, K_WINDOW), lambda h: (h, 0, 0))],
        out_specs=pl.BlockSpec((fold_heads, REL_DELTAS, REL_BLK, REL_BLK), lambda h: (h, 0, 0, 0)),
        out_shape=jax.ShapeDtypeStruct((heads, REL_DELTAS, REL_BLK, REL_BLK), F32),
        compiler_params=_params("parallel"),
    )(db)
    by_row = folded.transpose(2, 0, 1, 3).reshape(REL_BLK, heads * REL_DELTAS, REL_BLK)

    def body_diag(g_ref, d_ref):
        zeros = jnp.zeros((heads * REL_DELTAS, REL_BLK), F32)
        acc = None
        for r in range(REL_BLK):
            part = pltpu.roll(jnp.concatenate([g_ref[r], zeros], axis=1), REL_BLK - r, 1)
            acc = part if acc is None else acc + part
        d_ref[...] = acc

    diag = pl.pallas_call(
        body_diag, name=name + "_diag",
        out_shape=jax.ShapeDtypeStruct((heads * REL_DELTAS, 2 * REL_BLK), F32),
        compiler_params=pltpu.CompilerParams(vmem_limit_bytes=VMEM_LIMIT_V7X),
    )(by_row)
    diag = diag.reshape(heads, REL_DELTAS * 2 * REL_BLK)

    def body_bin(d_ref, o_ref):
        onehot = jnp.where(_rel_bin_matrix(), 1.0, 0.0).astype(BF16)
        hi, mid, lo = _split3(d_ref[...])
        o_ref[...] = _dot(hi, onehot) + _dot(mid, onehot) + _dot(lo, onehot)

    out = pl.pallas_call(
        body_bin, name=name + "_bin",
        out_shape=jax.ShapeDtypeStruct((heads, REL_PAD), F32),
        compiler_params=pltpu.CompilerParams(vmem_limit_bytes=VMEM_LIMIT_V7X),
    )(diag)
    return out[:, :REL_TABLE]


def _sum_leading(x, name):
    n, r, c = x.shape
    tr = _pick(r, 256, 8)

    def body(x_ref, o_ref):
        acc = x_ref[0].astype(F32)
        for k in range(1, n):
            acc = acc + x_ref[k].astype(F32)
        o_ref[...] = acc

    return pl.pallas_call(
        body, name=name, grid=(r // tr,),
        in_specs=[pl.BlockSpec((n, tr, c), lambda i: (0, i, 0))],
        out_specs=pl.BlockSpec((tr, c), lambda i: (i, 0)),
        out_shape=jax.ShapeDtypeStruct((r, c), F32),
        compiler_params=_params("parallel"),
    )(x)


def _pair_add(g, recv, parity, name):
    _, r, c = g.shape
    tr = _pick(r, 256, 16)

    def body(par_ref, g_ref, r_ref, o_ref):
        o_ref[...] = (g_ref[...].astype(F32) + r_ref[...].astype(F32)).astype(BF16)

    return pl.pallas_call(
        body, name=name,
        grid_spec=pltpu.PrefetchScalarGridSpec(
            num_scalar_prefetch=1, grid=(4, r // tr),
            in_specs=[pl.BlockSpec((1, tr, c), lambda k, i, par: (2 * k + par[0], i, 0)),
                      pl.BlockSpec((1, tr, c), lambda k, i, par: (k, i, 0))],
            out_specs=pl.BlockSpec((1, tr, c), lambda k, i, par: (k, i, 0))),
        out_shape=jax.ShapeDtypeStruct((4, r, c), BF16),
        compiler_params=_params("parallel", "parallel"),
    )(parity, g, recv)


def _adamw(w, g_parts, m, v, name):
    r, c = w.shape
    n = g_parts.shape[0]
    tr = _pick(r, 256, 16 if g_parts.dtype == BF16 else 8)
    c1 = 1.0 - ADAM_B1 ** ADAM_STEP
    c2 = 1.0 - ADAM_B2 ** ADAM_STEP

    def body(w_ref, g_ref, m_ref, v_ref, go_ref, d_ref, nm_ref, nv_ref):
        gv = g_ref[0].astype(F32)
        for k in range(1, n):
            gv = gv + g_ref[k].astype(F32)
        nm = ADAM_B1 * m_ref[...] + (1.0 - ADAM_B1) * gv
        nv = ADAM_B2 * v_ref[...] + (1.0 - ADAM_B2) * (gv * gv)
        go_ref[...] = gv
        d_ref[...] = -ADAM_LR * ((nm / c1) / (jnp.sqrt(nv / c2) + ADAM_EPS) + ADAM_WD * w_ref[...])
        nm_ref[...] = nm
        nv_ref[...] = nv

    spec = pl.BlockSpec((tr, c), lambda i: (i, 0))
    shp = jax.ShapeDtypeStruct((r, c), F32)
    return pl.pallas_call(
        body, name=name, grid=(r // tr,),
        in_specs=[spec, pl.BlockSpec((n, tr, c), lambda i: (0, i, 0)), spec, spec],
        out_specs=[spec] * 4, out_shape=[shp] * 4,
        compiler_params=_params("parallel"),
    )(w, g_parts, m, v)


BIG = (("a_w_in", 1), ("a_w_o", 0), ("a_w_gu", 0), ("a_w_down", 0), ("w_kv", 1),
       ("b_w_q", 0), ("b_w_o", 0), ("b_w_gu", 0), ("b_w_down", 0))
TRANSPOSED = ("a_w_gu", "b_w_gu")
FFN_BLK = 2 * FFN_HIDDEN // N_DEV

SMALL = (("a_norm_g", D_MODEL, True), ("a_gn_g", RET_V_COLS, True), ("a_ffn_norm_g", D_MODEL, True),
         ("kv_norm_g", D_MODEL, False), ("b_norm_g", D_MODEL, False), ("b_ffn_norm_g", D_MODEL, False),
         ("k_norm_g", ATT_DH, False), ("b_q_norm_g", ATT_DH, False),
         ("b_rel_bias", ATT_HEADS * REL_TABLE, False))
SMALL_ROWS, SMALL_COLS = 16, 1024


def _pack_small(vals, last=None):
    flat = jnp.concatenate([vals[n].reshape(-1) for n, _, _ in SMALL])
    room = SMALL_ROWS * SMALL_COLS - flat.shape[0]
    if last is None:
        flat = jnp.pad(flat, (0, room))
    else:
        flat = jnp.concatenate([jnp.pad(flat, (0, room - 1)), last.reshape(1)])
    return flat.reshape(SMALL_ROWS, SMALL_COLS)


def _unpack_small(packed, local):
    flat, out, pos = packed.reshape(-1), {}, 0
    for n, length, sharded in SMALL:
        ln = length // N_DEV if (local and sharded) else length
        out[n] = flat[pos:pos + ln]
        pos += ln
    return out


def _gather_rider(shards, names):
    return _GatherRider([shards[n] for n in names])


def _gathered(rider, names, axis_of):
    return {n: (r.reshape(-1, r.shape[2]) if axis_of[n] == 0 else r) for n, r in zip(names, rider.results)}


def _blocks(g):
    return g if g.ndim == 3 else g.reshape(N_DEV, -1, g.shape[-1])


def _local_step(x, target, shards, s, parity):
    t = x.shape[0]
    axis_of = dict(BIG)
    consts = _ret_consts(t)
    lane_to_head = np.zeros((D_MODEL, LANES), np.float32)
    lane_to_head[np.arange(D_MODEL), np.arange(D_MODEL) // ATT_DH] = 1.0
    bd = jnp.asarray(lane_to_head).astype(BF16)
    kg_t = jnp.tile(s["k_norm_g"], (1, ATT_HEADS))
    qg_t = jnp.tile(s["b_q_norm_g"], (1, ATT_HEADS))
    q_scale = ATT_DH ** -0.5
    w, g, recv = {}, {}, {}

    def gather_on(names):
        return _gather_rider(shards, names), names

    def landed(ride):
        w.update(_gathered(ride[0], ride[1], axis_of))

    def scatter_on(names):
        return _ScatterRider([_blocks(g[n]) for n in names]), names

    def reduced(ride):
        recv.update(zip(ride[1], ride[0].results))

    proj, (w["a_w_in"], w_o) = _proj_gather(x, s["a_norm_g"], shards["a_w_in"], [shards["a_w_o"]], "a_proj")
    w["a_w_o"] = w_o.reshape(-1, w_o.shape[2])
    ride = gather_on(["a_w_gu"])
    y, o_ret, states = _ret_fwd(proj, s["a_gn_g"], consts, "a_ret", rider=ride[0])
    landed(ride)
    ride = gather_on(["w_kv"])
    x1 = _mm(y, w["a_w_o"], "nn", "a_out", res=x, rider=ride[0])
    landed(ride)
    ride = gather_on(["a_w_down", "b_w_q", "b_w_o"])
    gu_a, act_a = _mm(x1, w["a_w_gu"], "nt", "a_ffn_gu", epilogue="swiglu", out_block=FFN_BLK,
                      norm_g=s["a_ffn_norm_g"], rider=ride[0])
    landed(ride)
    x2 = _mm(act_a, w["a_w_down"], "nn", "a_ffn_down", res=x1)

    kv = _mm(x2, w["w_kv"], "nn", "kv_proj", norm_g=s["kv_norm_g"])
    kp, vp = _kv_prep(kv, kg_t, bd, "kv_prep")

    q_raw = _mm(x2, w["b_w_q"], "nn", "b_q", norm_g=s["b_norm_g"])
    qn = _q_hnorm(q_raw, qg_t, bd, q_scale, "q_hnorm")
    bias = _bias_table(s["b_rel_bias"].reshape(ATT_HEADS, REL_TABLE), "rel")
    ride = gather_on(["b_w_gu", "b_w_down"])
    o_att, lse = _att_fwd(qn, kp, vp, bias, "b_att", rider=ride[0])
    landed(ride)
    x3 = _mm(o_att, w["b_w_o"], "nn", "b_out", res=x2)
    gu_b, act_b = _mm(x3, w["b_w_gu"], "nt", "b_ffn_gu", epilogue="swiglu", out_block=FFN_BLK,
                      norm_g=s["b_ffn_norm_g"])
    dy, loss = _mm(act_b, w["b_w_down"], "nn", "b_ffn_down", res=x3, epilogue="loss", extra=(target,))
    in_blk, kv_blk, ffn_blk = w["a_w_in"].shape[2], w["w_kv"].shape[2], FFN_BLK

    dgu = _mm(dy, w["b_w_down"], "nt", "b_ffn_dgu", out_block=ffn_blk, epilogue="swiglu_bwd", extra=gu_b)
    dgu = dgu.reshape(N_DEV, t, ffn_blk)
    g["b_w_down"] = _mm(act_b, dy, "tn", "b_ffn_gdown", out_dtype=BF16)
    ride = scatter_on(["b_w_down"])
    dx3, g["b_ffn_norm_g"] = _mm(dgu, w["b_w_gu"], "nn", "b_ffn_dh", epilogue="rms_bwd",
                                 extra=(x3, s["b_ffn_norm_g"], dy), rider=ride[0])
    reduced(ride)
    g["b_w_gu"] = _mm(dgu, x3, "tn", "b_ffn_ggu", out_dtype=BF16, norm_g=s["b_ffn_norm_g"], norm_b=True)

    do_att = _mm(dx3, w["b_w_o"], "nt", "b_dout", out_dtype=BF16)
    g["b_w_o"] = _mm(o_att, dx3, "tn", "b_gout", out_dtype=BF16)
    ride = scatter_on(["b_w_gu", "b_w_o"])
    dq, dkp, dvp, db = _att_bwd(qn, kp, vp, bias, do_att, o_att, lse, "b_datt", rider=ride[0])
    reduced(ride)
    g["b_rel_bias"] = _rel_reduce(db, "drel").reshape(1, -1)
    dq_raw, gq = _q_dhnorm(q_raw, qg_t, bd, dq, q_scale, "q_dhnorm")
    g["b_q_norm_g"] = gq.reshape(ATT_HEADS, ATT_DH).sum(axis=0, keepdims=True)
    g["b_w_q"] = _mm(x2, dq_raw, "tn", "b_gq", out_dtype=BF16, norm_g=s["b_norm_g"])
    dx2, g["b_norm_g"] = _mm(dq_raw, w["b_w_q"], "nt", "b_dq", epilogue="rms_bwd",
                             extra=(x2, s["b_norm_g"], dx3))

    dkv, gk = _kv_dprep(kv, kg_t, bd, dkp, dvp, "kv_dprep")
    g["k_norm_g"] = gk.reshape(ATT_HEADS, ATT_DH).sum(axis=0, keepdims=True)
    g["w_kv"] = _mm(x2, dkv, "tn", "kv_g", out_dtype=BF16, out_block=kv_blk, norm_g=s["kv_norm_g"])
    dx2, g["kv_norm_g"] = _mm(dkv, w["w_kv"], "nt", "kv_du", epilogue="rms_bwd",
                              extra=(x2, s["kv_norm_g"], dx2))

    ride = scatter_on(["b_w_q"])
    dgu = _mm(dx2, w["a_w_down"], "nt", "a_ffn_dgu", out_block=ffn_blk, epilogue="swiglu_bwd", extra=gu_a,
              rider=ride[0])
    reduced(ride)
    dgu = dgu.reshape(N_DEV, t, ffn_blk)
    g["a_w_down"] = _mm(act_a, dx2, "tn", "a_ffn_gdown", out_dtype=BF16)
    ride = scatter_on(["a_w_down"])
    dx1, g["a_ffn_norm_g"] = _mm(dgu, w["a_w_gu"], "nn", "a_ffn_dh", epilogue="rms_bwd",
                                 extra=(x1, s["a_ffn_norm_g"], dx2), rider=ride[0])
    reduced(ride)
    ride = scatter_on(["w_kv"])
    g["a_w_gu"] = _mm(dgu, x1, "tn", "a_ffn_ggu", out_dtype=BF16, norm_g=s["a_ffn_norm_g"], norm_b=True,
                      rider=ride[0])
    reduced(ride)

    swap = _SiblingSwapRider([_blocks(g["a_w_gu"])])
    dy_ret = _mm(dx1, w["a_w_o"], "nt", "a_dout", rider=swap)
    g["a_w_o"] = _mm(y, dx1, "tn", "a_gout", out_dtype=BF16)
    chips = _ChipScatterRider([_pair_add(_blocks(g["a_w_gu"]), swap.results[0], parity, "rs_pair_add_gu")])
    dproj, g["a_gn_g"] = _ret_bwd(proj, s["a_gn_g"], o_ret, states, dy_ret, consts, "a_dret", rider=chips)
    recv["a_w_gu"] = chips.results[0]
    ride = scatter_on(["a_w_o"])
    g["a_w_in"] = _mm(x, dproj, "tn", "a_gin", out_dtype=BF16, out_block=in_blk, norm_g=s["a_norm_g"],
                      rider=ride[0])
    reduced(ride)
    from_sibling = _exchange(_SiblingSwapRider([g["a_w_in"]]), "rs_sibling")[0]
    chip_sums = _pair_add(g["a_w_in"], from_sibling, parity, "rs_pair_add")
    last = _ChipScatterRider([chip_sums])
    grad_x, g["a_norm_g"] = _mm(dproj, w["a_w_in"], "nt", "a_dproj", epilogue="rms_bwd",
                                extra=(x, s["a_norm_g"], dx1), rider=last)
    recv["a_w_in"] = last.results[0]
    return loss, grad_x, recv, g


ARG_NAMES = ("x", "a_norm_g", "a_w_in", "a_gn_g", "a_w_o", "a_ffn_norm_g", "a_w_gu", "a_w_down",
             "kv_norm_g", "w_kv", "k_norm_g", "b_norm_g", "b_w_q", "b_q_norm_g", "b_rel_bias", "b_w_o",
             "b_ffn_norm_g", "b_w_gu", "b_w_down")
WEIGHT_NAMES = ARG_NAMES[1:]


def _big_shard(a, name):
    a = a[0] if a.ndim == 3 else a
    return a.T if name in TRANSPOSED else a


def _as_given(a, name, shape):
    return (a.T if name in TRANSPOSED else a).reshape(shape)


def kernel(x, a_norm_g, a_w_in, a_gn_g, a_w_o, a_ffn_norm_g, a_w_gu, a_w_down, kv_norm_g, w_kv, k_norm_g, b_norm_g, b_w_q, b_q_norm_g, b_rel_bias, b_w_o, b_ffn_norm_g, b_w_gu, b_w_down, loss_target, m_a_norm_g, m_a_w_in, m_a_gn_g, m_a_w_o, m_a_ffn_norm_g, m_a_w_gu, m_a_w_down, m_kv_norm_g, m_w_kv, m_k_norm_g, m_b_norm_g, m_b_w_q, m_b_q_norm_g, m_b_rel_bias, m_b_w_o, m_b_ffn_norm_g, m_b_w_gu, m_b_w_down, v_a_norm_g, v_a_w_in, v_a_gn_g, v_a_w_o, v_a_ffn_norm_g, v_a_w_gu, v_a_w_down, v_kv_norm_g, v_w_kv, v_k_norm_g, v_b_norm_g, v_b_w_q, v_b_q_norm_g, v_b_rel_bias, v_b_w_o, v_b_ffn_norm_g, v_b_w_gu, v_b_w_down):
    args = (x, a_norm_g, a_w_in, a_gn_g, a_w_o, a_ffn_norm_g, a_w_gu, a_w_down, kv_norm_g, w_kv, k_norm_g,
            b_norm_g, b_w_q, b_q_norm_g, b_rel_bias, b_w_o, b_ffn_norm_g, b_w_gu, b_w_down)
    p = dict(zip(ARG_NAMES, args))
    m_all = dict(zip(WEIGHT_NAMES, (m_a_norm_g, m_a_w_in, m_a_gn_g, m_a_w_o, m_a_ffn_norm_g, m_a_w_gu,
                                    m_a_w_down, m_kv_norm_g, m_w_kv, m_k_norm_g, m_b_norm_g, m_b_w_q,
                                    m_b_q_norm_g, m_b_rel_bias, m_b_w_o, m_b_ffn_norm_g, m_b_w_gu, m_b_w_down)))
    v_all = dict(zip(WEIGHT_NAMES, (v_a_norm_g, v_a_w_in, v_a_gn_g, v_a_w_o, v_a_ffn_norm_g, v_a_w_gu,
                                    v_a_w_down, v_kv_norm_g, v_w_kv, v_k_norm_g, v_b_norm_g, v_b_w_q,
                                    v_b_q_norm_g, v_b_rel_bias, v_b_w_o, v_b_ffn_norm_g, v_b_w_gu, v_b_w_down)))
    xi, yi, ci = _my_place()
    me = 4 * xi + 2 * yi + ci
    big_names = [n for n, _ in BIG]

    big_local = {n: _big_shard(p[n], n) for n in big_names}
    shards = {n: a.astype(BF16) for n, a in big_local.items()}
    small_local = _pack_small({n: p[n] for n, _, _ in SMALL})
    small_all = _exchange(_GatherRider([small_local]), "gather_small")[0]
    flat_g = small_all.reshape(N_DEV, -1)
    s_full, pos = {}, 0
    for n, length, sharded in SMALL:
        ln = length // N_DEV if sharded else length
        s_full[n] = flat_g[:, pos:pos + ln].reshape(1, -1) if sharded else p[n].reshape(1, -1)
        pos += ln

    parity = jnp.reshape(ci, (1,)).astype(jnp.int32)
    loss, grad_x, recv, g = _local_step(x[0], loss_target[0], shards, s_full, parity)

    partial = _pack_small({n: g[n] for n, _, _ in SMALL}, last=loss)
    summed = _sum_leading(_exchange(_GatherRider([partial]), "gather_gsmall")[0], "gsmall_sum")
    loss = summed[SMALL_ROWS - 1, SMALL_COLS - 1]
    g_small = _unpack_small(summed, local=False)
    for n, length, sharded in SMALL:
        if sharded:
            g_small[n] = lax.dynamic_slice(g_small[n], (me * (length // N_DEV),), (length // N_DEV,))

    grads, deltas, new_m, new_v = {}, {}, {}, {}
    for n in big_names:
        outs = _adamw(big_local[n], recv[n], _big_shard(m_all[n], n), _big_shard(v_all[n], n), "adamw_" + n)
        grads[n], deltas[n], new_m[n], new_v[n] = (_as_given(a, n, p[n].shape) for a in outs)
    pk = lambda src: _pack_small({n: src[n] for n, _, _ in SMALL})
    outs = _adamw(small_local, pk(g_small)[None], pk(m_all), pk(v_all), "adamw_small")
    g_s, d_s, nm_s, nv_s = (_unpack_small(a, local=True) for a in outs)
    for n, _, _ in SMALL:
        grads[n], deltas[n], new_m[n], new_v[n] = (a[n].reshape(p[n].shape) for a in (g_s, d_s, nm_s, nv_s))

    return (loss, grad_x[None], *[grads[n] for n in WEIGHT_NAMES], *[deltas[n] for n in WEIGHT_NAMES],
            *[new_m[n] for n in WEIGHT_NAMES], *[new_v[n] for n in WEIGHT_NAMES])
```

```python
import numpy as np
import jax
import jax.numpy as jnp
from jax import lax
from jax.experimental import pallas as pl
from jax.experimental.pallas import tpu as pltpu

F32 = jnp.float32
BF16 = jnp.bfloat16

N_DEV = 8
D_MODEL = 1024
CHUNK = 64
EPS = 1e-6
RET_HEADS, RET_DK, RET_DV = 4, 256, 512
RET_STEP = 4
RET_Q_COLS = RET_HEADS * RET_DK
RET_V_COLS = RET_HEADS * RET_DV
ATT_HEADS, ATT_DH = 16, 64
PAST_CHUNKS = 8
REL_CLIP = 256
REL_TABLE = 2 * REL_CLIP + 1
FFN_HIDDEN = 2816
ROPE_BASE = 10000.0
LANES = 128
Q_BLOCK = 256
ATT_SUBS = 4
ATT_ROWS = 32
K_PAD = PAST_CHUNKS * CHUNK
K_WINDOW = Q_BLOCK + K_PAD
REL_BLK = 128
REL_DELTAS = Q_BLOCK // REL_BLK + K_WINDOW // REL_BLK - 1
REL_PAD = 640
NEG = -1e30
VMEM_LIMIT_V7X = 56 * 1024 * 1024
ADAM_LR, ADAM_B1, ADAM_B2, ADAM_EPS, ADAM_WD, ADAM_STEP = 1e-3, 0.9, 0.999, 1e-8, 0.01, 10
MESH = pl.DeviceIdType.MESH
ANY = pl.BlockSpec(memory_space=pl.ANY)


def _params(*semantics):
    return pltpu.CompilerParams(dimension_semantics=semantics, vmem_limit_bytes=VMEM_LIMIT_V7X)


def _pick(dim, cap, align):
    best = None
    for t in range(align, min(dim, cap) + 1, align):
        if dim % t == 0:
            best = t
    assert best is not None, (dim, cap, align)
    return best


def _dot(a, b):
    return lax.dot_general(a, b, (((1,), (0,)), ((), ())), preferred_element_type=F32)


def _dot_nt(a, b):
    return lax.dot_general(a, b, (((1,), (1,)), ((), ())), preferred_element_type=F32)


def _dot_tn(a, b):
    return lax.dot_general(a, b, (((0,), (0,)), ((), ())), preferred_element_type=F32)


def _split2(x):
    hi = x.astype(BF16)
    lo = (x - hi.astype(F32)).astype(BF16)
    return hi, lo


def _split3(x):
    hi = x.astype(BF16)
    r = x - hi.astype(F32)
    mid = r.astype(BF16)
    lo = (r - mid.astype(F32)).astype(BF16)
    return hi, mid, lo


def _sigmoid(x):
    return 1.0 / (1.0 + jnp.exp(-x))


def _accumulate(ref, part, step):
    @pl.when(step == 0)
    def _():
        ref[...] = part

    @pl.when(step > 0)
    def _():
        ref[...] += part


RELAY_AT_NUM, RELAY_AT_DEN = 3, 4


def _my_place():
    return lax.axis_index("x"), lax.axis_index("y"), lax.axis_index("c")


def _flip(v, bit):
    return 1 - v if bit else v


class _NoRelay:
    def relay(self, in_refs, out_refs, sems):
        pass


class _GatherRider:
    def __init__(self, xs):
        self.inputs = list(xs)
        n = len(xs)
        self.out_shape = [jax.ShapeDtypeStruct((N_DEV,) + x.shape, x.dtype) for x in xs]
        self.scratch = [pltpu.SemaphoreType.DMA((7, n)), pltpu.SemaphoreType.DMA((7, n)),
                        pltpu.SemaphoreType.DMA((n,))]
        self.results = None

    def _copies(self, x_refs, out_refs, sems):
        send_sems, recv_sems, local_sems = sems
        n = len(x_refs)
        x, y, c = _my_place()
        me, sibling = (x, y, c), (x, y, 1 - c)
        chips = [(1 - x, y), (x, 1 - y), (1 - x, 1 - y)]

        def slot(a, px, py, pc):
            return out_refs[a].at[4 * px + 2 * py + pc]

        def copy(k, a, block, to, own=False):
            return pltpu.make_async_remote_copy(
                src_ref=x_refs[a] if own else slot(a, *block), dst_ref=slot(a, *block),
                send_sem=send_sems.at[k, a], recv_sem=recv_sems.at[k, a],
                device_id=to, device_id_type=MESH)

        mine = [pltpu.make_async_copy(x_refs[a], slot(a, *me), local_sems.at[a]) for a in range(n)]
        first = []
        for a in range(n):
            first.append(copy(0, a, me, sibling, own=True))
            first += [copy(1 + j, a, me, (*chip, c), own=True) for j, chip in enumerate(chips)]
        return n, c, me, sibling, chips, copy, mine, first

    def start(self, x_refs, out_refs, sems):
        _, _, _, _, _, _, mine, first = self._copies(x_refs, out_refs, sems)
        for cp in mine + first:
            cp.start()

    def relay(self, x_refs, out_refs, sems):
        n, c, me, sibling, chips, copy, _, _ = self._copies(x_refs, out_refs, sems)
        for j, chip in enumerate(chips):
            for a in range(n):
                copy(1 + j, a, (*chip, c), me).wait_recv()
                copy(4 + j, a, (*chip, c), sibling).start()

    def finish(self, x_refs, out_refs, sems):
        n, c, me, sibling, chips, copy, mine, first = self._copies(x_refs, out_refs, sems)
        passed = [copy(4 + j, a, (*chip, c), sibling) for j, chip in enumerate(chips) for a in range(n)]
        for a in range(n):
            copy(0, a, sibling, me).wait_recv()
            for j, chip in enumerate(chips):
                copy(4 + j, a, (*chip, 1 - c), me).wait_recv()
        for cp in first + passed:
            cp.wait_send()
        for cp in mine:
            cp.wait()


class _ScatterRider(_NoRelay):
    def __init__(self, gs):
        self.inputs = list(gs)
        n = len(gs)
        self.out_shape = [jax.ShapeDtypeStruct(g.shape, g.dtype) for g in gs]
        self.scratch = [pltpu.SemaphoreType.DMA((7, n)), pltpu.SemaphoreType.DMA((7, n)),
                        pltpu.SemaphoreType.DMA((n,))]
        self.results = None

    def _copies(self, g_refs, out_refs, sems):
        send_sems, recv_sems, local_sems = sems
        x, y, c = _my_place()
        me = 4 * x + 2 * y + c
        mine, copies = [], []
        for a in range(len(g_refs)):
            mine.append(pltpu.make_async_copy(g_refs[a].at[me], out_refs[a].at[me], local_sems.at[a]))
            for k in range(1, N_DEV):
                px, py, pc = _flip(x, k & 4), _flip(y, k & 2), _flip(c, k & 1)
                copies.append(pltpu.make_async_remote_copy(
                    src_ref=g_refs[a].at[4 * px + 2 * py + pc], dst_ref=out_refs[a].at[me],
                    send_sem=send_sems.at[k - 1, a], recv_sem=recv_sems.at[k - 1, a],
                    device_id=(px, py, pc), device_id_type=MESH))
        return mine, copies

    def start(self, g_refs, out_refs, sems):
        mine, copies = self._copies(g_refs, out_refs, sems)
        for cp in mine + copies:
            cp.start()

    def finish(self, g_refs, out_refs, sems):
        mine, copies = self._copies(g_refs, out_refs, sems)
        for cp in copies + mine:
            cp.wait()


class _SiblingSwapRider(_NoRelay):
    def __init__(self, gs):
        self.inputs = list(gs)
        n = len(gs)
        self.out_shape = [jax.ShapeDtypeStruct((4,) + g.shape[1:], g.dtype) for g in gs]
        self.scratch = [pltpu.SemaphoreType.DMA((4, n)), pltpu.SemaphoreType.DMA((4, n))]
        self.results = None

    def _copies(self, g_refs, out_refs, sems):
        send_sems, recv_sems = sems
        x, y, c = _my_place()
        return [pltpu.make_async_remote_copy(
            src_ref=g_refs[a].at[2 * k + 1 - c], dst_ref=out_refs[a].at[k],
            send_sem=send_sems.at[k, a], recv_sem=recv_sems.at[k, a],
            device_id=(x, y, 1 - c), device_id_type=MESH)
            for a in range(len(g_refs)) for k in range(4)]

    def start(self, g_refs, out_refs, sems):
        for cp in self._copies(g_refs, out_refs, sems):
            cp.start()

    def finish(self, g_refs, out_refs, sems):
        for cp in self._copies(g_refs, out_refs, sems):
            cp.wait()


class _ChipScatterRider(_NoRelay):
    def __init__(self, ps):
        self.inputs = list(ps)
        n = len(ps)
        self.out_shape = [jax.ShapeDtypeStruct(p.shape, p.dtype) for p in ps]
        self.scratch = [pltpu.SemaphoreType.DMA((3, n)), pltpu.SemaphoreType.DMA((3, n)),
                        pltpu.SemaphoreType.DMA((n,))]
        self.results = None

    def _copies(self, p_refs, out_refs, sems):
        send_sems, recv_sems, local_sems = sems
        x, y, c = _my_place()
        my_chip = 2 * x + y
        chips = [(1 - x, y), (x, 1 - y), (1 - x, 1 - y)]
        n = len(p_refs)
        mine = [pltpu.make_async_copy(p_refs[a].at[my_chip], out_refs[a].at[my_chip], local_sems.at[a])
                for a in range(n)]
        copies = [pltpu.make_async_remote_copy(
            src_ref=p_refs[a].at[2 * cx + cy], dst_ref=out_refs[a].at[my_chip],
            send_sem=send_sems.at[j, a], recv_sem=recv_sems.at[j, a],
            device_id=(cx, cy, c), device_id_type=MESH)
            for a in range(n) for j, (cx, cy) in enumerate(chips)]
        return mine, copies

    def start(self, p_refs, out_refs, sems):
        mine, copies = self._copies(p_refs, out_refs, sems)
        for cp in mine + copies:
            cp.start()

    def finish(self, p_refs, out_refs, sems):
        mine, copies = self._copies(p_refs, out_refs, sems)
        for cp in copies + mine:
            cp.wait()


def _call(body, name, grid, in_specs, out_specs, out_shape, scratch, semantics, args, rider=None):
    in_specs, out_specs, out_shape, scratch = list(in_specs), list(out_specs), list(out_shape), list(scratch)
    if rider is None:
        return list(pl.pallas_call(
            body, name=name, grid=grid, in_specs=in_specs, out_specs=out_specs, out_shape=out_shape,
            scratch_shapes=scratch, compiler_params=_params(*semantics))(*args))
    n_in, n_out, n_scr = len(in_specs), len(out_specs), len(scratch)
    r_in, r_out = len(rider.inputs), len(rider.out_shape)

    def wrapped(*refs):
        cuts = np.cumsum([0, n_in, r_in, n_out, r_out, n_scr])
        hi, ri, ho, ro, hs = (refs[cuts[i]:cuts[i + 1]] for i in range(5))
        rs = refs[cuts[5]:]
        step, steps = pl.program_id(0), grid[0]
        for d in range(1, len(grid)):
            step, steps = step * grid[d] + pl.program_id(d), steps * grid[d]

        @pl.when(step == 0)
        def _():
            rider.start(ri, ro, rs)

        body(*hi, *ho, *hs)

        @pl.when(step == (steps * RELAY_AT_NUM) // RELAY_AT_DEN)
        def _():
            rider.relay(ri, ro, rs)

        @pl.when(step == steps - 1)
        def _():
            rider.finish(ri, ro, rs)

    outs = pl.pallas_call(
        wrapped, name=name, grid=grid,
        in_specs=in_specs + [ANY] * r_in, out_specs=out_specs + [ANY] * r_out,
        out_shape=out_shape + rider.out_shape, scratch_shapes=scratch + rider.scratch,
        compiler_params=_params(*(["arbitrary"] * len(grid))),
    )(*args, *rider.inputs)
    rider.results = list(outs[n_out:])
    return list(outs[:n_out])


_WALK = ((None, None), (0, None), (1, 4), (2, 5), (4, None), (5, None), (3, 6), (6, None))


def _gather_order():
    x, y, c = _my_place()
    (ax, ay), (bx, by), (dx, dy) = (1 - x, y), (x, 1 - y), (1 - x, 1 - y)
    ids = [(x, y, c), (x, y, 1 - c), (ax, ay, c), (bx, by, c), (ax, ay, 1 - c), (bx, by, 1 - c),
           (dx, dy, c), (dx, dy, 1 - c)]
    return jnp.stack([4 * px + 2 * py + pc for px, py, pc in ids]).astype(jnp.int32)


def _proj_gather(x, norm_g, w_shard, extras, name):
    t, d = x.shape
    cols = w_shard.shape[1]
    tm = _pick(t, MM_CAP_MN, 16)
    ni = t // tm
    n = 1 + len(extras)
    rider = _GatherRider([w_shard] + list(extras))

    def body(ord_ref, x_ref, g_ref, *refs):
        sh_refs, proj_ref, gathered = refs[:n], refs[n], refs[n + 1:2 * n + 1]
        h_all, bbuf, bsem, send_sems, recv_sems, local_sems = refs[2 * n + 1:]
        j, i = pl.program_id(0), pl.program_id(1)
        _, c, me, sibling, chips, copy, mine, first = rider._copies(
            sh_refs, gathered, (send_sems, recv_sems, local_sems))
        rows = pl.ds(pl.multiple_of(i * tm, tm), tm)

        def load(step, src):
            return pltpu.make_async_copy(src, bbuf.at[step % 2], bsem.at[step % 2])

        def relayed(k, a):
            return copy(k, a, (*chips[k - 4], c), sibling)

        @pl.when(jnp.logical_and(j == 0, i == 0))
        def _():
            for cp in mine + first:
                cp.start()
            load(0, sh_refs[0]).start()

        @pl.when(i == 0)
        def _():
            load(j, sh_refs[0]).wait()

        @pl.when(j == 0)
        def _():
            groups = []
            for r in range(0, tm, NORM_ROWS):
                xv = x_ref[r:r + NORM_ROWS, :]
                rstd = lax.rsqrt(jnp.mean(xv * xv, axis=-1, keepdims=True) + EPS)
                groups.append((xv * rstd * g_ref[...]).astype(BF16))
            h_all[rows, :] = jnp.concatenate(groups, axis=0)

        proj_ref[...] = _dot(h_all[rows, :], bbuf[j % 2])

        for step in range(N_DEV - 1):
            @pl.when(jnp.logical_and(j == step, i == max(ni - 2, 0)))
            def _(step=step):
                need, relay = _WALK[step + 1]
                copy(need, 0, me, me).wait_recv()
                if relay is not None:
                    relayed(relay, 0).start()
                load(step + 1, gathered[0].at[ord_ref[step + 1]]).start()

        @pl.when(jnp.logical_and(j == N_DEV - 1, i == ni - 1))
        def _():
            for a in range(1, n):
                for k in range(3):
                    copy(1 + k, a, me, me).wait_recv()
                    relayed(4 + k, a).start()
            for a in range(1, n):
                for k in (0, 4, 5, 6):
                    copy(k, a, me, me).wait_recv()
            for cp in first + [relayed(4 + k, a) for a in range(n) for k in range(3)]:
                cp.wait_send()
            for cp in mine:
                cp.wait()

    outs = pl.pallas_call(
        body, name=name,
        grid_spec=pltpu.PrefetchScalarGridSpec(
            num_scalar_prefetch=1, grid=(N_DEV, ni),
            in_specs=[pl.BlockSpec((tm, d), lambda j, i, o: (jnp.where(j == 0, i, ni - 1), 0)),
                      pl.BlockSpec((1, d), lambda j, i, o: (0, 0))] + [ANY] * n,
            out_specs=[pl.BlockSpec((tm, cols), lambda j, i, o: (i, o[j]))] + [ANY] * n,
            scratch_shapes=[pltpu.VMEM((t, d), BF16), pltpu.VMEM((2, d, cols), BF16),
                            pltpu.SemaphoreType.DMA((2,))] + rider.scratch),
        out_shape=[jax.ShapeDtypeStruct((t, N_DEV * cols), F32)] + rider.out_shape,
        compiler_params=_params("arbitrary", "arbitrary"),
    )(_gather_order(), x, norm_g, w_shard, *extras)
    return outs[0], list(outs[1:])


def _exchange(rider, name):
    r_in, r_out = len(rider.inputs), len(rider.out_shape)

    def body(*refs):
        ri, ro, rs = refs[:r_in], refs[r_in:r_in + r_out], refs[r_in + r_out:]
        rider.start(ri, ro, rs)
        rider.relay(ri, ro, rs)
        rider.finish(ri, ro, rs)

    return list(pl.pallas_call(
        body, name=name, in_specs=[ANY] * r_in, out_specs=[ANY] * r_out,
        out_shape=rider.out_shape, scratch_shapes=rider.scratch)(*rider.inputs))


MM_CAP_MN = 1024
MM_CAP_M_GRAD = 1408
MM_CAP_N = 1536
MM_CAP_K = 3072
MM_CAP_K_TOKENS = 2048
MM_CAP_K_RMS = 8192
MM_CAP_M_RMS = 512
NORM_ROWS = 256


def _mm(a, b, mode, name, out_dtype=F32, res=None, out_block=None, epilogue=None, extra=None, norm_g=None,
        norm_b=False, rider=None):
    a3, b3 = a.ndim == 3, b.ndim == 3
    um = un = uk = None
    if mode in ("nn", "nt"):
        if a3:
            m, uk = a.shape[1:]
            k = a.shape[0] * uk
        else:
            m, k = a.shape
    else:
        if a3:
            k, um = a.shape[1:]
            m = a.shape[0] * um
        else:
            k, m = a.shape
    if mode in ("nn", "tn"):
        if b3:
            kb, un = b.shape[1:]
            n = b.shape[0] * un
        else:
            kb, n = b.shape
        assert kb == k, (a.shape, b.shape, mode)
    else:
        if b3:
            n, ukb = b.shape[1:]
            assert b.shape[0] * ukb == k and uk in (None, ukb), (a.shape, b.shape, mode)
            uk = ukb
        else:
            n, kb = b.shape
            assert kb == k, (a.shape, b.shape, mode)
    if out_block is not None:
        assert un in (None, out_block)
        un = out_block

    def tile(dim, unit, cap, align):
        if unit is None:
            return _pick(dim, cap, align), 1
        c = max(1, cap // unit)
        while (dim // unit) % c:
            c -= 1
        return unit, c

    cap_m = MM_CAP_M_GRAD if mode == "tn" else (MM_CAP_M_RMS if epilogue == "rms_bwd" else MM_CAP_MN)
    um, cm = tile(m, um, cap_m, 128 if mode == "tn" else 16)
    un, cn = tile(n, un, MM_CAP_N, 128)
    cap_k = MM_CAP_K_TOKENS if mode == "tn" else (MM_CAP_K_RMS if epilogue == "rms_bwd" else MM_CAP_K)
    uk, ck = tile(k, uk, cap_k, 128)
    if epilogue == "rms_bwd":
        assert mode != "tn" and n == D_MODEL and cm == cn == 1 and res is None and out_block is None
    if epilogue == "loss":
        assert n == D_MODEL and cm == cn == 1 and res is not None and out_block is None
    if norm_g is not None and norm_b:
        assert mode == "tn" and not b3 and n == D_MODEL and cn == 1
    elif norm_g is not None:
        assert not a3 and (m if mode == "tn" else k) == D_MODEL and (cm if mode == "tn" else ck) == 1
    if epilogue == "swiglu":
        assert res is None and ((mode == "nn" and b3 and out_block is None) or
                                (mode == "nt" and not b3 and out_block is not None))
        cn = 2
    if epilogue == "swiglu_bwd":
        assert mode == "nt" and out_block is not None and extra is not None and res is None
        cn = 1
    tm, tn, tk = cm * um, cn * un, ck * uk
    nk = k // tk
    dot = {"nn": _dot, "nt": _dot_nt, "tn": _dot_tn}[mode]
    half = n // un // 2
    blocked_out = out_block is not None or epilogue in ("swiglu", "swiglu_bwd")
    extras = [] if extra is None else (list(extra) if isinstance(extra, (tuple, list)) else [extra])

    def sl(idx, unit, count):
        return slice(None) if count == 1 else slice(idx * unit, (idx + 1) * unit)

    def body(*refs):
        a_ref, b_ref = refs[0], refs[1]
        pos = 2
        r_ref = ng_ref = None
        if res is not None:
            r_ref, pos = refs[pos], pos + 1
        e_refs, pos = refs[pos:pos + len(extras)], pos + len(extras)
        if norm_g is not None:
            ng_ref, pos = refs[pos], pos + 1
        outs, acc_ref = refs[pos:-1], refs[-1]
        kk = pl.program_id(2)

        def normed(x_ref):
            groups = []
            for r in range(0, x_ref.shape[0], NORM_ROWS):
                xv = x_ref[r:r + NORM_ROWS, :]
                rstd = lax.rsqrt(jnp.mean(xv * xv, axis=-1, keepdims=True) + EPS)
                groups.append((xv * rstd * ng_ref[...]).astype(BF16))
            return jnp.concatenate(groups, axis=0)

        def a_blk(mi, ki):
            if norm_g is not None and not norm_b:
                return normed(a_ref)
            if mode in ("nn", "nt"):
                return a_ref[ki] if a3 else a_ref[:, sl(ki, uk, ck)]
            return a_ref[mi] if a3 else a_ref[:, sl(mi, um, cm)]

        def b_blk(ki, ni):
            if norm_b:
                return normed(b_ref)
            if epilogue == "swiglu":
                return b_ref[ni, 0]
            if mode in ("nn", "tn"):
                return b_ref[ni] if b3 else b_ref[sl(ki, uk, ck), sl(ni, un, cn)]
            return b_ref[ki][sl(ni, un, cn), :] if b3 else b_ref[sl(ni, un, cn), sl(ki, uk, ck)]

        parts = {}
        for mi in range(cm):
            for ni in range(cn):
                part = None
                for ki in range(ck):
                    d = dot(a_blk(mi, ki).astype(BF16), b_blk(ki, ni).astype(BF16))
                    part = d if part is None else part + d
                parts[mi, ni] = part

        def finish(total):
            if epilogue == "swiglu":
                gate, up = total[0, 0], total[0, 1]
                outs[0][0, 0] = gate.astype(BF16)
                outs[0][1, 0] = up.astype(BF16)
                outs[1][0] = (gate * _sigmoid(gate) * up).astype(BF16)
                return
            if epilogue == "swiglu_bwd":
                dact = total[0, 0]
                gate, up = e_refs[0][0, 0].astype(F32), e_refs[0][1, 0].astype(F32)
                sg = _sigmoid(gate)
                outs[0][0, 0] = (dact * up * (sg * (1.0 + gate * (1.0 - sg)))).astype(BF16)
                outs[0][1, 0] = (dact * (gate * sg)).astype(BF16)
                return
            if epilogue == "rms_bwd":
                x_ref, g_ref, dres_ref = e_refs
                dh, dg = total[0, 0], None
                for r in range(0, tm, NORM_ROWS):
                    rows = slice(r, r + NORM_ROWS)
                    xv, dhv = x_ref[rows, :], dh[rows, :]
                    rstd = lax.rsqrt(jnp.mean(xv * xv, axis=-1, keepdims=True) + EPS)
                    xh = xv * rstd
                    dyg = dhv * g_ref[...]
                    c = jnp.mean(dyg * xh, axis=-1, keepdims=True)
                    outs[0][rows, :] = dres_ref[rows, :] + rstd * (dyg - xh * c)
                    part = jnp.sum(dhv * xh, axis=0, keepdims=True)
                    dg = part if dg is None else dg + part
                _accumulate(outs[1], dg, pl.program_id(0))
                return
            if epilogue == "loss":
                diff = r_ref[...] + total[0, 0] - e_refs[0][...]
                outs[0][...] = diff * (1.0 / n)
                sq = jnp.sum(jnp.sum(diff * diff, axis=-1, keepdims=True), axis=0, keepdims=True)
                _accumulate(outs[1], sq * (0.5 / n), pl.program_id(0))
                return
            for (mi, ni), val in total.items():
                rows, cols = sl(mi, um, cm), sl(ni, un, cn)
                if res is not None:
                    val = r_ref[rows, cols] + val
                if blocked_out:
                    outs[0][ni, rows] = val.astype(out_dtype)
                else:
                    outs[0][rows, cols] = val.astype(out_dtype)

        if nk == 1:
            finish(parts)
        else:
            @pl.when(kk == 0)
            def _():
                for (mi, ni), val in parts.items():
                    acc_ref[mi * cn + ni] = val

            @pl.when(jnp.logical_and(kk > 0, kk < nk - 1))
            def _():
                for (mi, ni), val in parts.items():
                    acc_ref[mi * cn + ni] += val

            @pl.when(kk == nk - 1)
            def _():
                finish({key: acc_ref[key[0] * cn + key[1]] + val for key, val in parts.items()})

    if mode in ("nn", "nt"):
        a_spec = (pl.BlockSpec((ck, tm, uk), lambda i, j, kk: (kk, i, 0)) if a3
                  else pl.BlockSpec((tm, tk), lambda i, j, kk: (i, kk)))
    else:
        a_spec = (pl.BlockSpec((cm, tk, um), lambda i, j, kk: (i, kk, 0)) if a3
                  else pl.BlockSpec((tk, tm), lambda i, j, kk: (kk, i)))
    pair_spec = pl.BlockSpec((2, 1, tm, un), lambda i, j, kk: (0, j, i, 0))
    row_spec = pl.BlockSpec((tm, tn), lambda i, j, kk: (i, 0))
    vec_spec = pl.BlockSpec((1, tn), lambda i, j, kk: (0, 0))
    if epilogue == "swiglu" and mode == "nn":
        b = b.reshape(2, half, k, un)
        b_spec = pl.BlockSpec((2, 1, tk, un), lambda i, j, kk: (0, j, kk, 0))
    elif epilogue == "swiglu":
        b = b.reshape(2, half, un, k)
        b_spec = pl.BlockSpec((2, 1, un, tk), lambda i, j, kk: (0, j, 0, kk))
    elif mode in ("nn", "tn"):
        b_spec = (pl.BlockSpec((cn, tk, un), lambda i, j, kk: (j, kk, 0)) if b3
                  else pl.BlockSpec((tk, tn), lambda i, j, kk: (kk, j)))
    else:
        b_spec = (pl.BlockSpec((ck, tn, uk), lambda i, j, kk: (kk, j, 0)) if b3
                  else pl.BlockSpec((tn, tk), lambda i, j, kk: (j, kk)))
    if epilogue == "swiglu":
        out_specs = [pair_spec, pl.BlockSpec((1, tm, un), lambda i, j, kk: (j, i, 0))]
        out_shape = [jax.ShapeDtypeStruct((2, half, m, un), BF16), jax.ShapeDtypeStruct((half, m, un), BF16)]
    elif epilogue == "swiglu_bwd":
        out_specs = [pair_spec]
        out_shape = [jax.ShapeDtypeStruct(extra.shape, BF16)]
    elif epilogue == "rms_bwd":
        out_specs = [row_spec, vec_spec]
        out_shape = [jax.ShapeDtypeStruct((m, n), F32), jax.ShapeDtypeStruct((1, n), F32)]
    elif epilogue == "loss":
        out_specs = [row_spec, pl.BlockSpec((1, 1), lambda i, j, kk: (0, 0))]
        out_shape = [jax.ShapeDtypeStruct((m, n), F32), jax.ShapeDtypeStruct((1, 1), F32)]
    elif blocked_out:
        out_specs = [pl.BlockSpec((cn, tm, un), lambda i, j, kk: (j, i, 0))]
        out_shape = [jax.ShapeDtypeStruct((n // un, m, un), out_dtype)]
    else:
        out_specs = [pl.BlockSpec((tm, tn), lambda i, j, kk: (i, j))]
        out_shape = [jax.ShapeDtypeStruct((m, n), out_dtype)]
    in_specs, args = [a_spec, b_spec], [a, b]
    if res is not None:
        in_specs.append(pl.BlockSpec((tm, tn), lambda i, j, kk: (i, j)))
        args.append(res)
    if epilogue == "swiglu_bwd":
        in_specs.append(pair_spec)
    elif epilogue == "rms_bwd":
        in_specs += [row_spec, vec_spec, row_spec]
    elif epilogue == "loss":
        in_specs.append(row_spec)
    args += extras
    if norm_g is not None:
        in_specs.append(pl.BlockSpec((1, D_MODEL), lambda i, j, kk: (0, 0)))
        args.append(norm_g)
    semantics = ("arbitrary",) * 3 if epilogue in ("rms_bwd", "loss") else ("parallel", "parallel", "arbitrary")
    out = _call(body, name, (m // tm, n // tn, nk), in_specs, out_specs, out_shape,
                [pltpu.VMEM((cm * cn, um, un), F32)], semantics, args, rider)
    return out if epilogue in ("swiglu", "rms_bwd", "loss") else out[0]


def _head_sums(v, ind):
    hi, lo = _split2(v)
    return _dot(hi, ind) + _dot(lo, ind)


def _head_spread(per_head, ind):
    hi, lo = _split2(per_head)
    return _dot_nt(hi, ind) + _dot_nt(lo, ind)


def _head_rstd(xv, ind):
    return _head_spread(lax.rsqrt(_head_sums(xv * xv, ind) * (1.0 / ATT_DH) + EPS), ind)


def _hn_bwd_math(xv, gv, ind, dyv, scale):
    rstd = _head_rstd(xv, ind)
    xh = xv * rstd
    dyn = dyv * scale
    dyg = dyn * gv
    dx = rstd * (dyg - xh * _head_spread(_head_sums(dyg * xh, ind) * (1.0 / ATT_DH), ind))
    return dx, jnp.sum(dyn * xh, axis=0, keepdims=True)


def _q_hnorm(x, g_tiled, bd, scale, name):
    t, d = x.shape
    tm = _pick(t, 512, 16)

    def body(x_ref, g_ref, bd_ref, o_ref):
        xv = x_ref[...]
        o_ref[...] = (xv * _head_rstd(xv, bd_ref[...]) * g_ref[...] * scale).astype(BF16)

    return pl.pallas_call(
        body, name=name, grid=(t // tm,),
        in_specs=[pl.BlockSpec((tm, d), lambda i: (i, 0)), pl.BlockSpec((1, d), lambda i: (0, 0)),
                  pl.BlockSpec((d, LANES), lambda i: (0, 0))],
        out_specs=pl.BlockSpec((tm, d), lambda i: (i, 0)),
        out_shape=jax.ShapeDtypeStruct((t, d), BF16),
        compiler_params=_params("parallel"),
    )(x, g_tiled, bd)


def _q_dhnorm(x, g_tiled, bd, dy, scale, name):
    t, d = x.shape
    tm = _pick(t, 512, 16)

    def body(x_ref, g_ref, bd_ref, dy_ref, dx_ref, dg_ref):
        dx, part = _hn_bwd_math(x_ref[...], g_ref[...], bd_ref[...], dy_ref[...], scale)
        dx_ref[...] = dx.astype(BF16)
        _accumulate(dg_ref, part, pl.program_id(0))

    row = pl.BlockSpec((tm, d), lambda i: (i, 0))
    vec = pl.BlockSpec((1, d), lambda i: (0, 0))
    return pl.pallas_call(
        body, name=name, grid=(t // tm,),
        in_specs=[row, vec, pl.BlockSpec((d, LANES), lambda i: (0, 0)), row],
        out_specs=[row, vec],
        out_shape=[jax.ShapeDtypeStruct((t, d), BF16), jax.ShapeDtypeStruct((1, d), F32)],
        compiler_params=_params("arbitrary"),
    )(x, g_tiled, bd, dy)


def _kv_prep(kv, g_tiled, bd, name):
    t = kv.shape[0]
    d = D_MODEL
    tm = K_PAD
    assert t % tm == 0

    def body(k_ref, v_ref, g_ref, bd_ref, kp_ref, vp_ref):
        i = pl.program_id(0)

        @pl.when(i == 0)
        def _():
            kp_ref[...] = jnp.zeros_like(kp_ref)
            vp_ref[...] = jnp.zeros_like(vp_ref)

        @pl.when(i > 0)
        def _():
            xv = k_ref[...]
            kp_ref[...] = (xv * _head_rstd(xv, bd_ref[...]) * g_ref[...]).astype(BF16)
            vp_ref[...] = v_ref[...].astype(BF16)

    shp = jax.ShapeDtypeStruct((t + K_PAD, d), BF16)
    out = pl.BlockSpec((tm, d), lambda i: (i, 0))
    return pl.pallas_call(
        body, name=name, grid=(t // tm + 1,),
        in_specs=[pl.BlockSpec((tm, d), lambda i: (jnp.maximum(i - 1, 0), 0)),
                  pl.BlockSpec((tm, d), lambda i: (jnp.maximum(i - 1, 0), 1)),
                  pl.BlockSpec((1, d), lambda i: (0, 0)), pl.BlockSpec((d, LANES), lambda i: (0, 0))],
        out_specs=[out, out], out_shape=[shp, shp],
        compiler_params=_params("arbitrary"),
    )(kv, kv, g_tiled, bd)


def _kv_dprep(kv, g_tiled, bd, dkp_t, dvp_t, name):
    t = kv.shape[0]
    d = D_MODEL
    tm = K_PAD

    def body(k_ref, g_ref, bd_ref, dk_ref, dv_ref, o_ref, dg_ref):
        dx, part = _hn_bwd_math(k_ref[...], g_ref[...], bd_ref[...], dk_ref[...].T, 1.0)
        o_ref[:, :d] = dx.astype(BF16)
        o_ref[:, d:] = dv_ref[...].T.astype(BF16)
        _accumulate(dg_ref, part, pl.program_id(0))

    vec = pl.BlockSpec((1, d), lambda i: (0, 0))
    padded = pl.BlockSpec((d, tm), lambda i: (0, i + 1))
    return pl.pallas_call(
        body, name=name, grid=(t // tm,),
        in_specs=[pl.BlockSpec((tm, d), lambda i: (i, 0)), vec, pl.BlockSpec((d, LANES), lambda i: (0, 0)),
                  padded, padded],
        out_specs=[pl.BlockSpec((tm, 2 * d), lambda i: (i, 0)), vec],
        out_shape=[jax.ShapeDtypeStruct((t, 2 * d), BF16), jax.ShapeDtypeStruct((1, d), F32)],
        compiler_params=_params("arbitrary"),
    )(kv, g_tiled, bd, dkp_t, dvp_t)


def _ret_consts(t):
    h = np.arange(RET_HEADS, dtype=np.float32)
    lg = np.log(np.float32(1.0) - np.float32(2.0) ** (np.float32(-5.0) - h)).astype(np.float32)
    tt = np.arange(CHUNK, dtype=np.float32)
    intra = np.exp(lg[:, None, None] * np.abs(tt[:, None] - tt[None, :])).astype(np.float32)
    q_dec = np.exp(lg[:, None] * (tt + 1.0)).astype(np.float32)
    k_dec = np.exp(lg[:, None] * (CHUNK - 1.0 - tt)).astype(np.float32)
    s_dec = [float(v) for v in np.exp(lg * np.float32(CHUNK)).astype(np.float32)]
    qd = np.broadcast_to(q_dec[:, :, None], (RET_HEADS, CHUNK, RET_DK)).copy()
    kd = np.broadcast_to(k_dec[:, :, None], (RET_HEADS, CHUNK, RET_DK)).copy()
    half = RET_DK // 2
    inv_freq = ROPE_BASE ** (-jnp.arange(half, dtype=F32) / half)
    ang = jnp.arange(t).astype(F32)[:, None] * inv_freq[None, :]
    return jnp.asarray(intra), jnp.asarray(qd), jnp.asarray(kd), s_dec, jnp.cos(ang), jnp.sin(ang)


def _rope(x, cos, sin):
    half = RET_DK // 2
    x1, x2 = x[:, :half], x[:, half:]
    return jnp.concatenate([x1 * cos - x2 * sin, x1 * sin + x2 * cos], axis=-1)


def _unrope(d, cos, sin):
    half = RET_DK // 2
    d1, d2 = d[:, :half], d[:, half:]
    return jnp.concatenate([d1 * cos + d2 * sin, d2 * cos - d1 * sin], axis=-1)


def _ret_slices(h):
    q = slice(h * RET_DK, (h + 1) * RET_DK)
    k = slice(RET_Q_COLS + h * RET_DK, RET_Q_COLS + (h + 1) * RET_DK)
    v = slice(2 * RET_Q_COLS + h * RET_DV, 2 * RET_Q_COLS + (h + 1) * RET_DV)
    g = slice(2 * RET_Q_COLS + RET_V_COLS + h * RET_DV, 2 * RET_Q_COLS + RET_V_COLS + (h + 1) * RET_DV)
    o = slice(h * RET_DV, (h + 1) * RET_DV)
    return q, k, v, g, o


def _ret_fwd(proj, gn, consts, name, rider=None):
    t, cols = proj.shape
    n = t // CHUNK
    intra, qd, kd, s_dec, cos, sin = consts
    k_scale = RET_DK ** -0.5

    def body(p_ref, cos_ref, sin_ref, intra_ref, qd_ref, kd_ref, gn_ref, y_ref, o_ref, st_ref, state):
        i = pl.program_id(0)

        @pl.when(i == 0)
        def _():
            state[...] = jnp.zeros_like(state)

        for c in range(RET_STEP):
            rows = slice(c * CHUNK, (c + 1) * CHUNK)
            cosv, sinv = cos_ref[rows, :], sin_ref[rows, :]
            for h in range(RET_HEADS):
                qs, ks, vs, gs, os_ = _ret_slices(h)
                qr = _rope(p_ref[rows, qs], cosv, sinv)
                kr = _rope(p_ref[rows, ks], cosv, sinv) * k_scale
                vb = p_ref[rows, vs].astype(BF16)
                gv = p_ref[rows, gs]
                scores = _dot_nt(qr.astype(BF16), kr.astype(BF16)) * intra_ref[h]
                s_old = state[h]
                s_old_b = s_old.astype(BF16)
                st_ref[c, h] = s_old_b
                o = _dot(scores.astype(BF16), vb) + _dot((qr * qd_ref[h]).astype(BF16), s_old_b)
                state[h] = s_old * s_dec[h] + _dot_tn((kr * kd_ref[h]).astype(BF16), vb)
                rstd = lax.rsqrt(jnp.mean(o * o, axis=-1, keepdims=True) + EPS)
                on = o * rstd * gn_ref[:, os_]
                o_ref[rows, os_] = o
                y_ref[rows, os_] = (gv * _sigmoid(gv) * on).astype(BF16)

    full3 = lambda a: pl.BlockSpec(a.shape, lambda i: (0, 0, 0))
    step = RET_STEP * CHUNK
    return _call(
        body, name, (n // RET_STEP,),
        [pl.BlockSpec((step, cols), lambda i: (i, 0)),
         pl.BlockSpec((step, RET_DK // 2), lambda i: (i, 0)),
         pl.BlockSpec((step, RET_DK // 2), lambda i: (i, 0)),
         full3(intra), full3(qd), full3(kd),
         pl.BlockSpec((1, RET_V_COLS), lambda i: (0, 0))],
        [pl.BlockSpec((step, RET_V_COLS), lambda i: (i, 0)),
         pl.BlockSpec((step, RET_V_COLS), lambda i: (i, 0)),
         pl.BlockSpec((RET_STEP, RET_HEADS, RET_DK, RET_DV), lambda i: (i, 0, 0, 0))],
        [jax.ShapeDtypeStruct((t, RET_V_COLS), BF16),
         jax.ShapeDtypeStruct((t, RET_V_COLS), F32),
         jax.ShapeDtypeStruct((n, RET_HEADS, RET_DK, RET_DV), BF16)],
        [pltpu.VMEM((RET_HEADS, RET_DK, RET_DV), F32)], ("arbitrary",),
        (proj, cos, sin, intra, qd, kd, gn), rider)


def _ret_bwd(proj, gn, o_saved, states, dy, consts, name, rider=None):
    t, cols = proj.shape
    n = t // CHUNK
    intra, qd, kd, s_dec, cos, sin = consts
    k_scale = RET_DK ** -0.5

    def body(p_ref, cos_ref, sin_ref, intra_ref, qd_ref, kd_ref, gn_ref, o_ref, st_ref, dy_ref,
             dp_ref, dgn_ref, dstate):
        i = pl.program_id(0)

        @pl.when(i == 0)
        def _():
            dstate[...] = jnp.zeros_like(dstate)

        dgn = None
        for c in reversed(range(RET_STEP)):
            rows = slice(c * CHUNK, (c + 1) * CHUNK)
            cosv, sinv = cos_ref[rows, :], sin_ref[rows, :]
            dgn_parts = []
            for h in range(RET_HEADS):
                qs, ks, vs, gs, os_ = _ret_slices(h)
                qr = _rope(p_ref[rows, qs], cosv, sinv)
                kr = _rope(p_ref[rows, ks], cosv, sinv) * k_scale
                qb, kb = qr.astype(BF16), kr.astype(BF16)
                vb = p_ref[rows, vs].astype(BF16)
                gv = p_ref[rows, gs]
                ov = o_ref[rows, os_]
                dyv = dy_ref[rows, os_]
                gnv = gn_ref[:, os_]
                sg = _sigmoid(gv)
                rstd = lax.rsqrt(jnp.mean(ov * ov, axis=-1, keepdims=True) + EPS)
                oh = ov * rstd
                d_on = dyv * (gv * sg)
                dg = dyv * (oh * gnv) * (sg * (1.0 + gv * (1.0 - sg)))
                dgn_parts.append(jnp.sum(d_on * oh, axis=0, keepdims=True))
                d_oh = d_on * gnv
                do = rstd * (d_oh - oh * jnp.mean(d_oh * oh, axis=-1, keepdims=True))
                dob = do.astype(BF16)
                mask = intra_ref[h]
                a_b = (_dot_nt(qb, kb) * mask).astype(BF16)
                da_b = (_dot_nt(dob, vb) * mask).astype(BF16)
                ds_new = dstate[h]
                ds_new_b = ds_new.astype(BF16)
                s_old_b = st_ref[c, h]
                qdv, kdv = qd_ref[h], kd_ref[h]
                dv = _dot_tn(a_b, dob) + _dot((kr * kdv).astype(BF16), ds_new_b)
                dqr = _dot(da_b, kb) + _dot_nt(dob, s_old_b) * qdv
                dkr = _dot_tn(da_b, qb) + _dot_nt(vb, ds_new_b) * kdv
                dstate[h] = ds_new * s_dec[h] + _dot_tn((qr * qdv).astype(BF16), dob)
                dp_ref[rows, qs] = _unrope(dqr, cosv, sinv).astype(BF16)
                dp_ref[rows, ks] = _unrope(dkr * k_scale, cosv, sinv).astype(BF16)
                dp_ref[rows, vs] = dv.astype(BF16)
                dp_ref[rows, gs] = dg.astype(BF16)
            part = jnp.concatenate(dgn_parts, axis=-1)
            dgn = part if dgn is None else dgn + part
        _accumulate(dgn_ref, dgn, i)

    steps = n // RET_STEP
    step = RET_STEP * CHUNK
    rev = lambda i: (steps - 1 - i, 0)
    full3 = lambda a: pl.BlockSpec(a.shape, lambda i: (0, 0, 0))
    return _call(
        body, name, (steps,),
        [pl.BlockSpec((step, cols), rev),
         pl.BlockSpec((step, RET_DK // 2), rev),
         pl.BlockSpec((step, RET_DK // 2), rev),
         full3(intra), full3(qd), full3(kd),
         pl.BlockSpec((1, RET_V_COLS), lambda i: (0, 0)),
         pl.BlockSpec((step, RET_V_COLS), rev),
         pl.BlockSpec((RET_STEP, RET_HEADS, RET_DK, RET_DV), lambda i: (steps - 1 - i, 0, 0, 0)),
         pl.BlockSpec((step, RET_V_COLS), rev)],
        [pl.BlockSpec((step, cols), rev),
         pl.BlockSpec((1, RET_V_COLS), lambda i: (0, 0))],
        [jax.ShapeDtypeStruct((t, cols), BF16),
         jax.ShapeDtypeStruct((1, RET_V_COLS), F32)],
        [pltpu.VMEM((RET_HEADS, RET_DK, RET_DV), F32)], ("arbitrary",),
        (proj, cos, sin, intra, qd, kd, gn, o_saved, states, dy), rider)


def _att_common(q_ref, kp_ref, vp_ref, sub):
    blk = pl.program_id(1) * ATT_SUBS + sub
    start = pl.multiple_of(blk * Q_BLOCK, Q_BLOCK)
    kw = kp_ref[pl.ds(start, K_WINDOW), :]
    vw = vp_ref[pl.ds(start, K_WINDOW), :]
    kvalid = blk * Q_BLOCK - K_PAD + lax.broadcasted_iota(jnp.int32, (1, K_WINDOW), 1) >= 0
    lane = lax.broadcasted_iota(jnp.int32, (1, LANES), 1)
    qrows = slice(sub * Q_BLOCK, (sub + 1) * Q_BLOCK)
    return start, qrows, q_ref[qrows, :], kw, vw, kvalid, (lane < ATT_DH, lane >= ATT_DH)


def _row_groups():
    return [slice(r * ATT_ROWS, (r + 1) * ATT_ROWS) for r in range(Q_BLOCK // ATT_ROWS)]


def _lane_copies(x):
    return jnp.tile(x, (1, K_WINDOW // LANES))


def _att_specs(t, tp):
    qspec = pl.BlockSpec((ATT_SUBS * Q_BLOCK, LANES), lambda h, i: (i, h))
    kspec = pl.BlockSpec((tp, LANES), lambda h, i: (0, h))
    bspec = pl.BlockSpec((2, Q_BLOCK, K_WINDOW), lambda h, i: (h, 0, 0))
    return qspec, kspec, bspec


def _att_fwd(q, kp, vp, bias, name, rider=None):
    t, d = q.shape
    tp = kp.shape[0]

    def body(q_ref, kp_ref, vp_ref, bias_ref, o_ref, lse_ref, s_scr, p_scr, lse_scr):
        for sub in range(ATT_SUBS):
            _, qrows, q2, kw, vw, kvalid, sel = _att_common(q_ref, kp_ref, vp_ref, sub)
            for hh in range(2):
                s_scr[sub, hh] = _dot_nt(jnp.where(sel[hh], q2, 0), kw)
            for hh in range(2):
                for rows in _row_groups():
                    s = jnp.where(kvalid, s_scr[sub, hh, rows, :] + bias_ref[hh, rows, :], NEG)
                    m = jnp.max(s, axis=-1, keepdims=True)
                    e = jnp.exp(s - m)
                    l = jnp.sum(e, axis=-1, keepdims=True)
                    p_scr[sub, hh, rows, :] = (e * (1.0 / l)).astype(BF16)
                    lse_scr[sub, hh, rows, :] = jnp.broadcast_to(m + jnp.log(l), (ATT_ROWS, LANES))
            outs = [_dot(p_scr[sub, hh], vw) for hh in range(2)]
            o_ref[qrows, :] = jnp.where(sel[0], outs[0], outs[1]).astype(BF16)
            lse_ref[qrows, :] = jnp.where(sel[0], lse_scr[sub, 0], lse_scr[sub, 1])

    qspec, kspec, bspec = _att_specs(t, tp)
    return _call(body, name, (d // LANES, t // (ATT_SUBS * Q_BLOCK)), [qspec, kspec, kspec, bspec], [qspec, qspec],
                 [jax.ShapeDtypeStruct((t, d), BF16), jax.ShapeDtypeStruct((t, d), F32)],
                 [pltpu.VMEM((ATT_SUBS, 2, Q_BLOCK, K_WINDOW), F32),
                  pltpu.VMEM((ATT_SUBS, 2, Q_BLOCK, K_WINDOW), BF16),
                  pltpu.VMEM((ATT_SUBS, 2, Q_BLOCK, LANES), F32)],
                 ("parallel", "arbitrary"), (q, kp, vp, bias), rider)


def _att_bwd(q, kp, vp, bias, do, o, lse, name, rider=None):
    t, d = q.shape
    tp = kp.shape[0]

    def body(q_ref, kp_ref, vp_ref, bias_ref, do_ref, o_ref, lse_ref, dq_ref, dkp_ref, dvp_ref, db_ref,
             s_scr, dp_scr, p_scr, ds_scr, row_scr):
        @pl.when(pl.program_id(1) == 0)
        def _():
            dkp_ref[...] = jnp.zeros_like(dkp_ref)
            dvp_ref[...] = jnp.zeros_like(dvp_ref)
            db_ref[...] = jnp.zeros_like(db_ref)

        for sub in range(ATT_SUBS):
            start, qrows, q2, kw, vw, kvalid, sel = _att_common(q_ref, kp_ref, vp_ref, sub)
            do2 = do_ref[qrows, :]
            qm = [jnp.where(sel[hh], q2, 0) for hh in range(2)]
            dom = [jnp.where(sel[hh], do2, 0) for hh in range(2)]
            do_o = do2.astype(F32) * o_ref[qrows, :].astype(F32)
            lse2 = lse_ref[qrows, :]
            for hh in range(2):
                s_scr[sub, hh] = _dot_nt(qm[hh], kw)
                dp_scr[sub, hh] = _dot_nt(dom[hh], vw)
                lse_h = jnp.max(jnp.where(sel[hh], lse2, NEG), axis=-1, keepdims=True)
                delta = jnp.sum(jnp.where(sel[hh], do_o, 0.0), axis=-1, keepdims=True)
                row_scr[sub, hh, 0] = jnp.broadcast_to(lse_h, (Q_BLOCK, LANES))
                row_scr[sub, hh, 1] = jnp.broadcast_to(delta, (Q_BLOCK, LANES))
            for hh in range(2):
                for rows in _row_groups():
                    s = jnp.where(kvalid, s_scr[sub, hh, rows, :] + bias_ref[hh, rows, :], NEG)
                    p = jnp.exp(s - _lane_copies(row_scr[sub, hh, 0, rows, :]))
                    ds = p * (dp_scr[sub, hh, rows, :] - _lane_copies(row_scr[sub, hh, 1, rows, :]))
                    db_ref[hh, rows, :] += ds
                    p_scr[sub, hh, rows, :] = p.astype(BF16)
                    ds_scr[sub, hh, rows, :] = ds.astype(BF16)
            dqs = [_dot(ds_scr[sub, hh], kw) for hh in range(2)]
            dq_ref[qrows, :] = jnp.where(sel[0], dqs[0], dqs[1])
            dkp_ref[:, pl.ds(start, K_WINDOW)] += (_dot_tn(qm[0], ds_scr[sub, 0]) +
                                                   _dot_tn(qm[1], ds_scr[sub, 1]))
            dvp_ref[:, pl.ds(start, K_WINDOW)] += (_dot_tn(dom[0], p_scr[sub, 0]) +
                                                   _dot_tn(dom[1], p_scr[sub, 1]))

    qspec, kspec, bspec = _att_specs(t, tp)
    tspec = pl.BlockSpec((LANES, tp), lambda h, i: (h, 0))
    stage = lambda dt: pltpu.VMEM((ATT_SUBS, 2, Q_BLOCK, K_WINDOW), dt)
    return _call(body, name, (d // LANES, t // (ATT_SUBS * Q_BLOCK)),
                 [qspec, kspec, kspec, bspec, qspec, qspec, qspec],
                 [qspec, tspec, tspec, bspec],
                 [jax.ShapeDtypeStruct((t, d), F32),
                  jax.ShapeDtypeStruct((d, tp), F32),
                  jax.ShapeDtypeStruct((d, tp), F32),
                  jax.ShapeDtypeStruct((ATT_HEADS, Q_BLOCK, K_WINDOW), F32)],
                 [stage(F32), stage(F32), stage(BF16), stage(BF16),
                  pltpu.VMEM((ATT_SUBS, 2, 2, Q_BLOCK, LANES), F32)],
                 ("parallel", "arbitrary"), (q, kp, vp, bias, do, o, lse), rider)


def _rel_bin_matrix():
    rows = REL_DELTAS * 2 * REL_BLK
    rho = lax.broadcasted_iota(jnp.int32, (rows, REL_PAD), 0)
    col = lax.broadcasted_iota(jnp.int32, (rows, REL_PAD), 1)
    assert 2 * REL_BLK == 256
    delta = rho >> 8
    c = 255 - (rho & 255)
    dist = K_PAD + REL_BLK * (delta - (K_WINDOW // REL_BLK - 1)) + (c - (REL_BLK - 1))
    idx = jnp.clip(dist, -REL_CLIP, REL_CLIP) + REL_CLIP
    return col == idx


def _rel_expand(rel_pad, name):
    heads = rel_pad.shape[0]
    rows = REL_DELTAS * 2 * REL_BLK

    def body_bin(r_ref, o_ref):
        onehot = jnp.where(_rel_bin_matrix(), 1.0, 0.0).astype(BF16)
        hi, mid, lo = _split3(r_ref[...])
        o_ref[...] = _dot_nt(hi, onehot) + _dot_nt(mid, onehot) + _dot_nt(lo, onehot)

    by_delta = pl.pallas_call(
        body_bin, name=name + "_bin",
        out_shape=jax.ShapeDtypeStruct((heads, rows), F32),
        compiler_params=pltpu.CompilerParams(vmem_limit_bytes=VMEM_LIMIT_V7X),
    )(rel_pad)
    by_delta = by_delta.reshape(heads * REL_DELTAS, 2 * REL_BLK)

    def body_shift(t_ref, o_ref):
        tv = t_ref[...]
        for r in range(REL_BLK):
            o_ref[r] = pltpu.roll(tv, (r + REL_BLK) % (2 * REL_BLK), 1)[:, :REL_BLK]

    return pl.pallas_call(
        body_shift, name=name + "_shift",
        out_shape=jax.ShapeDtypeStruct((REL_BLK, heads * REL_DELTAS, REL_BLK), F32),
        compiler_params=pltpu.CompilerParams(vmem_limit_bytes=VMEM_LIMIT_V7X),
    )(by_delta)


def _bias_table(rel_bias, name):
    heads = rel_bias.shape[0]
    rel_pad = jnp.pad(rel_bias, ((0, 0), (0, REL_PAD - REL_TABLE)))
    tiles = _rel_expand(rel_pad, name)
    tiles = tiles.reshape(REL_BLK, heads, REL_DELTAS, REL_BLK).transpose(1, 2, 0, 3)
    na, nb = Q_BLOCK // REL_BLK, K_WINDOW // REL_BLK
    rows = [jnp.concatenate([tiles[:, a - b + nb - 1] for b in range(nb)], axis=-1) for a in range(na)]
    table = jnp.concatenate(rows, axis=-2)
    qc = np.arange(Q_BLOCK)[:, None] // CHUNK
    kc = np.arange(K_WINDOW)[None, :] // CHUNK
    band = (kc >= qc) & (kc <= qc + PAST_CHUNKS)
    return jnp.where(jnp.asarray(band)[None], table, NEG)


def _rel_reduce(db, name):
    heads = db.shape[0]
    na, nb = Q_BLOCK // REL_BLK, K_WINDOW // REL_BLK

    fold_heads = 4

    def body_fold(db_ref, g_ref):
        for hd in range(fold_heads):
            for delta in range(REL_DELTAS):
                acc = None
                for a in range(na):
                    b = a - (delta - (nb - 1))
                    if 0 <= b < nb:
                        tile = db_ref[hd, a * REL_BLK:(a + 1) * REL_BLK, b * REL_BLK:(b + 1) * REL_BLK]
                        acc = tile if acc is None else acc + tile
                g_ref[hd, delta] = acc

    folded = pl.pallas_call(
        body_fold, name=name + "_fold", grid=(heads // fold_heads,),
        in_specs=[pl.BlockSpec((fold_heads, Q_BLOCK, K_WINDOW), lambda h: (h, 0, 0))],
        out_specs=pl.BlockSpec((fold_heads, REL_DELTAS, REL_BLK, REL_BLK), lambda h: (h, 0, 0, 0)),
        out_shape=jax.ShapeDtypeStruct((heads, REL_DELTAS, REL_BLK, REL_BLK), F32),
        compiler_params=_params("parallel"),
    )(db)
    by_row = folded.transpose(2, 0, 1, 3).reshape(REL_BLK, heads * REL_DELTAS, REL_BLK)

    def body_diag(g_ref, d_ref):
        zeros = jnp.zeros((heads * REL_DELTAS, REL_BLK), F32)
        acc = None
        for r in range(REL_BLK):
            part = pltpu.roll(jnp.concatenate([g_ref[r], zeros], axis=1), REL_BLK - r, 1)
            acc = part if acc is None else acc + part
        d_ref[...] = acc

    diag = pl.pallas_call(
        body_diag, name=name + "_diag",
        out_shape=jax.ShapeDtypeStruct((heads * REL_DELTAS, 2 * REL_BLK), F32),
        compiler_params=pltpu.CompilerParams(vmem_limit_bytes=VMEM_LIMIT_V7X),
    )(by_row)
    diag = diag.reshape(heads, REL_DELTAS * 2 * REL_BLK)

    def body_bin(d_ref, o_ref):
        onehot = jnp.where(_rel_bin_matrix(), 1.0, 0.0).astype(BF16)
        hi, mid, lo = _split3(d_ref[...])
        o_ref[...] = _dot(hi, onehot) + _dot(mid, onehot) + _dot(lo, onehot)

    out = pl.pallas_call(
        body_bin, name=name + "_bin",
        out_shape=jax.ShapeDtypeStruct((heads, REL_PAD), F32),
        compiler_params=pltpu.CompilerParams(vmem_limit_bytes=VMEM_LIMIT_V7X),
    )(diag)
    return out[:, :REL_TABLE]


def _sum_leading(x, name):
    n, r, c = x.shape
    tr = _pick(r, 256, 8)

    def body(x_ref, o_ref):
        acc = x_ref[0].astype(F32)
        for k in range(1, n):
            acc = acc + x_ref[k].astype(F32)
        o_ref[...] = acc

    return pl.pallas_call(
        body, name=name, grid=(r // tr,),
        in_specs=[pl.BlockSpec((n, tr, c), lambda i: (0, i, 0))],
        out_specs=pl.BlockSpec((tr, c), lambda i: (i, 0)),
        out_shape=jax.ShapeDtypeStruct((r, c), F32),
        compiler_params=_params("parallel"),
    )(x)


def _pair_add(g, recv, parity, name):
    _, r, c = g.shape
    tr = _pick(r, 256, 16)

    def body(par_ref, g_ref, r_ref, o_ref):
        o_ref[...] = (g_ref[...].astype(F32) + r_ref[...].astype(F32)).astype(BF16)

    return pl.pallas_call(
        body, name=name,
        grid_spec=pltpu.PrefetchScalarGridSpec(
            num_scalar_prefetch=1, grid=(4, r // tr),
            in_specs=[pl.BlockSpec((1, tr, c), lambda k, i, par: (2 * k + par[0], i, 0)),
                      pl.BlockSpec((1, tr, c), lambda k, i, par: (k, i, 0))],
            out_specs=pl.BlockSpec((1, tr, c), lambda k, i, par: (k, i, 0))),
        out_shape=jax.ShapeDtypeStruct((4, r, c), BF16),
        compiler_params=_params("parallel", "parallel"),
    )(parity, g, recv)


def _adamw(w, g_parts, m, v, name):
    r, c = w.shape
    n = g_parts.shape[0]
    tr = _pick(r, 256, 16 if g_parts.dtype == BF16 else 8)
    c1 = 1.0 - ADAM_B1 ** ADAM_STEP
    c2 = 1.0 - ADAM_B2 ** ADAM_STEP

    def body(w_ref, g_ref, m_ref, v_ref, go_ref, d_ref, nm_ref, nv_ref):
        gv = g_ref[0].astype(F32)
        for k in range(1, n):
            gv = gv + g_ref[k].astype(F32)
        nm = ADAM_B1 * m_ref[...] + (1.0 - ADAM_B1) * gv
        nv = ADAM_B2 * v_ref[...] + (1.0 - ADAM_B2) * (gv * gv)
        go_ref[...] = gv
        d_ref[...] = -ADAM_LR * ((nm / c1) / (jnp.sqrt(nv / c2) + ADAM_EPS) + ADAM_WD * w_ref[...])
        nm_ref[...] = nm
        nv_ref[...] = nv

    spec = pl.BlockSpec((tr, c), lambda i: (i, 0))
    shp = jax.ShapeDtypeStruct((r, c), F32)
    return pl.pallas_call(
        body, name=name, grid=(r // tr,),
        in_specs=[spec, pl.BlockSpec((n, tr, c), lambda i: (0, i, 0)), spec, spec],
        out_specs=[spec] * 4, out_shape=[shp] * 4,
        compiler_params=_params("parallel"),
    )(w, g_parts, m, v)


BIG = (("a_w_in", 1), ("a_w_o", 0), ("a_w_gu", 0), ("a_w_down", 0), ("w_kv", 1),
       ("b_w_q", 0), ("b_w_o", 0), ("b_w_gu", 0), ("b_w_down", 0))
TRANSPOSED = ("a_w_gu", "b_w_gu")
FFN_BLK = 2 * FFN_HIDDEN // N_DEV

SMALL = (("a_norm_g", D_MODEL, True), ("a_gn_g", RET_V_COLS, True), ("a_ffn_norm_g", D_MODEL, True),
         ("kv_norm_g", D_MODEL, False), ("b_norm_g", D_MODEL, False), ("b_ffn_norm_g", D_MODEL, False),
         ("k_norm_g", ATT_DH, False), ("b_q_norm_g", ATT_DH, False),
         ("b_rel_bias", ATT_HEADS * REL_TABLE, False))
SMALL_ROWS, SMALL_COLS = 16, 1024


def _pack_small(vals, last=None):
    flat = jnp.concatenate([vals[n].reshape(-1) for n, _, _ in SMALL])
    room = SMALL_ROWS * SMALL_COLS - flat.shape[0]
    if last is None:
        flat = jnp.pad(flat, (0, room))
    else:
        flat = jnp.concatenate([jnp.pad(flat, (0, room - 1)), last.reshape(1)])
    return flat.reshape(SMALL_ROWS, SMALL_COLS)


def _unpack_small(packed, local):
    flat, out, pos = packed.reshape(-1), {}, 0
    for n, length, sharded in SMALL:
        ln = length // N_DEV if (local and sharded) else length
        out[n] = flat[pos:pos + ln]
        pos += ln
    return out


def _gather_rider(shards, names):
    return _GatherRider([shards[n] for n in names])


def _gathered(rider, names, axis_of):
    return {n: (r.reshape(-1, r.shape[2]) if axis_of[n] == 0 else r) for n, r in zip(names, rider.results)}


def _blocks(g):
    return g if g.ndim == 3 else g.reshape(N_DEV, -1, g.shape[-1])


def _local_step(x, target, shards, s, parity):
    t = x.shape[0]
    axis_of = dict(BIG)
    consts = _ret_consts(t)
    lane_to_head = np.zeros((D_MODEL, LANES), np.float32)
    lane_to_head[np.arange(D_MODEL), np.arange(D_MODEL) // ATT_DH] = 1.0
    bd = jnp.asarray(lane_to_head).astype(BF16)
    kg_t = jnp.tile(s["k_norm_g"], (1, ATT_HEADS))
    qg_t = jnp.tile(s["b_q_norm_g"], (1, ATT_HEADS))
    q_scale = ATT_DH ** -0.5
    w, g, recv = {}, {}, {}

    def gather_on(names):
        return _gather_rider(shards, names), names

    def landed(ride):
        w.update(_gathered(ride[0], ride[1], axis_of))

    def scatter_on(names):
        return _ScatterRider([_blocks(g[n]) for n in names]), names

    def reduced(ride):
        recv.update(zip(ride[1], ride[0].results))

    proj, (w["a_w_in"], w_o) = _proj_gather(x, s["a_norm_g"], shards["a_w_in"], [shards["a_w_o"]], "a_proj")
    w["a_w_o"] = w_o.reshape(-1, w_o.shape[2])
    ride = gather_on(["a_w_gu"])
    y, o_ret, states = _ret_fwd(proj, s["a_gn_g"], consts, "a_ret", rider=ride[0])
    landed(ride)
    ride = gather_on(["w_kv"])
    x1 = _mm(y, w["a_w_o"], "nn", "a_out", res=x, rider=ride[0])
    landed(ride)
    ride = gather_on(["a_w_down", "b_w_q", "b_w_o"])
    gu_a, act_a = _mm(x1, w["a_w_gu"], "nt", "a_ffn_gu", epilogue="swiglu", out_block=FFN_BLK,
                      norm_g=s["a_ffn_norm_g"], rider=ride[0])
    landed(ride)
    x2 = _mm(act_a, w["a_w_down"], "nn", "a_ffn_down", res=x1)

    kv = _mm(x2, w["w_kv"], "nn", "kv_proj", norm_g=s["kv_norm_g"])
    kp, vp = _kv_prep(kv, kg_t, bd, "kv_prep")

    q_raw = _mm(x2, w["b_w_q"], "nn", "b_q", norm_g=s["b_norm_g"])
    qn = _q_hnorm(q_raw, qg_t, bd, q_scale, "q_hnorm")
    bias = _bias_table(s["b_rel_bias"].reshape(ATT_HEADS, REL_TABLE), "rel")
    ride = gather_on(["b_w_gu", "b_w_down"])
    o_att, lse = _att_fwd(qn, kp, vp, bias, "b_att", rider=ride[0])
    landed(ride)
    x3 = _mm(o_att, w["b_w_o"], "nn", "b_out", res=x2)
    gu_b, act_b = _mm(x3, w["b_w_gu"], "nt", "b_ffn_gu", epilogue="swiglu", out_block=FFN_BLK,
                      norm_g=s["b_ffn_norm_g"])
    dy, loss = _mm(act_b, w["b_w_down"], "nn", "b_ffn_down", res=x3, epilogue="loss", extra=(target,))
    in_blk, kv_blk, ffn_blk = w["a_w_in"].shape[2], w["w_kv"].shape[2], FFN_BLK

    dgu = _mm(dy, w["b_w_down"], "nt", "b_ffn_dgu", out_block=ffn_blk, epilogue="swiglu_bwd", extra=gu_b)
    dgu = dgu.reshape(N_DEV, t, ffn_blk)
    g["b_w_down"] = _mm(act_b, dy, "tn", "b_ffn_gdown", out_dtype=BF16)
    ride = scatter_on(["b_w_down"])
    dx3, g["b_ffn_norm_g"] = _mm(dgu, w["b_w_gu"], "nn", "b_ffn_dh", epilogue="rms_bwd",
                                 extra=(x3, s["b_ffn_norm_g"], dy), rider=ride[0])
    reduced(ride)
    g["b_w_gu"] = _mm(dgu, x3, "tn", "b_ffn_ggu", out_dtype=BF16, norm_g=s["b_ffn_norm_g"], norm_b=True)

    do_att = _mm(dx3, w["b_w_o"], "nt", "b_dout", out_dtype=BF16)
    g["b_w_o"] = _mm(o_att, dx3, "tn", "b_gout", out_dtype=BF16)
    ride = scatter_on(["b_w_gu", "b_w_o"])
    dq, dkp, dvp, db = _att_bwd(qn, kp, vp, bias, do_att, o_att, lse, "b_datt", rider=ride[0])
    reduced(ride)
    g["b_rel_bias"] = _rel_reduce(db, "drel").reshape(1, -1)
    dq_raw, gq = _q_dhnorm(q_raw, qg_t, bd, dq, q_scale, "q_dhnorm")
    g["b_q_norm_g"] = gq.reshape(ATT_HEADS, ATT_DH).sum(axis=0, keepdims=True)
    g["b_w_q"] = _mm(x2, dq_raw, "tn", "b_gq", out_dtype=BF16, norm_g=s["b_norm_g"])
    dx2, g["b_norm_g"] = _mm(dq_raw, w["b_w_q"], "nt", "b_dq", epilogue="rms_bwd",
                             extra=(x2, s["b_norm_g"], dx3))

    dkv, gk = _kv_dprep(kv, kg_t, bd, dkp, dvp, "kv_dprep")
    g["k_norm_g"] = gk.reshape(ATT_HEADS, ATT_DH).sum(axis=0, keepdims=True)
    g["w_kv"] = _mm(x2, dkv, "tn", "kv_g", out_dtype=BF16, out_block=kv_blk, norm_g=s["kv_norm_g"])
    dx2, g["kv_norm_g"] = _mm(dkv, w["w_kv"], "nt", "kv_du", epilogue="rms_bwd",
                              extra=(x2, s["kv_norm_g"], dx2))

    ride = scatter_on(["b_w_q"])
    dgu = _mm(dx2, w["a_w_down"], "nt", "a_ffn_dgu", out_block=ffn_blk, epilogue="swiglu_bwd", extra=gu_a,
              rider=ride[0])
    reduced(ride)
    dgu = dgu.reshape(N_DEV, t, ffn_blk)
    g["a_w_down"] = _mm(act_a, dx2, "tn", "a_ffn_gdown", out_dtype=BF16)
    ride = scatter_on(["a_w_down"])
    dx1, g["a_ffn_norm_g"] = _mm(dgu, w["a_w_gu"], "nn", "a_ffn_dh", epilogue="rms_bwd",
                                 extra=(x1, s["a_ffn_norm_g"], dx2), rider=ride[0])
    reduced(ride)
    ride = scatter_on(["w_kv"])
    g["a_w_gu"] = _mm(dgu, x1, "tn", "a_ffn_ggu", out_dtype=BF16, norm_g=s["a_ffn_norm_g"], norm_b=True,
                      rider=ride[0])
    reduced(ride)

    swap = _SiblingSwapRider([_blocks(g["a_w_gu"])])
    dy_ret = _mm(dx1, w["a_w_o"], "nt", "a_dout", rider=swap)
    g["a_w_o"] = _mm(y, dx1, "tn", "a_gout", out_dtype=BF16)
    chips = _ChipScatterRider([_pair_add(_blocks(g["a_w_gu"]), swap.results[0], parity, "rs_pair_add_gu")])
    dproj, g["a_gn_g"] = _ret_bwd(proj, s["a_gn_g"], o_ret, states, dy_ret, consts, "a_dret", rider=chips)
    recv["a_w_gu"] = chips.results[0]
    ride = scatter_on(["a_w_o"])
    g["a_w_in"] = _mm(x, dproj, "tn", "a_gin", out_dtype=BF16, out_block=in_blk, norm_g=s["a_norm_g"],
                      rider=ride[0])
    reduced(ride)
    from_sibling = _exchange(_SiblingSwapRider([g["a_w_in"]]), "rs_sibling")[0]
    chip_sums = _pair_add(g["a_w_in"], from_sibling, parity, "rs_pair_add")
    last = _ChipScatterRider([chip_sums])
    grad_x, g["a_norm_g"] = _mm(dproj, w["a_w_in"], "nt", "a_dproj", epilogue="rms_bwd",
                                extra=(x, s["a_norm_g"], dx1), rider=last)
    recv["a_w_in"] = last.results[0]
    return loss, grad_x, recv, g


ARG_NAMES = ("x", "a_norm_g", "a_w_in", "a_gn_g", "a_w_o", "a_ffn_norm_g", "a_w_gu", "a_w_down",
             "kv_norm_g", "w_kv", "k_norm_g", "b_norm_g", "b_w_q", "b_q_norm_g", "b_rel_bias", "b_w_o",
             "b_ffn_norm_g", "b_w_gu", "b_w_down")
WEIGHT_NAMES = ARG_NAMES[1:]


def _big_shard(a, name):
    a = a[0] if a.ndim == 3 else a
    return a.T if name in TRANSPOSED else a


def _as_given(a, name, shape):
    return (a.T if name in TRANSPOSED else a).reshape(shape)


def kernel(x, a_norm_g, a_w_in, a_gn_g, a_w_o, a_ffn_norm_g, a_w_gu, a_w_down, kv_norm_g, w_kv, k_norm_g, b_norm_g, b_w_q, b_q_norm_g, b_rel_bias, b_w_o, b_ffn_norm_g, b_w_gu, b_w_down, loss_target, m_a_norm_g, m_a_w_in, m_a_gn_g, m_a_w_o, m_a_ffn_norm_g, m_a_w_gu, m_a_w_down, m_kv_norm_g, m_w_kv, m_k_norm_g, m_b_norm_g, m_b_w_q, m_b_q_norm_g, m_b_rel_bias, m_b_w_o, m_b_ffn_norm_g, m_b_w_gu, m_b_w_down, v_a_norm_g, v_a_w_in, v_a_gn_g, v_a_w_o, v_a_ffn_norm_g, v_a_w_gu, v_a_w_down, v_kv_norm_g, v_w_kv, v_k_norm_g, v_b_norm_g, v_b_w_q, v_b_q_norm_g, v_b_rel_bias, v_b_w_o, v_b_ffn_norm_g, v_b_w_gu, v_b_w_down):
    args = (x, a_norm_g, a_w_in, a_gn_g, a_w_o, a_ffn_norm_g, a_w_gu, a_w_down, kv_norm_g, w_kv, k_norm_g,
            b_norm_g, b_w_q, b_q_norm_g, b_rel_bias, b_w_o, b_ffn_norm_g, b_w_gu, b_w_down)
    p = dict(zip(ARG_NAMES, args))
    m_all = dict(zip(WEIGHT_NAMES, (m_a_norm_g, m_a_w_in, m_a_gn_g, m_a_w_o, m_a_ffn_norm_g, m_a_w_gu,
                                    m_a_w_down, m_kv_norm_g, m_w_kv, m_k_norm_g, m_b_norm_g, m_b_w_q,
                                    m_b_q_norm_g, m_b_rel_bias, m_b_w_o, m_b_ffn_norm_g, m_b_w_gu, m_b_w_down)))
    v_all = dict(zip(WEIGHT_NAMES, (v_a_norm_g, v_a_w_in, v_a_gn_g, v_a_w_o, v_a_ffn_norm_g, v_a_w_gu,
                                    v_a_w_down, v_kv_norm_g, v_w_kv, v_k_norm_g, v_b_norm_g, v_b_w_q,
                                    v_b_q_norm_g, v_b_rel_bias, v_b_w_o, v_b_ffn_norm_g, v_b_w_gu, v_b_w_down)))
    xi, yi, ci = _my_place()
    me = 4 * xi + 2 * yi + ci
    big_names = [n for n, _ in BIG]

    big_local = {n: _big_shard(p[n], n) for n in big_names}
    shards = {n: a.astype(BF16) for n, a in big_local.items()}
    small_local = _pack_small({n: p[n] for n, _, _ in SMALL})
    small_all = _exchange(_GatherRider([small_local]), "gather_small")[0]
    flat_g = small_all.reshape(N_DEV, -1)
    s_full, pos = {}, 0
    for n, length, sharded in SMALL:
        ln = length // N_DEV if sharded else length
        s_full[n] = flat_g[:, pos:pos + ln].reshape(1, -1) if sharded else p[n].reshape(1, -1)
        pos += ln

    parity = jnp.reshape(ci, (1,)).astype(jnp.int32)
    loss, grad_x, recv, g = _local_step(x[0], loss_target[0], shards, s_full, parity)

    partial = _pack_small({n: g[n] for n, _, _ in SMALL}, last=loss)
    summed = _sum_leading(_exchange(_GatherRider([partial]), "gather_gsmall")[0], "gsmall_sum")
    loss = summed[SMALL_ROWS - 1, SMALL_COLS - 1]
    g_small = _unpack_small(summed, local=False)
    for n, length, sharded in SMALL:
        if sharded:
            g_small[n] = lax.dynamic_slice(g_small[n], (me * (length // N_DEV),), (length // N_DEV,))

    grads, deltas, new_m, new_v = {}, {}, {}, {}
    for n in big_names:
        outs = _adamw(big_local[n], recv[n], _big_shard(m_all[n], n), _big_shard(v_all[n], n), "adamw_" + n)
        grads[n], deltas[n], new_m[n], new_v[n] = (_as_given(a, n, p[n].shape) for a in outs)
    pk = lambda src: _pack_small({n: src[n] for n, _, _ in SMALL})
    outs = _adamw(small_local, pk(g_small)[None], pk(m_all), pk(v_all), "adamw_small")
    g_s, d_s, nm_s, nv_s = (_unpack_small(a, local=True) for a in outs)
    for n, _, _ in SMALL:
        grads[n], deltas[n], new_m[n], new_v[n] = (a[n].reshape(p[n].shape) for a in (g_s, d_s, nm_s, nv_s))

    return (loss, grad_x[None], *[grads[n] for n in WEIGHT_NAMES], *[deltas[n] for n in WEIGHT_NAMES],
            *[new_m[n] for n in WEIGHT_NAMES], *[new_v[n] for n in WEIGHT_NAMES])
```

```python
import numpy as np
import jax
import jax.numpy as jnp
from jax import lax
from jax.experimental import pallas as pl
from jax.experimental.pallas import tpu as pltpu

F32 = jnp.float32
BF16 = jnp.bfloat16

N_DEV = 8
D_MODEL = 1024
CHUNK = 64
EPS = 1e-6
RET_HEADS, RET_DK, RET_DV = 4, 256, 512
RET_STEP = 4
RET_Q_COLS = RET_HEADS * RET_DK
RET_V_COLS = RET_HEADS * RET_DV
ATT_HEADS, ATT_DH = 16, 64
PAST_CHUNKS = 8
REL_CLIP = 256
REL_TABLE = 2 * REL_CLIP + 1
FFN_HIDDEN = 2816
ROPE_BASE = 10000.0
LANES = 128
Q_BLOCK = 256
ATT_SUBS = 4
ATT_ROWS = 32
K_PAD = PAST_CHUNKS * CHUNK
K_WINDOW = Q_BLOCK + K_PAD
REL_BLK = 128
REL_DELTAS = Q_BLOCK // REL_BLK + K_WINDOW // REL_BLK - 1
REL_PAD = 640
NEG = -1e30
VMEM_LIMIT_V7X = 56 * 1024 * 1024
ADAM_LR, ADAM_B1, ADAM_B2, ADAM_EPS, ADAM_WD, ADAM_STEP = 1e-3, 0.9, 0.999, 1e-8, 0.01, 10
MESH = pl.DeviceIdType.MESH
ANY = pl.BlockSpec(memory_space=pl.ANY)


def _params(*semantics):
    return pltpu.CompilerParams(dimension_semantics=semantics, vmem_limit_bytes=VMEM_LIMIT_V7X)


def _pick(dim, cap, align):
    best = None
    for t in range(align, min(dim, cap) + 1, align):
        if dim % t == 0:
            best = t
    assert best is not None, (dim, cap, align)
    return best


def _dot(a, b):
    return lax.dot_general(a, b, (((1,), (0,)), ((), ())), preferred_element_type=F32)


def _dot_nt(a, b):
    return lax.dot_general(a, b, (((1,), (1,)), ((), ())), preferred_element_type=F32)


def _dot_tn(a, b):
    return lax.dot_general(a, b, (((0,), (0,)), ((), ())), preferred_element_type=F32)


def _split2(x):
    hi = x.astype(BF16)
    lo = (x - hi.astype(F32)).astype(BF16)
    return hi, lo


def _split3(x):
    hi = x.astype(BF16)
    r = x - hi.astype(F32)
    mid = r.astype(BF16)
    lo = (r - mid.astype(F32)).astype(BF16)
    return hi, mid, lo


def _sigmoid(x):
    return 1.0 / (1.0 + jnp.exp(-x))


def _accumulate(ref, part, step):
    @pl.when(step == 0)
    def _():
        ref[...] = part

    @pl.when(step > 0)
    def _():
        ref[...] += part


RELAY_AT_NUM, RELAY_AT_DEN = 3, 4


def _my_place():
    return lax.axis_index("x"), lax.axis_index("y"), lax.axis_index("c")


def _flip(v, bit):
    return 1 - v if bit else v


class _NoRelay:
    def relay(self, in_refs, out_refs, sems):
        pass


class _GatherRider:
    def __init__(self, xs, late_relay=False):
        self.inputs = list(xs)
        n = len(xs)
        self.out_shape = [jax.ShapeDtypeStruct((N_DEV,) + x.shape, x.dtype) for x in xs]
        self.scratch = [pltpu.SemaphoreType.DMA((7, n)), pltpu.SemaphoreType.DMA((7, n)),
                        pltpu.SemaphoreType.DMA((n,))]
        self.results = None
        self.late_relay = late_relay

    def _copies(self, x_refs, out_refs, sems):
        send_sems, recv_sems, local_sems = sems
        n = len(x_refs)
        x, y, c = _my_place()
        me, sibling = (x, y, c), (x, y, 1 - c)
        chips = [(1 - x, y), (x, 1 - y), (1 - x, 1 - y)]

        def slot(a, px, py, pc):
            return out_refs[a].at[4 * px + 2 * py + pc]

        def copy(k, a, block, to, own=False):
            return pltpu.make_async_remote_copy(
                src_ref=x_refs[a] if own else slot(a, *block), dst_ref=slot(a, *block),
                send_sem=send_sems.at[k, a], recv_sem=recv_sems.at[k, a],
                device_id=to, device_id_type=MESH)

        mine = [pltpu.make_async_copy(x_refs[a], slot(a, *me), local_sems.at[a]) for a in range(n)]
        first = []
        for a in range(n):
            first.append(copy(0, a, me, sibling, own=True))
            first += [copy(1 + j, a, me, (*chip, c), own=True) for j, chip in enumerate(chips)]
        return n, c, me, sibling, chips, copy, mine, first

    def start(self, x_refs, out_refs, sems):
        _, _, _, _, _, _, mine, first = self._copies(x_refs, out_refs, sems)
        for cp in mine + first:
            cp.start()

    def relay(self, x_refs, out_refs, sems):
        n, c, me, sibling, chips, copy, _, _ = self._copies(x_refs, out_refs, sems)
        for j, chip in enumerate(chips):
            for a in range(n):
                copy(1 + j, a, (*chip, c), me).wait_recv()
                copy(4 + j, a, (*chip, c), sibling).start()

    def finish(self, x_refs, out_refs, sems):
        n, c, me, sibling, chips, copy, mine, first = self._copies(x_refs, out_refs, sems)
        passed = [copy(4 + j, a, (*chip, c), sibling) for j, chip in enumerate(chips) for a in range(n)]
        for a in range(n):
            copy(0, a, sibling, me).wait_recv()
            for j, chip in enumerate(chips):
                copy(4 + j, a, (*chip, 1 - c), me).wait_recv()
        for cp in first + passed:
            cp.wait_send()
        for cp in mine:
            cp.wait()


class _ScatterRider(_NoRelay):
    def __init__(self, gs):
        self.inputs = list(gs)
        n = len(gs)
        self.out_shape = [jax.ShapeDtypeStruct(g.shape, g.dtype) for g in gs]
        self.scratch = [pltpu.SemaphoreType.DMA((7, n)), pltpu.SemaphoreType.DMA((7, n)),
                        pltpu.SemaphoreType.DMA((n,))]
        self.results = None

    def _copies(self, g_refs, out_refs, sems):
        send_sems, recv_sems, local_sems = sems
        x, y, c = _my_place()
        me = 4 * x + 2 * y + c
        mine, copies = [], []
        for a in range(len(g_refs)):
            mine.append(pltpu.make_async_copy(g_refs[a].at[me], out_refs[a].at[me], local_sems.at[a]))
            for k in range(1, N_DEV):
                px, py, pc = _flip(x, k & 4), _flip(y, k & 2), _flip(c, k & 1)
                copies.append(pltpu.make_async_remote_copy(
                    src_ref=g_refs[a].at[4 * px + 2 * py + pc], dst_ref=out_refs[a].at[me],
                    send_sem=send_sems.at[k - 1, a], recv_sem=recv_sems.at[k - 1, a],
                    device_id=(px, py, pc), device_id_type=MESH))
        return mine, copies

    def start(self, g_refs, out_refs, sems):
        mine, copies = self._copies(g_refs, out_refs, sems)
        for cp in mine + copies:
            cp.start()

    def finish(self, g_refs, out_refs, sems):
        mine, copies = self._copies(g_refs, out_refs, sems)
        for cp in copies + mine:
            cp.wait()


class _SiblingSwapRider(_NoRelay):
    def __init__(self, gs):
        self.inputs = list(gs)
        n = len(gs)
        self.out_shape = [jax.ShapeDtypeStruct((4,) + g.shape[1:], g.dtype) for g in gs]
        self.scratch = [pltpu.SemaphoreType.DMA((4, n)), pltpu.SemaphoreType.DMA((4, n))]
        self.results = None

    def _copies(self, g_refs, out_refs, sems):
        send_sems, recv_sems = sems
        x, y, c = _my_place()
        return [pltpu.make_async_remote_copy(
            src_ref=g_refs[a].at[2 * k + 1 - c], dst_ref=out_refs[a].at[k],
            send_sem=send_sems.at[k, a], recv_sem=recv_sems.at[k, a],
            device_id=(x, y, 1 - c), device_id_type=MESH)
            for a in range(len(g_refs)) for k in range(4)]

    def start(self, g_refs, out_refs, sems):
        for cp in self._copies(g_refs, out_refs, sems):
            cp.start()

    def finish(self, g_refs, out_refs, sems):
        for cp in self._copies(g_refs, out_refs, sems):
            cp.wait()


class _ChipScatterRider(_NoRelay):
    def __init__(self, ps):
        self.inputs = list(ps)
        n = len(ps)
        self.out_shape = [jax.ShapeDtypeStruct(p.shape, p.dtype) for p in ps]
        self.scratch = [pltpu.SemaphoreType.DMA((3, n)), pltpu.SemaphoreType.DMA((3, n)),
                        pltpu.SemaphoreType.DMA((n,))]
        self.results = None

    def _copies(self, p_refs, out_refs, sems):
        send_sems, recv_sems, local_sems = sems
        x, y, c = _my_place()
        my_chip = 2 * x + y
        chips = [(1 - x, y), (x, 1 - y), (1 - x, 1 - y)]
        n = len(p_refs)
        mine = [pltpu.make_async_copy(p_refs[a].at[my_chip], out_refs[a].at[my_chip], local_sems.at[a])
                for a in range(n)]
        copies = [pltpu.make_async_remote_copy(
            src_ref=p_refs[a].at[2 * cx + cy], dst_ref=out_refs[a].at[my_chip],
            send_sem=send_sems.at[j, a], recv_sem=recv_sems.at[j, a],
            device_id=(cx, cy, c), device_id_type=MESH)
            for a in range(n) for j, (cx, cy) in enumerate(chips)]
        return mine, copies

    def start(self, p_refs, out_refs, sems):
        mine, copies = self._copies(p_refs, out_refs, sems)
        for cp in mine + copies:
            cp.start()

    def finish(self, p_refs, out_refs, sems):
        mine, copies = self._copies(p_refs, out_refs, sems)
        for cp in copies + mine:
            cp.wait()


def _call(body, name, grid, in_specs, out_specs, out_shape, scratch, semantics, args, rider=None):
    in_specs, out_specs, out_shape, scratch = list(in_specs), list(out_specs), list(out_shape), list(scratch)
    if rider is None:
        return list(pl.pallas_call(
            body, name=name, grid=grid, in_specs=in_specs, out_specs=out_specs, out_shape=out_shape,
            scratch_shapes=scratch, compiler_params=_params(*semantics))(*args))
    n_in, n_out, n_scr = len(in_specs), len(out_specs), len(scratch)
    r_in, r_out = len(rider.inputs), len(rider.out_shape)

    def wrapped(*refs):
        cuts = np.cumsum([0, n_in, r_in, n_out, r_out, n_scr])
        hi, ri, ho, ro, hs = (refs[cuts[i]:cuts[i + 1]] for i in range(5))
        rs = refs[cuts[5]:]
        step, steps = pl.program_id(0), grid[0]
        for d in range(1, len(grid)):
            step, steps = step * grid[d] + pl.program_id(d), steps * grid[d]

        @pl.when(step == 0)
        def _():
            rider.start(ri, ro, rs)

        body(*hi, *ho, *hs)

        late = getattr(rider, "late_relay", False)

        @pl.when(step == (steps - 1 if late else (steps * RELAY_AT_NUM) // RELAY_AT_DEN))
        def _():
            rider.relay(ri, ro, rs)

        @pl.when(step == steps - 1)
        def _():
            rider.finish(ri, ro, rs)

    outs = pl.pallas_call(
        wrapped, name=name, grid=grid,
        in_specs=in_specs + [ANY] * r_in, out_specs=out_specs + [ANY] * r_out,
        out_shape=out_shape + rider.out_shape, scratch_shapes=scratch + rider.scratch,
        compiler_params=_params(*(["arbitrary"] * len(grid))),
    )(*args, *rider.inputs)
    rider.results = list(outs[n_out:])
    return list(outs[:n_out])


_WALK = ((None, None), (0, None), (1, 4), (2, 5), (4, None), (5, None), (3, 6), (6, None))


def _gather_order():
    x, y, c = _my_place()
    (ax, ay), (bx, by), (dx, dy) = (1 - x, y), (x, 1 - y), (1 - x, 1 - y)
    ids = [(x, y, c), (x, y, 1 - c), (ax, ay, c), (bx, by, c), (ax, ay, 1 - c), (bx, by, 1 - c),
           (dx, dy, c), (dx, dy, 1 - c)]
    return jnp.stack([4 * px + 2 * py + pc for px, py, pc in ids]).astype(jnp.int32)


def _proj_gather(x, norm_g, w_shard, extras, name):
    t, d = x.shape
    cols = w_shard.shape[1]
    tm = _pick(t, MM_CAP_MN, 16)
    ni = t // tm
    n = 1 + len(extras)
    rider = _GatherRider([w_shard] + list(extras))

    def body(ord_ref, x_ref, g_ref, *refs):
        sh_refs, proj_ref, gathered = refs[:n], refs[n], refs[n + 1:2 * n + 1]
        h_all, bbuf, bsem, send_sems, recv_sems, local_sems = refs[2 * n + 1:]
        j, i = pl.program_id(0), pl.program_id(1)
        _, c, me, sibling, chips, copy, mine, first = rider._copies(
            sh_refs, gathered, (send_sems, recv_sems, local_sems))
        rows = pl.ds(pl.multiple_of(i * tm, tm), tm)

        def load(step, src):
            return pltpu.make_async_copy(src, bbuf.at[step % 2], bsem.at[step % 2])

        def relayed(k, a):
            return copy(k, a, (*chips[k - 4], c), sibling)

        @pl.when(jnp.logical_and(j == 0, i == 0))
        def _():
            for cp in mine + first:
                cp.start()
            load(0, sh_refs[0]).start()

        @pl.when(i == 0)
        def _():
            load(j, sh_refs[0]).wait()

        @pl.when(j == 0)
        def _():
            groups = []
            for r in range(0, tm, NORM_ROWS):
                xv = x_ref[r:r + NORM_ROWS, :]
                rstd = lax.rsqrt(jnp.mean(xv * xv, axis=-1, keepdims=True) + EPS)
                groups.append((xv * rstd * g_ref[...]).astype(BF16))
            h_all[rows, :] = jnp.concatenate(groups, axis=0)

        proj_ref[...] = _dot(h_all[rows, :], bbuf[j % 2])

        for step in range(N_DEV - 1):
            @pl.when(jnp.logical_and(j == step, i == max(ni - 2, 0)))
            def _(step=step):
                need, relay = _WALK[step + 1]
                copy(need, 0, me, me).wait_recv()
                if relay is not None:
                    relayed(relay, 0).start()
                load(step + 1, gathered[0].at[ord_ref[step + 1]]).start()

        @pl.when(jnp.logical_and(j == N_DEV - 1, i == ni - 1))
        def _():
            for a in range(1, n):
                for k in range(3):
                    copy(1 + k, a, me, me).wait_recv()
                    relayed(4 + k, a).start()
            for a in range(1, n):
                for k in (0, 4, 5, 6):
                    copy(k, a, me, me).wait_recv()
            for cp in first + [relayed(4 + k, a) for a in range(n) for k in range(3)]:
                cp.wait_send()
            for cp in mine:
                cp.wait()

    outs = pl.pallas_call(
        body, name=name,
        grid_spec=pltpu.PrefetchScalarGridSpec(
            num_scalar_prefetch=1, grid=(N_DEV, ni),
            in_specs=[pl.BlockSpec((tm, d), lambda j, i, o: (jnp.where(j == 0, i, ni - 1), 0)),
                      pl.BlockSpec((1, d), lambda j, i, o: (0, 0))] + [ANY] * n,
            out_specs=[pl.BlockSpec((tm, cols), lambda j, i, o: (i, o[j]))] + [ANY] * n,
            scratch_shapes=[pltpu.VMEM((t, d), BF16), pltpu.VMEM((2, d, cols), BF16),
                            pltpu.SemaphoreType.DMA((2,))] + rider.scratch),
        out_shape=[jax.ShapeDtypeStruct((t, N_DEV * cols), F32)] + rider.out_shape,
        compiler_params=_params("arbitrary", "arbitrary"),
    )(_gather_order(), x, norm_g, w_shard, *extras)
    return outs[0], list(outs[1:])


def _exchange(rider, name):
    r_in, r_out = len(rider.inputs), len(rider.out_shape)

    def body(*refs):
        ri, ro, rs = refs[:r_in], refs[r_in:r_in + r_out], refs[r_in + r_out:]
        rider.start(ri, ro, rs)
        rider.relay(ri, ro, rs)
        rider.finish(ri, ro, rs)

    return list(pl.pallas_call(
        body, name=name, in_specs=[ANY] * r_in, out_specs=[ANY] * r_out,
        out_shape=rider.out_shape, scratch_shapes=rider.scratch)(*rider.inputs))


MM_CAP_MN = 1024
MM_CAP_M_GRAD = 1408
MM_CAP_N = 1536
MM_CAP_K = 3072
MM_CAP_K_TOKENS = 2048
MM_CAP_K_RMS = 8192
MM_CAP_M_RMS = 512
NORM_ROWS = 256


def _mm(a, b, mode, name, out_dtype=F32, res=None, out_block=None, epilogue=None, extra=None, norm_g=None,
        norm_b=False, rider=None):
    a3, b3 = a.ndim == 3, b.ndim == 3
    um = un = uk = None
    if mode in ("nn", "nt"):
        if a3:
            m, uk = a.shape[1:]
            k = a.shape[0] * uk
        else:
            m, k = a.shape
    else:
        if a3:
            k, um = a.shape[1:]
            m = a.shape[0] * um
        else:
            k, m = a.shape
    if mode in ("nn", "tn"):
        if b3:
            kb, un = b.shape[1:]
            n = b.shape[0] * un
        else:
            kb, n = b.shape
        assert kb == k, (a.shape, b.shape, mode)
    else:
        if b3:
            n, ukb = b.shape[1:]
            assert b.shape[0] * ukb == k and uk in (None, ukb), (a.shape, b.shape, mode)
            uk = ukb
        else:
            n, kb = b.shape
            assert kb == k, (a.shape, b.shape, mode)
    if out_block is not None:
        assert un in (None, out_block)
        un = out_block

    def tile(dim, unit, cap, align):
        if unit is None:
            return _pick(dim, cap, align), 1
        c = max(1, cap // unit)
        while (dim // unit) % c:
            c -= 1
        return unit, c

    cap_m = MM_CAP_M_GRAD if mode == "tn" else (MM_CAP_M_RMS if epilogue == "rms_bwd" else MM_CAP_MN)
    um, cm = tile(m, um, cap_m, 128 if mode == "tn" else 16)
    un, cn = tile(n, un, MM_CAP_N, 128)
    cap_k = MM_CAP_K_TOKENS if mode == "tn" else (MM_CAP_K_RMS if epilogue == "rms_bwd" else MM_CAP_K)
    uk, ck = tile(k, uk, cap_k, 128)
    if epilogue == "rms_bwd":
        assert mode != "tn" and n == D_MODEL and cm == cn == 1 and res is None and out_block is None
    if epilogue == "loss":
        assert n == D_MODEL and cm == cn == 1 and res is not None and out_block is None
    if norm_g is not None and norm_b:
        assert mode == "tn" and not b3 and n == D_MODEL and cn == 1
    elif norm_g is not None:
        assert not a3 and (m if mode == "tn" else k) == D_MODEL and (cm if mode == "tn" else ck) == 1
    if epilogue == "swiglu":
        assert res is None and ((mode == "nn" and b3 and out_block is None) or
                                (mode == "nt" and not b3 and out_block is not None))
        cn = 2
    if epilogue == "swiglu_bwd":
        assert mode == "nt" and out_block is not None and extra is not None and res is None
        cn = 1
    tm, tn, tk = cm * um, cn * un, ck * uk
    nk = k // tk
    dot = {"nn": _dot, "nt": _dot_nt, "tn": _dot_tn}[mode]
    half = n // un // 2
    blocked_out = out_block is not None or epilogue in ("swiglu", "swiglu_bwd")
    extras = [] if extra is None else (list(extra) if isinstance(extra, (tuple, list)) else [extra])

    def sl(idx, unit, count):
        return slice(None) if count == 1 else slice(idx * unit, (idx + 1) * unit)

    def body(*refs):
        a_ref, b_ref = refs[0], refs[1]
        pos = 2
        r_ref = ng_ref = None
        if res is not None:
            r_ref, pos = refs[pos], pos + 1
        e_refs, pos = refs[pos:pos + len(extras)], pos + len(extras)
        if norm_g is not None:
            ng_ref, pos = refs[pos], pos + 1
        outs, acc_ref = refs[pos:-1], refs[-1]
        kk = pl.program_id(2)

        def normed(x_ref):
            groups = []
            for r in range(0, x_ref.shape[0], NORM_ROWS):
                xv = x_ref[r:r + NORM_ROWS, :]
                rstd = lax.rsqrt(jnp.mean(xv * xv, axis=-1, keepdims=True) + EPS)
                groups.append((xv * rstd * ng_ref[...]).astype(BF16))
            return jnp.concatenate(groups, axis=0)

        def a_blk(mi, ki):
            if norm_g is not None and not norm_b:
                return normed(a_ref)
            if mode in ("nn", "nt"):
                return a_ref[ki] if a3 else a_ref[:, sl(ki, uk, ck)]
            return a_ref[mi] if a3 else a_ref[:, sl(mi, um, cm)]

        def b_blk(ki, ni):
            if norm_b:
                return normed(b_ref)
            if epilogue == "swiglu":
                return b_ref[ni, 0]
            if mode in ("nn", "tn"):
                return b_ref[ni] if b3 else b_ref[sl(ki, uk, ck), sl(ni, un, cn)]
            return b_ref[ki][sl(ni, un, cn), :] if b3 else b_ref[sl(ni, un, cn), sl(ki, uk, ck)]

        parts = {}
        for mi in range(cm):
            for ni in range(cn):
                part = None
                for ki in range(ck):
                    d = dot(a_blk(mi, ki).astype(BF16), b_blk(ki, ni).astype(BF16))
                    part = d if part is None else part + d
                parts[mi, ni] = part

        def finish(total):
            if epilogue == "swiglu":
                gate, up = total[0, 0], total[0, 1]
                outs[0][0, 0] = gate.astype(BF16)
                outs[0][1, 0] = up.astype(BF16)
                outs[1][0] = (gate * _sigmoid(gate) * up).astype(BF16)
                return
            if epilogue == "swiglu_bwd":
                dact = total[0, 0]
                gate, up = e_refs[0][0, 0].astype(F32), e_refs[0][1, 0].astype(F32)
                sg = _sigmoid(gate)
                outs[0][0, 0] = (dact * up * (sg * (1.0 + gate * (1.0 - sg)))).astype(BF16)
                outs[0][1, 0] = (dact * (gate * sg)).astype(BF16)
                return
            if epilogue == "rms_bwd":
                x_ref, g_ref, dres_ref = e_refs
                dh, dg = total[0, 0], None
                for r in range(0, tm, NORM_ROWS):
                    rows = slice(r, r + NORM_ROWS)
                    xv, dhv = x_ref[rows, :], dh[rows, :]
                    rstd = lax.rsqrt(jnp.mean(xv * xv, axis=-1, keepdims=True) + EPS)
                    xh = xv * rstd
                    dyg = dhv * g_ref[...]
                    c = jnp.mean(dyg * xh, axis=-1, keepdims=True)
                    outs[0][rows, :] = dres_ref[rows, :] + rstd * (dyg - xh * c)
                    part = jnp.sum(dhv * xh, axis=0, keepdims=True)
                    dg = part if dg is None else dg + part
                _accumulate(outs[1], dg, pl.program_id(0))
                return
            if epilogue == "loss":
                diff = r_ref[...] + total[0, 0] - e_refs[0][...]
                outs[0][...] = diff * (1.0 / n)
                sq = jnp.sum(jnp.sum(diff * diff, axis=-1, keepdims=True), axis=0, keepdims=True)
                _accumulate(outs[1], sq * (0.5 / n), pl.program_id(0))
                return
            for (mi, ni), val in total.items():
                rows, cols = sl(mi, um, cm), sl(ni, un, cn)
                if res is not None:
                    val = r_ref[rows, cols] + val
                if blocked_out:
                    outs[0][ni, rows] = val.astype(out_dtype)
                else:
                    outs[0][rows, cols] = val.astype(out_dtype)

        if nk == 1:
            finish(parts)
        else:
            @pl.when(kk == 0)
            def _():
                for (mi, ni), val in parts.items():
                    acc_ref[mi * cn + ni] = val

            @pl.when(jnp.logical_and(kk > 0, kk < nk - 1))
            def _():
                for (mi, ni), val in parts.items():
                    acc_ref[mi * cn + ni] += val

            @pl.when(kk == nk - 1)
            def _():
                finish({key: acc_ref[key[0] * cn + key[1]] + val for key, val in parts.items()})

    if mode in ("nn", "nt"):
        a_spec = (pl.BlockSpec((ck, tm, uk), lambda i, j, kk: (kk, i, 0)) if a3
                  else pl.BlockSpec((tm, tk), lambda i, j, kk: (i, kk)))
    else:
        a_spec = (pl.BlockSpec((cm, tk, um), lambda i, j, kk: (i, kk, 0)) if a3
                  else pl.BlockSpec((tk, tm), lambda i, j, kk: (kk, i)))
    pair_spec = pl.BlockSpec((2, 1, tm, un), lambda i, j, kk: (0, j, i, 0))
    row_spec = pl.BlockSpec((tm, tn), lambda i, j, kk: (i, 0))
    vec_spec = pl.BlockSpec((1, tn), lambda i, j, kk: (0, 0))
    if epilogue == "swiglu" and mode == "nn":
        b = b.reshape(2, half, k, un)
        b_spec = pl.BlockSpec((2, 1, tk, un), lambda i, j, kk: (0, j, kk, 0))
    elif epilogue == "swiglu":
        b = b.reshape(2, half, un, k)
        b_spec = pl.BlockSpec((2, 1, un, tk), lambda i, j, kk: (0, j, 0, kk))
    elif mode in ("nn", "tn"):
        b_spec = (pl.BlockSpec((cn, tk, un), lambda i, j, kk: (j, kk, 0)) if b3
                  else pl.BlockSpec((tk, tn), lambda i, j, kk: (kk, j)))
    else:
        b_spec = (pl.BlockSpec((ck, tn, uk), lambda i, j, kk: (kk, j, 0)) if b3
                  else pl.BlockSpec((tn, tk), lambda i, j, kk: (j, kk)))
    if epilogue == "swiglu":
        out_specs = [pair_spec, pl.BlockSpec((1, tm, un), lambda i, j, kk: (j, i, 0))]
        out_shape = [jax.ShapeDtypeStruct((2, half, m, un), BF16), jax.ShapeDtypeStruct((half, m, un), BF16)]
    elif epilogue == "swiglu_bwd":
        out_specs = [pair_spec]
        out_shape = [jax.ShapeDtypeStruct(extra.shape, BF16)]
    elif epilogue == "rms_bwd":
        out_specs = [row_spec, vec_spec]
        out_shape = [jax.ShapeDtypeStruct((m, n), F32), jax.ShapeDtypeStruct((1, n), F32)]
    elif epilogue == "loss":
        out_specs = [row_spec, pl.BlockSpec((1, 1), lambda i, j, kk: (0, 0))]
        out_shape = [jax.ShapeDtypeStruct((m, n), F32), jax.ShapeDtypeStruct((1, 1), F32)]
    elif blocked_out:
        out_specs = [pl.BlockSpec((cn, tm, un), lambda i, j, kk: (j, i, 0))]
        out_shape = [jax.ShapeDtypeStruct((n // un, m, un), out_dtype)]
    else:
        out_specs = [pl.BlockSpec((tm, tn), lambda i, j, kk: (i, j))]
        out_shape = [jax.ShapeDtypeStruct((m, n), out_dtype)]
    in_specs, args = [a_spec, b_spec], [a, b]
    if res is not None:
        in_specs.append(pl.BlockSpec((tm, tn), lambda i, j, kk: (i, j)))
        args.append(res)
    if epilogue == "swiglu_bwd":
        in_specs.append(pair_spec)
    elif epilogue == "rms_bwd":
        in_specs += [row_spec, vec_spec, row_spec]
    elif epilogue == "loss":
        in_specs.append(row_spec)
    args += extras
    if norm_g is not None:
        in_specs.append(pl.BlockSpec((1, D_MODEL), lambda i, j, kk: (0, 0)))
        args.append(norm_g)
    semantics = ("arbitrary",) * 3 if epilogue in ("rms_bwd", "loss") else ("parallel", "parallel", "arbitrary")
    out = _call(body, name, (m // tm, n // tn, nk), in_specs, out_specs, out_shape,
                [pltpu.VMEM((cm * cn, um, un), F32)], semantics, args, rider)
    return out if epilogue in ("swiglu", "rms_bwd", "loss") else out[0]


def _head_sums(v, ind):
    hi, lo = _split2(v)
    return _dot(hi, ind) + _dot(lo, ind)


def _head_spread(per_head, ind):
    hi, lo = _split2(per_head)
    return _dot_nt(hi, ind) + _dot_nt(lo, ind)


def _head_rstd(xv, ind):
    return _head_spread(lax.rsqrt(_head_sums(xv * xv, ind) * (1.0 / ATT_DH) + EPS), ind)


def _hn_bwd_math(xv, gv, ind, dyv, scale):
    rstd = _head_rstd(xv, ind)
    xh = xv * rstd
    dyn = dyv * scale
    dyg = dyn * gv
    dx = rstd * (dyg - xh * _head_spread(_head_sums(dyg * xh, ind) * (1.0 / ATT_DH), ind))
    return dx, jnp.sum(dyn * xh, axis=0, keepdims=True)


def _q_hnorm(x, g_tiled, bd, scale, name):
    t, d = x.shape
    tm = _pick(t, 512, 16)

    def body(x_ref, g_ref, bd_ref, o_ref):
        xv = x_ref[...]
        o_ref[...] = (xv * _head_rstd(xv, bd_ref[...]) * g_ref[...] * scale).astype(BF16)

    return pl.pallas_call(
        body, name=name, grid=(t // tm,),
        in_specs=[pl.BlockSpec((tm, d), lambda i: (i, 0)), pl.BlockSpec((1, d), lambda i: (0, 0)),
                  pl.BlockSpec((d, LANES), lambda i: (0, 0))],
        out_specs=pl.BlockSpec((tm, d), lambda i: (i, 0)),
        out_shape=jax.ShapeDtypeStruct((t, d), BF16),
        compiler_params=_params("parallel"),
    )(x, g_tiled, bd)


def _q_dhnorm(x, g_tiled, bd, dy, scale, name):
    t, d = x.shape
    tm = _pick(t, 512, 16)

    def body(x_ref, g_ref, bd_ref, dy_ref, dx_ref, dg_ref):
        dx, part = _hn_bwd_math(x_ref[...], g_ref[...], bd_ref[...], dy_ref[...], scale)
        dx_ref[...] = dx.astype(BF16)
        _accumulate(dg_ref, part, pl.program_id(0))

    row = pl.BlockSpec((tm, d), lambda i: (i, 0))
    vec = pl.BlockSpec((1, d), lambda i: (0, 0))
    return pl.pallas_call(
        body, name=name, grid=(t // tm,),
        in_specs=[row, vec, pl.BlockSpec((d, LANES), lambda i: (0, 0)), row],
        out_specs=[row, vec],
        out_shape=[jax.ShapeDtypeStruct((t, d), BF16), jax.ShapeDtypeStruct((1, d), F32)],
        compiler_params=_params("arbitrary"),
    )(x, g_tiled, bd, dy)


def _kv_prep(kv, g_tiled, bd, name):
    t = kv.shape[0]
    d = D_MODEL
    tm = K_PAD
    assert t % tm == 0

    def body(k_ref, v_ref, g_ref, bd_ref, kp_ref, vp_ref):
        i = pl.program_id(0)

        @pl.when(i == 0)
        def _():
            kp_ref[...] = jnp.zeros_like(kp_ref)
            vp_ref[...] = jnp.zeros_like(vp_ref)

        @pl.when(i > 0)
        def _():
            xv = k_ref[...]
            kp_ref[...] = (xv * _head_rstd(xv, bd_ref[...]) * g_ref[...]).astype(BF16)
            vp_ref[...] = v_ref[...].astype(BF16)

    shp = jax.ShapeDtypeStruct((t + K_PAD, d), BF16)
    out = pl.BlockSpec((tm, d), lambda i: (i, 0))
    return pl.pallas_call(
        body, name=name, grid=(t // tm + 1,),
        in_specs=[pl.BlockSpec((tm, d), lambda i: (jnp.maximum(i - 1, 0), 0)),
                  pl.BlockSpec((tm, d), lambda i: (jnp.maximum(i - 1, 0), 1)),
                  pl.BlockSpec((1, d), lambda i: (0, 0)), pl.BlockSpec((d, LANES), lambda i: (0, 0))],
        out_specs=[out, out], out_shape=[shp, shp],
        compiler_params=_params("arbitrary"),
    )(kv, kv, g_tiled, bd)


def _kv_dprep(kv, g_tiled, bd, dkp_t, dvp_t, name):
    t = kv.shape[0]
    d = D_MODEL
    tm = K_PAD

    def body(k_ref, g_ref, bd_ref, dk_ref, dv_ref, o_ref, dg_ref):
        dx, part = _hn_bwd_math(k_ref[...], g_ref[...], bd_ref[...], dk_ref[...].T, 1.0)
        o_ref[:, :d] = dx.astype(BF16)
        o_ref[:, d:] = dv_ref[...].T.astype(BF16)
        _accumulate(dg_ref, part, pl.program_id(0))

    vec = pl.BlockSpec((1, d), lambda i: (0, 0))
    padded = pl.BlockSpec((d, tm), lambda i: (0, i + 1))
    return pl.pallas_call(
        body, name=name, grid=(t // tm,),
        in_specs=[pl.BlockSpec((tm, d), lambda i: (i, 0)), vec, pl.BlockSpec((d, LANES), lambda i: (0, 0)),
                  padded, padded],
        out_specs=[pl.BlockSpec((tm, 2 * d), lambda i: (i, 0)), vec],
        out_shape=[jax.ShapeDtypeStruct((t, 2 * d), BF16), jax.ShapeDtypeStruct((1, d), F32)],
        compiler_params=_params("arbitrary"),
    )(kv, g_tiled, bd, dkp_t, dvp_t)


def _ret_consts(t):
    h = np.arange(RET_HEADS, dtype=np.float32)
    lg = np.log(np.float32(1.0) - np.float32(2.0) ** (np.float32(-5.0) - h)).astype(np.float32)
    tt = np.arange(CHUNK, dtype=np.float32)
    intra = np.exp(lg[:, None, None] * np.abs(tt[:, None] - tt[None, :])).astype(np.float32)
    q_dec = np.exp(lg[:, None] * (tt + 1.0)).astype(np.float32)
    k_dec = np.exp(lg[:, None] * (CHUNK - 1.0 - tt)).astype(np.float32)
    s_dec = [float(v) for v in np.exp(lg * np.float32(CHUNK)).astype(np.float32)]
    qd = np.broadcast_to(q_dec[:, :, None], (RET_HEADS, CHUNK, RET_DK)).copy()
    kd = np.broadcast_to(k_dec[:, :, None], (RET_HEADS, CHUNK, RET_DK)).copy()
    half = RET_DK // 2
    inv_freq = ROPE_BASE ** (-jnp.arange(half, dtype=F32) / half)
    ang = jnp.arange(t).astype(F32)[:, None] * inv_freq[None, :]
    return jnp.asarray(intra), jnp.asarray(qd), jnp.asarray(kd), s_dec, jnp.cos(ang), jnp.sin(ang)


def _rope(x, cos, sin):
    half = RET_DK // 2
    x1, x2 = x[:, :half], x[:, half:]
    return jnp.concatenate([x1 * cos - x2 * sin, x1 * sin + x2 * cos], axis=-1)


def _unrope(d, cos, sin):
    half = RET_DK // 2
    d1, d2 = d[:, :half], d[:, half:]
    return jnp.concatenate([d1 * cos + d2 * sin, d2 * cos - d1 * sin], axis=-1)


def _ret_slices(h):
    q = slice(h * RET_DK, (h + 1) * RET_DK)
    k = slice(RET_Q_COLS + h * RET_DK, RET_Q_COLS + (h + 1) * RET_DK)
    v = slice(2 * RET_Q_COLS + h * RET_DV, 2 * RET_Q_COLS + (h + 1) * RET_DV)
    g = slice(2 * RET_Q_COLS + RET_V_COLS + h * RET_DV, 2 * RET_Q_COLS + RET_V_COLS + (h + 1) * RET_DV)
    o = slice(h * RET_DV, (h + 1) * RET_DV)
    return q, k, v, g, o


def _ret_fwd(proj, gn, consts, name, rider=None):
    t, cols = proj.shape
    n = t // CHUNK
    intra, qd, kd, s_dec, cos, sin = consts
    k_scale = RET_DK ** -0.5

    def body(p_ref, cos_ref, sin_ref, intra_ref, qd_ref, kd_ref, gn_ref, y_ref, o_ref, st_ref, state):
        i = pl.program_id(0)

        @pl.when(i == 0)
        def _():
            state[...] = jnp.zeros_like(state)

        for c in range(RET_STEP):
            rows = slice(c * CHUNK, (c + 1) * CHUNK)
            cosv, sinv = cos_ref[rows, :], sin_ref[rows, :]
            for h in range(RET_HEADS):
                qs, ks, vs, gs, os_ = _ret_slices(h)
                qr = _rope(p_ref[rows, qs], cosv, sinv)
                kr = _rope(p_ref[rows, ks], cosv, sinv) * k_scale
                vb = p_ref[rows, vs].astype(BF16)
                gv = p_ref[rows, gs]
                scores = _dot_nt(qr.astype(BF16), kr.astype(BF16)) * intra_ref[h]
                s_old = state[h]
                s_old_b = s_old.astype(BF16)
                st_ref[c, h] = s_old_b
                o = _dot(scores.astype(BF16), vb) + _dot((qr * qd_ref[h]).astype(BF16), s_old_b)
                state[h] = s_old * s_dec[h] + _dot_tn((kr * kd_ref[h]).astype(BF16), vb)
                rstd = lax.rsqrt(jnp.mean(o * o, axis=-1, keepdims=True) + EPS)
                on = o * rstd * gn_ref[:, os_]
                o_ref[rows, os_] = o
                y_ref[rows, os_] = (gv * _sigmoid(gv) * on).astype(BF16)

    full3 = lambda a: pl.BlockSpec(a.shape, lambda i: (0, 0, 0))
    step = RET_STEP * CHUNK
    return _call(
        body, name, (n // RET_STEP,),
        [pl.BlockSpec((step, cols), lambda i: (i, 0)),
         pl.BlockSpec((step, RET_DK // 2), lambda i: (i, 0)),
         pl.BlockSpec((step, RET_DK // 2), lambda i: (i, 0)),
         full3(intra), full3(qd), full3(kd),
         pl.BlockSpec((1, RET_V_COLS), lambda i: (0, 0))],
        [pl.BlockSpec((step, RET_V_COLS), lambda i: (i, 0)),
         pl.BlockSpec((step, RET_V_COLS), lambda i: (i, 0)),
         pl.BlockSpec((RET_STEP, RET_HEADS, RET_DK, RET_DV), lambda i: (i, 0, 0, 0))],
        [jax.ShapeDtypeStruct((t, RET_V_COLS), BF16),
         jax.ShapeDtypeStruct((t, RET_V_COLS), F32),
         jax.ShapeDtypeStruct((n, RET_HEADS, RET_DK, RET_DV), BF16)],
        [pltpu.VMEM((RET_HEADS, RET_DK, RET_DV), F32)], ("arbitrary",),
        (proj, cos, sin, intra, qd, kd, gn), rider)


def _ret_bwd(proj, gn, o_saved, states, dy, consts, name, rider=None):
    t, cols = proj.shape
    n = t // CHUNK
    intra, qd, kd, s_dec, cos, sin = consts
    k_scale = RET_DK ** -0.5

    def body(p_ref, cos_ref, sin_ref, intra_ref, qd_ref, kd_ref, gn_ref, o_ref, st_ref, dy_ref,
             dp_ref, dgn_ref, dstate):
        i = pl.program_id(0)

        @pl.when(i == 0)
        def _():
            dstate[...] = jnp.zeros_like(dstate)

        dgn = None
        for c in reversed(range(RET_STEP)):
            rows = slice(c * CHUNK, (c + 1) * CHUNK)
            cosv, sinv = cos_ref[rows, :], sin_ref[rows, :]
            dgn_parts = []
            for h in range(RET_HEADS):
                qs, ks, vs, gs, os_ = _ret_slices(h)
                qr = _rope(p_ref[rows, qs], cosv, sinv)
                kr = _rope(p_ref[rows, ks], cosv, sinv) * k_scale
                qb, kb = qr.astype(BF16), kr.astype(BF16)
                vb = p_ref[rows, vs].astype(BF16)
                gv = p_ref[rows, gs]
                ov = o_ref[rows, os_]
                dyv = dy_ref[rows, os_]
                gnv = gn_ref[:, os_]
                sg = _sigmoid(gv)
                rstd = lax.rsqrt(jnp.mean(ov * ov, axis=-1, keepdims=True) + EPS)
                oh = ov * rstd
                d_on = dyv * (gv * sg)
                dg = dyv * (oh * gnv) * (sg * (1.0 + gv * (1.0 - sg)))
                dgn_parts.append(jnp.sum(d_on * oh, axis=0, keepdims=True))
                d_oh = d_on * gnv
                do = rstd * (d_oh - oh * jnp.mean(d_oh * oh, axis=-1, keepdims=True))
                dob = do.astype(BF16)
                mask = intra_ref[h]
                a_b = (_dot_nt(qb, kb) * mask).astype(BF16)
                da_b = (_dot_nt(dob, vb) * mask).astype(BF16)
                ds_new = dstate[h]
                ds_new_b = ds_new.astype(BF16)
                s_old_b = st_ref[c, h]
                qdv, kdv = qd_ref[h], kd_ref[h]
                dv = _dot_tn(a_b, dob) + _dot((kr * kdv).astype(BF16), ds_new_b)
                dqr = _dot(da_b, kb) + _dot_nt(dob, s_old_b) * qdv
                dkr = _dot_tn(da_b, qb) + _dot_nt(vb, ds_new_b) * kdv
                dstate[h] = ds_new * s_dec[h] + _dot_tn((qr * qdv).astype(BF16), dob)
                dp_ref[rows, qs] = _unrope(dqr, cosv, sinv).astype(BF16)
                dp_ref[rows, ks] = _unrope(dkr * k_scale, cosv, sinv).astype(BF16)
                dp_ref[rows, vs] = dv.astype(BF16)
                dp_ref[rows, gs] = dg.astype(BF16)
            part = jnp.concatenate(dgn_parts, axis=-1)
            dgn = part if dgn is None else dgn + part
        _accumulate(dgn_ref, dgn, i)

    steps = n // RET_STEP
    step = RET_STEP * CHUNK
    rev = lambda i: (steps - 1 - i, 0)
    full3 = lambda a: pl.BlockSpec(a.shape, lambda i: (0, 0, 0))
    return _call(
        body, name, (steps,),
        [pl.BlockSpec((step, cols), rev),
         pl.BlockSpec((step, RET_DK // 2), rev),
         pl.BlockSpec((step, RET_DK // 2), rev),
         full3(intra), full3(qd), full3(kd),
         pl.BlockSpec((1, RET_V_COLS), lambda i: (0, 0)),
         pl.BlockSpec((step, RET_V_COLS), rev),
         pl.BlockSpec((RET_STEP, RET_HEADS, RET_DK, RET_DV), lambda i: (steps - 1 - i, 0, 0, 0)),
         pl.BlockSpec((step, RET_V_COLS), rev)],
        [pl.BlockSpec((step, cols), rev),
         pl.BlockSpec((1, RET_V_COLS), lambda i: (0, 0))],
        [jax.ShapeDtypeStruct((t, cols), BF16),
         jax.ShapeDtypeStruct((1, RET_V_COLS), F32)],
        [pltpu.VMEM((RET_HEADS, RET_DK, RET_DV), F32)], ("arbitrary",),
        (proj, cos, sin, intra, qd, kd, gn, o_saved, states, dy), rider)


def _att_common(q_ref, kp_ref, vp_ref, sub):
    blk = pl.program_id(1) * ATT_SUBS + sub
    start = pl.multiple_of(blk * Q_BLOCK, Q_BLOCK)
    kw = kp_ref[pl.ds(start, K_WINDOW), :]
    vw = vp_ref[pl.ds(start, K_WINDOW), :]
    kvalid = blk * Q_BLOCK - K_PAD + lax.broadcasted_iota(jnp.int32, (1, K_WINDOW), 1) >= 0
    lane = lax.broadcasted_iota(jnp.int32, (1, LANES), 1)
    qrows = slice(sub * Q_BLOCK, (sub + 1) * Q_BLOCK)
    return start, qrows, q_ref[qrows, :], kw, vw, kvalid, (lane < ATT_DH, lane >= ATT_DH)


def _row_groups():
    return [slice(r * ATT_ROWS, (r + 1) * ATT_ROWS) for r in range(Q_BLOCK // ATT_ROWS)]


def _lane_copies(x):
    return jnp.tile(x, (1, K_WINDOW // LANES))


def _att_specs(t, tp):
    qspec = pl.BlockSpec((ATT_SUBS * Q_BLOCK, LANES), lambda h, i: (i, h))
    kspec = pl.BlockSpec((tp, LANES), lambda h, i: (0, h))
    bspec = pl.BlockSpec((2, Q_BLOCK, K_WINDOW), lambda h, i: (h, 0, 0))
    return qspec, kspec, bspec


def _att_fwd(q, kp, vp, bias, name, rider=None):
    t, d = q.shape
    tp = kp.shape[0]

    def body(q_ref, kp_ref, vp_ref, bias_ref, o_ref, lse_ref, s_scr, p_scr, lse_scr):
        for sub in range(ATT_SUBS):
            _, qrows, q2, kw, vw, kvalid, sel = _att_common(q_ref, kp_ref, vp_ref, sub)
            for hh in range(2):
                s_scr[sub, hh] = _dot_nt(jnp.where(sel[hh], q2, 0), kw)
            for hh in range(2):
                for rows in _row_groups():
                    s = jnp.where(kvalid, s_scr[sub, hh, rows, :] + bias_ref[hh, rows, :], NEG)
                    m = jnp.max(s, axis=-1, keepdims=True)
                    e = jnp.exp(s - m)
                    l = jnp.sum(e, axis=-1, keepdims=True)
                    p_scr[sub, hh, rows, :] = (e * (1.0 / l)).astype(BF16)
                    lse_scr[sub, hh, rows, :] = jnp.broadcast_to(m + jnp.log(l), (ATT_ROWS, LANES))
            outs = [_dot(p_scr[sub, hh], vw) for hh in range(2)]
            o_ref[qrows, :] = jnp.where(sel[0], outs[0], outs[1]).astype(BF16)
            lse_ref[qrows, :] = jnp.where(sel[0], lse_scr[sub, 0], lse_scr[sub, 1])

    qspec, kspec, bspec = _att_specs(t, tp)
    return _call(body, name, (d // LANES, t // (ATT_SUBS * Q_BLOCK)), [qspec, kspec, kspec, bspec], [qspec, qspec],
                 [jax.ShapeDtypeStruct((t, d), BF16), jax.ShapeDtypeStruct((t, d), F32)],
                 [pltpu.VMEM((ATT_SUBS, 2, Q_BLOCK, K_WINDOW), F32),
                  pltpu.VMEM((ATT_SUBS, 2, Q_BLOCK, K_WINDOW), BF16),
                  pltpu.VMEM((ATT_SUBS, 2, Q_BLOCK, LANES), F32)],
                 ("parallel", "arbitrary"), (q, kp, vp, bias), rider)


def _att_bwd(q, kp, vp, bias, do, o, lse, name, rider=None):
    t, d = q.shape
    tp = kp.shape[0]

    def body(q_ref, kp_ref, vp_ref, bias_ref, do_ref, o_ref, lse_ref, dq_ref, dkp_ref, dvp_ref, db_ref,
             s_scr, dp_scr, p_scr, ds_scr, row_scr):
        @pl.when(pl.program_id(1) == 0)
        def _():
            dkp_ref[...] = jnp.zeros_like(dkp_ref)
            dvp_ref[...] = jnp.zeros_like(dvp_ref)
            db_ref[...] = jnp.zeros_like(db_ref)

        for sub in range(ATT_SUBS):
            start, qrows, q2, kw, vw, kvalid, sel = _att_common(q_ref, kp_ref, vp_ref, sub)
            do2 = do_ref[qrows, :]
            qm = [jnp.where(sel[hh], q2, 0) for hh in range(2)]
            dom = [jnp.where(sel[hh], do2, 0) for hh in range(2)]
            do_o = do2.astype(F32) * o_ref[qrows, :].astype(F32)
            lse2 = lse_ref[qrows, :]
            for hh in range(2):
                s_scr[sub, hh] = _dot_nt(qm[hh], kw)
                dp_scr[sub, hh] = _dot_nt(dom[hh], vw)
                lse_h = jnp.max(jnp.where(sel[hh], lse2, NEG), axis=-1, keepdims=True)
                delta = jnp.sum(jnp.where(sel[hh], do_o, 0.0), axis=-1, keepdims=True)
                row_scr[sub, hh, 0] = jnp.broadcast_to(lse_h, (Q_BLOCK, LANES))
                row_scr[sub, hh, 1] = jnp.broadcast_to(delta, (Q_BLOCK, LANES))
            for hh in range(2):
                for rows in _row_groups():
                    s = jnp.where(kvalid, s_scr[sub, hh, rows, :] + bias_ref[hh, rows, :], NEG)
                    p = jnp.exp(s - _lane_copies(row_scr[sub, hh, 0, rows, :]))
                    ds = p * (dp_scr[sub, hh, rows, :] - _lane_copies(row_scr[sub, hh, 1, rows, :]))
                    db_ref[hh, rows, :] += ds
                    p_scr[sub, hh, rows, :] = p.astype(BF16)
                    ds_scr[sub, hh, rows, :] = ds.astype(BF16)
            dqs = [_dot(ds_scr[sub, hh], kw) for hh in range(2)]
            dq_ref[qrows, :] = jnp.where(sel[0], dqs[0], dqs[1])
            dkp_ref[:, pl.ds(start, K_WINDOW)] += (_dot_tn(qm[0], ds_scr[sub, 0]) +
                                                   _dot_tn(qm[1], ds_scr[sub, 1]))
            dvp_ref[:, pl.ds(start, K_WINDOW)] += (_dot_tn(dom[0], p_scr[sub, 0]) +
                                                   _dot_tn(dom[1], p_scr[sub, 1]))

    qspec, kspec, bspec = _att_specs(t, tp)
    tspec = pl.BlockSpec((LANES, tp), lambda h, i: (h, 0))
    stage = lambda dt: pltpu.VMEM((ATT_SUBS, 2, Q_BLOCK, K_WINDOW), dt)
    return _call(body, name, (d // LANES, t // (ATT_SUBS * Q_BLOCK)),
                 [qspec, kspec, kspec, bspec, qspec, qspec, qspec],
                 [qspec, tspec, tspec, bspec],
                 [jax.ShapeDtypeStruct((t, d), F32),
                  jax.ShapeDtypeStruct((d, tp), F32),
                  jax.ShapeDtypeStruct((d, tp), F32),
                  jax.ShapeDtypeStruct((ATT_HEADS, Q_BLOCK, K_WINDOW), F32)],
                 [stage(F32), stage(F32), stage(BF16), stage(BF16),
                  pltpu.VMEM((ATT_SUBS, 2, 2, Q_BLOCK, LANES), F32)],
                 ("parallel", "arbitrary"), (q, kp, vp, bias, do, o, lse), rider)


def _rel_bin_matrix():
    rows = REL_DELTAS * 2 * REL_BLK
    rho = lax.broadcasted_iota(jnp.int32, (rows, REL_PAD), 0)
    col = lax.broadcasted_iota(jnp.int32, (rows, REL_PAD), 1)
    assert 2 * REL_BLK == 256
    delta = rho >> 8
    c = 255 - (rho & 255)
    dist = K_PAD + REL_BLK * (delta - (K_WINDOW // REL_BLK - 1)) + (c - (REL_BLK - 1))
    idx = jnp.clip(dist, -REL_CLIP, REL_CLIP) + REL_CLIP
    return col == idx


def _rel_expand(rel_pad, name):
    heads = rel_pad.shape[0]
    rows = REL_DELTAS * 2 * REL_BLK

    def body_bin(r_ref, o_ref):
        onehot = jnp.where(_rel_bin_matrix(), 1.0, 0.0).astype(BF16)
        hi, mid, lo = _split3(r_ref[...])
        o_ref[...] = _dot_nt(hi, onehot) + _dot_nt(mid, onehot) + _dot_nt(lo, onehot)

    by_delta = pl.pallas_call(
        body_bin, name=name + "_bin",
        out_shape=jax.ShapeDtypeStruct((heads, rows), F32),
        compiler_params=pltpu.CompilerParams(vmem_limit_bytes=VMEM_LIMIT_V7X),
    )(rel_pad)
    by_delta = by_delta.reshape(heads * REL_DELTAS, 2 * REL_BLK)

    def body_shift(t_ref, o_ref):
        tv = t_ref[...]
        for r in range(REL_BLK):
            o_ref[r] = pltpu.roll(tv, (r + REL_BLK) % (2 * REL_BLK), 1)[:, :REL_BLK]

    return pl.pallas_call(
        body_shift, name=name + "_shift",
        out_shape=jax.ShapeDtypeStruct((REL_BLK, heads * REL_DELTAS, REL_BLK), F32),
        compiler_params=pltpu.CompilerParams(vmem_limit_bytes=VMEM_LIMIT_V7X),
    )(by_delta)


def _bias_table(rel_bias, name):
    heads = rel_bias.shape[0]
    rel_pad = jnp.pad(rel_bias, ((0, 0), (0, REL_PAD - REL_TABLE)))
    tiles = _rel_expand(rel_pad, name)
    tiles = tiles.reshape(REL_BLK, heads, REL_DELTAS, REL_BLK).transpose(1, 2, 0, 3)
    na, nb = Q_BLOCK // REL_BLK, K_WINDOW // REL_BLK
    rows = [jnp.concatenate([tiles[:, a - b + nb - 1] for b in range(nb)], axis=-1) for a in range(na)]
    table = jnp.concatenate(rows, axis=-2)
    qc = np.arange(Q_BLOCK)[:, None] // CHUNK
    kc = np.arange(K_WINDOW)[None, :] // CHUNK
    band = (kc >= qc) & (kc <= qc + PAST_CHUNKS)
    return jnp.where(jnp.asarray(band)[None], table, NEG)


def _rel_reduce(db, name):
    heads = db.shape[0]
    na, nb = Q_BLOCK // REL_BLK, K_WINDOW // REL_BLK

    fold_heads = 4

    def body_fold(db_ref, g_ref):
        for hd in range(fold_heads):
            for delta in range(REL_DELTAS):
                acc = None
                for a in range(na):
                    b = a - (delta - (nb - 1))
                    if 0 <= b < nb:
                        tile = db_ref[hd, a * REL_BLK:(a + 1) * REL_BLK, b * REL_BLK:(b + 1) * REL_BLK]
                        acc = tile if acc is None else acc + tile
                g_ref[hd, delta] = acc

    folded = pl.pallas_call(
        body_fold, name=name + "_fold", grid=(heads // fold_heads,),
        in_specs=[pl.BlockSpec((fold_heads, Q_BLOCK, K_WINDOW), lambda h: (h, 0, 0))],
        out_specs=pl.BlockSpec((fold_heads, REL_DELTAS, REL_BLK, REL_BLK), lambda h: (h, 0, 0, 0)),
        out_shape=jax.ShapeDtypeStruct((heads, REL_DELTAS, REL_BLK, REL_BLK), F32),
        compiler_params=_params("parallel"),
    )(db)
    by_row = folded.transpose(2, 0, 1, 3).reshape(REL_BLK, heads * REL_DELTAS, REL_BLK)

    def body_diag(g_ref, d_ref):
        zeros = jnp.zeros((heads * REL_DELTAS, REL_BLK), F32)
        acc = None
        for r in range(REL_BLK):
            part = pltpu.roll(jnp.concatenate([g_ref[r], zeros], axis=1), REL_BLK - r, 1)
            acc = part if acc is None else acc + part
        d_ref[...] = acc

    diag = pl.pallas_call(
        body_diag, name=name + "_diag",
        out_shape=jax.ShapeDtypeStruct((heads * REL_DELTAS, 2 * REL_BLK), F32),
        compiler_params=pltpu.CompilerParams(vmem_limit_bytes=VMEM_LIMIT_V7X),
    )(by_row)
    diag = diag.reshape(heads, REL_DELTAS * 2 * REL_BLK)

    def body_bin(d_ref, o_ref):
        onehot = jnp.where(_rel_bin_matrix(), 1.0, 0.0).astype(BF16)
        hi, mid, lo = _split3(d_ref[...])
        o_ref[...] = _dot(hi, onehot) + _dot(mid, onehot) + _dot(lo, onehot)

    out = pl.pallas_call(
        body_bin, name=name + "_bin",
        out_shape=jax.ShapeDtypeStruct((heads, REL_PAD), F32),
        compiler_params=pltpu.CompilerParams(vmem_limit_bytes=VMEM_LIMIT_V7X),
    )(diag)
    return out[:, :REL_TABLE]


def _sum_leading(x, name):
    n, r, c = x.shape
    tr = _pick(r, 256, 8)

    def body(x_ref, o_ref):
        acc = x_ref[0].astype(F32)
        for k in range(1, n):
            acc = acc + x_ref[k].astype(F32)
        o_ref[...] = acc

    return pl.pallas_call(
        body, name=name, grid=(r // tr,),
        in_specs=[pl.BlockSpec((n, tr, c), lambda i: (0, i, 0))],
        out_specs=pl.BlockSpec((tr, c), lambda i: (i, 0)),
        out_shape=jax.ShapeDtypeStruct((r, c), F32),
        compiler_params=_params("parallel"),
    )(x)


def _pair_add(g, recv, parity, name):
    _, r, c = g.shape
    tr = _pick(r, 256, 16)

    def body(par_ref, g_ref, r_ref, o_ref):
        o_ref[...] = (g_ref[...].astype(F32) + r_ref[...].astype(F32)).astype(BF16)

    return pl.pallas_call(
        body, name=name,
        grid_spec=pltpu.PrefetchScalarGridSpec(
            num_scalar_prefetch=1, grid=(4, r // tr),
            in_specs=[pl.BlockSpec((1, tr, c), lambda k, i, par: (2 * k + par[0], i, 0)),
                      pl.BlockSpec((1, tr, c), lambda k, i, par: (k, i, 0))],
            out_specs=pl.BlockSpec((1, tr, c), lambda k, i, par: (k, i, 0))),
        out_shape=jax.ShapeDtypeStruct((4, r, c), BF16),
        compiler_params=_params("parallel", "parallel"),
    )(parity, g, recv)


def _adamw(w, g_parts, m, v, name):
    r, c = w.shape
    n = g_parts.shape[0]
    tr = _pick(r, 256, 16 if g_parts.dtype == BF16 else 8)
    c1 = 1.0 - ADAM_B1 ** ADAM_STEP
    c2 = 1.0 - ADAM_B2 ** ADAM_STEP

    def body(w_ref, g_ref, m_ref, v_ref, go_ref, d_ref, nm_ref, nv_ref):
        gv = g_ref[0].astype(F32)
        for k in range(1, n):
            gv = gv + g_ref[k].astype(F32)
        nm = ADAM_B1 * m_ref[...] + (1.0 - ADAM_B1) * gv
        nv = ADAM_B2 * v_ref[...] + (1.0 - ADAM_B2) * (gv * gv)
        go_ref[...] = gv
        d_ref[...] = -ADAM_LR * ((nm / c1) / (jnp.sqrt(nv / c2) + ADAM_EPS) + ADAM_WD * w_ref[...])
        nm_ref[...] = nm
        nv_ref[...] = nv

    spec = pl.BlockSpec((tr, c), lambda i: (i, 0))
    shp = jax.ShapeDtypeStruct((r, c), F32)
    return pl.pallas_call(
        body, name=name, grid=(r // tr,),
        in_specs=[spec, pl.BlockSpec((n, tr, c), lambda i: (0, i, 0)), spec, spec],
        out_specs=[spec] * 4, out_shape=[shp] * 4,
        compiler_params=_params("parallel"),
    )(w, g_parts, m, v)


BIG = (("a_w_in", 1), ("a_w_o", 0), ("a_w_gu", 0), ("a_w_down", 0), ("w_kv", 1),
       ("b_w_q", 0), ("b_w_o", 0), ("b_w_gu", 0), ("b_w_down", 0))
TRANSPOSED = ("a_w_gu", "b_w_gu")
FFN_BLK = 2 * FFN_HIDDEN // N_DEV

SMALL = (("a_norm_g", D_MODEL, True), ("a_gn_g", RET_V_COLS, True), ("a_ffn_norm_g", D_MODEL, True),
         ("kv_norm_g", D_MODEL, False), ("b_norm_g", D_MODEL, False), ("b_ffn_norm_g", D_MODEL, False),
         ("k_norm_g", ATT_DH, False), ("b_q_norm_g", ATT_DH, False),
         ("b_rel_bias", ATT_HEADS * REL_TABLE, False))
SMALL_ROWS, SMALL_COLS = 16, 1024


def _pack_small(vals, last=None):
    flat = jnp.concatenate([vals[n].reshape(-1) for n, _, _ in SMALL])
    room = SMALL_ROWS * SMALL_COLS - flat.shape[0]
    if last is None:
        flat = jnp.pad(flat, (0, room))
    else:
        flat = jnp.concatenate([jnp.pad(flat, (0, room - 1)), last.reshape(1)])
    return flat.reshape(SMALL_ROWS, SMALL_COLS)


def _unpack_small(packed, local):
    flat, out, pos = packed.reshape(-1), {}, 0
    for n, length, sharded in SMALL:
        ln = length // N_DEV if (local and sharded) else length
        out[n] = flat[pos:pos + ln]
        pos += ln
    return out


def _gather_rider(shards, names, late_relay=False):
    return _GatherRider([shards[n] for n in names], late_relay)


def _gathered(rider, names, axis_of):
    return {n: (r.reshape(-1, r.shape[2]) if axis_of[n] == 0 else r) for n, r in zip(names, rider.results)}


def _blocks(g):
    return g if g.ndim == 3 else g.reshape(N_DEV, -1, g.shape[-1])


def _local_step(x, target, shards, s, parity):
    t = x.shape[0]
    axis_of = dict(BIG)
    consts = _ret_consts(t)
    lane_to_head = np.zeros((D_MODEL, LANES), np.float32)
    lane_to_head[np.arange(D_MODEL), np.arange(D_MODEL) // ATT_DH] = 1.0
    bd = jnp.asarray(lane_to_head).astype(BF16)
    kg_t = jnp.tile(s["k_norm_g"], (1, ATT_HEADS))
    qg_t = jnp.tile(s["b_q_norm_g"], (1, ATT_HEADS))
    q_scale = ATT_DH ** -0.5
    w, g, recv = {}, {}, {}

    def gather_on(names, late_relay=False):
        return _gather_rider(shards, names, late_relay), names

    def landed(ride):
        w.update(_gathered(ride[0], ride[1], axis_of))

    def scatter_on(names):
        return _ScatterRider([_blocks(g[n]) for n in names]), names

    def reduced(ride):
        recv.update(zip(ride[1], ride[0].results))

    proj, (w["a_w_in"], w_o) = _proj_gather(x, s["a_norm_g"], shards["a_w_in"], [shards["a_w_o"]], "a_proj")
    w["a_w_o"] = w_o.reshape(-1, w_o.shape[2])
    ride = gather_on(["a_w_gu"], late_relay=True)
    y, o_ret, states = _ret_fwd(proj, s["a_gn_g"], consts, "a_ret", rider=ride[0])
    landed(ride)
    ride = gather_on(["w_kv"], late_relay=True)
    x1 = _mm(y, w["a_w_o"], "nn", "a_out", res=x, rider=ride[0])
    landed(ride)
    ride = gather_on(["a_w_down", "b_w_q", "b_w_o"])
    gu_a, act_a = _mm(x1, w["a_w_gu"], "nt", "a_ffn_gu", epilogue="swiglu", out_block=FFN_BLK,
                      norm_g=s["a_ffn_norm_g"], rider=ride[0])
    landed(ride)
    x2 = _mm(act_a, w["a_w_down"], "nn", "a_ffn_down", res=x1)

    kv = _mm(x2, w["w_kv"], "nn", "kv_proj", norm_g=s["kv_norm_g"])
    kp, vp = _kv_prep(kv, kg_t, bd, "kv_prep")

    q_raw = _mm(x2, w["b_w_q"], "nn", "b_q", norm_g=s["b_norm_g"])
    qn = _q_hnorm(q_raw, qg_t, bd, q_scale, "q_hnorm")
    bias = _bias_table(s["b_rel_bias"].reshape(ATT_HEADS, REL_TABLE), "rel")
    ride = gather_on(["b_w_gu", "b_w_down"])
    o_att, lse = _att_fwd(qn, kp, vp, bias, "b_att", rider=ride[0])
    landed(ride)
    x3 = _mm(o_att, w["b_w_o"], "nn", "b_out", res=x2)
    gu_b, act_b = _mm(x3, w["b_w_gu"], "nt", "b_ffn_gu", epilogue="swiglu", out_block=FFN_BLK,
                      norm_g=s["b_ffn_norm_g"])
    dy, loss = _mm(act_b, w["b_w_down"], "nn", "b_ffn_down", res=x3, epilogue="loss", extra=(target,))
    in_blk, kv_blk, ffn_blk = w["a_w_in"].shape[2], w["w_kv"].shape[2], FFN_BLK

    dgu = _mm(dy, w["b_w_down"], "nt", "b_ffn_dgu", out_block=ffn_blk, epilogue="swiglu_bwd", extra=gu_b)
    dgu = dgu.reshape(N_DEV, t, ffn_blk)
    g["b_w_down"] = _mm(act_b, dy, "tn", "b_ffn_gdown", out_dtype=BF16)
    ride = scatter_on(["b_w_down"])
    dx3, g["b_ffn_norm_g"] = _mm(dgu, w["b_w_gu"], "nn", "b_ffn_dh", epilogue="rms_bwd",
                                 extra=(x3, s["b_ffn_norm_g"], dy), rider=ride[0])
    reduced(ride)
    g["b_w_gu"] = _mm(dgu, x3, "tn", "b_ffn_ggu", out_dtype=BF16, norm_g=s["b_ffn_norm_g"], norm_b=True)

    do_att = _mm(dx3, w["b_w_o"], "nt", "b_dout", out_dtype=BF16)
    g["b_w_o"] = _mm(o_att, dx3, "tn", "b_gout", out_dtype=BF16)
    ride = scatter_on(["b_w_gu", "b_w_o"])
    dq, dkp, dvp, db = _att_bwd(qn, kp, vp, bias, do_att, o_att, lse, "b_datt", rider=ride[0])
    reduced(ride)
    g["b_rel_bias"] = _rel_reduce(db, "drel").reshape(1, -1)
    dq_raw, gq = _q_dhnorm(q_raw, qg_t, bd, dq, q_scale, "q_dhnorm")
    g["b_q_norm_g"] = gq.reshape(ATT_HEADS, ATT_DH).sum(axis=0, keepdims=True)
    g["b_w_q"] = _mm(x2, dq_raw, "tn", "b_gq", out_dtype=BF16, norm_g=s["b_norm_g"])
    dx2, g["b_norm_g"] = _mm(dq_raw, w["b_w_q"], "nt", "b_dq", epilogue="rms_bwd",
                             extra=(x2, s["b_norm_g"], dx3))

    dkv, gk = _kv_dprep(kv, kg_t, bd, dkp, dvp, "kv_dprep")
    g["k_norm_g"] = gk.reshape(ATT_HEADS, ATT_DH).sum(axis=0, keepdims=True)
    g["w_kv"] = _mm(x2, dkv, "tn", "kv_g", out_dtype=BF16, out_block=kv_blk, norm_g=s["kv_norm_g"])
    dx2, g["kv_norm_g"] = _mm(dkv, w["w_kv"], "nt", "kv_du", epilogue="rms_bwd",
                              extra=(x2, s["kv_norm_g"], dx2))

    ride = scatter_on(["b_w_q"])
    dgu = _mm(dx2, w["a_w_down"], "nt", "a_ffn_dgu", out_block=ffn_blk, epilogue="swiglu_bwd", extra=gu_a,
              rider=ride[0])
    reduced(ride)
    dgu = dgu.reshape(N_DEV, t, ffn_blk)
    g["a_w_down"] = _mm(act_a, dx2, "tn", "a_ffn_gdown", out_dtype=BF16)
    ride = scatter_on(["a_w_down"])
    dx1, g["a_ffn_norm_g"] = _mm(dgu, w["a_w_gu"], "nn", "a_ffn_dh", epilogue="rms_bwd",
                                 extra=(x1, s["a_ffn_norm_g"], dx2), rider=ride[0])
    reduced(ride)
    ride = scatter_on(["w_kv"])
    g["a_w_gu"] = _mm(dgu, x1, "tn", "a_ffn_ggu", out_dtype=BF16, norm_g=s["a_ffn_norm_g"], norm_b=True,
                      rider=ride[0])
    reduced(ride)

    swap = _SiblingSwapRider([_blocks(g["a_w_gu"])])
    dy_ret = _mm(dx1, w["a_w_o"], "nt", "a_dout", rider=swap)
    g["a_w_o"] = _mm(y, dx1, "tn", "a_gout", out_dtype=BF16)
    chips = _ChipScatterRider([_pair_add(_blocks(g["a_w_gu"]), swap.results[0], parity, "rs_pair_add_gu")])
    dproj, g["a_gn_g"] = _ret_bwd(proj, s["a_gn_g"], o_ret, states, dy_ret, consts, "a_dret", rider=chips)
    recv["a_w_gu"] = chips.results[0]
    ride = scatter_on(["a_w_o"])
    g["a_w_in"] = _mm(x, dproj, "tn", "a_gin", out_dtype=BF16, out_block=in_blk, norm_g=s["a_norm_g"],
                      rider=ride[0])
    reduced(ride)
    from_sibling = _exchange(_SiblingSwapRider([g["a_w_in"]]), "rs_sibling")[0]
    chip_sums = _pair_add(g["a_w_in"], from_sibling, parity, "rs_pair_add")
    last = _ChipScatterRider([chip_sums])
    grad_x, g["a_norm_g"] = _mm(dproj, w["a_w_in"], "nt", "a_dproj", epilogue="rms_bwd",
                                extra=(x, s["a_norm_g"], dx1), rider=last)
    recv["a_w_in"] = last.results[0]
    return loss, grad_x, recv, g


ARG_NAMES = ("x", "a_norm_g", "a_w_in", "a_gn_g", "a_w_o", "a_ffn_norm_g", "a_w_gu", "a_w_down",
             "kv_norm_g", "w_kv", "k_norm_g", "b_norm_g", "b_w_q", "b_q_norm_g", "b_rel_bias", "b_w_o",
             "b_ffn_norm_g", "b_w_gu", "b_w_down")
WEIGHT_NAMES = ARG_NAMES[1:]


def _big_shard(a, name):
    a = a[0] if a.ndim == 3 else a
    return a.T if name in TRANSPOSED else a


def _as_given(a, name, shape):
    return (a.T if name in TRANSPOSED else a).reshape(shape)


def kernel(x, a_norm_g, a_w_in, a_gn_g, a_w_o, a_ffn_norm_g, a_w_gu, a_w_down, kv_norm_g, w_kv, k_norm_g, b_norm_g, b_w_q, b_q_norm_g, b_rel_bias, b_w_o, b_ffn_norm_g, b_w_gu, b_w_down, loss_target, m_a_norm_g, m_a_w_in, m_a_gn_g, m_a_w_o, m_a_ffn_norm_g, m_a_w_gu, m_a_w_down, m_kv_norm_g, m_w_kv, m_k_norm_g, m_b_norm_g, m_b_w_q, m_b_q_norm_g, m_b_rel_bias, m_b_w_o, m_b_ffn_norm_g, m_b_w_gu, m_b_w_down, v_a_norm_g, v_a_w_in, v_a_gn_g, v_a_w_o, v_a_ffn_norm_g, v_a_w_gu, v_a_w_down, v_kv_norm_g, v_w_kv, v_k_norm_g, v_b_norm_g, v_b_w_q, v_b_q_norm_g, v_b_rel_bias, v_b_w_o, v_b_ffn_norm_g, v_b_w_gu, v_b_w_down):
    args = (x, a_norm_g, a_w_in, a_gn_g, a_w_o, a_ffn_norm_g, a_w_gu, a_w_down, kv_norm_g, w_kv, k_norm_g,
            b_norm_g, b_w_q, b_q_norm_g, b_rel_bias, b_w_o, b_ffn_norm_g, b_w_gu, b_w_down)
    p = dict(zip(ARG_NAMES, args))
    m_all = dict(zip(WEIGHT_NAMES, (m_a_norm_g, m_a_w_in, m_a_gn_g, m_a_w_o, m_a_ffn_norm_g, m_a_w_gu,
                                    m_a_w_down, m_kv_norm_g, m_w_kv, m_k_norm_g, m_b_norm_g, m_b_w_q,
                                    m_b_q_norm_g, m_b_rel_bias, m_b_w_o, m_b_ffn_norm_g, m_b_w_gu, m_b_w_down)))
    v_all = dict(zip(WEIGHT_NAMES, (v_a_norm_g, v_a_w_in, v_a_gn_g, v_a_w_o, v_a_ffn_norm_g, v_a_w_gu,
                                    v_a_w_down, v_kv_norm_g, v_w_kv, v_k_norm_g, v_b_norm_g, v_b_w_q,
                                    v_b_q_norm_g, v_b_rel_bias, v_b_w_o, v_b_ffn_norm_g, v_b_w_gu, v_b_w_down)))
    xi, yi, ci = _my_place()
    me = 4 * xi + 2 * yi + ci
    big_names = [n for n, _ in BIG]

    big_local = {n: _big_shard(p[n], n) for n in big_names}
    shards = {n: a.astype(BF16) for n, a in big_local.items()}
    small_local = _pack_small({n: p[n] for n, _, _ in SMALL})
    small_all = _exchange(_GatherRider([small_local]), "gather_small")[0]
    flat_g = small_all.reshape(N_DEV, -1)
    s_full, pos = {}, 0
    for n, length, sharded in SMALL:
        ln = length // N_DEV if sharded else length
        s_full[n] = flat_g[:, pos:pos + ln].reshape(1, -1) if sharded else p[n].reshape(1, -1)
        pos += ln

    parity = jnp.reshape(ci, (1,)).astype(jnp.int32)
    loss, grad_x, recv, g = _local_step(x[0], loss_target[0], shards, s_full, parity)

    partial = _pack_small({n: g[n] for n, _, _ in SMALL}, last=loss)
    summed = _sum_leading(_exchange(_GatherRider([partial]), "gather_gsmall")[0], "gsmall_sum")
    loss = summed[SMALL_ROWS - 1, SMALL_COLS - 1]
    g_small = _unpack_small(summed, local=False)
    for n, length, sharded in SMALL:
        if sharded:
            g_small[n] = lax.dynamic_slice(g_small[n], (me * (length // N_DEV),), (length // N_DEV,))

    grads, deltas, new_m, new_v = {}, {}, {}, {}
    for n in big_names:
        outs = _adamw(big_local[n], recv[n], _big_shard(m_all[n], n), _big_shard(v_all[n], n), "adamw_" + n)
        grads[n], deltas[n], new_m[n], new_v[n] = (_as_given(a, n, p[n].shape) for a in outs)
    pk = lambda src: _pack_small({n: src[n] for n, _, _ in SMALL})
    outs = _adamw(small_local, pk(g_small)[None], pk(m_all), pk(v_all), "adamw_small")
    g_s, d_s, nm_s, nv_s = (_unpack_small(a, local=True) for a in outs)
    for n, _, _ in SMALL:
        grads[n], deltas[n], new_m[n], new_v[n] = (a[n].reshape(p[n].shape) for a in (g_s, d_s, nm_s, nv_s))

    return (loss, grad_x[None], *[grads[n] for n in WEIGHT_NAMES], *[deltas[n] for n in WEIGHT_NAMES],
            *[new_m[n] for n in WEIGHT_NAMES], *[new_v[n] for n in WEIGHT_NAMES])
```

```python
import numpy as np
import jax
import jax.numpy as jnp
from jax import lax
from jax.experimental import pallas as pl
from jax.experimental.pallas import tpu as pltpu

F32 = jnp.float32
BF16 = jnp.bfloat16

N_DEV = 8
D_MODEL = 1024
CHUNK = 64
EPS = 1e-6
RET_HEADS, RET_DK, RET_DV = 4, 256, 512
RET_STEP = 4
RET_Q_COLS = RET_HEADS * RET_DK
RET_V_COLS = RET_HEADS * RET_DV
ATT_HEADS, ATT_DH = 16, 64
PAST_CHUNKS = 8
REL_CLIP = 256
REL_TABLE = 2 * REL_CLIP + 1
FFN_HIDDEN = 2816
ROPE_BASE = 10000.0
LANES = 128
Q_BLOCK = 256
ATT_SUBS = 4
ATT_ROWS = 32
K_PAD = PAST_CHUNKS * CHUNK
K_WINDOW = Q_BLOCK + K_PAD
REL_BLK = 128
REL_DELTAS = Q_BLOCK // REL_BLK + K_WINDOW // REL_BLK - 1
REL_PAD = 640
NEG = -1e30
VMEM_LIMIT_V7X = 56 * 1024 * 1024
ADAM_LR, ADAM_B1, ADAM_B2, ADAM_EPS, ADAM_WD, ADAM_STEP = 1e-3, 0.9, 0.999, 1e-8, 0.01, 10
MESH = pl.DeviceIdType.MESH
ANY = pl.BlockSpec(memory_space=pl.ANY)


def _params(*semantics):
    return pltpu.CompilerParams(dimension_semantics=semantics, vmem_limit_bytes=VMEM_LIMIT_V7X)


def _pick(dim, cap, align):
    best = None
    for t in range(align, min(dim, cap) + 1, align):
        if dim % t == 0:
            best = t
    assert best is not None, (dim, cap, align)
    return best


def _dot(a, b):
    return lax.dot_general(a, b, (((1,), (0,)), ((), ())), preferred_element_type=F32)


def _dot_nt(a, b):
    return lax.dot_general(a, b, (((1,), (1,)), ((), ())), preferred_element_type=F32)


def _dot_tn(a, b):
    return lax.dot_general(a, b, (((0,), (0,)), ((), ())), preferred_element_type=F32)


def _split2(x):
    hi = x.astype(BF16)
    lo = (x - hi.astype(F32)).astype(BF16)
    return hi, lo


def _split3(x):
    hi = x.astype(BF16)
    r = x - hi.astype(F32)
    mid = r.astype(BF16)
    lo = (r - mid.astype(F32)).astype(BF16)
    return hi, mid, lo


def _sigmoid(x):
    return 1.0 / (1.0 + jnp.exp(-x))


def _accumulate(ref, part, step):
    @pl.when(step == 0)
    def _():
        ref[...] = part

    @pl.when(step > 0)
    def _():
        ref[...] += part


RELAY_AT_NUM, RELAY_AT_DEN = 3, 4


def _my_place():
    return lax.axis_index("x"), lax.axis_index("y"), lax.axis_index("c")


def _flip(v, bit):
    return 1 - v if bit else v


class _NoRelay:
    def relay(self, in_refs, out_refs, sems):
        pass


class _GatherRider:
    def __init__(self, xs):
        self.inputs = list(xs)
        n = len(xs)
        self.out_shape = [jax.ShapeDtypeStruct((N_DEV,) + x.shape, x.dtype) for x in xs]
        self.scratch = [pltpu.SemaphoreType.DMA((7, n)), pltpu.SemaphoreType.DMA((7, n)),
                        pltpu.SemaphoreType.DMA((n,))]
        self.results = None

    def _copies(self, x_refs, out_refs, sems):
        send_sems, recv_sems, local_sems = sems
        n = len(x_refs)
        x, y, c = _my_place()
        me, sibling = (x, y, c), (x, y, 1 - c)
        chips = [(1 - x, y), (x, 1 - y), (1 - x, 1 - y)]

        def slot(a, px, py, pc):
            return out_refs[a].at[4 * px + 2 * py + pc]

        def copy(k, a, block, to, own=False):
            return pltpu.make_async_remote_copy(
                src_ref=x_refs[a] if own else slot(a, *block), dst_ref=slot(a, *block),
                send_sem=send_sems.at[k, a], recv_sem=recv_sems.at[k, a],
                device_id=to, device_id_type=MESH)

        mine = [pltpu.make_async_copy(x_refs[a], slot(a, *me), local_sems.at[a]) for a in range(n)]
        first = []
        for a in range(n):
            first.append(copy(0, a, me, sibling, own=True))
            first += [copy(1 + j, a, me, (*chip, c), own=True) for j, chip in enumerate(chips)]
        return n, c, me, sibling, chips, copy, mine, first

    def start(self, x_refs, out_refs, sems):
        _, _, _, _, _, _, mine, first = self._copies(x_refs, out_refs, sems)
        for cp in mine + first:
            cp.start()

    def relay(self, x_refs, out_refs, sems):
        n, c, me, sibling, chips, copy, _, _ = self._copies(x_refs, out_refs, sems)
        for j, chip in enumerate(chips):
            for a in range(n):
                copy(1 + j, a, (*chip, c), me).wait_recv()
                copy(4 + j, a, (*chip, c), sibling).start()

    def finish(self, x_refs, out_refs, sems):
        n, c, me, sibling, chips, copy, mine, first = self._copies(x_refs, out_refs, sems)
        passed = [copy(4 + j, a, (*chip, c), sibling) for j, chip in enumerate(chips) for a in range(n)]
        for a in range(n):
            copy(0, a, sibling, me).wait_recv()
            for j, chip in enumerate(chips):
                copy(4 + j, a, (*chip, 1 - c), me).wait_recv()
        for cp in first + passed:
            cp.wait_send()
        for cp in mine:
            cp.wait()


class _ScatterRider(_NoRelay):
    def __init__(self, gs):
        self.inputs = list(gs)
        n = len(gs)
        self.out_shape = [jax.ShapeDtypeStruct(g.shape, g.dtype) for g in gs]
        self.scratch = [pltpu.SemaphoreType.DMA((7, n)), pltpu.SemaphoreType.DMA((7, n)),
                        pltpu.SemaphoreType.DMA((n,))]
        self.results = None

    def _copies(self, g_refs, out_refs, sems):
        send_sems, recv_sems, local_sems = sems
        x, y, c = _my_place()
        me = 4 * x + 2 * y + c
        mine, copies = [], []
        for a in range(len(g_refs)):
            mine.append(pltpu.make_async_copy(g_refs[a].at[me], out_refs[a].at[me], local_sems.at[a]))
            for k in range(1, N_DEV):
                px, py, pc = _flip(x, k & 4), _flip(y, k & 2), _flip(c, k & 1)
                copies.append(pltpu.make_async_remote_copy(
                    src_ref=g_refs[a].at[4 * px + 2 * py + pc], dst_ref=out_refs[a].at[me],
                    send_sem=send_sems.at[k - 1, a], recv_sem=recv_sems.at[k - 1, a],
                    device_id=(px, py, pc), device_id_type=MESH))
        return mine, copies

    def start(self, g_refs, out_refs, sems):
        mine, copies = self._copies(g_refs, out_refs, sems)
        for cp in mine + copies:
            cp.start()

    def finish(self, g_refs, out_refs, sems):
        mine, copies = self._copies(g_refs, out_refs, sems)
        for cp in copies + mine:
            cp.wait()


class _SiblingSwapRider(_NoRelay):
    def __init__(self, gs):
        self.inputs = list(gs)
        n = len(gs)
        self.out_shape = [jax.ShapeDtypeStruct((4,) + g.shape[1:], g.dtype) for g in gs]
        self.scratch = [pltpu.SemaphoreType.DMA((4, n)), pltpu.SemaphoreType.DMA((4, n))]
        self.results = None

    def _copies(self, g_refs, out_refs, sems):
        send_sems, recv_sems = sems
        x, y, c = _my_place()
        return [pltpu.make_async_remote_copy(
            src_ref=g_refs[a].at[2 * k + 1 - c], dst_ref=out_refs[a].at[k],
            send_sem=send_sems.at[k, a], recv_sem=recv_sems.at[k, a],
            device_id=(x, y, 1 - c), device_id_type=MESH)
            for a in range(len(g_refs)) for k in range(4)]

    def start(self, g_refs, out_refs, sems):
        for cp in self._copies(g_refs, out_refs, sems):
            cp.start()

    def finish(self, g_refs, out_refs, sems):
        for cp in self._copies(g_refs, out_refs, sems):
            cp.wait()


class _ChipScatterRider(_NoRelay):
    def __init__(self, ps):
        self.inputs = list(ps)
        n = len(ps)
        self.out_shape = [jax.ShapeDtypeStruct(p.shape, p.dtype) for p in ps]
        self.scratch = [pltpu.SemaphoreType.DMA((3, n)), pltpu.SemaphoreType.DMA((3, n)),
                        pltpu.SemaphoreType.DMA((n,))]
        self.results = None

    def _copies(self, p_refs, out_refs, sems):
        send_sems, recv_sems, local_sems = sems
        x, y, c = _my_place()
        my_chip = 2 * x + y
        chips = [(1 - x, y), (x, 1 - y), (1 - x, 1 - y)]
        n = len(p_refs)
        mine = [pltpu.make_async_copy(p_refs[a].at[my_chip], out_refs[a].at[my_chip], local_sems.at[a])
                for a in range(n)]
        copies = [pltpu.make_async_remote_copy(
            src_ref=p_refs[a].at[2 * cx + cy], dst_ref=out_refs[a].at[my_chip],
            send_sem=send_sems.at[j, a], recv_sem=recv_sems.at[j, a],
            device_id=(cx, cy, c), device_id_type=MESH)
            for a in range(n) for j, (cx, cy) in enumerate(chips)]
        return mine, copies

    def start(self, p_refs, out_refs, sems):
        mine, copies = self._copies(p_refs, out_refs, sems)
        for cp in mine + copies:
            cp.start()

    def finish(self, p_refs, out_refs, sems):
        mine, copies = self._copies(p_refs, out_refs, sems)
        for cp in copies + mine:
            cp.wait()


def _call(body, name, grid, in_specs, out_specs, out_shape, scratch, semantics, args, rider=None):
    in_specs, out_specs, out_shape, scratch = list(in_specs), list(out_specs), list(out_shape), list(scratch)
    if rider is None:
        return list(pl.pallas_call(
            body, name=name, grid=grid, in_specs=in_specs, out_specs=out_specs, out_shape=out_shape,
            scratch_shapes=scratch, compiler_params=_params(*semantics))(*args))
    n_in, n_out, n_scr = len(in_specs), len(out_specs), len(scratch)
    r_in, r_out = len(rider.inputs), len(rider.out_shape)

    def wrapped(*refs):
        cuts = np.cumsum([0, n_in, r_in, n_out, r_out, n_scr])
        hi, ri, ho, ro, hs = (refs[cuts[i]:cuts[i + 1]] for i in range(5))
        rs = refs[cuts[5]:]
        step, steps = pl.program_id(0), grid[0]
        for d in range(1, len(grid)):
            step, steps = step * grid[d] + pl.program_id(d), steps * grid[d]

        @pl.when(step == 0)
        def _():
            rider.start(ri, ro, rs)

        body(*hi, *ho, *hs)

        @pl.when(step == (steps * RELAY_AT_NUM) // RELAY_AT_DEN)
        def _():
            rider.relay(ri, ro, rs)

        @pl.when(step == steps - 1)
        def _():
            rider.finish(ri, ro, rs)

    outs = pl.pallas_call(
        wrapped, name=name, grid=grid,
        in_specs=in_specs + [ANY] * r_in, out_specs=out_specs + [ANY] * r_out,
        out_shape=out_shape + rider.out_shape, scratch_shapes=scratch + rider.scratch,
        compiler_params=_params(*(["arbitrary"] * len(grid))),
    )(*args, *rider.inputs)
    rider.results = list(outs[n_out:])
    return list(outs[:n_out])


_WALK = ((None, None), (0, None), (1, 4), (2, 5), (4, None), (5, None), (3, 6), (6, None))


def _gather_order():
    x, y, c = _my_place()
    (ax, ay), (bx, by), (dx, dy) = (1 - x, y), (x, 1 - y), (1 - x, 1 - y)
    ids = [(x, y, c), (x, y, 1 - c), (ax, ay, c), (bx, by, c), (ax, ay, 1 - c), (bx, by, 1 - c),
           (dx, dy, c), (dx, dy, 1 - c)]
    return jnp.stack([4 * px + 2 * py + pc for px, py, pc in ids]).astype(jnp.int32)


def _proj_gather(x, norm_g, w_shard, extras, name):
    t, d = x.shape
    cols = w_shard.shape[1]
    tm = _pick(t, MM_CAP_MN, 16)
    ni = t // tm
    n = 1 + len(extras)
    rider = _GatherRider([w_shard] + list(extras))

    def body(ord_ref, x_ref, g_ref, *refs):
        sh_refs, proj_ref, gathered = refs[:n], refs[n], refs[n + 1:2 * n + 1]
        h_all, bbuf, bsem, send_sems, recv_sems, local_sems = refs[2 * n + 1:]
        j, i = pl.program_id(0), pl.program_id(1)
        _, c, me, sibling, chips, copy, mine, first = rider._copies(
            sh_refs, gathered, (send_sems, recv_sems, local_sems))
        rows = pl.ds(pl.multiple_of(i * tm, tm), tm)

        def load(step, src):
            return pltpu.make_async_copy(src, bbuf.at[step % 2], bsem.at[step % 2])

        def relayed(k, a):
            return copy(k, a, (*chips[k - 4], c), sibling)

        @pl.when(jnp.logical_and(j == 0, i == 0))
        def _():
            for cp in mine + first:
                cp.start()
            load(0, sh_refs[0]).start()

        @pl.when(i == 0)
        def _():
            load(j, sh_refs[0]).wait()

        @pl.when(j == 0)
        def _():
            groups = []
            for r in range(0, tm, NORM_ROWS):
                xv = x_ref[r:r + NORM_ROWS, :]
                rstd = lax.rsqrt(jnp.mean(xv * xv, axis=-1, keepdims=True) + EPS)
                groups.append((xv * rstd * g_ref[...]).astype(BF16))
            h_all[rows, :] = jnp.concatenate(groups, axis=0)

        proj_ref[...] = _dot(h_all[rows, :], bbuf[j % 2])

        for step in range(N_DEV - 1):
            @pl.when(jnp.logical_and(j == step, i == max(ni - 2, 0)))
            def _(step=step):
                need, relay = _WALK[step + 1]
                copy(need, 0, me, me).wait_recv()
                if relay is not None:
                    relayed(relay, 0).start()
                load(step + 1, gathered[0].at[ord_ref[step + 1]]).start()

        @pl.when(jnp.logical_and(j == N_DEV - 1, i == ni - 1))
        def _():
            for a in range(1, n):
                for k in range(3):
                    copy(1 + k, a, me, me).wait_recv()
                    relayed(4 + k, a).start()
            for a in range(1, n):
                for k in (0, 4, 5, 6):
                    copy(k, a, me, me).wait_recv()
            for cp in first + [relayed(4 + k, a) for a in range(n) for k in range(3)]:
                cp.wait_send()
            for cp in mine:
                cp.wait()

    outs = pl.pallas_call(
        body, name=name,
        grid_spec=pltpu.PrefetchScalarGridSpec(
            num_scalar_prefetch=1, grid=(N_DEV, ni),
            in_specs=[pl.BlockSpec((tm, d), lambda j, i, o: (jnp.where(j == 0, i, ni - 1), 0)),
                      pl.BlockSpec((1, d), lambda j, i, o: (0, 0))] + [ANY] * n,
            out_specs=[pl.BlockSpec((tm, cols), lambda j, i, o: (i, o[j]))] + [ANY] * n,
            scratch_shapes=[pltpu.VMEM((t, d), BF16), pltpu.VMEM((2, d, cols), BF16),
                            pltpu.SemaphoreType.DMA((2,))] + rider.scratch),
        out_shape=[jax.ShapeDtypeStruct((t, N_DEV * cols), F32)] + rider.out_shape,
        compiler_params=_params("arbitrary", "arbitrary"),
    )(_gather_order(), x, norm_g, w_shard, *extras)
    return outs[0], list(outs[1:])


def _exchange(rider, name):
    r_in, r_out = len(rider.inputs), len(rider.out_shape)

    def body(*refs):
        ri, ro, rs = refs[:r_in], refs[r_in:r_in + r_out], refs[r_in + r_out:]
        rider.start(ri, ro, rs)
        rider.relay(ri, ro, rs)
        rider.finish(ri, ro, rs)

    return list(pl.pallas_call(
        body, name=name, in_specs=[ANY] * r_in, out_specs=[ANY] * r_out,
        out_shape=rider.out_shape, scratch_shapes=rider.scratch)(*rider.inputs))


MM_CAP_MN = 1024
MM_CAP_M_GRAD = 1408
MM_CAP_N = 1536
MM_CAP_K = 3072
MM_CAP_K_TOKENS = 2048
MM_CAP_K_RMS = 8192
MM_CAP_M_RMS = 512
NORM_ROWS = 256


def _mm(a, b, mode, name, out_dtype=F32, res=None, out_block=None, epilogue=None, extra=None, norm_g=None,
        norm_b=False, rider=None):
    a3, b3 = a.ndim == 3, b.ndim == 3
    um = un = uk = None
    if mode in ("nn", "nt"):
        if a3:
            m, uk = a.shape[1:]
            k = a.shape[0] * uk
        else:
            m, k = a.shape
    else:
        if a3:
            k, um = a.shape[1:]
            m = a.shape[0] * um
        else:
            k, m = a.shape
    if mode in ("nn", "tn"):
        if b3:
            kb, un = b.shape[1:]
            n = b.shape[0] * un
        else:
            kb, n = b.shape
        assert kb == k, (a.shape, b.shape, mode)
    else:
        if b3:
            n, ukb = b.shape[1:]
            assert b.shape[0] * ukb == k and uk in (None, ukb), (a.shape, b.shape, mode)
            uk = ukb
        else:
            n, kb = b.shape
            assert kb == k, (a.shape, b.shape, mode)
    if out_block is not None:
        assert un in (None, out_block)
        un = out_block

    def tile(dim, unit, cap, align):
        if unit is None:
            return _pick(dim, cap, align), 1
        c = max(1, cap // unit)
        while (dim // unit) % c:
            c -= 1
        return unit, c

    cap_m = MM_CAP_M_GRAD if mode == "tn" else (MM_CAP_M_RMS if epilogue == "rms_bwd" else MM_CAP_MN)
    um, cm = tile(m, um, cap_m, 128 if mode == "tn" else 16)
    un, cn = tile(n, un, MM_CAP_N, 128)
    cap_k = MM_CAP_K_TOKENS if mode == "tn" else (MM_CAP_K_RMS if epilogue == "rms_bwd" else MM_CAP_K)
    uk, ck = tile(k, uk, cap_k, 128)
    if epilogue == "rms_bwd":
        assert mode != "tn" and n == D_MODEL and cm == cn == 1 and res is None and out_block is None
    if epilogue == "loss":
        assert n == D_MODEL and cm == cn == 1 and res is not None and out_block is None
    if norm_g is not None and norm_b:
        assert mode == "tn" and not b3 and n == D_MODEL and cn == 1
    elif norm_g is not None:
        assert not a3 and (m if mode == "tn" else k) == D_MODEL and (cm if mode == "tn" else ck) == 1
    if epilogue == "swiglu":
        assert res is None and ((mode == "nn" and b3 and out_block is None) or
                                (mode == "nt" and not b3 and out_block is not None))
        cn = 2
    if epilogue == "swiglu_bwd":
        assert mode == "nt" and out_block is not None and extra is not None and res is None
        cn = 1
    tm, tn, tk = cm * um, cn * un, ck * uk
    nk = k // tk
    dot = {"nn": _dot, "nt": _dot_nt, "tn": _dot_tn}[mode]
    half = n // un // 2
    blocked_out = out_block is not None or epilogue in ("swiglu", "swiglu_bwd")
    extras = [] if extra is None else (list(extra) if isinstance(extra, (tuple, list)) else [extra])

    def sl(idx, unit, count):
        return slice(None) if count == 1 else slice(idx * unit, (idx + 1) * unit)

    def body(*refs):
        a_ref, b_ref = refs[0], refs[1]
        pos = 2
        r_ref = ng_ref = None
        if res is not None:
            r_ref, pos = refs[pos], pos + 1
        e_refs, pos = refs[pos:pos + len(extras)], pos + len(extras)
        if norm_g is not None:
            ng_ref, pos = refs[pos], pos + 1
        outs, acc_ref = refs[pos:-1], refs[-1]
        kk = pl.program_id(2)

        def normed(x_ref):
            groups = []
            for r in range(0, x_ref.shape[0], NORM_ROWS):
                xv = x_ref[r:r + NORM_ROWS, :]
                rstd = lax.rsqrt(jnp.mean(xv * xv, axis=-1, keepdims=True) + EPS)
                groups.append((xv * rstd * ng_ref[...]).astype(BF16))
            return jnp.concatenate(groups, axis=0)

        def a_blk(mi, ki):
            if norm_g is not None and not norm_b:
                return normed(a_ref)
            if mode in ("nn", "nt"):
                return a_ref[ki] if a3 else a_ref[:, sl(ki, uk, ck)]
            return a_ref[mi] if a3 else a_ref[:, sl(mi, um, cm)]

        def b_blk(ki, ni):
            if norm_b:
                return normed(b_ref)
            if epilogue == "swiglu":
                return b_ref[ni, 0]
            if mode in ("nn", "tn"):
                return b_ref[ni] if b3 else b_ref[sl(ki, uk, ck), sl(ni, un, cn)]
            return b_ref[ki][sl(ni, un, cn), :] if b3 else b_ref[sl(ni, un, cn), sl(ki, uk, ck)]

        parts = {}
        for mi in range(cm):
            for ni in range(cn):
                part = None
                for ki in range(ck):
                    d = dot(a_blk(mi, ki).astype(BF16), b_blk(ki, ni).astype(BF16))
                    part = d if part is None else part + d
                parts[mi, ni] = part

        def finish(total):
            if epilogue == "swiglu":
                gate, up = total[0, 0], total[0, 1]
                outs[0][0, 0] = gate.astype(BF16)
                outs[0][1, 0] = up.astype(BF16)
                outs[1][0] = (gate * _sigmoid(gate) * up).astype(BF16)
                return
            if epilogue == "swiglu_bwd":
                dact = total[0, 0]
                gate, up = e_refs[0][0, 0].astype(F32), e_refs[0][1, 0].astype(F32)
                sg = _sigmoid(gate)
                outs[0][0, 0] = (dact * up * (sg * (1.0 + gate * (1.0 - sg)))).astype(BF16)
                outs[0][1, 0] = (dact * (gate * sg)).astype(BF16)
                return
            if epilogue == "rms_bwd":
                x_ref, g_ref, dres_ref = e_refs
                dh, dg = total[0, 0], None
                for r in range(0, tm, NORM_ROWS):
                    rows = slice(r, r + NORM_ROWS)
                    xv, dhv = x_ref[rows, :], dh[rows, :]
                    rstd = lax.rsqrt(jnp.mean(xv * xv, axis=-1, keepdims=True) + EPS)
                    xh = xv * rstd
                    dyg = dhv * g_ref[...]
                    c = jnp.mean(dyg * xh, axis=-1, keepdims=True)
                    outs[0][rows, :] = dres_ref[rows, :] + rstd * (dyg - xh * c)
                    part = jnp.sum(dhv * xh, axis=0, keepdims=True)
                    dg = part if dg is None else dg + part
                _accumulate(outs[1], dg, pl.program_id(0))
                return
            if epilogue == "loss":
                diff = r_ref[...] + total[0, 0] - e_refs[0][...]
                outs[0][...] = diff * (1.0 / n)
                sq = jnp.sum(jnp.sum(diff * diff, axis=-1, keepdims=True), axis=0, keepdims=True)
                _accumulate(outs[1], sq * (0.5 / n), pl.program_id(0))
                return
            for (mi, ni), val in total.items():
                rows, cols = sl(mi, um, cm), sl(ni, un, cn)
                if res is not None:
                    val = r_ref[rows, cols] + val
                if blocked_out:
                    outs[0][ni, rows] = val.astype(out_dtype)
                else:
                    outs[0][rows, cols] = val.astype(out_dtype)

        if nk == 1:
            finish(parts)
        else:
            @pl.when(kk == 0)
            def _():
                for (mi, ni), val in parts.items():
                    acc_ref[mi * cn + ni] = val

            @pl.when(jnp.logical_and(kk > 0, kk < nk - 1))
            def _():
                for (mi, ni), val in parts.items():
                    acc_ref[mi * cn + ni] += val

            @pl.when(kk == nk - 1)
            def _():
                finish({key: acc_ref[key[0] * cn + key[1]] + val for key, val in parts.items()})

    if mode in ("nn", "nt"):
        a_spec = (pl.BlockSpec((ck, tm, uk), lambda i, j, kk: (kk, i, 0)) if a3
                  else pl.BlockSpec((tm, tk), lambda i, j, kk: (i, kk)))
    else:
        a_spec = (pl.BlockSpec((cm, tk, um), lambda i, j, kk: (i, kk, 0)) if a3
                  else pl.BlockSpec((tk, tm), lambda i, j, kk: (kk, i)))
    pair_spec = pl.BlockSpec((2, 1, tm, un), lambda i, j, kk: (0, j, i, 0))
    row_spec = pl.BlockSpec((tm, tn), lambda i, j, kk: (i, 0))
    vec_spec = pl.BlockSpec((1, tn), lambda i, j, kk: (0, 0))
    if epilogue == "swiglu" and mode == "nn":
        b = b.reshape(2, half, k, un)
        b_spec = pl.BlockSpec((2, 1, tk, un), lambda i, j, kk: (0, j, kk, 0))
    elif epilogue == "swiglu":
        b = b.reshape(2, half, un, k)
        b_spec = pl.BlockSpec((2, 1, un, tk), lambda i, j, kk: (0, j, 0, kk))
    elif mode in ("nn", "tn"):
        b_spec = (pl.BlockSpec((cn, tk, un), lambda i, j, kk: (j, kk, 0)) if b3
                  else pl.BlockSpec((tk, tn), lambda i, j, kk: (kk, j)))
    else:
        b_spec = (pl.BlockSpec((ck, tn, uk), lambda i, j, kk: (kk, j, 0)) if b3
                  else pl.BlockSpec((tn, tk), lambda i, j, kk: (j, kk)))
    if epilogue == "swiglu":
        out_specs = [pair_spec, pl.BlockSpec((1, tm, un), lambda i, j, kk: (j, i, 0))]
        out_shape = [jax.ShapeDtypeStruct((2, half, m, un), BF16), jax.ShapeDtypeStruct((half, m, un), BF16)]
    elif epilogue == "swiglu_bwd":
        out_specs = [pair_spec]
        out_shape = [jax.ShapeDtypeStruct(extra.shape, BF16)]
    elif epilogue == "rms_bwd":
        out_specs = [row_spec, vec_spec]
        out_shape = [jax.ShapeDtypeStruct((m, n), F32), jax.ShapeDtypeStruct((1, n), F32)]
    elif epilogue == "loss":
        out_specs = [row_spec, pl.BlockSpec((1, 1), lambda i, j, kk: (0, 0))]
        out_shape = [jax.ShapeDtypeStruct((m, n), F32), jax.ShapeDtypeStruct((1, 1), F32)]
    elif blocked_out:
        out_specs = [pl.BlockSpec((cn, tm, un), lambda i, j, kk: (j, i, 0))]
        out_shape = [jax.ShapeDtypeStruct((n // un, m, un), out_dtype)]
    else:
        out_specs = [pl.BlockSpec((tm, tn), lambda i, j, kk: (i, j))]
        out_shape = [jax.ShapeDtypeStruct((m, n), out_dtype)]
    in_specs, args = [a_spec, b_spec], [a, b]
    if res is not None:
        in_specs.append(pl.BlockSpec((tm, tn), lambda i, j, kk: (i, j)))
        args.append(res)
    if epilogue == "swiglu_bwd":
        in_specs.append(pair_spec)
    elif epilogue == "rms_bwd":
        in_specs += [row_spec, vec_spec, row_spec]
    elif epilogue == "loss":
        in_specs.append(row_spec)
    args += extras
    if norm_g is not None:
        in_specs.append(pl.BlockSpec((1, D_MODEL), lambda i, j, kk: (0, 0)))
        args.append(norm_g)
    semantics = ("arbitrary",) * 3 if epilogue in ("rms_bwd", "loss") else ("parallel", "parallel", "arbitrary")
    out = _call(body, name, (m // tm, n // tn, nk), in_specs, out_specs, out_shape,
                [pltpu.VMEM((cm * cn, um, un), F32)], semantics, args, rider)
    return out if epilogue in ("swiglu", "rms_bwd", "loss") else out[0]


def _head_sums(v, ind):
    return _dot(v.astype(BF16), ind)


def _head_spread(per_head, ind):
    hi, lo = _split2(per_head)
    return _dot_nt(hi, ind) + _dot_nt(lo, ind)


def _head_rstd(xv, ind):
    return _head_spread(lax.rsqrt(_head_sums(xv * xv, ind) * (1.0 / ATT_DH) + EPS), ind)


def _hn_bwd_math(xv, gv, ind, dyv, scale):
    rstd = _head_rstd(xv, ind)
    xh = xv * rstd
    dyn = dyv * scale
    dyg = dyn * gv
    dx = rstd * (dyg - xh * _head_spread(_head_sums(dyg * xh, ind) * (1.0 / ATT_DH), ind))
    return dx, jnp.sum(dyn * xh, axis=0, keepdims=True)


def _q_hnorm(x, g_tiled, bd, scale, name):
    t, d = x.shape
    tm = _pick(t, 512, 16)

    def body(x_ref, g_ref, bd_ref, o_ref):
        xv = x_ref[...]
        o_ref[...] = (xv * _head_rstd(xv, bd_ref[...]) * g_ref[...] * scale).astype(BF16)

    return pl.pallas_call(
        body, name=name, grid=(t // tm,),
        in_specs=[pl.BlockSpec((tm, d), lambda i: (i, 0)), pl.BlockSpec((1, d), lambda i: (0, 0)),
                  pl.BlockSpec((d, LANES), lambda i: (0, 0))],
        out_specs=pl.BlockSpec((tm, d), lambda i: (i, 0)),
        out_shape=jax.ShapeDtypeStruct((t, d), BF16),
        compiler_params=_params("parallel"),
    )(x, g_tiled, bd)


def _q_dhnorm(x, g_tiled, bd, dy, scale, name):
    t, d = x.shape
    tm = _pick(t, 512, 16)

    def body(x_ref, g_ref, bd_ref, dy_ref, dx_ref, dg_ref):
        dx, part = _hn_bwd_math(x_ref[...], g_ref[...], bd_ref[...], dy_ref[...], scale)
        dx_ref[...] = dx.astype(BF16)
        _accumulate(dg_ref, part, pl.program_id(0))

    row = pl.BlockSpec((tm, d), lambda i: (i, 0))
    vec = pl.BlockSpec((1, d), lambda i: (0, 0))
    return pl.pallas_call(
        body, name=name, grid=(t // tm,),
        in_specs=[row, vec, pl.BlockSpec((d, LANES), lambda i: (0, 0)), row],
        out_specs=[row, vec],
        out_shape=[jax.ShapeDtypeStruct((t, d), BF16), jax.ShapeDtypeStruct((1, d), F32)],
        compiler_params=_params("arbitrary"),
    )(x, g_tiled, bd, dy)


def _kv_prep(kv, g_tiled, bd, name):
    t = kv.shape[0]
    d = D_MODEL
    tm = K_PAD
    assert t % tm == 0

    def body(k_ref, v_ref, g_ref, bd_ref, kp_ref, vp_ref):
        i = pl.program_id(0)

        @pl.when(i == 0)
        def _():
            kp_ref[...] = jnp.zeros_like(kp_ref)
            vp_ref[...] = jnp.zeros_like(vp_ref)

        @pl.when(i > 0)
        def _():
            xv = k_ref[...]
            kp_ref[...] = (xv * _head_rstd(xv, bd_ref[...]) * g_ref[...]).astype(BF16)
            vp_ref[...] = v_ref[...].astype(BF16)

    shp = jax.ShapeDtypeStruct((t + K_PAD, d), BF16)
    out = pl.BlockSpec((tm, d), lambda i: (i, 0))
    return pl.pallas_call(
        body, name=name, grid=(t // tm + 1,),
        in_specs=[pl.BlockSpec((tm, d), lambda i: (jnp.maximum(i - 1, 0), 0)),
                  pl.BlockSpec((tm, d), lambda i: (jnp.maximum(i - 1, 0), 1)),
                  pl.BlockSpec((1, d), lambda i: (0, 0)), pl.BlockSpec((d, LANES), lambda i: (0, 0))],
        out_specs=[out, out], out_shape=[shp, shp],
        compiler_params=_params("arbitrary"),
    )(kv, kv, g_tiled, bd)


def _kv_dprep(kv, g_tiled, bd, dkp_t, dvp_t, name):
    t = kv.shape[0]
    d = D_MODEL
    tm = K_PAD

    def body(k_ref, g_ref, bd_ref, dk_ref, dv_ref, o_ref, dg_ref):
        dx, part = _hn_bwd_math(k_ref[...], g_ref[...], bd_ref[...], dk_ref[...].T, 1.0)
        o_ref[:, :d] = dx.astype(BF16)
        o_ref[:, d:] = dv_ref[...].T.astype(BF16)
        _accumulate(dg_ref, part, pl.program_id(0))

    vec = pl.BlockSpec((1, d), lambda i: (0, 0))
    padded = pl.BlockSpec((d, tm), lambda i: (0, i + 1))
    return pl.pallas_call(
        body, name=name, grid=(t // tm,),
        in_specs=[pl.BlockSpec((tm, d), lambda i: (i, 0)), vec, pl.BlockSpec((d, LANES), lambda i: (0, 0)),
                  padded, padded],
        out_specs=[pl.BlockSpec((tm, 2 * d), lambda i: (i, 0)), vec],
        out_shape=[jax.ShapeDtypeStruct((t, 2 * d), BF16), jax.ShapeDtypeStruct((1, d), F32)],
        compiler_params=_params("arbitrary"),
    )(kv, g_tiled, bd, dkp_t, dvp_t)


def _ret_consts(t):
    h = np.arange(RET_HEADS, dtype=np.float32)
    lg = np.log(np.float32(1.0) - np.float32(2.0) ** (np.float32(-5.0) - h)).astype(np.float32)
    tt = np.arange(CHUNK, dtype=np.float32)
    intra = np.exp(lg[:, None, None] * np.abs(tt[:, None] - tt[None, :])).astype(np.float32)
    q_dec = np.exp(lg[:, None] * (tt + 1.0)).astype(np.float32)
    k_dec = np.exp(lg[:, None] * (CHUNK - 1.0 - tt)).astype(np.float32)
    s_dec = [float(v) for v in np.exp(lg * np.float32(CHUNK)).astype(np.float32)]
    qd = np.broadcast_to(q_dec[:, :, None], (RET_HEADS, CHUNK, RET_DK)).copy()
    kd = np.broadcast_to(k_dec[:, :, None], (RET_HEADS, CHUNK, RET_DK)).copy()
    half = RET_DK // 2
    inv_freq = ROPE_BASE ** (-jnp.arange(half, dtype=F32) / half)
    ang = jnp.arange(t).astype(F32)[:, None] * inv_freq[None, :]
    return jnp.asarray(intra), jnp.asarray(qd), jnp.asarray(kd), s_dec, jnp.cos(ang), jnp.sin(ang)


def _rope(x, cos, sin):
    half = RET_DK // 2
    x1, x2 = x[:, :half], x[:, half:]
    return jnp.concatenate([x1 * cos - x2 * sin, x1 * sin + x2 * cos], axis=-1)


def _unrope(d, cos, sin):
    half = RET_DK // 2
    d1, d2 = d[:, :half], d[:, half:]
    return jnp.concatenate([d1 * cos + d2 * sin, d2 * cos - d1 * sin], axis=-1)


def _ret_slices(h):
    q = slice(h * RET_DK, (h + 1) * RET_DK)
    k = slice(RET_Q_COLS + h * RET_DK, RET_Q_COLS + (h + 1) * RET_DK)
    v = slice(2 * RET_Q_COLS + h * RET_DV, 2 * RET_Q_COLS + (h + 1) * RET_DV)
    g = slice(2 * RET_Q_COLS + RET_V_COLS + h * RET_DV, 2 * RET_Q_COLS + RET_V_COLS + (h + 1) * RET_DV)
    o = slice(h * RET_DV, (h + 1) * RET_DV)
    return q, k, v, g, o


def _ret_fwd(proj, gn, consts, name, rider=None):
    t, cols = proj.shape
    n = t // CHUNK
    intra, qd, kd, s_dec, cos, sin = consts
    k_scale = RET_DK ** -0.5

    def body(p_ref, cos_ref, sin_ref, intra_ref, qd_ref, kd_ref, gn_ref, y_ref, o_ref, st_ref, state):
        i = pl.program_id(0)

        @pl.when(i == 0)
        def _():
            state[...] = jnp.zeros_like(state)

        for c in range(RET_STEP):
            rows = slice(c * CHUNK, (c + 1) * CHUNK)
            cosv, sinv = cos_ref[rows, :], sin_ref[rows, :]
            for h in range(RET_HEADS):
                qs, ks, vs, gs, os_ = _ret_slices(h)
                qr = _rope(p_ref[rows, qs], cosv, sinv)
                kr = _rope(p_ref[rows, ks], cosv, sinv) * k_scale
                vb = p_ref[rows, vs].astype(BF16)
                gv = p_ref[rows, gs]
                scores = _dot_nt(qr.astype(BF16), kr.astype(BF16)) * intra_ref[h]
                s_old = state[h]
                s_old_b = s_old.astype(BF16)
                st_ref[c, h] = s_old_b
                o = _dot(scores.astype(BF16), vb) + _dot((qr * qd_ref[h]).astype(BF16), s_old_b)
                state[h] = s_old * s_dec[h] + _dot_tn((kr * kd_ref[h]).astype(BF16), vb)
                rstd = lax.rsqrt(jnp.mean(o * o, axis=-1, keepdims=True) + EPS)
                on = o * rstd * gn_ref[:, os_]
                o_ref[rows, os_] = o
                y_ref[rows, os_] = (gv * _sigmoid(gv) * on).astype(BF16)

    full3 = lambda a: pl.BlockSpec(a.shape, lambda i: (0, 0, 0))
    step = RET_STEP * CHUNK
    return _call(
        body, name, (n // RET_STEP,),
        [pl.BlockSpec((step, cols), lambda i: (i, 0)),
         pl.BlockSpec((step, RET_DK // 2), lambda i: (i, 0)),
         pl.BlockSpec((step, RET_DK // 2), lambda i: (i, 0)),
         full3(intra), full3(qd), full3(kd),
         pl.BlockSpec((1, RET_V_COLS), lambda i: (0, 0))],
        [pl.BlockSpec((step, RET_V_COLS), lambda i: (i, 0)),
         pl.BlockSpec((step, RET_V_COLS), lambda i: (i, 0)),
         pl.BlockSpec((RET_STEP, RET_HEADS, RET_DK, RET_DV), lambda i: (i, 0, 0, 0))],
        [jax.ShapeDtypeStruct((t, RET_V_COLS), BF16),
         jax.ShapeDtypeStruct((t, RET_V_COLS), F32),
         jax.ShapeDtypeStruct((n, RET_HEADS, RET_DK, RET_DV), BF16)],
        [pltpu.VMEM((RET_HEADS, RET_DK, RET_DV), F32)], ("arbitrary",),
        (proj, cos, sin, intra, qd, kd, gn), rider)


def _ret_bwd(proj, gn, o_saved, states, dy, consts, name, rider=None):
    t, cols = proj.shape
    n = t // CHUNK
    intra, qd, kd, s_dec, cos, sin = consts
    k_scale = RET_DK ** -0.5

    def body(p_ref, cos_ref, sin_ref, intra_ref, qd_ref, kd_ref, gn_ref, o_ref, st_ref, dy_ref,
             dp_ref, dgn_ref, dstate):
        i = pl.program_id(0)

        @pl.when(i == 0)
        def _():
            dstate[...] = jnp.zeros_like(dstate)

        dgn = None
        for c in reversed(range(RET_STEP)):
            rows = slice(c * CHUNK, (c + 1) * CHUNK)
            cosv, sinv = cos_ref[rows, :], sin_ref[rows, :]
            dgn_parts = []
            for h in range(RET_HEADS):
                qs, ks, vs, gs, os_ = _ret_slices(h)
                qr = _rope(p_ref[rows, qs], cosv, sinv)
                kr = _rope(p_ref[rows, ks], cosv, sinv) * k_scale
                qb, kb = qr.astype(BF16), kr.astype(BF16)
                vb = p_ref[rows, vs].astype(BF16)
                gv = p_ref[rows, gs]
                ov = o_ref[rows, os_]
                dyv = dy_ref[rows, os_]
                gnv = gn_ref[:, os_]
                sg = _sigmoid(gv)
                rstd = lax.rsqrt(jnp.mean(ov * ov, axis=-1, keepdims=True) + EPS)
                oh = ov * rstd
                d_on = dyv * (gv * sg)
                dg = dyv * (oh * gnv) * (sg * (1.0 + gv * (1.0 - sg)))
                dgn_parts.append(jnp.sum(d_on * oh, axis=0, keepdims=True))
                d_oh = d_on * gnv
                do = rstd * (d_oh - oh * jnp.mean(d_oh * oh, axis=-1, keepdims=True))
                dob = do.astype(BF16)
                mask = intra_ref[h]
                a_b = (_dot_nt(qb, kb) * mask).astype(BF16)
                da_b = (_dot_nt(dob, vb) * mask).astype(BF16)
                ds_new = dstate[h]
                ds_new_b = ds_new.astype(BF16)
                s_old_b = st_ref[c, h]
                qdv, kdv = qd_ref[h], kd_ref[h]
                dv = _dot_tn(a_b, dob) + _dot((kr * kdv).astype(BF16), ds_new_b)
                dqr = _dot(da_b, kb) + _dot_nt(dob, s_old_b) * qdv
                dkr = _dot_tn(da_b, qb) + _dot_nt(vb, ds_new_b) * kdv
                dstate[h] = ds_new * s_dec[h] + _dot_tn((qr * qdv).astype(BF16), dob)
                dp_ref[rows, qs] = _unrope(dqr, cosv, sinv).astype(BF16)
                dp_ref[rows, ks] = _unrope(dkr * k_scale, cosv, sinv).astype(BF16)
                dp_ref[rows, vs] = dv.astype(BF16)
                dp_ref[rows, gs] = dg.astype(BF16)
            part = jnp.concatenate(dgn_parts, axis=-1)
            dgn = part if dgn is None else dgn + part
        _accumulate(dgn_ref, dgn, i)

    steps = n // RET_STEP
    step = RET_STEP * CHUNK
    rev = lambda i: (steps - 1 - i, 0)
    full3 = lambda a: pl.BlockSpec(a.shape, lambda i: (0, 0, 0))
    return _call(
        body, name, (steps,),
        [pl.BlockSpec((step, cols), rev),
         pl.BlockSpec((step, RET_DK // 2), rev),
         pl.BlockSpec((step, RET_DK // 2), rev),
         full3(intra), full3(qd), full3(kd),
         pl.BlockSpec((1, RET_V_COLS), lambda i: (0, 0)),
         pl.BlockSpec((step, RET_V_COLS), rev),
         pl.BlockSpec((RET_STEP, RET_HEADS, RET_DK, RET_DV), lambda i: (steps - 1 - i, 0, 0, 0)),
         pl.BlockSpec((step, RET_V_COLS), rev)],
        [pl.BlockSpec((step, cols), rev),
         pl.BlockSpec((1, RET_V_COLS), lambda i: (0, 0))],
        [jax.ShapeDtypeStruct((t, cols), BF16),
         jax.ShapeDtypeStruct((1, RET_V_COLS), F32)],
        [pltpu.VMEM((RET_HEADS, RET_DK, RET_DV), F32)], ("arbitrary",),
        (proj, cos, sin, intra, qd, kd, gn, o_saved, states, dy), rider)


def _att_common(q_ref, kp_ref, vp_ref, sub):
    blk = pl.program_id(1) * ATT_SUBS + sub
    start = pl.multiple_of(blk * Q_BLOCK, Q_BLOCK)
    kw = kp_ref[pl.ds(start, K_WINDOW), :]
    vw = vp_ref[pl.ds(start, K_WINDOW), :]
    kvalid = blk * Q_BLOCK - K_PAD + lax.broadcasted_iota(jnp.int32, (1, K_WINDOW), 1) >= 0
    lane = lax.broadcasted_iota(jnp.int32, (1, LANES), 1)
    qrows = slice(sub * Q_BLOCK, (sub + 1) * Q_BLOCK)
    return start, qrows, q_ref[qrows, :], kw, vw, kvalid, (lane < ATT_DH, lane >= ATT_DH)


def _row_groups():
    return [slice(r * ATT_ROWS, (r + 1) * ATT_ROWS) for r in range(Q_BLOCK // ATT_ROWS)]


def _lane_copies(x):
    return jnp.tile(x, (1, K_WINDOW // LANES))


def _att_specs(t, tp):
    qspec = pl.BlockSpec((ATT_SUBS * Q_BLOCK, LANES), lambda h, i: (i, h))
    kspec = pl.BlockSpec((tp, LANES), lambda h, i: (0, h))
    bspec = pl.BlockSpec((2, Q_BLOCK, K_WINDOW), lambda h, i: (h, 0, 0))
    return qspec, kspec, bspec


def _att_fwd(q, kp, vp, bias, name, rider=None):
    t, d = q.shape
    tp = kp.shape[0]

    def body(q_ref, kp_ref, vp_ref, bias_ref, o_ref, lse_ref, s_scr, p_scr, lse_scr):
        for sub in range(ATT_SUBS):
            _, qrows, q2, kw, vw, kvalid, sel = _att_common(q_ref, kp_ref, vp_ref, sub)
            for hh in range(2):
                s_scr[sub, hh] = _dot_nt(jnp.where(sel[hh], q2, 0), kw)
            for hh in range(2):
                for rows in _row_groups():
                    s = jnp.where(kvalid, s_scr[sub, hh, rows, :] + bias_ref[hh, rows, :], NEG)
                    m = jnp.max(s, axis=-1, keepdims=True)
                    e = jnp.exp(s - m)
                    l = jnp.sum(e, axis=-1, keepdims=True)
                    p_scr[sub, hh, rows, :] = (e * (1.0 / l)).astype(BF16)
                    lse_scr[sub, hh, rows, :] = jnp.broadcast_to(m + jnp.log(l), (ATT_ROWS, LANES))
            outs = [_dot(p_scr[sub, hh], vw) for hh in range(2)]
            o_ref[qrows, :] = jnp.where(sel[0], outs[0], outs[1]).astype(BF16)
            lse_ref[qrows, :] = jnp.where(sel[0], lse_scr[sub, 0], lse_scr[sub, 1])

    qspec, kspec, bspec = _att_specs(t, tp)
    return _call(body, name, (d // LANES, t // (ATT_SUBS * Q_BLOCK)), [qspec, kspec, kspec, bspec], [qspec, qspec],
                 [jax.ShapeDtypeStruct((t, d), BF16), jax.ShapeDtypeStruct((t, d), F32)],
                 [pltpu.VMEM((ATT_SUBS, 2, Q_BLOCK, K_WINDOW), F32),
                  pltpu.VMEM((ATT_SUBS, 2, Q_BLOCK, K_WINDOW), BF16),
                  pltpu.VMEM((ATT_SUBS, 2, Q_BLOCK, LANES), F32)],
                 ("parallel", "arbitrary"), (q, kp, vp, bias), rider)


def _att_bwd(q, kp, vp, bias, do, o, lse, name, rider=None):
    t, d = q.shape
    tp = kp.shape[0]

    def body(q_ref, kp_ref, vp_ref, bias_ref, do_ref, o_ref, lse_ref, dq_ref, dkp_ref, dvp_ref, db_ref,
             s_scr, dp_scr, p_scr, ds_scr, row_scr):
        @pl.when(pl.program_id(1) == 0)
        def _():
            dkp_ref[...] = jnp.zeros_like(dkp_ref)
            dvp_ref[...] = jnp.zeros_like(dvp_ref)
            db_ref[...] = jnp.zeros_like(db_ref)

        for sub in range(ATT_SUBS):
            start, qrows, q2, kw, vw, kvalid, sel = _att_common(q_ref, kp_ref, vp_ref, sub)
            do2 = do_ref[qrows, :]
            qm = [jnp.where(sel[hh], q2, 0) for hh in range(2)]
            dom = [jnp.where(sel[hh], do2, 0) for hh in range(2)]
            do_o = do2.astype(F32) * o_ref[qrows, :].astype(F32)
            lse2 = lse_ref[qrows, :]
            for hh in range(2):
                s_scr[sub, hh] = _dot_nt(qm[hh], kw)
                dp_scr[sub, hh] = _dot_nt(dom[hh], vw)
                lse_h = jnp.max(jnp.where(sel[hh], lse2, NEG), axis=-1, keepdims=True)
                delta = jnp.sum(jnp.where(sel[hh], do_o, 0.0), axis=-1, keepdims=True)
                row_scr[sub, hh, 0] = jnp.broadcast_to(lse_h, (Q_BLOCK, LANES))
                row_scr[sub, hh, 1] = jnp.broadcast_to(delta, (Q_BLOCK, LANES))
            for hh in range(2):
                for rows in _row_groups():
                    s = jnp.where(kvalid, s_scr[sub, hh, rows, :] + bias_ref[hh, rows, :], NEG)
                    p = jnp.exp(s - _lane_copies(row_scr[sub, hh, 0, rows, :]))
                    ds = p * (dp_scr[sub, hh, rows, :] - _lane_copies(row_scr[sub, hh, 1, rows, :]))
                    db_ref[hh, rows, :] += ds
                    p_scr[sub, hh, rows, :] = p.astype(BF16)
                    ds_scr[sub, hh, rows, :] = ds.astype(BF16)
            dqs = [_dot(ds_scr[sub, hh], kw) for hh in range(2)]
            dq_ref[qrows, :] = jnp.where(sel[0], dqs[0], dqs[1])
            dkp_ref[:, pl.ds(start, K_WINDOW)] += (_dot_tn(qm[0], ds_scr[sub, 0]) +
                                                   _dot_tn(qm[1], ds_scr[sub, 1]))
            dvp_ref[:, pl.ds(start, K_WINDOW)] += (_dot_tn(dom[0], p_scr[sub, 0]) +
                                                   _dot_tn(dom[1], p_scr[sub, 1]))

    qspec, kspec, bspec = _att_specs(t, tp)
    tspec = pl.BlockSpec((LANES, tp), lambda h, i: (h, 0))
    stage = lambda dt: pltpu.VMEM((ATT_SUBS, 2, Q_BLOCK, K_WINDOW), dt)
    return _call(body, name, (d // LANES, t // (ATT_SUBS * Q_BLOCK)),
                 [qspec, kspec, kspec, bspec, qspec, qspec, qspec],
                 [qspec, tspec, tspec, bspec],
                 [jax.ShapeDtypeStruct((t, d), F32),
                  jax.ShapeDtypeStruct((d, tp), F32),
                  jax.ShapeDtypeStruct((d, tp), F32),
                  jax.ShapeDtypeStruct((ATT_HEADS, Q_BLOCK, K_WINDOW), F32)],
                 [stage(F32), stage(F32), stage(BF16), stage(BF16),
                  pltpu.VMEM((ATT_SUBS, 2, 2, Q_BLOCK, LANES), F32)],
                 ("parallel", "arbitrary"), (q, kp, vp, bias, do, o, lse), rider)


def _rel_bin_matrix():
    rows = REL_DELTAS * 2 * REL_BLK
    rho = lax.broadcasted_iota(jnp.int32, (rows, REL_PAD), 0)
    col = lax.broadcasted_iota(jnp.int32, (rows, REL_PAD), 1)
    assert 2 * REL_BLK == 256
    delta = rho >> 8
    c = 255 - (rho & 255)
    dist = K_PAD + REL_BLK * (delta - (K_WINDOW // REL_BLK - 1)) + (c - (REL_BLK - 1))
    idx = jnp.clip(dist, -REL_CLIP, REL_CLIP) + REL_CLIP
    return col == idx


def _rel_expand(rel_pad, name):
    heads = rel_pad.shape[0]
    rows = REL_DELTAS * 2 * REL_BLK

    def body_bin(r_ref, o_ref):
        onehot = jnp.where(_rel_bin_matrix(), 1.0, 0.0).astype(BF16)
        hi, mid, lo = _split3(r_ref[...])
        o_ref[...] = _dot_nt(hi, onehot) + _dot_nt(mid, onehot) + _dot_nt(lo, onehot)

    by_delta = pl.pallas_call(
        body_bin, name=name + "_bin",
        out_shape=jax.ShapeDtypeStruct((heads, rows), F32),
        compiler_params=pltpu.CompilerParams(vmem_limit_bytes=VMEM_LIMIT_V7X),
    )(rel_pad)
    by_delta = by_delta.reshape(heads * REL_DELTAS, 2 * REL_BLK)

    def body_shift(t_ref, o_ref):
        tv = t_ref[...]
        for r in range(REL_BLK):
            o_ref[r] = pltpu.roll(tv, (r + REL_BLK) % (2 * REL_BLK), 1)[:, :REL_BLK]

    return pl.pallas_call(
        body_shift, name=name + "_shift",
        out_shape=jax.ShapeDtypeStruct((REL_BLK, heads * REL_DELTAS, REL_BLK), F32),
        compiler_params=pltpu.CompilerParams(vmem_limit_bytes=VMEM_LIMIT_V7X),
    )(by_delta)


def _bias_table(rel_bias, name):
    heads = rel_bias.shape[0]
    rel_pad = jnp.pad(rel_bias, ((0, 0), (0, REL_PAD - REL_TABLE)))
    tiles = _rel_expand(rel_pad, name)
    tiles = tiles.reshape(REL_BLK, heads, REL_DELTAS, REL_BLK).transpose(1, 2, 0, 3)
    na, nb = Q_BLOCK // REL_BLK, K_WINDOW // REL_BLK
    rows = [jnp.concatenate([tiles[:, a - b + nb - 1] for b in range(nb)], axis=-1) for a in range(na)]
    table = jnp.concatenate(rows, axis=-2)
    qc = np.arange(Q_BLOCK)[:, None] // CHUNK
    kc = np.arange(K_WINDOW)[None, :] // CHUNK
    band = (kc >= qc) & (kc <= qc + PAST_CHUNKS)
    return jnp.where(jnp.asarray(band)[None], table, NEG)


def _rel_reduce(db, name):
    heads = db.shape[0]
    na, nb = Q_BLOCK // REL_BLK, K_WINDOW // REL_BLK

    fold_heads = 4

    def body_fold(db_ref, g_ref):
        for hd in range(fold_heads):
            for delta in range(REL_DELTAS):
                acc = None
                for a in range(na):
                    b = a - (delta - (nb - 1))
                    if 0 <= b < nb:
                        tile = db_ref[hd, a * REL_BLK:(a + 1) * REL_BLK, b * REL_BLK:(b + 1) * REL_BLK]
                        acc = tile if acc is None else acc + tile
                g_ref[hd, delta] = acc

    folded = pl.pallas_call(
        body_fold, name=name + "_fold", grid=(heads // fold_heads,),
        in_specs=[pl.BlockSpec((fold_heads, Q_BLOCK, K_WINDOW), lambda h: (h, 0, 0))],
        out_specs=pl.BlockSpec((fold_heads, REL_DELTAS, REL_BLK, REL_BLK), lambda h: (h, 0, 0, 0)),
        out_shape=jax.ShapeDtypeStruct((heads, REL_DELTAS, REL_BLK, REL_BLK), F32),
        compiler_params=_params("parallel"),
    )(db)
    by_row = folded.transpose(2, 0, 1, 3).reshape(REL_BLK, heads * REL_DELTAS, REL_BLK)

    def body_diag(g_ref, d_ref):
        zeros = jnp.zeros((heads * REL_DELTAS, REL_BLK), F32)
        acc = None
        for r in range(REL_BLK):
            part = pltpu.roll(jnp.concatenate([g_ref[r], zeros], axis=1), REL_BLK - r, 1)
            acc = part if acc is None else acc + part
        d_ref[...] = acc

    diag = pl.pallas_call(
        body_diag, name=name + "_diag",
        out_shape=jax.ShapeDtypeStruct((heads * REL_DELTAS, 2 * REL_BLK), F32),
        compiler_params=pltpu.CompilerParams(vmem_limit_bytes=VMEM_LIMIT_V7X),
    )(by_row)
    diag = diag.reshape(heads, REL_DELTAS * 2 * REL_BLK)

    def body_bin(d_ref, o_ref):
        onehot = jnp.where(_rel_bin_matrix(), 1.0, 0.0).astype(BF16)
        hi, mid, lo = _split3(d_ref[...])
        o_ref[...] = _dot(hi, onehot) + _dot(mid, onehot) + _dot(lo, onehot)

    out = pl.pallas_call(
        body_bin, name=name + "_bin",
        out_shape=jax.ShapeDtypeStruct((heads, REL_PAD), F32),
        compiler_params=pltpu.CompilerParams(vmem_limit_bytes=VMEM_LIMIT_V7X),
    )(diag)
    return out[:, :REL_TABLE]


def _sum_leading(x, name):
    n, r, c = x.shape
    tr = _pick(r, 256, 8)

    def body(x_ref, o_ref):
        acc = x_ref[0].astype(F32)
        for k in range(1, n):
            acc = acc + x_ref[k].astype(F32)
        o_ref[...] = acc

    return pl.pallas_call(
        body, name=name, grid=(r // tr,),
        in_specs=[pl.BlockSpec((n, tr, c), lambda i: (0, i, 0))],
        out_specs=pl.BlockSpec((tr, c), lambda i: (i, 0)),
        out_shape=jax.ShapeDtypeStruct((r, c), F32),
        compiler_params=_params("parallel"),
    )(x)


def _pair_add(g, recv, parity, name):
    _, r, c = g.shape
    tr = _pick(r, 256, 16)

    def body(par_ref, g_ref, r_ref, o_ref):
        o_ref[...] = (g_ref[...].astype(F32) + r_ref[...].astype(F32)).astype(BF16)

    return pl.pallas_call(
        body, name=name,
        grid_spec=pltpu.PrefetchScalarGridSpec(
            num_scalar_prefetch=1, grid=(4, r // tr),
            in_specs=[pl.BlockSpec((1, tr, c), lambda k, i, par: (2 * k + par[0], i, 0)),
                      pl.BlockSpec((1, tr, c), lambda k, i, par: (k, i, 0))],
            out_specs=pl.BlockSpec((1, tr, c), lambda k, i, par: (k, i, 0))),
        out_shape=jax.ShapeDtypeStruct((4, r, c), BF16),
        compiler_params=_params("parallel", "parallel"),
    )(parity, g, recv)


def _adamw(w, g_parts, m, v, name):
    r, c = w.shape
    n = g_parts.shape[0]
    tr = _pick(r, 256, 16 if g_parts.dtype == BF16 else 8)
    c1 = 1.0 - ADAM_B1 ** ADAM_STEP
    c2 = 1.0 - ADAM_B2 ** ADAM_STEP

    def body(w_ref, g_ref, m_ref, v_ref, go_ref, d_ref, nm_ref, nv_ref):
        gv = g_ref[0].astype(F32)
        for k in range(1, n):
            gv = gv + g_ref[k].astype(F32)
        nm = ADAM_B1 * m_ref[...] + (1.0 - ADAM_B1) * gv
        nv = ADAM_B2 * v_ref[...] + (1.0 - ADAM_B2) * (gv * gv)
        go_ref[...] = gv
        d_ref[...] = -ADAM_LR * ((nm / c1) / (jnp.sqrt(nv / c2) + ADAM_EPS) + ADAM_WD * w_ref[...])
        nm_ref[...] = nm
        nv_ref[...] = nv

    spec = pl.BlockSpec((tr, c), lambda i: (i, 0))
    shp = jax.ShapeDtypeStruct((r, c), F32)
    return pl.pallas_call(
        body, name=name, grid=(r // tr,),
        in_specs=[spec, pl.BlockSpec((n, tr, c), lambda i: (0, i, 0)), spec, spec],
        out_specs=[spec] * 4, out_shape=[shp] * 4,
        compiler_params=_params("parallel"),
    )(w, g_parts, m, v)


BIG = (("a_w_in", 1), ("a_w_o", 0), ("a_w_gu", 0), ("a_w_down", 0), ("w_kv", 1),
       ("b_w_q", 0), ("b_w_o", 0), ("b_w_gu", 0), ("b_w_down", 0))
TRANSPOSED = ("a_w_gu", "b_w_gu")
FFN_BLK = 2 * FFN_HIDDEN // N_DEV

SMALL = (("a_norm_g", D_MODEL, True), ("a_gn_g", RET_V_COLS, True), ("a_ffn_norm_g", D_MODEL, True),
         ("kv_norm_g", D_MODEL, False), ("b_norm_g", D_MODEL, False), ("b_ffn_norm_g", D_MODEL, False),
         ("k_norm_g", ATT_DH, False), ("b_q_norm_g", ATT_DH, False),
         ("b_rel_bias", ATT_HEADS * REL_TABLE, False))
SMALL_ROWS, SMALL_COLS = 16, 1024


def _pack_small(vals, last=None):
    flat = jnp.concatenate([vals[n].reshape(-1) for n, _, _ in SMALL])
    room = SMALL_ROWS * SMALL_COLS - flat.shape[0]
    if last is None:
        flat = jnp.pad(flat, (0, room))
    else:
        flat = jnp.concatenate([jnp.pad(flat, (0, room - 1)), last.reshape(1)])
    return flat.reshape(SMALL_ROWS, SMALL_COLS)


def _unpack_small(packed, local):
    flat, out, pos = packed.reshape(-1), {}, 0
    for n, length, sharded in SMALL:
        ln = length // N_DEV if (local and sharded) else length
        out[n] = flat[pos:pos + ln]
        pos += ln
    return out


def _gather_rider(shards, names):
    return _GatherRider([shards[n] for n in names])


def _gathered(rider, names, axis_of):
    return {n: (r.reshape(-1, r.shape[2]) if axis_of[n] == 0 else r) for n, r in zip(names, rider.results)}


def _blocks(g):
    return g if g.ndim == 3 else g.reshape(N_DEV, -1, g.shape[-1])


def _local_step(x, target, shards, s, parity):
    t = x.shape[0]
    axis_of = dict(BIG)
    consts = _ret_consts(t)
    lane_to_head = np.zeros((D_MODEL, LANES), np.float32)
    lane_to_head[np.arange(D_MODEL), np.arange(D_MODEL) // ATT_DH] = 1.0
    bd = jnp.asarray(lane_to_head).astype(BF16)
    kg_t = jnp.tile(s["k_norm_g"], (1, ATT_HEADS))
    qg_t = jnp.tile(s["b_q_norm_g"], (1, ATT_HEADS))
    q_scale = ATT_DH ** -0.5
    w, g, recv = {}, {}, {}

    def gather_on(names):
        return _gather_rider(shards, names), names

    def landed(ride):
        w.update(_gathered(ride[0], ride[1], axis_of))

    def scatter_on(names):
        return _ScatterRider([_blocks(g[n]) for n in names]), names

    def reduced(ride):
        recv.update(zip(ride[1], ride[0].results))

    proj, (w["a_w_in"], w_o) = _proj_gather(x, s["a_norm_g"], shards["a_w_in"], [shards["a_w_o"]], "a_proj")
    w["a_w_o"] = w_o.reshape(-1, w_o.shape[2])
    ride = gather_on(["a_w_gu"])
    y, o_ret, states = _ret_fwd(proj, s["a_gn_g"], consts, "a_ret", rider=ride[0])
    landed(ride)
    ride = gather_on(["w_kv"])
    x1 = _mm(y, w["a_w_o"], "nn", "a_out", res=x, rider=ride[0])
    landed(ride)
    ride = gather_on(["a_w_down", "b_w_q", "b_w_o"])
    gu_a, act_a = _mm(x1, w["a_w_gu"], "nt", "a_ffn_gu", epilogue="swiglu", out_block=FFN_BLK,
                      norm_g=s["a_ffn_norm_g"], rider=ride[0])
    landed(ride)
    x2 = _mm(act_a, w["a_w_down"], "nn", "a_ffn_down", res=x1)

    kv = _mm(x2, w["w_kv"], "nn", "kv_proj", norm_g=s["kv_norm_g"])
    kp, vp = _kv_prep(kv, kg_t, bd, "kv_prep")

    q_raw = _mm(x2, w["b_w_q"], "nn", "b_q", norm_g=s["b_norm_g"])
    qn = _q_hnorm(q_raw, qg_t, bd, q_scale, "q_hnorm")
    bias = _bias_table(s["b_rel_bias"].reshape(ATT_HEADS, REL_TABLE), "rel")
    ride = gather_on(["b_w_gu", "b_w_down"])
    o_att, lse = _att_fwd(qn, kp, vp, bias, "b_att", rider=ride[0])
    landed(ride)
    x3 = _mm(o_att, w["b_w_o"], "nn", "b_out", res=x2)
    gu_b, act_b = _mm(x3, w["b_w_gu"], "nt", "b_ffn_gu", epilogue="swiglu", out_block=FFN_BLK,
                      norm_g=s["b_ffn_norm_g"])
    dy, loss = _mm(act_b, w["b_w_down"], "nn", "b_ffn_down", res=x3, epilogue="loss", extra=(target,))
    in_blk, kv_blk, ffn_blk = w["a_w_in"].shape[2], w["w_kv"].shape[2], FFN_BLK

    dgu = _mm(dy, w["b_w_down"], "nt", "b_ffn_dgu", out_block=ffn_blk, epilogue="swiglu_bwd", extra=gu_b)
    dgu = dgu.reshape(N_DEV, t, ffn_blk)
    g["b_w_down"] = _mm(act_b, dy, "tn", "b_ffn_gdown", out_dtype=BF16)
    ride = scatter_on(["b_w_down"])
    dx3, g["b_ffn_norm_g"] = _mm(dgu, w["b_w_gu"], "nn", "b_ffn_dh", epilogue="rms_bwd",
                                 extra=(x3, s["b_ffn_norm_g"], dy), rider=ride[0])
    reduced(ride)
    g["b_w_gu"] = _mm(dgu, x3, "tn", "b_ffn_ggu", out_dtype=BF16, norm_g=s["b_ffn_norm_g"], norm_b=True)

    do_att = _mm(dx3, w["b_w_o"], "nt", "b_dout", out_dtype=BF16)
    g["b_w_o"] = _mm(o_att, dx3, "tn", "b_gout", out_dtype=BF16)
    ride = scatter_on(["b_w_gu", "b_w_o"])
    dq, dkp, dvp, db = _att_bwd(qn, kp, vp, bias, do_att, o_att, lse, "b_datt", rider=ride[0])
    reduced(ride)
    g["b_rel_bias"] = _rel_reduce(db, "drel").reshape(1, -1)
    dq_raw, gq = _q_dhnorm(q_raw, qg_t, bd, dq, q_scale, "q_dhnorm")
    g["b_q_norm_g"] = gq.reshape(ATT_HEADS, ATT_DH).sum(axis=0, keepdims=True)
    g["b_w_q"] = _mm(x2, dq_raw, "tn", "b_gq", out_dtype=BF16, norm_g=s["b_norm_g"])
    dx2, g["b_norm_g"] = _mm(dq_raw, w["b_w_q"], "nt", "b_dq", epilogue="rms_bwd",
                             extra=(x2, s["b_norm_g"], dx3))

    dkv, gk = _kv_dprep(kv, kg_t, bd, dkp, dvp, "kv_dprep")
    g["k_norm_g"] = gk.reshape(ATT_HEADS, ATT_DH).sum(axis=0, keepdims=True)
    g["w_kv"] = _mm(x2, dkv, "tn", "kv_g", out_dtype=BF16, out_block=kv_blk, norm_g=s["kv_norm_g"])
    dx2, g["kv_norm_g"] = _mm(dkv, w["w_kv"], "nt", "kv_du", epilogue="rms_bwd",
                              extra=(x2, s["kv_norm_g"], dx2))

    ride = scatter_on(["b_w_q"])
    dgu = _mm(dx2, w["a_w_down"], "nt", "a_ffn_dgu", out_block=ffn_blk, epilogue="swiglu_bwd", extra=gu_a,
              rider=ride[0])
    reduced(ride)
    dgu = dgu.reshape(N_DEV, t, ffn_blk)
    g["a_w_down"] = _mm(act_a, dx2, "tn", "a_ffn_gdown", out_dtype=BF16)
    ride = scatter_on(["a_w_down"])
    dx1, g["a_ffn_norm_g"] = _mm(dgu, w["a_w_gu"], "nn", "a_ffn_dh", epilogue="rms_bwd",
                                 extra=(x1, s["a_ffn_norm_g"], dx2), rider=ride[0])
    reduced(ride)
    ride = scatter_on(["w_kv"])
    g["a_w_gu"] = _mm(dgu, x1, "tn", "a_ffn_ggu", out_dtype=BF16, norm_g=s["a_ffn_norm_g"], norm_b=True,
                      rider=ride[0])
    reduced(ride)

    swap = _SiblingSwapRider([_blocks(g["a_w_gu"])])
    dy_ret = _mm(dx1, w["a_w_o"], "nt", "a_dout", rider=swap)
    g["a_w_o"] = _mm(y, dx1, "tn", "a_gout", out_dtype=BF16)
    chips = _ChipScatterRider([_pair_add(_blocks(g["a_w_gu"]), swap.results[0], parity, "rs_pair_add_gu")])
    dproj, g["a_gn_g"] = _ret_bwd(proj, s["a_gn_g"], o_ret, states, dy_ret, consts, "a_dret", rider=chips)
    recv["a_w_gu"] = chips.results[0]
    ride = scatter_on(["a_w_o"])
    g["a_w_in"] = _mm(x, dproj, "tn", "a_gin", out_dtype=BF16, out_block=in_blk, norm_g=s["a_norm_g"],
                      rider=ride[0])
    reduced(ride)
    from_sibling = _exchange(_SiblingSwapRider([g["a_w_in"]]), "rs_sibling")[0]
    chip_sums = _pair_add(g["a_w_in"], from_sibling, parity, "rs_pair_add")
    last = _ChipScatterRider([chip_sums])
    grad_x, g["a_norm_g"] = _mm(dproj, w["a_w_in"], "nt", "a_dproj", epilogue="rms_bwd",
                                extra=(x, s["a_norm_g"], dx1), rider=last)
    recv["a_w_in"] = last.results[0]
    return loss, grad_x, recv, g


ARG_NAMES = ("x", "a_norm_g", "a_w_in", "a_gn_g", "a_w_o", "a_ffn_norm_g", "a_w_gu", "a_w_down",
             "kv_norm_g", "w_kv", "k_norm_g", "b_norm_g", "b_w_q", "b_q_norm_g", "b_rel_bias", "b_w_o",
             "b_ffn_norm_g", "b_w_gu", "b_w_down")
WEIGHT_NAMES = ARG_NAMES[1:]


def _big_shard(a, name):
    a = a[0] if a.ndim == 3 else a
    return a.T if name in TRANSPOSED else a


def _as_given(a, name, shape):
    return (a.T if name in TRANSPOSED else a).reshape(shape)


def kernel(x, a_norm_g, a_w_in, a_gn_g, a_w_o, a_ffn_norm_g, a_w_gu, a_w_down, kv_norm_g, w_kv, k_norm_g, b_norm_g, b_w_q, b_q_norm_g, b_rel_bias, b_w_o, b_ffn_norm_g, b_w_gu, b_w_down, loss_target, m_a_norm_g, m_a_w_in, m_a_gn_g, m_a_w_o, m_a_ffn_norm_g, m_a_w_gu, m_a_w_down, m_kv_norm_g, m_w_kv, m_k_norm_g, m_b_norm_g, m_b_w_q, m_b_q_norm_g, m_b_rel_bias, m_b_w_o, m_b_ffn_norm_g, m_b_w_gu, m_b_w_down, v_a_norm_g, v_a_w_in, v_a_gn_g, v_a_w_o, v_a_ffn_norm_g, v_a_w_gu, v_a_w_down, v_kv_norm_g, v_w_kv, v_k_norm_g, v_b_norm_g, v_b_w_q, v_b_q_norm_g, v_b_rel_bias, v_b_w_o, v_b_ffn_norm_g, v_b_w_gu, v_b_w_down):
    args = (x, a_norm_g, a_w_in, a_gn_g, a_w_o, a_ffn_norm_g, a_w_gu, a_w_down, kv_norm_g, w_kv, k_norm_g,
            b_norm_g, b_w_q, b_q_norm_g, b_rel_bias, b_w_o, b_ffn_norm_g, b_w_gu, b_w_down)
    p = dict(zip(ARG_NAMES, args))
    m_all = dict(zip(WEIGHT_NAMES, (m_a_norm_g, m_a_w_in, m_a_gn_g, m_a_w_o, m_a_ffn_norm_g, m_a_w_gu,
                                    m_a_w_down, m_kv_norm_g, m_w_kv, m_k_norm_g, m_b_norm_g, m_b_w_q,
                                    m_b_q_norm_g, m_b_rel_bias, m_b_w_o, m_b_ffn_norm_g, m_b_w_gu, m_b_w_down)))
    v_all = dict(zip(WEIGHT_NAMES, (v_a_norm_g, v_a_w_in, v_a_gn_g, v_a_w_o, v_a_ffn_norm_g, v_a_w_gu,
                                    v_a_w_down, v_kv_norm_g, v_w_kv, v_k_norm_g, v_b_norm_g, v_b_w_q,
                                    v_b_q_norm_g, v_b_rel_bias, v_b_w_o, v_b_ffn_norm_g, v_b_w_gu, v_b_w_down)))
    xi, yi, ci = _my_place()
    me = 4 * xi + 2 * yi + ci
    big_names = [n for n, _ in BIG]

    big_local = {n: _big_shard(p[n], n) for n in big_names}
    shards = {n: a.astype(BF16) for n, a in big_local.items()}
    small_local = _pack_small({n: p[n] for n, _, _ in SMALL})
    small_all = _exchange(_GatherRider([small_local]), "gather_small")[0]
    flat_g = small_all.reshape(N_DEV, -1)
    s_full, pos = {}, 0
    for n, length, sharded in SMALL:
        ln = length // N_DEV if sharded else length
        s_full[n] = flat_g[:, pos:pos + ln].reshape(1, -1) if sharded else p[n].reshape(1, -1)
        pos += ln

    parity = jnp.reshape(ci, (1,)).astype(jnp.int32)
    loss, grad_x, recv, g = _local_step(x[0], loss_target[0], shards, s_full, parity)

    partial = _pack_small({n: g[n] for n, _, _ in SMALL}, last=loss)
    summed = _sum_leading(_exchange(_GatherRider([partial]), "gather_gsmall")[0], "gsmall_sum")
    loss = summed[SMALL_ROWS - 1, SMALL_COLS - 1]
    g_small = _unpack_small(summed, local=False)
    for n, length, sharded in SMALL:
        if sharded:
            g_small[n] = lax.dynamic_slice(g_small[n], (me * (length // N_DEV),), (length // N_DEV,))

    grads, deltas, new_m, new_v = {}, {}, {}, {}
    for n in big_names:
        outs = _adamw(big_local[n], recv[n], _big_shard(m_all[n], n), _big_shard(v_all[n], n), "adamw_" + n)
        grads[n], deltas[n], new_m[n], new_v[n] = (_as_given(a, n, p[n].shape) for a in outs)
    pk = lambda src: _pack_small({n: src[n] for n, _, _ in SMALL})
    outs = _adamw(small_local, pk(g_small)[None], pk(m_all), pk(v_all), "adamw_small")
    g_s, d_s, nm_s, nv_s = (_unpack_small(a, local=True) for a in outs)
    for n, _, _ in SMALL:
        grads[n], deltas[n], new_m[n], new_v[n] = (a[n].reshape(p[n].shape) for a in (g_s, d_s, nm_s, nv_s))

    return (loss, grad_x[None], *[grads[n] for n in WEIGHT_NAMES], *[deltas[n] for n in WEIGHT_NAMES],
            *[new_m[n] for n in WEIGHT_NAMES], *[new_v[n] for n in WEIGHT_NAMES])
```

```python
import numpy as np
import jax
import jax.numpy as jnp
from jax import lax
from jax.experimental import pallas as pl
from jax.experimental.pallas import tpu as pltpu

F32 = jnp.float32
BF16 = jnp.bfloat16

N_DEV = 8
D_MODEL = 1024
CHUNK = 64
EPS = 1e-6
RET_HEADS, RET_DK, RET_DV = 4, 256, 512
RET_STEP = 4
RET_Q_COLS = RET_HEADS * RET_DK
RET_V_COLS = RET_HEADS * RET_DV
ATT_HEADS, ATT_DH = 16, 64
PAST_CHUNKS = 8
REL_CLIP = 256
REL_TABLE = 2 * REL_CLIP + 1
FFN_HIDDEN = 2816
ROPE_BASE = 10000.0
LANES = 128
Q_BLOCK = 256
ATT_SUBS = 4
ATT_ROWS = 32
K_PAD = PAST_CHUNKS * CHUNK
K_WINDOW = Q_BLOCK + K_PAD
REL_BLK = 128
REL_DELTAS = Q_BLOCK // REL_BLK + K_WINDOW // REL_BLK - 1
REL_PAD = 640
NEG = -1e30
VMEM_LIMIT_V7X = 56 * 1024 * 1024
ADAM_LR, ADAM_B1, ADAM_B2, ADAM_EPS, ADAM_WD, ADAM_STEP = 1e-3, 0.9, 0.999, 1e-8, 0.01, 10
MESH = pl.DeviceIdType.MESH
ANY = pl.BlockSpec(memory_space=pl.ANY)


def _params(*semantics):
    return pltpu.CompilerParams(dimension_semantics=semantics, vmem_limit_bytes=VMEM_LIMIT_V7X)


def _pick(dim, cap, align):
    best = None
    for t in range(align, min(dim, cap) + 1, align):
        if dim % t == 0:
            best = t
    assert best is not None, (dim, cap, align)
    return best


def _dot(a, b):
    return lax.dot_general(a, b, (((1,), (0,)), ((), ())), preferred_element_type=F32)


def _dot_nt(a, b):
    return lax.dot_general(a, b, (((1,), (1,)), ((), ())), preferred_element_type=F32)


def _dot_tn(a, b):
    return lax.dot_general(a, b, (((0,), (0,)), ((), ())), preferred_element_type=F32)


def _split2(x):
    hi = x.astype(BF16)
    lo = (x - hi.astype(F32)).astype(BF16)
    return hi, lo


def _split3(x):
    hi = x.astype(BF16)
    r = x - hi.astype(F32)
    mid = r.astype(BF16)
    lo = (r - mid.astype(F32)).astype(BF16)
    return hi, mid, lo


def _sigmoid(x):
    return 1.0 / (1.0 + jnp.exp(-x))


def _accumulate(ref, part, step):
    @pl.when(step == 0)
    def _():
        ref[...] = part

    @pl.when(step > 0)
    def _():
        ref[...] += part


RELAY_AT_NUM, RELAY_AT_DEN = 3, 4


def _my_place():
    return lax.axis_index("x"), lax.axis_index("y"), lax.axis_index("c")


def _flip(v, bit):
    return 1 - v if bit else v


class _NoRelay:
    def relay(self, in_refs, out_refs, sems):
        pass


class _GatherRider:
    def __init__(self, xs):
        self.inputs = list(xs)
        n = len(xs)
        self.out_shape = [jax.ShapeDtypeStruct((N_DEV,) + x.shape, x.dtype) for x in xs]
        self.scratch = [pltpu.SemaphoreType.DMA((7, n)), pltpu.SemaphoreType.DMA((7, n)),
                        pltpu.SemaphoreType.DMA((n,))]
        self.results = None

    def _copies(self, x_refs, out_refs, sems):
        send_sems, recv_sems, local_sems = sems
        n = len(x_refs)
        x, y, c = _my_place()
        me, sibling = (x, y, c), (x, y, 1 - c)
        chips = [(1 - x, y), (x, 1 - y), (1 - x, 1 - y)]

        def slot(a, px, py, pc):
            return out_refs[a].at[4 * px + 2 * py + pc]

        def copy(k, a, block, to, own=False):
            return pltpu.make_async_remote_copy(
                src_ref=x_refs[a] if own else slot(a, *block), dst_ref=slot(a, *block),
                send_sem=send_sems.at[k, a], recv_sem=recv_sems.at[k, a],
                device_id=to, device_id_type=MESH)

        mine = [pltpu.make_async_copy(x_refs[a], slot(a, *me), local_sems.at[a]) for a in range(n)]
        first = []
        for a in range(n):
            first.append(copy(0, a, me, sibling, own=True))
            first += [copy(1 + j, a, me, (*chip, c), own=True) for j, chip in enumerate(chips)]
        return n, c, me, sibling, chips, copy, mine, first

    def start(self, x_refs, out_refs, sems):
        _, _, _, _, _, _, mine, first = self._copies(x_refs, out_refs, sems)
        for cp in mine + first:
            cp.start()

    def relay(self, x_refs, out_refs, sems):
        n, c, me, sibling, chips, copy, _, _ = self._copies(x_refs, out_refs, sems)
        for j, chip in enumerate(chips):
            for a in range(n):
                copy(1 + j, a, (*chip, c), me).wait_recv()
                copy(4 + j, a, (*chip, c), sibling).start()

    def finish(self, x_refs, out_refs, sems):
        n, c, me, sibling, chips, copy, mine, first = self._copies(x_refs, out_refs, sems)
        passed = [copy(4 + j, a, (*chip, c), sibling) for j, chip in enumerate(chips) for a in range(n)]
        for a in range(n):
            copy(0, a, sibling, me).wait_recv()
            for j, chip in enumerate(chips):
                copy(4 + j, a, (*chip, 1 - c), me).wait_recv()
        for cp in first + passed:
            cp.wait_send()
        for cp in mine:
            cp.wait()


class _ScatterRider(_NoRelay):
    def __init__(self, gs):
        self.inputs = list(gs)
        n = len(gs)
        self.out_shape = [jax.ShapeDtypeStruct(g.shape, g.dtype) for g in gs]
        self.scratch = [pltpu.SemaphoreType.DMA((7, n)), pltpu.SemaphoreType.DMA((7, n)),
                        pltpu.SemaphoreType.DMA((n,))]
        self.results = None

    def _copies(self, g_refs, out_refs, sems):
        send_sems, recv_sems, local_sems = sems
        x, y, c = _my_place()
        me = 4 * x + 2 * y + c
        mine, copies = [], []
        for a in range(len(g_refs)):
            mine.append(pltpu.make_async_copy(g_refs[a].at[me], out_refs[a].at[me], local_sems.at[a]))
            for k in range(1, N_DEV):
                px, py, pc = _flip(x, k & 4), _flip(y, k & 2), _flip(c, k & 1)
                copies.append(pltpu.make_async_remote_copy(
                    src_ref=g_refs[a].at[4 * px + 2 * py + pc], dst_ref=out_refs[a].at[me],
                    send_sem=send_sems.at[k - 1, a], recv_sem=recv_sems.at[k - 1, a],
                    device_id=(px, py, pc), device_id_type=MESH))
        return mine, copies

    def start(self, g_refs, out_refs, sems):
        mine, copies = self._copies(g_refs, out_refs, sems)
        for cp in mine + copies:
            cp.start()

    def finish(self, g_refs, out_refs, sems):
        mine, copies = self._copies(g_refs, out_refs, sems)
        for cp in copies + mine:
            cp.wait()


class _SiblingSwapRider(_NoRelay):
    def __init__(self, gs):
        self.inputs = list(gs)
        n = len(gs)
        self.out_shape = [jax.ShapeDtypeStruct((4,) + g.shape[1:], g.dtype) for g in gs]
        self.scratch = [pltpu.SemaphoreType.DMA((4, n)), pltpu.SemaphoreType.DMA((4, n))]
        self.results = None

    def _copies(self, g_refs, out_refs, sems):
        send_sems, recv_sems = sems
        x, y, c = _my_place()
        return [pltpu.make_async_remote_copy(
            src_ref=g_refs[a].at[2 * k + 1 - c], dst_ref=out_refs[a].at[k],
            send_sem=send_sems.at[k, a], recv_sem=recv_sems.at[k, a],
            device_id=(x, y, 1 - c), device_id_type=MESH)
            for a in range(len(g_refs)) for k in range(4)]

    def start(self, g_refs, out_refs, sems):
        for cp in self._copies(g_refs, out_refs, sems):
            cp.start()

    def finish(self, g_refs, out_refs, sems):
        for cp in self._copies(g_refs, out_refs, sems):
            cp.wait()


class _ChipScatterRider(_NoRelay):
    def __init__(self, ps):
        self.inputs = list(ps)
        n = len(ps)
        self.out_shape = [jax.ShapeDtypeStruct(p.shape, p.dtype) for p in ps]
        self.scratch = [pltpu.SemaphoreType.DMA((3, n)), pltpu.SemaphoreType.DMA((3, n)),
                        pltpu.SemaphoreType.DMA((n,))]
        self.results = None

    def _copies(self, p_refs, out_refs, sems):
        send_sems, recv_sems, local_sems = sems
        x, y, c = _my_place()
        my_chip = 2 * x + y
        chips = [(1 - x, y), (x, 1 - y), (1 - x, 1 - y)]
        n = len(p_refs)
        mine = [pltpu.make_async_copy(p_refs[a].at[my_chip], out_refs[a].at[my_chip], local_sems.at[a])
                for a in range(n)]
        copies = [pltpu.make_async_remote_copy(
            src_ref=p_refs[a].at[2 * cx + cy], dst_ref=out_refs[a].at[my_chip],
            send_sem=send_sems.at[j, a], recv_sem=recv_sems.at[j, a],
            device_id=(cx, cy, c), device_id_type=MESH)
            for a in range(n) for j, (cx, cy) in enumerate(chips)]
        return mine, copies

    def start(self, p_refs, out_refs, sems):
        mine, copies = self._copies(p_refs, out_refs, sems)
        for cp in mine + copies:
            cp.start()

    def finish(self, p_refs, out_refs, sems):
        mine, copies = self._copies(p_refs, out_refs, sems)
        for cp in copies + mine:
            cp.wait()


def _call(body, name, grid, in_specs, out_specs, out_shape, scratch, semantics, args, rider=None):
    in_specs, out_specs, out_shape, scratch = list(in_specs), list(out_specs), list(out_shape), list(scratch)
    if rider is None:
        return list(pl.pallas_call(
            body, name=name, grid=grid, in_specs=in_specs, out_specs=out_specs, out_shape=out_shape,
            scratch_shapes=scratch, compiler_params=_params(*semantics))(*args))
    n_in, n_out, n_scr = len(in_specs), len(out_specs), len(scratch)
    r_in, r_out = len(rider.inputs), len(rider.out_shape)

    def wrapped(*refs):
        cuts = np.cumsum([0, n_in, r_in, n_out, r_out, n_scr])
        hi, ri, ho, ro, hs = (refs[cuts[i]:cuts[i + 1]] for i in range(5))
        rs = refs[cuts[5]:]
        step, steps = pl.program_id(0), grid[0]
        for d in range(1, len(grid)):
            step, steps = step * grid[d] + pl.program_id(d), steps * grid[d]

        @pl.when(step == 0)
        def _():
            rider.start(ri, ro, rs)

        body(*hi, *ho, *hs)

        @pl.when(step == (steps * RELAY_AT_NUM) // RELAY_AT_DEN)
        def _():
            rider.relay(ri, ro, rs)

        @pl.when(step == steps - 1)
        def _():
            rider.finish(ri, ro, rs)

    outs = pl.pallas_call(
        wrapped, name=name, grid=grid,
        in_specs=in_specs + [ANY] * r_in, out_specs=out_specs + [ANY] * r_out,
        out_shape=out_shape + rider.out_shape, scratch_shapes=scratch + rider.scratch,
        compiler_params=_params(*(["arbitrary"] * len(grid))),
    )(*args, *rider.inputs)
    rider.results = list(outs[n_out:])
    return list(outs[:n_out])


_WALK = ((None, None), (0, None), (1, 4), (2, 5), (4, None), (5, None), (3, 6), (6, None))


def _gather_order():
    x, y, c = _my_place()
    (ax, ay), (bx, by), (dx, dy) = (1 - x, y), (x, 1 - y), (1 - x, 1 - y)
    ids = [(x, y, c), (x, y, 1 - c), (ax, ay, c), (bx, by, c), (ax, ay, 1 - c), (bx, by, 1 - c),
           (dx, dy, c), (dx, dy, 1 - c)]
    return jnp.stack([4 * px + 2 * py + pc for px, py, pc in ids]).astype(jnp.int32)


def _proj_gather(x, norm_g, w_shard, extras, name):
    t, d = x.shape
    cols = w_shard.shape[1]
    tm = _pick(t, MM_CAP_MN, 16)
    ni = t // tm
    n = 1 + len(extras)
    rider = _GatherRider([w_shard] + list(extras))

    def body(ord_ref, x_ref, g_ref, *refs):
        sh_refs, proj_ref, gathered = refs[:n], refs[n], refs[n + 1:2 * n + 1]
        h_all, bbuf, bsem, send_sems, recv_sems, local_sems = refs[2 * n + 1:]
        j, i = pl.program_id(0), pl.program_id(1)
        _, c, me, sibling, chips, copy, mine, first = rider._copies(
            sh_refs, gathered, (send_sems, recv_sems, local_sems))
        rows = pl.ds(pl.multiple_of(i * tm, tm), tm)

        def load(step, src):
            return pltpu.make_async_copy(src, bbuf.at[step % 2], bsem.at[step % 2])

        def relayed(k, a):
            return copy(k, a, (*chips[k - 4], c), sibling)

        @pl.when(jnp.logical_and(j == 0, i == 0))
        def _():
            for cp in mine + first:
                cp.start()
            load(0, sh_refs[0]).start()

        @pl.when(i == 0)
        def _():
            load(j, sh_refs[0]).wait()

        @pl.when(j == 0)
        def _():
            groups = []
            for r in range(0, tm, NORM_ROWS):
                xv = x_ref[r:r + NORM_ROWS, :]
                rstd = lax.rsqrt(jnp.mean(xv * xv, axis=-1, keepdims=True) + EPS)
                groups.append((xv * rstd * g_ref[...]).astype(BF16))
            h_all[rows, :] = jnp.concatenate(groups, axis=0)

        proj_ref[...] = _dot(h_all[rows, :], bbuf[j % 2])

        for step in range(N_DEV - 1):
            @pl.when(jnp.logical_and(j == step, i == max(ni - 2, 0)))
            def _(step=step):
                need, relay = _WALK[step + 1]
                copy(need, 0, me, me).wait_recv()
                if relay is not None:
                    relayed(relay, 0).start()
                load(step + 1, gathered[0].at[ord_ref[step + 1]]).start()

        @pl.when(jnp.logical_and(j == N_DEV - 1, i == ni - 1))
        def _():
            for a in range(1, n):
                for k in range(3):
                    copy(1 + k, a, me, me).wait_recv()
                    relayed(4 + k, a).start()
            for a in range(1, n):
                for k in (0, 4, 5, 6):
                    copy(k, a, me, me).wait_recv()
            for cp in first + [relayed(4 + k, a) for a in range(n) for k in range(3)]:
                cp.wait_send()
            for cp in mine:
                cp.wait()

    outs = pl.pallas_call(
        body, name=name,
        grid_spec=pltpu.PrefetchScalarGridSpec(
            num_scalar_prefetch=1, grid=(N_DEV, ni),
            in_specs=[pl.BlockSpec((tm, d), lambda j, i, o: (jnp.where(j == 0, i, ni - 1), 0)),
                      pl.BlockSpec((1, d), lambda j, i, o: (0, 0))] + [ANY] * n,
            out_specs=[pl.BlockSpec((tm, cols), lambda j, i, o: (i, o[j]))] + [ANY] * n,
            scratch_shapes=[pltpu.VMEM((t, d), BF16), pltpu.VMEM((2, d, cols), BF16),
                            pltpu.SemaphoreType.DMA((2,))] + rider.scratch),
        out_shape=[jax.ShapeDtypeStruct((t, N_DEV * cols), F32)] + rider.out_shape,
        compiler_params=_params("arbitrary", "arbitrary"),
    )(_gather_order(), x, norm_g, w_shard, *extras)
    return outs[0], list(outs[1:])


def _exchange(rider, name):
    r_in, r_out = len(rider.inputs), len(rider.out_shape)

    def body(*refs):
        ri, ro, rs = refs[:r_in], refs[r_in:r_in + r_out], refs[r_in + r_out:]
        rider.start(ri, ro, rs)
        rider.relay(ri, ro, rs)
        rider.finish(ri, ro, rs)

    return list(pl.pallas_call(
        body, name=name, in_specs=[ANY] * r_in, out_specs=[ANY] * r_out,
        out_shape=rider.out_shape, scratch_shapes=rider.scratch)(*rider.inputs))


MM_CAP_MN = 1024
MM_CAP_M_GRAD = 1408
MM_CAP_N = 1536
MM_CAP_K = 3072
MM_CAP_K_TOKENS = 2048
MM_CAP_K_RMS = 8192
MM_CAP_M_RMS = 512
NORM_ROWS = 256


def _mm(a, b, mode, name, out_dtype=F32, res=None, out_block=None, epilogue=None, extra=None, norm_g=None,
        norm_b=False, rider=None):
    a3, b3 = a.ndim == 3, b.ndim == 3
    um = un = uk = None
    if mode in ("nn", "nt"):
        if a3:
            m, uk = a.shape[1:]
            k = a.shape[0] * uk
        else:
            m, k = a.shape
    else:
        if a3:
            k, um = a.shape[1:]
            m = a.shape[0] * um
        else:
            k, m = a.shape
    if mode in ("nn", "tn"):
        if b3:
            kb, un = b.shape[1:]
            n = b.shape[0] * un
        else:
            kb, n = b.shape
        assert kb == k, (a.shape, b.shape, mode)
    else:
        if b3:
            n, ukb = b.shape[1:]
            assert b.shape[0] * ukb == k and uk in (None, ukb), (a.shape, b.shape, mode)
            uk = ukb
        else:
            n, kb = b.shape
            assert kb == k, (a.shape, b.shape, mode)
    if out_block is not None:
        assert un in (None, out_block)
        un = out_block

    def tile(dim, unit, cap, align):
        if unit is None:
            return _pick(dim, cap, align), 1
        c = max(1, cap // unit)
        while (dim // unit) % c:
            c -= 1
        return unit, c

    cap_m = MM_CAP_M_GRAD if mode == "tn" else (MM_CAP_M_RMS if epilogue == "rms_bwd" else MM_CAP_MN)
    um, cm = tile(m, um, cap_m, 128 if mode == "tn" else 16)
    un, cn = tile(n, un, MM_CAP_N, 128)
    cap_k = MM_CAP_K_TOKENS if mode == "tn" else (MM_CAP_K_RMS if epilogue == "rms_bwd" else MM_CAP_K)
    uk, ck = tile(k, uk, cap_k, 128)
    if epilogue == "rms_bwd":
        assert mode != "tn" and n == D_MODEL and cm == cn == 1 and res is None and out_block is None
    if epilogue == "loss":
        assert n == D_MODEL and cm == cn == 1 and res is not None and out_block is None
    if norm_g is not None and norm_b:
        assert mode == "tn" and not b3 and n == D_MODEL and cn == 1
    elif norm_g is not None:
        assert not a3 and (m if mode == "tn" else k) == D_MODEL and (cm if mode == "tn" else ck) == 1
    if epilogue == "swiglu":
        assert res is None and ((mode == "nn" and b3 and out_block is None) or
                                (mode == "nt" and not b3 and out_block is not None))
        cn = 2
    if epilogue == "swiglu_bwd":
        assert mode == "nt" and out_block is not None and extra is not None and res is None
        cn = 1
    tm, tn, tk = cm * um, cn * un, ck * uk
    nk = k // tk
    dot = {"nn": _dot, "nt": _dot_nt, "tn": _dot_tn}[mode]
    half = n // un // 2
    blocked_out = out_block is not None or epilogue in ("swiglu", "swiglu_bwd")
    extras = [] if extra is None else (list(extra) if isinstance(extra, (tuple, list)) else [extra])

    def sl(idx, unit, count):
        return slice(None) if count == 1 else slice(idx * unit, (idx + 1) * unit)

    def body(*refs):
        a_ref, b_ref = refs[0], refs[1]
        pos = 2
        r_ref = ng_ref = None
        if res is not None:
            r_ref, pos = refs[pos], pos + 1
        e_refs, pos = refs[pos:pos + len(extras)], pos + len(extras)
        if norm_g is not None:
            ng_ref, pos = refs[pos], pos + 1
        outs, acc_ref = refs[pos:-1], refs[-1]
        kk = pl.program_id(2)

        def normed(x_ref):
            groups = []
            for r in range(0, x_ref.shape[0], NORM_ROWS):
                xv = x_ref[r:r + NORM_ROWS, :]
                rstd = lax.rsqrt(jnp.mean(xv * xv, axis=-1, keepdims=True) + EPS)
                groups.append((xv * rstd * ng_ref[...]).astype(BF16))
            return jnp.concatenate(groups, axis=0)

        def a_blk(mi, ki):
            if norm_g is not None and not norm_b:
                return normed(a_ref)
            if mode in ("nn", "nt"):
                return a_ref[ki] if a3 else a_ref[:, sl(ki, uk, ck)]
            return a_ref[mi] if a3 else a_ref[:, sl(mi, um, cm)]

        def b_blk(ki, ni):
            if norm_b:
                return normed(b_ref)
            if epilogue == "swiglu":
                return b_ref[ni, 0]
            if mode in ("nn", "tn"):
                return b_ref[ni] if b3 else b_ref[sl(ki, uk, ck), sl(ni, un, cn)]
            return b_ref[ki][sl(ni, un, cn), :] if b3 else b_ref[sl(ni, un, cn), sl(ki, uk, ck)]

        parts = {}
        for mi in range(cm):
            for ni in range(cn):
                part = None
                for ki in range(ck):
                    d = dot(a_blk(mi, ki).astype(BF16), b_blk(ki, ni).astype(BF16))
                    part = d if part is None else part + d
                parts[mi, ni] = part

        def finish(total):
            if epilogue == "swiglu":
                gate, up = total[0, 0], total[0, 1]
                outs[0][0, 0] = gate.astype(BF16)
                outs[0][1, 0] = up.astype(BF16)
                outs[1][0] = (gate * _sigmoid(gate) * up).astype(BF16)
                return
            if epilogue == "swiglu_bwd":
                dact = total[0, 0]
                gate, up = e_refs[0][0, 0].astype(F32), e_refs[0][1, 0].astype(F32)
                sg = _sigmoid(gate)
                outs[0][0, 0] = (dact * up * (sg * (1.0 + gate * (1.0 - sg)))).astype(BF16)
                outs[0][1, 0] = (dact * (gate * sg)).astype(BF16)
                return
            if epilogue == "rms_bwd":
                x_ref, g_ref, dres_ref = e_refs
                dh, dg = total[0, 0], None
                for r in range(0, tm, NORM_ROWS):
                    rows = slice(r, r + NORM_ROWS)
                    xv, dhv = x_ref[rows, :], dh[rows, :]
                    rstd = lax.rsqrt(jnp.mean(xv * xv, axis=-1, keepdims=True) + EPS)
                    xh = xv * rstd
                    dyg = dhv * g_ref[...]
                    c = jnp.mean(dyg * xh, axis=-1, keepdims=True)
                    outs[0][rows, :] = dres_ref[rows, :] + rstd * (dyg - xh * c)
                    part = jnp.sum(dhv * xh, axis=0, keepdims=True)
                    dg = part if dg is None else dg + part
                _accumulate(outs[1], dg, pl.program_id(0))
                return
            if epilogue == "loss":
                diff = r_ref[...] + total[0, 0] - e_refs[0][...]
                outs[0][...] = diff * (1.0 / n)
                sq = jnp.sum(jnp.sum(diff * diff, axis=-1, keepdims=True), axis=0, keepdims=True)
                _accumulate(outs[1], sq * (0.5 / n), pl.program_id(0))
                return
            for (mi, ni), val in total.items():
                rows, cols = sl(mi, um, cm), sl(ni, un, cn)
                if res is not None:
                    val = r_ref[rows, cols] + val
                if blocked_out:
                    outs[0][ni, rows] = val.astype(out_dtype)
                else:
                    outs[0][rows, cols] = val.astype(out_dtype)

        if nk == 1:
            finish(parts)
        else:
            @pl.when(kk == 0)
            def _():
                for (mi, ni), val in parts.items():
                    acc_ref[mi * cn + ni] = val

            @pl.when(jnp.logical_and(kk > 0, kk < nk - 1))
            def _():
                for (mi, ni), val in parts.items():
                    acc_ref[mi * cn + ni] += val

            @pl.when(kk == nk - 1)
            def _():
                finish({key: acc_ref[key[0] * cn + key[1]] + val for key, val in parts.items()})

    if mode in ("nn", "nt"):
        a_spec = (pl.BlockSpec((ck, tm, uk), lambda i, j, kk: (kk, i, 0)) if a3
                  else pl.BlockSpec((tm, tk), lambda i, j, kk: (i, kk)))
    else:
        a_spec = (pl.BlockSpec((cm, tk, um), lambda i, j, kk: (i, kk, 0)) if a3
                  else pl.BlockSpec((tk, tm), lambda i, j, kk: (kk, i)))
    pair_spec = pl.BlockSpec((2, 1, tm, un), lambda i, j, kk: (0, j, i, 0))
    row_spec = pl.BlockSpec((tm, tn), lambda i, j, kk: (i, 0))
    vec_spec = pl.BlockSpec((1, tn), lambda i, j, kk: (0, 0))
    if epilogue == "swiglu" and mode == "nn":
        b = b.reshape(2, half, k, un)
        b_spec = pl.BlockSpec((2, 1, tk, un), lambda i, j, kk: (0, j, kk, 0))
    elif epilogue == "swiglu":
        b = b.reshape(2, half, un, k)
        b_spec = pl.BlockSpec((2, 1, un, tk), lambda i, j, kk: (0, j, 0, kk))
    elif mode in ("nn", "tn"):
        b_spec = (pl.BlockSpec((cn, tk, un), lambda i, j, kk: (j, kk, 0)) if b3
                  else pl.BlockSpec((tk, tn), lambda i, j, kk: (kk, j)))
    else:
        b_spec = (pl.BlockSpec((ck, tn, uk), lambda i, j, kk: (kk, j, 0)) if b3
                  else pl.BlockSpec((tn, tk), lambda i, j, kk: (j, kk)))
    if epilogue == "swiglu":
        out_specs = [pair_spec, pl.BlockSpec((1, tm, un), lambda i, j, kk: (j, i, 0))]
        out_shape = [jax.ShapeDtypeStruct((2, half, m, un), BF16), jax.ShapeDtypeStruct((half, m, un), BF16)]
    elif epilogue == "swiglu_bwd":
        out_specs = [pair_spec]
        out_shape = [jax.ShapeDtypeStruct(extra.shape, BF16)]
    elif epilogue == "rms_bwd":
        out_specs = [row_spec, vec_spec]
        out_shape = [jax.ShapeDtypeStruct((m, n), F32), jax.ShapeDtypeStruct((1, n), F32)]
    elif epilogue == "loss":
        out_specs = [row_spec, pl.BlockSpec((1, 1), lambda i, j, kk: (0, 0))]
        out_shape = [jax.ShapeDtypeStruct((m, n), F32), jax.ShapeDtypeStruct((1, 1), F32)]
    elif blocked_out:
        out_specs = [pl.BlockSpec((cn, tm, un), lambda i, j, kk: (j, i, 0))]
        out_shape = [jax.ShapeDtypeStruct((n // un, m, un), out_dtype)]
    else:
        out_specs = [pl.BlockSpec((tm, tn), lambda i, j, kk: (i, j))]
        out_shape = [jax.ShapeDtypeStruct((m, n), out_dtype)]
    in_specs, args = [a_spec, b_spec], [a, b]
    if res is not None:
        in_specs.append(pl.BlockSpec((tm, tn), lambda i, j, kk: (i, j)))
        args.append(res)
    if epilogue == "swiglu_bwd":
        in_specs.append(pair_spec)
    elif epilogue == "rms_bwd":
        in_specs += [row_spec, vec_spec, row_spec]
    elif epilogue == "loss":
        in_specs.append(row_spec)
    args += extras
    if norm_g is not None:
        in_specs.append(pl.BlockSpec((1, D_MODEL), lambda i, j, kk: (0, 0)))
        args.append(norm_g)
    semantics = ("arbitrary",) * 3 if epilogue in ("rms_bwd", "loss") else ("parallel", "parallel", "arbitrary")
    out = _call(body, name, (m // tm, n // tn, nk), in_specs, out_specs, out_shape,
                [pltpu.VMEM((cm * cn, um, un), F32)], semantics, args, rider)
    return out if epilogue in ("swiglu", "rms_bwd", "loss") else out[0]


def _head_sums(v, ind):
    return _dot(v.astype(BF16), ind)


def _head_spread(per_head, ind):
    hi, lo = _split2(per_head)
    return _dot_nt(hi, ind) + _dot_nt(lo, ind)


def _head_rstd(xv, ind):
    return _head_spread(lax.rsqrt(_head_sums(xv * xv, ind) * (1.0 / ATT_DH) + EPS), ind)


def _hn_bwd_math(xv, gv, ind, dyv, scale):
    rstd = _head_rstd(xv, ind)
    xh = xv * rstd
    dyn = dyv * scale
    dyg = dyn * gv
    dx = rstd * (dyg - xh * _head_spread(_head_sums(dyg * xh, ind) * (1.0 / ATT_DH), ind))
    return dx, jnp.sum(dyn * xh, axis=0, keepdims=True)


def _q_hnorm(x, g_tiled, bd, scale, name):
    t, d = x.shape
    tm = _pick(t, 512, 16)

    def body(x_ref, g_ref, bd_ref, o_ref):
        xv = x_ref[...]
        o_ref[...] = (xv * _head_rstd(xv, bd_ref[...]) * g_ref[...] * scale).astype(BF16)

    return pl.pallas_call(
        body, name=name, grid=(t // tm,),
        in_specs=[pl.BlockSpec((tm, d), lambda i: (i, 0)), pl.BlockSpec((1, d), lambda i: (0, 0)),
                  pl.BlockSpec((d, LANES), lambda i: (0, 0))],
        out_specs=pl.BlockSpec((tm, d), lambda i: (i, 0)),
        out_shape=jax.ShapeDtypeStruct((t, d), BF16),
        compiler_params=_params("parallel"),
    )(x, g_tiled, bd)


def _q_dhnorm(x, g_tiled, bd, dy, scale, name):
    t, d = x.shape
    tm = _pick(t, 512, 16)

    def body(x_ref, g_ref, bd_ref, dy_ref, dx_ref, dg_ref):
        dx, part = _hn_bwd_math(x_ref[...], g_ref[...], bd_ref[...], dy_ref[...], scale)
        dx_ref[...] = dx.astype(BF16)
        _accumulate(dg_ref, part, pl.program_id(0))

    row = pl.BlockSpec((tm, d), lambda i: (i, 0))
    vec = pl.BlockSpec((1, d), lambda i: (0, 0))
    return pl.pallas_call(
        body, name=name, grid=(t // tm,),
        in_specs=[row, vec, pl.BlockSpec((d, LANES), lambda i: (0, 0)), row],
        out_specs=[row, vec],
        out_shape=[jax.ShapeDtypeStruct((t, d), BF16), jax.ShapeDtypeStruct((1, d), F32)],
        compiler_params=_params("arbitrary"),
    )(x, g_tiled, bd, dy)


def _kv_prep(kv, g_tiled, bd, name):
    t = kv.shape[0]
    d = D_MODEL
    tm = K_PAD
    assert t % tm == 0

    def body(k_ref, v_ref, g_ref, bd_ref, kp_ref, vp_ref):
        i = pl.program_id(0)

        @pl.when(i == 0)
        def _():
            kp_ref[...] = jnp.zeros_like(kp_ref)
            vp_ref[...] = jnp.zeros_like(vp_ref)

        @pl.when(i > 0)
        def _():
            xv = k_ref[...]
            kp_ref[...] = (xv * _head_rstd(xv, bd_ref[...]) * g_ref[...]).astype(BF16)
            vp_ref[...] = v_ref[...].astype(BF16)

    shp = jax.ShapeDtypeStruct((t + K_PAD, d), BF16)
    out = pl.BlockSpec((tm, d), lambda i: (i, 0))
    return pl.pallas_call(
        body, name=name, grid=(t // tm + 1,),
        in_specs=[pl.BlockSpec((tm, d), lambda i: (jnp.maximum(i - 1, 0), 0)),
                  pl.BlockSpec((tm, d), lambda i: (jnp.maximum(i - 1, 0), 1)),
                  pl.BlockSpec((1, d), lambda i: (0, 0)), pl.BlockSpec((d, LANES), lambda i: (0, 0))],
        out_specs=[out, out], out_shape=[shp, shp],
        compiler_params=_params("arbitrary"),
    )(kv, kv, g_tiled, bd)


def _kv_dprep(kv, g_tiled, bd, dkp_t, dvp_t, name):
    t = kv.shape[0]
    d = D_MODEL
    tm = K_PAD

    def body(k_ref, g_ref, bd_ref, dk_ref, dv_ref, o_ref, dg_ref):
        dx, part = _hn_bwd_math(k_ref[...], g_ref[...], bd_ref[...], dk_ref[...].T, 1.0)
        o_ref[:, :d] = dx.astype(BF16)
        o_ref[:, d:] = dv_ref[...].T.astype(BF16)
        _accumulate(dg_ref, part, pl.program_id(0))

    vec = pl.BlockSpec((1, d), lambda i: (0, 0))
    padded = pl.BlockSpec((d, tm), lambda i: (0, i + 1))
    return pl.pallas_call(
        body, name=name, grid=(t // tm,),
        in_specs=[pl.BlockSpec((tm, d), lambda i: (i, 0)), vec, pl.BlockSpec((d, LANES), lambda i: (0, 0)),
                  padded, padded],
        out_specs=[pl.BlockSpec((tm, 2 * d), lambda i: (i, 0)), vec],
        out_shape=[jax.ShapeDtypeStruct((t, 2 * d), BF16), jax.ShapeDtypeStruct((1, d), F32)],
        compiler_params=_params("arbitrary"),
    )(kv, g_tiled, bd, dkp_t, dvp_t)


def _ret_consts(t):
    h = np.arange(RET_HEADS, dtype=np.float32)
    lg = np.log(np.float32(1.0) - np.float32(2.0) ** (np.float32(-5.0) - h)).astype(np.float32)
    tt = np.arange(CHUNK, dtype=np.float32)
    intra = np.exp(lg[:, None, None] * np.abs(tt[:, None] - tt[None, :])).astype(np.float32)
    q_dec = np.exp(lg[:, None] * (tt + 1.0)).astype(np.float32)
    k_dec = np.exp(lg[:, None] * (CHUNK - 1.0 - tt)).astype(np.float32)
    s_dec = [float(v) for v in np.exp(lg * np.float32(CHUNK)).astype(np.float32)]
    qd = np.broadcast_to(q_dec[:, :, None], (RET_HEADS, CHUNK, RET_DK)).copy()
    kd = np.broadcast_to(k_dec[:, :, None], (RET_HEADS, CHUNK, RET_DK)).copy()
    half = RET_DK // 2
    inv_freq = ROPE_BASE ** (-jnp.arange(half, dtype=F32) / half)
    ang = jnp.arange(t).astype(F32)[:, None] * inv_freq[None, :]
    return jnp.asarray(intra), jnp.asarray(qd), jnp.asarray(kd), s_dec, jnp.cos(ang), jnp.sin(ang)


def _rope(x, cos, sin):
    half = RET_DK // 2
    x1, x2 = x[:, :half], x[:, half:]
    return jnp.concatenate([x1 * cos - x2 * sin, x1 * sin + x2 * cos], axis=-1)


def _unrope(d, cos, sin):
    half = RET_DK // 2
    d1, d2 = d[:, :half], d[:, half:]
    return jnp.concatenate([d1 * cos + d2 * sin, d2 * cos - d1 * sin], axis=-1)


def _ret_slices(h):
    q = slice(h * RET_DK, (h + 1) * RET_DK)
    k = slice(RET_Q_COLS + h * RET_DK, RET_Q_COLS + (h + 1) * RET_DK)
    v = slice(2 * RET_Q_COLS + h * RET_DV, 2 * RET_Q_COLS + (h + 1) * RET_DV)
    g = slice(2 * RET_Q_COLS + RET_V_COLS + h * RET_DV, 2 * RET_Q_COLS + RET_V_COLS + (h + 1) * RET_DV)
    o = slice(h * RET_DV, (h + 1) * RET_DV)
    return q, k, v, g, o


def _ret_fwd(proj, gn, consts, name, rider=None):
    t, cols = proj.shape
    n = t // CHUNK
    intra, qd, kd, s_dec, cos, sin = consts
    k_scale = RET_DK ** -0.5

    def body(p_ref, cos_ref, sin_ref, intra_ref, qd_ref, kd_ref, gn_ref, y_ref, o_ref, st_ref, state):
        i = pl.program_id(0)

        @pl.when(i == 0)
        def _():
            state[...] = jnp.zeros_like(state)

        for c in range(RET_STEP):
            rows = slice(c * CHUNK, (c + 1) * CHUNK)
            cosv, sinv = cos_ref[rows, :], sin_ref[rows, :]
            for h in range(RET_HEADS):
                qs, ks, vs, gs, os_ = _ret_slices(h)
                qr = _rope(p_ref[rows, qs], cosv, sinv)
                kr = _rope(p_ref[rows, ks], cosv, sinv) * k_scale
                vb = p_ref[rows, vs].astype(BF16)
                gv = p_ref[rows, gs]
                scores = _dot_nt(qr.astype(BF16), kr.astype(BF16)) * intra_ref[h]
                s_old = state[h]
                s_old_b = s_old.astype(BF16)
                st_ref[c, h] = s_old_b
                o = _dot(scores.astype(BF16), vb) + _dot((qr * qd_ref[h]).astype(BF16), s_old_b)
                state[h] = s_old * s_dec[h] + _dot_tn((kr * kd_ref[h]).astype(BF16), vb)
                rstd = lax.rsqrt(jnp.mean(o * o, axis=-1, keepdims=True) + EPS)
                on = o * rstd * gn_ref[:, os_]
                o_ref[rows, os_] = o
                y_ref[rows, os_] = (gv * _sigmoid(gv) * on).astype(BF16)

    full3 = lambda a: pl.BlockSpec(a.shape, lambda i: (0, 0, 0))
    step = RET_STEP * CHUNK
    return _call(
        body, name, (n // RET_STEP,),
        [pl.BlockSpec((step, cols), lambda i: (i, 0)),
         pl.BlockSpec((step, RET_DK // 2), lambda i: (i, 0)),
         pl.BlockSpec((step, RET_DK // 2), lambda i: (i, 0)),
         full3(intra), full3(qd), full3(kd),
         pl.BlockSpec((1, RET_V_COLS), lambda i: (0, 0))],
        [pl.BlockSpec((step, RET_V_COLS), lambda i: (i, 0)),
         pl.BlockSpec((step, RET_V_COLS), lambda i: (i, 0)),
         pl.BlockSpec((RET_STEP, RET_HEADS, RET_DK, RET_DV), lambda i: (i, 0, 0, 0))],
        [jax.ShapeDtypeStruct((t, RET_V_COLS), BF16),
         jax.ShapeDtypeStruct((t, RET_V_COLS), F32),
         jax.ShapeDtypeStruct((n, RET_HEADS, RET_DK, RET_DV), BF16)],
        [pltpu.VMEM((RET_HEADS, RET_DK, RET_DV), F32)], ("arbitrary",),
        (proj, cos, sin, intra, qd, kd, gn), rider)


def _ret_bwd(proj, gn, o_saved, states, dy, consts, name, rider=None):
    t, cols = proj.shape
    n = t // CHUNK
    intra, qd, kd, s_dec, cos, sin = consts
    k_scale = RET_DK ** -0.5

    def body(p_ref, cos_ref, sin_ref, intra_ref, qd_ref, kd_ref, gn_ref, o_ref, st_ref, dy_ref,
             dp_ref, dgn_ref, dstate):
        i = pl.program_id(0)

        @pl.when(i == 0)
        def _():
            dstate[...] = jnp.zeros_like(dstate)

        dgn = None
        for c in reversed(range(RET_STEP)):
            rows = slice(c * CHUNK, (c + 1) * CHUNK)
            cosv, sinv = cos_ref[rows, :], sin_ref[rows, :]
            dgn_parts = []
            for h in range(RET_HEADS):
                qs, ks, vs, gs, os_ = _ret_slices(h)
                qr = _rope(p_ref[rows, qs], cosv, sinv)
                kr = _rope(p_ref[rows, ks], cosv, sinv) * k_scale
                qb, kb = qr.astype(BF16), kr.astype(BF16)
                vb = p_ref[rows, vs].astype(BF16)
                gv = p_ref[rows, gs]
                ov = o_ref[rows, os_]
                dyv = dy_ref[rows, os_]
                gnv = gn_ref[:, os_]
                sg = _sigmoid(gv)
                rstd = lax.rsqrt(jnp.mean(ov * ov, axis=-1, keepdims=True) + EPS)
                oh = ov * rstd
                d_on = dyv * (gv * sg)
                dg = dyv * (oh * gnv) * (sg * (1.0 + gv * (1.0 - sg)))
                dgn_parts.append(jnp.sum(d_on * oh, axis=0, keepdims=True))
                d_oh = d_on * gnv
                do = rstd * (d_oh - oh * jnp.mean(d_oh * oh, axis=-1, keepdims=True))
                dob = do.astype(BF16)
                mask = intra_ref[h]
                a_b = (_dot_nt(qb, kb) * mask).astype(BF16)
                da_b = (_dot_nt(dob, vb) * mask).astype(BF16)
                ds_new = dstate[h]
                ds_new_b = ds_new.astype(BF16)
                s_old_b = st_ref[c, h]
                qdv, kdv = qd_ref[h], kd_ref[h]
                dv = _dot_tn(a_b, dob) + _dot((kr * kdv).astype(BF16), ds_new_b)
                dqr = _dot(da_b, kb) + _dot_nt(dob, s_old_b) * qdv
                dkr = _dot_tn(da_b, qb) + _dot_nt(vb, ds_new_b) * kdv
                dstate[h] = ds_new * s_dec[h] + _dot_tn((qr * qdv).astype(BF16), dob)
                dp_ref[rows, qs] = _unrope(dqr, cosv, sinv).astype(BF16)
                dp_ref[rows, ks] = _unrope(dkr * k_scale, cosv, sinv).astype(BF16)
                dp_ref[rows, vs] = dv.astype(BF16)
                dp_ref[rows, gs] = dg.astype(BF16)
            part = jnp.concatenate(dgn_parts, axis=-1)
            dgn = part if dgn is None else dgn + part
        _accumulate(dgn_ref, dgn, i)

    steps = n // RET_STEP
    step = RET_STEP * CHUNK
    rev = lambda i: (steps - 1 - i, 0)
    full3 = lambda a: pl.BlockSpec(a.shape, lambda i: (0, 0, 0))
    return _call(
        body, name, (steps,),
        [pl.BlockSpec((step, cols), rev),
         pl.BlockSpec((step, RET_DK // 2), rev),
         pl.BlockSpec((step, RET_DK // 2), rev),
         full3(intra), full3(qd), full3(kd),
         pl.BlockSpec((1, RET_V_COLS), lambda i: (0, 0)),
         pl.BlockSpec((step, RET_V_COLS), rev),
         pl.BlockSpec((RET_STEP, RET_HEADS, RET_DK, RET_DV), lambda i: (steps - 1 - i, 0, 0, 0)),
         pl.BlockSpec((step, RET_V_COLS), rev)],
        [pl.BlockSpec((step, cols), rev),
         pl.BlockSpec((1, RET_V_COLS), lambda i: (0, 0))],
        [jax.ShapeDtypeStruct((t, cols), BF16),
         jax.ShapeDtypeStruct((1, RET_V_COLS), F32)],
        [pltpu.VMEM((RET_HEADS, RET_DK, RET_DV), F32)], ("arbitrary",),
        (proj, cos, sin, intra, qd, kd, gn, o_saved, states, dy), rider)


def _att_common(q_ref, kp_ref, vp_ref, sub):
    blk = pl.program_id(1) * ATT_SUBS + sub
    start = pl.multiple_of(blk * Q_BLOCK, Q_BLOCK)
    kw = kp_ref[pl.ds(start, K_WINDOW), :]
    vw = vp_ref[pl.ds(start, K_WINDOW), :]
    kvalid = blk * Q_BLOCK - K_PAD + lax.broadcasted_iota(jnp.int32, (1, K_WINDOW), 1) >= 0
    lane = lax.broadcasted_iota(jnp.int32, (1, LANES), 1)
    qrows = slice(sub * Q_BLOCK, (sub + 1) * Q_BLOCK)
    return start, qrows, q_ref[qrows, :], kw, vw, kvalid, (lane < ATT_DH, lane >= ATT_DH)


def _row_groups():
    return [slice(r * ATT_ROWS, (r + 1) * ATT_ROWS) for r in range(Q_BLOCK // ATT_ROWS)]


def _lane_copies(x):
    return jnp.tile(x, (1, K_WINDOW // LANES))


def _att_specs(t, tp):
    qspec = pl.BlockSpec((ATT_SUBS * Q_BLOCK, LANES), lambda h, i: (i, h))
    kspec = pl.BlockSpec((tp, LANES), lambda h, i: (0, h))
    bspec = pl.BlockSpec((2, Q_BLOCK, K_WINDOW), lambda h, i: (h, 0, 0))
    return qspec, kspec, bspec


def _att_fwd(q, kp, vp, bias, name, rider=None):
    t, d = q.shape
    tp = kp.shape[0]

    def body(q_ref, kp_ref, vp_ref, bias_ref, o_ref, lse_ref, s_scr, p_scr, lse_scr, inv_scr):
        for sub in range(ATT_SUBS):
            _, qrows, q2, kw, vw, kvalid, sel = _att_common(q_ref, kp_ref, vp_ref, sub)
            for hh in range(2):
                s_scr[sub, hh] = _dot_nt(jnp.where(sel[hh], q2, 0), kw)
            for hh in range(2):
                for rows in _row_groups():
                    s = jnp.where(kvalid, s_scr[sub, hh, rows, :] + bias_ref[hh, rows, :], NEG)
                    m = jnp.max(s, axis=-1, keepdims=True)
                    e = jnp.exp(s - m)
                    l = jnp.sum(e, axis=-1, keepdims=True)
                    p_scr[sub, hh, rows, :] = e.astype(BF16)
                    inv_scr[sub, hh, rows, :] = jnp.broadcast_to(1.0 / l, (ATT_ROWS, LANES))
                    lse_scr[sub, hh, rows, :] = jnp.broadcast_to(m + jnp.log(l), (ATT_ROWS, LANES))
            outs = [_dot(p_scr[sub, hh], vw) * inv_scr[sub, hh] for hh in range(2)]
            o_ref[qrows, :] = jnp.where(sel[0], outs[0], outs[1]).astype(BF16)
            lse_ref[qrows, :] = jnp.where(sel[0], lse_scr[sub, 0], lse_scr[sub, 1])

    qspec, kspec, bspec = _att_specs(t, tp)
    return _call(body, name, (d // LANES, t // (ATT_SUBS * Q_BLOCK)), [qspec, kspec, kspec, bspec], [qspec, qspec],
                 [jax.ShapeDtypeStruct((t, d), BF16), jax.ShapeDtypeStruct((t, d), F32)],
                 [pltpu.VMEM((ATT_SUBS, 2, Q_BLOCK, K_WINDOW), F32),
                  pltpu.VMEM((ATT_SUBS, 2, Q_BLOCK, K_WINDOW), BF16),
                  pltpu.VMEM((ATT_SUBS, 2, Q_BLOCK, LANES), F32),
                  pltpu.VMEM((ATT_SUBS, 2, Q_BLOCK, LANES), F32)],
                 ("parallel", "arbitrary"), (q, kp, vp, bias), rider)


def _att_bwd(q, kp, vp, bias, do, o, lse, name, rider=None):
    t, d = q.shape
    tp = kp.shape[0]

    def body(q_ref, kp_ref, vp_ref, bias_ref, do_ref, o_ref, lse_ref, dq_ref, dkp_ref, dvp_ref, db_ref,
             s_scr, dp_scr, p_scr, ds_scr, row_scr):
        @pl.when(pl.program_id(1) == 0)
        def _():
            dkp_ref[...] = jnp.zeros_like(dkp_ref)
            dvp_ref[...] = jnp.zeros_like(dvp_ref)
            db_ref[...] = jnp.zeros_like(db_ref)

        for sub in range(ATT_SUBS):
            start, qrows, q2, kw, vw, kvalid, sel = _att_common(q_ref, kp_ref, vp_ref, sub)
            do2 = do_ref[qrows, :]
            qm = [jnp.where(sel[hh], q2, 0) for hh in range(2)]
            dom = [jnp.where(sel[hh], do2, 0) for hh in range(2)]
            do_o = do2.astype(F32) * o_ref[qrows, :].astype(F32)
            lse2 = lse_ref[qrows, :]
            for hh in range(2):
                s_scr[sub, hh] = _dot_nt(qm[hh], kw)
                dp_scr[sub, hh] = _dot_nt(dom[hh], vw)
                lse_h = jnp.max(jnp.where(sel[hh], lse2, NEG), axis=-1, keepdims=True)
                delta = jnp.sum(jnp.where(sel[hh], do_o, 0.0), axis=-1, keepdims=True)
                row_scr[sub, hh, 0] = jnp.broadcast_to(lse_h, (Q_BLOCK, LANES))
                row_scr[sub, hh, 1] = jnp.broadcast_to(delta, (Q_BLOCK, LANES))
            for hh in range(2):
                for rows in _row_groups():
                    s = jnp.where(kvalid, s_scr[sub, hh, rows, :] + bias_ref[hh, rows, :], NEG)
                    p = jnp.exp(s - _lane_copies(row_scr[sub, hh, 0, rows, :]))
                    ds = p * (dp_scr[sub, hh, rows, :] - _lane_copies(row_scr[sub, hh, 1, rows, :]))
                    db_ref[hh, rows, :] += ds
                    p_scr[sub, hh, rows, :] = p.astype(BF16)
                    ds_scr[sub, hh, rows, :] = ds.astype(BF16)
            dqs = [_dot(ds_scr[sub, hh], kw) for hh in range(2)]
            dq_ref[qrows, :] = jnp.where(sel[0], dqs[0], dqs[1])
            dkp_ref[:, pl.ds(start, K_WINDOW)] += (_dot_tn(qm[0], ds_scr[sub, 0]) +
                                                   _dot_tn(qm[1], ds_scr[sub, 1]))
            dvp_ref[:, pl.ds(start, K_WINDOW)] += (_dot_tn(dom[0], p_scr[sub, 0]) +
                                                   _dot_tn(dom[1], p_scr[sub, 1]))

    qspec, kspec, bspec = _att_specs(t, tp)
    tspec = pl.BlockSpec((LANES, tp), lambda h, i: (h, 0))
    stage = lambda dt: pltpu.VMEM((ATT_SUBS, 2, Q_BLOCK, K_WINDOW), dt)
    return _call(body, name, (d // LANES, t // (ATT_SUBS * Q_BLOCK)),
                 [qspec, kspec, kspec, bspec, qspec, qspec, qspec],
                 [qspec, tspec, tspec, bspec],
                 [jax.ShapeDtypeStruct((t, d), F32),
                  jax.ShapeDtypeStruct((d, tp), F32),
                  jax.ShapeDtypeStruct((d, tp), F32),
                  jax.ShapeDtypeStruct((ATT_HEADS, Q_BLOCK, K_WINDOW), F32)],
                 [stage(F32), stage(F32), stage(BF16), stage(BF16),
                  pltpu.VMEM((ATT_SUBS, 2, 2, Q_BLOCK, LANES), F32)],
                 ("parallel", "arbitrary"), (q, kp, vp, bias, do, o, lse), rider)


def _rel_bin_matrix():
    rows = REL_DELTAS * 2 * REL_BLK
    rho = lax.broadcasted_iota(jnp.int32, (rows, REL_PAD), 0)
    col = lax.broadcasted_iota(jnp.int32, (rows, REL_PAD), 1)
    assert 2 * REL_BLK == 256
    delta = rho >> 8
    c = 255 - (rho & 255)
    dist = K_PAD + REL_BLK * (delta - (K_WINDOW // REL_BLK - 1)) + (c - (REL_BLK - 1))
    idx = jnp.clip(dist, -REL_CLIP, REL_CLIP) + REL_CLIP
    return col == idx


def _rel_expand(rel_pad, name):
    heads = rel_pad.shape[0]
    rows = REL_DELTAS * 2 * REL_BLK

    def body_bin(r_ref, o_ref):
        onehot = jnp.where(_rel_bin_matrix(), 1.0, 0.0).astype(BF16)
        hi, mid, lo = _split3(r_ref[...])
        o_ref[...] = _dot_nt(hi, onehot) + _dot_nt(mid, onehot) + _dot_nt(lo, onehot)

    by_delta = pl.pallas_call(
        body_bin, name=name + "_bin",
        out_shape=jax.ShapeDtypeStruct((heads, rows), F32),
        compiler_params=pltpu.CompilerParams(vmem_limit_bytes=VMEM_LIMIT_V7X),
    )(rel_pad)
    by_delta = by_delta.reshape(heads * REL_DELTAS, 2 * REL_BLK)

    def body_shift(t_ref, o_ref):
        tv = t_ref[...]
        for r in range(REL_BLK):
            o_ref[r] = pltpu.roll(tv, (r + REL_BLK) % (2 * REL_BLK), 1)[:, :REL_BLK]

    return pl.pallas_call(
        body_shift, name=name + "_shift",
        out_shape=jax.ShapeDtypeStruct((REL_BLK, heads * REL_DELTAS, REL_BLK), F32),
        compiler_params=pltpu.CompilerParams(vmem_limit_bytes=VMEM_LIMIT_V7X),
    )(by_delta)


def _bias_table(rel_bias, name):
    heads = rel_bias.shape[0]
    rel_pad = jnp.pad(rel_bias, ((0, 0), (0, REL_PAD - REL_TABLE)))
    tiles = _rel_expand(rel_pad, name)
    tiles = tiles.reshape(REL_BLK, heads, REL_DELTAS, REL_BLK).transpose(1, 2, 0, 3)
    na, nb = Q_BLOCK // REL_BLK, K_WINDOW // REL_BLK
    rows = [jnp.concatenate([tiles[:, a - b + nb - 1] for b in range(nb)], axis=-1) for a in range(na)]
    table = jnp.concatenate(rows, axis=-2)
    qc = np.arange(Q_BLOCK)[:, None] // CHUNK
    kc = np.arange(K_WINDOW)[None, :] // CHUNK
    band = (kc >= qc) & (kc <= qc + PAST_CHUNKS)
    return jnp.where(jnp.asarray(band)[None], table, NEG)


def _rel_reduce(db, name):
    heads = db.shape[0]
    na, nb = Q_BLOCK // REL_BLK, K_WINDOW // REL_BLK

    fold_heads = 4

    def body_fold(db_ref, g_ref):
        for hd in range(fold_heads):
            for delta in range(REL_DELTAS):
                acc = None
                for a in range(na):
                    b = a - (delta - (nb - 1))
                    if 0 <= b < nb:
                        tile = db_ref[hd, a * REL_BLK:(a + 1) * REL_BLK, b * REL_BLK:(b + 1) * REL_BLK]
                        acc = tile if acc is None else acc + tile
                g_ref[hd, delta] = acc

    folded = pl.pallas_call(
        body_fold, name=name + "_fold", grid=(heads // fold_heads,),
        in_specs=[pl.BlockSpec((fold_heads, Q_BLOCK, K_WINDOW), lambda h: (h, 0, 0))],
        out_specs=pl.BlockSpec((fold_heads, REL_DELTAS, REL_BLK, REL_BLK), lambda h: (h, 0, 0, 0)),
        out_shape=jax.ShapeDtypeStruct((heads, REL_DELTAS, REL_BLK, REL_BLK), F32),
        compiler_params=_params("parallel"),
    )(db)
    by_row = folded.transpose(2, 0, 1, 3).reshape(REL_BLK, heads * REL_DELTAS, REL_BLK)

    def body_diag(g_ref, d_ref):
        zeros = jnp.zeros((heads * REL_DELTAS, REL_BLK), F32)
        acc = None
        for r in range(REL_BLK):
            part = pltpu.roll(jnp.concatenate([g_ref[r], zeros], axis=1), REL_BLK - r, 1)
            acc = part if acc is None else acc + part
        d_ref[...] = acc

    diag = pl.pallas_call(
        body_diag, name=name + "_diag",
        out_shape=jax.ShapeDtypeStruct((heads * REL_DELTAS, 2 * REL_BLK), F32),
        compiler_params=pltpu.CompilerParams(vmem_limit_bytes=VMEM_LIMIT_V7X),
    )(by_row)
    diag = diag.reshape(heads, REL_DELTAS * 2 * REL_BLK)

    def body_bin(d_ref, o_ref):
        onehot = jnp.where(_rel_bin_matrix(), 1.0, 0.0).astype(BF16)
        hi, mid, lo = _split3(d_ref[...])
        o_ref[...] = _dot(hi, onehot) + _dot(mid, onehot) + _dot(lo, onehot)

    out = pl.pallas_call(
        body_bin, name=name + "_bin",
        out_shape=jax.ShapeDtypeStruct((heads, REL_PAD), F32),
        compiler_params=pltpu.CompilerParams(vmem_limit_bytes=VMEM_LIMIT_V7X),
    )(diag)
    return out[:, :REL_TABLE]


def _sum_leading(x, name):
    n, r, c = x.shape
    tr = _pick(r, 256, 8)

    def body(x_ref, o_ref):
        acc = x_ref[0].astype(F32)
        for k in range(1, n):
            acc = acc + x_ref[k].astype(F32)
        o_ref[...] = acc

    return pl.pallas_call(
        body, name=name, grid=(r // tr,),
        in_specs=[pl.BlockSpec((n, tr, c), lambda i: (0, i, 0))],
        out_specs=pl.BlockSpec((tr, c), lambda i: (i, 0)),
        out_shape=jax.ShapeDtypeStruct((r, c), F32),
        compiler_params=_params("parallel"),
    )(x)


def _pair_add(g, recv, parity, name):
    _, r, c = g.shape
    tr = _pick(r, 256, 16)

    def body(par_ref, g_ref, r_ref, o_ref):
        o_ref[...] = (g_ref[...].astype(F32) + r_ref[...].astype(F32)).astype(BF16)

    return pl.pallas_call(
        body, name=name,
        grid_spec=pltpu.PrefetchScalarGridSpec(
            num_scalar_prefetch=1, grid=(4, r // tr),
            in_specs=[pl.BlockSpec((1, tr, c), lambda k, i, par: (2 * k + par[0], i, 0)),
                      pl.BlockSpec((1, tr, c), lambda k, i, par: (k, i, 0))],
            out_specs=pl.BlockSpec((1, tr, c), lambda k, i, par: (k, i, 0))),
        out_shape=jax.ShapeDtypeStruct((4, r, c), BF16),
        compiler_params=_params("parallel", "parallel"),
    )(parity, g, recv)


def _adamw(w, g_parts, m, v, name):
    r, c = w.shape
    n = g_parts.shape[0]
    tr = _pick(r, 256, 16 if g_parts.dtype == BF16 else 8)
    c1 = 1.0 - ADAM_B1 ** ADAM_STEP
    c2 = 1.0 - ADAM_B2 ** ADAM_STEP

    def body(w_ref, g_ref, m_ref, v_ref, go_ref, d_ref, nm_ref, nv_ref):
        gv = g_ref[0].astype(F32)
        for k in range(1, n):
            gv = gv + g_ref[k].astype(F32)
        nm = ADAM_B1 * m_ref[...] + (1.0 - ADAM_B1) * gv
        nv = ADAM_B2 * v_ref[...] + (1.0 - ADAM_B2) * (gv * gv)
        go_ref[...] = gv
        d_ref[...] = -ADAM_LR * ((nm / c1) / (jnp.sqrt(nv / c2) + ADAM_EPS) + ADAM_WD * w_ref[...])
        nm_ref[...] = nm
        nv_ref[...] = nv

    spec = pl.BlockSpec((tr, c), lambda i: (i, 0))
    shp = jax.ShapeDtypeStruct((r, c), F32)
    return pl.pallas_call(
        body, name=name, grid=(r // tr,),
        in_specs=[spec, pl.BlockSpec((n, tr, c), lambda i: (0, i, 0)), spec, spec],
        out_specs=[spec] * 4, out_shape=[shp] * 4,
        compiler_params=_params("parallel"),
    )(w, g_parts, m, v)


BIG = (("a_w_in", 1), ("a_w_o", 0), ("a_w_gu", 0), ("a_w_down", 0), ("w_kv", 1),
       ("b_w_q", 0), ("b_w_o", 0), ("b_w_gu", 0), ("b_w_down", 0))
TRANSPOSED = ("a_w_gu", "b_w_gu")
FFN_BLK = 2 * FFN_HIDDEN // N_DEV

SMALL = (("a_norm_g", D_MODEL, True), ("a_gn_g", RET_V_COLS, True), ("a_ffn_norm_g", D_MODEL, True),
         ("kv_norm_g", D_MODEL, False), ("b_norm_g", D_MODEL, False), ("b_ffn_norm_g", D_MODEL, False),
         ("k_norm_g", ATT_DH, False), ("b_q_norm_g", ATT_DH, False),
         ("b_rel_bias", ATT_HEADS * REL_TABLE, False))
SMALL_ROWS, SMALL_COLS = 16, 1024


def _pack_small(vals, last=None):
    flat = jnp.concatenate([vals[n].reshape(-1) for n, _, _ in SMALL])
    room = SMALL_ROWS * SMALL_COLS - flat.shape[0]
    if last is None:
        flat = jnp.pad(flat, (0, room))
    else:
        flat = jnp.concatenate([jnp.pad(flat, (0, room - 1)), last.reshape(1)])
    return flat.reshape(SMALL_ROWS, SMALL_COLS)


def _unpack_small(packed, local):
    flat, out, pos = packed.reshape(-1), {}, 0
    for n, length, sharded in SMALL:
        ln = length // N_DEV if (local and sharded) else length
        out[n] = flat[pos:pos + ln]
        pos += ln
    return out


def _gather_rider(shards, names):
    return _GatherRider([shards[n] for n in names])


def _gathered(rider, names, axis_of):
    return {n: (r.reshape(-1, r.shape[2]) if axis_of[n] == 0 else r) for n, r in zip(names, rider.results)}


def _blocks(g):
    return g if g.ndim == 3 else g.reshape(N_DEV, -1, g.shape[-1])


def _local_step(x, target, shards, s, parity):
    t = x.shape[0]
    axis_of = dict(BIG)
    consts = _ret_consts(t)
    lane_to_head = np.zeros((D_MODEL, LANES), np.float32)
    lane_to_head[np.arange(D_MODEL), np.arange(D_MODEL) // ATT_DH] = 1.0
    bd = jnp.asarray(lane_to_head).astype(BF16)
    kg_t = jnp.tile(s["k_norm_g"], (1, ATT_HEADS))
    qg_t = jnp.tile(s["b_q_norm_g"], (1, ATT_HEADS))
    q_scale = ATT_DH ** -0.5
    w, g, recv = {}, {}, {}

    def gather_on(names):
        return _gather_rider(shards, names), names

    def landed(ride):
        w.update(_gathered(ride[0], ride[1], axis_of))

    def scatter_on(names):
        return _ScatterRider([_blocks(g[n]) for n in names]), names

    def reduced(ride):
        recv.update(zip(ride[1], ride[0].results))

    proj, (w["a_w_in"], w_o) = _proj_gather(x, s["a_norm_g"], shards["a_w_in"], [shards["a_w_o"]], "a_proj")
    w["a_w_o"] = w_o.reshape(-1, w_o.shape[2])
    ride = gather_on(["a_w_gu"])
    y, o_ret, states = _ret_fwd(proj, s["a_gn_g"], consts, "a_ret", rider=ride[0])
    landed(ride)
    ride = gather_on(["w_kv"])
    x1 = _mm(y, w["a_w_o"], "nn", "a_out", res=x, rider=ride[0])
    landed(ride)
    ride = gather_on(["a_w_down", "b_w_q", "b_w_o"])
    gu_a, act_a = _mm(x1, w["a_w_gu"], "nt", "a_ffn_gu", epilogue="swiglu", out_block=FFN_BLK,
                      norm_g=s["a_ffn_norm_g"], rider=ride[0])
    landed(ride)
    x2 = _mm(act_a, w["a_w_down"], "nn", "a_ffn_down", res=x1)

    kv = _mm(x2, w["w_kv"], "nn", "kv_proj", norm_g=s["kv_norm_g"])
    kp, vp = _kv_prep(kv, kg_t, bd, "kv_prep")

    q_raw = _mm(x2, w["b_w_q"], "nn", "b_q", norm_g=s["b_norm_g"])
    qn = _q_hnorm(q_raw, qg_t, bd, q_scale, "q_hnorm")
    bias = _bias_table(s["b_rel_bias"].reshape(ATT_HEADS, REL_TABLE), "rel")
    ride = gather_on(["b_w_gu", "b_w_down"])
    o_att, lse = _att_fwd(qn, kp, vp, bias, "b_att", rider=ride[0])
    landed(ride)
    x3 = _mm(o_att, w["b_w_o"], "nn", "b_out", res=x2)
    gu_b, act_b = _mm(x3, w["b_w_gu"], "nt", "b_ffn_gu", epilogue="swiglu", out_block=FFN_BLK,
                      norm_g=s["b_ffn_norm_g"])
    dy, loss = _mm(act_b, w["b_w_down"], "nn", "b_ffn_down", res=x3, epilogue="loss", extra=(target,))
    in_blk, kv_blk, ffn_blk = w["a_w_in"].shape[2], w["w_kv"].shape[2], FFN_BLK

    dgu = _mm(dy, w["b_w_down"], "nt", "b_ffn_dgu", out_block=ffn_blk, epilogue="swiglu_bwd", extra=gu_b)
    dgu = dgu.reshape(N_DEV, t, ffn_blk)
    g["b_w_down"] = _mm(act_b, dy, "tn", "b_ffn_gdown", out_dtype=BF16)
    ride = scatter_on(["b_w_down"])
    dx3, g["b_ffn_norm_g"] = _mm(dgu, w["b_w_gu"], "nn", "b_ffn_dh", epilogue="rms_bwd",
                                 extra=(x3, s["b_ffn_norm_g"], dy), rider=ride[0])
    reduced(ride)
    g["b_w_gu"] = _mm(dgu, x3, "tn", "b_ffn_ggu", out_dtype=BF16, norm_g=s["b_ffn_norm_g"], norm_b=True)

    do_att = _mm(dx3, w["b_w_o"], "nt", "b_dout", out_dtype=BF16)
    g["b_w_o"] = _mm(o_att, dx3, "tn", "b_gout", out_dtype=BF16)
    ride = scatter_on(["b_w_gu", "b_w_o"])
    dq, dkp, dvp, db = _att_bwd(qn, kp, vp, bias, do_att, o_att, lse, "b_datt", rider=ride[0])
    reduced(ride)
    g["b_rel_bias"] = _rel_reduce(db, "drel").reshape(1, -1)
    dq_raw, gq = _q_dhnorm(q_raw, qg_t, bd, dq, q_scale, "q_dhnorm")
    g["b_q_norm_g"] = gq.reshape(ATT_HEADS, ATT_DH).sum(axis=0, keepdims=True)
    g["b_w_q"] = _mm(x2, dq_raw, "tn", "b_gq", out_dtype=BF16, norm_g=s["b_norm_g"])
    dx2, g["b_norm_g"] = _mm(dq_raw, w["b_w_q"], "nt", "b_dq", epilogue="rms_bwd",
                             extra=(x2, s["b_norm_g"], dx3))

    dkv, gk = _kv_dprep(kv, kg_t, bd, dkp, dvp, "kv_dprep")
    g["k_norm_g"] = gk.reshape(ATT_HEADS, ATT_DH).sum(axis=0, keepdims=True)
    g["w_kv"] = _mm(x2, dkv, "tn", "kv_g", out_dtype=BF16, out_block=kv_blk, norm_g=s["kv_norm_g"])
    dx2, g["kv_norm_g"] = _mm(dkv, w["w_kv"], "nt", "kv_du", epilogue="rms_bwd",
                              extra=(x2, s["kv_norm_g"], dx2))

    ride = scatter_on(["b_w_q"])
    dgu = _mm(dx2, w["a_w_down"], "nt", "a_ffn_dgu", out_block=ffn_blk, epilogue="swiglu_bwd", extra=gu_a,
              rider=ride[0])
    reduced(ride)
    dgu = dgu.reshape(N_DEV, t, ffn_blk)
    g["a_w_down"] = _mm(act_a, dx2, "tn", "a_ffn_gdown", out_dtype=BF16)
    ride = scatter_on(["a_w_down"])
    dx1, g["a_ffn_norm_g"] = _mm(dgu, w["a_w_gu"], "nn", "a_ffn_dh", epilogue="rms_bwd",
                                 extra=(x1, s["a_ffn_norm_g"], dx2), rider=ride[0])
    reduced(ride)
    ride = scatter_on(["w_kv"])
    g["a_w_gu"] = _mm(dgu, x1, "tn", "a_ffn_ggu", out_dtype=BF16, norm_g=s["a_ffn_norm_g"], norm_b=True,
                      rider=ride[0])
    reduced(ride)

    swap = _SiblingSwapRider([_blocks(g["a_w_gu"])])
    dy_ret = _mm(dx1, w["a_w_o"], "nt", "a_dout", rider=swap)
    g["a_w_o"] = _mm(y, dx1, "tn", "a_gout", out_dtype=BF16)
    chips = _ChipScatterRider([_pair_add(_blocks(g["a_w_gu"]), swap.results[0], parity, "rs_pair_add_gu")])
    dproj, g["a_gn_g"] = _ret_bwd(proj, s["a_gn_g"], o_ret, states, dy_ret, consts, "a_dret", rider=chips)
    recv["a_w_gu"] = chips.results[0]
    ride = scatter_on(["a_w_o"])
    g["a_w_in"] = _mm(x, dproj, "tn", "a_gin", out_dtype=BF16, out_block=in_blk, norm_g=s["a_norm_g"],
                      rider=ride[0])
    reduced(ride)
    from_sibling = _exchange(_SiblingSwapRider([g["a_w_in"]]), "rs_sibling")[0]
    chip_sums = _pair_add(g["a_w_in"], from_sibling, parity, "rs_pair_add")
    last = _ChipScatterRider([chip_sums])
    grad_x, g["a_norm_g"] = _mm(dproj, w["a_w_in"], "nt", "a_dproj", epilogue="rms_bwd",
                                extra=(x, s["a_norm_g"], dx1), rider=last)
    recv["a_w_in"] = last.results[0]
    return loss, grad_x, recv, g


ARG_NAMES = ("x", "a_norm_g", "a_w_in", "a_gn_g", "a_w_o", "a_ffn_norm_g", "a_w_gu", "a_w_down",
             "kv_norm_g", "w_kv", "k_norm_g", "b_norm_g", "b_w_q", "b_q_norm_g", "b_rel_bias", "b_w_o",
             "b_ffn_norm_g", "b_w_gu", "b_w_down")
WEIGHT_NAMES = ARG_NAMES[1:]


def _big_shard(a, name):
    a = a[0] if a.ndim == 3 else a
    return a.T if name in TRANSPOSED else a


def _as_given(a, name, shape):
    return (a.T if name in TRANSPOSED else a).reshape(shape)


def kernel(x, a_norm_g, a_w_in, a_gn_g, a_w_o, a_ffn_norm_g, a_w_gu, a_w_down, kv_norm_g, w_kv, k_norm_g, b_norm_g, b_w_q, b_q_norm_g, b_rel_bias, b_w_o, b_ffn_norm_g, b_w_gu, b_w_down, loss_target, m_a_norm_g, m_a_w_in, m_a_gn_g, m_a_w_o, m_a_ffn_norm_g, m_a_w_gu, m_a_w_down, m_kv_norm_g, m_w_kv, m_k_norm_g, m_b_norm_g, m_b_w_q, m_b_q_norm_g, m_b_rel_bias, m_b_w_o, m_b_ffn_norm_g, m_b_w_gu, m_b_w_down, v_a_norm_g, v_a_w_in, v_a_gn_g, v_a_w_o, v_a_ffn_norm_g, v_a_w_gu, v_a_w_down, v_kv_norm_g, v_w_kv, v_k_norm_g, v_b_norm_g, v_b_w_q, v_b_q_norm_g, v_b_rel_bias, v_b_w_o, v_b_ffn_norm_g, v_b_w_gu, v_b_w_down):
    args = (x, a_norm_g, a_w_in, a_gn_g, a_w_o, a_ffn_norm_g, a_w_gu, a_w_down, kv_norm_g, w_kv, k_norm_g,
            b_norm_g, b_w_q, b_q_norm_g, b_rel_bias, b_w_o, b_ffn_norm_g, b_w_gu, b_w_down)
    p = dict(zip(ARG_NAMES, args))
    m_all = dict(zip(WEIGHT_NAMES, (m_a_norm_g, m_a_w_in, m_a_gn_g, m_a_w_o, m_a_ffn_norm_g, m_a_w_gu,
                                    m_a_w_down, m_kv_norm_g, m_w_kv, m_k_norm_g, m_b_norm_g, m_b_w_q,
                                    m_b_q_norm_g, m_b_rel_bias, m_b_w_o, m_b_ffn_norm_g, m_b_w_gu, m_b_w_down)))
    v_all = dict(zip(WEIGHT_NAMES, (v_a_norm_g, v_a_w_in, v_a_gn_g, v_a_w_o, v_a_ffn_norm_g, v_a_w_gu,
                                    v_a_w_down, v_kv_norm_g, v_w_kv, v_k_norm_g, v_b_norm_g, v_b_w_q,
                                    v_b_q_norm_g, v_b_rel_bias, v_b_w_o, v_b_ffn_norm_g, v_b_w_gu, v_b_w_down)))
    xi, yi, ci = _my_place()
    me = 4 * xi + 2 * yi + ci
    big_names = [n for n, _ in BIG]

    big_local = {n: _big_shard(p[n], n) for n in big_names}
    shards = {n: a.astype(BF16) for n, a in big_local.items()}
    small_local = _pack_small({n: p[n] for n, _, _ in SMALL})
    small_all = _exchange(_GatherRider([small_local]), "gather_small")[0]
    flat_g = small_all.reshape(N_DEV, -1)
    s_full, pos = {}, 0
    for n, length, sharded in SMALL:
        ln = length // N_DEV if sharded else length
        s_full[n] = flat_g[:, pos:pos + ln].reshape(1, -1) if sharded else p[n].reshape(1, -1)
        pos += ln

    parity = jnp.reshape(ci, (1,)).astype(jnp.int32)
    loss, grad_x, recv, g = _local_step(x[0], loss_target[0], shards, s_full, parity)

    partial = _pack_small({n: g[n] for n, _, _ in SMALL}, last=loss)
    summed = _sum_leading(_exchange(_GatherRider([partial]), "gather_gsmall")[0], "gsmall_sum")
    loss = summed[SMALL_ROWS - 1, SMALL_COLS - 1]
    g_small = _unpack_small(summed, local=False)
    for n, length, sharded in SMALL:
        if sharded:
            g_small[n] = lax.dynamic_slice(g_small[n], (me * (length // N_DEV),), (length // N_DEV,))

    grads, deltas, new_m, new_v = {}, {}, {}, {}
    for n in big_names:
        outs = _adamw(big_local[n], recv[n], _big_shard(m_all[n], n), _big_shard(v_all[n], n), "adamw_" + n)
        grads[n], deltas[n], new_m[n], new_v[n] = (_as_given(a, n, p[n].shape) for a in outs)
    pk = lambda src: _pack_small({n: src[n] for n, _, _ in SMALL})
    outs = _adamw(small_local, pk(g_small)[None], pk(m_all), pk(v_all), "adamw_small")
    g_s, d_s, nm_s, nv_s = (_unpack_small(a, local=True) for a in outs)
    for n, _, _ in SMALL:
        grads[n], deltas[n], new_m[n], new_v[n] = (a[n].reshape(p[n].shape) for a in (g_s, d_s, nm_s, nv_s))

    return (loss, grad_x[None], *[grads[n] for n in WEIGHT_NAMES], *[deltas[n] for n in WEIGHT_NAMES],
            *[new_m[n] for n in WEIGHT_NAMES], *[new_v[n] for n in WEIGHT_NAMES])
```

```python
import numpy as np
import jax
import jax.numpy as jnp
from jax import lax
from jax.experimental import pallas as pl
from jax.experimental.pallas import tpu as pltpu

F32 = jnp.float32
BF16 = jnp.bfloat16

N_DEV = 8
D_MODEL = 1024
CHUNK = 64
EPS = 1e-6
RET_HEADS, RET_DK, RET_DV = 4, 256, 512
RET_STEP = 4
RET_Q_COLS = RET_HEADS * RET_DK
RET_V_COLS = RET_HEADS * RET_DV
ATT_HEADS, ATT_DH = 16, 64
PAST_CHUNKS = 8
REL_CLIP = 256
REL_TABLE = 2 * REL_CLIP + 1
FFN_HIDDEN = 2816
ROPE_BASE = 10000.0
LANES = 128
Q_BLOCK = 256
ATT_SUBS = 4
ATT_ROWS = 32
K_PAD = PAST_CHUNKS * CHUNK
K_WINDOW = Q_BLOCK + K_PAD
REL_BLK = 128
REL_DELTAS = Q_BLOCK // REL_BLK + K_WINDOW // REL_BLK - 1
REL_PAD = 640
NEG = -1e30
VMEM_LIMIT_V7X = 56 * 1024 * 1024
ADAM_LR, ADAM_B1, ADAM_B2, ADAM_EPS, ADAM_WD, ADAM_STEP = 1e-3, 0.9, 0.999, 1e-8, 0.01, 10
MESH = pl.DeviceIdType.MESH
ANY = pl.BlockSpec(memory_space=pl.ANY)


def _params(*semantics):
    return pltpu.CompilerParams(dimension_semantics=semantics, vmem_limit_bytes=VMEM_LIMIT_V7X)


def _pick(dim, cap, align):
    best = None
    for t in range(align, min(dim, cap) + 1, align):
        if dim % t == 0:
            best = t
    assert best is not None, (dim, cap, align)
    return best


def _dot(a, b):
    return lax.dot_general(a, b, (((1,), (0,)), ((), ())), preferred_element_type=F32)


def _dot_nt(a, b):
    return lax.dot_general(a, b, (((1,), (1,)), ((), ())), preferred_element_type=F32)


def _dot_tn(a, b):
    return lax.dot_general(a, b, (((0,), (0,)), ((), ())), preferred_element_type=F32)


def _split2(x):
    hi = x.astype(BF16)
    lo = (x - hi.astype(F32)).astype(BF16)
    return hi, lo


def _split3(x):
    hi = x.astype(BF16)
    r = x - hi.astype(F32)
    mid = r.astype(BF16)
    lo = (r - mid.astype(F32)).astype(BF16)
    return hi, mid, lo


def _sigmoid(x):
    return 1.0 / (1.0 + jnp.exp(-x))


def _accumulate(ref, part, step):
    @pl.when(step == 0)
    def _():
        ref[...] = part

    @pl.when(step > 0)
    def _():
        ref[...] += part


RELAY_AT_NUM, RELAY_AT_DEN = 3, 4


def _my_place():
    return lax.axis_index("x"), lax.axis_index("y"), lax.axis_index("c")


def _flip(v, bit):
    return 1 - v if bit else v


class _NoRelay:
    def relay(self, in_refs, out_refs, sems):
        pass


class _GatherRider:
    def __init__(self, xs):
        self.inputs = list(xs)
        n = len(xs)
        self.out_shape = [jax.ShapeDtypeStruct((N_DEV,) + x.shape, x.dtype) for x in xs]
        self.scratch = [pltpu.SemaphoreType.DMA((7, n)), pltpu.SemaphoreType.DMA((7, n)),
                        pltpu.SemaphoreType.DMA((n,))]
        self.results = None

    def _copies(self, x_refs, out_refs, sems):
        send_sems, recv_sems, local_sems = sems
        n = len(x_refs)
        x, y, c = _my_place()
        me, sibling = (x, y, c), (x, y, 1 - c)
        chips = [(1 - x, y), (x, 1 - y), (1 - x, 1 - y)]

        def slot(a, px, py, pc):
            return out_refs[a].at[4 * px + 2 * py + pc]

        def copy(k, a, block, to, own=False):
            return pltpu.make_async_remote_copy(
                src_ref=x_refs[a] if own else slot(a, *block), dst_ref=slot(a, *block),
                send_sem=send_sems.at[k, a], recv_sem=recv_sems.at[k, a],
                device_id=to, device_id_type=MESH)

        mine = [pltpu.make_async_copy(x_refs[a], slot(a, *me), local_sems.at[a]) for a in range(n)]
        first = []
        for a in range(n):
            first.append(copy(0, a, me, sibling, own=True))
            first += [copy(1 + j, a, me, (*chip, c), own=True) for j, chip in enumerate(chips)]
        return n, c, me, sibling, chips, copy, mine, first

    def start(self, x_refs, out_refs, sems):
        _, _, _, _, _, _, mine, first = self._copies(x_refs, out_refs, sems)
        for cp in mine + first:
            cp.start()

    def relay(self, x_refs, out_refs, sems):
        n, c, me, sibling, chips, copy, _, _ = self._copies(x_refs, out_refs, sems)
        for j, chip in enumerate(chips):
            for a in range(n):
                copy(1 + j, a, (*chip, c), me).wait_recv()
                copy(4 + j, a, (*chip, c), sibling).start()

    def finish(self, x_refs, out_refs, sems):
        n, c, me, sibling, chips, copy, mine, first = self._copies(x_refs, out_refs, sems)
        passed = [copy(4 + j, a, (*chip, c), sibling) for j, chip in enumerate(chips) for a in range(n)]
        for a in range(n):
            copy(0, a, sibling, me).wait_recv()
            for j, chip in enumerate(chips):
                copy(4 + j, a, (*chip, 1 - c), me).wait_recv()
        for cp in first + passed:
            cp.wait_send()
        for cp in mine:
            cp.wait()


class _ScatterRider(_NoRelay):
    def __init__(self, gs):
        self.inputs = list(gs)
        n = len(gs)
        self.out_shape = [jax.ShapeDtypeStruct(g.shape, g.dtype) for g in gs]
        self.scratch = [pltpu.SemaphoreType.DMA((7, n)), pltpu.SemaphoreType.DMA((7, n)),
                        pltpu.SemaphoreType.DMA((n,))]
        self.results = None

    def _copies(self, g_refs, out_refs, sems):
        send_sems, recv_sems, local_sems = sems
        x, y, c = _my_place()
        me = 4 * x + 2 * y + c
        mine, copies = [], []
        for a in range(len(g_refs)):
            mine.append(pltpu.make_async_copy(g_refs[a].at[me], out_refs[a].at[me], local_sems.at[a]))
            for k in range(1, N_DEV):
                px, py, pc = _flip(x, k & 4), _flip(y, k & 2), _flip(c, k & 1)
                copies.append(pltpu.make_async_remote_copy(
                    src_ref=g_refs[a].at[4 * px + 2 * py + pc], dst_ref=out_refs[a].at[me],
                    send_sem=send_sems.at[k - 1, a], recv_sem=recv_sems.at[k - 1, a],
                    device_id=(px, py, pc), device_id_type=MESH))
        return mine, copies

    def start(self, g_refs, out_refs, sems):
        mine, copies = self._copies(g_refs, out_refs, sems)
        for cp in mine + copies:
            cp.start()

    def finish(self, g_refs, out_refs, sems):
        mine, copies = self._copies(g_refs, out_refs, sems)
        for cp in copies + mine:
            cp.wait()


class _SiblingSwapRider(_NoRelay):
    def __init__(self, gs):
        self.inputs = list(gs)
        n = len(gs)
        self.out_shape = [jax.ShapeDtypeStruct((4,) + g.shape[1:], g.dtype) for g in gs]
        self.scratch = [pltpu.SemaphoreType.DMA((4, n)), pltpu.SemaphoreType.DMA((4, n))]
        self.results = None

    def _copies(self, g_refs, out_refs, sems):
        send_sems, recv_sems = sems
        x, y, c = _my_place()
        return [pltpu.make_async_remote_copy(
            src_ref=g_refs[a].at[2 * k + 1 - c], dst_ref=out_refs[a].at[k],
            send_sem=send_sems.at[k, a], recv_sem=recv_sems.at[k, a],
            device_id=(x, y, 1 - c), device_id_type=MESH)
            for a in range(len(g_refs)) for k in range(4)]

    def start(self, g_refs, out_refs, sems):
        for cp in self._copies(g_refs, out_refs, sems):
            cp.start()

    def finish(self, g_refs, out_refs, sems):
        for cp in self._copies(g_refs, out_refs, sems):
            cp.wait()


class _ChipScatterRider(_NoRelay):
    def __init__(self, ps):
        self.inputs = list(ps)
        n = len(ps)
        self.out_shape = [jax.ShapeDtypeStruct(p.shape, p.dtype) for p in ps]
        self.scratch = [pltpu.SemaphoreType.DMA((3, n)), pltpu.SemaphoreType.DMA((3, n)),
                        pltpu.SemaphoreType.DMA((n,))]
        self.results = None

    def _copies(self, p_refs, out_refs, sems):
        send_sems, recv_sems, local_sems = sems
        x, y, c = _my_place()
        my_chip = 2 * x + y
        chips = [(1 - x, y), (x, 1 - y), (1 - x, 1 - y)]
        n = len(p_refs)
        mine = [pltpu.make_async_copy(p_refs[a].at[my_chip], out_refs[a].at[my_chip], local_sems.at[a])
                for a in range(n)]
        copies = [pltpu.make_async_remote_copy(
            src_ref=p_refs[a].at[2 * cx + cy], dst_ref=out_refs[a].at[my_chip],
            send_sem=send_sems.at[j, a], recv_sem=recv_sems.at[j, a],
            device_id=(cx, cy, c), device_id_type=MESH)
            for a in range(n) for j, (cx, cy) in enumerate(chips)]
        return mine, copies

    def start(self, p_refs, out_refs, sems):
        mine, copies = self._copies(p_refs, out_refs, sems)
        for cp in mine + copies:
            cp.start()

    def finish(self, p_refs, out_refs, sems):
        mine, copies = self._copies(p_refs, out_refs, sems)
        for cp in copies + mine:
            cp.wait()


def _call(body, name, grid, in_specs, out_specs, out_shape, scratch, semantics, args, rider=None):
    in_specs, out_specs, out_shape, scratch = list(in_specs), list(out_specs), list(out_shape), list(scratch)
    if rider is None:
        return list(pl.pallas_call(
            body, name=name, grid=grid, in_specs=in_specs, out_specs=out_specs, out_shape=out_shape,
            scratch_shapes=scratch, compiler_params=_params(*semantics))(*args))
    n_in, n_out, n_scr = len(in_specs), len(out_specs), len(scratch)
    r_in, r_out = len(rider.inputs), len(rider.out_shape)

    def wrapped(*refs):
        cuts = np.cumsum([0, n_in, r_in, n_out, r_out, n_scr])
        hi, ri, ho, ro, hs = (refs[cuts[i]:cuts[i + 1]] for i in range(5))
        rs = refs[cuts[5]:]
        step, steps = pl.program_id(0), grid[0]
        for d in range(1, len(grid)):
            step, steps = step * grid[d] + pl.program_id(d), steps * grid[d]

        @pl.when(step == 0)
        def _():
            rider.start(ri, ro, rs)

        body(*hi, *ho, *hs)

        @pl.when(step == (steps * RELAY_AT_NUM) // RELAY_AT_DEN)
        def _():
            rider.relay(ri, ro, rs)

        @pl.when(step == steps - 1)
        def _():
            rider.finish(ri, ro, rs)

    outs = pl.pallas_call(
        wrapped, name=name, grid=grid,
        in_specs=in_specs + [ANY] * r_in, out_specs=out_specs + [ANY] * r_out,
        out_shape=out_shape + rider.out_shape, scratch_shapes=scratch + rider.scratch,
        compiler_params=_params(*(["arbitrary"] * len(grid))),
    )(*args, *rider.inputs)
    rider.results = list(outs[n_out:])
    return list(outs[:n_out])


_WALK = ((None, None), (0, None), (1, 4), (2, 5), (4, None), (5, None), (3, 6), (6, None))


def _gather_order():
    x, y, c = _my_place()
    (ax, ay), (bx, by), (dx, dy) = (1 - x, y), (x, 1 - y), (1 - x, 1 - y)
    ids = [(x, y, c), (x, y, 1 - c), (ax, ay, c), (bx, by, c), (ax, ay, 1 - c), (bx, by, 1 - c),
           (dx, dy, c), (dx, dy, 1 - c)]
    return jnp.stack([4 * px + 2 * py + pc for px, py, pc in ids]).astype(jnp.int32)


def _proj_gather(x, norm_g, w_shard, extras, name):
    t, d = x.shape
    cols = w_shard.shape[1]
    tm = _pick(t, MM_CAP_MN, 16)
    ni = t // tm
    n = 1 + len(extras)
    rider = _GatherRider([w_shard] + list(extras))

    def body(ord_ref, x_ref, g_ref, *refs):
        sh_refs, proj_ref, gathered = refs[:n], refs[n], refs[n + 1:2 * n + 1]
        h_all, bbuf, bsem, send_sems, recv_sems, local_sems = refs[2 * n + 1:]
        j, i = pl.program_id(0), pl.program_id(1)
        _, c, me, sibling, chips, copy, mine, first = rider._copies(
            sh_refs, gathered, (send_sems, recv_sems, local_sems))
        rows = pl.ds(pl.multiple_of(i * tm, tm), tm)

        def load(step, src):
            return pltpu.make_async_copy(src, bbuf.at[step % 2], bsem.at[step % 2])

        def relayed(k, a):
            return copy(k, a, (*chips[k - 4], c), sibling)

        @pl.when(jnp.logical_and(j == 0, i == 0))
        def _():
            for cp in mine + first:
                cp.start()
            load(0, sh_refs[0]).start()

        @pl.when(i == 0)
        def _():
            load(j, sh_refs[0]).wait()

        @pl.when(j == 0)
        def _():
            groups = []
            for r in range(0, tm, NORM_ROWS):
                xv = x_ref[r:r + NORM_ROWS, :]
                rstd = lax.rsqrt(jnp.mean(xv * xv, axis=-1, keepdims=True) + EPS)
                groups.append((xv * rstd * g_ref[...]).astype(BF16))
            h_all[rows, :] = jnp.concatenate(groups, axis=0)

        proj_ref[...] = _dot(h_all[rows, :], bbuf[j % 2])

        for step in range(N_DEV - 1):
            @pl.when(jnp.logical_and(j == step, i == max(ni - 2, 0)))
            def _(step=step):
                need, relay = _WALK[step + 1]
                copy(need, 0, me, me).wait_recv()
                if relay is not None:
                    relayed(relay, 0).start()
                load(step + 1, gathered[0].at[ord_ref[step + 1]]).start()

        @pl.when(jnp.logical_and(j == N_DEV - 1, i == ni - 1))
        def _():
            for a in range(1, n):
                for k in range(3):
                    copy(1 + k, a, me, me).wait_recv()
                    relayed(4 + k, a).start()
            for a in range(1, n):
                for k in (0, 4, 5, 6):
                    copy(k, a, me, me).wait_recv()
            for cp in first + [relayed(4 + k, a) for a in range(n) for k in range(3)]:
                cp.wait_send()
            for cp in mine:
                cp.wait()

    outs = pl.pallas_call(
        body, name=name,
        grid_spec=pltpu.PrefetchScalarGridSpec(
            num_scalar_prefetch=1, grid=(N_DEV, ni),
            in_specs=[pl.BlockSpec((tm, d), lambda j, i, o: (jnp.where(j == 0, i, ni - 1), 0)),
                      pl.BlockSpec((1, d), lambda j, i, o: (0, 0))] + [ANY] * n,
            out_specs=[pl.BlockSpec((tm, cols), lambda j, i, o: (i, o[j]))] + [ANY] * n,
            scratch_shapes=[pltpu.VMEM((t, d), BF16), pltpu.VMEM((2, d, cols), BF16),
                            pltpu.SemaphoreType.DMA((2,))] + rider.scratch),
        out_shape=[jax.ShapeDtypeStruct((t, N_DEV * cols), F32)] + rider.out_shape,
        compiler_params=_params("arbitrary", "arbitrary"),
    )(_gather_order(), x, norm_g, w_shard, *extras)
    return outs[0], list(outs[1:])


def _exchange(rider, name):
    r_in, r_out = len(rider.inputs), len(rider.out_shape)

    def body(*refs):
        ri, ro, rs = refs[:r_in], refs[r_in:r_in + r_out], refs[r_in + r_out:]
        rider.start(ri, ro, rs)
        rider.relay(ri, ro, rs)
        rider.finish(ri, ro, rs)

    return list(pl.pallas_call(
        body, name=name, in_specs=[ANY] * r_in, out_specs=[ANY] * r_out,
        out_shape=rider.out_shape, scratch_shapes=rider.scratch)(*rider.inputs))


MM_CAP_MN = 1024
MM_CAP_M_GRAD = 1408
MM_CAP_N = 1536
MM_CAP_K = 3072
MM_CAP_K_TOKENS = 2048
MM_CAP_K_RMS = 8192
MM_CAP_M_RMS = 512
NORM_ROWS = 256


def _mm(a, b, mode, name, out_dtype=F32, res=None, out_block=None, epilogue=None, extra=None, norm_g=None,
        norm_b=False, rider=None):
    a3, b3 = a.ndim == 3, b.ndim == 3
    um = un = uk = None
    if mode in ("nn", "nt"):
        if a3:
            m, uk = a.shape[1:]
            k = a.shape[0] * uk
        else:
            m, k = a.shape
    else:
        if a3:
            k, um = a.shape[1:]
            m = a.shape[0] * um
        else:
            k, m = a.shape
    if mode in ("nn", "tn"):
        if b3:
            kb, un = b.shape[1:]
            n = b.shape[0] * un
        else:
            kb, n = b.shape
        assert kb == k, (a.shape, b.shape, mode)
    else:
        if b3:
            n, ukb = b.shape[1:]
            assert b.shape[0] * ukb == k and uk in (None, ukb), (a.shape, b.shape, mode)
            uk = ukb
        else:
            n, kb = b.shape
            assert kb == k, (a.shape, b.shape, mode)
    if out_block is not None:
        assert un in (None, out_block)
        un = out_block

    def tile(dim, unit, cap, align):
        if unit is None:
            return _pick(dim, cap, align), 1
        c = max(1, cap // unit)
        while (dim // unit) % c:
            c -= 1
        return unit, c

    cap_m = MM_CAP_M_GRAD if mode == "tn" else (MM_CAP_M_RMS if epilogue == "rms_bwd" else MM_CAP_MN)
    um, cm = tile(m, um, cap_m, 128 if mode == "tn" else 16)
    un, cn = tile(n, un, MM_CAP_N, 128)
    cap_k = MM_CAP_K_TOKENS if mode == "tn" else (MM_CAP_K_RMS if epilogue == "rms_bwd" else MM_CAP_K)
    uk, ck = tile(k, uk, cap_k, 128)
    if epilogue == "rms_bwd":
        assert mode != "tn" and n == D_MODEL and cm == cn == 1 and res is None and out_block is None
    if epilogue == "loss":
        assert n == D_MODEL and cm == cn == 1 and res is not None and out_block is None
    if norm_g is not None and norm_b:
        assert mode == "tn" and not b3 and n == D_MODEL and cn == 1
    elif norm_g is not None:
        assert not a3 and (m if mode == "tn" else k) == D_MODEL and (cm if mode == "tn" else ck) == 1
    if epilogue == "swiglu":
        assert res is None and ((mode == "nn" and b3 and out_block is None) or
                                (mode == "nt" and not b3 and out_block is not None))
        cn = 2
    if epilogue == "swiglu_bwd":
        assert mode == "nt" and out_block is not None and extra is not None and res is None
        cn = 1
    tm, tn, tk = cm * um, cn * un, ck * uk
    nk = k // tk
    dot = {"nn": _dot, "nt": _dot_nt, "tn": _dot_tn}[mode]
    half = n // un // 2
    blocked_out = out_block is not None or epilogue in ("swiglu", "swiglu_bwd")
    extras = [] if extra is None else (list(extra) if isinstance(extra, (tuple, list)) else [extra])

    def sl(idx, unit, count):
        return slice(None) if count == 1 else slice(idx * unit, (idx + 1) * unit)

    def body(*refs):
        a_ref, b_ref = refs[0], refs[1]
        pos = 2
        r_ref = ng_ref = None
        if res is not None:
            r_ref, pos = refs[pos], pos + 1
        e_refs, pos = refs[pos:pos + len(extras)], pos + len(extras)
        if norm_g is not None:
            ng_ref, pos = refs[pos], pos + 1
        outs, acc_ref = refs[pos:-1], refs[-1]
        kk = pl.program_id(2)

        def normed(x_ref):
            groups = []
            for r in range(0, x_ref.shape[0], NORM_ROWS):
                xv = x_ref[r:r + NORM_ROWS, :]
                rstd = lax.rsqrt(jnp.mean(xv * xv, axis=-1, keepdims=True) + EPS)
                groups.append((xv * rstd * ng_ref[...]).astype(BF16))
            return jnp.concatenate(groups, axis=0)

        def a_blk(mi, ki):
            if norm_g is not None and not norm_b:
                return normed(a_ref)
            if mode in ("nn", "nt"):
                return a_ref[ki] if a3 else a_ref[:, sl(ki, uk, ck)]
            return a_ref[mi] if a3 else a_ref[:, sl(mi, um, cm)]

        def b_blk(ki, ni):
            if norm_b:
                return normed(b_ref)
            if epilogue == "swiglu":
                return b_ref[ni, 0]
            if mode in ("nn", "tn"):
                return b_ref[ni] if b3 else b_ref[sl(ki, uk, ck), sl(ni, un, cn)]
            return b_ref[ki][sl(ni, un, cn), :] if b3 else b_ref[sl(ni, un, cn), sl(ki, uk, ck)]

        parts = {}
        for mi in range(cm):
            for ni in range(cn):
                part = None
                for ki in range(ck):
                    d = dot(a_blk(mi, ki).astype(BF16), b_blk(ki, ni).astype(BF16))
                    part = d if part is None else part + d
                parts[mi, ni] = part

        def finish(total):
            if epilogue == "swiglu":
                gate, up = total[0, 0], total[0, 1]
                outs[0][0, 0] = gate.astype(BF16)
                outs[0][1, 0] = up.astype(BF16)
                outs[1][0] = (gate * _sigmoid(gate) * up).astype(BF16)
                return
            if epilogue == "swiglu_bwd":
                dact = total[0, 0]
                gate, up = e_refs[0][0, 0].astype(F32), e_refs[0][1, 0].astype(F32)
                sg = _sigmoid(gate)
                outs[0][0, 0] = (dact * up * (sg * (1.0 + gate * (1.0 - sg)))).astype(BF16)
                outs[0][1, 0] = (dact * (gate * sg)).astype(BF16)
                return
            if epilogue == "rms_bwd":
                x_ref, g_ref, dres_ref = e_refs
                dh, dg = total[0, 0], None
                for r in range(0, tm, NORM_ROWS):
                    rows = slice(r, r + NORM_ROWS)
                    xv, dhv = x_ref[rows, :], dh[rows, :]
                    rstd = lax.rsqrt(jnp.mean(xv * xv, axis=-1, keepdims=True) + EPS)
                    xh = xv * rstd
                    dyg = dhv * g_ref[...]
                    c = jnp.mean(dyg * xh, axis=-1, keepdims=True)
                    outs[0][rows, :] = dres_ref[rows, :] + rstd * (dyg - xh * c)
                    part = jnp.sum(dhv * xh, axis=0, keepdims=True)
                    dg = part if dg is None else dg + part
                _accumulate(outs[1], dg, pl.program_id(0))
                return
            if epilogue == "loss":
                diff = r_ref[...] + total[0, 0] - e_refs[0][...]
                outs[0][...] = diff * (1.0 / n)
                sq = jnp.sum(jnp.sum(diff * diff, axis=-1, keepdims=True), axis=0, keepdims=True)
                _accumulate(outs[1], sq * (0.5 / n), pl.program_id(0))
                return
            for (mi, ni), val in total.items():
                rows, cols = sl(mi, um, cm), sl(ni, un, cn)
                if res is not None:
                    val = r_ref[rows, cols] + val
                if blocked_out:
                    outs[0][ni, rows] = val.astype(out_dtype)
                else:
                    outs[0][rows, cols] = val.astype(out_dtype)

        if nk == 1:
            finish(parts)
        else:
            @pl.when(kk == 0)
            def _():
                for (mi, ni), val in parts.items():
                    acc_ref[mi * cn + ni] = val

            @pl.when(jnp.logical_and(kk > 0, kk < nk - 1))
            def _():
                for (mi, ni), val in parts.items():
                    acc_ref[mi * cn + ni] += val

            @pl.when(kk == nk - 1)
            def _():
                finish({key: acc_ref[key[0] * cn + key[1]] + val for key, val in parts.items()})

    if mode in ("nn", "nt"):
        a_spec = (pl.BlockSpec((ck, tm, uk), lambda i, j, kk: (kk, i, 0)) if a3
                  else pl.BlockSpec((tm, tk), lambda i, j, kk: (i, kk)))
    else:
        a_spec = (pl.BlockSpec((cm, tk, um), lambda i, j, kk: (i, kk, 0)) if a3
                  else pl.BlockSpec((tk, tm), lambda i, j, kk: (kk, i)))
    pair_spec = pl.BlockSpec((2, 1, tm, un), lambda i, j, kk: (0, j, i, 0))
    row_spec = pl.BlockSpec((tm, tn), lambda i, j, kk: (i, 0))
    vec_spec = pl.BlockSpec((1, tn), lambda i, j, kk: (0, 0))
    if epilogue == "swiglu" and mode == "nn":
        b = b.reshape(2, half, k, un)
        b_spec = pl.BlockSpec((2, 1, tk, un), lambda i, j, kk: (0, j, kk, 0))
    elif epilogue == "swiglu":
        b = b.reshape(2, half, un, k)
        b_spec = pl.BlockSpec((2, 1, un, tk), lambda i, j, kk: (0, j, 0, kk))
    elif mode in ("nn", "tn"):
        b_spec = (pl.BlockSpec((cn, tk, un), lambda i, j, kk: (j, kk, 0)) if b3
                  else pl.BlockSpec((tk, tn), lambda i, j, kk: (kk, j)))
    else:
        b_spec = (pl.BlockSpec((ck, tn, uk), lambda i, j, kk: (kk, j, 0)) if b3
                  else pl.BlockSpec((tn, tk), lambda i, j, kk: (j, kk)))
    if epilogue == "swiglu":
        out_specs = [pair_spec, pl.BlockSpec((1, tm, un), lambda i, j, kk: (j, i, 0))]
        out_shape = [jax.ShapeDtypeStruct((2, half, m, un), BF16), jax.ShapeDtypeStruct((half, m, un), BF16)]
    elif epilogue == "swiglu_bwd":
        out_specs = [pair_spec]
        out_shape = [jax.ShapeDtypeStruct(extra.shape, BF16)]
    elif epilogue == "rms_bwd":
        out_specs = [row_spec, vec_spec]
        out_shape = [jax.ShapeDtypeStruct((m, n), F32), jax.ShapeDtypeStruct((1, n), F32)]
    elif epilogue == "loss":
        out_specs = [row_spec, pl.BlockSpec((1, 1), lambda i, j, kk: (0, 0))]
        out_shape = [jax.ShapeDtypeStruct((m, n), F32), jax.ShapeDtypeStruct((1, 1), F32)]
    elif blocked_out:
        out_specs = [pl.BlockSpec((cn, tm, un), lambda i, j, kk: (j, i, 0))]
        out_shape = [jax.ShapeDtypeStruct((n // un, m, un), out_dtype)]
    else:
        out_specs = [pl.BlockSpec((tm, tn), lambda i, j, kk: (i, j))]
        out_shape = [jax.ShapeDtypeStruct((m, n), out_dtype)]
    in_specs, args = [a_spec, b_spec], [a, b]
    if res is not None:
        in_specs.append(pl.BlockSpec((tm, tn), lambda i, j, kk: (i, j)))
        args.append(res)
    if epilogue == "swiglu_bwd":
        in_specs.append(pair_spec)
    elif epilogue == "rms_bwd":
        in_specs += [row_spec, vec_spec, row_spec]
    elif epilogue == "loss":
        in_specs.append(row_spec)
    args += extras
    if norm_g is not None:
        in_specs.append(pl.BlockSpec((1, D_MODEL), lambda i, j, kk: (0, 0)))
        args.append(norm_g)
    semantics = ("arbitrary",) * 3 if epilogue in ("rms_bwd", "loss") else ("parallel", "parallel", "arbitrary")
    out = _call(body, name, (m // tm, n // tn, nk), in_specs, out_specs, out_shape,
                [pltpu.VMEM((cm * cn, um, un), F32)], semantics, args, rider)
    return out if epilogue in ("swiglu", "rms_bwd", "loss") else out[0]


def _head_sums(v, ind):
    return _dot(v.astype(BF16), ind)


def _head_spread(per_head, ind):
    hi, lo = _split2(per_head)
    return _dot_nt(hi, ind) + _dot_nt(lo, ind)


def _head_rstd(xv, ind):
    return _head_spread(lax.rsqrt(_head_sums(xv * xv, ind) * (1.0 / ATT_DH) + EPS), ind)


def _hn_bwd_math(xv, gv, ind, dyv, scale):
    rstd = _head_rstd(xv, ind)
    xh = xv * rstd
    dyn = dyv * scale
    dyg = dyn * gv
    dx = rstd * (dyg - xh * _head_spread(_head_sums(dyg * xh, ind) * (1.0 / ATT_DH), ind))
    return dx, jnp.sum(dyn * xh, axis=0, keepdims=True)


def _q_hnorm(x, g_tiled, bd, scale, name):
    t, d = x.shape
    tm = _pick(t, 512, 16)

    def body(x_ref, g_ref, bd_ref, o_ref):
        xv = x_ref[...]
        o_ref[...] = (xv * _head_rstd(xv, bd_ref[...]) * g_ref[...] * scale).astype(BF16)

    return pl.pallas_call(
        body, name=name, grid=(t // tm,),
        in_specs=[pl.BlockSpec((tm, d), lambda i: (i, 0)), pl.BlockSpec((1, d), lambda i: (0, 0)),
                  pl.BlockSpec((d, LANES), lambda i: (0, 0))],
        out_specs=pl.BlockSpec((tm, d), lambda i: (i, 0)),
        out_shape=jax.ShapeDtypeStruct((t, d), BF16),
        compiler_params=_params("parallel"),
    )(x, g_tiled, bd)


def _q_dhnorm(x, g_tiled, bd, dy, scale, name):
    t, d = x.shape
    tm = _pick(t, 512, 16)

    def body(x_ref, g_ref, bd_ref, dy_ref, dx_ref, dg_ref):
        dx, part = _hn_bwd_math(x_ref[...], g_ref[...], bd_ref[...], dy_ref[...], scale)
        dx_ref[...] = dx.astype(BF16)
        _accumulate(dg_ref, part, pl.program_id(0))

    row = pl.BlockSpec((tm, d), lambda i: (i, 0))
    vec = pl.BlockSpec((1, d), lambda i: (0, 0))
    return pl.pallas_call(
        body, name=name, grid=(t // tm,),
        in_specs=[row, vec, pl.BlockSpec((d, LANES), lambda i: (0, 0)), row],
        out_specs=[row, vec],
        out_shape=[jax.ShapeDtypeStruct((t, d), BF16), jax.ShapeDtypeStruct((1, d), F32)],
        compiler_params=_params("arbitrary"),
    )(x, g_tiled, bd, dy)


def _kv_prep(kv, g_tiled, bd, name):
    t = kv.shape[0]
    d = D_MODEL
    tm = K_PAD
    assert t % tm == 0

    def body(k_ref, v_ref, g_ref, bd_ref, kp_ref, vp_ref):
        i = pl.program_id(0)

        @pl.when(i == 0)
        def _():
            kp_ref[...] = jnp.zeros_like(kp_ref)
            vp_ref[...] = jnp.zeros_like(vp_ref)

        @pl.when(i > 0)
        def _():
            xv = k_ref[...]
            kp_ref[...] = (xv * _head_rstd(xv, bd_ref[...]) * g_ref[...]).astype(BF16)
            vp_ref[...] = v_ref[...].astype(BF16)

    shp = jax.ShapeDtypeStruct((t + K_PAD, d), BF16)
    out = pl.BlockSpec((tm, d), lambda i: (i, 0))
    return pl.pallas_call(
        body, name=name, grid=(t // tm + 1,),
        in_specs=[pl.BlockSpec((tm, d), lambda i: (jnp.maximum(i - 1, 0), 0)),
                  pl.BlockSpec((tm, d), lambda i: (jnp.maximum(i - 1, 0), 1)),
                  pl.BlockSpec((1, d), lambda i: (0, 0)), pl.BlockSpec((d, LANES), lambda i: (0, 0))],
        out_specs=[out, out], out_shape=[shp, shp],
        compiler_params=_params("arbitrary"),
    )(kv, kv, g_tiled, bd)


def _kv_dprep(kv, g_tiled, bd, dkp_t, dvp_t, name):
    t = kv.shape[0]
    d = D_MODEL
    tm = K_PAD

    def body(k_ref, g_ref, bd_ref, dk_ref, dv_ref, o_ref, dg_ref):
        dx, part = _hn_bwd_math(k_ref[...], g_ref[...], bd_ref[...], dk_ref[...].T, 1.0)
        o_ref[:, :d] = dx.astype(BF16)
        o_ref[:, d:] = dv_ref[...].T.astype(BF16)
        _accumulate(dg_ref, part, pl.program_id(0))

    vec = pl.BlockSpec((1, d), lambda i: (0, 0))
    padded = pl.BlockSpec((d, tm), lambda i: (0, i + 1))
    return pl.pallas_call(
        body, name=name, grid=(t // tm,),
        in_specs=[pl.BlockSpec((tm, d), lambda i: (i, 0)), vec, pl.BlockSpec((d, LANES), lambda i: (0, 0)),
                  padded, padded],
        out_specs=[pl.BlockSpec((tm, 2 * d), lambda i: (i, 0)), vec],
        out_shape=[jax.ShapeDtypeStruct((t, 2 * d), BF16), jax.ShapeDtypeStruct((1, d), F32)],
        compiler_params=_params("arbitrary"),
    )(kv, g_tiled, bd, dkp_t, dvp_t)


def _ret_consts(t):
    h = np.arange(RET_HEADS, dtype=np.float32)
    lg = np.log(np.float32(1.0) - np.float32(2.0) ** (np.float32(-5.0) - h)).astype(np.float32)
    tt = np.arange(CHUNK, dtype=np.float32)
    intra = np.exp(lg[:, None, None] * np.abs(tt[:, None] - tt[None, :])).astype(np.float32)
    q_dec = np.exp(lg[:, None] * (tt + 1.0)).astype(np.float32)
    k_dec = np.exp(lg[:, None] * (CHUNK - 1.0 - tt)).astype(np.float32)
    s_dec = [float(v) for v in np.exp(lg * np.float32(CHUNK)).astype(np.float32)]
    qd = np.broadcast_to(q_dec[:, :, None], (RET_HEADS, CHUNK, RET_DK)).copy()
    kd = np.broadcast_to(k_dec[:, :, None], (RET_HEADS, CHUNK, RET_DK)).copy()
    half = RET_DK // 2
    inv_freq = ROPE_BASE ** (-jnp.arange(half, dtype=F32) / half)
    ang = jnp.arange(t).astype(F32)[:, None] * inv_freq[None, :]
    return jnp.asarray(intra), jnp.asarray(qd), jnp.asarray(kd), s_dec, jnp.cos(ang), jnp.sin(ang)


def _rope(x, cos, sin):
    half = RET_DK // 2
    x1, x2 = x[:, :half], x[:, half:]
    return jnp.concatenate([x1 * cos - x2 * sin, x1 * sin + x2 * cos], axis=-1)


def _unrope(d, cos, sin):
    half = RET_DK // 2
    d1, d2 = d[:, :half], d[:, half:]
    return jnp.concatenate([d1 * cos + d2 * sin, d2 * cos - d1 * sin], axis=-1)


def _ret_slices(h):
    q = slice(h * RET_DK, (h + 1) * RET_DK)
    k = slice(RET_Q_COLS + h * RET_DK, RET_Q_COLS + (h + 1) * RET_DK)
    v = slice(2 * RET_Q_COLS + h * RET_DV, 2 * RET_Q_COLS + (h + 1) * RET_DV)
    g = slice(2 * RET_Q_COLS + RET_V_COLS + h * RET_DV, 2 * RET_Q_COLS + RET_V_COLS + (h + 1) * RET_DV)
    o = slice(h * RET_DV, (h + 1) * RET_DV)
    return q, k, v, g, o


def _ret_fwd(proj, gn, consts, name, rider=None):
    t, cols = proj.shape
    n = t // CHUNK
    intra, qd, kd, s_dec, cos, sin = consts
    k_scale = RET_DK ** -0.5

    def body(p_ref, cos_ref, sin_ref, intra_ref, qd_ref, kd_ref, gn_ref, y_ref, o_ref, st_ref, state):
        i = pl.program_id(0)

        @pl.when(i == 0)
        def _():
            state[...] = jnp.zeros_like(state)

        for c in range(RET_STEP):
            rows = slice(c * CHUNK, (c + 1) * CHUNK)
            cosv, sinv = cos_ref[rows, :], sin_ref[rows, :]
            for h in range(RET_HEADS):
                qs, ks, vs, gs, os_ = _ret_slices(h)
                qr = _rope(p_ref[rows, qs], cosv, sinv)
                kr = _rope(p_ref[rows, ks], cosv, sinv) * k_scale
                vb = p_ref[rows, vs].astype(BF16)
                gv = p_ref[rows, gs]
                scores = _dot_nt(qr.astype(BF16), kr.astype(BF16)) * intra_ref[h]
                s_old = state[h]
                s_old_b = s_old.astype(BF16)
                st_ref[c, h] = s_old_b
                o = _dot(scores.astype(BF16), vb) + _dot((qr * qd_ref[h]).astype(BF16), s_old_b)
                state[h] = s_old * s_dec[h] + _dot_tn((kr * kd_ref[h]).astype(BF16), vb)
                rstd = lax.rsqrt(jnp.mean(o * o, axis=-1, keepdims=True) + EPS)
                on = o * rstd * gn_ref[:, os_]
                o_ref[rows, os_] = o
                y_ref[rows, os_] = (gv * _sigmoid(gv) * on).astype(BF16)

    full3 = lambda a: pl.BlockSpec(a.shape, lambda i: (0, 0, 0))
    step = RET_STEP * CHUNK
    return _call(
        body, name, (n // RET_STEP,),
        [pl.BlockSpec((step, cols), lambda i: (i, 0)),
         pl.BlockSpec((step, RET_DK // 2), lambda i: (i, 0)),
         pl.BlockSpec((step, RET_DK // 2), lambda i: (i, 0)),
         full3(intra), full3(qd), full3(kd),
         pl.BlockSpec((1, RET_V_COLS), lambda i: (0, 0))],
        [pl.BlockSpec((step, RET_V_COLS), lambda i: (i, 0)),
         pl.BlockSpec((step, RET_V_COLS), lambda i: (i, 0)),
         pl.BlockSpec((RET_STEP, RET_HEADS, RET_DK, RET_DV), lambda i: (i, 0, 0, 0))],
        [jax.ShapeDtypeStruct((t, RET_V_COLS), BF16),
         jax.ShapeDtypeStruct((t, RET_V_COLS), F32),
         jax.ShapeDtypeStruct((n, RET_HEADS, RET_DK, RET_DV), BF16)],
        [pltpu.VMEM((RET_HEADS, RET_DK, RET_DV), F32)], ("arbitrary",),
        (proj, cos, sin, intra, qd, kd, gn), rider)


def _ret_bwd(proj, gn, o_saved, states, dy, consts, name, rider=None):
    t, cols = proj.shape
    n = t // CHUNK
    intra, qd, kd, s_dec, cos, sin = consts
    k_scale = RET_DK ** -0.5

    def body(p_ref, cos_ref, sin_ref, intra_ref, qd_ref, kd_ref, gn_ref, o_ref, st_ref, dy_ref,
             dp_ref, dgn_ref, dstate):
        i = pl.program_id(0)

        @pl.when(i == 0)
        def _():
            dstate[...] = jnp.zeros_like(dstate)

        dgn = None
        for c in reversed(range(RET_STEP)):
            rows = slice(c * CHUNK, (c + 1) * CHUNK)
            cosv, sinv = cos_ref[rows, :], sin_ref[rows, :]
            dgn_parts = []
            for h in range(RET_HEADS):
                qs, ks, vs, gs, os_ = _ret_slices(h)
                qr = _rope(p_ref[rows, qs], cosv, sinv)
                kr = _rope(p_ref[rows, ks], cosv, sinv) * k_scale
                qb, kb = qr.astype(BF16), kr.astype(BF16)
                vb = p_ref[rows, vs].astype(BF16)
                gv = p_ref[rows, gs]
                ov = o_ref[rows, os_]
                dyv = dy_ref[rows, os_]
                gnv = gn_ref[:, os_]
                sg = _sigmoid(gv)
                rstd = lax.rsqrt(jnp.mean(ov * ov, axis=-1, keepdims=True) + EPS)
                oh = ov * rstd
                d_on = dyv * (gv * sg)
                dg = dyv * (oh * gnv) * (sg * (1.0 + gv * (1.0 - sg)))
                dgn_parts.append(jnp.sum(d_on * oh, axis=0, keepdims=True))
                d_oh = d_on * gnv
                do = rstd * (d_oh - oh * jnp.mean(d_oh * oh, axis=-1, keepdims=True))
                dob = do.astype(BF16)
                mask = intra_ref[h]
                a_b = (_dot_nt(qb, kb) * mask).astype(BF16)
                da_b = (_dot_nt(dob, vb) * mask).astype(BF16)
                ds_new = dstate[h]
                ds_new_b = ds_new.astype(BF16)
                s_old_b = st_ref[c, h]
                qdv, kdv = qd_ref[h], kd_ref[h]
                dv = _dot_tn(a_b, dob) + _dot((kr * kdv).astype(BF16), ds_new_b)
                dqr = _dot(da_b, kb) + _dot_nt(dob, s_old_b) * qdv
                dkr = _dot_tn(da_b, qb) + _dot_nt(vb, ds_new_b) * kdv
                dstate[h] = ds_new * s_dec[h] + _dot_tn((qr * qdv).astype(BF16), dob)
                dp_ref[rows, qs] = _unrope(dqr, cosv, sinv).astype(BF16)
                dp_ref[rows, ks] = _unrope(dkr * k_scale, cosv, sinv).astype(BF16)
                dp_ref[rows, vs] = dv.astype(BF16)
                dp_ref[rows, gs] = dg.astype(BF16)
            part = jnp.concatenate(dgn_parts, axis=-1)
            dgn = part if dgn is None else dgn + part
        _accumulate(dgn_ref, dgn, i)

    steps = n // RET_STEP
    step = RET_STEP * CHUNK
    rev = lambda i: (steps - 1 - i, 0)
    full3 = lambda a: pl.BlockSpec(a.shape, lambda i: (0, 0, 0))
    return _call(
        body, name, (steps,),
        [pl.BlockSpec((step, cols), rev),
         pl.BlockSpec((step, RET_DK // 2), rev),
         pl.BlockSpec((step, RET_DK // 2), rev),
         full3(intra), full3(qd), full3(kd),
         pl.BlockSpec((1, RET_V_COLS), lambda i: (0, 0)),
         pl.BlockSpec((step, RET_V_COLS), rev),
         pl.BlockSpec((RET_STEP, RET_HEADS, RET_DK, RET_DV), lambda i: (steps - 1 - i, 0, 0, 0)),
         pl.BlockSpec((step, RET_V_COLS), rev)],
        [pl.BlockSpec((step, cols), rev),
         pl.BlockSpec((1, RET_V_COLS), lambda i: (0, 0))],
        [jax.ShapeDtypeStruct((t, cols), BF16),
         jax.ShapeDtypeStruct((1, RET_V_COLS), F32)],
        [pltpu.VMEM((RET_HEADS, RET_DK, RET_DV), F32)], ("arbitrary",),
        (proj, cos, sin, intra, qd, kd, gn, o_saved, states, dy), rider)


def _att_common(q_ref, kp_ref, vp_ref, sub):
    blk = pl.program_id(1) * ATT_SUBS + sub
    start = pl.multiple_of(blk * Q_BLOCK, Q_BLOCK)
    kw = kp_ref[pl.ds(start, K_WINDOW), :]
    vw = vp_ref[pl.ds(start, K_WINDOW), :]
    kvalid = blk * Q_BLOCK - K_PAD + lax.broadcasted_iota(jnp.int32, (1, K_WINDOW), 1) >= 0
    lane = lax.broadcasted_iota(jnp.int32, (1, LANES), 1)
    qrows = slice(sub * Q_BLOCK, (sub + 1) * Q_BLOCK)
    return start, qrows, q_ref[qrows, :], kw, vw, kvalid, (lane < ATT_DH, lane >= ATT_DH)


def _row_groups():
    return [slice(r * ATT_ROWS, (r + 1) * ATT_ROWS) for r in range(Q_BLOCK // ATT_ROWS)]


def _lane_copies(x):
    return jnp.tile(x, (1, K_WINDOW // LANES))


def _att_specs(t, tp):
    qspec = pl.BlockSpec((ATT_SUBS * Q_BLOCK, LANES), lambda h, i: (i, h))
    kspec = pl.BlockSpec((tp, LANES), lambda h, i: (0, h))
    bspec = pl.BlockSpec((2, Q_BLOCK, K_WINDOW), lambda h, i: (h, 0, 0))
    return qspec, kspec, bspec


def _att_fwd(q, kp, vp, bias, name, rider=None):
    t, d = q.shape
    tp = kp.shape[0]

    def body(q_ref, kp_ref, vp_ref, bias_ref, o_ref, lse_ref, s_scr, p_scr, lse_scr, inv_scr):
        for sub in range(ATT_SUBS):
            _, qrows, q2, kw, vw, kvalid, sel = _att_common(q_ref, kp_ref, vp_ref, sub)
            for hh in range(2):
                s_scr[sub, hh] = _dot_nt(jnp.where(sel[hh], q2, 0), kw)
            for hh in range(2):
                for rows in _row_groups():
                    s = jnp.where(kvalid, s_scr[sub, hh, rows, :] + bias_ref[hh, rows, :], NEG)
                    m = jnp.max(s, axis=-1, keepdims=True)
                    e = jnp.exp(s - m)
                    l = jnp.sum(e, axis=-1, keepdims=True)
                    p_scr[sub, hh, rows, :] = e.astype(BF16)
                    inv_scr[sub, hh, rows, :] = jnp.broadcast_to(1.0 / l, (ATT_ROWS, LANES))
                    lse_scr[sub, hh, rows, :] = jnp.broadcast_to(m + jnp.log(l), (ATT_ROWS, LANES))
            outs = [_dot(p_scr[sub, hh], vw) * inv_scr[sub, hh] for hh in range(2)]
            o_ref[qrows, :] = jnp.where(sel[0], outs[0], outs[1]).astype(BF16)
            lse_ref[qrows, :] = jnp.where(sel[0], lse_scr[sub, 0], lse_scr[sub, 1])

    qspec, kspec, bspec = _att_specs(t, tp)
    return _call(body, name, (d // LANES, t // (ATT_SUBS * Q_BLOCK)), [qspec, kspec, kspec, bspec], [qspec, qspec],
                 [jax.ShapeDtypeStruct((t, d), BF16), jax.ShapeDtypeStruct((t, d), F32)],
                 [pltpu.VMEM((ATT_SUBS, 2, Q_BLOCK, K_WINDOW), F32),
                  pltpu.VMEM((ATT_SUBS, 2, Q_BLOCK, K_WINDOW), BF16),
                  pltpu.VMEM((ATT_SUBS, 2, Q_BLOCK, LANES), F32),
                  pltpu.VMEM((ATT_SUBS, 2, Q_BLOCK, LANES), F32)],
                 ("parallel", "arbitrary"), (q, kp, vp, bias), rider)


def _att_bwd(q, kp, vp, bias, do, o, lse, name, rider=None):
    t, d = q.shape
    tp = kp.shape[0]

    def body(q_ref, kp_ref, vp_ref, bias_ref, do_ref, o_ref, lse_ref, dq_ref, dkp_ref, dvp_ref, db_ref,
             s_scr, dp_scr, p_scr, ds_scr, row_scr):
        @pl.when(pl.program_id(1) == 0)
        def _():
            dkp_ref[...] = jnp.zeros_like(dkp_ref)
            dvp_ref[...] = jnp.zeros_like(dvp_ref)
            db_ref[...] = jnp.zeros_like(db_ref)

        for sub in range(ATT_SUBS):
            start, qrows, q2, kw, vw, kvalid, sel = _att_common(q_ref, kp_ref, vp_ref, sub)
            do2 = do_ref[qrows, :]
            qm = [jnp.where(sel[hh], q2, 0) for hh in range(2)]
            dom = [jnp.where(sel[hh], do2, 0) for hh in range(2)]
            do_o = do2.astype(F32) * o_ref[qrows, :].astype(F32)
            lse2 = lse_ref[qrows, :]
            for hh in range(2):
                s_scr[sub, hh] = _dot_nt(qm[hh], kw)
                dp_scr[sub, hh] = _dot_nt(dom[hh], vw)
                lse_h = jnp.max(jnp.where(sel[hh], lse2, NEG), axis=-1, keepdims=True)
                delta = jnp.sum(jnp.where(sel[hh], do_o, 0.0), axis=-1, keepdims=True)
                row_scr[sub, hh, 0] = jnp.broadcast_to(lse_h, (Q_BLOCK, LANES))
                row_scr[sub, hh, 1] = jnp.broadcast_to(delta, (Q_BLOCK, LANES))
            for hh in range(2):
                for rows in _row_groups():
                    s = jnp.where(kvalid, s_scr[sub, hh, rows, :] + bias_ref[hh, rows, :], NEG)
                    p = jnp.exp(s - _lane_copies(row_scr[sub, hh, 0, rows, :]))
                    ds = p * (dp_scr[sub, hh, rows, :] - _lane_copies(row_scr[sub, hh, 1, rows, :]))
                    db_ref[hh, rows, :] += ds
                    p_scr[sub, hh, rows, :] = p.astype(BF16)
                    ds_scr[sub, hh, rows, :] = ds.astype(BF16)
            dqs = [_dot(ds_scr[sub, hh], kw) for hh in range(2)]
            dq_ref[qrows, :] = jnp.where(sel[0], dqs[0], dqs[1])
            dkp_ref[:, pl.ds(start, K_WINDOW)] += (_dot_tn(qm[0], ds_scr[sub, 0]) +
                                                   _dot_tn(qm[1], ds_scr[sub, 1]))
            dvp_ref[:, pl.ds(start, K_WINDOW)] += (_dot_tn(dom[0], p_scr[sub, 0]) +
                                                   _dot_tn(dom[1], p_scr[sub, 1]))

    qspec, kspec, bspec = _att_specs(t, tp)
    tspec = pl.BlockSpec((LANES, tp), lambda h, i: (h, 0))
    stage = lambda dt: pltpu.VMEM((ATT_SUBS, 2, Q_BLOCK, K_WINDOW), dt)
    return _call(body, name, (d // LANES, t // (ATT_SUBS * Q_BLOCK)),
                 [qspec, kspec, kspec, bspec, qspec, qspec, qspec],
                 [qspec, tspec, tspec, bspec],
                 [jax.ShapeDtypeStruct((t, d), F32),
                  jax.ShapeDtypeStruct((d, tp), F32),
                  jax.ShapeDtypeStruct((d, tp), F32),
                  jax.ShapeDtypeStruct((ATT_HEADS, Q_BLOCK, K_WINDOW), F32)],
                 [stage(F32), stage(F32), stage(BF16), stage(BF16),
                  pltpu.VMEM((ATT_SUBS, 2, 2, Q_BLOCK, LANES), F32)],
                 ("parallel", "arbitrary"), (q, kp, vp, bias, do, o, lse), rider)


def _rel_bin_matrix():
    rows = REL_DELTAS * 2 * REL_BLK
    rho = lax.broadcasted_iota(jnp.int32, (rows, REL_PAD), 0)
    col = lax.broadcasted_iota(jnp.int32, (rows, REL_PAD), 1)
    assert 2 * REL_BLK == 256
    delta = rho >> 8
    c = 255 - (rho & 255)
    dist = K_PAD + REL_BLK * (delta - (K_WINDOW // REL_BLK - 1)) + (c - (REL_BLK - 1))
    idx = jnp.clip(dist, -REL_CLIP, REL_CLIP) + REL_CLIP
    return col == idx


def _rel_expand(rel_pad, name):
    heads = rel_pad.shape[0]
    rows = REL_DELTAS * 2 * REL_BLK

    def body_bin(r_ref, o_ref):
        onehot = jnp.where(_rel_bin_matrix(), 1.0, 0.0).astype(BF16)
        hi, mid, lo = _split3(r_ref[...])
        o_ref[...] = _dot_nt(hi, onehot) + _dot_nt(mid, onehot) + _dot_nt(lo, onehot)

    by_delta = pl.pallas_call(
        body_bin, name=name + "_bin",
        out_shape=jax.ShapeDtypeStruct((heads, rows), F32),
        compiler_params=pltpu.CompilerParams(vmem_limit_bytes=VMEM_LIMIT_V7X),
    )(rel_pad)
    by_delta = by_delta.reshape(heads * REL_DELTAS, 2 * REL_BLK)

    def body_shift(t_ref, o_ref):
        tv = t_ref[...]
        for r in range(REL_BLK):
            o_ref[r] = pltpu.roll(tv, (r + REL_BLK) % (2 * REL_BLK), 1)[:, :REL_BLK]

    return pl.pallas_call(
        body_shift, name=name + "_shift",
        out_shape=jax.ShapeDtypeStruct((REL_BLK, heads * REL_DELTAS, REL_BLK), F32),
        compiler_params=pltpu.CompilerParams(vmem_limit_bytes=VMEM_LIMIT_V7X),
    )(by_delta)


def _bias_table(rel_bias, name):
    heads = rel_bias.shape[0]
    rel_pad = jnp.pad(rel_bias, ((0, 0), (0, REL_PAD - REL_TABLE)))
    tiles = _rel_expand(rel_pad, name)
    tiles = tiles.reshape(REL_BLK, heads, REL_DELTAS, REL_BLK).transpose(1, 2, 0, 3)
    na, nb = Q_BLOCK // REL_BLK, K_WINDOW // REL_BLK
    rows = [jnp.concatenate([tiles[:, a - b + nb - 1] for b in range(nb)], axis=-1) for a in range(na)]
    table = jnp.concatenate(rows, axis=-2)
    qc = np.arange(Q_BLOCK)[:, None] // CHUNK
    kc = np.arange(K_WINDOW)[None, :] // CHUNK
    band = (kc >= qc) & (kc <= qc + PAST_CHUNKS)
    return jnp.where(jnp.asarray(band)[None], table, NEG)


def _rel_reduce(db, name):
    heads = db.shape[0]
    na, nb = Q_BLOCK // REL_BLK, K_WINDOW // REL_BLK

    fold_heads = 4

    def body_fold(db_ref, g_ref):
        for hd in range(fold_heads):
            for delta in range(REL_DELTAS):
                acc = None
                for a in range(na):
                    b = a - (delta - (nb - 1))
                    if 0 <= b < nb:
                        tile = db_ref[hd, a * REL_BLK:(a + 1) * REL_BLK, b * REL_BLK:(b + 1) * REL_BLK]
                        acc = tile if acc is None else acc + tile
                g_ref[hd, delta] = acc

    folded = pl.pallas_call(
        body_fold, name=name + "_fold", grid=(heads // fold_heads,),
        in_specs=[pl.BlockSpec((fold_heads, Q_BLOCK, K_WINDOW), lambda h: (h, 0, 0))],
        out_specs=pl.BlockSpec((fold_heads, REL_DELTAS, REL_BLK, REL_BLK), lambda h: (h, 0, 0, 0)),
        out_shape=jax.ShapeDtypeStruct((heads, REL_DELTAS, REL_BLK, REL_BLK), F32),
        compiler_params=_params("parallel"),
    )(db)
    by_row = folded.transpose(2, 0, 1, 3).reshape(REL_BLK, heads * REL_DELTAS, REL_BLK)

    def body_diag(g_ref, d_ref):
        zeros = jnp.zeros((heads * REL_DELTAS, REL_BLK), F32)
        acc = None
        for r in range(REL_BLK):
            part = pltpu.roll(jnp.concatenate([g_ref[r], zeros], axis=1), REL_BLK - r, 1)
            acc = part if acc is None else acc + part
        d_ref[...] = acc

    diag = pl.pallas_call(
        body_diag, name=name + "_diag",
        out_shape=jax.ShapeDtypeStruct((heads * REL_DELTAS, 2 * REL_BLK), F32),
        compiler_params=pltpu.CompilerParams(vmem_limit_bytes=VMEM_LIMIT_V7X),
    )(by_row)
    diag = diag.reshape(heads, REL_DELTAS * 2 * REL_BLK)

    def body_bin(d_ref, o_ref):
        onehot = jnp.where(_rel_bin_matrix(), 1.0, 0.0).astype(BF16)
        hi, mid, lo = _split3(d_ref[...])
        o_ref[...] = _dot(hi, onehot) + _dot(mid, onehot) + _dot(lo, onehot)

    out = pl.pallas_call(
        body_bin, name=name + "_bin",
        out_shape=jax.ShapeDtypeStruct((heads, REL_PAD), F32),
        compiler_params=pltpu.CompilerParams(vmem_limit_bytes=VMEM_LIMIT_V7X),
    )(diag)
    return out[:, :REL_TABLE]


def _sum_leading(x, name):
    n, r, c = x.shape
    tr = _pick(r, 256, 8)

    def body(x_ref, o_ref):
        acc = x_ref[0].astype(F32)
        for k in range(1, n):
            acc = acc + x_ref[k].astype(F32)
        o_ref[...] = acc

    return pl.pallas_call(
        body, name=name, grid=(r // tr,),
        in_specs=[pl.BlockSpec((n, tr, c), lambda i: (0, i, 0))],
        out_specs=pl.BlockSpec((tr, c), lambda i: (i, 0)),
        out_shape=jax.ShapeDtypeStruct((r, c), F32),
        compiler_params=_params("parallel"),
    )(x)


def _pair_add(g, recv, parity, name):
    _, r, c = g.shape
    tr = _pick(r, 256, 16)

    def body(par_ref, g_ref, r_ref, o_ref):
        o_ref[...] = (g_ref[...].astype(F32) + r_ref[...].astype(F32)).astype(BF16)

    return pl.pallas_call(
        body, name=name,
        grid_spec=pltpu.PrefetchScalarGridSpec(
            num_scalar_prefetch=1, grid=(4, r // tr),
            in_specs=[pl.BlockSpec((1, tr, c), lambda k, i, par: (2 * k + par[0], i, 0)),
                      pl.BlockSpec((1, tr, c), lambda k, i, par: (k, i, 0))],
            out_specs=pl.BlockSpec((1, tr, c), lambda k, i, par: (k, i, 0))),
        out_shape=jax.ShapeDtypeStruct((4, r, c), BF16),
        compiler_params=_params("parallel", "parallel"),
    )(parity, g, recv)


def _adamw(w, g_parts, m, v, name):
    r, c = w.shape
    n = g_parts.shape[0]
    tr = _pick(r, 256, 16 if g_parts.dtype == BF16 else 8)
    c1 = 1.0 - ADAM_B1 ** ADAM_STEP
    c2 = 1.0 - ADAM_B2 ** ADAM_STEP

    def body(w_ref, g_ref, m_ref, v_ref, go_ref, d_ref, nm_ref, nv_ref):
        gv = g_ref[0].astype(F32)
        for k in range(1, n):
            gv = gv + g_ref[k].astype(F32)
        nm = ADAM_B1 * m_ref[...] + (1.0 - ADAM_B1) * gv
        nv = ADAM_B2 * v_ref[...] + (1.0 - ADAM_B2) * (gv * gv)
        go_ref[...] = gv
        d_ref[...] = -ADAM_LR * ((nm / c1) / (jnp.sqrt(nv / c2) + ADAM_EPS) + ADAM_WD * w_ref[...])
        nm_ref[...] = nm
        nv_ref[...] = nv

    spec = pl.BlockSpec((tr, c), lambda i: (i, 0))
    shp = jax.ShapeDtypeStruct((r, c), F32)
    return pl.pallas_call(
        body, name=name, grid=(r // tr,),
        in_specs=[spec, pl.BlockSpec((n, tr, c), lambda i: (0, i, 0)), spec, spec],
        out_specs=[spec] * 4, out_shape=[shp] * 4,
        compiler_params=_params("parallel"),
    )(w, g_parts, m, v)


BIG = (("a_w_in", 1), ("a_w_o", 0), ("a_w_gu", 0), ("a_w_down", 0), ("w_kv", 1),
       ("b_w_q", 0), ("b_w_o", 0), ("b_w_gu", 0), ("b_w_down", 0))
TRANSPOSED = ("a_w_gu", "b_w_gu")
FFN_BLK = 2 * FFN_HIDDEN // N_DEV

SMALL = (("a_norm_g", D_MODEL, True), ("a_gn_g", RET_V_COLS, True), ("a_ffn_norm_g", D_MODEL, True),
         ("kv_norm_g", D_MODEL, False), ("b_norm_g", D_MODEL, False), ("b_ffn_norm_g", D_MODEL, False),
         ("k_norm_g", ATT_DH, False), ("b_q_norm_g", ATT_DH, False),
         ("b_rel_bias", ATT_HEADS * REL_TABLE, False))
SMALL_ROWS, SMALL_COLS = 16, 1024


def _pack_small(vals, last=None):
    flat = jnp.concatenate([vals[n].reshape(-1) for n, _, _ in SMALL])
    room = SMALL_ROWS * SMALL_COLS - flat.shape[0]
    if last is None:
        flat = jnp.pad(flat, (0, room))
    else:
        flat = jnp.concatenate([jnp.pad(flat, (0, room - 1)), last.reshape(1)])
    return flat.reshape(SMALL_ROWS, SMALL_COLS)


def _unpack_small(packed, local):
    flat, out, pos = packed.reshape(-1), {}, 0
    for n, length, sharded in SMALL:
        ln = length // N_DEV if (local and sharded) else length
        out[n] = flat[pos:pos + ln]
        pos += ln
    return out


def _gather_rider(shards, names):
    return _GatherRider([shards[n] for n in names])


def _gathered(rider, names, axis_of):
    return {n: (r.reshape(-1, r.shape[2]) if axis_of[n] == 0 else r) for n, r in zip(names, rider.results)}


def _blocks(g):
    return g if g.ndim == 3 else g.reshape(N_DEV, -1, g.shape[-1])


def _local_step(x, target, shards, s, parity):
    t = x.shape[0]
    axis_of = dict(BIG)
    consts = _ret_consts(t)
    lane_to_head = np.zeros((D_MODEL, LANES), np.float32)
    lane_to_head[np.arange(D_MODEL), np.arange(D_MODEL) // ATT_DH] = 1.0
    bd = jnp.asarray(lane_to_head).astype(BF16)
    kg_t = jnp.tile(s["k_norm_g"], (1, ATT_HEADS))
    qg_t = jnp.tile(s["b_q_norm_g"], (1, ATT_HEADS))
    q_scale = ATT_DH ** -0.5
    w, g, recv = {}, {}, {}

    def gather_on(names):
        return _gather_rider(shards, names), names

    def landed(ride):
        w.update(_gathered(ride[0], ride[1], axis_of))

    def scatter_on(names):
        return _ScatterRider([_blocks(g[n]) for n in names]), names

    def reduced(ride):
        recv.update(zip(ride[1], ride[0].results))

    proj, (w["a_w_in"], w_o) = _proj_gather(x, s["a_norm_g"], shards["a_w_in"], [shards["a_w_o"]], "a_proj")
    w["a_w_o"] = w_o.reshape(-1, w_o.shape[2])
    ride = gather_on(["a_w_gu"])
    y, o_ret, states = _ret_fwd(proj, s["a_gn_g"], consts, "a_ret", rider=ride[0])
    landed(ride)
    ride = gather_on(["w_kv"])
    x1 = _mm(y, w["a_w_o"], "nn", "a_out", res=x, rider=ride[0])
    landed(ride)
    ride = gather_on(["a_w_down", "b_w_q", "b_w_o"])
    gu_a, act_a = _mm(x1, w["a_w_gu"], "nt", "a_ffn_gu", epilogue="swiglu", out_block=FFN_BLK,
                      norm_g=s["a_ffn_norm_g"], rider=ride[0])
    landed(ride)
    x2 = _mm(act_a, w["a_w_down"], "nn", "a_ffn_down", res=x1)

    kv = _mm(x2, w["w_kv"], "nn", "kv_proj", norm_g=s["kv_norm_g"])
    kp, vp = _kv_prep(kv, kg_t, bd, "kv_prep")

    q_raw = _mm(x2, w["b_w_q"], "nn", "b_q", norm_g=s["b_norm_g"])
    qn = _q_hnorm(q_raw, qg_t, bd, q_scale, "q_hnorm")
    bias = _bias_table(s["b_rel_bias"].reshape(ATT_HEADS, REL_TABLE), "rel")
    ride = gather_on(["b_w_gu"])
    o_att, lse = _att_fwd(qn, kp, vp, bias, "b_att", rider=ride[0])
    landed(ride)
    x3 = _mm(o_att, w["b_w_o"], "nn", "b_out", res=x2)
    ride = gather_on(["b_w_down"])
    gu_b, act_b = _mm(x3, w["b_w_gu"], "nt", "b_ffn_gu", epilogue="swiglu", out_block=FFN_BLK,
                      norm_g=s["b_ffn_norm_g"], rider=ride[0])
    landed(ride)
    dy, loss = _mm(act_b, w["b_w_down"], "nn", "b_ffn_down", res=x3, epilogue="loss", extra=(target,))
    in_blk, kv_blk, ffn_blk = w["a_w_in"].shape[2], w["w_kv"].shape[2], FFN_BLK

    dgu = _mm(dy, w["b_w_down"], "nt", "b_ffn_dgu", out_block=ffn_blk, epilogue="swiglu_bwd", extra=gu_b)
    dgu = dgu.reshape(N_DEV, t, ffn_blk)
    g["b_w_down"] = _mm(act_b, dy, "tn", "b_ffn_gdown", out_dtype=BF16)
    ride = scatter_on(["b_w_down"])
    dx3, g["b_ffn_norm_g"] = _mm(dgu, w["b_w_gu"], "nn", "b_ffn_dh", epilogue="rms_bwd",
                                 extra=(x3, s["b_ffn_norm_g"], dy), rider=ride[0])
    reduced(ride)
    g["b_w_gu"] = _mm(dgu, x3, "tn", "b_ffn_ggu", out_dtype=BF16, norm_g=s["b_ffn_norm_g"], norm_b=True)

    do_att = _mm(dx3, w["b_w_o"], "nt", "b_dout", out_dtype=BF16)
    g["b_w_o"] = _mm(o_att, dx3, "tn", "b_gout", out_dtype=BF16)
    ride = scatter_on(["b_w_gu", "b_w_o"])
    dq, dkp, dvp, db = _att_bwd(qn, kp, vp, bias, do_att, o_att, lse, "b_datt", rider=ride[0])
    reduced(ride)
    g["b_rel_bias"] = _rel_reduce(db, "drel").reshape(1, -1)
    dq_raw, gq = _q_dhnorm(q_raw, qg_t, bd, dq, q_scale, "q_dhnorm")
    g["b_q_norm_g"] = gq.reshape(ATT_HEADS, ATT_DH).sum(axis=0, keepdims=True)
    g["b_w_q"] = _mm(x2, dq_raw, "tn", "b_gq", out_dtype=BF16, norm_g=s["b_norm_g"])
    dx2, g["b_norm_g"] = _mm(dq_raw, w["b_w_q"], "nt", "b_dq", epilogue="rms_bwd",
                             extra=(x2, s["b_norm_g"], dx3))

    dkv, gk = _kv_dprep(kv, kg_t, bd, dkp, dvp, "kv_dprep")
    g["k_norm_g"] = gk.reshape(ATT_HEADS, ATT_DH).sum(axis=0, keepdims=True)
    g["w_kv"] = _mm(x2, dkv, "tn", "kv_g", out_dtype=BF16, out_block=kv_blk, norm_g=s["kv_norm_g"])
    dx2, g["kv_norm_g"] = _mm(dkv, w["w_kv"], "nt", "kv_du", epilogue="rms_bwd",
                              extra=(x2, s["kv_norm_g"], dx2))

    ride = scatter_on(["b_w_q"])
    dgu = _mm(dx2, w["a_w_down"], "nt", "a_ffn_dgu", out_block=ffn_blk, epilogue="swiglu_bwd", extra=gu_a,
              rider=ride[0])
    reduced(ride)
    dgu = dgu.reshape(N_DEV, t, ffn_blk)
    g["a_w_down"] = _mm(act_a, dx2, "tn", "a_ffn_gdown", out_dtype=BF16)
    ride = scatter_on(["a_w_down"])
    dx1, g["a_ffn_norm_g"] = _mm(dgu, w["a_w_gu"], "nn", "a_ffn_dh", epilogue="rms_bwd",
                                 extra=(x1, s["a_ffn_norm_g"], dx2), rider=ride[0])
    reduced(ride)
    ride = scatter_on(["w_kv"])
    g["a_w_gu"] = _mm(dgu, x1, "tn", "a_ffn_ggu", out_dtype=BF16, norm_g=s["a_ffn_norm_g"], norm_b=True,
                      rider=ride[0])
    reduced(ride)

    swap = _SiblingSwapRider([_blocks(g["a_w_gu"])])
    dy_ret = _mm(dx1, w["a_w_o"], "nt", "a_dout", rider=swap)
    g["a_w_o"] = _mm(y, dx1, "tn", "a_gout", out_dtype=BF16)
    chips = _ChipScatterRider([_pair_add(_blocks(g["a_w_gu"]), swap.results[0], parity, "rs_pair_add_gu")])
    dproj, g["a_gn_g"] = _ret_bwd(proj, s["a_gn_g"], o_ret, states, dy_ret, consts, "a_dret", rider=chips)
    recv["a_w_gu"] = chips.results[0]
    ride = scatter_on(["a_w_o"])
    g["a_w_in"] = _mm(x, dproj, "tn", "a_gin", out_dtype=BF16, out_block=in_blk, norm_g=s["a_norm_g"],
                      rider=ride[0])
    reduced(ride)
    from_sibling = _exchange(_SiblingSwapRider([g["a_w_in"]]), "rs_sibling")[0]
    chip_sums = _pair_add(g["a_w_in"], from_sibling, parity, "rs_pair_add")
    last = _ChipScatterRider([chip_sums])
    grad_x, g["a_norm_g"] = _mm(dproj, w["a_w_in"], "nt", "a_dproj", epilogue="rms_bwd",
                                extra=(x, s["a_norm_g"], dx1), rider=last)
    recv["a_w_in"] = last.results[0]
    return loss, grad_x, recv, g


ARG_NAMES = ("x", "a_norm_g", "a_w_in", "a_gn_g", "a_w_o", "a_ffn_norm_g", "a_w_gu", "a_w_down",
             "kv_norm_g", "w_kv", "k_norm_g", "b_norm_g", "b_w_q", "b_q_norm_g", "b_rel_bias", "b_w_o",
             "b_ffn_norm_g", "b_w_gu", "b_w_down")
WEIGHT_NAMES = ARG_NAMES[1:]


def _big_shard(a, name):
    a = a[0] if a.ndim == 3 else a
    return a.T if name in TRANSPOSED else a


def _as_given(a, name, shape):
    return (a.T if name in TRANSPOSED else a).reshape(shape)


def kernel(x, a_norm_g, a_w_in, a_gn_g, a_w_o, a_ffn_norm_g, a_w_gu, a_w_down, kv_norm_g, w_kv, k_norm_g, b_norm_g, b_w_q, b_q_norm_g, b_rel_bias, b_w_o, b_ffn_norm_g, b_w_gu, b_w_down, loss_target, m_a_norm_g, m_a_w_in, m_a_gn_g, m_a_w_o, m_a_ffn_norm_g, m_a_w_gu, m_a_w_down, m_kv_norm_g, m_w_kv, m_k_norm_g, m_b_norm_g, m_b_w_q, m_b_q_norm_g, m_b_rel_bias, m_b_w_o, m_b_ffn_norm_g, m_b_w_gu, m_b_w_down, v_a_norm_g, v_a_w_in, v_a_gn_g, v_a_w_o, v_a_ffn_norm_g, v_a_w_gu, v_a_w_down, v_kv_norm_g, v_w_kv, v_k_norm_g, v_b_norm_g, v_b_w_q, v_b_q_norm_g, v_b_rel_bias, v_b_w_o, v_b_ffn_norm_g, v_b_w_gu, v_b_w_down):
    args = (x, a_norm_g, a_w_in, a_gn_g, a_w_o, a_ffn_norm_g, a_w_gu, a_w_down, kv_norm_g, w_kv, k_norm_g,
            b_norm_g, b_w_q, b_q_norm_g, b_rel_bias, b_w_o, b_ffn_norm_g, b_w_gu, b_w_down)
    p = dict(zip(ARG_NAMES, args))
    m_all = dict(zip(WEIGHT_NAMES, (m_a_norm_g, m_a_w_in, m_a_gn_g, m_a_w_o, m_a_ffn_norm_g, m_a_w_gu,
                                    m_a_w_down, m_kv_norm_g, m_w_kv, m_k_norm_g, m_b_norm_g, m_b_w_q,
                                    m_b_q_norm_g, m_b_rel_bias, m_b_w_o, m_b_ffn_norm_g, m_b_w_gu, m_b_w_down)))
    v_all = dict(zip(WEIGHT_NAMES, (v_a_norm_g, v_a_w_in, v_a_gn_g, v_a_w_o, v_a_ffn_norm_g, v_a_w_gu,
                                    v_a_w_down, v_kv_norm_g, v_w_kv, v_k_norm_g, v_b_norm_g, v_b_w_q,
                                    v_b_q_norm_g, v_b_rel_bias, v_b_w_o, v_b_ffn_norm_g, v_b_w_gu, v_b_w_down)))
    xi, yi, ci = _my_place()
    me = 4 * xi + 2 * yi + ci
    big_names = [n for n, _ in BIG]

    big_local = {n: _big_shard(p[n], n) for n in big_names}
    shards = {n: a.astype(BF16) for n, a in big_local.items()}
    small_local = _pack_small({n: p[n] for n, _, _ in SMALL})
    small_all = _exchange(_GatherRider([small_local]), "gather_small")[0]
    flat_g = small_all.reshape(N_DEV, -1)
    s_full, pos = {}, 0
    for n, length, sharded in SMALL:
        ln = length // N_DEV if sharded else length
        s_full[n] = flat_g[:, pos:pos + ln].reshape(1, -1) if sharded else p[n].reshape(1, -1)
        pos += ln

    parity = jnp.reshape(ci, (1,)).astype(jnp.int32)
    loss, grad_x, recv, g = _local_step(x[0], loss_target[0], shards, s_full, parity)

    partial = _pack_small({n: g[n] for n, _, _ in SMALL}, last=loss)
    summed = _sum_leading(_exchange(_GatherRider([partial]), "gather_gsmall")[0], "gsmall_sum")
    loss = summed[SMALL_ROWS - 1, SMALL_COLS - 1]
    g_small = _unpack_small(summed, local=False)
    for n, length, sharded in SMALL:
        if sharded:
            g_small[n] = lax.dynamic_slice(g_small[n], (me * (length // N_DEV),), (length // N_DEV,))

    grads, deltas, new_m, new_v = {}, {}, {}, {}
    for n in big_names:
        outs = _adamw(big_local[n], recv[n], _big_shard(m_all[n], n), _big_shard(v_all[n], n), "adamw_" + n)
        grads[n], deltas[n], new_m[n], new_v[n] = (_as_given(a, n, p[n].shape) for a in outs)
    pk = lambda src: _pack_small({n: src[n] for n, _, _ in SMALL})
    outs = _adamw(small_local, pk(g_small)[None], pk(m_all), pk(v_all), "adamw_small")
    g_s, d_s, nm_s, nv_s = (_unpack_small(a, local=True) for a in outs)
    for n, _, _ in SMALL:
        grads[n], deltas[n], new_m[n], new_v[n] = (a[n].reshape(p[n].shape) for a in (g_s, d_s, nm_s, nv_s))

    return (loss, grad_x[None], *[grads[n] for n in WEIGHT_NAMES], *[deltas[n] for n in WEIGHT_NAMES],
            *[new_m[n] for n in WEIGHT_NAMES], *[new_v[n] for n in WEIGHT_NAMES])
```

```python
import numpy as np
import jax
import jax.numpy as jnp
from jax import lax
from jax.experimental import pallas as pl
from jax.experimental.pallas import tpu as pltpu

F32 = jnp.float32
BF16 = jnp.bfloat16

N_DEV = 8
D_MODEL = 1024
CHUNK = 64
EPS = 1e-6
RET_HEADS, RET_DK, RET_DV = 4, 256, 512
RET_STEP = 4
RET_Q_COLS = RET_HEADS * RET_DK
RET_V_COLS = RET_HEADS * RET_DV
ATT_HEADS, ATT_DH = 16, 64
PAST_CHUNKS = 8
REL_CLIP = 256
REL_TABLE = 2 * REL_CLIP + 1
FFN_HIDDEN = 2816
ROPE_BASE = 10000.0
LANES = 128
Q_BLOCK = 256
ATT_SUBS = 4
ATT_ROWS = 32
K_PAD = PAST_CHUNKS * CHUNK
K_WINDOW = Q_BLOCK + K_PAD
REL_BLK = 128
REL_DELTAS = Q_BLOCK // REL_BLK + K_WINDOW // REL_BLK - 1
REL_PAD = 640
NEG = -1e30
VMEM_LIMIT_V7X = 56 * 1024 * 1024
ADAM_LR, ADAM_B1, ADAM_B2, ADAM_EPS, ADAM_WD, ADAM_STEP = 1e-3, 0.9, 0.999, 1e-8, 0.01, 10
MESH = pl.DeviceIdType.MESH
ANY = pl.BlockSpec(memory_space=pl.ANY)


def _params(*semantics):
    return pltpu.CompilerParams(dimension_semantics=semantics, vmem_limit_bytes=VMEM_LIMIT_V7X)


def _pick(dim, cap, align):
    best = None
    for t in range(align, min(dim, cap) + 1, align):
        if dim % t == 0:
            best = t
    assert best is not None, (dim, cap, align)
    return best


def _dot(a, b):
    return lax.dot_general(a, b, (((1,), (0,)), ((), ())), preferred_element_type=F32)


def _dot_nt(a, b):
    return lax.dot_general(a, b, (((1,), (1,)), ((), ())), preferred_element_type=F32)


def _dot_tn(a, b):
    return lax.dot_general(a, b, (((0,), (0,)), ((), ())), preferred_element_type=F32)


def _split2(x):
    hi = x.astype(BF16)
    lo = (x - hi.astype(F32)).astype(BF16)
    return hi, lo


def _split3(x):
    hi = x.astype(BF16)
    r = x - hi.astype(F32)
    mid = r.astype(BF16)
    lo = (r - mid.astype(F32)).astype(BF16)
    return hi, mid, lo


def _sigmoid(x):
    return 1.0 / (1.0 + jnp.exp(-x))


def _accumulate(ref, part, step):
    @pl.when(step == 0)
    def _():
        ref[...] = part

    @pl.when(step > 0)
    def _():
        ref[...] += part


RELAY_AT_NUM, RELAY_AT_DEN = 3, 4


def _my_place():
    return lax.axis_index("x"), lax.axis_index("y"), lax.axis_index("c")


def _flip(v, bit):
    return 1 - v if bit else v


class _NoRelay:
    def relay(self, in_refs, out_refs, sems):
        pass


class _GatherRider:
    def __init__(self, xs):
        self.inputs = list(xs)
        n = len(xs)
        self.out_shape = [jax.ShapeDtypeStruct((N_DEV,) + x.shape, x.dtype) for x in xs]
        self.scratch = [pltpu.SemaphoreType.DMA((7, n)), pltpu.SemaphoreType.DMA((7, n)),
                        pltpu.SemaphoreType.DMA((n,))]
        self.results = None

    def _copies(self, x_refs, out_refs, sems):
        send_sems, recv_sems, local_sems = sems
        n = len(x_refs)
        x, y, c = _my_place()
        me, sibling = (x, y, c), (x, y, 1 - c)
        chips = [(1 - x, y), (x, 1 - y), (1 - x, 1 - y)]

        def slot(a, px, py, pc):
            return out_refs[a].at[4 * px + 2 * py + pc]

        def copy(k, a, block, to, own=False):
            return pltpu.make_async_remote_copy(
                src_ref=x_refs[a] if own else slot(a, *block), dst_ref=slot(a, *block),
                send_sem=send_sems.at[k, a], recv_sem=recv_sems.at[k, a],
                device_id=to, device_id_type=MESH)

        mine = [pltpu.make_async_copy(x_refs[a], slot(a, *me), local_sems.at[a]) for a in range(n)]
        first = []
        for a in range(n):
            first.append(copy(0, a, me, sibling, own=True))
            first += [copy(1 + j, a, me, (*chip, c), own=True) for j, chip in enumerate(chips)]
        return n, c, me, sibling, chips, copy, mine, first

    def start(self, x_refs, out_refs, sems):
        _, _, _, _, _, _, mine, first = self._copies(x_refs, out_refs, sems)
        for cp in mine + first:
            cp.start()

    def relay(self, x_refs, out_refs, sems):
        n, c, me, sibling, chips, copy, _, _ = self._copies(x_refs, out_refs, sems)
        for j, chip in enumerate(chips):
            for a in range(n):
                copy(1 + j, a, (*chip, c), me).wait_recv()
                copy(4 + j, a, (*chip, c), sibling).start()

    def finish(self, x_refs, out_refs, sems):
        n, c, me, sibling, chips, copy, mine, first = self._copies(x_refs, out_refs, sems)
        passed = [copy(4 + j, a, (*chip, c), sibling) for j, chip in enumerate(chips) for a in range(n)]
        for a in range(n):
            copy(0, a, sibling, me).wait_recv()
            for j, chip in enumerate(chips):
                copy(4 + j, a, (*chip, 1 - c), me).wait_recv()
        for cp in first + passed:
            cp.wait_send()
        for cp in mine:
            cp.wait()


class _ScatterRider(_NoRelay):
    def __init__(self, gs):
        self.inputs = list(gs)
        n = len(gs)
        self.out_shape = [jax.ShapeDtypeStruct(g.shape, g.dtype) for g in gs]
        self.scratch = [pltpu.SemaphoreType.DMA((7, n)), pltpu.SemaphoreType.DMA((7, n)),
                        pltpu.SemaphoreType.DMA((n,))]
        self.results = None

    def _copies(self, g_refs, out_refs, sems):
        send_sems, recv_sems, local_sems = sems
        x, y, c = _my_place()
        me = 4 * x + 2 * y + c
        mine, copies = [], []
        for a in range(len(g_refs)):
            mine.append(pltpu.make_async_copy(g_refs[a].at[me], out_refs[a].at[me], local_sems.at[a]))
            for k in range(1, N_DEV):
                px, py, pc = _flip(x, k & 4), _flip(y, k & 2), _flip(c, k & 1)
                copies.append(pltpu.make_async_remote_copy(
                    src_ref=g_refs[a].at[4 * px + 2 * py + pc], dst_ref=out_refs[a].at[me],
                    send_sem=send_sems.at[k - 1, a], recv_sem=recv_sems.at[k - 1, a],
                    device_id=(px, py, pc), device_id_type=MESH))
        return mine, copies

    def start(self, g_refs, out_refs, sems):
        mine, copies = self._copies(g_refs, out_refs, sems)
        for cp in mine + copies:
            cp.start()

    def finish(self, g_refs, out_refs, sems):
        mine, copies = self._copies(g_refs, out_refs, sems)
        for cp in copies + mine:
            cp.wait()


class _SiblingSwapRider(_NoRelay):
    def __init__(self, gs):
        self.inputs = list(gs)
        n = len(gs)
        self.out_shape = [jax.ShapeDtypeStruct((4,) + g.shape[1:], g.dtype) for g in gs]
        self.scratch = [pltpu.SemaphoreType.DMA((4, n)), pltpu.SemaphoreType.DMA((4, n))]
        self.results = None

    def _copies(self, g_refs, out_refs, sems):
        send_sems, recv_sems = sems
        x, y, c = _my_place()
        return [pltpu.make_async_remote_copy(
            src_ref=g_refs[a].at[2 * k + 1 - c], dst_ref=out_refs[a].at[k],
            send_sem=send_sems.at[k, a], recv_sem=recv_sems.at[k, a],
            device_id=(x, y, 1 - c), device_id_type=MESH)
            for a in range(len(g_refs)) for k in range(4)]

    def start(self, g_refs, out_refs, sems):
        for cp in self._copies(g_refs, out_refs, sems):
            cp.start()

    def finish(self, g_refs, out_refs, sems):
        for cp in self._copies(g_refs, out_refs, sems):
            cp.wait()


class _ChipScatterRider(_NoRelay):
    def __init__(self, ps):
        self.inputs = list(ps)
        n = len(ps)
        self.out_shape = [jax.ShapeDtypeStruct(p.shape, p.dtype) for p in ps]
        self.scratch = [pltpu.SemaphoreType.DMA((3, n)), pltpu.SemaphoreType.DMA((3, n)),
                        pltpu.SemaphoreType.DMA((n,))]
        self.results = None

    def _copies(self, p_refs, out_refs, sems):
        send_sems, recv_sems, local_sems = sems
        x, y, c = _my_place()
        my_chip = 2 * x + y
        chips = [(1 - x, y), (x, 1 - y), (1 - x, 1 - y)]
        n = len(p_refs)
        mine = [pltpu.make_async_copy(p_refs[a].at[my_chip], out_refs[a].at[my_chip], local_sems.at[a])
                for a in range(n)]
        copies = [pltpu.make_async_remote_copy(
            src_ref=p_refs[a].at[2 * cx + cy], dst_ref=out_refs[a].at[my_chip],
            send_sem=send_sems.at[j, a], recv_sem=recv_sems.at[j, a],
            device_id=(cx, cy, c), device_id_type=MESH)
            for a in range(n) for j, (cx, cy) in enumerate(chips)]
        return mine, copies

    def start(self, p_refs, out_refs, sems):
        mine, copies = self._copies(p_refs, out_refs, sems)
        for cp in mine + copies:
            cp.start()

    def finish(self, p_refs, out_refs, sems):
        mine, copies = self._copies(p_refs, out_refs, sems)
        for cp in copies + mine:
            cp.wait()


def _call(body, name, grid, in_specs, out_specs, out_shape, scratch, semantics, args, rider=None):
    in_specs, out_specs, out_shape, scratch = list(in_specs), list(out_specs), list(out_shape), list(scratch)
    if rider is None:
        return list(pl.pallas_call(
            body, name=name, grid=grid, in_specs=in_specs, out_specs=out_specs, out_shape=out_shape,
            scratch_shapes=scratch, compiler_params=_params(*semantics))(*args))
    n_in, n_out, n_scr = len(in_specs), len(out_specs), len(scratch)
    r_in, r_out = len(rider.inputs), len(rider.out_shape)

    def wrapped(*refs):
        cuts = np.cumsum([0, n_in, r_in, n_out, r_out, n_scr])
        hi, ri, ho, ro, hs = (refs[cuts[i]:cuts[i + 1]] for i in range(5))
        rs = refs[cuts[5]:]
        step, steps = pl.program_id(0), grid[0]
        for d in range(1, len(grid)):
            step, steps = step * grid[d] + pl.program_id(d), steps * grid[d]

        @pl.when(step == 0)
        def _():
            rider.start(ri, ro, rs)

        body(*hi, *ho, *hs)

        @pl.when(step == (steps * RELAY_AT_NUM) // RELAY_AT_DEN)
        def _():
            rider.relay(ri, ro, rs)

        @pl.when(step == steps - 1)
        def _():
            rider.finish(ri, ro, rs)

    outs = pl.pallas_call(
        wrapped, name=name, grid=grid,
        in_specs=in_specs + [ANY] * r_in, out_specs=out_specs + [ANY] * r_out,
        out_shape=out_shape + rider.out_shape, scratch_shapes=scratch + rider.scratch,
        compiler_params=_params(*(["arbitrary"] * len(grid))),
    )(*args, *rider.inputs)
    rider.results = list(outs[n_out:])
    return list(outs[:n_out])


_WALK = ((None, None), (0, None), (1, 4), (2, 5), (4, None), (5, None), (3, 6), (6, None))


def _gather_order():
    x, y, c = _my_place()
    (ax, ay), (bx, by), (dx, dy) = (1 - x, y), (x, 1 - y), (1 - x, 1 - y)
    ids = [(x, y, c), (x, y, 1 - c), (ax, ay, c), (bx, by, c), (ax, ay, 1 - c), (bx, by, 1 - c),
           (dx, dy, c), (dx, dy, 1 - c)]
    return jnp.stack([4 * px + 2 * py + pc for px, py, pc in ids]).astype(jnp.int32)


def _proj_gather(x, norm_g, w_shard, extras, name):
    t, d = x.shape
    cols = w_shard.shape[1]
    tm = _pick(t, MM_CAP_MN, 16)
    ni = t // tm
    n = 1 + len(extras)
    rider = _GatherRider([w_shard] + list(extras))

    def body(ord_ref, x_ref, g_ref, *refs):
        sh_refs, proj_ref, gathered = refs[:n], refs[n], refs[n + 1:2 * n + 1]
        h_all, bbuf, bsem, send_sems, recv_sems, local_sems = refs[2 * n + 1:]
        j, i = pl.program_id(0), pl.program_id(1)
        _, c, me, sibling, chips, copy, mine, first = rider._copies(
            sh_refs, gathered, (send_sems, recv_sems, local_sems))
        rows = pl.ds(pl.multiple_of(i * tm, tm), tm)

        def load(step, src):
            return pltpu.make_async_copy(src, bbuf.at[step % 2], bsem.at[step % 2])

        def relayed(k, a):
            return copy(k, a, (*chips[k - 4], c), sibling)

        @pl.when(jnp.logical_and(j == 0, i == 0))
        def _():
            for cp in mine + first:
                cp.start()
            load(0, sh_refs[0]).start()

        @pl.when(i == 0)
        def _():
            load(j, sh_refs[0]).wait()

        @pl.when(j == 0)
        def _():
            groups = []
            for r in range(0, tm, NORM_ROWS):
                xv = x_ref[r:r + NORM_ROWS, :]
                rstd = lax.rsqrt(jnp.mean(xv * xv, axis=-1, keepdims=True) + EPS)
                groups.append((xv * rstd * g_ref[...]).astype(BF16))
            h_all[rows, :] = jnp.concatenate(groups, axis=0)

        proj_ref[...] = _dot(h_all[rows, :], bbuf[j % 2])

        for step in range(N_DEV - 1):
            @pl.when(jnp.logical_and(j == step, i == max(ni - 2, 0)))
            def _(step=step):
                need, relay = _WALK[step + 1]
                copy(need, 0, me, me).wait_recv()
                if relay is not None:
                    relayed(relay, 0).start()
                load(step + 1, gathered[0].at[ord_ref[step + 1]]).start()

        @pl.when(jnp.logical_and(j == N_DEV - 1, i == ni - 1))
        def _():
            for a in range(1, n):
                for k in range(3):
                    copy(1 + k, a, me, me).wait_recv()
                    relayed(4 + k, a).start()
            for a in range(1, n):
                for k in (0, 4, 5, 6):
                    copy(k, a, me, me).wait_recv()
            for cp in first + [relayed(4 + k, a) for a in range(n) for k in range(3)]:
                cp.wait_send()
            for cp in mine:
                cp.wait()

    outs = pl.pallas_call(
        body, name=name,
        grid_spec=pltpu.PrefetchScalarGridSpec(
            num_scalar_prefetch=1, grid=(N_DEV, ni),
            in_specs=[pl.BlockSpec((tm, d), lambda j, i, o: (jnp.where(j == 0, i, ni - 1), 0)),
                      pl.BlockSpec((1, d), lambda j, i, o: (0, 0))] + [ANY] * n,
            out_specs=[pl.BlockSpec((tm, cols), lambda j, i, o: (i, o[j]))] + [ANY] * n,
            scratch_shapes=[pltpu.VMEM((t, d), BF16), pltpu.VMEM((2, d, cols), BF16),
                            pltpu.SemaphoreType.DMA((2,))] + rider.scratch),
        out_shape=[jax.ShapeDtypeStruct((t, N_DEV * cols), F32)] + rider.out_shape,
        compiler_params=_params("arbitrary", "arbitrary"),
    )(_gather_order(), x, norm_g, w_shard, *extras)
    return outs[0], list(outs[1:])


def _exchange(rider, name):
    r_in, r_out = len(rider.inputs), len(rider.out_shape)

    def body(*refs):
        ri, ro, rs = refs[:r_in], refs[r_in:r_in + r_out], refs[r_in + r_out:]
        rider.start(ri, ro, rs)
        rider.relay(ri, ro, rs)
        rider.finish(ri, ro, rs)

    return list(pl.pallas_call(
        body, name=name, in_specs=[ANY] * r_in, out_specs=[ANY] * r_out,
        out_shape=rider.out_shape, scratch_shapes=rider.scratch)(*rider.inputs))


MM_CAP_MN = 1024
MM_CAP_M_GRAD = 1408
MM_CAP_N = 1536
MM_CAP_K = 3072
MM_CAP_K_TOKENS = 2048
MM_CAP_K_RMS = 8192
MM_CAP_M_RMS = 512
NORM_ROWS = 256


def _mm(a, b, mode, name, out_dtype=F32, res=None, out_block=None, epilogue=None, extra=None, norm_g=None,
        norm_b=False, rider=None):
    a3, b3 = a.ndim == 3, b.ndim == 3
    um = un = uk = None
    if mode in ("nn", "nt"):
        if a3:
            m, uk = a.shape[1:]
            k = a.shape[0] * uk
        else:
            m, k = a.shape
    else:
        if a3:
            k, um = a.shape[1:]
            m = a.shape[0] * um
        else:
            k, m = a.shape
    if mode in ("nn", "tn"):
        if b3:
            kb, un = b.shape[1:]
            n = b.shape[0] * un
        else:
            kb, n = b.shape
        assert kb == k, (a.shape, b.shape, mode)
    else:
        if b3:
            n, ukb = b.shape[1:]
            assert b.shape[0] * ukb == k and uk in (None, ukb), (a.shape, b.shape, mode)
            uk = ukb
        else:
            n, kb = b.shape
            assert kb == k, (a.shape, b.shape, mode)
    if out_block is not None:
        assert un in (None, out_block)
        un = out_block

    def tile(dim, unit, cap, align):
        if unit is None:
            return _pick(dim, cap, align), 1
        c = max(1, cap // unit)
        while (dim // unit) % c:
            c -= 1
        return unit, c

    cap_m = MM_CAP_M_GRAD if mode == "tn" else (MM_CAP_M_RMS if epilogue == "rms_bwd" else MM_CAP_MN)
    um, cm = tile(m, um, cap_m, 128 if mode == "tn" else 16)
    un, cn = tile(n, un, MM_CAP_N, 128)
    cap_k = MM_CAP_K_TOKENS if mode == "tn" else (MM_CAP_K_RMS if epilogue == "rms_bwd" else MM_CAP_K)
    uk, ck = tile(k, uk, cap_k, 128)
    if epilogue == "rms_bwd":
        assert mode != "tn" and n == D_MODEL and cm == cn == 1 and res is None and out_block is None
    if epilogue == "loss":
        assert n == D_MODEL and cm == cn == 1 and res is not None and out_block is None
    if norm_g is not None and norm_b:
        assert mode == "tn" and not b3 and n == D_MODEL and cn == 1
    elif norm_g is not None:
        assert not a3 and (m if mode == "tn" else k) == D_MODEL and (cm if mode == "tn" else ck) == 1
    if epilogue == "swiglu":
        assert res is None and ((mode == "nn" and b3 and out_block is None) or
                                (mode == "nt" and not b3 and out_block is not None))
        cn = 2
    if epilogue == "swiglu_bwd":
        assert mode == "nt" and out_block is not None and extra is not None and res is None
        cn = 1
    tm, tn, tk = cm * um, cn * un, ck * uk
    nk = k // tk
    dot = {"nn": _dot, "nt": _dot_nt, "tn": _dot_tn}[mode]
    half = n // un // 2
    blocked_out = out_block is not None or epilogue in ("swiglu", "swiglu_bwd")
    extras = [] if extra is None else (list(extra) if isinstance(extra, (tuple, list)) else [extra])

    def sl(idx, unit, count):
        return slice(None) if count == 1 else slice(idx * unit, (idx + 1) * unit)

    def body(*refs):
        a_ref, b_ref = refs[0], refs[1]
        pos = 2
        r_ref = ng_ref = None
        if res is not None:
            r_ref, pos = refs[pos], pos + 1
        e_refs, pos = refs[pos:pos + len(extras)], pos + len(extras)
        if norm_g is not None:
            ng_ref, pos = refs[pos], pos + 1
        outs, acc_ref = refs[pos:-1], refs[-1]
        kk = pl.program_id(2)

        def normed(x_ref):
            groups = []
            for r in range(0, x_ref.shape[0], NORM_ROWS):
                xv = x_ref[r:r + NORM_ROWS, :]
                rstd = lax.rsqrt(jnp.mean(xv * xv, axis=-1, keepdims=True) + EPS)
                groups.append((xv * rstd * ng_ref[...]).astype(BF16))
            return jnp.concatenate(groups, axis=0)

        def a_blk(mi, ki):
            if norm_g is not None and not norm_b:
                return normed(a_ref)
            if mode in ("nn", "nt"):
                return a_ref[ki] if a3 else a_ref[:, sl(ki, uk, ck)]
            return a_ref[mi] if a3 else a_ref[:, sl(mi, um, cm)]

        def b_blk(ki, ni):
            if norm_b:
                return normed(b_ref)
            if epilogue == "swiglu":
                return b_ref[ni, 0]
            if mode in ("nn", "tn"):
                return b_ref[ni] if b3 else b_ref[sl(ki, uk, ck), sl(ni, un, cn)]
            return b_ref[ki][sl(ni, un, cn), :] if b3 else b_ref[sl(ni, un, cn), sl(ki, uk, ck)]

        parts = {}
        for mi in range(cm):
            for ni in range(cn):
                part = None
                for ki in range(ck):
                    d = dot(a_blk(mi, ki).astype(BF16), b_blk(ki, ni).astype(BF16))
                    part = d if part is None else part + d
                parts[mi, ni] = part

        def finish(total):
            if epilogue == "swiglu":
                gate, up = total[0, 0], total[0, 1]
                outs[0][0, 0] = gate.astype(BF16)
                outs[0][1, 0] = up.astype(BF16)
                outs[1][0] = (gate * _sigmoid(gate) * up).astype(BF16)
                return
            if epilogue == "swiglu_bwd":
                dact = total[0, 0]
                gate, up = e_refs[0][0, 0].astype(F32), e_refs[0][1, 0].astype(F32)
                sg = _sigmoid(gate)
                outs[0][0, 0] = (dact * up * (sg * (1.0 + gate * (1.0 - sg)))).astype(BF16)
                outs[0][1, 0] = (dact * (gate * sg)).astype(BF16)
                return
            if epilogue == "rms_bwd":
                x_ref, g_ref, dres_ref = e_refs
                dh, dg = total[0, 0], None
                for r in range(0, tm, NORM_ROWS):
                    rows = slice(r, r + NORM_ROWS)
                    xv, dhv = x_ref[rows, :], dh[rows, :]
                    rstd = lax.rsqrt(jnp.mean(xv * xv, axis=-1, keepdims=True) + EPS)
                    xh = xv * rstd
                    dyg = dhv * g_ref[...]
                    c = jnp.mean(dyg * xh, axis=-1, keepdims=True)
                    outs[0][rows, :] = dres_ref[rows, :] + rstd * (dyg - xh * c)
                    part = jnp.sum(dhv * xh, axis=0, keepdims=True)
                    dg = part if dg is None else dg + part
                _accumulate(outs[1], dg, pl.program_id(0))
                return
            if epilogue == "loss":
                diff = r_ref[...] + total[0, 0] - e_refs[0][...]
                outs[0][...] = diff * (1.0 / n)
                sq = jnp.sum(jnp.sum(diff * diff, axis=-1, keepdims=True), axis=0, keepdims=True)
                _accumulate(outs[1], sq * (0.5 / n), pl.program_id(0))
                return
            for (mi, ni), val in total.items():
                rows, cols = sl(mi, um, cm), sl(ni, un, cn)
                if res is not None:
                    val = r_ref[rows, cols] + val
                if blocked_out:
                    outs[0][ni, rows] = val.astype(out_dtype)
                else:
                    outs[0][rows, cols] = val.astype(out_dtype)

        if nk == 1:
            finish(parts)
        else:
            @pl.when(kk == 0)
            def _():
                for (mi, ni), val in parts.items():
                    acc_ref[mi * cn + ni] = val

            @pl.when(jnp.logical_and(kk > 0, kk < nk - 1))
            def _():
                for (mi, ni), val in parts.items():
                    acc_ref[mi * cn + ni] += val

            @pl.when(kk == nk - 1)
            def _():
                finish({key: acc_ref[key[0] * cn + key[1]] + val for key, val in parts.items()})

    if mode in ("nn", "nt"):
        a_spec = (pl.BlockSpec((ck, tm, uk), lambda i, j, kk: (kk, i, 0)) if a3
                  else pl.BlockSpec((tm, tk), lambda i, j, kk: (i, kk)))
    else:
        a_spec = (pl.BlockSpec((cm, tk, um), lambda i, j, kk: (i, kk, 0)) if a3
                  else pl.BlockSpec((tk, tm), lambda i, j, kk: (kk, i)))
    pair_spec = pl.BlockSpec((2, 1, tm, un), lambda i, j, kk: (0, j, i, 0))
    row_spec = pl.BlockSpec((tm, tn), lambda i, j, kk: (i, 0))
    vec_spec = pl.BlockSpec((1, tn), lambda i, j, kk: (0, 0))
    if epilogue == "swiglu" and mode == "nn":
        b = b.reshape(2, half, k, un)
        b_spec = pl.BlockSpec((2, 1, tk, un), lambda i, j, kk: (0, j, kk, 0))
    elif epilogue == "swiglu":
        b = b.reshape(2, half, un, k)
        b_spec = pl.BlockSpec((2, 1, un, tk), lambda i, j, kk: (0, j, 0, kk))
    elif mode in ("nn", "tn"):
        b_spec = (pl.BlockSpec((cn, tk, un), lambda i, j, kk: (j, kk, 0)) if b3
                  else pl.BlockSpec((tk, tn), lambda i, j, kk: (kk, j)))
    else:
        b_spec = (pl.BlockSpec((ck, tn, uk), lambda i, j, kk: (kk, j, 0)) if b3
                  else pl.BlockSpec((tn, tk), lambda i, j, kk: (j, kk)))
    if epilogue == "swiglu":
        out_specs = [pair_spec, pl.BlockSpec((1, tm, un), lambda i, j, kk: (j, i, 0))]
        out_shape = [jax.ShapeDtypeStruct((2, half, m, un), BF16), jax.ShapeDtypeStruct((half, m, un), BF16)]
    elif epilogue == "swiglu_bwd":
        out_specs = [pair_spec]
        out_shape = [jax.ShapeDtypeStruct(extra.shape, BF16)]
    elif epilogue == "rms_bwd":
        out_specs = [row_spec, vec_spec]
        out_shape = [jax.ShapeDtypeStruct((m, n), F32), jax.ShapeDtypeStruct((1, n), F32)]
    elif epilogue == "loss":
        out_specs = [row_spec, pl.BlockSpec((1, 1), lambda i, j, kk: (0, 0))]
        out_shape = [jax.ShapeDtypeStruct((m, n), F32), jax.ShapeDtypeStruct((1, 1), F32)]
    elif blocked_out:
        out_specs = [pl.BlockSpec((cn, tm, un), lambda i, j, kk: (j, i, 0))]
        out_shape = [jax.ShapeDtypeStruct((n // un, m, un), out_dtype)]
    else:
        out_specs = [pl.BlockSpec((tm, tn), lambda i, j, kk: (i, j))]
        out_shape = [jax.ShapeDtypeStruct((m, n), out_dtype)]
    in_specs, args = [a_spec, b_spec], [a, b]
    if res is not None:
        in_specs.append(pl.BlockSpec((tm, tn), lambda i, j, kk: (i, j)))
        args.append(res)
    if epilogue == "swiglu_bwd":
        in_specs.append(pair_spec)
    elif epilogue == "rms_bwd":
        in_specs += [row_spec, vec_spec, row_spec]
    elif epilogue == "loss":
        in_specs.append(row_spec)
    args += extras
    if norm_g is not None:
        in_specs.append(pl.BlockSpec((1, D_MODEL), lambda i, j, kk: (0, 0)))
        args.append(norm_g)
    semantics = ("arbitrary",) * 3 if epilogue in ("rms_bwd", "loss") else ("parallel", "parallel", "arbitrary")
    out = _call(body, name, (m // tm, n // tn, nk), in_specs, out_specs, out_shape,
                [pltpu.VMEM((cm * cn, um, un), F32)], semantics, args, rider)
    return out if epilogue in ("swiglu", "rms_bwd", "loss") else out[0]


def _head_sums(v, ind):
    return _dot(v.astype(BF16), ind)


def _head_spread(per_head, ind):
    hi, lo = _split2(per_head)
    return _dot_nt(hi, ind) + _dot_nt(lo, ind)


def _head_rstd(xv, ind):
    return _head_spread(lax.rsqrt(_head_sums(xv * xv, ind) * (1.0 / ATT_DH) + EPS), ind)


def _hn_bwd_math(xv, gv, ind, dyv, scale):
    rstd = _head_rstd(xv, ind)
    xh = xv * rstd
    dyn = dyv * scale
    dyg = dyn * gv
    dx = rstd * (dyg - xh * _head_spread(_head_sums(dyg * xh, ind) * (1.0 / ATT_DH), ind))
    return dx, jnp.sum(dyn * xh, axis=0, keepdims=True)


def _q_hnorm(x, g_tiled, bd, scale, name):
    t, d = x.shape
    tm = _pick(t, 512, 16)

    def body(x_ref, g_ref, bd_ref, o_ref):
        xv = x_ref[...]
        o_ref[...] = (xv * _head_rstd(xv, bd_ref[...]) * g_ref[...] * scale).astype(BF16)

    return pl.pallas_call(
        body, name=name, grid=(t // tm,),
        in_specs=[pl.BlockSpec((tm, d), lambda i: (i, 0)), pl.BlockSpec((1, d), lambda i: (0, 0)),
                  pl.BlockSpec((d, LANES), lambda i: (0, 0))],
        out_specs=pl.BlockSpec((tm, d), lambda i: (i, 0)),
        out_shape=jax.ShapeDtypeStruct((t, d), BF16),
        compiler_params=_params("parallel"),
    )(x, g_tiled, bd)


def _q_dhnorm(x, g_tiled, bd, dy, scale, name):
    t, d = x.shape
    tm = _pick(t, 512, 16)

    def body(x_ref, g_ref, bd_ref, dy_ref, dx_ref, dg_ref):
        dx, part = _hn_bwd_math(x_ref[...], g_ref[...], bd_ref[...], dy_ref[...], scale)
        dx_ref[...] = dx.astype(BF16)
        _accumulate(dg_ref, part, pl.program_id(0))

    row = pl.BlockSpec((tm, d), lambda i: (i, 0))
    vec = pl.BlockSpec((1, d), lambda i: (0, 0))
    return pl.pallas_call(
        body, name=name, grid=(t // tm,),
        in_specs=[row, vec, pl.BlockSpec((d, LANES), lambda i: (0, 0)), row],
        out_specs=[row, vec],
        out_shape=[jax.ShapeDtypeStruct((t, d), BF16), jax.ShapeDtypeStruct((1, d), F32)],
        compiler_params=_params("arbitrary"),
    )(x, g_tiled, bd, dy)


def _kv_prep(kv, g_tiled, bd, name):
    t = kv.shape[0]
    d = D_MODEL
    tm = K_PAD
    assert t % tm == 0

    def body(k_ref, v_ref, g_ref, bd_ref, kp_ref, vp_ref):
        i = pl.program_id(0)

        @pl.when(i == 0)
        def _():
            kp_ref[...] = jnp.zeros_like(kp_ref)
            vp_ref[...] = jnp.zeros_like(vp_ref)

        @pl.when(i > 0)
        def _():
            xv = k_ref[...]
            kp_ref[...] = (xv * _head_rstd(xv, bd_ref[...]) * g_ref[...]).astype(BF16)
            vp_ref[...] = v_ref[...].astype(BF16)

    shp = jax.ShapeDtypeStruct((t + K_PAD, d), BF16)
    out = pl.BlockSpec((tm, d), lambda i: (i, 0))
    return pl.pallas_call(
        body, name=name, grid=(t // tm + 1,),
        in_specs=[pl.BlockSpec((tm, d), lambda i: (jnp.maximum(i - 1, 0), 0)),
                  pl.BlockSpec((tm, d), lambda i: (jnp.maximum(i - 1, 0), 1)),
                  pl.BlockSpec((1, d), lambda i: (0, 0)), pl.BlockSpec((d, LANES), lambda i: (0, 0))],
        out_specs=[out, out], out_shape=[shp, shp],
        compiler_params=_params("arbitrary"),
    )(kv, kv, g_tiled, bd)


def _kv_dprep(kv, g_tiled, bd, dkp_t, dvp_t, name):
    t = kv.shape[0]
    d = D_MODEL
    tm = K_PAD

    def body(k_ref, g_ref, bd_ref, dk_ref, dv_ref, o_ref, dg_ref):
        dx, part = _hn_bwd_math(k_ref[...], g_ref[...], bd_ref[...], dk_ref[...].T, 1.0)
        o_ref[:, :d] = dx.astype(BF16)
        o_ref[:, d:] = dv_ref[...].T.astype(BF16)
        _accumulate(dg_ref, part, pl.program_id(0))

    vec = pl.BlockSpec((1, d), lambda i: (0, 0))
    padded = pl.BlockSpec((d, tm), lambda i: (0, i + 1))
    return pl.pallas_call(
        body, name=name, grid=(t // tm,),
        in_specs=[pl.BlockSpec((tm, d), lambda i: (i, 0)), vec, pl.BlockSpec((d, LANES), lambda i: (0, 0)),
                  padded, padded],
        out_specs=[pl.BlockSpec((tm, 2 * d), lambda i: (i, 0)), vec],
        out_shape=[jax.ShapeDtypeStruct((t, 2 * d), BF16), jax.ShapeDtypeStruct((1, d), F32)],
        compiler_params=_params("arbitrary"),
    )(kv, g_tiled, bd, dkp_t, dvp_t)


def _ret_consts(t):
    h = np.arange(RET_HEADS, dtype=np.float32)
    lg = np.log(np.float32(1.0) - np.float32(2.0) ** (np.float32(-5.0) - h)).astype(np.float32)
    tt = np.arange(CHUNK, dtype=np.float32)
    intra = np.exp(lg[:, None, None] * np.abs(tt[:, None] - tt[None, :])).astype(np.float32)
    q_dec = np.exp(lg[:, None] * (tt + 1.0)).astype(np.float32)
    k_dec = np.exp(lg[:, None] * (CHUNK - 1.0 - tt)).astype(np.float32)
    s_dec = [float(v) for v in np.exp(lg * np.float32(CHUNK)).astype(np.float32)]
    qd = np.broadcast_to(q_dec[:, :, None], (RET_HEADS, CHUNK, RET_DK)).copy()
    kd = np.broadcast_to(k_dec[:, :, None], (RET_HEADS, CHUNK, RET_DK)).copy()
    half = RET_DK // 2
    inv_freq = np.float32(ROPE_BASE) ** (-np.arange(half, dtype=np.float32) / np.float32(half))
    ang = np.arange(t, dtype=np.float32)[:, None] * inv_freq[None, :]
    return jnp.asarray(intra), jnp.asarray(qd), jnp.asarray(kd), s_dec, jnp.asarray(np.cos(ang)), jnp.asarray(np.sin(ang))


def _rope(x, cos, sin):
    half = RET_DK // 2
    x1, x2 = x[:, :half], x[:, half:]
    return jnp.concatenate([x1 * cos - x2 * sin, x1 * sin + x2 * cos], axis=-1)


def _unrope(d, cos, sin):
    half = RET_DK // 2
    d1, d2 = d[:, :half], d[:, half:]
    return jnp.concatenate([d1 * cos + d2 * sin, d2 * cos - d1 * sin], axis=-1)


def _ret_slices(h):
    q = slice(h * RET_DK, (h + 1) * RET_DK)
    k = slice(RET_Q_COLS + h * RET_DK, RET_Q_COLS + (h + 1) * RET_DK)
    v = slice(2 * RET_Q_COLS + h * RET_DV, 2 * RET_Q_COLS + (h + 1) * RET_DV)
    g = slice(2 * RET_Q_COLS + RET_V_COLS + h * RET_DV, 2 * RET_Q_COLS + RET_V_COLS + (h + 1) * RET_DV)
    o = slice(h * RET_DV, (h + 1) * RET_DV)
    return q, k, v, g, o


def _ret_fwd(proj, gn, consts, name, rider=None):
    t, cols = proj.shape
    n = t // CHUNK
    intra, qd, kd, s_dec, cos, sin = consts
    k_scale = RET_DK ** -0.5

    def body(p_ref, cos_ref, sin_ref, intra_ref, qd_ref, kd_ref, gn_ref, y_ref, o_ref, st_ref, state):
        i = pl.program_id(0)

        @pl.when(i == 0)
        def _():
            state[...] = jnp.zeros_like(state)

        for c in range(RET_STEP):
            rows = slice(c * CHUNK, (c + 1) * CHUNK)
            cosv, sinv = cos_ref[rows, :], sin_ref[rows, :]
            for h in range(RET_HEADS):
                qs, ks, vs, gs, os_ = _ret_slices(h)
                qr = _rope(p_ref[rows, qs], cosv, sinv)
                kr = _rope(p_ref[rows, ks], cosv, sinv) * k_scale
                vb = p_ref[rows, vs].astype(BF16)
                gv = p_ref[rows, gs]
                scores = _dot_nt(qr.astype(BF16), kr.astype(BF16)) * intra_ref[h]
                s_old = state[h]
                s_old_b = s_old.astype(BF16)
                st_ref[c, h] = s_old_b
                o = _dot(scores.astype(BF16), vb) + _dot((qr * qd_ref[h]).astype(BF16), s_old_b)
                state[h] = s_old * s_dec[h] + _dot_tn((kr * kd_ref[h]).astype(BF16), vb)
                rstd = lax.rsqrt(jnp.mean(o * o, axis=-1, keepdims=True) + EPS)
                on = o * rstd * gn_ref[:, os_]
                o_ref[rows, os_] = o
                y_ref[rows, os_] = (gv * _sigmoid(gv) * on).astype(BF16)

    full3 = lambda a: pl.BlockSpec(a.shape, lambda i: (0, 0, 0))
    step = RET_STEP * CHUNK
    return _call(
        body, name, (n // RET_STEP,),
        [pl.BlockSpec((step, cols), lambda i: (i, 0)),
         pl.BlockSpec((step, RET_DK // 2), lambda i: (i, 0)),
         pl.BlockSpec((step, RET_DK // 2), lambda i: (i, 0)),
         full3(intra), full3(qd), full3(kd),
         pl.BlockSpec((1, RET_V_COLS), lambda i: (0, 0))],
        [pl.BlockSpec((step, RET_V_COLS), lambda i: (i, 0)),
         pl.BlockSpec((step, RET_V_COLS), lambda i: (i, 0)),
         pl.BlockSpec((RET_STEP, RET_HEADS, RET_DK, RET_DV), lambda i: (i, 0, 0, 0))],
        [jax.ShapeDtypeStruct((t, RET_V_COLS), BF16),
         jax.ShapeDtypeStruct((t, RET_V_COLS), F32),
         jax.ShapeDtypeStruct((n, RET_HEADS, RET_DK, RET_DV), BF16)],
        [pltpu.VMEM((RET_HEADS, RET_DK, RET_DV), F32)], ("arbitrary",),
        (proj, cos, sin, intra, qd, kd, gn), rider)


def _ret_bwd(proj, gn, o_saved, states, dy, consts, name, rider=None):
    t, cols = proj.shape
    n = t // CHUNK
    intra, qd, kd, s_dec, cos, sin = consts
    k_scale = RET_DK ** -0.5

    def body(p_ref, cos_ref, sin_ref, intra_ref, qd_ref, kd_ref, gn_ref, o_ref, st_ref, dy_ref,
             dp_ref, dgn_ref, dstate):
        i = pl.program_id(0)

        @pl.when(i == 0)
        def _():
            dstate[...] = jnp.zeros_like(dstate)

        dgn = None
        for c in reversed(range(RET_STEP)):
            rows = slice(c * CHUNK, (c + 1) * CHUNK)
            cosv, sinv = cos_ref[rows, :], sin_ref[rows, :]
            dgn_parts = []
            for h in range(RET_HEADS):
                qs, ks, vs, gs, os_ = _ret_slices(h)
                qr = _rope(p_ref[rows, qs], cosv, sinv)
                kr = _rope(p_ref[rows, ks], cosv, sinv) * k_scale
                qb, kb = qr.astype(BF16), kr.astype(BF16)
                vb = p_ref[rows, vs].astype(BF16)
                gv = p_ref[rows, gs]
                ov = o_ref[rows, os_]
                dyv = dy_ref[rows, os_]
                gnv = gn_ref[:, os_]
                sg = _sigmoid(gv)
                rstd = lax.rsqrt(jnp.mean(ov * ov, axis=-1, keepdims=True) + EPS)
                oh = ov * rstd
                d_on = dyv * (gv * sg)
                dg = dyv * (oh * gnv) * (sg * (1.0 + gv * (1.0 - sg)))
                dgn_parts.append(jnp.sum(d_on * oh, axis=0, keepdims=True))
                d_oh = d_on * gnv
                do = rstd * (d_oh - oh * jnp.mean(d_oh * oh, axis=-1, keepdims=True))
                dob = do.astype(BF16)
                mask = intra_ref[h]
                a_b = (_dot_nt(qb, kb) * mask).astype(BF16)
                da_b = (_dot_nt(dob, vb) * mask).astype(BF16)
                ds_new = dstate[h]
                ds_new_b = ds_new.astype(BF16)
                s_old_b = st_ref[c, h]
                qdv, kdv = qd_ref[h], kd_ref[h]
                dv = _dot_tn(a_b, dob) + _dot((kr * kdv).astype(BF16), ds_new_b)
                dqr = _dot(da_b, kb) + _dot_nt(dob, s_old_b) * qdv
                dkr = _dot_tn(da_b, qb) + _dot_nt(vb, ds_new_b) * kdv
                dstate[h] = ds_new * s_dec[h] + _dot_tn((qr * qdv).astype(BF16), dob)
                dp_ref[rows, qs] = _unrope(dqr, cosv, sinv).astype(BF16)
                dp_ref[rows, ks] = _unrope(dkr * k_scale, cosv, sinv).astype(BF16)
                dp_ref[rows, vs] = dv.astype(BF16)
                dp_ref[rows, gs] = dg.astype(BF16)
            part = jnp.concatenate(dgn_parts, axis=-1)
            dgn = part if dgn is None else dgn + part
        _accumulate(dgn_ref, dgn, i)

    steps = n // RET_STEP
    step = RET_STEP * CHUNK
    rev = lambda i: (steps - 1 - i, 0)
    full3 = lambda a: pl.BlockSpec(a.shape, lambda i: (0, 0, 0))
    return _call(
        body, name, (steps,),
        [pl.BlockSpec((step, cols), rev),
         pl.BlockSpec((step, RET_DK // 2), rev),
         pl.BlockSpec((step, RET_DK // 2), rev),
         full3(intra), full3(qd), full3(kd),
         pl.BlockSpec((1, RET_V_COLS), lambda i: (0, 0)),
         pl.BlockSpec((step, RET_V_COLS), rev),
         pl.BlockSpec((RET_STEP, RET_HEADS, RET_DK, RET_DV), lambda i: (steps - 1 - i, 0, 0, 0)),
         pl.BlockSpec((step, RET_V_COLS), rev)],
        [pl.BlockSpec((step, cols), rev),
         pl.BlockSpec((1, RET_V_COLS), lambda i: (0, 0))],
        [jax.ShapeDtypeStruct((t, cols), BF16),
         jax.ShapeDtypeStruct((1, RET_V_COLS), F32)],
        [pltpu.VMEM((RET_HEADS, RET_DK, RET_DV), F32)], ("arbitrary",),
        (proj, cos, sin, intra, qd, kd, gn, o_saved, states, dy), rider)


def _att_common(q_ref, kp_ref, vp_ref, sub):
    blk = pl.program_id(1) * ATT_SUBS + sub
    start = pl.multiple_of(blk * Q_BLOCK, Q_BLOCK)
    kw = kp_ref[pl.ds(start, K_WINDOW), :]
    vw = vp_ref[pl.ds(start, K_WINDOW), :]
    kvalid = blk * Q_BLOCK - K_PAD + lax.broadcasted_iota(jnp.int32, (1, K_WINDOW), 1) >= 0
    lane = lax.broadcasted_iota(jnp.int32, (1, LANES), 1)
    qrows = slice(sub * Q_BLOCK, (sub + 1) * Q_BLOCK)
    return start, qrows, q_ref[qrows, :], kw, vw, kvalid, (lane < ATT_DH, lane >= ATT_DH)


def _row_groups():
    return [slice(r * ATT_ROWS, (r + 1) * ATT_ROWS) for r in range(Q_BLOCK // ATT_ROWS)]


def _lane_copies(x):
    return jnp.tile(x, (1, K_WINDOW // LANES))


def _att_specs(t, tp):
    qspec = pl.BlockSpec((ATT_SUBS * Q_BLOCK, LANES), lambda h, i: (i, h))
    kspec = pl.BlockSpec((tp, LANES), lambda h, i: (0, h))
    bspec = pl.BlockSpec((2, Q_BLOCK, K_WINDOW), lambda h, i: (h, 0, 0))
    return qspec, kspec, bspec


def _att_fwd(q, kp, vp, bias, name, rider=None):
    t, d = q.shape
    tp = kp.shape[0]

    def body(q_ref, kp_ref, vp_ref, bias_ref, o_ref, lse_ref, s_scr, p_scr, lse_scr, inv_scr):
        for sub in range(ATT_SUBS):
            _, qrows, q2, kw, vw, kvalid, sel = _att_common(q_ref, kp_ref, vp_ref, sub)
            for hh in range(2):
                s_scr[sub, hh] = _dot_nt(jnp.where(sel[hh], q2, 0), kw)
            for hh in range(2):
                for rows in _row_groups():
                    s = jnp.where(kvalid, s_scr[sub, hh, rows, :] + bias_ref[hh, rows, :], NEG)
                    m = jnp.max(s, axis=-1, keepdims=True)
                    e = jnp.exp(s - m)
                    l = jnp.sum(e, axis=-1, keepdims=True)
                    p_scr[sub, hh, rows, :] = e.astype(BF16)
                    inv_scr[sub, hh, rows, :] = jnp.broadcast_to(1.0 / l, (ATT_ROWS, LANES))
                    lse_scr[sub, hh, rows, :] = jnp.broadcast_to(m + jnp.log(l), (ATT_ROWS, LANES))
            outs = [_dot(p_scr[sub, hh], vw) * inv_scr[sub, hh] for hh in range(2)]
            o_ref[qrows, :] = jnp.where(sel[0], outs[0], outs[1]).astype(BF16)
            lse_ref[qrows, :] = jnp.where(sel[0], lse_scr[sub, 0], lse_scr[sub, 1])

    qspec, kspec, bspec = _att_specs(t, tp)
    return _call(body, name, (d // LANES, t // (ATT_SUBS * Q_BLOCK)), [qspec, kspec, kspec, bspec], [qspec, qspec],
                 [jax.ShapeDtypeStruct((t, d), BF16), jax.ShapeDtypeStruct((t, d), F32)],
                 [pltpu.VMEM((ATT_SUBS, 2, Q_BLOCK, K_WINDOW), F32),
                  pltpu.VMEM((ATT_SUBS, 2, Q_BLOCK, K_WINDOW), BF16),
                  pltpu.VMEM((ATT_SUBS, 2, Q_BLOCK, LANES), F32),
                  pltpu.VMEM((ATT_SUBS, 2, Q_BLOCK, LANES), F32)],
                 ("parallel", "arbitrary"), (q, kp, vp, bias), rider)


def _att_bwd(q, kp, vp, bias, do, o, lse, name, rider=None):
    t, d = q.shape
    tp = kp.shape[0]

    def body(q_ref, kp_ref, vp_ref, bias_ref, do_ref, o_ref, lse_ref, dq_ref, dkp_ref, dvp_ref, db_ref,
             s_scr, dp_scr, p_scr, ds_scr, row_scr):
        @pl.when(pl.program_id(1) == 0)
        def _():
            dkp_ref[...] = jnp.zeros_like(dkp_ref)
            dvp_ref[...] = jnp.zeros_like(dvp_ref)
            db_ref[...] = jnp.zeros_like(db_ref)

        for sub in range(ATT_SUBS):
            start, qrows, q2, kw, vw, kvalid, sel = _att_common(q_ref, kp_ref, vp_ref, sub)
            do2 = do_ref[qrows, :]
            qm = [jnp.where(sel[hh], q2, 0) for hh in range(2)]
            dom = [jnp.where(sel[hh], do2, 0) for hh in range(2)]
            do_o = do2.astype(F32) * o_ref[qrows, :].astype(F32)
            lse2 = lse_ref[qrows, :]
            for hh in range(2):
                s_scr[sub, hh] = _dot_nt(qm[hh], kw)
                dp_scr[sub, hh] = _dot_nt(dom[hh], vw)
                lse_h = jnp.max(jnp.where(sel[hh], lse2, NEG), axis=-1, keepdims=True)
                delta = jnp.sum(jnp.where(sel[hh], do_o, 0.0), axis=-1, keepdims=True)
                row_scr[sub, hh, 0] = jnp.broadcast_to(lse_h, (Q_BLOCK, LANES))
                row_scr[sub, hh, 1] = jnp.broadcast_to(delta, (Q_BLOCK, LANES))
            for hh in range(2):
                for rows in _row_groups():
                    s = jnp.where(kvalid, s_scr[sub, hh, rows, :] + bias_ref[hh, rows, :], NEG)
                    p = jnp.exp(s - _lane_copies(row_scr[sub, hh, 0, rows, :]))
                    ds = p * (dp_scr[sub, hh, rows, :] - _lane_copies(row_scr[sub, hh, 1, rows, :]))
                    db_ref[hh, rows, :] += ds
                    p_scr[sub, hh, rows, :] = p.astype(BF16)
                    ds_scr[sub, hh, rows, :] = ds.astype(BF16)
            dqs = [_dot(ds_scr[sub, hh], kw) for hh in range(2)]
            dq_ref[qrows, :] = jnp.where(sel[0], dqs[0], dqs[1])
            dkp_ref[:, pl.ds(start, K_WINDOW)] += (_dot_tn(qm[0], ds_scr[sub, 0]) +
                                                   _dot_tn(qm[1], ds_scr[sub, 1]))
            dvp_ref[:, pl.ds(start, K_WINDOW)] += (_dot_tn(dom[0], p_scr[sub, 0]) +
                                                   _dot_tn(dom[1], p_scr[sub, 1]))

    qspec, kspec, bspec = _att_specs(t, tp)
    tspec = pl.BlockSpec((LANES, tp), lambda h, i: (h, 0))
    stage = lambda dt: pltpu.VMEM((ATT_SUBS, 2, Q_BLOCK, K_WINDOW), dt)
    return _call(body, name, (d // LANES, t // (ATT_SUBS * Q_BLOCK)),
                 [qspec, kspec, kspec, bspec, qspec, qspec, qspec],
                 [qspec, tspec, tspec, bspec],
                 [jax.ShapeDtypeStruct((t, d), F32),
                  jax.ShapeDtypeStruct((d, tp), F32),
                  jax.ShapeDtypeStruct((d, tp), F32),
                  jax.ShapeDtypeStruct((ATT_HEADS, Q_BLOCK, K_WINDOW), F32)],
                 [stage(F32), stage(F32), stage(BF16), stage(BF16),
                  pltpu.VMEM((ATT_SUBS, 2, 2, Q_BLOCK, LANES), F32)],
                 ("parallel", "arbitrary"), (q, kp, vp, bias, do, o, lse), rider)


def _rel_bin_matrix():
    rows = REL_DELTAS * 2 * REL_BLK
    rho = lax.broadcasted_iota(jnp.int32, (rows, REL_PAD), 0)
    col = lax.broadcasted_iota(jnp.int32, (rows, REL_PAD), 1)
    assert 2 * REL_BLK == 256
    delta = rho >> 8
    c = 255 - (rho & 255)
    dist = K_PAD + REL_BLK * (delta - (K_WINDOW // REL_BLK - 1)) + (c - (REL_BLK - 1))
    idx = jnp.clip(dist, -REL_CLIP, REL_CLIP) + REL_CLIP
    return col == idx


def _rel_expand(rel_pad, name):
    heads = rel_pad.shape[0]
    rows = REL_DELTAS * 2 * REL_BLK

    def body_bin(r_ref, o_ref):
        onehot = jnp.where(_rel_bin_matrix(), 1.0, 0.0).astype(BF16)
        hi, mid, lo = _split3(r_ref[...])
        o_ref[...] = _dot_nt(hi, onehot) + _dot_nt(mid, onehot) + _dot_nt(lo, onehot)

    by_delta = pl.pallas_call(
        body_bin, name=name + "_bin",
        out_shape=jax.ShapeDtypeStruct((heads, rows), F32),
        compiler_params=pltpu.CompilerParams(vmem_limit_bytes=VMEM_LIMIT_V7X),
    )(rel_pad)
    by_delta = by_delta.reshape(heads * REL_DELTAS, 2 * REL_BLK)

    def body_shift(t_ref, o_ref):
        tv = t_ref[...]
        for r in range(REL_BLK):
            o_ref[r] = pltpu.roll(tv, (r + REL_BLK) % (2 * REL_BLK), 1)[:, :REL_BLK]

    return pl.pallas_call(
        body_shift, name=name + "_shift",
        out_shape=jax.ShapeDtypeStruct((REL_BLK, heads * REL_DELTAS, REL_BLK), F32),
        compiler_params=pltpu.CompilerParams(vmem_limit_bytes=VMEM_LIMIT_V7X),
    )(by_delta)


def _bias_table(rel_bias, name):
    heads = rel_bias.shape[0]
    rel_pad = jnp.pad(rel_bias, ((0, 0), (0, REL_PAD - REL_TABLE)))
    tiles = _rel_expand(rel_pad, name)
    tiles = tiles.reshape(REL_BLK, heads, REL_DELTAS, REL_BLK).transpose(1, 2, 0, 3)
    na, nb = Q_BLOCK // REL_BLK, K_WINDOW // REL_BLK
    rows = [jnp.concatenate([tiles[:, a - b + nb - 1] for b in range(nb)], axis=-1) for a in range(na)]
    table = jnp.concatenate(rows, axis=-2)
    qc = np.arange(Q_BLOCK)[:, None] // CHUNK
    kc = np.arange(K_WINDOW)[None, :] // CHUNK
    band = (kc >= qc) & (kc <= qc + PAST_CHUNKS)
    return jnp.where(jnp.asarray(band)[None], table, NEG)


def _rel_reduce(db, name):
    heads = db.shape[0]
    na, nb = Q_BLOCK // REL_BLK, K_WINDOW // REL_BLK

    fold_heads = 4

    def body_fold(db_ref, g_ref):
        for hd in range(fold_heads):
            for delta in range(REL_DELTAS):
                acc = None
                for a in range(na):
                    b = a - (delta - (nb - 1))
                    if 0 <= b < nb:
                        tile = db_ref[hd, a * REL_BLK:(a + 1) * REL_BLK, b * REL_BLK:(b + 1) * REL_BLK]
                        acc = tile if acc is None else acc + tile
                g_ref[hd, delta] = acc

    folded = pl.pallas_call(
        body_fold, name=name + "_fold", grid=(heads // fold_heads,),
        in_specs=[pl.BlockSpec((fold_heads, Q_BLOCK, K_WINDOW), lambda h: (h, 0, 0))],
        out_specs=pl.BlockSpec((fold_heads, REL_DELTAS, REL_BLK, REL_BLK), lambda h: (h, 0, 0, 0)),
        out_shape=jax.ShapeDtypeStruct((heads, REL_DELTAS, REL_BLK, REL_BLK), F32),
        compiler_params=_params("parallel"),
    )(db)
    by_row = folded.transpose(2, 0, 1, 3).reshape(REL_BLK, heads * REL_DELTAS, REL_BLK)

    def body_diag(g_ref, d_ref):
        zeros = jnp.zeros((heads * REL_DELTAS, REL_BLK), F32)
        acc = None
        for r in range(REL_BLK):
            part = pltpu.roll(jnp.concatenate([g_ref[r], zeros], axis=1), REL_BLK - r, 1)
            acc = part if acc is None else acc + part
        d_ref[...] = acc

    diag = pl.pallas_call(
        body_diag, name=name + "_diag",
        out_shape=jax.ShapeDtypeStruct((heads * REL_DELTAS, 2 * REL_BLK), F32),
        compiler_params=pltpu.CompilerParams(vmem_limit_bytes=VMEM_LIMIT_V7X),
    )(by_row)
    diag = diag.reshape(heads, REL_DELTAS * 2 * REL_BLK)

    def body_bin(d_ref, o_ref):
        onehot = jnp.where(_rel_bin_matrix(), 1.0, 0.0).astype(BF16)
        hi, mid, lo = _split3(d_ref[...])
        o_ref[...] = _dot(hi, onehot) + _dot(mid, onehot) + _dot(lo, onehot)

    out = pl.pallas_call(
        body_bin, name=name + "_bin",
        out_shape=jax.ShapeDtypeStruct((heads, REL_PAD), F32),
        compiler_params=pltpu.CompilerParams(vmem_limit_bytes=VMEM_LIMIT_V7X),
    )(diag)
    return out[:, :REL_TABLE]


def _sum_leading(x, name):
    n, r, c = x.shape
    tr = _pick(r, 256, 8)

    def body(x_ref, o_ref):
        acc = x_ref[0].astype(F32)
        for k in range(1, n):
            acc = acc + x_ref[k].astype(F32)
        o_ref[...] = acc

    return pl.pallas_call(
        body, name=name, grid=(r // tr,),
        in_specs=[pl.BlockSpec((n, tr, c), lambda i: (0, i, 0))],
        out_specs=pl.BlockSpec((tr, c), lambda i: (i, 0)),
        out_shape=jax.ShapeDtypeStruct((r, c), F32),
        compiler_params=_params("parallel"),
    )(x)


def _pair_add(g, recv, parity, name):
    _, r, c = g.shape
    tr = _pick(r, 256, 16)

    def body(par_ref, g_ref, r_ref, o_ref):
        o_ref[...] = (g_ref[...].astype(F32) + r_ref[...].astype(F32)).astype(BF16)

    return pl.pallas_call(
        body, name=name,
        grid_spec=pltpu.PrefetchScalarGridSpec(
            num_scalar_prefetch=1, grid=(4, r // tr),
            in_specs=[pl.BlockSpec((1, tr, c), lambda k, i, par: (2 * k + par[0], i, 0)),
                      pl.BlockSpec((1, tr, c), lambda k, i, par: (k, i, 0))],
            out_specs=pl.BlockSpec((1, tr, c), lambda k, i, par: (k, i, 0))),
        out_shape=jax.ShapeDtypeStruct((4, r, c), BF16),
        compiler_params=_params("parallel", "parallel"),
    )(parity, g, recv)


def _adamw(w, g_parts, m, v, name):
    r, c = w.shape
    n = g_parts.shape[0]
    tr = _pick(r, 256, 16 if g_parts.dtype == BF16 else 8)
    c1 = 1.0 - ADAM_B1 ** ADAM_STEP
    c2 = 1.0 - ADAM_B2 ** ADAM_STEP

    def body(w_ref, g_ref, m_ref, v_ref, go_ref, d_ref, nm_ref, nv_ref):
        gv = g_ref[0].astype(F32)
        for k in range(1, n):
            gv = gv + g_ref[k].astype(F32)
        nm = ADAM_B1 * m_ref[...] + (1.0 - ADAM_B1) * gv
        nv = ADAM_B2 * v_ref[...] + (1.0 - ADAM_B2) * (gv * gv)
        go_ref[...] = gv
        d_ref[...] = -ADAM_LR * ((nm / c1) / (jnp.sqrt(nv / c2) + ADAM_EPS) + ADAM_WD * w_ref[...])
        nm_ref[...] = nm
        nv_ref[...] = nv

    spec = pl.BlockSpec((tr, c), lambda i: (i, 0))
    shp = jax.ShapeDtypeStruct((r, c), F32)
    return pl.pallas_call(
        body, name=name, grid=(r // tr,),
        in_specs=[spec, pl.BlockSpec((n, tr, c), lambda i: (0, i, 0)), spec, spec],
        out_specs=[spec] * 4, out_shape=[shp] * 4,
        compiler_params=_params("parallel"),
    )(w, g_parts, m, v)


BIG = (("a_w_in", 1), ("a_w_o", 0), ("a_w_gu", 0), ("a_w_down", 0), ("w_kv", 1),
       ("b_w_q", 0), ("b_w_o", 0), ("b_w_gu", 0), ("b_w_down", 0))
TRANSPOSED = ("a_w_gu", "b_w_gu")
FFN_BLK = 2 * FFN_HIDDEN // N_DEV

SMALL = (("a_norm_g", D_MODEL, True), ("a_gn_g", RET_V_COLS, True), ("a_ffn_norm_g", D_MODEL, True),
         ("kv_norm_g", D_MODEL, False), ("b_norm_g", D_MODEL, False), ("b_ffn_norm_g", D_MODEL, False),
         ("k_norm_g", ATT_DH, False), ("b_q_norm_g", ATT_DH, False),
         ("b_rel_bias", ATT_HEADS * REL_TABLE, False))
SMALL_ROWS, SMALL_COLS = 16, 1024


def _pack_small(vals, last=None):
    flat = jnp.concatenate([vals[n].reshape(-1) for n, _, _ in SMALL])
    room = SMALL_ROWS * SMALL_COLS - flat.shape[0]
    if last is None:
        flat = jnp.pad(flat, (0, room))
    else:
        flat = jnp.concatenate([jnp.pad(flat, (0, room - 1)), last.reshape(1)])
    return flat.reshape(SMALL_ROWS, SMALL_COLS)


def _unpack_small(packed, local):
    flat, out, pos = packed.reshape(-1), {}, 0
    for n, length, sharded in SMALL:
        ln = length // N_DEV if (local and sharded) else length
        out[n] = flat[pos:pos + ln]
        pos += ln
    return out


def _gather_rider(shards, names):
    return _GatherRider([shards[n] for n in names])


def _gathered(rider, names, axis_of):
    return {n: (r.reshape(-1, r.shape[2]) if axis_of[n] == 0 else r) for n, r in zip(names, rider.results)}


def _blocks(g):
    return g if g.ndim == 3 else g.reshape(N_DEV, -1, g.shape[-1])


def _local_step(x, target, shards, s, parity):
    t = x.shape[0]
    axis_of = dict(BIG)
    consts = _ret_consts(t)
    lane_to_head = np.zeros((D_MODEL, LANES), np.float32)
    lane_to_head[np.arange(D_MODEL), np.arange(D_MODEL) // ATT_DH] = 1.0
    bd = jnp.asarray(lane_to_head).astype(BF16)
    kg_t = jnp.tile(s["k_norm_g"], (1, ATT_HEADS))
    qg_t = jnp.tile(s["b_q_norm_g"], (1, ATT_HEADS))
    q_scale = ATT_DH ** -0.5
    w, g, recv = {}, {}, {}

    def gather_on(names):
        return _gather_rider(shards, names), names

    def landed(ride):
        w.update(_gathered(ride[0], ride[1], axis_of))

    def scatter_on(names):
        return _ScatterRider([_blocks(g[n]) for n in names]), names

    def reduced(ride):
        recv.update(zip(ride[1], ride[0].results))

    proj, (w["a_w_in"], w_o) = _proj_gather(x, s["a_norm_g"], shards["a_w_in"], [shards["a_w_o"]], "a_proj")
    w["a_w_o"] = w_o.reshape(-1, w_o.shape[2])
    ride = gather_on(["a_w_gu"])
    y, o_ret, states = _ret_fwd(proj, s["a_gn_g"], consts, "a_ret", rider=ride[0])
    landed(ride)
    ride = gather_on(["w_kv"])
    x1 = _mm(y, w["a_w_o"], "nn", "a_out", res=x, rider=ride[0])
    landed(ride)
    ride = gather_on(["a_w_down", "b_w_q", "b_w_o"])
    gu_a, act_a = _mm(x1, w["a_w_gu"], "nt", "a_ffn_gu", epilogue="swiglu", out_block=FFN_BLK,
                      norm_g=s["a_ffn_norm_g"], rider=ride[0])
    landed(ride)
    x2 = _mm(act_a, w["a_w_down"], "nn", "a_ffn_down", res=x1)

    kv = _mm(x2, w["w_kv"], "nn", "kv_proj", norm_g=s["kv_norm_g"])
    kp, vp = _kv_prep(kv, kg_t, bd, "kv_prep")

    q_raw = _mm(x2, w["b_w_q"], "nn", "b_q", norm_g=s["b_norm_g"])
    qn = _q_hnorm(q_raw, qg_t, bd, q_scale, "q_hnorm")
    bias = _bias_table(s["b_rel_bias"].reshape(ATT_HEADS, REL_TABLE), "rel")
    ride = gather_on(["b_w_gu"])
    o_att, lse = _att_fwd(qn, kp, vp, bias, "b_att", rider=ride[0])
    landed(ride)
    x3 = _mm(o_att, w["b_w_o"], "nn", "b_out", res=x2)
    ride = gather_on(["b_w_down"])
    gu_b, act_b = _mm(x3, w["b_w_gu"], "nt", "b_ffn_gu", epilogue="swiglu", out_block=FFN_BLK,
                      norm_g=s["b_ffn_norm_g"], rider=ride[0])
    landed(ride)
    dy, loss = _mm(act_b, w["b_w_down"], "nn", "b_ffn_down", res=x3, epilogue="loss", extra=(target,))
    in_blk, kv_blk, ffn_blk = w["a_w_in"].shape[2], w["w_kv"].shape[2], FFN_BLK

    dgu = _mm(dy, w["b_w_down"], "nt", "b_ffn_dgu", out_block=ffn_blk, epilogue="swiglu_bwd", extra=gu_b)
    dgu = dgu.reshape(N_DEV, t, ffn_blk)
    g["b_w_down"] = _mm(act_b, dy, "tn", "b_ffn_gdown", out_dtype=BF16)
    ride = scatter_on(["b_w_down"])
    dx3, g["b_ffn_norm_g"] = _mm(dgu, w["b_w_gu"], "nn", "b_ffn_dh", epilogue="rms_bwd",
                                 extra=(x3, s["b_ffn_norm_g"], dy), rider=ride[0])
    reduced(ride)
    g["b_w_gu"] = _mm(dgu, x3, "tn", "b_ffn_ggu", out_dtype=BF16, norm_g=s["b_ffn_norm_g"], norm_b=True)

    do_att = _mm(dx3, w["b_w_o"], "nt", "b_dout", out_dtype=BF16)
    g["b_w_o"] = _mm(o_att, dx3, "tn", "b_gout", out_dtype=BF16)
    ride = scatter_on(["b_w_gu", "b_w_o"])
    dq, dkp, dvp, db = _att_bwd(qn, kp, vp, bias, do_att, o_att, lse, "b_datt", rider=ride[0])
    reduced(ride)
    g["b_rel_bias"] = _rel_reduce(db, "drel").reshape(1, -1)
    dq_raw, gq = _q_dhnorm(q_raw, qg_t, bd, dq, q_scale, "q_dhnorm")
    g["b_q_norm_g"] = gq.reshape(ATT_HEADS, ATT_DH).sum(axis=0, keepdims=True)
    g["b_w_q"] = _mm(x2, dq_raw, "tn", "b_gq", out_dtype=BF16, norm_g=s["b_norm_g"])
    dx2, g["b_norm_g"] = _mm(dq_raw, w["b_w_q"], "nt", "b_dq", epilogue="rms_bwd",
                             extra=(x2, s["b_norm_g"], dx3))

    dkv, gk = _kv_dprep(kv, kg_t, bd, dkp, dvp, "kv_dprep")
    g["k_norm_g"] = gk.reshape(ATT_HEADS, ATT_DH).sum(axis=0, keepdims=True)
    g["w_kv"] = _mm(x2, dkv, "tn", "kv_g", out_dtype=BF16, out_block=kv_blk, norm_g=s["kv_norm_g"])
    dx2, g["kv_norm_g"] = _mm(dkv, w["w_kv"], "nt", "kv_du", epilogue="rms_bwd",
                              extra=(x2, s["kv_norm_g"], dx2))

    ride = scatter_on(["b_w_q"])
    dgu = _mm(dx2, w["a_w_down"], "nt", "a_ffn_dgu", out_block=ffn_blk, epilogue="swiglu_bwd", extra=gu_a,
              rider=ride[0])
    reduced(ride)
    dgu = dgu.reshape(N_DEV, t, ffn_blk)
    g["a_w_down"] = _mm(act_a, dx2, "tn", "a_ffn_gdown", out_dtype=BF16)
    ride = scatter_on(["a_w_down"])
    dx1, g["a_ffn_norm_g"] = _mm(dgu, w["a_w_gu"], "nn", "a_ffn_dh", epilogue="rms_bwd",
                                 extra=(x1, s["a_ffn_norm_g"], dx2), rider=ride[0])
    reduced(ride)
    ride = scatter_on(["w_kv"])
    g["a_w_gu"] = _mm(dgu, x1, "tn", "a_ffn_ggu", out_dtype=BF16, norm_g=s["a_ffn_norm_g"], norm_b=True,
                      rider=ride[0])
    reduced(ride)

    swap = _SiblingSwapRider([_blocks(g["a_w_gu"])])
    dy_ret = _mm(dx1, w["a_w_o"], "nt", "a_dout", rider=swap)
    g["a_w_o"] = _mm(y, dx1, "tn", "a_gout", out_dtype=BF16)
    chips = _ChipScatterRider([_pair_add(_blocks(g["a_w_gu"]), swap.results[0], parity, "rs_pair_add_gu")])
    dproj, g["a_gn_g"] = _ret_bwd(proj, s["a_gn_g"], o_ret, states, dy_ret, consts, "a_dret", rider=chips)
    recv["a_w_gu"] = chips.results[0]
    ride = scatter_on(["a_w_o"])
    g["a_w_in"] = _mm(x, dproj, "tn", "a_gin", out_dtype=BF16, out_block=in_blk, norm_g=s["a_norm_g"],
                      rider=ride[0])
    reduced(ride)
    from_sibling = _exchange(_SiblingSwapRider([g["a_w_in"]]), "rs_sibling")[0]
    chip_sums = _pair_add(g["a_w_in"], from_sibling, parity, "rs_pair_add")
    last = _ChipScatterRider([chip_sums])
    grad_x, g["a_norm_g"] = _mm(dproj, w["a_w_in"], "nt", "a_dproj", epilogue="rms_bwd",
                                extra=(x, s["a_norm_g"], dx1), rider=last)
    recv["a_w_in"] = last.results[0]
    return loss, grad_x, recv, g


ARG_NAMES = ("x", "a_norm_g", "a_w_in", "a_gn_g", "a_w_o", "a_ffn_norm_g", "a_w_gu", "a_w_down",
             "kv_norm_g", "w_kv", "k_norm_g", "b_norm_g", "b_w_q", "b_q_norm_g", "b_rel_bias", "b_w_o",
             "b_ffn_norm_g", "b_w_gu", "b_w_down")
WEIGHT_NAMES = ARG_NAMES[1:]


def _big_shard(a, name):
    a = a[0] if a.ndim == 3 else a
    return a.T if name in TRANSPOSED else a


def _as_given(a, name, shape):
    return (a.T if name in TRANSPOSED else a).reshape(shape)


def kernel(x, a_norm_g, a_w_in, a_gn_g, a_w_o, a_ffn_norm_g, a_w_gu, a_w_down, kv_norm_g, w_kv, k_norm_g, b_norm_g, b_w_q, b_q_norm_g, b_rel_bias, b_w_o, b_ffn_norm_g, b_w_gu, b_w_down, loss_target, m_a_norm_g, m_a_w_in, m_a_gn_g, m_a_w_o, m_a_ffn_norm_g, m_a_w_gu, m_a_w_down, m_kv_norm_g, m_w_kv, m_k_norm_g, m_b_norm_g, m_b_w_q, m_b_q_norm_g, m_b_rel_bias, m_b_w_o, m_b_ffn_norm_g, m_b_w_gu, m_b_w_down, v_a_norm_g, v_a_w_in, v_a_gn_g, v_a_w_o, v_a_ffn_norm_g, v_a_w_gu, v_a_w_down, v_kv_norm_g, v_w_kv, v_k_norm_g, v_b_norm_g, v_b_w_q, v_b_q_norm_g, v_b_rel_bias, v_b_w_o, v_b_ffn_norm_g, v_b_w_gu, v_b_w_down):
    args = (x, a_norm_g, a_w_in, a_gn_g, a_w_o, a_ffn_norm_g, a_w_gu, a_w_down, kv_norm_g, w_kv, k_norm_g,
            b_norm_g, b_w_q, b_q_norm_g, b_rel_bias, b_w_o, b_ffn_norm_g, b_w_gu, b_w_down)
    p = dict(zip(ARG_NAMES, args))
    m_all = dict(zip(WEIGHT_NAMES, (m_a_norm_g, m_a_w_in, m_a_gn_g, m_a_w_o, m_a_ffn_norm_g, m_a_w_gu,
                                    m_a_w_down, m_kv_norm_g, m_w_kv, m_k_norm_g, m_b_norm_g, m_b_w_q,
                                    m_b_q_norm_g, m_b_rel_bias, m_b_w_o, m_b_ffn_norm_g, m_b_w_gu, m_b_w_down)))
    v_all = dict(zip(WEIGHT_NAMES, (v_a_norm_g, v_a_w_in, v_a_gn_g, v_a_w_o, v_a_ffn_norm_g, v_a_w_gu,
                                    v_a_w_down, v_kv_norm_g, v_w_kv, v_k_norm_g, v_b_norm_g, v_b_w_q,
                                    v_b_q_norm_g, v_b_rel_bias, v_b_w_o, v_b_ffn_norm_g, v_b_w_gu, v_b_w_down)))
    xi, yi, ci = _my_place()
    me = 4 * xi + 2 * yi + ci
    big_names = [n for n, _ in BIG]

    big_local = {n: _big_shard(p[n], n) for n in big_names}
    shards = {n: a.astype(BF16) for n, a in big_local.items()}
    small_local = _pack_small({n: p[n] for n, _, _ in SMALL})
    small_all = _exchange(_GatherRider([small_local]), "gather_small")[0]
    flat_g = small_all.reshape(N_DEV, -1)
    s_full, pos = {}, 0
    for n, length, sharded in SMALL:
        ln = length // N_DEV if sharded else length
        s_full[n] = flat_g[:, pos:pos + ln].reshape(1, -1) if sharded else p[n].reshape(1, -1)
        pos += ln

    parity = jnp.reshape(ci, (1,)).astype(jnp.int32)
    loss, grad_x, recv, g = _local_step(x[0], loss_target[0], shards, s_full, parity)

    partial = _pack_small({n: g[n] for n, _, _ in SMALL}, last=loss)
    summed = _sum_leading(_exchange(_GatherRider([partial]), "gather_gsmall")[0], "gsmall_sum")
    loss = summed[SMALL_ROWS - 1, SMALL_COLS - 1]
    g_small = _unpack_small(summed, local=False)
    for n, length, sharded in SMALL:
        if sharded:
            g_small[n] = lax.dynamic_slice(g_small[n], (me * (length // N_DEV),), (length // N_DEV,))

    grads, deltas, new_m, new_v = {}, {}, {}, {}
    for n in big_names:
        outs = _adamw(big_local[n], recv[n], _big_shard(m_all[n], n), _big_shard(v_all[n], n), "adamw_" + n)
        grads[n], deltas[n], new_m[n], new_v[n] = (_as_given(a, n, p[n].shape) for a in outs)
    pk = lambda src: _pack_small({n: src[n] for n, _, _ in SMALL})
    outs = _adamw(small_local, pk(g_small)[None], pk(m_all), pk(v_all), "adamw_small")
    g_s, d_s, nm_s, nv_s = (_unpack_small(a, local=True) for a in outs)
    for n, _, _ in SMALL:
        grads[n], deltas[n], new_m[n], new_v[n] = (a[n].reshape(p[n].shape) for a in (g_s, d_s, nm_s, nv_s))

    return (loss, grad_x[None], *[grads[n] for n in WEIGHT_NAMES], *[deltas[n] for n in WEIGHT_NAMES],
            *[new_m[n] for n in WEIGHT_NAMES], *[new_v[n] for n in WEIGHT_NAMES])
```

```python
import numpy as np
import jax
import jax.numpy as jnp
from jax import lax
from jax.experimental import pallas as pl
from jax.experimental.pallas import tpu as pltpu

F32 = jnp.float32
BF16 = jnp.bfloat16

N_DEV = 8
D_MODEL = 1024
CHUNK = 64
EPS = 1e-6
RET_HEADS, RET_DK, RET_DV = 4, 256, 512
RET_STEP = 4
RET_Q_COLS = RET_HEADS * RET_DK
RET_V_COLS = RET_HEADS * RET_DV
ATT_HEADS, ATT_DH = 16, 64
PAST_CHUNKS = 8
REL_CLIP = 256
REL_TABLE = 2 * REL_CLIP + 1
FFN_HIDDEN = 2816
ROPE_BASE = 10000.0
LANES = 128
Q_BLOCK = 256
ATT_SUBS = 4
ATT_ROWS = 32
K_PAD = PAST_CHUNKS * CHUNK
K_WINDOW = Q_BLOCK + K_PAD
REL_BLK = 128
REL_DELTAS = Q_BLOCK // REL_BLK + K_WINDOW // REL_BLK - 1
REL_PAD = 640
NEG = -1e30
VMEM_LIMIT_V7X = 56 * 1024 * 1024
ADAM_LR, ADAM_B1, ADAM_B2, ADAM_EPS, ADAM_WD, ADAM_STEP = 1e-3, 0.9, 0.999, 1e-8, 0.01, 10
MESH = pl.DeviceIdType.MESH
ANY = pl.BlockSpec(memory_space=pl.ANY)


def _params(*semantics):
    return pltpu.CompilerParams(dimension_semantics=semantics, vmem_limit_bytes=VMEM_LIMIT_V7X)


def _pick(dim, cap, align):
    best = None
    for t in range(align, min(dim, cap) + 1, align):
        if dim % t == 0:
            best = t
    assert best is not None, (dim, cap, align)
    return best


def _dot(a, b):
    return lax.dot_general(a, b, (((1,), (0,)), ((), ())), preferred_element_type=F32)


def _dot_nt(a, b):
    return lax.dot_general(a, b, (((1,), (1,)), ((), ())), preferred_element_type=F32)


def _dot_tn(a, b):
    return lax.dot_general(a, b, (((0,), (0,)), ((), ())), preferred_element_type=F32)


def _split2(x):
    hi = x.astype(BF16)
    lo = (x - hi.astype(F32)).astype(BF16)
    return hi, lo


def _split3(x):
    hi = x.astype(BF16)
    r = x - hi.astype(F32)
    mid = r.astype(BF16)
    lo = (r - mid.astype(F32)).astype(BF16)
    return hi, mid, lo


def _sigmoid(x):
    return 1.0 / (1.0 + jnp.exp(-x))


def _accumulate(ref, part, step):
    @pl.when(step == 0)
    def _():
        ref[...] = part

    @pl.when(step > 0)
    def _():
        ref[...] += part


RELAY_AT_NUM, RELAY_AT_DEN = 3, 4


def _my_place():
    return lax.axis_index("x"), lax.axis_index("y"), lax.axis_index("c")


def _flip(v, bit):
    return 1 - v if bit else v


class _NoRelay:
    def relay(self, in_refs, out_refs, sems):
        pass


class _GatherRider:
    def __init__(self, xs):
        self.inputs = list(xs)
        n = len(xs)
        self.out_shape = [jax.ShapeDtypeStruct((N_DEV,) + x.shape, x.dtype) for x in xs]
        self.scratch = [pltpu.SemaphoreType.DMA((7, n)), pltpu.SemaphoreType.DMA((7, n)),
                        pltpu.SemaphoreType.DMA((n,))]
        self.results = None

    def _copies(self, x_refs, out_refs, sems):
        send_sems, recv_sems, local_sems = sems
        n = len(x_refs)
        x, y, c = _my_place()
        me, sibling = (x, y, c), (x, y, 1 - c)
        chips = [(1 - x, y), (x, 1 - y), (1 - x, 1 - y)]

        def slot(a, px, py, pc):
            return out_refs[a].at[4 * px + 2 * py + pc]

        def copy(k, a, block, to, own=False):
            return pltpu.make_async_remote_copy(
                src_ref=x_refs[a] if own else slot(a, *block), dst_ref=slot(a, *block),
                send_sem=send_sems.at[k, a], recv_sem=recv_sems.at[k, a],
                device_id=to, device_id_type=MESH)

        mine = [pltpu.make_async_copy(x_refs[a], slot(a, *me), local_sems.at[a]) for a in range(n)]
        first = []
        for a in range(n):
            first.append(copy(0, a, me, sibling, own=True))
            first += [copy(1 + j, a, me, (*chip, c), own=True) for j, chip in enumerate(chips)]
        return n, c, me, sibling, chips, copy, mine, first

    def start(self, x_refs, out_refs, sems):
        _, _, _, _, _, _, mine, first = self._copies(x_refs, out_refs, sems)
        for cp in mine + first:
            cp.start()

    def relay(self, x_refs, out_refs, sems):
        n, c, me, sibling, chips, copy, _, _ = self._copies(x_refs, out_refs, sems)
        for j, chip in enumerate(chips):
            for a in range(n):
                copy(1 + j, a, (*chip, c), me).wait_recv()
                copy(4 + j, a, (*chip, c), sibling).start()

    def finish(self, x_refs, out_refs, sems):
        n, c, me, sibling, chips, copy, mine, first = self._copies(x_refs, out_refs, sems)
        passed = [copy(4 + j, a, (*chip, c), sibling) for j, chip in enumerate(chips) for a in range(n)]
        for a in range(n):
            copy(0, a, sibling, me).wait_recv()
            for j, chip in enumerate(chips):
                copy(4 + j, a, (*chip, 1 - c), me).wait_recv()
        for cp in first + passed:
            cp.wait_send()
        for cp in mine:
            cp.wait()


class _ScatterRider(_NoRelay):
    def __init__(self, gs):
        self.inputs = list(gs)
        n = len(gs)
        self.out_shape = [jax.ShapeDtypeStruct(g.shape, g.dtype) for g in gs]
        self.scratch = [pltpu.SemaphoreType.DMA((7, n)), pltpu.SemaphoreType.DMA((7, n)),
                        pltpu.SemaphoreType.DMA((n,))]
        self.results = None

    def _copies(self, g_refs, out_refs, sems):
        send_sems, recv_sems, local_sems = sems
        x, y, c = _my_place()
        me = 4 * x + 2 * y + c
        mine, copies = [], []
        for a in range(len(g_refs)):
            mine.append(pltpu.make_async_copy(g_refs[a].at[me], out_refs[a].at[me], local_sems.at[a]))
            for k in range(1, N_DEV):
                px, py, pc = _flip(x, k & 4), _flip(y, k & 2), _flip(c, k & 1)
                copies.append(pltpu.make_async_remote_copy(
                    src_ref=g_refs[a].at[4 * px + 2 * py + pc], dst_ref=out_refs[a].at[me],
                    send_sem=send_sems.at[k - 1, a], recv_sem=recv_sems.at[k - 1, a],
                    device_id=(px, py, pc), device_id_type=MESH))
        return mine, copies

    def start(self, g_refs, out_refs, sems):
        mine, copies = self._copies(g_refs, out_refs, sems)
        for cp in mine + copies:
            cp.start()

    def finish(self, g_refs, out_refs, sems):
        mine, copies = self._copies(g_refs, out_refs, sems)
        for cp in copies + mine:
            cp.wait()


class _SiblingSwapRider(_NoRelay):
    def __init__(self, gs):
        self.inputs = list(gs)
        n = len(gs)
        self.out_shape = [jax.ShapeDtypeStruct((4,) + g.shape[1:], g.dtype) for g in gs]
        self.scratch = [pltpu.SemaphoreType.DMA((4, n)), pltpu.SemaphoreType.DMA((4, n))]
        self.results = None

    def _copies(self, g_refs, out_refs, sems):
        send_sems, recv_sems = sems
        x, y, c = _my_place()
        return [pltpu.make_async_remote_copy(
            src_ref=g_refs[a].at[2 * k + 1 - c], dst_ref=out_refs[a].at[k],
            send_sem=send_sems.at[k, a], recv_sem=recv_sems.at[k, a],
            device_id=(x, y, 1 - c), device_id_type=MESH)
            for a in range(len(g_refs)) for k in range(4)]

    def start(self, g_refs, out_refs, sems):
        for cp in self._copies(g_refs, out_refs, sems):
            cp.start()

    def finish(self, g_refs, out_refs, sems):
        for cp in self._copies(g_refs, out_refs, sems):
            cp.wait()


class _ChipScatterRider(_NoRelay):
    def __init__(self, ps):
        self.inputs = list(ps)
        n = len(ps)
        self.out_shape = [jax.ShapeDtypeStruct(p.shape, p.dtype) for p in ps]
        self.scratch = [pltpu.SemaphoreType.DMA((3, n)), pltpu.SemaphoreType.DMA((3, n)),
                        pltpu.SemaphoreType.DMA((n,))]
        self.results = None

    def _copies(self, p_refs, out_refs, sems):
        send_sems, recv_sems, local_sems = sems
        x, y, c = _my_place()
        my_chip = 2 * x + y
        chips = [(1 - x, y), (x, 1 - y), (1 - x, 1 - y)]
        n = len(p_refs)
        mine = [pltpu.make_async_copy(p_refs[a].at[my_chip], out_refs[a].at[my_chip], local_sems.at[a])
                for a in range(n)]
        copies = [pltpu.make_async_remote_copy(
            src_ref=p_refs[a].at[2 * cx + cy], dst_ref=out_refs[a].at[my_chip],
            send_sem=send_sems.at[j, a], recv_sem=recv_sems.at[j, a],
            device_id=(cx, cy, c), device_id_type=MESH)
            for a in range(n) for j, (cx, cy) in enumerate(chips)]
        return mine, copies

    def start(self, p_refs, out_refs, sems):
        mine, copies = self._copies(p_refs, out_refs, sems)
        for cp in mine + copies:
            cp.start()

    def finish(self, p_refs, out_refs, sems):
        mine, copies = self._copies(p_refs, out_refs, sems)
        for cp in copies + mine:
            cp.wait()


def _call(body, name, grid, in_specs, out_specs, out_shape, scratch, semantics, args, rider=None):
    in_specs, out_specs, out_shape, scratch = list(in_specs), list(out_specs), list(out_shape), list(scratch)
    if rider is None:
        return list(pl.pallas_call(
            body, name=name, grid=grid, in_specs=in_specs, out_specs=out_specs, out_shape=out_shape,
            scratch_shapes=scratch, compiler_params=_params(*semantics))(*args))
    n_in, n_out, n_scr = len(in_specs), len(out_specs), len(scratch)
    r_in, r_out = len(rider.inputs), len(rider.out_shape)

    def wrapped(*refs):
        cuts = np.cumsum([0, n_in, r_in, n_out, r_out, n_scr])
        hi, ri, ho, ro, hs = (refs[cuts[i]:cuts[i + 1]] for i in range(5))
        rs = refs[cuts[5]:]
        step, steps = pl.program_id(0), grid[0]
        for d in range(1, len(grid)):
            step, steps = step * grid[d] + pl.program_id(d), steps * grid[d]

        @pl.when(step == 0)
        def _():
            rider.start(ri, ro, rs)

        body(*hi, *ho, *hs)

        @pl.when(step == (steps * RELAY_AT_NUM) // RELAY_AT_DEN)
        def _():
            rider.relay(ri, ro, rs)

        @pl.when(step == steps - 1)
        def _():
            rider.finish(ri, ro, rs)

    outs = pl.pallas_call(
        wrapped, name=name, grid=grid,
        in_specs=in_specs + [ANY] * r_in, out_specs=out_specs + [ANY] * r_out,
        out_shape=out_shape + rider.out_shape, scratch_shapes=scratch + rider.scratch,
        compiler_params=_params(*(["arbitrary"] * len(grid))),
    )(*args, *rider.inputs)
    rider.results = list(outs[n_out:])
    return list(outs[:n_out])


_WALK = ((None, None), (0, None), (1, 4), (2, 5), (4, None), (5, None), (3, 6), (6, None))


def _gather_order():
    x, y, c = _my_place()
    (ax, ay), (bx, by), (dx, dy) = (1 - x, y), (x, 1 - y), (1 - x, 1 - y)
    ids = [(x, y, c), (x, y, 1 - c), (ax, ay, c), (bx, by, c), (ax, ay, 1 - c), (bx, by, 1 - c),
           (dx, dy, c), (dx, dy, 1 - c)]
    return jnp.stack([4 * px + 2 * py + pc for px, py, pc in ids]).astype(jnp.int32)


def _proj_gather(x, norm_g, w_shard, extras, name):
    t, d = x.shape
    cols = w_shard.shape[1]
    tm = _pick(t, MM_CAP_MN, 16)
    ni = t // tm
    n = 1 + len(extras)
    rider = _GatherRider([w_shard] + list(extras))

    def body(ord_ref, x_ref, g_ref, *refs):
        sh_refs, proj_ref, gathered = refs[:n], refs[n], refs[n + 1:2 * n + 1]
        h_all, bbuf, bsem, send_sems, recv_sems, local_sems = refs[2 * n + 1:]
        j, i = pl.program_id(0), pl.program_id(1)
        _, c, me, sibling, chips, copy, mine, first = rider._copies(
            sh_refs, gathered, (send_sems, recv_sems, local_sems))
        rows = pl.ds(pl.multiple_of(i * tm, tm), tm)

        def load(step, src):
            return pltpu.make_async_copy(src, bbuf.at[step % 2], bsem.at[step % 2])

        def relayed(k, a):
            return copy(k, a, (*chips[k - 4], c), sibling)

        @pl.when(jnp.logical_and(j == 0, i == 0))
        def _():
            for cp in mine + first:
                cp.start()
            load(0, sh_refs[0]).start()

        @pl.when(i == 0)
        def _():
            load(j, sh_refs[0]).wait()

        @pl.when(j == 0)
        def _():
            groups = []
            for r in range(0, tm, NORM_ROWS):
                xv = x_ref[r:r + NORM_ROWS, :]
                rstd = lax.rsqrt(jnp.mean(xv * xv, axis=-1, keepdims=True) + EPS)
                groups.append((xv * rstd * g_ref[...]).astype(BF16))
            h_all[rows, :] = jnp.concatenate(groups, axis=0)

        proj_ref[...] = _dot(h_all[rows, :], bbuf[j % 2])

        for step in range(N_DEV - 1):
            @pl.when(jnp.logical_and(j == step, i == max(ni - 2, 0)))
            def _(step=step):
                need, relay = _WALK[step + 1]
                copy(need, 0, me, me).wait_recv()
                if relay is not None:
                    relayed(relay, 0).start()
                load(step + 1, gathered[0].at[ord_ref[step + 1]]).start()

        @pl.when(jnp.logical_and(j == N_DEV - 1, i == ni - 1))
        def _():
            for a in range(1, n):
                for k in range(3):
                    copy(1 + k, a, me, me).wait_recv()
                    relayed(4 + k, a).start()
            for a in range(1, n):
                for k in (0, 4, 5, 6):
                    copy(k, a, me, me).wait_recv()
            for cp in first + [relayed(4 + k, a) for a in range(n) for k in range(3)]:
                cp.wait_send()
            for cp in mine:
                cp.wait()

    outs = pl.pallas_call(
        body, name=name,
        grid_spec=pltpu.PrefetchScalarGridSpec(
            num_scalar_prefetch=1, grid=(N_DEV, ni),
            in_specs=[pl.BlockSpec((tm, d), lambda j, i, o: (jnp.where(j == 0, i, ni - 1), 0)),
                      pl.BlockSpec((1, d), lambda j, i, o: (0, 0))] + [ANY] * n,
            out_specs=[pl.BlockSpec((tm, cols), lambda j, i, o: (i, o[j]))] + [ANY] * n,
            scratch_shapes=[pltpu.VMEM((t, d), BF16), pltpu.VMEM((2, d, cols), BF16),
                            pltpu.SemaphoreType.DMA((2,))] + rider.scratch),
        out_shape=[jax.ShapeDtypeStruct((t, N_DEV * cols), F32)] + rider.out_shape,
        compiler_params=_params("arbitrary", "arbitrary"),
    )(_gather_order(), x, norm_g, w_shard, *extras)
    return outs[0], list(outs[1:])


def _exchange(rider, name):
    r_in, r_out = len(rider.inputs), len(rider.out_shape)

    def body(*refs):
        ri, ro, rs = refs[:r_in], refs[r_in:r_in + r_out], refs[r_in + r_out:]
        rider.start(ri, ro, rs)
        rider.relay(ri, ro, rs)
        rider.finish(ri, ro, rs)

    return list(pl.pallas_call(
        body, name=name, in_specs=[ANY] * r_in, out_specs=[ANY] * r_out,
        out_shape=rider.out_shape, scratch_shapes=rider.scratch)(*rider.inputs))


MM_CAP_MN = 1024
MM_CAP_M_GRAD = 1408
MM_CAP_N = 1536
MM_CAP_K = 3072
MM_CAP_K_TOKENS = 2048
MM_CAP_K_RMS = 8192
MM_CAP_M_RMS = 512
NORM_ROWS = 256
GU_RING = 3


def _mm(a, b, mode, name, out_dtype=F32, res=None, out_block=None, epilogue=None, extra=None, norm_g=None,
        norm_b=False, rider=None):
    a3, b3 = a.ndim == 3, b.ndim == 3
    um = un = uk = None
    if mode in ("nn", "nt"):
        if a3:
            m, uk = a.shape[1:]
            k = a.shape[0] * uk
        else:
            m, k = a.shape
    else:
        if a3:
            k, um = a.shape[1:]
            m = a.shape[0] * um
        else:
            k, m = a.shape
    if mode in ("nn", "tn"):
        if b3:
            kb, un = b.shape[1:]
            n = b.shape[0] * un
        else:
            kb, n = b.shape
        assert kb == k, (a.shape, b.shape, mode)
    else:
        if b3:
            n, ukb = b.shape[1:]
            assert b.shape[0] * ukb == k and uk in (None, ukb), (a.shape, b.shape, mode)
            uk = ukb
        else:
            n, kb = b.shape
            assert kb == k, (a.shape, b.shape, mode)
    if out_block is not None:
        assert un in (None, out_block)
        un = out_block

    def tile(dim, unit, cap, align):
        if unit is None:
            return _pick(dim, cap, align), 1
        c = max(1, cap // unit)
        while (dim // unit) % c:
            c -= 1
        return unit, c

    cap_m = MM_CAP_M_GRAD if mode == "tn" else (MM_CAP_M_RMS if epilogue == "rms_bwd" else MM_CAP_MN)
    um, cm = tile(m, um, cap_m, 128 if mode == "tn" else 16)
    un, cn = tile(n, un, MM_CAP_N, 128)
    cap_k = MM_CAP_K_TOKENS if mode == "tn" else (MM_CAP_K_RMS if epilogue == "rms_bwd" else MM_CAP_K)
    uk, ck = tile(k, uk, cap_k, 128)
    if epilogue == "rms_bwd":
        assert mode != "tn" and n == D_MODEL and cm == cn == 1 and res is None and out_block is None
    if epilogue == "loss":
        assert n == D_MODEL and cm == cn == 1 and res is not None and out_block is None
    if norm_g is not None and norm_b:
        assert mode == "tn" and not b3 and n == D_MODEL and cn == 1
    elif norm_g is not None:
        assert not a3 and (m if mode == "tn" else k) == D_MODEL and (cm if mode == "tn" else ck) == 1
    if epilogue == "swiglu":
        assert res is None and ((mode == "nn" and b3 and out_block is None) or
                                (mode == "nt" and not b3 and out_block is not None))
        cn = 2
    if epilogue == "swiglu_bwd":
        assert mode == "nt" and out_block is not None and extra is not None and res is None
        cn = 1
    tm, tn, tk = cm * um, cn * un, ck * uk
    nk = k // tk
    ringed = epilogue == "swiglu_bwd"
    assert not ringed or nk == 1
    dot = {"nn": _dot, "nt": _dot_nt, "tn": _dot_tn}[mode]
    half = n // un // 2
    blocked_out = out_block is not None or epilogue in ("swiglu", "swiglu_bwd")
    extras = [] if extra is None else (list(extra) if isinstance(extra, (tuple, list)) else [extra])

    def sl(idx, unit, count):
        return slice(None) if count == 1 else slice(idx * unit, (idx + 1) * unit)

    def body(*refs):
        a_ref, b_ref = refs[0], refs[1]
        pos = 2
        r_ref = ng_ref = None
        if res is not None:
            r_ref, pos = refs[pos], pos + 1
        e_refs, pos = refs[pos:pos + len(extras)], pos + len(extras)
        if norm_g is not None:
            ng_ref, pos = refs[pos], pos + 1
        kk = pl.program_id(2)
        if ringed:
            outs, acc_ref, ring_ref, ring_sem = refs[pos:-3], refs[-3], refs[-2], refs[-1]
            nj = n // tn
            step, steps = pl.program_id(0) * nj + pl.program_id(1), (m // tm) * nj

            def gu_copies(s):
                row = s // nj * tm
                row = row if isinstance(row, int) else pl.multiple_of(row, tm)
                return [pltpu.make_async_copy(e_refs[0].at[h, s % nj, pl.ds(row, tm), :],
                                              ring_ref.at[s % GU_RING, h], ring_sem.at[s % GU_RING, h])
                        for h in range(2)]

            @pl.when(step == 0)
            def _():
                for s in range(min(GU_RING - 1, steps)):
                    for cp in gu_copies(s):
                        cp.start()

            @pl.when(step + (GU_RING - 1) < steps)
            def _():
                for cp in gu_copies(step + (GU_RING - 1)):
                    cp.start()

            for cp in gu_copies(step):
                cp.wait()

        else:
            outs, acc_ref = refs[pos:-1], refs[-1]

        def normed(x_ref):
            groups = []
            for r in range(0, x_ref.shape[0], NORM_ROWS):
                xv = x_ref[r:r + NORM_ROWS, :]
                rstd = lax.rsqrt(jnp.mean(xv * xv, axis=-1, keepdims=True) + EPS)
                groups.append((xv * rstd * ng_ref[...]).astype(BF16))
            return jnp.concatenate(groups, axis=0)

        def a_blk(mi, ki):
            if norm_g is not None and not norm_b:
                return normed(a_ref)
            if mode in ("nn", "nt"):
                return a_ref[ki] if a3 else a_ref[:, sl(ki, uk, ck)]
            return a_ref[mi] if a3 else a_ref[:, sl(mi, um, cm)]

        def b_blk(ki, ni):
            if norm_b:
                return normed(b_ref)
            if epilogue == "swiglu":
                return b_ref[ni, 0]
            if mode in ("nn", "tn"):
                return b_ref[ni] if b3 else b_ref[sl(ki, uk, ck), sl(ni, un, cn)]
            return b_ref[ki][sl(ni, un, cn), :] if b3 else b_ref[sl(ni, un, cn), sl(ki, uk, ck)]

        parts = {}
        for mi in range(cm):
            for ni in range(cn):
                part = None
                for ki in range(ck):
                    d = dot(a_blk(mi, ki).astype(BF16), b_blk(ki, ni).astype(BF16))
                    part = d if part is None else part + d
                parts[mi, ni] = part

        def finish(total):
            if epilogue == "swiglu":
                gate, up = total[0, 0], total[0, 1]
                outs[0][0, 0] = gate.astype(BF16)
                outs[0][1, 0] = up.astype(BF16)
                outs[1][0] = (gate * _sigmoid(gate) * up).astype(BF16)
                return
            if epilogue == "swiglu_bwd":
                dact = total[0, 0]
                gate, up = (ring_ref[step % GU_RING, h].astype(F32) for h in range(2))
                sg = _sigmoid(gate)
                outs[0][0, 0] = (dact * up * (sg * (1.0 + gate * (1.0 - sg)))).astype(BF16)
                outs[0][1, 0] = (dact * (gate * sg)).astype(BF16)
                return
            if epilogue == "rms_bwd":
                x_ref, g_ref, dres_ref = e_refs
                dh, dg = total[0, 0], None
                for r in range(0, tm, NORM_ROWS):
                    rows = slice(r, r + NORM_ROWS)
                    xv, dhv = x_ref[rows, :], dh[rows, :]
                    rstd = lax.rsqrt(jnp.mean(xv * xv, axis=-1, keepdims=True) + EPS)
                    xh = xv * rstd
                    dyg = dhv * g_ref[...]
                    c = jnp.mean(dyg * xh, axis=-1, keepdims=True)
                    outs[0][rows, :] = dres_ref[rows, :] + rstd * (dyg - xh * c)
                    part = jnp.sum(dhv * xh, axis=0, keepdims=True)
                    dg = part if dg is None else dg + part
                _accumulate(outs[1], dg, pl.program_id(0))
                return
            if epilogue == "loss":
                diff = r_ref[...] + total[0, 0] - e_refs[0][...]
                outs[0][...] = diff * (1.0 / n)
                sq = jnp.sum(jnp.sum(diff * diff, axis=-1, keepdims=True), axis=0, keepdims=True)
                _accumulate(outs[1], sq * (0.5 / n), pl.program_id(0))
                return
            for (mi, ni), val in total.items():
                rows, cols = sl(mi, um, cm), sl(ni, un, cn)
                if res is not None:
                    val = r_ref[rows, cols] + val
                if blocked_out:
                    outs[0][ni, rows] = val.astype(out_dtype)
                else:
                    outs[0][rows, cols] = val.astype(out_dtype)

        if nk == 1:
            finish(parts)
        else:
            @pl.when(kk == 0)
            def _():
                for (mi, ni), val in parts.items():
                    acc_ref[mi * cn + ni] = val

            @pl.when(jnp.logical_and(kk > 0, kk < nk - 1))
            def _():
                for (mi, ni), val in parts.items():
                    acc_ref[mi * cn + ni] += val

            @pl.when(kk == nk - 1)
            def _():
                finish({key: acc_ref[key[0] * cn + key[1]] + val for key, val in parts.items()})

    if mode in ("nn", "nt"):
        a_spec = (pl.BlockSpec((ck, tm, uk), lambda i, j, kk: (kk, i, 0)) if a3
                  else pl.BlockSpec((tm, tk), lambda i, j, kk: (i, kk)))
    else:
        a_spec = (pl.BlockSpec((cm, tk, um), lambda i, j, kk: (i, kk, 0)) if a3
                  else pl.BlockSpec((tk, tm), lambda i, j, kk: (kk, i)))
    pair_spec = pl.BlockSpec((2, 1, tm, un), lambda i, j, kk: (0, j, i, 0))
    row_spec = pl.BlockSpec((tm, tn), lambda i, j, kk: (i, 0))
    vec_spec = pl.BlockSpec((1, tn), lambda i, j, kk: (0, 0))
    if epilogue == "swiglu" and mode == "nn":
        b = b.reshape(2, half, k, un)
        b_spec = pl.BlockSpec((2, 1, tk, un), lambda i, j, kk: (0, j, kk, 0))
    elif epilogue == "swiglu":
        b = b.reshape(2, half, un, k)
        b_spec = pl.BlockSpec((2, 1, un, tk), lambda i, j, kk: (0, j, 0, kk))
    elif mode in ("nn", "tn"):
        b_spec = (pl.BlockSpec((cn, tk, un), lambda i, j, kk: (j, kk, 0)) if b3
                  else pl.BlockSpec((tk, tn), lambda i, j, kk: (kk, j)))
    else:
        b_spec = (pl.BlockSpec((ck, tn, uk), lambda i, j, kk: (kk, j, 0)) if b3
                  else pl.BlockSpec((tn, tk), lambda i, j, kk: (j, kk)))
    if epilogue == "swiglu":
        out_specs = [pair_spec, pl.BlockSpec((1, tm, un), lambda i, j, kk: (j, i, 0))]
        out_shape = [jax.ShapeDtypeStruct((2, half, m, un), BF16), jax.ShapeDtypeStruct((half, m, un), BF16)]
    elif epilogue == "swiglu_bwd":
        out_specs = [pair_spec]
        out_shape = [jax.ShapeDtypeStruct(extra.shape, BF16)]
    elif epilogue == "rms_bwd":
        out_specs = [row_spec, vec_spec]
        out_shape = [jax.ShapeDtypeStruct((m, n), F32), jax.ShapeDtypeStruct((1, n), F32)]
    elif epilogue == "loss":
        out_specs = [row_spec, pl.BlockSpec((1, 1), lambda i, j, kk: (0, 0))]
        out_shape = [jax.ShapeDtypeStruct((m, n), F32), jax.ShapeDtypeStruct((1, 1), F32)]
    elif blocked_out:
        out_specs = [pl.BlockSpec((cn, tm, un), lambda i, j, kk: (j, i, 0))]
        out_shape = [jax.ShapeDtypeStruct((n // un, m, un), out_dtype)]
    else:
        out_specs = [pl.BlockSpec((tm, tn), lambda i, j, kk: (i, j))]
        out_shape = [jax.ShapeDtypeStruct((m, n), out_dtype)]
    in_specs, args = [a_spec, b_spec], [a, b]
    if res is not None:
        in_specs.append(pl.BlockSpec((tm, tn), lambda i, j, kk: (i, j)))
        args.append(res)
    if ringed:
        in_specs.append(ANY)
    elif epilogue == "rms_bwd":
        in_specs += [row_spec, vec_spec, row_spec]
    elif epilogue == "loss":
        in_specs.append(row_spec)
    args += extras
    if norm_g is not None:
        in_specs.append(pl.BlockSpec((1, D_MODEL), lambda i, j, kk: (0, 0)))
        args.append(norm_g)
    ordered = ringed or epilogue in ("rms_bwd", "loss")
    semantics = ("arbitrary",) * 3 if ordered else ("parallel", "parallel", "arbitrary")
    scratch = [pltpu.VMEM((cm * cn, um, un), F32)]
    if ringed:
        scratch += [pltpu.VMEM((GU_RING, 2, tm, un), BF16), pltpu.SemaphoreType.DMA((GU_RING, 2))]
    out = _call(body, name, (m // tm, n // tn, nk), in_specs, out_specs, out_shape, scratch, semantics, args, rider)
    return out if epilogue in ("swiglu", "rms_bwd", "loss") else out[0]


def _head_sums(v, ind):
    return _dot(v.astype(BF16), ind)


def _head_spread(per_head, ind):
    hi, lo = _split2(per_head)
    return _dot_nt(hi, ind) + _dot_nt(lo, ind)


def _head_rstd(xv, ind):
    return _head_spread(lax.rsqrt(_head_sums(xv * xv, ind) * (1.0 / ATT_DH) + EPS), ind)


def _hn_bwd_math(xv, gv, ind, dyv, scale):
    rstd = _head_rstd(xv, ind)
    xh = xv * rstd
    dyn = dyv * scale
    dyg = dyn * gv
    dx = rstd * (dyg - xh * _head_spread(_head_sums(dyg * xh, ind) * (1.0 / ATT_DH), ind))
    return dx, jnp.sum(dyn * xh, axis=0, keepdims=True)


def _q_hnorm(x, g_tiled, bd, scale, name):
    t, d = x.shape
    tm = _pick(t, 512, 16)

    def body(x_ref, g_ref, bd_ref, o_ref):
        xv = x_ref[...]
        o_ref[...] = (xv * _head_rstd(xv, bd_ref[...]) * g_ref[...] * scale).astype(BF16)

    return pl.pallas_call(
        body, name=name, grid=(t // tm,),
        in_specs=[pl.BlockSpec((tm, d), lambda i: (i, 0)), pl.BlockSpec((1, d), lambda i: (0, 0)),
                  pl.BlockSpec((d, LANES), lambda i: (0, 0))],
        out_specs=pl.BlockSpec((tm, d), lambda i: (i, 0)),
        out_shape=jax.ShapeDtypeStruct((t, d), BF16),
        compiler_params=_params("parallel"),
    )(x, g_tiled, bd)


def _q_dhnorm(x, g_tiled, bd, dy, scale, name):
    t, d = x.shape
    tm = _pick(t, 512, 16)

    def body(x_ref, g_ref, bd_ref, dy_ref, dx_ref, dg_ref):
        dx, part = _hn_bwd_math(x_ref[...], g_ref[...], bd_ref[...], dy_ref[...], scale)
        dx_ref[...] = dx.astype(BF16)
        _accumulate(dg_ref, part, pl.program_id(0))

    row = pl.BlockSpec((tm, d), lambda i: (i, 0))
    vec = pl.BlockSpec((1, d), lambda i: (0, 0))
    return pl.pallas_call(
        body, name=name, grid=(t // tm,),
        in_specs=[row, vec, pl.BlockSpec((d, LANES), lambda i: (0, 0)), row],
        out_specs=[row, vec],
        out_shape=[jax.ShapeDtypeStruct((t, d), BF16), jax.ShapeDtypeStruct((1, d), F32)],
        compiler_params=_params("arbitrary"),
    )(x, g_tiled, bd, dy)


def _kv_prep(kv, g_tiled, bd, name):
    t = kv.shape[0]
    d = D_MODEL
    tm = K_PAD
    assert t % tm == 0

    def body(k_ref, v_ref, g_ref, bd_ref, kp_ref, vp_ref):
        i = pl.program_id(0)

        @pl.when(i == 0)
        def _():
            kp_ref[...] = jnp.zeros_like(kp_ref)
            vp_ref[...] = jnp.zeros_like(vp_ref)

        @pl.when(i > 0)
        def _():
            xv = k_ref[...]
            kp_ref[...] = (xv * _head_rstd(xv, bd_ref[...]) * g_ref[...]).astype(BF16)
            vp_ref[...] = v_ref[...].astype(BF16)

    shp = jax.ShapeDtypeStruct((t + K_PAD, d), BF16)
    out = pl.BlockSpec((tm, d), lambda i: (i, 0))
    return pl.pallas_call(
        body, name=name, grid=(t // tm + 1,),
        in_specs=[pl.BlockSpec((tm, d), lambda i: (jnp.maximum(i - 1, 0), 0)),
                  pl.BlockSpec((tm, d), lambda i: (jnp.maximum(i - 1, 0), 1)),
                  pl.BlockSpec((1, d), lambda i: (0, 0)), pl.BlockSpec((d, LANES), lambda i: (0, 0))],
        out_specs=[out, out], out_shape=[shp, shp],
        compiler_params=_params("arbitrary"),
    )(kv, kv, g_tiled, bd)


def _kv_dprep(kv, g_tiled, bd, dkp_t, dvp_t, name):
    t = kv.shape[0]
    d = D_MODEL
    tm = K_PAD

    def body(k_ref, g_ref, bd_ref, dk_ref, dv_ref, o_ref, dg_ref):
        dx, part = _hn_bwd_math(k_ref[...], g_ref[...], bd_ref[...], dk_ref[...].T, 1.0)
        o_ref[:, :d] = dx.astype(BF16)
        o_ref[:, d:] = dv_ref[...].T.astype(BF16)
        _accumulate(dg_ref, part, pl.program_id(0))

    vec = pl.BlockSpec((1, d), lambda i: (0, 0))
    padded = pl.BlockSpec((d, tm), lambda i: (0, i + 1))
    return pl.pallas_call(
        body, name=name, grid=(t // tm,),
        in_specs=[pl.BlockSpec((tm, d), lambda i: (i, 0)), vec, pl.BlockSpec((d, LANES), lambda i: (0, 0)),
                  padded, padded],
        out_specs=[pl.BlockSpec((tm, 2 * d), lambda i: (i, 0)), vec],
        out_shape=[jax.ShapeDtypeStruct((t, 2 * d), BF16), jax.ShapeDtypeStruct((1, d), F32)],
        compiler_params=_params("arbitrary"),
    )(kv, g_tiled, bd, dkp_t, dvp_t)


def _ret_consts(t):
    h = np.arange(RET_HEADS, dtype=np.float32)
    lg = np.log(np.float32(1.0) - np.float32(2.0) ** (np.float32(-5.0) - h)).astype(np.float32)
    tt = np.arange(CHUNK, dtype=np.float32)
    intra = np.exp(lg[:, None, None] * np.abs(tt[:, None] - tt[None, :])).astype(np.float32)
    q_dec = np.exp(lg[:, None] * (tt + 1.0)).astype(np.float32)
    k_dec = np.exp(lg[:, None] * (CHUNK - 1.0 - tt)).astype(np.float32)
    s_dec = [float(v) for v in np.exp(lg * np.float32(CHUNK)).astype(np.float32)]
    qd = np.broadcast_to(q_dec[:, :, None], (RET_HEADS, CHUNK, RET_DK)).copy()
    kd = np.broadcast_to(k_dec[:, :, None], (RET_HEADS, CHUNK, RET_DK)).copy()
    half = RET_DK // 2
    inv_freq = np.float32(ROPE_BASE) ** (-np.arange(half, dtype=np.float32) / np.float32(half))
    ang = np.arange(t, dtype=np.float32)[:, None] * inv_freq[None, :]
    return jnp.asarray(intra), jnp.asarray(qd), jnp.asarray(kd), s_dec, jnp.asarray(np.cos(ang)), jnp.asarray(np.sin(ang))


def _rope(x, cos, sin):
    half = RET_DK // 2
    x1, x2 = x[:, :half], x[:, half:]
    return jnp.concatenate([x1 * cos - x2 * sin, x1 * sin + x2 * cos], axis=-1)


def _unrope(d, cos, sin):
    half = RET_DK // 2
    d1, d2 = d[:, :half], d[:, half:]
    return jnp.concatenate([d1 * cos + d2 * sin, d2 * cos - d1 * sin], axis=-1)


def _ret_slices(h):
    q = slice(h * RET_DK, (h + 1) * RET_DK)
    k = slice(RET_Q_COLS + h * RET_DK, RET_Q_COLS + (h + 1) * RET_DK)
    v = slice(2 * RET_Q_COLS + h * RET_DV, 2 * RET_Q_COLS + (h + 1) * RET_DV)
    g = slice(2 * RET_Q_COLS + RET_V_COLS + h * RET_DV, 2 * RET_Q_COLS + RET_V_COLS + (h + 1) * RET_DV)
    o = slice(h * RET_DV, (h + 1) * RET_DV)
    return q, k, v, g, o


def _ret_fwd(proj, gn, consts, name, rider=None):
    t, cols = proj.shape
    n = t // CHUNK
    intra, qd, kd, s_dec, cos, sin = consts
    k_scale = RET_DK ** -0.5

    def body(p_ref, cos_ref, sin_ref, intra_ref, qd_ref, kd_ref, gn_ref, y_ref, o_ref, st_ref, state):
        i = pl.program_id(0)

        @pl.when(i == 0)
        def _():
            state[...] = jnp.zeros_like(state)

        for c in range(RET_STEP):
            rows = slice(c * CHUNK, (c + 1) * CHUNK)
            cosv, sinv = cos_ref[rows, :], sin_ref[rows, :]
            for h in range(RET_HEADS):
                qs, ks, vs, gs, os_ = _ret_slices(h)
                qr = _rope(p_ref[rows, qs], cosv, sinv)
                kr = _rope(p_ref[rows, ks], cosv, sinv) * k_scale
                vb = p_ref[rows, vs].astype(BF16)
                gv = p_ref[rows, gs]
                scores = _dot_nt(qr.astype(BF16), kr.astype(BF16)) * intra_ref[h]
                s_old = state[h]
                s_old_b = s_old.astype(BF16)
                st_ref[c, h] = s_old_b
                o = _dot(scores.astype(BF16), vb) + _dot((qr * qd_ref[h]).astype(BF16), s_old_b)
                state[h] = s_old * s_dec[h] + _dot_tn((kr * kd_ref[h]).astype(BF16), vb)
                rstd = lax.rsqrt(jnp.mean(o * o, axis=-1, keepdims=True) + EPS)
                on = o * rstd * gn_ref[:, os_]
                o_ref[rows, os_] = o
                y_ref[rows, os_] = (gv * _sigmoid(gv) * on).astype(BF16)

    full3 = lambda a: pl.BlockSpec(a.shape, lambda i: (0, 0, 0))
    step = RET_STEP * CHUNK
    return _call(
        body, name, (n // RET_STEP,),
        [pl.BlockSpec((step, cols), lambda i: (i, 0)),
         pl.BlockSpec((step, RET_DK // 2), lambda i: (i, 0)),
         pl.BlockSpec((step, RET_DK // 2), lambda i: (i, 0)),
         full3(intra), full3(qd), full3(kd),
         pl.BlockSpec((1, RET_V_COLS), lambda i: (0, 0))],
        [pl.BlockSpec((step, RET_V_COLS), lambda i: (i, 0)),
         pl.BlockSpec((step, RET_V_COLS), lambda i: (i, 0)),
         pl.BlockSpec((RET_STEP, RET_HEADS, RET_DK, RET_DV), lambda i: (i, 0, 0, 0))],
        [jax.ShapeDtypeStruct((t, RET_V_COLS), BF16),
         jax.ShapeDtypeStruct((t, RET_V_COLS), F32),
         jax.ShapeDtypeStruct((n, RET_HEADS, RET_DK, RET_DV), BF16)],
        [pltpu.VMEM((RET_HEADS, RET_DK, RET_DV), F32)], ("arbitrary",),
        (proj, cos, sin, intra, qd, kd, gn), rider)


def _ret_bwd(proj, gn, o_saved, states, dy, consts, name, rider=None):
    t, cols = proj.shape
    n = t // CHUNK
    intra, qd, kd, s_dec, cos, sin = consts
    k_scale = RET_DK ** -0.5

    def body(p_ref, cos_ref, sin_ref, intra_ref, qd_ref, kd_ref, gn_ref, o_ref, st_ref, dy_ref,
             dp_ref, dgn_ref, dstate):
        i = pl.program_id(0)

        @pl.when(i == 0)
        def _():
            dstate[...] = jnp.zeros_like(dstate)

        dgn = None
        for c in reversed(range(RET_STEP)):
            rows = slice(c * CHUNK, (c + 1) * CHUNK)
            cosv, sinv = cos_ref[rows, :], sin_ref[rows, :]
            dgn_parts = []
            for h in range(RET_HEADS):
                qs, ks, vs, gs, os_ = _ret_slices(h)
                qr = _rope(p_ref[rows, qs], cosv, sinv)
                kr = _rope(p_ref[rows, ks], cosv, sinv) * k_scale
                qb, kb = qr.astype(BF16), kr.astype(BF16)
                vb = p_ref[rows, vs].astype(BF16)
                gv = p_ref[rows, gs]
                ov = o_ref[rows, os_]
                dyv = dy_ref[rows, os_]
                gnv = gn_ref[:, os_]
                sg = _sigmoid(gv)
                rstd = lax.rsqrt(jnp.mean(ov * ov, axis=-1, keepdims=True) + EPS)
                oh = ov * rstd
                d_on = dyv * (gv * sg)
                dg = dyv * (oh * gnv) * (sg * (1.0 + gv * (1.0 - sg)))
                dgn_parts.append(jnp.sum(d_on * oh, axis=0, keepdims=True))
                d_oh = d_on * gnv
                do = rstd * (d_oh - oh * jnp.mean(d_oh * oh, axis=-1, keepdims=True))
                dob = do.astype(BF16)
                mask = intra_ref[h]
                a_b = (_dot_nt(qb, kb) * mask).astype(BF16)
                da_b = (_dot_nt(dob, vb) * mask).astype(BF16)
                ds_new = dstate[h]
                ds_new_b = ds_new.astype(BF16)
                s_old_b = st_ref[c, h]
                qdv, kdv = qd_ref[h], kd_ref[h]
                dv = _dot_tn(a_b, dob) + _dot((kr * kdv).astype(BF16), ds_new_b)
                dqr = _dot(da_b, kb) + _dot_nt(dob, s_old_b) * qdv
                dkr = _dot_tn(da_b, qb) + _dot_nt(vb, ds_new_b) * kdv
                dstate[h] = ds_new * s_dec[h] + _dot_tn((qr * qdv).astype(BF16), dob)
                dp_ref[rows, qs] = _unrope(dqr, cosv, sinv).astype(BF16)
                dp_ref[rows, ks] = _unrope(dkr * k_scale, cosv, sinv).astype(BF16)
                dp_ref[rows, vs] = dv.astype(BF16)
                dp_ref[rows, gs] = dg.astype(BF16)
            part = jnp.concatenate(dgn_parts, axis=-1)
            dgn = part if dgn is None else dgn + part
        _accumulate(dgn_ref, dgn, i)

    steps = n // RET_STEP
    step = RET_STEP * CHUNK
    rev = lambda i: (steps - 1 - i, 0)
    full3 = lambda a: pl.BlockSpec(a.shape, lambda i: (0, 0, 0))
    return _call(
        body, name, (steps,),
        [pl.BlockSpec((step, cols), rev),
         pl.BlockSpec((step, RET_DK // 2), rev),
         pl.BlockSpec((step, RET_DK // 2), rev),
         full3(intra), full3(qd), full3(kd),
         pl.BlockSpec((1, RET_V_COLS), lambda i: (0, 0)),
         pl.BlockSpec((step, RET_V_COLS), rev),
         pl.BlockSpec((RET_STEP, RET_HEADS, RET_DK, RET_DV), lambda i: (steps - 1 - i, 0, 0, 0)),
         pl.BlockSpec((step, RET_V_COLS), rev)],
        [pl.BlockSpec((step, cols), rev),
         pl.BlockSpec((1, RET_V_COLS), lambda i: (0, 0))],
        [jax.ShapeDtypeStruct((t, cols), BF16),
         jax.ShapeDtypeStruct((1, RET_V_COLS), F32)],
        [pltpu.VMEM((RET_HEADS, RET_DK, RET_DV), F32)], ("arbitrary",),
        (proj, cos, sin, intra, qd, kd, gn, o_saved, states, dy), rider)


def _att_common(q_ref, kp_ref, vp_ref, sub):
    blk = pl.program_id(1) * ATT_SUBS + sub
    start = pl.multiple_of(blk * Q_BLOCK, Q_BLOCK)
    kw = kp_ref[pl.ds(start, K_WINDOW), :]
    vw = vp_ref[pl.ds(start, K_WINDOW), :]
    kvalid = blk * Q_BLOCK - K_PAD + lax.broadcasted_iota(jnp.int32, (1, K_WINDOW), 1) >= 0
    lane = lax.broadcasted_iota(jnp.int32, (1, LANES), 1)
    qrows = slice(sub * Q_BLOCK, (sub + 1) * Q_BLOCK)
    return start, qrows, q_ref[qrows, :], kw, vw, kvalid, (lane < ATT_DH, lane >= ATT_DH)


def _row_groups():
    return [slice(r * ATT_ROWS, (r + 1) * ATT_ROWS) for r in range(Q_BLOCK // ATT_ROWS)]


def _lane_copies(x):
    return jnp.tile(x, (1, K_WINDOW // LANES))


def _att_specs(t, tp):
    qspec = pl.BlockSpec((ATT_SUBS * Q_BLOCK, LANES), lambda h, i: (i, h))
    kspec = pl.BlockSpec((tp, LANES), lambda h, i: (0, h))
    bspec = pl.BlockSpec((2, Q_BLOCK, K_WINDOW), lambda h, i: (h, 0, 0))
    return qspec, kspec, bspec


def _att_fwd(q, kp, vp, bias, name, rider=None):
    t, d = q.shape
    tp = kp.shape[0]

    def body(q_ref, kp_ref, vp_ref, bias_ref, o_ref, lse_ref, s_scr, p_scr, lse_scr, inv_scr):
        for sub in range(ATT_SUBS):
            _, qrows, q2, kw, vw, kvalid, sel = _att_common(q_ref, kp_ref, vp_ref, sub)
            for hh in range(2):
                s_scr[sub, hh] = _dot_nt(jnp.where(sel[hh], q2, 0), kw)
            for hh in range(2):
                for rows in _row_groups():
                    s = jnp.where(kvalid, s_scr[sub, hh, rows, :] + bias_ref[hh, rows, :], NEG)
                    m = jnp.max(s, axis=-1, keepdims=True)
                    e = jnp.exp(s - m)
                    l = jnp.sum(e, axis=-1, keepdims=True)
                    p_scr[sub, hh, rows, :] = e.astype(BF16)
                    inv_scr[sub, hh, rows, :] = jnp.broadcast_to(1.0 / l, (ATT_ROWS, LANES))
                    lse_scr[sub, hh, rows, :] = jnp.broadcast_to(m + jnp.log(l), (ATT_ROWS, LANES))
            outs = [_dot(p_scr[sub, hh], vw) * inv_scr[sub, hh] for hh in range(2)]
            o_ref[qrows, :] = jnp.where(sel[0], outs[0], outs[1]).astype(BF16)
            lse_ref[qrows, :] = jnp.where(sel[0], lse_scr[sub, 0], lse_scr[sub, 1])

    qspec, kspec, bspec = _att_specs(t, tp)
    return _call(body, name, (d // LANES, t // (ATT_SUBS * Q_BLOCK)), [qspec, kspec, kspec, bspec], [qspec, qspec],
                 [jax.ShapeDtypeStruct((t, d), BF16), jax.ShapeDtypeStruct((t, d), F32)],
                 [pltpu.VMEM((ATT_SUBS, 2, Q_BLOCK, K_WINDOW), F32),
                  pltpu.VMEM((ATT_SUBS, 2, Q_BLOCK, K_WINDOW), BF16),
                  pltpu.VMEM((ATT_SUBS, 2, Q_BLOCK, LANES), F32),
                  pltpu.VMEM((ATT_SUBS, 2, Q_BLOCK, LANES), F32)],
                 ("parallel", "arbitrary"), (q, kp, vp, bias), rider)


def _att_bwd(q, kp, vp, bias, do, o, lse, name, rider=None):
    t, d = q.shape
    tp = kp.shape[0]

    def body(q_ref, kp_ref, vp_ref, bias_ref, do_ref, o_ref, lse_ref, dq_ref, dkp_ref, dvp_ref, db_ref,
             s_scr, dp_scr, p_scr, ds_scr, row_scr):
        @pl.when(pl.program_id(1) == 0)
        def _():
            dkp_ref[...] = jnp.zeros_like(dkp_ref)
            dvp_ref[...] = jnp.zeros_like(dvp_ref)
            db_ref[...] = jnp.zeros_like(db_ref)

        for sub in range(ATT_SUBS):
            start, qrows, q2, kw, vw, kvalid, sel = _att_common(q_ref, kp_ref, vp_ref, sub)
            do2 = do_ref[qrows, :]
            qm = [jnp.where(sel[hh], q2, 0) for hh in range(2)]
            dom = [jnp.where(sel[hh], do2, 0) for hh in range(2)]
            do_o = do2.astype(F32) * o_ref[qrows, :].astype(F32)
            lse2 = lse_ref[qrows, :]
            for hh in range(2):
                s_scr[sub, hh] = _dot_nt(qm[hh], kw)
                dp_scr[sub, hh] = _dot_nt(dom[hh], vw)
                lse_h = jnp.max(jnp.where(sel[hh], lse2, NEG), axis=-1, keepdims=True)
                delta = jnp.sum(jnp.where(sel[hh], do_o, 0.0), axis=-1, keepdims=True)
                row_scr[sub, hh, 0] = jnp.broadcast_to(lse_h, (Q_BLOCK, LANES))
                row_scr[sub, hh, 1] = jnp.broadcast_to(delta, (Q_BLOCK, LANES))
            for hh in range(2):
                for rows in _row_groups():
                    s = jnp.where(kvalid, s_scr[sub, hh, rows, :] + bias_ref[hh, rows, :], NEG)
                    p = jnp.exp(s - _lane_copies(row_scr[sub, hh, 0, rows, :]))
                    ds = p * (dp_scr[sub, hh, rows, :] - _lane_copies(row_scr[sub, hh, 1, rows, :]))
                    db_ref[hh, rows, :] += ds
                    p_scr[sub, hh, rows, :] = p.astype(BF16)
                    ds_scr[sub, hh, rows, :] = ds.astype(BF16)
            dqs = [_dot(ds_scr[sub, hh], kw) for hh in range(2)]
            dq_ref[qrows, :] = jnp.where(sel[0], dqs[0], dqs[1])
            dkp_ref[:, pl.ds(start, K_WINDOW)] += (_dot_tn(qm[0], ds_scr[sub, 0]) +
                                                   _dot_tn(qm[1], ds_scr[sub, 1]))
            dvp_ref[:, pl.ds(start, K_WINDOW)] += (_dot_tn(dom[0], p_scr[sub, 0]) +
                                                   _dot_tn(dom[1], p_scr[sub, 1]))

    qspec, kspec, bspec = _att_specs(t, tp)
    tspec = pl.BlockSpec((LANES, tp), lambda h, i: (h, 0))
    stage = lambda dt: pltpu.VMEM((ATT_SUBS, 2, Q_BLOCK, K_WINDOW), dt)
    return _call(body, name, (d // LANES, t // (ATT_SUBS * Q_BLOCK)),
                 [qspec, kspec, kspec, bspec, qspec, qspec, qspec],
                 [qspec, tspec, tspec, bspec],
                 [jax.ShapeDtypeStruct((t, d), F32),
                  jax.ShapeDtypeStruct((d, tp), F32),
                  jax.ShapeDtypeStruct((d, tp), F32),
                  jax.ShapeDtypeStruct((ATT_HEADS, Q_BLOCK, K_WINDOW), F32)],
                 [stage(F32), stage(F32), stage(BF16), stage(BF16),
                  pltpu.VMEM((ATT_SUBS, 2, 2, Q_BLOCK, LANES), F32)],
                 ("parallel", "arbitrary"), (q, kp, vp, bias, do, o, lse), rider)


def _rel_bin_matrix():
    rows = REL_DELTAS * 2 * REL_BLK
    rho = lax.broadcasted_iota(jnp.int32, (rows, REL_PAD), 0)
    col = lax.broadcasted_iota(jnp.int32, (rows, REL_PAD), 1)
    assert 2 * REL_BLK == 256
    delta = rho >> 8
    c = 255 - (rho & 255)
    dist = K_PAD + REL_BLK * (delta - (K_WINDOW // REL_BLK - 1)) + (c - (REL_BLK - 1))
    idx = jnp.clip(dist, -REL_CLIP, REL_CLIP) + REL_CLIP
    return col == idx


def _rel_expand(rel_pad, name):
    heads = rel_pad.shape[0]
    rows = REL_DELTAS * 2 * REL_BLK

    def body_bin(r_ref, o_ref):
        onehot = jnp.where(_rel_bin_matrix(), 1.0, 0.0).astype(BF16)
        hi, mid, lo = _split3(r_ref[...])
        o_ref[...] = _dot_nt(hi, onehot) + _dot_nt(mid, onehot) + _dot_nt(lo, onehot)

    by_delta = pl.pallas_call(
        body_bin, name=name + "_bin",
        out_shape=jax.ShapeDtypeStruct((heads, rows), F32),
        compiler_params=pltpu.CompilerParams(vmem_limit_bytes=VMEM_LIMIT_V7X),
    )(rel_pad)
    by_delta = by_delta.reshape(heads * REL_DELTAS, 2 * REL_BLK)

    def body_shift(t_ref, o_ref):
        tv = t_ref[...]
        for r in range(REL_BLK):
            o_ref[r] = pltpu.roll(tv, (r + REL_BLK) % (2 * REL_BLK), 1)[:, :REL_BLK]

    return pl.pallas_call(
        body_shift, name=name + "_shift",
        out_shape=jax.ShapeDtypeStruct((REL_BLK, heads * REL_DELTAS, REL_BLK), F32),
        compiler_params=pltpu.CompilerParams(vmem_limit_bytes=VMEM_LIMIT_V7X),
    )(by_delta)


def _bias_table(rel_bias, name):
    heads = rel_bias.shape[0]
    rel_pad = jnp.pad(rel_bias, ((0, 0), (0, REL_PAD - REL_TABLE)))
    tiles = _rel_expand(rel_pad, name)
    tiles = tiles.reshape(REL_BLK, heads, REL_DELTAS, REL_BLK).transpose(1, 2, 0, 3)
    na, nb = Q_BLOCK // REL_BLK, K_WINDOW // REL_BLK
    rows = [jnp.concatenate([tiles[:, a - b + nb - 1] for b in range(nb)], axis=-1) for a in range(na)]
    table = jnp.concatenate(rows, axis=-2)
    qc = np.arange(Q_BLOCK)[:, None] // CHUNK
    kc = np.arange(K_WINDOW)[None, :] // CHUNK
    band = (kc >= qc) & (kc <= qc + PAST_CHUNKS)
    return jnp.where(jnp.asarray(band)[None], table, NEG)


def _rel_reduce(db, name):
    heads = db.shape[0]
    na, nb = Q_BLOCK // REL_BLK, K_WINDOW // REL_BLK

    fold_heads = 4

    def body_fold(db_ref, g_ref):
        for hd in range(fold_heads):
            for delta in range(REL_DELTAS):
                acc = None
                for a in range(na):
                    b = a - (delta - (nb - 1))
                    if 0 <= b < nb:
                        tile = db_ref[hd, a * REL_BLK:(a + 1) * REL_BLK, b * REL_BLK:(b + 1) * REL_BLK]
                        acc = tile if acc is None else acc + tile
                g_ref[hd, delta] = acc

    folded = pl.pallas_call(
        body_fold, name=name + "_fold", grid=(heads // fold_heads,),
        in_specs=[pl.BlockSpec((fold_heads, Q_BLOCK, K_WINDOW), lambda h: (h, 0, 0))],
        out_specs=pl.BlockSpec((fold_heads, REL_DELTAS, REL_BLK, REL_BLK), lambda h: (h, 0, 0, 0)),
        out_shape=jax.ShapeDtypeStruct((heads, REL_DELTAS, REL_BLK, REL_BLK), F32),
        compiler_params=_params("parallel"),
    )(db)
    by_row = folded.transpose(2, 0, 1, 3).reshape(REL_BLK, heads * REL_DELTAS, REL_BLK)

    def body_diag(g_ref, d_ref):
        zeros = jnp.zeros((heads * REL_DELTAS, REL_BLK), F32)
        acc = None
        for r in range(REL_BLK):
            part = pltpu.roll(jnp.concatenate([g_ref[r], zeros], axis=1), REL_BLK - r, 1)
            acc = part if acc is None else acc + part
        d_ref[...] = acc

    diag = pl.pallas_call(
        body_diag, name=name + "_diag",
        out_shape=jax.ShapeDtypeStruct((heads * REL_DELTAS, 2 * REL_BLK), F32),
        compiler_params=pltpu.CompilerParams(vmem_limit_bytes=VMEM_LIMIT_V7X),
    )(by_row)
    diag = diag.reshape(heads, REL_DELTAS * 2 * REL_BLK)

    def body_bin(d_ref, o_ref):
        onehot = jnp.where(_rel_bin_matrix(), 1.0, 0.0).astype(BF16)
        hi, mid, lo = _split3(d_ref[...])
        o_ref[...] = _dot(hi, onehot) + _dot(mid, onehot) + _dot(lo, onehot)

    out = pl.pallas_call(
        body_bin, name=name + "_bin",
        out_shape=jax.ShapeDtypeStruct((heads, REL_PAD), F32),
        compiler_params=pltpu.CompilerParams(vmem_limit_bytes=VMEM_LIMIT_V7X),
    )(diag)
    return out[:, :REL_TABLE]


def _sum_leading(x, name):
    n, r, c = x.shape
    tr = _pick(r, 256, 8)

    def body(x_ref, o_ref):
        acc = x_ref[0].astype(F32)
        for k in range(1, n):
            acc = acc + x_ref[k].astype(F32)
        o_ref[...] = acc

    return pl.pallas_call(
        body, name=name, grid=(r // tr,),
        in_specs=[pl.BlockSpec((n, tr, c), lambda i: (0, i, 0))],
        out_specs=pl.BlockSpec((tr, c), lambda i: (i, 0)),
        out_shape=jax.ShapeDtypeStruct((r, c), F32),
        compiler_params=_params("parallel"),
    )(x)


def _pair_add(g, recv, parity, name):
    _, r, c = g.shape
    tr = _pick(r, 256, 16)

    def body(par_ref, g_ref, r_ref, o_ref):
        o_ref[...] = (g_ref[...].astype(F32) + r_ref[...].astype(F32)).astype(BF16)

    return pl.pallas_call(
        body, name=name,
        grid_spec=pltpu.PrefetchScalarGridSpec(
            num_scalar_prefetch=1, grid=(4, r // tr),
            in_specs=[pl.BlockSpec((1, tr, c), lambda k, i, par: (2 * k + par[0], i, 0)),
                      pl.BlockSpec((1, tr, c), lambda k, i, par: (k, i, 0))],
            out_specs=pl.BlockSpec((1, tr, c), lambda k, i, par: (k, i, 0))),
        out_shape=jax.ShapeDtypeStruct((4, r, c), BF16),
        compiler_params=_params("parallel", "parallel"),
    )(parity, g, recv)


def _adamw(w, g_parts, m, v, name):
    r, c = w.shape
    n = g_parts.shape[0]
    tr = _pick(r, 256, 16 if g_parts.dtype == BF16 else 8)
    c1 = 1.0 - ADAM_B1 ** ADAM_STEP
    c2 = 1.0 - ADAM_B2 ** ADAM_STEP

    def body(w_ref, g_ref, m_ref, v_ref, go_ref, d_ref, nm_ref, nv_ref):
        gv = g_ref[0].astype(F32)
        for k in range(1, n):
            gv = gv + g_ref[k].astype(F32)
        nm = ADAM_B1 * m_ref[...] + (1.0 - ADAM_B1) * gv
        nv = ADAM_B2 * v_ref[...] + (1.0 - ADAM_B2) * (gv * gv)
        go_ref[...] = gv
        d_ref[...] = -ADAM_LR * ((nm / c1) / (jnp.sqrt(nv / c2) + ADAM_EPS) + ADAM_WD * w_ref[...])
        nm_ref[...] = nm
        nv_ref[...] = nv

    spec = pl.BlockSpec((tr, c), lambda i: (i, 0))
    shp = jax.ShapeDtypeStruct((r, c), F32)
    return pl.pallas_call(
        body, name=name, grid=(r // tr,),
        in_specs=[spec, pl.BlockSpec((n, tr, c), lambda i: (0, i, 0)), spec, spec],
        out_specs=[spec] * 4, out_shape=[shp] * 4,
        compiler_params=_params("parallel"),
    )(w, g_parts, m, v)


BIG = (("a_w_in", 1), ("a_w_o", 0), ("a_w_gu", 0), ("a_w_down", 0), ("w_kv", 1),
       ("b_w_q", 0), ("b_w_o", 0), ("b_w_gu", 0), ("b_w_down", 0))
TRANSPOSED = ("a_w_gu", "b_w_gu")
FFN_BLK = 2 * FFN_HIDDEN // N_DEV

SMALL = (("a_norm_g", D_MODEL, True), ("a_gn_g", RET_V_COLS, True), ("a_ffn_norm_g", D_MODEL, True),
         ("kv_norm_g", D_MODEL, False), ("b_norm_g", D_MODEL, False), ("b_ffn_norm_g", D_MODEL, False),
         ("k_norm_g", ATT_DH, False), ("b_q_norm_g", ATT_DH, False),
         ("b_rel_bias", ATT_HEADS * REL_TABLE, False))
SMALL_ROWS, SMALL_COLS = 16, 1024


def _pack_small(vals, last=None):
    flat = jnp.concatenate([vals[n].reshape(-1) for n, _, _ in SMALL])
    room = SMALL_ROWS * SMALL_COLS - flat.shape[0]
    if last is None:
        flat = jnp.pad(flat, (0, room))
    else:
        flat = jnp.concatenate([jnp.pad(flat, (0, room - 1)), last.reshape(1)])
    return flat.reshape(SMALL_ROWS, SMALL_COLS)


def _unpack_small(packed, local):
    flat, out, pos = packed.reshape(-1), {}, 0
    for n, length, sharded in SMALL:
        ln = length // N_DEV if (local and sharded) else length
        out[n] = flat[pos:pos + ln]
        pos += ln
    return out


def _gather_rider(shards, names):
    return _GatherRider([shards[n] for n in names])


def _gathered(rider, names, axis_of):
    return {n: (r.reshape(-1, r.shape[2]) if axis_of[n] == 0 else r) for n, r in zip(names, rider.results)}


def _blocks(g):
    return g if g.ndim == 3 else g.reshape(N_DEV, -1, g.shape[-1])


def _local_step(x, target, shards, s, parity):
    t = x.shape[0]
    axis_of = dict(BIG)
    consts = _ret_consts(t)
    lane_to_head = np.zeros((D_MODEL, LANES), np.float32)
    lane_to_head[np.arange(D_MODEL), np.arange(D_MODEL) // ATT_DH] = 1.0
    bd = jnp.asarray(lane_to_head).astype(BF16)
    kg_t = jnp.tile(s["k_norm_g"], (1, ATT_HEADS))
    qg_t = jnp.tile(s["b_q_norm_g"], (1, ATT_HEADS))
    q_scale = ATT_DH ** -0.5
    w, g, recv = {}, {}, {}

    def gather_on(names):
        return _gather_rider(shards, names), names

    def landed(ride):
        w.update(_gathered(ride[0], ride[1], axis_of))

    def scatter_on(names):
        return _ScatterRider([_blocks(g[n]) for n in names]), names

    def reduced(ride):
        recv.update(zip(ride[1], ride[0].results))

    proj, (w["a_w_in"], w_o) = _proj_gather(x, s["a_norm_g"], shards["a_w_in"], [shards["a_w_o"]], "a_proj")
    w["a_w_o"] = w_o.reshape(-1, w_o.shape[2])
    ride = gather_on(["a_w_gu"])
    y, o_ret, states = _ret_fwd(proj, s["a_gn_g"], consts, "a_ret", rider=ride[0])
    landed(ride)
    ride = gather_on(["w_kv"])
    x1 = _mm(y, w["a_w_o"], "nn", "a_out", res=x, rider=ride[0])
    landed(ride)
    ride = gather_on(["a_w_down", "b_w_q", "b_w_o"])
    gu_a, act_a = _mm(x1, w["a_w_gu"], "nt", "a_ffn_gu", epilogue="swiglu", out_block=FFN_BLK,
                      norm_g=s["a_ffn_norm_g"], rider=ride[0])
    landed(ride)
    x2 = _mm(act_a, w["a_w_down"], "nn", "a_ffn_down", res=x1)

    kv = _mm(x2, w["w_kv"], "nn", "kv_proj", norm_g=s["kv_norm_g"])
    kp, vp = _kv_prep(kv, kg_t, bd, "kv_prep")

    q_raw = _mm(x2, w["b_w_q"], "nn", "b_q", norm_g=s["b_norm_g"])
    qn = _q_hnorm(q_raw, qg_t, bd, q_scale, "q_hnorm")
    bias = _bias_table(s["b_rel_bias"].reshape(ATT_HEADS, REL_TABLE), "rel")
    ride = gather_on(["b_w_gu"])
    o_att, lse = _att_fwd(qn, kp, vp, bias, "b_att", rider=ride[0])
    landed(ride)
    x3 = _mm(o_att, w["b_w_o"], "nn", "b_out", res=x2)
    ride = gather_on(["b_w_down"])
    gu_b, act_b = _mm(x3, w["b_w_gu"], "nt", "b_ffn_gu", epilogue="swiglu", out_block=FFN_BLK,
                      norm_g=s["b_ffn_norm_g"], rider=ride[0])
    landed(ride)
    dy, loss = _mm(act_b, w["b_w_down"], "nn", "b_ffn_down", res=x3, epilogue="loss", extra=(target,))
    in_blk, kv_blk, ffn_blk = w["a_w_in"].shape[2], w["w_kv"].shape[2], FFN_BLK

    dgu = _mm(dy, w["b_w_down"], "nt", "b_ffn_dgu", out_block=ffn_blk, epilogue="swiglu_bwd", extra=gu_b)
    dgu = dgu.reshape(N_DEV, t, ffn_blk)
    g["b_w_down"] = _mm(act_b, dy, "tn", "b_ffn_gdown", out_dtype=BF16)
    ride = scatter_on(["b_w_down"])
    dx3, g["b_ffn_norm_g"] = _mm(dgu, w["b_w_gu"], "nn", "b_ffn_dh", epilogue="rms_bwd",
                                 extra=(x3, s["b_ffn_norm_g"], dy), rider=ride[0])
    reduced(ride)
    g["b_w_gu"] = _mm(dgu, x3, "tn", "b_ffn_ggu", out_dtype=BF16, norm_g=s["b_ffn_norm_g"], norm_b=True)

    do_att = _mm(dx3, w["b_w_o"], "nt", "b_dout", out_dtype=BF16)
    g["b_w_o"] = _mm(o_att, dx3, "tn", "b_gout", out_dtype=BF16)
    ride = scatter_on(["b_w_gu", "b_w_o"])
    dq, dkp, dvp, db = _att_bwd(qn, kp, vp, bias, do_att, o_att, lse, "b_datt", rider=ride[0])
    reduced(ride)
    g["b_rel_bias"] = _rel_reduce(db, "drel").reshape(1, -1)
    dq_raw, gq = _q_dhnorm(q_raw, qg_t, bd, dq, q_scale, "q_dhnorm")
    g["b_q_norm_g"] = gq.reshape(ATT_HEADS, ATT_DH).sum(axis=0, keepdims=True)
    g["b_w_q"] = _mm(x2, dq_raw, "tn", "b_gq", out_dtype=BF16, norm_g=s["b_norm_g"])
    dx2, g["b_norm_g"] = _mm(dq_raw, w["b_w_q"], "nt", "b_dq", epilogue="rms_bwd",
                             extra=(x2, s["b_norm_g"], dx3))

    dkv, gk = _kv_dprep(kv, kg_t, bd, dkp, dvp, "kv_dprep")
    g["k_norm_g"] = gk.reshape(ATT_HEADS, ATT_DH).sum(axis=0, keepdims=True)
    g["w_kv"] = _mm(x2, dkv, "tn", "kv_g", out_dtype=BF16, out_block=kv_blk, norm_g=s["kv_norm_g"])
    dx2, g["kv_norm_g"] = _mm(dkv, w["w_kv"], "nt", "kv_du", epilogue="rms_bwd",
                              extra=(x2, s["kv_norm_g"], dx2))

    ride = scatter_on(["b_w_q"])
    dgu = _mm(dx2, w["a_w_down"], "nt", "a_ffn_dgu", out_block=ffn_blk, epilogue="swiglu_bwd", extra=gu_a,
              rider=ride[0])
    reduced(ride)
    dgu = dgu.reshape(N_DEV, t, ffn_blk)
    g["a_w_down"] = _mm(act_a, dx2, "tn", "a_ffn_gdown", out_dtype=BF16)
    ride = scatter_on(["a_w_down"])
    dx1, g["a_ffn_norm_g"] = _mm(dgu, w["a_w_gu"], "nn", "a_ffn_dh", epilogue="rms_bwd",
                                 extra=(x1, s["a_ffn_norm_g"], dx2), rider=ride[0])
    reduced(ride)
    ride = scatter_on(["w_kv"])
    g["a_w_gu"] = _mm(dgu, x1, "tn", "a_ffn_ggu", out_dtype=BF16, norm_g=s["a_ffn_norm_g"], norm_b=True,
                      rider=ride[0])
    reduced(ride)

    swap = _SiblingSwapRider([_blocks(g["a_w_gu"])])
    dy_ret = _mm(dx1, w["a_w_o"], "nt", "a_dout", rider=swap)
    g["a_w_o"] = _mm(y, dx1, "tn", "a_gout", out_dtype=BF16)
    chips = _ChipScatterRider([_pair_add(_blocks(g["a_w_gu"]), swap.results[0], parity, "rs_pair_add_gu")])
    dproj, g["a_gn_g"] = _ret_bwd(proj, s["a_gn_g"], o_ret, states, dy_ret, consts, "a_dret", rider=chips)
    recv["a_w_gu"] = chips.results[0]
    ride = scatter_on(["a_w_o"])
    g["a_w_in"] = _mm(x, dproj, "tn", "a_gin", out_dtype=BF16, out_block=in_blk, norm_g=s["a_norm_g"],
                      rider=ride[0])
    reduced(ride)
    from_sibling = _exchange(_SiblingSwapRider([g["a_w_in"]]), "rs_sibling")[0]
    chip_sums = _pair_add(g["a_w_in"], from_sibling, parity, "rs_pair_add")
    last = _ChipScatterRider([chip_sums])
    grad_x, g["a_norm_g"] = _mm(dproj, w["a_w_in"], "nt", "a_dproj", epilogue="rms_bwd",
                                extra=(x, s["a_norm_g"], dx1), rider=last)
    recv["a_w_in"] = last.results[0]
    return loss, grad_x, recv, g


ARG_NAMES = ("x", "a_norm_g", "a_w_in", "a_gn_g", "a_w_o", "a_ffn_norm_g", "a_w_gu", "a_w_down",
             "kv_norm_g", "w_kv", "k_norm_g", "b_norm_g", "b_w_q", "b_q_norm_g", "b_rel_bias", "b_w_o",
             "b_ffn_norm_g", "b_w_gu", "b_w_down")
WEIGHT_NAMES = ARG_NAMES[1:]


def _big_shard(a, name):
    a = a[0] if a.ndim == 3 else a
    return a.T if name in TRANSPOSED else a


def _as_given(a, name, shape):
    return (a.T if name in TRANSPOSED else a).reshape(shape)


def kernel(x, a_norm_g, a_w_in, a_gn_g, a_w_o, a_ffn_norm_g, a_w_gu, a_w_down, kv_norm_g, w_kv, k_norm_g, b_norm_g, b_w_q, b_q_norm_g, b_rel_bias, b_w_o, b_ffn_norm_g, b_w_gu, b_w_down, loss_target, m_a_norm_g, m_a_w_in, m_a_gn_g, m_a_w_o, m_a_ffn_norm_g, m_a_w_gu, m_a_w_down, m_kv_norm_g, m_w_kv, m_k_norm_g, m_b_norm_g, m_b_w_q, m_b_q_norm_g, m_b_rel_bias, m_b_w_o, m_b_ffn_norm_g, m_b_w_gu, m_b_w_down, v_a_norm_g, v_a_w_in, v_a_gn_g, v_a_w_o, v_a_ffn_norm_g, v_a_w_gu, v_a_w_down, v_kv_norm_g, v_w_kv, v_k_norm_g, v_b_norm_g, v_b_w_q, v_b_q_norm_g, v_b_rel_bias, v_b_w_o, v_b_ffn_norm_g, v_b_w_gu, v_b_w_down):
    args = (x, a_norm_g, a_w_in, a_gn_g, a_w_o, a_ffn_norm_g, a_w_gu, a_w_down, kv_norm_g, w_kv, k_norm_g,
            b_norm_g, b_w_q, b_q_norm_g, b_rel_bias, b_w_o, b_ffn_norm_g, b_w_gu, b_w_down)
    p = dict(zip(ARG_NAMES, args))
    m_all = dict(zip(WEIGHT_NAMES, (m_a_norm_g, m_a_w_in, m_a_gn_g, m_a_w_o, m_a_ffn_norm_g, m_a_w_gu,
                                    m_a_w_down, m_kv_norm_g, m_w_kv, m_k_norm_g, m_b_norm_g, m_b_w_q,
                                    m_b_q_norm_g, m_b_rel_bias, m_b_w_o, m_b_ffn_norm_g, m_b_w_gu, m_b_w_down)))
    v_all = dict(zip(WEIGHT_NAMES, (v_a_norm_g, v_a_w_in, v_a_gn_g, v_a_w_o, v_a_ffn_norm_g, v_a_w_gu,
                                    v_a_w_down, v_kv_norm_g, v_w_kv, v_k_norm_g, v_b_norm_g, v_b_w_q,
                                    v_b_q_norm_g, v_b_rel_bias, v_b_w_o, v_b_ffn_norm_g, v_b_w_gu, v_b_w_down)))
    xi, yi, ci = _my_place()
    me = 4 * xi + 2 * yi + ci
    big_names = [n for n, _ in BIG]

    big_local = {n: _big_shard(p[n], n) for n in big_names}
    shards = {n: a.astype(BF16) for n, a in big_local.items()}
    small_local = _pack_small({n: p[n] for n, _, _ in SMALL})
    small_all = _exchange(_GatherRider([small_local]), "gather_small")[0]
    flat_g = small_all.reshape(N_DEV, -1)
    s_full, pos = {}, 0
    for n, length, sharded in SMALL:
        ln = length // N_DEV if sharded else length
        s_full[n] = flat_g[:, pos:pos + ln].reshape(1, -1) if sharded else p[n].reshape(1, -1)
        pos += ln

    parity = jnp.reshape(ci, (1,)).astype(jnp.int32)
    loss, grad_x, recv, g = _local_step(x[0], loss_target[0], shards, s_full, parity)

    partial = _pack_small({n: g[n] for n, _, _ in SMALL}, last=loss)
    summed = _sum_leading(_exchange(_GatherRider([partial]), "gather_gsmall")[0], "gsmall_sum")
    loss = summed[SMALL_ROWS - 1, SMALL_COLS - 1]
    g_small = _unpack_small(summed, local=False)
    for n, length, sharded in SMALL:
        if sharded:
            g_small[n] = lax.dynamic_slice(g_small[n], (me * (length // N_DEV),), (length // N_DEV,))

    grads, deltas, new_m, new_v = {}, {}, {}, {}
    for n in big_names:
        outs = _adamw(big_local[n], recv[n], _big_shard(m_all[n], n), _big_shard(v_all[n], n), "adamw_" + n)
        grads[n], deltas[n], new_m[n], new_v[n] = (_as_given(a, n, p[n].shape) for a in outs)
    pk = lambda src: _pack_small({n: src[n] for n, _, _ in SMALL})
    outs = _adamw(small_local, pk(g_small)[None], pk(m_all), pk(v_all), "adamw_small")
    g_s, d_s, nm_s, nv_s = (_unpack_small(a, local=True) for a in outs)
    for n, _, _ in SMALL:
        grads[n], deltas[n], new_m[n], new_v[n] = (a[n].reshape(p[n].shape) for a in (g_s, d_s, nm_s, nv_s))

    return (loss, grad_x[None], *[grads[n] for n in WEIGHT_NAMES], *[deltas[n] for n in WEIGHT_NAMES],
            *[new_m[n] for n in WEIGHT_NAMES], *[new_v[n] for n in WEIGHT_NAMES])
```

```python
import numpy as np
import jax
import jax.numpy as jnp
from jax import lax
from jax.experimental import pallas as pl
from jax.experimental.pallas import tpu as pltpu

F32 = jnp.float32
BF16 = jnp.bfloat16

N_DEV = 8
D_MODEL = 1024
CHUNK = 64
EPS = 1e-6
RET_HEADS, RET_DK, RET_DV = 4, 256, 512
RET_STEP = 4
RET_Q_COLS = RET_HEADS * RET_DK
RET_V_COLS = RET_HEADS * RET_DV
ATT_HEADS, ATT_DH = 16, 64
PAST_CHUNKS = 8
REL_CLIP = 256
REL_TABLE = 2 * REL_CLIP + 1
FFN_HIDDEN = 2816
ROPE_BASE = 10000.0
LANES = 128
Q_BLOCK = 256
ATT_SUBS = 4
ATT_ROWS = 32
K_PAD = PAST_CHUNKS * CHUNK
K_WINDOW = Q_BLOCK + K_PAD
REL_BLK = 128
REL_DELTAS = Q_BLOCK // REL_BLK + K_WINDOW // REL_BLK - 1
REL_PAD = 640
NEG = -1e30
VMEM_LIMIT_V7X = 56 * 1024 * 1024
ADAM_LR, ADAM_B1, ADAM_B2, ADAM_EPS, ADAM_WD, ADAM_STEP = 1e-3, 0.9, 0.999, 1e-8, 0.01, 10
MESH = pl.DeviceIdType.MESH
ANY = pl.BlockSpec(memory_space=pl.ANY)


def _params(*semantics):
    return pltpu.CompilerParams(dimension_semantics=semantics, vmem_limit_bytes=VMEM_LIMIT_V7X)


def _pick(dim, cap, align):
    best = None
    for t in range(align, min(dim, cap) + 1, align):
        if dim % t == 0:
            best = t
    assert best is not None, (dim, cap, align)
    return best


def _dot(a, b):
    return lax.dot_general(a, b, (((1,), (0,)), ((), ())), preferred_element_type=F32)


def _dot_nt(a, b):
    return lax.dot_general(a, b, (((1,), (1,)), ((), ())), preferred_element_type=F32)


def _dot_tn(a, b):
    return lax.dot_general(a, b, (((0,), (0,)), ((), ())), preferred_element_type=F32)


def _split2(x):
    hi = x.astype(BF16)
    lo = (x - hi.astype(F32)).astype(BF16)
    return hi, lo


def _split3(x):
    hi = x.astype(BF16)
    r = x - hi.astype(F32)
    mid = r.astype(BF16)
    lo = (r - mid.astype(F32)).astype(BF16)
    return hi, mid, lo


def _sigmoid(x):
    return 1.0 / (1.0 + jnp.exp(-x))


def _accumulate(ref, part, step):
    @pl.when(step == 0)
    def _():
        ref[...] = part

    @pl.when(step > 0)
    def _():
        ref[...] += part


RELAY_AT_NUM, RELAY_AT_DEN = 3, 4


def _my_place():
    return lax.axis_index("x"), lax.axis_index("y"), lax.axis_index("c")


def _flip(v, bit):
    return 1 - v if bit else v


class _NoRelay:
    def relay(self, in_refs, out_refs, sems):
        pass


class _GatherRider:
    def __init__(self, xs):
        self.inputs = list(xs)
        n = len(xs)
        self.out_shape = [jax.ShapeDtypeStruct((N_DEV,) + x.shape, x.dtype) for x in xs]
        self.scratch = [pltpu.SemaphoreType.DMA((7, n)), pltpu.SemaphoreType.DMA((7, n)),
                        pltpu.SemaphoreType.DMA((n,))]
        self.results = None

    def _copies(self, x_refs, out_refs, sems):
        send_sems, recv_sems, local_sems = sems
        n = len(x_refs)
        x, y, c = _my_place()
        me, sibling = (x, y, c), (x, y, 1 - c)
        chips = [(1 - x, y), (x, 1 - y), (1 - x, 1 - y)]

        def slot(a, px, py, pc):
            return out_refs[a].at[4 * px + 2 * py + pc]

        def copy(k, a, block, to, own=False):
            return pltpu.make_async_remote_copy(
                src_ref=x_refs[a] if own else slot(a, *block), dst_ref=slot(a, *block),
                send_sem=send_sems.at[k, a], recv_sem=recv_sems.at[k, a],
                device_id=to, device_id_type=MESH)

        mine = [pltpu.make_async_copy(x_refs[a], slot(a, *me), local_sems.at[a]) for a in range(n)]
        first = []
        for a in range(n):
            first.append(copy(0, a, me, sibling, own=True))
            first += [copy(1 + j, a, me, (*chip, c), own=True) for j, chip in enumerate(chips)]
        return n, c, me, sibling, chips, copy, mine, first

    def start(self, x_refs, out_refs, sems):
        _, _, _, _, _, _, mine, first = self._copies(x_refs, out_refs, sems)
        for cp in mine + first:
            cp.start()

    def relay(self, x_refs, out_refs, sems):
        n, c, me, sibling, chips, copy, _, _ = self._copies(x_refs, out_refs, sems)
        for j, chip in enumerate(chips):
            for a in range(n):
                copy(1 + j, a, (*chip, c), me).wait_recv()
                copy(4 + j, a, (*chip, c), sibling).start()

    def finish(self, x_refs, out_refs, sems):
        n, c, me, sibling, chips, copy, mine, first = self._copies(x_refs, out_refs, sems)
        passed = [copy(4 + j, a, (*chip, c), sibling) for j, chip in enumerate(chips) for a in range(n)]
        for a in range(n):
            copy(0, a, sibling, me).wait_recv()
            for j, chip in enumerate(chips):
                copy(4 + j, a, (*chip, 1 - c), me).wait_recv()
        for cp in first + passed:
            cp.wait_send()
        for cp in mine:
            cp.wait()


class _ScatterRider(_NoRelay):
    def __init__(self, gs):
        self.inputs = list(gs)
        n = len(gs)
        self.out_shape = [jax.ShapeDtypeStruct(g.shape, g.dtype) for g in gs]
        self.scratch = [pltpu.SemaphoreType.DMA((7, n)), pltpu.SemaphoreType.DMA((7, n)),
                        pltpu.SemaphoreType.DMA((n,))]
        self.results = None

    def _copies(self, g_refs, out_refs, sems):
        send_sems, recv_sems, local_sems = sems
        x, y, c = _my_place()
        me = 4 * x + 2 * y + c
        mine, copies = [], []
        for a in range(len(g_refs)):
            mine.append(pltpu.make_async_copy(g_refs[a].at[me], out_refs[a].at[me], local_sems.at[a]))
            for k in range(1, N_DEV):
                px, py, pc = _flip(x, k & 4), _flip(y, k & 2), _flip(c, k & 1)
                copies.append(pltpu.make_async_remote_copy(
                    src_ref=g_refs[a].at[4 * px + 2 * py + pc], dst_ref=out_refs[a].at[me],
                    send_sem=send_sems.at[k - 1, a], recv_sem=recv_sems.at[k - 1, a],
                    device_id=(px, py, pc), device_id_type=MESH))
        return mine, copies

    def start(self, g_refs, out_refs, sems):
        mine, copies = self._copies(g_refs, out_refs, sems)
        for cp in mine + copies:
            cp.start()

    def finish(self, g_refs, out_refs, sems):
        mine, copies = self._copies(g_refs, out_refs, sems)
        for cp in copies + mine:
            cp.wait()


class _SiblingSwapRider(_NoRelay):
    def __init__(self, gs):
        self.inputs = list(gs)
        n = len(gs)
        self.out_shape = [jax.ShapeDtypeStruct((4,) + g.shape[1:], g.dtype) for g in gs]
        self.scratch = [pltpu.SemaphoreType.DMA((4, n)), pltpu.SemaphoreType.DMA((4, n))]
        self.results = None

    def _copies(self, g_refs, out_refs, sems):
        send_sems, recv_sems = sems
        x, y, c = _my_place()
        return [pltpu.make_async_remote_copy(
            src_ref=g_refs[a].at[2 * k + 1 - c], dst_ref=out_refs[a].at[k],
            send_sem=send_sems.at[k, a], recv_sem=recv_sems.at[k, a],
            device_id=(x, y, 1 - c), device_id_type=MESH)
            for a in range(len(g_refs)) for k in range(4)]

    def start(self, g_refs, out_refs, sems):
        for cp in self._copies(g_refs, out_refs, sems):
            cp.start()

    def finish(self, g_refs, out_refs, sems):
        for cp in self._copies(g_refs, out_refs, sems):
            cp.wait()


class _ChipScatterRider(_NoRelay):
    def __init__(self, ps):
        self.inputs = list(ps)
        n = len(ps)
        self.out_shape = [jax.ShapeDtypeStruct(p.shape, p.dtype) for p in ps]
        self.scratch = [pltpu.SemaphoreType.DMA((3, n)), pltpu.SemaphoreType.DMA((3, n)),
                        pltpu.SemaphoreType.DMA((n,))]
        self.results = None

    def _copies(self, p_refs, out_refs, sems):
        send_sems, recv_sems, local_sems = sems
        x, y, c = _my_place()
        my_chip = 2 * x + y
        chips = [(1 - x, y), (x, 1 - y), (1 - x, 1 - y)]
        n = len(p_refs)
        mine = [pltpu.make_async_copy(p_refs[a].at[my_chip], out_refs[a].at[my_chip], local_sems.at[a])
                for a in range(n)]
        copies = [pltpu.make_async_remote_copy(
            src_ref=p_refs[a].at[2 * cx + cy], dst_ref=out_refs[a].at[my_chip],
            send_sem=send_sems.at[j, a], recv_sem=recv_sems.at[j, a],
            device_id=(cx, cy, c), device_id_type=MESH)
            for a in range(n) for j, (cx, cy) in enumerate(chips)]
        return mine, copies

    def start(self, p_refs, out_refs, sems):
        mine, copies = self._copies(p_refs, out_refs, sems)
        for cp in mine + copies:
            cp.start()

    def finish(self, p_refs, out_refs, sems):
        mine, copies = self._copies(p_refs, out_refs, sems)
        for cp in copies + mine:
            cp.wait()


def _call(body, name, grid, in_specs, out_specs, out_shape, scratch, semantics, args, rider=None):
    in_specs, out_specs, out_shape, scratch = list(in_specs), list(out_specs), list(out_shape), list(scratch)
    if rider is None:
        return list(pl.pallas_call(
            body, name=name, grid=grid, in_specs=in_specs, out_specs=out_specs, out_shape=out_shape,
            scratch_shapes=scratch, compiler_params=_params(*semantics))(*args))
    n_in, n_out, n_scr = len(in_specs), len(out_specs), len(scratch)
    r_in, r_out = len(rider.inputs), len(rider.out_shape)

    def wrapped(*refs):
        cuts = np.cumsum([0, n_in, r_in, n_out, r_out, n_scr])
        hi, ri, ho, ro, hs = (refs[cuts[i]:cuts[i + 1]] for i in range(5))
        rs = refs[cuts[5]:]
        step, steps = pl.program_id(0), grid[0]
        for d in range(1, len(grid)):
            step, steps = step * grid[d] + pl.program_id(d), steps * grid[d]

        @pl.when(step == 0)
        def _():
            rider.start(ri, ro, rs)

        body(*hi, *ho, *hs)

        @pl.when(step == (steps * RELAY_AT_NUM) // RELAY_AT_DEN)
        def _():
            rider.relay(ri, ro, rs)

        @pl.when(step == steps - 1)
        def _():
            rider.finish(ri, ro, rs)

    outs = pl.pallas_call(
        wrapped, name=name, grid=grid,
        in_specs=in_specs + [ANY] * r_in, out_specs=out_specs + [ANY] * r_out,
        out_shape=out_shape + rider.out_shape, scratch_shapes=scratch + rider.scratch,
        compiler_params=_params(*(["arbitrary"] * len(grid))),
    )(*args, *rider.inputs)
    rider.results = list(outs[n_out:])
    return list(outs[:n_out])


_WALK = ((None, None), (0, None), (1, 4), (2, 5), (4, None), (5, None), (3, 6), (6, None))


def _gather_order():
    x, y, c = _my_place()
    (ax, ay), (bx, by), (dx, dy) = (1 - x, y), (x, 1 - y), (1 - x, 1 - y)
    ids = [(x, y, c), (x, y, 1 - c), (ax, ay, c), (bx, by, c), (ax, ay, 1 - c), (bx, by, 1 - c),
           (dx, dy, c), (dx, dy, 1 - c)]
    return jnp.stack([4 * px + 2 * py + pc for px, py, pc in ids]).astype(jnp.int32)


def _proj_gather(x, norm_g, w_shard, extras, name):
    t, d = x.shape
    cols = w_shard.shape[1]
    tm = _pick(t, MM_CAP_MN, 16)
    ni = t // tm
    n = 1 + len(extras)
    rider = _GatherRider([w_shard] + list(extras))

    def body(ord_ref, x_ref, g_ref, *refs):
        sh_refs, proj_ref, gathered = refs[:n], refs[n], refs[n + 1:2 * n + 1]
        h_all, bbuf, bsem, send_sems, recv_sems, local_sems = refs[2 * n + 1:]
        j, i = pl.program_id(0), pl.program_id(1)
        _, c, me, sibling, chips, copy, mine, first = rider._copies(
            sh_refs, gathered, (send_sems, recv_sems, local_sems))
        rows = pl.ds(pl.multiple_of(i * tm, tm), tm)

        def load(step, src):
            return pltpu.make_async_copy(src, bbuf.at[step % 2], bsem.at[step % 2])

        def relayed(k, a):
            return copy(k, a, (*chips[k - 4], c), sibling)

        @pl.when(jnp.logical_and(j == 0, i == 0))
        def _():
            for cp in mine + first:
                cp.start()
            load(0, sh_refs[0]).start()

        @pl.when(i == 0)
        def _():
            load(j, sh_refs[0]).wait()

        @pl.when(j == 0)
        def _():
            groups = []
            for r in range(0, tm, NORM_ROWS):
                xv = x_ref[r:r + NORM_ROWS, :]
                rstd = lax.rsqrt(jnp.mean(xv * xv, axis=-1, keepdims=True) + EPS)
                groups.append((xv * rstd * g_ref[...]).astype(BF16))
            h_all[rows, :] = jnp.concatenate(groups, axis=0)

        proj_ref[...] = _dot(h_all[rows, :], bbuf[j % 2])

        for step in range(N_DEV - 1):
            @pl.when(jnp.logical_and(j == step, i == max(ni - 2, 0)))
            def _(step=step):
                need, relay = _WALK[step + 1]
                copy(need, 0, me, me).wait_recv()
                if relay is not None:
                    relayed(relay, 0).start()
                load(step + 1, gathered[0].at[ord_ref[step + 1]]).start()

        @pl.when(jnp.logical_and(j == N_DEV - 1, i == ni - 1))
        def _():
            for a in range(1, n):
                for k in range(3):
                    copy(1 + k, a, me, me).wait_recv()
                    relayed(4 + k, a).start()
            for a in range(1, n):
                for k in (0, 4, 5, 6):
                    copy(k, a, me, me).wait_recv()
            for cp in first + [relayed(4 + k, a) for a in range(n) for k in range(3)]:
                cp.wait_send()
            for cp in mine:
                cp.wait()

    outs = pl.pallas_call(
        body, name=name,
        grid_spec=pltpu.PrefetchScalarGridSpec(
            num_scalar_prefetch=1, grid=(N_DEV, ni),
            in_specs=[pl.BlockSpec((tm, d), lambda j, i, o: (jnp.where(j == 0, i, ni - 1), 0)),
                      pl.BlockSpec((1, d), lambda j, i, o: (0, 0))] + [ANY] * n,
            out_specs=[pl.BlockSpec((tm, cols), lambda j, i, o: (i, o[j]))] + [ANY] * n,
            scratch_shapes=[pltpu.VMEM((t, d), BF16), pltpu.VMEM((2, d, cols), BF16),
                            pltpu.SemaphoreType.DMA((2,))] + rider.scratch),
        out_shape=[jax.ShapeDtypeStruct((t, N_DEV * cols), F32)] + rider.out_shape,
        compiler_params=_params("arbitrary", "arbitrary"),
    )(_gather_order(), x, norm_g, w_shard, *extras)
    return outs[0], list(outs[1:])


def _exchange(rider, name):
    r_in, r_out = len(rider.inputs), len(rider.out_shape)

    def body(*refs):
        ri, ro, rs = refs[:r_in], refs[r_in:r_in + r_out], refs[r_in + r_out:]
        rider.start(ri, ro, rs)
        rider.relay(ri, ro, rs)
        rider.finish(ri, ro, rs)

    return list(pl.pallas_call(
        body, name=name, in_specs=[ANY] * r_in, out_specs=[ANY] * r_out,
        out_shape=rider.out_shape, scratch_shapes=rider.scratch)(*rider.inputs))


MM_CAP_MN = 1024
MM_CAP_M_GRAD = 1408
MM_CAP_N = 1536
MM_CAP_K = 3072
MM_CAP_K_TOKENS = 2048
MM_CAP_K_RMS = 8192
MM_CAP_M_RMS = 512
NORM_ROWS = 256
RING_SLOTS = 3


def _mm(a, b, mode, name, out_dtype=F32, res=None, out_block=None, epilogue=None, extra=None, norm_g=None,
        norm_b=False, rider=None):
    a3, b3 = a.ndim == 3, b.ndim == 3
    um = un = uk = None
    if mode in ("nn", "nt"):
        if a3:
            m, uk = a.shape[1:]
            k = a.shape[0] * uk
        else:
            m, k = a.shape
    else:
        if a3:
            k, um = a.shape[1:]
            m = a.shape[0] * um
        else:
            k, m = a.shape
    if mode in ("nn", "tn"):
        if b3:
            kb, un = b.shape[1:]
            n = b.shape[0] * un
        else:
            kb, n = b.shape
        assert kb == k, (a.shape, b.shape, mode)
    else:
        if b3:
            n, ukb = b.shape[1:]
            assert b.shape[0] * ukb == k and uk in (None, ukb), (a.shape, b.shape, mode)
            uk = ukb
        else:
            n, kb = b.shape
            assert kb == k, (a.shape, b.shape, mode)
    if out_block is not None:
        assert un in (None, out_block)
        un = out_block

    def tile(dim, unit, cap, align):
        if unit is None:
            return _pick(dim, cap, align), 1
        c = max(1, cap // unit)
        while (dim // unit) % c:
            c -= 1
        return unit, c

    cap_m = MM_CAP_M_GRAD if mode == "tn" else (MM_CAP_M_RMS if epilogue == "rms_bwd" else MM_CAP_MN)
    um, cm = tile(m, um, cap_m, 128 if mode == "tn" else 16)
    un, cn = tile(n, un, MM_CAP_N, 128)
    cap_k = MM_CAP_K_TOKENS if mode == "tn" else (MM_CAP_K_RMS if epilogue == "rms_bwd" else MM_CAP_K)
    uk, ck = tile(k, uk, cap_k, 128)
    if epilogue == "rms_bwd":
        assert mode != "tn" and n == D_MODEL and cm == cn == 1 and res is None and out_block is None
    if epilogue == "loss":
        assert n == D_MODEL and cm == cn == 1 and res is not None and out_block is None
    if norm_g is not None and norm_b:
        assert mode == "tn" and not b3 and n == D_MODEL and cn == 1
    elif norm_g is not None:
        assert not a3 and (m if mode == "tn" else k) == D_MODEL and (cm if mode == "tn" else ck) == 1
    if epilogue == "swiglu":
        assert res is None and ((mode == "nn" and b3 and out_block is None) or
                                (mode == "nt" and not b3 and out_block is not None))
        cn = 2
    if epilogue == "swiglu_bwd":
        assert mode == "nt" and out_block is not None and extra is not None and res is None
        cn = 1
    tm, tn, tk = cm * um, cn * un, ck * uk
    nk = k // tk
    ringed = epilogue == "swiglu_bwd"
    assert not ringed or nk == 1
    dot = {"nn": _dot, "nt": _dot_nt, "tn": _dot_tn}[mode]
    half = n // un // 2
    blocked_out = out_block is not None or epilogue in ("swiglu", "swiglu_bwd")
    extras = [] if extra is None else (list(extra) if isinstance(extra, (tuple, list)) else [extra])

    def sl(idx, unit, count):
        return slice(None) if count == 1 else slice(idx * unit, (idx + 1) * unit)

    def body(*refs):
        a_ref, b_ref = refs[0], refs[1]
        pos = 2
        r_ref = ng_ref = None
        if res is not None:
            r_ref, pos = refs[pos], pos + 1
        e_refs, pos = refs[pos:pos + len(extras)], pos + len(extras)
        if norm_g is not None:
            ng_ref, pos = refs[pos], pos + 1
        kk = pl.program_id(2)
        if ringed:
            outs, acc_ref, ring_ref, ring_sem = refs[pos:-3], refs[-3], refs[-2], refs[-1]
            nj = n // tn
            step, steps = pl.program_id(0) * nj + pl.program_id(1), (m // tm) * nj

            def gu_copies(s):
                row = s // nj * tm
                row = row if isinstance(row, int) else pl.multiple_of(row, tm)
                return [pltpu.make_async_copy(e_refs[0].at[h, s % nj, pl.ds(row, tm), :],
                                              ring_ref.at[s % RING_SLOTS, h], ring_sem.at[s % RING_SLOTS, h])
                        for h in range(2)]

            @pl.when(step == 0)
            def _():
                for s in range(min(RING_SLOTS - 1, steps)):
                    for cp in gu_copies(s):
                        cp.start()

            @pl.when(step + (RING_SLOTS - 1) < steps)
            def _():
                for cp in gu_copies(step + (RING_SLOTS - 1)):
                    cp.start()

            for cp in gu_copies(step):
                cp.wait()

        else:
            outs, acc_ref = refs[pos:-1], refs[-1]

        def normed(x_ref):
            groups = []
            for r in range(0, x_ref.shape[0], NORM_ROWS):
                xv = x_ref[r:r + NORM_ROWS, :]
                rstd = lax.rsqrt(jnp.mean(xv * xv, axis=-1, keepdims=True) + EPS)
                groups.append((xv * rstd * ng_ref[...]).astype(BF16))
            return jnp.concatenate(groups, axis=0)

        def a_blk(mi, ki):
            if norm_g is not None and not norm_b:
                return normed(a_ref)
            if mode in ("nn", "nt"):
                return a_ref[ki] if a3 else a_ref[:, sl(ki, uk, ck)]
            return a_ref[mi] if a3 else a_ref[:, sl(mi, um, cm)]

        def b_blk(ki, ni):
            if norm_b:
                return normed(b_ref)
            if epilogue == "swiglu":
                return b_ref[ni, 0]
            if mode in ("nn", "tn"):
                return b_ref[ni] if b3 else b_ref[sl(ki, uk, ck), sl(ni, un, cn)]
            return b_ref[ki][sl(ni, un, cn), :] if b3 else b_ref[sl(ni, un, cn), sl(ki, uk, ck)]

        parts = {}
        for mi in range(cm):
            for ni in range(cn):
                part = None
                for ki in range(ck):
                    d = dot(a_blk(mi, ki).astype(BF16), b_blk(ki, ni).astype(BF16))
                    part = d if part is None else part + d
                parts[mi, ni] = part

        def finish(total):
            if epilogue == "swiglu":
                gate, up = total[0, 0], total[0, 1]
                outs[0][0, 0] = gate.astype(BF16)
                outs[0][1, 0] = up.astype(BF16)
                outs[1][0] = (gate * _sigmoid(gate) * up).astype(BF16)
                return
            if epilogue == "swiglu_bwd":
                dact = total[0, 0]
                gate, up = (ring_ref[step % RING_SLOTS, h].astype(F32) for h in range(2))
                sg = _sigmoid(gate)
                outs[0][0, 0] = (dact * up * (sg * (1.0 + gate * (1.0 - sg)))).astype(BF16)
                outs[0][1, 0] = (dact * (gate * sg)).astype(BF16)
                return
            if epilogue == "rms_bwd":
                x_ref, g_ref, dres_ref = e_refs
                dh, dg = total[0, 0], None
                for r in range(0, tm, NORM_ROWS):
                    rows = slice(r, r + NORM_ROWS)
                    xv, dhv = x_ref[rows, :], dh[rows, :]
                    rstd = lax.rsqrt(jnp.mean(xv * xv, axis=-1, keepdims=True) + EPS)
                    xh = xv * rstd
                    dyg = dhv * g_ref[...]
                    c = jnp.mean(dyg * xh, axis=-1, keepdims=True)
                    outs[0][rows, :] = dres_ref[rows, :] + rstd * (dyg - xh * c)
                    part = jnp.sum(dhv * xh, axis=0, keepdims=True)
                    dg = part if dg is None else dg + part
                _accumulate(outs[1], dg, pl.program_id(0))
                return
            if epilogue == "loss":
                diff = r_ref[...] + total[0, 0] - e_refs[0][...]
                outs[0][...] = diff * (1.0 / n)
                sq = jnp.sum(jnp.sum(diff * diff, axis=-1, keepdims=True), axis=0, keepdims=True)
                _accumulate(outs[1], sq * (0.5 / n), pl.program_id(0))
                return
            for (mi, ni), val in total.items():
                rows, cols = sl(mi, um, cm), sl(ni, un, cn)
                if res is not None:
                    val = r_ref[rows, cols] + val
                if blocked_out:
                    outs[0][ni, rows] = val.astype(out_dtype)
                else:
                    outs[0][rows, cols] = val.astype(out_dtype)

        if nk == 1:
            finish(parts)
        else:
            @pl.when(kk == 0)
            def _():
                for (mi, ni), val in parts.items():
                    acc_ref[mi * cn + ni] = val

            @pl.when(jnp.logical_and(kk > 0, kk < nk - 1))
            def _():
                for (mi, ni), val in parts.items():
                    acc_ref[mi * cn + ni] += val

            @pl.when(kk == nk - 1)
            def _():
                finish({key: acc_ref[key[0] * cn + key[1]] + val for key, val in parts.items()})

    if mode in ("nn", "nt"):
        a_spec = (pl.BlockSpec((ck, tm, uk), lambda i, j, kk: (kk, i, 0)) if a3
                  else pl.BlockSpec((tm, tk), lambda i, j, kk: (i, kk)))
    else:
        a_spec = (pl.BlockSpec((cm, tk, um), lambda i, j, kk: (i, kk, 0)) if a3
                  else pl.BlockSpec((tk, tm), lambda i, j, kk: (kk, i)))
    pair_spec = pl.BlockSpec((2, 1, tm, un), lambda i, j, kk: (0, j, i, 0))
    row_spec = pl.BlockSpec((tm, tn), lambda i, j, kk: (i, 0))
    vec_spec = pl.BlockSpec((1, tn), lambda i, j, kk: (0, 0))
    if epilogue == "swiglu" and mode == "nn":
        b = b.reshape(2, half, k, un)
        b_spec = pl.BlockSpec((2, 1, tk, un), lambda i, j, kk: (0, j, kk, 0))
    elif epilogue == "swiglu":
        b = b.reshape(2, half, un, k)
        b_spec = pl.BlockSpec((2, 1, un, tk), lambda i, j, kk: (0, j, 0, kk))
    elif mode in ("nn", "tn"):
        b_spec = (pl.BlockSpec((cn, tk, un), lambda i, j, kk: (j, kk, 0)) if b3
                  else pl.BlockSpec((tk, tn), lambda i, j, kk: (kk, j)))
    else:
        b_spec = (pl.BlockSpec((ck, tn, uk), lambda i, j, kk: (kk, j, 0)) if b3
                  else pl.BlockSpec((tn, tk), lambda i, j, kk: (j, kk)))
    if epilogue == "swiglu":
        out_specs = [pair_spec, pl.BlockSpec((1, tm, un), lambda i, j, kk: (j, i, 0))]
        out_shape = [jax.ShapeDtypeStruct((2, half, m, un), BF16), jax.ShapeDtypeStruct((half, m, un), BF16)]
    elif epilogue == "swiglu_bwd":
        out_specs = [pair_spec]
        out_shape = [jax.ShapeDtypeStruct(extra.shape, BF16)]
    elif epilogue == "rms_bwd":
        out_specs = [row_spec, vec_spec]
        out_shape = [jax.ShapeDtypeStruct((m, n), F32), jax.ShapeDtypeStruct((1, n), F32)]
    elif epilogue == "loss":
        out_specs = [row_spec, pl.BlockSpec((1, 1), lambda i, j, kk: (0, 0))]
        out_shape = [jax.ShapeDtypeStruct((m, n), F32), jax.ShapeDtypeStruct((1, 1), F32)]
    elif blocked_out:
        out_specs = [pl.BlockSpec((cn, tm, un), lambda i, j, kk: (j, i, 0))]
        out_shape = [jax.ShapeDtypeStruct((n // un, m, un), out_dtype)]
    else:
        out_specs = [pl.BlockSpec((tm, tn), lambda i, j, kk: (i, j))]
        out_shape = [jax.ShapeDtypeStruct((m, n), out_dtype)]
    in_specs, args = [a_spec, b_spec], [a, b]
    if res is not None:
        in_specs.append(pl.BlockSpec((tm, tn), lambda i, j, kk: (i, j)))
        args.append(res)
    if ringed:
        in_specs.append(ANY)
    elif epilogue == "rms_bwd":
        in_specs += [row_spec, vec_spec, row_spec]
    elif epilogue == "loss":
        in_specs.append(row_spec)
    args += extras
    if norm_g is not None:
        in_specs.append(pl.BlockSpec((1, D_MODEL), lambda i, j, kk: (0, 0)))
        args.append(norm_g)
    ordered = ringed or epilogue in ("rms_bwd", "loss")
    semantics = ("arbitrary",) * 3 if ordered else ("parallel", "parallel", "arbitrary")
    scratch = [pltpu.VMEM((cm * cn, um, un), F32)]
    if ringed:
        scratch += [pltpu.VMEM((RING_SLOTS, 2, tm, un), BF16), pltpu.SemaphoreType.DMA((RING_SLOTS, 2))]
    out = _call(body, name, (m // tm, n // tn, nk), in_specs, out_specs, out_shape, scratch, semantics, args, rider)
    return out if epilogue in ("swiglu", "rms_bwd", "loss") else out[0]


def _head_sums(v, ind):
    return _dot(v.astype(BF16), ind)


def _head_spread(per_head, ind):
    hi, lo = _split2(per_head)
    return _dot_nt(hi, ind) + _dot_nt(lo, ind)


def _head_rstd(xv, ind):
    return _head_spread(lax.rsqrt(_head_sums(xv * xv, ind) * (1.0 / ATT_DH) + EPS), ind)


def _hn_bwd_math(xv, gv, ind, dyv, scale):
    rstd = _head_rstd(xv, ind)
    xh = xv * rstd
    dyn = dyv * scale
    dyg = dyn * gv
    dx = rstd * (dyg - xh * _head_spread(_head_sums(dyg * xh, ind) * (1.0 / ATT_DH), ind))
    return dx, jnp.sum(dyn * xh, axis=0, keepdims=True)


def _q_hnorm(x, g_tiled, bd, scale, name):
    t, d = x.shape
    tm = _pick(t, 512, 16)

    def body(x_ref, g_ref, bd_ref, o_ref):
        xv = x_ref[...]
        o_ref[...] = (xv * _head_rstd(xv, bd_ref[...]) * g_ref[...] * scale).astype(BF16)

    return pl.pallas_call(
        body, name=name, grid=(t // tm,),
        in_specs=[pl.BlockSpec((tm, d), lambda i: (i, 0)), pl.BlockSpec((1, d), lambda i: (0, 0)),
                  pl.BlockSpec((d, LANES), lambda i: (0, 0))],
        out_specs=pl.BlockSpec((tm, d), lambda i: (i, 0)),
        out_shape=jax.ShapeDtypeStruct((t, d), BF16),
        compiler_params=_params("parallel"),
    )(x, g_tiled, bd)


def _q_dhnorm(x, g_tiled, bd, dy, scale, name):
    t, d = x.shape
    tm = _pick(t, 512, 16)

    def body(x_ref, g_ref, bd_ref, dy_ref, dx_ref, dg_ref):
        dx, part = _hn_bwd_math(x_ref[...], g_ref[...], bd_ref[...], dy_ref[...], scale)
        dx_ref[...] = dx.astype(BF16)
        _accumulate(dg_ref, part, pl.program_id(0))

    row = pl.BlockSpec((tm, d), lambda i: (i, 0))
    vec = pl.BlockSpec((1, d), lambda i: (0, 0))
    return pl.pallas_call(
        body, name=name, grid=(t // tm,),
        in_specs=[row, vec, pl.BlockSpec((d, LANES), lambda i: (0, 0)), row],
        out_specs=[row, vec],
        out_shape=[jax.ShapeDtypeStruct((t, d), BF16), jax.ShapeDtypeStruct((1, d), F32)],
        compiler_params=_params("arbitrary"),
    )(x, g_tiled, bd, dy)


def _kv_prep(kv, g_tiled, bd, name):
    t = kv.shape[0]
    d = D_MODEL
    tm = K_PAD
    assert t % tm == 0

    def body(k_ref, v_ref, g_ref, bd_ref, kp_ref, vp_ref):
        i = pl.program_id(0)

        @pl.when(i == 0)
        def _():
            kp_ref[...] = jnp.zeros_like(kp_ref)
            vp_ref[...] = jnp.zeros_like(vp_ref)

        @pl.when(i > 0)
        def _():
            xv = k_ref[...]
            kp_ref[...] = (xv * _head_rstd(xv, bd_ref[...]) * g_ref[...]).astype(BF16)
            vp_ref[...] = v_ref[...].astype(BF16)

    shp = jax.ShapeDtypeStruct((t + K_PAD, d), BF16)
    out = pl.BlockSpec((tm, d), lambda i: (i, 0))
    return pl.pallas_call(
        body, name=name, grid=(t // tm + 1,),
        in_specs=[pl.BlockSpec((tm, d), lambda i: (jnp.maximum(i - 1, 0), 0)),
                  pl.BlockSpec((tm, d), lambda i: (jnp.maximum(i - 1, 0), 1)),
                  pl.BlockSpec((1, d), lambda i: (0, 0)), pl.BlockSpec((d, LANES), lambda i: (0, 0))],
        out_specs=[out, out], out_shape=[shp, shp],
        compiler_params=_params("arbitrary"),
    )(kv, kv, g_tiled, bd)


def _kv_dprep(kv, g_tiled, bd, dkp_t, dvp_t, name):
    t = kv.shape[0]
    d = D_MODEL
    tm = K_PAD

    def body(k_ref, g_ref, bd_ref, dk_ref, dv_ref, o_ref, dg_ref):
        dx, part = _hn_bwd_math(k_ref[...], g_ref[...], bd_ref[...], dk_ref[...].T, 1.0)
        o_ref[:, :d] = dx.astype(BF16)
        o_ref[:, d:] = dv_ref[...].T.astype(BF16)
        _accumulate(dg_ref, part, pl.program_id(0))

    vec = pl.BlockSpec((1, d), lambda i: (0, 0))
    padded = pl.BlockSpec((d, tm), lambda i: (0, i + 1))
    return pl.pallas_call(
        body, name=name, grid=(t // tm,),
        in_specs=[pl.BlockSpec((tm, d), lambda i: (i, 0)), vec, pl.BlockSpec((d, LANES), lambda i: (0, 0)),
                  padded, padded],
        out_specs=[pl.BlockSpec((tm, 2 * d), lambda i: (i, 0)), vec],
        out_shape=[jax.ShapeDtypeStruct((t, 2 * d), BF16), jax.ShapeDtypeStruct((1, d), F32)],
        compiler_params=_params("arbitrary"),
    )(kv, g_tiled, bd, dkp_t, dvp_t)


def _ret_consts(t):
    h = np.arange(RET_HEADS, dtype=np.float32)
    lg = np.log(np.float32(1.0) - np.float32(2.0) ** (np.float32(-5.0) - h)).astype(np.float32)
    tt = np.arange(CHUNK, dtype=np.float32)
    intra = np.exp(lg[:, None, None] * np.abs(tt[:, None] - tt[None, :])).astype(np.float32)
    q_dec = np.exp(lg[:, None] * (tt + 1.0)).astype(np.float32)
    k_dec = np.exp(lg[:, None] * (CHUNK - 1.0 - tt)).astype(np.float32)
    s_dec = [float(v) for v in np.exp(lg * np.float32(CHUNK)).astype(np.float32)]
    qd = np.broadcast_to(q_dec[:, :, None], (RET_HEADS, CHUNK, RET_DK)).copy()
    kd = np.broadcast_to(k_dec[:, :, None], (RET_HEADS, CHUNK, RET_DK)).copy()
    half = RET_DK // 2
    inv_freq = np.float32(ROPE_BASE) ** (-np.arange(half, dtype=np.float32) / np.float32(half))
    ang = np.arange(t, dtype=np.float32)[:, None] * inv_freq[None, :]
    return jnp.asarray(intra), jnp.asarray(qd), jnp.asarray(kd), s_dec, jnp.asarray(np.cos(ang)), jnp.asarray(np.sin(ang))


def _rope(x, cos, sin):
    half = RET_DK // 2
    x1, x2 = x[:, :half], x[:, half:]
    return jnp.concatenate([x1 * cos - x2 * sin, x1 * sin + x2 * cos], axis=-1)


def _unrope(d, cos, sin):
    half = RET_DK // 2
    d1, d2 = d[:, :half], d[:, half:]
    return jnp.concatenate([d1 * cos + d2 * sin, d2 * cos - d1 * sin], axis=-1)


def _ret_slices(h):
    q = slice(h * RET_DK, (h + 1) * RET_DK)
    k = slice(RET_Q_COLS + h * RET_DK, RET_Q_COLS + (h + 1) * RET_DK)
    v = slice(2 * RET_Q_COLS + h * RET_DV, 2 * RET_Q_COLS + (h + 1) * RET_DV)
    g = slice(2 * RET_Q_COLS + RET_V_COLS + h * RET_DV, 2 * RET_Q_COLS + RET_V_COLS + (h + 1) * RET_DV)
    o = slice(h * RET_DV, (h + 1) * RET_DV)
    return q, k, v, g, o


def _ret_fwd(proj, gn, consts, name, rider=None):
    t, cols = proj.shape
    n = t // CHUNK
    intra, qd, kd, s_dec, cos, sin = consts
    k_scale = RET_DK ** -0.5

    def body(p_hbm, cos_ref, sin_ref, intra_ref, qd_ref, kd_ref, gn_ref, y_ref, o_ref, st_ref, state,
             ring_ref, ring_sem):
        i = pl.program_id(0)

        @pl.when(i == 0)
        def _():
            state[...] = jnp.zeros_like(state)

        def p_copy(s):
            first = s * step if isinstance(s, int) else pl.multiple_of(s * step, step)
            return pltpu.make_async_copy(p_hbm.at[pl.ds(first, step), :], ring_ref.at[s % RING_SLOTS],
                                         ring_sem.at[s % RING_SLOTS])

        @pl.when(i == 0)
        def _():
            for s in range(min(RING_SLOTS - 1, steps)):
                p_copy(s).start()

        @pl.when(i + (RING_SLOTS - 1) < steps)
        def _():
            p_copy(i + (RING_SLOTS - 1)).start()

        p_copy(i).wait()
        p_ref = ring_ref.at[i % RING_SLOTS]

        for c in range(RET_STEP):
            rows = slice(c * CHUNK, (c + 1) * CHUNK)
            cosv, sinv = cos_ref[rows, :], sin_ref[rows, :]
            for h in range(RET_HEADS):
                qs, ks, vs, gs, os_ = _ret_slices(h)
                qr = _rope(p_ref[rows, qs], cosv, sinv)
                kr = _rope(p_ref[rows, ks], cosv, sinv) * k_scale
                vb = p_ref[rows, vs].astype(BF16)
                gv = p_ref[rows, gs]
                scores = _dot_nt(qr.astype(BF16), kr.astype(BF16)) * intra_ref[h]
                s_old = state[h]
                s_old_b = s_old.astype(BF16)
                st_ref[c, h] = s_old_b
                o = _dot(scores.astype(BF16), vb) + _dot((qr * qd_ref[h]).astype(BF16), s_old_b)
                state[h] = s_old * s_dec[h] + _dot_tn((kr * kd_ref[h]).astype(BF16), vb)
                rstd = lax.rsqrt(jnp.mean(o * o, axis=-1, keepdims=True) + EPS)
                on = o * rstd * gn_ref[:, os_]
                o_ref[rows, os_] = o
                y_ref[rows, os_] = (gv * _sigmoid(gv) * on).astype(BF16)

    full3 = lambda a: pl.BlockSpec(a.shape, lambda i: (0, 0, 0))
    step, steps = RET_STEP * CHUNK, n // RET_STEP
    return _call(
        body, name, (steps,),
        [ANY,
         pl.BlockSpec((step, RET_DK // 2), lambda i: (i, 0)),
         pl.BlockSpec((step, RET_DK // 2), lambda i: (i, 0)),
         full3(intra), full3(qd), full3(kd),
         pl.BlockSpec((1, RET_V_COLS), lambda i: (0, 0))],
        [pl.BlockSpec((step, RET_V_COLS), lambda i: (i, 0)),
         pl.BlockSpec((step, RET_V_COLS), lambda i: (i, 0)),
         pl.BlockSpec((RET_STEP, RET_HEADS, RET_DK, RET_DV), lambda i: (i, 0, 0, 0))],
        [jax.ShapeDtypeStruct((t, RET_V_COLS), BF16),
         jax.ShapeDtypeStruct((t, RET_V_COLS), F32),
         jax.ShapeDtypeStruct((n, RET_HEADS, RET_DK, RET_DV), BF16)],
        [pltpu.VMEM((RET_HEADS, RET_DK, RET_DV), F32), pltpu.VMEM((RING_SLOTS, step, cols), proj.dtype),
         pltpu.SemaphoreType.DMA((RING_SLOTS,))], ("arbitrary",),
        (proj, cos, sin, intra, qd, kd, gn), rider)


def _ret_bwd(proj, gn, o_saved, states, dy, consts, name, rider=None):
    t, cols = proj.shape
    n = t // CHUNK
    intra, qd, kd, s_dec, cos, sin = consts
    k_scale = RET_DK ** -0.5

    def body(p_ref, cos_ref, sin_ref, intra_ref, qd_ref, kd_ref, gn_ref, o_ref, st_ref, dy_ref,
             dp_ref, dgn_ref, dstate):
        i = pl.program_id(0)

        @pl.when(i == 0)
        def _():
            dstate[...] = jnp.zeros_like(dstate)

        dgn = None
        for c in reversed(range(RET_STEP)):
            rows = slice(c * CHUNK, (c + 1) * CHUNK)
            cosv, sinv = cos_ref[rows, :], sin_ref[rows, :]
            dgn_parts = []
            for h in range(RET_HEADS):
                qs, ks, vs, gs, os_ = _ret_slices(h)
                qr = _rope(p_ref[rows, qs], cosv, sinv)
                kr = _rope(p_ref[rows, ks], cosv, sinv) * k_scale
                qb, kb = qr.astype(BF16), kr.astype(BF16)
                vb = p_ref[rows, vs].astype(BF16)
                gv = p_ref[rows, gs]
                ov = o_ref[rows, os_]
                dyv = dy_ref[rows, os_]
                gnv = gn_ref[:, os_]
                sg = _sigmoid(gv)
                rstd = lax.rsqrt(jnp.mean(ov * ov, axis=-1, keepdims=True) + EPS)
                oh = ov * rstd
                d_on = dyv * (gv * sg)
                dg = dyv * (oh * gnv) * (sg * (1.0 + gv * (1.0 - sg)))
                dgn_parts.append(jnp.sum(d_on * oh, axis=0, keepdims=True))
                d_oh = d_on * gnv
                do = rstd * (d_oh - oh * jnp.mean(d_oh * oh, axis=-1, keepdims=True))
                dob = do.astype(BF16)
                mask = intra_ref[h]
                a_b = (_dot_nt(qb, kb) * mask).astype(BF16)
                da_b = (_dot_nt(dob, vb) * mask).astype(BF16)
                ds_new = dstate[h]
                ds_new_b = ds_new.astype(BF16)
                s_old_b = st_ref[c, h]
                qdv, kdv = qd_ref[h], kd_ref[h]
                dv = _dot_tn(a_b, dob) + _dot((kr * kdv).astype(BF16), ds_new_b)
                dqr = _dot(da_b, kb) + _dot_nt(dob, s_old_b) * qdv
                dkr = _dot_tn(da_b, qb) + _dot_nt(vb, ds_new_b) * kdv
                dstate[h] = ds_new * s_dec[h] + _dot_tn((qr * qdv).astype(BF16), dob)
                dp_ref[rows, qs] = _unrope(dqr, cosv, sinv).astype(BF16)
                dp_ref[rows, ks] = _unrope(dkr * k_scale, cosv, sinv).astype(BF16)
                dp_ref[rows, vs] = dv.astype(BF16)
                dp_ref[rows, gs] = dg.astype(BF16)
            part = jnp.concatenate(dgn_parts, axis=-1)
            dgn = part if dgn is None else dgn + part
        _accumulate(dgn_ref, dgn, i)

    steps = n // RET_STEP
    step = RET_STEP * CHUNK
    rev = lambda i: (steps - 1 - i, 0)
    full3 = lambda a: pl.BlockSpec(a.shape, lambda i: (0, 0, 0))
    return _call(
        body, name, (steps,),
        [pl.BlockSpec((step, cols), rev),
         pl.BlockSpec((step, RET_DK // 2), rev),
         pl.BlockSpec((step, RET_DK // 2), rev),
         full3(intra), full3(qd), full3(kd),
         pl.BlockSpec((1, RET_V_COLS), lambda i: (0, 0)),
         pl.BlockSpec((step, RET_V_COLS), rev),
         pl.BlockSpec((RET_STEP, RET_HEADS, RET_DK, RET_DV), lambda i: (steps - 1 - i, 0, 0, 0)),
         pl.BlockSpec((step, RET_V_COLS), rev)],
        [pl.BlockSpec((step, cols), rev),
         pl.BlockSpec((1, RET_V_COLS), lambda i: (0, 0))],
        [jax.ShapeDtypeStruct((t, cols), BF16),
         jax.ShapeDtypeStruct((1, RET_V_COLS), F32)],
        [pltpu.VMEM((RET_HEADS, RET_DK, RET_DV), F32)], ("arbitrary",),
        (proj, cos, sin, intra, qd, kd, gn, o_saved, states, dy), rider)


def _att_common(q_ref, kp_ref, vp_ref, sub):
    blk = pl.program_id(1) * ATT_SUBS + sub
    start = pl.multiple_of(blk * Q_BLOCK, Q_BLOCK)
    kw = kp_ref[pl.ds(start, K_WINDOW), :]
    vw = vp_ref[pl.ds(start, K_WINDOW), :]
    kvalid = blk * Q_BLOCK - K_PAD + lax.broadcasted_iota(jnp.int32, (1, K_WINDOW), 1) >= 0
    lane = lax.broadcasted_iota(jnp.int32, (1, LANES), 1)
    qrows = slice(sub * Q_BLOCK, (sub + 1) * Q_BLOCK)
    return start, qrows, q_ref[qrows, :], kw, vw, kvalid, (lane < ATT_DH, lane >= ATT_DH)


def _row_groups():
    return [slice(r * ATT_ROWS, (r + 1) * ATT_ROWS) for r in range(Q_BLOCK // ATT_ROWS)]


def _lane_copies(x):
    return jnp.tile(x, (1, K_WINDOW // LANES))


def _att_specs(t, tp):
    qspec = pl.BlockSpec((ATT_SUBS * Q_BLOCK, LANES), lambda h, i: (i, h))
    kspec = pl.BlockSpec((tp, LANES), lambda h, i: (0, h))
    bspec = pl.BlockSpec((2, Q_BLOCK, K_WINDOW), lambda h, i: (h, 0, 0))
    return qspec, kspec, bspec


def _att_fwd(q, kp, vp, bias, name, rider=None):
    t, d = q.shape
    tp = kp.shape[0]

    def body(q_ref, kp_ref, vp_ref, bias_ref, o_ref, lse_ref, s_scr, p_scr, lse_scr, inv_scr):
        for sub in range(ATT_SUBS):
            _, qrows, q2, kw, vw, kvalid, sel = _att_common(q_ref, kp_ref, vp_ref, sub)
            for hh in range(2):
                s_scr[sub, hh] = _dot_nt(jnp.where(sel[hh], q2, 0), kw)
            for hh in range(2):
                for rows in _row_groups():
                    s = jnp.where(kvalid, s_scr[sub, hh, rows, :] + bias_ref[hh, rows, :], NEG)
                    m = jnp.max(s, axis=-1, keepdims=True)
                    e = jnp.exp(s - m)
                    l = jnp.sum(e, axis=-1, keepdims=True)
                    p_scr[sub, hh, rows, :] = e.astype(BF16)
                    inv_scr[sub, hh, rows, :] = jnp.broadcast_to(1.0 / l, (ATT_ROWS, LANES))
                    lse_scr[sub, hh, rows, :] = jnp.broadcast_to(m + jnp.log(l), (ATT_ROWS, LANES))
            outs = [_dot(p_scr[sub, hh], vw) * inv_scr[sub, hh] for hh in range(2)]
            o_ref[qrows, :] = jnp.where(sel[0], outs[0], outs[1]).astype(BF16)
            lse_ref[qrows, :] = jnp.where(sel[0], lse_scr[sub, 0], lse_scr[sub, 1])

    qspec, kspec, bspec = _att_specs(t, tp)
    return _call(body, name, (d // LANES, t // (ATT_SUBS * Q_BLOCK)), [qspec, kspec, kspec, bspec], [qspec, qspec],
                 [jax.ShapeDtypeStruct((t, d), BF16), jax.ShapeDtypeStruct((t, d), F32)],
                 [pltpu.VMEM((ATT_SUBS, 2, Q_BLOCK, K_WINDOW), F32),
                  pltpu.VMEM((ATT_SUBS, 2, Q_BLOCK, K_WINDOW), BF16),
                  pltpu.VMEM((ATT_SUBS, 2, Q_BLOCK, LANES), F32),
                  pltpu.VMEM((ATT_SUBS, 2, Q_BLOCK, LANES), F32)],
                 ("parallel", "arbitrary"), (q, kp, vp, bias), rider)


def _att_bwd(q, kp, vp, bias, do, o, lse, name, rider=None):
    t, d = q.shape
    tp = kp.shape[0]

    def body(q_ref, kp_ref, vp_ref, bias_ref, do_ref, o_ref, lse_ref, dq_ref, dkp_ref, dvp_ref, db_ref,
             s_scr, dp_scr, p_scr, ds_scr, row_scr):
        @pl.when(pl.program_id(1) == 0)
        def _():
            dkp_ref[...] = jnp.zeros_like(dkp_ref)
            dvp_ref[...] = jnp.zeros_like(dvp_ref)
            db_ref[...] = jnp.zeros_like(db_ref)

        for sub in range(ATT_SUBS):
            start, qrows, q2, kw, vw, kvalid, sel = _att_common(q_ref, kp_ref, vp_ref, sub)
            do2 = do_ref[qrows, :]
            qm = [jnp.where(sel[hh], q2, 0) for hh in range(2)]
            dom = [jnp.where(sel[hh], do2, 0) for hh in range(2)]
            do_o = do2.astype(F32) * o_ref[qrows, :].astype(F32)
            lse2 = lse_ref[qrows, :]
            for hh in range(2):
                s_scr[sub, hh] = _dot_nt(qm[hh], kw)
                dp_scr[sub, hh] = _dot_nt(dom[hh], vw)
                lse_h = jnp.max(jnp.where(sel[hh], lse2, NEG), axis=-1, keepdims=True)
                delta = jnp.sum(jnp.where(sel[hh], do_o, 0.0), axis=-1, keepdims=True)
                row_scr[sub, hh, 0] = jnp.broadcast_to(lse_h, (Q_BLOCK, LANES))
                row_scr[sub, hh, 1] = jnp.broadcast_to(delta, (Q_BLOCK, LANES))
            for hh in range(2):
                for rows in _row_groups():
                    s = jnp.where(kvalid, s_scr[sub, hh, rows, :] + bias_ref[hh, rows, :], NEG)
                    p = jnp.exp(s - _lane_copies(row_scr[sub, hh, 0, rows, :]))
                    ds = p * (dp_scr[sub, hh, rows, :] - _lane_copies(row_scr[sub, hh, 1, rows, :]))
                    db_ref[hh, rows, :] += ds
                    p_scr[sub, hh, rows, :] = p.astype(BF16)
                    ds_scr[sub, hh, rows, :] = ds.astype(BF16)
            dqs = [_dot(ds_scr[sub, hh], kw) for hh in range(2)]
            dq_ref[qrows, :] = jnp.where(sel[0], dqs[0], dqs[1])
            dkp_ref[:, pl.ds(start, K_WINDOW)] += (_dot_tn(qm[0], ds_scr[sub, 0]) +
                                                   _dot_tn(qm[1], ds_scr[sub, 1]))
            dvp_ref[:, pl.ds(start, K_WINDOW)] += (_dot_tn(dom[0], p_scr[sub, 0]) +
                                                   _dot_tn(dom[1], p_scr[sub, 1]))

    qspec, kspec, bspec = _att_specs(t, tp)
    tspec = pl.BlockSpec((LANES, tp), lambda h, i: (h, 0))
    stage = lambda dt: pltpu.VMEM((ATT_SUBS, 2, Q_BLOCK, K_WINDOW), dt)
    return _call(body, name, (d // LANES, t // (ATT_SUBS * Q_BLOCK)),
                 [qspec, kspec, kspec, bspec, qspec, qspec, qspec],
                 [qspec, tspec, tspec, bspec],
                 [jax.ShapeDtypeStruct((t, d), F32),
                  jax.ShapeDtypeStruct((d, tp), F32),
                  jax.ShapeDtypeStruct((d, tp), F32),
                  jax.ShapeDtypeStruct((ATT_HEADS, Q_BLOCK, K_WINDOW), F32)],
                 [stage(F32), stage(F32), stage(BF16), stage(BF16),
                  pltpu.VMEM((ATT_SUBS, 2, 2, Q_BLOCK, LANES), F32)],
                 ("parallel", "arbitrary"), (q, kp, vp, bias, do, o, lse), rider)


def _rel_bin_matrix():
    rows = REL_DELTAS * 2 * REL_BLK
    rho = lax.broadcasted_iota(jnp.int32, (rows, REL_PAD), 0)
    col = lax.broadcasted_iota(jnp.int32, (rows, REL_PAD), 1)
    assert 2 * REL_BLK == 256
    delta = rho >> 8
    c = 255 - (rho & 255)
    dist = K_PAD + REL_BLK * (delta - (K_WINDOW // REL_BLK - 1)) + (c - (REL_BLK - 1))
    idx = jnp.clip(dist, -REL_CLIP, REL_CLIP) + REL_CLIP
    return col == idx


def _rel_expand(rel_pad, name):
    heads = rel_pad.shape[0]
    rows = REL_DELTAS * 2 * REL_BLK

    def body_bin(r_ref, o_ref):
        onehot = jnp.where(_rel_bin_matrix(), 1.0, 0.0).astype(BF16)
        hi, mid, lo = _split3(r_ref[...])
        o_ref[...] = _dot_nt(hi, onehot) + _dot_nt(mid, onehot) + _dot_nt(lo, onehot)

    by_delta = pl.pallas_call(
        body_bin, name=name + "_bin",
        out_shape=jax.ShapeDtypeStruct((heads, rows), F32),
        compiler_params=pltpu.CompilerParams(vmem_limit_bytes=VMEM_LIMIT_V7X),
    )(rel_pad)
    by_delta = by_delta.reshape(heads * REL_DELTAS, 2 * REL_BLK)

    def body_shift(t_ref, o_ref):
        tv = t_ref[...]
        for r in range(REL_BLK):
            o_ref[r] = pltpu.roll(tv, (r + REL_BLK) % (2 * REL_BLK), 1)[:, :REL_BLK]

    return pl.pallas_call(
        body_shift, name=name + "_shift",
        out_shape=jax.ShapeDtypeStruct((REL_BLK, heads * REL_DELTAS, REL_BLK), F32),
        compiler_params=pltpu.CompilerParams(vmem_limit_bytes=VMEM_LIMIT_V7X),
    )(by_delta)


def _bias_table(rel_bias, name):
    heads = rel_bias.shape[0]
    rel_pad = jnp.pad(rel_bias, ((0, 0), (0, REL_PAD - REL_TABLE)))
    tiles = _rel_expand(rel_pad, name)
    tiles = tiles.reshape(REL_BLK, heads, REL_DELTAS, REL_BLK).transpose(1, 2, 0, 3)
    na, nb = Q_BLOCK // REL_BLK, K_WINDOW // REL_BLK
    rows = [jnp.concatenate([tiles[:, a - b + nb - 1] for b in range(nb)], axis=-1) for a in range(na)]
    table = jnp.concatenate(rows, axis=-2)
    qc = np.arange(Q_BLOCK)[:, None] // CHUNK
    kc = np.arange(K_WINDOW)[None, :] // CHUNK
    band = (kc >= qc) & (kc <= qc + PAST_CHUNKS)
    return jnp.where(jnp.asarray(band)[None], table, NEG)


def _rel_reduce(db, name):
    heads = db.shape[0]
    na, nb = Q_BLOCK // REL_BLK, K_WINDOW // REL_BLK

    fold_heads = 4

    def body_fold(db_ref, g_ref):
        for hd in range(fold_heads):
            for delta in range(REL_DELTAS):
                acc = None
                for a in range(na):
                    b = a - (delta - (nb - 1))
                    if 0 <= b < nb:
                        tile = db_ref[hd, a * REL_BLK:(a + 1) * REL_BLK, b * REL_BLK:(b + 1) * REL_BLK]
                        acc = tile if acc is None else acc + tile
                g_ref[hd, delta] = acc

    folded = pl.pallas_call(
        body_fold, name=name + "_fold", grid=(heads // fold_heads,),
        in_specs=[pl.BlockSpec((fold_heads, Q_BLOCK, K_WINDOW), lambda h: (h, 0, 0))],
        out_specs=pl.BlockSpec((fold_heads, REL_DELTAS, REL_BLK, REL_BLK), lambda h: (h, 0, 0, 0)),
        out_shape=jax.ShapeDtypeStruct((heads, REL_DELTAS, REL_BLK, REL_BLK), F32),
        compiler_params=_params("parallel"),
    )(db)
    by_row = folded.transpose(2, 0, 1, 3).reshape(REL_BLK, heads * REL_DELTAS, REL_BLK)

    def body_diag(g_ref, d_ref):
        zeros = jnp.zeros((heads * REL_DELTAS, REL_BLK), F32)
        acc = None
        for r in range(REL_BLK):
            part = pltpu.roll(jnp.concatenate([g_ref[r], zeros], axis=1), REL_BLK - r, 1)
            acc = part if acc is None else acc + part
        d_ref[...] = acc

    diag = pl.pallas_call(
        body_diag, name=name + "_diag",
        out_shape=jax.ShapeDtypeStruct((heads * REL_DELTAS, 2 * REL_BLK), F32),
        compiler_params=pltpu.CompilerParams(vmem_limit_bytes=VMEM_LIMIT_V7X),
    )(by_row)
    diag = diag.reshape(heads, REL_DELTAS * 2 * REL_BLK)

    def body_bin(d_ref, o_ref):
        onehot = jnp.where(_rel_bin_matrix(), 1.0, 0.0).astype(BF16)
        hi, mid, lo = _split3(d_ref[...])
        o_ref[...] = _dot(hi, onehot) + _dot(mid, onehot) + _dot(lo, onehot)

    out = pl.pallas_call(
        body_bin, name=name + "_bin",
        out_shape=jax.ShapeDtypeStruct((heads, REL_PAD), F32),
        compiler_params=pltpu.CompilerParams(vmem_limit_bytes=VMEM_LIMIT_V7X),
    )(diag)
    return out[:, :REL_TABLE]


def _sum_leading(x, name):
    n, r, c = x.shape
    tr = _pick(r, 256, 8)

    def body(x_ref, o_ref):
        acc = x_ref[0].astype(F32)
        for k in range(1, n):
            acc = acc + x_ref[k].astype(F32)
        o_ref[...] = acc

    return pl.pallas_call(
        body, name=name, grid=(r // tr,),
        in_specs=[pl.BlockSpec((n, tr, c), lambda i: (0, i, 0))],
        out_specs=pl.BlockSpec((tr, c), lambda i: (i, 0)),
        out_shape=jax.ShapeDtypeStruct((r, c), F32),
        compiler_params=_params("parallel"),
    )(x)


def _pair_add(g, recv, parity, name):
    _, r, c = g.shape
    tr = _pick(r, 256, 16)

    def body(par_ref, g_ref, r_ref, o_ref):
        o_ref[...] = (g_ref[...].astype(F32) + r_ref[...].astype(F32)).astype(BF16)

    return pl.pallas_call(
        body, name=name,
        grid_spec=pltpu.PrefetchScalarGridSpec(
            num_scalar_prefetch=1, grid=(4, r // tr),
            in_specs=[pl.BlockSpec((1, tr, c), lambda k, i, par: (2 * k + par[0], i, 0)),
                      pl.BlockSpec((1, tr, c), lambda k, i, par: (k, i, 0))],
            out_specs=pl.BlockSpec((1, tr, c), lambda k, i, par: (k, i, 0))),
        out_shape=jax.ShapeDtypeStruct((4, r, c), BF16),
        compiler_params=_params("parallel", "parallel"),
    )(parity, g, recv)


def _adamw(w, g_parts, m, v, name):
    r, c = w.shape
    n = g_parts.shape[0]
    tr = _pick(r, 256, 16 if g_parts.dtype == BF16 else 8)
    c1 = 1.0 - ADAM_B1 ** ADAM_STEP
    c2 = 1.0 - ADAM_B2 ** ADAM_STEP

    def body(w_ref, g_ref, m_ref, v_ref, go_ref, d_ref, nm_ref, nv_ref):
        gv = g_ref[0].astype(F32)
        for k in range(1, n):
            gv = gv + g_ref[k].astype(F32)
        nm = ADAM_B1 * m_ref[...] + (1.0 - ADAM_B1) * gv
        nv = ADAM_B2 * v_ref[...] + (1.0 - ADAM_B2) * (gv * gv)
        go_ref[...] = gv
        d_ref[...] = -ADAM_LR * ((nm / c1) / (jnp.sqrt(nv / c2) + ADAM_EPS) + ADAM_WD * w_ref[...])
        nm_ref[...] = nm
        nv_ref[...] = nv

    spec = pl.BlockSpec((tr, c), lambda i: (i, 0))
    shp = jax.ShapeDtypeStruct((r, c), F32)
    return pl.pallas_call(
        body, name=name, grid=(r // tr,),
        in_specs=[spec, pl.BlockSpec((n, tr, c), lambda i: (0, i, 0)), spec, spec],
        out_specs=[spec] * 4, out_shape=[shp] * 4,
        compiler_params=_params("parallel"),
    )(w, g_parts, m, v)


BIG = (("a_w_in", 1), ("a_w_o", 0), ("a_w_gu", 0), ("a_w_down", 0), ("w_kv", 1),
       ("b_w_q", 0), ("b_w_o", 0), ("b_w_gu", 0), ("b_w_down", 0))
TRANSPOSED = ("a_w_gu", "b_w_gu")
FFN_BLK = 2 * FFN_HIDDEN // N_DEV

SMALL = (("a_norm_g", D_MODEL, True), ("a_gn_g", RET_V_COLS, True), ("a_ffn_norm_g", D_MODEL, True),
         ("kv_norm_g", D_MODEL, False), ("b_norm_g", D_MODEL, False), ("b_ffn_norm_g", D_MODEL, False),
         ("k_norm_g", ATT_DH, False), ("b_q_norm_g", ATT_DH, False),
         ("b_rel_bias", ATT_HEADS * REL_TABLE, False))
SMALL_ROWS, SMALL_COLS = 16, 1024


def _pack_small(vals, last=None):
    flat = jnp.concatenate([vals[n].reshape(-1) for n, _, _ in SMALL])
    room = SMALL_ROWS * SMALL_COLS - flat.shape[0]
    if last is None:
        flat = jnp.pad(flat, (0, room))
    else:
        flat = jnp.concatenate([jnp.pad(flat, (0, room - 1)), last.reshape(1)])
    return flat.reshape(SMALL_ROWS, SMALL_COLS)


def _unpack_small(packed, local):
    flat, out, pos = packed.reshape(-1), {}, 0
    for n, length, sharded in SMALL:
        ln = length // N_DEV if (local and sharded) else length
        out[n] = flat[pos:pos + ln]
        pos += ln
    return out


def _gather_rider(shards, names):
    return _GatherRider([shards[n] for n in names])


def _gathered(rider, names, axis_of):
    return {n: (r.reshape(-1, r.shape[2]) if axis_of[n] == 0 else r) for n, r in zip(names, rider.results)}


def _blocks(g):
    return g if g.ndim == 3 else g.reshape(N_DEV, -1, g.shape[-1])


def _local_step(x, target, shards, s, parity):
    t = x.shape[0]
    axis_of = dict(BIG)
    consts = _ret_consts(t)
    lane_to_head = np.zeros((D_MODEL, LANES), np.float32)
    lane_to_head[np.arange(D_MODEL), np.arange(D_MODEL) // ATT_DH] = 1.0
    bd = jnp.asarray(lane_to_head).astype(BF16)
    kg_t = jnp.tile(s["k_norm_g"], (1, ATT_HEADS))
    qg_t = jnp.tile(s["b_q_norm_g"], (1, ATT_HEADS))
    q_scale = ATT_DH ** -0.5
    w, g, recv = {}, {}, {}

    def gather_on(names):
        return _gather_rider(shards, names), names

    def landed(ride):
        w.update(_gathered(ride[0], ride[1], axis_of))

    def scatter_on(names):
        return _ScatterRider([_blocks(g[n]) for n in names]), names

    def reduced(ride):
        recv.update(zip(ride[1], ride[0].results))

    proj, (w["a_w_in"], w_o) = _proj_gather(x, s["a_norm_g"], shards["a_w_in"], [shards["a_w_o"]], "a_proj")
    w["a_w_o"] = w_o.reshape(-1, w_o.shape[2])
    ride = gather_on(["a_w_gu"])
    y, o_ret, states = _ret_fwd(proj, s["a_gn_g"], consts, "a_ret", rider=ride[0])
    landed(ride)
    ride = gather_on(["w_kv"])
    x1 = _mm(y, w["a_w_o"], "nn", "a_out", res=x, rider=ride[0])
    landed(ride)
    ride = gather_on(["a_w_down", "b_w_q", "b_w_o"])
    gu_a, act_a = _mm(x1, w["a_w_gu"], "nt", "a_ffn_gu", epilogue="swiglu", out_block=FFN_BLK,
                      norm_g=s["a_ffn_norm_g"], rider=ride[0])
    landed(ride)
    x2 = _mm(act_a, w["a_w_down"], "nn", "a_ffn_down", res=x1)

    kv = _mm(x2, w["w_kv"], "nn", "kv_proj", norm_g=s["kv_norm_g"])
    kp, vp = _kv_prep(kv, kg_t, bd, "kv_prep")

    q_raw = _mm(x2, w["b_w_q"], "nn", "b_q", norm_g=s["b_norm_g"])
    qn = _q_hnorm(q_raw, qg_t, bd, q_scale, "q_hnorm")
    bias = _bias_table(s["b_rel_bias"].reshape(ATT_HEADS, REL_TABLE), "rel")
    ride = gather_on(["b_w_gu"])
    o_att, lse = _att_fwd(qn, kp, vp, bias, "b_att", rider=ride[0])
    landed(ride)
    x3 = _mm(o_att, w["b_w_o"], "nn", "b_out", res=x2)
    ride = gather_on(["b_w_down"])
    gu_b, act_b = _mm(x3, w["b_w_gu"], "nt", "b_ffn_gu", epilogue="swiglu", out_block=FFN_BLK,
                      norm_g=s["b_ffn_norm_g"], rider=ride[0])
    landed(ride)
    dy, loss = _mm(act_b, w["b_w_down"], "nn", "b_ffn_down", res=x3, epilogue="loss", extra=(target,))
    in_blk, kv_blk, ffn_blk = w["a_w_in"].shape[2], w["w_kv"].shape[2], FFN_BLK

    dgu = _mm(dy, w["b_w_down"], "nt", "b_ffn_dgu", out_block=ffn_blk, epilogue="swiglu_bwd", extra=gu_b)
    dgu = dgu.reshape(N_DEV, t, ffn_blk)
    g["b_w_down"] = _mm(act_b, dy, "tn", "b_ffn_gdown", out_dtype=BF16)
    ride = scatter_on(["b_w_down"])
    dx3, g["b_ffn_norm_g"] = _mm(dgu, w["b_w_gu"], "nn", "b_ffn_dh", epilogue="rms_bwd",
                                 extra=(x3, s["b_ffn_norm_g"], dy), rider=ride[0])
    reduced(ride)
    g["b_w_gu"] = _mm(dgu, x3, "tn", "b_ffn_ggu", out_dtype=BF16, norm_g=s["b_ffn_norm_g"], norm_b=True)

    do_att = _mm(dx3, w["b_w_o"], "nt", "b_dout", out_dtype=BF16)
    g["b_w_o"] = _mm(o_att, dx3, "tn", "b_gout", out_dtype=BF16)
    ride = scatter_on(["b_w_gu", "b_w_o"])
    dq, dkp, dvp, db = _att_bwd(qn, kp, vp, bias, do_att, o_att, lse, "b_datt", rider=ride[0])
    reduced(ride)
    g["b_rel_bias"] = _rel_reduce(db, "drel").reshape(1, -1)
    dq_raw, gq = _q_dhnorm(q_raw, qg_t, bd, dq, q_scale, "q_dhnorm")
    g["b_q_norm_g"] = gq.reshape(ATT_HEADS, ATT_DH).sum(axis=0, keepdims=True)
    g["b_w_q"] = _mm(x2, dq_raw, "tn", "b_gq", out_dtype=BF16, norm_g=s["b_norm_g"])
    dx2, g["b_norm_g"] = _mm(dq_raw, w["b_w_q"], "nt", "b_dq", epilogue="rms_bwd",
                             extra=(x2, s["b_norm_g"], dx3))

    dkv, gk = _kv_dprep(kv, kg_t, bd, dkp, dvp, "kv_dprep")
    g["k_norm_g"] = gk.reshape(ATT_HEADS, ATT_DH).sum(axis=0, keepdims=True)
    g["w_kv"] = _mm(x2, dkv, "tn", "kv_g", out_dtype=BF16, out_block=kv_blk, norm_g=s["kv_norm_g"])
    dx2, g["kv_norm_g"] = _mm(dkv, w["w_kv"], "nt", "kv_du", epilogue="rms_bwd",
                              extra=(x2, s["kv_norm_g"], dx2))

    ride = scatter_on(["b_w_q"])
    dgu = _mm(dx2, w["a_w_down"], "nt", "a_ffn_dgu", out_block=ffn_blk, epilogue="swiglu_bwd", extra=gu_a,
              rider=ride[0])
    reduced(ride)
    dgu = dgu.reshape(N_DEV, t, ffn_blk)
    g["a_w_down"] = _mm(act_a, dx2, "tn", "a_ffn_gdown", out_dtype=BF16)
    ride = scatter_on(["a_w_down"])
    dx1, g["a_ffn_norm_g"] = _mm(dgu, w["a_w_gu"], "nn", "a_ffn_dh", epilogue="rms_bwd",
                                 extra=(x1, s["a_ffn_norm_g"], dx2), rider=ride[0])
    reduced(ride)
    ride = scatter_on(["w_kv"])
    g["a_w_gu"] = _mm(dgu, x1, "tn", "a_ffn_ggu", out_dtype=BF16, norm_g=s["a_ffn_norm_g"], norm_b=True,
                      rider=ride[0])
    reduced(ride)

    swap = _SiblingSwapRider([_blocks(g["a_w_gu"])])
    dy_ret = _mm(dx1, w["a_w_o"], "nt", "a_dout", rider=swap)
    g["a_w_o"] = _mm(y, dx1, "tn", "a_gout", out_dtype=BF16)
    chips = _ChipScatterRider([_pair_add(_blocks(g["a_w_gu"]), swap.results[0], parity, "rs_pair_add_gu")])
    dproj, g["a_gn_g"] = _ret_bwd(proj, s["a_gn_g"], o_ret, states, dy_ret, consts, "a_dret", rider=chips)
    recv["a_w_gu"] = chips.results[0]
    ride = scatter_on(["a_w_o"])
    g["a_w_in"] = _mm(x, dproj, "tn", "a_gin", out_dtype=BF16, out_block=in_blk, norm_g=s["a_norm_g"],
                      rider=ride[0])
    reduced(ride)
    from_sibling = _exchange(_SiblingSwapRider([g["a_w_in"]]), "rs_sibling")[0]
    chip_sums = _pair_add(g["a_w_in"], from_sibling, parity, "rs_pair_add")
    last = _ChipScatterRider([chip_sums])
    grad_x, g["a_norm_g"] = _mm(dproj, w["a_w_in"], "nt", "a_dproj", epilogue="rms_bwd",
                                extra=(x, s["a_norm_g"], dx1), rider=last)
    recv["a_w_in"] = last.results[0]
    return loss, grad_x, recv, g


ARG_NAMES = ("x", "a_norm_g", "a_w_in", "a_gn_g", "a_w_o", "a_ffn_norm_g", "a_w_gu", "a_w_down",
             "kv_norm_g", "w_kv", "k_norm_g", "b_norm_g", "b_w_q", "b_q_norm_g", "b_rel_bias", "b_w_o",
             "b_ffn_norm_g", "b_w_gu", "b_w_down")
WEIGHT_NAMES = ARG_NAMES[1:]


def _big_shard(a, name):
    a = a[0] if a.ndim == 3 else a
    return a.T if name in TRANSPOSED else a


def _as_given(a, name, shape):
    return (a.T if name in TRANSPOSED else a).reshape(shape)


def kernel(x, a_norm_g, a_w_in, a_gn_g, a_w_o, a_ffn_norm_g, a_w_gu, a_w_down, kv_norm_g, w_kv, k_norm_g, b_norm_g, b_w_q, b_q_norm_g, b_rel_bias, b_w_o, b_ffn_norm_g, b_w_gu, b_w_down, loss_target, m_a_norm_g, m_a_w_in, m_a_gn_g, m_a_w_o, m_a_ffn_norm_g, m_a_w_gu, m_a_w_down, m_kv_norm_g, m_w_kv, m_k_norm_g, m_b_norm_g, m_b_w_q, m_b_q_norm_g, m_b_rel_bias, m_b_w_o, m_b_ffn_norm_g, m_b_w_gu, m_b_w_down, v_a_norm_g, v_a_w_in, v_a_gn_g, v_a_w_o, v_a_ffn_norm_g, v_a_w_gu, v_a_w_down, v_kv_norm_g, v_w_kv, v_k_norm_g, v_b_norm_g, v_b_w_q, v_b_q_norm_g, v_b_rel_bias, v_b_w_o, v_b_ffn_norm_g, v_b_w_gu, v_b_w_down):
    args = (x, a_norm_g, a_w_in, a_gn_g, a_w_o, a_ffn_norm_g, a_w_gu, a_w_down, kv_norm_g, w_kv, k_norm_g,
            b_norm_g, b_w_q, b_q_norm_g, b_rel_bias, b_w_o, b_ffn_norm_g, b_w_gu, b_w_down)
    p = dict(zip(ARG_NAMES, args))
    m_all = dict(zip(WEIGHT_NAMES, (m_a_norm_g, m_a_w_in, m_a_gn_g, m_a_w_o, m_a_ffn_norm_g, m_a_w_gu,
                                    m_a_w_down, m_kv_norm_g, m_w_kv, m_k_norm_g, m_b_norm_g, m_b_w_q,
                                    m_b_q_norm_g, m_b_rel_bias, m_b_w_o, m_b_ffn_norm_g, m_b_w_gu, m_b_w_down)))
    v_all = dict(zip(WEIGHT_NAMES, (v_a_norm_g, v_a_w_in, v_a_gn_g, v_a_w_o, v_a_ffn_norm_g, v_a_w_gu,
                                    v_a_w_down, v_kv_norm_g, v_w_kv, v_k_norm_g, v_b_norm_g, v_b_w_q,
                                    v_b_q_norm_g, v_b_rel_bias, v_b_w_o, v_b_ffn_norm_g, v_b_w_gu, v_b_w_down)))
    xi, yi, ci = _my_place()
    me = 4 * xi + 2 * yi + ci
    big_names = [n for n, _ in BIG]

    big_local = {n: _big_shard(p[n], n) for n in big_names}
    shards = {n: a.astype(BF16) for n, a in big_local.items()}
    small_local = _pack_small({n: p[n] for n, _, _ in SMALL})
    small_all = _exchange(_GatherRider([small_local]), "gather_small")[0]
    flat_g = small_all.reshape(N_DEV, -1)
    s_full, pos = {}, 0
    for n, length, sharded in SMALL:
        ln = length // N_DEV if sharded else length
        s_full[n] = flat_g[:, pos:pos + ln].reshape(1, -1) if sharded else p[n].reshape(1, -1)
        pos += ln

    parity = jnp.reshape(ci, (1,)).astype(jnp.int32)
    loss, grad_x, recv, g = _local_step(x[0], loss_target[0], shards, s_full, parity)

    partial = _pack_small({n: g[n] for n, _, _ in SMALL}, last=loss)
    summed = _sum_leading(_exchange(_GatherRider([partial]), "gather_gsmall")[0], "gsmall_sum")
    loss = summed[SMALL_ROWS - 1, SMALL_COLS - 1]
    g_small = _unpack_small(summed, local=False)
    for n, length, sharded in SMALL:
        if sharded:
            g_small[n] = lax.dynamic_slice(g_small[n], (me * (length // N_DEV),), (length // N_DEV,))

    grads, deltas, new_m, new_v = {}, {}, {}, {}
    for n in big_names:
        outs = _adamw(big_local[n], recv[n], _big_shard(m_all[n], n), _big_shard(v_all[n], n), "adamw_" + n)
        grads[n], deltas[n], new_m[n], new_v[n] = (_as_given(a, n, p[n].shape) for a in outs)
    pk = lambda src: _pack_small({n: src[n] for n, _, _ in SMALL})
    outs = _adamw(small_local, pk(g_small)[None], pk(m_all), pk(v_all), "adamw_small")
    g_s, d_s, nm_s, nv_s = (_unpack_small(a, local=True) for a in outs)
    for n, _, _ in SMALL:
        grads[n], deltas[n], new_m[n], new_v[n] = (a[n].reshape(p[n].shape) for a in (g_s, d_s, nm_s, nv_s))

    return (loss, grad_x[None], *[grads[n] for n in WEIGHT_NAMES], *[deltas[n] for n in WEIGHT_NAMES],
            *[new_m[n] for n in WEIGHT_NAMES], *[new_v[n] for n in WEIGHT_NAMES])
```

```python
import numpy as np
import jax
import jax.numpy as jnp
from jax import lax
from jax.experimental import pallas as pl
from jax.experimental.pallas import tpu as pltpu

F32 = jnp.float32
BF16 = jnp.bfloat16

N_DEV = 8
D_MODEL = 1024
CHUNK = 64
EPS = 1e-6
RET_HEADS, RET_DK, RET_DV = 4, 256, 512
RET_STEP = 4
RET_Q_COLS = RET_HEADS * RET_DK
RET_V_COLS = RET_HEADS * RET_DV
ATT_HEADS, ATT_DH = 16, 64
PAST_CHUNKS = 8
REL_CLIP = 256
REL_TABLE = 2 * REL_CLIP + 1
FFN_HIDDEN = 2816
ROPE_BASE = 10000.0
LANES = 128
Q_BLOCK = 256
ATT_SUBS = 4
ATT_ROWS = 32
K_PAD = PAST_CHUNKS * CHUNK
K_WINDOW = Q_BLOCK + K_PAD
REL_BLK = 128
REL_DELTAS = Q_BLOCK // REL_BLK + K_WINDOW // REL_BLK - 1
REL_PAD = 640
NEG = -1e30
VMEM_LIMIT_V7X = 56 * 1024 * 1024
ADAM_LR, ADAM_B1, ADAM_B2, ADAM_EPS, ADAM_WD, ADAM_STEP = 1e-3, 0.9, 0.999, 1e-8, 0.01, 10
MESH = pl.DeviceIdType.MESH
ANY = pl.BlockSpec(memory_space=pl.ANY)


def _params(*semantics):
    return pltpu.CompilerParams(dimension_semantics=semantics, vmem_limit_bytes=VMEM_LIMIT_V7X)


def _pick(dim, cap, align):
    best = None
    for t in range(align, min(dim, cap) + 1, align):
        if dim % t == 0:
            best = t
    assert best is not None, (dim, cap, align)
    return best


def _dot(a, b):
    return lax.dot_general(a, b, (((1,), (0,)), ((), ())), preferred_element_type=F32)


def _dot_nt(a, b):
    return lax.dot_general(a, b, (((1,), (1,)), ((), ())), preferred_element_type=F32)


def _dot_tn(a, b):
    return lax.dot_general(a, b, (((0,), (0,)), ((), ())), preferred_element_type=F32)


def _split2(x):
    hi = x.astype(BF16)
    lo = (x - hi.astype(F32)).astype(BF16)
    return hi, lo


def _split3(x):
    hi = x.astype(BF16)
    r = x - hi.astype(F32)
    mid = r.astype(BF16)
    lo = (r - mid.astype(F32)).astype(BF16)
    return hi, mid, lo


def _sigmoid(x):
    return 1.0 / (1.0 + jnp.exp(-x))


def _accumulate(ref, part, step):
    @pl.when(step == 0)
    def _():
        ref[...] = part

    @pl.when(step > 0)
    def _():
        ref[...] += part


RELAY_AT_NUM, RELAY_AT_DEN = 3, 4


def _my_place():
    return lax.axis_index("x"), lax.axis_index("y"), lax.axis_index("c")


def _flip(v, bit):
    return 1 - v if bit else v


class _NoRelay:
    def relay(self, in_refs, out_refs, sems):
        pass


class _GatherRider:
    def __init__(self, xs):
        self.inputs = list(xs)
        n = len(xs)
        self.out_shape = [jax.ShapeDtypeStruct((N_DEV,) + x.shape, x.dtype) for x in xs]
        self.scratch = [pltpu.SemaphoreType.DMA((7, n)), pltpu.SemaphoreType.DMA((7, n)),
                        pltpu.SemaphoreType.DMA((n,))]
        self.results = None

    def _copies(self, x_refs, out_refs, sems):
        send_sems, recv_sems, local_sems = sems
        n = len(x_refs)
        x, y, c = _my_place()
        me, sibling = (x, y, c), (x, y, 1 - c)
        chips = [(1 - x, y), (x, 1 - y), (1 - x, 1 - y)]

        def slot(a, px, py, pc):
            return out_refs[a].at[4 * px + 2 * py + pc]

        def copy(k, a, block, to, own=False):
            return pltpu.make_async_remote_copy(
                src_ref=x_refs[a] if own else slot(a, *block), dst_ref=slot(a, *block),
                send_sem=send_sems.at[k, a], recv_sem=recv_sems.at[k, a],
                device_id=to, device_id_type=MESH)

        mine = [pltpu.make_async_copy(x_refs[a], slot(a, *me), local_sems.at[a]) for a in range(n)]
        first = []
        for a in range(n):
            first.append(copy(0, a, me, sibling, own=True))
            first += [copy(1 + j, a, me, (*chip, c), own=True) for j, chip in enumerate(chips)]
        return n, c, me, sibling, chips, copy, mine, first

    def start(self, x_refs, out_refs, sems):
        _, _, _, _, _, _, mine, first = self._copies(x_refs, out_refs, sems)
        for cp in mine + first:
            cp.start()

    def relay(self, x_refs, out_refs, sems):
        n, c, me, sibling, chips, copy, _, _ = self._copies(x_refs, out_refs, sems)
        for j, chip in enumerate(chips):
            for a in range(n):
                copy(1 + j, a, (*chip, c), me).wait_recv()
                copy(4 + j, a, (*chip, c), sibling).start()

    def finish(self, x_refs, out_refs, sems):
        n, c, me, sibling, chips, copy, mine, first = self._copies(x_refs, out_refs, sems)
        passed = [copy(4 + j, a, (*chip, c), sibling) for j, chip in enumerate(chips) for a in range(n)]
        for a in range(n):
            copy(0, a, sibling, me).wait_recv()
            for j, chip in enumerate(chips):
                copy(4 + j, a, (*chip, 1 - c), me).wait_recv()
        for cp in first + passed:
            cp.wait_send()
        for cp in mine:
            cp.wait()


class _ScatterRider(_NoRelay):
    def __init__(self, gs):
        self.inputs = list(gs)
        n = len(gs)
        self.out_shape = [jax.ShapeDtypeStruct(g.shape, g.dtype) for g in gs]
        self.scratch = [pltpu.SemaphoreType.DMA((7, n)), pltpu.SemaphoreType.DMA((7, n)),
                        pltpu.SemaphoreType.DMA((n,))]
        self.results = None

    def _copies(self, g_refs, out_refs, sems):
        send_sems, recv_sems, local_sems = sems
        x, y, c = _my_place()
        me = 4 * x + 2 * y + c
        mine, copies = [], []
        for a in range(len(g_refs)):
            mine.append(pltpu.make_async_copy(g_refs[a].at[me], out_refs[a].at[me], local_sems.at[a]))
            for k in range(1, N_DEV):
                px, py, pc = _flip(x, k & 4), _flip(y, k & 2), _flip(c, k & 1)
                copies.append(pltpu.make_async_remote_copy(
                    src_ref=g_refs[a].at[4 * px + 2 * py + pc], dst_ref=out_refs[a].at[me],
                    send_sem=send_sems.at[k - 1, a], recv_sem=recv_sems.at[k - 1, a],
                    device_id=(px, py, pc), device_id_type=MESH))
        return mine, copies

    def start(self, g_refs, out_refs, sems):
        mine, copies = self._copies(g_refs, out_refs, sems)
        for cp in mine + copies:
            cp.start()

    def finish(self, g_refs, out_refs, sems):
        mine, copies = self._copies(g_refs, out_refs, sems)
        for cp in copies + mine:
            cp.wait()


class _SiblingSwapRider(_NoRelay):
    def __init__(self, gs):
        self.inputs = list(gs)
        n = len(gs)
        self.out_shape = [jax.ShapeDtypeStruct((4,) + g.shape[1:], g.dtype) for g in gs]
        self.scratch = [pltpu.SemaphoreType.DMA((4, n)), pltpu.SemaphoreType.DMA((4, n))]
        self.results = None

    def _copies(self, g_refs, out_refs, sems):
        send_sems, recv_sems = sems
        x, y, c = _my_place()
        return [pltpu.make_async_remote_copy(
            src_ref=g_refs[a].at[2 * k + 1 - c], dst_ref=out_refs[a].at[k],
            send_sem=send_sems.at[k, a], recv_sem=recv_sems.at[k, a],
            device_id=(x, y, 1 - c), device_id_type=MESH)
            for a in range(len(g_refs)) for k in range(4)]

    def start(self, g_refs, out_refs, sems):
        for cp in self._copies(g_refs, out_refs, sems):
            cp.start()

    def finish(self, g_refs, out_refs, sems):
        for cp in self._copies(g_refs, out_refs, sems):
            cp.wait()


class _ChipScatterRider(_NoRelay):
    def __init__(self, ps):
        self.inputs = list(ps)
        n = len(ps)
        self.out_shape = [jax.ShapeDtypeStruct(p.shape, p.dtype) for p in ps]
        self.scratch = [pltpu.SemaphoreType.DMA((3, n)), pltpu.SemaphoreType.DMA((3, n)),
                        pltpu.SemaphoreType.DMA((n,))]
        self.results = None

    def _copies(self, p_refs, out_refs, sems):
        send_sems, recv_sems, local_sems = sems
        x, y, c = _my_place()
        my_chip = 2 * x + y
        chips = [(1 - x, y), (x, 1 - y), (1 - x, 1 - y)]
        n = len(p_refs)
        mine = [pltpu.make_async_copy(p_refs[a].at[my_chip], out_refs[a].at[my_chip], local_sems.at[a])
                for a in range(n)]
        copies = [pltpu.make_async_remote_copy(
            src_ref=p_refs[a].at[2 * cx + cy], dst_ref=out_refs[a].at[my_chip],
            send_sem=send_sems.at[j, a], recv_sem=recv_sems.at[j, a],
            device_id=(cx, cy, c), device_id_type=MESH)
            for a in range(n) for j, (cx, cy) in enumerate(chips)]
        return mine, copies

    def start(self, p_refs, out_refs, sems):
        mine, copies = self._copies(p_refs, out_refs, sems)
        for cp in mine + copies:
            cp.start()

    def finish(self, p_refs, out_refs, sems):
        mine, copies = self._copies(p_refs, out_refs, sems)
        for cp in copies + mine:
            cp.wait()


def _call(body, name, grid, in_specs, out_specs, out_shape, scratch, semantics, args, rider=None):
    in_specs, out_specs, out_shape, scratch = list(in_specs), list(out_specs), list(out_shape), list(scratch)
    if rider is None:
        return list(pl.pallas_call(
            body, name=name, grid=grid, in_specs=in_specs, out_specs=out_specs, out_shape=out_shape,
            scratch_shapes=scratch, compiler_params=_params(*semantics))(*args))
    n_in, n_out, n_scr = len(in_specs), len(out_specs), len(scratch)
    r_in, r_out = len(rider.inputs), len(rider.out_shape)

    def wrapped(*refs):
        cuts = np.cumsum([0, n_in, r_in, n_out, r_out, n_scr])
        hi, ri, ho, ro, hs = (refs[cuts[i]:cuts[i + 1]] for i in range(5))
        rs = refs[cuts[5]:]
        step, steps = pl.program_id(0), grid[0]
        for d in range(1, len(grid)):
            step, steps = step * grid[d] + pl.program_id(d), steps * grid[d]

        @pl.when(step == 0)
        def _():
            rider.start(ri, ro, rs)

        body(*hi, *ho, *hs)

        @pl.when(step == (steps * RELAY_AT_NUM) // RELAY_AT_DEN)
        def _():
            rider.relay(ri, ro, rs)

        @pl.when(step == steps - 1)
        def _():
            rider.finish(ri, ro, rs)

    outs = pl.pallas_call(
        wrapped, name=name, grid=grid,
        in_specs=in_specs + [ANY] * r_in, out_specs=out_specs + [ANY] * r_out,
        out_shape=out_shape + rider.out_shape, scratch_shapes=scratch + rider.scratch,
        compiler_params=_params(*(["arbitrary"] * len(grid))),
    )(*args, *rider.inputs)
    rider.results = list(outs[n_out:])
    return list(outs[:n_out])


_WALK = ((None, None), (0, None), (1, 4), (2, 5), (4, None), (5, None), (3, 6), (6, None))


def _gather_order():
    x, y, c = _my_place()
    (ax, ay), (bx, by), (dx, dy) = (1 - x, y), (x, 1 - y), (1 - x, 1 - y)
    ids = [(x, y, c), (x, y, 1 - c), (ax, ay, c), (bx, by, c), (ax, ay, 1 - c), (bx, by, 1 - c),
           (dx, dy, c), (dx, dy, 1 - c)]
    return jnp.stack([4 * px + 2 * py + pc for px, py, pc in ids]).astype(jnp.int32)


def _proj_gather(x, norm_g, w_shard, extras, name):
    t, d = x.shape
    cols = w_shard.shape[1]
    tm = _pick(t, MM_CAP_MN, 16)
    ni = t // tm
    n = 1 + len(extras)
    rider = _GatherRider([w_shard] + list(extras))

    def body(ord_ref, x_ref, g_ref, *refs):
        sh_refs, proj_ref, gathered = refs[:n], refs[n], refs[n + 1:2 * n + 1]
        h_all, bbuf, bsem, send_sems, recv_sems, local_sems = refs[2 * n + 1:]
        j, i = pl.program_id(0), pl.program_id(1)
        _, c, me, sibling, chips, copy, mine, first = rider._copies(
            sh_refs, gathered, (send_sems, recv_sems, local_sems))
        rows = pl.ds(pl.multiple_of(i * tm, tm), tm)

        def load(step, src):
            return pltpu.make_async_copy(src, bbuf.at[step % 2], bsem.at[step % 2])

        def relayed(k, a):
            return copy(k, a, (*chips[k - 4], c), sibling)

        @pl.when(jnp.logical_and(j == 0, i == 0))
        def _():
            for cp in mine + first:
                cp.start()
            load(0, sh_refs[0]).start()

        @pl.when(i == 0)
        def _():
            load(j, sh_refs[0]).wait()

        @pl.when(j == 0)
        def _():
            groups = []
            for r in range(0, tm, NORM_ROWS):
                xv = x_ref[r:r + NORM_ROWS, :]
                rstd = lax.rsqrt(jnp.mean(xv * xv, axis=-1, keepdims=True) + EPS)
                groups.append((xv * rstd * g_ref[...]).astype(BF16))
            h_all[rows, :] = jnp.concatenate(groups, axis=0)

        proj_ref[...] = _dot(h_all[rows, :], bbuf[j % 2])

        for step in range(N_DEV - 1):
            @pl.when(jnp.logical_and(j == step, i == max(ni - 2, 0)))
            def _(step=step):
                need, relay = _WALK[step + 1]
                copy(need, 0, me, me).wait_recv()
                if relay is not None:
                    relayed(relay, 0).start()
                load(step + 1, gathered[0].at[ord_ref[step + 1]]).start()

        @pl.when(jnp.logical_and(j == N_DEV - 1, i == ni - 1))
        def _():
            for a in range(1, n):
                for k in range(3):
                    copy(1 + k, a, me, me).wait_recv()
                    relayed(4 + k, a).start()
            for a in range(1, n):
                for k in (0, 4, 5, 6):
                    copy(k, a, me, me).wait_recv()
            for cp in first + [relayed(4 + k, a) for a in range(n) for k in range(3)]:
                cp.wait_send()
            for cp in mine:
                cp.wait()

    outs = pl.pallas_call(
        body, name=name,
        grid_spec=pltpu.PrefetchScalarGridSpec(
            num_scalar_prefetch=1, grid=(N_DEV, ni),
            in_specs=[pl.BlockSpec((tm, d), lambda j, i, o: (jnp.where(j == 0, i, ni - 1), 0)),
                      pl.BlockSpec((1, d), lambda j, i, o: (0, 0))] + [ANY] * n,
            out_specs=[pl.BlockSpec((tm, cols), lambda j, i, o: (i, o[j]))] + [ANY] * n,
            scratch_shapes=[pltpu.VMEM((t, d), BF16), pltpu.VMEM((2, d, cols), BF16),
                            pltpu.SemaphoreType.DMA((2,))] + rider.scratch),
        out_shape=[jax.ShapeDtypeStruct((t, N_DEV * cols), F32)] + rider.out_shape,
        compiler_params=_params("arbitrary", "arbitrary"),
    )(_gather_order(), x, norm_g, w_shard, *extras)
    return outs[0], list(outs[1:])


def _exchange(rider, name):
    r_in, r_out = len(rider.inputs), len(rider.out_shape)

    def body(*refs):
        ri, ro, rs = refs[:r_in], refs[r_in:r_in + r_out], refs[r_in + r_out:]
        rider.start(ri, ro, rs)
        rider.relay(ri, ro, rs)
        rider.finish(ri, ro, rs)

    return list(pl.pallas_call(
        body, name=name, in_specs=[ANY] * r_in, out_specs=[ANY] * r_out,
        out_shape=rider.out_shape, scratch_shapes=rider.scratch)(*rider.inputs))


MM_CAP_MN = 1024
MM_CAP_M_GRAD = 1408
MM_CAP_N = 1536
MM_CAP_K = 3072
MM_CAP_K_TOKENS = 2048
MM_CAP_K_RMS = 8192
MM_CAP_M_RMS = 512
NORM_ROWS = 256
RING_SLOTS = 3


def _mm(a, b, mode, name, out_dtype=F32, res=None, out_block=None, epilogue=None, extra=None, norm_g=None,
        norm_b=False, rider=None):
    a3, b3 = a.ndim == 3, b.ndim == 3
    um = un = uk = None
    if mode in ("nn", "nt"):
        if a3:
            m, uk = a.shape[1:]
            k = a.shape[0] * uk
        else:
            m, k = a.shape
    else:
        if a3:
            k, um = a.shape[1:]
            m = a.shape[0] * um
        else:
            k, m = a.shape
    if mode in ("nn", "tn"):
        if b3:
            kb, un = b.shape[1:]
            n = b.shape[0] * un
        else:
            kb, n = b.shape
        assert kb == k, (a.shape, b.shape, mode)
    else:
        if b3:
            n, ukb = b.shape[1:]
            assert b.shape[0] * ukb == k and uk in (None, ukb), (a.shape, b.shape, mode)
            uk = ukb
        else:
            n, kb = b.shape
            assert kb == k, (a.shape, b.shape, mode)
    if out_block is not None:
        assert un in (None, out_block)
        un = out_block

    def tile(dim, unit, cap, align):
        if unit is None:
            return _pick(dim, cap, align), 1
        c = max(1, cap // unit)
        while (dim // unit) % c:
            c -= 1
        return unit, c

    cap_m = MM_CAP_M_GRAD if mode == "tn" else (MM_CAP_M_RMS if epilogue == "rms_bwd" else MM_CAP_MN)
    um, cm = tile(m, um, cap_m, 128 if mode == "tn" else 16)
    un, cn = tile(n, un, MM_CAP_N, 128)
    cap_k = MM_CAP_K_TOKENS if mode == "tn" else (MM_CAP_K_RMS if epilogue == "rms_bwd" else MM_CAP_K)
    uk, ck = tile(k, uk, cap_k, 128)
    if epilogue == "rms_bwd":
        assert mode != "tn" and n == D_MODEL and cm == cn == 1 and res is None and out_block is None
    if epilogue == "loss":
        assert n == D_MODEL and cm == cn == 1 and res is not None and out_block is None
    if norm_g is not None and norm_b:
        assert mode == "tn" and not b3 and n == D_MODEL and cn == 1
    elif norm_g is not None:
        assert not a3 and (m if mode == "tn" else k) == D_MODEL and (cm if mode == "tn" else ck) == 1
    if epilogue == "swiglu":
        assert res is None and ((mode == "nn" and b3 and out_block is None) or
                                (mode == "nt" and not b3 and out_block is not None))
        cn = 2
    if epilogue == "swiglu_bwd":
        assert mode == "nt" and out_block is not None and extra is not None and res is None
        cn = 1
    tm, tn, tk = cm * um, cn * un, ck * uk
    nk = k // tk
    ringed = epilogue == "swiglu_bwd"
    assert not ringed or nk == 1
    dot = {"nn": _dot, "nt": _dot_nt, "tn": _dot_tn}[mode]
    half = n // un // 2
    blocked_out = out_block is not None or epilogue in ("swiglu", "swiglu_bwd")
    extras = [] if extra is None else (list(extra) if isinstance(extra, (tuple, list)) else [extra])

    def sl(idx, unit, count):
        return slice(None) if count == 1 else slice(idx * unit, (idx + 1) * unit)

    def body(*refs):
        a_ref, b_ref = refs[0], refs[1]
        pos = 2
        r_ref = ng_ref = None
        if res is not None:
            r_ref, pos = refs[pos], pos + 1
        e_refs, pos = refs[pos:pos + len(extras)], pos + len(extras)
        if norm_g is not None:
            ng_ref, pos = refs[pos], pos + 1
        kk = pl.program_id(2)
        if ringed:
            outs, acc_ref, ring_ref, ring_sem = refs[pos:-3], refs[-3], refs[-2], refs[-1]
            nj = n // tn
            step, steps = pl.program_id(0) * nj + pl.program_id(1), (m // tm) * nj

            def gu_copies(s):
                row = s // nj * tm
                row = row if isinstance(row, int) else pl.multiple_of(row, tm)
                return [pltpu.make_async_copy(e_refs[0].at[h, s % nj, pl.ds(row, tm), :],
                                              ring_ref.at[s % RING_SLOTS, h], ring_sem.at[s % RING_SLOTS, h])
                        for h in range(2)]

            @pl.when(step == 0)
            def _():
                for s in range(min(RING_SLOTS - 1, steps)):
                    for cp in gu_copies(s):
                        cp.start()

            @pl.when(step + (RING_SLOTS - 1) < steps)
            def _():
                for cp in gu_copies(step + (RING_SLOTS - 1)):
                    cp.start()

            for cp in gu_copies(step):
                cp.wait()

        else:
            outs, acc_ref = refs[pos:-1], refs[-1]

        def normed(x_ref):
            groups = []
            for r in range(0, x_ref.shape[0], NORM_ROWS):
                xv = x_ref[r:r + NORM_ROWS, :]
                rstd = lax.rsqrt(jnp.mean(xv * xv, axis=-1, keepdims=True) + EPS)
                groups.append((xv * rstd * ng_ref[...]).astype(BF16))
            return jnp.concatenate(groups, axis=0)

        def a_blk(mi, ki):
            if norm_g is not None and not norm_b:
                return normed(a_ref)
            if mode in ("nn", "nt"):
                return a_ref[ki] if a3 else a_ref[:, sl(ki, uk, ck)]
            return a_ref[mi] if a3 else a_ref[:, sl(mi, um, cm)]

        def b_blk(ki, ni):
            if norm_b:
                return normed(b_ref)
            if epilogue == "swiglu":
                return b_ref[ni, 0]
            if mode in ("nn", "tn"):
                return b_ref[ni] if b3 else b_ref[sl(ki, uk, ck), sl(ni, un, cn)]
            return b_ref[ki][sl(ni, un, cn), :] if b3 else b_ref[sl(ni, un, cn), sl(ki, uk, ck)]

        parts = {}
        for mi in range(cm):
            for ni in range(cn):
                part = None
                for ki in range(ck):
                    d = dot(a_blk(mi, ki).astype(BF16), b_blk(ki, ni).astype(BF16))
                    part = d if part is None else part + d
                parts[mi, ni] = part

        def finish(total):
            if epilogue == "swiglu":
                gate, up = total[0, 0], total[0, 1]
                outs[0][0, 0] = gate.astype(BF16)
                outs[0][1, 0] = up.astype(BF16)
                outs[1][0] = (gate * _sigmoid(gate) * up).astype(BF16)
                return
            if epilogue == "swiglu_bwd":
                dact = total[0, 0]
                gate, up = (ring_ref[step % RING_SLOTS, h].astype(F32) for h in range(2))
                sg = _sigmoid(gate)
                outs[0][0, 0] = (dact * up * (sg * (1.0 + gate * (1.0 - sg)))).astype(BF16)
                outs[0][1, 0] = (dact * (gate * sg)).astype(BF16)
                return
            if epilogue == "rms_bwd":
                x_ref, g_ref, dres_ref = e_refs
                dh, dg = total[0, 0], None
                for r in range(0, tm, NORM_ROWS):
                    rows = slice(r, r + NORM_ROWS)
                    xv, dhv = x_ref[rows, :], dh[rows, :]
                    rstd = lax.rsqrt(jnp.mean(xv * xv, axis=-1, keepdims=True) + EPS)
                    xh = xv * rstd
                    dyg = dhv * g_ref[...]
                    c = jnp.mean(dyg * xh, axis=-1, keepdims=True)
                    outs[0][rows, :] = dres_ref[rows, :] + rstd * (dyg - xh * c)
                    part = jnp.sum(dhv * xh, axis=0, keepdims=True)
                    dg = part if dg is None else dg + part
                _accumulate(outs[1], dg, pl.program_id(0))
                return
            if epilogue == "loss":
                diff = r_ref[...] + total[0, 0] - e_refs[0][...]
                outs[0][...] = diff * (1.0 / n)
                sq = jnp.sum(jnp.sum(diff * diff, axis=-1, keepdims=True), axis=0, keepdims=True)
                _accumulate(outs[1], sq * (0.5 / n), pl.program_id(0))
                return
            for (mi, ni), val in total.items():
                rows, cols = sl(mi, um, cm), sl(ni, un, cn)
                if res is not None:
                    val = r_ref[rows, cols] + val
                if blocked_out:
                    outs[0][ni, rows] = val.astype(out_dtype)
                else:
                    outs[0][rows, cols] = val.astype(out_dtype)

        if nk == 1:
            finish(parts)
        else:
            @pl.when(kk == 0)
            def _():
                for (mi, ni), val in parts.items():
                    acc_ref[mi * cn + ni] = val

            @pl.when(jnp.logical_and(kk > 0, kk < nk - 1))
            def _():
                for (mi, ni), val in parts.items():
                    acc_ref[mi * cn + ni] += val

            @pl.when(kk == nk - 1)
            def _():
                finish({key: acc_ref[key[0] * cn + key[1]] + val for key, val in parts.items()})

    if mode in ("nn", "nt"):
        a_spec = (pl.BlockSpec((ck, tm, uk), lambda i, j, kk: (kk, i, 0)) if a3
                  else pl.BlockSpec((tm, tk), lambda i, j, kk: (i, kk)))
    else:
        a_spec = (pl.BlockSpec((cm, tk, um), lambda i, j, kk: (i, kk, 0)) if a3
                  else pl.BlockSpec((tk, tm), lambda i, j, kk: (kk, i)))
    pair_spec = pl.BlockSpec((2, 1, tm, un), lambda i, j, kk: (0, j, i, 0))
    row_spec = pl.BlockSpec((tm, tn), lambda i, j, kk: (i, 0))
    vec_spec = pl.BlockSpec((1, tn), lambda i, j, kk: (0, 0))
    if epilogue == "swiglu" and mode == "nn":
        b = b.reshape(2, half, k, un)
        b_spec = pl.BlockSpec((2, 1, tk, un), lambda i, j, kk: (0, j, kk, 0))
    elif epilogue == "swiglu":
        b = b.reshape(2, half, un, k)
        b_spec = pl.BlockSpec((2, 1, un, tk), lambda i, j, kk: (0, j, 0, kk))
    elif mode in ("nn", "tn"):
        b_spec = (pl.BlockSpec((cn, tk, un), lambda i, j, kk: (j, kk, 0)) if b3
                  else pl.BlockSpec((tk, tn), lambda i, j, kk: (kk, j)))
    else:
        b_spec = (pl.BlockSpec((ck, tn, uk), lambda i, j, kk: (kk, j, 0)) if b3
                  else pl.BlockSpec((tn, tk), lambda i, j, kk: (j, kk)))
    if epilogue == "swiglu":
        out_specs = [pair_spec, pl.BlockSpec((1, tm, un), lambda i, j, kk: (j, i, 0))]
        out_shape = [jax.ShapeDtypeStruct((2, half, m, un), BF16), jax.ShapeDtypeStruct((half, m, un), BF16)]
    elif epilogue == "swiglu_bwd":
        out_specs = [pair_spec]
        out_shape = [jax.ShapeDtypeStruct(extra.shape, BF16)]
    elif epilogue == "rms_bwd":
        out_specs = [row_spec, vec_spec]
        out_shape = [jax.ShapeDtypeStruct((m, n), F32), jax.ShapeDtypeStruct((1, n), F32)]
    elif epilogue == "loss":
        out_specs = [row_spec, pl.BlockSpec((1, 1), lambda i, j, kk: (0, 0))]
        out_shape = [jax.ShapeDtypeStruct((m, n), F32), jax.ShapeDtypeStruct((1, 1), F32)]
    elif blocked_out:
        out_specs = [pl.BlockSpec((cn, tm, un), lambda i, j, kk: (j, i, 0))]
        out_shape = [jax.ShapeDtypeStruct((n // un, m, un), out_dtype)]
    else:
        out_specs = [pl.BlockSpec((tm, tn), lambda i, j, kk: (i, j))]
        out_shape = [jax.ShapeDtypeStruct((m, n), out_dtype)]
    in_specs, args = [a_spec, b_spec], [a, b]
    if res is not None:
        in_specs.append(pl.BlockSpec((tm, tn), lambda i, j, kk: (i, j)))
        args.append(res)
    if ringed:
        in_specs.append(ANY)
    elif epilogue == "rms_bwd":
        in_specs += [row_spec, vec_spec, row_spec]
    elif epilogue == "loss":
        in_specs.append(row_spec)
    args += extras
    if norm_g is not None:
        in_specs.append(pl.BlockSpec((1, D_MODEL), lambda i, j, kk: (0, 0)))
        args.append(norm_g)
    ordered = ringed or epilogue in ("rms_bwd", "loss")
    semantics = ("arbitrary",) * 3 if ordered else ("parallel", "parallel", "arbitrary")
    scratch = [pltpu.VMEM((cm * cn, um, un), F32)]
    if ringed:
        scratch += [pltpu.VMEM((RING_SLOTS, 2, tm, un), BF16), pltpu.SemaphoreType.DMA((RING_SLOTS, 2))]
    out = _call(body, name, (m // tm, n // tn, nk), in_specs, out_specs, out_shape, scratch, semantics, args, rider)
    return out if epilogue in ("swiglu", "rms_bwd", "loss") else out[0]


def _head_sums(v, ind):
    return _dot(v.astype(BF16), ind)


def _head_spread(per_head, ind):
    hi, lo = _split2(per_head)
    return _dot_nt(hi, ind) + _dot_nt(lo, ind)


def _head_rstd(xv, ind):
    return _head_spread(lax.rsqrt(_head_sums(xv * xv, ind) * (1.0 / ATT_DH) + EPS), ind)


def _hn_bwd_math(xv, gv, ind, dyv, scale):
    rstd = _head_rstd(xv, ind)
    xh = xv * rstd
    dyn = dyv * scale
    dyg = dyn * gv
    dx = rstd * (dyg - xh * _head_spread(_head_sums(dyg * xh, ind) * (1.0 / ATT_DH), ind))
    return dx, jnp.sum(dyn * xh, axis=0, keepdims=True)


def _q_hnorm(x, g_tiled, bd, scale, name):
    t, d = x.shape
    tm = _pick(t, 512, 16)

    def body(x_ref, g_ref, bd_ref, o_ref):
        xv = x_ref[...]
        o_ref[...] = (xv * _head_rstd(xv, bd_ref[...]) * g_ref[...] * scale).astype(BF16)

    return pl.pallas_call(
        body, name=name, grid=(t // tm,),
        in_specs=[pl.BlockSpec((tm, d), lambda i: (i, 0)), pl.BlockSpec((1, d), lambda i: (0, 0)),
                  pl.BlockSpec((d, LANES), lambda i: (0, 0))],
        out_specs=pl.BlockSpec((tm, d), lambda i: (i, 0)),
        out_shape=jax.ShapeDtypeStruct((t, d), BF16),
        compiler_params=_params("parallel"),
    )(x, g_tiled, bd)


def _q_dhnorm(x, g_tiled, bd, dy, scale, name):
    t, d = x.shape
    tm = _pick(t, 512, 16)

    def body(x_ref, g_ref, bd_ref, dy_ref, dx_ref, dg_ref):
        dx, part = _hn_bwd_math(x_ref[...], g_ref[...], bd_ref[...], dy_ref[...], scale)
        dx_ref[...] = dx.astype(BF16)
        _accumulate(dg_ref, part, pl.program_id(0))

    row = pl.BlockSpec((tm, d), lambda i: (i, 0))
    vec = pl.BlockSpec((1, d), lambda i: (0, 0))
    return pl.pallas_call(
        body, name=name, grid=(t // tm,),
        in_specs=[row, vec, pl.BlockSpec((d, LANES), lambda i: (0, 0)), row],
        out_specs=[row, vec],
        out_shape=[jax.ShapeDtypeStruct((t, d), BF16), jax.ShapeDtypeStruct((1, d), F32)],
        compiler_params=_params("arbitrary"),
    )(x, g_tiled, bd, dy)


def _kv_prep(kv, g_tiled, bd, name):
    t = kv.shape[0]
    d = D_MODEL
    tm = K_PAD
    assert t % tm == 0

    def body(k_ref, v_ref, g_ref, bd_ref, kp_ref, vp_ref):
        i = pl.program_id(0)

        @pl.when(i == 0)
        def _():
            kp_ref[...] = jnp.zeros_like(kp_ref)
            vp_ref[...] = jnp.zeros_like(vp_ref)

        @pl.when(i > 0)
        def _():
            xv = k_ref[...]
            kp_ref[...] = (xv * _head_rstd(xv, bd_ref[...]) * g_ref[...]).astype(BF16)
            vp_ref[...] = v_ref[...].astype(BF16)

    shp = jax.ShapeDtypeStruct((t + K_PAD, d), BF16)
    out = pl.BlockSpec((tm, d), lambda i: (i, 0))
    return pl.pallas_call(
        body, name=name, grid=(t // tm + 1,),
        in_specs=[pl.BlockSpec((tm, d), lambda i: (jnp.maximum(i - 1, 0), 0)),
                  pl.BlockSpec((tm, d), lambda i: (jnp.maximum(i - 1, 0), 1)),
                  pl.BlockSpec((1, d), lambda i: (0, 0)), pl.BlockSpec((d, LANES), lambda i: (0, 0))],
        out_specs=[out, out], out_shape=[shp, shp],
        compiler_params=_params("arbitrary"),
    )(kv, kv, g_tiled, bd)


def _kv_dprep(kv, g_tiled, bd, dkp_t, dvp_t, name):
    t = kv.shape[0]
    d = D_MODEL
    tm = K_PAD

    def body(k_ref, g_ref, bd_ref, dk_ref, dv_ref, o_ref, dg_ref):
        dx, part = _hn_bwd_math(k_ref[...], g_ref[...], bd_ref[...], dk_ref[...].T, 1.0)
        o_ref[:, :d] = dx.astype(BF16)
        o_ref[:, d:] = dv_ref[...].T.astype(BF16)
        _accumulate(dg_ref, part, pl.program_id(0))

    vec = pl.BlockSpec((1, d), lambda i: (0, 0))
    padded = pl.BlockSpec((d, tm), lambda i: (0, i + 1))
    return pl.pallas_call(
        body, name=name, grid=(t // tm,),
        in_specs=[pl.BlockSpec((tm, d), lambda i: (i, 0)), vec, pl.BlockSpec((d, LANES), lambda i: (0, 0)),
                  padded, padded],
        out_specs=[pl.BlockSpec((tm, 2 * d), lambda i: (i, 0)), vec],
        out_shape=[jax.ShapeDtypeStruct((t, 2 * d), BF16), jax.ShapeDtypeStruct((1, d), F32)],
        compiler_params=_params("arbitrary"),
    )(kv, g_tiled, bd, dkp_t, dvp_t)


def _ret_consts(t):
    h = np.arange(RET_HEADS, dtype=np.float32)
    lg = np.log(np.float32(1.0) - np.float32(2.0) ** (np.float32(-5.0) - h)).astype(np.float32)
    tt = np.arange(CHUNK, dtype=np.float32)
    intra = np.exp(lg[:, None, None] * np.abs(tt[:, None] - tt[None, :])).astype(np.float32)
    q_dec = np.exp(lg[:, None] * (tt + 1.0)).astype(np.float32)
    k_dec = np.exp(lg[:, None] * (CHUNK - 1.0 - tt)).astype(np.float32)
    s_dec = [float(v) for v in np.exp(lg * np.float32(CHUNK)).astype(np.float32)]
    qd = np.broadcast_to(q_dec[:, :, None], (RET_HEADS, CHUNK, RET_DK)).copy()
    kd = np.broadcast_to(k_dec[:, :, None], (RET_HEADS, CHUNK, RET_DK)).copy()
    half = RET_DK // 2
    inv_freq = np.float32(ROPE_BASE) ** (-np.arange(half, dtype=np.float32) / np.float32(half))
    ang = np.arange(t, dtype=np.float32)[:, None] * inv_freq[None, :]
    return jnp.asarray(intra), jnp.asarray(qd), jnp.asarray(kd), s_dec, jnp.asarray(np.cos(ang)), jnp.asarray(np.sin(ang))


def _rope(x, cos, sin):
    half = RET_DK // 2
    x1, x2 = x[:, :half], x[:, half:]
    return jnp.concatenate([x1 * cos - x2 * sin, x1 * sin + x2 * cos], axis=-1)


def _unrope(d, cos, sin):
    half = RET_DK // 2
    d1, d2 = d[:, :half], d[:, half:]
    return jnp.concatenate([d1 * cos + d2 * sin, d2 * cos - d1 * sin], axis=-1)


def _ret_slices(h):
    q = slice(h * RET_DK, (h + 1) * RET_DK)
    k = slice(RET_Q_COLS + h * RET_DK, RET_Q_COLS + (h + 1) * RET_DK)
    v = slice(2 * RET_Q_COLS + h * RET_DV, 2 * RET_Q_COLS + (h + 1) * RET_DV)
    g = slice(2 * RET_Q_COLS + RET_V_COLS + h * RET_DV, 2 * RET_Q_COLS + RET_V_COLS + (h + 1) * RET_DV)
    o = slice(h * RET_DV, (h + 1) * RET_DV)
    return q, k, v, g, o


def _ret_fwd(proj, gn, consts, name, rider=None):
    t, cols = proj.shape
    n = t // CHUNK
    intra, qd, kd, s_dec, cos, sin = consts
    k_scale = RET_DK ** -0.5

    def body(p_hbm, cos_ref, sin_ref, intra_ref, qd_ref, kd_ref, gn_ref, y_ref, o_ref, st_ref, state,
             ring_ref, ring_sem):
        i = pl.program_id(0)

        @pl.when(i == 0)
        def _():
            state[...] = jnp.zeros_like(state)

        def p_copy(s):
            first = s * step if isinstance(s, int) else pl.multiple_of(s * step, step)
            return pltpu.make_async_copy(p_hbm.at[pl.ds(first, step), :], ring_ref.at[s % RING_SLOTS],
                                         ring_sem.at[s % RING_SLOTS])

        @pl.when(i == 0)
        def _():
            for s in range(min(RING_SLOTS - 1, steps)):
                p_copy(s).start()

        @pl.when(i + (RING_SLOTS - 1) < steps)
        def _():
            p_copy(i + (RING_SLOTS - 1)).start()

        p_copy(i).wait()
        p_ref = ring_ref.at[i % RING_SLOTS]

        for c in range(RET_STEP):
            rows = slice(c * CHUNK, (c + 1) * CHUNK)
            cosv, sinv = cos_ref[rows, :], sin_ref[rows, :]
            for h in range(RET_HEADS):
                qs, ks, vs, gs, os_ = _ret_slices(h)
                qr = _rope(p_ref[rows, qs], cosv, sinv)
                kr = _rope(p_ref[rows, ks], cosv, sinv) * k_scale
                vb = p_ref[rows, vs].astype(BF16)
                gv = p_ref[rows, gs]
                scores = _dot_nt(qr.astype(BF16), kr.astype(BF16)) * intra_ref[h]
                s_old = state[h]
                s_old_b = s_old.astype(BF16)
                st_ref[c, h] = s_old_b
                o = _dot(scores.astype(BF16), vb) + _dot((qr * qd_ref[h]).astype(BF16), s_old_b)
                state[h] = s_old * s_dec[h] + _dot_tn((kr * kd_ref[h]).astype(BF16), vb)
                rstd = lax.rsqrt(jnp.mean(o * o, axis=-1, keepdims=True) + EPS)
                on = o * rstd * gn_ref[:, os_]
                o_ref[rows, os_] = o
                y_ref[rows, os_] = (gv * _sigmoid(gv) * on).astype(BF16)

    full3 = lambda a: pl.BlockSpec(a.shape, lambda i: (0, 0, 0))
    step, steps = RET_STEP * CHUNK, n // RET_STEP
    return _call(
        body, name, (steps,),
        [ANY,
         pl.BlockSpec((step, RET_DK // 2), lambda i: (i, 0)),
         pl.BlockSpec((step, RET_DK // 2), lambda i: (i, 0)),
         full3(intra), full3(qd), full3(kd),
         pl.BlockSpec((1, RET_V_COLS), lambda i: (0, 0))],
        [pl.BlockSpec((step, RET_V_COLS), lambda i: (i, 0)),
         pl.BlockSpec((step, RET_V_COLS), lambda i: (i, 0)),
         pl.BlockSpec((RET_STEP, RET_HEADS, RET_DK, RET_DV), lambda i: (i, 0, 0, 0))],
        [jax.ShapeDtypeStruct((t, RET_V_COLS), BF16),
         jax.ShapeDtypeStruct((t, RET_V_COLS), F32),
         jax.ShapeDtypeStruct((n, RET_HEADS, RET_DK, RET_DV), BF16)],
        [pltpu.VMEM((RET_HEADS, RET_DK, RET_DV), F32), pltpu.VMEM((RING_SLOTS, step, cols), proj.dtype),
         pltpu.SemaphoreType.DMA((RING_SLOTS,))], ("arbitrary",),
        (proj, cos, sin, intra, qd, kd, gn), rider)


def _ret_bwd(proj, gn, o_saved, states, dy, consts, name, rider=None):
    t, cols = proj.shape
    n = t // CHUNK
    intra, qd, kd, s_dec, cos, sin = consts
    k_scale = RET_DK ** -0.5

    def body(p_hbm, cos_ref, sin_ref, intra_ref, qd_ref, kd_ref, gn_ref, o_ref, st_ref, dy_ref,
             dp_ref, dgn_ref, dstate, ring_ref, ring_sem):
        i = pl.program_id(0)

        @pl.when(i == 0)
        def _():
            dstate[...] = jnp.zeros_like(dstate)

        def p_copy(s):
            first = (steps - 1 - s) * step
            first = first if isinstance(first, int) else pl.multiple_of(first, step)
            return pltpu.make_async_copy(p_hbm.at[pl.ds(first, step), :], ring_ref.at[s % RING_SLOTS],
                                         ring_sem.at[s % RING_SLOTS])

        @pl.when(i == 0)
        def _():
            for s in range(min(RING_SLOTS - 1, steps)):
                p_copy(s).start()

        @pl.when(i + (RING_SLOTS - 1) < steps)
        def _():
            p_copy(i + (RING_SLOTS - 1)).start()

        p_copy(i).wait()
        p_ref = ring_ref.at[i % RING_SLOTS]

        dgn = None
        for c in reversed(range(RET_STEP)):
            rows = slice(c * CHUNK, (c + 1) * CHUNK)
            cosv, sinv = cos_ref[rows, :], sin_ref[rows, :]
            dgn_parts = []
            for h in range(RET_HEADS):
                qs, ks, vs, gs, os_ = _ret_slices(h)
                qr = _rope(p_ref[rows, qs], cosv, sinv)
                kr = _rope(p_ref[rows, ks], cosv, sinv) * k_scale
                qb, kb = qr.astype(BF16), kr.astype(BF16)
                vb = p_ref[rows, vs].astype(BF16)
                gv = p_ref[rows, gs]
                ov = o_ref[rows, os_]
                dyv = dy_ref[rows, os_]
                gnv = gn_ref[:, os_]
                sg = _sigmoid(gv)
                rstd = lax.rsqrt(jnp.mean(ov * ov, axis=-1, keepdims=True) + EPS)
                oh = ov * rstd
                d_on = dyv * (gv * sg)
                dg = dyv * (oh * gnv) * (sg * (1.0 + gv * (1.0 - sg)))
                dgn_parts.append(jnp.sum(d_on * oh, axis=0, keepdims=True))
                d_oh = d_on * gnv
                do = rstd * (d_oh - oh * jnp.mean(d_oh * oh, axis=-1, keepdims=True))
                dob = do.astype(BF16)
                mask = intra_ref[h]
                a_b = (_dot_nt(qb, kb) * mask).astype(BF16)
                da_b = (_dot_nt(dob, vb) * mask).astype(BF16)
                ds_new = dstate[h]
                ds_new_b = ds_new.astype(BF16)
                s_old_b = st_ref[c, h]
                qdv, kdv = qd_ref[h], kd_ref[h]
                dv = _dot_tn(a_b, dob) + _dot((kr * kdv).astype(BF16), ds_new_b)
                dqr = _dot(da_b, kb) + _dot_nt(dob, s_old_b) * qdv
                dkr = _dot_tn(da_b, qb) + _dot_nt(vb, ds_new_b) * kdv
                dstate[h] = ds_new * s_dec[h] + _dot_tn((qr * qdv).astype(BF16), dob)
                dp_ref[rows, qs] = _unrope(dqr, cosv, sinv).astype(BF16)
                dp_ref[rows, ks] = _unrope(dkr * k_scale, cosv, sinv).astype(BF16)
                dp_ref[rows, vs] = dv.astype(BF16)
                dp_ref[rows, gs] = dg.astype(BF16)
            part = jnp.concatenate(dgn_parts, axis=-1)
            dgn = part if dgn is None else dgn + part
        _accumulate(dgn_ref, dgn, i)

    steps = n // RET_STEP
    step = RET_STEP * CHUNK
    rev = lambda i: (steps - 1 - i, 0)
    full3 = lambda a: pl.BlockSpec(a.shape, lambda i: (0, 0, 0))
    return _call(
        body, name, (steps,),
        [ANY,
         pl.BlockSpec((step, RET_DK // 2), rev),
         pl.BlockSpec((step, RET_DK // 2), rev),
         full3(intra), full3(qd), full3(kd),
         pl.BlockSpec((1, RET_V_COLS), lambda i: (0, 0)),
         pl.BlockSpec((step, RET_V_COLS), rev),
         pl.BlockSpec((RET_STEP, RET_HEADS, RET_DK, RET_DV), lambda i: (steps - 1 - i, 0, 0, 0)),
         pl.BlockSpec((step, RET_V_COLS), rev)],
        [pl.BlockSpec((step, cols), rev),
         pl.BlockSpec((1, RET_V_COLS), lambda i: (0, 0))],
        [jax.ShapeDtypeStruct((t, cols), BF16),
         jax.ShapeDtypeStruct((1, RET_V_COLS), F32)],
        [pltpu.VMEM((RET_HEADS, RET_DK, RET_DV), F32), pltpu.VMEM((RING_SLOTS, step, cols), proj.dtype),
         pltpu.SemaphoreType.DMA((RING_SLOTS,))], ("arbitrary",),
        (proj, cos, sin, intra, qd, kd, gn, o_saved, states, dy), rider)


def _att_common(q_ref, kp_ref, vp_ref, sub):
    blk = pl.program_id(1) * ATT_SUBS + sub
    start = pl.multiple_of(blk * Q_BLOCK, Q_BLOCK)
    kw = kp_ref[pl.ds(start, K_WINDOW), :]
    vw = vp_ref[pl.ds(start, K_WINDOW), :]
    kvalid = blk * Q_BLOCK - K_PAD + lax.broadcasted_iota(jnp.int32, (1, K_WINDOW), 1) >= 0
    lane = lax.broadcasted_iota(jnp.int32, (1, LANES), 1)
    qrows = slice(sub * Q_BLOCK, (sub + 1) * Q_BLOCK)
    return start, qrows, q_ref[qrows, :], kw, vw, kvalid, (lane < ATT_DH, lane >= ATT_DH)


def _row_groups():
    return [slice(r * ATT_ROWS, (r + 1) * ATT_ROWS) for r in range(Q_BLOCK // ATT_ROWS)]


def _lane_copies(x):
    return jnp.tile(x, (1, K_WINDOW // LANES))


def _att_specs(t, tp):
    qspec = pl.BlockSpec((ATT_SUBS * Q_BLOCK, LANES), lambda h, i: (i, h))
    kspec = pl.BlockSpec((tp, LANES), lambda h, i: (0, h))
    bspec = pl.BlockSpec((2, Q_BLOCK, K_WINDOW), lambda h, i: (h, 0, 0))
    return qspec, kspec, bspec


def _att_fwd(q, kp, vp, bias, name, rider=None):
    t, d = q.shape
    tp = kp.shape[0]

    def body(q_ref, kp_ref, vp_ref, bias_ref, o_ref, lse_ref, s_scr, p_scr, lse_scr, inv_scr):
        for sub in range(ATT_SUBS):
            _, qrows, q2, kw, vw, kvalid, sel = _att_common(q_ref, kp_ref, vp_ref, sub)
            for hh in range(2):
                s_scr[sub, hh] = _dot_nt(jnp.where(sel[hh], q2, 0), kw)
            for hh in range(2):
                for rows in _row_groups():
                    s = jnp.where(kvalid, s_scr[sub, hh, rows, :] + bias_ref[hh, rows, :], NEG)
                    m = jnp.max(s, axis=-1, keepdims=True)
                    e = jnp.exp(s - m)
                    l = jnp.sum(e, axis=-1, keepdims=True)
                    p_scr[sub, hh, rows, :] = e.astype(BF16)
                    inv_scr[sub, hh, rows, :] = jnp.broadcast_to(1.0 / l, (ATT_ROWS, LANES))
                    lse_scr[sub, hh, rows, :] = jnp.broadcast_to(m + jnp.log(l), (ATT_ROWS, LANES))
            outs = [_dot(p_scr[sub, hh], vw) * inv_scr[sub, hh] for hh in range(2)]
            o_ref[qrows, :] = jnp.where(sel[0], outs[0], outs[1]).astype(BF16)
            lse_ref[qrows, :] = jnp.where(sel[0], lse_scr[sub, 0], lse_scr[sub, 1])

    qspec, kspec, bspec = _att_specs(t, tp)
    return _call(body, name, (d // LANES, t // (ATT_SUBS * Q_BLOCK)), [qspec, kspec, kspec, bspec], [qspec, qspec],
                 [jax.ShapeDtypeStruct((t, d), BF16), jax.ShapeDtypeStruct((t, d), F32)],
                 [pltpu.VMEM((ATT_SUBS, 2, Q_BLOCK, K_WINDOW), F32),
                  pltpu.VMEM((ATT_SUBS, 2, Q_BLOCK, K_WINDOW), BF16),
                  pltpu.VMEM((ATT_SUBS, 2, Q_BLOCK, LANES), F32),
                  pltpu.VMEM((ATT_SUBS, 2, Q_BLOCK, LANES), F32)],
                 ("parallel", "arbitrary"), (q, kp, vp, bias), rider)


def _att_bwd(q, kp, vp, bias, do, o, lse, name, rider=None):
    t, d = q.shape
    tp = kp.shape[0]

    def body(q_ref, kp_ref, vp_ref, bias_ref, do_ref, o_ref, lse_ref, dq_ref, dkp_ref, dvp_ref, db_ref,
             s_scr, dp_scr, p_scr, ds_scr, row_scr):
        @pl.when(pl.program_id(1) == 0)
        def _():
            dkp_ref[...] = jnp.zeros_like(dkp_ref)
            dvp_ref[...] = jnp.zeros_like(dvp_ref)
            db_ref[...] = jnp.zeros_like(db_ref)

        for sub in range(ATT_SUBS):
            start, qrows, q2, kw, vw, kvalid, sel = _att_common(q_ref, kp_ref, vp_ref, sub)
            do2 = do_ref[qrows, :]
            qm = [jnp.where(sel[hh], q2, 0) for hh in range(2)]
            dom = [jnp.where(sel[hh], do2, 0) for hh in range(2)]
            do_o = do2.astype(F32) * o_ref[qrows, :].astype(F32)
            lse2 = lse_ref[qrows, :]
            for hh in range(2):
                s_scr[sub, hh] = _dot_nt(qm[hh], kw)
                dp_scr[sub, hh] = _dot_nt(dom[hh], vw)
                lse_h = jnp.max(jnp.where(sel[hh], lse2, NEG), axis=-1, keepdims=True)
                delta = jnp.sum(jnp.where(sel[hh], do_o, 0.0), axis=-1, keepdims=True)
                row_scr[sub, hh, 0] = jnp.broadcast_to(lse_h, (Q_BLOCK, LANES))
                row_scr[sub, hh, 1] = jnp.broadcast_to(delta, (Q_BLOCK, LANES))
            for hh in range(2):
                for rows in _row_groups():
                    s = jnp.where(kvalid, s_scr[sub, hh, rows, :] + bias_ref[hh, rows, :], NEG)
                    p = jnp.exp(s - _lane_copies(row_scr[sub, hh, 0, rows, :]))
                    ds = p * (dp_scr[sub, hh, rows, :] - _lane_copies(row_scr[sub, hh, 1, rows, :]))
                    db_ref[hh, rows, :] += ds
                    p_scr[sub, hh, rows, :] = p.astype(BF16)
                    ds_scr[sub, hh, rows, :] = ds.astype(BF16)
            dqs = [_dot(ds_scr[sub, hh], kw) for hh in range(2)]
            dq_ref[qrows, :] = jnp.where(sel[0], dqs[0], dqs[1])
            dkp_ref[:, pl.ds(start, K_WINDOW)] += (_dot_tn(qm[0], ds_scr[sub, 0]) +
                                                   _dot_tn(qm[1], ds_scr[sub, 1]))
            dvp_ref[:, pl.ds(start, K_WINDOW)] += (_dot_tn(dom[0], p_scr[sub, 0]) +
                                                   _dot_tn(dom[1], p_scr[sub, 1]))

    qspec, kspec, bspec = _att_specs(t, tp)
    tspec = pl.BlockSpec((LANES, tp), lambda h, i: (h, 0))
    stage = lambda dt: pltpu.VMEM((ATT_SUBS, 2, Q_BLOCK, K_WINDOW), dt)
    return _call(body, name, (d // LANES, t // (ATT_SUBS * Q_BLOCK)),
                 [qspec, kspec, kspec, bspec, qspec, qspec, qspec],
                 [qspec, tspec, tspec, bspec],
                 [jax.ShapeDtypeStruct((t, d), F32),
                  jax.ShapeDtypeStruct((d, tp), F32),
                  jax.ShapeDtypeStruct((d, tp), F32),
                  jax.ShapeDtypeStruct((ATT_HEADS, Q_BLOCK, K_WINDOW), F32)],
                 [stage(F32), stage(F32), stage(BF16), stage(BF16),
                  pltpu.VMEM((ATT_SUBS, 2, 2, Q_BLOCK, LANES), F32)],
                 ("parallel", "arbitrary"), (q, kp, vp, bias, do, o, lse), rider)


def _rel_bin_matrix():
    rows = REL_DELTAS * 2 * REL_BLK
    rho = lax.broadcasted_iota(jnp.int32, (rows, REL_PAD), 0)
    col = lax.broadcasted_iota(jnp.int32, (rows, REL_PAD), 1)
    assert 2 * REL_BLK == 256
    delta = rho >> 8
    c = 255 - (rho & 255)
    dist = K_PAD + REL_BLK * (delta - (K_WINDOW // REL_BLK - 1)) + (c - (REL_BLK - 1))
    idx = jnp.clip(dist, -REL_CLIP, REL_CLIP) + REL_CLIP
    return col == idx


def _rel_expand(rel_pad, name):
    heads = rel_pad.shape[0]
    rows = REL_DELTAS * 2 * REL_BLK

    def body_bin(r_ref, o_ref):
        onehot = jnp.where(_rel_bin_matrix(), 1.0, 0.0).astype(BF16)
        hi, mid, lo = _split3(r_ref[...])
        o_ref[...] = _dot_nt(hi, onehot) + _dot_nt(mid, onehot) + _dot_nt(lo, onehot)

    by_delta = pl.pallas_call(
        body_bin, name=name + "_bin",
        out_shape=jax.ShapeDtypeStruct((heads, rows), F32),
        compiler_params=pltpu.CompilerParams(vmem_limit_bytes=VMEM_LIMIT_V7X),
    )(rel_pad)
    by_delta = by_delta.reshape(heads * REL_DELTAS, 2 * REL_BLK)

    def body_shift(t_ref, o_ref):
        tv = t_ref[...]
        for r in range(REL_BLK):
            o_ref[r] = pltpu.roll(tv, (r + REL_BLK) % (2 * REL_BLK), 1)[:, :REL_BLK]

    return pl.pallas_call(
        body_shift, name=name + "_shift",
        out_shape=jax.ShapeDtypeStruct((REL_BLK, heads * REL_DELTAS, REL_BLK), F32),
        compiler_params=pltpu.CompilerParams(vmem_limit_bytes=VMEM_LIMIT_V7X),
    )(by_delta)


def _bias_table(rel_bias, name):
    heads = rel_bias.shape[0]
    rel_pad = jnp.pad(rel_bias, ((0, 0), (0, REL_PAD - REL_TABLE)))
    tiles = _rel_expand(rel_pad, name)
    tiles = tiles.reshape(REL_BLK, heads, REL_DELTAS, REL_BLK).transpose(1, 2, 0, 3)
    na, nb = Q_BLOCK // REL_BLK, K_WINDOW // REL_BLK
    rows = [jnp.concatenate([tiles[:, a - b + nb - 1] for b in range(nb)], axis=-1) for a in range(na)]
    table = jnp.concatenate(rows, axis=-2)
    qc = np.arange(Q_BLOCK)[:, None] // CHUNK
    kc = np.arange(K_WINDOW)[None, :] // CHUNK
    band = (kc >= qc) & (kc <= qc + PAST_CHUNKS)
    return jnp.where(jnp.asarray(band)[None], table, NEG)


def _rel_reduce(db, name):
    heads = db.shape[0]
    na, nb = Q_BLOCK // REL_BLK, K_WINDOW // REL_BLK

    fold_heads = 4

    def body_fold(db_ref, g_ref):
        for hd in range(fold_heads):
            for delta in range(REL_DELTAS):
                acc = None
                for a in range(na):
                    b = a - (delta - (nb - 1))
                    if 0 <= b < nb:
                        tile = db_ref[hd, a * REL_BLK:(a + 1) * REL_BLK, b * REL_BLK:(b + 1) * REL_BLK]
                        acc = tile if acc is None else acc + tile
                g_ref[hd, delta] = acc

    folded = pl.pallas_call(
        body_fold, name=name + "_fold", grid=(heads // fold_heads,),
        in_specs=[pl.BlockSpec((fold_heads, Q_BLOCK, K_WINDOW), lambda h: (h, 0, 0))],
        out_specs=pl.BlockSpec((fold_heads, REL_DELTAS, REL_BLK, REL_BLK), lambda h: (h, 0, 0, 0)),
        out_shape=jax.ShapeDtypeStruct((heads, REL_DELTAS, REL_BLK, REL_BLK), F32),
        compiler_params=_params("parallel"),
    )(db)
    by_row = folded.transpose(2, 0, 1, 3).reshape(REL_BLK, heads * REL_DELTAS, REL_BLK)

    def body_diag(g_ref, d_ref):
        zeros = jnp.zeros((heads * REL_DELTAS, REL_BLK), F32)
        acc = None
        for r in range(REL_BLK):
            part = pltpu.roll(jnp.concatenate([g_ref[r], zeros], axis=1), REL_BLK - r, 1)
            acc = part if acc is None else acc + part
        d_ref[...] = acc

    diag = pl.pallas_call(
        body_diag, name=name + "_diag",
        out_shape=jax.ShapeDtypeStruct((heads * REL_DELTAS, 2 * REL_BLK), F32),
        compiler_params=pltpu.CompilerParams(vmem_limit_bytes=VMEM_LIMIT_V7X),
    )(by_row)
    diag = diag.reshape(heads, REL_DELTAS * 2 * REL_BLK)

    def body_bin(d_ref, o_ref):
        onehot = jnp.where(_rel_bin_matrix(), 1.0, 0.0).astype(BF16)
        hi, mid, lo = _split3(d_ref[...])
        o_ref[...] = _dot(hi, onehot) + _dot(mid, onehot) + _dot(lo, onehot)

    out = pl.pallas_call(
        body_bin, name=name + "_bin",
        out_shape=jax.ShapeDtypeStruct((heads, REL_PAD), F32),
        compiler_params=pltpu.CompilerParams(vmem_limit_bytes=VMEM_LIMIT_V7X),
    )(diag)
    return out[:, :REL_TABLE]


def _sum_leading(x, name):
    n, r, c = x.shape
    tr = _pick(r, 256, 8)

    def body(x_ref, o_ref):
        acc = x_ref[0].astype(F32)
        for k in range(1, n):
            acc = acc + x_ref[k].astype(F32)
        o_ref[...] = acc

    return pl.pallas_call(
        body, name=name, grid=(r // tr,),
        in_specs=[pl.BlockSpec((n, tr, c), lambda i: (0, i, 0))],
        out_specs=pl.BlockSpec((tr, c), lambda i: (i, 0)),
        out_shape=jax.ShapeDtypeStruct((r, c), F32),
        compiler_params=_params("parallel"),
    )(x)


def _pair_add(g, recv, parity, name):
    _, r, c = g.shape
    tr = _pick(r, 256, 16)

    def body(par_ref, g_ref, r_ref, o_ref):
        o_ref[...] = (g_ref[...].astype(F32) + r_ref[...].astype(F32)).astype(BF16)

    return pl.pallas_call(
        body, name=name,
        grid_spec=pltpu.PrefetchScalarGridSpec(
            num_scalar_prefetch=1, grid=(4, r // tr),
            in_specs=[pl.BlockSpec((1, tr, c), lambda k, i, par: (2 * k + par[0], i, 0)),
                      pl.BlockSpec((1, tr, c), lambda k, i, par: (k, i, 0))],
            out_specs=pl.BlockSpec((1, tr, c), lambda k, i, par: (k, i, 0))),
        out_shape=jax.ShapeDtypeStruct((4, r, c), BF16),
        compiler_params=_params("parallel", "parallel"),
    )(parity, g, recv)


def _adamw(w, g_parts, m, v, name):
    r, c = w.shape
    n = g_parts.shape[0]
    tr = _pick(r, 256, 16 if g_parts.dtype == BF16 else 8)
    c1 = 1.0 - ADAM_B1 ** ADAM_STEP
    c2 = 1.0 - ADAM_B2 ** ADAM_STEP

    def body(w_ref, g_ref, m_ref, v_ref, go_ref, d_ref, nm_ref, nv_ref):
        gv = g_ref[0].astype(F32)
        for k in range(1, n):
            gv = gv + g_ref[k].astype(F32)
        nm = ADAM_B1 * m_ref[...] + (1.0 - ADAM_B1) * gv
        nv = ADAM_B2 * v_ref[...] + (1.0 - ADAM_B2) * (gv * gv)
        go_ref[...] = gv
        d_ref[...] = -ADAM_LR * ((nm / c1) / (jnp.sqrt(nv / c2) + ADAM_EPS) + ADAM_WD * w_ref[...])
        nm_ref[...] = nm
        nv_ref[...] = nv

    spec = pl.BlockSpec((tr, c), lambda i: (i, 0))
    shp = jax.ShapeDtypeStruct((r, c), F32)
    return pl.pallas_call(
        body, name=name, grid=(r // tr,),
        in_specs=[spec, pl.BlockSpec((n, tr, c), lambda i: (0, i, 0)), spec, spec],
        out_specs=[spec] * 4, out_shape=[shp] * 4,
        compiler_params=_params("parallel"),
    )(w, g_parts, m, v)


BIG = (("a_w_in", 1), ("a_w_o", 0), ("a_w_gu", 0), ("a_w_down", 0), ("w_kv", 1),
       ("b_w_q", 0), ("b_w_o", 0), ("b_w_gu", 0), ("b_w_down", 0))
TRANSPOSED = ("a_w_gu", "b_w_gu")
FFN_BLK = 2 * FFN_HIDDEN // N_DEV

SMALL = (("a_norm_g", D_MODEL, True), ("a_gn_g", RET_V_COLS, True), ("a_ffn_norm_g", D_MODEL, True),
         ("kv_norm_g", D_MODEL, False), ("b_norm_g", D_MODEL, False), ("b_ffn_norm_g", D_MODEL, False),
         ("k_norm_g", ATT_DH, False), ("b_q_norm_g", ATT_DH, False),
         ("b_rel_bias", ATT_HEADS * REL_TABLE, False))
SMALL_ROWS, SMALL_COLS = 16, 1024


def _pack_small(vals, last=None):
    flat = jnp.concatenate([vals[n].reshape(-1) for n, _, _ in SMALL])
    room = SMALL_ROWS * SMALL_COLS - flat.shape[0]
    if last is None:
        flat = jnp.pad(flat, (0, room))
    else:
        flat = jnp.concatenate([jnp.pad(flat, (0, room - 1)), last.reshape(1)])
    return flat.reshape(SMALL_ROWS, SMALL_COLS)


def _unpack_small(packed, local):
    flat, out, pos = packed.reshape(-1), {}, 0
    for n, length, sharded in SMALL:
        ln = length // N_DEV if (local and sharded) else length
        out[n] = flat[pos:pos + ln]
        pos += ln
    return out


def _gather_rider(shards, names):
    return _GatherRider([shards[n] for n in names])


def _gathered(rider, names, axis_of):
    return {n: (r.reshape(-1, r.shape[2]) if axis_of[n] == 0 else r) for n, r in zip(names, rider.results)}


def _blocks(g):
    return g if g.ndim == 3 else g.reshape(N_DEV, -1, g.shape[-1])


def _local_step(x, target, shards, s, parity):
    t = x.shape[0]
    axis_of = dict(BIG)
    consts = _ret_consts(t)
    lane_to_head = np.zeros((D_MODEL, LANES), np.float32)
    lane_to_head[np.arange(D_MODEL), np.arange(D_MODEL) // ATT_DH] = 1.0
    bd = jnp.asarray(lane_to_head).astype(BF16)
    kg_t = jnp.tile(s["k_norm_g"], (1, ATT_HEADS))
    qg_t = jnp.tile(s["b_q_norm_g"], (1, ATT_HEADS))
    q_scale = ATT_DH ** -0.5
    w, g, recv = {}, {}, {}

    def gather_on(names):
        return _gather_rider(shards, names), names

    def landed(ride):
        w.update(_gathered(ride[0], ride[1], axis_of))

    def scatter_on(names):
        return _ScatterRider([_blocks(g[n]) for n in names]), names

    def reduced(ride):
        recv.update(zip(ride[1], ride[0].results))

    proj, (w["a_w_in"], w_o) = _proj_gather(x, s["a_norm_g"], shards["a_w_in"], [shards["a_w_o"]], "a_proj")
    w["a_w_o"] = w_o.reshape(-1, w_o.shape[2])
    ride = gather_on(["a_w_gu"])
    y, o_ret, states = _ret_fwd(proj, s["a_gn_g"], consts, "a_ret", rider=ride[0])
    landed(ride)
    ride = gather_on(["w_kv"])
    x1 = _mm(y, w["a_w_o"], "nn", "a_out", res=x, rider=ride[0])
    landed(ride)
    ride = gather_on(["a_w_down", "b_w_q", "b_w_o"])
    gu_a, act_a = _mm(x1, w["a_w_gu"], "nt", "a_ffn_gu", epilogue="swiglu", out_block=FFN_BLK,
                      norm_g=s["a_ffn_norm_g"], rider=ride[0])
    landed(ride)
    x2 = _mm(act_a, w["a_w_down"], "nn", "a_ffn_down", res=x1)

    kv = _mm(x2, w["w_kv"], "nn", "kv_proj", norm_g=s["kv_norm_g"])
    kp, vp = _kv_prep(kv, kg_t, bd, "kv_prep")

    q_raw = _mm(x2, w["b_w_q"], "nn", "b_q", norm_g=s["b_norm_g"])
    qn = _q_hnorm(q_raw, qg_t, bd, q_scale, "q_hnorm")
    bias = _bias_table(s["b_rel_bias"].reshape(ATT_HEADS, REL_TABLE), "rel")
    ride = gather_on(["b_w_gu"])
    o_att, lse = _att_fwd(qn, kp, vp, bias, "b_att", rider=ride[0])
    landed(ride)
    x3 = _mm(o_att, w["b_w_o"], "nn", "b_out", res=x2)
    ride = gather_on(["b_w_down"])
    gu_b, act_b = _mm(x3, w["b_w_gu"], "nt", "b_ffn_gu", epilogue="swiglu", out_block=FFN_BLK,
                      norm_g=s["b_ffn_norm_g"], rider=ride[0])
    landed(ride)
    dy, loss = _mm(act_b, w["b_w_down"], "nn", "b_ffn_down", res=x3, epilogue="loss", extra=(target,))
    in_blk, kv_blk, ffn_blk = w["a_w_in"].shape[2], w["w_kv"].shape[2], FFN_BLK

    dgu = _mm(dy, w["b_w_down"], "nt", "b_ffn_dgu", out_block=ffn_blk, epilogue="swiglu_bwd", extra=gu_b)
    dgu = dgu.reshape(N_DEV, t, ffn_blk)
    g["b_w_down"] = _mm(act_b, dy, "tn", "b_ffn_gdown", out_dtype=BF16)
    ride = scatter_on(["b_w_down"])
    dx3, g["b_ffn_norm_g"] = _mm(dgu, w["b_w_gu"], "nn", "b_ffn_dh", epilogue="rms_bwd",
                                 extra=(x3, s["b_ffn_norm_g"], dy), rider=ride[0])
    reduced(ride)
    g["b_w_gu"] = _mm(dgu, x3, "tn", "b_ffn_ggu", out_dtype=BF16, norm_g=s["b_ffn_norm_g"], norm_b=True)

    do_att = _mm(dx3, w["b_w_o"], "nt", "b_dout", out_dtype=BF16)
    g["b_w_o"] = _mm(o_att, dx3, "tn", "b_gout", out_dtype=BF16)
    ride = scatter_on(["b_w_gu", "b_w_o"])
    dq, dkp, dvp, db = _att_bwd(qn, kp, vp, bias, do_att, o_att, lse, "b_datt", rider=ride[0])
    reduced(ride)
    g["b_rel_bias"] = _rel_reduce(db, "drel").reshape(1, -1)
    dq_raw, gq = _q_dhnorm(q_raw, qg_t, bd, dq, q_scale, "q_dhnorm")
    g["b_q_norm_g"] = gq.reshape(ATT_HEADS, ATT_DH).sum(axis=0, keepdims=True)
    g["b_w_q"] = _mm(x2, dq_raw, "tn", "b_gq", out_dtype=BF16, norm_g=s["b_norm_g"])
    dx2, g["b_norm_g"] = _mm(dq_raw, w["b_w_q"], "nt", "b_dq", epilogue="rms_bwd",
                             extra=(x2, s["b_norm_g"], dx3))

    dkv, gk = _kv_dprep(kv, kg_t, bd, dkp, dvp, "kv_dprep")
    g["k_norm_g"] = gk.reshape(ATT_HEADS, ATT_DH).sum(axis=0, keepdims=True)
    g["w_kv"] = _mm(x2, dkv, "tn", "kv_g", out_dtype=BF16, out_block=kv_blk, norm_g=s["kv_norm_g"])
    dx2, g["kv_norm_g"] = _mm(dkv, w["w_kv"], "nt", "kv_du", epilogue="rms_bwd",
                              extra=(x2, s["kv_norm_g"], dx2))

    ride = scatter_on(["b_w_q"])
    dgu = _mm(dx2, w["a_w_down"], "nt", "a_ffn_dgu", out_block=ffn_blk, epilogue="swiglu_bwd", extra=gu_a,
              rider=ride[0])
    reduced(ride)
    dgu = dgu.reshape(N_DEV, t, ffn_blk)
    g["a_w_down"] = _mm(act_a, dx2, "tn", "a_ffn_gdown", out_dtype=BF16)
    ride = scatter_on(["a_w_down"])
    dx1, g["a_ffn_norm_g"] = _mm(dgu, w["a_w_gu"], "nn", "a_ffn_dh", epilogue="rms_bwd",
                                 extra=(x1, s["a_ffn_norm_g"], dx2), rider=ride[0])
    reduced(ride)
    ride = scatter_on(["w_kv"])
    g["a_w_gu"] = _mm(dgu, x1, "tn", "a_ffn_ggu", out_dtype=BF16, norm_g=s["a_ffn_norm_g"], norm_b=True,
                      rider=ride[0])
    reduced(ride)

    swap = _SiblingSwapRider([_blocks(g["a_w_gu"])])
    dy_ret = _mm(dx1, w["a_w_o"], "nt", "a_dout", rider=swap)
    g["a_w_o"] = _mm(y, dx1, "tn", "a_gout", out_dtype=BF16)
    chips = _ChipScatterRider([_pair_add(_blocks(g["a_w_gu"]), swap.results[0], parity, "rs_pair_add_gu")])
    dproj, g["a_gn_g"] = _ret_bwd(proj, s["a_gn_g"], o_ret, states, dy_ret, consts, "a_dret", rider=chips)
    recv["a_w_gu"] = chips.results[0]
    ride = scatter_on(["a_w_o"])
    g["a_w_in"] = _mm(x, dproj, "tn", "a_gin", out_dtype=BF16, out_block=in_blk, norm_g=s["a_norm_g"],
                      rider=ride[0])
    reduced(ride)
    from_sibling = _exchange(_SiblingSwapRider([g["a_w_in"]]), "rs_sibling")[0]
    chip_sums = _pair_add(g["a_w_in"], from_sibling, parity, "rs_pair_add")
    last = _ChipScatterRider([chip_sums])
    grad_x, g["a_norm_g"] = _mm(dproj, w["a_w_in"], "nt", "a_dproj", epilogue="rms_bwd",
                                extra=(x, s["a_norm_g"], dx1), rider=last)
    recv["a_w_in"] = last.results[0]
    return loss, grad_x, recv, g


ARG_NAMES = ("x", "a_norm_g", "a_w_in", "a_gn_g", "a_w_o", "a_ffn_norm_g", "a_w_gu", "a_w_down",
             "kv_norm_g", "w_kv", "k_norm_g", "b_norm_g", "b_w_q", "b_q_norm_g", "b_rel_bias", "b_w_o",
             "b_ffn_norm_g", "b_w_gu", "b_w_down")
WEIGHT_NAMES = ARG_NAMES[1:]


def _big_shard(a, name):
    a = a[0] if a.ndim == 3 else a
    return a.T if name in TRANSPOSED else a


def _as_given(a, name, shape):
    return (a.T if name in TRANSPOSED else a).reshape(shape)


def kernel(x, a_norm_g, a_w_in, a_gn_g, a_w_o, a_ffn_norm_g, a_w_gu, a_w_down, kv_norm_g, w_kv, k_norm_g, b_norm_g, b_w_q, b_q_norm_g, b_rel_bias, b_w_o, b_ffn_norm_g, b_w_gu, b_w_down, loss_target, m_a_norm_g, m_a_w_in, m_a_gn_g, m_a_w_o, m_a_ffn_norm_g, m_a_w_gu, m_a_w_down, m_kv_norm_g, m_w_kv, m_k_norm_g, m_b_norm_g, m_b_w_q, m_b_q_norm_g, m_b_rel_bias, m_b_w_o, m_b_ffn_norm_g, m_b_w_gu, m_b_w_down, v_a_norm_g, v_a_w_in, v_a_gn_g, v_a_w_o, v_a_ffn_norm_g, v_a_w_gu, v_a_w_down, v_kv_norm_g, v_w_kv, v_k_norm_g, v_b_norm_g, v_b_w_q, v_b_q_norm_g, v_b_rel_bias, v_b_w_o, v_b_ffn_norm_g, v_b_w_gu, v_b_w_down):
    args = (x, a_norm_g, a_w_in, a_gn_g, a_w_o, a_ffn_norm_g, a_w_gu, a_w_down, kv_norm_g, w_kv, k_norm_g,
            b_norm_g, b_w_q, b_q_norm_g, b_rel_bias, b_w_o, b_ffn_norm_g, b_w_gu, b_w_down)
    p = dict(zip(ARG_NAMES, args))
    m_all = dict(zip(WEIGHT_NAMES, (m_a_norm_g, m_a_w_in, m_a_gn_g, m_a_w_o, m_a_ffn_norm_g, m_a_w_gu,
                                    m_a_w_down, m_kv_norm_g, m_w_kv, m_k_norm_g, m_b_norm_g, m_b_w_q,
                                    m_b_q_norm_g, m_b_rel_bias, m_b_w_o, m_b_ffn_norm_g, m_b_w_gu, m_b_w_down)))
    v_all = dict(zip(WEIGHT_NAMES, (v_a_norm_g, v_a_w_in, v_a_gn_g, v_a_w_o, v_a_ffn_norm_g, v_a_w_gu,
                                    v_a_w_down, v_kv_norm_g, v_w_kv, v_k_norm_g, v_b_norm_g, v_b_w_q,
                                    v_b_q_norm_g, v_b_rel_bias, v_b_w_o, v_b_ffn_norm_g, v_b_w_gu, v_b_w_down)))
    xi, yi, ci = _my_place()
    me = 4 * xi + 2 * yi + ci
    big_names = [n for n, _ in BIG]

    big_local = {n: _big_shard(p[n], n) for n in big_names}
    shards = {n: a.astype(BF16) for n, a in big_local.items()}
    small_local = _pack_small({n: p[n] for n, _, _ in SMALL})
    small_all = _exchange(_GatherRider([small_local]), "gather_small")[0]
    flat_g = small_all.reshape(N_DEV, -1)
    s_full, pos = {}, 0
    for n, length, sharded in SMALL:
        ln = length // N_DEV if sharded else length
        s_full[n] = flat_g[:, pos:pos + ln].reshape(1, -1) if sharded else p[n].reshape(1, -1)
        pos += ln

    parity = jnp.reshape(ci, (1,)).astype(jnp.int32)
    loss, grad_x, recv, g = _local_step(x[0], loss_target[0], shards, s_full, parity)

    partial = _pack_small({n: g[n] for n, _, _ in SMALL}, last=loss)
    summed = _sum_leading(_exchange(_GatherRider([partial]), "gather_gsmall")[0], "gsmall_sum")
    loss = summed[SMALL_ROWS - 1, SMALL_COLS - 1]
    g_small = _unpack_small(summed, local=False)
    for n, length, sharded in SMALL:
        if sharded:
            g_small[n] = lax.dynamic_slice(g_small[n], (me * (length // N_DEV),), (length // N_DEV,))

    grads, deltas, new_m, new_v = {}, {}, {}, {}
    for n in big_names:
        outs = _adamw(big_local[n], recv[n], _big_shard(m_all[n], n), _big_shard(v_all[n], n), "adamw_" + n)
        grads[n], deltas[n], new_m[n], new_v[n] = (_as_given(a, n, p[n].shape) for a in outs)
    pk = lambda src: _pack_small({n: src[n] for n, _, _ in SMALL})
    outs = _adamw(small_local, pk(g_small)[None], pk(m_all), pk(v_all), "adamw_small")
    g_s, d_s, nm_s, nv_s = (_unpack_small(a, local=True) for a in outs)
    for n, _, _ in SMALL:
        grads[n], deltas[n], new_m[n], new_v[n] = (a[n].reshape(p[n].shape) for a in (g_s, d_s, nm_s, nv_s))

    return (loss, grad_x[None], *[grads[n] for n in WEIGHT_NAMES], *[deltas[n] for n in WEIGHT_NAMES],
            *[new_m[n] for n in WEIGHT_NAMES], *[new_v[n] for n in WEIGHT_NAMES])
```
